```python
import jax, jax.numpy as jnp
from jax import lax
import numpy as np

D_MODEL = 1024
BATCH = 8
SEQ = 4096
DEPTH = 2

N_MIXERS = 2
ATTN_PAIRS = ((128, 1), (512, 4), (2048, 16))
N_ATTN_GROUPS = len(ATTN_PAIRS)
HEADS_PER_GROUP = 8
HEAD_DIM = D_MODEL // HEADS_PER_GROUP
ATTN_WIDTH = HEADS_PER_GROUP * HEAD_DIM
QKV_WIDTH = N_ATTN_GROUPS * 3 * ATTN_WIDTH
N_ALIBI_HEADS = N_ATTN_GROUPS * HEADS_PER_GROUP
Q_BLOCK = 128
POOL_WINDOWS = (2, 4, 8, 16)
POOL_GROUPS = len(POOL_WINDOWS)
POOL_GROUP_DIM = D_MODEL // POOL_GROUPS
D_FF = ((8 * D_MODEL + 3 * 256 - 1) // (3 * 256)) * 256
N_ATTN_LAYERS = (DEPTH + 1) // 2
N_POOL_LAYERS = DEPTH // 2
RMS_EPS = 1e-6

kernel_name = "dilated_attn_pool_hybrid_trunk"


def rmsnorm(x, g):
    xf = x.astype(jnp.float32)
    y = xf * lax.rsqrt(jnp.mean(xf * xf, axis=-1, keepdims=True) + RMS_EPS)
    return (y * g.astype(jnp.float32)).astype(x.dtype)


def alibi_slopes():
    n = N_ALIBI_HEADS
    return jnp.exp2(-8.0 * jnp.arange(1, n + 1, dtype=jnp.float32) / n)


def dilated_window_attention(q, k, v, window, dil, slopes):
    B, S, H, D = q.shape
    L = S // dil
    w_sub = window // dil
    nb = -(-L // Q_BLOCK)
    Lp = nb * Q_BLOCK
    pad = Lp - L
    Bd = B * dil

    def fold(a):
        return a.reshape(B, L, dil, H, D).transpose(0, 2, 1, 3, 4).reshape(Bd, L, H, D)

    def band(a):
        a = jnp.pad(a, ((0, 0), (Q_BLOCK, pad), (0, 0), (0, 0))).reshape(Bd, nb + 1, Q_BLOCK, H, D)
        return jnp.concatenate([a[:, :-1], a[:, 1:]], axis=2)

    qb = jnp.pad(fold(q), ((0, 0), (0, pad), (0, 0), (0, 0))).reshape(Bd, nb, Q_BLOCK, H, D)
    kb = band(fold(k))
    vb = band(fold(v))

    s = jnp.einsum('bnqhd,bnkhd->bnhqk', qb, kb) * (D ** -0.5)
    qi = jnp.arange(Q_BLOCK)[:, None]
    ki = jnp.arange(2 * Q_BLOCK)[None, :]
    delta = Q_BLOCK + qi - ki
    blk = jnp.arange(nb)[:, None, None]
    key_idx = blk * Q_BLOCK + ki[None] - Q_BLOCK
    valid = (delta >= 0)[None] & (delta <= w_sub)[None] & (key_idx >= 0)
    bias = -slopes[:, None, None] * (delta * dil).astype(jnp.float32)[None]
    s = s + bias[None, None]
    s = jnp.where(valid[None, :, None], s, -jnp.inf)
    m = jnp.max(s, axis=-1, keepdims=True)
    p = jnp.exp(s - m)
    den = jnp.sum(p, axis=-1, keepdims=True)
    o = jnp.einsum('bnhqk,bnkhd->bnqhd', p / den, vb)
    lse = (m + jnp.log(den))[..., 0].transpose(0, 1, 3, 2)

    o = o.reshape(Bd, Lp, H, D)[:, :L]
    o = o.reshape(B, dil, L, H, D).transpose(0, 2, 1, 3, 4).reshape(B, S, H, D)
    lse = lse.reshape(Bd, Lp, H)[:, :L]
    lse = lse.reshape(B, dil, L, H).transpose(0, 2, 1, 3).reshape(B, S, H)
    return o, lse


def dilated_attention_mixer(h, w_qkv, w_out):
    B, S, _ = h.shape
    qkv = (h @ w_qkv).astype(jnp.float32).reshape(B, S, N_ATTN_GROUPS, 3, HEADS_PER_GROUP, HEAD_DIM)
    slopes = alibi_slopes().reshape(N_ATTN_GROUPS, HEADS_PER_GROUP)
    outs, lses = [], []
    for g, (window, dil) in enumerate(ATTN_PAIRS):
        o, lse = dilated_window_attention(qkv[:, :, g, 0], qkv[:, :, g, 1], qkv[:, :, g, 2], window, dil, slopes[g])
        outs.append(o)
        lses.append(lse)
    o = jnp.stack(outs, axis=0)
    wts = jax.nn.softmax(jnp.stack(lses, axis=0), axis=0)
    o = jnp.sum(wts[..., None] * o, axis=0).reshape(B, S, ATTN_WIDTH)
    return o.astype(h.dtype) @ w_out


def trailing_mean(u, w):
    B, S, C = u.shape
    c = jnp.cumsum(u, axis=1)
    shifted = jnp.concatenate([jnp.zeros((B, w, C), u.dtype), c[:, :S - w]], axis=1)
    count = jnp.minimum(jnp.arange(1, S + 1), w).astype(jnp.float32)[None, :, None]
    return (c - shifted) / count


def pooling_mixer(h, w_in, w_group, scale):
    B, S, _ = h.shape
    u = (h @ w_in).astype(jnp.float32).reshape(B, S, POOL_GROUPS, POOL_GROUP_DIM)
    ys = [trailing_mean(u[:, :, g], w) - u[:, :, g] for g, w in enumerate(POOL_WINDOWS)]
    y = jnp.stack(ys, axis=2)
    y = jnp.einsum('bsgc,gcd->bsgd', y, w_group.astype(jnp.float32)).reshape(B, S, D_MODEL)
    return (y * scale.astype(jnp.float32)).astype(h.dtype)


def swiglu_ffn(h, w_gate_up, w_down):
    gu = h @ w_gate_up
    gate, up = gu[..., :D_FF], gu[..., D_FF:]
    return (jax.nn.silu(gate) * up) @ w_down


def _fwd_setup_inputs(seed: int = 0) -> dict:
    key = jax.random.key(seed)
    ks = jax.random.split(key, 13)
    f32 = jnp.float32
    nA, nP = N_ATTN_LAYERS, N_POOL_LAYERS
    return {
        "x": jax.random.normal(ks[0], (BATCH, SEQ, D_MODEL), f32),
        "attn_norm": 1.0 + 0.02 * jax.random.normal(ks[1], (nA, D_MODEL), f32),
        "w_qkv": jax.random.normal(ks[2], (nA, D_MODEL, QKV_WIDTH), f32) * D_MODEL ** -0.5,
        "w_attn_out": jax.random.normal(ks[3], (nA, ATTN_WIDTH, D_MODEL), f32) * ATTN_WIDTH ** -0.5,
        "pool_norm": 1.0 + 0.02 * jax.random.normal(ks[4], (nP, D_MODEL), f32),
        "w_pool_in": jax.random.normal(ks[5], (nP, D_MODEL, D_MODEL), f32) * D_MODEL ** -0.5,
        "w_pool_group": jax.random.normal(ks[6], (nP, POOL_GROUPS, POOL_GROUP_DIM, POOL_GROUP_DIM), f32) * POOL_GROUP_DIM ** -0.5,
        "pool_scale": 0.5 + 0.1 * jax.random.normal(ks[7], (nP, D_MODEL), f32),
        "ffn_norm": 1.0 + 0.02 * jax.random.normal(ks[8], (DEPTH, D_MODEL), f32),
        "w_ffn_gate_up": jax.random.normal(ks[9], (DEPTH, D_MODEL, 2 * D_FF), f32) * D_MODEL ** -0.5,
        "w_ffn_down": jax.random.normal(ks[10], (DEPTH, D_FF, D_MODEL), f32) * D_FF ** -0.5,
        "final_norm": 1.0 + 0.02 * jax.random.normal(ks[11], (D_MODEL,), f32),
    }


def _fwd_reference(x, attn_norm, w_qkv, w_attn_out, pool_norm, w_pool_in, w_pool_group, pool_scale,
              ffn_norm, w_ffn_gate_up, w_ffn_down, final_norm):
    for i in range(DEPTH):
        j = i // N_MIXERS
        if i % N_MIXERS == 0:
            x = x + dilated_attention_mixer(rmsnorm(x, attn_norm[j]), w_qkv[j], w_attn_out[j])
        else:
            x = x + pooling_mixer(rmsnorm(x, pool_norm[j]), w_pool_in[j], w_pool_group[j], pool_scale[j])
        x = x + swiglu_ffn(rmsnorm(x, ffn_norm[i]), w_ffn_gate_up[i], w_ffn_down[i])
    return rmsnorm(x, final_norm)


import jax as _jax
import jax.numpy as _jnp

TWIN_FORMAT = 'train_step'
FWD_PARAMS = ['x', 'attn_norm', 'w_qkv', 'w_attn_out', 'pool_norm', 'w_pool_in', 'w_pool_group', 'pool_scale', 'ffn_norm', 'w_ffn_gate_up', 'w_ffn_down', 'final_norm']
TWIN_WEIGHTS = ['attn_norm', 'w_qkv', 'w_attn_out', 'pool_norm', 'w_pool_in', 'w_pool_group', 'pool_scale', 'ffn_norm', 'w_ffn_gate_up', 'w_ffn_down', 'final_norm']
TWIN_DIFF_INPUT = 'x'
TWIN_INPUTS = ['x', 'attn_norm', 'w_qkv', 'w_attn_out', 'pool_norm', 'w_pool_in', 'w_pool_group', 'pool_scale', 'ffn_norm', 'w_ffn_gate_up', 'w_ffn_down', 'final_norm', 'loss_target', 'm_attn_norm', 'm_w_qkv', 'm_w_attn_out', 'm_pool_norm', 'm_w_pool_in', 'm_w_pool_group', 'm_pool_scale', 'm_ffn_norm', 'm_w_ffn_gate_up', 'm_w_ffn_down', 'm_final_norm', 'v_attn_norm', 'v_w_qkv', 'v_w_attn_out', 'v_pool_norm', 'v_w_pool_in', 'v_w_pool_group', 'v_pool_scale', 'v_ffn_norm', 'v_w_ffn_gate_up', 'v_w_ffn_down', 'v_final_norm']
TWIN_OUTPUTS = ['loss', 'grad_x', 'grad_attn_norm', 'grad_w_qkv', 'grad_w_attn_out', 'grad_pool_norm', 'grad_w_pool_in', 'grad_w_pool_group', 'grad_pool_scale', 'grad_ffn_norm', 'grad_w_ffn_gate_up', 'grad_w_ffn_down', 'grad_final_norm', 'delta_attn_norm', 'delta_w_qkv', 'delta_w_attn_out', 'delta_pool_norm', 'delta_w_pool_in', 'delta_w_pool_group', 'delta_pool_scale', 'delta_ffn_norm', 'delta_w_ffn_gate_up', 'delta_w_ffn_down', 'delta_final_norm', 'new_m_attn_norm', 'new_m_w_qkv', 'new_m_w_attn_out', 'new_m_pool_norm', 'new_m_w_pool_in', 'new_m_w_pool_group', 'new_m_pool_scale', 'new_m_ffn_norm', 'new_m_w_ffn_gate_up', 'new_m_w_ffn_down', 'new_m_final_norm', 'new_v_attn_norm', 'new_v_w_qkv', 'new_v_w_attn_out', 'new_v_pool_norm', 'new_v_w_pool_in', 'new_v_w_pool_group', 'new_v_pool_scale', 'new_v_ffn_norm', 'new_v_w_ffn_gate_up', 'new_v_w_ffn_down', 'new_v_final_norm']
TWIN_LEAF_KINDS = {'loss': 'loss', 'grad_x': 'grad_x', 'grad_attn_norm': 'grad_w', 'grad_w_qkv': 'grad_w', 'grad_w_attn_out': 'grad_w', 'grad_pool_norm': 'grad_w', 'grad_w_pool_in': 'grad_w', 'grad_w_pool_group': 'grad_w', 'grad_pool_scale': 'grad_w', 'grad_ffn_norm': 'grad_w', 'grad_w_ffn_gate_up': 'grad_w', 'grad_w_ffn_down': 'grad_w', 'grad_final_norm': 'grad_w', 'delta_attn_norm': 'delta_w', 'delta_w_qkv': 'delta_w', 'delta_w_attn_out': 'delta_w', 'delta_pool_norm': 'delta_w', 'delta_w_pool_in': 'delta_w', 'delta_w_pool_group': 'delta_w', 'delta_pool_scale': 'delta_w', 'delta_ffn_norm': 'delta_w', 'delta_w_ffn_gate_up': 'delta_w', 'delta_w_ffn_down': 'delta_w', 'delta_final_norm': 'delta_w', 'new_m_attn_norm': 'new_m', 'new_m_w_qkv': 'new_m', 'new_m_w_attn_out': 'new_m', 'new_m_pool_norm': 'new_m', 'new_m_w_pool_in': 'new_m', 'new_m_w_pool_group': 'new_m', 'new_m_pool_scale': 'new_m', 'new_m_ffn_norm': 'new_m', 'new_m_w_ffn_gate_up': 'new_m', 'new_m_w_ffn_down': 'new_m', 'new_m_final_norm': 'new_m', 'new_v_attn_norm': 'new_v', 'new_v_w_qkv': 'new_v', 'new_v_w_attn_out': 'new_v', 'new_v_pool_norm': 'new_v', 'new_v_w_pool_in': 'new_v', 'new_v_w_pool_group': 'new_v', 'new_v_pool_scale': 'new_v', 'new_v_ffn_norm': 'new_v', 'new_v_w_ffn_gate_up': 'new_v', 'new_v_w_ffn_down': 'new_v', 'new_v_final_norm': 'new_v'}


def _forward(args):
    return _fwd_reference(*[args[k] for k in FWD_PARAMS])


def _output_shape():
    out = _jax.eval_shape(lambda: _forward(_fwd_setup_inputs(0)))
    return out.shape, out.dtype

N_MICROBATCH = 1
ADAM_LR = 0.001
ADAM_B1 = 0.9
ADAM_B2 = 0.999
ADAM_EPS = 1e-08
ADAM_WD = 0.01
ADAM_STEP = 10
PER_EXAMPLE_BATCH_AXIS = {'x': 0, 'loss_target': 0}
SHARED_INPUTS = []
_WEIGHT_DTYPES = {'attn_norm': _jnp.float32, 'w_qkv': _jnp.float32, 'w_attn_out': _jnp.float32, 'pool_norm': _jnp.float32, 'w_pool_in': _jnp.float32, 'w_pool_group': _jnp.float32, 'pool_scale': _jnp.float32, 'ffn_norm': _jnp.float32, 'w_ffn_gate_up': _jnp.float32, 'w_ffn_down': _jnp.float32, 'final_norm': _jnp.float32}
MOMENT_SCALE = {'attn_norm': 1.061452e-01, 'w_qkv': 3.535737e-02, 'w_attn_out': 6.895835e-02, 'pool_norm': 6.680790e-02, 'w_pool_in': 6.768704e-02, 'w_pool_group': 6.743902e-02, 'pool_scale': 1.826029e-01, 'ffn_norm': 1.309985e-01, 'w_ffn_gate_up': 5.319807e-02, 'w_ffn_down': 8.691447e-02, 'final_norm': 3.205909e+01}


def _to_microbatches(a, axis):
    t = _jnp.moveaxis(a, axis, 0)
    t = t.reshape((N_MICROBATCH, t.shape[0] // N_MICROBATCH) + t.shape[1:])
    return _jnp.moveaxis(t, 1, axis + 1)


def setup_inputs(seed: int = 0) -> dict:
    inp = _fwd_setup_inputs(seed)
    key = _jax.random.fold_in(_jax.random.key(seed), 7919)
    shape, _ = _output_shape()
    out = dict(inp)
    out["loss_target"] = _jax.random.normal(_jax.random.fold_in(key, 0), shape, _jnp.float32)
    for i, name in enumerate(TWIN_WEIGHTS):
        w = inp[name].astype(_jnp.float32)
        if MOMENT_SCALE is None:
            s = _jnp.sqrt(_jnp.mean(_jnp.square(w)) + 1e-30)
        else:
            s = MOMENT_SCALE[name]
        km, kv = _jax.random.split(_jax.random.fold_in(key, i + 1))
        out[name] = w
        out["m_" + name] = s * _jax.random.normal(km, w.shape, _jnp.float32)
        out["v_" + name] = (s * s) * _jax.random.uniform(kv, w.shape, _jnp.float32, 0.5, 1.5)
    if N_MICROBATCH > 1:
        for name, axis in PER_EXAMPLE_BATCH_AXIS.items():
            out[name] = _to_microbatches(out[name], axis)
    return {'x': out['x'], 'attn_norm': out['attn_norm'], 'w_qkv': out['w_qkv'], 'w_attn_out': out['w_attn_out'], 'pool_norm': out['pool_norm'], 'w_pool_in': out['w_pool_in'], 'w_pool_group': out['w_pool_group'], 'pool_scale': out['pool_scale'], 'ffn_norm': out['ffn_norm'], 'w_ffn_gate_up': out['w_ffn_gate_up'], 'w_ffn_down': out['w_ffn_down'], 'final_norm': out['final_norm'], 'loss_target': out['loss_target'], 'm_attn_norm': out['m_attn_norm'], 'm_w_qkv': out['m_w_qkv'], 'm_w_attn_out': out['m_w_attn_out'], 'm_pool_norm': out['m_pool_norm'], 'm_w_pool_in': out['m_w_pool_in'], 'm_w_pool_group': out['m_w_pool_group'], 'm_pool_scale': out['m_pool_scale'], 'm_ffn_norm': out['m_ffn_norm'], 'm_w_ffn_gate_up': out['m_w_ffn_gate_up'], 'm_w_ffn_down': out['m_w_ffn_down'], 'm_final_norm': out['m_final_norm'], 'v_attn_norm': out['v_attn_norm'], 'v_w_qkv': out['v_w_qkv'], 'v_w_attn_out': out['v_w_attn_out'], 'v_pool_norm': out['v_pool_norm'], 'v_w_pool_in': out['v_w_pool_in'], 'v_w_pool_group': out['v_w_pool_group'], 'v_pool_scale': out['v_pool_scale'], 'v_ffn_norm': out['v_ffn_norm'], 'v_w_ffn_gate_up': out['v_w_ffn_gate_up'], 'v_w_ffn_down': out['v_w_ffn_down'], 'v_final_norm': out['v_final_norm']}


def _loss(weights, diff, rest, loss_target):
    with _jax.named_scope("forward"):
        args = {**rest, TWIN_DIFF_INPUT: diff, **{k: w.astype(_WEIGHT_DTYPES[k]) for k, w in weights.items()}}
        y = _forward(args)
    with _jax.named_scope("loss_head"):
        err = _jnp.square(y.astype(_jnp.float32) - loss_target)
        return 0.5 * _jnp.sum(_jnp.mean(err, axis=-1)) if err.ndim else 0.5 * err


def _adamw(w, g, m, v):
    m = ADAM_B1 * m + (1.0 - ADAM_B1) * g
    v = ADAM_B2 * v + (1.0 - ADAM_B2) * _jnp.square(g)
    m_hat = m / (1.0 - ADAM_B1 ** ADAM_STEP)
    v_hat = v / (1.0 - ADAM_B2 ** ADAM_STEP)
    delta = -ADAM_LR * (m_hat / (_jnp.sqrt(v_hat) + ADAM_EPS) + ADAM_WD * w)
    return delta, m, v


def reference(x, attn_norm, w_qkv, w_attn_out, pool_norm, w_pool_in, w_pool_group, pool_scale, ffn_norm, w_ffn_gate_up, w_ffn_down, final_norm, loss_target, m_attn_norm, m_w_qkv, m_w_attn_out, m_pool_norm, m_w_pool_in, m_w_pool_group, m_pool_scale, m_ffn_norm, m_w_ffn_gate_up, m_w_ffn_down, m_final_norm, v_attn_norm, v_w_qkv, v_w_attn_out, v_pool_norm, v_w_pool_in, v_w_pool_group, v_pool_scale, v_ffn_norm, v_w_ffn_gate_up, v_w_ffn_down, v_final_norm):
    given = dict(x=x, attn_norm=attn_norm, w_qkv=w_qkv, w_attn_out=w_attn_out, pool_norm=pool_norm, w_pool_in=w_pool_in, w_pool_group=w_pool_group, pool_scale=pool_scale, ffn_norm=ffn_norm, w_ffn_gate_up=w_ffn_gate_up, w_ffn_down=w_ffn_down, final_norm=final_norm, loss_target=loss_target, m_attn_norm=m_attn_norm, m_w_qkv=m_w_qkv, m_w_attn_out=m_w_attn_out, m_pool_norm=m_pool_norm, m_w_pool_in=m_w_pool_in, m_w_pool_group=m_w_pool_group, m_pool_scale=m_pool_scale, m_ffn_norm=m_ffn_norm, m_w_ffn_gate_up=m_w_ffn_gate_up, m_w_ffn_down=m_w_ffn_down, m_final_norm=m_final_norm, v_attn_norm=v_attn_norm, v_w_qkv=v_w_qkv, v_w_attn_out=v_w_attn_out, v_pool_norm=v_pool_norm, v_w_pool_in=v_w_pool_in, v_w_pool_group=v_w_pool_group, v_pool_scale=v_pool_scale, v_ffn_norm=v_ffn_norm, v_w_ffn_gate_up=v_w_ffn_gate_up, v_w_ffn_down=v_w_ffn_down, v_final_norm=v_final_norm)
    weights = {n: given[n] for n in TWIN_WEIGHTS}
    shared = {n: given[n] for n in SHARED_INPUTS}
    per_example = {n: given[n] for n in ['x']}
    grad_fn = _jax.value_and_grad(_loss, argnums=(0, 1))

    def one_microbatch(ex, loss_target):
        ex = dict(ex)
        diff = ex.pop(TWIN_DIFF_INPUT)
        return grad_fn(weights, diff, {**shared, **ex}, loss_target)

    if N_MICROBATCH == 1:
        loss, (grad_w, grad_x) = one_microbatch(per_example, given["loss_target"])
    else:
        def body(carry, xs):
            loss_sum, grad_sum = carry
            l_k, (gw_k, gx_k) = one_microbatch(xs[0], xs[1])
            with _jax.named_scope("update"):
                return (loss_sum + l_k, _jax.tree.map(_jnp.add, grad_sum, gw_k)), gx_k

        init = (_jnp.zeros((), _jnp.float32), _jax.tree.map(_jnp.zeros_like, weights))
        (loss, grad_w), grad_x = _jax.lax.scan(body, init, (per_example, given["loss_target"]))
    with _jax.named_scope("update"):
        delta_w, new_m, new_v = {}, {}, {}
        for n in TWIN_WEIGHTS:
            delta_w[n], new_m[n], new_v[n] = _adamw(weights[n], grad_w[n], given["m_" + n], given["v_" + n])
    return (loss, grad_x, *[grad_w[n] for n in TWIN_WEIGHTS], *[delta_w[n] for n in TWIN_WEIGHTS],
            *[new_m[n] for n in TWIN_WEIGHTS], *[new_v[n] for n in TWIN_WEIGHTS])
```

```python
import functools

import jax
import jax.numpy as jnp
from jax import lax
from jax.experimental import pallas as pl
from jax.experimental.pallas import tpu as pltpu

F32 = jnp.float32
BF16 = jnp.bfloat16
MESH = pl.DeviceIdType.MESH

D_MODEL = 1024
N_HEADS = 8
HEAD_DIM = 128
Q_BLOCK = 128
ATTN_DILATIONS = (1, 4, 16)
N_GROUPS = 3
D_FF = 2816
N_CHIPS = 4
FF_SHARD = 2 * D_FF // N_CHIPS
QKV_SHARD = 3 * 3 * D_MODEL // N_CHIPS
QKV_TILE = 768
POOL_WINDOWS = (2, 4, 8, 16)
POOL_DIM = 256
POOL_HALO = 16
RMS_EPS = 1e-6
NEG = -1e30
ADAM_LR = 0.001
ADAM_B1 = 0.9
ADAM_B2 = 0.999
ADAM_EPS = 1e-08
ADAM_WD = 0.01
ADAM_STEP = 10
VMEM_LIMIT = 56 * 1024 * 1024

_DN = {
    "nn": (((1,), (0,)), ((), ())),
    "nt": (((1,), (1,)), ((), ())),
    "tn": (((0,), (0,)), ((), ())),
}


def _pcall(body, **kw):
    return pl.pallas_call(body, **kw)


def _params(sem):
    return pltpu.CompilerParams(dimension_semantics=sem, vmem_limit_bytes=VMEM_LIMIT)


def _dot(a, b, mode):
    return lax.dot_general(a, b, _DN[mode], preferred_element_type=F32)


def _mm(name, mode, grid, a, a_spec, b, b_spec, out_shape, o_spec, acc_shape, res=None, res_spec=None):
    nk = grid[-1]
    has_res = res is not None

    def body(*refs):
        a_ref, b_ref = refs[0], refs[1]
        res_ref = refs[2] if has_res else None
        o_ref = refs[2 + has_res]
        part = _dot(a_ref[...].astype(BF16), b_ref[...].astype(BF16), mode)
        if nk == 1:
            if has_res:
                part = part + res_ref[...]
            o_ref[...] = part.astype(o_ref.dtype)
            return
        acc_ref = refs[3 + has_res]
        k = pl.program_id(len(grid) - 1)

        @pl.when(k == 0)
        def _():
            acc_ref[...] = part

        @pl.when(k > 0)
        def _():
            acc_ref[...] += part

        @pl.when(k == nk - 1)
        def _():
            r = acc_ref[...]
            if has_res:
                r = r + res_ref[...]
            o_ref[...] = r.astype(o_ref.dtype)

    in_specs = [a_spec, b_spec] + ([res_spec] if has_res else [])
    operands = [a, b] + ([res] if has_res else [])
    scratch = [] if nk == 1 else [pltpu.VMEM(acc_shape, F32)]
    sem = ("parallel",) * (len(grid) - 1) + ("arbitrary",)
    return _pcall(body, name=name, grid=grid, in_specs=in_specs, out_specs=o_spec, out_shape=out_shape,
                  scratch_shapes=scratch, compiler_params=_params(sem))(*operands)


def _mm_rows(name, a, b, out_dtype, tm, res=None):
    M, K = a.shape
    N = b.shape[1]
    return _mm(name, "nn", (M // tm, 1), a, pl.BlockSpec((tm, K), lambda i, k: (i, 0)),
               b, pl.BlockSpec((K, N), lambda i, k: (0, 0)),
               jax.ShapeDtypeStruct((M, N), out_dtype), pl.BlockSpec((tm, N), lambda i, k: (i, 0)), None,
               res=res, res_spec=pl.BlockSpec((tm, N), lambda i, k: (i, 0)))


def _mm_nt_rows(name, a, b, tm):
    M, N = a.shape
    K = b.shape[0]
    return _mm(name, "nt", (M // tm, 1), a, pl.BlockSpec((tm, N), lambda i, k: (i, 0)),
               b, pl.BlockSpec((K, N), lambda i, k: (0, 0)),
               jax.ShapeDtypeStruct((M, K), F32), pl.BlockSpec((tm, K), lambda i, k: (i, 0)), None)


def _mm_tn(name, a, b, tm, tk):
    T, M = a.shape
    N = b.shape[1]
    return _mm(name, "tn", (M // tm, T // tk), a, pl.BlockSpec((tk, tm), lambda i, k: (k, i)),
               b, pl.BlockSpec((tk, N), lambda i, k: (k, 0)),
               jax.ShapeDtypeStruct((M, N), BF16), pl.BlockSpec((tm, N), lambda i, k: (i, 0)), (tm, N))


def _rms_fwd(name, x, g, tr=512):
    S, D = x.shape

    def body(x_ref, g_ref, o_ref):
        xf = x_ref[...]
        r = lax.rsqrt(jnp.mean(xf * xf, axis=-1, keepdims=True) + RMS_EPS)
        o_ref[...] = ((xf * r) * g_ref[...]).astype(o_ref.dtype)

    return _pcall(body, name=name, grid=(S // tr,),
                  in_specs=[pl.BlockSpec((tr, D), lambda i: (i, 0)), pl.BlockSpec((1, D), lambda i: (0, 0))],
                  out_specs=pl.BlockSpec((tr, D), lambda i: (i, 0)),
                  out_shape=jax.ShapeDtypeStruct((S, D), BF16), compiler_params=_params(("parallel",)))(x, g)


def _rms_bwd(name, x, g, dh, dres, tr=512):
    S, D = x.shape

    def body(x_ref, g_ref, dh_ref, dres_ref, dx_ref, dg_ref):
        i = pl.program_id(0)
        xf = x_ref[...]
        r = lax.rsqrt(jnp.mean(xf * xf, axis=-1, keepdims=True) + RMS_EPS)
        xh = xf * r
        dh_v = dh_ref[...]
        dg = jnp.sum(dh_v * xh, axis=0, keepdims=True)
        t = dh_v * g_ref[...]
        dx_ref[...] = dres_ref[...] + r * (t - xh * jnp.mean(t * xh, axis=-1, keepdims=True))

        @pl.when(i == 0)
        def _():
            dg_ref[...] = dg

        @pl.when(i > 0)
        def _():
            dg_ref[...] += dg

    row = pl.BlockSpec((tr, D), lambda i: (i, 0))
    vec = pl.BlockSpec((1, D), lambda i: (0, 0))
    return _pcall(body, name=name, grid=(S // tr,), in_specs=[row, vec, row, row], out_specs=[row, vec],
                  out_shape=[jax.ShapeDtypeStruct((S, D), F32), jax.ShapeDtypeStruct((1, D), F32)],
                  compiler_params=_params(("arbitrary",)))(x, g, dh, dres)


def _loss_bwd(name, x, g, tgt, tr=512):
    S, D = x.shape

    def body(x_ref, g_ref, t_ref, loss_ref, dx_ref, dg_ref):
        i = pl.program_id(0)
        xf = x_ref[...]
        r = lax.rsqrt(jnp.mean(xf * xf, axis=-1, keepdims=True) + RMS_EPS)
        xh = xf * r
        gv = g_ref[...]
        e = xh * gv - t_ref[...]
        lp = 0.5 * jnp.sum(jnp.mean(e * e, axis=-1, keepdims=True), axis=0, keepdims=True)
        dy = e * (1.0 / D)
        dg = jnp.sum(dy * xh, axis=0, keepdims=True)
        t = dy * gv
        dx_ref[...] = r * (t - xh * jnp.mean(t * xh, axis=-1, keepdims=True))

        @pl.when(i == 0)
        def _():
            dg_ref[...] = dg
            loss_ref[...] = lp

        @pl.when(i > 0)
        def _():
            dg_ref[...] += dg
            loss_ref[...] += lp

    row = pl.BlockSpec((tr, D), lambda i: (i, 0))
    vec = pl.BlockSpec((1, D), lambda i: (0, 0))
    one = pl.BlockSpec((1, 1), lambda i: (0, 0))
    return _pcall(body, name=name, grid=(S // tr,), in_specs=[row, vec, row], out_specs=[one, row, vec],
                  out_shape=[jax.ShapeDtypeStruct((1, 1), F32), jax.ShapeDtypeStruct((S, D), F32),
                             jax.ShapeDtypeStruct((1, D), F32)],
                  compiler_params=_params(("arbitrary",)))(x, g, tgt)


def _ffn_up(name, h, w_gu, tm=512):
    S, D = h.shape
    W = FF_SHARD

    def body(h_ref, wg_ref, wu_ref, gu_ref, act_ref):
        hv = h_ref[...]
        gate = _dot(hv, wg_ref[...], "nn")
        up = _dot(hv, wu_ref[...], "nn")
        gu_ref[0] = gate
        gu_ref[1] = up
        sig = 1.0 / (1.0 + jnp.exp(-gate))
        act_ref[...] = ((gate * sig) * up).astype(act_ref.dtype)

    return _pcall(
        body, name=name, grid=(2, S // tm),
        in_specs=[pl.BlockSpec((tm, D), lambda j, i: (i, 0)),
                  pl.BlockSpec((None, D, W), lambda j, i: (j, 0, 0)),
                  pl.BlockSpec((None, D, W), lambda j, i: (j + 2, 0, 0))],
        out_specs=[pl.BlockSpec((2, tm, W), lambda j, i: (0, i, j)), pl.BlockSpec((tm, W), lambda j, i: (i, j))],
        out_shape=[jax.ShapeDtypeStruct((2, S, D_FF), F32), jax.ShapeDtypeStruct((S, D_FF), BF16)],
        compiler_params=_params(("parallel", "parallel")))(h, w_gu, w_gu)


def _ffn_down_bwd(name, dx, w_down, gu, tm=512):
    S, D = dx.shape
    W = FF_SHARD

    def body(dx_ref, w_ref, gu_ref, dgu_ref):
        dact = _dot(dx_ref[...].astype(BF16), w_ref[...], "nt")
        gate = gu_ref[0]
        up = gu_ref[1]
        sig = 1.0 / (1.0 + jnp.exp(-gate))
        silu = gate * sig
        dgu_ref[0] = (dact * up * (sig * (1.0 + gate * (1.0 - sig)))).astype(dgu_ref.dtype)
        dgu_ref[1] = (dact * silu).astype(dgu_ref.dtype)

    blk = pl.BlockSpec((2, tm, W), lambda j, i: (0, i, j))
    return _pcall(
        body, name=name, grid=(2, S // tm),
        in_specs=[pl.BlockSpec((tm, D), lambda j, i: (i, 0)), pl.BlockSpec((W, D), lambda j, i: (j, 0)), blk],
        out_specs=blk, out_shape=jax.ShapeDtypeStruct((2, S, D_FF), BF16),
        compiler_params=_params(("parallel", "parallel")))(dx, w_down, gu)


def _ffn_dw_up(name, h, dgu, tk=1024):
    S, D = h.shape
    W = FF_SHARD
    return _mm(name, "tn", (N_CHIPS, S // tk), h, pl.BlockSpec((tk, D), lambda s, k: (k, 0)),
               dgu, pl.BlockSpec((None, tk, W), lambda s, k: (s // 2, k, s % 2)),
               jax.ShapeDtypeStruct((N_CHIPS, D, W), BF16), pl.BlockSpec((None, D, W), lambda s, k: (s, 0, 0)),
               (D, W))


def _ffn_dh(name, dgu, w_gu, tm=1024):
    S = dgu.shape[1]
    W = FF_SHARD
    return _mm(name, "nt", (S // tm, N_CHIPS), dgu, pl.BlockSpec((None, tm, W), lambda i, k: (k // 2, i, k % 2)),
               w_gu, pl.BlockSpec((None, D_MODEL, W), lambda i, k: (k, 0, 0)),
               jax.ShapeDtypeStruct((S, D_MODEL), F32), pl.BlockSpec((tm, D_MODEL), lambda i, k: (i, 0)),
               (tm, D_MODEL))


def _qkv_proj(name, hf, w_qkv, tm=1024):
    S = hf.shape[1]
    per_chip = QKV_SHARD // QKV_TILE
    per_group = 3 * D_MODEL // QKV_TILE
    return _mm(name, "nn", (N_GROUPS, S // tm, per_group, 1),
               hf, pl.BlockSpec((None, tm, D_MODEL), lambda g, i, j, k: (g, i, 0)),
               w_qkv, pl.BlockSpec((None, D_MODEL, QKV_TILE),
                                   lambda g, i, j, k: ((per_group * g + j) // per_chip, 0, (per_group * g + j) % per_chip)),
               jax.ShapeDtypeStruct((N_GROUPS, S, 3 * D_MODEL), BF16),
               pl.BlockSpec((None, tm, QKV_TILE), lambda g, i, j, k: (g, i, j)), None)


def _qkv_dw(name, hf, dqkv, tk=1024):
    S = hf.shape[1]
    per_chip = QKV_SHARD // QKV_TILE
    per_group = 3 * D_MODEL // QKV_TILE
    return _mm(name, "tn", (N_GROUPS, per_group, S // tk),
               hf, pl.BlockSpec((None, tk, D_MODEL), lambda g, j, k: (g, k, 0)),
               dqkv, pl.BlockSpec((None, tk, QKV_TILE), lambda g, j, k: (g, k, j)),
               jax.ShapeDtypeStruct((N_CHIPS, D_MODEL, QKV_SHARD), BF16),
               pl.BlockSpec((None, D_MODEL, QKV_TILE),
                            lambda g, j, k: ((per_group * g + j) // per_chip, 0, (per_group * g + j) % per_chip)),
               (D_MODEL, QKV_TILE))


def _qkv_dh(name, dqkv, w_qkv, tm=1024):
    S = dqkv.shape[1]
    per_chip = QKV_SHARD // QKV_TILE
    per_group = 3 * D_MODEL // QKV_TILE
    return _mm(name, "nt", (N_GROUPS, S // tm, per_group),
               dqkv, pl.BlockSpec((None, tm, QKV_TILE), lambda g, i, k: (g, i, k)),
               w_qkv, pl.BlockSpec((None, D_MODEL, QKV_TILE),
                                   lambda g, i, k: ((per_group * g + k) // per_chip, 0, (per_group * g + k) % per_chip)),
               jax.ShapeDtypeStruct((N_GROUPS, S, D_MODEL), F32),
               pl.BlockSpec((None, tm, D_MODEL), lambda g, i, k: (g, i, 0)), (tm, D_MODEL))


def _alibi_coef(g, h):
    vals = [-(2.0 ** (-8.0 * (gg * N_HEADS + h + 1) / (N_GROUPS * N_HEADS))) * ATTN_DILATIONS[gg]
            for gg in range(N_GROUPS)]
    return jnp.where(g == 0, vals[0], jnp.where(g == 1, vals[1], vals[2])).astype(F32)


def _blocks_per_segment(g, S):
    return jnp.right_shift(S // Q_BLOCK, 2 * g)


def _band_consts():
    row = lax.broadcasted_iota(jnp.int32, (Q_BLOCK, Q_BLOCK), 0)
    col = lax.broadcasted_iota(jnp.int32, (Q_BLOCK, Q_BLOCK), 1)
    d_cur = (row - col).astype(F32)
    return d_cur, d_cur + float(Q_BLOCK), col <= row, col >= row


def _attn_fwd(name, qkv, tq=256):
    S = qkv.shape[1]
    nqb = tq // Q_BLOCK
    scale = HEAD_DIM ** -0.5

    def body(q_ref, k_ref, kp_ref, v_ref, vp_ref, o_ref, lse_ref):
        g = pl.program_id(0)
        i = pl.program_id(1)
        nb = _blocks_per_segment(g, S)
        d_cur, d_prev, m_cur, m_prev = _band_consts()
        for b in range(nqb):
            rows = slice(b * Q_BLOCK, (b + 1) * Q_BLOCK)
            prows = slice((b - 1) * Q_BLOCK, b * Q_BLOCK)
            pen = jnp.where((i * nqb + b) % nb != 0, 0.0, NEG).astype(F32)
            for h in range(N_HEADS):
                cols = slice(h * HEAD_DIM, (h + 1) * HEAD_DIM)
                coef = _alibi_coef(g, h)
                q = q_ref[rows, cols]
                kc, vc = k_ref[rows, cols], v_ref[rows, cols]
                if b == 0:
                    kp, vp = kp_ref[:, cols], vp_ref[:, cols]
                else:
                    kp, vp = k_ref[prows, cols], v_ref[prows, cols]
                s_c = jnp.where(m_cur, _dot(q, kc, "nt") * scale + coef * d_cur, NEG)
                s_p = jnp.where(m_prev, _dot(q, kp, "nt") * scale + coef * d_prev + pen, NEG)
                m = jnp.maximum(jnp.max(s_c, axis=-1, keepdims=True), jnp.max(s_p, axis=-1, keepdims=True))
                p_c = jnp.exp(s_c - m)
                p_p = jnp.exp(s_p - m)
                den = jnp.sum(p_c, axis=-1, keepdims=True) + jnp.sum(p_p, axis=-1, keepdims=True)
                inv = 1.0 / den
                o = _dot((p_c * inv).astype(BF16), vc, "nn") + _dot((p_p * inv).astype(BF16), vp, "nn")
                o_ref[rows, cols] = o
                lse_ref[rows, h:h + 1] = m + jnp.log(den)

    main = lambda c: pl.BlockSpec((None, tq, D_MODEL), lambda g, i: (g, i, c))
    prev = lambda c: pl.BlockSpec((None, Q_BLOCK, D_MODEL), lambda g, i: (g, jnp.maximum(i * nqb - 1, 0), c))
    return _pcall(
        body, name=name, grid=(N_GROUPS, S // tq),
        in_specs=[main(0), main(1), prev(1), main(2), prev(2)],
        out_specs=[pl.BlockSpec((None, tq, D_MODEL), lambda g, i: (g, i, 0)),
                   pl.BlockSpec((None, tq, N_HEADS), lambda g, i: (g, i, 0))],
        out_shape=[jax.ShapeDtypeStruct((N_GROUPS, S, D_MODEL), F32),
                   jax.ShapeDtypeStruct((N_GROUPS, S, N_HEADS), F32)],
        compiler_params=_params(("parallel", "parallel")))(qkv, qkv, qkv, qkv, qkv)


def _attn_bwd(name, qkv, do, lse, dl, tq=256):
    S = qkv.shape[1]
    nqb = tq // Q_BLOCK
    n_blocks = S // Q_BLOCK
    scale = HEAD_DIM ** -0.5

    def body(q_ref, k_ref, v_ref, kp_ref, vp_ref, qn_ref, do_ref, don_ref, lse_ref, lsen_ref, dl_ref, dln_ref,
             out_ref, acc_ref):
        g = pl.program_id(0)
        i = pl.program_id(1)
        nb = _blocks_per_segment(g, S)
        d_cur, d_prev, m_cur, m_prev = _band_consts()
        acc_ref[...] = jnp.zeros_like(acc_ref)

        def unit(q, k, v, do_b, lse_b, dl_b, dist, mask, pen, coef, dq_at, dkv_at, h):
            cols = slice(h * HEAD_DIM, (h + 1) * HEAD_DIM)
            s = jnp.where(mask, _dot(q, k, "nt") * scale + coef * dist + pen, NEG)
            p = jnp.exp(s - lse_b)
            dp = _dot(do_b, v, "nt")
            ds = ((p * (dp - dl_b)) * scale).astype(BF16)
            if dq_at is not None:
                acc_ref[dq_at, cols] += _dot(ds, k, "nn")
            if dkv_at is not None:
                kcols = slice(D_MODEL + h * HEAD_DIM, D_MODEL + (h + 1) * HEAD_DIM)
                vcols = slice(2 * D_MODEL + h * HEAD_DIM, 2 * D_MODEL + (h + 1) * HEAD_DIM)
                acc_ref[dkv_at, kcols] += _dot(ds, q, "tn")
                acc_ref[dkv_at, vcols] += _dot(p.astype(BF16), do_b, "tn")

        zero = jnp.zeros((), F32)
        for h in range(N_HEADS):
            cols = slice(h * HEAD_DIM, (h + 1) * HEAD_DIM)
            coef = _alibi_coef(g, h)
            for b in range(nqb):
                rows = slice(b * Q_BLOCK, (b + 1) * Q_BLOCK)
                prows = slice((b - 1) * Q_BLOCK, b * Q_BLOCK)
                q, do_b = q_ref[rows, cols], do_ref[rows, cols]
                lse_b, dl_b = lse_ref[rows, h:h + 1], dl_ref[rows, h:h + 1]
                unit(q, k_ref[rows, cols], v_ref[rows, cols], do_b, lse_b, dl_b, d_cur, m_cur, zero, coef,
                     rows, rows, h)
                pen = jnp.where((i * nqb + b) % nb != 0, 0.0, NEG).astype(F32)
                if b == 0:
                    unit(q, kp_ref[:, cols], vp_ref[:, cols], do_b, lse_b, dl_b, d_prev, m_prev, pen, coef,
                         rows, None, h)
                else:
                    unit(q, k_ref[prows, cols], v_ref[prows, cols], do_b, lse_b, dl_b, d_prev, m_prev, pen, coef,
                         rows, prows, h)
            nxt = (i + 1) * nqb
            pen = jnp.where(jnp.logical_and(nxt < n_blocks, nxt % nb != 0), 0.0, NEG).astype(F32)
            last = slice((nqb - 1) * Q_BLOCK, nqb * Q_BLOCK)
            unit(qn_ref[:, cols], k_ref[last, cols], v_ref[last, cols], don_ref[:, cols], lsen_ref[:, h:h + 1],
                 dln_ref[:, h:h + 1], d_prev, m_prev, pen, coef, None, last, h)
        out_ref[...] = acc_ref[...].astype(out_ref.dtype)

    nxt_blk = lambda i: jnp.minimum((i + 1) * nqb, n_blocks - 1)
    main = lambda c: pl.BlockSpec((None, tq, D_MODEL), lambda g, i: (g, i, c))
    prev = lambda c: pl.BlockSpec((None, Q_BLOCK, D_MODEL), lambda g, i: (g, jnp.maximum(i * nqb - 1, 0), c))
    row = pl.BlockSpec((None, tq, D_MODEL), lambda g, i: (g, i, 0))
    row_n = pl.BlockSpec((None, Q_BLOCK, D_MODEL), lambda g, i: (g, nxt_blk(i), 0))
    col = pl.BlockSpec((None, tq, N_HEADS), lambda g, i: (g, i, 0))
    col_n = pl.BlockSpec((None, Q_BLOCK, N_HEADS), lambda g, i: (g, nxt_blk(i), 0))
    return _pcall(
        body, name=name, grid=(N_GROUPS, S // tq),
        in_specs=[main(0), main(1), main(2), prev(1), prev(2), row_n, row, row_n, col, col_n, col, col_n],
        out_specs=pl.BlockSpec((None, tq, 3 * D_MODEL), lambda g, i: (g, i, 0)),
        out_shape=jax.ShapeDtypeStruct((N_GROUPS, S, 3 * D_MODEL), BF16),
        scratch_shapes=[pltpu.VMEM((tq, 3 * D_MODEL), F32)],
        compiler_params=_params(("parallel", "parallel")))(qkv, qkv, qkv, qkv, qkv, qkv, do, do, lse, lse, dl, dl)


def _group_weights(lse_ref):
    l0, l1, l2 = lse_ref[0], lse_ref[1], lse_ref[2]
    m = jnp.maximum(jnp.maximum(l0, l1), l2)
    e = [jnp.exp(l0 - m), jnp.exp(l1 - m), jnp.exp(l2 - m)]
    den = e[0] + e[1] + e[2]
    return [ei / den for ei in e]


def _merge_fwd(name, o, lse, tr=256):
    S = o.shape[1]

    def body(o_ref, lse_ref, out_ref):
        w = _group_weights(lse_ref)
        for h in range(N_HEADS):
            cols = slice(h * HEAD_DIM, (h + 1) * HEAD_DIM)
            acc = w[0][:, h:h + 1] * o_ref[0, :, cols]
            for gg in range(1, N_GROUPS):
                acc = acc + w[gg][:, h:h + 1] * o_ref[gg, :, cols]
            out_ref[:, cols] = acc.astype(out_ref.dtype)

    return _pcall(body, name=name, grid=(S // tr,),
                  in_specs=[pl.BlockSpec((N_GROUPS, tr, D_MODEL), lambda i: (0, i, 0)),
                            pl.BlockSpec((N_GROUPS, tr, N_HEADS), lambda i: (0, i, 0))],
                  out_specs=pl.BlockSpec((tr, D_MODEL), lambda i: (i, 0)),
                  out_shape=jax.ShapeDtypeStruct((S, D_MODEL), BF16), compiler_params=_params(("parallel",)))(o, lse)


def _merge_bwd(name, d_out, o, lse, tr=256):
    S = o.shape[1]

    def body(d_ref, o_ref, lse_ref, do_ref, dl_ref):
        w = _group_weights(lse_ref)
        for h in range(N_HEADS):
            cols = slice(h * HEAD_DIM, (h + 1) * HEAD_DIM)
            dv = d_ref[:, cols]
            merged = w[0][:, h:h + 1] * o_ref[0, :, cols]
            for gg in range(1, N_GROUPS):
                merged = merged + w[gg][:, h:h + 1] * o_ref[gg, :, cols]
            dsum = jnp.sum(dv * merged, axis=-1, keepdims=True)
            for gg in range(N_GROUPS):
                wg = w[gg][:, h:h + 1]
                do_ref[gg, :, cols] = (wg * dv).astype(do_ref.dtype)
                dl_ref[gg, :, h:h + 1] = wg * dsum

    big = pl.BlockSpec((N_GROUPS, tr, D_MODEL), lambda i: (0, i, 0))
    small = pl.BlockSpec((N_GROUPS, tr, N_HEADS), lambda i: (0, i, 0))
    return _pcall(body, name=name, grid=(S // tr,),
                  in_specs=[pl.BlockSpec((tr, D_MODEL), lambda i: (i, 0)), big, small], out_specs=[big, small],
                  out_shape=[jax.ShapeDtypeStruct((N_GROUPS, S, D_MODEL), BF16),
                             jax.ShapeDtypeStruct((N_GROUPS, S, N_HEADS), F32)],
                  compiler_params=_params(("parallel",)))(d_out, o, lse)


def _shift_rows(a, k):
    return pltpu.roll(a, k % a.shape[0], axis=0)


def _pool_level(g, levels):
    return jnp.where(g == 0, levels[0], jnp.where(g == 1, levels[1], jnp.where(g == 2, levels[2], levels[3])))


def _pool_fwd(name, u, w_group, scale, x_in, tr=1024):
    S, D = u.shape
    H = POOL_HALO

    def body(u_ref, halo_ref, w_ref, sc_ref, x_ref, y_ref, z_ref, xo_ref):
        g = pl.program_id(0)
        i = pl.program_id(1)
        uv = u_ref[...]
        halo = jnp.where(i > 0, halo_ref[...], 0.0)
        s = jnp.concatenate([halo, uv], axis=0)
        levels = []
        for k in (1, 2, 4, 8):
            s = s + _shift_rows(s, k)
            levels.append(s[H:])
        t = i * tr + lax.broadcasted_iota(jnp.int32, (tr, 1), 0)
        cnt = jnp.minimum(t + 1, jnp.left_shift(2, g)).astype(F32)
        y = _pool_level(g, levels) / cnt - uv
        yb = y.astype(BF16)
        z = _dot(yb, w_ref[...], "nn")
        y_ref[...] = yb
        z_ref[...] = z
        xo_ref[...] = x_ref[...] + z * sc_ref[...]

    blk = pl.BlockSpec((tr, POOL_DIM), lambda g, i: (i, g))
    return _pcall(
        body, name=name, grid=(D // POOL_DIM, S // tr),
        in_specs=[blk, pl.BlockSpec((H, POOL_DIM), lambda g, i: (jnp.maximum(i * (tr // H) - 1, 0), g)),
                  pl.BlockSpec((None, POOL_DIM, POOL_DIM), lambda g, i: (g, 0, 0)),
                  pl.BlockSpec((1, POOL_DIM), lambda g, i: (0, g)), blk],
        out_specs=[blk, blk, blk],
        out_shape=[jax.ShapeDtypeStruct((S, D), BF16), jax.ShapeDtypeStruct((S, D), F32),
                   jax.ShapeDtypeStruct((S, D), F32)],
        compiler_params=_params(("parallel", "parallel")))(u, u, w_group, scale, x_in)


def _pool_bwd(name, d_out, z, y, scale, w_group, tr=1024):
    S, D = d_out.shape
    H = POOL_HALO

    def body(d_ref, dn_ref, z_ref, y_ref, sc_ref, w_ref, du_ref, dw_ref, dsc_ref):
        g = pl.program_id(0)
        i = pl.program_id(1)
        dv = d_ref[...]
        dz = jnp.concatenate([dv, dn_ref[...]], axis=0) * sc_ref[...]
        dzb = dz.astype(BF16)
        dy = _dot(dzb, w_ref[...], "nt")
        t = i * tr + lax.broadcasted_iota(jnp.int32, (tr + H, 1), 0)
        cnt = jnp.minimum(t + 1, jnp.left_shift(2, g)).astype(F32)
        s = jnp.where(t < S, dy / cnt, 0.0)
        levels = []
        for k in (1, 2, 4, 8):
            s = s + _shift_rows(s, -k)
            levels.append(s[:tr])
        du_ref[...] = (_pool_level(g, levels) - dy[:tr]).astype(du_ref.dtype)
        dw = _dot(y_ref[...], dzb[:tr], "tn")
        dsc = jnp.sum(dv * z_ref[...], axis=0, keepdims=True)

        @pl.when(i == 0)
        def _():
            dw_ref[...] = dw
            dsc_ref[...] = dsc

        @pl.when(i > 0)
        def _():
            dw_ref[...] += dw
            dsc_ref[...] += dsc

    blk = pl.BlockSpec((tr, POOL_DIM), lambda g, i: (i, g))
    vec = pl.BlockSpec((1, POOL_DIM), lambda g, i: (0, g))
    mat = pl.BlockSpec((None, POOL_DIM, POOL_DIM), lambda g, i: (g, 0, 0))
    nxt = pl.BlockSpec((H, POOL_DIM), lambda g, i: (jnp.minimum((i + 1) * (tr // H), S // H - 1), g))
    return _pcall(
        body, name=name, grid=(D // POOL_DIM, S // tr), in_specs=[blk, nxt, blk, blk, vec, mat],
        out_specs=[blk, mat, vec],
        out_shape=[jax.ShapeDtypeStruct((S, D), BF16), jax.ShapeDtypeStruct((D // POOL_DIM, POOL_DIM, POOL_DIM), F32),
                   jax.ShapeDtypeStruct((1, D), F32)],
        compiler_params=_params(("parallel", "arbitrary")))(d_out, d_out, z, y, scale, w_group)


def _row_tile(rows, cols, target_bytes=1 << 20):
    best = rows
    for tr in range(8, rows + 1, 8):
        if rows % tr == 0 and tr * cols * 4 <= target_bytes:
            best = tr
    return best if best * cols * 4 <= 4 * target_bytes else rows


def _adamw(name, w, g, m, v):
    R, C = w.shape
    tr = _row_tile(R, C) if R % 8 == 0 else R

    def body(w_ref, g_ref, m_ref, v_ref, d_ref, mo_ref, vo_ref):
        gv = g_ref[...]
        m2 = ADAM_B1 * m_ref[...] + (1.0 - ADAM_B1) * gv
        v2 = ADAM_B2 * v_ref[...] + (1.0 - ADAM_B2) * (gv * gv)
        m_hat = m2 / (1.0 - ADAM_B1 ** ADAM_STEP)
        v_hat = v2 / (1.0 - ADAM_B2 ** ADAM_STEP)
        d_ref[...] = -ADAM_LR * (m_hat / (jnp.sqrt(v_hat) + ADAM_EPS) + ADAM_WD * w_ref[...])
        mo_ref[...] = m2
        vo_ref[...] = v2

    blk = pl.BlockSpec((tr, C), lambda i: (i, 0))
    sds = jax.ShapeDtypeStruct((R, C), F32)
    return _pcall(body, name=name, grid=(R // tr,), in_specs=[blk] * 4, out_specs=[blk] * 3, out_shape=[sds] * 3,
                  compiler_params=_params(("parallel",)))(w, g, m, v)


def _sum_leading(name, a, out_dtype):
    n, R, C = a.shape
    tr = _row_tile(R, C) if R % 16 == 0 else R
    if tr % 16:
        tr = R

    def body(a_ref, o_ref):
        acc = a_ref[0].astype(F32)
        for j in range(1, n):
            acc = acc + a_ref[j].astype(F32)
        o_ref[...] = acc.astype(o_ref.dtype)

    return _pcall(body, name=name, grid=(R // tr,), in_specs=[pl.BlockSpec((n, tr, C), lambda i: (0, i, 0))],
                  out_specs=pl.BlockSpec((tr, C), lambda i: (i, 0)), out_shape=jax.ShapeDtypeStruct((R, C), out_dtype),
                  compiler_params=_params(("parallel",)))(a)


def _add_my_half(name, p, q, core):
    n, R, C = p.shape
    half = R // 2

    def body(c_ref, p_ref, q_ref, o_ref):
        o_ref[...] = (p_ref[...].astype(F32) + q_ref[...].astype(F32)).astype(o_ref.dtype)

    grid_spec = pltpu.PrefetchScalarGridSpec(
        num_scalar_prefetch=1, grid=(n,),
        in_specs=[pl.BlockSpec((None, half, C), lambda s, c_ref: (s, c_ref[0], 0)),
                  pl.BlockSpec((None, half, C), lambda s, c_ref: (s, 0, 0))],
        out_specs=pl.BlockSpec((None, half, C), lambda s, c_ref: (s, 0, 0)))
    return _pcall(body, name=name, grid_spec=grid_spec, out_shape=jax.ShapeDtypeStruct((n, half, C), BF16),
                  compiler_params=_params(("parallel",)))(core, p, q)


ANY = pl.BlockSpec(memory_space=pl.ANY)


def _place():
    x, y, c = lax.axis_index("x"), lax.axis_index("y"), lax.axis_index("c")
    other_chips = [(1 - x, y), (x, 1 - y), (1 - x, 1 - y)]
    return x, y, c, other_chips


def _gather_weights(name, shards, split):
    n = len(shards)

    def body(*refs):
        ins, outs = refs[:n], refs[n:2 * n]
        ici_send, ici_recv, d2d_send, d2d_recv, local_sem = refs[2 * n:]
        x, y, c, chips = _place()
        me = 2 * x + y
        sibling = (x, y, 1 - c)

        def piece(t, chip, core_half):
            if not split[t]:
                return outs[t].at[chip]
            half = ins[t].shape[0] // 2
            return outs[t].at[chip, pl.ds(core_half * half, half)]

        def own_piece(t):
            if not split[t]:
                return ins[t]
            half = ins[t].shape[0] // 2
            return ins[t].at[pl.ds(c * half, half)]

        local = [pltpu.make_async_copy(ins[t], outs[t].at[me], local_sem.at[t]) for t in range(n)]
        for cp in local:
            cp.start()
        sends = []
        for t in range(n):
            for j, (px, py) in enumerate(chips):
                sends.append(pltpu.make_async_remote_copy(
                    src_ref=own_piece(t), dst_ref=piece(t, me, c), send_sem=ici_send.at[3 * t + j],
                    recv_sem=ici_recv.at[3 * t + j], device_id=(px, py, c), device_id_type=MESH))
        for cp in sends:
            cp.start()
        passed = []
        for t in range(n):
            for j, (px, py) in enumerate(chips):
                chip = 2 * px + py
                landed = piece(t, chip, c)
                pltpu.make_async_remote_copy(
                    src_ref=own_piece(t), dst_ref=landed, send_sem=ici_send.at[3 * t + j],
                    recv_sem=ici_recv.at[3 * t + j], device_id=(px, py, c), device_id_type=MESH).wait_recv()
                if split[t]:
                    fwd = pltpu.make_async_remote_copy(
                        src_ref=landed, dst_ref=landed, send_sem=d2d_send.at[3 * t + j],
                        recv_sem=d2d_recv.at[3 * t + j], device_id=sibling, device_id_type=MESH)
                    fwd.start()
                    passed.append(fwd)
        for t in range(n):
            if split[t]:
                for j, (px, py) in enumerate(chips):
                    theirs = piece(t, 2 * px + py, 1 - c)
                    pltpu.make_async_remote_copy(
                        src_ref=theirs, dst_ref=theirs, send_sem=d2d_send.at[3 * t + j],
                        recv_sem=d2d_recv.at[3 * t + j], device_id=sibling, device_id_type=MESH).wait_recv()
        for cp in sends + passed:
            cp.wait_send()
        for cp in local:
            cp.wait()

    out_shape = [jax.ShapeDtypeStruct((N_CHIPS,) + s.shape, s.dtype) for s in shards]
    return _pcall(body, name=name, in_specs=[ANY] * n, out_specs=[ANY] * n, out_shape=out_shape,
                  scratch_shapes=[pltpu.SemaphoreType.DMA((3 * n,)), pltpu.SemaphoreType.DMA((3 * n,)),
                                  pltpu.SemaphoreType.DMA((3 * n,)), pltpu.SemaphoreType.DMA((3 * n,)),
                                  pltpu.SemaphoreType.DMA((n,))])(*shards)


def _swap_other_half(name, parts):
    n = len(parts)

    def body(*refs):
        ins, outs = refs[:n], refs[n:2 * n]
        send_sem, recv_sem = refs[2 * n:]
        x, y, c, _ = _place()
        copies = []
        for t in range(n):
            half = ins[t].shape[1] // 2
            copies.append(pltpu.make_async_remote_copy(
                src_ref=ins[t].at[:, pl.ds((1 - c) * half, half)], dst_ref=outs[t], send_sem=send_sem.at[t],
                recv_sem=recv_sem.at[t], device_id=(x, y, 1 - c), device_id_type=MESH))
        for cp in copies:
            cp.start()
        for cp in copies:
            cp.wait()

    out_shape = [jax.ShapeDtypeStruct((p.shape[0], p.shape[1] // 2, p.shape[2]), p.dtype) for p in parts]
    return _pcall(body, name=name, in_specs=[ANY] * n, out_specs=[ANY] * n, out_shape=out_shape,
                  scratch_shapes=[pltpu.SemaphoreType.DMA((n,)), pltpu.SemaphoreType.DMA((n,))])(*parts)


def _scatter_to_owners(name, sums):
    n = len(sums)

    def body(*refs):
        ins, outs = refs[:n], refs[n:2 * n]
        send_sem, recv_sem, local_sem = refs[2 * n:]
        x, y, c, chips = _place()
        me = 2 * x + y
        local = [pltpu.make_async_copy(ins[t].at[me], outs[t].at[me], local_sem.at[t]) for t in range(n)]
        for cp in local:
            cp.start()
        copies = []
        for t in range(n):
            for j, (px, py) in enumerate(chips):
                copies.append(pltpu.make_async_remote_copy(
                    src_ref=ins[t].at[2 * px + py], dst_ref=outs[t].at[me], send_sem=send_sem.at[3 * t + j],
                    recv_sem=recv_sem.at[3 * t + j], device_id=(px, py, c), device_id_type=MESH))
        for cp in copies:
            cp.start()
        for t in range(n):
            for j, (px, py) in enumerate(chips):
                pltpu.make_async_remote_copy(
                    src_ref=ins[t].at[me], dst_ref=outs[t].at[2 * px + py], send_sem=send_sem.at[3 * t + j],
                    recv_sem=recv_sem.at[3 * t + j], device_id=(px, py, c), device_id_type=MESH).wait_recv()
        for cp in copies:
            cp.wait_send()
        for cp in local:
            cp.wait()

    out_shape = [jax.ShapeDtypeStruct(s.shape, s.dtype) for s in sums]
    return _pcall(body, name=name, in_specs=[ANY] * n, out_specs=[ANY] * n, out_shape=out_shape,
                  scratch_shapes=[pltpu.SemaphoreType.DMA((3 * n,)), pltpu.SemaphoreType.DMA((3 * n,)),
                                  pltpu.SemaphoreType.DMA((n,))])(*sums)


def _share_halves(name, halves, layout):
    n = len(halves)
    out_shapes = {}
    for t, (o, layer) in enumerate(layout):
        r2, cc = halves[t].shape
        out_shapes[o] = (2 * r2, cc) if layer is None else (2, 2 * r2, cc)
    n_out = len(out_shapes)

    def body(*refs):
        ins, outs = refs[:n], refs[n:n + n_out]
        send_sem, recv_sem, local_sem = refs[n + n_out:]
        x, y, c, _ = _place()
        local, remote = [], []
        for t, (o, layer) in enumerate(layout):
            r2 = ins[t].shape[0]
            dst = outs[o] if layer is None else outs[o].at[layer]
            dst = dst.at[pl.ds(c * r2, r2)]
            local.append(pltpu.make_async_copy(ins[t], dst, local_sem.at[t]))
            remote.append(pltpu.make_async_remote_copy(
                src_ref=ins[t], dst_ref=dst, send_sem=send_sem.at[t], recv_sem=recv_sem.at[t],
                device_id=(x, y, 1 - c), device_id_type=MESH))
        for cp in local + remote:
            cp.start()
        for t, (o, layer) in enumerate(layout):
            r2 = ins[t].shape[0]
            dst = outs[o] if layer is None else outs[o].at[layer]
            theirs = dst.at[pl.ds((1 - c) * r2, r2)]
            pltpu.make_async_remote_copy(
                src_ref=ins[t], dst_ref=theirs, send_sem=send_sem.at[t], recv_sem=recv_sem.at[t],
                device_id=(x, y, 1 - c), device_id_type=MESH).wait_recv()
        for cp in remote:
            cp.wait_send()
        for cp in local:
            cp.wait()

    out_shape = [jax.ShapeDtypeStruct(out_shapes[o], F32) for o in range(n_out)]
    return _pcall(body, name=name, in_specs=[ANY] * n, out_specs=[ANY] * n_out, out_shape=out_shape,
                  scratch_shapes=[pltpu.SemaphoreType.DMA((n,)), pltpu.SemaphoreType.DMA((n,)),
                                  pltpu.SemaphoreType.DMA((n,))])(*halves)


def _gather_small(name, v):
    def body(v_ref, out_ref, send_sem, recv_sem, local_sem):
        x, y, c, _ = _place()
        me = 4 * x + 2 * y + c
        flips = [(fx, fy, fc) for fx in (0, 1) for fy in (0, 1) for fc in (0, 1)][1:]
        local = pltpu.make_async_copy(v_ref, out_ref.at[me], local_sem)
        local.start()
        copies = []
        for j, (fx, fy, fc) in enumerate(flips):
            peer = (x ^ fx, y ^ fy, c ^ fc)
            copies.append(pltpu.make_async_remote_copy(
                src_ref=v_ref, dst_ref=out_ref.at[me], send_sem=send_sem.at[j], recv_sem=recv_sem.at[j],
                device_id=peer, device_id_type=MESH))
        for cp in copies:
            cp.start()
        for j, (fx, fy, fc) in enumerate(flips):
            peer = (x ^ fx, y ^ fy, c ^ fc)
            pltpu.make_async_remote_copy(
                src_ref=v_ref, dst_ref=out_ref.at[4 * peer[0] + 2 * peer[1] + peer[2]], send_sem=send_sem.at[j],
                recv_sem=recv_sem.at[j], device_id=peer, device_id_type=MESH).wait_recv()
        for cp in copies:
            cp.wait_send()
        local.wait()

    return _pcall(body, name=name, in_specs=[ANY], out_specs=ANY,
                  out_shape=jax.ShapeDtypeStruct((8,) + v.shape, v.dtype),
                  scratch_shapes=[pltpu.SemaphoreType.DMA((7,)), pltpu.SemaphoreType.DMA((7,)),
                                  pltpu.SemaphoreType.DMA])(v)


def _fold(a, dil):
    if dil == 1:
        return a
    S, C = a.shape
    return a.reshape(S // dil, dil, C).transpose(1, 0, 2).reshape(S, C)


def _unfold(a, dil):
    if dil == 1:
        return a
    S, C = a.shape
    return a.reshape(dil, S // dil, C).transpose(1, 0, 2).reshape(S, C)


def _fold_groups(a):
    return jnp.stack([_fold(a[g] if a.ndim == 3 else a, d) for g, d in enumerate(ATTN_DILATIONS)])


def _unfold_groups(a):
    return jnp.stack([_unfold(a[g], d) for g, d in enumerate(ATTN_DILATIONS)])


def _local_step(xs, tgt, attn_norm, w_qkv, w_o, pool_norm, w_p, w_pg, pool_scale, ffn_norm, w_gu, w_d, final_norm):
    h0 = _rms_fwd("norm_attn", xs, attn_norm)
    hf = _fold_groups(h0)
    qkv = _qkv_proj("qkv_proj", hf, w_qkv)
    o_f, lse_f = _attn_fwd("attn_fwd", qkv)
    o_t, lse_t = _unfold_groups(o_f), _unfold_groups(lse_f)
    merged = _merge_fwd("merge_fwd", o_t, lse_t)
    x1 = _mm_rows("attn_out", merged, w_o, F32, 512, res=xs)
    h1 = _rms_fwd("norm_ffn0", x1, ffn_norm[0:1])
    gu0, act0 = _ffn_up("ffn0_up", h1, w_gu[0])
    x2 = _mm_rows("ffn0_down", act0, w_d[0], F32, 512, res=x1)
    h2 = _rms_fwd("norm_pool", x2, pool_norm)
    u = _mm_rows("pool_in", h2, w_p, F32, 512)
    y, z, x3 = _pool_fwd("pool_fwd", u, w_pg, pool_scale, x2)
    h3 = _rms_fwd("norm_ffn1", x3, ffn_norm[1:2])
    gu1, act1 = _ffn_up("ffn1_up", h3, w_gu[1])
    x4 = _mm_rows("ffn1_down", act1, w_d[1], F32, 512, res=x3)
    loss, dx4, d_final = _loss_bwd("loss_head", x4, final_norm, tgt)

    dgu1 = _ffn_down_bwd("ffn1_down_bwd", dx4, w_d[1], gu1)
    dw_d1 = _mm_tn("ffn1_down_dw", act1, dx4, FF_SHARD, 1024)
    dw_gu1 = _ffn_dw_up("ffn1_up_dw", h3, dgu1)
    dh3 = _ffn_dh("ffn1_up_dh", dgu1, w_gu[1])
    dx3, d_ffn1 = _rms_bwd("norm_ffn1_bwd", x3, ffn_norm[1:2], dh3, dx4)

    du, dw_pg, d_scale = _pool_bwd("pool_bwd", dx3, z, y, pool_scale, w_pg)
    dw_p = _mm_tn("pool_in_dw", h2, du, D_MODEL, 1024)
    dh2 = _mm_nt_rows("pool_in_dh", du, w_p, 512)
    dx2, d_pool = _rms_bwd("norm_pool_bwd", x2, pool_norm, dh2, dx3)

    dgu0 = _ffn_down_bwd("ffn0_down_bwd", dx2, w_d[0], gu0)
    dw_d0 = _mm_tn("ffn0_down_dw", act0, dx2, FF_SHARD, 1024)
    dw_gu0 = _ffn_dw_up("ffn0_up_dw", h1, dgu0)
    dh1 = _ffn_dh("ffn0_up_dh", dgu0, w_gu[0])
    dx1, d_ffn0 = _rms_bwd("norm_ffn0_bwd", x1, ffn_norm[0:1], dh1, dx2)

    d_merged = _mm_nt_rows("attn_out_dh", dx1, w_o, 512)
    dw_o = _mm_tn("attn_out_dw", merged, dx1, D_MODEL, 1024)
    do_t, dl_t = _merge_bwd("merge_bwd", d_merged, o_t, lse_t)
    dqkv = _attn_bwd("attn_bwd", qkv, _fold_groups(do_t), lse_f, _fold_groups(dl_t))
    dw_qkv = _qkv_dw("qkv_dw", hf, dqkv)
    dhf = _qkv_dh("qkv_dh", dqkv, w_qkv)
    dh0 = dhf[0] + _unfold(dhf[1], ATTN_DILATIONS[1]) + _unfold(dhf[2], ATTN_DILATIONS[2])
    grad_x, d_attn = _rms_bwd("norm_attn_bwd", xs, attn_norm, dh0, dx1)

    big = dict(qkv=dw_qkv, o=dw_o, p=dw_p, pg=dw_pg, gu=(dw_gu0, dw_gu1), d=(dw_d0, dw_d1))
    small = dict(attn=d_attn, ffn0=d_ffn0, ffn1=d_ffn1, final=d_final, pool=d_pool, scale=d_scale)
    return loss, grad_x, big, small


def kernel(x, attn_norm, w_qkv, w_attn_out, pool_norm, w_pool_in, w_pool_group, pool_scale, ffn_norm, w_ffn_gate_up, w_ffn_down, final_norm, loss_target, m_attn_norm, m_w_qkv, m_w_attn_out, m_pool_norm, m_w_pool_in, m_w_pool_group, m_pool_scale, m_ffn_norm, m_w_ffn_gate_up, m_w_ffn_down, m_final_norm, v_attn_norm, v_w_qkv, v_w_attn_out, v_pool_norm, v_w_pool_in, v_w_pool_group, v_pool_scale, v_ffn_norm, v_w_ffn_gate_up, v_w_ffn_down, v_final_norm):
    D = D_MODEL
    chip = 2 * lax.axis_index("x") + lax.axis_index("y")
    core = lax.axis_index("c")
    pg_rows = w_pool_group.shape[2]

    shards = [w_qkv[0], w_attn_out[0], w_pool_in[0], w_pool_group[0].reshape(4 * pg_rows, POOL_DIM),
              w_ffn_gate_up[0], w_ffn_gate_up[1], w_ffn_down[0], w_ffn_down[1]]
    shards = [s.astype(BF16) for s in shards] + [pool_norm, pool_scale]
    gathered = _gather_weights("gather_weights", shards, [True] * 8 + [False] * 2)
    g_qkv, g_o, g_p, g_pg, g_gu0, g_gu1, g_d0, g_d1, g_pn, g_ps = gathered
    w_pg_full = g_pg.reshape(N_CHIPS, 4, pg_rows, POOL_DIM).transpose(1, 0, 2, 3).reshape(4, POOL_DIM, POOL_DIM)
    pool_norm_full = g_pn.reshape(1, D)
    pool_scale_full = g_ps.reshape(1, D)

    loss_part, grad_x, big, small = _local_step(
        x[0], loss_target[0], attn_norm, g_qkv, g_o.reshape(D, D), pool_norm_full, g_p.reshape(D, D), w_pg_full,
        pool_scale_full, ffn_norm, (g_gu0, g_gu1), (g_d0.reshape(D_FF, D), g_d1.reshape(D_FF, D)),
        final_norm.reshape(1, D))

    dw_pg = big["pg"].reshape(4, N_CHIPS, pg_rows, POOL_DIM).transpose(1, 0, 2, 3)
    dw_pg = dw_pg.reshape(N_CHIPS, 4 * pg_rows, POOL_DIM).astype(BF16)
    parts = [big["qkv"], big["o"].reshape(N_CHIPS, D // N_CHIPS, D), big["p"].reshape(N_CHIPS, D // N_CHIPS, D),
             dw_pg, big["gu"][0], big["gu"][1], big["d"][0].reshape(N_CHIPS, D_FF // N_CHIPS, D),
             big["d"][1].reshape(N_CHIPS, D_FF // N_CHIPS, D)]
    names = ["qkv", "o", "p", "pg", "gu0", "gu1", "d0", "d1"]
    theirs = _swap_other_half("grads_to_sibling", parts)
    core_arr = core.astype(jnp.int32).reshape(1)
    chip_sums = [_add_my_half("chip_sum_" + nm, p, q, core_arr) for nm, p, q in zip(names, parts, theirs)]
    landed = _scatter_to_owners("grads_to_owners", chip_sums)
    halves = [_sum_leading("owner_sum_" + nm, a, F32) for nm, a in zip(names, landed)]
    layout = [(0, None), (1, None), (2, None), (3, None), (4, 0), (4, 1), (5, 0), (5, 1)]
    g_w_qkv, g_w_o, g_w_p, g_w_pg, g_w_gu, g_w_d = _share_halves("halves_to_sibling", halves, layout)

    rows = jnp.concatenate([small["attn"], small["ffn0"], small["ffn1"], small["final"], small["pool"],
                            small["scale"], jnp.zeros((2, D), F32)], axis=0)
    all_rows = _gather_small("gather_gain_grads", rows)
    tot = _sum_leading("sum_gain_grads", all_rows, F32)
    g_attn_norm = tot[0:1]
    g_ffn_norm = tot[1:3]
    g_final_norm = tot[3]
    g_pool_norm = lax.dynamic_slice(tot, (4, chip * POOL_DIM), (1, POOL_DIM))
    g_pool_scale = lax.dynamic_slice(tot, (5, chip * POOL_DIM), (1, POOL_DIM))

    loss = lax.psum(loss_part[0, 0], ("x", "y", "c"))

    def update(name, w, g, m, v):
        shape = w.shape
        cols = shape[-1]
        as2d = lambda a: a.reshape(-1, cols)
        d, m2, v2 = _adamw("adamw_" + name, as2d(w), as2d(g), as2d(m), as2d(v))
        return g.reshape(shape), d.reshape(shape), m2.reshape(shape), v2.reshape(shape)

    results = [
        update("attn_norm", attn_norm, g_attn_norm, m_attn_norm, v_attn_norm),
        update("w_qkv", w_qkv, g_w_qkv, m_w_qkv, v_w_qkv),
        update("w_attn_out", w_attn_out, g_w_o, m_w_attn_out, v_w_attn_out),
        update("pool_norm", pool_norm, g_pool_norm, m_pool_norm, v_pool_norm),
        update("w_pool_in", w_pool_in, g_w_p, m_w_pool_in, v_w_pool_in),
        update("w_pool_group", w_pool_group, g_w_pg, m_w_pool_group, v_w_pool_group),
        update("pool_scale", pool_scale, g_pool_scale, m_pool_scale, v_pool_scale),
        update("ffn_norm", ffn_norm, g_ffn_norm, m_ffn_norm, v_ffn_norm),
        update("w_ffn_gate_up", w_ffn_gate_up, g_w_gu, m_w_ffn_gate_up, v_w_ffn_gate_up),
        update("w_ffn_down", w_ffn_down, g_w_d, m_w_ffn_down, v_w_ffn_down),
        update("final_norm", final_norm, g_final_norm, m_final_norm, v_final_norm),
    ]
    grads = [r[0] for r in results]
    deltas = [r[1] for r in results]
    new_m = [r[2] for r in results]
    new_v = [r[3] for r in results]
    return (loss, grad_x[None], *grads, *deltas, *new_m, *new_v)
```

```python
import functools

import jax
import jax.numpy as jnp
from jax import lax
from jax.experimental import pallas as pl
from jax.experimental.pallas import tpu as pltpu

F32 = jnp.float32
BF16 = jnp.bfloat16
MESH = pl.DeviceIdType.MESH

D_MODEL = 1024
N_HEADS = 8
HEAD_DIM = 128
Q_BLOCK = 128
ATTN_DILATIONS = (1, 4, 16)
N_GROUPS = 3
D_FF = 2816
N_CHIPS = 4
FF_SHARD = 2 * D_FF // N_CHIPS
QKV_SHARD = 3 * 3 * D_MODEL // N_CHIPS
QKV_TILE = 768
POOL_WINDOWS = (2, 4, 8, 16)
POOL_DIM = 256
POOL_HALO = 16
RMS_EPS = 1e-6
NEG = -1e30
ADAM_LR = 0.001
ADAM_B1 = 0.9
ADAM_B2 = 0.999
ADAM_EPS = 1e-08
ADAM_WD = 0.01
ADAM_STEP = 10
VMEM_LIMIT = 56 * 1024 * 1024

_DN = {
    "nn": (((1,), (0,)), ((), ())),
    "nt": (((1,), (1,)), ((), ())),
    "tn": (((0,), (0,)), ((), ())),
}


def _pcall(body, **kw):
    return pl.pallas_call(body, **kw)


def _params(sem):
    return pltpu.CompilerParams(dimension_semantics=sem, vmem_limit_bytes=VMEM_LIMIT)


def _dot(a, b, mode):
    return lax.dot_general(a, b, _DN[mode], preferred_element_type=F32)


def _mm(name, mode, grid, a, a_spec, b, b_spec, out_shape, o_spec, acc_shape, res=None, res_spec=None):
    nk = grid[-1]
    has_res = res is not None

    def body(*refs):
        a_ref, b_ref = refs[0], refs[1]
        res_ref = refs[2] if has_res else None
        o_ref = refs[2 + has_res]
        part = _dot(a_ref[...].astype(BF16), b_ref[...].astype(BF16), mode)
        if nk == 1:
            if has_res:
                part = part + res_ref[...]
            o_ref[...] = part.astype(o_ref.dtype)
            return
        acc_ref = refs[3 + has_res]
        k = pl.program_id(len(grid) - 1)

        @pl.when(k == 0)
        def _():
            acc_ref[...] = part

        @pl.when(k > 0)
        def _():
            acc_ref[...] += part

        @pl.when(k == nk - 1)
        def _():
            r = acc_ref[...]
            if has_res:
                r = r + res_ref[...]
            o_ref[...] = r.astype(o_ref.dtype)

    in_specs = [a_spec, b_spec] + ([res_spec] if has_res else [])
    operands = [a, b] + ([res] if has_res else [])
    scratch = [] if nk == 1 else [pltpu.VMEM(acc_shape, F32)]
    sem = ("parallel",) * (len(grid) - 1) + ("arbitrary",)
    return _pcall(body, name=name, grid=grid, in_specs=in_specs, out_specs=o_spec, out_shape=out_shape,
                  scratch_shapes=scratch, compiler_params=_params(sem))(*operands)


def _mm_rows(name, a, b, out_dtype, tm, res=None):
    M, K = a.shape
    N = b.shape[1]
    return _mm(name, "nn", (M // tm, 1), a, pl.BlockSpec((tm, K), lambda i, k: (i, 0)),
               b, pl.BlockSpec((K, N), lambda i, k: (0, 0)),
               jax.ShapeDtypeStruct((M, N), out_dtype), pl.BlockSpec((tm, N), lambda i, k: (i, 0)), None,
               res=res, res_spec=pl.BlockSpec((tm, N), lambda i, k: (i, 0)))


def _mm_nt_rows(name, a, b, tm):
    M, N = a.shape
    K = b.shape[0]
    return _mm(name, "nt", (M // tm, 1), a, pl.BlockSpec((tm, N), lambda i, k: (i, 0)),
               b, pl.BlockSpec((K, N), lambda i, k: (0, 0)),
               jax.ShapeDtypeStruct((M, K), F32), pl.BlockSpec((tm, K), lambda i, k: (i, 0)), None)


def _mm_tn(name, a, b, tm, tk):
    T, M = a.shape
    N = b.shape[1]
    return _mm(name, "tn", (M // tm, T // tk), a, pl.BlockSpec((tk, tm), lambda i, k: (k, i)),
               b, pl.BlockSpec((tk, N), lambda i, k: (k, 0)),
               jax.ShapeDtypeStruct((M, N), BF16), pl.BlockSpec((tm, N), lambda i, k: (i, 0)), (tm, N))


def _rms_fwd(name, x, g, tr=512):
    S, D = x.shape

    def body(x_ref, g_ref, o_ref):
        xf = x_ref[...]
        r = lax.rsqrt(jnp.mean(xf * xf, axis=-1, keepdims=True) + RMS_EPS)
        o_ref[...] = ((xf * r) * g_ref[...]).astype(o_ref.dtype)

    return _pcall(body, name=name, grid=(S // tr,),
                  in_specs=[pl.BlockSpec((tr, D), lambda i: (i, 0)), pl.BlockSpec((1, D), lambda i: (0, 0))],
                  out_specs=pl.BlockSpec((tr, D), lambda i: (i, 0)),
                  out_shape=jax.ShapeDtypeStruct((S, D), BF16), compiler_params=_params(("parallel",)))(x, g)


def _rms_bwd(name, x, g, dh, dres, tr=512):
    S, D = x.shape

    def body(x_ref, g_ref, dh_ref, dres_ref, dx_ref, dg_ref):
        i = pl.program_id(0)
        xf = x_ref[...]
        r = lax.rsqrt(jnp.mean(xf * xf, axis=-1, keepdims=True) + RMS_EPS)
        xh = xf * r
        dh_v = dh_ref[...]
        dg = jnp.sum(dh_v * xh, axis=0, keepdims=True)
        t = dh_v * g_ref[...]
        dx_ref[...] = dres_ref[...] + r * (t - xh * jnp.mean(t * xh, axis=-1, keepdims=True))

        @pl.when(i == 0)
        def _():
            dg_ref[...] = dg

        @pl.when(i > 0)
        def _():
            dg_ref[...] += dg

    row = pl.BlockSpec((tr, D), lambda i: (i, 0))
    vec = pl.BlockSpec((1, D), lambda i: (0, 0))
    return _pcall(body, name=name, grid=(S // tr,), in_specs=[row, vec, row, row], out_specs=[row, vec],
                  out_shape=[jax.ShapeDtypeStruct((S, D), F32), jax.ShapeDtypeStruct((1, D), F32)],
                  compiler_params=_params(("arbitrary",)))(x, g, dh, dres)


def _loss_bwd(name, x, g, tgt, tr=512):
    S, D = x.shape

    def body(x_ref, g_ref, t_ref, loss_ref, dx_ref, dg_ref):
        i = pl.program_id(0)
        xf = x_ref[...]
        r = lax.rsqrt(jnp.mean(xf * xf, axis=-1, keepdims=True) + RMS_EPS)
        xh = xf * r
        gv = g_ref[...]
        e = xh * gv - t_ref[...]
        lp = 0.5 * jnp.sum(jnp.mean(e * e, axis=-1, keepdims=True), axis=0, keepdims=True)
        dy = e * (1.0 / D)
        dg = jnp.sum(dy * xh, axis=0, keepdims=True)
        t = dy * gv
        dx_ref[...] = r * (t - xh * jnp.mean(t * xh, axis=-1, keepdims=True))

        @pl.when(i == 0)
        def _():
            dg_ref[...] = dg
            loss_ref[...] = lp

        @pl.when(i > 0)
        def _():
            dg_ref[...] += dg
            loss_ref[...] += lp

    row = pl.BlockSpec((tr, D), lambda i: (i, 0))
    vec = pl.BlockSpec((1, D), lambda i: (0, 0))
    one = pl.BlockSpec((1, 1), lambda i: (0, 0))
    return _pcall(body, name=name, grid=(S // tr,), in_specs=[row, vec, row], out_specs=[one, row, vec],
                  out_shape=[jax.ShapeDtypeStruct((1, 1), F32), jax.ShapeDtypeStruct((S, D), F32),
                             jax.ShapeDtypeStruct((1, D), F32)],
                  compiler_params=_params(("arbitrary",)))(x, g, tgt)


def _ffn_up(name, h, w_gu, tm=512):
    S, D = h.shape
    W = FF_SHARD

    def body(h_ref, wg_ref, wu_ref, gu_ref, act_ref):
        hv = h_ref[...]
        gate = _dot(hv, wg_ref[...], "nn")
        up = _dot(hv, wu_ref[...], "nn")
        gu_ref[0] = gate
        gu_ref[1] = up
        sig = 1.0 / (1.0 + jnp.exp(-gate))
        act_ref[...] = ((gate * sig) * up).astype(act_ref.dtype)

    return _pcall(
        body, name=name, grid=(2, S // tm),
        in_specs=[pl.BlockSpec((tm, D), lambda j, i: (i, 0)),
                  pl.BlockSpec((None, D, W), lambda j, i: (j, 0, 0)),
                  pl.BlockSpec((None, D, W), lambda j, i: (j + 2, 0, 0))],
        out_specs=[pl.BlockSpec((2, tm, W), lambda j, i: (0, i, j)), pl.BlockSpec((tm, W), lambda j, i: (i, j))],
        out_shape=[jax.ShapeDtypeStruct((2, S, D_FF), F32), jax.ShapeDtypeStruct((S, D_FF), BF16)],
        compiler_params=_params(("parallel", "parallel")))(h, w_gu, w_gu)


def _ffn_down_bwd(name, dx, w_down, gu, tm=512):
    S, D = dx.shape
    W = FF_SHARD

    def body(dx_ref, w_ref, gu_ref, dgu_ref):
        dact = _dot(dx_ref[...].astype(BF16), w_ref[...], "nt")
        gate = gu_ref[0]
        up = gu_ref[1]
        sig = 1.0 / (1.0 + jnp.exp(-gate))
        silu = gate * sig
        dgu_ref[0] = (dact * up * (sig * (1.0 + gate * (1.0 - sig)))).astype(dgu_ref.dtype)
        dgu_ref[1] = (dact * silu).astype(dgu_ref.dtype)

    blk = pl.BlockSpec((2, tm, W), lambda j, i: (0, i, j))
    return _pcall(
        body, name=name, grid=(2, S // tm),
        in_specs=[pl.BlockSpec((tm, D), lambda j, i: (i, 0)), pl.BlockSpec((W, D), lambda j, i: (j, 0)), blk],
        out_specs=blk, out_shape=jax.ShapeDtypeStruct((2, S, D_FF), BF16),
        compiler_params=_params(("parallel", "parallel")))(dx, w_down, gu)


def _ffn_dw_up(name, h, dgu, tk=1024):
    S, D = h.shape
    W = FF_SHARD
    return _mm(name, "tn", (N_CHIPS, S // tk), h, pl.BlockSpec((tk, D), lambda s, k: (k, 0)),
               dgu, pl.BlockSpec((None, tk, W), lambda s, k: (s // 2, k, s % 2)),
               jax.ShapeDtypeStruct((N_CHIPS, D, W), BF16), pl.BlockSpec((None, D, W), lambda s, k: (s, 0, 0)),
               (D, W))


def _ffn_dh(name, dgu, w_gu, tm=1024):
    S = dgu.shape[1]
    W = FF_SHARD
    return _mm(name, "nt", (S // tm, N_CHIPS), dgu, pl.BlockSpec((None, tm, W), lambda i, k: (k // 2, i, k % 2)),
               w_gu, pl.BlockSpec((None, D_MODEL, W), lambda i, k: (k, 0, 0)),
               jax.ShapeDtypeStruct((S, D_MODEL), F32), pl.BlockSpec((tm, D_MODEL), lambda i, k: (i, 0)),
               (tm, D_MODEL))


def _qkv_proj(name, hf, w_qkv, tm=1024):
    S = hf.shape[1]
    per_chip = QKV_SHARD // QKV_TILE
    per_group = 3 * D_MODEL // QKV_TILE
    return _mm(name, "nn", (N_GROUPS, S // tm, per_group, 1),
               hf, pl.BlockSpec((None, tm, D_MODEL), lambda g, i, j, k: (g, i, 0)),
               w_qkv, pl.BlockSpec((None, D_MODEL, QKV_TILE),
                                   lambda g, i, j, k: ((per_group * g + j) // per_chip, 0, (per_group * g + j) % per_chip)),
               jax.ShapeDtypeStruct((N_GROUPS, S, 3 * D_MODEL), BF16),
               pl.BlockSpec((None, tm, QKV_TILE), lambda g, i, j, k: (g, i, j)), None)


def _qkv_dw(name, hf, dqkv, tk=1024):
    S = hf.shape[1]
    per_chip = QKV_SHARD // QKV_TILE
    per_group = 3 * D_MODEL // QKV_TILE
    return _mm(name, "tn", (N_GROUPS, per_group, S // tk),
               hf, pl.BlockSpec((None, tk, D_MODEL), lambda g, j, k: (g, k, 0)),
               dqkv, pl.BlockSpec((None, tk, QKV_TILE), lambda g, j, k: (g, k, j)),
               jax.ShapeDtypeStruct((N_CHIPS, D_MODEL, QKV_SHARD), BF16),
               pl.BlockSpec((None, D_MODEL, QKV_TILE),
                            lambda g, j, k: ((per_group * g + j) // per_chip, 0, (per_group * g + j) % per_chip)),
               (D_MODEL, QKV_TILE))


def _qkv_dh(name, dqkv, w_qkv, tm=1024):
    S = dqkv.shape[1]
    per_chip = QKV_SHARD // QKV_TILE
    per_group = 3 * D_MODEL // QKV_TILE
    return _mm(name, "nt", (N_GROUPS, S // tm, per_group),
               dqkv, pl.BlockSpec((None, tm, QKV_TILE), lambda g, i, k: (g, i, k)),
               w_qkv, pl.BlockSpec((None, D_MODEL, QKV_TILE),
                                   lambda g, i, k: ((per_group * g + k) // per_chip, 0, (per_group * g + k) % per_chip)),
               jax.ShapeDtypeStruct((N_GROUPS, S, D_MODEL), F32),
               pl.BlockSpec((None, tm, D_MODEL), lambda g, i, k: (g, i, 0)), (tm, D_MODEL))


def _alibi_coef(g, h):
    vals = [-(2.0 ** (-8.0 * (gg * N_HEADS + h + 1) / (N_GROUPS * N_HEADS))) * ATTN_DILATIONS[gg]
            for gg in range(N_GROUPS)]
    return jnp.where(g == 0, vals[0], jnp.where(g == 1, vals[1], vals[2])).astype(F32)


def _blocks_per_segment(g, S):
    return jnp.right_shift(S // Q_BLOCK, 2 * g)


def _band_consts():
    row = lax.broadcasted_iota(jnp.int32, (Q_BLOCK, Q_BLOCK), 0)
    col = lax.broadcasted_iota(jnp.int32, (Q_BLOCK, Q_BLOCK), 1)
    d_cur = (row - col).astype(F32)
    return d_cur, d_cur + float(Q_BLOCK), col <= row, col >= row


def _attn_fwd(name, qkv, tq=256):
    S = qkv.shape[1]
    nqb = tq // Q_BLOCK
    scale = HEAD_DIM ** -0.5

    def body(q_ref, k_ref, kp_ref, v_ref, vp_ref, o_ref, lse_ref):
        g = pl.program_id(0)
        i = pl.program_id(1)
        nb = _blocks_per_segment(g, S)
        d_cur, d_prev, m_cur, m_prev = _band_consts()
        for b in range(nqb):
            rows = slice(b * Q_BLOCK, (b + 1) * Q_BLOCK)
            prows = slice((b - 1) * Q_BLOCK, b * Q_BLOCK)
            pen = jnp.where((i * nqb + b) % nb != 0, 0.0, NEG).astype(F32)
            for h in range(N_HEADS):
                cols = slice(h * HEAD_DIM, (h + 1) * HEAD_DIM)
                coef = _alibi_coef(g, h)
                q = q_ref[rows, cols]
                kc, vc = k_ref[rows, cols], v_ref[rows, cols]
                if b == 0:
                    kp, vp = kp_ref[:, cols], vp_ref[:, cols]
                else:
                    kp, vp = k_ref[prows, cols], v_ref[prows, cols]
                s_c = jnp.where(m_cur, _dot(q, kc, "nt") * scale + coef * d_cur, NEG)
                s_p = jnp.where(m_prev, _dot(q, kp, "nt") * scale + coef * d_prev + pen, NEG)
                m = jnp.maximum(jnp.max(s_c, axis=-1, keepdims=True), jnp.max(s_p, axis=-1, keepdims=True))
                p_c = jnp.exp(s_c - m)
                p_p = jnp.exp(s_p - m)
                den = jnp.sum(p_c, axis=-1, keepdims=True) + jnp.sum(p_p, axis=-1, keepdims=True)
                inv = 1.0 / den
                o = _dot((p_c * inv).astype(BF16), vc, "nn") + _dot((p_p * inv).astype(BF16), vp, "nn")
                o_ref[rows, cols] = o
                lse_ref[rows, h:h + 1] = m + jnp.log(den)

    main = lambda c: pl.BlockSpec((None, tq, D_MODEL), lambda g, i: (g, i, c))
    prev = lambda c: pl.BlockSpec((None, Q_BLOCK, D_MODEL), lambda g, i: (g, jnp.maximum(i * nqb - 1, 0), c))
    return _pcall(
        body, name=name, grid=(N_GROUPS, S // tq),
        in_specs=[main(0), main(1), prev(1), main(2), prev(2)],
        out_specs=[pl.BlockSpec((None, tq, D_MODEL), lambda g, i: (g, i, 0)),
                   pl.BlockSpec((None, tq, N_HEADS), lambda g, i: (g, i, 0))],
        out_shape=[jax.ShapeDtypeStruct((N_GROUPS, S, D_MODEL), F32),
                   jax.ShapeDtypeStruct((N_GROUPS, S, N_HEADS), F32)],
        compiler_params=_params(("parallel", "parallel")))(qkv, qkv, qkv, qkv, qkv)


def _attn_bwd(name, qkv, do, lse, dl, tq=256):
    S = qkv.shape[1]
    nqb = tq // Q_BLOCK
    n_blocks = S // Q_BLOCK
    scale = HEAD_DIM ** -0.5

    def body(q_ref, k_ref, v_ref, kp_ref, vp_ref, qn_ref, do_ref, don_ref, lse_ref, lsen_ref, dl_ref, dln_ref,
             out_ref, acc_ref):
        g = pl.program_id(0)
        i = pl.program_id(1)
        nb = _blocks_per_segment(g, S)
        d_cur, d_prev, m_cur, m_prev = _band_consts()
        acc_ref[...] = jnp.zeros_like(acc_ref)

        def unit(q, k, v, do_b, lse_b, dl_b, dist, mask, pen, coef, dq_at, dkv_at, h):
            cols = slice(h * HEAD_DIM, (h + 1) * HEAD_DIM)
            s = jnp.where(mask, _dot(q, k, "nt") * scale + coef * dist + pen, NEG)
            p = jnp.exp(s - lse_b)
            dp = _dot(do_b, v, "nt")
            ds = ((p * (dp - dl_b)) * scale).astype(BF16)
            if dq_at is not None:
                acc_ref[dq_at, cols] += _dot(ds, k, "nn")
            if dkv_at is not None:
                kcols = slice(D_MODEL + h * HEAD_DIM, D_MODEL + (h + 1) * HEAD_DIM)
                vcols = slice(2 * D_MODEL + h * HEAD_DIM, 2 * D_MODEL + (h + 1) * HEAD_DIM)
                acc_ref[dkv_at, kcols] += _dot(ds, q, "tn")
                acc_ref[dkv_at, vcols] += _dot(p.astype(BF16), do_b, "tn")

        zero = jnp.zeros((), F32)
        for h in range(N_HEADS):
            cols = slice(h * HEAD_DIM, (h + 1) * HEAD_DIM)
            coef = _alibi_coef(g, h)
            for b in range(nqb):
                rows = slice(b * Q_BLOCK, (b + 1) * Q_BLOCK)
                prows = slice((b - 1) * Q_BLOCK, b * Q_BLOCK)
                q, do_b = q_ref[rows, cols], do_ref[rows, cols]
                lse_b, dl_b = lse_ref[rows, h:h + 1], dl_ref[rows, h:h + 1]
                unit(q, k_ref[rows, cols], v_ref[rows, cols], do_b, lse_b, dl_b, d_cur, m_cur, zero, coef,
                     rows, rows, h)
                pen = jnp.where((i * nqb + b) % nb != 0, 0.0, NEG).astype(F32)
                if b == 0:
                    unit(q, kp_ref[:, cols], vp_ref[:, cols], do_b, lse_b, dl_b, d_prev, m_prev, pen, coef,
                         rows, None, h)
                else:
                    unit(q, k_ref[prows, cols], v_ref[prows, cols], do_b, lse_b, dl_b, d_prev, m_prev, pen, coef,
                         rows, prows, h)
            nxt = (i + 1) * nqb
            pen = jnp.where(jnp.logical_and(nxt < n_blocks, nxt % nb != 0), 0.0, NEG).astype(F32)
            last = slice((nqb - 1) * Q_BLOCK, nqb * Q_BLOCK)
            unit(qn_ref[:, cols], k_ref[last, cols], v_ref[last, cols], don_ref[:, cols], lsen_ref[:, h:h + 1],
                 dln_ref[:, h:h + 1], d_prev, m_prev, pen, coef, None, last, h)
        out_ref[...] = acc_ref[...].astype(out_ref.dtype)

    nxt_blk = lambda i: jnp.minimum((i + 1) * nqb, n_blocks - 1)
    main = lambda c: pl.BlockSpec((None, tq, D_MODEL), lambda g, i: (g, i, c))
    prev = lambda c: pl.BlockSpec((None, Q_BLOCK, D_MODEL), lambda g, i: (g, jnp.maximum(i * nqb - 1, 0), c))
    row = pl.BlockSpec((None, tq, D_MODEL), lambda g, i: (g, i, 0))
    row_n = pl.BlockSpec((None, Q_BLOCK, D_MODEL), lambda g, i: (g, nxt_blk(i), 0))
    col = pl.BlockSpec((None, tq, N_HEADS), lambda g, i: (g, i, 0))
    col_n = pl.BlockSpec((None, Q_BLOCK, N_HEADS), lambda g, i: (g, nxt_blk(i), 0))
    return _pcall(
        body, name=name, grid=(N_GROUPS, S // tq),
        in_specs=[main(0), main(1), main(2), prev(1), prev(2), row_n, row, row_n, col, col_n, col, col_n],
        out_specs=pl.BlockSpec((None, tq, 3 * D_MODEL), lambda g, i: (g, i, 0)),
        out_shape=jax.ShapeDtypeStruct((N_GROUPS, S, 3 * D_MODEL), BF16),
        scratch_shapes=[pltpu.VMEM((tq, 3 * D_MODEL), F32)],
        compiler_params=_params(("parallel", "parallel")))(qkv, qkv, qkv, qkv, qkv, qkv, do, do, lse, lse, dl, dl)


def _group_weights(lse_ref):
    l0, l1, l2 = lse_ref[0], lse_ref[1], lse_ref[2]
    m = jnp.maximum(jnp.maximum(l0, l1), l2)
    e = [jnp.exp(l0 - m), jnp.exp(l1 - m), jnp.exp(l2 - m)]
    den = e[0] + e[1] + e[2]
    return [ei / den for ei in e]


def _merge_fwd(name, o, lse, tr=256):
    S = o.shape[1]

    def body(o_ref, lse_ref, out_ref):
        w = _group_weights(lse_ref)
        for h in range(N_HEADS):
            cols = slice(h * HEAD_DIM, (h + 1) * HEAD_DIM)
            acc = w[0][:, h:h + 1] * o_ref[0, :, cols]
            for gg in range(1, N_GROUPS):
                acc = acc + w[gg][:, h:h + 1] * o_ref[gg, :, cols]
            out_ref[:, cols] = acc.astype(out_ref.dtype)

    return _pcall(body, name=name, grid=(S // tr,),
                  in_specs=[pl.BlockSpec((N_GROUPS, tr, D_MODEL), lambda i: (0, i, 0)),
                            pl.BlockSpec((N_GROUPS, tr, N_HEADS), lambda i: (0, i, 0))],
                  out_specs=pl.BlockSpec((tr, D_MODEL), lambda i: (i, 0)),
                  out_shape=jax.ShapeDtypeStruct((S, D_MODEL), BF16), compiler_params=_params(("parallel",)))(o, lse)


def _merge_bwd(name, d_out, o, lse, tr=256):
    S = o.shape[1]

    def body(d_ref, o_ref, lse_ref, do_ref, dl_ref):
        w = _group_weights(lse_ref)
        for h in range(N_HEADS):
            cols = slice(h * HEAD_DIM, (h + 1) * HEAD_DIM)
            dv = d_ref[:, cols]
            merged = w[0][:, h:h + 1] * o_ref[0, :, cols]
            for gg in range(1, N_GROUPS):
                merged = merged + w[gg][:, h:h + 1] * o_ref[gg, :, cols]
            dsum = jnp.sum(dv * merged, axis=-1, keepdims=True)
            for gg in range(N_GROUPS):
                wg = w[gg][:, h:h + 1]
                do_ref[gg, :, cols] = (wg * dv).astype(do_ref.dtype)
                dl_ref[gg, :, h:h + 1] = wg * dsum

    big = pl.BlockSpec((N_GROUPS, tr, D_MODEL), lambda i: (0, i, 0))
    small = pl.BlockSpec((N_GROUPS, tr, N_HEADS), lambda i: (0, i, 0))
    return _pcall(body, name=name, grid=(S // tr,),
                  in_specs=[pl.BlockSpec((tr, D_MODEL), lambda i: (i, 0)), big, small], out_specs=[big, small],
                  out_shape=[jax.ShapeDtypeStruct((N_GROUPS, S, D_MODEL), BF16),
                             jax.ShapeDtypeStruct((N_GROUPS, S, N_HEADS), F32)],
                  compiler_params=_params(("parallel",)))(d_out, o, lse)


def _shift_rows(a, k):
    return pltpu.roll(a, k % a.shape[0], axis=0)


def _pool_level(g, levels):
    return jnp.where(g == 0, levels[0], jnp.where(g == 1, levels[1], jnp.where(g == 2, levels[2], levels[3])))


def _pool_fwd(name, u, w_group, scale, x_in, tr=1024):
    S, D = u.shape
    H = POOL_HALO

    def body(u_ref, halo_ref, w_ref, sc_ref, x_ref, y_ref, z_ref, xo_ref):
        g = pl.program_id(0)
        i = pl.program_id(1)
        uv = u_ref[...]
        halo = jnp.where(i > 0, halo_ref[...], 0.0)
        s = jnp.concatenate([halo, uv], axis=0)
        levels = []
        for k in (1, 2, 4, 8):
            s = s + _shift_rows(s, k)
            levels.append(s[H:])
        t = i * tr + lax.broadcasted_iota(jnp.int32, (tr, 1), 0)
        cnt = jnp.minimum(t + 1, jnp.left_shift(2, g)).astype(F32)
        y = _pool_level(g, levels) / cnt - uv
        yb = y.astype(BF16)
        z = _dot(yb, w_ref[...], "nn")
        y_ref[...] = yb
        z_ref[...] = z
        xo_ref[...] = x_ref[...] + z * sc_ref[...]

    blk = pl.BlockSpec((tr, POOL_DIM), lambda g, i: (i, g))
    return _pcall(
        body, name=name, grid=(D // POOL_DIM, S // tr),
        in_specs=[blk, pl.BlockSpec((H, POOL_DIM), lambda g, i: (jnp.maximum(i * (tr // H) - 1, 0), g)),
                  pl.BlockSpec((None, POOL_DIM, POOL_DIM), lambda g, i: (g, 0, 0)),
                  pl.BlockSpec((1, POOL_DIM), lambda g, i: (0, g)), blk],
        out_specs=[blk, blk, blk],
        out_shape=[jax.ShapeDtypeStruct((S, D), BF16), jax.ShapeDtypeStruct((S, D), F32),
                   jax.ShapeDtypeStruct((S, D), F32)],
        compiler_params=_params(("parallel", "parallel")))(u, u, w_group, scale, x_in)


def _pool_bwd(name, d_out, z, y, scale, w_group, tr=1024):
    S, D = d_out.shape
    H = POOL_HALO

    def body(d_ref, dn_ref, z_ref, y_ref, sc_ref, w_ref, du_ref, dw_ref, dsc_ref):
        g = pl.program_id(0)
        i = pl.program_id(1)
        dv = d_ref[...]
        dz = jnp.concatenate([dv, dn_ref[...]], axis=0) * sc_ref[...]
        dzb = dz.astype(BF16)
        dy = _dot(dzb, w_ref[...], "nt")
        t = i * tr + lax.broadcasted_iota(jnp.int32, (tr + H, 1), 0)
        cnt = jnp.minimum(t + 1, jnp.left_shift(2, g)).astype(F32)
        s = jnp.where(t < S, dy / cnt, 0.0)
        levels = []
        for k in (1, 2, 4, 8):
            s = s + _shift_rows(s, -k)
            levels.append(s[:tr])
        du_ref[...] = (_pool_level(g, levels) - dy[:tr]).astype(du_ref.dtype)
        dw = _dot(y_ref[...], dzb[:tr], "tn")
        dsc = jnp.sum(dv * z_ref[...], axis=0, keepdims=True)

        @pl.when(i == 0)
        def _():
            dw_ref[...] = dw
            dsc_ref[...] = dsc

        @pl.when(i > 0)
        def _():
            dw_ref[...] += dw
            dsc_ref[...] += dsc

    blk = pl.BlockSpec((tr, POOL_DIM), lambda g, i: (i, g))
    vec = pl.BlockSpec((1, POOL_DIM), lambda g, i: (0, g))
    mat = pl.BlockSpec((None, POOL_DIM, POOL_DIM), lambda g, i: (g, 0, 0))
    nxt = pl.BlockSpec((H, POOL_DIM), lambda g, i: (jnp.minimum((i + 1) * (tr // H), S // H - 1), g))
    return _pcall(
        body, name=name, grid=(D // POOL_DIM, S // tr), in_specs=[blk, nxt, blk, blk, vec, mat],
        out_specs=[blk, mat, vec],
        out_shape=[jax.ShapeDtypeStruct((S, D), BF16), jax.ShapeDtypeStruct((D // POOL_DIM, POOL_DIM, POOL_DIM), F32),
                   jax.ShapeDtypeStruct((1, D), F32)],
        compiler_params=_params(("parallel", "arbitrary")))(d_out, d_out, z, y, scale, w_group)


def _row_tile(rows, cols, target_bytes=1 << 20):
    best = rows
    for tr in range(8, rows + 1, 8):
        if rows % tr == 0 and tr * cols * 4 <= target_bytes:
            best = tr
    return best if best * cols * 4 <= 4 * target_bytes else rows


def _adamw(name, w, g, m, v):
    R, C = w.shape
    tr = _row_tile(R, C) if R % 8 == 0 else R

    def body(w_ref, g_ref, m_ref, v_ref, d_ref, mo_ref, vo_ref):
        gv = g_ref[...]
        m2 = ADAM_B1 * m_ref[...] + (1.0 - ADAM_B1) * gv
        v2 = ADAM_B2 * v_ref[...] + (1.0 - ADAM_B2) * (gv * gv)
        m_hat = m2 / (1.0 - ADAM_B1 ** ADAM_STEP)
        v_hat = v2 / (1.0 - ADAM_B2 ** ADAM_STEP)
        d_ref[...] = -ADAM_LR * (m_hat / (jnp.sqrt(v_hat) + ADAM_EPS) + ADAM_WD * w_ref[...])
        mo_ref[...] = m2
        vo_ref[...] = v2

    blk = pl.BlockSpec((tr, C), lambda i: (i, 0))
    sds = jax.ShapeDtypeStruct((R, C), F32)
    return _pcall(body, name=name, grid=(R // tr,), in_specs=[blk] * 4, out_specs=[blk] * 3, out_shape=[sds] * 3,
                  compiler_params=_params(("parallel",)))(w, g, m, v)


def _sum_leading(name, a, out_dtype):
    n, R, C = a.shape
    tr = _row_tile(R, C) if R % 16 == 0 else R
    if tr % 16:
        tr = R

    def body(a_ref, o_ref):
        acc = a_ref[0].astype(F32)
        for j in range(1, n):
            acc = acc + a_ref[j].astype(F32)
        o_ref[...] = acc.astype(o_ref.dtype)

    return _pcall(body, name=name, grid=(R // tr,), in_specs=[pl.BlockSpec((n, tr, C), lambda i: (0, i, 0))],
                  out_specs=pl.BlockSpec((tr, C), lambda i: (i, 0)), out_shape=jax.ShapeDtypeStruct((R, C), out_dtype),
                  compiler_params=_params(("parallel",)))(a)


def _cast_into_slot(name, w, chip, dtype):
    R, C = w.shape
    tr = _row_tile(R, C)
    if tr % 16:
        tr = R

    def body(c_ref, w_ref, o_ref):
        o_ref[...] = w_ref[...].astype(o_ref.dtype)

    grid_spec = pltpu.PrefetchScalarGridSpec(
        num_scalar_prefetch=1, grid=(R // tr,), in_specs=[pl.BlockSpec((tr, C), lambda i, c_ref: (i, 0))],
        out_specs=pl.BlockSpec((None, tr, C), lambda i, c_ref: (c_ref[0], i, 0)))
    return _pcall(body, name=name, grid_spec=grid_spec, out_shape=jax.ShapeDtypeStruct((N_CHIPS, R, C), dtype),
                  compiler_params=_params(("parallel",)))(chip, w)


def _owner_sum(name, mine, landed, chip, core, out=None, layer=None):
    _, half, C = mine.shape
    tr = _row_tile(half, C)
    if tr % 16:
        tr = half
    nrt = half // tr
    has_out = out is not None

    def body(*refs):
        m_ref, l_ref, o_ref = refs[2], refs[3], refs[-1]
        acc = m_ref[...].astype(F32)
        for j in range(3):
            acc = acc + l_ref[j].astype(F32)
        o_ref[...] = acc

    if layer is None:
        out_shape = jax.ShapeDtypeStruct((2 * half, C), F32)
        o_spec = pl.BlockSpec((tr, C), lambda i, ch, co: (co[0] * nrt + i, 0))
    else:
        out_shape = jax.ShapeDtypeStruct((2, 2 * half, C), F32)
        o_spec = pl.BlockSpec((None, tr, C), lambda i, ch, co: (layer, co[0] * nrt + i, 0))
    in_specs = [pl.BlockSpec((None, tr, C), lambda i, ch, co: (ch[0], i, 0)),
                pl.BlockSpec((3, tr, C), lambda i, ch, co: (0, i, 0))]
    operands = [chip, core, mine, landed]
    aliases = {}
    if has_out:
        in_specs.append(ANY)
        operands.append(out)
        aliases = {4: 0}
    grid_spec = pltpu.PrefetchScalarGridSpec(num_scalar_prefetch=2, grid=(nrt,), in_specs=in_specs, out_specs=o_spec)
    return _pcall(body, name=name, grid_spec=grid_spec, out_shape=out_shape, input_output_aliases=aliases,
                  compiler_params=_params(("parallel",)))(*operands)


def _add_my_half(name, p, q, core):
    n, R, C = p.shape
    half = R // 2

    def body(c_ref, p_ref, q_ref, o_ref):
        o_ref[...] = (p_ref[...].astype(F32) + q_ref[...].astype(F32)).astype(o_ref.dtype)

    grid_spec = pltpu.PrefetchScalarGridSpec(
        num_scalar_prefetch=1, grid=(n,),
        in_specs=[pl.BlockSpec((None, half, C), lambda s, c_ref: (s, c_ref[0], 0)),
                  pl.BlockSpec((None, half, C), lambda s, c_ref: (s, 0, 0))],
        out_specs=pl.BlockSpec((None, half, C), lambda s, c_ref: (s, 0, 0)))
    return _pcall(body, name=name, grid_spec=grid_spec, out_shape=jax.ShapeDtypeStruct((n, half, C), BF16),
                  compiler_params=_params(("parallel",)))(core, p, q)


ANY = pl.BlockSpec(memory_space=pl.ANY)


def _place():
    x, y, c = lax.axis_index("x"), lax.axis_index("y"), lax.axis_index("c")
    other_chips = [(1 - x, y), (x, 1 - y), (1 - x, 1 - y)]
    return x, y, c, other_chips


def _gather_weights(name, bufs, split):
    n = len(bufs)

    def body(*refs):
        outs = refs[n:2 * n]
        ici_send, ici_recv, d2d_send, d2d_recv = refs[2 * n:]
        x, y, c, chips = _place()
        me = 2 * x + y
        sibling = (x, y, 1 - c)

        def piece(t, chip, core_half):
            if not split[t]:
                return outs[t].at[chip]
            half = outs[t].shape[1] // 2
            return outs[t].at[chip, pl.ds(core_half * half, half)]

        sends = []
        for t in range(n):
            for j, (px, py) in enumerate(chips):
                sends.append(pltpu.make_async_remote_copy(
                    src_ref=piece(t, me, c), dst_ref=piece(t, me, c), send_sem=ici_send.at[3 * t + j],
                    recv_sem=ici_recv.at[3 * t + j], device_id=(px, py, c), device_id_type=MESH))
        for cp in sends:
            cp.start()
        passed = []
        for t in range(n):
            for j, (px, py) in enumerate(chips):
                landed = piece(t, 2 * px + py, c)
                pltpu.make_async_remote_copy(
                    src_ref=landed, dst_ref=landed, send_sem=ici_send.at[3 * t + j],
                    recv_sem=ici_recv.at[3 * t + j], device_id=(px, py, c), device_id_type=MESH).wait_recv()
                if split[t]:
                    fwd = pltpu.make_async_remote_copy(
                        src_ref=landed, dst_ref=landed, send_sem=d2d_send.at[3 * t + j],
                        recv_sem=d2d_recv.at[3 * t + j], device_id=sibling, device_id_type=MESH)
                    fwd.start()
                    passed.append(fwd)
        for t in range(n):
            if split[t]:
                for j, (px, py) in enumerate(chips):
                    theirs = piece(t, 2 * px + py, 1 - c)
                    pltpu.make_async_remote_copy(
                        src_ref=theirs, dst_ref=theirs, send_sem=d2d_send.at[3 * t + j],
                        recv_sem=d2d_recv.at[3 * t + j], device_id=sibling, device_id_type=MESH).wait_recv()
        for cp in sends + passed:
            cp.wait_send()

    out_shape = [jax.ShapeDtypeStruct(b.shape, b.dtype) for b in bufs]
    return _pcall(body, name=name, in_specs=[ANY] * n, out_specs=[ANY] * n, out_shape=out_shape,
                  input_output_aliases={t: t for t in range(n)},
                  scratch_shapes=[pltpu.SemaphoreType.DMA((3 * n,)), pltpu.SemaphoreType.DMA((3 * n,)),
                                  pltpu.SemaphoreType.DMA((3 * n,)), pltpu.SemaphoreType.DMA((3 * n,))])(*bufs)


def _swap_other_half(name, parts):
    n = len(parts)

    def body(*refs):
        ins, outs = refs[:n], refs[n:2 * n]
        send_sem, recv_sem = refs[2 * n:]
        x, y, c, _ = _place()
        copies = []
        for t in range(n):
            half = ins[t].shape[1] // 2
            copies.append(pltpu.make_async_remote_copy(
                src_ref=ins[t].at[:, pl.ds((1 - c) * half, half)], dst_ref=outs[t], send_sem=send_sem.at[t],
                recv_sem=recv_sem.at[t], device_id=(x, y, 1 - c), device_id_type=MESH))
        for cp in copies:
            cp.start()
        for cp in copies:
            cp.wait()

    out_shape = [jax.ShapeDtypeStruct((p.shape[0], p.shape[1] // 2, p.shape[2]), p.dtype) for p in parts]
    return _pcall(body, name=name, in_specs=[ANY] * n, out_specs=[ANY] * n, out_shape=out_shape,
                  scratch_shapes=[pltpu.SemaphoreType.DMA((n,)), pltpu.SemaphoreType.DMA((n,))])(*parts)


def _scatter_to_owners(name, sums):
    n = len(sums)

    def body(*refs):
        ins, outs = refs[:n], refs[n:2 * n]
        send_sem, recv_sem = refs[2 * n:]
        x, y, c, chips = _place()
        copies = []
        for t in range(n):
            for j, (px, py) in enumerate(chips):
                copies.append(pltpu.make_async_remote_copy(
                    src_ref=ins[t].at[2 * px + py], dst_ref=outs[t].at[j], send_sem=send_sem.at[3 * t + j],
                    recv_sem=recv_sem.at[3 * t + j], device_id=(px, py, c), device_id_type=MESH))
        for cp in copies:
            cp.start()
        for cp in copies:
            cp.wait_recv()
        for cp in copies:
            cp.wait_send()

    out_shape = [jax.ShapeDtypeStruct((3,) + s.shape[1:], s.dtype) for s in sums]
    return _pcall(body, name=name, in_specs=[ANY] * n, out_specs=[ANY] * n, out_shape=out_shape,
                  scratch_shapes=[pltpu.SemaphoreType.DMA((3 * n,)), pltpu.SemaphoreType.DMA((3 * n,))])(*sums)


def _share_halves(name, blocks, layers):
    n = len(blocks)
    pieces = [(t, l) for t in range(n) for l in (range(layers[t]) if layers[t] else [None])]

    def body(*refs):
        outs = refs[n:2 * n]
        send_sem, recv_sem = refs[2 * n:]
        x, y, c, _ = _place()

        def half(t, l, core_half):
            ref = outs[t] if l is None else outs[t].at[l]
            r2 = ref.shape[0] // 2
            return ref.at[pl.ds(core_half * r2, r2)]

        copies = [pltpu.make_async_remote_copy(
            src_ref=half(t, l, c), dst_ref=half(t, l, c), send_sem=send_sem.at[k], recv_sem=recv_sem.at[k],
            device_id=(x, y, 1 - c), device_id_type=MESH) for k, (t, l) in enumerate(pieces)]
        for cp in copies:
            cp.start()
        for k, (t, l) in enumerate(pieces):
            theirs = half(t, l, 1 - c)
            pltpu.make_async_remote_copy(
                src_ref=theirs, dst_ref=theirs, send_sem=send_sem.at[k], recv_sem=recv_sem.at[k],
                device_id=(x, y, 1 - c), device_id_type=MESH).wait_recv()
        for cp in copies:
            cp.wait_send()

    out_shape = [jax.ShapeDtypeStruct(b.shape, b.dtype) for b in blocks]
    return _pcall(body, name=name, in_specs=[ANY] * n, out_specs=[ANY] * n, out_shape=out_shape,
                  input_output_aliases={t: t for t in range(n)},
                  scratch_shapes=[pltpu.SemaphoreType.DMA((len(pieces),)),
                                  pltpu.SemaphoreType.DMA((len(pieces),))])(*blocks)


def _gather_small(name, v):
    def body(v_ref, out_ref, send_sem, recv_sem, local_sem):
        x, y, c, _ = _place()
        me = 4 * x + 2 * y + c
        flips = [(fx, fy, fc) for fx in (0, 1) for fy in (0, 1) for fc in (0, 1)][1:]
        local = pltpu.make_async_copy(v_ref, out_ref.at[me], local_sem)
        local.start()
        copies = []
        for j, (fx, fy, fc) in enumerate(flips):
            peer = (x ^ fx, y ^ fy, c ^ fc)
            copies.append(pltpu.make_async_remote_copy(
                src_ref=v_ref, dst_ref=out_ref.at[me], send_sem=send_sem.at[j], recv_sem=recv_sem.at[j],
                device_id=peer, device_id_type=MESH))
        for cp in copies:
            cp.start()
        for j, (fx, fy, fc) in enumerate(flips):
            peer = (x ^ fx, y ^ fy, c ^ fc)
            pltpu.make_async_remote_copy(
                src_ref=v_ref, dst_ref=out_ref.at[4 * peer[0] + 2 * peer[1] + peer[2]], send_sem=send_sem.at[j],
                recv_sem=recv_sem.at[j], device_id=peer, device_id_type=MESH).wait_recv()
        for cp in copies:
            cp.wait_send()
        local.wait()

    return _pcall(body, name=name, in_specs=[ANY], out_specs=ANY,
                  out_shape=jax.ShapeDtypeStruct((8,) + v.shape, v.dtype),
                  scratch_shapes=[pltpu.SemaphoreType.DMA((7,)), pltpu.SemaphoreType.DMA((7,)),
                                  pltpu.SemaphoreType.DMA])(v)


def _fold(a, dil):
    if dil == 1:
        return a
    S, C = a.shape
    return a.reshape(S // dil, dil, C).transpose(1, 0, 2).reshape(S, C)


def _unfold(a, dil):
    if dil == 1:
        return a
    S, C = a.shape
    return a.reshape(dil, S // dil, C).transpose(1, 0, 2).reshape(S, C)


def _fold_groups(a):
    return jnp.stack([_fold(a[g] if a.ndim == 3 else a, d) for g, d in enumerate(ATTN_DILATIONS)])


def _unfold_groups(a):
    return jnp.stack([_unfold(a[g], d) for g, d in enumerate(ATTN_DILATIONS)])


def _local_step(xs, tgt, attn_norm, w_qkv, w_o, pool_norm, w_p, w_pg, pool_scale, ffn_norm, w_gu, w_d, final_norm):
    h0 = _rms_fwd("norm_attn", xs, attn_norm)
    hf = _fold_groups(h0)
    qkv = _qkv_proj("qkv_proj", hf, w_qkv)
    o_f, lse_f = _attn_fwd("attn_fwd", qkv)
    o_t, lse_t = _unfold_groups(o_f), _unfold_groups(lse_f)
    merged = _merge_fwd("merge_fwd", o_t, lse_t)
    x1 = _mm_rows("attn_out", merged, w_o, F32, 512, res=xs)
    h1 = _rms_fwd("norm_ffn0", x1, ffn_norm[0:1])
    gu0, act0 = _ffn_up("ffn0_up", h1, w_gu[0])
    x2 = _mm_rows("ffn0_down", act0, w_d[0], F32, 512, res=x1)
    h2 = _rms_fwd("norm_pool", x2, pool_norm)
    u = _mm_rows("pool_in", h2, w_p, F32, 512)
    y, z, x3 = _pool_fwd("pool_fwd", u, w_pg, pool_scale, x2)
    h3 = _rms_fwd("norm_ffn1", x3, ffn_norm[1:2])
    gu1, act1 = _ffn_up("ffn1_up", h3, w_gu[1])
    x4 = _mm_rows("ffn1_down", act1, w_d[1], F32, 512, res=x3)
    loss, dx4, d_final = _loss_bwd("loss_head", x4, final_norm, tgt)

    dgu1 = _ffn_down_bwd("ffn1_down_bwd", dx4, w_d[1], gu1)
    dw_d1 = _mm_tn("ffn1_down_dw", act1, dx4, FF_SHARD, 1024)
    dw_gu1 = _ffn_dw_up("ffn1_up_dw", h3, dgu1)
    dh3 = _ffn_dh("ffn1_up_dh", dgu1, w_gu[1])
    dx3, d_ffn1 = _rms_bwd("norm_ffn1_bwd", x3, ffn_norm[1:2], dh3, dx4)

    du, dw_pg, d_scale = _pool_bwd("pool_bwd", dx3, z, y, pool_scale, w_pg)
    dw_p = _mm_tn("pool_in_dw", h2, du, D_MODEL, 1024)
    dh2 = _mm_nt_rows("pool_in_dh", du, w_p, 512)
    dx2, d_pool = _rms_bwd("norm_pool_bwd", x2, pool_norm, dh2, dx3)

    dgu0 = _ffn_down_bwd("ffn0_down_bwd", dx2, w_d[0], gu0)
    dw_d0 = _mm_tn("ffn0_down_dw", act0, dx2, FF_SHARD, 1024)
    dw_gu0 = _ffn_dw_up("ffn0_up_dw", h1, dgu0)
    dh1 = _ffn_dh("ffn0_up_dh", dgu0, w_gu[0])
    dx1, d_ffn0 = _rms_bwd("norm_ffn0_bwd", x1, ffn_norm[0:1], dh1, dx2)

    d_merged = _mm_nt_rows("attn_out_dh", dx1, w_o, 512)
    dw_o = _mm_tn("attn_out_dw", merged, dx1, D_MODEL, 1024)
    do_t, dl_t = _merge_bwd("merge_bwd", d_merged, o_t, lse_t)
    dqkv = _attn_bwd("attn_bwd", qkv, _fold_groups(do_t), lse_f, _fold_groups(dl_t))
    dw_qkv = _qkv_dw("qkv_dw", hf, dqkv)
    dhf = _qkv_dh("qkv_dh", dqkv, w_qkv)
    dh0 = dhf[0] + _unfold(dhf[1], ATTN_DILATIONS[1]) + _unfold(dhf[2], ATTN_DILATIONS[2])
    grad_x, d_attn = _rms_bwd("norm_attn_bwd", xs, attn_norm, dh0, dx1)

    big = dict(qkv=dw_qkv, o=dw_o, p=dw_p, pg=dw_pg, gu=(dw_gu0, dw_gu1), d=(dw_d0, dw_d1))
    small = dict(attn=d_attn, ffn0=d_ffn0, ffn1=d_ffn1, final=d_final, pool=d_pool, scale=d_scale)
    return loss, grad_x, big, small


def kernel(x, attn_norm, w_qkv, w_attn_out, pool_norm, w_pool_in, w_pool_group, pool_scale, ffn_norm, w_ffn_gate_up, w_ffn_down, final_norm, loss_target, m_attn_norm, m_w_qkv, m_w_attn_out, m_pool_norm, m_w_pool_in, m_w_pool_group, m_pool_scale, m_ffn_norm, m_w_ffn_gate_up, m_w_ffn_down, m_final_norm, v_attn_norm, v_w_qkv, v_w_attn_out, v_pool_norm, v_w_pool_in, v_w_pool_group, v_pool_scale, v_ffn_norm, v_w_ffn_gate_up, v_w_ffn_down, v_final_norm):
    D = D_MODEL
    chip = 2 * lax.axis_index("x") + lax.axis_index("y")
    core = lax.axis_index("c")
    pg_rows = w_pool_group.shape[2]

    chip_arr = chip.astype(jnp.int32).reshape(1)
    core_arr = core.astype(jnp.int32).reshape(1)
    names = ["qkv", "o", "p", "pg", "gu0", "gu1", "d0", "d1"]
    shards = [w_qkv[0], w_attn_out[0], w_pool_in[0], w_pool_group[0].reshape(4 * pg_rows, POOL_DIM),
              w_ffn_gate_up[0], w_ffn_gate_up[1], w_ffn_down[0], w_ffn_down[1]]
    bufs = [_cast_into_slot("cast_" + nm, s, chip_arr, BF16) for nm, s in zip(names, shards)]
    bufs += [_cast_into_slot("place_pool_norm", pool_norm, chip_arr, F32),
             _cast_into_slot("place_pool_scale", pool_scale, chip_arr, F32)]
    gathered = _gather_weights("gather_weights", bufs, [True] * 8 + [False] * 2)
    g_qkv, g_o, g_p, g_pg, g_gu0, g_gu1, g_d0, g_d1, g_pn, g_ps = gathered
    w_pg_full = g_pg.reshape(N_CHIPS, 4, pg_rows, POOL_DIM).transpose(1, 0, 2, 3).reshape(4, POOL_DIM, POOL_DIM)
    pool_norm_full = g_pn.reshape(1, D)
    pool_scale_full = g_ps.reshape(1, D)

    loss_part, grad_x, big, small = _local_step(
        x[0], loss_target[0], attn_norm, g_qkv, g_o.reshape(D, D), pool_norm_full, g_p.reshape(D, D), w_pg_full,
        pool_scale_full, ffn_norm, (g_gu0, g_gu1), (g_d0.reshape(D_FF, D), g_d1.reshape(D_FF, D)),
        final_norm.reshape(1, D))

    dw_pg = big["pg"].reshape(4, N_CHIPS, pg_rows, POOL_DIM).transpose(1, 0, 2, 3)
    dw_pg = dw_pg.reshape(N_CHIPS, 4 * pg_rows, POOL_DIM).astype(BF16)
    parts = [big["qkv"], big["o"].reshape(N_CHIPS, D // N_CHIPS, D), big["p"].reshape(N_CHIPS, D // N_CHIPS, D),
             dw_pg, big["gu"][0], big["gu"][1], big["d"][0].reshape(N_CHIPS, D_FF // N_CHIPS, D),
             big["d"][1].reshape(N_CHIPS, D_FF // N_CHIPS, D)]
    theirs = _swap_other_half("grads_to_sibling", parts)
    chip_sums = [_add_my_half("chip_sum_" + nm, p, q, core_arr) for nm, p, q in zip(names, parts, theirs)]
    landed = _scatter_to_owners("grads_to_owners", chip_sums)
    owner = lambda t, **kw: _owner_sum("owner_sum_" + names[t], chip_sums[t], landed[t], chip_arr, core_arr, **kw)
    blocks = [owner(0), owner(1), owner(2), owner(3),
              owner(5, out=owner(4, layer=0), layer=1), owner(7, out=owner(6, layer=0), layer=1)]
    g_w_qkv, g_w_o, g_w_p, g_w_pg, g_w_gu, g_w_d = _share_halves("halves_to_sibling", blocks, [0, 0, 0, 0, 2, 2])

    rows = jnp.concatenate([small["attn"], small["ffn0"], small["ffn1"], small["final"], small["pool"],
                            small["scale"], jnp.zeros((2, D), F32)], axis=0)
    all_rows = _gather_small("gather_gain_grads", rows)
    tot = _sum_leading("sum_gain_grads", all_rows, F32)
    g_attn_norm = tot[0:1]
    g_ffn_norm = tot[1:3]
    g_final_norm = tot[3]
    g_pool_norm = lax.dynamic_slice(tot, (4, chip * POOL_DIM), (1, POOL_DIM))
    g_pool_scale = lax.dynamic_slice(tot, (5, chip * POOL_DIM), (1, POOL_DIM))

    loss = lax.psum(loss_part[0, 0], ("x", "y", "c"))

    def update(name, w, g, m, v):
        shape = w.shape
        cols = shape[-1]
        as2d = lambda a: a.reshape(-1, cols)
        d, m2, v2 = _adamw("adamw_" + name, as2d(w), as2d(g), as2d(m), as2d(v))
        return g.reshape(shape), d.reshape(shape), m2.reshape(shape), v2.reshape(shape)

    results = [
        update("attn_norm", attn_norm, g_attn_norm, m_attn_norm, v_attn_norm),
        update("w_qkv", w_qkv, g_w_qkv, m_w_qkv, v_w_qkv),
        update("w_attn_out", w_attn_out, g_w_o, m_w_attn_out, v_w_attn_out),
        update("pool_norm", pool_norm, g_pool_norm, m_pool_norm, v_pool_norm),
        update("w_pool_in", w_pool_in, g_w_p, m_w_pool_in, v_w_pool_in),
        update("w_pool_group", w_pool_group, g_w_pg, m_w_pool_group, v_w_pool_group),
        update("pool_scale", pool_scale, g_pool_scale, m_pool_scale, v_pool_scale),
        update("ffn_norm", ffn_norm, g_ffn_norm, m_ffn_norm, v_ffn_norm),
        update("w_ffn_gate_up", w_ffn_gate_up, g_w_gu, m_w_ffn_gate_up, v_w_ffn_gate_up),
        update("w_ffn_down", w_ffn_down, g_w_d, m_w_ffn_down, v_w_ffn_down),
        update("final_norm", final_norm, g_final_norm, m_final_norm, v_final_norm),
    ]
    grads = [r[0] for r in results]
    deltas = [r[1] for r in results]
    new_m = [r[2] for r in results]
    new_v = [r[3] for r in results]
    return (loss, grad_x[None], *grads, *deltas, *new_m, *new_v)
```

```python
import functools

import jax
import jax.numpy as jnp
from jax import lax
from jax.experimental import pallas as pl
from jax.experimental.pallas import tpu as pltpu

F32 = jnp.float32
BF16 = jnp.bfloat16
MESH = pl.DeviceIdType.MESH

D_MODEL = 1024
N_HEADS = 8
HEAD_DIM = 128
Q_BLOCK = 128
ATTN_DILATIONS = (1, 4, 16)
N_GROUPS = 3
D_FF = 2816
N_CHIPS = 4
FF_SHARD = 2 * D_FF // N_CHIPS
QKV_SHARD = 3 * 3 * D_MODEL // N_CHIPS
QKV_TILE = 768
POOL_WINDOWS = (2, 4, 8, 16)
POOL_DIM = 256
POOL_HALO = 16
RMS_EPS = 1e-6
NEG = -1e30
ADAM_LR = 0.001
ADAM_B1 = 0.9
ADAM_B2 = 0.999
ADAM_EPS = 1e-08
ADAM_WD = 0.01
ADAM_STEP = 10
VMEM_LIMIT = 56 * 1024 * 1024

_DN = {
    "nn": (((1,), (0,)), ((), ())),
    "nt": (((1,), (1,)), ((), ())),
    "tn": (((0,), (0,)), ((), ())),
}


def _pcall(body, **kw):
    return pl.pallas_call(body, **kw)


def _params(sem):
    return pltpu.CompilerParams(dimension_semantics=sem, vmem_limit_bytes=VMEM_LIMIT)


def _dot(a, b, mode):
    return lax.dot_general(a, b, _DN[mode], preferred_element_type=F32)


def _mm(name, mode, grid, a, a_spec, b, b_spec, out_shape, o_spec, acc_shape, res=None, res_spec=None):
    nk = grid[-1]
    has_res = res is not None

    def body(*refs):
        a_ref, b_ref = refs[0], refs[1]
        res_ref = refs[2] if has_res else None
        o_ref = refs[2 + has_res]
        part = _dot(a_ref[...].astype(BF16), b_ref[...].astype(BF16), mode)
        if nk == 1:
            if has_res:
                part = part + res_ref[...]
            o_ref[...] = part.astype(o_ref.dtype)
            return
        acc_ref = refs[3 + has_res]
        k = pl.program_id(len(grid) - 1)

        @pl.when(k == 0)
        def _():
            acc_ref[...] = part

        @pl.when(k > 0)
        def _():
            acc_ref[...] += part

        @pl.when(k == nk - 1)
        def _():
            r = acc_ref[...]
            if has_res:
                r = r + res_ref[...]
            o_ref[...] = r.astype(o_ref.dtype)

    in_specs = [a_spec, b_spec] + ([res_spec] if has_res else [])
    operands = [a, b] + ([res] if has_res else [])
    scratch = [] if nk == 1 else [pltpu.VMEM(acc_shape, F32)]
    sem = ("parallel",) * (len(grid) - 1) + ("arbitrary",)
    return _pcall(body, name=name, grid=grid, in_specs=in_specs, out_specs=o_spec, out_shape=out_shape,
                  scratch_shapes=scratch, compiler_params=_params(sem))(*operands)


def _mm_rows(name, a, b, out_dtype, tm, res=None):
    M, K = a.shape
    N = b.shape[1]
    return _mm(name, "nn", (M // tm, 1), a, pl.BlockSpec((tm, K), lambda i, k: (i, 0)),
               b, pl.BlockSpec((K, N), lambda i, k: (0, 0)),
               jax.ShapeDtypeStruct((M, N), out_dtype), pl.BlockSpec((tm, N), lambda i, k: (i, 0)), None,
               res=res, res_spec=pl.BlockSpec((tm, N), lambda i, k: (i, 0)))


def _mm_nt_rows(name, a, b, tm):
    M, N = a.shape
    K = b.shape[0]
    return _mm(name, "nt", (M // tm, 1), a, pl.BlockSpec((tm, N), lambda i, k: (i, 0)),
               b, pl.BlockSpec((K, N), lambda i, k: (0, 0)),
               jax.ShapeDtypeStruct((M, K), F32), pl.BlockSpec((tm, K), lambda i, k: (i, 0)), None)


def _mm_tn(name, a, b, tm, tk):
    T, M = a.shape
    N = b.shape[1]
    return _mm(name, "tn", (M // tm, T // tk), a, pl.BlockSpec((tk, tm), lambda i, k: (k, i)),
               b, pl.BlockSpec((tk, N), lambda i, k: (k, 0)),
               jax.ShapeDtypeStruct((M, N), BF16), pl.BlockSpec((tm, N), lambda i, k: (i, 0)), (tm, N))


def _rms_fwd(name, x, g, tr=512):
    S, D = x.shape

    def body(x_ref, g_ref, o_ref):
        xf = x_ref[...]
        r = lax.rsqrt(jnp.mean(xf * xf, axis=-1, keepdims=True) + RMS_EPS)
        o_ref[...] = ((xf * r) * g_ref[...]).astype(o_ref.dtype)

    return _pcall(body, name=name, grid=(S // tr,),
                  in_specs=[pl.BlockSpec((tr, D), lambda i: (i, 0)), pl.BlockSpec((1, D), lambda i: (0, 0))],
                  out_specs=pl.BlockSpec((tr, D), lambda i: (i, 0)),
                  out_shape=jax.ShapeDtypeStruct((S, D), BF16), compiler_params=_params(("parallel",)))(x, g)


def _rms_bwd(name, x, g, dh, dres, tr=512):
    S, D = x.shape

    def body(x_ref, g_ref, dh_ref, dres_ref, dx_ref, dg_ref):
        i = pl.program_id(0)
        xf = x_ref[...]
        r = lax.rsqrt(jnp.mean(xf * xf, axis=-1, keepdims=True) + RMS_EPS)
        xh = xf * r
        dh_v = dh_ref[...]
        dg = jnp.sum(dh_v * xh, axis=0, keepdims=True)
        t = dh_v * g_ref[...]
        dx_ref[...] = dres_ref[...] + r * (t - xh * jnp.mean(t * xh, axis=-1, keepdims=True))

        @pl.when(i == 0)
        def _():
            dg_ref[...] = dg

        @pl.when(i > 0)
        def _():
            dg_ref[...] += dg

    row = pl.BlockSpec((tr, D), lambda i: (i, 0))
    vec = pl.BlockSpec((1, D), lambda i: (0, 0))
    return _pcall(body, name=name, grid=(S // tr,), in_specs=[row, vec, row, row], out_specs=[row, vec],
                  out_shape=[jax.ShapeDtypeStruct((S, D), F32), jax.ShapeDtypeStruct((1, D), F32)],
                  compiler_params=_params(("arbitrary",)))(x, g, dh, dres)


def _loss_bwd(name, x, g, tgt, tr=512):
    S, D = x.shape

    def body(x_ref, g_ref, t_ref, loss_ref, dx_ref, dg_ref):
        i = pl.program_id(0)
        xf = x_ref[...]
        r = lax.rsqrt(jnp.mean(xf * xf, axis=-1, keepdims=True) + RMS_EPS)
        xh = xf * r
        gv = g_ref[...]
        e = xh * gv - t_ref[...]
        lp = 0.5 * jnp.sum(jnp.mean(e * e, axis=-1, keepdims=True), axis=0, keepdims=True)
        dy = e * (1.0 / D)
        dg = jnp.sum(dy * xh, axis=0, keepdims=True)
        t = dy * gv
        dx_ref[...] = r * (t - xh * jnp.mean(t * xh, axis=-1, keepdims=True))

        @pl.when(i == 0)
        def _():
            dg_ref[...] = dg
            loss_ref[...] = lp

        @pl.when(i > 0)
        def _():
            dg_ref[...] += dg
            loss_ref[...] += lp

    row = pl.BlockSpec((tr, D), lambda i: (i, 0))
    vec = pl.BlockSpec((1, D), lambda i: (0, 0))
    one = pl.BlockSpec((1, 1), lambda i: (0, 0))
    return _pcall(body, name=name, grid=(S // tr,), in_specs=[row, vec, row], out_specs=[one, row, vec],
                  out_shape=[jax.ShapeDtypeStruct((1, 1), F32), jax.ShapeDtypeStruct((S, D), F32),
                             jax.ShapeDtypeStruct((1, D), F32)],
                  compiler_params=_params(("arbitrary",)))(x, g, tgt)


def _ffn_up(name, h, w_gu, tm=512):
    S, D = h.shape
    W = FF_SHARD

    def body(h_ref, wg_ref, wu_ref, gu_ref, act_ref):
        hv = h_ref[...]
        gate = _dot(hv, wg_ref[...], "nn")
        up = _dot(hv, wu_ref[...], "nn")
        gu_ref[0] = gate
        gu_ref[1] = up
        sig = 1.0 / (1.0 + jnp.exp(-gate))
        act_ref[...] = ((gate * sig) * up).astype(act_ref.dtype)

    return _pcall(
        body, name=name, grid=(2, S // tm),
        in_specs=[pl.BlockSpec((tm, D), lambda j, i: (i, 0)),
                  pl.BlockSpec((None, D, W), lambda j, i: (j, 0, 0)),
                  pl.BlockSpec((None, D, W), lambda j, i: (j + 2, 0, 0))],
        out_specs=[pl.BlockSpec((2, tm, W), lambda j, i: (0, i, j)), pl.BlockSpec((tm, W), lambda j, i: (i, j))],
        out_shape=[jax.ShapeDtypeStruct((2, S, D_FF), F32), jax.ShapeDtypeStruct((S, D_FF), BF16)],
        compiler_params=_params(("parallel", "parallel")))(h, w_gu, w_gu)


def _ffn_down_bwd(name, dx, w_down, gu, tm=512):
    S, D = dx.shape
    W = FF_SHARD

    def body(dx_ref, w_ref, gu_ref, dgu_ref):
        dact = _dot(dx_ref[...].astype(BF16), w_ref[...], "nt")
        gate = gu_ref[0]
        up = gu_ref[1]
        sig = 1.0 / (1.0 + jnp.exp(-gate))
        silu = gate * sig
        dgu_ref[0] = (dact * up * (sig * (1.0 + gate * (1.0 - sig)))).astype(dgu_ref.dtype)
        dgu_ref[1] = (dact * silu).astype(dgu_ref.dtype)

    blk = pl.BlockSpec((2, tm, W), lambda j, i: (0, i, j))
    return _pcall(
        body, name=name, grid=(2, S // tm),
        in_specs=[pl.BlockSpec((tm, D), lambda j, i: (i, 0)), pl.BlockSpec((W, D), lambda j, i: (j, 0)), blk],
        out_specs=blk, out_shape=jax.ShapeDtypeStruct((2, S, D_FF), BF16),
        compiler_params=_params(("parallel", "parallel")))(dx, w_down, gu)


def _ffn_dw_up(name, h, dgu, tk=1024):
    S, D = h.shape
    W = FF_SHARD
    return _mm(name, "tn", (N_CHIPS, S // tk), h, pl.BlockSpec((tk, D), lambda s, k: (k, 0)),
               dgu, pl.BlockSpec((None, tk, W), lambda s, k: (s // 2, k, s % 2)),
               jax.ShapeDtypeStruct((N_CHIPS, D, W), BF16), pl.BlockSpec((None, D, W), lambda s, k: (s, 0, 0)),
               (D, W))


def _ffn_dh(name, dgu, w_gu, tm=1024):
    S = dgu.shape[1]
    W = FF_SHARD
    return _mm(name, "nt", (S // tm, N_CHIPS), dgu, pl.BlockSpec((None, tm, W), lambda i, k: (k // 2, i, k % 2)),
               w_gu, pl.BlockSpec((None, D_MODEL, W), lambda i, k: (k, 0, 0)),
               jax.ShapeDtypeStruct((S, D_MODEL), F32), pl.BlockSpec((tm, D_MODEL), lambda i, k: (i, 0)),
               (tm, D_MODEL))


def _qkv_proj(name, hf, w_qkv, tm=1024):
    S = hf.shape[1]
    per_chip = QKV_SHARD // QKV_TILE
    per_group = 3 * D_MODEL // QKV_TILE
    return _mm(name, "nn", (N_GROUPS, S // tm, per_group, 1),
               hf, pl.BlockSpec((None, tm, D_MODEL), lambda g, i, j, k: (g, i, 0)),
               w_qkv, pl.BlockSpec((None, D_MODEL, QKV_TILE),
                                   lambda g, i, j, k: ((per_group * g + j) // per_chip, 0, (per_group * g + j) % per_chip)),
               jax.ShapeDtypeStruct((N_GROUPS, S, 3 * D_MODEL), BF16),
               pl.BlockSpec((None, tm, QKV_TILE), lambda g, i, j, k: (g, i, j)), None)


def _qkv_dw(name, hf, dqkv, tk=1024):
    S = hf.shape[1]
    per_chip = QKV_SHARD // QKV_TILE
    per_group = 3 * D_MODEL // QKV_TILE
    return _mm(name, "tn", (N_GROUPS, per_group, S // tk),
               hf, pl.BlockSpec((None, tk, D_MODEL), lambda g, j, k: (g, k, 0)),
               dqkv, pl.BlockSpec((None, tk, QKV_TILE), lambda g, j, k: (g, k, j)),
               jax.ShapeDtypeStruct((N_CHIPS, D_MODEL, QKV_SHARD), BF16),
               pl.BlockSpec((None, D_MODEL, QKV_TILE),
                            lambda g, j, k: ((per_group * g + j) // per_chip, 0, (per_group * g + j) % per_chip)),
               (D_MODEL, QKV_TILE))


def _qkv_dh(name, dqkv, w_qkv, tm=1024):
    S = dqkv.shape[1]
    per_chip = QKV_SHARD // QKV_TILE
    per_group = 3 * D_MODEL // QKV_TILE
    return _mm(name, "nt", (N_GROUPS, S // tm, per_group),
               dqkv, pl.BlockSpec((None, tm, QKV_TILE), lambda g, i, k: (g, i, k)),
               w_qkv, pl.BlockSpec((None, D_MODEL, QKV_TILE),
                                   lambda g, i, k: ((per_group * g + k) // per_chip, 0, (per_group * g + k) % per_chip)),
               jax.ShapeDtypeStruct((N_GROUPS, S, D_MODEL), F32),
               pl.BlockSpec((None, tm, D_MODEL), lambda g, i, k: (g, i, 0)), (tm, D_MODEL))


def _alibi_coef(g, h):
    vals = [-(2.0 ** (-8.0 * (gg * N_HEADS + h + 1) / (N_GROUPS * N_HEADS))) * ATTN_DILATIONS[gg]
            for gg in range(N_GROUPS)]
    return jnp.where(g == 0, vals[0], jnp.where(g == 1, vals[1], vals[2])).astype(F32)


def _blocks_per_segment(g, S):
    return jnp.right_shift(S // Q_BLOCK, 2 * g)


def _band_bias(pen):
    row = lax.broadcasted_iota(jnp.int32, (Q_BLOCK, 2 * Q_BLOCK), 0)
    col = lax.broadcasted_iota(jnp.int32, (Q_BLOCK, 2 * Q_BLOCK), 1)
    dist = Q_BLOCK + row - col
    valid = jnp.logical_and(dist >= 0, dist <= Q_BLOCK)
    base = jnp.where(valid, jnp.where(col < Q_BLOCK, pen, 0.0), NEG).astype(F32)
    return base, dist.astype(F32)


def _skewed(n_units, stages):
    state = {}
    for step in range(n_units + len(stages) - 1):
        for s, stage in enumerate(stages):
            u = step - s
            if 0 <= u < n_units:
                state[u] = stage(u, state.get(u))
                if s == len(stages) - 1:
                    del state[u]


def _attn_fwd(name, qkv, tq=512):
    S = qkv.shape[1]
    nqb = tq // Q_BLOCK
    scale = HEAD_DIM ** -0.5

    def body(q_ref, k_ref, kp_ref, v_ref, vp_ref, o_ref, lse_ref, kf_ref, vf_ref):
        g = pl.program_id(0)
        i = pl.program_id(1)
        nb = _blocks_per_segment(g, S)
        kf_ref[:Q_BLOCK] = kp_ref[...]
        kf_ref[Q_BLOCK:] = k_ref[...]
        vf_ref[:Q_BLOCK] = vp_ref[...]
        vf_ref[Q_BLOCK:] = v_ref[...]
        coefs = [_alibi_coef(g, h) for h in range(N_HEADS)]
        units = [(b, h) for b in range(nqb) for h in range(N_HEADS)]
        bias = {}

        def scores(u, _):
            b, h = units[u]
            if b not in bias:
                pen = jnp.where((i * nqb + b) % nb != 0, 0.0, NEG).astype(F32)
                bias.clear()
                bias[b] = _band_bias(pen)
            base, dist = bias[b]
            cols = slice(h * HEAD_DIM, (h + 1) * HEAD_DIM)
            q = q_ref[b * Q_BLOCK:(b + 1) * Q_BLOCK, cols]
            kk = kf_ref[b * Q_BLOCK:(b + 2) * Q_BLOCK, cols]
            return _dot(q, kk, "nt") * scale + (coefs[h] * dist + base)

        def softmax(u, s):
            m = jnp.max(s, axis=-1, keepdims=True)
            p = jnp.exp(s - m)
            den = jnp.sum(p, axis=-1, keepdims=True)
            return (p * (1.0 / den)).astype(BF16), m + jnp.log(den)

        def output(u, st):
            b, h = units[u]
            pn, lse = st
            cols = slice(h * HEAD_DIM, (h + 1) * HEAD_DIM)
            rows = slice(b * Q_BLOCK, (b + 1) * Q_BLOCK)
            o_ref[rows, cols] = _dot(pn, vf_ref[b * Q_BLOCK:(b + 2) * Q_BLOCK, cols], "nn")
            lse_ref[rows, h:h + 1] = lse

        _skewed(len(units), [scores, softmax, output])

    main = lambda c: pl.BlockSpec((None, tq, D_MODEL), lambda g, i: (g, i, c))
    prev = lambda c: pl.BlockSpec((None, Q_BLOCK, D_MODEL), lambda g, i: (g, jnp.maximum(i * nqb - 1, 0), c))
    return _pcall(
        body, name=name, grid=(N_GROUPS, S // tq),
        in_specs=[main(0), main(1), prev(1), main(2), prev(2)],
        out_specs=[pl.BlockSpec((None, tq, D_MODEL), lambda g, i: (g, i, 0)),
                   pl.BlockSpec((None, tq, N_HEADS), lambda g, i: (g, i, 0))],
        out_shape=[jax.ShapeDtypeStruct((N_GROUPS, S, D_MODEL), F32),
                   jax.ShapeDtypeStruct((N_GROUPS, S, N_HEADS), F32)],
        scratch_shapes=[pltpu.VMEM((tq + Q_BLOCK, D_MODEL), BF16), pltpu.VMEM((tq + Q_BLOCK, D_MODEL), BF16)],
        compiler_params=_params(("parallel", "parallel")))(qkv, qkv, qkv, qkv, qkv)


def _attn_bwd(name, qkv, do, lse, dl, tq=512):
    S = qkv.shape[1]
    nqb = tq // Q_BLOCK
    n_blocks = S // Q_BLOCK
    scale = HEAD_DIM ** -0.5
    KEYS = 2 * Q_BLOCK

    def body(q_ref, k_ref, v_ref, kp_ref, vp_ref, qn_ref, do_ref, don_ref, lse_ref, lsen_ref, dl_ref, dln_ref,
             out_ref, kf_ref, vf_ref, dk_ref, dv_ref):
        g = pl.program_id(0)
        i = pl.program_id(1)
        nb = _blocks_per_segment(g, S)
        kf_ref[:Q_BLOCK] = kp_ref[...]
        kf_ref[Q_BLOCK:] = k_ref[...]
        vf_ref[:Q_BLOCK] = vp_ref[...]
        vf_ref[Q_BLOCK:] = v_ref[...]
        coefs = [_alibi_coef(g, h) for h in range(N_HEADS)]
        units = [(h, b) for h in range(N_HEADS) for b in range(nqb + 1)]
        bias = {}

        def band(b):
            if b not in bias:
                blk = i * nqb + b
                ok = blk % nb != 0
                if b == nqb:
                    ok = jnp.logical_and(ok, blk < n_blocks)
                bias[b] = _band_bias(jnp.where(ok, 0.0, NEG).astype(F32))
            return bias[b]

        def operands(h, b):
            cols = slice(h * HEAD_DIM, (h + 1) * HEAD_DIM)
            if b == nqb:
                keys = slice(tq, tq + Q_BLOCK)
                return (qn_ref[:, cols], don_ref[:, cols], lsen_ref[:, h:h + 1], dln_ref[:, h:h + 1],
                        kf_ref[keys, cols], vf_ref[keys, cols])
            rows = slice(b * Q_BLOCK, (b + 1) * Q_BLOCK)
            keys = slice(b * Q_BLOCK, b * Q_BLOCK + KEYS)
            return (q_ref[rows, cols], do_ref[rows, cols], lse_ref[rows, h:h + 1], dl_ref[rows, h:h + 1],
                    kf_ref[keys, cols], vf_ref[keys, cols])

        def probs(u, _):
            h, b = units[u]
            q, do_b, lse_b, dl_b, kk, vv = operands(h, b)
            base, dist = band(b)
            if b == nqb:
                base, dist = base[:, :Q_BLOCK], dist[:, :Q_BLOCK]
            p = jnp.exp(_dot(q, kk, "nt") * scale + (coefs[h] * dist + base) - lse_b)
            return p, _dot(do_b, vv, "nt")

        def dscores(u, st):
            h, b = units[u]
            p, dp = st
            dl_b = operands(h, b)[3]
            return ((p * (dp - dl_b)) * scale).astype(BF16), p.astype(BF16)

        def grads(u, st):
            h, b = units[u]
            ds, pb = st
            q, do_b, _, _, kk, _ = operands(h, b)
            cols = slice(h * HEAD_DIM, (h + 1) * HEAD_DIM)
            if b < nqb:
                out_ref[b * Q_BLOCK:(b + 1) * Q_BLOCK, cols] = _dot(ds, kk, "nn").astype(out_ref.dtype)
            if b == 0:
                dk_ref[:Q_BLOCK] = _dot(ds[:, Q_BLOCK:], q, "tn")
                dv_ref[:Q_BLOCK] = _dot(pb[:, Q_BLOCK:], do_b, "tn")
                return None
            dk = _dot(ds, q, "tn")
            dv = _dot(pb, do_b, "tn")
            before = slice((b - 1) * Q_BLOCK, b * Q_BLOCK)
            dk_ref[before] += dk[:Q_BLOCK]
            dv_ref[before] += dv[:Q_BLOCK]
            if b < nqb:
                own = slice(b * Q_BLOCK, (b + 1) * Q_BLOCK)
                dk_ref[own] = dk[Q_BLOCK:]
                dv_ref[own] = dv[Q_BLOCK:]
            else:
                out_ref[:, D_MODEL + h * HEAD_DIM:D_MODEL + (h + 1) * HEAD_DIM] = dk_ref[...].astype(out_ref.dtype)
                out_ref[:, 2 * D_MODEL + h * HEAD_DIM:2 * D_MODEL + (h + 1) * HEAD_DIM] = (
                    dv_ref[...].astype(out_ref.dtype))
            return None

        _skewed(len(units), [probs, dscores, grads])

    nxt_blk = lambda i: jnp.minimum((i + 1) * nqb, n_blocks - 1)
    main = lambda c: pl.BlockSpec((None, tq, D_MODEL), lambda g, i: (g, i, c))
    prev = lambda c: pl.BlockSpec((None, Q_BLOCK, D_MODEL), lambda g, i: (g, jnp.maximum(i * nqb - 1, 0), c))
    row = pl.BlockSpec((None, tq, D_MODEL), lambda g, i: (g, i, 0))
    row_n = pl.BlockSpec((None, Q_BLOCK, D_MODEL), lambda g, i: (g, nxt_blk(i), 0))
    col = pl.BlockSpec((None, tq, N_HEADS), lambda g, i: (g, i, 0))
    col_n = pl.BlockSpec((None, Q_BLOCK, N_HEADS), lambda g, i: (g, nxt_blk(i), 0))
    return _pcall(
        body, name=name, grid=(N_GROUPS, S // tq),
        in_specs=[main(0), main(1), main(2), prev(1), prev(2), row_n, row, row_n, col, col_n, col, col_n],
        out_specs=pl.BlockSpec((None, tq, 3 * D_MODEL), lambda g, i: (g, i, 0)),
        out_shape=jax.ShapeDtypeStruct((N_GROUPS, S, 3 * D_MODEL), BF16),
        scratch_shapes=[pltpu.VMEM((tq + Q_BLOCK, D_MODEL), BF16), pltpu.VMEM((tq + Q_BLOCK, D_MODEL), BF16),
                        pltpu.VMEM((tq, HEAD_DIM), F32), pltpu.VMEM((tq, HEAD_DIM), F32)],
        compiler_params=_params(("parallel", "parallel")))(qkv, qkv, qkv, qkv, qkv, qkv, do, do, lse, lse, dl, dl)


def _group_weights(lse_ref):
    l0, l1, l2 = lse_ref[0], lse_ref[1], lse_ref[2]
    m = jnp.maximum(jnp.maximum(l0, l1), l2)
    e = [jnp.exp(l0 - m), jnp.exp(l1 - m), jnp.exp(l2 - m)]
    den = e[0] + e[1] + e[2]
    return [ei / den for ei in e]


def _merge_fwd(name, o, lse, tr=256):
    S = o.shape[1]

    def body(o_ref, lse_ref, out_ref):
        w = _group_weights(lse_ref)
        for h in range(N_HEADS):
            cols = slice(h * HEAD_DIM, (h + 1) * HEAD_DIM)
            acc = w[0][:, h:h + 1] * o_ref[0, :, cols]
            for gg in range(1, N_GROUPS):
                acc = acc + w[gg][:, h:h + 1] * o_ref[gg, :, cols]
            out_ref[:, cols] = acc.astype(out_ref.dtype)

    return _pcall(body, name=name, grid=(S // tr,),
                  in_specs=[pl.BlockSpec((N_GROUPS, tr, D_MODEL), lambda i: (0, i, 0)),
                            pl.BlockSpec((N_GROUPS, tr, N_HEADS), lambda i: (0, i, 0))],
                  out_specs=pl.BlockSpec((tr, D_MODEL), lambda i: (i, 0)),
                  out_shape=jax.ShapeDtypeStruct((S, D_MODEL), BF16), compiler_params=_params(("parallel",)))(o, lse)


def _merge_bwd(name, d_out, o, lse, tr=256):
    S = o.shape[1]

    def body(d_ref, o_ref, lse_ref, do_ref, dl_ref):
        w = _group_weights(lse_ref)
        for h in range(N_HEADS):
            cols = slice(h * HEAD_DIM, (h + 1) * HEAD_DIM)
            dv = d_ref[:, cols]
            merged = w[0][:, h:h + 1] * o_ref[0, :, cols]
            for gg in range(1, N_GROUPS):
                merged = merged + w[gg][:, h:h + 1] * o_ref[gg, :, cols]
            dsum = jnp.sum(dv * merged, axis=-1, keepdims=True)
            for gg in range(N_GROUPS):
                wg = w[gg][:, h:h + 1]
                do_ref[gg, :, cols] = (wg * dv).astype(do_ref.dtype)
                dl_ref[gg, :, h:h + 1] = wg * dsum

    big = pl.BlockSpec((N_GROUPS, tr, D_MODEL), lambda i: (0, i, 0))
    small = pl.BlockSpec((N_GROUPS, tr, N_HEADS), lambda i: (0, i, 0))
    return _pcall(body, name=name, grid=(S // tr,),
                  in_specs=[pl.BlockSpec((tr, D_MODEL), lambda i: (i, 0)), big, small], out_specs=[big, small],
                  out_shape=[jax.ShapeDtypeStruct((N_GROUPS, S, D_MODEL), BF16),
                             jax.ShapeDtypeStruct((N_GROUPS, S, N_HEADS), F32)],
                  compiler_params=_params(("parallel",)))(d_out, o, lse)


def _shift_rows(a, k):
    return pltpu.roll(a, k % a.shape[0], axis=0)


def _pool_level(g, levels):
    return jnp.where(g == 0, levels[0], jnp.where(g == 1, levels[1], jnp.where(g == 2, levels[2], levels[3])))


def _pool_fwd(name, u, w_group, scale, x_in, tr=1024):
    S, D = u.shape
    H = POOL_HALO

    def body(u_ref, halo_ref, w_ref, sc_ref, x_ref, y_ref, z_ref, xo_ref):
        g = pl.program_id(0)
        i = pl.program_id(1)
        uv = u_ref[...]
        halo = jnp.where(i > 0, halo_ref[...], 0.0)
        s = jnp.concatenate([halo, uv], axis=0)
        levels = []
        for k in (1, 2, 4, 8):
            s = s + _shift_rows(s, k)
            levels.append(s[H:])
        t = i * tr + lax.broadcasted_iota(jnp.int32, (tr, 1), 0)
        cnt = jnp.minimum(t + 1, jnp.left_shift(2, g)).astype(F32)
        y = _pool_level(g, levels) / cnt - uv
        yb = y.astype(BF16)
        z = _dot(yb, w_ref[...], "nn")
        y_ref[...] = yb
        z_ref[...] = z
        xo_ref[...] = x_ref[...] + z * sc_ref[...]

    blk = pl.BlockSpec((tr, POOL_DIM), lambda g, i: (i, g))
    return _pcall(
        body, name=name, grid=(D // POOL_DIM, S // tr),
        in_specs=[blk, pl.BlockSpec((H, POOL_DIM), lambda g, i: (jnp.maximum(i * (tr // H) - 1, 0), g)),
                  pl.BlockSpec((None, POOL_DIM, POOL_DIM), lambda g, i: (g, 0, 0)),
                  pl.BlockSpec((1, POOL_DIM), lambda g, i: (0, g)), blk],
        out_specs=[blk, blk, blk],
        out_shape=[jax.ShapeDtypeStruct((S, D), BF16), jax.ShapeDtypeStruct((S, D), F32),
                   jax.ShapeDtypeStruct((S, D), F32)],
        compiler_params=_params(("parallel", "parallel")))(u, u, w_group, scale, x_in)


def _pool_bwd(name, d_out, z, y, scale, w_group, tr=1024):
    S, D = d_out.shape
    H = POOL_HALO

    def body(d_ref, dn_ref, z_ref, y_ref, sc_ref, w_ref, du_ref, dw_ref, dsc_ref):
        g = pl.program_id(0)
        i = pl.program_id(1)
        dv = d_ref[...]
        dz = jnp.concatenate([dv, dn_ref[...]], axis=0) * sc_ref[...]
        dzb = dz.astype(BF16)
        dy = _dot(dzb, w_ref[...], "nt")
        t = i * tr + lax.broadcasted_iota(jnp.int32, (tr + H, 1), 0)
        cnt = jnp.minimum(t + 1, jnp.left_shift(2, g)).astype(F32)
        s = jnp.where(t < S, dy / cnt, 0.0)
        levels = []
        for k in (1, 2, 4, 8):
            s = s + _shift_rows(s, -k)
            levels.append(s[:tr])
        du_ref[...] = (_pool_level(g, levels) - dy[:tr]).astype(du_ref.dtype)
        dw = _dot(y_ref[...], dzb[:tr], "tn")
        dsc = jnp.sum(dv * z_ref[...], axis=0, keepdims=True)

        @pl.when(i == 0)
        def _():
            dw_ref[...] = dw
            dsc_ref[...] = dsc

        @pl.when(i > 0)
        def _():
            dw_ref[...] += dw
            dsc_ref[...] += dsc

    blk = pl.BlockSpec((tr, POOL_DIM), lambda g, i: (i, g))
    vec = pl.BlockSpec((1, POOL_DIM), lambda g, i: (0, g))
    mat = pl.BlockSpec((None, POOL_DIM, POOL_DIM), lambda g, i: (g, 0, 0))
    nxt = pl.BlockSpec((H, POOL_DIM), lambda g, i: (jnp.minimum((i + 1) * (tr // H), S // H - 1), g))
    return _pcall(
        body, name=name, grid=(D // POOL_DIM, S // tr), in_specs=[blk, nxt, blk, blk, vec, mat],
        out_specs=[blk, mat, vec],
        out_shape=[jax.ShapeDtypeStruct((S, D), BF16), jax.ShapeDtypeStruct((D // POOL_DIM, POOL_DIM, POOL_DIM), F32),
                   jax.ShapeDtypeStruct((1, D), F32)],
        compiler_params=_params(("parallel", "arbitrary")))(d_out, d_out, z, y, scale, w_group)


def _row_tile(rows, cols, target_bytes=1 << 20):
    best = rows
    for tr in range(8, rows + 1, 8):
        if rows % tr == 0 and tr * cols * 4 <= target_bytes:
            best = tr
    return best if best * cols * 4 <= 4 * target_bytes else rows


def _adamw(name, w, g, m, v):
    R, C = w.shape
    tr = _row_tile(R, C) if R % 8 == 0 else R

    def body(w_ref, g_ref, m_ref, v_ref, d_ref, mo_ref, vo_ref):
        gv = g_ref[...]
        m2 = ADAM_B1 * m_ref[...] + (1.0 - ADAM_B1) * gv
        v2 = ADAM_B2 * v_ref[...] + (1.0 - ADAM_B2) * (gv * gv)
        m_hat = m2 / (1.0 - ADAM_B1 ** ADAM_STEP)
        v_hat = v2 / (1.0 - ADAM_B2 ** ADAM_STEP)
        d_ref[...] = -ADAM_LR * (m_hat / (jnp.sqrt(v_hat) + ADAM_EPS) + ADAM_WD * w_ref[...])
        mo_ref[...] = m2
        vo_ref[...] = v2

    blk = pl.BlockSpec((tr, C), lambda i: (i, 0))
    sds = jax.ShapeDtypeStruct((R, C), F32)
    return _pcall(body, name=name, grid=(R // tr,), in_specs=[blk] * 4, out_specs=[blk] * 3, out_shape=[sds] * 3,
                  compiler_params=_params(("parallel",)))(w, g, m, v)


def _sum_leading(name, a, out_dtype):
    n, R, C = a.shape
    tr = _row_tile(R, C) if R % 16 == 0 else R
    if tr % 16:
        tr = R

    def body(a_ref, o_ref):
        acc = a_ref[0].astype(F32)
        for j in range(1, n):
            acc = acc + a_ref[j].astype(F32)
        o_ref[...] = acc.astype(o_ref.dtype)

    return _pcall(body, name=name, grid=(R // tr,), in_specs=[pl.BlockSpec((n, tr, C), lambda i: (0, i, 0))],
                  out_specs=pl.BlockSpec((tr, C), lambda i: (i, 0)), out_shape=jax.ShapeDtypeStruct((R, C), out_dtype),
                  compiler_params=_params(("parallel",)))(a)


def _cast_into_slot(name, w, chip, dtype):
    R, C = w.shape
    tr = _row_tile(R, C)
    if tr % 16:
        tr = R

    def body(c_ref, w_ref, o_ref):
        o_ref[...] = w_ref[...].astype(o_ref.dtype)

    grid_spec = pltpu.PrefetchScalarGridSpec(
        num_scalar_prefetch=1, grid=(R // tr,), in_specs=[pl.BlockSpec((tr, C), lambda i, c_ref: (i, 0))],
        out_specs=pl.BlockSpec((None, tr, C), lambda i, c_ref: (c_ref[0], i, 0)))
    return _pcall(body, name=name, grid_spec=grid_spec, out_shape=jax.ShapeDtypeStruct((N_CHIPS, R, C), dtype),
                  compiler_params=_params(("parallel",)))(chip, w)


def _owner_sum(name, mine, landed, chip, core, out=None, layer=None):
    _, half, C = mine.shape
    tr = _row_tile(half, C)
    if tr % 16:
        tr = half
    nrt = half // tr
    has_out = out is not None

    def body(*refs):
        m_ref, l_ref, o_ref = refs[2], refs[3], refs[-1]
        acc = m_ref[...].astype(F32)
        for j in range(3):
            acc = acc + l_ref[j].astype(F32)
        o_ref[...] = acc

    if layer is None:
        out_shape = jax.ShapeDtypeStruct((2 * half, C), F32)
        o_spec = pl.BlockSpec((tr, C), lambda i, ch, co: (co[0] * nrt + i, 0))
    else:
        out_shape = jax.ShapeDtypeStruct((2, 2 * half, C), F32)
        o_spec = pl.BlockSpec((None, tr, C), lambda i, ch, co: (layer, co[0] * nrt + i, 0))
    in_specs = [pl.BlockSpec((None, tr, C), lambda i, ch, co: (ch[0], i, 0)),
                pl.BlockSpec((3, tr, C), lambda i, ch, co: (0, i, 0))]
    operands = [chip, core, mine, landed]
    aliases = {}
    if has_out:
        in_specs.append(ANY)
        operands.append(out)
        aliases = {4: 0}
    grid_spec = pltpu.PrefetchScalarGridSpec(num_scalar_prefetch=2, grid=(nrt,), in_specs=in_specs, out_specs=o_spec)
    return _pcall(body, name=name, grid_spec=grid_spec, out_shape=out_shape, input_output_aliases=aliases,
                  compiler_params=_params(("parallel",)))(*operands)


def _add_my_half(name, p, q, core):
    n, R, C = p.shape
    half = R // 2

    def body(c_ref, p_ref, q_ref, o_ref):
        o_ref[...] = (p_ref[...].astype(F32) + q_ref[...].astype(F32)).astype(o_ref.dtype)

    grid_spec = pltpu.PrefetchScalarGridSpec(
        num_scalar_prefetch=1, grid=(n,),
        in_specs=[pl.BlockSpec((None, half, C), lambda s, c_ref: (s, c_ref[0], 0)),
                  pl.BlockSpec((None, half, C), lambda s, c_ref: (s, 0, 0))],
        out_specs=pl.BlockSpec((None, half, C), lambda s, c_ref: (s, 0, 0)))
    return _pcall(body, name=name, grid_spec=grid_spec, out_shape=jax.ShapeDtypeStruct((n, half, C), BF16),
                  compiler_params=_params(("parallel",)))(core, p, q)


ANY = pl.BlockSpec(memory_space=pl.ANY)


def _place():
    x, y, c = lax.axis_index("x"), lax.axis_index("y"), lax.axis_index("c")
    other_chips = [(1 - x, y), (x, 1 - y), (1 - x, 1 - y)]
    return x, y, c, other_chips


def _gather_weights(name, bufs, split):
    n = len(bufs)

    def body(*refs):
        outs = refs[n:2 * n]
        ici_send, ici_recv, d2d_send, d2d_recv = refs[2 * n:]
        x, y, c, chips = _place()
        me = 2 * x + y
        sibling = (x, y, 1 - c)

        def piece(t, chip, core_half):
            if not split[t]:
                return outs[t].at[chip]
            half = outs[t].shape[1] // 2
            return outs[t].at[chip, pl.ds(core_half * half, half)]

        sends = []
        for t in range(n):
            for j, (px, py) in enumerate(chips):
                sends.append(pltpu.make_async_remote_copy(
                    src_ref=piece(t, me, c), dst_ref=piece(t, me, c), send_sem=ici_send.at[3 * t + j],
                    recv_sem=ici_recv.at[3 * t + j], device_id=(px, py, c), device_id_type=MESH))
        for cp in sends:
            cp.start()
        passed = []
        for t in range(n):
            for j, (px, py) in enumerate(chips):
                landed = piece(t, 2 * px + py, c)
                pltpu.make_async_remote_copy(
                    src_ref=landed, dst_ref=landed, send_sem=ici_send.at[3 * t + j],
                    recv_sem=ici_recv.at[3 * t + j], device_id=(px, py, c), device_id_type=MESH).wait_recv()
                if split[t]:
                    fwd = pltpu.make_async_remote_copy(
                        src_ref=landed, dst_ref=landed, send_sem=d2d_send.at[3 * t + j],
                        recv_sem=d2d_recv.at[3 * t + j], device_id=sibling, device_id_type=MESH)
                    fwd.start()
                    passed.append(fwd)
        for t in range(n):
            if split[t]:
                for j, (px, py) in enumerate(chips):
                    theirs = piece(t, 2 * px + py, 1 - c)
                    pltpu.make_async_remote_copy(
                        src_ref=theirs, dst_ref=theirs, send_sem=d2d_send.at[3 * t + j],
                        recv_sem=d2d_recv.at[3 * t + j], device_id=sibling, device_id_type=MESH).wait_recv()
        for cp in sends + passed:
            cp.wait_send()

    out_shape = [jax.ShapeDtypeStruct(b.shape, b.dtype) for b in bufs]
    return _pcall(body, name=name, in_specs=[ANY] * n, out_specs=[ANY] * n, out_shape=out_shape,
                  input_output_aliases={t: t for t in range(n)},
                  scratch_shapes=[pltpu.SemaphoreType.DMA((3 * n,)), pltpu.SemaphoreType.DMA((3 * n,)),
                                  pltpu.SemaphoreType.DMA((3 * n,)), pltpu.SemaphoreType.DMA((3 * n,))])(*bufs)


def _swap_other_half(name, parts):
    n = len(parts)

    def body(*refs):
        ins, outs = refs[:n], refs[n:2 * n]
        send_sem, recv_sem = refs[2 * n:]
        x, y, c, _ = _place()
        copies = []
        for t in range(n):
            half = ins[t].shape[1] // 2
            copies.append(pltpu.make_async_remote_copy(
                src_ref=ins[t].at[:, pl.ds((1 - c) * half, half)], dst_ref=outs[t], send_sem=send_sem.at[t],
                recv_sem=recv_sem.at[t], device_id=(x, y, 1 - c), device_id_type=MESH))
        for cp in copies:
            cp.start()
        for cp in copies:
            cp.wait()

    out_shape = [jax.ShapeDtypeStruct((p.shape[0], p.shape[1] // 2, p.shape[2]), p.dtype) for p in parts]
    return _pcall(body, name=name, in_specs=[ANY] * n, out_specs=[ANY] * n, out_shape=out_shape,
                  scratch_shapes=[pltpu.SemaphoreType.DMA((n,)), pltpu.SemaphoreType.DMA((n,))])(*parts)


def _scatter_to_owners(name, sums):
    n = len(sums)

    def body(*refs):
        ins, outs = refs[:n], refs[n:2 * n]
        send_sem, recv_sem = refs[2 * n:]
        x, y, c, chips = _place()
        copies = []
        for t in range(n):
            for j, (px, py) in enumerate(chips):
                copies.append(pltpu.make_async_remote_copy(
                    src_ref=ins[t].at[2 * px + py], dst_ref=outs[t].at[j], send_sem=send_sem.at[3 * t + j],
                    recv_sem=recv_sem.at[3 * t + j], device_id=(px, py, c), device_id_type=MESH))
        for cp in copies:
            cp.start()
        for cp in copies:
            cp.wait_recv()
        for cp in copies:
            cp.wait_send()

    out_shape = [jax.ShapeDtypeStruct((3,) + s.shape[1:], s.dtype) for s in sums]
    return _pcall(body, name=name, in_specs=[ANY] * n, out_specs=[ANY] * n, out_shape=out_shape,
                  scratch_shapes=[pltpu.SemaphoreType.DMA((3 * n,)), pltpu.SemaphoreType.DMA((3 * n,))])(*sums)


def _share_halves(name, blocks, layers):
    n = len(blocks)
    pieces = [(t, l) for t in range(n) for l in (range(layers[t]) if layers[t] else [None])]

    def body(*refs):
        outs = refs[n:2 * n]
        send_sem, recv_sem = refs[2 * n:]
        x, y, c, _ = _place()

        def half(t, l, core_half):
            ref = outs[t] if l is None else outs[t].at[l]
            r2 = ref.shape[0] // 2
            return ref.at[pl.ds(core_half * r2, r2)]

        copies = [pltpu.make_async_remote_copy(
            src_ref=half(t, l, c), dst_ref=half(t, l, c), send_sem=send_sem.at[k], recv_sem=recv_sem.at[k],
            device_id=(x, y, 1 - c), device_id_type=MESH) for k, (t, l) in enumerate(pieces)]
        for cp in copies:
            cp.start()
        for k, (t, l) in enumerate(pieces):
            theirs = half(t, l, 1 - c)
            pltpu.make_async_remote_copy(
                src_ref=theirs, dst_ref=theirs, send_sem=send_sem.at[k], recv_sem=recv_sem.at[k],
                device_id=(x, y, 1 - c), device_id_type=MESH).wait_recv()
        for cp in copies:
            cp.wait_send()

    out_shape = [jax.ShapeDtypeStruct(b.shape, b.dtype) for b in blocks]
    return _pcall(body, name=name, in_specs=[ANY] * n, out_specs=[ANY] * n, out_shape=out_shape,
                  input_output_aliases={t: t for t in range(n)},
                  scratch_shapes=[pltpu.SemaphoreType.DMA((len(pieces),)),
                                  pltpu.SemaphoreType.DMA((len(pieces),))])(*blocks)


def _gather_small(name, v):
    def body(v_ref, out_ref, send_sem, recv_sem, local_sem):
        x, y, c, _ = _place()
        me = 4 * x + 2 * y + c
        flips = [(fx, fy, fc) for fx in (0, 1) for fy in (0, 1) for fc in (0, 1)][1:]
        local = pltpu.make_async_copy(v_ref, out_ref.at[me], local_sem)
        local.start()
        copies = []
        for j, (fx, fy, fc) in enumerate(flips):
            peer = (x ^ fx, y ^ fy, c ^ fc)
            copies.append(pltpu.make_async_remote_copy(
                src_ref=v_ref, dst_ref=out_ref.at[me], send_sem=send_sem.at[j], recv_sem=recv_sem.at[j],
                device_id=peer, device_id_type=MESH))
        for cp in copies:
            cp.start()
        for j, (fx, fy, fc) in enumerate(flips):
            peer = (x ^ fx, y ^ fy, c ^ fc)
            pltpu.make_async_remote_copy(
                src_ref=v_ref, dst_ref=out_ref.at[4 * peer[0] + 2 * peer[1] + peer[2]], send_sem=send_sem.at[j],
                recv_sem=recv_sem.at[j], device_id=peer, device_id_type=MESH).wait_recv()
        for cp in copies:
            cp.wait_send()
        local.wait()

    return _pcall(body, name=name, in_specs=[ANY], out_specs=ANY,
                  out_shape=jax.ShapeDtypeStruct((8,) + v.shape, v.dtype),
                  scratch_shapes=[pltpu.SemaphoreType.DMA((7,)), pltpu.SemaphoreType.DMA((7,)),
                                  pltpu.SemaphoreType.DMA])(v)


def _fold(a, dil):
    if dil == 1:
        return a
    S, C = a.shape
    return a.reshape(S // dil, dil, C).transpose(1, 0, 2).reshape(S, C)


def _unfold(a, dil):
    if dil == 1:
        return a
    S, C = a.shape
    return a.reshape(dil, S // dil, C).transpose(1, 0, 2).reshape(S, C)


def _fold_groups(a):
    return jnp.stack([_fold(a[g] if a.ndim == 3 else a, d) for g, d in enumerate(ATTN_DILATIONS)])


def _unfold_groups(a):
    return jnp.stack([_unfold(a[g], d) for g, d in enumerate(ATTN_DILATIONS)])


def _local_step(xs, tgt, attn_norm, w_qkv, w_o, pool_norm, w_p, w_pg, pool_scale, ffn_norm, w_gu, w_d, final_norm):
    h0 = _rms_fwd("norm_attn", xs, attn_norm)
    hf = _fold_groups(h0)
    qkv = _qkv_proj("qkv_proj", hf, w_qkv)
    o_f, lse_f = _attn_fwd("attn_fwd", qkv)
    o_t, lse_t = _unfold_groups(o_f), _unfold_groups(lse_f)
    merged = _merge_fwd("merge_fwd", o_t, lse_t)
    x1 = _mm_rows("attn_out", merged, w_o, F32, 512, res=xs)
    h1 = _rms_fwd("norm_ffn0", x1, ffn_norm[0:1])
    gu0, act0 = _ffn_up("ffn0_up", h1, w_gu[0])
    x2 = _mm_rows("ffn0_down", act0, w_d[0], F32, 512, res=x1)
    h2 = _rms_fwd("norm_pool", x2, pool_norm)
    u = _mm_rows("pool_in", h2, w_p, F32, 512)
    y, z, x3 = _pool_fwd("pool_fwd", u, w_pg, pool_scale, x2)
    h3 = _rms_fwd("norm_ffn1", x3, ffn_norm[1:2])
    gu1, act1 = _ffn_up("ffn1_up", h3, w_gu[1])
    x4 = _mm_rows("ffn1_down", act1, w_d[1], F32, 512, res=x3)
    loss, dx4, d_final = _loss_bwd("loss_head", x4, final_norm, tgt)

    dgu1 = _ffn_down_bwd("ffn1_down_bwd", dx4, w_d[1], gu1)
    dw_d1 = _mm_tn("ffn1_down_dw", act1, dx4, FF_SHARD, 1024)
    dw_gu1 = _ffn_dw_up("ffn1_up_dw", h3, dgu1)
    dh3 = _ffn_dh("ffn1_up_dh", dgu1, w_gu[1])
    dx3, d_ffn1 = _rms_bwd("norm_ffn1_bwd", x3, ffn_norm[1:2], dh3, dx4)

    du, dw_pg, d_scale = _pool_bwd("pool_bwd", dx3, z, y, pool_scale, w_pg)
    dw_p = _mm_tn("pool_in_dw", h2, du, D_MODEL, 1024)
    dh2 = _mm_nt_rows("pool_in_dh", du, w_p, 512)
    dx2, d_pool = _rms_bwd("norm_pool_bwd", x2, pool_norm, dh2, dx3)

    dgu0 = _ffn_down_bwd("ffn0_down_bwd", dx2, w_d[0], gu0)
    dw_d0 = _mm_tn("ffn0_down_dw", act0, dx2, FF_SHARD, 1024)
    dw_gu0 = _ffn_dw_up("ffn0_up_dw", h1, dgu0)
    dh1 = _ffn_dh("ffn0_up_dh", dgu0, w_gu[0])
    dx1, d_ffn0 = _rms_bwd("norm_ffn0_bwd", x1, ffn_norm[0:1], dh1, dx2)

    d_merged = _mm_nt_rows("attn_out_dh", dx1, w_o, 512)
    dw_o = _mm_tn("attn_out_dw", merged, dx1, D_MODEL, 1024)
    do_t, dl_t = _merge_bwd("merge_bwd", d_merged, o_t, lse_t)
    dqkv = _attn_bwd("attn_bwd", qkv, _fold_groups(do_t), lse_f, _fold_groups(dl_t))
    dw_qkv = _qkv_dw("qkv_dw", hf, dqkv)
    dhf = _qkv_dh("qkv_dh", dqkv, w_qkv)
    dh0 = dhf[0] + _unfold(dhf[1], ATTN_DILATIONS[1]) + _unfold(dhf[2], ATTN_DILATIONS[2])
    grad_x, d_attn = _rms_bwd("norm_attn_bwd", xs, attn_norm, dh0, dx1)

    big = dict(qkv=dw_qkv, o=dw_o, p=dw_p, pg=dw_pg, gu=(dw_gu0, dw_gu1), d=(dw_d0, dw_d1))
    small = dict(attn=d_attn, ffn0=d_ffn0, ffn1=d_ffn1, final=d_final, pool=d_pool, scale=d_scale)
    return loss, grad_x, big, small


def kernel(x, attn_norm, w_qkv, w_attn_out, pool_norm, w_pool_in, w_pool_group, pool_scale, ffn_norm, w_ffn_gate_up, w_ffn_down, final_norm, loss_target, m_attn_norm, m_w_qkv, m_w_attn_out, m_pool_norm, m_w_pool_in, m_w_pool_group, m_pool_scale, m_ffn_norm, m_w_ffn_gate_up, m_w_ffn_down, m_final_norm, v_attn_norm, v_w_qkv, v_w_attn_out, v_pool_norm, v_w_pool_in, v_w_pool_group, v_pool_scale, v_ffn_norm, v_w_ffn_gate_up, v_w_ffn_down, v_final_norm):
    D = D_MODEL
    chip = 2 * lax.axis_index("x") + lax.axis_index("y")
    core = lax.axis_index("c")
    pg_rows = w_pool_group.shape[2]

    chip_arr = chip.astype(jnp.int32).reshape(1)
    core_arr = core.astype(jnp.int32).reshape(1)
    names = ["qkv", "o", "p", "pg", "gu0", "gu1", "d0", "d1"]
    shards = [w_qkv[0], w_attn_out[0], w_pool_in[0], w_pool_group[0].reshape(4 * pg_rows, POOL_DIM),
              w_ffn_gate_up[0], w_ffn_gate_up[1], w_ffn_down[0], w_ffn_down[1]]
    bufs = [_cast_into_slot("cast_" + nm, s, chip_arr, BF16) for nm, s in zip(names, shards)]
    bufs += [_cast_into_slot("place_pool_norm", pool_norm, chip_arr, F32),
             _cast_into_slot("place_pool_scale", pool_scale, chip_arr, F32)]
    gathered = _gather_weights("gather_weights", bufs, [True] * 8 + [False] * 2)
    g_qkv, g_o, g_p, g_pg, g_gu0, g_gu1, g_d0, g_d1, g_pn, g_ps = gathered
    w_pg_full = g_pg.reshape(N_CHIPS, 4, pg_rows, POOL_DIM).transpose(1, 0, 2, 3).reshape(4, POOL_DIM, POOL_DIM)
    pool_norm_full = g_pn.reshape(1, D)
    pool_scale_full = g_ps.reshape(1, D)

    loss_part, grad_x, big, small = _local_step(
        x[0], loss_target[0], attn_norm, g_qkv, g_o.reshape(D, D), pool_norm_full, g_p.reshape(D, D), w_pg_full,
        pool_scale_full, ffn_norm, (g_gu0, g_gu1), (g_d0.reshape(D_FF, D), g_d1.reshape(D_FF, D)),
        final_norm.reshape(1, D))

    dw_pg = big["pg"].reshape(4, N_CHIPS, pg_rows, POOL_DIM).transpose(1, 0, 2, 3)
    dw_pg = dw_pg.reshape(N_CHIPS, 4 * pg_rows, POOL_DIM).astype(BF16)
    parts = [big["qkv"], big["o"].reshape(N_CHIPS, D // N_CHIPS, D), big["p"].reshape(N_CHIPS, D // N_CHIPS, D),
             dw_pg, big["gu"][0], big["gu"][1], big["d"][0].reshape(N_CHIPS, D_FF // N_CHIPS, D),
             big["d"][1].reshape(N_CHIPS, D_FF // N_CHIPS, D)]
    theirs = _swap_other_half("grads_to_sibling", parts)
    chip_sums = [_add_my_half("chip_sum_" + nm, p, q, core_arr) for nm, p, q in zip(names, parts, theirs)]
    landed = _scatter_to_owners("grads_to_owners", chip_sums)
    owner = lambda t, **kw: _owner_sum("owner_sum_" + names[t], chip_sums[t], landed[t], chip_arr, core_arr, **kw)
    blocks = [owner(0), owner(1), owner(2), owner(3),
              owner(5, out=owner(4, layer=0), layer=1), owner(7, out=owner(6, layer=0), layer=1)]
    g_w_qkv, g_w_o, g_w_p, g_w_pg, g_w_gu, g_w_d = _share_halves("halves_to_sibling", blocks, [0, 0, 0, 0, 2, 2])

    rows = jnp.concatenate([small["attn"], small["ffn0"], small["ffn1"], small["final"], small["pool"],
                            small["scale"], jnp.zeros((2, D), F32)], axis=0)
    all_rows = _gather_small("gather_gain_grads", rows)
    tot = _sum_leading("sum_gain_grads", all_rows, F32)
    g_attn_norm = tot[0:1]
    g_ffn_norm = tot[1:3]
    g_final_norm = tot[3]
    g_pool_norm = lax.dynamic_slice(tot, (4, chip * POOL_DIM), (1, POOL_DIM))
    g_pool_scale = lax.dynamic_slice(tot, (5, chip * POOL_DIM), (1, POOL_DIM))

    loss = lax.psum(loss_part[0, 0], ("x", "y", "c"))

    def update(name, w, g, m, v):
        shape = w.shape
        cols = shape[-1]
        as2d = lambda a: a.reshape(-1, cols)
        d, m2, v2 = _adamw("adamw_" + name, as2d(w), as2d(g), as2d(m), as2d(v))
        return g.reshape(shape), d.reshape(shape), m2.reshape(shape), v2.reshape(shape)

    results = [
        update("attn_norm", attn_norm, g_attn_norm, m_attn_norm, v_attn_norm),
        update("w_qkv", w_qkv, g_w_qkv, m_w_qkv, v_w_qkv),
        update("w_attn_out", w_attn_out, g_w_o, m_w_attn_out, v_w_attn_out),
        update("pool_norm", pool_norm, g_pool_norm, m_pool_norm, v_pool_norm),
        update("w_pool_in", w_pool_in, g_w_p, m_w_pool_in, v_w_pool_in),
        update("w_pool_group", w_pool_group, g_w_pg, m_w_pool_group, v_w_pool_group),
        update("pool_scale", pool_scale, g_pool_scale, m_pool_scale, v_pool_scale),
        update("ffn_norm", ffn_norm, g_ffn_norm, m_ffn_norm, v_ffn_norm),
        update("w_ffn_gate_up", w_ffn_gate_up, g_w_gu, m_w_ffn_gate_up, v_w_ffn_gate_up),
        update("w_ffn_down", w_ffn_down, g_w_d, m_w_ffn_down, v_w_ffn_down),
        update("final_norm", final_norm, g_final_norm, m_final_norm, v_final_norm),
    ]
    grads = [r[0] for r in results]
    deltas = [r[1] for r in results]
    new_m = [r[2] for r in results]
    new_v = [r[3] for r in results]
    return (loss, grad_x[None], *grads, *deltas, *new_m, *new_v)
```

```python
import functools

import jax
import jax.numpy as jnp
from jax import lax
from jax.experimental import pallas as pl
from jax.experimental.pallas import tpu as pltpu

F32 = jnp.float32
BF16 = jnp.bfloat16
MESH = pl.DeviceIdType.MESH

D_MODEL = 1024
N_HEADS = 8
HEAD_DIM = 128
Q_BLOCK = 128
ATTN_DILATIONS = (1, 4, 16)
N_GROUPS = 3
D_FF = 2816
N_CHIPS = 4
FF_SHARD = 2 * D_FF // N_CHIPS
QKV_SHARD = 3 * 3 * D_MODEL // N_CHIPS
QKV_TILE = 768
POOL_WINDOWS = (2, 4, 8, 16)
POOL_DIM = 256
POOL_HALO = 16
RMS_EPS = 1e-6
NEG = -1e30
ADAM_LR = 0.001
ADAM_B1 = 0.9
ADAM_B2 = 0.999
ADAM_EPS = 1e-08
ADAM_WD = 0.01
ADAM_STEP = 10
VMEM_LIMIT = 56 * 1024 * 1024

_DN = {
    "nn": (((1,), (0,)), ((), ())),
    "nt": (((1,), (1,)), ((), ())),
    "tn": (((0,), (0,)), ((), ())),
}


class _Rider:
    def __init__(self, operands, out_shapes, aliases, n_copies, copies):
        self.operands = list(operands)
        self.out_shapes = list(out_shapes)
        self.aliases = dict(aliases)
        self.n_copies = n_copies
        self.copies = copies
        self.results = None


def _riders(*riders):
    operands, out_shapes, aliases, offsets = [], [], {}, []
    n = 0
    for r in riders:
        offsets.append((len(operands), len(out_shapes), n))
        aliases.update({len(operands) + i: len(out_shapes) + o for i, o in r.aliases.items()})
        operands += r.operands
        out_shapes += r.out_shapes
        n += r.n_copies

    def copies(ins, outs, send, recv, base=0):
        out = []
        for r, (i0, o0, s0) in zip(riders, offsets):
            out += r.copies(ins[i0:i0 + len(r.operands)], outs[o0:o0 + len(r.out_shapes)], send, recv, base + s0)
        return out

    both = _Rider(operands, out_shapes, aliases, n, copies)
    both.parts = (riders, offsets)
    return both


_pending_rider = []


def _ride(rider, fn, *args, **kw):
    _pending_rider.append(rider)
    out = fn(*args, **kw)
    assert not _pending_rider
    if hasattr(rider, "parts"):
        for r, (_, o0, _) in zip(*rider.parts):
            r.results = rider.results[o0:o0 + len(r.out_shapes)]
    return out


def _pcall(body, **kw):
    if not _pending_rider:
        return pl.pallas_call(body, **kw)
    rider = _pending_rider.pop()
    grid = kw.get("grid", ())
    in_specs = list(kw.get("in_specs", []))
    single = not isinstance(kw["out_shape"], (list, tuple))
    out_shape = [kw["out_shape"]] if single else list(kw["out_shape"])
    out_specs = [kw["out_specs"]] if single else list(kw["out_specs"])
    scratch = list(kw.get("scratch_shapes", []))
    n_in, n_out, n_scr = len(in_specs), len(out_shape), len(scratch)
    r_in, r_out = len(rider.operands), len(rider.out_shapes)

    def wrapped(*refs):
        ins, rins = refs[:n_in], refs[n_in:n_in + r_in]
        outs = refs[n_in + r_in:n_in + r_in + n_out]
        routs = refs[n_in + r_in + n_out:n_in + r_in + n_out + r_out]
        scr = refs[n_in + r_in + n_out + r_out:n_in + r_in + n_out + r_out + n_scr]
        send, recv = refs[-2:]
        ids = [pl.program_id(a) for a in range(len(grid))]
        first, last = True, True
        for a, pid in enumerate(ids):
            first = jnp.logical_and(first, pid == 0)
            last = jnp.logical_and(last, pid == grid[a] - 1)
        if grid:
            @pl.when(first)
            def _():
                for cp in rider.copies(rins, routs, send, recv):
                    cp.start()
        else:
            for cp in rider.copies(rins, routs, send, recv):
                cp.start()
        body(*ins, *outs, *scr)
        if grid:
            @pl.when(last)
            def _():
                for cp in rider.copies(rins, routs, send, recv):
                    cp.wait()
        else:
            for cp in rider.copies(rins, routs, send, recv):
                cp.wait()

    any_spec = pl.BlockSpec(memory_space=pl.ANY)
    kw2 = dict(kw)
    kw2.update(
        in_specs=in_specs + [any_spec] * r_in, out_specs=out_specs + [any_spec] * r_out,
        out_shape=out_shape + rider.out_shapes,
        scratch_shapes=scratch + [pltpu.SemaphoreType.DMA((rider.n_copies,)),
                                  pltpu.SemaphoreType.DMA((rider.n_copies,))],
        input_output_aliases={n_in + i: n_out + o for i, o in rider.aliases.items()})
    if grid:
        kw2["compiler_params"] = _params(("arbitrary",) * len(grid))
    call = pl.pallas_call(wrapped, **kw2)

    def run(*operands):
        res = call(*operands, *rider.operands)
        rider.results = list(res[n_out:])
        return res[0] if single else list(res[:n_out])

    return run


def _comm_only(name, rider):
    def make():
        return _pcall(lambda: None, name=name, in_specs=[], out_specs=[], out_shape=[])()
    _ride(rider, make)
    return rider.results


def _params(sem):
    return pltpu.CompilerParams(dimension_semantics=sem, vmem_limit_bytes=VMEM_LIMIT)


def _dot(a, b, mode):
    return lax.dot_general(a, b, _DN[mode], preferred_element_type=F32)


def _mm(name, mode, grid, a, a_spec, b, b_spec, out_shape, o_spec, acc_shape, res=None, res_spec=None):
    nk = grid[-1]
    has_res = res is not None

    def body(*refs):
        a_ref, b_ref = refs[0], refs[1]
        res_ref = refs[2] if has_res else None
        o_ref = refs[2 + has_res]
        part = _dot(a_ref[...].astype(BF16), b_ref[...].astype(BF16), mode)
        if nk == 1:
            if has_res:
                part = part + res_ref[...]
            o_ref[...] = part.astype(o_ref.dtype)
            return
        acc_ref = refs[3 + has_res]
        k = pl.program_id(len(grid) - 1)

        @pl.when(k == 0)
        def _():
            acc_ref[...] = part

        @pl.when(k > 0)
        def _():
            acc_ref[...] += part

        @pl.when(k == nk - 1)
        def _():
            r = acc_ref[...]
            if has_res:
                r = r + res_ref[...]
            o_ref[...] = r.astype(o_ref.dtype)

    in_specs = [a_spec, b_spec] + ([res_spec] if has_res else [])
    operands = [a, b] + ([res] if has_res else [])
    scratch = [] if nk == 1 else [pltpu.VMEM(acc_shape, F32)]
    sem = ("parallel",) * (len(grid) - 1) + ("arbitrary",)
    return _pcall(body, name=name, grid=grid, in_specs=in_specs, out_specs=o_spec, out_shape=out_shape,
                  scratch_shapes=scratch, compiler_params=_params(sem))(*operands)


def _mm_rows(name, a, b, out_dtype, tm, res=None):
    M, K = a.shape
    N = b.shape[1]
    return _mm(name, "nn", (M // tm, 1), a, pl.BlockSpec((tm, K), lambda i, k: (i, 0)),
               b, pl.BlockSpec((K, N), lambda i, k: (0, 0)),
               jax.ShapeDtypeStruct((M, N), out_dtype), pl.BlockSpec((tm, N), lambda i, k: (i, 0)), None,
               res=res, res_spec=pl.BlockSpec((tm, N), lambda i, k: (i, 0)))


def _mm_nt_rows(name, a, b, tm):
    M, N = a.shape
    K = b.shape[0]
    return _mm(name, "nt", (M // tm, 1), a, pl.BlockSpec((tm, N), lambda i, k: (i, 0)),
               b, pl.BlockSpec((K, N), lambda i, k: (0, 0)),
               jax.ShapeDtypeStruct((M, K), F32), pl.BlockSpec((tm, K), lambda i, k: (i, 0)), None)


def _mm_tn(name, a, b, tm, tk):
    T, M = a.shape
    N = b.shape[1]
    return _mm(name, "tn", (M // tm, T // tk), a, pl.BlockSpec((tk, tm), lambda i, k: (k, i)),
               b, pl.BlockSpec((tk, N), lambda i, k: (k, 0)),
               jax.ShapeDtypeStruct((M, N), BF16), pl.BlockSpec((tm, N), lambda i, k: (i, 0)), (tm, N))


def _rms_fwd(name, x, g, tr=512):
    S, D = x.shape

    def body(x_ref, g_ref, o_ref):
        xf = x_ref[...]
        r = lax.rsqrt(jnp.mean(xf * xf, axis=-1, keepdims=True) + RMS_EPS)
        o_ref[...] = ((xf * r) * g_ref[...]).astype(o_ref.dtype)

    return _pcall(body, name=name, grid=(S // tr,),
                  in_specs=[pl.BlockSpec((tr, D), lambda i: (i, 0)), pl.BlockSpec((1, D), lambda i: (0, 0))],
                  out_specs=pl.BlockSpec((tr, D), lambda i: (i, 0)),
                  out_shape=jax.ShapeDtypeStruct((S, D), BF16), compiler_params=_params(("parallel",)))(x, g)


def _rms_bwd(name, x, g, dh, dres, tr=512):
    S, D = x.shape

    def body(x_ref, g_ref, dh_ref, dres_ref, dx_ref, dg_ref):
        i = pl.program_id(0)
        xf = x_ref[...]
        r = lax.rsqrt(jnp.mean(xf * xf, axis=-1, keepdims=True) + RMS_EPS)
        xh = xf * r
        dh_v = dh_ref[...]
        dg = jnp.sum(dh_v * xh, axis=0, keepdims=True)
        t = dh_v * g_ref[...]
        dx_ref[...] = dres_ref[...] + r * (t - xh * jnp.mean(t * xh, axis=-1, keepdims=True))

        @pl.when(i == 0)
        def _():
            dg_ref[...] = dg

        @pl.when(i > 0)
        def _():
            dg_ref[...] += dg

    row = pl.BlockSpec((tr, D), lambda i: (i, 0))
    vec = pl.BlockSpec((1, D), lambda i: (0, 0))
    return _pcall(body, name=name, grid=(S // tr,), in_specs=[row, vec, row, row], out_specs=[row, vec],
                  out_shape=[jax.ShapeDtypeStruct((S, D), F32), jax.ShapeDtypeStruct((1, D), F32)],
                  compiler_params=_params(("arbitrary",)))(x, g, dh, dres)


def _loss_bwd(name, x, g, tgt, tr=512):
    S, D = x.shape

    def body(x_ref, g_ref, t_ref, loss_ref, dx_ref, dg_ref):
        i = pl.program_id(0)
        xf = x_ref[...]
        r = lax.rsqrt(jnp.mean(xf * xf, axis=-1, keepdims=True) + RMS_EPS)
        xh = xf * r
        gv = g_ref[...]
        e = xh * gv - t_ref[...]
        lp = 0.5 * jnp.sum(jnp.mean(e * e, axis=-1, keepdims=True), axis=0, keepdims=True)
        dy = e * (1.0 / D)
        dg = jnp.sum(dy * xh, axis=0, keepdims=True)
        t = dy * gv
        dx_ref[...] = r * (t - xh * jnp.mean(t * xh, axis=-1, keepdims=True))

        @pl.when(i == 0)
        def _():
            dg_ref[...] = dg
            loss_ref[...] = lp

        @pl.when(i > 0)
        def _():
            dg_ref[...] += dg
            loss_ref[...] += lp

    row = pl.BlockSpec((tr, D), lambda i: (i, 0))
    vec = pl.BlockSpec((1, D), lambda i: (0, 0))
    one = pl.BlockSpec((1, 1), lambda i: (0, 0))
    return _pcall(body, name=name, grid=(S // tr,), in_specs=[row, vec, row], out_specs=[one, row, vec],
                  out_shape=[jax.ShapeDtypeStruct((1, 1), F32), jax.ShapeDtypeStruct((S, D), F32),
                             jax.ShapeDtypeStruct((1, D), F32)],
                  compiler_params=_params(("arbitrary",)))(x, g, tgt)


def _ffn_up(name, h, w_gu, tm=512):
    S, D = h.shape
    W = FF_SHARD

    def body(h_ref, wg_ref, wu_ref, gu_ref, act_ref):
        hv = h_ref[...]
        gate = _dot(hv, wg_ref[...], "nn")
        up = _dot(hv, wu_ref[...], "nn")
        gu_ref[0] = gate
        gu_ref[1] = up
        sig = 1.0 / (1.0 + jnp.exp(-gate))
        act_ref[...] = ((gate * sig) * up).astype(act_ref.dtype)

    return _pcall(
        body, name=name, grid=(2, S // tm),
        in_specs=[pl.BlockSpec((tm, D), lambda j, i: (i, 0)),
                  pl.BlockSpec((None, D, W), lambda j, i: (j, 0, 0)),
                  pl.BlockSpec((None, D, W), lambda j, i: (j + 2, 0, 0))],
        out_specs=[pl.BlockSpec((2, tm, W), lambda j, i: (0, i, j)), pl.BlockSpec((tm, W), lambda j, i: (i, j))],
        out_shape=[jax.ShapeDtypeStruct((2, S, D_FF), F32), jax.ShapeDtypeStruct((S, D_FF), BF16)],
        compiler_params=_params(("parallel", "parallel")))(h, w_gu, w_gu)


def _ffn_down_bwd(name, dx, w_down, gu, tm=512):
    S, D = dx.shape
    W = FF_SHARD

    def body(dx_ref, w_ref, gu_ref, dgu_ref):
        dact = _dot(dx_ref[...].astype(BF16), w_ref[...], "nt")
        gate = gu_ref[0]
        up = gu_ref[1]
        sig = 1.0 / (1.0 + jnp.exp(-gate))
        silu = gate * sig
        dgu_ref[0] = (dact * up * (sig * (1.0 + gate * (1.0 - sig)))).astype(dgu_ref.dtype)
        dgu_ref[1] = (dact * silu).astype(dgu_ref.dtype)

    blk = pl.BlockSpec((2, tm, W), lambda j, i: (0, i, j))
    return _pcall(
        body, name=name, grid=(2, S // tm),
        in_specs=[pl.BlockSpec((tm, D), lambda j, i: (i, 0)), pl.BlockSpec((W, D), lambda j, i: (j, 0)), blk],
        out_specs=blk, out_shape=jax.ShapeDtypeStruct((2, S, D_FF), BF16),
        compiler_params=_params(("parallel", "parallel")))(dx, w_down, gu)


def _ffn_dw_up(name, h, dgu, tk=1024):
    S, D = h.shape
    W = FF_SHARD
    return _mm(name, "tn", (N_CHIPS, S // tk), h, pl.BlockSpec((tk, D), lambda s, k: (k, 0)),
               dgu, pl.BlockSpec((None, tk, W), lambda s, k: (s // 2, k, s % 2)),
               jax.ShapeDtypeStruct((N_CHIPS, D, W), BF16), pl.BlockSpec((None, D, W), lambda s, k: (s, 0, 0)),
               (D, W))


def _ffn_dh(name, dgu, w_gu, tm=1024):
    S = dgu.shape[1]
    W = FF_SHARD
    return _mm(name, "nt", (S // tm, N_CHIPS), dgu, pl.BlockSpec((None, tm, W), lambda i, k: (k // 2, i, k % 2)),
               w_gu, pl.BlockSpec((None, D_MODEL, W), lambda i, k: (k, 0, 0)),
               jax.ShapeDtypeStruct((S, D_MODEL), F32), pl.BlockSpec((tm, D_MODEL), lambda i, k: (i, 0)),
               (tm, D_MODEL))


def _qkv_proj(name, hf, w_qkv, tm=1024):
    S = hf.shape[1]
    per_chip = QKV_SHARD // QKV_TILE
    per_group = 3 * D_MODEL // QKV_TILE
    return _mm(name, "nn", (N_GROUPS, S // tm, per_group, 1),
               hf, pl.BlockSpec((None, tm, D_MODEL), lambda g, i, j, k: (g, i, 0)),
               w_qkv, pl.BlockSpec((None, D_MODEL, QKV_TILE),
                                   lambda g, i, j, k: ((per_group * g + j) // per_chip, 0, (per_group * g + j) % per_chip)),
               jax.ShapeDtypeStruct((N_GROUPS, S, 3 * D_MODEL), BF16),
               pl.BlockSpec((None, tm, QKV_TILE), lambda g, i, j, k: (g, i, j)), None)


def _qkv_dw(name, hf, dqkv, tk=1024):
    S = hf.shape[1]
    per_chip = QKV_SHARD // QKV_TILE
    per_group = 3 * D_MODEL // QKV_TILE
    return _mm(name, "tn", (N_GROUPS, per_group, S // tk),
               hf, pl.BlockSpec((None, tk, D_MODEL), lambda g, j, k: (g, k, 0)),
               dqkv, pl.BlockSpec((None, tk, QKV_TILE), lambda g, j, k: (g, k, j)),
               jax.ShapeDtypeStruct((N_CHIPS, D_MODEL, QKV_SHARD), BF16),
               pl.BlockSpec((None, D_MODEL, QKV_TILE),
                            lambda g, j, k: ((per_group * g + j) // per_chip, 0, (per_group * g + j) % per_chip)),
               (D_MODEL, QKV_TILE))


def _qkv_dh(name, dqkv, w_qkv, tm=1024):
    S = dqkv.shape[1]
    per_chip = QKV_SHARD // QKV_TILE
    per_group = 3 * D_MODEL // QKV_TILE
    return _mm(name, "nt", (N_GROUPS, S // tm, per_group),
               dqkv, pl.BlockSpec((None, tm, QKV_TILE), lambda g, i, k: (g, i, k)),
               w_qkv, pl.BlockSpec((None, D_MODEL, QKV_TILE),
                                   lambda g, i, k: ((per_group * g + k) // per_chip, 0, (per_group * g + k) % per_chip)),
               jax.ShapeDtypeStruct((N_GROUPS, S, D_MODEL), F32),
               pl.BlockSpec((None, tm, D_MODEL), lambda g, i, k: (g, i, 0)), (tm, D_MODEL))


def _alibi_coef(g, h):
    vals = [-(2.0 ** (-8.0 * (gg * N_HEADS + h + 1) / (N_GROUPS * N_HEADS))) * ATTN_DILATIONS[gg]
            for gg in range(N_GROUPS)]
    return jnp.where(g == 0, vals[0], jnp.where(g == 1, vals[1], vals[2])).astype(F32)


def _blocks_per_segment(g, S):
    return jnp.right_shift(S // Q_BLOCK, 2 * g)


def _band_bias(pen):
    row = lax.broadcasted_iota(jnp.int32, (Q_BLOCK, 2 * Q_BLOCK), 0)
    col = lax.broadcasted_iota(jnp.int32, (Q_BLOCK, 2 * Q_BLOCK), 1)
    dist = Q_BLOCK + row - col
    valid = jnp.logical_and(dist >= 0, dist <= Q_BLOCK)
    base = jnp.where(valid, jnp.where(col < Q_BLOCK, pen, 0.0), NEG).astype(F32)
    return base, dist.astype(F32)


def _skewed(n_units, stages):
    state = {}
    for step in range(n_units + len(stages) - 1):
        for s, stage in enumerate(stages):
            u = step - s
            if 0 <= u < n_units:
                state[u] = stage(u, state.get(u))
                if s == len(stages) - 1:
                    del state[u]


def _attn_fwd(name, qkv, tq=512):
    S = qkv.shape[1]
    nqb = tq // Q_BLOCK
    scale = HEAD_DIM ** -0.5

    def body(q_ref, k_ref, kp_ref, v_ref, vp_ref, o_ref, lse_ref, kf_ref, vf_ref):
        g = pl.program_id(0)
        i = pl.program_id(1)
        nb = _blocks_per_segment(g, S)
        kf_ref[:Q_BLOCK] = kp_ref[...]
        kf_ref[Q_BLOCK:] = k_ref[...]
        vf_ref[:Q_BLOCK] = vp_ref[...]
        vf_ref[Q_BLOCK:] = v_ref[...]
        coefs = [_alibi_coef(g, h) for h in range(N_HEADS)]
        units = [(b, h) for b in range(nqb) for h in range(N_HEADS)]
        bias = {}

        def scores(u, _):
            b, h = units[u]
            if b not in bias:
                pen = jnp.where((i * nqb + b) % nb != 0, 0.0, NEG).astype(F32)
                bias.clear()
                bias[b] = _band_bias(pen)
            base, dist = bias[b]
            cols = slice(h * HEAD_DIM, (h + 1) * HEAD_DIM)
            q = q_ref[b * Q_BLOCK:(b + 1) * Q_BLOCK, cols]
            kk = kf_ref[b * Q_BLOCK:(b + 2) * Q_BLOCK, cols]
            return _dot(q, kk, "nt") * scale + (coefs[h] * dist + base)

        def softmax(u, s):
            m = jnp.max(s, axis=-1, keepdims=True)
            p = jnp.exp(s - m)
            den = jnp.sum(p, axis=-1, keepdims=True)
            return (p * (1.0 / den)).astype(BF16), m + jnp.log(den)

        def output(u, st):
            b, h = units[u]
            pn, lse = st
            cols = slice(h * HEAD_DIM, (h + 1) * HEAD_DIM)
            rows = slice(b * Q_BLOCK, (b + 1) * Q_BLOCK)
            o_ref[rows, cols] = _dot(pn, vf_ref[b * Q_BLOCK:(b + 2) * Q_BLOCK, cols], "nn")
            lse_ref[rows, h:h + 1] = lse

        _skewed(len(units), [scores, softmax, output])

    main = lambda c: pl.BlockSpec((None, tq, D_MODEL), lambda g, i: (g, i, c))
    prev = lambda c: pl.BlockSpec((None, Q_BLOCK, D_MODEL), lambda g, i: (g, jnp.maximum(i * nqb - 1, 0), c))
    return _pcall(
        body, name=name, grid=(N_GROUPS, S // tq),
        in_specs=[main(0), main(1), prev(1), main(2), prev(2)],
        out_specs=[pl.BlockSpec((None, tq, D_MODEL), lambda g, i: (g, i, 0)),
                   pl.BlockSpec((None, tq, N_HEADS), lambda g, i: (g, i, 0))],
        out_shape=[jax.ShapeDtypeStruct((N_GROUPS, S, D_MODEL), F32),
                   jax.ShapeDtypeStruct((N_GROUPS, S, N_HEADS), F32)],
        scratch_shapes=[pltpu.VMEM((tq + Q_BLOCK, D_MODEL), BF16), pltpu.VMEM((tq + Q_BLOCK, D_MODEL), BF16)],
        compiler_params=_params(("parallel", "parallel")))(qkv, qkv, qkv, qkv, qkv)


def _attn_bwd(name, qkv, do, lse, dl, tq=512):
    S = qkv.shape[1]
    nqb = tq // Q_BLOCK
    n_blocks = S // Q_BLOCK
    scale = HEAD_DIM ** -0.5
    KEYS = 2 * Q_BLOCK

    def body(q_ref, k_ref, v_ref, kp_ref, vp_ref, qn_ref, do_ref, don_ref, lse_ref, lsen_ref, dl_ref, dln_ref,
             out_ref, kf_ref, vf_ref, dk_ref, dv_ref):
        g = pl.program_id(0)
        i = pl.program_id(1)
        nb = _blocks_per_segment(g, S)
        kf_ref[:Q_BLOCK] = kp_ref[...]
        kf_ref[Q_BLOCK:] = k_ref[...]
        vf_ref[:Q_BLOCK] = vp_ref[...]
        vf_ref[Q_BLOCK:] = v_ref[...]
        coefs = [_alibi_coef(g, h) for h in range(N_HEADS)]
        units = [(h, b) for h in range(N_HEADS) for b in range(nqb + 1)]
        bias = {}

        def band(b):
            if b not in bias:
                blk = i * nqb + b
                ok = blk % nb != 0
                if b == nqb:
                    ok = jnp.logical_and(ok, blk < n_blocks)
                bias[b] = _band_bias(jnp.where(ok, 0.0, NEG).astype(F32))
            return bias[b]

        def operands(h, b):
            cols = slice(h * HEAD_DIM, (h + 1) * HEAD_DIM)
            if b == nqb:
                keys = slice(tq, tq + Q_BLOCK)
                return (qn_ref[:, cols], don_ref[:, cols], lsen_ref[:, h:h + 1], dln_ref[:, h:h + 1],
                        kf_ref[keys, cols], vf_ref[keys, cols])
            rows = slice(b * Q_BLOCK, (b + 1) * Q_BLOCK)
            keys = slice(b * Q_BLOCK, b * Q_BLOCK + KEYS)
            return (q_ref[rows, cols], do_ref[rows, cols], lse_ref[rows, h:h + 1], dl_ref[rows, h:h + 1],
                    kf_ref[keys, cols], vf_ref[keys, cols])

        def probs(u, _):
            h, b = units[u]
            q, do_b, lse_b, dl_b, kk, vv = operands(h, b)
            base, dist = band(b)
            if b == nqb:
                base, dist = base[:, :Q_BLOCK], dist[:, :Q_BLOCK]
            p = jnp.exp(_dot(q, kk, "nt") * scale + (coefs[h] * dist + base) - lse_b)
            return p, _dot(do_b, vv, "nt")

        def dscores(u, st):
            h, b = units[u]
            p, dp = st
            dl_b = operands(h, b)[3]
            return ((p * (dp - dl_b)) * scale).astype(BF16), p.astype(BF16)

        def grads(u, st):
            h, b = units[u]
            ds, pb = st
            q, do_b, _, _, kk, _ = operands(h, b)
            cols = slice(h * HEAD_DIM, (h + 1) * HEAD_DIM)
            if b < nqb:
                out_ref[b * Q_BLOCK:(b + 1) * Q_BLOCK, cols] = _dot(ds, kk, "nn").astype(out_ref.dtype)
            if b == 0:
                dk_ref[:Q_BLOCK] = _dot(ds[:, Q_BLOCK:], q, "tn")
                dv_ref[:Q_BLOCK] = _dot(pb[:, Q_BLOCK:], do_b, "tn")
                return None
            dk = _dot(ds, q, "tn")
            dv = _dot(pb, do_b, "tn")
            before = slice((b - 1) * Q_BLOCK, b * Q_BLOCK)
            dk_ref[before] += dk[:Q_BLOCK]
            dv_ref[before] += dv[:Q_BLOCK]
            if b < nqb:
                own = slice(b * Q_BLOCK, (b + 1) * Q_BLOCK)
                dk_ref[own] = dk[Q_BLOCK:]
                dv_ref[own] = dv[Q_BLOCK:]
            else:
                out_ref[:, D_MODEL + h * HEAD_DIM:D_MODEL + (h + 1) * HEAD_DIM] = dk_ref[...].astype(out_ref.dtype)
                out_ref[:, 2 * D_MODEL + h * HEAD_DIM:2 * D_MODEL + (h + 1) * HEAD_DIM] = (
                    dv_ref[...].astype(out_ref.dtype))
            return None

        _skewed(len(units), [probs, dscores, grads])

    nxt_blk = lambda i: jnp.minimum((i + 1) * nqb, n_blocks - 1)
    main = lambda c: pl.BlockSpec((None, tq, D_MODEL), lambda g, i: (g, i, c))
    prev = lambda c: pl.BlockSpec((None, Q_BLOCK, D_MODEL), lambda g, i: (g, jnp.maximum(i * nqb - 1, 0), c))
    row = pl.BlockSpec((None, tq, D_MODEL), lambda g, i: (g, i, 0))
    row_n = pl.BlockSpec((None, Q_BLOCK, D_MODEL), lambda g, i: (g, nxt_blk(i), 0))
    col = pl.BlockSpec((None, tq, N_HEADS), lambda g, i: (g, i, 0))
    col_n = pl.BlockSpec((None, Q_BLOCK, N_HEADS), lambda g, i: (g, nxt_blk(i), 0))
    return _pcall(
        body, name=name, grid=(N_GROUPS, S // tq),
        in_specs=[main(0), main(1), main(2), prev(1), prev(2), row_n, row, row_n, col, col_n, col, col_n],
        out_specs=pl.BlockSpec((None, tq, 3 * D_MODEL), lambda g, i: (g, i, 0)),
        out_shape=jax.ShapeDtypeStruct((N_GROUPS, S, 3 * D_MODEL), BF16),
        scratch_shapes=[pltpu.VMEM((tq + Q_BLOCK, D_MODEL), BF16), pltpu.VMEM((tq + Q_BLOCK, D_MODEL), BF16),
                        pltpu.VMEM((tq, HEAD_DIM), F32), pltpu.VMEM((tq, HEAD_DIM), F32)],
        compiler_params=_params(("parallel", "parallel")))(qkv, qkv, qkv, qkv, qkv, qkv, do, do, lse, lse, dl, dl)


def _group_weights(lse_ref):
    l0, l1, l2 = lse_ref[0], lse_ref[1], lse_ref[2]
    m = jnp.maximum(jnp.maximum(l0, l1), l2)
    e = [jnp.exp(l0 - m), jnp.exp(l1 - m), jnp.exp(l2 - m)]
    den = e[0] + e[1] + e[2]
    return [ei / den for ei in e]


def _merge_fwd(name, o, lse, tr=256):
    S = o.shape[1]

    def body(o_ref, lse_ref, out_ref):
        w = _group_weights(lse_ref)
        for h in range(N_HEADS):
            cols = slice(h * HEAD_DIM, (h + 1) * HEAD_DIM)
            acc = w[0][:, h:h + 1] * o_ref[0, :, cols]
            for gg in range(1, N_GROUPS):
                acc = acc + w[gg][:, h:h + 1] * o_ref[gg, :, cols]
            out_ref[:, cols] = acc.astype(out_ref.dtype)

    return _pcall(body, name=name, grid=(S // tr,),
                  in_specs=[pl.BlockSpec((N_GROUPS, tr, D_MODEL), lambda i: (0, i, 0)),
                            pl.BlockSpec((N_GROUPS, tr, N_HEADS), lambda i: (0, i, 0))],
                  out_specs=pl.BlockSpec((tr, D_MODEL), lambda i: (i, 0)),
                  out_shape=jax.ShapeDtypeStruct((S, D_MODEL), BF16), compiler_params=_params(("parallel",)))(o, lse)


def _merge_bwd(name, d_out, o, lse, tr=256):
    S = o.shape[1]

    def body(d_ref, o_ref, lse_ref, do_ref, dl_ref):
        w = _group_weights(lse_ref)
        for h in range(N_HEADS):
            cols = slice(h * HEAD_DIM, (h + 1) * HEAD_DIM)
            dv = d_ref[:, cols]
            merged = w[0][:, h:h + 1] * o_ref[0, :, cols]
            for gg in range(1, N_GROUPS):
                merged = merged + w[gg][:, h:h + 1] * o_ref[gg, :, cols]
            dsum = jnp.sum(dv * merged, axis=-1, keepdims=True)
            for gg in range(N_GROUPS):
                wg = w[gg][:, h:h + 1]
                do_ref[gg, :, cols] = (wg * dv).astype(do_ref.dtype)
                dl_ref[gg, :, h:h + 1] = wg * dsum

    big = pl.BlockSpec((N_GROUPS, tr, D_MODEL), lambda i: (0, i, 0))
    small = pl.BlockSpec((N_GROUPS, tr, N_HEADS), lambda i: (0, i, 0))
    return _pcall(body, name=name, grid=(S // tr,),
                  in_specs=[pl.BlockSpec((tr, D_MODEL), lambda i: (i, 0)), big, small], out_specs=[big, small],
                  out_shape=[jax.ShapeDtypeStruct((N_GROUPS, S, D_MODEL), BF16),
                             jax.ShapeDtypeStruct((N_GROUPS, S, N_HEADS), F32)],
                  compiler_params=_params(("parallel",)))(d_out, o, lse)


def _shift_rows(a, k):
    return pltpu.roll(a, k % a.shape[0], axis=0)


def _pool_level(g, levels):
    return jnp.where(g == 0, levels[0], jnp.where(g == 1, levels[1], jnp.where(g == 2, levels[2], levels[3])))


def _pool_fwd(name, u, w_group, scale, x_in, tr=1024):
    S, D = u.shape
    H = POOL_HALO

    def body(u_ref, halo_ref, w_ref, sc_ref, x_ref, y_ref, z_ref, xo_ref):
        g = pl.program_id(0)
        i = pl.program_id(1)
        uv = u_ref[...]
        halo = jnp.where(i > 0, halo_ref[...], 0.0)
        s = jnp.concatenate([halo, uv], axis=0)
        levels = []
        for k in (1, 2, 4, 8):
            s = s + _shift_rows(s, k)
            levels.append(s[H:])
        t = i * tr + lax.broadcasted_iota(jnp.int32, (tr, 1), 0)
        cnt = jnp.minimum(t + 1, jnp.left_shift(2, g)).astype(F32)
        y = _pool_level(g, levels) / cnt - uv
        yb = y.astype(BF16)
        z = _dot(yb, w_ref[...], "nn")
        y_ref[...] = yb
        z_ref[...] = z
        xo_ref[...] = x_ref[...] + z * sc_ref[...]

    blk = pl.BlockSpec((tr, POOL_DIM), lambda g, i: (i, g))
    return _pcall(
        body, name=name, grid=(D // POOL_DIM, S // tr),
        in_specs=[blk, pl.BlockSpec((H, POOL_DIM), lambda g, i: (jnp.maximum(i * (tr // H) - 1, 0), g)),
                  pl.BlockSpec((None, POOL_DIM, POOL_DIM), lambda g, i: (g, 0, 0)),
                  pl.BlockSpec((1, POOL_DIM), lambda g, i: (0, g)), blk],
        out_specs=[blk, blk, blk],
        out_shape=[jax.ShapeDtypeStruct((S, D), BF16), jax.ShapeDtypeStruct((S, D), F32),
                   jax.ShapeDtypeStruct((S, D), F32)],
        compiler_params=_params(("parallel", "parallel")))(u, u, w_group, scale, x_in)


def _pool_bwd(name, d_out, z, y, scale, w_group, tr=1024):
    S, D = d_out.shape
    H = POOL_HALO

    def body(d_ref, dn_ref, z_ref, y_ref, sc_ref, w_ref, du_ref, dw_ref, dsc_ref):
        g = pl.program_id(0)
        i = pl.program_id(1)
        dv = d_ref[...]
        dz = jnp.concatenate([dv, dn_ref[...]], axis=0) * sc_ref[...]
        dzb = dz.astype(BF16)
        dy = _dot(dzb, w_ref[...], "nt")
        t = i * tr + lax.broadcasted_iota(jnp.int32, (tr + H, 1), 0)
        cnt = jnp.minimum(t + 1, jnp.left_shift(2, g)).astype(F32)
        s = jnp.where(t < S, dy / cnt, 0.0)
        levels = []
        for k in (1, 2, 4, 8):
            s = s + _shift_rows(s, -k)
            levels.append(s[:tr])
        du_ref[...] = (_pool_level(g, levels) - dy[:tr]).astype(du_ref.dtype)
        dw = _dot(y_ref[...], dzb[:tr], "tn")
        dsc = jnp.sum(dv * z_ref[...], axis=0, keepdims=True)

        @pl.when(i == 0)
        def _():
            dw_ref[...] = dw
            dsc_ref[...] = dsc

        @pl.when(i > 0)
        def _():
            dw_ref[...] += dw
            dsc_ref[...] += dsc

    blk = pl.BlockSpec((tr, POOL_DIM), lambda g, i: (i, g))
    vec = pl.BlockSpec((1, POOL_DIM), lambda g, i: (0, g))
    mat = pl.BlockSpec((None, POOL_DIM, POOL_DIM), lambda g, i: (g, 0, 0))
    nxt = pl.BlockSpec((H, POOL_DIM), lambda g, i: (jnp.minimum((i + 1) * (tr // H), S // H - 1), g))
    return _pcall(
        body, name=name, grid=(D // POOL_DIM, S // tr), in_specs=[blk, nxt, blk, blk, vec, mat],
        out_specs=[blk, mat, vec],
        out_shape=[jax.ShapeDtypeStruct((S, D), BF16), jax.ShapeDtypeStruct((D // POOL_DIM, POOL_DIM, POOL_DIM), F32),
                   jax.ShapeDtypeStruct((1, D), F32)],
        compiler_params=_params(("parallel", "arbitrary")))(d_out, d_out, z, y, scale, w_group)


def _row_tile(rows, cols, target_bytes=1 << 20):
    best = rows
    for tr in range(8, rows + 1, 8):
        if rows % tr == 0 and tr * cols * 4 <= target_bytes:
            best = tr
    return best if best * cols * 4 <= 4 * target_bytes else rows


def _adamw(name, w, g, m, v):
    R, C = w.shape
    tr = _row_tile(R, C) if R % 8 == 0 else R

    def body(w_ref, g_ref, m_ref, v_ref, d_ref, mo_ref, vo_ref):
        gv = g_ref[...]
        m2 = ADAM_B1 * m_ref[...] + (1.0 - ADAM_B1) * gv
        v2 = ADAM_B2 * v_ref[...] + (1.0 - ADAM_B2) * (gv * gv)
        m_hat = m2 / (1.0 - ADAM_B1 ** ADAM_STEP)
        v_hat = v2 / (1.0 - ADAM_B2 ** ADAM_STEP)
        d_ref[...] = -ADAM_LR * (m_hat / (jnp.sqrt(v_hat) + ADAM_EPS) + ADAM_WD * w_ref[...])
        mo_ref[...] = m2
        vo_ref[...] = v2

    blk = pl.BlockSpec((tr, C), lambda i: (i, 0))
    sds = jax.ShapeDtypeStruct((R, C), F32)
    return _pcall(body, name=name, grid=(R // tr,), in_specs=[blk] * 4, out_specs=[blk] * 3, out_shape=[sds] * 3,
                  compiler_params=_params(("parallel",)))(w, g, m, v)


def _sum_leading(name, a, out_dtype):
    n, R, C = a.shape
    tr = _row_tile(R, C) if R % 16 == 0 else R
    if tr % 16:
        tr = R

    def body(a_ref, o_ref):
        acc = a_ref[0].astype(F32)
        for j in range(1, n):
            acc = acc + a_ref[j].astype(F32)
        o_ref[...] = acc.astype(o_ref.dtype)

    return _pcall(body, name=name, grid=(R // tr,), in_specs=[pl.BlockSpec((n, tr, C), lambda i: (0, i, 0))],
                  out_specs=pl.BlockSpec((tr, C), lambda i: (i, 0)), out_shape=jax.ShapeDtypeStruct((R, C), out_dtype),
                  compiler_params=_params(("parallel",)))(a)


def _cast_into_slot(name, w, chip, dtype):
    R, C = w.shape
    tr = _row_tile(R, C)
    if tr % 16:
        tr = R

    def body(c_ref, w_ref, o_ref):
        o_ref[...] = w_ref[...].astype(o_ref.dtype)

    grid_spec = pltpu.PrefetchScalarGridSpec(
        num_scalar_prefetch=1, grid=(R // tr,), in_specs=[pl.BlockSpec((tr, C), lambda i, c_ref: (i, 0))],
        out_specs=pl.BlockSpec((None, tr, C), lambda i, c_ref: (c_ref[0], i, 0)))
    return _pcall(body, name=name, grid_spec=grid_spec, out_shape=jax.ShapeDtypeStruct((N_CHIPS, R, C), dtype),
                  compiler_params=_params(("parallel",)))(chip, w)


def _owner_sum(name, mine, landed, chip, core, out=None, layer=None):
    _, half, C = mine.shape
    tr = _row_tile(half, C)
    if tr % 16:
        tr = half
    nrt = half // tr
    has_out = out is not None

    def body(*refs):
        m_ref, l_ref, o_ref = refs[2], refs[3], refs[-1]
        acc = m_ref[...].astype(F32)
        for j in range(3):
            acc = acc + l_ref[j].astype(F32)
        o_ref[...] = acc

    if layer is None:
        out_shape = jax.ShapeDtypeStruct((2 * half, C), F32)
        o_spec = pl.BlockSpec((tr, C), lambda i, ch, co: (co[0] * nrt + i, 0))
    else:
        out_shape = jax.ShapeDtypeStruct((2, 2 * half, C), F32)
        o_spec = pl.BlockSpec((None, tr, C), lambda i, ch, co: (layer, co[0] * nrt + i, 0))
    in_specs = [pl.BlockSpec((None, tr, C), lambda i, ch, co: (ch[0], i, 0)),
                pl.BlockSpec((3, tr, C), lambda i, ch, co: (0, i, 0))]
    operands = [chip, core, mine, landed]
    aliases = {}
    if has_out:
        in_specs.append(ANY)
        operands.append(out)
        aliases = {4: 0}
    grid_spec = pltpu.PrefetchScalarGridSpec(num_scalar_prefetch=2, grid=(nrt,), in_specs=in_specs, out_specs=o_spec)
    return _pcall(body, name=name, grid_spec=grid_spec, out_shape=out_shape, input_output_aliases=aliases,
                  compiler_params=_params(("parallel",)))(*operands)


def _add_my_half(name, p, q, core):
    n, R, C = p.shape
    half = R // 2

    def body(c_ref, p_ref, q_ref, o_ref):
        o_ref[...] = (p_ref[...].astype(F32) + q_ref[...].astype(F32)).astype(o_ref.dtype)

    grid_spec = pltpu.PrefetchScalarGridSpec(
        num_scalar_prefetch=1, grid=(n,),
        in_specs=[pl.BlockSpec((None, half, C), lambda s, c_ref: (s, c_ref[0], 0)),
                  pl.BlockSpec((None, half, C), lambda s, c_ref: (s, 0, 0))],
        out_specs=pl.BlockSpec((None, half, C), lambda s, c_ref: (s, 0, 0)))
    return _pcall(body, name=name, grid_spec=grid_spec, out_shape=jax.ShapeDtypeStruct((n, half, C), BF16),
                  compiler_params=_params(("parallel",)))(core, p, q)


ANY = pl.BlockSpec(memory_space=pl.ANY)


def _place():
    x, y, c = lax.axis_index("x"), lax.axis_index("y"), lax.axis_index("c")
    other_chips = [(1 - x, y), (x, 1 - y), (1 - x, 1 - y)]
    return x, y, c, other_chips


def _gather_weights(name, bufs, split):
    n = len(bufs)

    def body(*refs):
        outs = refs[n:2 * n]
        ici_send, ici_recv, d2d_send, d2d_recv = refs[2 * n:]
        x, y, c, chips = _place()
        me = 2 * x + y
        sibling = (x, y, 1 - c)

        def piece(t, chip, core_half):
            if not split[t]:
                return outs[t].at[chip]
            half = outs[t].shape[1] // 2
            return outs[t].at[chip, pl.ds(core_half * half, half)]

        sends = []
        for t in range(n):
            for j, (px, py) in enumerate(chips):
                sends.append(pltpu.make_async_remote_copy(
                    src_ref=piece(t, me, c), dst_ref=piece(t, me, c), send_sem=ici_send.at[3 * t + j],
                    recv_sem=ici_recv.at[3 * t + j], device_id=(px, py, c), device_id_type=MESH))
        for cp in sends:
            cp.start()
        passed = []
        for t in range(n):
            for j, (px, py) in enumerate(chips):
                landed = piece(t, 2 * px + py, c)
                pltpu.make_async_remote_copy(
                    src_ref=landed, dst_ref=landed, send_sem=ici_send.at[3 * t + j],
                    recv_sem=ici_recv.at[3 * t + j], device_id=(px, py, c), device_id_type=MESH).wait_recv()
                if split[t]:
                    fwd = pltpu.make_async_remote_copy(
                        src_ref=landed, dst_ref=landed, send_sem=d2d_send.at[3 * t + j],
                        recv_sem=d2d_recv.at[3 * t + j], device_id=sibling, device_id_type=MESH)
                    fwd.start()
                    passed.append(fwd)
        for t in range(n):
            if split[t]:
                for j, (px, py) in enumerate(chips):
                    theirs = piece(t, 2 * px + py, 1 - c)
                    pltpu.make_async_remote_copy(
                        src_ref=theirs, dst_ref=theirs, send_sem=d2d_send.at[3 * t + j],
                        recv_sem=d2d_recv.at[3 * t + j], device_id=sibling, device_id_type=MESH).wait_recv()
        for cp in sends + passed:
            cp.wait_send()

    out_shape = [jax.ShapeDtypeStruct(b.shape, b.dtype) for b in bufs]
    return _pcall(body, name=name, in_specs=[ANY] * n, out_specs=[ANY] * n, out_shape=out_shape,
                  input_output_aliases={t: t for t in range(n)},
                  scratch_shapes=[pltpu.SemaphoreType.DMA((3 * n,)), pltpu.SemaphoreType.DMA((3 * n,)),
                                  pltpu.SemaphoreType.DMA((3 * n,)), pltpu.SemaphoreType.DMA((3 * n,))])(*bufs)


def _remote(src, dst, send, recv, k, to):
    return pltpu.make_async_remote_copy(src_ref=src, dst_ref=dst, send_sem=send.at[k], recv_sem=recv.at[k],
                                        device_id=to, device_id_type=MESH)


def _in_place(bufs):
    return dict(operands=bufs, out_shapes=[jax.ShapeDtypeStruct(b.shape, b.dtype) for b in bufs],
                aliases={t: t for t in range(len(bufs))})


def _gather_over_ici(bufs, split):
    n = len(bufs)

    def copies(ins, outs, send, recv, base=0):
        x, y, c, chips = _place()
        me = 2 * x + y
        out = []
        for t in range(n):
            piece = outs[t].at[me]
            if split[t]:
                half = outs[t].shape[1] // 2
                piece = outs[t].at[me, pl.ds(c * half, half)]
            for j, (px, py) in enumerate(chips):
                out.append(_remote(piece, piece, send, recv, base + 3 * t + j, (px, py, c)))
        return out

    return _Rider(n_copies=3 * n, copies=copies, **_in_place(bufs))


def _gather_to_sibling(bufs):
    n = len(bufs)

    def copies(ins, outs, send, recv, base=0):
        x, y, c, chips = _place()
        out = []
        for t in range(n):
            half = outs[t].shape[1] // 2
            for j, (px, py) in enumerate(chips):
                piece = outs[t].at[2 * px + py, pl.ds(c * half, half)]
                out.append(_remote(piece, piece, send, recv, base + 3 * t + j, (x, y, 1 - c)))
        return out

    return _Rider(n_copies=3 * n, copies=copies, **_in_place(bufs))


def _swap_halves_rider(parts):
    n = len(parts)

    def copies(ins, outs, send, recv, base=0):
        x, y, c, _ = _place()
        out = []
        for t in range(n):
            half = ins[t].shape[1] // 2
            out.append(_remote(ins[t].at[:, pl.ds((1 - c) * half, half)], outs[t], send, recv, base + t,
                               (x, y, 1 - c)))
        return out

    shapes = [jax.ShapeDtypeStruct((p.shape[0], p.shape[1] // 2, p.shape[2]), p.dtype) for p in parts]
    return _Rider(parts, shapes, {}, n, copies)


def _scatter_rider(sums):
    n = len(sums)

    def copies(ins, outs, send, recv, base=0):
        x, y, c, chips = _place()
        out = []
        for t in range(n):
            for j, (px, py) in enumerate(chips):
                out.append(_remote(ins[t].at[2 * px + py], outs[t].at[j], send, recv, base + 3 * t + j, (px, py, c)))
        return out

    shapes = [jax.ShapeDtypeStruct((3,) + s.shape[1:], s.dtype) for s in sums]
    return _Rider(sums, shapes, {}, 3 * n, copies)


def _share_rider(blocks, pieces):
    def copies(ins, outs, send, recv, base=0):
        x, y, c, _ = _place()
        out = []
        for k, (t, l) in enumerate(pieces):
            ref = outs[t] if l is None else outs[t].at[l]
            r2 = ref.shape[0] // 2
            piece = ref.at[pl.ds(c * r2, r2)]
            out.append(_remote(piece, piece, send, recv, base + k, (x, y, 1 - c)))
        return out

    return _Rider(n_copies=len(pieces), copies=copies, **_in_place(blocks))


def _gather_small(name, v):
    def body(v_ref, out_ref, send_sem, recv_sem, local_sem):
        x, y, c, _ = _place()
        me = 4 * x + 2 * y + c
        flips = [(fx, fy, fc) for fx in (0, 1) for fy in (0, 1) for fc in (0, 1)][1:]
        local = pltpu.make_async_copy(v_ref, out_ref.at[me], local_sem)
        local.start()
        copies = []
        for j, (fx, fy, fc) in enumerate(flips):
            peer = (x ^ fx, y ^ fy, c ^ fc)
            copies.append(pltpu.make_async_remote_copy(
                src_ref=v_ref, dst_ref=out_ref.at[me], send_sem=send_sem.at[j], recv_sem=recv_sem.at[j],
                device_id=peer, device_id_type=MESH))
        for cp in copies:
            cp.start()
        for j, (fx, fy, fc) in enumerate(flips):
            peer = (x ^ fx, y ^ fy, c ^ fc)
            pltpu.make_async_remote_copy(
                src_ref=v_ref, dst_ref=out_ref.at[4 * peer[0] + 2 * peer[1] + peer[2]], send_sem=send_sem.at[j],
                recv_sem=recv_sem.at[j], device_id=peer, device_id_type=MESH).wait_recv()
        for cp in copies:
            cp.wait_send()
        local.wait()

    return _pcall(body, name=name, in_specs=[ANY], out_specs=ANY,
                  out_shape=jax.ShapeDtypeStruct((8,) + v.shape, v.dtype),
                  scratch_shapes=[pltpu.SemaphoreType.DMA((7,)), pltpu.SemaphoreType.DMA((7,)),
                                  pltpu.SemaphoreType.DMA])(v)


def _fold(a, dil):
    if dil == 1:
        return a
    S, C = a.shape
    return a.reshape(S // dil, dil, C).transpose(1, 0, 2).reshape(S, C)


def _unfold(a, dil):
    if dil == 1:
        return a
    S, C = a.shape
    return a.reshape(dil, S // dil, C).transpose(1, 0, 2).reshape(S, C)


def _fold_groups(a):
    return jnp.stack([_fold(a[g] if a.ndim == 3 else a, d) for g, d in enumerate(ATTN_DILATIONS)])


def _unfold_groups(a):
    return jnp.stack([_unfold(a[g], d) for g, d in enumerate(ATTN_DILATIONS)])


class _NoComm:
    def __init__(self, weights):
        self.weights = weights
        self.grads = {}

    def w(self, name):
        return self.weights[name]

    def run(self, fn, name, *args, **kw):
        return fn(name, *args, **kw)

    def grad(self, name, value):
        self.grads[name] = value


def _local_step(xs, tgt, attn_norm, ffn_norm, final_norm, comm):
    run = comm.run
    h0 = run(_rms_fwd, "norm_attn", xs, attn_norm)
    hf = _fold_groups(h0)
    qkv = run(_qkv_proj, "qkv_proj", hf, comm.w("qkv"))
    o_f, lse_f = run(_attn_fwd, "attn_fwd", qkv)
    o_t, lse_t = _unfold_groups(o_f), _unfold_groups(lse_f)
    merged = run(_merge_fwd, "merge_fwd", o_t, lse_t)
    x1 = run(_mm_rows, "attn_out", merged, comm.w("o"), F32, 512, res=xs)
    h1 = run(_rms_fwd, "norm_ffn0", x1, ffn_norm[0:1])
    gu0, act0 = run(_ffn_up, "ffn0_up", h1, comm.w("gu0"))
    x2 = run(_mm_rows, "ffn0_down", act0, comm.w("d0"), F32, 512, res=x1)
    h2 = run(_rms_fwd, "norm_pool", x2, comm.w("pool_norm"))
    u = run(_mm_rows, "pool_in", h2, comm.w("p"), F32, 512)
    y, z, x3 = run(_pool_fwd, "pool_fwd", u, comm.w("pg"), comm.w("pool_scale"), x2)
    h3 = run(_rms_fwd, "norm_ffn1", x3, ffn_norm[1:2])
    gu1, act1 = run(_ffn_up, "ffn1_up", h3, comm.w("gu1"))
    x4 = run(_mm_rows, "ffn1_down", act1, comm.w("d1"), F32, 512, res=x3)
    loss, dx4, d_final = run(_loss_bwd, "loss_head", x4, final_norm, tgt)

    dgu1 = run(_ffn_down_bwd, "ffn1_down_bwd", dx4, comm.w("d1"), gu1)
    comm.grad("d1", run(_mm_tn, "ffn1_down_dw", act1, dx4, FF_SHARD, 1024))
    comm.grad("gu1", run(_ffn_dw_up, "ffn1_up_dw", h3, dgu1))
    dh3 = run(_ffn_dh, "ffn1_up_dh", dgu1, comm.w("gu1"))
    dx3, d_ffn1 = run(_rms_bwd, "norm_ffn1_bwd", x3, ffn_norm[1:2], dh3, dx4)

    du, dw_pg, d_scale = run(_pool_bwd, "pool_bwd", dx3, z, y, comm.w("pool_scale"), comm.w("pg"))
    comm.grad("pg", dw_pg)
    comm.grad("p", run(_mm_tn, "pool_in_dw", h2, du, D_MODEL, 1024))
    dh2 = run(_mm_nt_rows, "pool_in_dh", du, comm.w("p"), 512)
    dx2, d_pool = run(_rms_bwd, "norm_pool_bwd", x2, comm.w("pool_norm"), dh2, dx3)

    dgu0 = run(_ffn_down_bwd, "ffn0_down_bwd", dx2, comm.w("d0"), gu0)
    comm.grad("d0", run(_mm_tn, "ffn0_down_dw", act0, dx2, FF_SHARD, 1024))
    comm.grad("gu0", run(_ffn_dw_up, "ffn0_up_dw", h1, dgu0))
    dh1 = run(_ffn_dh, "ffn0_up_dh", dgu0, comm.w("gu0"))
    dx1, d_ffn0 = run(_rms_bwd, "norm_ffn0_bwd", x1, ffn_norm[0:1], dh1, dx2)

    d_merged = run(_mm_nt_rows, "attn_out_dh", dx1, comm.w("o"), 512)
    comm.grad("o", run(_mm_tn, "attn_out_dw", merged, dx1, D_MODEL, 1024))
    do_t, dl_t = run(_merge_bwd, "merge_bwd", d_merged, o_t, lse_t)
    dqkv = run(_attn_bwd, "attn_bwd", qkv, _fold_groups(do_t), lse_f, _fold_groups(dl_t))
    comm.grad("qkv", run(_qkv_dw, "qkv_dw", hf, dqkv))
    dhf = run(_qkv_dh, "qkv_dh", dqkv, comm.w("qkv"))
    dh0 = dhf[0] + _unfold(dhf[1], ATTN_DILATIONS[1]) + _unfold(dhf[2], ATTN_DILATIONS[2])
    grad_x, d_attn = run(_rms_bwd, "norm_attn_bwd", xs, attn_norm, dh0, dx1)

    small = dict(attn=d_attn, ffn0=d_ffn0, ffn1=d_ffn1, final=d_final, pool=d_pool, scale=d_scale)
    return loss, grad_x, small


TENSORS = ("qkv", "o", "p", "pg", "gu0", "gu1", "d0", "d1")


def _block_of(name):
    if name[:-1] in ("gu", "d"):
        return name[:-1], int(name[-1])
    return name, None


class _MeshComm:
    def __init__(self, bufs, chip_arr, core_arr, pg_rows):
        self.bufs = dict(bufs)
        self.chip_arr, self.core_arr, self.pg_rows = chip_arr, core_arr, pg_rows
        self.parts, self.sums, self.blocks = {}, {}, {}
        ici = lambda *names: lambda: self.gather(names, True)
        d2d = lambda *names: lambda: self.gather(names, False)
        swap = lambda *names: lambda: self.swap(names)
        scatter = lambda *names: lambda: self.scatter(names)
        share = lambda *names: lambda: self.share(names)
        self.plan = {
            "qkv_proj": [ici("o", "gu0", "d0")],
            "attn_fwd": [d2d("o", "gu0", "d0"), ici("p", "pg", "pool_norm", "pool_scale")],
            "merge_fwd": [d2d("p", "pg")],
            "ffn0_up": [ici("gu1")],
            "ffn0_down": [d2d("gu1"), ici("d1")],
            "pool_in": [d2d("d1")],
            "ffn1_up_dh": [swap("d1", "gu1")],
            "ffn0_down_bwd": [scatter("gu1")],
            "ffn0_down_dw": [scatter("d1")],
            "ffn0_up_dh": [swap("pg", "p", "d0", "gu0"), share("gu1", "d1")],
            "merge_bwd": [swap("o")],
            "attn_bwd": [scatter("pg", "p", "d0", "gu0", "o")],
            "qkv_dw": [share("pg", "p", "d0", "gu0", "o")],
            "qkv_dh": [swap("qkv")],
            "norm_attn_bwd": [scatter("qkv")],
        }

    def gather(self, names, over_ici):
        bufs = [self.bufs[n] for n in names]
        rider = _gather_over_ici(bufs, [n in TENSORS for n in names]) if over_ici else _gather_to_sibling(bufs)
        return rider, lambda res: self.bufs.update(zip(names, res))

    def swap(self, names):
        def done(res):
            for n, q in zip(names, res):
                self.sums[n] = _add_my_half("chip_sum_" + n, self.parts[n], q, self.core_arr)
        return _swap_halves_rider([self.parts[n] for n in names]), done

    def scatter(self, names):
        def done(res):
            for n, landed in zip(names, res):
                key, layer = _block_of(n)
                self.blocks[key] = _owner_sum("owner_sum_" + n, self.sums[n], landed, self.chip_arr, self.core_arr,
                                              out=self.blocks.get(key), layer=layer)
        return _scatter_rider([self.sums[n] for n in names]), done

    def share(self, names):
        keys, pieces = [], []
        for n in names:
            key, layer = _block_of(n)
            if key not in keys:
                keys.append(key)
            pieces.append((keys.index(key), layer))
        return _share_rider([self.blocks[k] for k in keys], pieces), lambda res: self.blocks.update(zip(keys, res))

    def alone(self, name, made):
        rider, done = made
        done(_comm_only(name, rider))

    def run(self, fn, name, *args, **kw):
        made = [make() for make in self.plan.get(name, [])]
        if not made:
            return fn(name, *args, **kw)
        riders = [r for r, _ in made]
        out = _ride(riders[0] if len(riders) == 1 else _riders(*riders), fn, name, *args, **kw)
        for r, done in made:
            done(r.results)
        return out

    def w(self, name):
        b = self.bufs[name]
        if name in ("o", "p", "d0", "d1"):
            return b.reshape(-1, b.shape[-1])
        if name == "pg":
            b = b.reshape(N_CHIPS, 4, self.pg_rows, POOL_DIM).transpose(1, 0, 2, 3)
            return b.reshape(4, POOL_DIM, POOL_DIM)
        if name in ("pool_norm", "pool_scale"):
            return b.reshape(1, -1)
        return b

    def grad(self, name, value):
        if name == "pg":
            value = value.reshape(4, N_CHIPS, self.pg_rows, POOL_DIM).transpose(1, 0, 2, 3)
            value = value.reshape(N_CHIPS, 4 * self.pg_rows, POOL_DIM).astype(BF16)
        elif name in ("o", "p", "d0", "d1"):
            value = value.reshape(N_CHIPS, value.shape[0] // N_CHIPS, value.shape[1])
        self.parts[name] = value


def kernel(x, attn_norm, w_qkv, w_attn_out, pool_norm, w_pool_in, w_pool_group, pool_scale, ffn_norm, w_ffn_gate_up, w_ffn_down, final_norm, loss_target, m_attn_norm, m_w_qkv, m_w_attn_out, m_pool_norm, m_w_pool_in, m_w_pool_group, m_pool_scale, m_ffn_norm, m_w_ffn_gate_up, m_w_ffn_down, m_final_norm, v_attn_norm, v_w_qkv, v_w_attn_out, v_pool_norm, v_w_pool_in, v_w_pool_group, v_pool_scale, v_ffn_norm, v_w_ffn_gate_up, v_w_ffn_down, v_final_norm):
    D = D_MODEL
    chip = 2 * lax.axis_index("x") + lax.axis_index("y")
    core = lax.axis_index("c")
    pg_rows = w_pool_group.shape[2]

    chip_arr = chip.astype(jnp.int32).reshape(1)
    core_arr = core.astype(jnp.int32).reshape(1)

    shards = [w_qkv[0], w_attn_out[0], w_pool_in[0], w_pool_group[0].reshape(4 * pg_rows, POOL_DIM),
              w_ffn_gate_up[0], w_ffn_gate_up[1], w_ffn_down[0], w_ffn_down[1]]
    bufs = {nm: _cast_into_slot("cast_" + nm, s, chip_arr, BF16) for nm, s in zip(TENSORS, shards)}
    bufs["pool_norm"] = _cast_into_slot("place_pool_norm", pool_norm, chip_arr, F32)
    bufs["pool_scale"] = _cast_into_slot("place_pool_scale", pool_scale, chip_arr, F32)
    bufs["qkv"] = _gather_weights("gather_qkv", [bufs["qkv"]], [True])[0]

    comm = _MeshComm(bufs, chip_arr, core_arr, pg_rows)
    loss_part, grad_x, small = _local_step(x[0], loss_target[0], attn_norm, ffn_norm, final_norm.reshape(1, D), comm)
    comm.alone("halves_to_sibling", comm.share(["qkv"]))
    g_w_qkv, g_w_o, g_w_p, g_w_pg, g_w_gu, g_w_d = [comm.blocks[k] for k in ("qkv", "o", "p", "pg", "gu", "d")]

    rows = jnp.concatenate([small["attn"], small["ffn0"], small["ffn1"], small["final"], small["pool"],
                            small["scale"], jnp.zeros((2, D), F32)], axis=0)
    all_rows = _gather_small("gather_gain_grads", rows)
    tot = _sum_leading("sum_gain_grads", all_rows, F32)
    g_attn_norm = tot[0:1]
    g_ffn_norm = tot[1:3]
    g_final_norm = tot[3]
    g_pool_norm = lax.dynamic_slice(tot, (4, chip * POOL_DIM), (1, POOL_DIM))
    g_pool_scale = lax.dynamic_slice(tot, (5, chip * POOL_DIM), (1, POOL_DIM))

    loss = lax.psum(loss_part[0, 0], ("x", "y", "c"))

    def update(name, w, g, m, v):
        shape = w.shape
        cols = shape[-1]
        as2d = lambda a: a.reshape(-1, cols)
        d, m2, v2 = _adamw("adamw_" + name, as2d(w), as2d(g), as2d(m), as2d(v))
        return g.reshape(shape), d.reshape(shape), m2.reshape(shape), v2.reshape(shape)

    results = [
        update("attn_norm", attn_norm, g_attn_norm, m_attn_norm, v_attn_norm),
        update("w_qkv", w_qkv, g_w_qkv, m_w_qkv, v_w_qkv),
        update("w_attn_out", w_attn_out, g_w_o, m_w_attn_out, v_w_attn_out),
        update("pool_norm", pool_norm, g_pool_norm, m_pool_norm, v_pool_norm),
        update("w_pool_in", w_pool_in, g_w_p, m_w_pool_in, v_w_pool_in),
        update("w_pool_group", w_pool_group, g_w_pg, m_w_pool_group, v_w_pool_group),
        update("pool_scale", pool_scale, g_pool_scale, m_pool_scale, v_pool_scale),
        update("ffn_norm", ffn_norm, g_ffn_norm, m_ffn_norm, v_ffn_norm),
        update("w_ffn_gate_up", w_ffn_gate_up, g_w_gu, m_w_ffn_gate_up, v_w_ffn_gate_up),
        update("w_ffn_down", w_ffn_down, g_w_d, m_w_ffn_down, v_w_ffn_down),
        update("final_norm", final_norm, g_final_norm, m_final_norm, v_final_norm),
    ]
    grads = [r[0] for r in results]
    deltas = [r[1] for r in results]
    new_m = [r[2] for r in results]
    new_v = [r[3] for r in results]
    return (loss, grad_x[None], *grads, *deltas, *new_m, *new_v)
```

```python
import functools

import jax
import jax.numpy as jnp
from jax import lax
from jax.experimental import pallas as pl
from jax.experimental.pallas import tpu as pltpu

F32 = jnp.float32
BF16 = jnp.bfloat16
MESH = pl.DeviceIdType.MESH

D_MODEL = 1024
N_HEADS = 8
HEAD_DIM = 128
Q_BLOCK = 128
ATTN_DILATIONS = (1, 4, 16)
N_GROUPS = 3
D_FF = 2816
N_CHIPS = 4
FF_SHARD = 2 * D_FF // N_CHIPS
QKV_SHARD = 3 * 3 * D_MODEL // N_CHIPS
QKV_TILE = 768
POOL_WINDOWS = (2, 4, 8, 16)
POOL_DIM = 256
POOL_HALO = 16
RMS_EPS = 1e-6
NEG = -1e30
ADAM_LR = 0.001
ADAM_B1 = 0.9
ADAM_B2 = 0.999
ADAM_EPS = 1e-08
ADAM_WD = 0.01
ADAM_STEP = 10
VMEM_LIMIT = 56 * 1024 * 1024

_DN = {
    "nn": (((1,), (0,)), ((), ())),
    "nt": (((1,), (1,)), ((), ())),
    "tn": (((0,), (0,)), ((), ())),
}


class _Rider:
    def __init__(self, operands, out_shapes, aliases, n_copies, copies):
        self.operands = list(operands)
        self.out_shapes = list(out_shapes)
        self.aliases = dict(aliases)
        self.n_copies = n_copies
        self.copies = copies
        self.results = None


def _riders(*riders):
    operands, out_shapes, aliases, offsets = [], [], {}, []
    n = 0
    for r in riders:
        offsets.append((len(operands), len(out_shapes), n))
        aliases.update({len(operands) + i: len(out_shapes) + o for i, o in r.aliases.items()})
        operands += r.operands
        out_shapes += r.out_shapes
        n += r.n_copies

    def copies(ins, outs, send, recv, base=0):
        out = []
        for r, (i0, o0, s0) in zip(riders, offsets):
            out += r.copies(ins[i0:i0 + len(r.operands)], outs[o0:o0 + len(r.out_shapes)], send, recv, base + s0)
        return out

    both = _Rider(operands, out_shapes, aliases, n, copies)
    both.parts = (riders, offsets)
    return both


_pending_rider = []


def _ride(rider, fn, *args, **kw):
    _pending_rider.append(rider)
    out = fn(*args, **kw)
    assert not _pending_rider
    if hasattr(rider, "parts"):
        for r, (_, o0, _) in zip(*rider.parts):
            r.results = rider.results[o0:o0 + len(r.out_shapes)]
    return out


def _pcall(body, **kw):
    if not _pending_rider:
        return pl.pallas_call(body, **kw)
    rider = _pending_rider.pop()
    grid = kw.get("grid", ())
    in_specs = list(kw.get("in_specs", []))
    single = not isinstance(kw["out_shape"], (list, tuple))
    out_shape = [kw["out_shape"]] if single else list(kw["out_shape"])
    out_specs = [kw["out_specs"]] if single else list(kw["out_specs"])
    scratch = list(kw.get("scratch_shapes", []))
    n_in, n_out, n_scr = len(in_specs), len(out_shape), len(scratch)
    r_in, r_out = len(rider.operands), len(rider.out_shapes)

    def wrapped(*refs):
        ins, rins = refs[:n_in], refs[n_in:n_in + r_in]
        outs = refs[n_in + r_in:n_in + r_in + n_out]
        routs = refs[n_in + r_in + n_out:n_in + r_in + n_out + r_out]
        scr = refs[n_in + r_in + n_out + r_out:n_in + r_in + n_out + r_out + n_scr]
        send, recv = refs[-2:]
        ids = [pl.program_id(a) for a in range(len(grid))]
        first, last = True, True
        for a, pid in enumerate(ids):
            first = jnp.logical_and(first, pid == 0)
            last = jnp.logical_and(last, pid == grid[a] - 1)
        if grid:
            @pl.when(first)
            def _():
                for cp in rider.copies(rins, routs, send, recv):
                    cp.start()
        else:
            for cp in rider.copies(rins, routs, send, recv):
                cp.start()
        body(*ins, *outs, *scr)
        if grid:
            @pl.when(last)
            def _():
                for cp in rider.copies(rins, routs, send, recv):
                    cp.wait()
        else:
            for cp in rider.copies(rins, routs, send, recv):
                cp.wait()

    any_spec = pl.BlockSpec(memory_space=pl.ANY)
    kw2 = dict(kw)
    kw2.update(
        in_specs=in_specs + [any_spec] * r_in, out_specs=out_specs + [any_spec] * r_out,
        out_shape=out_shape + rider.out_shapes,
        scratch_shapes=scratch + [pltpu.SemaphoreType.DMA((rider.n_copies,)),
                                  pltpu.SemaphoreType.DMA((rider.n_copies,))],
        input_output_aliases={**kw.get("input_output_aliases", {}),
                              **{n_in + i: n_out + o for i, o in rider.aliases.items()}})
    if grid:
        kw2["compiler_params"] = _params(("arbitrary",) * len(grid))
    call = pl.pallas_call(wrapped, **kw2)

    def run(*operands):
        res = call(*operands, *rider.operands)
        rider.results = list(res[n_out:])
        return res[0] if single else list(res[:n_out])

    return run


def _comm_only(name, rider):
    def make():
        return _pcall(lambda: None, name=name, in_specs=[], out_specs=[], out_shape=[])()
    _ride(rider, make)
    return rider.results


def _params(sem):
    return pltpu.CompilerParams(dimension_semantics=sem, vmem_limit_bytes=VMEM_LIMIT)


def _dot(a, b, mode):
    return lax.dot_general(a, b, _DN[mode], preferred_element_type=F32)


def _mm(name, mode, grid, a, a_spec, b, b_spec, out_shape, o_spec, acc_shape, res=None, res_spec=None):
    nk = grid[-1]
    has_res = res is not None

    def body(*refs):
        a_ref, b_ref = refs[0], refs[1]
        res_ref = refs[2] if has_res else None
        o_ref = refs[2 + has_res]
        part = _dot(a_ref[...].astype(BF16), b_ref[...].astype(BF16), mode)
        if nk == 1:
            if has_res:
                part = part + res_ref[...]
            o_ref[...] = part.astype(o_ref.dtype)
            return
        acc_ref = refs[3 + has_res]
        k = pl.program_id(len(grid) - 1)

        @pl.when(k == 0)
        def _():
            acc_ref[...] = part

        @pl.when(k > 0)
        def _():
            acc_ref[...] += part

        @pl.when(k == nk - 1)
        def _():
            r = acc_ref[...]
            if has_res:
                r = r + res_ref[...]
            o_ref[...] = r.astype(o_ref.dtype)

    in_specs = [a_spec, b_spec] + ([res_spec] if has_res else [])
    operands = [a, b] + ([res] if has_res else [])
    scratch = [] if nk == 1 else [pltpu.VMEM(acc_shape, F32)]
    sem = ("parallel",) * (len(grid) - 1) + ("arbitrary",)
    return _pcall(body, name=name, grid=grid, in_specs=in_specs, out_specs=o_spec, out_shape=out_shape,
                  scratch_shapes=scratch, compiler_params=_params(sem))(*operands)


def _mm_rows(name, a, b, out_dtype, tm, res=None):
    M, K = a.shape
    N = b.shape[1]
    return _mm(name, "nn", (M // tm, 1), a, pl.BlockSpec((tm, K), lambda i, k: (i, 0)),
               b, pl.BlockSpec((K, N), lambda i, k: (0, 0)),
               jax.ShapeDtypeStruct((M, N), out_dtype), pl.BlockSpec((tm, N), lambda i, k: (i, 0)), None,
               res=res, res_spec=pl.BlockSpec((tm, N), lambda i, k: (i, 0)))


def _mm_nt_rows(name, a, b, tm):
    M, N = a.shape
    K = b.shape[0]
    return _mm(name, "nt", (M // tm, 1), a, pl.BlockSpec((tm, N), lambda i, k: (i, 0)),
               b, pl.BlockSpec((K, N), lambda i, k: (0, 0)),
               jax.ShapeDtypeStruct((M, K), F32), pl.BlockSpec((tm, K), lambda i, k: (i, 0)), None)


def _mm_tn(name, a, b, tm, tk):
    T, M = a.shape
    N = b.shape[1]
    return _mm(name, "tn", (M // tm, T // tk), a, pl.BlockSpec((tk, tm), lambda i, k: (k, i)),
               b, pl.BlockSpec((tk, N), lambda i, k: (k, 0)),
               jax.ShapeDtypeStruct((M, N), BF16), pl.BlockSpec((tm, N), lambda i, k: (i, 0)), (tm, N))


def _rms_fwd(name, x, g, tr=512):
    S, D = x.shape

    def body(x_ref, g_ref, o_ref):
        xf = x_ref[...]
        r = lax.rsqrt(jnp.mean(xf * xf, axis=-1, keepdims=True) + RMS_EPS)
        o_ref[...] = ((xf * r) * g_ref[...]).astype(o_ref.dtype)

    return _pcall(body, name=name, grid=(S // tr,),
                  in_specs=[pl.BlockSpec((tr, D), lambda i: (i, 0)), pl.BlockSpec((1, D), lambda i: (0, 0))],
                  out_specs=pl.BlockSpec((tr, D), lambda i: (i, 0)),
                  out_shape=jax.ShapeDtypeStruct((S, D), BF16), compiler_params=_params(("parallel",)))(x, g)


def _rms_bwd(name, x, g, dh, dres, tr=512):
    S, D = x.shape

    def body(x_ref, g_ref, dh_ref, dres_ref, dx_ref, dg_ref):
        i = pl.program_id(0)
        xf = x_ref[...]
        r = lax.rsqrt(jnp.mean(xf * xf, axis=-1, keepdims=True) + RMS_EPS)
        xh = xf * r
        dh_v = dh_ref[...]
        dg = jnp.sum(dh_v * xh, axis=0, keepdims=True)
        t = dh_v * g_ref[...]
        dx_ref[...] = dres_ref[...] + r * (t - xh * jnp.mean(t * xh, axis=-1, keepdims=True))

        @pl.when(i == 0)
        def _():
            dg_ref[...] = dg

        @pl.when(i > 0)
        def _():
            dg_ref[...] += dg

    row = pl.BlockSpec((tr, D), lambda i: (i, 0))
    vec = pl.BlockSpec((1, D), lambda i: (0, 0))
    return _pcall(body, name=name, grid=(S // tr,), in_specs=[row, vec, row, row], out_specs=[row, vec],
                  out_shape=[jax.ShapeDtypeStruct((S, D), F32), jax.ShapeDtypeStruct((1, D), F32)],
                  compiler_params=_params(("arbitrary",)))(x, g, dh, dres)


def _rms_bwd_folded(name, x, g, dhf, dres, tb=512):
    S, D = x.shape

    def body(x_ref, g_ref, d0_ref, d1_ref, d2_ref, dres_ref, dx_ref, dg_ref, dh_ref):
        i = pl.program_id(0)
        chunks = _lane_chunks(D)
        for c, cols in enumerate(chunks):
            dh_ref[c] = d0_ref[0, :, cols]
        for dil, ref in ((ATTN_DILATIONS[1], d1_ref), (ATTN_DILATIONS[2], d2_ref)):
            n = tb // dil
            for k in range(dil):
                for c, cols in enumerate(chunks):
                    dh_ref[c, _strided_rows(k, n, dil), :] += ref[k, :, cols]
        xf = x_ref[...]
        r = lax.rsqrt(jnp.mean(xf * xf, axis=-1, keepdims=True) + RMS_EPS)
        xh = xf * r
        dh_v = jnp.concatenate([dh_ref[c] for c in range(len(chunks))], axis=1)
        dg = jnp.sum(dh_v * xh, axis=0, keepdims=True)
        t = dh_v * g_ref[...]
        dx_ref[...] = dres_ref[...] + r * (t - xh * jnp.mean(t * xh, axis=-1, keepdims=True))

        @pl.when(i == 0)
        def _():
            dg_ref[...] = dg

        @pl.when(i > 0)
        def _():
            dg_ref[...] += dg

    views, specs = _folded_views(dhf, tb)
    row = pl.BlockSpec((tb, D), lambda i: (i, 0))
    vec = pl.BlockSpec((1, D), lambda i: (0, 0))
    return _pcall(body, name=name, grid=(S // tb,), in_specs=[row, vec] + specs + [row], out_specs=[row, vec],
                  out_shape=[jax.ShapeDtypeStruct((S, D), F32), jax.ShapeDtypeStruct((1, D), F32)],
                  scratch_shapes=[pltpu.VMEM((D // LANES, tb, LANES), F32)],
                  compiler_params=_params(("arbitrary",)))(x, g, *views, dres)


def _loss_bwd(name, x, g, tgt, tr=512):
    S, D = x.shape

    def body(x_ref, g_ref, t_ref, loss_ref, dx_ref, dg_ref):
        i = pl.program_id(0)
        xf = x_ref[...]
        r = lax.rsqrt(jnp.mean(xf * xf, axis=-1, keepdims=True) + RMS_EPS)
        xh = xf * r
        gv = g_ref[...]
        e = xh * gv - t_ref[...]
        lp = 0.5 * jnp.sum(jnp.mean(e * e, axis=-1, keepdims=True), axis=0, keepdims=True)
        dy = e * (1.0 / D)
        dg = jnp.sum(dy * xh, axis=0, keepdims=True)
        t = dy * gv
        dx_ref[...] = r * (t - xh * jnp.mean(t * xh, axis=-1, keepdims=True))

        @pl.when(i == 0)
        def _():
            dg_ref[...] = dg
            loss_ref[...] = lp

        @pl.when(i > 0)
        def _():
            dg_ref[...] += dg
            loss_ref[...] += lp

    row = pl.BlockSpec((tr, D), lambda i: (i, 0))
    vec = pl.BlockSpec((1, D), lambda i: (0, 0))
    one = pl.BlockSpec((1, 1), lambda i: (0, 0))
    return _pcall(body, name=name, grid=(S // tr,), in_specs=[row, vec, row], out_specs=[one, row, vec],
                  out_shape=[jax.ShapeDtypeStruct((1, 1), F32), jax.ShapeDtypeStruct((S, D), F32),
                             jax.ShapeDtypeStruct((1, D), F32)],
                  compiler_params=_params(("arbitrary",)))(x, g, tgt)


def _ffn_up(name, h, w_gu, tm=512):
    S, D = h.shape
    W = FF_SHARD

    def body(h_ref, wg_ref, wu_ref, gu_ref, act_ref):
        hv = h_ref[...]
        gate = _dot(hv, wg_ref[...], "nn")
        up = _dot(hv, wu_ref[...], "nn")
        gu_ref[0] = gate
        gu_ref[1] = up
        sig = 1.0 / (1.0 + jnp.exp(-gate))
        act_ref[...] = ((gate * sig) * up).astype(act_ref.dtype)

    return _pcall(
        body, name=name, grid=(2, S // tm),
        in_specs=[pl.BlockSpec((tm, D), lambda j, i: (i, 0)),
                  pl.BlockSpec((None, D, W), lambda j, i: (j, 0, 0)),
                  pl.BlockSpec((None, D, W), lambda j, i: (j + 2, 0, 0))],
        out_specs=[pl.BlockSpec((2, tm, W), lambda j, i: (0, i, j)), pl.BlockSpec((tm, W), lambda j, i: (i, j))],
        out_shape=[jax.ShapeDtypeStruct((2, S, D_FF), F32), jax.ShapeDtypeStruct((S, D_FF), BF16)],
        compiler_params=_params(("parallel", "parallel")))(h, w_gu, w_gu)


def _ffn_down_bwd(name, dx, w_down, gu, tm=512):
    S, D = dx.shape
    W = FF_SHARD

    def body(dx_ref, w_ref, gu_ref, dgu_ref):
        dact = _dot(dx_ref[...].astype(BF16), w_ref[...], "nt")
        gate = gu_ref[0]
        up = gu_ref[1]
        sig = 1.0 / (1.0 + jnp.exp(-gate))
        silu = gate * sig
        dgu_ref[0] = (dact * up * (sig * (1.0 + gate * (1.0 - sig)))).astype(dgu_ref.dtype)
        dgu_ref[1] = (dact * silu).astype(dgu_ref.dtype)

    blk = pl.BlockSpec((2, tm, W), lambda j, i: (0, i, j))
    return _pcall(
        body, name=name, grid=(2, S // tm),
        in_specs=[pl.BlockSpec((tm, D), lambda j, i: (i, 0)), pl.BlockSpec((W, D), lambda j, i: (j, 0)), blk],
        out_specs=blk, out_shape=jax.ShapeDtypeStruct((2, S, D_FF), BF16),
        compiler_params=_params(("parallel", "parallel")))(dx, w_down, gu)


def _ffn_dw_up(name, h, dgu, tk=1024):
    S, D = h.shape
    W = FF_SHARD
    return _mm(name, "tn", (N_CHIPS, S // tk), h, pl.BlockSpec((tk, D), lambda s, k: (k, 0)),
               dgu, pl.BlockSpec((None, tk, W), lambda s, k: (s // 2, k, s % 2)),
               jax.ShapeDtypeStruct((N_CHIPS, D, W), BF16), pl.BlockSpec((None, D, W), lambda s, k: (s, 0, 0)),
               (D, W))


def _ffn_dh(name, dgu, w_gu, tm=1024):
    S = dgu.shape[1]
    W = FF_SHARD
    return _mm(name, "nt", (S // tm, N_CHIPS), dgu, pl.BlockSpec((None, tm, W), lambda i, k: (k // 2, i, k % 2)),
               w_gu, pl.BlockSpec((None, D_MODEL, W), lambda i, k: (k, 0, 0)),
               jax.ShapeDtypeStruct((S, D_MODEL), F32), pl.BlockSpec((tm, D_MODEL), lambda i, k: (i, 0)),
               (tm, D_MODEL))


FOLD_TOKENS = 1024
LANES = 128


def _lane_chunks(width):
    return [slice(c * LANES, (c + 1) * LANES) for c in range(width // LANES)]


def _strided_rows(r, n, dil):
    return pl.ds(r, n) if dil == 1 else pl.ds(r, n, stride=dil)


def _norm_fold_qkv(name, g, x, gain, w_qkv, hf, qkv):
    S, D = x.shape
    dil = ATTN_DILATIONS[g]
    n = FOLD_TOKENS // dil
    seg = S // dil
    per_chip = QKV_SHARD // QKV_TILE
    per_group = 3 * D_MODEL // QKV_TILE

    def body(*refs):
        x_ref, g_ref, w_ref = refs[:3]
        hf_ref, qkv_ref, a_ref, xc_ref = refs[-4:]
        j = pl.program_id(1)

        @pl.when(j == 0)
        def _():
            xf = x_ref[...]
            inv = lax.rsqrt(jnp.mean(xf * xf, axis=-1, keepdims=True) + RMS_EPS)
            h = (xf * inv) * g_ref[...]
            if dil == 1:
                a_ref[...] = h.astype(BF16)
            else:
                for c, cols in enumerate(_lane_chunks(D)):
                    xc_ref[c] = h[:, cols]
                for r in range(dil):
                    for c, cols in enumerate(_lane_chunks(D)):
                        a_ref[r * n:(r + 1) * n, cols] = xc_ref[c, _strided_rows(r, n, dil), :].astype(BF16)
            for r in range(dil):
                hf_ref[r] = a_ref[r * n:(r + 1) * n, :]

        res = _dot(a_ref[...], w_ref[...], "nn")
        for r in range(dil):
            qkv_ref[r] = res[r * n:(r + 1) * n].astype(qkv_ref.dtype)

    in_specs = [pl.BlockSpec((FOLD_TOKENS, D), lambda i, j: (i, 0)), pl.BlockSpec((1, D), lambda i, j: (0, 0)),
                pl.BlockSpec((None, D, QKV_TILE),
                             lambda i, j: ((per_group * g + j) // per_chip, 0, (per_group * g + j) % per_chip))]
    operands = [x, gain, w_qkv]
    aliases = {}
    if hf is not None:
        in_specs += [pl.BlockSpec(memory_space=pl.ANY)] * 2
        operands += [hf.reshape(N_GROUPS * dil, seg, D), qkv.reshape(N_GROUPS * dil, seg, 3 * D)]
        aliases = {3: 0, 4: 1}
    hf, qkv = _pcall(
        body, name=name, grid=(S // FOLD_TOKENS, per_group), in_specs=in_specs,
        out_specs=[pl.BlockSpec((dil, n, D), lambda i, j: (g, i, 0)),
                   pl.BlockSpec((dil, n, QKV_TILE), lambda i, j: (g, i, j))],
        out_shape=[jax.ShapeDtypeStruct((N_GROUPS * dil, seg, D), BF16),
                   jax.ShapeDtypeStruct((N_GROUPS * dil, seg, 3 * D), BF16)],
        scratch_shapes=[pltpu.VMEM((FOLD_TOKENS, D), BF16), pltpu.VMEM((D // LANES, FOLD_TOKENS, LANES), F32)],
        input_output_aliases=aliases,
        compiler_params=_params(("arbitrary", "arbitrary")))(*operands)
    return hf.reshape(N_GROUPS, S, D), qkv.reshape(N_GROUPS, S, 3 * D)


def _qkv_dw(name, hf, dqkv, tk=1024):
    S = hf.shape[1]
    per_chip = QKV_SHARD // QKV_TILE
    per_group = 3 * D_MODEL // QKV_TILE
    return _mm(name, "tn", (N_GROUPS, per_group, S // tk),
               hf, pl.BlockSpec((None, tk, D_MODEL), lambda g, j, k: (g, k, 0)),
               dqkv, pl.BlockSpec((None, tk, QKV_TILE), lambda g, j, k: (g, k, j)),
               jax.ShapeDtypeStruct((N_CHIPS, D_MODEL, QKV_SHARD), BF16),
               pl.BlockSpec((None, D_MODEL, QKV_TILE),
                            lambda g, j, k: ((per_group * g + j) // per_chip, 0, (per_group * g + j) % per_chip)),
               (D_MODEL, QKV_TILE))


def _qkv_dh(name, dqkv, w_qkv, tm=1024):
    S = dqkv.shape[1]
    per_chip = QKV_SHARD // QKV_TILE
    per_group = 3 * D_MODEL // QKV_TILE
    return _mm(name, "nt", (N_GROUPS, S // tm, per_group),
               dqkv, pl.BlockSpec((None, tm, QKV_TILE), lambda g, i, k: (g, i, k)),
               w_qkv, pl.BlockSpec((None, D_MODEL, QKV_TILE),
                                   lambda g, i, k: ((per_group * g + k) // per_chip, 0, (per_group * g + k) % per_chip)),
               jax.ShapeDtypeStruct((N_GROUPS, S, D_MODEL), F32),
               pl.BlockSpec((None, tm, D_MODEL), lambda g, i, k: (g, i, 0)), (tm, D_MODEL))


def _alibi_coef(g, h):
    vals = [-(2.0 ** (-8.0 * (gg * N_HEADS + h + 1) / (N_GROUPS * N_HEADS))) * ATTN_DILATIONS[gg]
            for gg in range(N_GROUPS)]
    return jnp.where(g == 0, vals[0], jnp.where(g == 1, vals[1], vals[2])).astype(F32)


def _blocks_per_segment(g, S):
    return jnp.right_shift(S // Q_BLOCK, 2 * g)


def _band_bias(pen):
    row = lax.broadcasted_iota(jnp.int32, (Q_BLOCK, 2 * Q_BLOCK), 0)
    col = lax.broadcasted_iota(jnp.int32, (Q_BLOCK, 2 * Q_BLOCK), 1)
    dist = Q_BLOCK + row - col
    valid = jnp.logical_and(dist >= 0, dist <= Q_BLOCK)
    base = jnp.where(valid, jnp.where(col < Q_BLOCK, pen, 0.0), NEG).astype(F32)
    return base, dist.astype(F32)


def _skewed(n_units, stages):
    state = {}
    for step in range(n_units + len(stages) - 1):
        for s, stage in enumerate(stages):
            u = step - s
            if 0 <= u < n_units:
                state[u] = stage(u, state.get(u))
                if s == len(stages) - 1:
                    del state[u]


def _attn_fwd(name, qkv, tq=512):
    S = qkv.shape[1]
    nqb = tq // Q_BLOCK
    scale = HEAD_DIM ** -0.5

    def body(q_ref, k_ref, kp_ref, v_ref, vp_ref, o_ref, lse_ref, kf_ref, vf_ref):
        g = pl.program_id(0)
        i = pl.program_id(1)
        nb = _blocks_per_segment(g, S)
        kf_ref[:Q_BLOCK] = kp_ref[...]
        kf_ref[Q_BLOCK:] = k_ref[...]
        vf_ref[:Q_BLOCK] = vp_ref[...]
        vf_ref[Q_BLOCK:] = v_ref[...]
        coefs = [_alibi_coef(g, h) for h in range(N_HEADS)]
        units = [(b, h) for b in range(nqb) for h in range(N_HEADS)]
        bias = {}

        def scores(u, _):
            b, h = units[u]
            if b not in bias:
                pen = jnp.where((i * nqb + b) % nb != 0, 0.0, NEG).astype(F32)
                bias.clear()
                bias[b] = _band_bias(pen)
            base, dist = bias[b]
            cols = slice(h * HEAD_DIM, (h + 1) * HEAD_DIM)
            q = q_ref[b * Q_BLOCK:(b + 1) * Q_BLOCK, cols]
            kk = kf_ref[b * Q_BLOCK:(b + 2) * Q_BLOCK, cols]
            return _dot(q, kk, "nt") * scale + (coefs[h] * dist + base)

        def softmax(u, s):
            m = jnp.max(s, axis=-1, keepdims=True)
            p = jnp.exp(s - m)
            den = jnp.sum(p, axis=-1, keepdims=True)
            return (p * (1.0 / den)).astype(BF16), m + jnp.log(den)

        def output(u, st):
            b, h = units[u]
            pn, lse = st
            cols = slice(h * HEAD_DIM, (h + 1) * HEAD_DIM)
            rows = slice(b * Q_BLOCK, (b + 1) * Q_BLOCK)
            o_ref[rows, cols] = _dot(pn, vf_ref[b * Q_BLOCK:(b + 2) * Q_BLOCK, cols], "nn")
            lse_ref[rows, h:h + 1] = lse

        _skewed(len(units), [scores, softmax, output])

    main = lambda c: pl.BlockSpec((None, tq, D_MODEL), lambda g, i: (g, i, c))
    prev = lambda c: pl.BlockSpec((None, Q_BLOCK, D_MODEL), lambda g, i: (g, jnp.maximum(i * nqb - 1, 0), c))
    return _pcall(
        body, name=name, grid=(N_GROUPS, S // tq),
        in_specs=[main(0), main(1), prev(1), main(2), prev(2)],
        out_specs=[pl.BlockSpec((None, tq, D_MODEL), lambda g, i: (g, i, 0)),
                   pl.BlockSpec((None, tq, N_HEADS), lambda g, i: (g, i, 0))],
        out_shape=[jax.ShapeDtypeStruct((N_GROUPS, S, D_MODEL), F32),
                   jax.ShapeDtypeStruct((N_GROUPS, S, N_HEADS), F32)],
        scratch_shapes=[pltpu.VMEM((tq + Q_BLOCK, D_MODEL), BF16), pltpu.VMEM((tq + Q_BLOCK, D_MODEL), BF16)],
        compiler_params=_params(("parallel", "parallel")))(qkv, qkv, qkv, qkv, qkv)


def _attn_bwd(name, qkv, do, lse, dl, tq=512):
    S = qkv.shape[1]
    nqb = tq // Q_BLOCK
    n_blocks = S // Q_BLOCK
    scale = HEAD_DIM ** -0.5
    KEYS = 2 * Q_BLOCK

    def body(q_ref, k_ref, v_ref, kp_ref, vp_ref, qn_ref, do_ref, don_ref, lse_ref, lsen_ref, dl_ref, dln_ref,
             out_ref, kf_ref, vf_ref, dk_ref, dv_ref):
        g = pl.program_id(0)
        i = pl.program_id(1)
        nb = _blocks_per_segment(g, S)
        kf_ref[:Q_BLOCK] = kp_ref[...]
        kf_ref[Q_BLOCK:] = k_ref[...]
        vf_ref[:Q_BLOCK] = vp_ref[...]
        vf_ref[Q_BLOCK:] = v_ref[...]
        coefs = [_alibi_coef(g, h) for h in range(N_HEADS)]
        units = [(h, b) for h in range(N_HEADS) for b in range(nqb + 1)]
        bias = {}

        def band(b):
            if b not in bias:
                blk = i * nqb + b
                ok = blk % nb != 0
                if b == nqb:
                    ok = jnp.logical_and(ok, blk < n_blocks)
                bias[b] = _band_bias(jnp.where(ok, 0.0, NEG).astype(F32))
            return bias[b]

        def operands(h, b):
            cols = slice(h * HEAD_DIM, (h + 1) * HEAD_DIM)
            if b == nqb:
                keys = slice(tq, tq + Q_BLOCK)
                return (qn_ref[:, cols], don_ref[:, cols], lsen_ref[:, h:h + 1], dln_ref[:, h:h + 1],
                        kf_ref[keys, cols], vf_ref[keys, cols])
            rows = slice(b * Q_BLOCK, (b + 1) * Q_BLOCK)
            keys = slice(b * Q_BLOCK, b * Q_BLOCK + KEYS)
            return (q_ref[rows, cols], do_ref[rows, cols], lse_ref[rows, h:h + 1], dl_ref[rows, h:h + 1],
                    kf_ref[keys, cols], vf_ref[keys, cols])

        def probs(u, _):
            h, b = units[u]
            q, do_b, lse_b, dl_b, kk, vv = operands(h, b)
            base, dist = band(b)
            if b == nqb:
                base, dist = base[:, :Q_BLOCK], dist[:, :Q_BLOCK]
            p = jnp.exp(_dot(q, kk, "nt") * scale + (coefs[h] * dist + base) - lse_b)
            return p, _dot(do_b, vv, "nt")

        def dscores(u, st):
            h, b = units[u]
            p, dp = st
            dl_b = operands(h, b)[3]
            return ((p * (dp - dl_b)) * scale).astype(BF16), p.astype(BF16)

        def grads(u, st):
            h, b = units[u]
            ds, pb = st
            q, do_b, _, _, kk, _ = operands(h, b)
            cols = slice(h * HEAD_DIM, (h + 1) * HEAD_DIM)
            if b < nqb:
                out_ref[b * Q_BLOCK:(b + 1) * Q_BLOCK, cols] = _dot(ds, kk, "nn").astype(out_ref.dtype)
            if b == 0:
                dk_ref[:Q_BLOCK] = _dot(ds[:, Q_BLOCK:], q, "tn")
                dv_ref[:Q_BLOCK] = _dot(pb[:, Q_BLOCK:], do_b, "tn")
                return None
            dk = _dot(ds, q, "tn")
            dv = _dot(pb, do_b, "tn")
            before = slice((b - 1) * Q_BLOCK, b * Q_BLOCK)
            dk_ref[before] += dk[:Q_BLOCK]
            dv_ref[before] += dv[:Q_BLOCK]
            if b < nqb:
                own = slice(b * Q_BLOCK, (b + 1) * Q_BLOCK)
                dk_ref[own] = dk[Q_BLOCK:]
                dv_ref[own] = dv[Q_BLOCK:]
            else:
                out_ref[:, D_MODEL + h * HEAD_DIM:D_MODEL + (h + 1) * HEAD_DIM] = dk_ref[...].astype(out_ref.dtype)
                out_ref[:, 2 * D_MODEL + h * HEAD_DIM:2 * D_MODEL + (h + 1) * HEAD_DIM] = (
                    dv_ref[...].astype(out_ref.dtype))
            return None

        _skewed(len(units), [probs, dscores, grads])

    nxt_blk = lambda i: jnp.minimum((i + 1) * nqb, n_blocks - 1)
    main = lambda c: pl.BlockSpec((None, tq, D_MODEL), lambda g, i: (g, i, c))
    prev = lambda c: pl.BlockSpec((None, Q_BLOCK, D_MODEL), lambda g, i: (g, jnp.maximum(i * nqb - 1, 0), c))
    row = pl.BlockSpec((None, tq, D_MODEL), lambda g, i: (g, i, 0))
    row_n = pl.BlockSpec((None, Q_BLOCK, D_MODEL), lambda g, i: (g, nxt_blk(i), 0))
    col = pl.BlockSpec((None, tq, N_HEADS), lambda g, i: (g, i, 0))
    col_n = pl.BlockSpec((None, Q_BLOCK, N_HEADS), lambda g, i: (g, nxt_blk(i), 0))
    return _pcall(
        body, name=name, grid=(N_GROUPS, S // tq),
        in_specs=[main(0), main(1), main(2), prev(1), prev(2), row_n, row, row_n, col, col_n, col, col_n],
        out_specs=pl.BlockSpec((None, tq, 3 * D_MODEL), lambda g, i: (g, i, 0)),
        out_shape=jax.ShapeDtypeStruct((N_GROUPS, S, 3 * D_MODEL), BF16),
        scratch_shapes=[pltpu.VMEM((tq + Q_BLOCK, D_MODEL), BF16), pltpu.VMEM((tq + Q_BLOCK, D_MODEL), BF16),
                        pltpu.VMEM((tq, HEAD_DIM), F32), pltpu.VMEM((tq, HEAD_DIM), F32)],
        compiler_params=_params(("parallel", "parallel")))(qkv, qkv, qkv, qkv, qkv, qkv, do, do, lse, lse, dl, dl)


def _group_weights(lse_ref):
    l0, l1, l2 = lse_ref[0], lse_ref[1], lse_ref[2]
    m = jnp.maximum(jnp.maximum(l0, l1), l2)
    e = [jnp.exp(l0 - m), jnp.exp(l1 - m), jnp.exp(l2 - m)]
    den = e[0] + e[1] + e[2]
    return [ei / den for ei in e]


MERGE_TOKENS = 512


def _folded_views(a, tb):
    _, S, C = a.shape
    views, specs = [], []
    for g, dil in enumerate(ATTN_DILATIONS):
        views.append(a.reshape(N_GROUPS * dil, S // dil, C))
        specs.append(pl.BlockSpec((dil, tb // dil, C), lambda i, g=g: (g, i, 0)))
    return views, specs


def _unfold_into(dst_ref, folded_refs):
    for g, (dil, ref) in enumerate(zip(ATTN_DILATIONS, folded_refs)):
        n = ref.shape[1]
        for r in range(dil):
            for c, cols in enumerate(_lane_chunks(ref.shape[2])):
                dst_ref[g, c, _strided_rows(r, n, dil), :] = ref[r, :, cols]


def _merge_fwd(name, o, lse, tb=MERGE_TOKENS):
    S = o.shape[1]

    def body(o0_ref, o1_ref, o2_ref, lse_ref, out_ref, o_ref):
        _unfold_into(o_ref, (o0_ref, o1_ref, o2_ref))
        w = _group_weights(lse_ref)
        for h in range(N_HEADS):
            cols = slice(h * HEAD_DIM, (h + 1) * HEAD_DIM)
            acc = w[0][:, h:h + 1] * o_ref[0, h]
            for gg in range(1, N_GROUPS):
                acc = acc + w[gg][:, h:h + 1] * o_ref[gg, h]
            out_ref[:, cols] = acc.astype(out_ref.dtype)

    views, specs = _folded_views(o, tb)
    return _pcall(body, name=name, grid=(S // tb,),
                  in_specs=specs + [pl.BlockSpec((N_GROUPS, tb, N_HEADS), lambda i: (0, i, 0))],
                  out_specs=pl.BlockSpec((tb, D_MODEL), lambda i: (i, 0)),
                  out_shape=jax.ShapeDtypeStruct((S, D_MODEL), BF16),
                  scratch_shapes=[pltpu.VMEM((N_GROUPS, N_HEADS, tb, HEAD_DIM), F32)],
                  compiler_params=_params(("parallel",)))(*views, lse)


def _merge_bwd(name, d_out, o, lse, tb=MERGE_TOKENS):
    S = o.shape[1]
    n_chunks = sum(ATTN_DILATIONS)

    def body(d_ref, o0_ref, o1_ref, o2_ref, lse_ref, do_hbm, dl_ref, o_ref, dt_ref, df_ref, sems):
        i = pl.program_id(0)
        _unfold_into(o_ref, (o0_ref, o1_ref, o2_ref))
        w = _group_weights(lse_ref)
        for h in range(N_HEADS):
            cols = slice(h * HEAD_DIM, (h + 1) * HEAD_DIM)
            dv = d_ref[:, cols]
            merged = w[0][:, h:h + 1] * o_ref[0, h]
            for gg in range(1, N_GROUPS):
                merged = merged + w[gg][:, h:h + 1] * o_ref[gg, h]
            dsum = jnp.sum(dv * merged, axis=-1, keepdims=True)
            for gg in range(N_GROUPS):
                wg = w[gg][:, h:h + 1]
                dt_ref[gg, h] = wg * dv
                dl_ref[gg, :, h:h + 1] = wg * dsum
        copies = []
        for gg, dil in enumerate(ATTN_DILATIONS):
            n = tb // dil
            for r in range(dil):
                for h, cols in enumerate(_lane_chunks(D_MODEL)):
                    df_ref[gg, r * n:(r + 1) * n, cols] = (
                        dt_ref[gg, h, _strided_rows(r, n, dil), :].astype(df_ref.dtype))
                cp = pltpu.make_async_copy(df_ref.at[gg, pl.ds(r * n, n)],
                                           do_hbm.at[gg, pl.ds(r * (S // dil) + i * n, n)], sems.at[len(copies)])
                cp.start()
                copies.append(cp)
        for cp in copies:
            cp.wait()

    views, specs = _folded_views(o, tb)
    small = pl.BlockSpec((N_GROUPS, tb, N_HEADS), lambda i: (0, i, 0))
    return _pcall(body, name=name, grid=(S // tb,),
                  in_specs=[pl.BlockSpec((tb, D_MODEL), lambda i: (i, 0))] + specs + [small],
                  out_specs=[pl.BlockSpec(memory_space=pl.ANY), small],
                  out_shape=[jax.ShapeDtypeStruct((N_GROUPS, S, D_MODEL), BF16),
                             jax.ShapeDtypeStruct((N_GROUPS, S, N_HEADS), F32)],
                  scratch_shapes=[pltpu.VMEM((N_GROUPS, N_HEADS, tb, HEAD_DIM), F32),
                                  pltpu.VMEM((N_GROUPS, N_HEADS, tb, HEAD_DIM), F32),
                                  pltpu.VMEM((N_GROUPS, tb, D_MODEL), BF16), pltpu.SemaphoreType.DMA((n_chunks,))],
                  compiler_params=_params(("arbitrary",)))(d_out, *views, lse)


def _shift_rows(a, k):
    return pltpu.roll(a, k % a.shape[0], axis=0)


def _pool_level(g, levels):
    return jnp.where(g == 0, levels[0], jnp.where(g == 1, levels[1], jnp.where(g == 2, levels[2], levels[3])))


def _pool_fwd(name, u, w_group, scale, x_in, tr=1024):
    S, D = u.shape
    H = POOL_HALO

    def body(u_ref, halo_ref, w_ref, sc_ref, x_ref, y_ref, z_ref, xo_ref):
        g = pl.program_id(0)
        i = pl.program_id(1)
        uv = u_ref[...]
        halo = jnp.where(i > 0, halo_ref[...], 0.0)
        s = jnp.concatenate([halo, uv], axis=0)
        levels = []
        for k in (1, 2, 4, 8):
            s = s + _shift_rows(s, k)
            levels.append(s[H:])
        t = i * tr + lax.broadcasted_iota(jnp.int32, (tr, 1), 0)
        cnt = jnp.minimum(t + 1, jnp.left_shift(2, g)).astype(F32)
        y = _pool_level(g, levels) / cnt - uv
        yb = y.astype(BF16)
        z = _dot(yb, w_ref[...], "nn")
        y_ref[...] = yb
        z_ref[...] = z
        xo_ref[...] = x_ref[...] + z * sc_ref[...]

    blk = pl.BlockSpec((tr, POOL_DIM), lambda g, i: (i, g))
    return _pcall(
        body, name=name, grid=(D // POOL_DIM, S // tr),
        in_specs=[blk, pl.BlockSpec((H, POOL_DIM), lambda g, i: (jnp.maximum(i * (tr // H) - 1, 0), g)),
                  pl.BlockSpec((None, POOL_DIM, POOL_DIM), lambda g, i: (g, 0, 0)),
                  pl.BlockSpec((1, POOL_DIM), lambda g, i: (0, g)), blk],
        out_specs=[blk, blk, blk],
        out_shape=[jax.ShapeDtypeStruct((S, D), BF16), jax.ShapeDtypeStruct((S, D), F32),
                   jax.ShapeDtypeStruct((S, D), F32)],
        compiler_params=_params(("parallel", "parallel")))(u, u, w_group, scale, x_in)


def _pool_bwd(name, d_out, z, y, scale, w_group, tr=1024):
    S, D = d_out.shape
    H = POOL_HALO

    def body(d_ref, dn_ref, z_ref, y_ref, sc_ref, w_ref, du_ref, dw_ref, dsc_ref):
        g = pl.program_id(0)
        i = pl.program_id(1)
        dv = d_ref[...]
        dz = jnp.concatenate([dv, dn_ref[...]], axis=0) * sc_ref[...]
        dzb = dz.astype(BF16)
        dy = _dot(dzb, w_ref[...], "nt")
        t = i * tr + lax.broadcasted_iota(jnp.int32, (tr + H, 1), 0)
        cnt = jnp.minimum(t + 1, jnp.left_shift(2, g)).astype(F32)
        s = jnp.where(t < S, dy / cnt, 0.0)
        levels = []
        for k in (1, 2, 4, 8):
            s = s + _shift_rows(s, -k)
            levels.append(s[:tr])
        du_ref[...] = (_pool_level(g, levels) - dy[:tr]).astype(du_ref.dtype)
        dw = _dot(y_ref[...], dzb[:tr], "tn")
        dsc = jnp.sum(dv * z_ref[...], axis=0, keepdims=True)

        @pl.when(i == 0)
        def _():
            dw_ref[...] = dw
            dsc_ref[...] = dsc

        @pl.when(i > 0)
        def _():
            dw_ref[...] += dw
            dsc_ref[...] += dsc

    blk = pl.BlockSpec((tr, POOL_DIM), lambda g, i: (i, g))
    vec = pl.BlockSpec((1, POOL_DIM), lambda g, i: (0, g))
    mat = pl.BlockSpec((None, POOL_DIM, POOL_DIM), lambda g, i: (g, 0, 0))
    nxt = pl.BlockSpec((H, POOL_DIM), lambda g, i: (jnp.minimum((i + 1) * (tr // H), S // H - 1), g))
    return _pcall(
        body, name=name, grid=(D // POOL_DIM, S // tr), in_specs=[blk, nxt, blk, blk, vec, mat],
        out_specs=[blk, mat, vec],
        out_shape=[jax.ShapeDtypeStruct((S, D), BF16), jax.ShapeDtypeStruct((D // POOL_DIM, POOL_DIM, POOL_DIM), F32),
                   jax.ShapeDtypeStruct((1, D), F32)],
        compiler_params=_params(("parallel", "arbitrary")))(d_out, d_out, z, y, scale, w_group)


def _row_tile(rows, cols, target_bytes=1 << 20):
    best = rows
    for tr in range(8, rows + 1, 8):
        if rows % tr == 0 and tr * cols * 4 <= target_bytes:
            best = tr
    return best if best * cols * 4 <= 4 * target_bytes else rows


def _adamw(name, w, g, m, v):
    R, C = w.shape
    tr = _row_tile(R, C) if R % 8 == 0 else R

    def body(w_ref, g_ref, m_ref, v_ref, d_ref, mo_ref, vo_ref):
        gv = g_ref[...]
        m2 = ADAM_B1 * m_ref[...] + (1.0 - ADAM_B1) * gv
        v2 = ADAM_B2 * v_ref[...] + (1.0 - ADAM_B2) * (gv * gv)
        m_hat = m2 / (1.0 - ADAM_B1 ** ADAM_STEP)
        v_hat = v2 / (1.0 - ADAM_B2 ** ADAM_STEP)
        d_ref[...] = -ADAM_LR * (m_hat / (jnp.sqrt(v_hat) + ADAM_EPS) + ADAM_WD * w_ref[...])
        mo_ref[...] = m2
        vo_ref[...] = v2

    blk = pl.BlockSpec((tr, C), lambda i: (i, 0))
    sds = jax.ShapeDtypeStruct((R, C), F32)
    return _pcall(body, name=name, grid=(R // tr,), in_specs=[blk] * 4, out_specs=[blk] * 3, out_shape=[sds] * 3,
                  compiler_params=_params(("parallel",)))(w, g, m, v)


def _sum_leading(name, a, out_dtype):
    n, R, C = a.shape
    tr = _row_tile(R, C) if R % 16 == 0 else R
    if tr % 16:
        tr = R

    def body(a_ref, o_ref):
        acc = a_ref[0].astype(F32)
        for j in range(1, n):
            acc = acc + a_ref[j].astype(F32)
        o_ref[...] = acc.astype(o_ref.dtype)

    return _pcall(body, name=name, grid=(R // tr,), in_specs=[pl.BlockSpec((n, tr, C), lambda i: (0, i, 0))],
                  out_specs=pl.BlockSpec((tr, C), lambda i: (i, 0)), out_shape=jax.ShapeDtypeStruct((R, C), out_dtype),
                  compiler_params=_params(("parallel",)))(a)


def _cast_into_slot(name, w, chip, dtype):
    R, C = w.shape
    tr = _row_tile(R, C)
    if tr % 16:
        tr = R

    def body(c_ref, w_ref, o_ref):
        o_ref[...] = w_ref[...].astype(o_ref.dtype)

    grid_spec = pltpu.PrefetchScalarGridSpec(
        num_scalar_prefetch=1, grid=(R // tr,), in_specs=[pl.BlockSpec((tr, C), lambda i, c_ref: (i, 0))],
        out_specs=pl.BlockSpec((None, tr, C), lambda i, c_ref: (c_ref[0], i, 0)))
    return _pcall(body, name=name, grid_spec=grid_spec, out_shape=jax.ShapeDtypeStruct((N_CHIPS, R, C), dtype),
                  compiler_params=_params(("parallel",)))(chip, w)


def _owner_sum(name, mine, landed, chip, core, out=None, layer=None):
    _, half, C = mine.shape
    tr = _row_tile(half, C)
    if tr % 16:
        tr = half
    nrt = half // tr
    has_out = out is not None

    def body(*refs):
        m_ref, l_ref, o_ref = refs[2], refs[3], refs[-1]
        acc = m_ref[...].astype(F32)
        for j in range(3):
            acc = acc + l_ref[j].astype(F32)
        o_ref[...] = acc

    if layer is None:
        out_shape = jax.ShapeDtypeStruct((2 * half, C), F32)
        o_spec = pl.BlockSpec((tr, C), lambda i, ch, co: (co[0] * nrt + i, 0))
    else:
        out_shape = jax.ShapeDtypeStruct((2, 2 * half, C), F32)
        o_spec = pl.BlockSpec((None, tr, C), lambda i, ch, co: (layer, co[0] * nrt + i, 0))
    in_specs = [pl.BlockSpec((None, tr, C), lambda i, ch, co: (ch[0], i, 0)),
                pl.BlockSpec((3, tr, C), lambda i, ch, co: (0, i, 0))]
    operands = [chip, core, mine, landed]
    aliases = {}
    if has_out:
        in_specs.append(ANY)
        operands.append(out)
        aliases = {4: 0}
    grid_spec = pltpu.PrefetchScalarGridSpec(num_scalar_prefetch=2, grid=(nrt,), in_specs=in_specs, out_specs=o_spec)
    return _pcall(body, name=name, grid_spec=grid_spec, out_shape=out_shape, input_output_aliases=aliases,
                  compiler_params=_params(("parallel",)))(*operands)


def _add_my_half(name, p, q, core):
    n, R, C = p.shape
    half = R // 2

    def body(c_ref, p_ref, q_ref, o_ref):
        o_ref[...] = (p_ref[...].astype(F32) + q_ref[...].astype(F32)).astype(o_ref.dtype)

    grid_spec = pltpu.PrefetchScalarGridSpec(
        num_scalar_prefetch=1, grid=(n,),
        in_specs=[pl.BlockSpec((None, half, C), lambda s, c_ref: (s, c_ref[0], 0)),
                  pl.BlockSpec((None, half, C), lambda s, c_ref: (s, 0, 0))],
        out_specs=pl.BlockSpec((None, half, C), lambda s, c_ref: (s, 0, 0)))
    return _pcall(body, name=name, grid_spec=grid_spec, out_shape=jax.ShapeDtypeStruct((n, half, C), BF16),
                  compiler_params=_params(("parallel",)))(core, p, q)


ANY = pl.BlockSpec(memory_space=pl.ANY)


def _place():
    x, y, c = lax.axis_index("x"), lax.axis_index("y"), lax.axis_index("c")
    other_chips = [(1 - x, y), (x, 1 - y), (1 - x, 1 - y)]
    return x, y, c, other_chips


def _gather_weights(name, bufs, split):
    n = len(bufs)

    def body(*refs):
        outs = refs[n:2 * n]
        ici_send, ici_recv, d2d_send, d2d_recv = refs[2 * n:]
        x, y, c, chips = _place()
        me = 2 * x + y
        sibling = (x, y, 1 - c)

        def piece(t, chip, core_half):
            if not split[t]:
                return outs[t].at[chip]
            half = outs[t].shape[1] // 2
            return outs[t].at[chip, pl.ds(core_half * half, half)]

        sends = []
        for t in range(n):
            for j, (px, py) in enumerate(chips):
                sends.append(pltpu.make_async_remote_copy(
                    src_ref=piece(t, me, c), dst_ref=piece(t, me, c), send_sem=ici_send.at[3 * t + j],
                    recv_sem=ici_recv.at[3 * t + j], device_id=(px, py, c), device_id_type=MESH))
        for cp in sends:
            cp.start()
        passed = []
        for t in range(n):
            for j, (px, py) in enumerate(chips):
                landed = piece(t, 2 * px + py, c)
                pltpu.make_async_remote_copy(
                    src_ref=landed, dst_ref=landed, send_sem=ici_send.at[3 * t + j],
                    recv_sem=ici_recv.at[3 * t + j], device_id=(px, py, c), device_id_type=MESH).wait_recv()
                if split[t]:
                    fwd = pltpu.make_async_remote_copy(
                        src_ref=landed, dst_ref=landed, send_sem=d2d_send.at[3 * t + j],
                        recv_sem=d2d_recv.at[3 * t + j], device_id=sibling, device_id_type=MESH)
                    fwd.start()
                    passed.append(fwd)
        for t in range(n):
            if split[t]:
                for j, (px, py) in enumerate(chips):
                    theirs = piece(t, 2 * px + py, 1 - c)
                    pltpu.make_async_remote_copy(
                        src_ref=theirs, dst_ref=theirs, send_sem=d2d_send.at[3 * t + j],
                        recv_sem=d2d_recv.at[3 * t + j], device_id=sibling, device_id_type=MESH).wait_recv()
        for cp in sends + passed:
            cp.wait_send()

    out_shape = [jax.ShapeDtypeStruct(b.shape, b.dtype) for b in bufs]
    return _pcall(body, name=name, in_specs=[ANY] * n, out_specs=[ANY] * n, out_shape=out_shape,
                  input_output_aliases={t: t for t in range(n)},
                  scratch_shapes=[pltpu.SemaphoreType.DMA((3 * n,)), pltpu.SemaphoreType.DMA((3 * n,)),
                                  pltpu.SemaphoreType.DMA((3 * n,)), pltpu.SemaphoreType.DMA((3 * n,))])(*bufs)


def _remote(src, dst, send, recv, k, to):
    return pltpu.make_async_remote_copy(src_ref=src, dst_ref=dst, send_sem=send.at[k], recv_sem=recv.at[k],
                                        device_id=to, device_id_type=MESH)


def _in_place(bufs):
    return dict(operands=bufs, out_shapes=[jax.ShapeDtypeStruct(b.shape, b.dtype) for b in bufs],
                aliases={t: t for t in range(len(bufs))})


def _gather_over_ici(bufs, split):
    n = len(bufs)

    def copies(ins, outs, send, recv, base=0):
        x, y, c, chips = _place()
        me = 2 * x + y
        out = []
        for t in range(n):
            piece = outs[t].at[me]
            if split[t]:
                half = outs[t].shape[1] // 2
                piece = outs[t].at[me, pl.ds(c * half, half)]
            for j, (px, py) in enumerate(chips):
                out.append(_remote(piece, piece, send, recv, base + 3 * t + j, (px, py, c)))
        return out

    return _Rider(n_copies=3 * n, copies=copies, **_in_place(bufs))


def _gather_to_sibling(bufs):
    n = len(bufs)

    def copies(ins, outs, send, recv, base=0):
        x, y, c, chips = _place()
        out = []
        for t in range(n):
            half = outs[t].shape[1] // 2
            for j, (px, py) in enumerate(chips):
                piece = outs[t].at[2 * px + py, pl.ds(c * half, half)]
                out.append(_remote(piece, piece, send, recv, base + 3 * t + j, (x, y, 1 - c)))
        return out

    return _Rider(n_copies=3 * n, copies=copies, **_in_place(bufs))


def _swap_halves_rider(parts):
    n = len(parts)

    def copies(ins, outs, send, recv, base=0):
        x, y, c, _ = _place()
        out = []
        for t in range(n):
            half = ins[t].shape[1] // 2
            out.append(_remote(ins[t].at[:, pl.ds((1 - c) * half, half)], outs[t], send, recv, base + t,
                               (x, y, 1 - c)))
        return out

    shapes = [jax.ShapeDtypeStruct((p.shape[0], p.shape[1] // 2, p.shape[2]), p.dtype) for p in parts]
    return _Rider(parts, shapes, {}, n, copies)


def _scatter_rider(sums):
    n = len(sums)

    def copies(ins, outs, send, recv, base=0):
        x, y, c, chips = _place()
        out = []
        for t in range(n):
            for j, (px, py) in enumerate(chips):
                out.append(_remote(ins[t].at[2 * px + py], outs[t].at[j], send, recv, base + 3 * t + j, (px, py, c)))
        return out

    shapes = [jax.ShapeDtypeStruct((3,) + s.shape[1:], s.dtype) for s in sums]
    return _Rider(sums, shapes, {}, 3 * n, copies)


def _share_rider(blocks, pieces):
    def copies(ins, outs, send, recv, base=0):
        x, y, c, _ = _place()
        out = []
        for k, (t, l) in enumerate(pieces):
            ref = outs[t] if l is None else outs[t].at[l]
            r2 = ref.shape[0] // 2
            piece = ref.at[pl.ds(c * r2, r2)]
            out.append(_remote(piece, piece, send, recv, base + k, (x, y, 1 - c)))
        return out

    return _Rider(n_copies=len(pieces), copies=copies, **_in_place(blocks))


def _gather_small(name, v):
    def body(v_ref, out_ref, send_sem, recv_sem, local_sem):
        x, y, c, _ = _place()
        me = 4 * x + 2 * y + c
        flips = [(fx, fy, fc) for fx in (0, 1) for fy in (0, 1) for fc in (0, 1)][1:]
        local = pltpu.make_async_copy(v_ref, out_ref.at[me], local_sem)
        local.start()
        copies = []
        for j, (fx, fy, fc) in enumerate(flips):
            peer = (x ^ fx, y ^ fy, c ^ fc)
            copies.append(pltpu.make_async_remote_copy(
                src_ref=v_ref, dst_ref=out_ref.at[me], send_sem=send_sem.at[j], recv_sem=recv_sem.at[j],
                device_id=peer, device_id_type=MESH))
        for cp in copies:
            cp.start()
        for j, (fx, fy, fc) in enumerate(flips):
            peer = (x ^ fx, y ^ fy, c ^ fc)
            pltpu.make_async_remote_copy(
                src_ref=v_ref, dst_ref=out_ref.at[4 * peer[0] + 2 * peer[1] + peer[2]], send_sem=send_sem.at[j],
                recv_sem=recv_sem.at[j], device_id=peer, device_id_type=MESH).wait_recv()
        for cp in copies:
            cp.wait_send()
        local.wait()

    return _pcall(body, name=name, in_specs=[ANY], out_specs=ANY,
                  out_shape=jax.ShapeDtypeStruct((8,) + v.shape, v.dtype),
                  scratch_shapes=[pltpu.SemaphoreType.DMA((7,)), pltpu.SemaphoreType.DMA((7,)),
                                  pltpu.SemaphoreType.DMA])(v)


def _fold(a, dil):
    if dil == 1:
        return a
    S, C = a.shape
    return a.reshape(S // dil, dil, C).transpose(1, 0, 2).reshape(S, C)


def _unfold(a, dil):
    if dil == 1:
        return a
    S, C = a.shape
    return a.reshape(dil, S // dil, C).transpose(1, 0, 2).reshape(S, C)


def _fold_groups(a):
    return jnp.stack([_fold(a[g] if a.ndim == 3 else a, d) for g, d in enumerate(ATTN_DILATIONS)])


def _unfold_groups(a):
    return jnp.stack([_unfold(a[g], d) for g, d in enumerate(ATTN_DILATIONS)])


class _NoComm:
    def __init__(self, weights):
        self.weights = weights
        self.grads = {}

    def w(self, name):
        return self.weights[name]

    def run(self, fn, name, *args, **kw):
        return fn(name, *args, **kw)

    def grad(self, name, value):
        self.grads[name] = value


def _local_step(xs, tgt, attn_norm, ffn_norm, final_norm, comm):
    run = comm.run
    hf, qkv = None, None
    for g in range(N_GROUPS):
        hf, qkv = run(_norm_fold_qkv, "qkv_proj%d" % g, g, xs, attn_norm, comm.w("qkv"), hf, qkv)
    o_f, lse_f = run(_attn_fwd, "attn_fwd", qkv)
    lse_t = _unfold_groups(lse_f)
    merged = run(_merge_fwd, "merge_fwd", o_f, lse_t)
    x1 = run(_mm_rows, "attn_out", merged, comm.w("o"), F32, 512, res=xs)
    h1 = run(_rms_fwd, "norm_ffn0", x1, ffn_norm[0:1])
    gu0, act0 = run(_ffn_up, "ffn0_up", h1, comm.w("gu0"))
    x2 = run(_mm_rows, "ffn0_down", act0, comm.w("d0"), F32, 512, res=x1)
    h2 = run(_rms_fwd, "norm_pool", x2, comm.w("pool_norm"))
    u = run(_mm_rows, "pool_in", h2, comm.w("p"), F32, 512)
    y, z, x3 = run(_pool_fwd, "pool_fwd", u, comm.w("pg"), comm.w("pool_scale"), x2)
    h3 = run(_rms_fwd, "norm_ffn1", x3, ffn_norm[1:2])
    gu1, act1 = run(_ffn_up, "ffn1_up", h3, comm.w("gu1"))
    x4 = run(_mm_rows, "ffn1_down", act1, comm.w("d1"), F32, 512, res=x3)
    loss, dx4, d_final = run(_loss_bwd, "loss_head", x4, final_norm, tgt)

    dgu1 = run(_ffn_down_bwd, "ffn1_down_bwd", dx4, comm.w("d1"), gu1)
    comm.grad("d1", run(_mm_tn, "ffn1_down_dw", act1, dx4, FF_SHARD, 1024))
    comm.grad("gu1", run(_ffn_dw_up, "ffn1_up_dw", h3, dgu1))
    dh3 = run(_ffn_dh, "ffn1_up_dh", dgu1, comm.w("gu1"))
    dx3, d_ffn1 = run(_rms_bwd, "norm_ffn1_bwd", x3, ffn_norm[1:2], dh3, dx4)

    du, dw_pg, d_scale = run(_pool_bwd, "pool_bwd", dx3, z, y, comm.w("pool_scale"), comm.w("pg"))
    comm.grad("pg", dw_pg)
    comm.grad("p", run(_mm_tn, "pool_in_dw", h2, du, D_MODEL, 1024))
    dh2 = run(_mm_nt_rows, "pool_in_dh", du, comm.w("p"), 512)
    dx2, d_pool = run(_rms_bwd, "norm_pool_bwd", x2, comm.w("pool_norm"), dh2, dx3)

    dgu0 = run(_ffn_down_bwd, "ffn0_down_bwd", dx2, comm.w("d0"), gu0)
    comm.grad("d0", run(_mm_tn, "ffn0_down_dw", act0, dx2, FF_SHARD, 1024))
    comm.grad("gu0", run(_ffn_dw_up, "ffn0_up_dw", h1, dgu0))
    dh1 = run(_ffn_dh, "ffn0_up_dh", dgu0, comm.w("gu0"))
    dx1, d_ffn0 = run(_rms_bwd, "norm_ffn0_bwd", x1, ffn_norm[0:1], dh1, dx2)

    d_merged = run(_mm_nt_rows, "attn_out_dh", dx1, comm.w("o"), 512)
    comm.grad("o", run(_mm_tn, "attn_out_dw", merged, dx1, D_MODEL, 1024))
    do_f, dl_t = run(_merge_bwd, "merge_bwd", d_merged, o_f, lse_t)
    dqkv = run(_attn_bwd, "attn_bwd", qkv, do_f, lse_f, _fold_groups(dl_t))
    comm.grad("qkv", run(_qkv_dw, "qkv_dw", hf, dqkv))
    dhf = run(_qkv_dh, "qkv_dh", dqkv, comm.w("qkv"))
    grad_x, d_attn = run(_rms_bwd_folded, "norm_attn_bwd", xs, attn_norm, dhf, dx1)

    small = dict(attn=d_attn, ffn0=d_ffn0, ffn1=d_ffn1, final=d_final, pool=d_pool, scale=d_scale)
    return loss, grad_x, small


TENSORS = ("qkv", "o", "p", "pg", "gu0", "gu1", "d0", "d1")


def _block_of(name):
    if name[:-1] in ("gu", "d"):
        return name[:-1], int(name[-1])
    return name, None


class _MeshComm:
    def __init__(self, bufs, chip_arr, core_arr, pg_rows):
        self.bufs = dict(bufs)
        self.chip_arr, self.core_arr, self.pg_rows = chip_arr, core_arr, pg_rows
        self.parts, self.sums, self.blocks = {}, {}, {}
        ici = lambda *names: lambda: self.gather(names, True)
        d2d = lambda *names: lambda: self.gather(names, False)
        swap = lambda *names: lambda: self.swap(names)
        scatter = lambda *names: lambda: self.scatter(names)
        share = lambda *names: lambda: self.share(names)
        self.plan = {
            "qkv_proj0": [ici("d0")],
            "qkv_proj1": [ici("o", "p", "pg", "pool_norm", "pool_scale")],
            "qkv_proj2": [d2d("d0", "o", "p", "pg")],
            "attn_fwd": [ici("gu0")],
            "merge_fwd": [d2d("gu0")],
            "ffn0_up": [ici("gu1")],
            "ffn0_down": [d2d("gu1"), ici("d1")],
            "pool_in": [d2d("d1")],
            "ffn1_up_dh": [swap("d1", "gu1")],
            "ffn0_down_bwd": [scatter("gu1")],
            "ffn0_down_dw": [scatter("d1")],
            "ffn0_up_dh": [swap("pg", "p", "d0", "gu0"), share("gu1", "d1")],
            "merge_bwd": [swap("o")],
            "attn_bwd": [scatter("pg", "p", "d0", "gu0", "o")],
            "qkv_dw": [share("pg", "p", "d0", "gu0", "o")],
            "qkv_dh": [swap("qkv")],
            "norm_attn_bwd": [scatter("qkv")],
        }

    def gather(self, names, over_ici):
        bufs = [self.bufs[n] for n in names]
        rider = _gather_over_ici(bufs, [n in TENSORS for n in names]) if over_ici else _gather_to_sibling(bufs)
        return rider, lambda res: self.bufs.update(zip(names, res))

    def swap(self, names):
        def done(res):
            for n, q in zip(names, res):
                self.sums[n] = _add_my_half("chip_sum_" + n, self.parts[n], q, self.core_arr)
        return _swap_halves_rider([self.parts[n] for n in names]), done

    def scatter(self, names):
        def done(res):
            for n, landed in zip(names, res):
                key, layer = _block_of(n)
                self.blocks[key] = _owner_sum("owner_sum_" + n, self.sums[n], landed, self.chip_arr, self.core_arr,
                                              out=self.blocks.get(key), layer=layer)
        return _scatter_rider([self.sums[n] for n in names]), done

    def share(self, names):
        keys, pieces = [], []
        for n in names:
            key, layer = _block_of(n)
            if key not in keys:
                keys.append(key)
            pieces.append((keys.index(key), layer))
        return _share_rider([self.blocks[k] for k in keys], pieces), lambda res: self.blocks.update(zip(keys, res))

    def alone(self, name, made):
        rider, done = made
        done(_comm_only(name, rider))

    def run(self, fn, name, *args, **kw):
        made = [make() for make in self.plan.get(name, [])]
        if not made:
            return fn(name, *args, **kw)
        riders = [r for r, _ in made]
        out = _ride(riders[0] if len(riders) == 1 else _riders(*riders), fn, name, *args, **kw)
        for r, done in made:
            done(r.results)
        return out

    def w(self, name):
        b = self.bufs[name]
        if name in ("o", "p", "d0", "d1"):
            return b.reshape(-1, b.shape[-1])
        if name == "pg":
            b = b.reshape(N_CHIPS, 4, self.pg_rows, POOL_DIM).transpose(1, 0, 2, 3)
            return b.reshape(4, POOL_DIM, POOL_DIM)
        if name in ("pool_norm", "pool_scale"):
            return b.reshape(1, -1)
        return b

    def grad(self, name, value):
        if name == "pg":
            value = value.reshape(4, N_CHIPS, self.pg_rows, POOL_DIM).transpose(1, 0, 2, 3)
            value = value.reshape(N_CHIPS, 4 * self.pg_rows, POOL_DIM).astype(BF16)
        elif name in ("o", "p", "d0", "d1"):
            value = value.reshape(N_CHIPS, value.shape[0] // N_CHIPS, value.shape[1])
        self.parts[name] = value


def kernel(x, attn_norm, w_qkv, w_attn_out, pool_norm, w_pool_in, w_pool_group, pool_scale, ffn_norm, w_ffn_gate_up, w_ffn_down, final_norm, loss_target, m_attn_norm, m_w_qkv, m_w_attn_out, m_pool_norm, m_w_pool_in, m_w_pool_group, m_pool_scale, m_ffn_norm, m_w_ffn_gate_up, m_w_ffn_down, m_final_norm, v_attn_norm, v_w_qkv, v_w_attn_out, v_pool_norm, v_w_pool_in, v_w_pool_group, v_pool_scale, v_ffn_norm, v_w_ffn_gate_up, v_w_ffn_down, v_final_norm):
    D = D_MODEL
    chip = 2 * lax.axis_index("x") + lax.axis_index("y")
    core = lax.axis_index("c")
    pg_rows = w_pool_group.shape[2]

    chip_arr = chip.astype(jnp.int32).reshape(1)
    core_arr = core.astype(jnp.int32).reshape(1)

    shards = [w_qkv[0], w_attn_out[0], w_pool_in[0], w_pool_group[0].reshape(4 * pg_rows, POOL_DIM),
              w_ffn_gate_up[0], w_ffn_gate_up[1], w_ffn_down[0], w_ffn_down[1]]
    bufs = {nm: _cast_into_slot("cast_" + nm, s, chip_arr, BF16) for nm, s in zip(TENSORS, shards)}
    bufs["pool_norm"] = _cast_into_slot("place_pool_norm", pool_norm, chip_arr, F32)
    bufs["pool_scale"] = _cast_into_slot("place_pool_scale", pool_scale, chip_arr, F32)
    bufs["qkv"] = _gather_weights("gather_qkv", [bufs["qkv"]], [True])[0]

    comm = _MeshComm(bufs, chip_arr, core_arr, pg_rows)
    loss_part, grad_x, small = _local_step(x[0], loss_target[0], attn_norm, ffn_norm, final_norm.reshape(1, D), comm)
    comm.alone("halves_to_sibling", comm.share(["qkv"]))
    g_w_qkv, g_w_o, g_w_p, g_w_pg, g_w_gu, g_w_d = [comm.blocks[k] for k in ("qkv", "o", "p", "pg", "gu", "d")]

    rows = jnp.concatenate([small["attn"], small["ffn0"], small["ffn1"], small["final"], small["pool"],
                            small["scale"], jnp.zeros((2, D), F32)], axis=0)
    all_rows = _gather_small("gather_gain_grads", rows)
    tot = _sum_leading("sum_gain_grads", all_rows, F32)
    g_attn_norm = tot[0:1]
    g_ffn_norm = tot[1:3]
    g_final_norm = tot[3]
    g_pool_norm = lax.dynamic_slice(tot, (4, chip * POOL_DIM), (1, POOL_DIM))
    g_pool_scale = lax.dynamic_slice(tot, (5, chip * POOL_DIM), (1, POOL_DIM))

    loss = lax.psum(loss_part[0, 0], ("x", "y", "c"))

    def update(name, w, g, m, v):
        shape = w.shape
        cols = shape[-1]
        as2d = lambda a: a.reshape(-1, cols)
        d, m2, v2 = _adamw("adamw_" + name, as2d(w), as2d(g), as2d(m), as2d(v))
        return g.reshape(shape), d.reshape(shape), m2.reshape(shape), v2.reshape(shape)

    results = [
        update("attn_norm", attn_norm, g_attn_norm, m_attn_norm, v_attn_norm),
        update("w_qkv", w_qkv, g_w_qkv, m_w_qkv, v_w_qkv),
        update("w_attn_out", w_attn_out, g_w_o, m_w_attn_out, v_w_attn_out),
        update("pool_norm", pool_norm, g_pool_norm, m_pool_norm, v_pool_norm),
        update("w_pool_in", w_pool_in, g_w_p, m_w_pool_in, v_w_pool_in),
        update("w_pool_group", w_pool_group, g_w_pg, m_w_pool_group, v_w_pool_group),
        update("pool_scale", pool_scale, g_pool_scale, m_pool_scale, v_pool_scale),
        update("ffn_norm", ffn_norm, g_ffn_norm, m_ffn_norm, v_ffn_norm),
        update("w_ffn_gate_up", w_ffn_gate_up, g_w_gu, m_w_ffn_gate_up, v_w_ffn_gate_up),
        update("w_ffn_down", w_ffn_down, g_w_d, m_w_ffn_down, v_w_ffn_down),
        update("final_norm", final_norm, g_final_norm, m_final_norm, v_final_norm),
    ]
    grads = [r[0] for r in results]
    deltas = [r[1] for r in results]
    new_m = [r[2] for r in results]
    new_v = [r[3] for r in results]
    return (loss, grad_x[None], *grads, *deltas, *new_m, *new_v)
```

```python
import functools

import jax
import jax.numpy as jnp
from jax import lax
from jax.experimental import pallas as pl
from jax.experimental.pallas import tpu as pltpu

F32 = jnp.float32
BF16 = jnp.bfloat16
MESH = pl.DeviceIdType.MESH

D_MODEL = 1024
N_HEADS = 8
HEAD_DIM = 128
Q_BLOCK = 128
ATTN_DILATIONS = (1, 4, 16)
N_GROUPS = 3
D_FF = 2816
N_CHIPS = 4
FF_SHARD = 2 * D_FF // N_CHIPS
QKV_SHARD = 3 * 3 * D_MODEL // N_CHIPS
QKV_TILE = 768
POOL_WINDOWS = (2, 4, 8, 16)
POOL_DIM = 256
POOL_HALO = 16
RMS_EPS = 1e-6
NEG = -1e30
ADAM_LR = 0.001
ADAM_B1 = 0.9
ADAM_B2 = 0.999
ADAM_EPS = 1e-08
ADAM_WD = 0.01
ADAM_STEP = 10
VMEM_LIMIT = 56 * 1024 * 1024

_DN = {
    "nn": (((1,), (0,)), ((), ())),
    "nt": (((1,), (1,)), ((), ())),
    "tn": (((0,), (0,)), ((), ())),
}


class _Rider:
    def __init__(self, operands, out_shapes, aliases, n_copies, copies):
        self.operands = list(operands)
        self.out_shapes = list(out_shapes)
        self.aliases = dict(aliases)
        self.n_copies = n_copies
        self.copies = copies
        self.results = None


def _riders(*riders):
    operands, out_shapes, aliases, offsets = [], [], {}, []
    n = 0
    for r in riders:
        offsets.append((len(operands), len(out_shapes), n))
        aliases.update({len(operands) + i: len(out_shapes) + o for i, o in r.aliases.items()})
        operands += r.operands
        out_shapes += r.out_shapes
        n += r.n_copies

    def copies(ins, outs, send, recv, base=0):
        out = []
        for r, (i0, o0, s0) in zip(riders, offsets):
            out += r.copies(ins[i0:i0 + len(r.operands)], outs[o0:o0 + len(r.out_shapes)], send, recv, base + s0)
        return out

    both = _Rider(operands, out_shapes, aliases, n, copies)
    both.parts = (riders, offsets)
    return both


_pending_rider = []


def _ride(rider, fn, *args, **kw):
    _pending_rider.append(rider)
    out = fn(*args, **kw)
    assert not _pending_rider
    if hasattr(rider, "parts"):
        for r, (_, o0, _) in zip(*rider.parts):
            r.results = rider.results[o0:o0 + len(r.out_shapes)]
    return out


def _pcall(body, **kw):
    if not _pending_rider:
        return pl.pallas_call(body, **kw)
    rider = _pending_rider.pop()
    grid = kw.get("grid", ())
    in_specs = list(kw.get("in_specs", []))
    single = not isinstance(kw["out_shape"], (list, tuple))
    out_shape = [kw["out_shape"]] if single else list(kw["out_shape"])
    out_specs = [kw["out_specs"]] if single else list(kw["out_specs"])
    scratch = list(kw.get("scratch_shapes", []))
    n_in, n_out, n_scr = len(in_specs), len(out_shape), len(scratch)
    r_in, r_out = len(rider.operands), len(rider.out_shapes)

    def wrapped(*refs):
        ins, rins = refs[:n_in], refs[n_in:n_in + r_in]
        outs = refs[n_in + r_in:n_in + r_in + n_out]
        routs = refs[n_in + r_in + n_out:n_in + r_in + n_out + r_out]
        scr = refs[n_in + r_in + n_out + r_out:n_in + r_in + n_out + r_out + n_scr]
        send, recv = refs[-2:]
        ids = [pl.program_id(a) for a in range(len(grid))]
        first, last = True, True
        for a, pid in enumerate(ids):
            first = jnp.logical_and(first, pid == 0)
            last = jnp.logical_and(last, pid == grid[a] - 1)
        if grid:
            @pl.when(first)
            def _():
                for cp in rider.copies(rins, routs, send, recv):
                    cp.start()
        else:
            for cp in rider.copies(rins, routs, send, recv):
                cp.start()
        body(*ins, *outs, *scr)
        if grid:
            @pl.when(last)
            def _():
                for cp in rider.copies(rins, routs, send, recv):
                    cp.wait()
        else:
            for cp in rider.copies(rins, routs, send, recv):
                cp.wait()

    any_spec = pl.BlockSpec(memory_space=pl.ANY)
    kw2 = dict(kw)
    kw2.update(
        in_specs=in_specs + [any_spec] * r_in, out_specs=out_specs + [any_spec] * r_out,
        out_shape=out_shape + rider.out_shapes,
        scratch_shapes=scratch + [pltpu.SemaphoreType.DMA((rider.n_copies,)),
                                  pltpu.SemaphoreType.DMA((rider.n_copies,))],
        input_output_aliases={**kw.get("input_output_aliases", {}),
                              **{n_in + i: n_out + o for i, o in rider.aliases.items()}})
    if grid:
        kw2["compiler_params"] = _params(("arbitrary",) * len(grid))
    call = pl.pallas_call(wrapped, **kw2)

    def run(*operands):
        res = call(*operands, *rider.operands)
        rider.results = list(res[n_out:])
        return res[0] if single else list(res[:n_out])

    return run


def _comm_only(name, rider):
    def make():
        return _pcall(lambda: None, name=name, in_specs=[], out_specs=[], out_shape=[])()
    _ride(rider, make)
    return rider.results


def _params(sem):
    return pltpu.CompilerParams(dimension_semantics=sem, vmem_limit_bytes=VMEM_LIMIT)


def _dot(a, b, mode):
    return lax.dot_general(a, b, _DN[mode], preferred_element_type=F32)


def _mm(name, mode, grid, a, a_spec, b, b_spec, out_shape, o_spec, acc_shape, res=None, res_spec=None):
    nk = grid[-1]
    has_res = res is not None

    def body(*refs):
        a_ref, b_ref = refs[0], refs[1]
        res_ref = refs[2] if has_res else None
        o_ref = refs[2 + has_res]
        part = _dot(a_ref[...].astype(BF16), b_ref[...].astype(BF16), mode)
        if nk == 1:
            if has_res:
                part = part + res_ref[...]
            o_ref[...] = part.astype(o_ref.dtype)
            return
        acc_ref = refs[3 + has_res]
        k = pl.program_id(len(grid) - 1)

        @pl.when(k == 0)
        def _():
            acc_ref[...] = part

        @pl.when(k > 0)
        def _():
            acc_ref[...] += part

        @pl.when(k == nk - 1)
        def _():
            r = acc_ref[...]
            if has_res:
                r = r + res_ref[...]
            o_ref[...] = r.astype(o_ref.dtype)

    in_specs = [a_spec, b_spec] + ([res_spec] if has_res else [])
    operands = [a, b] + ([res] if has_res else [])
    scratch = [] if nk == 1 else [pltpu.VMEM(acc_shape, F32)]
    sem = ("parallel",) * (len(grid) - 1) + ("arbitrary",)
    return _pcall(body, name=name, grid=grid, in_specs=in_specs, out_specs=o_spec, out_shape=out_shape,
                  scratch_shapes=scratch, compiler_params=_params(sem))(*operands)


def _mm_rows(name, a, b, out_dtype, tm, res=None):
    M, K = a.shape
    N = b.shape[1]
    return _mm(name, "nn", (M // tm, 1), a, pl.BlockSpec((tm, K), lambda i, k: (i, 0)),
               b, pl.BlockSpec((K, N), lambda i, k: (0, 0)),
               jax.ShapeDtypeStruct((M, N), out_dtype), pl.BlockSpec((tm, N), lambda i, k: (i, 0)), None,
               res=res, res_spec=pl.BlockSpec((tm, N), lambda i, k: (i, 0)))


def _mm_nt_rows(name, a, b, tm):
    M, N = a.shape
    K = b.shape[0]
    return _mm(name, "nt", (M // tm, 1), a, pl.BlockSpec((tm, N), lambda i, k: (i, 0)),
               b, pl.BlockSpec((K, N), lambda i, k: (0, 0)),
               jax.ShapeDtypeStruct((M, K), F32), pl.BlockSpec((tm, K), lambda i, k: (i, 0)), None)


def _mm_tn(name, a, b, tm, tk):
    T, M = a.shape
    N = b.shape[1]
    return _mm(name, "tn", (M // tm, T // tk), a, pl.BlockSpec((tk, tm), lambda i, k: (k, i)),
               b, pl.BlockSpec((tk, N), lambda i, k: (k, 0)),
               jax.ShapeDtypeStruct((M, N), BF16), pl.BlockSpec((tm, N), lambda i, k: (i, 0)), (tm, N))


def _rms_fwd(name, x, g, tr=512):
    S, D = x.shape

    def body(x_ref, g_ref, o_ref):
        xf = x_ref[...]
        r = lax.rsqrt(jnp.mean(xf * xf, axis=-1, keepdims=True) + RMS_EPS)
        o_ref[...] = ((xf * r) * g_ref[...]).astype(o_ref.dtype)

    return _pcall(body, name=name, grid=(S // tr,),
                  in_specs=[pl.BlockSpec((tr, D), lambda i: (i, 0)), pl.BlockSpec((1, D), lambda i: (0, 0))],
                  out_specs=pl.BlockSpec((tr, D), lambda i: (i, 0)),
                  out_shape=jax.ShapeDtypeStruct((S, D), BF16), compiler_params=_params(("parallel",)))(x, g)


def _rms_bwd(name, x, g, dh, dres, tr=512):
    S, D = x.shape

    def body(x_ref, g_ref, dh_ref, dres_ref, dx_ref, dg_ref):
        i = pl.program_id(0)
        xf = x_ref[...]
        r = lax.rsqrt(jnp.mean(xf * xf, axis=-1, keepdims=True) + RMS_EPS)
        xh = xf * r
        dh_v = dh_ref[...]
        dg = jnp.sum(dh_v * xh, axis=0, keepdims=True)
        t = dh_v * g_ref[...]
        dx_ref[...] = dres_ref[...] + r * (t - xh * jnp.mean(t * xh, axis=-1, keepdims=True))

        @pl.when(i == 0)
        def _():
            dg_ref[...] = dg

        @pl.when(i > 0)
        def _():
            dg_ref[...] += dg

    row = pl.BlockSpec((tr, D), lambda i: (i, 0))
    vec = pl.BlockSpec((1, D), lambda i: (0, 0))
    return _pcall(body, name=name, grid=(S // tr,), in_specs=[row, vec, row, row], out_specs=[row, vec],
                  out_shape=[jax.ShapeDtypeStruct((S, D), F32), jax.ShapeDtypeStruct((1, D), F32)],
                  compiler_params=_params(("arbitrary",)))(x, g, dh, dres)


def _rms_bwd_folded(name, x, g, dhf, dres, tb=512):
    S, D = x.shape

    def body(x_ref, g_ref, d0_ref, d1_ref, d2_ref, dres_ref, dx_ref, dg_ref, dh_ref):
        i = pl.program_id(0)
        chunks = _lane_chunks(D)
        for c, cols in enumerate(chunks):
            dh_ref[c] = d0_ref[0, :, cols]
        for dil, ref in ((ATTN_DILATIONS[1], d1_ref), (ATTN_DILATIONS[2], d2_ref)):
            n = tb // dil
            for k in range(dil):
                for c, cols in enumerate(chunks):
                    dh_ref[c, _strided_rows(k, n, dil), :] += ref[k, :, cols]
        xf = x_ref[...]
        r = lax.rsqrt(jnp.mean(xf * xf, axis=-1, keepdims=True) + RMS_EPS)
        xh = xf * r
        dh_v = jnp.concatenate([dh_ref[c] for c in range(len(chunks))], axis=1)
        dg = jnp.sum(dh_v * xh, axis=0, keepdims=True)
        t = dh_v * g_ref[...]
        dx_ref[...] = dres_ref[...] + r * (t - xh * jnp.mean(t * xh, axis=-1, keepdims=True))

        @pl.when(i == 0)
        def _():
            dg_ref[...] = dg

        @pl.when(i > 0)
        def _():
            dg_ref[...] += dg

    views, specs = _folded_views(dhf, tb)
    row = pl.BlockSpec((tb, D), lambda i: (i, 0))
    vec = pl.BlockSpec((1, D), lambda i: (0, 0))
    return _pcall(body, name=name, grid=(S // tb,), in_specs=[row, vec] + specs + [row], out_specs=[row, vec],
                  out_shape=[jax.ShapeDtypeStruct((S, D), F32), jax.ShapeDtypeStruct((1, D), F32)],
                  scratch_shapes=[pltpu.VMEM((D // LANES, tb, LANES), F32)],
                  compiler_params=_params(("arbitrary",)))(x, g, *views, dres)


def _loss_bwd(name, x, g, tgt, tr=512):
    S, D = x.shape

    def body(x_ref, g_ref, t_ref, loss_ref, dx_ref, dg_ref):
        i = pl.program_id(0)
        xf = x_ref[...]
        r = lax.rsqrt(jnp.mean(xf * xf, axis=-1, keepdims=True) + RMS_EPS)
        xh = xf * r
        gv = g_ref[...]
        e = xh * gv - t_ref[...]
        lp = 0.5 * jnp.sum(jnp.mean(e * e, axis=-1, keepdims=True), axis=0, keepdims=True)
        dy = e * (1.0 / D)
        dg = jnp.sum(dy * xh, axis=0, keepdims=True)
        t = dy * gv
        dx_ref[...] = r * (t - xh * jnp.mean(t * xh, axis=-1, keepdims=True))

        @pl.when(i == 0)
        def _():
            dg_ref[...] = dg
            loss_ref[...] = lp

        @pl.when(i > 0)
        def _():
            dg_ref[...] += dg
            loss_ref[...] += lp

    row = pl.BlockSpec((tr, D), lambda i: (i, 0))
    vec = pl.BlockSpec((1, D), lambda i: (0, 0))
    one = pl.BlockSpec((1, 1), lambda i: (0, 0))
    return _pcall(body, name=name, grid=(S // tr,), in_specs=[row, vec, row], out_specs=[one, row, vec],
                  out_shape=[jax.ShapeDtypeStruct((1, 1), F32), jax.ShapeDtypeStruct((S, D), F32),
                             jax.ShapeDtypeStruct((1, D), F32)],
                  compiler_params=_params(("arbitrary",)))(x, g, tgt)


def _sigmoid(v):
    return 0.5 * jnp.tanh(0.5 * v) + 0.5


def _ffn_up(name, h, w_gu, tm=512):
    S, D = h.shape
    W = FF_SHARD

    def body(h_ref, wg_ref, wu_ref, gu_ref, act_ref):
        hv = h_ref[...]
        gate = _dot(hv, wg_ref[...], "nn")
        up = _dot(hv, wu_ref[...], "nn")
        gu_ref[0] = gate
        gu_ref[1] = up
        sig = _sigmoid(gate)
        act_ref[...] = ((gate * sig) * up).astype(act_ref.dtype)

    return _pcall(
        body, name=name, grid=(2, S // tm),
        in_specs=[pl.BlockSpec((tm, D), lambda j, i: (i, 0)),
                  pl.BlockSpec((None, D, W), lambda j, i: (j, 0, 0)),
                  pl.BlockSpec((None, D, W), lambda j, i: (j + 2, 0, 0))],
        out_specs=[pl.BlockSpec((2, tm, W), lambda j, i: (0, i, j)), pl.BlockSpec((tm, W), lambda j, i: (i, j))],
        out_shape=[jax.ShapeDtypeStruct((2, S, D_FF), F32), jax.ShapeDtypeStruct((S, D_FF), BF16)],
        compiler_params=_params(("parallel", "parallel")))(h, w_gu, w_gu)


def _ffn_down_bwd(name, dx, w_down, gu, tm=512):
    S, D = dx.shape
    W = FF_SHARD

    def body(dx_ref, w_ref, gu_ref, dgu_ref):
        dact = _dot(dx_ref[...].astype(BF16), w_ref[...], "nt")
        gate = gu_ref[0]
        up = gu_ref[1]
        sig = _sigmoid(gate)
        silu = gate * sig
        dgu_ref[0] = (dact * up * (sig * (1.0 + gate * (1.0 - sig)))).astype(dgu_ref.dtype)
        dgu_ref[1] = (dact * silu).astype(dgu_ref.dtype)

    blk = pl.BlockSpec((2, tm, W), lambda j, i: (0, i, j))
    return _pcall(
        body, name=name, grid=(2, S // tm),
        in_specs=[pl.BlockSpec((tm, D), lambda j, i: (i, 0)), pl.BlockSpec((W, D), lambda j, i: (j, 0)), blk],
        out_specs=blk, out_shape=jax.ShapeDtypeStruct((2, S, D_FF), BF16),
        compiler_params=_params(("parallel", "parallel")))(dx, w_down, gu)


def _ffn_dw_up(name, h, dgu, tk=1024):
    S, D = h.shape
    W = FF_SHARD
    return _mm(name, "tn", (N_CHIPS, S // tk), h, pl.BlockSpec((tk, D), lambda s, k: (k, 0)),
               dgu, pl.BlockSpec((None, tk, W), lambda s, k: (s // 2, k, s % 2)),
               jax.ShapeDtypeStruct((N_CHIPS, D, W), BF16), pl.BlockSpec((None, D, W), lambda s, k: (s, 0, 0)),
               (D, W))


def _ffn_dh(name, dgu, w_gu, tm=1024):
    S = dgu.shape[1]
    W = FF_SHARD
    return _mm(name, "nt", (S // tm, N_CHIPS), dgu, pl.BlockSpec((None, tm, W), lambda i, k: (k // 2, i, k % 2)),
               w_gu, pl.BlockSpec((None, D_MODEL, W), lambda i, k: (k, 0, 0)),
               jax.ShapeDtypeStruct((S, D_MODEL), F32), pl.BlockSpec((tm, D_MODEL), lambda i, k: (i, 0)),
               (tm, D_MODEL))


FOLD_TOKENS = 1024
LANES = 128


def _lane_chunks(width):
    return [slice(c * LANES, (c + 1) * LANES) for c in range(width // LANES)]


def _strided_rows(r, n, dil):
    return pl.ds(r, n) if dil == 1 else pl.ds(r, n, stride=dil)


def _norm_fold_qkv(name, g, x, gain, w_qkv, hf, qkv):
    S, D = x.shape
    dil = ATTN_DILATIONS[g]
    n = FOLD_TOKENS // dil
    seg = S // dil
    per_chip = QKV_SHARD // QKV_TILE
    per_group = 3 * D_MODEL // QKV_TILE

    def body(*refs):
        x_ref, g_ref, w_ref = refs[:3]
        hf_ref, qkv_ref, a_ref, xc_ref = refs[-4:]
        j = pl.program_id(1)

        @pl.when(j == 0)
        def _():
            xf = x_ref[...]
            inv = lax.rsqrt(jnp.mean(xf * xf, axis=-1, keepdims=True) + RMS_EPS)
            h = (xf * inv) * g_ref[...]
            if dil == 1:
                a_ref[...] = h.astype(BF16)
            else:
                for c, cols in enumerate(_lane_chunks(D)):
                    xc_ref[c] = h[:, cols]
                for r in range(dil):
                    for c, cols in enumerate(_lane_chunks(D)):
                        a_ref[r * n:(r + 1) * n, cols] = xc_ref[c, _strided_rows(r, n, dil), :].astype(BF16)
            for r in range(dil):
                hf_ref[r] = a_ref[r * n:(r + 1) * n, :]

        res = _dot(a_ref[...], w_ref[...], "nn")
        for r in range(dil):
            qkv_ref[r] = res[r * n:(r + 1) * n].astype(qkv_ref.dtype)

    in_specs = [pl.BlockSpec((FOLD_TOKENS, D), lambda i, j: (i, 0)), pl.BlockSpec((1, D), lambda i, j: (0, 0)),
                pl.BlockSpec((None, D, QKV_TILE),
                             lambda i, j: ((per_group * g + j) // per_chip, 0, (per_group * g + j) % per_chip))]
    operands = [x, gain, w_qkv]
    aliases = {}
    if hf is not None:
        in_specs += [pl.BlockSpec(memory_space=pl.ANY)] * 2
        operands += [hf.reshape(N_GROUPS * dil, seg, D), qkv.reshape(N_GROUPS * dil, seg, 3 * D)]
        aliases = {3: 0, 4: 1}
    hf, qkv = _pcall(
        body, name=name, grid=(S // FOLD_TOKENS, per_group), in_specs=in_specs,
        out_specs=[pl.BlockSpec((dil, n, D), lambda i, j: (g, i, 0)),
                   pl.BlockSpec((dil, n, QKV_TILE), lambda i, j: (g, i, j))],
        out_shape=[jax.ShapeDtypeStruct((N_GROUPS * dil, seg, D), BF16),
                   jax.ShapeDtypeStruct((N_GROUPS * dil, seg, 3 * D), BF16)],
        scratch_shapes=[pltpu.VMEM((FOLD_TOKENS, D), BF16), pltpu.VMEM((D // LANES, FOLD_TOKENS, LANES), F32)],
        input_output_aliases=aliases,
        compiler_params=_params(("arbitrary", "arbitrary")))(*operands)
    return hf.reshape(N_GROUPS, S, D), qkv.reshape(N_GROUPS, S, 3 * D)


QKV_PIECES = QKV_SHARD // QKV_TILE


def _qkv_dw(name, p, hf, dqkv, tk=1024):
    S = hf.shape[1]
    per_group = 3 * D_MODEL // QKV_TILE
    tile = lambda s: QKV_PIECES * s + p
    return _mm(name, "tn", (N_CHIPS, S // tk),
               hf, pl.BlockSpec((None, tk, D_MODEL), lambda s, k: (tile(s) // per_group, k, 0)),
               dqkv, pl.BlockSpec((None, tk, QKV_TILE), lambda s, k: (tile(s) // per_group, k, tile(s) % per_group)),
               jax.ShapeDtypeStruct((N_CHIPS, D_MODEL, QKV_TILE), BF16),
               pl.BlockSpec((None, D_MODEL, QKV_TILE), lambda s, k: (s, 0, 0)), (D_MODEL, QKV_TILE))


def _qkv_dh(name, dqkv, w_qkv, tm=1024):
    S = dqkv.shape[1]
    per_chip = QKV_SHARD // QKV_TILE
    per_group = 3 * D_MODEL // QKV_TILE
    return _mm(name, "nt", (N_GROUPS, S // tm, per_group),
               dqkv, pl.BlockSpec((None, tm, QKV_TILE), lambda g, i, k: (g, i, k)),
               w_qkv, pl.BlockSpec((None, D_MODEL, QKV_TILE),
                                   lambda g, i, k: ((per_group * g + k) // per_chip, 0, (per_group * g + k) % per_chip)),
               jax.ShapeDtypeStruct((N_GROUPS, S, D_MODEL), F32),
               pl.BlockSpec((None, tm, D_MODEL), lambda g, i, k: (g, i, 0)), (tm, D_MODEL))


def _alibi_coef(g, h):
    vals = [-(2.0 ** (-8.0 * (gg * N_HEADS + h + 1) / (N_GROUPS * N_HEADS))) * ATTN_DILATIONS[gg]
            for gg in range(N_GROUPS)]
    return jnp.where(g == 0, vals[0], jnp.where(g == 1, vals[1], vals[2])).astype(F32)


def _blocks_per_segment(g, S):
    return jnp.right_shift(S // Q_BLOCK, 2 * g)


def _band_bias(pen):
    row = lax.broadcasted_iota(jnp.int32, (Q_BLOCK, 2 * Q_BLOCK), 0)
    col = lax.broadcasted_iota(jnp.int32, (Q_BLOCK, 2 * Q_BLOCK), 1)
    dist = Q_BLOCK + row - col
    valid = jnp.logical_and(dist >= 0, dist <= Q_BLOCK)
    base = jnp.where(valid, jnp.where(col < Q_BLOCK, pen, 0.0), NEG).astype(F32)
    return base, dist.astype(F32)


def _skewed(n_units, stages):
    state = {}
    for step in range(n_units + len(stages) - 1):
        for s, stage in enumerate(stages):
            u = step - s
            if 0 <= u < n_units:
                state[u] = stage(u, state.get(u))
                if s == len(stages) - 1:
                    del state[u]


def _attn_fwd(name, qkv, tq=512):
    S = qkv.shape[1]
    nqb = tq // Q_BLOCK
    scale = HEAD_DIM ** -0.5

    def body(q_ref, k_ref, kp_ref, v_ref, vp_ref, o_ref, lse_ref, kf_ref, vf_ref):
        g = pl.program_id(0)
        i = pl.program_id(1)
        nb = _blocks_per_segment(g, S)
        kf_ref[:Q_BLOCK] = kp_ref[...]
        kf_ref[Q_BLOCK:] = k_ref[...]
        vf_ref[:Q_BLOCK] = vp_ref[...]
        vf_ref[Q_BLOCK:] = v_ref[...]
        coefs = [_alibi_coef(g, h) for h in range(N_HEADS)]
        units = [(b, h) for b in range(nqb) for h in range(N_HEADS)]
        bias = {}

        def scores(u, _):
            b, h = units[u]
            if b not in bias:
                pen = jnp.where((i * nqb + b) % nb != 0, 0.0, NEG).astype(F32)
                bias.clear()
                bias[b] = _band_bias(pen)
            base, dist = bias[b]
            cols = slice(h * HEAD_DIM, (h + 1) * HEAD_DIM)
            q = q_ref[b * Q_BLOCK:(b + 1) * Q_BLOCK, cols]
            kk = kf_ref[b * Q_BLOCK:(b + 2) * Q_BLOCK, cols]
            return _dot(q, kk, "nt") * scale + (coefs[h] * dist + base)

        def softmax(u, s):
            m = jnp.max(s, axis=-1, keepdims=True)
            p = jnp.exp(s - m)
            den = jnp.sum(p, axis=-1, keepdims=True)
            return (p * (1.0 / den)).astype(BF16), m + jnp.log(den)

        def output(u, st):
            b, h = units[u]
            pn, lse = st
            cols = slice(h * HEAD_DIM, (h + 1) * HEAD_DIM)
            rows = slice(b * Q_BLOCK, (b + 1) * Q_BLOCK)
            o_ref[rows, cols] = _dot(pn, vf_ref[b * Q_BLOCK:(b + 2) * Q_BLOCK, cols], "nn")
            lse_ref[rows, h:h + 1] = lse

        _skewed(len(units), [scores, softmax, output])

    main = lambda c: pl.BlockSpec((None, tq, D_MODEL), lambda g, i: (g, i, c))
    prev = lambda c: pl.BlockSpec((None, Q_BLOCK, D_MODEL), lambda g, i: (g, jnp.maximum(i * nqb - 1, 0), c))
    return _pcall(
        body, name=name, grid=(N_GROUPS, S // tq),
        in_specs=[main(0), main(1), prev(1), main(2), prev(2)],
        out_specs=[pl.BlockSpec((None, tq, D_MODEL), lambda g, i: (g, i, 0)),
                   pl.BlockSpec((None, tq, N_HEADS), lambda g, i: (g, i, 0))],
        out_shape=[jax.ShapeDtypeStruct((N_GROUPS, S, D_MODEL), F32),
                   jax.ShapeDtypeStruct((N_GROUPS, S, N_HEADS), F32)],
        scratch_shapes=[pltpu.VMEM((tq + Q_BLOCK, D_MODEL), BF16), pltpu.VMEM((tq + Q_BLOCK, D_MODEL), BF16)],
        compiler_params=_params(("parallel", "parallel")))(qkv, qkv, qkv, qkv, qkv)


def _attn_bwd(name, qkv, do, lse, dl, tq=512):
    S = qkv.shape[1]
    nqb = tq // Q_BLOCK
    n_blocks = S // Q_BLOCK
    scale = HEAD_DIM ** -0.5
    KEYS = 2 * Q_BLOCK

    def body(q_ref, k_ref, v_ref, kp_ref, vp_ref, qn_ref, do_ref, don_ref, lse_ref, lsen_ref, dl_ref, dln_ref,
             out_ref, kf_ref, vf_ref, dk_ref, dv_ref):
        g = pl.program_id(0)
        i = pl.program_id(1)
        nb = _blocks_per_segment(g, S)
        kf_ref[:Q_BLOCK] = kp_ref[...]
        kf_ref[Q_BLOCK:] = k_ref[...]
        vf_ref[:Q_BLOCK] = vp_ref[...]
        vf_ref[Q_BLOCK:] = v_ref[...]
        coefs = [_alibi_coef(g, h) for h in range(N_HEADS)]
        units = [(h, b) for h in range(N_HEADS) for b in range(nqb + 1)]
        bias = {}

        def band(b):
            if b not in bias:
                blk = i * nqb + b
                ok = blk % nb != 0
                if b == nqb:
                    ok = jnp.logical_and(ok, blk < n_blocks)
                bias[b] = _band_bias(jnp.where(ok, 0.0, NEG).astype(F32))
            return bias[b]

        def operands(h, b):
            cols = slice(h * HEAD_DIM, (h + 1) * HEAD_DIM)
            if b == nqb:
                keys = slice(tq, tq + Q_BLOCK)
                return (qn_ref[:, cols], don_ref[:, cols], lsen_ref[:, h:h + 1], dln_ref[:, h:h + 1],
                        kf_ref[keys, cols], vf_ref[keys, cols])
            rows = slice(b * Q_BLOCK, (b + 1) * Q_BLOCK)
            keys = slice(b * Q_BLOCK, b * Q_BLOCK + KEYS)
            return (q_ref[rows, cols], do_ref[rows, cols], lse_ref[rows, h:h + 1], dl_ref[rows, h:h + 1],
                    kf_ref[keys, cols], vf_ref[keys, cols])

        def probs(u, _):
            h, b = units[u]
            q, do_b, lse_b, dl_b, kk, vv = operands(h, b)
            base, dist = band(b)
            if b == nqb:
                base, dist = base[:, :Q_BLOCK], dist[:, :Q_BLOCK]
            p = jnp.exp(_dot(q, kk, "nt") * scale + (coefs[h] * dist + base) - lse_b)
            return p, _dot(do_b, vv, "nt")

        def dscores(u, st):
            h, b = units[u]
            p, dp = st
            dl_b = operands(h, b)[3]
            return ((p * (dp - dl_b)) * scale).astype(BF16), p.astype(BF16)

        def grads(u, st):
            h, b = units[u]
            ds, pb = st
            q, do_b, _, _, kk, _ = operands(h, b)
            cols = slice(h * HEAD_DIM, (h + 1) * HEAD_DIM)
            if b < nqb:
                out_ref[b * Q_BLOCK:(b + 1) * Q_BLOCK, cols] = _dot(ds, kk, "nn").astype(out_ref.dtype)
            if b == 0:
                dk_ref[:Q_BLOCK] = _dot(ds[:, Q_BLOCK:], q, "tn")
                dv_ref[:Q_BLOCK] = _dot(pb[:, Q_BLOCK:], do_b, "tn")
                return None
            dk = _dot(ds, q, "tn")
            dv = _dot(pb, do_b, "tn")
            before = slice((b - 1) * Q_BLOCK, b * Q_BLOCK)
            dk_ref[before] += dk[:Q_BLOCK]
            dv_ref[before] += dv[:Q_BLOCK]
            if b < nqb:
                own = slice(b * Q_BLOCK, (b + 1) * Q_BLOCK)
                dk_ref[own] = dk[Q_BLOCK:]
                dv_ref[own] = dv[Q_BLOCK:]
            else:
                out_ref[:, D_MODEL + h * HEAD_DIM:D_MODEL + (h + 1) * HEAD_DIM] = dk_ref[...].astype(out_ref.dtype)
                out_ref[:, 2 * D_MODEL + h * HEAD_DIM:2 * D_MODEL + (h + 1) * HEAD_DIM] = (
                    dv_ref[...].astype(out_ref.dtype))
            return None

        _skewed(len(units), [probs, dscores, grads])

    nxt_blk = lambda i: jnp.minimum((i + 1) * nqb, n_blocks - 1)
    main = lambda c: pl.BlockSpec((None, tq, D_MODEL), lambda g, i: (g, i, c))
    prev = lambda c: pl.BlockSpec((None, Q_BLOCK, D_MODEL), lambda g, i: (g, jnp.maximum(i * nqb - 1, 0), c))
    row = pl.BlockSpec((None, tq, D_MODEL), lambda g, i: (g, i, 0))
    row_n = pl.BlockSpec((None, Q_BLOCK, D_MODEL), lambda g, i: (g, nxt_blk(i), 0))
    col = pl.BlockSpec((None, tq, N_HEADS), lambda g, i: (g, i, 0))
    col_n = pl.BlockSpec((None, Q_BLOCK, N_HEADS), lambda g, i: (g, nxt_blk(i), 0))
    return _pcall(
        body, name=name, grid=(N_GROUPS, S // tq),
        in_specs=[main(0), main(1), main(2), prev(1), prev(2), row_n, row, row_n, col, col_n, col, col_n],
        out_specs=pl.BlockSpec((None, tq, 3 * D_MODEL), lambda g, i: (g, i, 0)),
        out_shape=jax.ShapeDtypeStruct((N_GROUPS, S, 3 * D_MODEL), BF16),
        scratch_shapes=[pltpu.VMEM((tq + Q_BLOCK, D_MODEL), BF16), pltpu.VMEM((tq + Q_BLOCK, D_MODEL), BF16),
                        pltpu.VMEM((tq, HEAD_DIM), F32), pltpu.VMEM((tq, HEAD_DIM), F32)],
        compiler_params=_params(("parallel", "parallel")))(qkv, qkv, qkv, qkv, qkv, qkv, do, do, lse, lse, dl, dl)


def _group_weights(lse_ref):
    l0, l1, l2 = lse_ref[0], lse_ref[1], lse_ref[2]
    m = jnp.maximum(jnp.maximum(l0, l1), l2)
    e = [jnp.exp(l0 - m), jnp.exp(l1 - m), jnp.exp(l2 - m)]
    den = e[0] + e[1] + e[2]
    return [ei / den for ei in e]


MERGE_TOKENS = 512


def _folded_views(a, tb):
    _, S, C = a.shape
    views, specs = [], []
    for g, dil in enumerate(ATTN_DILATIONS):
        views.append(a.reshape(N_GROUPS * dil, S // dil, C))
        specs.append(pl.BlockSpec((dil, tb // dil, C), lambda i, g=g: (g, i, 0)))
    return views, specs


def _unfold_into(dst_ref, folded_refs):
    for g, (dil, ref) in enumerate(zip(ATTN_DILATIONS, folded_refs)):
        n = ref.shape[1]
        for r in range(dil):
            for c, cols in enumerate(_lane_chunks(ref.shape[2])):
                dst_ref[g, c, _strided_rows(r, n, dil), :] = ref[r, :, cols]


def _merge_fwd(name, o, lse, tb=MERGE_TOKENS):
    S = o.shape[1]

    def body(o0_ref, o1_ref, o2_ref, lse_ref, out_ref, o_ref):
        _unfold_into(o_ref, (o0_ref, o1_ref, o2_ref))
        w = _group_weights(lse_ref)
        for h in range(N_HEADS):
            cols = slice(h * HEAD_DIM, (h + 1) * HEAD_DIM)
            acc = w[0][:, h:h + 1] * o_ref[0, h]
            for gg in range(1, N_GROUPS):
                acc = acc + w[gg][:, h:h + 1] * o_ref[gg, h]
            out_ref[:, cols] = acc.astype(out_ref.dtype)

    views, specs = _folded_views(o, tb)
    return _pcall(body, name=name, grid=(S // tb,),
                  in_specs=specs + [pl.BlockSpec((N_GROUPS, tb, N_HEADS), lambda i: (0, i, 0))],
                  out_specs=pl.BlockSpec((tb, D_MODEL), lambda i: (i, 0)),
                  out_shape=jax.ShapeDtypeStruct((S, D_MODEL), BF16),
                  scratch_shapes=[pltpu.VMEM((N_GROUPS, N_HEADS, tb, HEAD_DIM), F32)],
                  compiler_params=_params(("parallel",)))(*views, lse)


def _merge_bwd(name, d_out, o, lse, tb=MERGE_TOKENS):
    S = o.shape[1]
    n_chunks = sum(ATTN_DILATIONS)

    def body(d_ref, o0_ref, o1_ref, o2_ref, lse_ref, do_hbm, dl_ref, o_ref, dt_ref, df_ref, sems):
        i = pl.program_id(0)
        _unfold_into(o_ref, (o0_ref, o1_ref, o2_ref))
        w = _group_weights(lse_ref)
        for h in range(N_HEADS):
            cols = slice(h * HEAD_DIM, (h + 1) * HEAD_DIM)
            dv = d_ref[:, cols]
            merged = w[0][:, h:h + 1] * o_ref[0, h]
            for gg in range(1, N_GROUPS):
                merged = merged + w[gg][:, h:h + 1] * o_ref[gg, h]
            dsum = jnp.sum(dv * merged, axis=-1, keepdims=True)
            for gg in range(N_GROUPS):
                wg = w[gg][:, h:h + 1]
                dt_ref[gg, h] = wg * dv
                dl_ref[gg, :, h:h + 1] = wg * dsum
        copies = []
        for gg, dil in enumerate(ATTN_DILATIONS):
            n = tb // dil
            for r in range(dil):
                for h, cols in enumerate(_lane_chunks(D_MODEL)):
                    df_ref[gg, r * n:(r + 1) * n, cols] = (
                        dt_ref[gg, h, _strided_rows(r, n, dil), :].astype(df_ref.dtype))
                cp = pltpu.make_async_copy(df_ref.at[gg, pl.ds(r * n, n)],
                                           do_hbm.at[gg, pl.ds(r * (S // dil) + i * n, n)], sems.at[len(copies)])
                cp.start()
                copies.append(cp)
        for cp in copies:
            cp.wait()

    views, specs = _folded_views(o, tb)
    small = pl.BlockSpec((N_GROUPS, tb, N_HEADS), lambda i: (0, i, 0))
    return _pcall(body, name=name, grid=(S // tb,),
                  in_specs=[pl.BlockSpec((tb, D_MODEL), lambda i: (i, 0))] + specs + [small],
                  out_specs=[pl.BlockSpec(memory_space=pl.ANY), small],
                  out_shape=[jax.ShapeDtypeStruct((N_GROUPS, S, D_MODEL), BF16),
                             jax.ShapeDtypeStruct((N_GROUPS, S, N_HEADS), F32)],
                  scratch_shapes=[pltpu.VMEM((N_GROUPS, N_HEADS, tb, HEAD_DIM), F32),
                                  pltpu.VMEM((N_GROUPS, N_HEADS, tb, HEAD_DIM), F32),
                                  pltpu.VMEM((N_GROUPS, tb, D_MODEL), BF16), pltpu.SemaphoreType.DMA((n_chunks,))],
                  compiler_params=_params(("arbitrary",)))(d_out, *views, lse)


def _shift_rows(a, k):
    return pltpu.roll(a, k % a.shape[0], axis=0)


def _pool_level(g, levels):
    return jnp.where(g == 0, levels[0], jnp.where(g == 1, levels[1], jnp.where(g == 2, levels[2], levels[3])))


def _pool_fwd(name, u, w_group, scale, x_in, tr=1024):
    S, D = u.shape
    H = POOL_HALO

    def body(u_ref, halo_ref, w_ref, sc_ref, x_ref, y_ref, z_ref, xo_ref):
        g = pl.program_id(0)
        i = pl.program_id(1)
        uv = u_ref[...]
        halo = jnp.where(i > 0, halo_ref[...], 0.0)
        s = jnp.concatenate([halo, uv], axis=0)
        levels = []
        for k in (1, 2, 4, 8):
            s = s + _shift_rows(s, k)
            levels.append(s[H:])
        t = i * tr + lax.broadcasted_iota(jnp.int32, (tr, 1), 0)
        cnt = jnp.minimum(t + 1, jnp.left_shift(2, g)).astype(F32)
        y = _pool_level(g, levels) / cnt - uv
        yb = y.astype(BF16)
        z = _dot(yb, w_ref[...], "nn")
        y_ref[...] = yb
        z_ref[...] = z
        xo_ref[...] = x_ref[...] + z * sc_ref[...]

    blk = pl.BlockSpec((tr, POOL_DIM), lambda g, i: (i, g))
    return _pcall(
        body, name=name, grid=(D // POOL_DIM, S // tr),
        in_specs=[blk, pl.BlockSpec((H, POOL_DIM), lambda g, i: (jnp.maximum(i * (tr // H) - 1, 0), g)),
                  pl.BlockSpec((None, POOL_DIM, POOL_DIM), lambda g, i: (g, 0, 0)),
                  pl.BlockSpec((1, POOL_DIM), lambda g, i: (0, g)), blk],
        out_specs=[blk, blk, blk],
        out_shape=[jax.ShapeDtypeStruct((S, D), BF16), jax.ShapeDtypeStruct((S, D), F32),
                   jax.ShapeDtypeStruct((S, D), F32)],
        compiler_params=_params(("parallel", "parallel")))(u, u, w_group, scale, x_in)


def _pool_bwd(name, d_out, z, y, scale, w_group, tr=1024):
    S, D = d_out.shape
    H = POOL_HALO

    def body(d_ref, dn_ref, z_ref, y_ref, sc_ref, w_ref, du_ref, dw_ref, dsc_ref):
        g = pl.program_id(0)
        i = pl.program_id(1)
        dv = d_ref[...]
        dz = jnp.concatenate([dv, dn_ref[...]], axis=0) * sc_ref[...]
        dzb = dz.astype(BF16)
        dy = _dot(dzb, w_ref[...], "nt")
        t = i * tr + lax.broadcasted_iota(jnp.int32, (tr + H, 1), 0)
        cnt = jnp.minimum(t + 1, jnp.left_shift(2, g)).astype(F32)
        s = jnp.where(t < S, dy / cnt, 0.0)
        levels = []
        for k in (1, 2, 4, 8):
            s = s + _shift_rows(s, -k)
            levels.append(s[:tr])
        du_ref[...] = (_pool_level(g, levels) - dy[:tr]).astype(du_ref.dtype)
        dw = _dot(y_ref[...], dzb[:tr], "tn")
        dsc = jnp.sum(dv * z_ref[...], axis=0, keepdims=True)

        @pl.when(i == 0)
        def _():
            dw_ref[...] = dw
            dsc_ref[...] = dsc

        @pl.when(i > 0)
        def _():
            dw_ref[...] += dw
            dsc_ref[...] += dsc

    blk = pl.BlockSpec((tr, POOL_DIM), lambda g, i: (i, g))
    vec = pl.BlockSpec((1, POOL_DIM), lambda g, i: (0, g))
    mat = pl.BlockSpec((None, POOL_DIM, POOL_DIM), lambda g, i: (g, 0, 0))
    nxt = pl.BlockSpec((H, POOL_DIM), lambda g, i: (jnp.minimum((i + 1) * (tr // H), S // H - 1), g))
    return _pcall(
        body, name=name, grid=(D // POOL_DIM, S // tr), in_specs=[blk, nxt, blk, blk, vec, mat],
        out_specs=[blk, mat, vec],
        out_shape=[jax.ShapeDtypeStruct((S, D), BF16), jax.ShapeDtypeStruct((D // POOL_DIM, POOL_DIM, POOL_DIM), F32),
                   jax.ShapeDtypeStruct((1, D), F32)],
        compiler_params=_params(("parallel", "arbitrary")))(d_out, d_out, z, y, scale, w_group)


def _row_tile(rows, cols, target_bytes=1 << 20):
    best = rows
    for tr in range(8, rows + 1, 8):
        if rows % tr == 0 and tr * cols * 4 <= target_bytes:
            best = tr
    return best if best * cols * 4 <= 4 * target_bytes else rows


def _adamw(name, w, g, m, v):
    R, C = w.shape
    tr = _row_tile(R, C) if R % 8 == 0 else R

    def body(w_ref, g_ref, m_ref, v_ref, go_ref, d_ref, mo_ref, vo_ref):
        gv = g_ref[...]
        m2 = ADAM_B1 * m_ref[...] + (1.0 - ADAM_B1) * gv
        v2 = ADAM_B2 * v_ref[...] + (1.0 - ADAM_B2) * (gv * gv)
        m_hat = m2 / (1.0 - ADAM_B1 ** ADAM_STEP)
        v_hat = v2 / (1.0 - ADAM_B2 ** ADAM_STEP)
        d_ref[...] = -ADAM_LR * (m_hat / (jnp.sqrt(v_hat) + ADAM_EPS) + ADAM_WD * w_ref[...])
        go_ref[...] = gv
        mo_ref[...] = m2
        vo_ref[...] = v2

    blk = pl.BlockSpec((tr, C), lambda i: (i, 0))
    sds = jax.ShapeDtypeStruct((R, C), F32)
    return _pcall(body, name=name, grid=(R // tr,), in_specs=[blk] * 4, out_specs=[blk] * 4, out_shape=[sds] * 4,
                  compiler_params=_params(("parallel",)))(w, g, m, v)


def _sum_leading(name, a, out_dtype):
    n, R, C = a.shape
    tr = _row_tile(R, C) if R % 16 == 0 else R
    if tr % 16:
        tr = R

    def body(a_ref, o_ref):
        acc = a_ref[0].astype(F32)
        for j in range(1, n):
            acc = acc + a_ref[j].astype(F32)
        o_ref[...] = acc.astype(o_ref.dtype)

    return _pcall(body, name=name, grid=(R // tr,), in_specs=[pl.BlockSpec((n, tr, C), lambda i: (0, i, 0))],
                  out_specs=pl.BlockSpec((tr, C), lambda i: (i, 0)), out_shape=jax.ShapeDtypeStruct((R, C), out_dtype),
                  compiler_params=_params(("parallel",)))(a)


def _cast_into_slot(name, w, layer, chip, dtype):
    _, R, C = w.shape
    tr = _row_tile(R, C, 2 << 20)
    if tr % 16:
        tr = R

    def body(c_ref, w_ref, o_ref):
        o_ref[...] = w_ref[...].astype(o_ref.dtype)

    grid_spec = pltpu.PrefetchScalarGridSpec(
        num_scalar_prefetch=1, grid=(R // tr,),
        in_specs=[pl.BlockSpec((None, tr, C), lambda i, c_ref: (layer, i, 0))],
        out_specs=pl.BlockSpec((None, tr, C), lambda i, c_ref: (c_ref[0], i, 0)))
    return _pcall(body, name=name, grid_spec=grid_spec, out_shape=jax.ShapeDtypeStruct((N_CHIPS, R, C), dtype),
                  compiler_params=_params(("parallel",)))(chip, w)


def _owner_sum(name, mine, landed, chip, core, out=None, layer=None, col=None):
    _, half, C = mine.shape
    k, n_col = col if col is not None else (0, 1)
    tr = _row_tile(half, C)
    if tr % 16:
        tr = half
    nrt = half // tr
    has_out = out is not None

    def body(*refs):
        m_ref, l_ref, o_ref = refs[2], refs[3], refs[-1]
        acc = m_ref[...].astype(F32)
        for j in range(3):
            acc = acc + l_ref[j].astype(F32)
        o_ref[...] = acc

    if layer is None:
        out_shape = jax.ShapeDtypeStruct((2 * half, n_col * C), F32)
        o_spec = pl.BlockSpec((tr, C), lambda i, ch, co: (co[0] * nrt + i, k))
    else:
        out_shape = jax.ShapeDtypeStruct((2, 2 * half, C), F32)
        o_spec = pl.BlockSpec((None, tr, C), lambda i, ch, co: (layer, co[0] * nrt + i, 0))
    in_specs = [pl.BlockSpec((None, tr, C), lambda i, ch, co: (ch[0], i, 0)),
                pl.BlockSpec((3, tr, C), lambda i, ch, co: (0, i, 0))]
    operands = [chip, core, mine, landed]
    aliases = {}
    if has_out:
        in_specs.append(ANY)
        operands.append(out)
        aliases = {4: 0}
    grid_spec = pltpu.PrefetchScalarGridSpec(num_scalar_prefetch=2, grid=(nrt,), in_specs=in_specs, out_specs=o_spec)
    return _pcall(body, name=name, grid_spec=grid_spec, out_shape=out_shape, input_output_aliases=aliases,
                  compiler_params=_params(("parallel",)))(*operands)


def _add_my_half(name, p, q, core):
    n, R, C = p.shape
    half = R // 2

    def body(c_ref, p_ref, q_ref, o_ref):
        o_ref[...] = (p_ref[...].astype(F32) + q_ref[...].astype(F32)).astype(o_ref.dtype)

    grid_spec = pltpu.PrefetchScalarGridSpec(
        num_scalar_prefetch=1, grid=(n,),
        in_specs=[pl.BlockSpec((None, half, C), lambda s, c_ref: (s, c_ref[0], 0)),
                  pl.BlockSpec((None, half, C), lambda s, c_ref: (s, 0, 0))],
        out_specs=pl.BlockSpec((None, half, C), lambda s, c_ref: (s, 0, 0)))
    return _pcall(body, name=name, grid_spec=grid_spec, out_shape=jax.ShapeDtypeStruct((n, half, C), BF16),
                  compiler_params=_params(("parallel",)))(core, p, q)


ANY = pl.BlockSpec(memory_space=pl.ANY)


def _place():
    x, y, c = lax.axis_index("x"), lax.axis_index("y"), lax.axis_index("c")
    other_chips = [(1 - x, y), (x, 1 - y), (1 - x, 1 - y)]
    return x, y, c, other_chips


def _gather_weights(name, bufs, split):
    n = len(bufs)

    def body(*refs):
        outs = refs[n:2 * n]
        ici_send, ici_recv, d2d_send, d2d_recv = refs[2 * n:]
        x, y, c, chips = _place()
        me = 2 * x + y
        sibling = (x, y, 1 - c)

        def piece(t, chip, core_half):
            if not split[t]:
                return outs[t].at[chip]
            half = outs[t].shape[1] // 2
            return outs[t].at[chip, pl.ds(core_half * half, half)]

        sends = []
        for t in range(n):
            for j, (px, py) in enumerate(chips):
                sends.append(pltpu.make_async_remote_copy(
                    src_ref=piece(t, me, c), dst_ref=piece(t, me, c), send_sem=ici_send.at[3 * t + j],
                    recv_sem=ici_recv.at[3 * t + j], device_id=(px, py, c), device_id_type=MESH))
        for cp in sends:
            cp.start()
        passed = []
        for t in range(n):
            for j, (px, py) in enumerate(chips):
                landed = piece(t, 2 * px + py, c)
                pltpu.make_async_remote_copy(
                    src_ref=landed, dst_ref=landed, send_sem=ici_send.at[3 * t + j],
                    recv_sem=ici_recv.at[3 * t + j], device_id=(px, py, c), device_id_type=MESH).wait_recv()
                if split[t]:
                    fwd = pltpu.make_async_remote_copy(
                        src_ref=landed, dst_ref=landed, send_sem=d2d_send.at[3 * t + j],
                        recv_sem=d2d_recv.at[3 * t + j], device_id=sibling, device_id_type=MESH)
                    fwd.start()
                    passed.append(fwd)
        for t in range(n):
            if split[t]:
                for j, (px, py) in enumerate(chips):
                    theirs = piece(t, 2 * px + py, 1 - c)
                    pltpu.make_async_remote_copy(
                        src_ref=theirs, dst_ref=theirs, send_sem=d2d_send.at[3 * t + j],
                        recv_sem=d2d_recv.at[3 * t + j], device_id=sibling, device_id_type=MESH).wait_recv()
        for cp in sends + passed:
            cp.wait_send()

    out_shape = [jax.ShapeDtypeStruct(b.shape, b.dtype) for b in bufs]
    return _pcall(body, name=name, in_specs=[ANY] * n, out_specs=[ANY] * n, out_shape=out_shape,
                  input_output_aliases={t: t for t in range(n)},
                  scratch_shapes=[pltpu.SemaphoreType.DMA((3 * n,)), pltpu.SemaphoreType.DMA((3 * n,)),
                                  pltpu.SemaphoreType.DMA((3 * n,)), pltpu.SemaphoreType.DMA((3 * n,))])(*bufs)


def _remote(src, dst, send, recv, k, to):
    return pltpu.make_async_remote_copy(src_ref=src, dst_ref=dst, send_sem=send.at[k], recv_sem=recv.at[k],
                                        device_id=to, device_id_type=MESH)


def _in_place(bufs):
    return dict(operands=bufs, out_shapes=[jax.ShapeDtypeStruct(b.shape, b.dtype) for b in bufs],
                aliases={t: t for t in range(len(bufs))})


def _gather_over_ici(bufs, split):
    n = len(bufs)

    def copies(ins, outs, send, recv, base=0):
        x, y, c, chips = _place()
        me = 2 * x + y
        out = []
        for t in range(n):
            piece = outs[t].at[me]
            if split[t]:
                half = outs[t].shape[1] // 2
                piece = outs[t].at[me, pl.ds(c * half, half)]
            for j, (px, py) in enumerate(chips):
                out.append(_remote(piece, piece, send, recv, base + 3 * t + j, (px, py, c)))
        return out

    return _Rider(n_copies=3 * n, copies=copies, **_in_place(bufs))


def _gather_to_sibling(bufs):
    n = len(bufs)

    def copies(ins, outs, send, recv, base=0):
        x, y, c, chips = _place()
        out = []
        for t in range(n):
            half = outs[t].shape[1] // 2
            for j, (px, py) in enumerate(chips):
                piece = outs[t].at[2 * px + py, pl.ds(c * half, half)]
                out.append(_remote(piece, piece, send, recv, base + 3 * t + j, (x, y, 1 - c)))
        return out

    return _Rider(n_copies=3 * n, copies=copies, **_in_place(bufs))


def _swap_halves_rider(parts):
    n = len(parts)

    def copies(ins, outs, send, recv, base=0):
        x, y, c, _ = _place()
        out = []
        for t in range(n):
            half = ins[t].shape[1] // 2
            out.append(_remote(ins[t].at[:, pl.ds((1 - c) * half, half)], outs[t], send, recv, base + t,
                               (x, y, 1 - c)))
        return out

    shapes = [jax.ShapeDtypeStruct((p.shape[0], p.shape[1] // 2, p.shape[2]), p.dtype) for p in parts]
    return _Rider(parts, shapes, {}, n, copies)


def _scatter_rider(sums):
    n = len(sums)

    def copies(ins, outs, send, recv, base=0):
        x, y, c, chips = _place()
        out = []
        for t in range(n):
            for j, (px, py) in enumerate(chips):
                out.append(_remote(ins[t].at[2 * px + py], outs[t].at[j], send, recv, base + 3 * t + j, (px, py, c)))
        return out

    shapes = [jax.ShapeDtypeStruct((3,) + s.shape[1:], s.dtype) for s in sums]
    return _Rider(sums, shapes, {}, 3 * n, copies)


def _share_rider(blocks, pieces):
    def copies(ins, outs, send, recv, base=0):
        x, y, c, _ = _place()
        out = []
        for k, (t, l, col) in enumerate(pieces):
            ref = outs[t] if l is None else outs[t].at[l]
            r2 = ref.shape[0] // 2
            piece = ref.at[pl.ds(c * r2, r2)]
            if col is not None:
                width = ref.shape[1] // col[1]
                piece = ref.at[pl.ds(c * r2, r2), pl.ds(col[0] * width, width)]
            out.append(_remote(piece, piece, send, recv, base + k, (x, y, 1 - c)))
        return out

    return _Rider(n_copies=len(pieces), copies=copies, **_in_place(blocks))


def _gather_small(name, v):
    def body(v_ref, out_ref, send_sem, recv_sem, local_sem):
        x, y, c, _ = _place()
        me = 4 * x + 2 * y + c
        flips = [(fx, fy, fc) for fx in (0, 1) for fy in (0, 1) for fc in (0, 1)][1:]
        local = pltpu.make_async_copy(v_ref, out_ref.at[me], local_sem)
        local.start()
        copies = []
        for j, (fx, fy, fc) in enumerate(flips):
            peer = (x ^ fx, y ^ fy, c ^ fc)
            copies.append(pltpu.make_async_remote_copy(
                src_ref=v_ref, dst_ref=out_ref.at[me], send_sem=send_sem.at[j], recv_sem=recv_sem.at[j],
                device_id=peer, device_id_type=MESH))
        for cp in copies:
            cp.start()
        for j, (fx, fy, fc) in enumerate(flips):
            peer = (x ^ fx, y ^ fy, c ^ fc)
            pltpu.make_async_remote_copy(
                src_ref=v_ref, dst_ref=out_ref.at[4 * peer[0] + 2 * peer[1] + peer[2]], send_sem=send_sem.at[j],
                recv_sem=recv_sem.at[j], device_id=peer, device_id_type=MESH).wait_recv()
        for cp in copies:
            cp.wait_send()
        local.wait()

    return _pcall(body, name=name, in_specs=[ANY], out_specs=ANY,
                  out_shape=jax.ShapeDtypeStruct((8,) + v.shape, v.dtype),
                  scratch_shapes=[pltpu.SemaphoreType.DMA((7,)), pltpu.SemaphoreType.DMA((7,)),
                                  pltpu.SemaphoreType.DMA])(v)


def _fold(a, dil):
    if dil == 1:
        return a
    S, C = a.shape
    return a.reshape(S // dil, dil, C).transpose(1, 0, 2).reshape(S, C)


def _unfold(a, dil):
    if dil == 1:
        return a
    S, C = a.shape
    return a.reshape(dil, S // dil, C).transpose(1, 0, 2).reshape(S, C)


def _fold_groups(a):
    return jnp.stack([_fold(a[g] if a.ndim == 3 else a, d) for g, d in enumerate(ATTN_DILATIONS)])


def _unfold_groups(a):
    return jnp.stack([_unfold(a[g], d) for g, d in enumerate(ATTN_DILATIONS)])


class _NoComm:
    def __init__(self, weights):
        self.weights = weights
        self.grads = {}

    def w(self, name):
        return self.weights[name]

    def run(self, fn, name, *args, **kw):
        return fn(name, *args, **kw)

    def grad(self, name, value):
        self.grads[name] = value


def _local_step(xs, tgt, attn_norm, ffn_norm, final_norm, comm):
    run = comm.run
    hf, qkv = None, None
    for g in range(N_GROUPS):
        hf, qkv = run(_norm_fold_qkv, "qkv_proj%d" % g, g, xs, attn_norm, comm.w("qkv"), hf, qkv)
    o_f, lse_f = run(_attn_fwd, "attn_fwd", qkv)
    lse_t = _unfold_groups(lse_f)
    merged = run(_merge_fwd, "merge_fwd", o_f, lse_t)
    x1 = run(_mm_rows, "attn_out", merged, comm.w("o"), F32, 512, res=xs)
    h1 = run(_rms_fwd, "norm_ffn0", x1, ffn_norm[0:1])
    gu0, act0 = run(_ffn_up, "ffn0_up", h1, comm.w("gu0"))
    x2 = run(_mm_rows, "ffn0_down", act0, comm.w("d0"), F32, 512, res=x1)
    h2 = run(_rms_fwd, "norm_pool", x2, comm.w("pool_norm"))
    u = run(_mm_rows, "pool_in", h2, comm.w("p"), F32, 512)
    y, z, x3 = run(_pool_fwd, "pool_fwd", u, comm.w("pg"), comm.w("pool_scale"), x2)
    h3 = run(_rms_fwd, "norm_ffn1", x3, ffn_norm[1:2])
    gu1, act1 = run(_ffn_up, "ffn1_up", h3, comm.w("gu1"))
    x4 = run(_mm_rows, "ffn1_down", act1, comm.w("d1"), F32, 512, res=x3)
    loss, dx4, d_final = run(_loss_bwd, "loss_head", x4, final_norm, tgt)

    dgu1 = run(_ffn_down_bwd, "ffn1_down_bwd", dx4, comm.w("d1"), gu1)
    comm.grad("d1", run(_mm_tn, "ffn1_down_dw", act1, dx4, FF_SHARD, 1024))
    comm.grad("gu1", run(_ffn_dw_up, "ffn1_up_dw", h3, dgu1))
    dh3 = run(_ffn_dh, "ffn1_up_dh", dgu1, comm.w("gu1"))
    dx3, d_ffn1 = run(_rms_bwd, "norm_ffn1_bwd", x3, ffn_norm[1:2], dh3, dx4)

    du, dw_pg, d_scale = run(_pool_bwd, "pool_bwd", dx3, z, y, comm.w("pool_scale"), comm.w("pg"))
    comm.grad("pg", dw_pg)
    comm.grad("p", run(_mm_tn, "pool_in_dw", h2, du, D_MODEL, 1024))
    dh2 = run(_mm_nt_rows, "pool_in_dh", du, comm.w("p"), 512)
    dx2, d_pool = run(_rms_bwd, "norm_pool_bwd", x2, comm.w("pool_norm"), dh2, dx3)

    dgu0 = run(_ffn_down_bwd, "ffn0_down_bwd", dx2, comm.w("d0"), gu0)
    comm.grad("d0", run(_mm_tn, "ffn0_down_dw", act0, dx2, FF_SHARD, 1024))
    comm.grad("gu0", run(_ffn_dw_up, "ffn0_up_dw", h1, dgu0))
    dh1 = run(_ffn_dh, "ffn0_up_dh", dgu0, comm.w("gu0"))
    dx1, d_ffn0 = run(_rms_bwd, "norm_ffn0_bwd", x1, ffn_norm[0:1], dh1, dx2)

    d_merged = run(_mm_nt_rows, "attn_out_dh", dx1, comm.w("o"), 512)
    comm.grad("o", run(_mm_tn, "attn_out_dw", merged, dx1, D_MODEL, 1024))
    do_f, dl_t = run(_merge_bwd, "merge_bwd", d_merged, o_f, lse_t)
    dqkv = run(_attn_bwd, "attn_bwd", qkv, do_f, lse_f, _fold_groups(dl_t))
    for p in range(QKV_PIECES):
        comm.grad("qkv%d" % p, run(_qkv_dw, "qkv_dw%d" % p, p, hf, dqkv))
    dhf = run(_qkv_dh, "qkv_dh", dqkv, comm.w("qkv"))
    grad_x, d_attn = run(_rms_bwd_folded, "norm_attn_bwd", xs, attn_norm, dhf, dx1)

    small = dict(attn=d_attn, ffn0=d_ffn0, ffn1=d_ffn1, final=d_final, pool=d_pool, scale=d_scale)
    return loss, grad_x, small


TENSORS = ("qkv", "o", "p", "pg", "gu0", "gu1", "d0", "d1")


def _block_of(name):
    if name[:-1] in ("gu", "d"):
        return name[:-1], int(name[-1]), None
    if name[:-1] == "qkv":
        return "qkv", None, (int(name[-1]), QKV_PIECES)
    return name, None, None


class _MeshComm:
    def __init__(self, bufs, chip_arr, core_arr, pg_rows):
        self.bufs = dict(bufs)
        self.chip_arr, self.core_arr, self.pg_rows = chip_arr, core_arr, pg_rows
        self.parts, self.sums, self.blocks = {}, {}, {}
        ici = lambda *names: lambda: self.gather(names, True)
        d2d = lambda *names: lambda: self.gather(names, False)
        swap = lambda *names: lambda: self.swap(names)
        scatter = lambda *names: lambda: self.scatter(names)
        share = lambda *names: lambda: self.share(names)
        self.plan = {
            "qkv_proj0": [ici("d0")],
            "qkv_proj1": [ici("o", "p", "pg", "pool_norm", "pool_scale")],
            "qkv_proj2": [d2d("d0", "o", "p", "pg")],
            "attn_fwd": [ici("gu0")],
            "merge_fwd": [d2d("gu0")],
            "ffn0_up": [ici("gu1")],
            "ffn0_down": [d2d("gu1"), ici("d1")],
            "pool_in": [d2d("d1")],
            "ffn1_up_dh": [swap("d1", "gu1")],
            "ffn0_down_bwd": [scatter("gu1")],
            "ffn0_down_dw": [scatter("d1")],
            "ffn0_up_dh": [swap("pg", "p", "d0", "gu0"), share("gu1", "d1")],
            "merge_bwd": [swap("o")],
            "attn_bwd": [scatter("pg", "p", "d0", "gu0", "o")],
            "qkv_dw0": [share("pg", "p", "d0", "gu0", "o")],
            "qkv_dw1": [swap("qkv0")],
            "qkv_dw2": [scatter("qkv0"), swap("qkv1")],
            "qkv_dh": [share("qkv0"), scatter("qkv1"), swap("qkv2")],
            "norm_attn_bwd": [share("qkv1"), scatter("qkv2")],
        }

    def gather(self, names, over_ici):
        bufs = [self.bufs[n] for n in names]
        rider = _gather_over_ici(bufs, [n in TENSORS for n in names]) if over_ici else _gather_to_sibling(bufs)
        return rider, lambda res: self.bufs.update(zip(names, res))

    def swap(self, names):
        def done(res):
            for n, q in zip(names, res):
                self.sums[n] = _add_my_half("chip_sum_" + n, self.parts[n], q, self.core_arr)
        return _swap_halves_rider([self.parts[n] for n in names]), done

    def scatter(self, names):
        def done(res):
            for n, landed in zip(names, res):
                key, layer, col = _block_of(n)
                self.blocks[key] = _owner_sum("owner_sum_" + n, self.sums[n], landed, self.chip_arr, self.core_arr,
                                              out=self.blocks.get(key), layer=layer, col=col)
        return _scatter_rider([self.sums[n] for n in names]), done

    def share(self, names):
        keys, pieces = [], []
        for n in names:
            key, layer, col = _block_of(n)
            if key not in keys:
                keys.append(key)
            pieces.append((keys.index(key), layer, col))
        return _share_rider([self.blocks[k] for k in keys], pieces), lambda res: self.blocks.update(zip(keys, res))

    def alone(self, name, made):
        rider, done = made
        done(_comm_only(name, rider))

    def run(self, fn, name, *args, **kw):
        made = [make() for make in self.plan.get(name, [])]
        if not made:
            return fn(name, *args, **kw)
        riders = [r for r, _ in made]
        out = _ride(riders[0] if len(riders) == 1 else _riders(*riders), fn, name, *args, **kw)
        for r, done in made:
            done(r.results)
        return out

    def w(self, name):
        b = self.bufs[name]
        if name in ("o", "p", "d0", "d1"):
            return b.reshape(-1, b.shape[-1])
        if name == "pg":
            b = b.reshape(N_CHIPS, 4, self.pg_rows, POOL_DIM).transpose(1, 0, 2, 3)
            return b.reshape(4, POOL_DIM, POOL_DIM)
        if name in ("pool_norm", "pool_scale"):
            return b.reshape(1, -1)
        return b

    def grad(self, name, value):
        if name == "pg":
            value = value.reshape(4, N_CHIPS, self.pg_rows, POOL_DIM).transpose(1, 0, 2, 3)
            value = value.reshape(N_CHIPS, 4 * self.pg_rows, POOL_DIM).astype(BF16)
        elif name in ("o", "p", "d0", "d1"):
            value = value.reshape(N_CHIPS, value.shape[0] // N_CHIPS, value.shape[1])
        self.parts[name] = value


def kernel(x, attn_norm, w_qkv, w_attn_out, pool_norm, w_pool_in, w_pool_group, pool_scale, ffn_norm, w_ffn_gate_up, w_ffn_down, final_norm, loss_target, m_attn_norm, m_w_qkv, m_w_attn_out, m_pool_norm, m_w_pool_in, m_w_pool_group, m_pool_scale, m_ffn_norm, m_w_ffn_gate_up, m_w_ffn_down, m_final_norm, v_attn_norm, v_w_qkv, v_w_attn_out, v_pool_norm, v_w_pool_in, v_w_pool_group, v_pool_scale, v_ffn_norm, v_w_ffn_gate_up, v_w_ffn_down, v_final_norm):
    D = D_MODEL
    chip = 2 * lax.axis_index("x") + lax.axis_index("y")
    core = lax.axis_index("c")
    pg_rows = w_pool_group.shape[2]

    chip_arr = chip.astype(jnp.int32).reshape(1)
    core_arr = core.astype(jnp.int32).reshape(1)

    shards = [(w_qkv, 0), (w_attn_out, 0), (w_pool_in, 0), (w_pool_group.reshape(1, 4 * pg_rows, POOL_DIM), 0),
              (w_ffn_gate_up, 0), (w_ffn_gate_up, 1), (w_ffn_down, 0), (w_ffn_down, 1)]
    bufs = {nm: _cast_into_slot("cast_" + nm, s, l, chip_arr, BF16) for nm, (s, l) in zip(TENSORS, shards)}
    bufs["pool_norm"] = _cast_into_slot("place_pool_norm", pool_norm[None], 0, chip_arr, F32)
    bufs["pool_scale"] = _cast_into_slot("place_pool_scale", pool_scale[None], 0, chip_arr, F32)
    bufs["qkv"] = _gather_weights("gather_qkv", [bufs["qkv"]], [True])[0]

    comm = _MeshComm(bufs, chip_arr, core_arr, pg_rows)
    loss_part, grad_x, small = _local_step(x[0], loss_target[0], attn_norm, ffn_norm, final_norm.reshape(1, D), comm)
    comm.alone("halves_to_sibling", comm.share(["qkv%d" % (QKV_PIECES - 1)]))
    g_w_qkv, g_w_o, g_w_p, g_w_pg, g_w_gu, g_w_d = [comm.blocks[k] for k in ("qkv", "o", "p", "pg", "gu", "d")]

    rows = jnp.concatenate([small["attn"], small["ffn0"], small["ffn1"], small["final"], small["pool"],
                            small["scale"], jnp.zeros((2, D), F32)], axis=0)
    all_rows = _gather_small("gather_gain_grads", rows)
    tot = _sum_leading("sum_gain_grads", all_rows, F32)
    g_attn_norm = tot[0:1]
    g_ffn_norm = tot[1:3]
    g_final_norm = tot[3]
    g_pool_norm = lax.dynamic_slice(tot, (4, chip * POOL_DIM), (1, POOL_DIM))
    g_pool_scale = lax.dynamic_slice(tot, (5, chip * POOL_DIM), (1, POOL_DIM))

    loss = lax.psum(loss_part[0, 0], ("x", "y", "c"))

    def update(name, w, g, m, v):
        shape = w.shape
        cols = shape[-1]
        as2d = lambda a: a.reshape(-1, cols)
        return [a.reshape(shape) for a in _adamw("adamw_" + name, as2d(w), as2d(g), as2d(m), as2d(v))]

    results = [
        update("attn_norm", attn_norm, g_attn_norm, m_attn_norm, v_attn_norm),
        update("w_qkv", w_qkv, g_w_qkv, m_w_qkv, v_w_qkv),
        update("w_attn_out", w_attn_out, g_w_o, m_w_attn_out, v_w_attn_out),
        update("pool_norm", pool_norm, g_pool_norm, m_pool_norm, v_pool_norm),
        update("w_pool_in", w_pool_in, g_w_p, m_w_pool_in, v_w_pool_in),
        update("w_pool_group", w_pool_group, g_w_pg, m_w_pool_group, v_w_pool_group),
        update("pool_scale", pool_scale, g_pool_scale, m_pool_scale, v_pool_scale),
        update("ffn_norm", ffn_norm, g_ffn_norm, m_ffn_norm, v_ffn_norm),
        update("w_ffn_gate_up", w_ffn_gate_up, g_w_gu, m_w_ffn_gate_up, v_w_ffn_gate_up),
        update("w_ffn_down", w_ffn_down, g_w_d, m_w_ffn_down, v_w_ffn_down),
        update("final_norm", final_norm, g_final_norm, m_final_norm, v_final_norm),
    ]
    grads = [r[0] for r in results]
    deltas = [r[1] for r in results]
    new_m = [r[2] for r in results]
    new_v = [r[3] for r in results]
    return (loss, grad_x[None], *grads, *deltas, *new_m, *new_v)
```

```python
import functools

import jax
import jax.numpy as jnp
from jax import lax
from jax.experimental import pallas as pl
from jax.experimental.pallas import tpu as pltpu

F32 = jnp.float32
BF16 = jnp.bfloat16
MESH = pl.DeviceIdType.MESH

D_MODEL = 1024
N_HEADS = 8
HEAD_DIM = 128
Q_BLOCK = 128
ATTN_DILATIONS = (1, 4, 16)
N_GROUPS = 3
D_FF = 2816
N_CHIPS = 4
FF_SHARD = 2 * D_FF // N_CHIPS
QKV_SHARD = 3 * 3 * D_MODEL // N_CHIPS
QKV_TILE = 768
POOL_WINDOWS = (2, 4, 8, 16)
POOL_DIM = 256
POOL_HALO = 16
RMS_EPS = 1e-6
NEG = -1e30
ADAM_LR = 0.001
ADAM_B1 = 0.9
ADAM_B2 = 0.999
ADAM_EPS = 1e-08
ADAM_WD = 0.01
ADAM_STEP = 10
VMEM_LIMIT = 56 * 1024 * 1024

_DN = {
    "nn": (((1,), (0,)), ((), ())),
    "nt": (((1,), (1,)), ((), ())),
    "tn": (((0,), (0,)), ((), ())),
}


class _Rider:
    def __init__(self, operands, out_shapes, aliases, n_copies, copies):
        self.operands = list(operands)
        self.out_shapes = list(out_shapes)
        self.aliases = dict(aliases)
        self.n_copies = n_copies
        self.copies = copies
        self.results = None


def _riders(*riders):
    operands, out_shapes, aliases, offsets = [], [], {}, []
    n = 0
    for r in riders:
        offsets.append((len(operands), len(out_shapes), n))
        aliases.update({len(operands) + i: len(out_shapes) + o for i, o in r.aliases.items()})
        operands += r.operands
        out_shapes += r.out_shapes
        n += r.n_copies

    def copies(ins, outs, send, recv, base=0):
        out = []
        for r, (i0, o0, s0) in zip(riders, offsets):
            out += r.copies(ins[i0:i0 + len(r.operands)], outs[o0:o0 + len(r.out_shapes)], send, recv, base + s0)
        return out

    both = _Rider(operands, out_shapes, aliases, n, copies)
    both.parts = (riders, offsets)
    return both


_pending_rider = []


def _ride(rider, fn, *args, **kw):
    _pending_rider.append(rider)
    out = fn(*args, **kw)
    assert not _pending_rider
    if hasattr(rider, "parts"):
        for r, (_, o0, _) in zip(*rider.parts):
            r.results = rider.results[o0:o0 + len(r.out_shapes)]
    return out


def _pcall(body, **kw):
    if not _pending_rider:
        return pl.pallas_call(body, **kw)
    rider = _pending_rider.pop()
    grid = kw.get("grid", ())
    in_specs = list(kw.get("in_specs", []))
    single = not isinstance(kw["out_shape"], (list, tuple))
    out_shape = [kw["out_shape"]] if single else list(kw["out_shape"])
    out_specs = [kw["out_specs"]] if single else list(kw["out_specs"])
    scratch = list(kw.get("scratch_shapes", []))
    n_in, n_out, n_scr = len(in_specs), len(out_shape), len(scratch)
    r_in, r_out = len(rider.operands), len(rider.out_shapes)

    def wrapped(*refs):
        ins, rins = refs[:n_in], refs[n_in:n_in + r_in]
        outs = refs[n_in + r_in:n_in + r_in + n_out]
        routs = refs[n_in + r_in + n_out:n_in + r_in + n_out + r_out]
        scr = refs[n_in + r_in + n_out + r_out:n_in + r_in + n_out + r_out + n_scr]
        send, recv = refs[-2:]
        ids = [pl.program_id(a) for a in range(len(grid))]
        first, last = True, True
        for a, pid in enumerate(ids):
            first = jnp.logical_and(first, pid == 0)
            last = jnp.logical_and(last, pid == grid[a] - 1)
        if grid:
            @pl.when(first)
            def _():
                for cp in rider.copies(rins, routs, send, recv):
                    cp.start()
        else:
            for cp in rider.copies(rins, routs, send, recv):
                cp.start()
        body(*ins, *outs, *scr)
        if grid:
            @pl.when(last)
            def _():
                for cp in rider.copies(rins, routs, send, recv):
                    cp.wait()
        else:
            for cp in rider.copies(rins, routs, send, recv):
                cp.wait()

    any_spec = pl.BlockSpec(memory_space=pl.ANY)
    kw2 = dict(kw)
    kw2.update(
        in_specs=in_specs + [any_spec] * r_in, out_specs=out_specs + [any_spec] * r_out,
        out_shape=out_shape + rider.out_shapes,
        scratch_shapes=scratch + [pltpu.SemaphoreType.DMA((rider.n_copies,)),
                                  pltpu.SemaphoreType.DMA((rider.n_copies,))],
        input_output_aliases={**kw.get("input_output_aliases", {}),
                              **{n_in + i: n_out + o for i, o in rider.aliases.items()}})
    if grid:
        kw2["compiler_params"] = _params(("arbitrary",) * len(grid))
    call = pl.pallas_call(wrapped, **kw2)

    def run(*operands):
        res = call(*operands, *rider.operands)
        rider.results = list(res[n_out:])
        return res[0] if single else list(res[:n_out])

    return run


def _comm_only(name, rider):
    def make():
        return _pcall(lambda: None, name=name, in_specs=[], out_specs=[], out_shape=[])()
    _ride(rider, make)
    return rider.results


def _params(sem):
    return pltpu.CompilerParams(dimension_semantics=sem, vmem_limit_bytes=VMEM_LIMIT)


def _dot(a, b, mode):
    return lax.dot_general(a, b, _DN[mode], preferred_element_type=F32)


def _mm(name, mode, grid, a, a_spec, b, b_spec, out_shape, o_spec, acc_shape, res=None, res_spec=None):
    nk = grid[-1]
    has_res = res is not None

    def body(*refs):
        a_ref, b_ref = refs[0], refs[1]
        res_ref = refs[2] if has_res else None
        o_ref = refs[2 + has_res]
        part = _dot(a_ref[...].astype(BF16), b_ref[...].astype(BF16), mode)
        if nk == 1:
            if has_res:
                part = part + res_ref[...]
            o_ref[...] = part.astype(o_ref.dtype)
            return
        acc_ref = refs[3 + has_res]
        k = pl.program_id(len(grid) - 1)

        @pl.when(k == 0)
        def _():
            acc_ref[...] = part

        @pl.when(k > 0)
        def _():
            acc_ref[...] += part

        @pl.when(k == nk - 1)
        def _():
            r = acc_ref[...]
            if has_res:
                r = r + res_ref[...]
            o_ref[...] = r.astype(o_ref.dtype)

    in_specs = [a_spec, b_spec] + ([res_spec] if has_res else [])
    operands = [a, b] + ([res] if has_res else [])
    scratch = [] if nk == 1 else [pltpu.VMEM(acc_shape, F32)]
    sem = ("parallel",) * (len(grid) - 1) + ("arbitrary",)
    return _pcall(body, name=name, grid=grid, in_specs=in_specs, out_specs=o_spec, out_shape=out_shape,
                  scratch_shapes=scratch, compiler_params=_params(sem))(*operands)


def _mm_rows(name, a, b, out_dtype, tm, res=None):
    M, K = a.shape
    N = b.shape[1]
    return _mm(name, "nn", (M // tm, 1), a, pl.BlockSpec((tm, K), lambda i, k: (i, 0)),
               b, pl.BlockSpec((K, N), lambda i, k: (0, 0)),
               jax.ShapeDtypeStruct((M, N), out_dtype), pl.BlockSpec((tm, N), lambda i, k: (i, 0)), None,
               res=res, res_spec=pl.BlockSpec((tm, N), lambda i, k: (i, 0)))


def _mm_nt_rows(name, a, b, tm):
    M, N = a.shape
    K = b.shape[0]
    return _mm(name, "nt", (M // tm, 1), a, pl.BlockSpec((tm, N), lambda i, k: (i, 0)),
               b, pl.BlockSpec((K, N), lambda i, k: (0, 0)),
               jax.ShapeDtypeStruct((M, K), F32), pl.BlockSpec((tm, K), lambda i, k: (i, 0)), None)


def _mm_tn(name, a, b, tm, tk):
    T, M = a.shape
    N = b.shape[1]
    return _mm(name, "tn", (M // tm, T // tk), a, pl.BlockSpec((tk, tm), lambda i, k: (k, i)),
               b, pl.BlockSpec((tk, N), lambda i, k: (k, 0)),
               jax.ShapeDtypeStruct((M, N), BF16), pl.BlockSpec((tm, N), lambda i, k: (i, 0)), (tm, N))


def _rms_fwd(name, x, g, tr=512):
    S, D = x.shape

    def body(x_ref, g_ref, o_ref):
        xf = x_ref[...]
        r = lax.rsqrt(jnp.mean(xf * xf, axis=-1, keepdims=True) + RMS_EPS)
        o_ref[...] = ((xf * r) * g_ref[...]).astype(o_ref.dtype)

    return _pcall(body, name=name, grid=(S // tr,),
                  in_specs=[pl.BlockSpec((tr, D), lambda i: (i, 0)), pl.BlockSpec((1, D), lambda i: (0, 0))],
                  out_specs=pl.BlockSpec((tr, D), lambda i: (i, 0)),
                  out_shape=jax.ShapeDtypeStruct((S, D), BF16), compiler_params=_params(("parallel",)))(x, g)


def _rms_bwd(name, x, g, dh, dres, tr=512):
    S, D = x.shape

    def body(x_ref, g_ref, dh_ref, dres_ref, dx_ref, dg_ref):
        i = pl.program_id(0)
        xf = x_ref[...]
        r = lax.rsqrt(jnp.mean(xf * xf, axis=-1, keepdims=True) + RMS_EPS)
        xh = xf * r
        dh_v = dh_ref[...]
        dg = jnp.sum(dh_v * xh, axis=0, keepdims=True)
        t = dh_v * g_ref[...]
        dx_ref[...] = dres_ref[...] + r * (t - xh * jnp.mean(t * xh, axis=-1, keepdims=True))

        @pl.when(i == 0)
        def _():
            dg_ref[...] = dg

        @pl.when(i > 0)
        def _():
            dg_ref[...] += dg

    row = pl.BlockSpec((tr, D), lambda i: (i, 0))
    vec = pl.BlockSpec((1, D), lambda i: (0, 0))
    return _pcall(body, name=name, grid=(S // tr,), in_specs=[row, vec, row, row], out_specs=[row, vec],
                  out_shape=[jax.ShapeDtypeStruct((S, D), F32), jax.ShapeDtypeStruct((1, D), F32)],
                  compiler_params=_params(("arbitrary",)))(x, g, dh, dres)


def _rms_bwd_folded(name, x, g, dhf, dres, tb=512):
    S, D = x.shape

    def body(x_ref, g_ref, d0_ref, d1_ref, d2_ref, dres_ref, dx_ref, dg_ref, dh_ref):
        i = pl.program_id(0)
        chunks = _lane_chunks(D)
        for c, cols in enumerate(chunks):
            dh_ref[c] = d0_ref[0, :, cols]
        for dil, ref in ((ATTN_DILATIONS[1], d1_ref), (ATTN_DILATIONS[2], d2_ref)):
            n = tb // dil
            for k in range(dil):
                for c, cols in enumerate(chunks):
                    dh_ref[c, _strided_rows(k, n, dil), :] += ref[k, :, cols]
        xf = x_ref[...]
        r = lax.rsqrt(jnp.mean(xf * xf, axis=-1, keepdims=True) + RMS_EPS)
        xh = xf * r
        dh_v = jnp.concatenate([dh_ref[c] for c in range(len(chunks))], axis=1)
        dg = jnp.sum(dh_v * xh, axis=0, keepdims=True)
        t = dh_v * g_ref[...]
        dx_ref[...] = dres_ref[...] + r * (t - xh * jnp.mean(t * xh, axis=-1, keepdims=True))

        @pl.when(i == 0)
        def _():
            dg_ref[...] = dg

        @pl.when(i > 0)
        def _():
            dg_ref[...] += dg

    views, specs = _folded_views(dhf, tb)
    row = pl.BlockSpec((tb, D), lambda i: (i, 0))
    vec = pl.BlockSpec((1, D), lambda i: (0, 0))
    return _pcall(body, name=name, grid=(S // tb,), in_specs=[row, vec] + specs + [row], out_specs=[row, vec],
                  out_shape=[jax.ShapeDtypeStruct((S, D), F32), jax.ShapeDtypeStruct((1, D), F32)],
                  scratch_shapes=[pltpu.VMEM((D // LANES, tb, LANES), F32)],
                  compiler_params=_params(("arbitrary",)))(x, g, *views, dres)


def _loss_bwd(name, x, g, tgt, tr=512):
    S, D = x.shape

    def body(x_ref, g_ref, t_ref, loss_ref, dx_ref, dg_ref):
        i = pl.program_id(0)
        xf = x_ref[...]
        r = lax.rsqrt(jnp.mean(xf * xf, axis=-1, keepdims=True) + RMS_EPS)
        xh = xf * r
        gv = g_ref[...]
        e = xh * gv - t_ref[...]
        lp = 0.5 * jnp.sum(jnp.mean(e * e, axis=-1, keepdims=True), axis=0, keepdims=True)
        dy = e * (1.0 / D)
        dg = jnp.sum(dy * xh, axis=0, keepdims=True)
        t = dy * gv
        dx_ref[...] = r * (t - xh * jnp.mean(t * xh, axis=-1, keepdims=True))

        @pl.when(i == 0)
        def _():
            dg_ref[...] = dg
            loss_ref[...] = lp

        @pl.when(i > 0)
        def _():
            dg_ref[...] += dg
            loss_ref[...] += lp

    row = pl.BlockSpec((tr, D), lambda i: (i, 0))
    vec = pl.BlockSpec((1, D), lambda i: (0, 0))
    one = pl.BlockSpec((1, 1), lambda i: (0, 0))
    return _pcall(body, name=name, grid=(S // tr,), in_specs=[row, vec, row], out_specs=[one, row, vec],
                  out_shape=[jax.ShapeDtypeStruct((1, 1), F32), jax.ShapeDtypeStruct((S, D), F32),
                             jax.ShapeDtypeStruct((1, D), F32)],
                  compiler_params=_params(("arbitrary",)))(x, g, tgt)


def _sigmoid(v):
    return 0.5 * jnp.tanh(0.5 * v) + 0.5


def _ffn_up(name, h, w_gu, tm=512):
    S, D = h.shape
    W = FF_SHARD

    def body(h_ref, wg_ref, wu_ref, gu_ref, act_ref):
        hv = h_ref[...]
        gate = _dot(hv, wg_ref[...], "nn")
        up = _dot(hv, wu_ref[...], "nn")
        gu_ref[0] = gate.astype(gu_ref.dtype)
        gu_ref[1] = up.astype(gu_ref.dtype)
        sig = _sigmoid(gate)
        act_ref[...] = ((gate * sig) * up).astype(act_ref.dtype)

    return _pcall(
        body, name=name, grid=(2, S // tm),
        in_specs=[pl.BlockSpec((tm, D), lambda j, i: (i, 0)),
                  pl.BlockSpec((None, D, W), lambda j, i: (j, 0, 0)),
                  pl.BlockSpec((None, D, W), lambda j, i: (j + 2, 0, 0))],
        out_specs=[pl.BlockSpec((2, tm, W), lambda j, i: (0, i, j)), pl.BlockSpec((tm, W), lambda j, i: (i, j))],
        out_shape=[jax.ShapeDtypeStruct((2, S, D_FF), BF16), jax.ShapeDtypeStruct((S, D_FF), BF16)],
        compiler_params=_params(("parallel", "parallel")))(h, w_gu, w_gu)


def _ffn_down_bwd(name, dx, w_down, gu, tm=512):
    S, D = dx.shape
    W = FF_SHARD

    def body(dx_ref, w_ref, gu_ref, dgu_ref):
        dact = _dot(dx_ref[...].astype(BF16), w_ref[...], "nt")
        gate = gu_ref[0].astype(F32)
        up = gu_ref[1].astype(F32)
        sig = _sigmoid(gate)
        silu = gate * sig
        dgu_ref[0] = (dact * up * (sig * (1.0 + gate * (1.0 - sig)))).astype(dgu_ref.dtype)
        dgu_ref[1] = (dact * silu).astype(dgu_ref.dtype)

    blk = pl.BlockSpec((2, tm, W), lambda j, i: (0, i, j))
    return _pcall(
        body, name=name, grid=(2, S // tm),
        in_specs=[pl.BlockSpec((tm, D), lambda j, i: (i, 0)), pl.BlockSpec((W, D), lambda j, i: (j, 0)), blk],
        out_specs=blk, out_shape=jax.ShapeDtypeStruct((2, S, D_FF), BF16),
        compiler_params=_params(("parallel", "parallel")))(dx, w_down, gu)


def _ffn_dw_up(name, h, dgu, tk=2048):
    S, D = h.shape
    W = FF_SHARD
    tk = min(tk, S)
    return _mm(name, "tn", (N_CHIPS, S // tk), h, pl.BlockSpec((tk, D), lambda s, k: (k, 0)),
               dgu, pl.BlockSpec((None, tk, W), lambda s, k: (s // 2, k, s % 2)),
               jax.ShapeDtypeStruct((N_CHIPS, D, W), BF16), pl.BlockSpec((None, D, W), lambda s, k: (s, 0, 0)),
               (D, W))


def _ffn_dh(name, dgu, w_gu, tm=512):
    S = dgu.shape[1]
    W = FF_SHARD

    def body(a_ref, w_hbm, o_ref, w_ref, acat_ref, sems):
        @pl.when(pl.program_id(0) == 0)
        def _():
            copies = [pltpu.make_async_copy(w_hbm.at[s], w_ref.at[:, pl.ds(s * W, W)], sems.at[s])
                      for s in range(N_CHIPS)]
            for cp in copies:
                cp.start()
            for cp in copies:
                cp.wait()

        acat_ref[:, :D_FF] = a_ref[0]
        acat_ref[:, D_FF:] = a_ref[1]
        o_ref[...] = _dot(acat_ref[...], w_ref[...], "nt")

    return _pcall(body, name=name, grid=(S // tm,),
                  in_specs=[pl.BlockSpec((2, tm, D_FF), lambda i: (0, i, 0)), pl.BlockSpec(memory_space=pl.ANY)],
                  out_specs=pl.BlockSpec((tm, D_MODEL), lambda i: (i, 0)),
                  out_shape=jax.ShapeDtypeStruct((S, D_MODEL), F32),
                  scratch_shapes=[pltpu.VMEM((D_MODEL, 2 * D_FF), BF16), pltpu.VMEM((tm, 2 * D_FF), BF16),
                                  pltpu.SemaphoreType.DMA((N_CHIPS,))],
                  compiler_params=_params(("arbitrary",)))(dgu, w_gu)


FOLD_TOKENS = 1024
LANES = 128


def _lane_chunks(width):
    return [slice(c * LANES, (c + 1) * LANES) for c in range(width // LANES)]


def _strided_rows(r, n, dil):
    return pl.ds(r, n) if dil == 1 else pl.ds(r, n, stride=dil)


def _norm_fold_qkv(name, g, x, gain, w_qkv, hf, qkv):
    S, D = x.shape
    dil = ATTN_DILATIONS[g]
    n = FOLD_TOKENS // dil
    seg = S // dil
    per_chip = QKV_SHARD // QKV_TILE
    per_group = 3 * D_MODEL // QKV_TILE

    def body(*refs):
        x_ref, g_ref, w_ref = refs[:3]
        hf_ref, qkv_ref, a_ref, xc_ref = refs[-4:]
        j = pl.program_id(1)

        @pl.when(j == 0)
        def _():
            xf = x_ref[...]
            inv = lax.rsqrt(jnp.mean(xf * xf, axis=-1, keepdims=True) + RMS_EPS)
            h = (xf * inv) * g_ref[...]
            if dil == 1:
                a_ref[...] = h.astype(BF16)
            else:
                for c, cols in enumerate(_lane_chunks(D)):
                    xc_ref[c] = h[:, cols]
                for r in range(dil):
                    for c, cols in enumerate(_lane_chunks(D)):
                        a_ref[r * n:(r + 1) * n, cols] = xc_ref[c, _strided_rows(r, n, dil), :].astype(BF16)
            for r in range(dil):
                hf_ref[r] = a_ref[r * n:(r + 1) * n, :]

        res = _dot(a_ref[...], w_ref[...], "nn")
        for r in range(dil):
            qkv_ref[r] = res[r * n:(r + 1) * n].astype(qkv_ref.dtype)

    in_specs = [pl.BlockSpec((FOLD_TOKENS, D), lambda i, j: (i, 0)), pl.BlockSpec((1, D), lambda i, j: (0, 0)),
                pl.BlockSpec((None, D, QKV_TILE),
                             lambda i, j: ((per_group * g + j) // per_chip, 0, (per_group * g + j) % per_chip))]
    operands = [x, gain, w_qkv]
    aliases = {}
    if hf is not None:
        in_specs += [pl.BlockSpec(memory_space=pl.ANY)] * 2
        operands += [hf.reshape(N_GROUPS * dil, seg, D), qkv.reshape(N_GROUPS * dil, seg, 3 * D)]
        aliases = {3: 0, 4: 1}
    hf, qkv = _pcall(
        body, name=name, grid=(S // FOLD_TOKENS, per_group), in_specs=in_specs,
        out_specs=[pl.BlockSpec((dil, n, D), lambda i, j: (g, i, 0)),
                   pl.BlockSpec((dil, n, QKV_TILE), lambda i, j: (g, i, j))],
        out_shape=[jax.ShapeDtypeStruct((N_GROUPS * dil, seg, D), BF16),
                   jax.ShapeDtypeStruct((N_GROUPS * dil, seg, 3 * D), BF16)],
        scratch_shapes=[pltpu.VMEM((FOLD_TOKENS, D), BF16), pltpu.VMEM((D // LANES, FOLD_TOKENS, LANES), F32)],
        input_output_aliases=aliases,
        compiler_params=_params(("arbitrary", "arbitrary")))(*operands)
    return hf.reshape(N_GROUPS, S, D), qkv.reshape(N_GROUPS, S, 3 * D)


QKV_PIECES = QKV_SHARD // QKV_TILE


def _qkv_dw(name, p, hf, dqkv):
    S = hf.shape[1]
    per_group = 3 * D_MODEL // QKV_TILE
    tile = lambda s: QKV_PIECES * s + p
    return _mm(name, "tn", (N_CHIPS, 1),
               hf, pl.BlockSpec((None, S, D_MODEL), lambda s, k: (tile(s) // per_group, 0, 0)),
               dqkv, pl.BlockSpec((None, S, QKV_TILE), lambda s, k: (tile(s) // per_group, 0, tile(s) % per_group)),
               jax.ShapeDtypeStruct((N_CHIPS, D_MODEL, QKV_TILE), BF16),
               pl.BlockSpec((None, D_MODEL, QKV_TILE), lambda s, k: (s, 0, 0)), None)


def _qkv_dh(name, dqkv, w_qkv, tm=1024):
    S = dqkv.shape[1]
    per_chip = QKV_SHARD // QKV_TILE
    per_group = 3 * D_MODEL // QKV_TILE

    def body(a_ref, w_hbm, o_ref, w_ref, sems):
        g = pl.program_id(0)

        @pl.when(pl.program_id(1) == 0)
        def _():
            copies = []
            for j in range(per_group):
                t = per_group * g + j
                copies.append(pltpu.make_async_copy(
                    w_hbm.at[t // per_chip, :, pl.ds((t % per_chip) * QKV_TILE, QKV_TILE)],
                    w_ref.at[:, pl.ds(j * QKV_TILE, QKV_TILE)], sems.at[j]))
            for cp in copies:
                cp.start()
            for cp in copies:
                cp.wait()

        o_ref[...] = _dot(a_ref[...], w_ref[...], "nt")

    return _pcall(body, name=name, grid=(N_GROUPS, S // tm),
                  in_specs=[pl.BlockSpec((None, tm, 3 * D_MODEL), lambda g, i: (g, i, 0)),
                            pl.BlockSpec(memory_space=pl.ANY)],
                  out_specs=pl.BlockSpec((None, tm, D_MODEL), lambda g, i: (g, i, 0)),
                  out_shape=jax.ShapeDtypeStruct((N_GROUPS, S, D_MODEL), F32),
                  scratch_shapes=[pltpu.VMEM((D_MODEL, 3 * D_MODEL), BF16), pltpu.SemaphoreType.DMA((per_group,))],
                  compiler_params=_params(("arbitrary", "arbitrary")))(dqkv, w_qkv)


def _alibi_coef(g, h):
    vals = [-(2.0 ** (-8.0 * (gg * N_HEADS + h + 1) / (N_GROUPS * N_HEADS))) * ATTN_DILATIONS[gg]
            for gg in range(N_GROUPS)]
    return jnp.where(g == 0, vals[0], jnp.where(g == 1, vals[1], vals[2])).astype(F32)


def _blocks_per_segment(g, S):
    return jnp.right_shift(S // Q_BLOCK, 2 * g)


def _band_bias(pen):
    row = lax.broadcasted_iota(jnp.int32, (Q_BLOCK, 2 * Q_BLOCK), 0)
    col = lax.broadcasted_iota(jnp.int32, (Q_BLOCK, 2 * Q_BLOCK), 1)
    dist = Q_BLOCK + row - col
    valid = jnp.logical_and(dist >= 0, dist <= Q_BLOCK)
    base = jnp.where(valid, jnp.where(col < Q_BLOCK, pen, 0.0), NEG).astype(F32)
    return base, dist.astype(F32)


def _skewed(n_units, stages):
    state = {}
    for step in range(n_units + len(stages) - 1):
        for s, stage in enumerate(stages):
            u = step - s
            if 0 <= u < n_units:
                state[u] = stage(u, state.get(u))
                if s == len(stages) - 1:
                    del state[u]


def _attn_fwd(name, qkv, tq=512):
    S = qkv.shape[1]
    nqb = tq // Q_BLOCK
    scale = HEAD_DIM ** -0.5

    def body(q_ref, k_ref, kp_ref, v_ref, vp_ref, o_ref, lse_ref, kf_ref, vf_ref):
        g = pl.program_id(0)
        i = pl.program_id(1)
        nb = _blocks_per_segment(g, S)
        kf_ref[:Q_BLOCK] = kp_ref[...]
        kf_ref[Q_BLOCK:] = k_ref[...]
        vf_ref[:Q_BLOCK] = vp_ref[...]
        vf_ref[Q_BLOCK:] = v_ref[...]
        coefs = [_alibi_coef(g, h) for h in range(N_HEADS)]
        units = [(b, h) for b in range(nqb) for h in range(N_HEADS)]
        bias = {}

        def scores(u, _):
            b, h = units[u]
            if b not in bias:
                pen = jnp.where((i * nqb + b) % nb != 0, 0.0, NEG).astype(F32)
                bias.clear()
                bias[b] = _band_bias(pen)
            base, dist = bias[b]
            cols = slice(h * HEAD_DIM, (h + 1) * HEAD_DIM)
            q = q_ref[b * Q_BLOCK:(b + 1) * Q_BLOCK, cols]
            kk = kf_ref[b * Q_BLOCK:(b + 2) * Q_BLOCK, cols]
            return _dot(q, kk, "nt") * scale + (coefs[h] * dist + base)

        def softmax(u, s):
            m = jnp.max(s, axis=-1, keepdims=True)
            p = jnp.exp(s - m)
            den = jnp.sum(p, axis=-1, keepdims=True)
            return (p * (1.0 / den)).astype(BF16), m + jnp.log(den)

        def output(u, st):
            b, h = units[u]
            pn, lse = st
            cols = slice(h * HEAD_DIM, (h + 1) * HEAD_DIM)
            rows = slice(b * Q_BLOCK, (b + 1) * Q_BLOCK)
            o_ref[rows, cols] = _dot(pn, vf_ref[b * Q_BLOCK:(b + 2) * Q_BLOCK, cols], "nn")
            lse_ref[rows, h:h + 1] = lse

        _skewed(len(units), [scores, softmax, output])

    main = lambda c: pl.BlockSpec((None, tq, D_MODEL), lambda g, i: (g, i, c))
    prev = lambda c: pl.BlockSpec((None, Q_BLOCK, D_MODEL), lambda g, i: (g, jnp.maximum(i * nqb - 1, 0), c))
    return _pcall(
        body, name=name, grid=(N_GROUPS, S // tq),
        in_specs=[main(0), main(1), prev(1), main(2), prev(2)],
        out_specs=[pl.BlockSpec((None, tq, D_MODEL), lambda g, i: (g, i, 0)),
                   pl.BlockSpec((None, tq, N_HEADS), lambda g, i: (g, i, 0))],
        out_shape=[jax.ShapeDtypeStruct((N_GROUPS, S, D_MODEL), F32),
                   jax.ShapeDtypeStruct((N_GROUPS, S, N_HEADS), F32)],
        scratch_shapes=[pltpu.VMEM((tq + Q_BLOCK, D_MODEL), BF16), pltpu.VMEM((tq + Q_BLOCK, D_MODEL), BF16)],
        compiler_params=_params(("parallel", "parallel")))(qkv, qkv, qkv, qkv, qkv)


def _attn_bwd(name, qkv, do, lse, dl, tq=512):
    S = qkv.shape[1]
    nqb = tq // Q_BLOCK
    n_blocks = S // Q_BLOCK
    scale = HEAD_DIM ** -0.5
    KEYS = 2 * Q_BLOCK

    def body(q_ref, k_ref, v_ref, kp_ref, vp_ref, qn_ref, do_ref, don_ref, lse_ref, lsen_ref, dl_ref, dln_ref,
             out_ref, kf_ref, vf_ref, dk_ref, dv_ref):
        g = pl.program_id(0)
        i = pl.program_id(1)
        nb = _blocks_per_segment(g, S)
        kf_ref[:Q_BLOCK] = kp_ref[...]
        kf_ref[Q_BLOCK:] = k_ref[...]
        vf_ref[:Q_BLOCK] = vp_ref[...]
        vf_ref[Q_BLOCK:] = v_ref[...]
        coefs = [_alibi_coef(g, h) for h in range(N_HEADS)]
        units = [(h, b) for h in range(N_HEADS) for b in range(nqb + 1)]
        bias = {}

        def band(b):
            if b not in bias:
                blk = i * nqb + b
                ok = blk % nb != 0
                if b == nqb:
                    ok = jnp.logical_and(ok, blk < n_blocks)
                bias[b] = _band_bias(jnp.where(ok, 0.0, NEG).astype(F32))
            return bias[b]

        def operands(h, b):
            cols = slice(h * HEAD_DIM, (h + 1) * HEAD_DIM)
            if b == nqb:
                keys = slice(tq, tq + Q_BLOCK)
                return (qn_ref[:, cols], don_ref[:, cols], lsen_ref[:, h:h + 1], dln_ref[:, h:h + 1],
                        kf_ref[keys, cols], vf_ref[keys, cols])
            rows = slice(b * Q_BLOCK, (b + 1) * Q_BLOCK)
            keys = slice(b * Q_BLOCK, b * Q_BLOCK + KEYS)
            return (q_ref[rows, cols], do_ref[rows, cols], lse_ref[rows, h:h + 1], dl_ref[rows, h:h + 1],
                    kf_ref[keys, cols], vf_ref[keys, cols])

        def probs(u, _):
            h, b = units[u]
            q, do_b, lse_b, dl_b, kk, vv = operands(h, b)
            base, dist = band(b)
            if b == nqb:
                base, dist = base[:, :Q_BLOCK], dist[:, :Q_BLOCK]
            p = jnp.exp(_dot(q, kk, "nt") * scale + (coefs[h] * dist + base) - lse_b)
            return p, _dot(do_b, vv, "nt")

        def dscores(u, st):
            h, b = units[u]
            p, dp = st
            dl_b = operands(h, b)[3]
            return ((p * (dp - dl_b)) * scale).astype(BF16), p.astype(BF16)

        def grads(u, st):
            h, b = units[u]
            ds, pb = st
            q, do_b, _, _, kk, _ = operands(h, b)
            cols = slice(h * HEAD_DIM, (h + 1) * HEAD_DIM)
            if b < nqb:
                out_ref[b * Q_BLOCK:(b + 1) * Q_BLOCK, cols] = _dot(ds, kk, "nn").astype(out_ref.dtype)
            if b == 0:
                dk_ref[:Q_BLOCK] = _dot(ds[:, Q_BLOCK:], q, "tn")
                dv_ref[:Q_BLOCK] = _dot(pb[:, Q_BLOCK:], do_b, "tn")
                return None
            dk = _dot(ds, q, "tn")
            dv = _dot(pb, do_b, "tn")
            before = slice((b - 1) * Q_BLOCK, b * Q_BLOCK)
            dk_ref[before] += dk[:Q_BLOCK]
            dv_ref[before] += dv[:Q_BLOCK]
            if b < nqb:
                own = slice(b * Q_BLOCK, (b + 1) * Q_BLOCK)
                dk_ref[own] = dk[Q_BLOCK:]
                dv_ref[own] = dv[Q_BLOCK:]
            else:
                out_ref[:, D_MODEL + h * HEAD_DIM:D_MODEL + (h + 1) * HEAD_DIM] = dk_ref[...].astype(out_ref.dtype)
                out_ref[:, 2 * D_MODEL + h * HEAD_DIM:2 * D_MODEL + (h + 1) * HEAD_DIM] = (
                    dv_ref[...].astype(out_ref.dtype))
            return None

        _skewed(len(units), [probs, dscores, grads])

    nxt_blk = lambda i: jnp.minimum((i + 1) * nqb, n_blocks - 1)
    main = lambda c: pl.BlockSpec((None, tq, D_MODEL), lambda g, i: (g, i, c))
    prev = lambda c: pl.BlockSpec((None, Q_BLOCK, D_MODEL), lambda g, i: (g, jnp.maximum(i * nqb - 1, 0), c))
    row = pl.BlockSpec((None, tq, D_MODEL), lambda g, i: (g, i, 0))
    row_n = pl.BlockSpec((None, Q_BLOCK, D_MODEL), lambda g, i: (g, nxt_blk(i), 0))
    col = pl.BlockSpec((None, tq, N_HEADS), lambda g, i: (g, i, 0))
    col_n = pl.BlockSpec((None, Q_BLOCK, N_HEADS), lambda g, i: (g, nxt_blk(i), 0))
    return _pcall(
        body, name=name, grid=(N_GROUPS, S // tq),
        in_specs=[main(0), main(1), main(2), prev(1), prev(2), row_n, row, row_n, col, col_n, col, col_n],
        out_specs=pl.BlockSpec((None, tq, 3 * D_MODEL), lambda g, i: (g, i, 0)),
        out_shape=jax.ShapeDtypeStruct((N_GROUPS, S, 3 * D_MODEL), BF16),
        scratch_shapes=[pltpu.VMEM((tq + Q_BLOCK, D_MODEL), BF16), pltpu.VMEM((tq + Q_BLOCK, D_MODEL), BF16),
                        pltpu.VMEM((tq, HEAD_DIM), F32), pltpu.VMEM((tq, HEAD_DIM), F32)],
        compiler_params=_params(("parallel", "parallel")))(qkv, qkv, qkv, qkv, qkv, qkv, do, do, lse, lse, dl, dl)


def _group_weights(lse_ref):
    l0, l1, l2 = lse_ref[0], lse_ref[1], lse_ref[2]
    m = jnp.maximum(jnp.maximum(l0, l1), l2)
    e = [jnp.exp(l0 - m), jnp.exp(l1 - m), jnp.exp(l2 - m)]
    den = e[0] + e[1] + e[2]
    return [ei / den for ei in e]


MERGE_TOKENS = 512


def _folded_views(a, tb):
    _, S, C = a.shape
    views, specs = [], []
    for g, dil in enumerate(ATTN_DILATIONS):
        views.append(a.reshape(N_GROUPS * dil, S // dil, C))
        specs.append(pl.BlockSpec((dil, tb // dil, C), lambda i, g=g: (g, i, 0)))
    return views, specs


def _unfold_into(dst_ref, folded_refs):
    for g, (dil, ref) in enumerate(zip(ATTN_DILATIONS, folded_refs)):
        n = ref.shape[1]
        for r in range(dil):
            for c, cols in enumerate(_lane_chunks(ref.shape[2])):
                dst_ref[g, c, _strided_rows(r, n, dil), :] = ref[r, :, cols]


def _merge_fwd(name, o, lse, tb=MERGE_TOKENS):
    S = o.shape[1]

    def body(o0_ref, o1_ref, o2_ref, lse_ref, out_ref, o_ref):
        _unfold_into(o_ref, (o0_ref, o1_ref, o2_ref))
        w = _group_weights(lse_ref)
        for h in range(N_HEADS):
            cols = slice(h * HEAD_DIM, (h + 1) * HEAD_DIM)
            acc = w[0][:, h:h + 1] * o_ref[0, h]
            for gg in range(1, N_GROUPS):
                acc = acc + w[gg][:, h:h + 1] * o_ref[gg, h]
            out_ref[:, cols] = acc.astype(out_ref.dtype)

    views, specs = _folded_views(o, tb)
    return _pcall(body, name=name, grid=(S // tb,),
                  in_specs=specs + [pl.BlockSpec((N_GROUPS, tb, N_HEADS), lambda i: (0, i, 0))],
                  out_specs=pl.BlockSpec((tb, D_MODEL), lambda i: (i, 0)),
                  out_shape=jax.ShapeDtypeStruct((S, D_MODEL), BF16),
                  scratch_shapes=[pltpu.VMEM((N_GROUPS, N_HEADS, tb, HEAD_DIM), F32)],
                  compiler_params=_params(("parallel",)))(*views, lse)


def _merge_bwd(name, d_out, o, lse, tb=MERGE_TOKENS):
    S = o.shape[1]
    n_chunks = sum(ATTN_DILATIONS)

    def body(d_ref, o0_ref, o1_ref, o2_ref, lse_ref, do_hbm, dl_ref, o_ref, dt_ref, df_ref, sems):
        i = pl.program_id(0)
        _unfold_into(o_ref, (o0_ref, o1_ref, o2_ref))
        w = _group_weights(lse_ref)
        for h in range(N_HEADS):
            cols = slice(h * HEAD_DIM, (h + 1) * HEAD_DIM)
            dv = d_ref[:, cols]
            merged = w[0][:, h:h + 1] * o_ref[0, h]
            for gg in range(1, N_GROUPS):
                merged = merged + w[gg][:, h:h + 1] * o_ref[gg, h]
            dsum = jnp.sum(dv * merged, axis=-1, keepdims=True)
            for gg in range(N_GROUPS):
                wg = w[gg][:, h:h + 1]
                dt_ref[gg, h] = wg * dv
                dl_ref[gg, :, h:h + 1] = wg * dsum
        copies = []
        for gg, dil in enumerate(ATTN_DILATIONS):
            n = tb // dil
            for r in range(dil):
                for h, cols in enumerate(_lane_chunks(D_MODEL)):
                    df_ref[gg, r * n:(r + 1) * n, cols] = (
                        dt_ref[gg, h, _strided_rows(r, n, dil), :].astype(df_ref.dtype))
                cp = pltpu.make_async_copy(df_ref.at[gg, pl.ds(r * n, n)],
                                           do_hbm.at[gg, pl.ds(r * (S // dil) + i * n, n)], sems.at[len(copies)])
                cp.start()
                copies.append(cp)
        for cp in copies:
            cp.wait()

    views, specs = _folded_views(o, tb)
    small = pl.BlockSpec((N_GROUPS, tb, N_HEADS), lambda i: (0, i, 0))
    return _pcall(body, name=name, grid=(S // tb,),
                  in_specs=[pl.BlockSpec((tb, D_MODEL), lambda i: (i, 0))] + specs + [small],
                  out_specs=[pl.BlockSpec(memory_space=pl.ANY), small],
                  out_shape=[jax.ShapeDtypeStruct((N_GROUPS, S, D_MODEL), BF16),
                             jax.ShapeDtypeStruct((N_GROUPS, S, N_HEADS), F32)],
                  scratch_shapes=[pltpu.VMEM((N_GROUPS, N_HEADS, tb, HEAD_DIM), F32),
                                  pltpu.VMEM((N_GROUPS, N_HEADS, tb, HEAD_DIM), F32),
                                  pltpu.VMEM((N_GROUPS, tb, D_MODEL), BF16), pltpu.SemaphoreType.DMA((n_chunks,))],
                  compiler_params=_params(("arbitrary",)))(d_out, *views, lse)


def _shift_rows(a, k):
    return pltpu.roll(a, k % a.shape[0], axis=0)


def _pool_level(g, levels):
    return jnp.where(g == 0, levels[0], jnp.where(g == 1, levels[1], jnp.where(g == 2, levels[2], levels[3])))


def _pool_fwd(name, u, w_group, scale, x_in, tr=1024):
    S, D = u.shape
    H = POOL_HALO

    def body(u_ref, halo_ref, w_ref, sc_ref, x_ref, y_ref, z_ref, xo_ref):
        g = pl.program_id(0)
        i = pl.program_id(1)
        uv = u_ref[...]
        halo = jnp.where(i > 0, halo_ref[...], 0.0)
        s = jnp.concatenate([halo, uv], axis=0)
        levels = []
        for k in (1, 2, 4, 8):
            s = s + _shift_rows(s, k)
            levels.append(s[H:])
        t = i * tr + lax.broadcasted_iota(jnp.int32, (tr, 1), 0)
        cnt = jnp.minimum(t + 1, jnp.left_shift(2, g)).astype(F32)
        y = _pool_level(g, levels) / cnt - uv
        yb = y.astype(BF16)
        z = _dot(yb, w_ref[...], "nn")
        y_ref[...] = yb
        z_ref[...] = z
        xo_ref[...] = x_ref[...] + z * sc_ref[...]

    blk = pl.BlockSpec((tr, POOL_DIM), lambda g, i: (i, g))
    return _pcall(
        body, name=name, grid=(D // POOL_DIM, S // tr),
        in_specs=[blk, pl.BlockSpec((H, POOL_DIM), lambda g, i: (jnp.maximum(i * (tr // H) - 1, 0), g)),
                  pl.BlockSpec((None, POOL_DIM, POOL_DIM), lambda g, i: (g, 0, 0)),
                  pl.BlockSpec((1, POOL_DIM), lambda g, i: (0, g)), blk],
        out_specs=[blk, blk, blk],
        out_shape=[jax.ShapeDtypeStruct((S, D), BF16), jax.ShapeDtypeStruct((S, D), F32),
                   jax.ShapeDtypeStruct((S, D), F32)],
        compiler_params=_params(("parallel", "parallel")))(u, u, w_group, scale, x_in)


def _pool_bwd(name, d_out, z, y, scale, w_group, tr=1024):
    S, D = d_out.shape
    H = POOL_HALO

    def body(d_ref, dn_ref, z_ref, y_ref, sc_ref, w_ref, du_ref, dw_ref, dsc_ref):
        g = pl.program_id(0)
        i = pl.program_id(1)
        dv = d_ref[...]
        dz = jnp.concatenate([dv, dn_ref[...]], axis=0) * sc_ref[...]
        dzb = dz.astype(BF16)
        dy = _dot(dzb, w_ref[...], "nt")
        t = i * tr + lax.broadcasted_iota(jnp.int32, (tr + H, 1), 0)
        cnt = jnp.minimum(t + 1, jnp.left_shift(2, g)).astype(F32)
        s = jnp.where(t < S, dy / cnt, 0.0)
        levels = []
        for k in (1, 2, 4, 8):
            s = s + _shift_rows(s, -k)
            levels.append(s[:tr])
        du_ref[...] = (_pool_level(g, levels) - dy[:tr]).astype(du_ref.dtype)
        dw = _dot(y_ref[...], dzb[:tr], "tn")
        dsc = jnp.sum(dv * z_ref[...], axis=0, keepdims=True)

        @pl.when(i == 0)
        def _():
            dw_ref[...] = dw
            dsc_ref[...] = dsc

        @pl.when(i > 0)
        def _():
            dw_ref[...] += dw
            dsc_ref[...] += dsc

    blk = pl.BlockSpec((tr, POOL_DIM), lambda g, i: (i, g))
    vec = pl.BlockSpec((1, POOL_DIM), lambda g, i: (0, g))
    mat = pl.BlockSpec((None, POOL_DIM, POOL_DIM), lambda g, i: (g, 0, 0))
    nxt = pl.BlockSpec((H, POOL_DIM), lambda g, i: (jnp.minimum((i + 1) * (tr // H), S // H - 1), g))
    return _pcall(
        body, name=name, grid=(D // POOL_DIM, S // tr), in_specs=[blk, nxt, blk, blk, vec, mat],
        out_specs=[blk, mat, vec],
        out_shape=[jax.ShapeDtypeStruct((S, D), BF16), jax.ShapeDtypeStruct((D // POOL_DIM, POOL_DIM, POOL_DIM), F32),
                   jax.ShapeDtypeStruct((1, D), F32)],
        compiler_params=_params(("parallel", "arbitrary")))(d_out, d_out, z, y, scale, w_group)


def _row_tile(rows, cols, target_bytes=1 << 20):
    best = rows
    for tr in range(8, rows + 1, 8):
        if rows % tr == 0 and tr * cols * 4 <= target_bytes:
            best = tr
    return best if best * cols * 4 <= 4 * target_bytes else rows


def _adamw(name, w, g, m, v):
    R, C = w.shape
    tr = _row_tile(R, C) if R % 8 == 0 else R

    def body(w_ref, g_ref, m_ref, v_ref, go_ref, d_ref, mo_ref, vo_ref):
        gv = g_ref[...]
        m2 = ADAM_B1 * m_ref[...] + (1.0 - ADAM_B1) * gv
        v2 = ADAM_B2 * v_ref[...] + (1.0 - ADAM_B2) * (gv * gv)
        m_hat = m2 / (1.0 - ADAM_B1 ** ADAM_STEP)
        v_hat = v2 / (1.0 - ADAM_B2 ** ADAM_STEP)
        d_ref[...] = -ADAM_LR * (m_hat / (jnp.sqrt(v_hat) + ADAM_EPS) + ADAM_WD * w_ref[...])
        go_ref[...] = gv
        mo_ref[...] = m2
        vo_ref[...] = v2

    blk = pl.BlockSpec((tr, C), lambda i: (i, 0))
    sds = jax.ShapeDtypeStruct((R, C), F32)
    return _pcall(body, name=name, grid=(R // tr,), in_specs=[blk] * 4, out_specs=[blk] * 4, out_shape=[sds] * 4,
                  compiler_params=_params(("parallel",)))(w, g, m, v)


def _sum_leading(name, a, out_dtype):
    n, R, C = a.shape
    tr = _row_tile(R, C) if R % 16 == 0 else R
    if tr % 16:
        tr = R

    def body(a_ref, o_ref):
        acc = a_ref[0].astype(F32)
        for j in range(1, n):
            acc = acc + a_ref[j].astype(F32)
        o_ref[...] = acc.astype(o_ref.dtype)

    return _pcall(body, name=name, grid=(R // tr,), in_specs=[pl.BlockSpec((n, tr, C), lambda i: (0, i, 0))],
                  out_specs=pl.BlockSpec((tr, C), lambda i: (i, 0)), out_shape=jax.ShapeDtypeStruct((R, C), out_dtype),
                  compiler_params=_params(("parallel",)))(a)


def _cast_into_slot(name, w, layer, chip, dtype):
    _, R, C = w.shape
    tr = _row_tile(R, C, 2 << 20)
    if tr % 16:
        tr = R

    def body(c_ref, w_ref, o_ref):
        o_ref[...] = w_ref[...].astype(o_ref.dtype)

    grid_spec = pltpu.PrefetchScalarGridSpec(
        num_scalar_prefetch=1, grid=(R // tr,),
        in_specs=[pl.BlockSpec((None, tr, C), lambda i, c_ref: (layer, i, 0))],
        out_specs=pl.BlockSpec((None, tr, C), lambda i, c_ref: (c_ref[0], i, 0)))
    return _pcall(body, name=name, grid_spec=grid_spec, out_shape=jax.ShapeDtypeStruct((N_CHIPS, R, C), dtype),
                  compiler_params=_params(("parallel",)))(chip, w)


def _owner_sum(name, mine, landed, chip, core, out=None, layer=None, col=None):
    _, half, C = mine.shape
    k, n_col = col if col is not None else (0, 1)
    tr = _row_tile(half, C)
    if tr % 16:
        tr = half
    nrt = half // tr
    has_out = out is not None

    def body(*refs):
        m_ref, l_ref, o_ref = refs[2], refs[3], refs[-1]
        acc = m_ref[...].astype(F32)
        for j in range(3):
            acc = acc + l_ref[j].astype(F32)
        o_ref[...] = acc

    if layer is None:
        out_shape = jax.ShapeDtypeStruct((2 * half, n_col * C), F32)
        o_spec = pl.BlockSpec((tr, C), lambda i, ch, co: (co[0] * nrt + i, k))
    else:
        out_shape = jax.ShapeDtypeStruct((2, 2 * half, C), F32)
        o_spec = pl.BlockSpec((None, tr, C), lambda i, ch, co: (layer, co[0] * nrt + i, 0))
    in_specs = [pl.BlockSpec((None, tr, C), lambda i, ch, co: (ch[0], i, 0)),
                pl.BlockSpec((3, tr, C), lambda i, ch, co: (0, i, 0))]
    operands = [chip, core, mine, landed]
    aliases = {}
    if has_out:
        in_specs.append(ANY)
        operands.append(out)
        aliases = {4: 0}
    grid_spec = pltpu.PrefetchScalarGridSpec(num_scalar_prefetch=2, grid=(nrt,), in_specs=in_specs, out_specs=o_spec)
    return _pcall(body, name=name, grid_spec=grid_spec, out_shape=out_shape, input_output_aliases=aliases,
                  compiler_params=_params(("parallel",)))(*operands)


def _add_my_half(name, p, q, core):
    n, R, C = p.shape
    half = R // 2

    def body(c_ref, p_ref, q_ref, o_ref):
        o_ref[...] = (p_ref[...].astype(F32) + q_ref[...].astype(F32)).astype(o_ref.dtype)

    grid_spec = pltpu.PrefetchScalarGridSpec(
        num_scalar_prefetch=1, grid=(n,),
        in_specs=[pl.BlockSpec((None, half, C), lambda s, c_ref: (s, c_ref[0], 0)),
                  pl.BlockSpec((None, half, C), lambda s, c_ref: (s, 0, 0))],
        out_specs=pl.BlockSpec((None, half, C), lambda s, c_ref: (s, 0, 0)))
    return _pcall(body, name=name, grid_spec=grid_spec, out_shape=jax.ShapeDtypeStruct((n, half, C), BF16),
                  compiler_params=_params(("parallel",)))(core, p, q)


ANY = pl.BlockSpec(memory_space=pl.ANY)


def _place():
    x, y, c = lax.axis_index("x"), lax.axis_index("y"), lax.axis_index("c")
    other_chips = [(1 - x, y), (x, 1 - y), (1 - x, 1 - y)]
    return x, y, c, other_chips


def _gather_weights(name, bufs, split):
    n = len(bufs)

    def body(*refs):
        outs = refs[n:2 * n]
        ici_send, ici_recv, d2d_send, d2d_recv = refs[2 * n:]
        x, y, c, chips = _place()
        me = 2 * x + y
        sibling = (x, y, 1 - c)

        def piece(t, chip, core_half):
            if not split[t]:
                return outs[t].at[chip]
            half = outs[t].shape[1] // 2
            return outs[t].at[chip, pl.ds(core_half * half, half)]

        sends = []
        for t in range(n):
            for j, (px, py) in enumerate(chips):
                sends.append(pltpu.make_async_remote_copy(
                    src_ref=piece(t, me, c), dst_ref=piece(t, me, c), send_sem=ici_send.at[3 * t + j],
                    recv_sem=ici_recv.at[3 * t + j], device_id=(px, py, c), device_id_type=MESH))
        for cp in sends:
            cp.start()
        passed = []
        for t in range(n):
            for j, (px, py) in enumerate(chips):
                landed = piece(t, 2 * px + py, c)
                pltpu.make_async_remote_copy(
                    src_ref=landed, dst_ref=landed, send_sem=ici_send.at[3 * t + j],
                    recv_sem=ici_recv.at[3 * t + j], device_id=(px, py, c), device_id_type=MESH).wait_recv()
                if split[t]:
                    fwd = pltpu.make_async_remote_copy(
                        src_ref=landed, dst_ref=landed, send_sem=d2d_send.at[3 * t + j],
                        recv_sem=d2d_recv.at[3 * t + j], device_id=sibling, device_id_type=MESH)
                    fwd.start()
                    passed.append(fwd)
        for t in range(n):
            if split[t]:
                for j, (px, py) in enumerate(chips):
                    theirs = piece(t, 2 * px + py, 1 - c)
                    pltpu.make_async_remote_copy(
                        src_ref=theirs, dst_ref=theirs, send_sem=d2d_send.at[3 * t + j],
                        recv_sem=d2d_recv.at[3 * t + j], device_id=sibling, device_id_type=MESH).wait_recv()
        for cp in sends + passed:
            cp.wait_send()

    out_shape = [jax.ShapeDtypeStruct(b.shape, b.dtype) for b in bufs]
    return _pcall(body, name=name, in_specs=[ANY] * n, out_specs=[ANY] * n, out_shape=out_shape,
                  input_output_aliases={t: t for t in range(n)},
                  scratch_shapes=[pltpu.SemaphoreType.DMA((3 * n,)), pltpu.SemaphoreType.DMA((3 * n,)),
                                  pltpu.SemaphoreType.DMA((3 * n,)), pltpu.SemaphoreType.DMA((3 * n,))])(*bufs)


def _remote(src, dst, send, recv, k, to):
    return pltpu.make_async_remote_copy(src_ref=src, dst_ref=dst, send_sem=send.at[k], recv_sem=recv.at[k],
                                        device_id=to, device_id_type=MESH)


def _in_place(bufs):
    return dict(operands=bufs, out_shapes=[jax.ShapeDtypeStruct(b.shape, b.dtype) for b in bufs],
                aliases={t: t for t in range(len(bufs))})


def _gather_over_ici(bufs, split):
    n = len(bufs)

    def copies(ins, outs, send, recv, base=0):
        x, y, c, chips = _place()
        me = 2 * x + y
        out = []
        for t in range(n):
            piece = outs[t].at[me]
            if split[t]:
                half = outs[t].shape[1] // 2
                piece = outs[t].at[me, pl.ds(c * half, half)]
            for j, (px, py) in enumerate(chips):
                out.append(_remote(piece, piece, send, recv, base + 3 * t + j, (px, py, c)))
        return out

    return _Rider(n_copies=3 * n, copies=copies, **_in_place(bufs))


def _gather_to_sibling(bufs):
    n = len(bufs)

    def copies(ins, outs, send, recv, base=0):
        x, y, c, chips = _place()
        out = []
        for t in range(n):
            half = outs[t].shape[1] // 2
            for j, (px, py) in enumerate(chips):
                piece = outs[t].at[2 * px + py, pl.ds(c * half, half)]
                out.append(_remote(piece, piece, send, recv, base + 3 * t + j, (x, y, 1 - c)))
        return out

    return _Rider(n_copies=3 * n, copies=copies, **_in_place(bufs))


def _swap_halves_rider(parts):
    n = len(parts)

    def copies(ins, outs, send, recv, base=0):
        x, y, c, _ = _place()
        out = []
        for t in range(n):
            half = ins[t].shape[1] // 2
            out.append(_remote(ins[t].at[:, pl.ds((1 - c) * half, half)], outs[t], send, recv, base + t,
                               (x, y, 1 - c)))
        return out

    shapes = [jax.ShapeDtypeStruct((p.shape[0], p.shape[1] // 2, p.shape[2]), p.dtype) for p in parts]
    return _Rider(parts, shapes, {}, n, copies)


def _scatter_rider(sums):
    n = len(sums)

    def copies(ins, outs, send, recv, base=0):
        x, y, c, chips = _place()
        out = []
        for t in range(n):
            for j, (px, py) in enumerate(chips):
                out.append(_remote(ins[t].at[2 * px + py], outs[t].at[j], send, recv, base + 3 * t + j, (px, py, c)))
        return out

    shapes = [jax.ShapeDtypeStruct((3,) + s.shape[1:], s.dtype) for s in sums]
    return _Rider(sums, shapes, {}, 3 * n, copies)


def _share_rider(blocks, pieces):
    def copies(ins, outs, send, recv, base=0):
        x, y, c, _ = _place()
        out = []
        for k, (t, l, col) in enumerate(pieces):
            ref = outs[t] if l is None else outs[t].at[l]
            r2 = ref.shape[0] // 2
            piece = ref.at[pl.ds(c * r2, r2)]
            if col is not None:
                width = ref.shape[1] // col[1]
                piece = ref.at[pl.ds(c * r2, r2), pl.ds(col[0] * width, width)]
            out.append(_remote(piece, piece, send, recv, base + k, (x, y, 1 - c)))
        return out

    return _Rider(n_copies=len(pieces), copies=copies, **_in_place(blocks))


def _gather_small(name, v):
    def body(v_ref, out_ref, send_sem, recv_sem, local_sem):
        x, y, c, _ = _place()
        me = 4 * x + 2 * y + c
        flips = [(fx, fy, fc) for fx in (0, 1) for fy in (0, 1) for fc in (0, 1)][1:]
        local = pltpu.make_async_copy(v_ref, out_ref.at[me], local_sem)
        local.start()
        copies = []
        for j, (fx, fy, fc) in enumerate(flips):
            peer = (x ^ fx, y ^ fy, c ^ fc)
            copies.append(pltpu.make_async_remote_copy(
                src_ref=v_ref, dst_ref=out_ref.at[me], send_sem=send_sem.at[j], recv_sem=recv_sem.at[j],
                device_id=peer, device_id_type=MESH))
        for cp in copies:
            cp.start()
        for j, (fx, fy, fc) in enumerate(flips):
            peer = (x ^ fx, y ^ fy, c ^ fc)
            pltpu.make_async_remote_copy(
                src_ref=v_ref, dst_ref=out_ref.at[4 * peer[0] + 2 * peer[1] + peer[2]], send_sem=send_sem.at[j],
                recv_sem=recv_sem.at[j], device_id=peer, device_id_type=MESH).wait_recv()
        for cp in copies:
            cp.wait_send()
        local.wait()

    return _pcall(body, name=name, in_specs=[ANY], out_specs=ANY,
                  out_shape=jax.ShapeDtypeStruct((8,) + v.shape, v.dtype),
                  scratch_shapes=[pltpu.SemaphoreType.DMA((7,)), pltpu.SemaphoreType.DMA((7,)),
                                  pltpu.SemaphoreType.DMA])(v)


def _fold(a, dil):
    if dil == 1:
        return a
    S, C = a.shape
    return a.reshape(S // dil, dil, C).transpose(1, 0, 2).reshape(S, C)


def _unfold(a, dil):
    if dil == 1:
        return a
    S, C = a.shape
    return a.reshape(dil, S // dil, C).transpose(1, 0, 2).reshape(S, C)


def _fold_groups(a):
    return jnp.stack([_fold(a[g] if a.ndim == 3 else a, d) for g, d in enumerate(ATTN_DILATIONS)])


def _unfold_groups(a):
    return jnp.stack([_unfold(a[g], d) for g, d in enumerate(ATTN_DILATIONS)])


class _NoComm:
    def __init__(self, weights):
        self.weights = weights
        self.grads = {}

    def w(self, name):
        return self.weights[name]

    def run(self, fn, name, *args, **kw):
        return fn(name, *args, **kw)

    def grad(self, name, value):
        self.grads[name] = value


def _local_step(xs, tgt, attn_norm, ffn_norm, final_norm, comm):
    run = comm.run
    hf, qkv = None, None
    for g in range(N_GROUPS):
        hf, qkv = run(_norm_fold_qkv, "qkv_proj%d" % g, g, xs, attn_norm, comm.w("qkv"), hf, qkv)
    o_f, lse_f = run(_attn_fwd, "attn_fwd", qkv)
    lse_t = _unfold_groups(lse_f)
    merged = run(_merge_fwd, "merge_fwd", o_f, lse_t)
    x1 = run(_mm_rows, "attn_out", merged, comm.w("o"), F32, 512, res=xs)
    h1 = run(_rms_fwd, "norm_ffn0", x1, ffn_norm[0:1])
    gu0, act0 = run(_ffn_up, "ffn0_up", h1, comm.w("gu0"))
    x2 = run(_mm_rows, "ffn0_down", act0, comm.w("d0"), F32, 512, res=x1)
    h2 = run(_rms_fwd, "norm_pool", x2, comm.w("pool_norm"))
    u = run(_mm_rows, "pool_in", h2, comm.w("p"), F32, 512)
    y, z, x3 = run(_pool_fwd, "pool_fwd", u, comm.w("pg"), comm.w("pool_scale"), x2)
    h3 = run(_rms_fwd, "norm_ffn1", x3, ffn_norm[1:2])
    gu1, act1 = run(_ffn_up, "ffn1_up", h3, comm.w("gu1"))
    x4 = run(_mm_rows, "ffn1_down", act1, comm.w("d1"), F32, 512, res=x3)
    loss, dx4, d_final = run(_loss_bwd, "loss_head", x4, final_norm, tgt)

    dgu1 = run(_ffn_down_bwd, "ffn1_down_bwd", dx4, comm.w("d1"), gu1)
    comm.grad("d1", run(_mm_tn, "ffn1_down_dw", act1, dx4, FF_SHARD, 2048))
    comm.grad("gu1", run(_ffn_dw_up, "ffn1_up_dw", h3, dgu1))
    dh3 = run(_ffn_dh, "ffn1_up_dh", dgu1, comm.w("gu1"))
    dx3, d_ffn1 = run(_rms_bwd, "norm_ffn1_bwd", x3, ffn_norm[1:2], dh3, dx4)

    du, dw_pg, d_scale = run(_pool_bwd, "pool_bwd", dx3, z, y, comm.w("pool_scale"), comm.w("pg"))
    comm.grad("pg", dw_pg)
    comm.grad("p", run(_mm_tn, "pool_in_dw", h2, du, D_MODEL, h2.shape[0]))
    dh2 = run(_mm_nt_rows, "pool_in_dh", du, comm.w("p"), 512)
    dx2, d_pool = run(_rms_bwd, "norm_pool_bwd", x2, comm.w("pool_norm"), dh2, dx3)

    dgu0 = run(_ffn_down_bwd, "ffn0_down_bwd", dx2, comm.w("d0"), gu0)
    comm.grad("d0", run(_mm_tn, "ffn0_down_dw", act0, dx2, FF_SHARD, 2048))
    comm.grad("gu0", run(_ffn_dw_up, "ffn0_up_dw", h1, dgu0))
    dh1 = run(_ffn_dh, "ffn0_up_dh", dgu0, comm.w("gu0"))
    dx1, d_ffn0 = run(_rms_bwd, "norm_ffn0_bwd", x1, ffn_norm[0:1], dh1, dx2)

    d_merged = run(_mm_nt_rows, "attn_out_dh", dx1, comm.w("o"), 512)
    comm.grad("o", run(_mm_tn, "attn_out_dw", merged, dx1, D_MODEL, 2048))
    do_f, dl_t = run(_merge_bwd, "merge_bwd", d_merged, o_f, lse_t)
    dqkv = run(_attn_bwd, "attn_bwd", qkv, do_f, lse_f, _fold_groups(dl_t))
    for p in range(QKV_PIECES):
        comm.grad("qkv%d" % p, run(_qkv_dw, "qkv_dw%d" % p, p, hf, dqkv))
    dhf = run(_qkv_dh, "qkv_dh", dqkv, comm.w("qkv"))
    grad_x, d_attn = run(_rms_bwd_folded, "norm_attn_bwd", xs, attn_norm, dhf, dx1)

    small = dict(attn=d_attn, ffn0=d_ffn0, ffn1=d_ffn1, final=d_final, pool=d_pool, scale=d_scale)
    return loss, grad_x, small


TENSORS = ("qkv", "o", "p", "pg", "gu0", "gu1", "d0", "d1")


def _block_of(name):
    if name[:-1] in ("gu", "d"):
        return name[:-1], int(name[-1]), None
    if name[:-1] == "qkv":
        return "qkv", None, (int(name[-1]), QKV_PIECES)
    return name, None, None


class _MeshComm:
    def __init__(self, bufs, chip_arr, core_arr, pg_rows):
        self.bufs = dict(bufs)
        self.chip_arr, self.core_arr, self.pg_rows = chip_arr, core_arr, pg_rows
        self.parts, self.sums, self.blocks = {}, {}, {}
        ici = lambda *names: lambda: self.gather(names, True)
        d2d = lambda *names: lambda: self.gather(names, False)
        swap = lambda *names: lambda: self.swap(names)
        scatter = lambda *names: lambda: self.scatter(names)
        share = lambda *names: lambda: self.share(names)
        self.plan = {
            "qkv_proj0": [ici("d0")],
            "qkv_proj1": [ici("o", "p", "pg", "pool_norm", "pool_scale")],
            "qkv_proj2": [d2d("d0", "o", "p", "pg")],
            "attn_fwd": [ici("gu0")],
            "merge_fwd": [d2d("gu0")],
            "ffn0_up": [ici("gu1")],
            "ffn0_down": [d2d("gu1"), ici("d1")],
            "pool_in": [d2d("d1")],
            "ffn1_up_dh": [swap("d1", "gu1")],
            "ffn0_down_bwd": [scatter("gu1")],
            "ffn0_down_dw": [scatter("d1")],
            "ffn0_up_dh": [swap("pg", "p", "d0", "gu0"), share("gu1", "d1")],
            "merge_bwd": [swap("o")],
            "attn_bwd": [scatter("pg", "p", "d0", "gu0", "o")],
            "qkv_dw0": [share("pg", "p", "d0", "gu0", "o")],
            "qkv_dw1": [swap("qkv0")],
            "qkv_dw2": [scatter("qkv0"), swap("qkv1")],
            "qkv_dh": [share("qkv0"), scatter("qkv1"), swap("qkv2")],
            "norm_attn_bwd": [share("qkv1"), scatter("qkv2")],
        }

    def gather(self, names, over_ici):
        bufs = [self.bufs[n] for n in names]
        rider = _gather_over_ici(bufs, [n in TENSORS for n in names]) if over_ici else _gather_to_sibling(bufs)
        return rider, lambda res: self.bufs.update(zip(names, res))

    def swap(self, names):
        def done(res):
            for n, q in zip(names, res):
                self.sums[n] = _add_my_half("chip_sum_" + n, self.parts[n], q, self.core_arr)
        return _swap_halves_rider([self.parts[n] for n in names]), done

    def scatter(self, names):
        def done(res):
            for n, landed in zip(names, res):
                key, layer, col = _block_of(n)
                self.blocks[key] = _owner_sum("owner_sum_" + n, self.sums[n], landed, self.chip_arr, self.core_arr,
                                              out=self.blocks.get(key), layer=layer, col=col)
        return _scatter_rider([self.sums[n] for n in names]), done

    def share(self, names):
        keys, pieces = [], []
        for n in names:
            key, layer, col = _block_of(n)
            if key not in keys:
                keys.append(key)
            pieces.append((keys.index(key), layer, col))
        return _share_rider([self.blocks[k] for k in keys], pieces), lambda res: self.blocks.update(zip(keys, res))

    def alone(self, name, made):
        rider, done = made
        done(_comm_only(name, rider))

    def run(self, fn, name, *args, **kw):
        made = [make() for make in self.plan.get(name, [])]
        if not made:
            return fn(name, *args, **kw)
        riders = [r for r, _ in made]
        out = _ride(riders[0] if len(riders) == 1 else _riders(*riders), fn, name, *args, **kw)
        for r, done in made:
            done(r.results)
        return out

    def w(self, name):
        b = self.bufs[name]
        if name in ("o", "p", "d0", "d1"):
            return b.reshape(-1, b.shape[-1])
        if name == "pg":
            b = b.reshape(N_CHIPS, 4, self.pg_rows, POOL_DIM).transpose(1, 0, 2, 3)
            return b.reshape(4, POOL_DIM, POOL_DIM)
        if name in ("pool_norm", "pool_scale"):
            return b.reshape(1, -1)
        return b

    def grad(self, name, value):
        if name == "pg":
            value = value.reshape(4, N_CHIPS, self.pg_rows, POOL_DIM).transpose(1, 0, 2, 3)
            value = value.reshape(N_CHIPS, 4 * self.pg_rows, POOL_DIM).astype(BF16)
        elif name in ("o", "p", "d0", "d1"):
            value = value.reshape(N_CHIPS, value.shape[0] // N_CHIPS, value.shape[1])
        self.parts[name] = value


def kernel(x, attn_norm, w_qkv, w_attn_out, pool_norm, w_pool_in, w_pool_group, pool_scale, ffn_norm, w_ffn_gate_up, w_ffn_down, final_norm, loss_target, m_attn_norm, m_w_qkv, m_w_attn_out, m_pool_norm, m_w_pool_in, m_w_pool_group, m_pool_scale, m_ffn_norm, m_w_ffn_gate_up, m_w_ffn_down, m_final_norm, v_attn_norm, v_w_qkv, v_w_attn_out, v_pool_norm, v_w_pool_in, v_w_pool_group, v_pool_scale, v_ffn_norm, v_w_ffn_gate_up, v_w_ffn_down, v_final_norm):
    D = D_MODEL
    chip = 2 * lax.axis_index("x") + lax.axis_index("y")
    core = lax.axis_index("c")
    pg_rows = w_pool_group.shape[2]

    chip_arr = chip.astype(jnp.int32).reshape(1)
    core_arr = core.astype(jnp.int32).reshape(1)

    shards = [(w_qkv, 0), (w_attn_out, 0), (w_pool_in, 0), (w_pool_group.reshape(1, 4 * pg_rows, POOL_DIM), 0),
              (w_ffn_gate_up, 0), (w_ffn_gate_up, 1), (w_ffn_down, 0), (w_ffn_down, 1)]
    bufs = {nm: _cast_into_slot("cast_" + nm, s, l, chip_arr, BF16) for nm, (s, l) in zip(TENSORS, shards)}
    bufs["pool_norm"] = _cast_into_slot("place_pool_norm", pool_norm[None], 0, chip_arr, F32)
    bufs["pool_scale"] = _cast_into_slot("place_pool_scale", pool_scale[None], 0, chip_arr, F32)
    bufs["qkv"] = _gather_weights("gather_qkv", [bufs["qkv"]], [True])[0]

    comm = _MeshComm(bufs, chip_arr, core_arr, pg_rows)
    loss_part, grad_x, small = _local_step(x[0], loss_target[0], attn_norm, ffn_norm, final_norm.reshape(1, D), comm)
    comm.alone("halves_to_sibling", comm.share(["qkv%d" % (QKV_PIECES - 1)]))
    g_w_qkv, g_w_o, g_w_p, g_w_pg, g_w_gu, g_w_d = [comm.blocks[k] for k in ("qkv", "o", "p", "pg", "gu", "d")]

    rows = jnp.concatenate([small["attn"], small["ffn0"], small["ffn1"], small["final"], small["pool"],
                            small["scale"], jnp.zeros((2, D), F32)], axis=0)
    all_rows = _gather_small("gather_gain_grads", rows)
    tot = _sum_leading("sum_gain_grads", all_rows, F32)
    g_attn_norm = tot[0:1]
    g_ffn_norm = tot[1:3]
    g_final_norm = tot[3]
    g_pool_norm = lax.dynamic_slice(tot, (4, chip * POOL_DIM), (1, POOL_DIM))
    g_pool_scale = lax.dynamic_slice(tot, (5, chip * POOL_DIM), (1, POOL_DIM))

    loss = lax.psum(loss_part[0, 0], ("x", "y", "c"))

    def update(name, w, g, m, v):
        shape = w.shape
        cols = shape[-1]
        as2d = lambda a: a.reshape(-1, cols)
        return [a.reshape(shape) for a in _adamw("adamw_" + name, as2d(w), as2d(g), as2d(m), as2d(v))]

    results = [
        update("attn_norm", attn_norm, g_attn_norm, m_attn_norm, v_attn_norm),
        update("w_qkv", w_qkv, g_w_qkv, m_w_qkv, v_w_qkv),
        update("w_attn_out", w_attn_out, g_w_o, m_w_attn_out, v_w_attn_out),
        update("pool_norm", pool_norm, g_pool_norm, m_pool_norm, v_pool_norm),
        update("w_pool_in", w_pool_in, g_w_p, m_w_pool_in, v_w_pool_in),
        update("w_pool_group", w_pool_group, g_w_pg, m_w_pool_group, v_w_pool_group),
        update("pool_scale", pool_scale, g_pool_scale, m_pool_scale, v_pool_scale),
        update("ffn_norm", ffn_norm, g_ffn_norm, m_ffn_norm, v_ffn_norm),
        update("w_ffn_gate_up", w_ffn_gate_up, g_w_gu, m_w_ffn_gate_up, v_w_ffn_gate_up),
        update("w_ffn_down", w_ffn_down, g_w_d, m_w_ffn_down, v_w_ffn_down),
        update("final_norm", final_norm, g_final_norm, m_final_norm, v_final_norm),
    ]
    grads = [r[0] for r in results]
    deltas = [r[1] for r in results]
    new_m = [r[2] for r in results]
    new_v = [r[3] for r in results]
    return (loss, grad_x[None], *grads, *deltas, *new_m, *new_v)
```

```python
import functools

import jax
import jax.numpy as jnp
from jax import lax
from jax.experimental import pallas as pl
from jax.experimental.pallas import tpu as pltpu

F32 = jnp.float32
BF16 = jnp.bfloat16
MESH = pl.DeviceIdType.MESH

D_MODEL = 1024
N_HEADS = 8
HEAD_DIM = 128
Q_BLOCK = 128
ATTN_DILATIONS = (1, 4, 16)
N_GROUPS = 3
D_FF = 2816
N_CHIPS = 4
FF_SHARD = 2 * D_FF // N_CHIPS
QKV_SHARD = 3 * 3 * D_MODEL // N_CHIPS
QKV_TILE = 768
POOL_WINDOWS = (2, 4, 8, 16)
POOL_DIM = 256
POOL_HALO = 16
RMS_EPS = 1e-6
NEG = -1e30
ADAM_LR = 0.001
ADAM_B1 = 0.9
ADAM_B2 = 0.999
ADAM_EPS = 1e-08
ADAM_WD = 0.01
ADAM_STEP = 10
VMEM_LIMIT = 56 * 1024 * 1024

_DN = {
    "nn": (((1,), (0,)), ((), ())),
    "nt": (((1,), (1,)), ((), ())),
    "tn": (((0,), (0,)), ((), ())),
}


class _Rider:
    def __init__(self, operands, out_shapes, aliases, n_copies, copies):
        self.operands = list(operands)
        self.out_shapes = list(out_shapes)
        self.aliases = dict(aliases)
        self.n_copies = n_copies
        self.copies = copies
        self.results = None


def _riders(*riders):
    operands, out_shapes, aliases, offsets = [], [], {}, []
    n = 0
    for r in riders:
        offsets.append((len(operands), len(out_shapes), n))
        aliases.update({len(operands) + i: len(out_shapes) + o for i, o in r.aliases.items()})
        operands += r.operands
        out_shapes += r.out_shapes
        n += r.n_copies

    def copies(ins, outs, send, recv, base=0):
        out = []
        for r, (i0, o0, s0) in zip(riders, offsets):
            out += r.copies(ins[i0:i0 + len(r.operands)], outs[o0:o0 + len(r.out_shapes)], send, recv, base + s0)
        return out

    both = _Rider(operands, out_shapes, aliases, n, copies)
    both.parts = (riders, offsets)
    return both


_pending_rider = []


def _ride(rider, fn, *args, **kw):
    _pending_rider.append(rider)
    out = fn(*args, **kw)
    assert not _pending_rider
    if hasattr(rider, "parts"):
        for r, (_, o0, _) in zip(*rider.parts):
            r.results = rider.results[o0:o0 + len(r.out_shapes)]
    return out


def _pcall(body, **kw):
    if not _pending_rider:
        return pl.pallas_call(body, **kw)
    rider = _pending_rider.pop()
    grid = kw.get("grid", ())
    in_specs = list(kw.get("in_specs", []))
    single = not isinstance(kw["out_shape"], (list, tuple))
    out_shape = [kw["out_shape"]] if single else list(kw["out_shape"])
    out_specs = [kw["out_specs"]] if single else list(kw["out_specs"])
    scratch = list(kw.get("scratch_shapes", []))
    n_in, n_out, n_scr = len(in_specs), len(out_shape), len(scratch)
    r_in, r_out = len(rider.operands), len(rider.out_shapes)

    def wrapped(*refs):
        ins, rins = refs[:n_in], refs[n_in:n_in + r_in]
        outs = refs[n_in + r_in:n_in + r_in + n_out]
        routs = refs[n_in + r_in + n_out:n_in + r_in + n_out + r_out]
        scr = refs[n_in + r_in + n_out + r_out:n_in + r_in + n_out + r_out + n_scr]
        send, recv = refs[-2:]
        ids = [pl.program_id(a) for a in range(len(grid))]
        first, last = True, True
        for a, pid in enumerate(ids):
            first = jnp.logical_and(first, pid == 0)
            last = jnp.logical_and(last, pid == grid[a] - 1)
        if grid:
            @pl.when(first)
            def _():
                for cp in rider.copies(rins, routs, send, recv):
                    cp.start()
        else:
            for cp in rider.copies(rins, routs, send, recv):
                cp.start()
        body(*ins, *outs, *scr)
        if grid:
            @pl.when(last)
            def _():
                for cp in rider.copies(rins, routs, send, recv):
                    cp.wait()
        else:
            for cp in rider.copies(rins, routs, send, recv):
                cp.wait()

    any_spec = pl.BlockSpec(memory_space=pl.ANY)
    kw2 = dict(kw)
    kw2.update(
        in_specs=in_specs + [any_spec] * r_in, out_specs=out_specs + [any_spec] * r_out,
        out_shape=out_shape + rider.out_shapes,
        scratch_shapes=scratch + [pltpu.SemaphoreType.DMA((rider.n_copies,)),
                                  pltpu.SemaphoreType.DMA((rider.n_copies,))],
        input_output_aliases={**kw.get("input_output_aliases", {}),
                              **{n_in + i: n_out + o for i, o in rider.aliases.items()}})
    if grid:
        kw2["compiler_params"] = _params(("arbitrary",) * len(grid))
    call = pl.pallas_call(wrapped, **kw2)

    def run(*operands):
        res = call(*operands, *rider.operands)
        rider.results = list(res[n_out:])
        return res[0] if single else list(res[:n_out])

    return run


def _comm_only(name, rider):
    def make():
        return _pcall(lambda: None, name=name, in_specs=[], out_specs=[], out_shape=[])()
    _ride(rider, make)
    return rider.results


def _params(sem):
    return pltpu.CompilerParams(dimension_semantics=sem, vmem_limit_bytes=VMEM_LIMIT)


def _dot(a, b, mode):
    return lax.dot_general(a, b, _DN[mode], preferred_element_type=F32)


def _mm(name, mode, grid, a, a_spec, b, b_spec, out_shape, o_spec, acc_shape, res=None, res_spec=None):
    nk = grid[-1]
    has_res = res is not None

    def body(*refs):
        a_ref, b_ref = refs[0], refs[1]
        res_ref = refs[2] if has_res else None
        o_ref = refs[2 + has_res]
        part = _dot(a_ref[...].astype(BF16), b_ref[...].astype(BF16), mode)
        if nk == 1:
            if has_res:
                part = part + res_ref[...]
            o_ref[...] = part.astype(o_ref.dtype)
            return
        acc_ref = refs[3 + has_res]
        k = pl.program_id(len(grid) - 1)

        @pl.when(k == 0)
        def _():
            acc_ref[...] = part

        @pl.when(k > 0)
        def _():
            acc_ref[...] += part

        @pl.when(k == nk - 1)
        def _():
            r = acc_ref[...]
            if has_res:
                r = r + res_ref[...]
            o_ref[...] = r.astype(o_ref.dtype)

    in_specs = [a_spec, b_spec] + ([res_spec] if has_res else [])
    operands = [a, b] + ([res] if has_res else [])
    scratch = [] if nk == 1 else [pltpu.VMEM(acc_shape, F32)]
    sem = ("parallel",) * (len(grid) - 1) + ("arbitrary",)
    return _pcall(body, name=name, grid=grid, in_specs=in_specs, out_specs=o_spec, out_shape=out_shape,
                  scratch_shapes=scratch, compiler_params=_params(sem))(*operands)


def _mm_rows(name, a, b, out_dtype, tm, res=None):
    M, K = a.shape
    N = b.shape[1]
    return _mm(name, "nn", (M // tm, 1), a, pl.BlockSpec((tm, K), lambda i, k: (i, 0)),
               b, pl.BlockSpec((K, N), lambda i, k: (0, 0)),
               jax.ShapeDtypeStruct((M, N), out_dtype), pl.BlockSpec((tm, N), lambda i, k: (i, 0)), None,
               res=res, res_spec=pl.BlockSpec((tm, N), lambda i, k: (i, 0)))


def _mm_nt_rows(name, a, b, tm):
    M, N = a.shape
    K = b.shape[0]
    return _mm(name, "nt", (M // tm, 1), a, pl.BlockSpec((tm, N), lambda i, k: (i, 0)),
               b, pl.BlockSpec((K, N), lambda i, k: (0, 0)),
               jax.ShapeDtypeStruct((M, K), F32), pl.BlockSpec((tm, K), lambda i, k: (i, 0)), None)


def _mm_tn(name, a, b, tm, tk):
    T, M = a.shape
    N = b.shape[1]
    return _mm(name, "tn", (M // tm, T // tk), a, pl.BlockSpec((tk, tm), lambda i, k: (k, i)),
               b, pl.BlockSpec((tk, N), lambda i, k: (k, 0)),
               jax.ShapeDtypeStruct((M, N), BF16), pl.BlockSpec((tm, N), lambda i, k: (i, 0)), (tm, N))


def _rms_fwd(name, x, g, tr=512):
    S, D = x.shape

    def body(x_ref, g_ref, o_ref):
        xf = x_ref[...]
        r = lax.rsqrt(jnp.mean(xf * xf, axis=-1, keepdims=True) + RMS_EPS)
        o_ref[...] = ((xf * r) * g_ref[...]).astype(o_ref.dtype)

    return _pcall(body, name=name, grid=(S // tr,),
                  in_specs=[pl.BlockSpec((tr, D), lambda i: (i, 0)), pl.BlockSpec((1, D), lambda i: (0, 0))],
                  out_specs=pl.BlockSpec((tr, D), lambda i: (i, 0)),
                  out_shape=jax.ShapeDtypeStruct((S, D), BF16), compiler_params=_params(("parallel",)))(x, g)


def _norm_bwd_rows(xf, gain, dh, dres):
    r = lax.rsqrt(jnp.mean(xf * xf, axis=-1, keepdims=True) + RMS_EPS)
    xh = xf * r
    t = dh * gain
    dx = dres + r * (t - xh * jnp.mean(t * xh, axis=-1, keepdims=True))
    return dx, jnp.sum(dh * xh, axis=0, keepdims=True)


def _accumulate(ref, value, first):
    @pl.when(first)
    def _():
        ref[...] = value

    @pl.when(jnp.logical_not(first))
    def _():
        ref[...] += value


def _rms_bwd_folded(name, x, g, dhf, dres, tb=512):
    S, D = x.shape

    def body(x_ref, g_ref, d0_ref, d1_ref, d2_ref, dres_ref, dx_ref, dg_ref, dh_ref):
        chunks = _lane_chunks(D)
        for c, cols in enumerate(chunks):
            dh_ref[c] = d0_ref[0, :, cols]
        for dil, ref in ((ATTN_DILATIONS[1], d1_ref), (ATTN_DILATIONS[2], d2_ref)):
            n = tb // dil
            for k in range(dil):
                for c, cols in enumerate(chunks):
                    dh_ref[c, _strided_rows(k, n, dil), :] += ref[k, :, cols]
        dh = jnp.concatenate([dh_ref[c] for c in range(len(chunks))], axis=1)
        dx, dg = _norm_bwd_rows(x_ref[...], g_ref[...], dh, dres_ref[...])
        dx_ref[...] = dx
        _accumulate(dg_ref, dg, pl.program_id(0) == 0)

    views, specs = _folded_views(dhf, tb)
    row = pl.BlockSpec((tb, D), lambda i: (i, 0))
    vec = pl.BlockSpec((1, D), lambda i: (0, 0))
    return _pcall(body, name=name, grid=(S // tb,), in_specs=[row, vec] + specs + [row], out_specs=[row, vec],
                  out_shape=[jax.ShapeDtypeStruct((S, D), F32), jax.ShapeDtypeStruct((1, D), F32)],
                  scratch_shapes=[pltpu.VMEM((D // LANES, tb, LANES), F32)],
                  compiler_params=_params(("arbitrary",)))(x, g, *views, dres)


def _proj_res_norm(name, a, w, res, gain, tm=512):
    M, K = a.shape
    D = w.shape[1]

    def body(a_ref, w_ref, res_ref, g_ref, x_ref, h_ref):
        xf = res_ref[...] + _dot(a_ref[...], w_ref[...], "nn")
        x_ref[...] = xf
        r = lax.rsqrt(jnp.mean(xf * xf, axis=-1, keepdims=True) + RMS_EPS)
        h_ref[...] = ((xf * r) * g_ref[...]).astype(h_ref.dtype)

    row = pl.BlockSpec((tm, D), lambda i: (i, 0))
    return _pcall(body, name=name, grid=(M // tm,),
                  in_specs=[pl.BlockSpec((tm, K), lambda i: (i, 0)), pl.BlockSpec((K, D), lambda i: (0, 0)), row,
                            pl.BlockSpec((1, D), lambda i: (0, 0))],
                  out_specs=[row, row],
                  out_shape=[jax.ShapeDtypeStruct((M, D), F32), jax.ShapeDtypeStruct((M, D), BF16)],
                  compiler_params=_params(("parallel",)))(a, w, res, gain)


def _proj_res_loss(name, a, w, res, gain, tgt, tm=512):
    M, K = a.shape
    D = w.shape[1]

    def body(a_ref, w_ref, res_ref, g_ref, t_ref, loss_ref, dx_ref, dg_ref):
        first = pl.program_id(0) == 0
        xf = res_ref[...] + _dot(a_ref[...], w_ref[...], "nn")
        r = lax.rsqrt(jnp.mean(xf * xf, axis=-1, keepdims=True) + RMS_EPS)
        xh = xf * r
        gv = g_ref[...]
        e = xh * gv - t_ref[...]
        lp = 0.5 * jnp.sum(jnp.mean(e * e, axis=-1, keepdims=True), axis=0, keepdims=True)
        dy = e * (1.0 / D)
        t = dy * gv
        dx_ref[...] = r * (t - xh * jnp.mean(t * xh, axis=-1, keepdims=True))
        _accumulate(dg_ref, jnp.sum(dy * xh, axis=0, keepdims=True), first)
        _accumulate(loss_ref, lp, first)

    row = pl.BlockSpec((tm, D), lambda i: (i, 0))
    vec = pl.BlockSpec((1, D), lambda i: (0, 0))
    return _pcall(body, name=name, grid=(M // tm,),
                  in_specs=[pl.BlockSpec((tm, K), lambda i: (i, 0)), pl.BlockSpec((K, D), lambda i: (0, 0)), row,
                            vec, row],
                  out_specs=[pl.BlockSpec((1, 1), lambda i: (0, 0)), row, vec],
                  out_shape=[jax.ShapeDtypeStruct((1, 1), F32), jax.ShapeDtypeStruct((M, D), F32),
                             jax.ShapeDtypeStruct((1, D), F32)],
                  compiler_params=_params(("arbitrary",)))(a, w, res, gain, tgt)


def _dh_norm_bwd(name, d, w, x, gain, dres, tm=512):
    M, N = d.shape
    D = w.shape[0]

    def body(d_ref, w_ref, x_ref, g_ref, dres_ref, dx_ref, dg_ref):
        dh = _dot(d_ref[...].astype(BF16), w_ref[...], "nt")
        dx, dg = _norm_bwd_rows(x_ref[...], g_ref[...], dh, dres_ref[...])
        dx_ref[...] = dx
        _accumulate(dg_ref, dg, pl.program_id(0) == 0)

    row = pl.BlockSpec((tm, D), lambda i: (i, 0))
    vec = pl.BlockSpec((1, D), lambda i: (0, 0))
    return _pcall(body, name=name, grid=(M // tm,),
                  in_specs=[pl.BlockSpec((tm, N), lambda i: (i, 0)), pl.BlockSpec((D, N), lambda i: (0, 0)), row, vec,
                            row],
                  out_specs=[row, vec],
                  out_shape=[jax.ShapeDtypeStruct((M, D), F32), jax.ShapeDtypeStruct((1, D), F32)],
                  compiler_params=_params(("arbitrary",)))(d, w, x, gain, dres)


def _sigmoid(v):
    return 0.5 * jnp.tanh(0.5 * v) + 0.5


def _ffn_up(name, h, w_gu, tm=512):
    S, D = h.shape
    W = FF_SHARD

    def body(h_ref, wg_ref, wu_ref, gu_ref, act_ref):
        hv = h_ref[...]
        gate = _dot(hv, wg_ref[...], "nn")
        up = _dot(hv, wu_ref[...], "nn")
        gu_ref[0] = gate.astype(gu_ref.dtype)
        gu_ref[1] = up.astype(gu_ref.dtype)
        sig = _sigmoid(gate)
        act_ref[...] = ((gate * sig) * up).astype(act_ref.dtype)

    return _pcall(
        body, name=name, grid=(2, S // tm),
        in_specs=[pl.BlockSpec((tm, D), lambda j, i: (i, 0)),
                  pl.BlockSpec((None, D, W), lambda j, i: (j, 0, 0)),
                  pl.BlockSpec((None, D, W), lambda j, i: (j + 2, 0, 0))],
        out_specs=[pl.BlockSpec((2, tm, W), lambda j, i: (0, i, j)), pl.BlockSpec((tm, W), lambda j, i: (i, j))],
        out_shape=[jax.ShapeDtypeStruct((2, S, D_FF), BF16), jax.ShapeDtypeStruct((S, D_FF), BF16)],
        compiler_params=_params(("parallel", "parallel")))(h, w_gu, w_gu)


def _ffn_down_bwd(name, dx, w_down, gu, tm=512):
    S, D = dx.shape
    W = FF_SHARD

    def body(dx_ref, w_ref, gu_ref, dgu_ref):
        dact = _dot(dx_ref[...].astype(BF16), w_ref[...], "nt")
        gate = gu_ref[0].astype(F32)
        up = gu_ref[1].astype(F32)
        sig = _sigmoid(gate)
        silu = gate * sig
        dgu_ref[0] = (dact * up * (sig * (1.0 + gate * (1.0 - sig)))).astype(dgu_ref.dtype)
        dgu_ref[1] = (dact * silu).astype(dgu_ref.dtype)

    blk = pl.BlockSpec((2, tm, W), lambda j, i: (0, i, j))
    return _pcall(
        body, name=name, grid=(2, S // tm),
        in_specs=[pl.BlockSpec((tm, D), lambda j, i: (i, 0)), pl.BlockSpec((W, D), lambda j, i: (j, 0)), blk],
        out_specs=blk, out_shape=jax.ShapeDtypeStruct((2, S, D_FF), BF16),
        compiler_params=_params(("parallel", "parallel")))(dx, w_down, gu)


def _ffn_dw_up(name, h, dgu, tk=2048):
    S, D = h.shape
    W = FF_SHARD
    tk = min(tk, S)
    return _mm(name, "tn", (N_CHIPS, S // tk), h, pl.BlockSpec((tk, D), lambda s, k: (k, 0)),
               dgu, pl.BlockSpec((None, tk, W), lambda s, k: (s // 2, k, s % 2)),
               jax.ShapeDtypeStruct((N_CHIPS, D, W), BF16), pl.BlockSpec((None, D, W), lambda s, k: (s, 0, 0)),
               (D, W))


def _ffn_dh(name, dgu, w_gu, x, gain, dres, tm=512):
    S = dgu.shape[1]
    W = FF_SHARD

    def body(a_ref, w_hbm, x_ref, g_ref, dres_ref, dx_ref, dg_ref, w_ref, acat_ref, sems):
        first = pl.program_id(0) == 0

        @pl.when(first)
        def _():
            copies = [pltpu.make_async_copy(w_hbm.at[s], w_ref.at[:, pl.ds(s * W, W)], sems.at[s])
                      for s in range(N_CHIPS)]
            for cp in copies:
                cp.start()
            for cp in copies:
                cp.wait()

        acat_ref[:, :D_FF] = a_ref[0]
        acat_ref[:, D_FF:] = a_ref[1]
        dh = _dot(acat_ref[...], w_ref[...], "nt")
        dx, dg = _norm_bwd_rows(x_ref[...], g_ref[...], dh, dres_ref[...])
        dx_ref[...] = dx
        _accumulate(dg_ref, dg, first)

    row = pl.BlockSpec((tm, D_MODEL), lambda i: (i, 0))
    vec = pl.BlockSpec((1, D_MODEL), lambda i: (0, 0))
    return _pcall(body, name=name, grid=(S // tm,),
                  in_specs=[pl.BlockSpec((2, tm, D_FF), lambda i: (0, i, 0)), pl.BlockSpec(memory_space=pl.ANY),
                            row, vec, row],
                  out_specs=[row, vec],
                  out_shape=[jax.ShapeDtypeStruct((S, D_MODEL), F32), jax.ShapeDtypeStruct((1, D_MODEL), F32)],
                  scratch_shapes=[pltpu.VMEM((D_MODEL, 2 * D_FF), BF16), pltpu.VMEM((tm, 2 * D_FF), BF16),
                                  pltpu.SemaphoreType.DMA((N_CHIPS,))],
                  compiler_params=_params(("arbitrary",)))(dgu, w_gu, x, gain, dres)


FOLD_TOKENS = 1024
LANES = 128


def _lane_chunks(width):
    return [slice(c * LANES, (c + 1) * LANES) for c in range(width // LANES)]


def _strided_rows(r, n, dil):
    return pl.ds(r, n) if dil == 1 else pl.ds(r, n, stride=dil)


def _norm_fold_qkv(name, g, x, gain, w_qkv, hf, qkv):
    S, D = x.shape
    dil = ATTN_DILATIONS[g]
    n = FOLD_TOKENS // dil
    seg = S // dil
    per_chip = QKV_SHARD // QKV_TILE
    per_group = 3 * D_MODEL // QKV_TILE

    def body(*refs):
        x_ref, g_ref, w_ref = refs[:3]
        hf_ref, qkv_ref, a_ref, xc_ref = refs[-4:]
        j = pl.program_id(1)

        @pl.when(j == 0)
        def _():
            xf = x_ref[...]
            inv = lax.rsqrt(jnp.mean(xf * xf, axis=-1, keepdims=True) + RMS_EPS)
            h = (xf * inv) * g_ref[...]
            if dil == 1:
                a_ref[...] = h.astype(BF16)
            else:
                for c, cols in enumerate(_lane_chunks(D)):
                    xc_ref[c] = h[:, cols]
                for r in range(dil):
                    for c, cols in enumerate(_lane_chunks(D)):
                        a_ref[r * n:(r + 1) * n, cols] = xc_ref[c, _strided_rows(r, n, dil), :].astype(BF16)
            for r in range(dil):
                hf_ref[r] = a_ref[r * n:(r + 1) * n, :]

        res = _dot(a_ref[...], w_ref[...], "nn")
        for r in range(dil):
            qkv_ref[r] = res[r * n:(r + 1) * n].astype(qkv_ref.dtype)

    in_specs = [pl.BlockSpec((FOLD_TOKENS, D), lambda i, j: (i, 0)), pl.BlockSpec((1, D), lambda i, j: (0, 0)),
                pl.BlockSpec((None, D, QKV_TILE),
                             lambda i, j: ((per_group * g + j) // per_chip, 0, (per_group * g + j) % per_chip))]
    operands = [x, gain, w_qkv]
    aliases = {}
    if hf is not None:
        in_specs += [pl.BlockSpec(memory_space=pl.ANY)] * 2
        operands += [hf.reshape(N_GROUPS * dil, seg, D), qkv.reshape(N_GROUPS * dil, seg, 3 * D)]
        aliases = {3: 0, 4: 1}
    hf, qkv = _pcall(
        body, name=name, grid=(S // FOLD_TOKENS, per_group), in_specs=in_specs,
        out_specs=[pl.BlockSpec((dil, n, D), lambda i, j: (g, i, 0)),
                   pl.BlockSpec((dil, n, QKV_TILE), lambda i, j: (g, i, j))],
        out_shape=[jax.ShapeDtypeStruct((N_GROUPS * dil, seg, D), BF16),
                   jax.ShapeDtypeStruct((N_GROUPS * dil, seg, 3 * D), BF16)],
        scratch_shapes=[pltpu.VMEM((FOLD_TOKENS, D), BF16), pltpu.VMEM((D // LANES, FOLD_TOKENS, LANES), F32)],
        input_output_aliases=aliases,
        compiler_params=_params(("arbitrary", "arbitrary")))(*operands)
    return hf.reshape(N_GROUPS, S, D), qkv.reshape(N_GROUPS, S, 3 * D)


QKV_PIECES = QKV_SHARD // QKV_TILE


def _qkv_dw(name, p, hf, dqkv):
    S = hf.shape[1]
    per_group = 3 * D_MODEL // QKV_TILE
    tile = lambda s: QKV_PIECES * s + p
    return _mm(name, "tn", (N_CHIPS, 1),
               hf, pl.BlockSpec((None, S, D_MODEL), lambda s, k: (tile(s) // per_group, 0, 0)),
               dqkv, pl.BlockSpec((None, S, QKV_TILE), lambda s, k: (tile(s) // per_group, 0, tile(s) % per_group)),
               jax.ShapeDtypeStruct((N_CHIPS, D_MODEL, QKV_TILE), BF16),
               pl.BlockSpec((None, D_MODEL, QKV_TILE), lambda s, k: (s, 0, 0)), None)


def _qkv_dh(name, dqkv, w_qkv, tm=1024):
    S = dqkv.shape[1]
    per_chip = QKV_SHARD // QKV_TILE
    per_group = 3 * D_MODEL // QKV_TILE

    def body(a_ref, w_hbm, o_ref, w_ref, sems):
        g = pl.program_id(0)

        @pl.when(pl.program_id(1) == 0)
        def _():
            copies = []
            for j in range(per_group):
                t = per_group * g + j
                copies.append(pltpu.make_async_copy(
                    w_hbm.at[t // per_chip, :, pl.ds((t % per_chip) * QKV_TILE, QKV_TILE)],
                    w_ref.at[:, pl.ds(j * QKV_TILE, QKV_TILE)], sems.at[j]))
            for cp in copies:
                cp.start()
            for cp in copies:
                cp.wait()

        o_ref[...] = _dot(a_ref[...], w_ref[...], "nt")

    return _pcall(body, name=name, grid=(N_GROUPS, S // tm),
                  in_specs=[pl.BlockSpec((None, tm, 3 * D_MODEL), lambda g, i: (g, i, 0)),
                            pl.BlockSpec(memory_space=pl.ANY)],
                  out_specs=pl.BlockSpec((None, tm, D_MODEL), lambda g, i: (g, i, 0)),
                  out_shape=jax.ShapeDtypeStruct((N_GROUPS, S, D_MODEL), F32),
                  scratch_shapes=[pltpu.VMEM((D_MODEL, 3 * D_MODEL), BF16), pltpu.SemaphoreType.DMA((per_group,))],
                  compiler_params=_params(("arbitrary", "arbitrary")))(dqkv, w_qkv)


def _alibi_coef(g, h):
    vals = [-(2.0 ** (-8.0 * (gg * N_HEADS + h + 1) / (N_GROUPS * N_HEADS))) * ATTN_DILATIONS[gg]
            for gg in range(N_GROUPS)]
    return jnp.where(g == 0, vals[0], jnp.where(g == 1, vals[1], vals[2])).astype(F32)


def _blocks_per_segment(g, S):
    return jnp.right_shift(S // Q_BLOCK, 2 * g)


def _band_bias(pen):
    row = lax.broadcasted_iota(jnp.int32, (Q_BLOCK, 2 * Q_BLOCK), 0)
    col = lax.broadcasted_iota(jnp.int32, (Q_BLOCK, 2 * Q_BLOCK), 1)
    dist = Q_BLOCK + row - col
    valid = jnp.logical_and(dist >= 0, dist <= Q_BLOCK)
    base = jnp.where(valid, jnp.where(col < Q_BLOCK, pen, 0.0), NEG).astype(F32)
    return base, dist.astype(F32)


def _skewed(n_units, stages):
    state = {}
    for step in range(n_units + len(stages) - 1):
        for s, stage in enumerate(stages):
            u = step - s
            if 0 <= u < n_units:
                state[u] = stage(u, state.get(u))
                if s == len(stages) - 1:
                    del state[u]


def _attn_fwd(name, qkv, tq=512):
    S = qkv.shape[1]
    nqb = tq // Q_BLOCK
    scale = HEAD_DIM ** -0.5

    def body(q_ref, k_ref, kp_ref, v_ref, vp_ref, o_ref, lse_ref, kf_ref, vf_ref):
        g = pl.program_id(0)
        i = pl.program_id(1)
        nb = _blocks_per_segment(g, S)
        kf_ref[:Q_BLOCK] = kp_ref[...]
        kf_ref[Q_BLOCK:] = k_ref[...]
        vf_ref[:Q_BLOCK] = vp_ref[...]
        vf_ref[Q_BLOCK:] = v_ref[...]
        coefs = [_alibi_coef(g, h) for h in range(N_HEADS)]
        units = [(b, h) for b in range(nqb) for h in range(N_HEADS)]
        bias = {}

        def scores(u, _):
            b, h = units[u]
            if b not in bias:
                pen = jnp.where((i * nqb + b) % nb != 0, 0.0, NEG).astype(F32)
                bias.clear()
                bias[b] = _band_bias(pen)
            base, dist = bias[b]
            cols = slice(h * HEAD_DIM, (h + 1) * HEAD_DIM)
            q = q_ref[b * Q_BLOCK:(b + 1) * Q_BLOCK, cols]
            kk = kf_ref[b * Q_BLOCK:(b + 2) * Q_BLOCK, cols]
            return _dot(q, kk, "nt") * scale + (coefs[h] * dist + base)

        def softmax(u, s):
            m = jnp.max(s, axis=-1, keepdims=True)
            p = jnp.exp(s - m)
            den = jnp.sum(p, axis=-1, keepdims=True)
            return (p * (1.0 / den)).astype(BF16), m + jnp.log(den)

        def output(u, st):
            b, h = units[u]
            pn, lse = st
            cols = slice(h * HEAD_DIM, (h + 1) * HEAD_DIM)
            rows = slice(b * Q_BLOCK, (b + 1) * Q_BLOCK)
            o_ref[rows, cols] = _dot(pn, vf_ref[b * Q_BLOCK:(b + 2) * Q_BLOCK, cols], "nn")
            lse_ref[rows, h:h + 1] = lse

        _skewed(len(units), [scores, softmax, output])

    main = lambda c: pl.BlockSpec((None, tq, D_MODEL), lambda g, i: (g, i, c))
    prev = lambda c: pl.BlockSpec((None, Q_BLOCK, D_MODEL), lambda g, i: (g, jnp.maximum(i * nqb - 1, 0), c))
    return _pcall(
        body, name=name, grid=(N_GROUPS, S // tq),
        in_specs=[main(0), main(1), prev(1), main(2), prev(2)],
        out_specs=[pl.BlockSpec((None, tq, D_MODEL), lambda g, i: (g, i, 0)),
                   pl.BlockSpec((None, tq, N_HEADS), lambda g, i: (g, i, 0))],
        out_shape=[jax.ShapeDtypeStruct((N_GROUPS, S, D_MODEL), F32),
                   jax.ShapeDtypeStruct((N_GROUPS, S, N_HEADS), F32)],
        scratch_shapes=[pltpu.VMEM((tq + Q_BLOCK, D_MODEL), BF16), pltpu.VMEM((tq + Q_BLOCK, D_MODEL), BF16)],
        compiler_params=_params(("parallel", "parallel")))(qkv, qkv, qkv, qkv, qkv)


def _attn_bwd(name, qkv, do, lse, dl, tq=512):
    S = qkv.shape[1]
    nqb = tq // Q_BLOCK
    n_blocks = S // Q_BLOCK
    scale = HEAD_DIM ** -0.5
    KEYS = 2 * Q_BLOCK

    def body(q_ref, k_ref, v_ref, kp_ref, vp_ref, qn_ref, do_ref, don_ref, lse_ref, lsen_ref, dl_ref, dln_ref,
             out_ref, kf_ref, vf_ref, dk_ref, dv_ref):
        g = pl.program_id(0)
        i = pl.program_id(1)
        nb = _blocks_per_segment(g, S)
        kf_ref[:Q_BLOCK] = kp_ref[...]
        kf_ref[Q_BLOCK:] = k_ref[...]
        vf_ref[:Q_BLOCK] = vp_ref[...]
        vf_ref[Q_BLOCK:] = v_ref[...]
        coefs = [_alibi_coef(g, h) for h in range(N_HEADS)]
        units = [(h, b) for h in range(N_HEADS) for b in range(nqb + 1)]
        bias = {}

        def band(b):
            if b not in bias:
                blk = i * nqb + b
                ok = blk % nb != 0
                if b == nqb:
                    ok = jnp.logical_and(ok, blk < n_blocks)
                bias[b] = _band_bias(jnp.where(ok, 0.0, NEG).astype(F32))
            return bias[b]

        def operands(h, b):
            cols = slice(h * HEAD_DIM, (h + 1) * HEAD_DIM)
            if b == nqb:
                keys = slice(tq, tq + Q_BLOCK)
                return (qn_ref[:, cols], don_ref[:, cols], lsen_ref[:, h:h + 1], dln_ref[:, h:h + 1],
                        kf_ref[keys, cols], vf_ref[keys, cols])
            rows = slice(b * Q_BLOCK, (b + 1) * Q_BLOCK)
            keys = slice(b * Q_BLOCK, b * Q_BLOCK + KEYS)
            return (q_ref[rows, cols], do_ref[rows, cols], lse_ref[rows, h:h + 1], dl_ref[rows, h:h + 1],
                    kf_ref[keys, cols], vf_ref[keys, cols])

        def probs(u, _):
            h, b = units[u]
            q, do_b, lse_b, dl_b, kk, vv = operands(h, b)
            base, dist = band(b)
            if b == nqb:
                base, dist = base[:, :Q_BLOCK], dist[:, :Q_BLOCK]
            p = jnp.exp(_dot(q, kk, "nt") * scale + (coefs[h] * dist + base) - lse_b)
            return p, _dot(do_b, vv, "nt")

        def dscores(u, st):
            h, b = units[u]
            p, dp = st
            dl_b = operands(h, b)[3]
            return ((p * (dp - dl_b)) * scale).astype(BF16), p.astype(BF16)

        def grads(u, st):
            h, b = units[u]
            ds, pb = st
            q, do_b, _, _, kk, _ = operands(h, b)
            cols = slice(h * HEAD_DIM, (h + 1) * HEAD_DIM)
            if b < nqb:
                out_ref[b * Q_BLOCK:(b + 1) * Q_BLOCK, cols] = _dot(ds, kk, "nn").astype(out_ref.dtype)
            if b == 0:
                dk_ref[:Q_BLOCK] = _dot(ds[:, Q_BLOCK:], q, "tn")
                dv_ref[:Q_BLOCK] = _dot(pb[:, Q_BLOCK:], do_b, "tn")
                return None
            dk = _dot(ds, q, "tn")
            dv = _dot(pb, do_b, "tn")
            before = slice((b - 1) * Q_BLOCK, b * Q_BLOCK)
            dk_ref[before] += dk[:Q_BLOCK]
            dv_ref[before] += dv[:Q_BLOCK]
            if b < nqb:
                own = slice(b * Q_BLOCK, (b + 1) * Q_BLOCK)
                dk_ref[own] = dk[Q_BLOCK:]
                dv_ref[own] = dv[Q_BLOCK:]
            else:
                out_ref[:, D_MODEL + h * HEAD_DIM:D_MODEL + (h + 1) * HEAD_DIM] = dk_ref[...].astype(out_ref.dtype)
                out_ref[:, 2 * D_MODEL + h * HEAD_DIM:2 * D_MODEL + (h + 1) * HEAD_DIM] = (
                    dv_ref[...].astype(out_ref.dtype))
            return None

        _skewed(len(units), [probs, dscores, grads])

    nxt_blk = lambda i: jnp.minimum((i + 1) * nqb, n_blocks - 1)
    main = lambda c: pl.BlockSpec((None, tq, D_MODEL), lambda g, i: (g, i, c))
    prev = lambda c: pl.BlockSpec((None, Q_BLOCK, D_MODEL), lambda g, i: (g, jnp.maximum(i * nqb - 1, 0), c))
    row = pl.BlockSpec((None, tq, D_MODEL), lambda g, i: (g, i, 0))
    row_n = pl.BlockSpec((None, Q_BLOCK, D_MODEL), lambda g, i: (g, nxt_blk(i), 0))
    col = pl.BlockSpec((None, tq, N_HEADS), lambda g, i: (g, i, 0))
    col_n = pl.BlockSpec((None, Q_BLOCK, N_HEADS), lambda g, i: (g, nxt_blk(i), 0))
    return _pcall(
        body, name=name, grid=(N_GROUPS, S // tq),
        in_specs=[main(0), main(1), main(2), prev(1), prev(2), row_n, row, row_n, col, col_n, col, col_n],
        out_specs=pl.BlockSpec((None, tq, 3 * D_MODEL), lambda g, i: (g, i, 0)),
        out_shape=jax.ShapeDtypeStruct((N_GROUPS, S, 3 * D_MODEL), BF16),
        scratch_shapes=[pltpu.VMEM((tq + Q_BLOCK, D_MODEL), BF16), pltpu.VMEM((tq + Q_BLOCK, D_MODEL), BF16),
                        pltpu.VMEM((tq, HEAD_DIM), F32), pltpu.VMEM((tq, HEAD_DIM), F32)],
        compiler_params=_params(("parallel", "parallel")))(qkv, qkv, qkv, qkv, qkv, qkv, do, do, lse, lse, dl, dl)


def _group_weights(lse_ref):
    l0, l1, l2 = lse_ref[0], lse_ref[1], lse_ref[2]
    m = jnp.maximum(jnp.maximum(l0, l1), l2)
    e = [jnp.exp(l0 - m), jnp.exp(l1 - m), jnp.exp(l2 - m)]
    den = e[0] + e[1] + e[2]
    return [ei / den for ei in e]


MERGE_TOKENS = 512


def _folded_views(a, tb):
    _, S, C = a.shape
    views, specs = [], []
    for g, dil in enumerate(ATTN_DILATIONS):
        views.append(a.reshape(N_GROUPS * dil, S // dil, C))
        specs.append(pl.BlockSpec((dil, tb // dil, C), lambda i, g=g: (g, i, 0)))
    return views, specs


def _unfold_into(dst_ref, folded_refs):
    for g, (dil, ref) in enumerate(zip(ATTN_DILATIONS, folded_refs)):
        n = ref.shape[1]
        for r in range(dil):
            for c, cols in enumerate(_lane_chunks(ref.shape[2])):
                dst_ref[g, c, _strided_rows(r, n, dil), :] = ref[r, :, cols]


def _merge_fwd(name, o, lse, tb=MERGE_TOKENS):
    S = o.shape[1]

    def body(o0_ref, o1_ref, o2_ref, lse_ref, out_ref, o_ref):
        _unfold_into(o_ref, (o0_ref, o1_ref, o2_ref))
        w = _group_weights(lse_ref)
        for h in range(N_HEADS):
            cols = slice(h * HEAD_DIM, (h + 1) * HEAD_DIM)
            acc = w[0][:, h:h + 1] * o_ref[0, h]
            for gg in range(1, N_GROUPS):
                acc = acc + w[gg][:, h:h + 1] * o_ref[gg, h]
            out_ref[:, cols] = acc.astype(out_ref.dtype)

    views, specs = _folded_views(o, tb)
    return _pcall(body, name=name, grid=(S // tb,),
                  in_specs=specs + [pl.BlockSpec((N_GROUPS, tb, N_HEADS), lambda i: (0, i, 0))],
                  out_specs=pl.BlockSpec((tb, D_MODEL), lambda i: (i, 0)),
                  out_shape=jax.ShapeDtypeStruct((S, D_MODEL), BF16),
                  scratch_shapes=[pltpu.VMEM((N_GROUPS, N_HEADS, tb, HEAD_DIM), F32)],
                  compiler_params=_params(("parallel",)))(*views, lse)


def _merge_bwd(name, d_out, o, lse, tb=MERGE_TOKENS):
    S = o.shape[1]
    n_chunks = sum(ATTN_DILATIONS)

    def body(d_ref, o0_ref, o1_ref, o2_ref, lse_ref, do_hbm, dl_ref, o_ref, dt_ref, df_ref, sems):
        i = pl.program_id(0)
        _unfold_into(o_ref, (o0_ref, o1_ref, o2_ref))
        w = _group_weights(lse_ref)
        for h in range(N_HEADS):
            cols = slice(h * HEAD_DIM, (h + 1) * HEAD_DIM)
            dv = d_ref[:, cols]
            merged = w[0][:, h:h + 1] * o_ref[0, h]
            for gg in range(1, N_GROUPS):
                merged = merged + w[gg][:, h:h + 1] * o_ref[gg, h]
            dsum = jnp.sum(dv * merged, axis=-1, keepdims=True)
            for gg in range(N_GROUPS):
                wg = w[gg][:, h:h + 1]
                dt_ref[gg, h] = wg * dv
                dl_ref[gg, :, h:h + 1] = wg * dsum
        copies = []
        for gg, dil in enumerate(ATTN_DILATIONS):
            n = tb // dil
            for r in range(dil):
                for h, cols in enumerate(_lane_chunks(D_MODEL)):
                    df_ref[gg, r * n:(r + 1) * n, cols] = (
                        dt_ref[gg, h, _strided_rows(r, n, dil), :].astype(df_ref.dtype))
                cp = pltpu.make_async_copy(df_ref.at[gg, pl.ds(r * n, n)],
                                           do_hbm.at[gg, pl.ds(r * (S // dil) + i * n, n)], sems.at[len(copies)])
                cp.start()
                copies.append(cp)
        for cp in copies:
            cp.wait()

    views, specs = _folded_views(o, tb)
    small = pl.BlockSpec((N_GROUPS, tb, N_HEADS), lambda i: (0, i, 0))
    return _pcall(body, name=name, grid=(S // tb,),
                  in_specs=[pl.BlockSpec((tb, D_MODEL), lambda i: (i, 0))] + specs + [small],
                  out_specs=[pl.BlockSpec(memory_space=pl.ANY), small],
                  out_shape=[jax.ShapeDtypeStruct((N_GROUPS, S, D_MODEL), BF16),
                             jax.ShapeDtypeStruct((N_GROUPS, S, N_HEADS), F32)],
                  scratch_shapes=[pltpu.VMEM((N_GROUPS, N_HEADS, tb, HEAD_DIM), F32),
                                  pltpu.VMEM((N_GROUPS, N_HEADS, tb, HEAD_DIM), F32),
                                  pltpu.VMEM((N_GROUPS, tb, D_MODEL), BF16), pltpu.SemaphoreType.DMA((n_chunks,))],
                  compiler_params=_params(("arbitrary",)))(d_out, *views, lse)


def _shift_rows(a, k):
    return pltpu.roll(a, k % a.shape[0], axis=0)


def _pool_level(g, levels):
    return jnp.where(g == 0, levels[0], jnp.where(g == 1, levels[1], jnp.where(g == 2, levels[2], levels[3])))


def _pool_fwd(name, u, w_group, scale, x_in, tr=1024):
    S, D = u.shape
    H = POOL_HALO

    def body(u_ref, halo_ref, w_ref, sc_ref, x_ref, y_ref, z_ref, xo_ref):
        g = pl.program_id(0)
        i = pl.program_id(1)
        uv = u_ref[...]
        halo = jnp.where(i > 0, halo_ref[...], 0.0)
        s = jnp.concatenate([halo, uv], axis=0)
        levels = []
        for k in (1, 2, 4, 8):
            s = s + _shift_rows(s, k)
            levels.append(s[H:])
        t = i * tr + lax.broadcasted_iota(jnp.int32, (tr, 1), 0)
        cnt = jnp.minimum(t + 1, jnp.left_shift(2, g)).astype(F32)
        y = _pool_level(g, levels) / cnt - uv
        yb = y.astype(BF16)
        z = _dot(yb, w_ref[...], "nn")
        y_ref[...] = yb
        z_ref[...] = z
        xo_ref[...] = x_ref[...] + z * sc_ref[...]

    blk = pl.BlockSpec((tr, POOL_DIM), lambda g, i: (i, g))
    return _pcall(
        body, name=name, grid=(D // POOL_DIM, S // tr),
        in_specs=[blk, pl.BlockSpec((H, POOL_DIM), lambda g, i: (jnp.maximum(i * (tr // H) - 1, 0), g)),
                  pl.BlockSpec((None, POOL_DIM, POOL_DIM), lambda g, i: (g, 0, 0)),
                  pl.BlockSpec((1, POOL_DIM), lambda g, i: (0, g)), blk],
        out_specs=[blk, blk, blk],
        out_shape=[jax.ShapeDtypeStruct((S, D), BF16), jax.ShapeDtypeStruct((S, D), F32),
                   jax.ShapeDtypeStruct((S, D), F32)],
        compiler_params=_params(("parallel", "parallel")))(u, u, w_group, scale, x_in)


def _pool_bwd(name, d_out, z, y, scale, w_group, tr=1024):
    S, D = d_out.shape
    H = POOL_HALO

    def body(d_ref, dn_ref, z_ref, y_ref, sc_ref, w_ref, du_ref, dw_ref, dsc_ref):
        g = pl.program_id(0)
        i = pl.program_id(1)
        dv = d_ref[...]
        dz = jnp.concatenate([dv, dn_ref[...]], axis=0) * sc_ref[...]
        dzb = dz.astype(BF16)
        dy = _dot(dzb, w_ref[...], "nt")
        t = i * tr + lax.broadcasted_iota(jnp.int32, (tr + H, 1), 0)
        cnt = jnp.minimum(t + 1, jnp.left_shift(2, g)).astype(F32)
        s = jnp.where(t < S, dy / cnt, 0.0)
        levels = []
        for k in (1, 2, 4, 8):
            s = s + _shift_rows(s, -k)
            levels.append(s[:tr])
        du_ref[...] = (_pool_level(g, levels) - dy[:tr]).astype(du_ref.dtype)
        dw = _dot(y_ref[...], dzb[:tr], "tn")
        dsc = jnp.sum(dv * z_ref[...], axis=0, keepdims=True)

        @pl.when(i == 0)
        def _():
            dw_ref[...] = dw
            dsc_ref[...] = dsc

        @pl.when(i > 0)
        def _():
            dw_ref[...] += dw
            dsc_ref[...] += dsc

    blk = pl.BlockSpec((tr, POOL_DIM), lambda g, i: (i, g))
    vec = pl.BlockSpec((1, POOL_DIM), lambda g, i: (0, g))
    mat = pl.BlockSpec((None, POOL_DIM, POOL_DIM), lambda g, i: (g, 0, 0))
    nxt = pl.BlockSpec((H, POOL_DIM), lambda g, i: (jnp.minimum((i + 1) * (tr // H), S // H - 1), g))
    return _pcall(
        body, name=name, grid=(D // POOL_DIM, S // tr), in_specs=[blk, nxt, blk, blk, vec, mat],
        out_specs=[blk, mat, vec],
        out_shape=[jax.ShapeDtypeStruct((S, D), BF16), jax.ShapeDtypeStruct((D // POOL_DIM, POOL_DIM, POOL_DIM), F32),
                   jax.ShapeDtypeStruct((1, D), F32)],
        compiler_params=_params(("parallel", "arbitrary")))(d_out, d_out, z, y, scale, w_group)


def _row_tile(rows, cols, target_bytes=1 << 20):
    best = rows
    for tr in range(8, rows + 1, 8):
        if rows % tr == 0 and tr * cols * 4 <= target_bytes:
            best = tr
    return best if best * cols * 4 <= 4 * target_bytes else rows


def _adamw(name, w, g, m, v):
    R, C = w.shape
    tr = _row_tile(R, C) if R % 8 == 0 else R

    def body(w_ref, g_ref, m_ref, v_ref, go_ref, d_ref, mo_ref, vo_ref):
        gv = g_ref[...]
        m2 = ADAM_B1 * m_ref[...] + (1.0 - ADAM_B1) * gv
        v2 = ADAM_B2 * v_ref[...] + (1.0 - ADAM_B2) * (gv * gv)
        m_hat = m2 / (1.0 - ADAM_B1 ** ADAM_STEP)
        v_hat = v2 / (1.0 - ADAM_B2 ** ADAM_STEP)
        d_ref[...] = -ADAM_LR * (m_hat / (jnp.sqrt(v_hat) + ADAM_EPS) + ADAM_WD * w_ref[...])
        go_ref[...] = gv
        mo_ref[...] = m2
        vo_ref[...] = v2

    blk = pl.BlockSpec((tr, C), lambda i: (i, 0))
    sds = jax.ShapeDtypeStruct((R, C), F32)
    return _pcall(body, name=name, grid=(R // tr,), in_specs=[blk] * 4, out_specs=[blk] * 4, out_shape=[sds] * 4,
                  compiler_params=_params(("parallel",)))(w, g, m, v)


def _sum_leading(name, a, out_dtype):
    n, R, C = a.shape
    tr = _row_tile(R, C) if R % 16 == 0 else R
    if tr % 16:
        tr = R

    def body(a_ref, o_ref):
        acc = a_ref[0].astype(F32)
        for j in range(1, n):
            acc = acc + a_ref[j].astype(F32)
        o_ref[...] = acc.astype(o_ref.dtype)

    return _pcall(body, name=name, grid=(R // tr,), in_specs=[pl.BlockSpec((n, tr, C), lambda i: (0, i, 0))],
                  out_specs=pl.BlockSpec((tr, C), lambda i: (i, 0)), out_shape=jax.ShapeDtypeStruct((R, C), out_dtype),
                  compiler_params=_params(("parallel",)))(a)


def _cast_into_slot(name, w, layer, chip, dtype):
    _, R, C = w.shape
    tr = _row_tile(R, C, 2 << 20)
    if tr % 16:
        tr = R

    def body(c_ref, w_ref, o_ref):
        o_ref[...] = w_ref[...].astype(o_ref.dtype)

    grid_spec = pltpu.PrefetchScalarGridSpec(
        num_scalar_prefetch=1, grid=(R // tr,),
        in_specs=[pl.BlockSpec((None, tr, C), lambda i, c_ref: (layer, i, 0))],
        out_specs=pl.BlockSpec((None, tr, C), lambda i, c_ref: (c_ref[0], i, 0)))
    return _pcall(body, name=name, grid_spec=grid_spec, out_shape=jax.ShapeDtypeStruct((N_CHIPS, R, C), dtype),
                  compiler_params=_params(("parallel",)))(chip, w)


def _owner_sum(name, mine, landed, chip, core, out=None, layer=None, col=None):
    _, half, C = mine.shape
    k, n_col = col if col is not None else (0, 1)
    tr = _row_tile(half, C)
    if tr % 16:
        tr = half
    nrt = half // tr
    has_out = out is not None

    def body(*refs):
        m_ref, l_ref, o_ref = refs[2], refs[3], refs[-1]
        acc = m_ref[...].astype(F32)
        for j in range(3):
            acc = acc + l_ref[j].astype(F32)
        o_ref[...] = acc

    if layer is None:
        out_shape = jax.ShapeDtypeStruct((2 * half, n_col * C), F32)
        o_spec = pl.BlockSpec((tr, C), lambda i, ch, co: (co[0] * nrt + i, k))
    else:
        out_shape = jax.ShapeDtypeStruct((2, 2 * half, C), F32)
        o_spec = pl.BlockSpec((None, tr, C), lambda i, ch, co: (layer, co[0] * nrt + i, 0))
    in_specs = [pl.BlockSpec((None, tr, C), lambda i, ch, co: (ch[0], i, 0)),
                pl.BlockSpec((3, tr, C), lambda i, ch, co: (0, i, 0))]
    operands = [chip, core, mine, landed]
    aliases = {}
    if has_out:
        in_specs.append(ANY)
        operands.append(out)
        aliases = {4: 0}
    grid_spec = pltpu.PrefetchScalarGridSpec(num_scalar_prefetch=2, grid=(nrt,), in_specs=in_specs, out_specs=o_spec)
    return _pcall(body, name=name, grid_spec=grid_spec, out_shape=out_shape, input_output_aliases=aliases,
                  compiler_params=_params(("parallel",)))(*operands)


def _add_my_half(name, p, q, core):
    n, R, C = p.shape
    half = R // 2

    def body(c_ref, p_ref, q_ref, o_ref):
        o_ref[...] = (p_ref[...].astype(F32) + q_ref[...].astype(F32)).astype(o_ref.dtype)

    grid_spec = pltpu.PrefetchScalarGridSpec(
        num_scalar_prefetch=1, grid=(n,),
        in_specs=[pl.BlockSpec((None, half, C), lambda s, c_ref: (s, c_ref[0], 0)),
                  pl.BlockSpec((None, half, C), lambda s, c_ref: (s, 0, 0))],
        out_specs=pl.BlockSpec((None, half, C), lambda s, c_ref: (s, 0, 0)))
    return _pcall(body, name=name, grid_spec=grid_spec, out_shape=jax.ShapeDtypeStruct((n, half, C), BF16),
                  compiler_params=_params(("parallel",)))(core, p, q)


ANY = pl.BlockSpec(memory_space=pl.ANY)


def _place():
    x, y, c = lax.axis_index("x"), lax.axis_index("y"), lax.axis_index("c")
    other_chips = [(1 - x, y), (x, 1 - y), (1 - x, 1 - y)]
    return x, y, c, other_chips


def _gather_weights(name, bufs, split):
    n = len(bufs)

    def body(*refs):
        outs = refs[n:2 * n]
        ici_send, ici_recv, d2d_send, d2d_recv = refs[2 * n:]
        x, y, c, chips = _place()
        me = 2 * x + y
        sibling = (x, y, 1 - c)

        def piece(t, chip, core_half):
            if not split[t]:
                return outs[t].at[chip]
            half = outs[t].shape[1] // 2
            return outs[t].at[chip, pl.ds(core_half * half, half)]

        sends = []
        for t in range(n):
            for j, (px, py) in enumerate(chips):
                sends.append(pltpu.make_async_remote_copy(
                    src_ref=piece(t, me, c), dst_ref=piece(t, me, c), send_sem=ici_send.at[3 * t + j],
                    recv_sem=ici_recv.at[3 * t + j], device_id=(px, py, c), device_id_type=MESH))
        for cp in sends:
            cp.start()
        passed = []
        for t in range(n):
            for j, (px, py) in enumerate(chips):
                landed = piece(t, 2 * px + py, c)
                pltpu.make_async_remote_copy(
                    src_ref=landed, dst_ref=landed, send_sem=ici_send.at[3 * t + j],
                    recv_sem=ici_recv.at[3 * t + j], device_id=(px, py, c), device_id_type=MESH).wait_recv()
                if split[t]:
                    fwd = pltpu.make_async_remote_copy(
                        src_ref=landed, dst_ref=landed, send_sem=d2d_send.at[3 * t + j],
                        recv_sem=d2d_recv.at[3 * t + j], device_id=sibling, device_id_type=MESH)
                    fwd.start()
                    passed.append(fwd)
        for t in range(n):
            if split[t]:
                for j, (px, py) in enumerate(chips):
                    theirs = piece(t, 2 * px + py, 1 - c)
                    pltpu.make_async_remote_copy(
                        src_ref=theirs, dst_ref=theirs, send_sem=d2d_send.at[3 * t + j],
                        recv_sem=d2d_recv.at[3 * t + j], device_id=sibling, device_id_type=MESH).wait_recv()
        for cp in sends + passed:
            cp.wait_send()

    out_shape = [jax.ShapeDtypeStruct(b.shape, b.dtype) for b in bufs]
    return _pcall(body, name=name, in_specs=[ANY] * n, out_specs=[ANY] * n, out_shape=out_shape,
                  input_output_aliases={t: t for t in range(n)},
                  scratch_shapes=[pltpu.SemaphoreType.DMA((3 * n,)), pltpu.SemaphoreType.DMA((3 * n,)),
                                  pltpu.SemaphoreType.DMA((3 * n,)), pltpu.SemaphoreType.DMA((3 * n,))])(*bufs)


def _remote(src, dst, send, recv, k, to):
    return pltpu.make_async_remote_copy(src_ref=src, dst_ref=dst, send_sem=send.at[k], recv_sem=recv.at[k],
                                        device_id=to, device_id_type=MESH)


def _in_place(bufs):
    return dict(operands=bufs, out_shapes=[jax.ShapeDtypeStruct(b.shape, b.dtype) for b in bufs],
                aliases={t: t for t in range(len(bufs))})


def _gather_over_ici(bufs, split):
    n = len(bufs)

    def copies(ins, outs, send, recv, base=0):
        x, y, c, chips = _place()
        me = 2 * x + y
        out = []
        for t in range(n):
            piece = outs[t].at[me]
            if split[t]:
                half = outs[t].shape[1] // 2
                piece = outs[t].at[me, pl.ds(c * half, half)]
            for j, (px, py) in enumerate(chips):
                out.append(_remote(piece, piece, send, recv, base + 3 * t + j, (px, py, c)))
        return out

    return _Rider(n_copies=3 * n, copies=copies, **_in_place(bufs))


def _gather_to_sibling(bufs):
    n = len(bufs)

    def copies(ins, outs, send, recv, base=0):
        x, y, c, chips = _place()
        out = []
        for t in range(n):
            half = outs[t].shape[1] // 2
            for j, (px, py) in enumerate(chips):
                piece = outs[t].at[2 * px + py, pl.ds(c * half, half)]
                out.append(_remote(piece, piece, send, recv, base + 3 * t + j, (x, y, 1 - c)))
        return out

    return _Rider(n_copies=3 * n, copies=copies, **_in_place(bufs))


def _swap_halves_rider(parts):
    n = len(parts)

    def copies(ins, outs, send, recv, base=0):
        x, y, c, _ = _place()
        out = []
        for t in range(n):
            half = ins[t].shape[1] // 2
            out.append(_remote(ins[t].at[:, pl.ds((1 - c) * half, half)], outs[t], send, recv, base + t,
                               (x, y, 1 - c)))
        return out

    shapes = [jax.ShapeDtypeStruct((p.shape[0], p.shape[1] // 2, p.shape[2]), p.dtype) for p in parts]
    return _Rider(parts, shapes, {}, n, copies)


def _scatter_rider(sums):
    n = len(sums)

    def copies(ins, outs, send, recv, base=0):
        x, y, c, chips = _place()
        out = []
        for t in range(n):
            for j, (px, py) in enumerate(chips):
                out.append(_remote(ins[t].at[2 * px + py], outs[t].at[j], send, recv, base + 3 * t + j, (px, py, c)))
        return out

    shapes = [jax.ShapeDtypeStruct((3,) + s.shape[1:], s.dtype) for s in sums]
    return _Rider(sums, shapes, {}, 3 * n, copies)


def _share_rider(blocks, pieces):
    def copies(ins, outs, send, recv, base=0):
        x, y, c, _ = _place()
        out = []
        for k, (t, l, col) in enumerate(pieces):
            ref = outs[t] if l is None else outs[t].at[l]
            r2 = ref.shape[0] // 2
            piece = ref.at[pl.ds(c * r2, r2)]
            if col is not None:
                width = ref.shape[1] // col[1]
                piece = ref.at[pl.ds(c * r2, r2), pl.ds(col[0] * width, width)]
            out.append(_remote(piece, piece, send, recv, base + k, (x, y, 1 - c)))
        return out

    return _Rider(n_copies=len(pieces), copies=copies, **_in_place(blocks))


def _gather_small(name, v):
    def body(v_ref, out_ref, send_sem, recv_sem, local_sem):
        x, y, c, _ = _place()
        me = 4 * x + 2 * y + c
        flips = [(fx, fy, fc) for fx in (0, 1) for fy in (0, 1) for fc in (0, 1)][1:]
        local = pltpu.make_async_copy(v_ref, out_ref.at[me], local_sem)
        local.start()
        copies = []
        for j, (fx, fy, fc) in enumerate(flips):
            peer = (x ^ fx, y ^ fy, c ^ fc)
            copies.append(pltpu.make_async_remote_copy(
                src_ref=v_ref, dst_ref=out_ref.at[me], send_sem=send_sem.at[j], recv_sem=recv_sem.at[j],
                device_id=peer, device_id_type=MESH))
        for cp in copies:
            cp.start()
        for j, (fx, fy, fc) in enumerate(flips):
            peer = (x ^ fx, y ^ fy, c ^ fc)
            pltpu.make_async_remote_copy(
                src_ref=v_ref, dst_ref=out_ref.at[4 * peer[0] + 2 * peer[1] + peer[2]], send_sem=send_sem.at[j],
                recv_sem=recv_sem.at[j], device_id=peer, device_id_type=MESH).wait_recv()
        for cp in copies:
            cp.wait_send()
        local.wait()

    return _pcall(body, name=name, in_specs=[ANY], out_specs=ANY,
                  out_shape=jax.ShapeDtypeStruct((8,) + v.shape, v.dtype),
                  scratch_shapes=[pltpu.SemaphoreType.DMA((7,)), pltpu.SemaphoreType.DMA((7,)),
                                  pltpu.SemaphoreType.DMA])(v)


def _fold(a, dil):
    if dil == 1:
        return a
    S, C = a.shape
    return a.reshape(S // dil, dil, C).transpose(1, 0, 2).reshape(S, C)


def _unfold(a, dil):
    if dil == 1:
        return a
    S, C = a.shape
    return a.reshape(dil, S // dil, C).transpose(1, 0, 2).reshape(S, C)


def _fold_groups(a):
    return jnp.stack([_fold(a[g] if a.ndim == 3 else a, d) for g, d in enumerate(ATTN_DILATIONS)])


def _unfold_groups(a):
    return jnp.stack([_unfold(a[g], d) for g, d in enumerate(ATTN_DILATIONS)])


class _NoComm:
    def __init__(self, weights):
        self.weights = weights
        self.grads = {}

    def w(self, name):
        return self.weights[name]

    def run(self, fn, name, *args, **kw):
        return fn(name, *args, **kw)

    def grad(self, name, value):
        self.grads[name] = value


def _local_step(xs, tgt, attn_norm, ffn_norm, final_norm, comm):
    run = comm.run
    hf, qkv = None, None
    for g in range(N_GROUPS):
        hf, qkv = run(_norm_fold_qkv, "qkv_proj%d" % g, g, xs, attn_norm, comm.w("qkv"), hf, qkv)
    o_f, lse_f = run(_attn_fwd, "attn_fwd", qkv)
    lse_t = _unfold_groups(lse_f)
    merged = run(_merge_fwd, "merge_fwd", o_f, lse_t)
    x1, h1 = run(_proj_res_norm, "attn_out", merged, comm.w("o"), xs, ffn_norm[0:1])
    gu0, act0 = run(_ffn_up, "ffn0_up", h1, comm.w("gu0"))
    x2, h2 = run(_proj_res_norm, "ffn0_down", act0, comm.w("d0"), x1, comm.w("pool_norm"))
    u = run(_mm_rows, "pool_in", h2, comm.w("p"), F32, 512)
    y, z, x3 = run(_pool_fwd, "pool_fwd", u, comm.w("pg"), comm.w("pool_scale"), x2)
    h3 = run(_rms_fwd, "norm_ffn1", x3, ffn_norm[1:2])
    gu1, act1 = run(_ffn_up, "ffn1_up", h3, comm.w("gu1"))
    loss, dx4, d_final = run(_proj_res_loss, "ffn1_down", act1, comm.w("d1"), x3, final_norm, tgt)

    dgu1 = run(_ffn_down_bwd, "ffn1_down_bwd", dx4, comm.w("d1"), gu1)
    comm.grad("d1", run(_mm_tn, "ffn1_down_dw", act1, dx4, FF_SHARD, 2048))
    comm.grad("gu1", run(_ffn_dw_up, "ffn1_up_dw", h3, dgu1))
    dx3, d_ffn1 = run(_ffn_dh, "ffn1_up_dh", dgu1, comm.w("gu1"), x3, ffn_norm[1:2], dx4)

    du, dw_pg, d_scale = run(_pool_bwd, "pool_bwd", dx3, z, y, comm.w("pool_scale"), comm.w("pg"))
    comm.grad("pg", dw_pg)
    comm.grad("p", run(_mm_tn, "pool_in_dw", h2, du, D_MODEL, h2.shape[0]))
    dx2, d_pool = run(_dh_norm_bwd, "pool_in_dh", du, comm.w("p"), x2, comm.w("pool_norm"), dx3)

    dgu0 = run(_ffn_down_bwd, "ffn0_down_bwd", dx2, comm.w("d0"), gu0)
    comm.grad("d0", run(_mm_tn, "ffn0_down_dw", act0, dx2, FF_SHARD, 2048))
    comm.grad("gu0", run(_ffn_dw_up, "ffn0_up_dw", h1, dgu0))
    dx1, d_ffn0 = run(_ffn_dh, "ffn0_up_dh", dgu0, comm.w("gu0"), x1, ffn_norm[0:1], dx2)

    d_merged = run(_mm_nt_rows, "attn_out_dh", dx1, comm.w("o"), 512)
    comm.grad("o", run(_mm_tn, "attn_out_dw", merged, dx1, D_MODEL, 2048))
    do_f, dl_t = run(_merge_bwd, "merge_bwd", d_merged, o_f, lse_t)
    dqkv = run(_attn_bwd, "attn_bwd", qkv, do_f, lse_f, _fold_groups(dl_t))
    for p in range(QKV_PIECES):
        comm.grad("qkv%d" % p, run(_qkv_dw, "qkv_dw%d" % p, p, hf, dqkv))
    dhf = run(_qkv_dh, "qkv_dh", dqkv, comm.w("qkv"))
    grad_x, d_attn = run(_rms_bwd_folded, "norm_attn_bwd", xs, attn_norm, dhf, dx1)

    small = dict(attn=d_attn, ffn0=d_ffn0, ffn1=d_ffn1, final=d_final, pool=d_pool, scale=d_scale)
    return loss, grad_x, small


TENSORS = ("qkv", "o", "p", "pg", "gu0", "gu1", "d0", "d1")


def _block_of(name):
    if name[:-1] in ("gu", "d"):
        return name[:-1], int(name[-1]), None
    if name[:-1] == "qkv":
        return "qkv", None, (int(name[-1]), QKV_PIECES)
    return name, None, None


class _MeshComm:
    def __init__(self, bufs, chip_arr, core_arr, pg_rows):
        self.bufs = dict(bufs)
        self.chip_arr, self.core_arr, self.pg_rows = chip_arr, core_arr, pg_rows
        self.parts, self.sums, self.blocks = {}, {}, {}
        ici = lambda *names: lambda: self.gather(names, True)
        d2d = lambda *names: lambda: self.gather(names, False)
        swap = lambda *names: lambda: self.swap(names)
        scatter = lambda *names: lambda: self.scatter(names)
        share = lambda *names: lambda: self.share(names)
        self.plan = {
            "qkv_proj0": [ici("d0")],
            "qkv_proj1": [ici("o", "p", "pg", "pool_norm", "pool_scale")],
            "qkv_proj2": [d2d("d0", "o", "p", "pg")],
            "attn_fwd": [ici("gu0")],
            "merge_fwd": [d2d("gu0")],
            "ffn0_up": [ici("gu1")],
            "ffn0_down": [d2d("gu1"), ici("d1")],
            "pool_in": [d2d("d1")],
            "ffn1_up_dh": [swap("d1", "gu1")],
            "ffn0_down_bwd": [scatter("gu1")],
            "ffn0_down_dw": [scatter("d1")],
            "ffn0_up_dh": [swap("pg", "p", "d0", "gu0"), share("gu1", "d1")],
            "merge_bwd": [swap("o")],
            "attn_bwd": [scatter("pg", "p", "d0", "gu0", "o")],
            "qkv_dw0": [share("pg", "p", "d0", "gu0", "o")],
            "qkv_dw1": [swap("qkv0")],
            "qkv_dw2": [scatter("qkv0"), swap("qkv1")],
            "qkv_dh": [share("qkv0"), scatter("qkv1"), swap("qkv2")],
            "norm_attn_bwd": [share("qkv1"), scatter("qkv2")],
        }

    def gather(self, names, over_ici):
        bufs = [self.bufs[n] for n in names]
        rider = _gather_over_ici(bufs, [n in TENSORS for n in names]) if over_ici else _gather_to_sibling(bufs)
        return rider, lambda res: self.bufs.update(zip(names, res))

    def swap(self, names):
        def done(res):
            for n, q in zip(names, res):
                self.sums[n] = _add_my_half("chip_sum_" + n, self.parts[n], q, self.core_arr)
        return _swap_halves_rider([self.parts[n] for n in names]), done

    def scatter(self, names):
        def done(res):
            for n, landed in zip(names, res):
                key, layer, col = _block_of(n)
                self.blocks[key] = _owner_sum("owner_sum_" + n, self.sums[n], landed, self.chip_arr, self.core_arr,
                                              out=self.blocks.get(key), layer=layer, col=col)
        return _scatter_rider([self.sums[n] for n in names]), done

    def share(self, names):
        keys, pieces = [], []
        for n in names:
            key, layer, col = _block_of(n)
            if key not in keys:
                keys.append(key)
            pieces.append((keys.index(key), layer, col))
        return _share_rider([self.blocks[k] for k in keys], pieces), lambda res: self.blocks.update(zip(keys, res))

    def alone(self, name, made):
        rider, done = made
        done(_comm_only(name, rider))

    def run(self, fn, name, *args, **kw):
        made = [make() for make in self.plan.get(name, [])]
        if not made:
            return fn(name, *args, **kw)
        riders = [r for r, _ in made]
        out = _ride(riders[0] if len(riders) == 1 else _riders(*riders), fn, name, *args, **kw)
        for r, done in made:
            done(r.results)
        return out

    def w(self, name):
        b = self.bufs[name]
        if name in ("o", "p", "d0", "d1"):
            return b.reshape(-1, b.shape[-1])
        if name == "pg":
            b = b.reshape(N_CHIPS, 4, self.pg_rows, POOL_DIM).transpose(1, 0, 2, 3)
            return b.reshape(4, POOL_DIM, POOL_DIM)
        if name in ("pool_norm", "pool_scale"):
            return b.reshape(1, -1)
        return b

    def grad(self, name, value):
        if name == "pg":
            value = value.reshape(4, N_CHIPS, self.pg_rows, POOL_DIM).transpose(1, 0, 2, 3)
            value = value.reshape(N_CHIPS, 4 * self.pg_rows, POOL_DIM).astype(BF16)
        elif name in ("o", "p", "d0", "d1"):
            value = value.reshape(N_CHIPS, value.shape[0] // N_CHIPS, value.shape[1])
        self.parts[name] = value


def kernel(x, attn_norm, w_qkv, w_attn_out, pool_norm, w_pool_in, w_pool_group, pool_scale, ffn_norm, w_ffn_gate_up, w_ffn_down, final_norm, loss_target, m_attn_norm, m_w_qkv, m_w_attn_out, m_pool_norm, m_w_pool_in, m_w_pool_group, m_pool_scale, m_ffn_norm, m_w_ffn_gate_up, m_w_ffn_down, m_final_norm, v_attn_norm, v_w_qkv, v_w_attn_out, v_pool_norm, v_w_pool_in, v_w_pool_group, v_pool_scale, v_ffn_norm, v_w_ffn_gate_up, v_w_ffn_down, v_final_norm):
    D = D_MODEL
    chip = 2 * lax.axis_index("x") + lax.axis_index("y")
    core = lax.axis_index("c")
    pg_rows = w_pool_group.shape[2]

    chip_arr = chip.astype(jnp.int32).reshape(1)
    core_arr = core.astype(jnp.int32).reshape(1)

    shards = [(w_qkv, 0), (w_attn_out, 0), (w_pool_in, 0), (w_pool_group.reshape(1, 4 * pg_rows, POOL_DIM), 0),
              (w_ffn_gate_up, 0), (w_ffn_gate_up, 1), (w_ffn_down, 0), (w_ffn_down, 1)]
    bufs = {nm: _cast_into_slot("cast_" + nm, s, l, chip_arr, BF16) for nm, (s, l) in zip(TENSORS, shards)}
    bufs["pool_norm"] = _cast_into_slot("place_pool_norm", pool_norm[None], 0, chip_arr, F32)
    bufs["pool_scale"] = _cast_into_slot("place_pool_scale", pool_scale[None], 0, chip_arr, F32)
    bufs["qkv"] = _gather_weights("gather_qkv", [bufs["qkv"]], [True])[0]

    comm = _MeshComm(bufs, chip_arr, core_arr, pg_rows)
    loss_part, grad_x, small = _local_step(x[0], loss_target[0], attn_norm, ffn_norm, final_norm.reshape(1, D), comm)
    comm.alone("halves_to_sibling", comm.share(["qkv%d" % (QKV_PIECES - 1)]))
    g_w_qkv, g_w_o, g_w_p, g_w_pg, g_w_gu, g_w_d = [comm.blocks[k] for k in ("qkv", "o", "p", "pg", "gu", "d")]

    rows = jnp.concatenate([small["attn"], small["ffn0"], small["ffn1"], small["final"], small["pool"],
                            small["scale"], jnp.zeros((2, D), F32)], axis=0)
    all_rows = _gather_small("gather_gain_grads", rows)
    tot = _sum_leading("sum_gain_grads", all_rows, F32)
    g_attn_norm = tot[0:1]
    g_ffn_norm = tot[1:3]
    g_final_norm = tot[3]
    g_pool_norm = lax.dynamic_slice(tot, (4, chip * POOL_DIM), (1, POOL_DIM))
    g_pool_scale = lax.dynamic_slice(tot, (5, chip * POOL_DIM), (1, POOL_DIM))

    loss = lax.psum(loss_part[0, 0], ("x", "y", "c"))

    def update(name, w, g, m, v):
        shape = w.shape
        cols = shape[-1]
        as2d = lambda a: a.reshape(-1, cols)
        return [a.reshape(shape) for a in _adamw("adamw_" + name, as2d(w), as2d(g), as2d(m), as2d(v))]

    results = [
        update("attn_norm", attn_norm, g_attn_norm, m_attn_norm, v_attn_norm),
        update("w_qkv", w_qkv, g_w_qkv, m_w_qkv, v_w_qkv),
        update("w_attn_out", w_attn_out, g_w_o, m_w_attn_out, v_w_attn_out),
        update("pool_norm", pool_norm, g_pool_norm, m_pool_norm, v_pool_norm),
        update("w_pool_in", w_pool_in, g_w_p, m_w_pool_in, v_w_pool_in),
        update("w_pool_group", w_pool_group, g_w_pg, m_w_pool_group, v_w_pool_group),
        update("pool_scale", pool_scale, g_pool_scale, m_pool_scale, v_pool_scale),
        update("ffn_norm", ffn_norm, g_ffn_norm, m_ffn_norm, v_ffn_norm),
        update("w_ffn_gate_up", w_ffn_gate_up, g_w_gu, m_w_ffn_gate_up, v_w_ffn_gate_up),
        update("w_ffn_down", w_ffn_down, g_w_d, m_w_ffn_down, v_w_ffn_down),
        update("final_norm", final_norm, g_final_norm, m_final_norm, v_final_norm),
    ]
    grads = [r[0] for r in results]
    deltas = [r[1] for r in results]
    new_m = [r[2] for r in results]
    new_v = [r[3] for r in results]
    return (loss, grad_x[None], *grads, *deltas, *new_m, *new_v)
```

```python
import functools

import jax
import jax.numpy as jnp
from jax import lax
from jax.experimental import pallas as pl
from jax.experimental.pallas import tpu as pltpu

F32 = jnp.float32
BF16 = jnp.bfloat16
MESH = pl.DeviceIdType.MESH

D_MODEL = 1024
N_HEADS = 8
HEAD_DIM = 128
Q_BLOCK = 128
ATTN_DILATIONS = (1, 4, 16)
N_GROUPS = 3
D_FF = 2816
N_CHIPS = 4
FF_SHARD = 2 * D_FF // N_CHIPS
QKV_SHARD = 3 * 3 * D_MODEL // N_CHIPS
QKV_TILE = 768
POOL_WINDOWS = (2, 4, 8, 16)
POOL_DIM = 256
POOL_HALO = 16
RMS_EPS = 1e-6
NEG = -1e30
ADAM_LR = 0.001
ADAM_B1 = 0.9
ADAM_B2 = 0.999
ADAM_EPS = 1e-08
ADAM_WD = 0.01
ADAM_STEP = 10
VMEM_LIMIT = 56 * 1024 * 1024

_DN = {
    "nn": (((1,), (0,)), ((), ())),
    "nt": (((1,), (1,)), ((), ())),
    "tn": (((0,), (0,)), ((), ())),
}


class _Rider:
    def __init__(self, operands, out_shapes, aliases, n_copies, copies=None, start=None, finish=None):
        self.operands = list(operands)
        self.out_shapes = list(out_shapes)
        self.aliases = dict(aliases)
        self.n_copies = n_copies
        self.results = None

        def start_copies(ins, outs, send, recv, base=0):
            for cp in copies(ins, outs, send, recv, base):
                cp.start()

        def wait_copies(ins, outs, send, recv, base=0):
            for cp in copies(ins, outs, send, recv, base):
                cp.wait()

        self.start = start or start_copies
        self.finish = finish or wait_copies


def _riders(*riders):
    operands, out_shapes, aliases, offsets = [], [], {}, []
    n = 0
    for r in riders:
        offsets.append((len(operands), len(out_shapes), n))
        aliases.update({len(operands) + i: len(out_shapes) + o for i, o in r.aliases.items()})
        operands += r.operands
        out_shapes += r.out_shapes
        n += r.n_copies

    def each(which):
        def go(ins, outs, send, recv, base=0):
            for r, (i0, o0, s0) in zip(riders, offsets):
                getattr(r, which)(ins[i0:i0 + len(r.operands)], outs[o0:o0 + len(r.out_shapes)], send, recv,
                                  base + s0)
        return go

    both = _Rider(operands, out_shapes, aliases, n, start=each("start"), finish=each("finish"))
    both.parts = (riders, offsets)
    return both


_pending_rider = []


def _ride(rider, fn, *args, **kw):
    _pending_rider.append(rider)
    out = fn(*args, **kw)
    assert not _pending_rider
    if hasattr(rider, "parts"):
        for r, (_, o0, _) in zip(*rider.parts):
            r.results = rider.results[o0:o0 + len(r.out_shapes)]
    return out


def _pcall(body, prefetch=0, **kw):
    def build(body, kw):
        if not prefetch:
            return pl.pallas_call(body, **kw)
        kw = dict(kw)
        grid_spec = pltpu.PrefetchScalarGridSpec(
            num_scalar_prefetch=prefetch, grid=kw.pop("grid"), in_specs=kw.pop("in_specs"),
            out_specs=kw.pop("out_specs"), scratch_shapes=kw.pop("scratch_shapes", []))
        return pl.pallas_call(body, grid_spec=grid_spec, **kw)

    if not _pending_rider:
        return build(body, kw)
    rider = _pending_rider.pop()
    grid = kw.get("grid", ())
    in_specs = list(kw.get("in_specs", []))
    single = not isinstance(kw["out_shape"], (list, tuple))
    out_shape = [kw["out_shape"]] if single else list(kw["out_shape"])
    out_specs = [kw["out_specs"]] if single else list(kw["out_specs"])
    scratch = list(kw.get("scratch_shapes", []))
    n_in, n_out, n_scr = len(in_specs), len(out_shape), len(scratch)
    r_in, r_out = len(rider.operands), len(rider.out_shapes)

    def wrapped(*refs):
        scalars, refs = refs[:prefetch], refs[prefetch:]
        ins, rins = refs[:n_in], refs[n_in:n_in + r_in]
        outs = refs[n_in + r_in:n_in + r_in + n_out]
        routs = refs[n_in + r_in + n_out:n_in + r_in + n_out + r_out]
        scr = refs[n_in + r_in + n_out + r_out:n_in + r_in + n_out + r_out + n_scr]
        send, recv = refs[-2:]
        ids = [pl.program_id(a) for a in range(len(grid))]
        first, last = True, True
        for a, pid in enumerate(ids):
            first = jnp.logical_and(first, pid == 0)
            last = jnp.logical_and(last, pid == grid[a] - 1)
        if grid:
            pl.when(first)(lambda: rider.start(rins, routs, send, recv))
        else:
            rider.start(rins, routs, send, recv)
        body(*scalars, *ins, *outs, *scr)
        if grid:
            pl.when(last)(lambda: rider.finish(rins, routs, send, recv))
        else:
            rider.finish(rins, routs, send, recv)

    any_spec = pl.BlockSpec(memory_space=pl.ANY)
    kw2 = dict(kw)
    kw2.update(
        in_specs=in_specs + [any_spec] * r_in, out_specs=out_specs + [any_spec] * r_out,
        out_shape=out_shape + rider.out_shapes,
        scratch_shapes=scratch + [pltpu.SemaphoreType.DMA((rider.n_copies,)),
                                  pltpu.SemaphoreType.DMA((rider.n_copies,))],
        input_output_aliases={**kw.get("input_output_aliases", {}),
                              **{prefetch + n_in + i: n_out + o for i, o in rider.aliases.items()}})
    if grid:
        kw2["compiler_params"] = _params(("arbitrary",) * len(grid))
    call = build(wrapped, kw2)

    def run(*operands):
        res = call(*operands, *rider.operands)
        rider.results = list(res[n_out:])
        return res[0] if single else list(res[:n_out])

    return run


def _comm_only(name, rider):
    def make():
        return _pcall(lambda: None, name=name, in_specs=[], out_specs=[], out_shape=[])()
    _ride(rider, make)
    return rider.results


def _params(sem):
    return pltpu.CompilerParams(dimension_semantics=sem, vmem_limit_bytes=VMEM_LIMIT)


def _dot(a, b, mode):
    return lax.dot_general(a, b, _DN[mode], preferred_element_type=F32)


def _mm(name, mode, grid, a, a_spec, b, b_spec, out_shape, o_spec, acc_shape, res=None, res_spec=None):
    nk = grid[-1]
    has_res = res is not None

    def body(*refs):
        a_ref, b_ref = refs[0], refs[1]
        res_ref = refs[2] if has_res else None
        o_ref = refs[2 + has_res]
        part = _dot(a_ref[...].astype(BF16), b_ref[...].astype(BF16), mode)
        if nk == 1:
            if has_res:
                part = part + res_ref[...]
            o_ref[...] = part.astype(o_ref.dtype)
            return
        acc_ref = refs[3 + has_res]
        k = pl.program_id(len(grid) - 1)

        @pl.when(k == 0)
        def _():
            acc_ref[...] = part

        @pl.when(k > 0)
        def _():
            acc_ref[...] += part

        @pl.when(k == nk - 1)
        def _():
            r = acc_ref[...]
            if has_res:
                r = r + res_ref[...]
            o_ref[...] = r.astype(o_ref.dtype)

    in_specs = [a_spec, b_spec] + ([res_spec] if has_res else [])
    operands = [a, b] + ([res] if has_res else [])
    scratch = [] if nk == 1 else [pltpu.VMEM(acc_shape, F32)]
    sem = ("parallel",) * (len(grid) - 1) + ("arbitrary",)
    return _pcall(body, name=name, grid=grid, in_specs=in_specs, out_specs=o_spec, out_shape=out_shape,
                  scratch_shapes=scratch, compiler_params=_params(sem))(*operands)


def _mm_rows(name, a, b, out_dtype, tm, res=None):
    M, K = a.shape
    N = b.shape[1]
    return _mm(name, "nn", (M // tm, 1), a, pl.BlockSpec((tm, K), lambda i, k: (i, 0)),
               b, pl.BlockSpec((K, N), lambda i, k: (0, 0)),
               jax.ShapeDtypeStruct((M, N), out_dtype), pl.BlockSpec((tm, N), lambda i, k: (i, 0)), None,
               res=res, res_spec=pl.BlockSpec((tm, N), lambda i, k: (i, 0)))


def _mm_nt_rows(name, a, b, tm):
    M, N = a.shape
    K = b.shape[0]
    return _mm(name, "nt", (M // tm, 1), a, pl.BlockSpec((tm, N), lambda i, k: (i, 0)),
               b, pl.BlockSpec((K, N), lambda i, k: (0, 0)),
               jax.ShapeDtypeStruct((M, K), F32), pl.BlockSpec((tm, K), lambda i, k: (i, 0)), None)


def _mm_tn(name, a, b, tm, tk):
    T, M = a.shape
    N = b.shape[1]
    return _mm(name, "tn", (M // tm, T // tk), a, pl.BlockSpec((tk, tm), lambda i, k: (k, i)),
               b, pl.BlockSpec((tk, N), lambda i, k: (k, 0)),
               jax.ShapeDtypeStruct((M, N), BF16), pl.BlockSpec((tm, N), lambda i, k: (i, 0)), (tm, N))


def _rms_fwd(name, x, g, tr=512):
    S, D = x.shape

    def body(x_ref, g_ref, o_ref):
        xf = x_ref[...]
        r = lax.rsqrt(jnp.mean(xf * xf, axis=-1, keepdims=True) + RMS_EPS)
        o_ref[...] = ((xf * r) * g_ref[...]).astype(o_ref.dtype)

    return _pcall(body, name=name, grid=(S // tr,),
                  in_specs=[pl.BlockSpec((tr, D), lambda i: (i, 0)), pl.BlockSpec((1, D), lambda i: (0, 0))],
                  out_specs=pl.BlockSpec((tr, D), lambda i: (i, 0)),
                  out_shape=jax.ShapeDtypeStruct((S, D), BF16), compiler_params=_params(("parallel",)))(x, g)


def _norm_bwd_rows(xf, gain, dh, dres):
    r = lax.rsqrt(jnp.mean(xf * xf, axis=-1, keepdims=True) + RMS_EPS)
    xh = xf * r
    t = dh * gain
    dx = dres + r * (t - xh * jnp.mean(t * xh, axis=-1, keepdims=True))
    return dx, jnp.sum(dh * xh, axis=0, keepdims=True)


def _accumulate(ref, value, first):
    @pl.when(first)
    def _():
        ref[...] = value

    @pl.when(jnp.logical_not(first))
    def _():
        ref[...] += value


def _rms_bwd_folded(name, x, g, dhf, dres, tb=512):
    S, D = x.shape

    def body(x_ref, g_ref, d0_ref, d1_ref, d2_ref, dres_ref, dx_ref, dg_ref, dh_ref):
        chunks = _lane_chunks(D)
        for c, cols in enumerate(chunks):
            dh_ref[c] = d0_ref[0, :, cols]
        for dil, ref in ((ATTN_DILATIONS[1], d1_ref), (ATTN_DILATIONS[2], d2_ref)):
            n = tb // dil
            for k in range(dil):
                for c, cols in enumerate(chunks):
                    dh_ref[c, _strided_rows(k, n, dil), :] += ref[k, :, cols]
        dh = jnp.concatenate([dh_ref[c] for c in range(len(chunks))], axis=1)
        dx, dg = _norm_bwd_rows(x_ref[...], g_ref[...], dh, dres_ref[...])
        dx_ref[...] = dx
        _accumulate(dg_ref, dg, pl.program_id(0) == 0)

    views, specs = _folded_views(dhf, tb)
    row = pl.BlockSpec((tb, D), lambda i: (i, 0))
    vec = pl.BlockSpec((1, D), lambda i: (0, 0))
    return _pcall(body, name=name, grid=(S // tb,), in_specs=[row, vec] + specs + [row], out_specs=[row, vec],
                  out_shape=[jax.ShapeDtypeStruct((S, D), F32), jax.ShapeDtypeStruct((1, D), F32)],
                  scratch_shapes=[pltpu.VMEM((D // LANES, tb, LANES), F32)],
                  compiler_params=_params(("arbitrary",)))(x, g, *views, dres)


def _proj_res_norm(name, a, w, res, gain, tm=512):
    M, K = a.shape
    D = w.shape[1]

    def body(a_ref, w_ref, res_ref, g_ref, x_ref, h_ref):
        xf = res_ref[...] + _dot(a_ref[...], w_ref[...], "nn")
        x_ref[...] = xf
        r = lax.rsqrt(jnp.mean(xf * xf, axis=-1, keepdims=True) + RMS_EPS)
        h_ref[...] = ((xf * r) * g_ref[...]).astype(h_ref.dtype)

    row = pl.BlockSpec((tm, D), lambda i: (i, 0))
    return _pcall(body, name=name, grid=(M // tm,),
                  in_specs=[pl.BlockSpec((tm, K), lambda i: (i, 0)), pl.BlockSpec((K, D), lambda i: (0, 0)), row,
                            pl.BlockSpec((1, D), lambda i: (0, 0))],
                  out_specs=[row, row],
                  out_shape=[jax.ShapeDtypeStruct((M, D), F32), jax.ShapeDtypeStruct((M, D), BF16)],
                  compiler_params=_params(("parallel",)))(a, w, res, gain)


def _proj_res_loss(name, a, w, res, gain, tgt, tm=512):
    M, K = a.shape
    D = w.shape[1]

    def body(a_ref, w_ref, res_ref, g_ref, t_ref, loss_ref, dx_ref, dg_ref):
        first = pl.program_id(0) == 0
        xf = res_ref[...] + _dot(a_ref[...], w_ref[...], "nn")
        r = lax.rsqrt(jnp.mean(xf * xf, axis=-1, keepdims=True) + RMS_EPS)
        xh = xf * r
        gv = g_ref[...]
        e = xh * gv - t_ref[...]
        lp = 0.5 * jnp.sum(jnp.mean(e * e, axis=-1, keepdims=True), axis=0, keepdims=True)
        dy = e * (1.0 / D)
        t = dy * gv
        dx_ref[...] = r * (t - xh * jnp.mean(t * xh, axis=-1, keepdims=True))
        _accumulate(dg_ref, jnp.sum(dy * xh, axis=0, keepdims=True), first)
        _accumulate(loss_ref, lp, first)

    row = pl.BlockSpec((tm, D), lambda i: (i, 0))
    vec = pl.BlockSpec((1, D), lambda i: (0, 0))
    return _pcall(body, name=name, grid=(M // tm,),
                  in_specs=[pl.BlockSpec((tm, K), lambda i: (i, 0)), pl.BlockSpec((K, D), lambda i: (0, 0)), row,
                            vec, row],
                  out_specs=[pl.BlockSpec((1, 1), lambda i: (0, 0)), row, vec],
                  out_shape=[jax.ShapeDtypeStruct((1, 1), F32), jax.ShapeDtypeStruct((M, D), F32),
                             jax.ShapeDtypeStruct((1, D), F32)],
                  compiler_params=_params(("arbitrary",)))(a, w, res, gain, tgt)


def _dh_norm_bwd(name, d, w, x, gain, dres, tm=512):
    M, N = d.shape
    D = w.shape[0]

    def body(d_ref, w_ref, x_ref, g_ref, dres_ref, dx_ref, dg_ref):
        dh = _dot(d_ref[...].astype(BF16), w_ref[...], "nt")
        dx, dg = _norm_bwd_rows(x_ref[...], g_ref[...], dh, dres_ref[...])
        dx_ref[...] = dx
        _accumulate(dg_ref, dg, pl.program_id(0) == 0)

    row = pl.BlockSpec((tm, D), lambda i: (i, 0))
    vec = pl.BlockSpec((1, D), lambda i: (0, 0))
    return _pcall(body, name=name, grid=(M // tm,),
                  in_specs=[pl.BlockSpec((tm, N), lambda i: (i, 0)), pl.BlockSpec((D, N), lambda i: (0, 0)), row, vec,
                            row],
                  out_specs=[row, vec],
                  out_shape=[jax.ShapeDtypeStruct((M, D), F32), jax.ShapeDtypeStruct((1, D), F32)],
                  compiler_params=_params(("arbitrary",)))(d, w, x, gain, dres)


def _sigmoid(v):
    return 0.5 * jnp.tanh(0.5 * v) + 0.5


def _ffn_up(name, h, w_gu, tm=512):
    S, D = h.shape
    W = FF_SHARD

    def body(h_ref, wg_ref, wu_ref, gu_ref, act_ref):
        hv = h_ref[...]
        gate = _dot(hv, wg_ref[...], "nn")
        up = _dot(hv, wu_ref[...], "nn")
        gu_ref[0] = gate.astype(gu_ref.dtype)
        gu_ref[1] = up.astype(gu_ref.dtype)
        sig = _sigmoid(gate)
        act_ref[...] = ((gate * sig) * up).astype(act_ref.dtype)

    return _pcall(
        body, name=name, grid=(2, S // tm),
        in_specs=[pl.BlockSpec((tm, D), lambda j, i: (i, 0)),
                  pl.BlockSpec((None, D, W), lambda j, i: (j, 0, 0)),
                  pl.BlockSpec((None, D, W), lambda j, i: (j + 2, 0, 0))],
        out_specs=[pl.BlockSpec((2, tm, W), lambda j, i: (0, i, j)), pl.BlockSpec((tm, W), lambda j, i: (i, j))],
        out_shape=[jax.ShapeDtypeStruct((2, S, D_FF), BF16), jax.ShapeDtypeStruct((S, D_FF), BF16)],
        compiler_params=_params(("parallel", "parallel")))(h, w_gu, w_gu)


def _ffn_down_bwd(name, dx, w_down, gu, tm=512):
    S, D = dx.shape
    W = FF_SHARD

    def body(dx_ref, w_ref, gu_ref, dgu_ref):
        dact = _dot(dx_ref[...].astype(BF16), w_ref[...], "nt")
        gate = gu_ref[0].astype(F32)
        up = gu_ref[1].astype(F32)
        sig = _sigmoid(gate)
        silu = gate * sig
        dgu_ref[0] = (dact * up * (sig * (1.0 + gate * (1.0 - sig)))).astype(dgu_ref.dtype)
        dgu_ref[1] = (dact * silu).astype(dgu_ref.dtype)

    blk = pl.BlockSpec((2, tm, W), lambda j, i: (0, i, j))
    return _pcall(
        body, name=name, grid=(2, S // tm),
        in_specs=[pl.BlockSpec((tm, D), lambda j, i: (i, 0)), pl.BlockSpec((W, D), lambda j, i: (j, 0)), blk],
        out_specs=blk, out_shape=jax.ShapeDtypeStruct((2, S, D_FF), BF16),
        compiler_params=_params(("parallel", "parallel")))(dx, w_down, gu)


def _ffn_dw_up(name, h, dgu, tk=2048):
    S, D = h.shape
    W = FF_SHARD
    tk = min(tk, S)
    return _mm(name, "tn", (N_CHIPS, S // tk), h, pl.BlockSpec((tk, D), lambda s, k: (k, 0)),
               dgu, pl.BlockSpec((None, tk, W), lambda s, k: (s // 2, k, s % 2)),
               jax.ShapeDtypeStruct((N_CHIPS, D, W), BF16), pl.BlockSpec((None, D, W), lambda s, k: (s, 0, 0)),
               (D, W))


def _ffn_dh(name, dgu, w_gu, x, gain, dres, tm=512):
    S = dgu.shape[1]
    W = FF_SHARD

    def body(a_ref, w_hbm, x_ref, g_ref, dres_ref, dx_ref, dg_ref, w_ref, acat_ref, sems):
        first = pl.program_id(0) == 0

        @pl.when(first)
        def _():
            copies = [pltpu.make_async_copy(w_hbm.at[s], w_ref.at[:, pl.ds(s * W, W)], sems.at[s])
                      for s in range(N_CHIPS)]
            for cp in copies:
                cp.start()
            for cp in copies:
                cp.wait()

        acat_ref[:, :D_FF] = a_ref[0]
        acat_ref[:, D_FF:] = a_ref[1]
        dh = _dot(acat_ref[...], w_ref[...], "nt")
        dx, dg = _norm_bwd_rows(x_ref[...], g_ref[...], dh, dres_ref[...])
        dx_ref[...] = dx
        _accumulate(dg_ref, dg, first)

    row = pl.BlockSpec((tm, D_MODEL), lambda i: (i, 0))
    vec = pl.BlockSpec((1, D_MODEL), lambda i: (0, 0))
    return _pcall(body, name=name, grid=(S // tm,),
                  in_specs=[pl.BlockSpec((2, tm, D_FF), lambda i: (0, i, 0)), pl.BlockSpec(memory_space=pl.ANY),
                            row, vec, row],
                  out_specs=[row, vec],
                  out_shape=[jax.ShapeDtypeStruct((S, D_MODEL), F32), jax.ShapeDtypeStruct((1, D_MODEL), F32)],
                  scratch_shapes=[pltpu.VMEM((D_MODEL, 2 * D_FF), BF16), pltpu.VMEM((tm, 2 * D_FF), BF16),
                                  pltpu.SemaphoreType.DMA((N_CHIPS,))],
                  compiler_params=_params(("arbitrary",)))(dgu, w_gu, x, gain, dres)


FOLD_TOKENS = 1024
LANES = 128


def _lane_chunks(width):
    return [slice(c * LANES, (c + 1) * LANES) for c in range(width // LANES)]


def _strided_rows(r, n, dil):
    return pl.ds(r, n) if dil == 1 else pl.ds(r, n, stride=dil)


def _norm_fold(name, g, x, gain, hf):
    S, D = x.shape
    dil = ATTN_DILATIONS[g]
    n = FOLD_TOKENS // dil
    seg = S // dil

    def body(*refs):
        x_ref, g_ref = refs[:2]
        hf_ref, xc_ref = refs[-2:]
        xf = x_ref[...]
        inv = lax.rsqrt(jnp.mean(xf * xf, axis=-1, keepdims=True) + RMS_EPS)
        h = (xf * inv) * g_ref[...]
        if dil == 1:
            hf_ref[0] = h.astype(BF16)
            return
        for c, cols in enumerate(_lane_chunks(D)):
            xc_ref[c] = h[:, cols]
        for r in range(dil):
            for c, cols in enumerate(_lane_chunks(D)):
                hf_ref[r, :, cols] = xc_ref[c, _strided_rows(r, n, dil), :].astype(BF16)

    in_specs = [pl.BlockSpec((FOLD_TOKENS, D), lambda i: (i, 0)), pl.BlockSpec((1, D), lambda i: (0, 0))]
    operands = [x, gain]
    aliases = {}
    if hf is not None:
        in_specs.append(pl.BlockSpec(memory_space=pl.ANY))
        operands.append(hf.reshape(N_GROUPS * dil, seg, D))
        aliases = {2: 0}
    hf = _pcall(body, name=name, grid=(S // FOLD_TOKENS,), in_specs=in_specs,
                out_specs=pl.BlockSpec((dil, n, D), lambda i: (g, i, 0)),
                out_shape=jax.ShapeDtypeStruct((N_GROUPS * dil, seg, D), BF16),
                scratch_shapes=[pltpu.VMEM((D // LANES, FOLD_TOKENS, LANES), F32)], input_output_aliases=aliases,
                compiler_params=_params(("arbitrary",)))(*operands)
    return hf.reshape(N_GROUPS, S, D)


def _qkv_tiles(name, piece, hf, w, qkv, chip, tm=1024):
    S = hf.shape[1]
    per_group = 3 * D_MODEL // QKV_TILE

    def tile(q, c_ref):
        if piece is None:
            return QKV_PIECES * c_ref[0] + q
        return QKV_PIECES * ((c_ref[0] + 1 + q) % N_CHIPS) + piece

    def body(*refs):
        a_ref, w_ref, o_ref = refs[1], refs[2], refs[-1]
        o_ref[...] = _dot(a_ref[...], w_ref[...], "nn").astype(o_ref.dtype)

    if piece is None:
        w_spec = pl.BlockSpec((D_MODEL, QKV_TILE), lambda q, i, c_ref: (0, q))
    else:
        w_spec = pl.BlockSpec((None, D_MODEL, QKV_TILE), lambda q, i, c_ref: ((c_ref[0] + 1 + q) % N_CHIPS, 0, 0))
    in_specs = [pl.BlockSpec((None, tm, D_MODEL), lambda q, i, c_ref: (tile(q, c_ref) // per_group, i, 0)), w_spec]
    operands = [chip, hf, w]
    aliases = {}
    if qkv is not None:
        in_specs.append(pl.BlockSpec(memory_space=pl.ANY))
        operands.append(qkv)
        aliases = {3: 0}
    return _pcall(
        body, prefetch=1, name=name, grid=(N_CHIPS - 1, S // tm), in_specs=in_specs,
        out_specs=pl.BlockSpec((None, tm, QKV_TILE),
                               lambda q, i, c_ref: (tile(q, c_ref) // per_group, i, tile(q, c_ref) % per_group)),
        out_shape=jax.ShapeDtypeStruct((N_GROUPS, S, 3 * D_MODEL), BF16), input_output_aliases=aliases,
        compiler_params=_params(("arbitrary", "arbitrary")))(*operands)


QKV_PIECES = QKV_SHARD // QKV_TILE


def _qkv_dw(name, p, hf, dqkv):
    S = hf.shape[1]
    per_group = 3 * D_MODEL // QKV_TILE
    tile = lambda s: QKV_PIECES * s + p
    return _mm(name, "tn", (N_CHIPS, 1),
               hf, pl.BlockSpec((None, S, D_MODEL), lambda s, k: (tile(s) // per_group, 0, 0)),
               dqkv, pl.BlockSpec((None, S, QKV_TILE), lambda s, k: (tile(s) // per_group, 0, tile(s) % per_group)),
               jax.ShapeDtypeStruct((N_CHIPS, D_MODEL, QKV_TILE), BF16),
               pl.BlockSpec((None, D_MODEL, QKV_TILE), lambda s, k: (s, 0, 0)), None)


def _qkv_dh(name, g, dqkv, w_pieces, dhf, tm=1024):
    S = dqkv.shape[1]
    per_group = 3 * D_MODEL // QKV_TILE

    def body(*refs):
        a_ref, w_hbm = refs[0], refs[1:1 + QKV_PIECES]
        o_ref, w_ref, sems = refs[-3:]

        @pl.when(pl.program_id(0) == 0)
        def _():
            copies = []
            for j in range(per_group):
                t = per_group * g + j
                copies.append(pltpu.make_async_copy(w_hbm[t % QKV_PIECES].at[t // QKV_PIECES],
                                                    w_ref.at[:, pl.ds(j * QKV_TILE, QKV_TILE)], sems.at[j]))
            for cp in copies:
                cp.start()
            for cp in copies:
                cp.wait()

        o_ref[...] = _dot(a_ref[...], w_ref[...], "nt")

    any_spec = pl.BlockSpec(memory_space=pl.ANY)
    in_specs = [pl.BlockSpec((None, tm, 3 * D_MODEL), lambda i: (g, i, 0))] + [any_spec] * QKV_PIECES
    operands = [dqkv] + list(w_pieces)
    aliases = {}
    if dhf is not None:
        in_specs.append(any_spec)
        operands.append(dhf)
        aliases = {1 + QKV_PIECES: 0}
    return _pcall(body, name=name, grid=(S // tm,), in_specs=in_specs,
                  out_specs=pl.BlockSpec((None, tm, D_MODEL), lambda i: (g, i, 0)),
                  out_shape=jax.ShapeDtypeStruct((N_GROUPS, S, D_MODEL), F32),
                  scratch_shapes=[pltpu.VMEM((D_MODEL, 3 * D_MODEL), BF16), pltpu.SemaphoreType.DMA((per_group,))],
                  input_output_aliases=aliases, compiler_params=_params(("arbitrary",)))(*operands)


def _alibi_coef(g, h):
    vals = [-(2.0 ** (-8.0 * (gg * N_HEADS + h + 1) / (N_GROUPS * N_HEADS))) * ATTN_DILATIONS[gg]
            for gg in range(N_GROUPS)]
    return jnp.where(g == 0, vals[0], jnp.where(g == 1, vals[1], vals[2])).astype(F32)


def _blocks_per_segment(g, S):
    return jnp.right_shift(S // Q_BLOCK, 2 * g)


def _band_bias(pen):
    row = lax.broadcasted_iota(jnp.int32, (Q_BLOCK, 2 * Q_BLOCK), 0)
    col = lax.broadcasted_iota(jnp.int32, (Q_BLOCK, 2 * Q_BLOCK), 1)
    dist = Q_BLOCK + row - col
    valid = jnp.logical_and(dist >= 0, dist <= Q_BLOCK)
    base = jnp.where(valid, jnp.where(col < Q_BLOCK, pen, 0.0), NEG).astype(F32)
    return base, dist.astype(F32)


def _skewed(n_units, stages):
    state = {}
    for step in range(n_units + len(stages) - 1):
        for s, stage in enumerate(stages):
            u = step - s
            if 0 <= u < n_units:
                state[u] = stage(u, state.get(u))
                if s == len(stages) - 1:
                    del state[u]


def _attn_fwd(name, qkv, tq=512):
    S = qkv.shape[1]
    nqb = tq // Q_BLOCK
    scale = HEAD_DIM ** -0.5

    def body(q_ref, k_ref, kp_ref, v_ref, vp_ref, o_ref, lse_ref, kf_ref, vf_ref):
        g = pl.program_id(0)
        i = pl.program_id(1)
        nb = _blocks_per_segment(g, S)
        kf_ref[:Q_BLOCK] = kp_ref[...]
        kf_ref[Q_BLOCK:] = k_ref[...]
        vf_ref[:Q_BLOCK] = vp_ref[...]
        vf_ref[Q_BLOCK:] = v_ref[...]
        coefs = [_alibi_coef(g, h) for h in range(N_HEADS)]
        units = [(b, h) for b in range(nqb) for h in range(N_HEADS)]
        bias = {}

        def scores(u, _):
            b, h = units[u]
            if b not in bias:
                pen = jnp.where((i * nqb + b) % nb != 0, 0.0, NEG).astype(F32)
                bias.clear()
                bias[b] = _band_bias(pen)
            base, dist = bias[b]
            cols = slice(h * HEAD_DIM, (h + 1) * HEAD_DIM)
            q = q_ref[b * Q_BLOCK:(b + 1) * Q_BLOCK, cols]
            kk = kf_ref[b * Q_BLOCK:(b + 2) * Q_BLOCK, cols]
            return _dot(q, kk, "nt") * scale + (coefs[h] * dist + base)

        def softmax(u, s):
            m = jnp.max(s, axis=-1, keepdims=True)
            p = jnp.exp(s - m)
            den = jnp.sum(p, axis=-1, keepdims=True)
            return (p * (1.0 / den)).astype(BF16), m + jnp.log(den)

        def output(u, st):
            b, h = units[u]
            pn, lse = st
            cols = slice(h * HEAD_DIM, (h + 1) * HEAD_DIM)
            rows = slice(b * Q_BLOCK, (b + 1) * Q_BLOCK)
            o_ref[rows, cols] = _dot(pn, vf_ref[b * Q_BLOCK:(b + 2) * Q_BLOCK, cols], "nn")
            lse_ref[rows, h:h + 1] = lse

        _skewed(len(units), [scores, softmax, output])

    main = lambda c: pl.BlockSpec((None, tq, D_MODEL), lambda g, i: (g, i, c))
    prev = lambda c: pl.BlockSpec((None, Q_BLOCK, D_MODEL), lambda g, i: (g, jnp.maximum(i * nqb - 1, 0), c))
    return _pcall(
        body, name=name, grid=(N_GROUPS, S // tq),
        in_specs=[main(0), main(1), prev(1), main(2), prev(2)],
        out_specs=[pl.BlockSpec((None, tq, D_MODEL), lambda g, i: (g, i, 0)),
                   pl.BlockSpec((None, tq, N_HEADS), lambda g, i: (g, i, 0))],
        out_shape=[jax.ShapeDtypeStruct((N_GROUPS, S, D_MODEL), F32),
                   jax.ShapeDtypeStruct((N_GROUPS, S, N_HEADS), F32)],
        scratch_shapes=[pltpu.VMEM((tq + Q_BLOCK, D_MODEL), BF16), pltpu.VMEM((tq + Q_BLOCK, D_MODEL), BF16)],
        compiler_params=_params(("parallel", "parallel")))(qkv, qkv, qkv, qkv, qkv)


def _attn_bwd(name, qkv, do, lse, dl, tq=512):
    S = qkv.shape[1]
    nqb = tq // Q_BLOCK
    n_blocks = S // Q_BLOCK
    scale = HEAD_DIM ** -0.5
    KEYS = 2 * Q_BLOCK

    def body(q_ref, k_ref, v_ref, kp_ref, vp_ref, qn_ref, do_ref, don_ref, lse_ref, lsen_ref, dl_ref, dln_ref,
             out_ref, kf_ref, vf_ref, dk_ref, dv_ref):
        g = pl.program_id(0)
        i = pl.program_id(1)
        nb = _blocks_per_segment(g, S)
        kf_ref[:Q_BLOCK] = kp_ref[...]
        kf_ref[Q_BLOCK:] = k_ref[...]
        vf_ref[:Q_BLOCK] = vp_ref[...]
        vf_ref[Q_BLOCK:] = v_ref[...]
        coefs = [_alibi_coef(g, h) for h in range(N_HEADS)]
        units = [(h, b) for h in range(N_HEADS) for b in range(nqb + 1)]
        bias = {}

        def band(b):
            if b not in bias:
                blk = i * nqb + b
                ok = blk % nb != 0
                if b == nqb:
                    ok = jnp.logical_and(ok, blk < n_blocks)
                bias[b] = _band_bias(jnp.where(ok, 0.0, NEG).astype(F32))
            return bias[b]

        def operands(h, b):
            cols = slice(h * HEAD_DIM, (h + 1) * HEAD_DIM)
            if b == nqb:
                keys = slice(tq, tq + Q_BLOCK)
                return (qn_ref[:, cols], don_ref[:, cols], lsen_ref[:, h:h + 1], dln_ref[:, h:h + 1],
                        kf_ref[keys, cols], vf_ref[keys, cols])
            rows = slice(b * Q_BLOCK, (b + 1) * Q_BLOCK)
            keys = slice(b * Q_BLOCK, b * Q_BLOCK + KEYS)
            return (q_ref[rows, cols], do_ref[rows, cols], lse_ref[rows, h:h + 1], dl_ref[rows, h:h + 1],
                    kf_ref[keys, cols], vf_ref[keys, cols])

        def probs(u, _):
            h, b = units[u]
            q, do_b, lse_b, dl_b, kk, vv = operands(h, b)
            base, dist = band(b)
            if b == nqb:
                base, dist = base[:, :Q_BLOCK], dist[:, :Q_BLOCK]
            p = jnp.exp(_dot(q, kk, "nt") * scale + (coefs[h] * dist + base) - lse_b)
            return p, _dot(do_b, vv, "nt")

        def dscores(u, st):
            h, b = units[u]
            p, dp = st
            dl_b = operands(h, b)[3]
            return ((p * (dp - dl_b)) * scale).astype(BF16), p.astype(BF16)

        def grads(u, st):
            h, b = units[u]
            ds, pb = st
            q, do_b, _, _, kk, _ = operands(h, b)
            cols = slice(h * HEAD_DIM, (h + 1) * HEAD_DIM)
            if b < nqb:
                out_ref[b * Q_BLOCK:(b + 1) * Q_BLOCK, cols] = _dot(ds, kk, "nn").astype(out_ref.dtype)
            if b == 0:
                dk_ref[:Q_BLOCK] = _dot(ds[:, Q_BLOCK:], q, "tn")
                dv_ref[:Q_BLOCK] = _dot(pb[:, Q_BLOCK:], do_b, "tn")
                return None
            dk = _dot(ds, q, "tn")
            dv = _dot(pb, do_b, "tn")
            before = slice((b - 1) * Q_BLOCK, b * Q_BLOCK)
            dk_ref[before] += dk[:Q_BLOCK]
            dv_ref[before] += dv[:Q_BLOCK]
            if b < nqb:
                own = slice(b * Q_BLOCK, (b + 1) * Q_BLOCK)
                dk_ref[own] = dk[Q_BLOCK:]
                dv_ref[own] = dv[Q_BLOCK:]
            else:
                out_ref[:, D_MODEL + h * HEAD_DIM:D_MODEL + (h + 1) * HEAD_DIM] = dk_ref[...].astype(out_ref.dtype)
                out_ref[:, 2 * D_MODEL + h * HEAD_DIM:2 * D_MODEL + (h + 1) * HEAD_DIM] = (
                    dv_ref[...].astype(out_ref.dtype))
            return None

        _skewed(len(units), [probs, dscores, grads])

    nxt_blk = lambda i: jnp.minimum((i + 1) * nqb, n_blocks - 1)
    main = lambda c: pl.BlockSpec((None, tq, D_MODEL), lambda g, i: (g, i, c))
    prev = lambda c: pl.BlockSpec((None, Q_BLOCK, D_MODEL), lambda g, i: (g, jnp.maximum(i * nqb - 1, 0), c))
    row = pl.BlockSpec((None, tq, D_MODEL), lambda g, i: (g, i, 0))
    row_n = pl.BlockSpec((None, Q_BLOCK, D_MODEL), lambda g, i: (g, nxt_blk(i), 0))
    col = pl.BlockSpec((None, tq, N_HEADS), lambda g, i: (g, i, 0))
    col_n = pl.BlockSpec((None, Q_BLOCK, N_HEADS), lambda g, i: (g, nxt_blk(i), 0))
    return _pcall(
        body, name=name, grid=(N_GROUPS, S // tq),
        in_specs=[main(0), main(1), main(2), prev(1), prev(2), row_n, row, row_n, col, col_n, col, col_n],
        out_specs=pl.BlockSpec((None, tq, 3 * D_MODEL), lambda g, i: (g, i, 0)),
        out_shape=jax.ShapeDtypeStruct((N_GROUPS, S, 3 * D_MODEL), BF16),
        scratch_shapes=[pltpu.VMEM((tq + Q_BLOCK, D_MODEL), BF16), pltpu.VMEM((tq + Q_BLOCK, D_MODEL), BF16),
                        pltpu.VMEM((tq, HEAD_DIM), F32), pltpu.VMEM((tq, HEAD_DIM), F32)],
        compiler_params=_params(("parallel", "parallel")))(qkv, qkv, qkv, qkv, qkv, qkv, do, do, lse, lse, dl, dl)


def _group_weights(lse_ref):
    l0, l1, l2 = lse_ref[0], lse_ref[1], lse_ref[2]
    m = jnp.maximum(jnp.maximum(l0, l1), l2)
    e = [jnp.exp(l0 - m), jnp.exp(l1 - m), jnp.exp(l2 - m)]
    den = e[0] + e[1] + e[2]
    return [ei / den for ei in e]


MERGE_TOKENS = 512


def _folded_views(a, tb):
    _, S, C = a.shape
    views, specs = [], []
    for g, dil in enumerate(ATTN_DILATIONS):
        views.append(a.reshape(N_GROUPS * dil, S // dil, C))
        specs.append(pl.BlockSpec((dil, tb // dil, C), lambda i, g=g: (g, i, 0)))
    return views, specs


def _unfold_into(dst_ref, folded_refs):
    for g, (dil, ref) in enumerate(zip(ATTN_DILATIONS, folded_refs)):
        n = ref.shape[1]
        for r in range(dil):
            for c, cols in enumerate(_lane_chunks(ref.shape[2])):
                dst_ref[g, c, _strided_rows(r, n, dil), :] = ref[r, :, cols]


def _merge_fwd(name, o, lse, tb=MERGE_TOKENS):
    S = o.shape[1]

    def body(o0_ref, o1_ref, o2_ref, lse_ref, out_ref, o_ref):
        _unfold_into(o_ref, (o0_ref, o1_ref, o2_ref))
        w = _group_weights(lse_ref)
        for h in range(N_HEADS):
            cols = slice(h * HEAD_DIM, (h + 1) * HEAD_DIM)
            acc = w[0][:, h:h + 1] * o_ref[0, h]
            for gg in range(1, N_GROUPS):
                acc = acc + w[gg][:, h:h + 1] * o_ref[gg, h]
            out_ref[:, cols] = acc.astype(out_ref.dtype)

    views, specs = _folded_views(o, tb)
    return _pcall(body, name=name, grid=(S // tb,),
                  in_specs=specs + [pl.BlockSpec((N_GROUPS, tb, N_HEADS), lambda i: (0, i, 0))],
                  out_specs=pl.BlockSpec((tb, D_MODEL), lambda i: (i, 0)),
                  out_shape=jax.ShapeDtypeStruct((S, D_MODEL), BF16),
                  scratch_shapes=[pltpu.VMEM((N_GROUPS, N_HEADS, tb, HEAD_DIM), F32)],
                  compiler_params=_params(("parallel",)))(*views, lse)


def _merge_bwd(name, d_out, o, lse, tb=MERGE_TOKENS):
    S = o.shape[1]
    n_chunks = sum(ATTN_DILATIONS)

    def body(d_ref, o0_ref, o1_ref, o2_ref, lse_ref, do_hbm, dl_ref, o_ref, dt_ref, df_ref, sems):
        i = pl.program_id(0)
        _unfold_into(o_ref, (o0_ref, o1_ref, o2_ref))
        w = _group_weights(lse_ref)
        for h in range(N_HEADS):
            cols = slice(h * HEAD_DIM, (h + 1) * HEAD_DIM)
            dv = d_ref[:, cols]
            merged = w[0][:, h:h + 1] * o_ref[0, h]
            for gg in range(1, N_GROUPS):
                merged = merged + w[gg][:, h:h + 1] * o_ref[gg, h]
            dsum = jnp.sum(dv * merged, axis=-1, keepdims=True)
            for gg in range(N_GROUPS):
                wg = w[gg][:, h:h + 1]
                dt_ref[gg, h] = wg * dv
                dl_ref[gg, :, h:h + 1] = wg * dsum
        copies = []
        for gg, dil in enumerate(ATTN_DILATIONS):
            n = tb // dil
            for r in range(dil):
                for h, cols in enumerate(_lane_chunks(D_MODEL)):
                    df_ref[gg, r * n:(r + 1) * n, cols] = (
                        dt_ref[gg, h, _strided_rows(r, n, dil), :].astype(df_ref.dtype))
                cp = pltpu.make_async_copy(df_ref.at[gg, pl.ds(r * n, n)],
                                           do_hbm.at[gg, pl.ds(r * (S // dil) + i * n, n)], sems.at[len(copies)])
                cp.start()
                copies.append(cp)
        for cp in copies:
            cp.wait()

    views, specs = _folded_views(o, tb)
    small = pl.BlockSpec((N_GROUPS, tb, N_HEADS), lambda i: (0, i, 0))
    return _pcall(body, name=name, grid=(S // tb,),
                  in_specs=[pl.BlockSpec((tb, D_MODEL), lambda i: (i, 0))] + specs + [small],
                  out_specs=[pl.BlockSpec(memory_space=pl.ANY), small],
                  out_shape=[jax.ShapeDtypeStruct((N_GROUPS, S, D_MODEL), BF16),
                             jax.ShapeDtypeStruct((N_GROUPS, S, N_HEADS), F32)],
                  scratch_shapes=[pltpu.VMEM((N_GROUPS, N_HEADS, tb, HEAD_DIM), F32),
                                  pltpu.VMEM((N_GROUPS, N_HEADS, tb, HEAD_DIM), F32),
                                  pltpu.VMEM((N_GROUPS, tb, D_MODEL), BF16), pltpu.SemaphoreType.DMA((n_chunks,))],
                  compiler_params=_params(("arbitrary",)))(d_out, *views, lse)


def _shift_rows(a, k):
    return pltpu.roll(a, k % a.shape[0], axis=0)


def _pool_level(g, levels):
    return jnp.where(g == 0, levels[0], jnp.where(g == 1, levels[1], jnp.where(g == 2, levels[2], levels[3])))


def _pool_fwd(name, u, w_group, scale, x_in, tr=1024):
    S, D = u.shape
    H = POOL_HALO

    def body(u_ref, halo_ref, w_ref, sc_ref, x_ref, y_ref, z_ref, xo_ref):
        g = pl.program_id(0)
        i = pl.program_id(1)
        uv = u_ref[...]
        halo = jnp.where(i > 0, halo_ref[...], 0.0)
        s = jnp.concatenate([halo, uv], axis=0)
        levels = []
        for k in (1, 2, 4, 8):
            s = s + _shift_rows(s, k)
            levels.append(s[H:])
        t = i * tr + lax.broadcasted_iota(jnp.int32, (tr, 1), 0)
        cnt = jnp.minimum(t + 1, jnp.left_shift(2, g)).astype(F32)
        y = _pool_level(g, levels) / cnt - uv
        yb = y.astype(BF16)
        z = _dot(yb, w_ref[...], "nn")
        y_ref[...] = yb
        z_ref[...] = z
        xo_ref[...] = x_ref[...] + z * sc_ref[...]

    blk = pl.BlockSpec((tr, POOL_DIM), lambda g, i: (i, g))
    return _pcall(
        body, name=name, grid=(D // POOL_DIM, S // tr),
        in_specs=[blk, pl.BlockSpec((H, POOL_DIM), lambda g, i: (jnp.maximum(i * (tr // H) - 1, 0), g)),
                  pl.BlockSpec((None, POOL_DIM, POOL_DIM), lambda g, i: (g, 0, 0)),
                  pl.BlockSpec((1, POOL_DIM), lambda g, i: (0, g)), blk],
        out_specs=[blk, blk, blk],
        out_shape=[jax.ShapeDtypeStruct((S, D), BF16), jax.ShapeDtypeStruct((S, D), F32),
                   jax.ShapeDtypeStruct((S, D), F32)],
        compiler_params=_params(("parallel", "parallel")))(u, u, w_group, scale, x_in)


def _pool_bwd(name, d_out, z, y, scale, w_group, tr=1024):
    S, D = d_out.shape
    H = POOL_HALO

    def body(d_ref, dn_ref, z_ref, y_ref, sc_ref, w_ref, du_ref, dw_ref, dsc_ref):
        g = pl.program_id(0)
        i = pl.program_id(1)
        dv = d_ref[...]
        dz = jnp.concatenate([dv, dn_ref[...]], axis=0) * sc_ref[...]
        dzb = dz.astype(BF16)
        dy = _dot(dzb, w_ref[...], "nt")
        t = i * tr + lax.broadcasted_iota(jnp.int32, (tr + H, 1), 0)
        cnt = jnp.minimum(t + 1, jnp.left_shift(2, g)).astype(F32)
        s = jnp.where(t < S, dy / cnt, 0.0)
        levels = []
        for k in (1, 2, 4, 8):
            s = s + _shift_rows(s, -k)
            levels.append(s[:tr])
        du_ref[...] = (_pool_level(g, levels) - dy[:tr]).astype(du_ref.dtype)
        dw = _dot(y_ref[...], dzb[:tr], "tn")
        dsc = jnp.sum(dv * z_ref[...], axis=0, keepdims=True)

        @pl.when(i == 0)
        def _():
            dw_ref[...] = dw
            dsc_ref[...] = dsc

        @pl.when(i > 0)
        def _():
            dw_ref[...] += dw
            dsc_ref[...] += dsc

    blk = pl.BlockSpec((tr, POOL_DIM), lambda g, i: (i, g))
    vec = pl.BlockSpec((1, POOL_DIM), lambda g, i: (0, g))
    mat = pl.BlockSpec((None, POOL_DIM, POOL_DIM), lambda g, i: (g, 0, 0))
    nxt = pl.BlockSpec((H, POOL_DIM), lambda g, i: (jnp.minimum((i + 1) * (tr // H), S // H - 1), g))
    return _pcall(
        body, name=name, grid=(D // POOL_DIM, S // tr), in_specs=[blk, nxt, blk, blk, vec, mat],
        out_specs=[blk, mat, vec],
        out_shape=[jax.ShapeDtypeStruct((S, D), BF16), jax.ShapeDtypeStruct((D // POOL_DIM, POOL_DIM, POOL_DIM), F32),
                   jax.ShapeDtypeStruct((1, D), F32)],
        compiler_params=_params(("parallel", "arbitrary")))(d_out, d_out, z, y, scale, w_group)


def _row_tile(rows, cols, target_bytes=1 << 20):
    best = rows
    for tr in range(8, rows + 1, 8):
        if rows % tr == 0 and tr * cols * 4 <= target_bytes:
            best = tr
    return best if best * cols * 4 <= 4 * target_bytes else rows


def _adamw(name, w, g, m, v):
    R, C = w.shape
    tr = _row_tile(R, C) if R % 8 == 0 else R

    def body(w_ref, g_ref, m_ref, v_ref, go_ref, d_ref, mo_ref, vo_ref):
        gv = g_ref[...]
        m2 = ADAM_B1 * m_ref[...] + (1.0 - ADAM_B1) * gv
        v2 = ADAM_B2 * v_ref[...] + (1.0 - ADAM_B2) * (gv * gv)
        m_hat = m2 * (1.0 / (1.0 - ADAM_B1 ** ADAM_STEP))
        v_hat = v2 * (1.0 / (1.0 - ADAM_B2 ** ADAM_STEP))
        d_ref[...] = -ADAM_LR * (m_hat / (jnp.sqrt(v_hat) + ADAM_EPS) + ADAM_WD * w_ref[...])
        go_ref[...] = gv
        mo_ref[...] = m2
        vo_ref[...] = v2

    blk = pl.BlockSpec((tr, C), lambda i: (i, 0))
    sds = jax.ShapeDtypeStruct((R, C), F32)
    return _pcall(body, name=name, grid=(R // tr,), in_specs=[blk] * 4, out_specs=[blk] * 4, out_shape=[sds] * 4,
                  compiler_params=_params(("parallel",)))(w, g, m, v)


def _sum_leading(name, a, out_dtype):
    n, R, C = a.shape
    tr = _row_tile(R, C) if R % 16 == 0 else R
    if tr % 16:
        tr = R

    def body(a_ref, o_ref):
        acc = a_ref[0].astype(F32)
        for j in range(1, n):
            acc = acc + a_ref[j].astype(F32)
        o_ref[...] = acc.astype(o_ref.dtype)

    return _pcall(body, name=name, grid=(R // tr,), in_specs=[pl.BlockSpec((n, tr, C), lambda i: (0, i, 0))],
                  out_specs=pl.BlockSpec((tr, C), lambda i: (i, 0)), out_shape=jax.ShapeDtypeStruct((R, C), out_dtype),
                  compiler_params=_params(("parallel",)))(a)


def _cast_into_slot(name, w, layer, chip, dtype):
    _, R, C = w.shape
    tr = _row_tile(R, C, 2 << 20)
    if tr % 16:
        tr = R

    def body(c_ref, w_ref, o_ref):
        o_ref[...] = w_ref[...].astype(o_ref.dtype)

    grid_spec = pltpu.PrefetchScalarGridSpec(
        num_scalar_prefetch=1, grid=(R // tr,),
        in_specs=[pl.BlockSpec((None, tr, C), lambda i, c_ref: (layer, i, 0))],
        out_specs=pl.BlockSpec((None, tr, C), lambda i, c_ref: (c_ref[0], i, 0)))
    return _pcall(body, name=name, grid_spec=grid_spec, out_shape=jax.ShapeDtypeStruct((N_CHIPS, R, C), dtype),
                  compiler_params=_params(("parallel",)))(chip, w)


def _cast_qkv(name, w, chip, tr=256):
    _, R, C = w.shape

    def body(c_ref, w_ref, own_ref, *piece_refs):
        v = w_ref[...].astype(BF16)
        own_ref[...] = v
        for p, ref in enumerate(piece_refs):
            ref[...] = v[:, p * QKV_TILE:(p + 1) * QKV_TILE]

    piece = pl.BlockSpec((None, tr, QKV_TILE), lambda i, c_ref: (c_ref[0], i, 0))
    return _pcall(body, prefetch=1, name=name, grid=(R // tr,),
                  in_specs=[pl.BlockSpec((None, tr, C), lambda i, c_ref: (0, i, 0))],
                  out_specs=[pl.BlockSpec((tr, C), lambda i, c_ref: (i, 0))] + [piece] * QKV_PIECES,
                  out_shape=[jax.ShapeDtypeStruct((R, C), BF16)]
                  + [jax.ShapeDtypeStruct((N_CHIPS, R, QKV_TILE), BF16)] * QKV_PIECES,
                  compiler_params=_params(("parallel",)))(chip, w)


def _owner_sum(name, mine, landed, chip, core, out=None, layer=None, col=None):
    _, half, C = mine.shape
    k, n_col = col if col is not None else (0, 1)
    tr = _row_tile(half, C)
    if tr % 16:
        tr = half
    nrt = half // tr
    has_out = out is not None

    def body(*refs):
        m_ref, l_ref, o_ref = refs[2], refs[3], refs[-1]
        acc = m_ref[...].astype(F32)
        for j in range(3):
            acc = acc + l_ref[j].astype(F32)
        o_ref[...] = acc

    if layer is None:
        out_shape = jax.ShapeDtypeStruct((2 * half, n_col * C), F32)
        o_spec = pl.BlockSpec((tr, C), lambda i, ch, co: (co[0] * nrt + i, k))
    else:
        out_shape = jax.ShapeDtypeStruct((2, 2 * half, C), F32)
        o_spec = pl.BlockSpec((None, tr, C), lambda i, ch, co: (layer, co[0] * nrt + i, 0))
    in_specs = [pl.BlockSpec((None, tr, C), lambda i, ch, co: (ch[0], i, 0)),
                pl.BlockSpec((3, tr, C), lambda i, ch, co: (0, i, 0))]
    operands = [chip, core, mine, landed]
    aliases = {}
    if has_out:
        in_specs.append(ANY)
        operands.append(out)
        aliases = {4: 0}
    grid_spec = pltpu.PrefetchScalarGridSpec(num_scalar_prefetch=2, grid=(nrt,), in_specs=in_specs, out_specs=o_spec)
    return _pcall(body, name=name, grid_spec=grid_spec, out_shape=out_shape, input_output_aliases=aliases,
                  compiler_params=_params(("parallel",)))(*operands)


def _add_my_half(name, p, q, core):
    n, R, C = p.shape
    half = R // 2

    def body(c_ref, p_ref, q_ref, o_ref):
        o_ref[...] = (p_ref[...].astype(F32) + q_ref[...].astype(F32)).astype(o_ref.dtype)

    grid_spec = pltpu.PrefetchScalarGridSpec(
        num_scalar_prefetch=1, grid=(n,),
        in_specs=[pl.BlockSpec((None, half, C), lambda s, c_ref: (s, c_ref[0], 0)),
                  pl.BlockSpec((None, half, C), lambda s, c_ref: (s, 0, 0))],
        out_specs=pl.BlockSpec((None, half, C), lambda s, c_ref: (s, 0, 0)))
    return _pcall(body, name=name, grid_spec=grid_spec, out_shape=jax.ShapeDtypeStruct((n, half, C), BF16),
                  compiler_params=_params(("parallel",)))(core, p, q)


ANY = pl.BlockSpec(memory_space=pl.ANY)


def _place():
    x, y, c = lax.axis_index("x"), lax.axis_index("y"), lax.axis_index("c")
    other_chips = [(1 - x, y), (x, 1 - y), (1 - x, 1 - y)]
    return x, y, c, other_chips


def _remote(src, dst, send, recv, k, to):
    return pltpu.make_async_remote_copy(src_ref=src, dst_ref=dst, send_sem=send.at[k], recv_sem=recv.at[k],
                                        device_id=to, device_id_type=MESH)


def _in_place(bufs):
    return dict(operands=bufs, out_shapes=[jax.ShapeDtypeStruct(b.shape, b.dtype) for b in bufs],
                aliases={t: t for t in range(len(bufs))})


def _gather_over_ici(bufs, split):
    n = len(bufs)

    def copies(ins, outs, send, recv, base=0):
        x, y, c, chips = _place()
        me = 2 * x + y
        out = []
        for t in range(n):
            piece = outs[t].at[me]
            if split[t]:
                half = outs[t].shape[1] // 2
                piece = outs[t].at[me, pl.ds(c * half, half)]
            for j, (px, py) in enumerate(chips):
                out.append(_remote(piece, piece, send, recv, base + 3 * t + j, (px, py, c)))
        return out

    return _Rider(n_copies=3 * n, copies=copies, **_in_place(bufs))


def _gather_to_sibling(bufs):
    n = len(bufs)

    def copies(ins, outs, send, recv, base=0):
        x, y, c, chips = _place()
        out = []
        for t in range(n):
            half = outs[t].shape[1] // 2
            for j, (px, py) in enumerate(chips):
                piece = outs[t].at[2 * px + py, pl.ds(c * half, half)]
                out.append(_remote(piece, piece, send, recv, base + 3 * t + j, (x, y, 1 - c)))
        return out

    return _Rider(n_copies=3 * n, copies=copies, **_in_place(bufs))


def _gather_and_pass_rider(buf):
    half = buf.shape[1] // 2

    def pieces(outs):
        x, y, c, chips = _place()
        rows = lambda core: pl.ds(core * half, half)
        mine = outs[0].at[2 * x + y, rows(c)]
        landed = [outs[0].at[2 * px + py, rows(c)] for px, py in chips]
        theirs = [outs[0].at[2 * px + py, rows(1 - c)] for px, py in chips]
        return (x, y, c, chips), mine, landed, theirs

    def start(ins, outs, send, recv, base=0):
        (x, y, c, chips), mine, _, _ = pieces(outs)
        for j, (px, py) in enumerate(chips):
            _remote(mine, mine, send, recv, base + j, (px, py, c)).start()

    def finish(ins, outs, send, recv, base=0):
        (x, y, c, chips), mine, landed, theirs = pieces(outs)
        sibling = (x, y, 1 - c)
        passed = []
        for j, (px, py) in enumerate(chips):
            _remote(landed[j], landed[j], send, recv, base + j, (px, py, c)).wait_recv()
            passed.append(_remote(landed[j], landed[j], send, recv, base + 3 + j, sibling))
            passed[-1].start()
        for j in range(3):
            _remote(theirs[j], theirs[j], send, recv, base + 3 + j, sibling).wait_recv()
        for j, (px, py) in enumerate(chips):
            _remote(mine, mine, send, recv, base + j, (px, py, c)).wait_send()
            passed[j].wait_send()

    return _Rider(n_copies=6, start=start, finish=finish, **_in_place([buf]))


def _swap_halves_rider(parts):
    n = len(parts)

    def copies(ins, outs, send, recv, base=0):
        x, y, c, _ = _place()
        out = []
        for t in range(n):
            half = ins[t].shape[1] // 2
            out.append(_remote(ins[t].at[:, pl.ds((1 - c) * half, half)], outs[t], send, recv, base + t,
                               (x, y, 1 - c)))
        return out

    shapes = [jax.ShapeDtypeStruct((p.shape[0], p.shape[1] // 2, p.shape[2]), p.dtype) for p in parts]
    return _Rider(parts, shapes, {}, n, copies)


def _scatter_rider(sums):
    n = len(sums)

    def copies(ins, outs, send, recv, base=0):
        x, y, c, chips = _place()
        out = []
        for t in range(n):
            for j, (px, py) in enumerate(chips):
                out.append(_remote(ins[t].at[2 * px + py], outs[t].at[j], send, recv, base + 3 * t + j, (px, py, c)))
        return out

    shapes = [jax.ShapeDtypeStruct((3,) + s.shape[1:], s.dtype) for s in sums]
    return _Rider(sums, shapes, {}, 3 * n, copies)


def _share_rider(blocks, pieces):
    def copies(ins, outs, send, recv, base=0):
        x, y, c, _ = _place()
        out = []
        for k, (t, l, col) in enumerate(pieces):
            ref = outs[t] if l is None else outs[t].at[l]
            r2 = ref.shape[0] // 2
            piece = ref.at[pl.ds(c * r2, r2)]
            if col is not None:
                width = ref.shape[1] // col[1]
                piece = ref.at[pl.ds(c * r2, r2), pl.ds(col[0] * width, width)]
            out.append(_remote(piece, piece, send, recv, base + k, (x, y, 1 - c)))
        return out

    return _Rider(n_copies=len(pieces), copies=copies, **_in_place(blocks))


def _gather_small(name, v):
    def body(v_ref, out_ref, send_sem, recv_sem, local_sem):
        x, y, c, _ = _place()
        me = 4 * x + 2 * y + c
        flips = [(fx, fy, fc) for fx in (0, 1) for fy in (0, 1) for fc in (0, 1)][1:]
        local = pltpu.make_async_copy(v_ref, out_ref.at[me], local_sem)
        local.start()
        copies = []
        for j, (fx, fy, fc) in enumerate(flips):
            peer = (x ^ fx, y ^ fy, c ^ fc)
            copies.append(pltpu.make_async_remote_copy(
                src_ref=v_ref, dst_ref=out_ref.at[me], send_sem=send_sem.at[j], recv_sem=recv_sem.at[j],
                device_id=peer, device_id_type=MESH))
        for cp in copies:
            cp.start()
        for j, (fx, fy, fc) in enumerate(flips):
            peer = (x ^ fx, y ^ fy, c ^ fc)
            pltpu.make_async_remote_copy(
                src_ref=v_ref, dst_ref=out_ref.at[4 * peer[0] + 2 * peer[1] + peer[2]], send_sem=send_sem.at[j],
                recv_sem=recv_sem.at[j], device_id=peer, device_id_type=MESH).wait_recv()
        for cp in copies:
            cp.wait_send()
        local.wait()

    return _pcall(body, name=name, in_specs=[ANY], out_specs=ANY,
                  out_shape=jax.ShapeDtypeStruct((8,) + v.shape, v.dtype),
                  scratch_shapes=[pltpu.SemaphoreType.DMA((7,)), pltpu.SemaphoreType.DMA((7,)),
                                  pltpu.SemaphoreType.DMA])(v)


def _fold(a, dil):
    if dil == 1:
        return a
    S, C = a.shape
    return a.reshape(S // dil, dil, C).transpose(1, 0, 2).reshape(S, C)


def _unfold(a, dil):
    if dil == 1:
        return a
    S, C = a.shape
    return a.reshape(dil, S // dil, C).transpose(1, 0, 2).reshape(S, C)


def _fold_groups(a):
    return jnp.stack([_fold(a[g] if a.ndim == 3 else a, d) for g, d in enumerate(ATTN_DILATIONS)])


def _unfold_groups(a):
    return jnp.stack([_unfold(a[g], d) for g, d in enumerate(ATTN_DILATIONS)])


class _NoComm:
    def __init__(self, weights):
        self.weights = weights
        self.grads = {}
        self.chip = jnp.zeros((1,), jnp.int32)

    def w(self, name):
        return self.weights[name]

    def run(self, fn, name, *args, **kw):
        return fn(name, *args, **kw)

    def grad(self, name, value):
        self.grads[name] = value


def _local_step(xs, tgt, attn_norm, ffn_norm, final_norm, comm):
    run = comm.run
    hf = None
    for g in range(N_GROUPS):
        hf = run(_norm_fold, "fold%d" % g, g, xs, attn_norm, hf)
    qkv = run(_qkv_tiles, "qkv_own", None, hf, comm.w("wq_own"), None, comm.chip)
    for p in range(QKV_PIECES):
        qkv = run(_qkv_tiles, "qkv_piece%d" % p, p, hf, comm.w("wq%d" % p), qkv, comm.chip)
    o_f, lse_f = run(_attn_fwd, "attn_fwd", qkv)
    lse_t = _unfold_groups(lse_f)
    merged = run(_merge_fwd, "merge_fwd", o_f, lse_t)
    x1, h1 = run(_proj_res_norm, "attn_out", merged, comm.w("o"), xs, ffn_norm[0:1])
    gu0, act0 = run(_ffn_up, "ffn0_up", h1, comm.w("gu0"))
    x2, h2 = run(_proj_res_norm, "ffn0_down", act0, comm.w("d0"), x1, comm.w("pool_norm"))
    u = run(_mm_rows, "pool_in", h2, comm.w("p"), F32, 512)
    y, z, x3 = run(_pool_fwd, "pool_fwd", u, comm.w("pg"), comm.w("pool_scale"), x2)
    h3 = run(_rms_fwd, "norm_ffn1", x3, ffn_norm[1:2])
    gu1, act1 = run(_ffn_up, "ffn1_up", h3, comm.w("gu1"))
    loss, dx4, d_final = run(_proj_res_loss, "ffn1_down", act1, comm.w("d1"), x3, final_norm, tgt)

    dgu1 = run(_ffn_down_bwd, "ffn1_down_bwd", dx4, comm.w("d1"), gu1)
    comm.grad("d1", run(_mm_tn, "ffn1_down_dw", act1, dx4, FF_SHARD, 2048))
    comm.grad("gu1", run(_ffn_dw_up, "ffn1_up_dw", h3, dgu1))
    dx3, d_ffn1 = run(_ffn_dh, "ffn1_up_dh", dgu1, comm.w("gu1"), x3, ffn_norm[1:2], dx4)

    du, dw_pg, d_scale = run(_pool_bwd, "pool_bwd", dx3, z, y, comm.w("pool_scale"), comm.w("pg"))
    comm.grad("pg", dw_pg)
    comm.grad("p", run(_mm_tn, "pool_in_dw", h2, du, D_MODEL, h2.shape[0]))
    dx2, d_pool = run(_dh_norm_bwd, "pool_in_dh", du, comm.w("p"), x2, comm.w("pool_norm"), dx3)

    dgu0 = run(_ffn_down_bwd, "ffn0_down_bwd", dx2, comm.w("d0"), gu0)
    comm.grad("d0", run(_mm_tn, "ffn0_down_dw", act0, dx2, FF_SHARD, 2048))
    comm.grad("gu0", run(_ffn_dw_up, "ffn0_up_dw", h1, dgu0))
    dx1, d_ffn0 = run(_ffn_dh, "ffn0_up_dh", dgu0, comm.w("gu0"), x1, ffn_norm[0:1], dx2)

    d_merged = run(_mm_nt_rows, "attn_out_dh", dx1, comm.w("o"), 512)
    comm.grad("o", run(_mm_tn, "attn_out_dw", merged, dx1, D_MODEL, 2048))
    do_f, dl_t = run(_merge_bwd, "merge_bwd", d_merged, o_f, lse_t)
    dqkv = run(_attn_bwd, "attn_bwd", qkv, do_f, lse_f, _fold_groups(dl_t))
    for p in range(QKV_PIECES):
        comm.grad("qkv%d" % p, run(_qkv_dw, "qkv_dw%d" % p, p, hf, dqkv))
    dhf = None
    for g in range(N_GROUPS):
        dhf = run(_qkv_dh, "qkv_dh%d" % g, g, dqkv, [comm.w("wq%d" % p) for p in range(QKV_PIECES)], dhf)
    grad_x, d_attn = run(_rms_bwd_folded, "norm_attn_bwd", xs, attn_norm, dhf, dx1)

    small = dict(attn=d_attn, ffn0=d_ffn0, ffn1=d_ffn1, final=d_final, pool=d_pool, scale=d_scale)
    return loss, grad_x, small


TENSORS = ("qkv", "o", "p", "pg", "gu0", "gu1", "d0", "d1")


def _block_of(name):
    if name[:-1] in ("gu", "d"):
        return name[:-1], int(name[-1]), None
    if name[:-1] == "qkv":
        return "qkv", None, (int(name[-1]), QKV_PIECES)
    return name, None, None


class _MeshComm:
    def __init__(self, bufs, chip_arr, core_arr, pg_rows):
        self.bufs = dict(bufs)
        self.chip, self.chip_arr, self.core_arr, self.pg_rows = chip_arr, chip_arr, core_arr, pg_rows
        self.parts, self.sums, self.blocks = {}, {}, {}
        ici = lambda *names: lambda: self.gather(names, True)
        d2d = lambda *names: lambda: self.gather(names, False)
        whole = lambda name: lambda: self.gather_and_pass(name)
        swap = lambda *names: lambda: self.swap(names)
        scatter = lambda *names: lambda: self.scatter(names)
        share = lambda *names: lambda: self.share(names)
        self.plan = {
            "qkv_own": [whole("wq0")],
            "qkv_piece0": [whole("wq1")],
            "qkv_piece1": [whole("wq2")],
            "qkv_piece2": [ici("d0")],
            "attn_fwd": [ici("gu0", "o"), d2d("d0")],
            "merge_fwd": [d2d("gu0", "o")],
            "attn_out": [ici("p", "pg", "pool_norm", "pool_scale")],
            "ffn0_up": [ici("gu1"), d2d("p", "pg")],
            "ffn0_down": [d2d("gu1"), ici("d1")],
            "pool_in": [d2d("d1")],
            "ffn1_up_dh": [swap("d1", "gu1")],
            "ffn0_down_bwd": [scatter("gu1")],
            "ffn0_down_dw": [scatter("d1")],
            "ffn0_up_dh": [swap("pg", "p", "d0", "gu0"), share("gu1", "d1")],
            "merge_bwd": [swap("o")],
            "attn_bwd": [scatter("pg", "p", "d0", "gu0", "o")],
            "qkv_dw0": [share("pg", "p", "d0", "gu0", "o")],
            "qkv_dw1": [swap("qkv0")],
            "qkv_dw2": [scatter("qkv0"), swap("qkv1")],
            "qkv_dh0": [share("qkv0"), scatter("qkv1"), swap("qkv2")],
            "qkv_dh2": [share("qkv1"), scatter("qkv2")],
        }

    def gather_and_pass(self, name):
        return _gather_and_pass_rider(self.bufs[name]), lambda res: self.bufs.update({name: res[0]})

    def gather(self, names, over_ici):
        bufs = [self.bufs[n] for n in names]
        rider = _gather_over_ici(bufs, [n in TENSORS for n in names]) if over_ici else _gather_to_sibling(bufs)
        return rider, lambda res: self.bufs.update(zip(names, res))

    def swap(self, names):
        def done(res):
            for n, q in zip(names, res):
                self.sums[n] = _add_my_half("chip_sum_" + n, self.parts[n], q, self.core_arr)
        return _swap_halves_rider([self.parts[n] for n in names]), done

    def scatter(self, names):
        def done(res):
            for n, landed in zip(names, res):
                key, layer, col = _block_of(n)
                self.blocks[key] = _owner_sum("owner_sum_" + n, self.sums[n], landed, self.chip_arr, self.core_arr,
                                              out=self.blocks.get(key), layer=layer, col=col)
        return _scatter_rider([self.sums[n] for n in names]), done

    def share(self, names):
        keys, pieces = [], []
        for n in names:
            key, layer, col = _block_of(n)
            if key not in keys:
                keys.append(key)
            pieces.append((keys.index(key), layer, col))
        return _share_rider([self.blocks[k] for k in keys], pieces), lambda res: self.blocks.update(zip(keys, res))

    def alone(self, name, made):
        rider, done = made
        done(_comm_only(name, rider))

    def run(self, fn, name, *args, **kw):
        made = [make() for make in self.plan.get(name, [])]
        if not made:
            return fn(name, *args, **kw)
        riders = [r for r, _ in made]
        out = _ride(riders[0] if len(riders) == 1 else _riders(*riders), fn, name, *args, **kw)
        for r, done in made:
            done(r.results)
        return out

    def w(self, name):
        b = self.bufs[name]
        if name in ("o", "p", "d0", "d1"):
            return b.reshape(-1, b.shape[-1])
        if name == "pg":
            b = b.reshape(N_CHIPS, 4, self.pg_rows, POOL_DIM).transpose(1, 0, 2, 3)
            return b.reshape(4, POOL_DIM, POOL_DIM)
        if name in ("pool_norm", "pool_scale"):
            return b.reshape(1, -1)
        return b

    def grad(self, name, value):
        if name == "pg":
            value = value.reshape(4, N_CHIPS, self.pg_rows, POOL_DIM).transpose(1, 0, 2, 3)
            value = value.reshape(N_CHIPS, 4 * self.pg_rows, POOL_DIM).astype(BF16)
        elif name in ("o", "p", "d0", "d1"):
            value = value.reshape(N_CHIPS, value.shape[0] // N_CHIPS, value.shape[1])
        self.parts[name] = value


def kernel(x, attn_norm, w_qkv, w_attn_out, pool_norm, w_pool_in, w_pool_group, pool_scale, ffn_norm, w_ffn_gate_up, w_ffn_down, final_norm, loss_target, m_attn_norm, m_w_qkv, m_w_attn_out, m_pool_norm, m_w_pool_in, m_w_pool_group, m_pool_scale, m_ffn_norm, m_w_ffn_gate_up, m_w_ffn_down, m_final_norm, v_attn_norm, v_w_qkv, v_w_attn_out, v_pool_norm, v_w_pool_in, v_w_pool_group, v_pool_scale, v_ffn_norm, v_w_ffn_gate_up, v_w_ffn_down, v_final_norm):
    D = D_MODEL
    chip = 2 * lax.axis_index("x") + lax.axis_index("y")
    core = lax.axis_index("c")
    pg_rows = w_pool_group.shape[2]

    chip_arr = chip.astype(jnp.int32).reshape(1)
    core_arr = core.astype(jnp.int32).reshape(1)

    pieces = _cast_qkv("cast_qkv", w_qkv, chip_arr)
    bufs = dict(zip(["wq_own"] + ["wq%d" % p for p in range(QKV_PIECES)], pieces))
    shards = [(w_attn_out, 0), (w_pool_in, 0), (w_pool_group.reshape(1, 4 * pg_rows, POOL_DIM), 0),
              (w_ffn_gate_up, 0), (w_ffn_gate_up, 1), (w_ffn_down, 0), (w_ffn_down, 1)]
    for nm, (s, l) in zip(TENSORS[1:], shards):
        bufs[nm] = _cast_into_slot("cast_" + nm, s, l, chip_arr, BF16)
    bufs["pool_norm"] = _cast_into_slot("place_pool_norm", pool_norm[None], 0, chip_arr, F32)
    bufs["pool_scale"] = _cast_into_slot("place_pool_scale", pool_scale[None], 0, chip_arr, F32)

    comm = _MeshComm(bufs, chip_arr, core_arr, pg_rows)
    loss_part, grad_x, small = _local_step(x[0], loss_target[0], attn_norm, ffn_norm, final_norm.reshape(1, D), comm)
    comm.alone("halves_to_sibling", comm.share(["qkv%d" % (QKV_PIECES - 1)]))
    g_w_qkv, g_w_o, g_w_p, g_w_pg, g_w_gu, g_w_d = [comm.blocks[k] for k in ("qkv", "o", "p", "pg", "gu", "d")]

    rows = jnp.concatenate([small["attn"], small["ffn0"], small["ffn1"], small["final"], small["pool"],
                            small["scale"], jnp.zeros((2, D), F32)], axis=0)
    all_rows = _gather_small("gather_gain_grads", rows)
    tot = _sum_leading("sum_gain_grads", all_rows, F32)
    g_attn_norm = tot[0:1]
    g_ffn_norm = tot[1:3]
    g_final_norm = tot[3]
    g_pool_norm = lax.dynamic_slice(tot, (4, chip * POOL_DIM), (1, POOL_DIM))
    g_pool_scale = lax.dynamic_slice(tot, (5, chip * POOL_DIM), (1, POOL_DIM))

    loss = lax.psum(loss_part[0, 0], ("x", "y", "c"))

    def update(name, w, g, m, v):
        shape = w.shape
        cols = shape[-1]
        as2d = lambda a: a.reshape(-1, cols)
        return [a.reshape(shape) for a in _adamw("adamw_" + name, as2d(w), as2d(g), as2d(m), as2d(v))]

    results = [
        update("attn_norm", attn_norm, g_attn_norm, m_attn_norm, v_attn_norm),
        update("w_qkv", w_qkv, g_w_qkv, m_w_qkv, v_w_qkv),
        update("w_attn_out", w_attn_out, g_w_o, m_w_attn_out, v_w_attn_out),
        update("pool_norm", pool_norm, g_pool_norm, m_pool_norm, v_pool_norm),
        update("w_pool_in", w_pool_in, g_w_p, m_w_pool_in, v_w_pool_in),
        update("w_pool_group", w_pool_group, g_w_pg, m_w_pool_group, v_w_pool_group),
        update("pool_scale", pool_scale, g_pool_scale, m_pool_scale, v_pool_scale),
        update("ffn_norm", ffn_norm, g_ffn_norm, m_ffn_norm, v_ffn_norm),
        update("w_ffn_gate_up", w_ffn_gate_up, g_w_gu, m_w_ffn_gate_up, v_w_ffn_gate_up),
        update("w_ffn_down", w_ffn_down, g_w_d, m_w_ffn_down, v_w_ffn_down),
        update("final_norm", final_norm, g_final_norm, m_final_norm, v_final_norm),
    ]
    grads = [r[0] for r in results]
    deltas = [r[1] for r in results]
    new_m = [r[2] for r in results]
    new_v = [r[3] for r in results]
    return (loss, grad_x[None], *grads, *deltas, *new_m, *new_v)
```

```python
import functools

import jax
import jax.numpy as jnp
from jax import lax
from jax.experimental import pallas as pl
from jax.experimental.pallas import tpu as pltpu

F32 = jnp.float32
BF16 = jnp.bfloat16
MESH = pl.DeviceIdType.MESH

D_MODEL = 1024
N_HEADS = 8
HEAD_DIM = 128
Q_BLOCK = 128
ATTN_DILATIONS = (1, 4, 16)
N_GROUPS = 3
D_FF = 2816
N_CHIPS = 4
FF_SHARD = 2 * D_FF // N_CHIPS
QKV_SHARD = 3 * 3 * D_MODEL // N_CHIPS
QKV_TILE = 768
POOL_WINDOWS = (2, 4, 8, 16)
POOL_DIM = 256
POOL_HALO = 16
RMS_EPS = 1e-6
NEG = -1e30
ADAM_LR = 0.001
ADAM_B1 = 0.9
ADAM_B2 = 0.999
ADAM_EPS = 1e-08
ADAM_WD = 0.01
ADAM_STEP = 10
VMEM_LIMIT = 56 * 1024 * 1024

_DN = {
    "nn": (((1,), (0,)), ((), ())),
    "nt": (((1,), (1,)), ((), ())),
    "tn": (((0,), (0,)), ((), ())),
}


class _Rider:
    def __init__(self, operands, out_shapes, aliases, n_copies, copies=None, start=None, finish=None):
        self.operands = list(operands)
        self.out_shapes = list(out_shapes)
        self.aliases = dict(aliases)
        self.n_copies = n_copies
        self.results = None

        def start_copies(ins, outs, send, recv, base=0):
            for cp in copies(ins, outs, send, recv, base):
                cp.start()

        def wait_copies(ins, outs, send, recv, base=0):
            for cp in copies(ins, outs, send, recv, base):
                cp.wait()

        self.start = start or start_copies
        self.finish = finish or wait_copies


def _riders(*riders):
    operands, out_shapes, aliases, offsets = [], [], {}, []
    n = 0
    for r in riders:
        offsets.append((len(operands), len(out_shapes), n))
        aliases.update({len(operands) + i: len(out_shapes) + o for i, o in r.aliases.items()})
        operands += r.operands
        out_shapes += r.out_shapes
        n += r.n_copies

    def each(which):
        def go(ins, outs, send, recv, base=0):
            for r, (i0, o0, s0) in zip(riders, offsets):
                getattr(r, which)(ins[i0:i0 + len(r.operands)], outs[o0:o0 + len(r.out_shapes)], send, recv,
                                  base + s0)
        return go

    both = _Rider(operands, out_shapes, aliases, n, start=each("start"), finish=each("finish"))
    both.parts = (riders, offsets)
    return both


_pending_rider = []


def _ride(rider, fn, *args, **kw):
    _pending_rider.append(rider)
    out = fn(*args, **kw)
    assert not _pending_rider
    if hasattr(rider, "parts"):
        for r, (_, o0, _) in zip(*rider.parts):
            r.results = rider.results[o0:o0 + len(r.out_shapes)]
    return out


def _pcall(body, prefetch=0, **kw):
    def build(body, kw):
        if not prefetch:
            return pl.pallas_call(body, **kw)
        kw = dict(kw)
        grid_spec = pltpu.PrefetchScalarGridSpec(
            num_scalar_prefetch=prefetch, grid=kw.pop("grid"), in_specs=kw.pop("in_specs"),
            out_specs=kw.pop("out_specs"), scratch_shapes=kw.pop("scratch_shapes", []))
        return pl.pallas_call(body, grid_spec=grid_spec, **kw)

    if not _pending_rider:
        return build(body, kw)
    rider = _pending_rider.pop()
    grid = kw.get("grid", ())
    in_specs = list(kw.get("in_specs", []))
    single = not isinstance(kw["out_shape"], (list, tuple))
    out_shape = [kw["out_shape"]] if single else list(kw["out_shape"])
    out_specs = [kw["out_specs"]] if single else list(kw["out_specs"])
    scratch = list(kw.get("scratch_shapes", []))
    n_in, n_out, n_scr = len(in_specs), len(out_shape), len(scratch)
    r_in, r_out = len(rider.operands), len(rider.out_shapes)

    def wrapped(*refs):
        scalars, refs = refs[:prefetch], refs[prefetch:]
        ins, rins = refs[:n_in], refs[n_in:n_in + r_in]
        outs = refs[n_in + r_in:n_in + r_in + n_out]
        routs = refs[n_in + r_in + n_out:n_in + r_in + n_out + r_out]
        scr = refs[n_in + r_in + n_out + r_out:n_in + r_in + n_out + r_out + n_scr]
        send, recv = refs[-2:]
        ids = [pl.program_id(a) for a in range(len(grid))]
        first, last = True, True
        for a, pid in enumerate(ids):
            first = jnp.logical_and(first, pid == 0)
            last = jnp.logical_and(last, pid == grid[a] - 1)
        if grid:
            pl.when(first)(lambda: rider.start(rins, routs, send, recv))
        else:
            rider.start(rins, routs, send, recv)
        body(*scalars, *ins, *outs, *scr)
        if grid:
            pl.when(last)(lambda: rider.finish(rins, routs, send, recv))
        else:
            rider.finish(rins, routs, send, recv)

    any_spec = pl.BlockSpec(memory_space=pl.ANY)
    kw2 = dict(kw)
    kw2.update(
        in_specs=in_specs + [any_spec] * r_in, out_specs=out_specs + [any_spec] * r_out,
        out_shape=out_shape + rider.out_shapes,
        scratch_shapes=scratch + [pltpu.SemaphoreType.DMA((rider.n_copies,)),
                                  pltpu.SemaphoreType.DMA((rider.n_copies,))],
        input_output_aliases={**kw.get("input_output_aliases", {}),
                              **{prefetch + n_in + i: n_out + o for i, o in rider.aliases.items()}})
    if grid:
        kw2["compiler_params"] = _params(("arbitrary",) * len(grid))
    call = build(wrapped, kw2)

    def run(*operands):
        res = call(*operands, *rider.operands)
        rider.results = list(res[n_out:])
        return res[0] if single else list(res[:n_out])

    return run


def _params(sem):
    return pltpu.CompilerParams(dimension_semantics=sem, vmem_limit_bytes=VMEM_LIMIT)


def _dot(a, b, mode):
    return lax.dot_general(a, b, _DN[mode], preferred_element_type=F32)


def _mm(name, mode, grid, a, a_spec, b, b_spec, out_shape, o_spec, acc_shape, res=None, res_spec=None):
    nk = grid[-1]
    has_res = res is not None

    def body(*refs):
        a_ref, b_ref = refs[0], refs[1]
        res_ref = refs[2] if has_res else None
        o_ref = refs[2 + has_res]
        part = _dot(a_ref[...].astype(BF16), b_ref[...].astype(BF16), mode)
        if nk == 1:
            if has_res:
                part = part + res_ref[...]
            o_ref[...] = part.astype(o_ref.dtype)
            return
        acc_ref = refs[3 + has_res]
        k = pl.program_id(len(grid) - 1)

        @pl.when(k == 0)
        def _():
            acc_ref[...] = part

        @pl.when(k > 0)
        def _():
            acc_ref[...] += part

        @pl.when(k == nk - 1)
        def _():
            r = acc_ref[...]
            if has_res:
                r = r + res_ref[...]
            o_ref[...] = r.astype(o_ref.dtype)

    in_specs = [a_spec, b_spec] + ([res_spec] if has_res else [])
    operands = [a, b] + ([res] if has_res else [])
    scratch = [] if nk == 1 else [pltpu.VMEM(acc_shape, F32)]
    sem = ("parallel",) * (len(grid) - 1) + ("arbitrary",)
    return _pcall(body, name=name, grid=grid, in_specs=in_specs, out_specs=o_spec, out_shape=out_shape,
                  scratch_shapes=scratch, compiler_params=_params(sem))(*operands)


def _mm_rows(name, a, b, out_dtype, tm, res=None):
    M, K = a.shape
    N = b.shape[1]
    return _mm(name, "nn", (M // tm, 1), a, pl.BlockSpec((tm, K), lambda i, k: (i, 0)),
               b, pl.BlockSpec((K, N), lambda i, k: (0, 0)),
               jax.ShapeDtypeStruct((M, N), out_dtype), pl.BlockSpec((tm, N), lambda i, k: (i, 0)), None,
               res=res, res_spec=pl.BlockSpec((tm, N), lambda i, k: (i, 0)))


def _mm_nt_rows(name, a, b, tm):
    M, N = a.shape
    K = b.shape[0]
    return _mm(name, "nt", (M // tm, 1), a, pl.BlockSpec((tm, N), lambda i, k: (i, 0)),
               b, pl.BlockSpec((K, N), lambda i, k: (0, 0)),
               jax.ShapeDtypeStruct((M, K), F32), pl.BlockSpec((tm, K), lambda i, k: (i, 0)), None)


def _mm_tn(name, a, b, tm, tk):
    T, M = a.shape
    N = b.shape[1]
    return _mm(name, "tn", (M // tm, T // tk), a, pl.BlockSpec((tk, tm), lambda i, k: (k, i)),
               b, pl.BlockSpec((tk, N), lambda i, k: (k, 0)),
               jax.ShapeDtypeStruct((M, N), BF16), pl.BlockSpec((tm, N), lambda i, k: (i, 0)), (tm, N))


def _rms_fwd(name, x, g, tr=512):
    S, D = x.shape

    def body(x_ref, g_ref, o_ref):
        xf = x_ref[...]
        r = lax.rsqrt(jnp.mean(xf * xf, axis=-1, keepdims=True) + RMS_EPS)
        o_ref[...] = ((xf * r) * g_ref[...]).astype(o_ref.dtype)

    return _pcall(body, name=name, grid=(S // tr,),
                  in_specs=[pl.BlockSpec((tr, D), lambda i: (i, 0)), pl.BlockSpec((1, D), lambda i: (0, 0))],
                  out_specs=pl.BlockSpec((tr, D), lambda i: (i, 0)),
                  out_shape=jax.ShapeDtypeStruct((S, D), BF16), compiler_params=_params(("parallel",)))(x, g)


def _norm_bwd_rows(xf, gain, dh, dres):
    r = lax.rsqrt(jnp.mean(xf * xf, axis=-1, keepdims=True) + RMS_EPS)
    xh = xf * r
    t = dh * gain
    dx = dres + r * (t - xh * jnp.mean(t * xh, axis=-1, keepdims=True))
    return dx, jnp.sum(dh * xh, axis=0, keepdims=True)


def _accumulate(ref, value, first):
    @pl.when(first)
    def _():
        ref[...] = value

    @pl.when(jnp.logical_not(first))
    def _():
        ref[...] += value


def _rms_bwd_folded(name, x, g, dhf, dres, tb=512):
    S, D = x.shape

    def body(x_ref, g_ref, d0_ref, d1_ref, d2_ref, dres_ref, dx_ref, dg_ref, dh_ref):
        chunks = _lane_chunks(D)
        for c, cols in enumerate(chunks):
            dh_ref[c] = d0_ref[0, :, cols]
        for dil, ref in ((ATTN_DILATIONS[1], d1_ref), (ATTN_DILATIONS[2], d2_ref)):
            n = tb // dil
            for k in range(dil):
                for c, cols in enumerate(chunks):
                    dh_ref[c, _strided_rows(k, n, dil), :] += ref[k, :, cols]
        dh = jnp.concatenate([dh_ref[c] for c in range(len(chunks))], axis=1)
        dx, dg = _norm_bwd_rows(x_ref[...], g_ref[...], dh, dres_ref[...])
        dx_ref[...] = dx
        _accumulate(dg_ref, dg, pl.program_id(0) == 0)

    views, specs = _folded_views(dhf, tb)
    row = pl.BlockSpec((tb, D), lambda i: (i, 0))
    vec = pl.BlockSpec((1, D), lambda i: (0, 0))
    return _pcall(body, name=name, grid=(S // tb,), in_specs=[row, vec] + specs + [row], out_specs=[row, vec],
                  out_shape=[jax.ShapeDtypeStruct((S, D), F32), jax.ShapeDtypeStruct((1, D), F32)],
                  scratch_shapes=[pltpu.VMEM((D // LANES, tb, LANES), F32)],
                  compiler_params=_params(("arbitrary",)))(x, g, *views, dres)


def _proj_res_norm(name, a, w, res, gain, tm=512):
    M, K = a.shape
    D = w.shape[1]

    def body(a_ref, w_ref, res_ref, g_ref, x_ref, h_ref):
        xf = res_ref[...] + _dot(a_ref[...], w_ref[...], "nn")
        x_ref[...] = xf
        r = lax.rsqrt(jnp.mean(xf * xf, axis=-1, keepdims=True) + RMS_EPS)
        h_ref[...] = ((xf * r) * g_ref[...]).astype(h_ref.dtype)

    row = pl.BlockSpec((tm, D), lambda i: (i, 0))
    return _pcall(body, name=name, grid=(M // tm,),
                  in_specs=[pl.BlockSpec((tm, K), lambda i: (i, 0)), pl.BlockSpec((K, D), lambda i: (0, 0)), row,
                            pl.BlockSpec((1, D), lambda i: (0, 0))],
                  out_specs=[row, row],
                  out_shape=[jax.ShapeDtypeStruct((M, D), F32), jax.ShapeDtypeStruct((M, D), BF16)],
                  compiler_params=_params(("parallel",)))(a, w, res, gain)


def _proj_res_loss(name, a, w, res, gain, tgt, tm=512):
    M, K = a.shape
    D = w.shape[1]

    def body(a_ref, w_ref, res_ref, g_ref, t_ref, loss_ref, dx_ref, dg_ref):
        first = pl.program_id(0) == 0
        xf = res_ref[...] + _dot(a_ref[...], w_ref[...], "nn")
        r = lax.rsqrt(jnp.mean(xf * xf, axis=-1, keepdims=True) + RMS_EPS)
        xh = xf * r
        gv = g_ref[...]
        e = xh * gv - t_ref[...]
        lp = 0.5 * jnp.sum(jnp.mean(e * e, axis=-1, keepdims=True), axis=0, keepdims=True)
        dy = e * (1.0 / D)
        t = dy * gv
        dx_ref[...] = r * (t - xh * jnp.mean(t * xh, axis=-1, keepdims=True))
        _accumulate(dg_ref, jnp.sum(dy * xh, axis=0, keepdims=True), first)
        _accumulate(loss_ref, lp, first)

    row = pl.BlockSpec((tm, D), lambda i: (i, 0))
    vec = pl.BlockSpec((1, D), lambda i: (0, 0))
    return _pcall(body, name=name, grid=(M // tm,),
                  in_specs=[pl.BlockSpec((tm, K), lambda i: (i, 0)), pl.BlockSpec((K, D), lambda i: (0, 0)), row,
                            vec, row],
                  out_specs=[pl.BlockSpec((1, 1), lambda i: (0, 0)), row, vec],
                  out_shape=[jax.ShapeDtypeStruct((1, 1), F32), jax.ShapeDtypeStruct((M, D), F32),
                             jax.ShapeDtypeStruct((1, D), F32)],
                  compiler_params=_params(("arbitrary",)))(a, w, res, gain, tgt)


def _dh_norm_bwd(name, d, w, x, gain, dres, tm=512):
    M, N = d.shape
    D = w.shape[0]

    def body(d_ref, w_ref, x_ref, g_ref, dres_ref, dx_ref, dg_ref):
        dh = _dot(d_ref[...].astype(BF16), w_ref[...], "nt")
        dx, dg = _norm_bwd_rows(x_ref[...], g_ref[...], dh, dres_ref[...])
        dx_ref[...] = dx
        _accumulate(dg_ref, dg, pl.program_id(0) == 0)

    row = pl.BlockSpec((tm, D), lambda i: (i, 0))
    vec = pl.BlockSpec((1, D), lambda i: (0, 0))
    return _pcall(body, name=name, grid=(M // tm,),
                  in_specs=[pl.BlockSpec((tm, N), lambda i: (i, 0)), pl.BlockSpec((D, N), lambda i: (0, 0)), row, vec,
                            row],
                  out_specs=[row, vec],
                  out_shape=[jax.ShapeDtypeStruct((M, D), F32), jax.ShapeDtypeStruct((1, D), F32)],
                  compiler_params=_params(("arbitrary",)))(d, w, x, gain, dres)


def _sigmoid(v):
    return 0.5 * jnp.tanh(0.5 * v) + 0.5


def _ffn_up(name, h, w_gu, tm=512):
    S, D = h.shape
    W = FF_SHARD

    def body(h_ref, wg_ref, wu_ref, gu_ref, act_ref):
        hv = h_ref[...]
        gate = _dot(hv, wg_ref[...], "nn")
        up = _dot(hv, wu_ref[...], "nn")
        gu_ref[0] = gate.astype(gu_ref.dtype)
        gu_ref[1] = up.astype(gu_ref.dtype)
        sig = _sigmoid(gate)
        act_ref[...] = ((gate * sig) * up).astype(act_ref.dtype)

    return _pcall(
        body, name=name, grid=(2, S // tm),
        in_specs=[pl.BlockSpec((tm, D), lambda j, i: (i, 0)),
                  pl.BlockSpec((None, D, W), lambda j, i: (j, 0, 0)),
                  pl.BlockSpec((None, D, W), lambda j, i: (j + 2, 0, 0))],
        out_specs=[pl.BlockSpec((2, tm, W), lambda j, i: (0, i, j)), pl.BlockSpec((tm, W), lambda j, i: (i, j))],
        out_shape=[jax.ShapeDtypeStruct((2, S, D_FF), BF16), jax.ShapeDtypeStruct((S, D_FF), BF16)],
        compiler_params=_params(("parallel", "parallel")))(h, w_gu, w_gu)


def _ffn_down_bwd(name, dx, w_down, gu, tm=512):
    S, D = dx.shape
    W = FF_SHARD

    def body(dx_ref, w_ref, gu_ref, dgu_ref):
        dact = _dot(dx_ref[...].astype(BF16), w_ref[...], "nt")
        gate = gu_ref[0].astype(F32)
        up = gu_ref[1].astype(F32)
        sig = _sigmoid(gate)
        silu = gate * sig
        dgu_ref[0] = (dact * up * (sig * (1.0 + gate * (1.0 - sig)))).astype(dgu_ref.dtype)
        dgu_ref[1] = (dact * silu).astype(dgu_ref.dtype)

    blk = pl.BlockSpec((2, tm, W), lambda j, i: (0, i, j))
    return _pcall(
        body, name=name, grid=(2, S // tm),
        in_specs=[pl.BlockSpec((tm, D), lambda j, i: (i, 0)), pl.BlockSpec((W, D), lambda j, i: (j, 0)), blk],
        out_specs=blk, out_shape=jax.ShapeDtypeStruct((2, S, D_FF), BF16),
        compiler_params=_params(("parallel", "parallel")))(dx, w_down, gu)


def _ffn_dw_up(name, h, dgu, tk=2048):
    S, D = h.shape
    W = FF_SHARD
    tk = min(tk, S)
    return _mm(name, "tn", (N_CHIPS, S // tk), h, pl.BlockSpec((tk, D), lambda s, k: (k, 0)),
               dgu, pl.BlockSpec((None, tk, W), lambda s, k: (s // 2, k, s % 2)),
               jax.ShapeDtypeStruct((N_CHIPS, D, W), BF16), pl.BlockSpec((None, D, W), lambda s, k: (s, 0, 0)),
               (D, W))


def _ffn_dh(name, dgu, w_gu, x, gain, dres, tm=512):
    S = dgu.shape[1]
    W = FF_SHARD

    def body(a_ref, w_hbm, x_ref, g_ref, dres_ref, dx_ref, dg_ref, w_ref, acat_ref, sems):
        first = pl.program_id(0) == 0

        @pl.when(first)
        def _():
            copies = [pltpu.make_async_copy(w_hbm.at[s], w_ref.at[:, pl.ds(s * W, W)], sems.at[s])
                      for s in range(N_CHIPS)]
            for cp in copies:
                cp.start()
            for cp in copies:
                cp.wait()

        acat_ref[:, :D_FF] = a_ref[0]
        acat_ref[:, D_FF:] = a_ref[1]
        dh = _dot(acat_ref[...], w_ref[...], "nt")
        dx, dg = _norm_bwd_rows(x_ref[...], g_ref[...], dh, dres_ref[...])
        dx_ref[...] = dx
        _accumulate(dg_ref, dg, first)

    row = pl.BlockSpec((tm, D_MODEL), lambda i: (i, 0))
    vec = pl.BlockSpec((1, D_MODEL), lambda i: (0, 0))
    return _pcall(body, name=name, grid=(S // tm,),
                  in_specs=[pl.BlockSpec((2, tm, D_FF), lambda i: (0, i, 0)), pl.BlockSpec(memory_space=pl.ANY),
                            row, vec, row],
                  out_specs=[row, vec],
                  out_shape=[jax.ShapeDtypeStruct((S, D_MODEL), F32), jax.ShapeDtypeStruct((1, D_MODEL), F32)],
                  scratch_shapes=[pltpu.VMEM((D_MODEL, 2 * D_FF), BF16), pltpu.VMEM((tm, 2 * D_FF), BF16),
                                  pltpu.SemaphoreType.DMA((N_CHIPS,))],
                  compiler_params=_params(("arbitrary",)))(dgu, w_gu, x, gain, dres)


FOLD_TOKENS = 1024
LANES = 128


def _lane_chunks(width):
    return [slice(c * LANES, (c + 1) * LANES) for c in range(width // LANES)]


def _strided_rows(r, n, dil):
    return pl.ds(r, n) if dil == 1 else pl.ds(r, n, stride=dil)


def _norm_fold(name, g, x, gain, hf):
    S, D = x.shape
    dil = ATTN_DILATIONS[g]
    n = FOLD_TOKENS // dil
    seg = S // dil

    def body(*refs):
        x_ref, g_ref = refs[:2]
        hf_ref, xc_ref = refs[-2:]
        xf = x_ref[...]
        inv = lax.rsqrt(jnp.mean(xf * xf, axis=-1, keepdims=True) + RMS_EPS)
        h = (xf * inv) * g_ref[...]
        if dil == 1:
            hf_ref[0] = h.astype(BF16)
            return
        for c, cols in enumerate(_lane_chunks(D)):
            xc_ref[c] = h[:, cols]
        for r in range(dil):
            for c, cols in enumerate(_lane_chunks(D)):
                hf_ref[r, :, cols] = xc_ref[c, _strided_rows(r, n, dil), :].astype(BF16)

    in_specs = [pl.BlockSpec((FOLD_TOKENS, D), lambda i: (i, 0)), pl.BlockSpec((1, D), lambda i: (0, 0))]
    operands = [x, gain]
    aliases = {}
    if hf is not None:
        in_specs.append(pl.BlockSpec(memory_space=pl.ANY))
        operands.append(hf.reshape(N_GROUPS * dil, seg, D))
        aliases = {2: 0}
    hf = _pcall(body, name=name, grid=(S // FOLD_TOKENS,), in_specs=in_specs,
                out_specs=pl.BlockSpec((dil, n, D), lambda i: (g, i, 0)),
                out_shape=jax.ShapeDtypeStruct((N_GROUPS * dil, seg, D), BF16),
                scratch_shapes=[pltpu.VMEM((D // LANES, FOLD_TOKENS, LANES), F32)], input_output_aliases=aliases,
                compiler_params=_params(("arbitrary",)))(*operands)
    return hf.reshape(N_GROUPS, S, D)


def _qkv_tiles(name, piece, hf, w, qkv, chip, tm=1024):
    S = hf.shape[1]
    per_group = 3 * D_MODEL // QKV_TILE

    def tile(q, c_ref):
        if piece is None:
            return QKV_PIECES * c_ref[0] + q
        return QKV_PIECES * ((c_ref[0] + 1 + q) % N_CHIPS) + piece

    def body(*refs):
        a_ref, w_ref, o_ref = refs[1], refs[2], refs[-1]
        o_ref[...] = _dot(a_ref[...], w_ref[...], "nn").astype(o_ref.dtype)

    if piece is None:
        w_spec = pl.BlockSpec((D_MODEL, QKV_TILE), lambda q, i, c_ref: (0, q))
    else:
        w_spec = pl.BlockSpec((None, D_MODEL, QKV_TILE), lambda q, i, c_ref: ((c_ref[0] + 1 + q) % N_CHIPS, 0, 0))
    in_specs = [pl.BlockSpec((None, tm, D_MODEL), lambda q, i, c_ref: (tile(q, c_ref) // per_group, i, 0)), w_spec]
    operands = [chip, hf, w]
    aliases = {}
    if qkv is not None:
        in_specs.append(pl.BlockSpec(memory_space=pl.ANY))
        operands.append(qkv)
        aliases = {3: 0}
    return _pcall(
        body, prefetch=1, name=name, grid=(N_CHIPS - 1, S // tm), in_specs=in_specs,
        out_specs=pl.BlockSpec((None, tm, QKV_TILE),
                               lambda q, i, c_ref: (tile(q, c_ref) // per_group, i, tile(q, c_ref) % per_group)),
        out_shape=jax.ShapeDtypeStruct((N_GROUPS, S, 3 * D_MODEL), BF16), input_output_aliases=aliases,
        compiler_params=_params(("arbitrary", "arbitrary")))(*operands)


QKV_PIECES = QKV_SHARD // QKV_TILE


def _qkv_dw(name, p, hf, dqkv):
    S = hf.shape[1]
    per_group = 3 * D_MODEL // QKV_TILE
    tile = lambda s: QKV_PIECES * s + p
    return _mm(name, "tn", (N_CHIPS, 1),
               hf, pl.BlockSpec((None, S, D_MODEL), lambda s, k: (tile(s) // per_group, 0, 0)),
               dqkv, pl.BlockSpec((None, S, QKV_TILE), lambda s, k: (tile(s) // per_group, 0, tile(s) % per_group)),
               jax.ShapeDtypeStruct((N_CHIPS, D_MODEL, QKV_TILE), BF16),
               pl.BlockSpec((None, D_MODEL, QKV_TILE), lambda s, k: (s, 0, 0)), None)


def _qkv_dh(name, g, dqkv, w_pieces, dhf, tm=1024):
    S = dqkv.shape[1]
    per_group = 3 * D_MODEL // QKV_TILE

    def body(*refs):
        a_ref, w_hbm = refs[0], refs[1:1 + QKV_PIECES]
        o_ref, w_ref, sems = refs[-3:]

        @pl.when(pl.program_id(0) == 0)
        def _():
            copies = []
            for j in range(per_group):
                t = per_group * g + j
                copies.append(pltpu.make_async_copy(w_hbm[t % QKV_PIECES].at[t // QKV_PIECES],
                                                    w_ref.at[:, pl.ds(j * QKV_TILE, QKV_TILE)], sems.at[j]))
            for cp in copies:
                cp.start()
            for cp in copies:
                cp.wait()

        o_ref[...] = _dot(a_ref[...], w_ref[...], "nt")

    any_spec = pl.BlockSpec(memory_space=pl.ANY)
    in_specs = [pl.BlockSpec((None, tm, 3 * D_MODEL), lambda i: (g, i, 0))] + [any_spec] * QKV_PIECES
    operands = [dqkv] + list(w_pieces)
    aliases = {}
    if dhf is not None:
        in_specs.append(any_spec)
        operands.append(dhf)
        aliases = {1 + QKV_PIECES: 0}
    return _pcall(body, name=name, grid=(S // tm,), in_specs=in_specs,
                  out_specs=pl.BlockSpec((None, tm, D_MODEL), lambda i: (g, i, 0)),
                  out_shape=jax.ShapeDtypeStruct((N_GROUPS, S, D_MODEL), F32),
                  scratch_shapes=[pltpu.VMEM((D_MODEL, 3 * D_MODEL), BF16), pltpu.SemaphoreType.DMA((per_group,))],
                  input_output_aliases=aliases, compiler_params=_params(("arbitrary",)))(*operands)


def _alibi_coef(g, h):
    vals = [-(2.0 ** (-8.0 * (gg * N_HEADS + h + 1) / (N_GROUPS * N_HEADS))) * ATTN_DILATIONS[gg]
            for gg in range(N_GROUPS)]
    return jnp.where(g == 0, vals[0], jnp.where(g == 1, vals[1], vals[2])).astype(F32)


def _blocks_per_segment(g, S):
    return jnp.right_shift(S // Q_BLOCK, 2 * g)


def _band_bias(pen):
    row = lax.broadcasted_iota(jnp.int32, (Q_BLOCK, 2 * Q_BLOCK), 0)
    col = lax.broadcasted_iota(jnp.int32, (Q_BLOCK, 2 * Q_BLOCK), 1)
    dist = Q_BLOCK + row - col
    valid = jnp.logical_and(dist >= 0, dist <= Q_BLOCK)
    base = jnp.where(valid, jnp.where(col < Q_BLOCK, pen, 0.0), NEG).astype(F32)
    return base, dist.astype(F32)


def _skewed(n_units, stages):
    state = {}
    for step in range(n_units + len(stages) - 1):
        for s, stage in enumerate(stages):
            u = step - s
            if 0 <= u < n_units:
                state[u] = stage(u, state.get(u))
                if s == len(stages) - 1:
                    del state[u]


def _attn_fwd(name, qkv, tq=512):
    S = qkv.shape[1]
    nqb = tq // Q_BLOCK
    scale = HEAD_DIM ** -0.5

    def body(q_ref, k_ref, kp_ref, v_ref, vp_ref, o_ref, lse_ref, kf_ref, vf_ref):
        g = pl.program_id(0)
        i = pl.program_id(1)
        nb = _blocks_per_segment(g, S)
        kf_ref[:Q_BLOCK] = kp_ref[...]
        kf_ref[Q_BLOCK:] = k_ref[...]
        vf_ref[:Q_BLOCK] = vp_ref[...]
        vf_ref[Q_BLOCK:] = v_ref[...]
        coefs = [_alibi_coef(g, h) for h in range(N_HEADS)]
        units = [(b, h) for b in range(nqb) for h in range(N_HEADS)]
        bias = {}

        def scores(u, _):
            b, h = units[u]
            if b not in bias:
                pen = jnp.where((i * nqb + b) % nb != 0, 0.0, NEG).astype(F32)
                bias.clear()
                bias[b] = _band_bias(pen)
            base, dist = bias[b]
            cols = slice(h * HEAD_DIM, (h + 1) * HEAD_DIM)
            q = q_ref[b * Q_BLOCK:(b + 1) * Q_BLOCK, cols]
            kk = kf_ref[b * Q_BLOCK:(b + 2) * Q_BLOCK, cols]
            return _dot(q, kk, "nt") * scale + (coefs[h] * dist + base)

        def softmax(u, s):
            m = jnp.max(s, axis=-1, keepdims=True)
            p = jnp.exp(s - m)
            den = jnp.sum(p, axis=-1, keepdims=True)
            return (p * (1.0 / den)).astype(BF16), m + jnp.log(den)

        def output(u, st):
            b, h = units[u]
            pn, lse = st
            cols = slice(h * HEAD_DIM, (h + 1) * HEAD_DIM)
            rows = slice(b * Q_BLOCK, (b + 1) * Q_BLOCK)
            o_ref[rows, cols] = _dot(pn, vf_ref[b * Q_BLOCK:(b + 2) * Q_BLOCK, cols], "nn")
            lse_ref[rows, h:h + 1] = lse

        _skewed(len(units), [scores, softmax, output])

    main = lambda c: pl.BlockSpec((None, tq, D_MODEL), lambda g, i: (g, i, c))
    prev = lambda c: pl.BlockSpec((None, Q_BLOCK, D_MODEL), lambda g, i: (g, jnp.maximum(i * nqb - 1, 0), c))
    return _pcall(
        body, name=name, grid=(N_GROUPS, S // tq),
        in_specs=[main(0), main(1), prev(1), main(2), prev(2)],
        out_specs=[pl.BlockSpec((None, tq, D_MODEL), lambda g, i: (g, i, 0)),
                   pl.BlockSpec((None, tq, N_HEADS), lambda g, i: (g, i, 0))],
        out_shape=[jax.ShapeDtypeStruct((N_GROUPS, S, D_MODEL), F32),
                   jax.ShapeDtypeStruct((N_GROUPS, S, N_HEADS), F32)],
        scratch_shapes=[pltpu.VMEM((tq + Q_BLOCK, D_MODEL), BF16), pltpu.VMEM((tq + Q_BLOCK, D_MODEL), BF16)],
        compiler_params=_params(("parallel", "parallel")))(qkv, qkv, qkv, qkv, qkv)


def _attn_bwd(name, qkv, do, lse, dl, tq=512):
    S = qkv.shape[1]
    nqb = tq // Q_BLOCK
    n_blocks = S // Q_BLOCK
    scale = HEAD_DIM ** -0.5
    KEYS = 2 * Q_BLOCK

    def body(q_ref, k_ref, v_ref, kp_ref, vp_ref, qn_ref, do_ref, don_ref, lse_ref, lsen_ref, dl_ref, dln_ref,
             out_ref, kf_ref, vf_ref, dk_ref, dv_ref):
        g = pl.program_id(0)
        i = pl.program_id(1)
        nb = _blocks_per_segment(g, S)
        kf_ref[:Q_BLOCK] = kp_ref[...]
        kf_ref[Q_BLOCK:] = k_ref[...]
        vf_ref[:Q_BLOCK] = vp_ref[...]
        vf_ref[Q_BLOCK:] = v_ref[...]
        coefs = [_alibi_coef(g, h) for h in range(N_HEADS)]
        units = [(h, b) for h in range(N_HEADS) for b in range(nqb + 1)]
        bias = {}

        def band(b):
            if b not in bias:
                blk = i * nqb + b
                ok = blk % nb != 0
                if b == nqb:
                    ok = jnp.logical_and(ok, blk < n_blocks)
                bias[b] = _band_bias(jnp.where(ok, 0.0, NEG).astype(F32))
            return bias[b]

        def operands(h, b):
            cols = slice(h * HEAD_DIM, (h + 1) * HEAD_DIM)
            if b == nqb:
                keys = slice(tq, tq + Q_BLOCK)
                return (qn_ref[:, cols], don_ref[:, cols], lsen_ref[:, h:h + 1], dln_ref[:, h:h + 1],
                        kf_ref[keys, cols], vf_ref[keys, cols])
            rows = slice(b * Q_BLOCK, (b + 1) * Q_BLOCK)
            keys = slice(b * Q_BLOCK, b * Q_BLOCK + KEYS)
            return (q_ref[rows, cols], do_ref[rows, cols], lse_ref[rows, h:h + 1], dl_ref[rows, h:h + 1],
                    kf_ref[keys, cols], vf_ref[keys, cols])

        def probs(u, _):
            h, b = units[u]
            q, do_b, lse_b, dl_b, kk, vv = operands(h, b)
            base, dist = band(b)
            if b == nqb:
                base, dist = base[:, :Q_BLOCK], dist[:, :Q_BLOCK]
            p = jnp.exp(_dot(q, kk, "nt") * scale + (coefs[h] * dist + base) - lse_b)
            return p, _dot(do_b, vv, "nt")

        def dscores(u, st):
            h, b = units[u]
            p, dp = st
            dl_b = operands(h, b)[3]
            return ((p * (dp - dl_b)) * scale).astype(BF16), p.astype(BF16)

        def grads(u, st):
            h, b = units[u]
            ds, pb = st
            q, do_b, _, _, kk, _ = operands(h, b)
            cols = slice(h * HEAD_DIM, (h + 1) * HEAD_DIM)
            if b < nqb:
                out_ref[b * Q_BLOCK:(b + 1) * Q_BLOCK, cols] = _dot(ds, kk, "nn").astype(out_ref.dtype)
            if b == 0:
                dk_ref[:Q_BLOCK] = _dot(ds[:, Q_BLOCK:], q, "tn")
                dv_ref[:Q_BLOCK] = _dot(pb[:, Q_BLOCK:], do_b, "tn")
                return None
            dk = _dot(ds, q, "tn")
            dv = _dot(pb, do_b, "tn")
            before = slice((b - 1) * Q_BLOCK, b * Q_BLOCK)
            dk_ref[before] += dk[:Q_BLOCK]
            dv_ref[before] += dv[:Q_BLOCK]
            if b < nqb:
                own = slice(b * Q_BLOCK, (b + 1) * Q_BLOCK)
                dk_ref[own] = dk[Q_BLOCK:]
                dv_ref[own] = dv[Q_BLOCK:]
            else:
                out_ref[:, D_MODEL + h * HEAD_DIM:D_MODEL + (h + 1) * HEAD_DIM] = dk_ref[...].astype(out_ref.dtype)
                out_ref[:, 2 * D_MODEL + h * HEAD_DIM:2 * D_MODEL + (h + 1) * HEAD_DIM] = (
                    dv_ref[...].astype(out_ref.dtype))
            return None

        _skewed(len(units), [probs, dscores, grads])

    nxt_blk = lambda i: jnp.minimum((i + 1) * nqb, n_blocks - 1)
    main = lambda c: pl.BlockSpec((None, tq, D_MODEL), lambda g, i: (g, i, c))
    prev = lambda c: pl.BlockSpec((None, Q_BLOCK, D_MODEL), lambda g, i: (g, jnp.maximum(i * nqb - 1, 0), c))
    row = pl.BlockSpec((None, tq, D_MODEL), lambda g, i: (g, i, 0))
    row_n = pl.BlockSpec((None, Q_BLOCK, D_MODEL), lambda g, i: (g, nxt_blk(i), 0))
    col = pl.BlockSpec((None, tq, N_HEADS), lambda g, i: (g, i, 0))
    col_n = pl.BlockSpec((None, Q_BLOCK, N_HEADS), lambda g, i: (g, nxt_blk(i), 0))
    return _pcall(
        body, name=name, grid=(N_GROUPS, S // tq),
        in_specs=[main(0), main(1), main(2), prev(1), prev(2), row_n, row, row_n, col, col_n, col, col_n],
        out_specs=pl.BlockSpec((None, tq, 3 * D_MODEL), lambda g, i: (g, i, 0)),
        out_shape=jax.ShapeDtypeStruct((N_GROUPS, S, 3 * D_MODEL), BF16),
        scratch_shapes=[pltpu.VMEM((tq + Q_BLOCK, D_MODEL), BF16), pltpu.VMEM((tq + Q_BLOCK, D_MODEL), BF16),
                        pltpu.VMEM((tq, HEAD_DIM), F32), pltpu.VMEM((tq, HEAD_DIM), F32)],
        compiler_params=_params(("parallel", "parallel")))(qkv, qkv, qkv, qkv, qkv, qkv, do, do, lse, lse, dl, dl)


def _group_weights(lse_ref):
    l0, l1, l2 = lse_ref[0], lse_ref[1], lse_ref[2]
    m = jnp.maximum(jnp.maximum(l0, l1), l2)
    e = [jnp.exp(l0 - m), jnp.exp(l1 - m), jnp.exp(l2 - m)]
    den = e[0] + e[1] + e[2]
    return [ei / den for ei in e]


MERGE_TOKENS = 512


def _folded_views(a, tb):
    _, S, C = a.shape
    views, specs = [], []
    for g, dil in enumerate(ATTN_DILATIONS):
        views.append(a.reshape(N_GROUPS * dil, S // dil, C))
        specs.append(pl.BlockSpec((dil, tb // dil, C), lambda i, g=g: (g, i, 0)))
    return views, specs


def _unfold_into(dst_ref, folded_refs):
    for g, (dil, ref) in enumerate(zip(ATTN_DILATIONS, folded_refs)):
        n = ref.shape[1]
        for r in range(dil):
            for c, cols in enumerate(_lane_chunks(ref.shape[2])):
                dst_ref[g, c, _strided_rows(r, n, dil), :] = ref[r, :, cols]


def _merge_fwd(name, o, lse, tb=MERGE_TOKENS):
    S = o.shape[1]

    def body(o0_ref, o1_ref, o2_ref, lse_ref, out_ref, o_ref):
        _unfold_into(o_ref, (o0_ref, o1_ref, o2_ref))
        w = _group_weights(lse_ref)
        for h in range(N_HEADS):
            cols = slice(h * HEAD_DIM, (h + 1) * HEAD_DIM)
            acc = w[0][:, h:h + 1] * o_ref[0, h]
            for gg in range(1, N_GROUPS):
                acc = acc + w[gg][:, h:h + 1] * o_ref[gg, h]
            out_ref[:, cols] = acc.astype(out_ref.dtype)

    views, specs = _folded_views(o, tb)
    return _pcall(body, name=name, grid=(S // tb,),
                  in_specs=specs + [pl.BlockSpec((N_GROUPS, tb, N_HEADS), lambda i: (0, i, 0))],
                  out_specs=pl.BlockSpec((tb, D_MODEL), lambda i: (i, 0)),
                  out_shape=jax.ShapeDtypeStruct((S, D_MODEL), BF16),
                  scratch_shapes=[pltpu.VMEM((N_GROUPS, N_HEADS, tb, HEAD_DIM), F32)],
                  compiler_params=_params(("parallel",)))(*views, lse)


def _merge_bwd(name, d_out, o, lse, tb=MERGE_TOKENS):
    S = o.shape[1]
    n_chunks = sum(ATTN_DILATIONS)

    def body(d_ref, o0_ref, o1_ref, o2_ref, lse_ref, do_hbm, dl_ref, o_ref, dt_ref, df_ref, sems):
        i = pl.program_id(0)
        _unfold_into(o_ref, (o0_ref, o1_ref, o2_ref))
        w = _group_weights(lse_ref)
        for h in range(N_HEADS):
            cols = slice(h * HEAD_DIM, (h + 1) * HEAD_DIM)
            dv = d_ref[:, cols]
            merged = w[0][:, h:h + 1] * o_ref[0, h]
            for gg in range(1, N_GROUPS):
                merged = merged + w[gg][:, h:h + 1] * o_ref[gg, h]
            dsum = jnp.sum(dv * merged, axis=-1, keepdims=True)
            for gg in range(N_GROUPS):
                wg = w[gg][:, h:h + 1]
                dt_ref[gg, h] = wg * dv
                dl_ref[gg, :, h:h + 1] = wg * dsum
        copies = []
        for gg, dil in enumerate(ATTN_DILATIONS):
            n = tb // dil
            for r in range(dil):
                for h, cols in enumerate(_lane_chunks(D_MODEL)):
                    df_ref[gg, r * n:(r + 1) * n, cols] = (
                        dt_ref[gg, h, _strided_rows(r, n, dil), :].astype(df_ref.dtype))
                cp = pltpu.make_async_copy(df_ref.at[gg, pl.ds(r * n, n)],
                                           do_hbm.at[gg, pl.ds(r * (S // dil) + i * n, n)], sems.at[len(copies)])
                cp.start()
                copies.append(cp)
        for cp in copies:
            cp.wait()

    views, specs = _folded_views(o, tb)
    small = pl.BlockSpec((N_GROUPS, tb, N_HEADS), lambda i: (0, i, 0))
    return _pcall(body, name=name, grid=(S // tb,),
                  in_specs=[pl.BlockSpec((tb, D_MODEL), lambda i: (i, 0))] + specs + [small],
                  out_specs=[pl.BlockSpec(memory_space=pl.ANY), small],
                  out_shape=[jax.ShapeDtypeStruct((N_GROUPS, S, D_MODEL), BF16),
                             jax.ShapeDtypeStruct((N_GROUPS, S, N_HEADS), F32)],
                  scratch_shapes=[pltpu.VMEM((N_GROUPS, N_HEADS, tb, HEAD_DIM), F32),
                                  pltpu.VMEM((N_GROUPS, N_HEADS, tb, HEAD_DIM), F32),
                                  pltpu.VMEM((N_GROUPS, tb, D_MODEL), BF16), pltpu.SemaphoreType.DMA((n_chunks,))],
                  compiler_params=_params(("arbitrary",)))(d_out, *views, lse)


def _shift_rows(a, k):
    return pltpu.roll(a, k % a.shape[0], axis=0)


def _pool_level(g, levels):
    return jnp.where(g == 0, levels[0], jnp.where(g == 1, levels[1], jnp.where(g == 2, levels[2], levels[3])))


def _pool_fwd(name, u, w_group, scale, x_in, tr=1024):
    S, D = u.shape
    H = POOL_HALO

    def body(u_ref, halo_ref, w_ref, sc_ref, x_ref, y_ref, z_ref, xo_ref):
        g = pl.program_id(0)
        i = pl.program_id(1)
        uv = u_ref[...]
        halo = jnp.where(i > 0, halo_ref[...], 0.0)
        s = jnp.concatenate([halo, uv], axis=0)
        levels = []
        for k in (1, 2, 4, 8):
            s = s + _shift_rows(s, k)
            levels.append(s[H:])
        t = i * tr + lax.broadcasted_iota(jnp.int32, (tr, 1), 0)
        cnt = jnp.minimum(t + 1, jnp.left_shift(2, g)).astype(F32)
        y = _pool_level(g, levels) / cnt - uv
        yb = y.astype(BF16)
        z = _dot(yb, w_ref[...], "nn")
        y_ref[...] = yb
        z_ref[...] = z
        xo_ref[...] = x_ref[...] + z * sc_ref[...]

    blk = pl.BlockSpec((tr, POOL_DIM), lambda g, i: (i, g))
    return _pcall(
        body, name=name, grid=(D // POOL_DIM, S // tr),
        in_specs=[blk, pl.BlockSpec((H, POOL_DIM), lambda g, i: (jnp.maximum(i * (tr // H) - 1, 0), g)),
                  pl.BlockSpec((None, POOL_DIM, POOL_DIM), lambda g, i: (g, 0, 0)),
                  pl.BlockSpec((1, POOL_DIM), lambda g, i: (0, g)), blk],
        out_specs=[blk, blk, blk],
        out_shape=[jax.ShapeDtypeStruct((S, D), BF16), jax.ShapeDtypeStruct((S, D), F32),
                   jax.ShapeDtypeStruct((S, D), F32)],
        compiler_params=_params(("parallel", "parallel")))(u, u, w_group, scale, x_in)


def _pool_bwd(name, d_out, z, y, scale, w_group, tr=1024):
    S, D = d_out.shape
    H = POOL_HALO

    def body(d_ref, dn_ref, z_ref, y_ref, sc_ref, w_ref, du_ref, dw_ref, dsc_ref):
        g = pl.program_id(0)
        i = pl.program_id(1)
        dv = d_ref[...]
        dz = jnp.concatenate([dv, dn_ref[...]], axis=0) * sc_ref[...]
        dzb = dz.astype(BF16)
        dy = _dot(dzb, w_ref[...], "nt")
        t = i * tr + lax.broadcasted_iota(jnp.int32, (tr + H, 1), 0)
        cnt = jnp.minimum(t + 1, jnp.left_shift(2, g)).astype(F32)
        s = jnp.where(t < S, dy / cnt, 0.0)
        levels = []
        for k in (1, 2, 4, 8):
            s = s + _shift_rows(s, -k)
            levels.append(s[:tr])
        du_ref[...] = (_pool_level(g, levels) - dy[:tr]).astype(du_ref.dtype)
        dw = _dot(y_ref[...], dzb[:tr], "tn")
        dsc = jnp.sum(dv * z_ref[...], axis=0, keepdims=True)

        @pl.when(i == 0)
        def _():
            dw_ref[...] = dw
            dsc_ref[...] = dsc

        @pl.when(i > 0)
        def _():
            dw_ref[...] += dw
            dsc_ref[...] += dsc

    blk = pl.BlockSpec((tr, POOL_DIM), lambda g, i: (i, g))
    vec = pl.BlockSpec((1, POOL_DIM), lambda g, i: (0, g))
    mat = pl.BlockSpec((None, POOL_DIM, POOL_DIM), lambda g, i: (g, 0, 0))
    nxt = pl.BlockSpec((H, POOL_DIM), lambda g, i: (jnp.minimum((i + 1) * (tr // H), S // H - 1), g))
    return _pcall(
        body, name=name, grid=(D // POOL_DIM, S // tr), in_specs=[blk, nxt, blk, blk, vec, mat],
        out_specs=[blk, mat, vec],
        out_shape=[jax.ShapeDtypeStruct((S, D), BF16), jax.ShapeDtypeStruct((D // POOL_DIM, POOL_DIM, POOL_DIM), F32),
                   jax.ShapeDtypeStruct((1, D), F32)],
        compiler_params=_params(("parallel", "arbitrary")))(d_out, d_out, z, y, scale, w_group)


def _row_tile(rows, cols, target_bytes=1 << 20):
    best = rows
    for tr in range(8, rows + 1, 8):
        if rows % tr == 0 and tr * cols * 4 <= target_bytes:
            best = tr
    return best if best * cols * 4 <= 4 * target_bytes else rows


def _adamw(name, w, g, m, v):
    R, C = w.shape
    tr = _row_tile(R, C) if R % 8 == 0 else R

    def body(w_ref, g_ref, m_ref, v_ref, go_ref, d_ref, mo_ref, vo_ref):
        gv = g_ref[...]
        m2 = ADAM_B1 * m_ref[...] + (1.0 - ADAM_B1) * gv
        v2 = ADAM_B2 * v_ref[...] + (1.0 - ADAM_B2) * (gv * gv)
        m_hat = m2 * (1.0 / (1.0 - ADAM_B1 ** ADAM_STEP))
        v_hat = v2 * (1.0 / (1.0 - ADAM_B2 ** ADAM_STEP))
        d_ref[...] = -ADAM_LR * (m_hat / (jnp.sqrt(v_hat) + ADAM_EPS) + ADAM_WD * w_ref[...])
        go_ref[...] = gv
        mo_ref[...] = m2
        vo_ref[...] = v2

    blk = pl.BlockSpec((tr, C), lambda i: (i, 0))
    sds = jax.ShapeDtypeStruct((R, C), F32)
    return _pcall(body, name=name, grid=(R // tr,), in_specs=[blk] * 4, out_specs=[blk] * 4, out_shape=[sds] * 4,
                  compiler_params=_params(("parallel",)))(w, g, m, v)


def _sum_leading(name, a, out_dtype):
    n, R, C = a.shape
    tr = _row_tile(R, C) if R % 16 == 0 else R
    if tr % 16:
        tr = R

    def body(a_ref, o_ref):
        acc = a_ref[0].astype(F32)
        for j in range(1, n):
            acc = acc + a_ref[j].astype(F32)
        o_ref[...] = acc.astype(o_ref.dtype)

    return _pcall(body, name=name, grid=(R // tr,), in_specs=[pl.BlockSpec((n, tr, C), lambda i: (0, i, 0))],
                  out_specs=pl.BlockSpec((tr, C), lambda i: (i, 0)), out_shape=jax.ShapeDtypeStruct((R, C), out_dtype),
                  compiler_params=_params(("parallel",)))(a)


def _cast_into_slot(name, w, layer, chip, dtype):
    _, R, C = w.shape
    tr = _row_tile(R, C, 2 << 20)
    if tr % 16:
        tr = R

    def body(c_ref, w_ref, o_ref):
        o_ref[...] = w_ref[...].astype(o_ref.dtype)

    grid_spec = pltpu.PrefetchScalarGridSpec(
        num_scalar_prefetch=1, grid=(R // tr,),
        in_specs=[pl.BlockSpec((None, tr, C), lambda i, c_ref: (layer, i, 0))],
        out_specs=pl.BlockSpec((None, tr, C), lambda i, c_ref: (c_ref[0], i, 0)))
    return _pcall(body, name=name, grid_spec=grid_spec, out_shape=jax.ShapeDtypeStruct((N_CHIPS, R, C), dtype),
                  compiler_params=_params(("parallel",)))(chip, w)


def _cast_qkv(name, w, chip, tr=256):
    _, R, C = w.shape

    def body(c_ref, w_ref, own_ref, *piece_refs):
        v = w_ref[...].astype(BF16)
        own_ref[...] = v
        for p, ref in enumerate(piece_refs):
            ref[...] = v[:, p * QKV_TILE:(p + 1) * QKV_TILE]

    piece = pl.BlockSpec((None, tr, QKV_TILE), lambda i, c_ref: (c_ref[0], i, 0))
    return _pcall(body, prefetch=1, name=name, grid=(R // tr,),
                  in_specs=[pl.BlockSpec((None, tr, C), lambda i, c_ref: (0, i, 0))],
                  out_specs=[pl.BlockSpec((tr, C), lambda i, c_ref: (i, 0))] + [piece] * QKV_PIECES,
                  out_shape=[jax.ShapeDtypeStruct((R, C), BF16)]
                  + [jax.ShapeDtypeStruct((N_CHIPS, R, QKV_TILE), BF16)] * QKV_PIECES,
                  compiler_params=_params(("parallel",)))(chip, w)


def _owner_sum(name, mine, landed, chip, core, out=None, layer=None, col=None):
    _, half, C = mine.shape
    k, n_col = col if col is not None else (0, 1)
    tr = _row_tile(half, C)
    if tr % 16:
        tr = half
    nrt = half // tr
    has_out = out is not None

    def body(*refs):
        m_ref, l_ref, o_ref = refs[2], refs[3], refs[-1]
        acc = m_ref[...].astype(F32)
        for j in range(3):
            acc = acc + l_ref[j].astype(F32)
        o_ref[...] = acc

    if layer is None:
        out_shape = jax.ShapeDtypeStruct((2 * half, n_col * C), F32)
        o_spec = pl.BlockSpec((tr, C), lambda i, ch, co: (co[0] * nrt + i, k))
    else:
        out_shape = jax.ShapeDtypeStruct((2, 2 * half, C), F32)
        o_spec = pl.BlockSpec((None, tr, C), lambda i, ch, co: (layer, co[0] * nrt + i, 0))
    in_specs = [pl.BlockSpec((None, tr, C), lambda i, ch, co: (ch[0], i, 0)),
                pl.BlockSpec((3, tr, C), lambda i, ch, co: (0, i, 0))]
    operands = [chip, core, mine, landed]
    aliases = {}
    if has_out:
        in_specs.append(ANY)
        operands.append(out)
        aliases = {4: 0}
    grid_spec = pltpu.PrefetchScalarGridSpec(num_scalar_prefetch=2, grid=(nrt,), in_specs=in_specs, out_specs=o_spec)
    return _pcall(body, name=name, grid_spec=grid_spec, out_shape=out_shape, input_output_aliases=aliases,
                  compiler_params=_params(("parallel",)))(*operands)


def _add_my_half(name, p, q, core):
    n, R, C = p.shape
    half = R // 2

    def body(c_ref, p_ref, q_ref, o_ref):
        o_ref[...] = (p_ref[...].astype(F32) + q_ref[...].astype(F32)).astype(o_ref.dtype)

    grid_spec = pltpu.PrefetchScalarGridSpec(
        num_scalar_prefetch=1, grid=(n,),
        in_specs=[pl.BlockSpec((None, half, C), lambda s, c_ref: (s, c_ref[0], 0)),
                  pl.BlockSpec((None, half, C), lambda s, c_ref: (s, 0, 0))],
        out_specs=pl.BlockSpec((None, half, C), lambda s, c_ref: (s, 0, 0)))
    return _pcall(body, name=name, grid_spec=grid_spec, out_shape=jax.ShapeDtypeStruct((n, half, C), BF16),
                  compiler_params=_params(("parallel",)))(core, p, q)


ANY = pl.BlockSpec(memory_space=pl.ANY)


def _place():
    x, y, c = lax.axis_index("x"), lax.axis_index("y"), lax.axis_index("c")
    other_chips = [(1 - x, y), (x, 1 - y), (1 - x, 1 - y)]
    return x, y, c, other_chips


def _remote(src, dst, send, recv, k, to):
    return pltpu.make_async_remote_copy(src_ref=src, dst_ref=dst, send_sem=send.at[k], recv_sem=recv.at[k],
                                        device_id=to, device_id_type=MESH)


def _in_place(bufs):
    return dict(operands=bufs, out_shapes=[jax.ShapeDtypeStruct(b.shape, b.dtype) for b in bufs],
                aliases={t: t for t in range(len(bufs))})


def _gather_over_ici(bufs, split):
    n = len(bufs)

    def copies(ins, outs, send, recv, base=0):
        x, y, c, chips = _place()
        me = 2 * x + y
        out = []
        for t in range(n):
            piece = outs[t].at[me]
            if split[t]:
                half = outs[t].shape[1] // 2
                piece = outs[t].at[me, pl.ds(c * half, half)]
            for j, (px, py) in enumerate(chips):
                out.append(_remote(piece, piece, send, recv, base + 3 * t + j, (px, py, c)))
        return out

    return _Rider(n_copies=3 * n, copies=copies, **_in_place(bufs))


def _gather_to_sibling(bufs):
    n = len(bufs)

    def copies(ins, outs, send, recv, base=0):
        x, y, c, chips = _place()
        out = []
        for t in range(n):
            half = outs[t].shape[1] // 2
            for j, (px, py) in enumerate(chips):
                piece = outs[t].at[2 * px + py, pl.ds(c * half, half)]
                out.append(_remote(piece, piece, send, recv, base + 3 * t + j, (x, y, 1 - c)))
        return out

    return _Rider(n_copies=3 * n, copies=copies, **_in_place(bufs))


def _gather_and_pass_rider(buf):
    half = buf.shape[1] // 2

    def pieces(outs):
        x, y, c, chips = _place()
        rows = lambda core: pl.ds(core * half, half)
        mine = outs[0].at[2 * x + y, rows(c)]
        landed = [outs[0].at[2 * px + py, rows(c)] for px, py in chips]
        theirs = [outs[0].at[2 * px + py, rows(1 - c)] for px, py in chips]
        return (x, y, c, chips), mine, landed, theirs

    def start(ins, outs, send, recv, base=0):
        (x, y, c, chips), mine, _, _ = pieces(outs)
        for j, (px, py) in enumerate(chips):
            _remote(mine, mine, send, recv, base + j, (px, py, c)).start()

    def finish(ins, outs, send, recv, base=0):
        (x, y, c, chips), mine, landed, theirs = pieces(outs)
        sibling = (x, y, 1 - c)
        passed = []
        for j, (px, py) in enumerate(chips):
            _remote(landed[j], landed[j], send, recv, base + j, (px, py, c)).wait_recv()
            passed.append(_remote(landed[j], landed[j], send, recv, base + 3 + j, sibling))
            passed[-1].start()
        for j in range(3):
            _remote(theirs[j], theirs[j], send, recv, base + 3 + j, sibling).wait_recv()
        for j, (px, py) in enumerate(chips):
            _remote(mine, mine, send, recv, base + j, (px, py, c)).wait_send()
            passed[j].wait_send()

    return _Rider(n_copies=6, start=start, finish=finish, **_in_place([buf]))


def _swap_halves_rider(parts):
    n = len(parts)

    def copies(ins, outs, send, recv, base=0):
        x, y, c, _ = _place()
        out = []
        for t in range(n):
            half = ins[t].shape[1] // 2
            out.append(_remote(ins[t].at[:, pl.ds((1 - c) * half, half)], outs[t], send, recv, base + t,
                               (x, y, 1 - c)))
        return out

    shapes = [jax.ShapeDtypeStruct((p.shape[0], p.shape[1] // 2, p.shape[2]), p.dtype) for p in parts]
    return _Rider(parts, shapes, {}, n, copies)


def _scatter_rider(sums):
    n = len(sums)

    def copies(ins, outs, send, recv, base=0):
        x, y, c, chips = _place()
        out = []
        for t in range(n):
            for j, (px, py) in enumerate(chips):
                out.append(_remote(ins[t].at[2 * px + py], outs[t].at[j], send, recv, base + 3 * t + j, (px, py, c)))
        return out

    shapes = [jax.ShapeDtypeStruct((3,) + s.shape[1:], s.dtype) for s in sums]
    return _Rider(sums, shapes, {}, 3 * n, copies)


def _share_rider(blocks, pieces):
    def copies(ins, outs, send, recv, base=0):
        x, y, c, _ = _place()
        out = []
        for k, (t, l, col) in enumerate(pieces):
            ref = outs[t] if l is None else outs[t].at[l]
            r2 = ref.shape[0] // 2
            piece = ref.at[pl.ds(c * r2, r2)]
            if col is not None:
                width = ref.shape[1] // col[1]
                piece = ref.at[pl.ds(c * r2, r2), pl.ds(col[0] * width, width)]
            out.append(_remote(piece, piece, send, recv, base + k, (x, y, 1 - c)))
        return out

    return _Rider(n_copies=len(pieces), copies=copies, **_in_place(blocks))


def _gather_small(name, v):
    def body(v_ref, out_ref, send_sem, recv_sem, local_sem):
        x, y, c, _ = _place()
        me = 4 * x + 2 * y + c
        flips = [(fx, fy, fc) for fx in (0, 1) for fy in (0, 1) for fc in (0, 1)][1:]
        local = pltpu.make_async_copy(v_ref, out_ref.at[me], local_sem)
        local.start()
        copies = []
        for j, (fx, fy, fc) in enumerate(flips):
            peer = (x ^ fx, y ^ fy, c ^ fc)
            copies.append(pltpu.make_async_remote_copy(
                src_ref=v_ref, dst_ref=out_ref.at[me], send_sem=send_sem.at[j], recv_sem=recv_sem.at[j],
                device_id=peer, device_id_type=MESH))
        for cp in copies:
            cp.start()
        for j, (fx, fy, fc) in enumerate(flips):
            peer = (x ^ fx, y ^ fy, c ^ fc)
            pltpu.make_async_remote_copy(
                src_ref=v_ref, dst_ref=out_ref.at[4 * peer[0] + 2 * peer[1] + peer[2]], send_sem=send_sem.at[j],
                recv_sem=recv_sem.at[j], device_id=peer, device_id_type=MESH).wait_recv()
        for cp in copies:
            cp.wait_send()
        local.wait()

    return _pcall(body, name=name, in_specs=[ANY], out_specs=ANY,
                  out_shape=jax.ShapeDtypeStruct((8,) + v.shape, v.dtype),
                  scratch_shapes=[pltpu.SemaphoreType.DMA((7,)), pltpu.SemaphoreType.DMA((7,)),
                                  pltpu.SemaphoreType.DMA])(v)


def _fold(a, dil):
    if dil == 1:
        return a
    S, C = a.shape
    return a.reshape(S // dil, dil, C).transpose(1, 0, 2).reshape(S, C)


def _unfold(a, dil):
    if dil == 1:
        return a
    S, C = a.shape
    return a.reshape(dil, S // dil, C).transpose(1, 0, 2).reshape(S, C)


def _fold_groups(a):
    return jnp.stack([_fold(a[g] if a.ndim == 3 else a, d) for g, d in enumerate(ATTN_DILATIONS)])


def _unfold_groups(a):
    return jnp.stack([_unfold(a[g], d) for g, d in enumerate(ATTN_DILATIONS)])


class _NoComm:
    def __init__(self, weights):
        self.weights = weights
        self.grads = {}
        self.chip = jnp.zeros((1,), jnp.int32)

    def w(self, name):
        return self.weights[name]

    def run(self, fn, name, *args, **kw):
        return fn(name, *args, **kw)

    def grad(self, name, value):
        self.grads[name] = value


def _local_step(xs, tgt, attn_norm, ffn_norm, final_norm, comm):
    run = comm.run
    hf = None
    for g in range(N_GROUPS):
        hf = run(_norm_fold, "fold%d" % g, g, xs, attn_norm, hf)
    qkv = run(_qkv_tiles, "qkv_own", None, hf, comm.w("wq_own"), None, comm.chip)
    for p in range(QKV_PIECES):
        qkv = run(_qkv_tiles, "qkv_piece%d" % p, p, hf, comm.w("wq%d" % p), qkv, comm.chip)
    o_f, lse_f = run(_attn_fwd, "attn_fwd", qkv)
    lse_t = _unfold_groups(lse_f)
    merged = run(_merge_fwd, "merge_fwd", o_f, lse_t)
    x1, h1 = run(_proj_res_norm, "attn_out", merged, comm.w("o"), xs, ffn_norm[0:1])
    gu0, act0 = run(_ffn_up, "ffn0_up", h1, comm.w("gu0"))
    x2, h2 = run(_proj_res_norm, "ffn0_down", act0, comm.w("d0"), x1, comm.w("pool_norm"))
    u = run(_mm_rows, "pool_in", h2, comm.w("p"), F32, 512)
    y, z, x3 = run(_pool_fwd, "pool_fwd", u, comm.w("pg"), comm.w("pool_scale"), x2)
    h3 = run(_rms_fwd, "norm_ffn1", x3, ffn_norm[1:2])
    gu1, act1 = run(_ffn_up, "ffn1_up", h3, comm.w("gu1"))
    loss, dx4, d_final = run(_proj_res_loss, "ffn1_down", act1, comm.w("d1"), x3, final_norm, tgt)

    dgu1 = run(_ffn_down_bwd, "ffn1_down_bwd", dx4, comm.w("d1"), gu1)
    comm.grad("d1", run(_mm_tn, "ffn1_down_dw", act1, dx4, FF_SHARD, 2048))
    comm.grad("gu1", run(_ffn_dw_up, "ffn1_up_dw", h3, dgu1))
    dx3, d_ffn1 = run(_ffn_dh, "ffn1_up_dh", dgu1, comm.w("gu1"), x3, ffn_norm[1:2], dx4)

    du, dw_pg, d_scale = run(_pool_bwd, "pool_bwd", dx3, z, y, comm.w("pool_scale"), comm.w("pg"))
    comm.grad("pg", dw_pg)
    comm.grad("p", run(_mm_tn, "pool_in_dw", h2, du, D_MODEL, h2.shape[0]))
    dx2, d_pool = run(_dh_norm_bwd, "pool_in_dh", du, comm.w("p"), x2, comm.w("pool_norm"), dx3)

    dgu0 = run(_ffn_down_bwd, "ffn0_down_bwd", dx2, comm.w("d0"), gu0)
    comm.grad("d0", run(_mm_tn, "ffn0_down_dw", act0, dx2, FF_SHARD, 2048))
    comm.grad("gu0", run(_ffn_dw_up, "ffn0_up_dw", h1, dgu0))
    dx1, d_ffn0 = run(_ffn_dh, "ffn0_up_dh", dgu0, comm.w("gu0"), x1, ffn_norm[0:1], dx2)

    d_merged = run(_mm_nt_rows, "attn_out_dh", dx1, comm.w("o"), 512)
    comm.grad("o", run(_mm_tn, "attn_out_dw", merged, dx1, D_MODEL, 2048))
    do_f, dl_t = run(_merge_bwd, "merge_bwd", d_merged, o_f, lse_t)
    dqkv = run(_attn_bwd, "attn_bwd", qkv, do_f, lse_f, _fold_groups(dl_t))
    for p in range(QKV_PIECES):
        comm.grad("qkv%d" % p, run(_qkv_dw, "qkv_dw%d" % p, p, hf, dqkv))
    dhf = None
    for g in range(N_GROUPS):
        dhf = run(_qkv_dh, "qkv_dh%d" % g, g, dqkv, [comm.w("wq%d" % p) for p in range(QKV_PIECES)], dhf)
    grad_x, d_attn = run(_rms_bwd_folded, "norm_attn_bwd", xs, attn_norm, dhf, dx1)

    small = dict(attn=d_attn, ffn0=d_ffn0, ffn1=d_ffn1, final=d_final, pool=d_pool, scale=d_scale)
    return loss, grad_x, small


TENSORS = ("qkv", "o", "p", "pg", "gu0", "gu1", "d0", "d1")


def _block_of(name):
    if name[:-1] in ("gu", "d"):
        return name[:-1], int(name[-1]), None
    if name[:-1] == "qkv":
        return "qkv", None, (int(name[-1]), QKV_PIECES)
    return name, None, None


class _MeshComm:
    def __init__(self, bufs, chip_arr, core_arr, pg_rows):
        self.bufs = dict(bufs)
        self.chip, self.chip_arr, self.core_arr, self.pg_rows = chip_arr, chip_arr, core_arr, pg_rows
        self.parts, self.sums, self.blocks = {}, {}, {}
        ici = lambda *names: lambda: self.gather(names, True)
        d2d = lambda *names: lambda: self.gather(names, False)
        whole = lambda name: lambda: self.gather_and_pass(name)
        swap = lambda *names: lambda: self.swap(names)
        scatter = lambda *names: lambda: self.scatter(names)
        share = lambda *names: lambda: self.share(names)
        self.plan = {
            "qkv_own": [whole("wq0")],
            "qkv_piece0": [whole("wq1")],
            "qkv_piece1": [whole("wq2")],
            "attn_fwd": [ici("gu0", "o")],
            "merge_fwd": [ici("d0"), d2d("gu0", "o")],
            "attn_out": [ici("p", "pg", "pool_norm", "pool_scale"), d2d("d0")],
            "ffn0_up": [ici("gu1"), d2d("p", "pg")],
            "ffn0_down": [d2d("gu1"), ici("d1")],
            "pool_in": [d2d("d1")],
            "ffn1_up_dh": [swap("d1", "gu1")],
            "ffn0_down_bwd": [scatter("gu1")],
            "ffn0_down_dw": [scatter("d1")],
            "ffn0_up_dh": [swap("pg", "p", "d0", "gu0"), share("gu1", "d1")],
            "merge_bwd": [swap("o")],
            "attn_bwd": [scatter("gu0", "o")],
            "qkv_dw0": [scatter("d0")],
            "qkv_dw1": [scatter("p", "pg"), swap("qkv0")],
            "qkv_dw2": [share("gu0", "o", "d0"), scatter("qkv0"), swap("qkv1")],
            "qkv_dh0": [share("qkv0", "p", "pg"), scatter("qkv1"), swap("qkv2")],
            "qkv_dh1": [share("qkv1"), scatter("qkv2")],
            "qkv_dh2": [share("qkv2")],
        }

    def gather_and_pass(self, name):
        return _gather_and_pass_rider(self.bufs[name]), lambda res: self.bufs.update({name: res[0]})

    def gather(self, names, over_ici):
        bufs = [self.bufs[n] for n in names]
        rider = _gather_over_ici(bufs, [n in TENSORS for n in names]) if over_ici else _gather_to_sibling(bufs)
        return rider, lambda res: self.bufs.update(zip(names, res))

    def swap(self, names):
        def done(res):
            for n, q in zip(names, res):
                self.sums[n] = _add_my_half("chip_sum_" + n, self.parts[n], q, self.core_arr)
        return _swap_halves_rider([self.parts[n] for n in names]), done

    def scatter(self, names):
        def done(res):
            for n, landed in zip(names, res):
                key, layer, col = _block_of(n)
                self.blocks[key] = _owner_sum("owner_sum_" + n, self.sums[n], landed, self.chip_arr, self.core_arr,
                                              out=self.blocks.get(key), layer=layer, col=col)
        return _scatter_rider([self.sums[n] for n in names]), done

    def share(self, names):
        keys, pieces = [], []
        for n in names:
            key, layer, col = _block_of(n)
            if key not in keys:
                keys.append(key)
            pieces.append((keys.index(key), layer, col))
        return _share_rider([self.blocks[k] for k in keys], pieces), lambda res: self.blocks.update(zip(keys, res))

    def run(self, fn, name, *args, **kw):
        made = [make() for make in self.plan.get(name, [])]
        if not made:
            return fn(name, *args, **kw)
        riders = [r for r, _ in made]
        out = _ride(riders[0] if len(riders) == 1 else _riders(*riders), fn, name, *args, **kw)
        for r, done in made:
            done(r.results)
        return out

    def w(self, name):
        b = self.bufs[name]
        if name in ("o", "p", "d0", "d1"):
            return b.reshape(-1, b.shape[-1])
        if name == "pg":
            b = b.reshape(N_CHIPS, 4, self.pg_rows, POOL_DIM).transpose(1, 0, 2, 3)
            return b.reshape(4, POOL_DIM, POOL_DIM)
        if name in ("pool_norm", "pool_scale"):
            return b.reshape(1, -1)
        return b

    def grad(self, name, value):
        if name == "pg":
            value = value.reshape(4, N_CHIPS, self.pg_rows, POOL_DIM).transpose(1, 0, 2, 3)
            value = value.reshape(N_CHIPS, 4 * self.pg_rows, POOL_DIM).astype(BF16)
        elif name in ("o", "p", "d0", "d1"):
            value = value.reshape(N_CHIPS, value.shape[0] // N_CHIPS, value.shape[1])
        self.parts[name] = value


def kernel(x, attn_norm, w_qkv, w_attn_out, pool_norm, w_pool_in, w_pool_group, pool_scale, ffn_norm, w_ffn_gate_up, w_ffn_down, final_norm, loss_target, m_attn_norm, m_w_qkv, m_w_attn_out, m_pool_norm, m_w_pool_in, m_w_pool_group, m_pool_scale, m_ffn_norm, m_w_ffn_gate_up, m_w_ffn_down, m_final_norm, v_attn_norm, v_w_qkv, v_w_attn_out, v_pool_norm, v_w_pool_in, v_w_pool_group, v_pool_scale, v_ffn_norm, v_w_ffn_gate_up, v_w_ffn_down, v_final_norm):
    D = D_MODEL
    chip = 2 * lax.axis_index("x") + lax.axis_index("y")
    core = lax.axis_index("c")
    pg_rows = w_pool_group.shape[2]

    chip_arr = chip.astype(jnp.int32).reshape(1)
    core_arr = core.astype(jnp.int32).reshape(1)

    pieces = _cast_qkv("cast_qkv", w_qkv, chip_arr)
    bufs = dict(zip(["wq_own"] + ["wq%d" % p for p in range(QKV_PIECES)], pieces))
    shards = [(w_attn_out, 0), (w_pool_in, 0), (w_pool_group.reshape(1, 4 * pg_rows, POOL_DIM), 0),
              (w_ffn_gate_up, 0), (w_ffn_gate_up, 1), (w_ffn_down, 0), (w_ffn_down, 1)]
    for nm, (s, l) in zip(TENSORS[1:], shards):
        bufs[nm] = _cast_into_slot("cast_" + nm, s, l, chip_arr, BF16)
    bufs["pool_norm"] = _cast_into_slot("place_pool_norm", pool_norm[None], 0, chip_arr, F32)
    bufs["pool_scale"] = _cast_into_slot("place_pool_scale", pool_scale[None], 0, chip_arr, F32)

    comm = _MeshComm(bufs, chip_arr, core_arr, pg_rows)
    loss_part, grad_x, small = _local_step(x[0], loss_target[0], attn_norm, ffn_norm, final_norm.reshape(1, D), comm)
    g_w_qkv, g_w_o, g_w_p, g_w_pg, g_w_gu, g_w_d = [comm.blocks[k] for k in ("qkv", "o", "p", "pg", "gu", "d")]

    rows = jnp.concatenate([small["attn"], small["ffn0"], small["ffn1"], small["final"], small["pool"],
                            small["scale"], jnp.pad(loss_part, ((0, 0), (0, D - 1))), jnp.zeros((1, D), F32)], axis=0)
    all_rows = _gather_small("gather_gain_grads", rows)
    tot = _sum_leading("sum_gain_grads", all_rows, F32)
    loss = tot[6, 0]
    g_attn_norm = tot[0:1]
    g_ffn_norm = tot[1:3]
    g_final_norm = tot[3]
    g_pool_norm = lax.dynamic_slice(tot, (4, chip * POOL_DIM), (1, POOL_DIM))
    g_pool_scale = lax.dynamic_slice(tot, (5, chip * POOL_DIM), (1, POOL_DIM))

    def update(name, w, g, m, v):
        shape = w.shape
        cols = shape[-1]
        as2d = lambda a: a.reshape(-1, cols)
        return [a.reshape(shape) for a in _adamw("adamw_" + name, as2d(w), as2d(g), as2d(m), as2d(v))]

    results = [
        update("attn_norm", attn_norm, g_attn_norm, m_attn_norm, v_attn_norm),
        update("w_qkv", w_qkv, g_w_qkv, m_w_qkv, v_w_qkv),
        update("w_attn_out", w_attn_out, g_w_o, m_w_attn_out, v_w_attn_out),
        update("pool_norm", pool_norm, g_pool_norm, m_pool_norm, v_pool_norm),
        update("w_pool_in", w_pool_in, g_w_p, m_w_pool_in, v_w_pool_in),
        update("w_pool_group", w_pool_group, g_w_pg, m_w_pool_group, v_w_pool_group),
        update("pool_scale", pool_scale, g_pool_scale, m_pool_scale, v_pool_scale),
        update("ffn_norm", ffn_norm, g_ffn_norm, m_ffn_norm, v_ffn_norm),
        update("w_ffn_gate_up", w_ffn_gate_up, g_w_gu, m_w_ffn_gate_up, v_w_ffn_gate_up),
        update("w_ffn_down", w_ffn_down, g_w_d, m_w_ffn_down, v_w_ffn_down),
        update("final_norm", final_norm, g_final_norm, m_final_norm, v_final_norm),
    ]
    grads = [r[0] for r in results]
    deltas = [r[1] for r in results]
    new_m = [r[2] for r in results]
    new_v = [r[3] for r in results]
    return (loss, grad_x[None], *grads, *deltas, *new_m, *new_v)
```

```python
import functools

import jax
import jax.numpy as jnp
from jax import lax
from jax.experimental import pallas as pl
from jax.experimental.pallas import tpu as pltpu

F32 = jnp.float32
BF16 = jnp.bfloat16
MESH = pl.DeviceIdType.MESH

D_MODEL = 1024
N_HEADS = 8
HEAD_DIM = 128
Q_BLOCK = 128
ATTN_DILATIONS = (1, 4, 16)
N_GROUPS = 3
D_FF = 2816
N_CHIPS = 4
FF_SHARD = 2 * D_FF // N_CHIPS
QKV_SHARD = 3 * 3 * D_MODEL // N_CHIPS
QKV_TILE = 768
POOL_WINDOWS = (2, 4, 8, 16)
POOL_DIM = 256
POOL_HALO = 16
RMS_EPS = 1e-6
NEG = -1e30
ADAM_LR = 0.001
ADAM_B1 = 0.9
ADAM_B2 = 0.999
ADAM_EPS = 1e-08
ADAM_WD = 0.01
ADAM_STEP = 10
VMEM_LIMIT = 56 * 1024 * 1024

_DN = {
    "nn": (((1,), (0,)), ((), ())),
    "nt": (((1,), (1,)), ((), ())),
    "tn": (((0,), (0,)), ((), ())),
}


class _Rider:
    def __init__(self, operands, out_shapes, aliases, n_copies, copies=None, start=None, finish=None):
        self.operands = list(operands)
        self.out_shapes = list(out_shapes)
        self.aliases = dict(aliases)
        self.n_copies = n_copies
        self.results = None

        def start_copies(ins, outs, send, recv, base=0):
            for cp in copies(ins, outs, send, recv, base):
                cp.start()

        def wait_copies(ins, outs, send, recv, base=0):
            for cp in copies(ins, outs, send, recv, base):
                cp.wait()

        self.start = start or start_copies
        self.finish = finish or wait_copies


def _riders(*riders):
    operands, out_shapes, aliases, offsets = [], [], {}, []
    n = 0
    for r in riders:
        offsets.append((len(operands), len(out_shapes), n))
        aliases.update({len(operands) + i: len(out_shapes) + o for i, o in r.aliases.items()})
        operands += r.operands
        out_shapes += r.out_shapes
        n += r.n_copies

    def each(which):
        def go(ins, outs, send, recv, base=0):
            for r, (i0, o0, s0) in zip(riders, offsets):
                getattr(r, which)(ins[i0:i0 + len(r.operands)], outs[o0:o0 + len(r.out_shapes)], send, recv,
                                  base + s0)
        return go

    both = _Rider(operands, out_shapes, aliases, n, start=each("start"), finish=each("finish"))
    both.parts = (riders, offsets)
    return both


_pending_rider = []


def _ride(rider, fn, *args, **kw):
    _pending_rider.append(rider)
    out = fn(*args, **kw)
    assert not _pending_rider
    if hasattr(rider, "parts"):
        for r, (_, o0, _) in zip(*rider.parts):
            r.results = rider.results[o0:o0 + len(r.out_shapes)]
    return out


def _pcall(body, prefetch=0, **kw):
    def build(body, kw):
        if not prefetch:
            return pl.pallas_call(body, **kw)
        kw = dict(kw)
        grid_spec = pltpu.PrefetchScalarGridSpec(
            num_scalar_prefetch=prefetch, grid=kw.pop("grid"), in_specs=kw.pop("in_specs"),
            out_specs=kw.pop("out_specs"), scratch_shapes=kw.pop("scratch_shapes", []))
        return pl.pallas_call(body, grid_spec=grid_spec, **kw)

    if not _pending_rider:
        return build(body, kw)
    rider = _pending_rider.pop()
    grid = kw.get("grid", ())
    in_specs = list(kw.get("in_specs", []))
    single = not isinstance(kw["out_shape"], (list, tuple))
    out_shape = [kw["out_shape"]] if single else list(kw["out_shape"])
    out_specs = [kw["out_specs"]] if single else list(kw["out_specs"])
    scratch = list(kw.get("scratch_shapes", []))
    n_in, n_out, n_scr = len(in_specs), len(out_shape), len(scratch)
    r_in, r_out = len(rider.operands), len(rider.out_shapes)

    def wrapped(*refs):
        scalars, refs = refs[:prefetch], refs[prefetch:]
        ins, rins = refs[:n_in], refs[n_in:n_in + r_in]
        outs = refs[n_in + r_in:n_in + r_in + n_out]
        routs = refs[n_in + r_in + n_out:n_in + r_in + n_out + r_out]
        scr = refs[n_in + r_in + n_out + r_out:n_in + r_in + n_out + r_out + n_scr]
        send, recv = refs[-2:]
        ids = [pl.program_id(a) for a in range(len(grid))]
        first, last = True, True
        for a, pid in enumerate(ids):
            first = jnp.logical_and(first, pid == 0)
            last = jnp.logical_and(last, pid == grid[a] - 1)
        if grid:
            pl.when(first)(lambda: rider.start(rins, routs, send, recv))
        else:
            rider.start(rins, routs, send, recv)
        body(*scalars, *ins, *outs, *scr)
        if grid:
            pl.when(last)(lambda: rider.finish(rins, routs, send, recv))
        else:
            rider.finish(rins, routs, send, recv)

    any_spec = pl.BlockSpec(memory_space=pl.ANY)
    kw2 = dict(kw)
    kw2.update(
        in_specs=in_specs + [any_spec] * r_in, out_specs=out_specs + [any_spec] * r_out,
        out_shape=out_shape + rider.out_shapes,
        scratch_shapes=scratch + [pltpu.SemaphoreType.DMA((rider.n_copies,)),
                                  pltpu.SemaphoreType.DMA((rider.n_copies,))],
        input_output_aliases={**kw.get("input_output_aliases", {}),
                              **{prefetch + n_in + i: n_out + o for i, o in rider.aliases.items()}})
    if grid:
        kw2["compiler_params"] = _params(("arbitrary",) * len(grid))
    call = build(wrapped, kw2)

    def run(*operands):
        res = call(*operands, *rider.operands)
        rider.results = list(res[n_out:])
        return res[0] if single else list(res[:n_out])

    return run


def _params(sem):
    return pltpu.CompilerParams(dimension_semantics=sem, vmem_limit_bytes=VMEM_LIMIT)


def _dot(a, b, mode):
    return lax.dot_general(a, b, _DN[mode], preferred_element_type=F32)


def _mm(name, mode, grid, a, a_spec, b, b_spec, out_shape, o_spec, acc_shape, res=None, res_spec=None):
    nk = grid[-1]
    has_res = res is not None

    def body(*refs):
        a_ref, b_ref = refs[0], refs[1]
        res_ref = refs[2] if has_res else None
        o_ref = refs[2 + has_res]
        part = _dot(a_ref[...].astype(BF16), b_ref[...].astype(BF16), mode)
        if nk == 1:
            if has_res:
                part = part + res_ref[...]
            o_ref[...] = part.astype(o_ref.dtype)
            return
        acc_ref = refs[3 + has_res]
        k = pl.program_id(len(grid) - 1)

        @pl.when(k == 0)
        def _():
            acc_ref[...] = part

        @pl.when(k > 0)
        def _():
            acc_ref[...] += part

        @pl.when(k == nk - 1)
        def _():
            r = acc_ref[...]
            if has_res:
                r = r + res_ref[...]
            o_ref[...] = r.astype(o_ref.dtype)

    in_specs = [a_spec, b_spec] + ([res_spec] if has_res else [])
    operands = [a, b] + ([res] if has_res else [])
    scratch = [] if nk == 1 else [pltpu.VMEM(acc_shape, F32)]
    sem = ("parallel",) * (len(grid) - 1) + ("arbitrary",)
    return _pcall(body, name=name, grid=grid, in_specs=in_specs, out_specs=o_spec, out_shape=out_shape,
                  scratch_shapes=scratch, compiler_params=_params(sem))(*operands)


def _mm_rows(name, a, b, out_dtype, tm, res=None):
    M, K = a.shape
    N = b.shape[1]
    return _mm(name, "nn", (M // tm, 1), a, pl.BlockSpec((tm, K), lambda i, k: (i, 0)),
               b, pl.BlockSpec((K, N), lambda i, k: (0, 0)),
               jax.ShapeDtypeStruct((M, N), out_dtype), pl.BlockSpec((tm, N), lambda i, k: (i, 0)), None,
               res=res, res_spec=pl.BlockSpec((tm, N), lambda i, k: (i, 0)))


def _mm_nt_rows(name, a, b, tm):
    M, N = a.shape
    K = b.shape[0]
    return _mm(name, "nt", (M // tm, 1), a, pl.BlockSpec((tm, N), lambda i, k: (i, 0)),
               b, pl.BlockSpec((K, N), lambda i, k: (0, 0)),
               jax.ShapeDtypeStruct((M, K), F32), pl.BlockSpec((tm, K), lambda i, k: (i, 0)), None)


def _mm_tn(name, a, b, tm, tk):
    T, M = a.shape
    N = b.shape[1]
    return _mm(name, "tn", (M // tm, T // tk), a, pl.BlockSpec((tk, tm), lambda i, k: (k, i)),
               b, pl.BlockSpec((tk, N), lambda i, k: (k, 0)),
               jax.ShapeDtypeStruct((M, N), BF16), pl.BlockSpec((tm, N), lambda i, k: (i, 0)), (tm, N))


def _rms_fwd(name, x, g, tr=512):
    S, D = x.shape

    def body(x_ref, g_ref, o_ref):
        xf = x_ref[...]
        r = lax.rsqrt(jnp.mean(xf * xf, axis=-1, keepdims=True) + RMS_EPS)
        o_ref[...] = ((xf * r) * g_ref[...]).astype(o_ref.dtype)

    return _pcall(body, name=name, grid=(S // tr,),
                  in_specs=[pl.BlockSpec((tr, D), lambda i: (i, 0)), pl.BlockSpec((1, D), lambda i: (0, 0))],
                  out_specs=pl.BlockSpec((tr, D), lambda i: (i, 0)),
                  out_shape=jax.ShapeDtypeStruct((S, D), BF16), compiler_params=_params(("parallel",)))(x, g)


def _norm_bwd_rows(xf, gain, dh, dres):
    r = lax.rsqrt(jnp.mean(xf * xf, axis=-1, keepdims=True) + RMS_EPS)
    xh = xf * r
    t = dh * gain
    dx = dres + r * (t - xh * jnp.mean(t * xh, axis=-1, keepdims=True))
    return dx, jnp.sum(dh * xh, axis=0, keepdims=True)


def _accumulate(ref, value, first):
    @pl.when(first)
    def _():
        ref[...] = value

    @pl.when(jnp.logical_not(first))
    def _():
        ref[...] += value


def _rms_bwd_folded(name, x, g, dhf, dres, tb=512):
    S, D = x.shape

    def body(x_ref, g_ref, d0_ref, d1_ref, d2_ref, dres_ref, dx_ref, dg_ref, dh_ref):
        chunks = _lane_chunks(D)
        for c, cols in enumerate(chunks):
            dh_ref[c] = d0_ref[0, :, cols].astype(F32)
        for dil, ref in ((ATTN_DILATIONS[1], d1_ref), (ATTN_DILATIONS[2], d2_ref)):
            n = tb // dil
            for k in range(dil):
                for c, cols in enumerate(chunks):
                    dh_ref[c, _strided_rows(k, n, dil), :] += ref[k, :, cols].astype(F32)
        dh = jnp.concatenate([dh_ref[c] for c in range(len(chunks))], axis=1)
        dx, dg = _norm_bwd_rows(x_ref[...], g_ref[...], dh, dres_ref[...])
        dx_ref[...] = dx
        _accumulate(dg_ref, dg, pl.program_id(0) == 0)

    views, specs = _folded_views(dhf, tb)
    row = pl.BlockSpec((tb, D), lambda i: (i, 0))
    vec = pl.BlockSpec((1, D), lambda i: (0, 0))
    return _pcall(body, name=name, grid=(S // tb,), in_specs=[row, vec] + specs + [row], out_specs=[row, vec],
                  out_shape=[jax.ShapeDtypeStruct((S, D), F32), jax.ShapeDtypeStruct((1, D), F32)],
                  scratch_shapes=[pltpu.VMEM((D // LANES, tb, LANES), F32)],
                  compiler_params=_params(("arbitrary",)))(x, g, *views, dres)


def _proj_res_norm(name, a, w, res, gain, tm=512):
    M, K = a.shape
    D = w.shape[1]

    def body(a_ref, w_ref, res_ref, g_ref, x_ref, h_ref):
        xf = res_ref[...] + _dot(a_ref[...], w_ref[...], "nn")
        x_ref[...] = xf
        r = lax.rsqrt(jnp.mean(xf * xf, axis=-1, keepdims=True) + RMS_EPS)
        h_ref[...] = ((xf * r) * g_ref[...]).astype(h_ref.dtype)

    row = pl.BlockSpec((tm, D), lambda i: (i, 0))
    return _pcall(body, name=name, grid=(M // tm,),
                  in_specs=[pl.BlockSpec((tm, K), lambda i: (i, 0)), pl.BlockSpec((K, D), lambda i: (0, 0)), row,
                            pl.BlockSpec((1, D), lambda i: (0, 0))],
                  out_specs=[row, row],
                  out_shape=[jax.ShapeDtypeStruct((M, D), F32), jax.ShapeDtypeStruct((M, D), BF16)],
                  compiler_params=_params(("parallel",)))(a, w, res, gain)


def _proj_res_loss(name, a, w, res, gain, tgt, tm=512):
    M, K = a.shape
    D = w.shape[1]

    def body(a_ref, w_ref, res_ref, g_ref, t_ref, loss_ref, dx_ref, dg_ref):
        first = pl.program_id(0) == 0
        xf = res_ref[...] + _dot(a_ref[...], w_ref[...], "nn")
        r = lax.rsqrt(jnp.mean(xf * xf, axis=-1, keepdims=True) + RMS_EPS)
        xh = xf * r
        gv = g_ref[...]
        e = xh * gv - t_ref[...]
        lp = 0.5 * jnp.sum(jnp.mean(e * e, axis=-1, keepdims=True), axis=0, keepdims=True)
        dy = e * (1.0 / D)
        t = dy * gv
        dx_ref[...] = r * (t - xh * jnp.mean(t * xh, axis=-1, keepdims=True))
        _accumulate(dg_ref, jnp.sum(dy * xh, axis=0, keepdims=True), first)
        _accumulate(loss_ref, lp, first)

    row = pl.BlockSpec((tm, D), lambda i: (i, 0))
    vec = pl.BlockSpec((1, D), lambda i: (0, 0))
    return _pcall(body, name=name, grid=(M // tm,),
                  in_specs=[pl.BlockSpec((tm, K), lambda i: (i, 0)), pl.BlockSpec((K, D), lambda i: (0, 0)), row,
                            vec, row],
                  out_specs=[pl.BlockSpec((1, 1), lambda i: (0, 0)), row, vec],
                  out_shape=[jax.ShapeDtypeStruct((1, 1), F32), jax.ShapeDtypeStruct((M, D), F32),
                             jax.ShapeDtypeStruct((1, D), F32)],
                  compiler_params=_params(("arbitrary",)))(a, w, res, gain, tgt)


def _dh_norm_bwd(name, d, w, x, gain, dres, tm=512):
    M, N = d.shape
    D = w.shape[0]

    def body(d_ref, w_ref, x_ref, g_ref, dres_ref, dx_ref, dg_ref):
        dh = _dot(d_ref[...].astype(BF16), w_ref[...], "nt")
        dx, dg = _norm_bwd_rows(x_ref[...], g_ref[...], dh, dres_ref[...])
        dx_ref[...] = dx
        _accumulate(dg_ref, dg, pl.program_id(0) == 0)

    row = pl.BlockSpec((tm, D), lambda i: (i, 0))
    vec = pl.BlockSpec((1, D), lambda i: (0, 0))
    return _pcall(body, name=name, grid=(M // tm,),
                  in_specs=[pl.BlockSpec((tm, N), lambda i: (i, 0)), pl.BlockSpec((D, N), lambda i: (0, 0)), row, vec,
                            row],
                  out_specs=[row, vec],
                  out_shape=[jax.ShapeDtypeStruct((M, D), F32), jax.ShapeDtypeStruct((1, D), F32)],
                  compiler_params=_params(("arbitrary",)))(d, w, x, gain, dres)


def _sigmoid(v):
    return 0.5 * jnp.tanh(0.5 * v) + 0.5


def _ffn_up(name, h, w_gu, tm=512):
    S, D = h.shape
    W = FF_SHARD

    def body(h_ref, wg_ref, wu_ref, gu_ref, act_ref):
        hv = h_ref[...]
        gate = _dot(hv, wg_ref[...], "nn")
        up = _dot(hv, wu_ref[...], "nn")
        gu_ref[0] = gate.astype(gu_ref.dtype)
        gu_ref[1] = up.astype(gu_ref.dtype)
        sig = _sigmoid(gate)
        act_ref[...] = ((gate * sig) * up).astype(act_ref.dtype)

    return _pcall(
        body, name=name, grid=(2, S // tm),
        in_specs=[pl.BlockSpec((tm, D), lambda j, i: (i, 0)),
                  pl.BlockSpec((None, D, W), lambda j, i: (j, 0, 0)),
                  pl.BlockSpec((None, D, W), lambda j, i: (j + 2, 0, 0))],
        out_specs=[pl.BlockSpec((2, tm, W), lambda j, i: (0, i, j)), pl.BlockSpec((tm, W), lambda j, i: (i, j))],
        out_shape=[jax.ShapeDtypeStruct((2, S, D_FF), BF16), jax.ShapeDtypeStruct((S, D_FF), BF16)],
        compiler_params=_params(("parallel", "parallel")))(h, w_gu, w_gu)


def _ffn_down_bwd(name, dx, w_down, gu, tm=512):
    S, D = dx.shape
    W = FF_SHARD

    def body(dx_ref, w_ref, gu_ref, dgu_ref):
        dact = _dot(dx_ref[...].astype(BF16), w_ref[...], "nt")
        gate = gu_ref[0].astype(F32)
        up = gu_ref[1].astype(F32)
        sig = _sigmoid(gate)
        silu = gate * sig
        dgu_ref[0] = (dact * up * (sig * (1.0 + gate * (1.0 - sig)))).astype(dgu_ref.dtype)
        dgu_ref[1] = (dact * silu).astype(dgu_ref.dtype)

    blk = pl.BlockSpec((2, tm, W), lambda j, i: (0, i, j))
    return _pcall(
        body, name=name, grid=(2, S // tm),
        in_specs=[pl.BlockSpec((tm, D), lambda j, i: (i, 0)), pl.BlockSpec((W, D), lambda j, i: (j, 0)), blk],
        out_specs=blk, out_shape=jax.ShapeDtypeStruct((2, S, D_FF), BF16),
        compiler_params=_params(("parallel", "parallel")))(dx, w_down, gu)


def _ffn_dw_up(name, h, dgu, tk=2048):
    S, D = h.shape
    W = FF_SHARD
    tk = min(tk, S)
    return _mm(name, "tn", (N_CHIPS, S // tk), h, pl.BlockSpec((tk, D), lambda s, k: (k, 0)),
               dgu, pl.BlockSpec((None, tk, W), lambda s, k: (s // 2, k, s % 2)),
               jax.ShapeDtypeStruct((N_CHIPS, D, W), BF16), pl.BlockSpec((None, D, W), lambda s, k: (s, 0, 0)),
               (D, W))


def _ffn_dh(name, dgu, w_gu, x, gain, dres, tm=512):
    S = dgu.shape[1]
    W = FF_SHARD

    def body(a_ref, w_hbm, x_ref, g_ref, dres_ref, dx_ref, dg_ref, w_ref, acat_ref, sems):
        first = pl.program_id(0) == 0

        @pl.when(first)
        def _():
            copies = [pltpu.make_async_copy(w_hbm.at[s], w_ref.at[:, pl.ds(s * W, W)], sems.at[s])
                      for s in range(N_CHIPS)]
            for cp in copies:
                cp.start()
            for cp in copies:
                cp.wait()

        acat_ref[:, :D_FF] = a_ref[0]
        acat_ref[:, D_FF:] = a_ref[1]
        dh = _dot(acat_ref[...], w_ref[...], "nt")
        dx, dg = _norm_bwd_rows(x_ref[...], g_ref[...], dh, dres_ref[...])
        dx_ref[...] = dx
        _accumulate(dg_ref, dg, first)

    row = pl.BlockSpec((tm, D_MODEL), lambda i: (i, 0))
    vec = pl.BlockSpec((1, D_MODEL), lambda i: (0, 0))
    return _pcall(body, name=name, grid=(S // tm,),
                  in_specs=[pl.BlockSpec((2, tm, D_FF), lambda i: (0, i, 0)), pl.BlockSpec(memory_space=pl.ANY),
                            row, vec, row],
                  out_specs=[row, vec],
                  out_shape=[jax.ShapeDtypeStruct((S, D_MODEL), F32), jax.ShapeDtypeStruct((1, D_MODEL), F32)],
                  scratch_shapes=[pltpu.VMEM((D_MODEL, 2 * D_FF), BF16), pltpu.VMEM((tm, 2 * D_FF), BF16),
                                  pltpu.SemaphoreType.DMA((N_CHIPS,))],
                  compiler_params=_params(("arbitrary",)))(dgu, w_gu, x, gain, dres)


FOLD_TOKENS = 1024
LANES = 128


def _lane_chunks(width):
    return [slice(c * LANES, (c + 1) * LANES) for c in range(width // LANES)]


def _strided_rows(r, n, dil):
    return pl.ds(r, n) if dil == 1 else pl.ds(r, n, stride=dil)


def _norm_fold(name, g, x, gain, hf):
    S, D = x.shape
    dil = ATTN_DILATIONS[g]
    n = FOLD_TOKENS // dil
    seg = S // dil

    def body(*refs):
        x_ref, g_ref = refs[:2]
        hf_ref, xc_ref = refs[-2:]
        xf = x_ref[...]
        inv = lax.rsqrt(jnp.mean(xf * xf, axis=-1, keepdims=True) + RMS_EPS)
        h = (xf * inv) * g_ref[...]
        if dil == 1:
            hf_ref[0] = h.astype(BF16)
            return
        for c, cols in enumerate(_lane_chunks(D)):
            xc_ref[c] = h[:, cols]
        for r in range(dil):
            for c, cols in enumerate(_lane_chunks(D)):
                hf_ref[r, :, cols] = xc_ref[c, _strided_rows(r, n, dil), :].astype(BF16)

    in_specs = [pl.BlockSpec((FOLD_TOKENS, D), lambda i: (i, 0)), pl.BlockSpec((1, D), lambda i: (0, 0))]
    operands = [x, gain]
    aliases = {}
    if hf is not None:
        in_specs.append(pl.BlockSpec(memory_space=pl.ANY))
        operands.append(hf.reshape(N_GROUPS * dil, seg, D))
        aliases = {2: 0}
    hf = _pcall(body, name=name, grid=(S // FOLD_TOKENS,), in_specs=in_specs,
                out_specs=pl.BlockSpec((dil, n, D), lambda i: (g, i, 0)),
                out_shape=jax.ShapeDtypeStruct((N_GROUPS * dil, seg, D), BF16),
                scratch_shapes=[pltpu.VMEM((D // LANES, FOLD_TOKENS, LANES), F32)], input_output_aliases=aliases,
                compiler_params=_params(("arbitrary",)))(*operands)
    return hf.reshape(N_GROUPS, S, D)


def _qkv_tiles(name, piece, hf, w, qkv, chip, tm=1024):
    S = hf.shape[1]
    per_group = 3 * D_MODEL // QKV_TILE

    def tile(q, c_ref):
        if piece is None:
            return QKV_PIECES * c_ref[0] + q
        return QKV_PIECES * ((c_ref[0] + 1 + q) % N_CHIPS) + piece

    def body(*refs):
        a_ref, w_ref, o_ref = refs[1], refs[2], refs[-1]
        o_ref[...] = _dot(a_ref[...], w_ref[...], "nn").astype(o_ref.dtype)

    if piece is None:
        w_spec = pl.BlockSpec((D_MODEL, QKV_TILE), lambda q, i, c_ref: (0, q))
    else:
        w_spec = pl.BlockSpec((None, D_MODEL, QKV_TILE), lambda q, i, c_ref: ((c_ref[0] + 1 + q) % N_CHIPS, 0, 0))
    in_specs = [pl.BlockSpec((None, tm, D_MODEL), lambda q, i, c_ref: (tile(q, c_ref) // per_group, i, 0)), w_spec]
    operands = [chip, hf, w]
    aliases = {}
    if qkv is not None:
        in_specs.append(pl.BlockSpec(memory_space=pl.ANY))
        operands.append(qkv)
        aliases = {3: 0}
    return _pcall(
        body, prefetch=1, name=name, grid=(N_CHIPS - 1, S // tm), in_specs=in_specs,
        out_specs=pl.BlockSpec((None, tm, QKV_TILE),
                               lambda q, i, c_ref: (tile(q, c_ref) // per_group, i, tile(q, c_ref) % per_group)),
        out_shape=jax.ShapeDtypeStruct((N_GROUPS, S, 3 * D_MODEL), BF16), input_output_aliases=aliases,
        compiler_params=_params(("arbitrary", "arbitrary")))(*operands)


QKV_PIECES = QKV_SHARD // QKV_TILE


def _qkv_dw(name, p, hf, dqkv):
    S = hf.shape[1]
    per_group = 3 * D_MODEL // QKV_TILE
    tile = lambda s: QKV_PIECES * s + p
    return _mm(name, "tn", (N_CHIPS, 1),
               hf, pl.BlockSpec((None, S, D_MODEL), lambda s, k: (tile(s) // per_group, 0, 0)),
               dqkv, pl.BlockSpec((None, S, QKV_TILE), lambda s, k: (tile(s) // per_group, 0, tile(s) % per_group)),
               jax.ShapeDtypeStruct((N_CHIPS, D_MODEL, QKV_TILE), BF16),
               pl.BlockSpec((None, D_MODEL, QKV_TILE), lambda s, k: (s, 0, 0)), None)


def _qkv_dh(name, g, dqkv, w_pieces, dhf, tm=1024):
    S = dqkv.shape[1]
    per_group = 3 * D_MODEL // QKV_TILE

    def body(*refs):
        a_ref, w_hbm = refs[0], refs[1:1 + QKV_PIECES]
        o_ref, w_ref, sems = refs[-3:]

        @pl.when(pl.program_id(0) == 0)
        def _():
            copies = []
            for j in range(per_group):
                t = per_group * g + j
                copies.append(pltpu.make_async_copy(w_hbm[t % QKV_PIECES].at[t // QKV_PIECES],
                                                    w_ref.at[:, pl.ds(j * QKV_TILE, QKV_TILE)], sems.at[j]))
            for cp in copies:
                cp.start()
            for cp in copies:
                cp.wait()

        o_ref[...] = _dot(a_ref[...], w_ref[...], "nt").astype(o_ref.dtype)

    any_spec = pl.BlockSpec(memory_space=pl.ANY)
    in_specs = [pl.BlockSpec((None, tm, 3 * D_MODEL), lambda i: (g, i, 0))] + [any_spec] * QKV_PIECES
    operands = [dqkv] + list(w_pieces)
    aliases = {}
    if dhf is not None:
        in_specs.append(any_spec)
        operands.append(dhf)
        aliases = {1 + QKV_PIECES: 0}
    return _pcall(body, name=name, grid=(S // tm,), in_specs=in_specs,
                  out_specs=pl.BlockSpec((None, tm, D_MODEL), lambda i: (g, i, 0)),
                  out_shape=jax.ShapeDtypeStruct((N_GROUPS, S, D_MODEL), BF16),
                  scratch_shapes=[pltpu.VMEM((D_MODEL, 3 * D_MODEL), BF16), pltpu.SemaphoreType.DMA((per_group,))],
                  input_output_aliases=aliases, compiler_params=_params(("arbitrary",)))(*operands)


def _alibi_coef(g, h):
    vals = [-(2.0 ** (-8.0 * (gg * N_HEADS + h + 1) / (N_GROUPS * N_HEADS))) * ATTN_DILATIONS[gg]
            for gg in range(N_GROUPS)]
    return jnp.where(g == 0, vals[0], jnp.where(g == 1, vals[1], vals[2])).astype(F32)


def _blocks_per_segment(g, S):
    return jnp.right_shift(S // Q_BLOCK, 2 * g)


def _band_bias(pen):
    row = lax.broadcasted_iota(jnp.int32, (Q_BLOCK, 2 * Q_BLOCK), 0)
    col = lax.broadcasted_iota(jnp.int32, (Q_BLOCK, 2 * Q_BLOCK), 1)
    dist = Q_BLOCK + row - col
    valid = jnp.logical_and(dist >= 0, dist <= Q_BLOCK)
    base = jnp.where(valid, jnp.where(col < Q_BLOCK, pen, 0.0), NEG).astype(F32)
    return base, dist.astype(F32)


def _skewed(n_units, stages):
    state = {}
    for step in range(n_units + len(stages) - 1):
        for s, stage in enumerate(stages):
            u = step - s
            if 0 <= u < n_units:
                state[u] = stage(u, state.get(u))
                if s == len(stages) - 1:
                    del state[u]


def _attn_fwd(name, qkv, tq=512):
    S = qkv.shape[1]
    nqb = tq // Q_BLOCK
    scale = HEAD_DIM ** -0.5

    def body(q_ref, k_ref, kp_ref, v_ref, vp_ref, o_ref, lse_ref, kf_ref, vf_ref):
        g = pl.program_id(0)
        i = pl.program_id(1)
        nb = _blocks_per_segment(g, S)
        kf_ref[:Q_BLOCK] = kp_ref[...]
        kf_ref[Q_BLOCK:] = k_ref[...]
        vf_ref[:Q_BLOCK] = vp_ref[...]
        vf_ref[Q_BLOCK:] = v_ref[...]
        coefs = [_alibi_coef(g, h) for h in range(N_HEADS)]
        units = [(b, h) for b in range(nqb) for h in range(N_HEADS)]
        bias = {}

        def scores(u, _):
            b, h = units[u]
            if b not in bias:
                pen = jnp.where((i * nqb + b) % nb != 0, 0.0, NEG).astype(F32)
                bias.clear()
                bias[b] = _band_bias(pen)
            base, dist = bias[b]
            cols = slice(h * HEAD_DIM, (h + 1) * HEAD_DIM)
            q = q_ref[b * Q_BLOCK:(b + 1) * Q_BLOCK, cols]
            kk = kf_ref[b * Q_BLOCK:(b + 2) * Q_BLOCK, cols]
            return _dot(q, kk, "nt") * scale + (coefs[h] * dist + base)

        def softmax(u, s):
            m = jnp.max(s, axis=-1, keepdims=True)
            p = jnp.exp(s - m)
            den = jnp.sum(p, axis=-1, keepdims=True)
            return (p * (1.0 / den)).astype(BF16), m + jnp.log(den)

        def output(u, st):
            b, h = units[u]
            pn, lse = st
            cols = slice(h * HEAD_DIM, (h + 1) * HEAD_DIM)
            rows = slice(b * Q_BLOCK, (b + 1) * Q_BLOCK)
            o_ref[rows, cols] = _dot(pn, vf_ref[b * Q_BLOCK:(b + 2) * Q_BLOCK, cols], "nn").astype(o_ref.dtype)
            lse_ref[rows, h:h + 1] = lse

        _skewed(len(units), [scores, softmax, output])

    main = lambda c: pl.BlockSpec((None, tq, D_MODEL), lambda g, i: (g, i, c))
    prev = lambda c: pl.BlockSpec((None, Q_BLOCK, D_MODEL), lambda g, i: (g, jnp.maximum(i * nqb - 1, 0), c))
    return _pcall(
        body, name=name, grid=(N_GROUPS, S // tq),
        in_specs=[main(0), main(1), prev(1), main(2), prev(2)],
        out_specs=[pl.BlockSpec((None, tq, D_MODEL), lambda g, i: (g, i, 0)),
                   pl.BlockSpec((None, tq, N_HEADS), lambda g, i: (g, i, 0))],
        out_shape=[jax.ShapeDtypeStruct((N_GROUPS, S, D_MODEL), BF16),
                   jax.ShapeDtypeStruct((N_GROUPS, S, N_HEADS), F32)],
        scratch_shapes=[pltpu.VMEM((tq + Q_BLOCK, D_MODEL), BF16), pltpu.VMEM((tq + Q_BLOCK, D_MODEL), BF16)],
        compiler_params=_params(("parallel", "parallel")))(qkv, qkv, qkv, qkv, qkv)


def _attn_bwd(name, qkv, do, lse, dl, tq=512):
    S = qkv.shape[1]
    nqb = tq // Q_BLOCK
    n_blocks = S // Q_BLOCK
    scale = HEAD_DIM ** -0.5
    KEYS = 2 * Q_BLOCK

    def body(q_ref, k_ref, v_ref, kp_ref, vp_ref, qn_ref, do_ref, don_ref, lse_ref, lsen_ref, dl_ref, dln_ref,
             out_ref, kf_ref, vf_ref, dk_ref, dv_ref):
        g = pl.program_id(0)
        i = pl.program_id(1)
        nb = _blocks_per_segment(g, S)
        kf_ref[:Q_BLOCK] = kp_ref[...]
        kf_ref[Q_BLOCK:] = k_ref[...]
        vf_ref[:Q_BLOCK] = vp_ref[...]
        vf_ref[Q_BLOCK:] = v_ref[...]
        coefs = [_alibi_coef(g, h) for h in range(N_HEADS)]
        units = [(h, b) for h in range(N_HEADS) for b in range(nqb + 1)]
        bias = {}

        def band(b):
            if b not in bias:
                blk = i * nqb + b
                ok = blk % nb != 0
                if b == nqb:
                    ok = jnp.logical_and(ok, blk < n_blocks)
                bias[b] = _band_bias(jnp.where(ok, 0.0, NEG).astype(F32))
            return bias[b]

        def operands(h, b):
            cols = slice(h * HEAD_DIM, (h + 1) * HEAD_DIM)
            if b == nqb:
                keys = slice(tq, tq + Q_BLOCK)
                return (qn_ref[:, cols], don_ref[:, cols], lsen_ref[:, h:h + 1], dln_ref[:, h:h + 1],
                        kf_ref[keys, cols], vf_ref[keys, cols])
            rows = slice(b * Q_BLOCK, (b + 1) * Q_BLOCK)
            keys = slice(b * Q_BLOCK, b * Q_BLOCK + KEYS)
            return (q_ref[rows, cols], do_ref[rows, cols], lse_ref[rows, h:h + 1], dl_ref[rows, h:h + 1],
                    kf_ref[keys, cols], vf_ref[keys, cols])

        def probs(u, _):
            h, b = units[u]
            q, do_b, lse_b, dl_b, kk, vv = operands(h, b)
            base, dist = band(b)
            if b == nqb:
                base, dist = base[:, :Q_BLOCK], dist[:, :Q_BLOCK]
            p = jnp.exp(_dot(q, kk, "nt") * scale + (coefs[h] * dist + base) - lse_b)
            return p, _dot(do_b, vv, "nt")

        def dscores(u, st):
            h, b = units[u]
            p, dp = st
            dl_b = operands(h, b)[3]
            return ((p * (dp - dl_b)) * scale).astype(BF16), p.astype(BF16)

        def grads(u, st):
            h, b = units[u]
            ds, pb = st
            q, do_b, _, _, kk, _ = operands(h, b)
            cols = slice(h * HEAD_DIM, (h + 1) * HEAD_DIM)
            if b < nqb:
                out_ref[b * Q_BLOCK:(b + 1) * Q_BLOCK, cols] = _dot(ds, kk, "nn").astype(out_ref.dtype)
            if b == 0:
                dk_ref[:Q_BLOCK] = _dot(ds[:, Q_BLOCK:], q, "tn")
                dv_ref[:Q_BLOCK] = _dot(pb[:, Q_BLOCK:], do_b, "tn")
                return None
            dk = _dot(ds, q, "tn")
            dv = _dot(pb, do_b, "tn")
            before = slice((b - 1) * Q_BLOCK, b * Q_BLOCK)
            dk_ref[before] += dk[:Q_BLOCK]
            dv_ref[before] += dv[:Q_BLOCK]
            if b < nqb:
                own = slice(b * Q_BLOCK, (b + 1) * Q_BLOCK)
                dk_ref[own] = dk[Q_BLOCK:]
                dv_ref[own] = dv[Q_BLOCK:]
            else:
                out_ref[:, D_MODEL + h * HEAD_DIM:D_MODEL + (h + 1) * HEAD_DIM] = dk_ref[...].astype(out_ref.dtype)
                out_ref[:, 2 * D_MODEL + h * HEAD_DIM:2 * D_MODEL + (h + 1) * HEAD_DIM] = (
                    dv_ref[...].astype(out_ref.dtype))
            return None

        _skewed(len(units), [probs, dscores, grads])

    nxt_blk = lambda i: jnp.minimum((i + 1) * nqb, n_blocks - 1)
    main = lambda c: pl.BlockSpec((None, tq, D_MODEL), lambda g, i: (g, i, c))
    prev = lambda c: pl.BlockSpec((None, Q_BLOCK, D_MODEL), lambda g, i: (g, jnp.maximum(i * nqb - 1, 0), c))
    row = pl.BlockSpec((None, tq, D_MODEL), lambda g, i: (g, i, 0))
    row_n = pl.BlockSpec((None, Q_BLOCK, D_MODEL), lambda g, i: (g, nxt_blk(i), 0))
    col = pl.BlockSpec((None, tq, N_HEADS), lambda g, i: (g, i, 0))
    col_n = pl.BlockSpec((None, Q_BLOCK, N_HEADS), lambda g, i: (g, nxt_blk(i), 0))
    return _pcall(
        body, name=name, grid=(N_GROUPS, S // tq),
        in_specs=[main(0), main(1), main(2), prev(1), prev(2), row_n, row, row_n, col, col_n, col, col_n],
        out_specs=pl.BlockSpec((None, tq, 3 * D_MODEL), lambda g, i: (g, i, 0)),
        out_shape=jax.ShapeDtypeStruct((N_GROUPS, S, 3 * D_MODEL), BF16),
        scratch_shapes=[pltpu.VMEM((tq + Q_BLOCK, D_MODEL), BF16), pltpu.VMEM((tq + Q_BLOCK, D_MODEL), BF16),
                        pltpu.VMEM((tq, HEAD_DIM), F32), pltpu.VMEM((tq, HEAD_DIM), F32)],
        compiler_params=_params(("parallel", "parallel")))(qkv, qkv, qkv, qkv, qkv, qkv, do, do, lse, lse, dl, dl)


def _group_weights(lse_ref):
    l0, l1, l2 = lse_ref[0], lse_ref[1], lse_ref[2]
    m = jnp.maximum(jnp.maximum(l0, l1), l2)
    e = [jnp.exp(l0 - m), jnp.exp(l1 - m), jnp.exp(l2 - m)]
    den = e[0] + e[1] + e[2]
    return [ei / den for ei in e]


MERGE_TOKENS = 512


def _folded_views(a, tb):
    _, S, C = a.shape
    views, specs = [], []
    for g, dil in enumerate(ATTN_DILATIONS):
        views.append(a.reshape(N_GROUPS * dil, S // dil, C))
        specs.append(pl.BlockSpec((dil, tb // dil, C), lambda i, g=g: (g, i, 0)))
    return views, specs


def _unfold_into(dst_ref, folded_refs):
    for g, (dil, ref) in enumerate(zip(ATTN_DILATIONS, folded_refs)):
        n = ref.shape[1]
        for r in range(dil):
            for c, cols in enumerate(_lane_chunks(ref.shape[2])):
                dst_ref[g, c, _strided_rows(r, n, dil), :] = ref[r, :, cols].astype(F32)


def _merge_fwd(name, o, lse, tb=MERGE_TOKENS):
    S = o.shape[1]

    def body(o0_ref, o1_ref, o2_ref, lse_ref, out_ref, o_ref):
        _unfold_into(o_ref, (o0_ref, o1_ref, o2_ref))
        w = _group_weights(lse_ref)
        for h in range(N_HEADS):
            cols = slice(h * HEAD_DIM, (h + 1) * HEAD_DIM)
            acc = w[0][:, h:h + 1] * o_ref[0, h]
            for gg in range(1, N_GROUPS):
                acc = acc + w[gg][:, h:h + 1] * o_ref[gg, h]
            out_ref[:, cols] = acc.astype(out_ref.dtype)

    views, specs = _folded_views(o, tb)
    return _pcall(body, name=name, grid=(S // tb,),
                  in_specs=specs + [pl.BlockSpec((N_GROUPS, tb, N_HEADS), lambda i: (0, i, 0))],
                  out_specs=pl.BlockSpec((tb, D_MODEL), lambda i: (i, 0)),
                  out_shape=jax.ShapeDtypeStruct((S, D_MODEL), BF16),
                  scratch_shapes=[pltpu.VMEM((N_GROUPS, N_HEADS, tb, HEAD_DIM), F32)],
                  compiler_params=_params(("parallel",)))(*views, lse)


def _merge_bwd(name, d_out, o, lse, tb=MERGE_TOKENS):
    S = o.shape[1]
    n_chunks = sum(ATTN_DILATIONS)

    def body(d_ref, o0_ref, o1_ref, o2_ref, lse_ref, do_hbm, dl_ref, o_ref, dt_ref, df_ref, sems):
        i = pl.program_id(0)
        _unfold_into(o_ref, (o0_ref, o1_ref, o2_ref))
        w = _group_weights(lse_ref)
        for h in range(N_HEADS):
            cols = slice(h * HEAD_DIM, (h + 1) * HEAD_DIM)
            dv = d_ref[:, cols]
            merged = w[0][:, h:h + 1] * o_ref[0, h]
            for gg in range(1, N_GROUPS):
                merged = merged + w[gg][:, h:h + 1] * o_ref[gg, h]
            dsum = jnp.sum(dv * merged, axis=-1, keepdims=True)
            for gg in range(N_GROUPS):
                wg = w[gg][:, h:h + 1]
                dt_ref[gg, h] = wg * dv
                dl_ref[gg, :, h:h + 1] = wg * dsum
        copies = []
        for gg, dil in enumerate(ATTN_DILATIONS):
            n = tb // dil
            for r in range(dil):
                for h, cols in enumerate(_lane_chunks(D_MODEL)):
                    df_ref[gg, r * n:(r + 1) * n, cols] = (
                        dt_ref[gg, h, _strided_rows(r, n, dil), :].astype(df_ref.dtype))
                cp = pltpu.make_async_copy(df_ref.at[gg, pl.ds(r * n, n)],
                                           do_hbm.at[gg, pl.ds(r * (S // dil) + i * n, n)], sems.at[len(copies)])
                cp.start()
                copies.append(cp)
        for cp in copies:
            cp.wait()

    views, specs = _folded_views(o, tb)
    small = pl.BlockSpec((N_GROUPS, tb, N_HEADS), lambda i: (0, i, 0))
    return _pcall(body, name=name, grid=(S // tb,),
                  in_specs=[pl.BlockSpec((tb, D_MODEL), lambda i: (i, 0))] + specs + [small],
                  out_specs=[pl.BlockSpec(memory_space=pl.ANY), small],
                  out_shape=[jax.ShapeDtypeStruct((N_GROUPS, S, D_MODEL), BF16),
                             jax.ShapeDtypeStruct((N_GROUPS, S, N_HEADS), F32)],
                  scratch_shapes=[pltpu.VMEM((N_GROUPS, N_HEADS, tb, HEAD_DIM), F32),
                                  pltpu.VMEM((N_GROUPS, N_HEADS, tb, HEAD_DIM), F32),
                                  pltpu.VMEM((N_GROUPS, tb, D_MODEL), BF16), pltpu.SemaphoreType.DMA((n_chunks,))],
                  compiler_params=_params(("arbitrary",)))(d_out, *views, lse)


def _shift_rows(a, k):
    return pltpu.roll(a, k % a.shape[0], axis=0)


def _pool_level(g, levels):
    return jnp.where(g == 0, levels[0], jnp.where(g == 1, levels[1], jnp.where(g == 2, levels[2], levels[3])))


def _pool_fwd(name, u, w_group, scale, x_in, tr=1024):
    S, D = u.shape
    H = POOL_HALO

    def body(u_ref, halo_ref, w_ref, sc_ref, x_ref, y_ref, z_ref, xo_ref):
        g = pl.program_id(0)
        i = pl.program_id(1)
        uv = u_ref[...]
        halo = jnp.where(i > 0, halo_ref[...], 0.0)
        s = jnp.concatenate([halo, uv], axis=0)
        levels = []
        for k in (1, 2, 4, 8):
            s = s + _shift_rows(s, k)
            levels.append(s[H:])
        t = i * tr + lax.broadcasted_iota(jnp.int32, (tr, 1), 0)
        cnt = jnp.minimum(t + 1, jnp.left_shift(2, g)).astype(F32)
        y = _pool_level(g, levels) / cnt - uv
        yb = y.astype(BF16)
        z = _dot(yb, w_ref[...], "nn")
        y_ref[...] = yb
        z_ref[...] = z.astype(z_ref.dtype)
        xo_ref[...] = x_ref[...] + z * sc_ref[...]

    blk = pl.BlockSpec((tr, POOL_DIM), lambda g, i: (i, g))
    return _pcall(
        body, name=name, grid=(D // POOL_DIM, S // tr),
        in_specs=[blk, pl.BlockSpec((H, POOL_DIM), lambda g, i: (jnp.maximum(i * (tr // H) - 1, 0), g)),
                  pl.BlockSpec((None, POOL_DIM, POOL_DIM), lambda g, i: (g, 0, 0)),
                  pl.BlockSpec((1, POOL_DIM), lambda g, i: (0, g)), blk],
        out_specs=[blk, blk, blk],
        out_shape=[jax.ShapeDtypeStruct((S, D), BF16), jax.ShapeDtypeStruct((S, D), BF16),
                   jax.ShapeDtypeStruct((S, D), F32)],
        compiler_params=_params(("parallel", "parallel")))(u, u, w_group, scale, x_in)


def _pool_bwd(name, d_out, z, y, scale, w_group, tr=1024):
    S, D = d_out.shape
    H = POOL_HALO

    def body(d_ref, dn_ref, z_ref, y_ref, sc_ref, w_ref, du_ref, dw_ref, dsc_ref):
        g = pl.program_id(0)
        i = pl.program_id(1)
        dv = d_ref[...]
        dz = jnp.concatenate([dv, dn_ref[...]], axis=0) * sc_ref[...]
        dzb = dz.astype(BF16)
        dy = _dot(dzb, w_ref[...], "nt")
        t = i * tr + lax.broadcasted_iota(jnp.int32, (tr + H, 1), 0)
        cnt = jnp.minimum(t + 1, jnp.left_shift(2, g)).astype(F32)
        s = jnp.where(t < S, dy / cnt, 0.0)
        levels = []
        for k in (1, 2, 4, 8):
            s = s + _shift_rows(s, -k)
            levels.append(s[:tr])
        du_ref[...] = (_pool_level(g, levels) - dy[:tr]).astype(du_ref.dtype)
        dw = _dot(y_ref[...], dzb[:tr], "tn")
        dsc = jnp.sum(dv * z_ref[...].astype(F32), axis=0, keepdims=True)

        @pl.when(i == 0)
        def _():
            dw_ref[...] = dw
            dsc_ref[...] = dsc

        @pl.when(i > 0)
        def _():
            dw_ref[...] += dw
            dsc_ref[...] += dsc

    blk = pl.BlockSpec((tr, POOL_DIM), lambda g, i: (i, g))
    vec = pl.BlockSpec((1, POOL_DIM), lambda g, i: (0, g))
    mat = pl.BlockSpec((None, POOL_DIM, POOL_DIM), lambda g, i: (g, 0, 0))
    nxt = pl.BlockSpec((H, POOL_DIM), lambda g, i: (jnp.minimum((i + 1) * (tr // H), S // H - 1), g))
    return _pcall(
        body, name=name, grid=(D // POOL_DIM, S // tr), in_specs=[blk, nxt, blk, blk, vec, mat],
        out_specs=[blk, mat, vec],
        out_shape=[jax.ShapeDtypeStruct((S, D), BF16), jax.ShapeDtypeStruct((D // POOL_DIM, POOL_DIM, POOL_DIM), F32),
                   jax.ShapeDtypeStruct((1, D), F32)],
        compiler_params=_params(("parallel", "arbitrary")))(d_out, d_out, z, y, scale, w_group)


def _row_tile(rows, cols, target_bytes=1 << 20):
    best = rows
    for tr in range(8, rows + 1, 8):
        if rows % tr == 0 and tr * cols * 4 <= target_bytes:
            best = tr
    return best if best * cols * 4 <= 4 * target_bytes else rows


def _adamw_refs(w_ref, g_ref, m_ref, v_ref, go_ref, d_ref, mo_ref, vo_ref):
    gv = g_ref[...]
    m2 = ADAM_B1 * m_ref[...] + (1.0 - ADAM_B1) * gv
    v2 = ADAM_B2 * v_ref[...] + (1.0 - ADAM_B2) * (gv * gv)
    m_hat = m2 / (1.0 - ADAM_B1 ** ADAM_STEP)
    v_hat = v2 / (1.0 - ADAM_B2 ** ADAM_STEP)
    d_ref[...] = -ADAM_LR * (m_hat / (jnp.sqrt(v_hat) + ADAM_EPS) + ADAM_WD * w_ref[...])
    go_ref[...] = gv
    mo_ref[...] = m2
    vo_ref[...] = v2


def _adamw(name, w, g, m, v):
    R, C = w.shape
    tr = _row_tile(R, C) if R % 8 == 0 else R
    blk = pl.BlockSpec((tr, C), lambda i: (i, 0))
    sds = jax.ShapeDtypeStruct((R, C), F32)
    return _pcall(_adamw_refs, name=name, grid=(R // tr,), in_specs=[blk] * 4, out_specs=[blk] * 4,
                  out_shape=[sds] * 4, compiler_params=_params(("parallel",)))(w, g, m, v)


def _adamw_small(name, items):
    n = len(items)

    def body(*refs):
        for t in range(n):
            _adamw_refs(*refs[4 * t:4 * t + 4], *refs[4 * n + 4 * t:4 * n + 4 * t + 4])

    specs, shapes = [], []
    for w, _, _, _ in items:
        specs += [pl.BlockSpec(w.shape, lambda i: (0, 0))] * 4
        shapes += [jax.ShapeDtypeStruct(w.shape, F32)] * 4
    res = _pcall(body, name=name, grid=(1,), in_specs=specs, out_specs=specs, out_shape=shapes,
                 compiler_params=_params(("arbitrary",)))(*[a for item in items for a in item])
    return [res[4 * t:4 * t + 4] for t in range(n)]


def _sum_leading(name, a, out_dtype):
    n, R, C = a.shape
    tr = _row_tile(R, C) if R % 16 == 0 else R
    if tr % 16:
        tr = R

    def body(a_ref, o_ref):
        acc = a_ref[0].astype(F32)
        for j in range(1, n):
            acc = acc + a_ref[j].astype(F32)
        o_ref[...] = acc.astype(o_ref.dtype)

    return _pcall(body, name=name, grid=(R // tr,), in_specs=[pl.BlockSpec((n, tr, C), lambda i: (0, i, 0))],
                  out_specs=pl.BlockSpec((tr, C), lambda i: (i, 0)), out_shape=jax.ShapeDtypeStruct((R, C), out_dtype),
                  compiler_params=_params(("parallel",)))(a)


def _cast_many(name, items, chip, steps=4):
    tiles = []
    for w, _, dtype in items:
        R = w.shape[1]
        nt = steps if R % (steps * 16) == 0 else 1
        tiles.append((nt, R // nt))

    def body(c_ref, *refs):
        i = pl.program_id(0)
        for t, (nt, _) in enumerate(tiles):
            w_ref, o_ref = refs[t], refs[len(items) + t]

            def cast(w_ref=w_ref, o_ref=o_ref):
                o_ref[...] = w_ref[...].astype(o_ref.dtype)

            if nt == steps:
                cast()
            else:
                pl.when(i < nt)(cast)

    in_specs, out_specs, out_shape = [], [], []
    for (w, layer, dtype), (nt, tr) in zip(items, tiles):
        C = w.shape[2]
        in_specs.append(pl.BlockSpec((None, tr, C), lambda i, c_ref, l=layer, nt=nt: (l, jnp.minimum(i, nt - 1), 0)))
        out_specs.append(pl.BlockSpec((None, tr, C), lambda i, c_ref, nt=nt: (c_ref[0], jnp.minimum(i, nt - 1), 0)))
        out_shape.append(jax.ShapeDtypeStruct((N_CHIPS, w.shape[1], C), dtype))
    return _pcall(body, prefetch=1, name=name, grid=(steps,), in_specs=in_specs, out_specs=out_specs,
                  out_shape=out_shape, compiler_params=_params(("arbitrary",)))(chip, *[w for w, _, _ in items])


def _cast_qkv(name, w, chip, tr=256):
    _, R, C = w.shape

    def body(c_ref, w_ref, own_ref, *piece_refs):
        v = w_ref[...].astype(BF16)
        own_ref[...] = v
        for p, ref in enumerate(piece_refs):
            ref[...] = v[:, p * QKV_TILE:(p + 1) * QKV_TILE]

    piece = pl.BlockSpec((None, tr, QKV_TILE), lambda i, c_ref: (c_ref[0], i, 0))
    return _pcall(body, prefetch=1, name=name, grid=(R // tr,),
                  in_specs=[pl.BlockSpec((None, tr, C), lambda i, c_ref: (0, i, 0))],
                  out_specs=[pl.BlockSpec((tr, C), lambda i, c_ref: (i, 0))] + [piece] * QKV_PIECES,
                  out_shape=[jax.ShapeDtypeStruct((R, C), BF16)]
                  + [jax.ShapeDtypeStruct((N_CHIPS, R, QKV_TILE), BF16)] * QKV_PIECES,
                  compiler_params=_params(("parallel",)))(chip, w)


def _owner_sum(name, mine, landed, chip, core, out=None, layer=None, col=None):
    _, half, C = mine.shape
    k, n_col = col if col is not None else (0, 1)
    tr = _row_tile(half, C)
    if tr % 16:
        tr = half
    nrt = half // tr
    has_out = out is not None

    def body(*refs):
        m_ref, l_ref, o_ref = refs[2], refs[3], refs[-1]
        acc = m_ref[...].astype(F32)
        for j in range(3):
            acc = acc + l_ref[j].astype(F32)
        o_ref[...] = acc

    if layer is None:
        out_shape = jax.ShapeDtypeStruct((2 * half, n_col * C), F32)
        o_spec = pl.BlockSpec((tr, C), lambda i, ch, co: (co[0] * nrt + i, k))
    else:
        out_shape = jax.ShapeDtypeStruct((2, 2 * half, C), F32)
        o_spec = pl.BlockSpec((None, tr, C), lambda i, ch, co: (layer, co[0] * nrt + i, 0))
    in_specs = [pl.BlockSpec((None, tr, C), lambda i, ch, co: (ch[0], i, 0)),
                pl.BlockSpec((3, tr, C), lambda i, ch, co: (0, i, 0))]
    operands = [chip, core, mine, landed]
    aliases = {}
    if has_out:
        in_specs.append(ANY)
        operands.append(out)
        aliases = {4: 0}
    grid_spec = pltpu.PrefetchScalarGridSpec(num_scalar_prefetch=2, grid=(nrt,), in_specs=in_specs, out_specs=o_spec)
    return _pcall(body, name=name, grid_spec=grid_spec, out_shape=out_shape, input_output_aliases=aliases,
                  compiler_params=_params(("parallel",)))(*operands)


def _add_my_half(name, ps, qs, core):
    n = len(ps)

    def body(c_ref, *refs):
        for t in range(n):
            p_ref, q_ref, o_ref = refs[t], refs[n + t], refs[2 * n + t]
            o_ref[...] = (p_ref[...].astype(F32) + q_ref[...].astype(F32)).astype(o_ref.dtype)

    halves = [(p.shape[1] // 2, p.shape[2]) for p in ps]
    mine = [pl.BlockSpec((None, h, c), lambda s, c_ref: (s, c_ref[0], 0)) for h, c in halves]
    whole = [pl.BlockSpec((None, h, c), lambda s, c_ref: (s, 0, 0)) for h, c in halves]
    return _pcall(body, prefetch=1, name=name, grid=(N_CHIPS,), in_specs=mine + whole, out_specs=whole,
                  out_shape=[jax.ShapeDtypeStruct((N_CHIPS, h, c), BF16) for h, c in halves],
                  compiler_params=_params(("parallel",)))(core, *ps, *qs)


ANY = pl.BlockSpec(memory_space=pl.ANY)


def _place():
    x, y, c = lax.axis_index("x"), lax.axis_index("y"), lax.axis_index("c")
    other_chips = [(1 - x, y), (x, 1 - y), (1 - x, 1 - y)]
    return x, y, c, other_chips


def _remote(src, dst, send, recv, k, to):
    return pltpu.make_async_remote_copy(src_ref=src, dst_ref=dst, send_sem=send.at[k], recv_sem=recv.at[k],
                                        device_id=to, device_id_type=MESH)


def _in_place(bufs):
    return dict(operands=bufs, out_shapes=[jax.ShapeDtypeStruct(b.shape, b.dtype) for b in bufs],
                aliases={t: t for t in range(len(bufs))})


def _gather_over_ici(bufs, split):
    n = len(bufs)

    def copies(ins, outs, send, recv, base=0):
        x, y, c, chips = _place()
        me = 2 * x + y
        out = []
        for t in range(n):
            piece = outs[t].at[me]
            if split[t]:
                half = outs[t].shape[1] // 2
                piece = outs[t].at[me, pl.ds(c * half, half)]
            for j, (px, py) in enumerate(chips):
                out.append(_remote(piece, piece, send, recv, base + 3 * t + j, (px, py, c)))
        return out

    return _Rider(n_copies=3 * n, copies=copies, **_in_place(bufs))


def _gather_to_sibling(bufs):
    n = len(bufs)

    def copies(ins, outs, send, recv, base=0):
        x, y, c, chips = _place()
        out = []
        for t in range(n):
            half = outs[t].shape[1] // 2
            for j, (px, py) in enumerate(chips):
                piece = outs[t].at[2 * px + py, pl.ds(c * half, half)]
                out.append(_remote(piece, piece, send, recv, base + 3 * t + j, (x, y, 1 - c)))
        return out

    return _Rider(n_copies=3 * n, copies=copies, **_in_place(bufs))


def _gather_and_pass_rider(buf):
    half = buf.shape[1] // 2

    def pieces(outs):
        x, y, c, chips = _place()
        rows = lambda core: pl.ds(core * half, half)
        mine = outs[0].at[2 * x + y, rows(c)]
        landed = [outs[0].at[2 * px + py, rows(c)] for px, py in chips]
        theirs = [outs[0].at[2 * px + py, rows(1 - c)] for px, py in chips]
        return (x, y, c, chips), mine, landed, theirs

    def start(ins, outs, send, recv, base=0):
        (x, y, c, chips), mine, _, _ = pieces(outs)
        for j, (px, py) in enumerate(chips):
            _remote(mine, mine, send, recv, base + j, (px, py, c)).start()

    def finish(ins, outs, send, recv, base=0):
        (x, y, c, chips), mine, landed, theirs = pieces(outs)
        sibling = (x, y, 1 - c)
        passed = []
        for j, (px, py) in enumerate(chips):
            _remote(landed[j], landed[j], send, recv, base + j, (px, py, c)).wait_recv()
            passed.append(_remote(landed[j], landed[j], send, recv, base + 3 + j, sibling))
            passed[-1].start()
        for j in range(3):
            _remote(theirs[j], theirs[j], send, recv, base + 3 + j, sibling).wait_recv()
        for j, (px, py) in enumerate(chips):
            _remote(mine, mine, send, recv, base + j, (px, py, c)).wait_send()
            passed[j].wait_send()

    return _Rider(n_copies=6, start=start, finish=finish, **_in_place([buf]))


def _swap_halves_rider(parts):
    n = len(parts)

    def copies(ins, outs, send, recv, base=0):
        x, y, c, _ = _place()
        out = []
        for t in range(n):
            half = ins[t].shape[1] // 2
            out.append(_remote(ins[t].at[:, pl.ds((1 - c) * half, half)], outs[t], send, recv, base + t,
                               (x, y, 1 - c)))
        return out

    shapes = [jax.ShapeDtypeStruct((p.shape[0], p.shape[1] // 2, p.shape[2]), p.dtype) for p in parts]
    return _Rider(parts, shapes, {}, n, copies)


def _scatter_rider(sums):
    n = len(sums)

    def copies(ins, outs, send, recv, base=0):
        x, y, c, chips = _place()
        out = []
        for t in range(n):
            for j, (px, py) in enumerate(chips):
                out.append(_remote(ins[t].at[2 * px + py], outs[t].at[j], send, recv, base + 3 * t + j, (px, py, c)))
        return out

    shapes = [jax.ShapeDtypeStruct((3,) + s.shape[1:], s.dtype) for s in sums]
    return _Rider(sums, shapes, {}, 3 * n, copies)


def _share_rider(blocks, pieces):
    def copies(ins, outs, send, recv, base=0):
        x, y, c, _ = _place()
        out = []
        for k, (t, l, col) in enumerate(pieces):
            ref = outs[t] if l is None else outs[t].at[l]
            r2 = ref.shape[0] // 2
            piece = ref.at[pl.ds(c * r2, r2)]
            if col is not None:
                width = ref.shape[1] // col[1]
                piece = ref.at[pl.ds(c * r2, r2), pl.ds(col[0] * width, width)]
            out.append(_remote(piece, piece, send, recv, base + k, (x, y, 1 - c)))
        return out

    return _Rider(n_copies=len(pieces), copies=copies, **_in_place(blocks))


def _gather_small(name, v):
    def body(v_ref, out_ref, send_sem, recv_sem, local_sem):
        x, y, c, _ = _place()
        me = 4 * x + 2 * y + c
        flips = [(fx, fy, fc) for fx in (0, 1) for fy in (0, 1) for fc in (0, 1)][1:]
        local = pltpu.make_async_copy(v_ref, out_ref.at[me], local_sem)
        local.start()
        copies = []
        for j, (fx, fy, fc) in enumerate(flips):
            peer = (x ^ fx, y ^ fy, c ^ fc)
            copies.append(pltpu.make_async_remote_copy(
                src_ref=v_ref, dst_ref=out_ref.at[me], send_sem=send_sem.at[j], recv_sem=recv_sem.at[j],
                device_id=peer, device_id_type=MESH))
        for cp in copies:
            cp.start()
        for j, (fx, fy, fc) in enumerate(flips):
            peer = (x ^ fx, y ^ fy, c ^ fc)
            pltpu.make_async_remote_copy(
                src_ref=v_ref, dst_ref=out_ref.at[4 * peer[0] + 2 * peer[1] + peer[2]], send_sem=send_sem.at[j],
                recv_sem=recv_sem.at[j], device_id=peer, device_id_type=MESH).wait_recv()
        for cp in copies:
            cp.wait_send()
        local.wait()

    return _pcall(body, name=name, in_specs=[ANY], out_specs=ANY,
                  out_shape=jax.ShapeDtypeStruct((8,) + v.shape, v.dtype),
                  scratch_shapes=[pltpu.SemaphoreType.DMA((7,)), pltpu.SemaphoreType.DMA((7,)),
                                  pltpu.SemaphoreType.DMA])(v)


def _fold(a, dil):
    if dil == 1:
        return a
    S, C = a.shape
    return a.reshape(S // dil, dil, C).transpose(1, 0, 2).reshape(S, C)


def _unfold(a, dil):
    if dil == 1:
        return a
    S, C = a.shape
    return a.reshape(dil, S // dil, C).transpose(1, 0, 2).reshape(S, C)


def _fold_groups(a):
    return jnp.stack([_fold(a[g] if a.ndim == 3 else a, d) for g, d in enumerate(ATTN_DILATIONS)])


def _unfold_groups(a):
    return jnp.stack([_unfold(a[g], d) for g, d in enumerate(ATTN_DILATIONS)])


class _NoComm:
    def __init__(self, weights):
        self.weights = weights
        self.grads = {}
        self.chip = jnp.zeros((1,), jnp.int32)

    def w(self, name):
        return self.weights[name]

    def run(self, fn, name, *args, **kw):
        return fn(name, *args, **kw)

    def grad(self, name, value):
        self.grads[name] = value


def _local_step(xs, tgt, attn_norm, ffn_norm, final_norm, comm):
    run = comm.run
    hf = None
    for g in range(N_GROUPS):
        hf = run(_norm_fold, "fold%d" % g, g, xs, attn_norm, hf)
    qkv = run(_qkv_tiles, "qkv_own", None, hf, comm.w("wq_own"), None, comm.chip)
    for p in range(QKV_PIECES):
        qkv = run(_qkv_tiles, "qkv_piece%d" % p, p, hf, comm.w("wq%d" % p), qkv, comm.chip)
    o_f, lse_f = run(_attn_fwd, "attn_fwd", qkv)
    lse_t = _unfold_groups(lse_f)
    merged = run(_merge_fwd, "merge_fwd", o_f, lse_t)
    x1, h1 = run(_proj_res_norm, "attn_out", merged, comm.w("o"), xs, ffn_norm[0:1])
    gu0, act0 = run(_ffn_up, "ffn0_up", h1, comm.w("gu0"))
    x2, h2 = run(_proj_res_norm, "ffn0_down", act0, comm.w("d0"), x1, comm.w("pool_norm"))
    u = run(_mm_rows, "pool_in", h2, comm.w("p"), F32, 512)
    y, z, x3 = run(_pool_fwd, "pool_fwd", u, comm.w("pg"), comm.w("pool_scale"), x2)
    h3 = run(_rms_fwd, "norm_ffn1", x3, ffn_norm[1:2])
    gu1, act1 = run(_ffn_up, "ffn1_up", h3, comm.w("gu1"))
    loss, dx4, d_final = run(_proj_res_loss, "ffn1_down", act1, comm.w("d1"), x3, final_norm, tgt)

    dgu1 = run(_ffn_down_bwd, "ffn1_down_bwd", dx4, comm.w("d1"), gu1)
    comm.grad("d1", run(_mm_tn, "ffn1_down_dw", act1, dx4, FF_SHARD, 2048))
    comm.grad("gu1", run(_ffn_dw_up, "ffn1_up_dw", h3, dgu1))
    dx3, d_ffn1 = run(_ffn_dh, "ffn1_up_dh", dgu1, comm.w("gu1"), x3, ffn_norm[1:2], dx4)

    du, dw_pg, d_scale = run(_pool_bwd, "pool_bwd", dx3, z, y, comm.w("pool_scale"), comm.w("pg"))
    comm.grad("pg", dw_pg)
    comm.grad("p", run(_mm_tn, "pool_in_dw", h2, du, D_MODEL, h2.shape[0]))
    dx2, d_pool = run(_dh_norm_bwd, "pool_in_dh", du, comm.w("p"), x2, comm.w("pool_norm"), dx3)

    dgu0 = run(_ffn_down_bwd, "ffn0_down_bwd", dx2, comm.w("d0"), gu0)
    comm.grad("d0", run(_mm_tn, "ffn0_down_dw", act0, dx2, FF_SHARD, 2048))
    comm.grad("gu0", run(_ffn_dw_up, "ffn0_up_dw", h1, dgu0))
    dx1, d_ffn0 = run(_ffn_dh, "ffn0_up_dh", dgu0, comm.w("gu0"), x1, ffn_norm[0:1], dx2)

    d_merged = run(_mm_nt_rows, "attn_out_dh", dx1, comm.w("o"), 512)
    comm.grad("o", run(_mm_tn, "attn_out_dw", merged, dx1, D_MODEL, 2048))
    do_f, dl_t = run(_merge_bwd, "merge_bwd", d_merged, o_f, lse_t)
    dqkv = run(_attn_bwd, "attn_bwd", qkv, do_f, lse_f, _fold_groups(dl_t))
    for p in range(QKV_PIECES):
        comm.grad("qkv%d" % p, run(_qkv_dw, "qkv_dw%d" % p, p, hf, dqkv))
    dhf = None
    for g in range(N_GROUPS):
        dhf = run(_qkv_dh, "qkv_dh%d" % g, g, dqkv, [comm.w("wq%d" % p) for p in range(QKV_PIECES)], dhf)
    grad_x, d_attn = run(_rms_bwd_folded, "norm_attn_bwd", xs, attn_norm, dhf, dx1)

    small = dict(attn=d_attn, ffn0=d_ffn0, ffn1=d_ffn1, final=d_final, pool=d_pool, scale=d_scale)
    return loss, grad_x, small


TENSORS = ("qkv", "o", "p", "pg", "gu0", "gu1", "d0", "d1")


def _block_of(name):
    if name[:-1] in ("gu", "d"):
        return name[:-1], int(name[-1]), None
    if name[:-1] == "qkv":
        return "qkv", None, (int(name[-1]), QKV_PIECES)
    return name, None, None


class _MeshComm:
    def __init__(self, bufs, chip_arr, core_arr, pg_rows):
        self.bufs = dict(bufs)
        self.chip, self.chip_arr, self.core_arr, self.pg_rows = chip_arr, chip_arr, core_arr, pg_rows
        self.parts, self.sums, self.blocks = {}, {}, {}
        ici = lambda *names: lambda: self.gather(names, True)
        d2d = lambda *names: lambda: self.gather(names, False)
        whole = lambda name: lambda: self.gather_and_pass(name)
        swap = lambda *names: lambda: self.swap(names)
        scatter = lambda *names: lambda: self.scatter(names)
        share = lambda *names: lambda: self.share(names)
        self.plan = {
            "qkv_own": [whole("wq0")],
            "qkv_piece0": [whole("wq1")],
            "qkv_piece1": [whole("wq2")],
            "attn_fwd": [ici("gu0", "o")],
            "merge_fwd": [ici("d0"), d2d("gu0", "o")],
            "attn_out": [ici("p", "pg", "pool_norm", "pool_scale"), d2d("d0")],
            "ffn0_up": [ici("gu1"), d2d("p", "pg")],
            "ffn0_down": [d2d("gu1"), ici("d1")],
            "pool_in": [d2d("d1")],
            "ffn1_up_dh": [swap("d1", "gu1")],
            "ffn0_down_bwd": [scatter("gu1")],
            "ffn0_down_dw": [scatter("d1")],
            "ffn0_up_dh": [swap("pg", "p", "d0", "gu0"), share("gu1", "d1")],
            "merge_bwd": [swap("o")],
            "attn_bwd": [scatter("gu0", "o")],
            "qkv_dw0": [scatter("d0")],
            "qkv_dw1": [scatter("p", "pg"), swap("qkv0")],
            "qkv_dw2": [share("gu0", "o", "d0"), scatter("qkv0"), swap("qkv1")],
            "qkv_dh0": [share("qkv0", "p", "pg"), scatter("qkv1"), swap("qkv2")],
            "qkv_dh1": [share("qkv1"), scatter("qkv2")],
            "qkv_dh2": [share("qkv2")],
        }

    def gather_and_pass(self, name):
        return _gather_and_pass_rider(self.bufs[name]), lambda res: self.bufs.update({name: res[0]})

    def gather(self, names, over_ici):
        bufs = [self.bufs[n] for n in names]
        rider = _gather_over_ici(bufs, [n in TENSORS for n in names]) if over_ici else _gather_to_sibling(bufs)
        return rider, lambda res: self.bufs.update(zip(names, res))

    def swap(self, names):
        def done(res):
            sums = _add_my_half("chip_sum_" + "_".join(names), [self.parts[n] for n in names], res, self.core_arr)
            self.sums.update(zip(names, sums))
        return _swap_halves_rider([self.parts[n] for n in names]), done

    def scatter(self, names):
        def done(res):
            for n, landed in zip(names, res):
                key, layer, col = _block_of(n)
                self.blocks[key] = _owner_sum("owner_sum_" + n, self.sums[n], landed, self.chip_arr, self.core_arr,
                                              out=self.blocks.get(key), layer=layer, col=col)
        return _scatter_rider([self.sums[n] for n in names]), done

    def share(self, names):
        keys, pieces = [], []
        for n in names:
            key, layer, col = _block_of(n)
            if key not in keys:
                keys.append(key)
            pieces.append((keys.index(key), layer, col))
        return _share_rider([self.blocks[k] for k in keys], pieces), lambda res: self.blocks.update(zip(keys, res))

    def run(self, fn, name, *args, **kw):
        made = [make() for make in self.plan.get(name, [])]
        if not made:
            return fn(name, *args, **kw)
        riders = [r for r, _ in made]
        out = _ride(riders[0] if len(riders) == 1 else _riders(*riders), fn, name, *args, **kw)
        for r, done in made:
            done(r.results)
        return out

    def w(self, name):
        b = self.bufs[name]
        if name in ("o", "p", "d0", "d1"):
            return b.reshape(-1, b.shape[-1])
        if name == "pg":
            b = b.reshape(N_CHIPS, 4, self.pg_rows, POOL_DIM).transpose(1, 0, 2, 3)
            return b.reshape(4, POOL_DIM, POOL_DIM)
        if name in ("pool_norm", "pool_scale"):
            return b.reshape(1, -1)
        return b

    def grad(self, name, value):
        if name == "pg":
            value = value.reshape(4, N_CHIPS, self.pg_rows, POOL_DIM).transpose(1, 0, 2, 3)
            value = value.reshape(N_CHIPS, 4 * self.pg_rows, POOL_DIM).astype(BF16)
        elif name in ("o", "p", "d0", "d1"):
            value = value.reshape(N_CHIPS, value.shape[0] // N_CHIPS, value.shape[1])
        self.parts[name] = value


def kernel(x, attn_norm, w_qkv, w_attn_out, pool_norm, w_pool_in, w_pool_group, pool_scale, ffn_norm, w_ffn_gate_up, w_ffn_down, final_norm, loss_target, m_attn_norm, m_w_qkv, m_w_attn_out, m_pool_norm, m_w_pool_in, m_w_pool_group, m_pool_scale, m_ffn_norm, m_w_ffn_gate_up, m_w_ffn_down, m_final_norm, v_attn_norm, v_w_qkv, v_w_attn_out, v_pool_norm, v_w_pool_in, v_w_pool_group, v_pool_scale, v_ffn_norm, v_w_ffn_gate_up, v_w_ffn_down, v_final_norm):
    D = D_MODEL
    chip = 2 * lax.axis_index("x") + lax.axis_index("y")
    core = lax.axis_index("c")
    pg_rows = w_pool_group.shape[2]

    chip_arr = chip.astype(jnp.int32).reshape(1)
    core_arr = core.astype(jnp.int32).reshape(1)

    pieces = _cast_qkv("cast_qkv", w_qkv, chip_arr)
    bufs = dict(zip(["wq_own"] + ["wq%d" % p for p in range(QKV_PIECES)], pieces))
    shards = [(w_attn_out, 0, BF16), (w_pool_in, 0, BF16), (w_pool_group.reshape(1, 4 * pg_rows, POOL_DIM), 0, BF16),
              (w_ffn_gate_up, 0, BF16), (w_ffn_gate_up, 1, BF16), (w_ffn_down, 0, BF16), (w_ffn_down, 1, BF16),
              (pool_norm[None], 0, F32), (pool_scale[None], 0, F32)]
    bufs.update(zip(TENSORS[1:] + ("pool_norm", "pool_scale"), _cast_many("cast_rest", shards, chip_arr)))

    comm = _MeshComm(bufs, chip_arr, core_arr, pg_rows)
    loss_part, grad_x, small = _local_step(x[0], loss_target[0], attn_norm, ffn_norm, final_norm.reshape(1, D), comm)
    g_w_qkv, g_w_o, g_w_p, g_w_pg, g_w_gu, g_w_d = [comm.blocks[k] for k in ("qkv", "o", "p", "pg", "gu", "d")]

    rows = jnp.concatenate([small["attn"], small["ffn0"], small["ffn1"], small["final"], small["pool"],
                            small["scale"], jnp.pad(loss_part, ((0, 0), (0, D - 1))), jnp.zeros((1, D), F32)], axis=0)
    all_rows = _gather_small("gather_gain_grads", rows)
    tot = _sum_leading("sum_gain_grads", all_rows, F32)
    loss = tot[6, 0]
    g_attn_norm = tot[0:1]
    g_ffn_norm = tot[1:3]
    g_final_norm = tot[3]
    g_pool_norm = lax.dynamic_slice(tot, (4, chip * POOL_DIM), (1, POOL_DIM))
    g_pool_scale = lax.dynamic_slice(tot, (5, chip * POOL_DIM), (1, POOL_DIM))

    def update(name, w, g, m, v):
        shape = w.shape
        cols = shape[-1]
        as2d = lambda a: a.reshape(-1, cols)
        return [a.reshape(shape) for a in _adamw("adamw_" + name, as2d(w), as2d(g), as2d(m), as2d(v))]

    gains = [(attn_norm, g_attn_norm, m_attn_norm, v_attn_norm), (pool_norm, g_pool_norm, m_pool_norm, v_pool_norm),
             (pool_scale, g_pool_scale, m_pool_scale, v_pool_scale), (ffn_norm, g_ffn_norm, m_ffn_norm, v_ffn_norm),
             (final_norm, g_final_norm, m_final_norm, v_final_norm)]
    as_rows = lambda a: a.reshape(-1, a.shape[-1])
    small_res = _adamw_small("adamw_gains", [tuple(as_rows(a) for a in item) for item in gains])
    small_res = [[a.reshape(item[0].shape) for a in r] for r, item in zip(small_res, gains)]
    results = [
        small_res[0],
        update("w_qkv", w_qkv, g_w_qkv, m_w_qkv, v_w_qkv),
        update("w_attn_out", w_attn_out, g_w_o, m_w_attn_out, v_w_attn_out),
        small_res[1],
        update("w_pool_in", w_pool_in, g_w_p, m_w_pool_in, v_w_pool_in),
        update("w_pool_group", w_pool_group, g_w_pg, m_w_pool_group, v_w_pool_group),
        small_res[2],
        small_res[3],
        update("w_ffn_gate_up", w_ffn_gate_up, g_w_gu, m_w_ffn_gate_up, v_w_ffn_gate_up),
        update("w_ffn_down", w_ffn_down, g_w_d, m_w_ffn_down, v_w_ffn_down),
        small_res[4],
    ]
    grads = [r[0] for r in results]
    deltas = [r[1] for r in results]
    new_m = [r[2] for r in results]
    new_v = [r[3] for r in results]
    return (loss, grad_x[None], *grads, *deltas, *new_m, *new_v)
```

```python
import functools

import jax
import jax.numpy as jnp
from jax import lax
from jax.experimental import pallas as pl
from jax.experimental.pallas import tpu as pltpu

F32 = jnp.float32
BF16 = jnp.bfloat16
MESH = pl.DeviceIdType.MESH

D_MODEL = 1024
N_HEADS = 8
HEAD_DIM = 128
Q_BLOCK = 128
ATTN_DILATIONS = (1, 4, 16)
N_GROUPS = 3
D_FF = 2816
N_CHIPS = 4
FF_SHARD = 2 * D_FF // N_CHIPS
QKV_SHARD = 3 * 3 * D_MODEL // N_CHIPS
QKV_TILE = 768
POOL_WINDOWS = (2, 4, 8, 16)
POOL_DIM = 256
POOL_HALO = 16
RMS_EPS = 1e-6
NEG = -1e30
ADAM_LR = 0.001
ADAM_B1 = 0.9
ADAM_B2 = 0.999
ADAM_EPS = 1e-08
ADAM_WD = 0.01
ADAM_STEP = 10
VMEM_LIMIT = 56 * 1024 * 1024

_DN = {
    "nn": (((1,), (0,)), ((), ())),
    "nt": (((1,), (1,)), ((), ())),
    "tn": (((0,), (0,)), ((), ())),
}


class _Rider:
    def __init__(self, operands, out_shapes, aliases, n_copies, copies=None, start=None, finish=None):
        self.operands = list(operands)
        self.out_shapes = list(out_shapes)
        self.aliases = dict(aliases)
        self.n_copies = n_copies
        self.results = None

        def start_copies(ins, outs, send, recv, base=0):
            for cp in copies(ins, outs, send, recv, base):
                cp.start()

        def wait_copies(ins, outs, send, recv, base=0):
            for cp in copies(ins, outs, send, recv, base):
                cp.wait()

        self.start = start or start_copies
        self.finish = finish or wait_copies


def _riders(*riders):
    operands, out_shapes, aliases, offsets = [], [], {}, []
    n = 0
    for r in riders:
        offsets.append((len(operands), len(out_shapes), n))
        aliases.update({len(operands) + i: len(out_shapes) + o for i, o in r.aliases.items()})
        operands += r.operands
        out_shapes += r.out_shapes
        n += r.n_copies

    def each(which):
        def go(ins, outs, send, recv, base=0):
            for r, (i0, o0, s0) in zip(riders, offsets):
                getattr(r, which)(ins[i0:i0 + len(r.operands)], outs[o0:o0 + len(r.out_shapes)], send, recv,
                                  base + s0)
        return go

    both = _Rider(operands, out_shapes, aliases, n, start=each("start"), finish=each("finish"))
    both.parts = (riders, offsets)
    return both


_pending_rider = []


def _ride(rider, fn, *args, **kw):
    _pending_rider.append(rider)
    out = fn(*args, **kw)
    assert not _pending_rider
    if hasattr(rider, "parts"):
        for r, (_, o0, _) in zip(*rider.parts):
            r.results = rider.results[o0:o0 + len(r.out_shapes)]
    return out


def _pcall(body, prefetch=0, **kw):
    def build(body, kw):
        if not prefetch:
            return pl.pallas_call(body, **kw)
        kw = dict(kw)
        grid_spec = pltpu.PrefetchScalarGridSpec(
            num_scalar_prefetch=prefetch, grid=kw.pop("grid"), in_specs=kw.pop("in_specs"),
            out_specs=kw.pop("out_specs"), scratch_shapes=kw.pop("scratch_shapes", []))
        return pl.pallas_call(body, grid_spec=grid_spec, **kw)

    if not _pending_rider:
        return build(body, kw)
    rider = _pending_rider.pop()
    grid = kw.get("grid", ())
    in_specs = list(kw.get("in_specs", []))
    single = not isinstance(kw["out_shape"], (list, tuple))
    out_shape = [kw["out_shape"]] if single else list(kw["out_shape"])
    out_specs = [kw["out_specs"]] if single else list(kw["out_specs"])
    scratch = list(kw.get("scratch_shapes", []))
    n_in, n_out, n_scr = len(in_specs), len(out_shape), len(scratch)
    r_in, r_out = len(rider.operands), len(rider.out_shapes)

    def wrapped(*refs):
        scalars, refs = refs[:prefetch], refs[prefetch:]
        ins, rins = refs[:n_in], refs[n_in:n_in + r_in]
        outs = refs[n_in + r_in:n_in + r_in + n_out]
        routs = refs[n_in + r_in + n_out:n_in + r_in + n_out + r_out]
        scr = refs[n_in + r_in + n_out + r_out:n_in + r_in + n_out + r_out + n_scr]
        send, recv = refs[-2:]
        ids = [pl.program_id(a) for a in range(len(grid))]
        first, last = True, True
        for a, pid in enumerate(ids):
            first = jnp.logical_and(first, pid == 0)
            last = jnp.logical_and(last, pid == grid[a] - 1)
        if grid:
            pl.when(first)(lambda: rider.start(rins, routs, send, recv))
        else:
            rider.start(rins, routs, send, recv)
        body(*scalars, *ins, *outs, *scr)
        if grid:
            pl.when(last)(lambda: rider.finish(rins, routs, send, recv))
        else:
            rider.finish(rins, routs, send, recv)

    any_spec = pl.BlockSpec(memory_space=pl.ANY)
    kw2 = dict(kw)
    kw2.update(
        in_specs=in_specs + [any_spec] * r_in, out_specs=out_specs + [any_spec] * r_out,
        out_shape=out_shape + rider.out_shapes,
        scratch_shapes=scratch + [pltpu.SemaphoreType.DMA((rider.n_copies,)),
                                  pltpu.SemaphoreType.DMA((rider.n_copies,))],
        input_output_aliases={**kw.get("input_output_aliases", {}),
                              **{prefetch + n_in + i: n_out + o for i, o in rider.aliases.items()}})
    if grid:
        kw2["compiler_params"] = _params(("arbitrary",) * len(grid))
    call = build(wrapped, kw2)

    def run(*operands):
        res = call(*operands, *rider.operands)
        rider.results = list(res[n_out:])
        return res[0] if single else list(res[:n_out])

    return run


def _params(sem):
    return pltpu.CompilerParams(dimension_semantics=sem, vmem_limit_bytes=VMEM_LIMIT)


def _dot(a, b, mode):
    return lax.dot_general(a, b, _DN[mode], preferred_element_type=F32)


def _mm(name, mode, grid, a, a_spec, b, b_spec, out_shape, o_spec, acc_shape, res=None, res_spec=None):
    nk = grid[-1]
    has_res = res is not None

    def body(*refs):
        a_ref, b_ref = refs[0], refs[1]
        res_ref = refs[2] if has_res else None
        o_ref = refs[2 + has_res]
        part = _dot(a_ref[...].astype(BF16), b_ref[...].astype(BF16), mode)
        if nk == 1:
            if has_res:
                part = part + res_ref[...]
            o_ref[...] = part.astype(o_ref.dtype)
            return
        acc_ref = refs[3 + has_res]
        k = pl.program_id(len(grid) - 1)

        @pl.when(k == 0)
        def _():
            acc_ref[...] = part

        @pl.when(k > 0)
        def _():
            acc_ref[...] += part

        @pl.when(k == nk - 1)
        def _():
            r = acc_ref[...]
            if has_res:
                r = r + res_ref[...]
            o_ref[...] = r.astype(o_ref.dtype)

    in_specs = [a_spec, b_spec] + ([res_spec] if has_res else [])
    operands = [a, b] + ([res] if has_res else [])
    scratch = [] if nk == 1 else [pltpu.VMEM(acc_shape, F32)]
    sem = ("parallel",) * (len(grid) - 1) + ("arbitrary",)
    return _pcall(body, name=name, grid=grid, in_specs=in_specs, out_specs=o_spec, out_shape=out_shape,
                  scratch_shapes=scratch, compiler_params=_params(sem))(*operands)


def _mm_tn(name, a, b, tm, tk):
    T, M = a.shape
    N = b.shape[1]
    return _mm(name, "tn", (M // tm, T // tk), a, pl.BlockSpec((tk, tm), lambda i, k: (k, i)),
               b, pl.BlockSpec((tk, N), lambda i, k: (k, 0)),
               jax.ShapeDtypeStruct((M, N), BF16), pl.BlockSpec((tm, N), lambda i, k: (i, 0)), (tm, N))


def _rms_fwd(name, x, g, tr=512):
    S, D = x.shape

    def body(x_ref, g_ref, o_ref):
        xf = x_ref[...]
        r = lax.rsqrt(jnp.mean(xf * xf, axis=-1, keepdims=True) + RMS_EPS)
        o_ref[...] = ((xf * r) * g_ref[...]).astype(o_ref.dtype)

    return _pcall(body, name=name, grid=(S // tr,),
                  in_specs=[pl.BlockSpec((tr, D), lambda i: (i, 0)), pl.BlockSpec((1, D), lambda i: (0, 0))],
                  out_specs=pl.BlockSpec((tr, D), lambda i: (i, 0)),
                  out_shape=jax.ShapeDtypeStruct((S, D), BF16), compiler_params=_params(("parallel",)))(x, g)


def _norm_bwd_rows(xf, gain, dh, dres):
    r = lax.rsqrt(jnp.mean(xf * xf, axis=-1, keepdims=True) + RMS_EPS)
    xh = xf * r
    t = dh * gain
    dx = dres + r * (t - xh * jnp.mean(t * xh, axis=-1, keepdims=True))
    return dx, jnp.sum(dh * xh, axis=0, keepdims=True)


def _accumulate(ref, value, first):
    @pl.when(first)
    def _():
        ref[...] = value

    @pl.when(jnp.logical_not(first))
    def _():
        ref[...] += value


def _rms_bwd_folded(name, x, g, dhf, dres, tb=512):
    S, D = x.shape

    def body(x_ref, g_ref, d0_ref, d1_ref, d2_ref, dres_ref, dx_ref, dg_ref, dh_ref):
        chunks = _lane_chunks(D)
        for c, cols in enumerate(chunks):
            dh_ref[c] = d0_ref[0, :, cols].astype(F32)
        for dil, ref in ((ATTN_DILATIONS[1], d1_ref), (ATTN_DILATIONS[2], d2_ref)):
            n = tb // dil
            for k in range(dil):
                for c, cols in enumerate(chunks):
                    dh_ref[c, _strided_rows(k, n, dil), :] += ref[k, :, cols].astype(F32)
        dh = jnp.concatenate([dh_ref[c] for c in range(len(chunks))], axis=1)
        dx, dg = _norm_bwd_rows(x_ref[...], g_ref[...], dh, dres_ref[...])
        dx_ref[...] = dx
        _accumulate(dg_ref, dg, pl.program_id(0) == 0)

    views, specs = _folded_views(dhf, tb)
    row = pl.BlockSpec((tb, D), lambda i: (i, 0))
    vec = pl.BlockSpec((1, D), lambda i: (0, 0))
    return _pcall(body, name=name, grid=(S // tb,), in_specs=[row, vec] + specs + [row], out_specs=[row, vec],
                  out_shape=[jax.ShapeDtypeStruct((S, D), F32), jax.ShapeDtypeStruct((1, D), F32)],
                  scratch_shapes=[pltpu.VMEM((D // LANES, tb, LANES), F32)],
                  compiler_params=_params(("arbitrary",)))(x, g, *views, dres)


def _proj_res_norm(name, a, w, res, gain, tm=512):
    M, K = a.shape
    D = w.shape[1]

    def body(a_ref, w_ref, res_ref, g_ref, x_ref, h_ref):
        xf = res_ref[...] + _dot(a_ref[...], w_ref[...], "nn")
        x_ref[...] = xf
        r = lax.rsqrt(jnp.mean(xf * xf, axis=-1, keepdims=True) + RMS_EPS)
        h_ref[...] = ((xf * r) * g_ref[...]).astype(h_ref.dtype)

    row = pl.BlockSpec((tm, D), lambda i: (i, 0))
    return _pcall(body, name=name, grid=(M // tm,),
                  in_specs=[pl.BlockSpec((tm, K), lambda i: (i, 0)), pl.BlockSpec((K, D), lambda i: (0, 0)), row,
                            pl.BlockSpec((1, D), lambda i: (0, 0))],
                  out_specs=[row, row],
                  out_shape=[jax.ShapeDtypeStruct((M, D), F32), jax.ShapeDtypeStruct((M, D), BF16)],
                  compiler_params=_params(("parallel",)))(a, w, res, gain)


def _proj_res_loss(name, a, w, res, gain, tgt, tm=512):
    M, K = a.shape
    D = w.shape[1]

    def body(a_ref, w_ref, res_ref, g_ref, t_ref, loss_ref, dx_ref, dg_ref):
        first = pl.program_id(0) == 0
        xf = res_ref[...] + _dot(a_ref[...], w_ref[...], "nn")
        r = lax.rsqrt(jnp.mean(xf * xf, axis=-1, keepdims=True) + RMS_EPS)
        xh = xf * r
        gv = g_ref[...]
        e = xh * gv - t_ref[...]
        lp = 0.5 * jnp.sum(jnp.mean(e * e, axis=-1, keepdims=True), axis=0, keepdims=True)
        dy = e * (1.0 / D)
        t = dy * gv
        dx_ref[...] = r * (t - xh * jnp.mean(t * xh, axis=-1, keepdims=True))
        _accumulate(dg_ref, jnp.sum(dy * xh, axis=0, keepdims=True), first)
        _accumulate(loss_ref, lp, first)

    row = pl.BlockSpec((tm, D), lambda i: (i, 0))
    vec = pl.BlockSpec((1, D), lambda i: (0, 0))
    return _pcall(body, name=name, grid=(M // tm,),
                  in_specs=[pl.BlockSpec((tm, K), lambda i: (i, 0)), pl.BlockSpec((K, D), lambda i: (0, 0)), row,
                            vec, row],
                  out_specs=[pl.BlockSpec((1, 1), lambda i: (0, 0)), row, vec],
                  out_shape=[jax.ShapeDtypeStruct((1, 1), F32), jax.ShapeDtypeStruct((M, D), F32),
                             jax.ShapeDtypeStruct((1, D), F32)],
                  compiler_params=_params(("arbitrary",)))(a, w, res, gain, tgt)


def _dh_norm_bwd(name, d, w, x, gain, dres, tm=512):
    M, N = d.shape
    D = w.shape[0]

    def body(d_ref, w_ref, x_ref, g_ref, dres_ref, dx_ref, dg_ref):
        dh = _dot(d_ref[...].astype(BF16), w_ref[...], "nt")
        dx, dg = _norm_bwd_rows(x_ref[...], g_ref[...], dh, dres_ref[...])
        dx_ref[...] = dx
        _accumulate(dg_ref, dg, pl.program_id(0) == 0)

    row = pl.BlockSpec((tm, D), lambda i: (i, 0))
    vec = pl.BlockSpec((1, D), lambda i: (0, 0))
    return _pcall(body, name=name, grid=(M // tm,),
                  in_specs=[pl.BlockSpec((tm, N), lambda i: (i, 0)), pl.BlockSpec((D, N), lambda i: (0, 0)), row, vec,
                            row],
                  out_specs=[row, vec],
                  out_shape=[jax.ShapeDtypeStruct((M, D), F32), jax.ShapeDtypeStruct((1, D), F32)],
                  compiler_params=_params(("arbitrary",)))(d, w, x, gain, dres)


def _sigmoid(v):
    return 0.5 * jnp.tanh(0.5 * v) + 0.5


def _ffn_up(name, h, w_gu, tm=512):
    S, D = h.shape
    W = FF_SHARD

    def body(h_ref, wg_ref, wu_ref, gu_ref, act_ref):
        hv = h_ref[...]
        gate = _dot(hv, wg_ref[...], "nn")
        up = _dot(hv, wu_ref[...], "nn")
        gu_ref[0] = gate.astype(gu_ref.dtype)
        gu_ref[1] = up.astype(gu_ref.dtype)
        sig = _sigmoid(gate)
        act_ref[...] = ((gate * sig) * up).astype(act_ref.dtype)

    return _pcall(
        body, name=name, grid=(2, S // tm),
        in_specs=[pl.BlockSpec((tm, D), lambda j, i: (i, 0)),
                  pl.BlockSpec((None, D, W), lambda j, i: (j, 0, 0)),
                  pl.BlockSpec((None, D, W), lambda j, i: (j + 2, 0, 0))],
        out_specs=[pl.BlockSpec((2, tm, W), lambda j, i: (0, i, j)), pl.BlockSpec((tm, W), lambda j, i: (i, j))],
        out_shape=[jax.ShapeDtypeStruct((2, S, D_FF), BF16), jax.ShapeDtypeStruct((S, D_FF), BF16)],
        compiler_params=_params(("parallel", "parallel")))(h, w_gu, w_gu)


def _ffn_down_bwd(name, dx, w_down, gu, tm=512):
    S, D = dx.shape
    W = FF_SHARD

    def body(dx_ref, w_ref, gu_ref, dgu_ref):
        dact = _dot(dx_ref[...].astype(BF16), w_ref[...], "nt")
        gate = gu_ref[0].astype(F32)
        up = gu_ref[1].astype(F32)
        sig = _sigmoid(gate)
        silu = gate * sig
        dgu_ref[0] = (dact * up * (sig * (1.0 + gate * (1.0 - sig)))).astype(dgu_ref.dtype)
        dgu_ref[1] = (dact * silu).astype(dgu_ref.dtype)

    blk = pl.BlockSpec((2, tm, W), lambda j, i: (0, i, j))
    return _pcall(
        body, name=name, grid=(2, S // tm),
        in_specs=[pl.BlockSpec((tm, D), lambda j, i: (i, 0)), pl.BlockSpec((W, D), lambda j, i: (j, 0)), blk],
        out_specs=blk, out_shape=jax.ShapeDtypeStruct((2, S, D_FF), BF16),
        compiler_params=_params(("parallel", "parallel")))(dx, w_down, gu)


def _ffn_dw_up(name, h, dgu, tk=2048):
    S, D = h.shape
    W = FF_SHARD
    tk = min(tk, S)
    return _mm(name, "tn", (N_CHIPS, S // tk), h, pl.BlockSpec((tk, D), lambda s, k: (k, 0)),
               dgu, pl.BlockSpec((None, tk, W), lambda s, k: (s // 2, k, s % 2)),
               jax.ShapeDtypeStruct((N_CHIPS, D, W), BF16), pl.BlockSpec((None, D, W), lambda s, k: (s, 0, 0)),
               (D, W))


def _ffn_dh(name, dgu, w_gu, x, gain, dres, tm=512):
    S = dgu.shape[1]
    W = FF_SHARD

    def body(a_ref, w_hbm, x_ref, g_ref, dres_ref, dx_ref, dg_ref, w_ref, acat_ref, sems):
        first = pl.program_id(0) == 0

        @pl.when(first)
        def _():
            copies = [pltpu.make_async_copy(w_hbm.at[s], w_ref.at[:, pl.ds(s * W, W)], sems.at[s])
                      for s in range(N_CHIPS)]
            for cp in copies:
                cp.start()
            for cp in copies:
                cp.wait()

        acat_ref[:, :D_FF] = a_ref[0]
        acat_ref[:, D_FF:] = a_ref[1]
        dh = _dot(acat_ref[...], w_ref[...], "nt")
        dx, dg = _norm_bwd_rows(x_ref[...], g_ref[...], dh, dres_ref[...])
        dx_ref[...] = dx
        _accumulate(dg_ref, dg, first)

    row = pl.BlockSpec((tm, D_MODEL), lambda i: (i, 0))
    vec = pl.BlockSpec((1, D_MODEL), lambda i: (0, 0))
    return _pcall(body, name=name, grid=(S // tm,),
                  in_specs=[pl.BlockSpec((2, tm, D_FF), lambda i: (0, i, 0)), pl.BlockSpec(memory_space=pl.ANY),
                            row, vec, row],
                  out_specs=[row, vec],
                  out_shape=[jax.ShapeDtypeStruct((S, D_MODEL), F32), jax.ShapeDtypeStruct((1, D_MODEL), F32)],
                  scratch_shapes=[pltpu.VMEM((D_MODEL, 2 * D_FF), BF16), pltpu.VMEM((tm, 2 * D_FF), BF16),
                                  pltpu.SemaphoreType.DMA((N_CHIPS,))],
                  compiler_params=_params(("arbitrary",)))(dgu, w_gu, x, gain, dres)


FOLD_TOKENS = 1024
LANES = 128


def _lane_chunks(width):
    return [slice(c * LANES, (c + 1) * LANES) for c in range(width // LANES)]


def _strided_rows(r, n, dil):
    return pl.ds(r, n) if dil == 1 else pl.ds(r, n, stride=dil)


def _norm_fold(name, g, x, gain, hf):
    S, D = x.shape
    dil = ATTN_DILATIONS[g]
    n = FOLD_TOKENS // dil
    seg = S // dil

    def body(*refs):
        x_ref, g_ref = refs[:2]
        hf_ref, xc_ref = refs[-2:]
        xf = x_ref[...]
        inv = lax.rsqrt(jnp.mean(xf * xf, axis=-1, keepdims=True) + RMS_EPS)
        h = (xf * inv) * g_ref[...]
        if dil == 1:
            hf_ref[0] = h.astype(BF16)
            return
        for c, cols in enumerate(_lane_chunks(D)):
            xc_ref[c] = h[:, cols]
        for r in range(dil):
            for c, cols in enumerate(_lane_chunks(D)):
                hf_ref[r, :, cols] = xc_ref[c, _strided_rows(r, n, dil), :].astype(BF16)

    in_specs = [pl.BlockSpec((FOLD_TOKENS, D), lambda i: (i, 0)), pl.BlockSpec((1, D), lambda i: (0, 0))]
    operands = [x, gain]
    aliases = {}
    if hf is not None:
        in_specs.append(pl.BlockSpec(memory_space=pl.ANY))
        operands.append(hf.reshape(N_GROUPS * dil, seg, D))
        aliases = {2: 0}
    hf = _pcall(body, name=name, grid=(S // FOLD_TOKENS,), in_specs=in_specs,
                out_specs=pl.BlockSpec((dil, n, D), lambda i: (g, i, 0)),
                out_shape=jax.ShapeDtypeStruct((N_GROUPS * dil, seg, D), BF16),
                scratch_shapes=[pltpu.VMEM((D // LANES, FOLD_TOKENS, LANES), F32)], input_output_aliases=aliases,
                compiler_params=_params(("arbitrary",)))(*operands)
    return hf.reshape(N_GROUPS, S, D)


def _qkv_tiles(name, piece, hf, w, qkv, chip, tm=1024):
    S = hf.shape[1]
    per_group = 3 * D_MODEL // QKV_TILE

    def tile(q, c_ref):
        if piece is None:
            return QKV_PIECES * c_ref[0] + q
        return QKV_PIECES * ((c_ref[0] + 1 + q) % N_CHIPS) + piece

    def body(*refs):
        a_ref, w_ref, o_ref = refs[1], refs[2], refs[-1]
        o_ref[...] = _dot(a_ref[...], w_ref[...], "nn").astype(o_ref.dtype)

    if piece is None:
        w_spec = pl.BlockSpec((D_MODEL, QKV_TILE), lambda q, i, c_ref: (0, q))
    else:
        w_spec = pl.BlockSpec((None, D_MODEL, QKV_TILE), lambda q, i, c_ref: ((c_ref[0] + 1 + q) % N_CHIPS, 0, 0))
    in_specs = [pl.BlockSpec((None, tm, D_MODEL), lambda q, i, c_ref: (tile(q, c_ref) // per_group, i, 0)), w_spec]
    operands = [chip, hf, w]
    aliases = {}
    if qkv is not None:
        in_specs.append(pl.BlockSpec(memory_space=pl.ANY))
        operands.append(qkv)
        aliases = {3: 0}
    return _pcall(
        body, prefetch=1, name=name, grid=(N_CHIPS - 1, S // tm), in_specs=in_specs,
        out_specs=pl.BlockSpec((None, tm, QKV_TILE),
                               lambda q, i, c_ref: (tile(q, c_ref) // per_group, i, tile(q, c_ref) % per_group)),
        out_shape=jax.ShapeDtypeStruct((N_GROUPS, S, 3 * D_MODEL), BF16), input_output_aliases=aliases,
        compiler_params=_params(("arbitrary", "arbitrary")))(*operands)


QKV_PIECES = QKV_SHARD // QKV_TILE


def _qkv_dw(name, p, hf, dqkv):
    S = hf.shape[1]
    per_group = 3 * D_MODEL // QKV_TILE
    tile = lambda s: QKV_PIECES * s + p
    return _mm(name, "tn", (N_CHIPS, 1),
               hf, pl.BlockSpec((None, S, D_MODEL), lambda s, k: (tile(s) // per_group, 0, 0)),
               dqkv, pl.BlockSpec((None, S, QKV_TILE), lambda s, k: (tile(s) // per_group, 0, tile(s) % per_group)),
               jax.ShapeDtypeStruct((N_CHIPS, D_MODEL, QKV_TILE), BF16),
               pl.BlockSpec((None, D_MODEL, QKV_TILE), lambda s, k: (s, 0, 0)), None)


def _qkv_dh(name, g, dqkv, w_pieces, dhf, tm=1024):
    S = dqkv.shape[1]
    per_group = 3 * D_MODEL // QKV_TILE

    def body(*refs):
        a_ref, w_hbm = refs[0], refs[1:1 + QKV_PIECES]
        o_ref, w_ref, sems = refs[-3:]

        @pl.when(pl.program_id(0) == 0)
        def _():
            copies = []
            for j in range(per_group):
                t = per_group * g + j
                copies.append(pltpu.make_async_copy(w_hbm[t % QKV_PIECES].at[t // QKV_PIECES],
                                                    w_ref.at[:, pl.ds(j * QKV_TILE, QKV_TILE)], sems.at[j]))
            for cp in copies:
                cp.start()
            for cp in copies:
                cp.wait()

        o_ref[...] = _dot(a_ref[...], w_ref[...], "nt").astype(o_ref.dtype)

    any_spec = pl.BlockSpec(memory_space=pl.ANY)
    in_specs = [pl.BlockSpec((None, tm, 3 * D_MODEL), lambda i: (g, i, 0))] + [any_spec] * QKV_PIECES
    operands = [dqkv] + list(w_pieces)
    aliases = {}
    if dhf is not None:
        in_specs.append(any_spec)
        operands.append(dhf)
        aliases = {1 + QKV_PIECES: 0}
    return _pcall(body, name=name, grid=(S // tm,), in_specs=in_specs,
                  out_specs=pl.BlockSpec((None, tm, D_MODEL), lambda i: (g, i, 0)),
                  out_shape=jax.ShapeDtypeStruct((N_GROUPS, S, D_MODEL), BF16),
                  scratch_shapes=[pltpu.VMEM((D_MODEL, 3 * D_MODEL), BF16), pltpu.SemaphoreType.DMA((per_group,))],
                  input_output_aliases=aliases, compiler_params=_params(("arbitrary",)))(*operands)


def _alibi_coef(g, h):
    vals = [-(2.0 ** (-8.0 * (gg * N_HEADS + h + 1) / (N_GROUPS * N_HEADS))) * ATTN_DILATIONS[gg]
            for gg in range(N_GROUPS)]
    return jnp.where(g == 0, vals[0], jnp.where(g == 1, vals[1], vals[2])).astype(F32)


def _blocks_per_segment(g, S):
    return jnp.right_shift(S // Q_BLOCK, 2 * g)


def _band_bias(pen):
    row = lax.broadcasted_iota(jnp.int32, (Q_BLOCK, 2 * Q_BLOCK), 0)
    col = lax.broadcasted_iota(jnp.int32, (Q_BLOCK, 2 * Q_BLOCK), 1)
    dist = Q_BLOCK + row - col
    valid = jnp.logical_and(dist >= 0, dist <= Q_BLOCK)
    base = jnp.where(valid, jnp.where(col < Q_BLOCK, pen, 0.0), NEG).astype(F32)
    return base, dist.astype(F32)


def _skewed(n_units, stages):
    state = {}
    for step in range(n_units + len(stages) - 1):
        for s, stage in enumerate(stages):
            u = step - s
            if 0 <= u < n_units:
                state[u] = stage(u, state.get(u))
                if s == len(stages) - 1:
                    del state[u]


def _attn_fwd(name, qkv, tq=1024):
    S = qkv.shape[1]
    nqb = tq // Q_BLOCK
    scale = HEAD_DIM ** -0.5

    def body(q_ref, k_ref, kp_ref, v_ref, vp_ref, o_ref, lse_ref, kf_ref, vf_ref):
        g = pl.program_id(0)
        i = pl.program_id(1)
        nb = _blocks_per_segment(g, S)
        kf_ref[:Q_BLOCK] = kp_ref[...]
        kf_ref[Q_BLOCK:] = k_ref[...]
        vf_ref[:Q_BLOCK] = vp_ref[...]
        vf_ref[Q_BLOCK:] = v_ref[...]
        coefs = [_alibi_coef(g, h) for h in range(N_HEADS)]
        units = [(b, h) for b in range(nqb) for h in range(N_HEADS)]
        bias = {}

        def scores(u, _):
            b, h = units[u]
            if b not in bias:
                pen = jnp.where((i * nqb + b) % nb != 0, 0.0, NEG).astype(F32)
                bias.clear()
                bias[b] = _band_bias(pen)
            base, dist = bias[b]
            cols = slice(h * HEAD_DIM, (h + 1) * HEAD_DIM)
            q = q_ref[b * Q_BLOCK:(b + 1) * Q_BLOCK, cols]
            kk = kf_ref[b * Q_BLOCK:(b + 2) * Q_BLOCK, cols]
            return _dot(q, kk, "nt") * scale + (coefs[h] * dist + base)

        def softmax(u, s):
            m = jnp.max(s, axis=-1, keepdims=True)
            p = jnp.exp(s - m)
            den = jnp.sum(p, axis=-1, keepdims=True)
            return (p * (1.0 / den)).astype(BF16), m + jnp.log(den)

        def output(u, st):
            b, h = units[u]
            pn, lse = st
            cols = slice(h * HEAD_DIM, (h + 1) * HEAD_DIM)
            rows = slice(b * Q_BLOCK, (b + 1) * Q_BLOCK)
            o_ref[rows, cols] = _dot(pn, vf_ref[b * Q_BLOCK:(b + 2) * Q_BLOCK, cols], "nn").astype(o_ref.dtype)
            lse_ref[rows, h:h + 1] = lse

        _skewed(len(units), [scores, softmax, output])

    main = lambda c: pl.BlockSpec((None, tq, D_MODEL), lambda g, i: (g, i, c))
    prev = lambda c: pl.BlockSpec((None, Q_BLOCK, D_MODEL), lambda g, i: (g, jnp.maximum(i * nqb - 1, 0), c))
    return _pcall(
        body, name=name, grid=(N_GROUPS, S // tq),
        in_specs=[main(0), main(1), prev(1), main(2), prev(2)],
        out_specs=[pl.BlockSpec((None, tq, D_MODEL), lambda g, i: (g, i, 0)),
                   pl.BlockSpec((None, tq, N_HEADS), lambda g, i: (g, i, 0))],
        out_shape=[jax.ShapeDtypeStruct((N_GROUPS, S, D_MODEL), BF16),
                   jax.ShapeDtypeStruct((N_GROUPS, S, N_HEADS), F32)],
        scratch_shapes=[pltpu.VMEM((tq + Q_BLOCK, D_MODEL), BF16), pltpu.VMEM((tq + Q_BLOCK, D_MODEL), BF16)],
        compiler_params=_params(("parallel", "parallel")))(qkv, qkv, qkv, qkv, qkv)


def _attn_bwd(name, qkv, do, lse, dl, tq=1024):
    S = qkv.shape[1]
    nqb = tq // Q_BLOCK
    n_blocks = S // Q_BLOCK
    scale = HEAD_DIM ** -0.5
    KEYS = 2 * Q_BLOCK

    def body(q_ref, k_ref, v_ref, kp_ref, vp_ref, qn_ref, do_ref, don_ref, lse_ref, lsen_ref, dl_ref, dln_ref,
             out_ref, kf_ref, vf_ref, dk_ref, dv_ref):
        g = pl.program_id(0)
        i = pl.program_id(1)
        nb = _blocks_per_segment(g, S)
        kf_ref[:Q_BLOCK] = kp_ref[...]
        kf_ref[Q_BLOCK:] = k_ref[...]
        vf_ref[:Q_BLOCK] = vp_ref[...]
        vf_ref[Q_BLOCK:] = v_ref[...]
        coefs = [_alibi_coef(g, h) for h in range(N_HEADS)]
        units = [(h, b) for h in range(N_HEADS) for b in range(nqb + 1)]
        bias = {}

        def band(b):
            if b not in bias:
                blk = i * nqb + b
                ok = blk % nb != 0
                if b == nqb:
                    ok = jnp.logical_and(ok, blk < n_blocks)
                bias[b] = _band_bias(jnp.where(ok, 0.0, NEG).astype(F32))
            return bias[b]

        def operands(h, b):
            cols = slice(h * HEAD_DIM, (h + 1) * HEAD_DIM)
            if b == nqb:
                keys = slice(tq, tq + Q_BLOCK)
                return (qn_ref[:, cols], don_ref[:, cols], lsen_ref[:, h:h + 1], dln_ref[:, h:h + 1],
                        kf_ref[keys, cols], vf_ref[keys, cols])
            rows = slice(b * Q_BLOCK, (b + 1) * Q_BLOCK)
            keys = slice(b * Q_BLOCK, b * Q_BLOCK + KEYS)
            return (q_ref[rows, cols], do_ref[rows, cols], lse_ref[rows, h:h + 1], dl_ref[rows, h:h + 1],
                    kf_ref[keys, cols], vf_ref[keys, cols])

        def probs(u, _):
            h, b = units[u]
            q, do_b, lse_b, dl_b, kk, vv = operands(h, b)
            base, dist = band(b)
            if b == nqb:
                base, dist = base[:, :Q_BLOCK], dist[:, :Q_BLOCK]
            p = jnp.exp(_dot(q, kk, "nt") * scale + (coefs[h] * dist + base) - lse_b)
            return p, _dot(do_b, vv, "nt")

        def dscores(u, st):
            h, b = units[u]
            p, dp = st
            dl_b = operands(h, b)[3]
            return ((p * (dp - dl_b)) * scale).astype(BF16), p.astype(BF16)

        def grads(u, st):
            h, b = units[u]
            ds, pb = st
            q, do_b, _, _, kk, _ = operands(h, b)
            cols = slice(h * HEAD_DIM, (h + 1) * HEAD_DIM)
            if b < nqb:
                out_ref[b * Q_BLOCK:(b + 1) * Q_BLOCK, cols] = _dot(ds, kk, "nn").astype(out_ref.dtype)
            if b == 0:
                dk_ref[:Q_BLOCK] = _dot(ds[:, Q_BLOCK:], q, "tn")
                dv_ref[:Q_BLOCK] = _dot(pb[:, Q_BLOCK:], do_b, "tn")
                return None
            dk = _dot(ds, q, "tn")
            dv = _dot(pb, do_b, "tn")
            before = slice((b - 1) * Q_BLOCK, b * Q_BLOCK)
            dk_ref[before] += dk[:Q_BLOCK]
            dv_ref[before] += dv[:Q_BLOCK]
            if b < nqb:
                own = slice(b * Q_BLOCK, (b + 1) * Q_BLOCK)
                dk_ref[own] = dk[Q_BLOCK:]
                dv_ref[own] = dv[Q_BLOCK:]
            else:
                out_ref[:, D_MODEL + h * HEAD_DIM:D_MODEL + (h + 1) * HEAD_DIM] = dk_ref[...].astype(out_ref.dtype)
                out_ref[:, 2 * D_MODEL + h * HEAD_DIM:2 * D_MODEL + (h + 1) * HEAD_DIM] = (
                    dv_ref[...].astype(out_ref.dtype))
            return None

        _skewed(len(units), [probs, dscores, grads])

    nxt_blk = lambda i: jnp.minimum((i + 1) * nqb, n_blocks - 1)
    main = lambda c: pl.BlockSpec((None, tq, D_MODEL), lambda g, i: (g, i, c))
    prev = lambda c: pl.BlockSpec((None, Q_BLOCK, D_MODEL), lambda g, i: (g, jnp.maximum(i * nqb - 1, 0), c))
    row = pl.BlockSpec((None, tq, D_MODEL), lambda g, i: (g, i, 0))
    row_n = pl.BlockSpec((None, Q_BLOCK, D_MODEL), lambda g, i: (g, nxt_blk(i), 0))
    col = pl.BlockSpec((None, tq, N_HEADS), lambda g, i: (g, i, 0))
    col_n = pl.BlockSpec((None, Q_BLOCK, N_HEADS), lambda g, i: (g, nxt_blk(i), 0))
    return _pcall(
        body, name=name, grid=(N_GROUPS, S // tq),
        in_specs=[main(0), main(1), main(2), prev(1), prev(2), row_n, row, row_n, col, col_n, col, col_n],
        out_specs=pl.BlockSpec((None, tq, 3 * D_MODEL), lambda g, i: (g, i, 0)),
        out_shape=jax.ShapeDtypeStruct((N_GROUPS, S, 3 * D_MODEL), BF16),
        scratch_shapes=[pltpu.VMEM((tq + Q_BLOCK, D_MODEL), BF16), pltpu.VMEM((tq + Q_BLOCK, D_MODEL), BF16),
                        pltpu.VMEM((tq, HEAD_DIM), F32), pltpu.VMEM((tq, HEAD_DIM), F32)],
        compiler_params=_params(("parallel", "parallel")))(qkv, qkv, qkv, qkv, qkv, qkv, do, do, lse, lse, dl, dl)


def _group_weights(lse_ref):
    l0, l1, l2 = lse_ref[0], lse_ref[1], lse_ref[2]
    m = jnp.maximum(jnp.maximum(l0, l1), l2)
    e = [jnp.exp(l0 - m), jnp.exp(l1 - m), jnp.exp(l2 - m)]
    den = e[0] + e[1] + e[2]
    return [ei / den for ei in e]


MERGE_TOKENS = 512


def _folded_views(a, tb):
    _, S, C = a.shape
    views, specs = [], []
    for g, dil in enumerate(ATTN_DILATIONS):
        views.append(a.reshape(N_GROUPS * dil, S // dil, C))
        specs.append(pl.BlockSpec((dil, tb // dil, C), lambda i, g=g: (g, i, 0)))
    return views, specs


def _unfold_into(dst_ref, folded_refs):
    for g, (dil, ref) in enumerate(zip(ATTN_DILATIONS, folded_refs)):
        n = ref.shape[1]
        for r in range(dil):
            for c, cols in enumerate(_lane_chunks(ref.shape[2])):
                dst_ref[g, c, _strided_rows(r, n, dil), :] = ref[r, :, cols].astype(F32)


def _merge_fwd(name, o, lse, tb=MERGE_TOKENS):
    S = o.shape[1]

    def body(o0_ref, o1_ref, o2_ref, lse_ref, out_ref, o_ref):
        _unfold_into(o_ref, (o0_ref, o1_ref, o2_ref))
        w = _group_weights(lse_ref)
        for h in range(N_HEADS):
            cols = slice(h * HEAD_DIM, (h + 1) * HEAD_DIM)
            acc = w[0][:, h:h + 1] * o_ref[0, h]
            for gg in range(1, N_GROUPS):
                acc = acc + w[gg][:, h:h + 1] * o_ref[gg, h]
            out_ref[:, cols] = acc.astype(out_ref.dtype)

    views, specs = _folded_views(o, tb)
    return _pcall(body, name=name, grid=(S // tb,),
                  in_specs=specs + [pl.BlockSpec((N_GROUPS, tb, N_HEADS), lambda i: (0, i, 0))],
                  out_specs=pl.BlockSpec((tb, D_MODEL), lambda i: (i, 0)),
                  out_shape=jax.ShapeDtypeStruct((S, D_MODEL), BF16),
                  scratch_shapes=[pltpu.VMEM((N_GROUPS, N_HEADS, tb, HEAD_DIM), F32)],
                  compiler_params=_params(("parallel",)))(*views, lse)


def _merge_bwd(name, dx, w_out, o, lse, tb=MERGE_TOKENS):
    S = o.shape[1]
    n_chunks = sum(ATTN_DILATIONS)

    def body(dx_ref, w_ref, o0_ref, o1_ref, o2_ref, lse_ref, do_hbm, dl_ref, o_ref, dt_ref, df_ref, d_ref, sems):
        i = pl.program_id(0)
        d_ref[...] = _dot(dx_ref[...].astype(BF16), w_ref[...], "nt")
        _unfold_into(o_ref, (o0_ref, o1_ref, o2_ref))
        w = _group_weights(lse_ref)
        for h in range(N_HEADS):
            cols = slice(h * HEAD_DIM, (h + 1) * HEAD_DIM)
            dv = d_ref[:, cols]
            merged = w[0][:, h:h + 1] * o_ref[0, h]
            for gg in range(1, N_GROUPS):
                merged = merged + w[gg][:, h:h + 1] * o_ref[gg, h]
            dsum = jnp.sum(dv * merged, axis=-1, keepdims=True)
            for gg in range(N_GROUPS):
                wg = w[gg][:, h:h + 1]
                dt_ref[gg, h] = wg * dv
                dl_ref[gg, :, h:h + 1] = wg * dsum
        copies = []
        for gg, dil in enumerate(ATTN_DILATIONS):
            n = tb // dil
            for r in range(dil):
                for h, cols in enumerate(_lane_chunks(D_MODEL)):
                    df_ref[gg, r * n:(r + 1) * n, cols] = (
                        dt_ref[gg, h, _strided_rows(r, n, dil), :].astype(df_ref.dtype))
                cp = pltpu.make_async_copy(df_ref.at[gg, pl.ds(r * n, n)],
                                           do_hbm.at[gg, pl.ds(r * (S // dil) + i * n, n)], sems.at[len(copies)])
                cp.start()
                copies.append(cp)
        for cp in copies:
            cp.wait()

    views, specs = _folded_views(o, tb)
    small = pl.BlockSpec((N_GROUPS, tb, N_HEADS), lambda i: (0, i, 0))
    return _pcall(body, name=name, grid=(S // tb,),
                  in_specs=[pl.BlockSpec((tb, D_MODEL), lambda i: (i, 0)),
                            pl.BlockSpec((D_MODEL, D_MODEL), lambda i: (0, 0))] + specs + [small],
                  out_specs=[pl.BlockSpec(memory_space=pl.ANY), small],
                  out_shape=[jax.ShapeDtypeStruct((N_GROUPS, S, D_MODEL), BF16),
                             jax.ShapeDtypeStruct((N_GROUPS, S, N_HEADS), F32)],
                  scratch_shapes=[pltpu.VMEM((N_GROUPS, N_HEADS, tb, HEAD_DIM), F32),
                                  pltpu.VMEM((N_GROUPS, N_HEADS, tb, HEAD_DIM), F32),
                                  pltpu.VMEM((N_GROUPS, tb, D_MODEL), BF16), pltpu.VMEM((tb, D_MODEL), F32),
                                  pltpu.SemaphoreType.DMA((n_chunks,))],
                  compiler_params=_params(("arbitrary",)))(dx, w_out, *views, lse)


def _shift_rows(a, k):
    return pltpu.roll(a, k % a.shape[0], axis=0)


def _pool_level(g, levels):
    return jnp.where(g == 0, levels[0], jnp.where(g == 1, levels[1], jnp.where(g == 2, levels[2], levels[3])))


def _pool_fwd(name, h, w_in, w_group, scale, x_in, tr=1024):
    S, D = h.shape
    H = POOL_HALO

    def body(h_ref, hh_ref, wi_ref, w_ref, sc_ref, x_ref, y_ref, z_ref, xo_ref):
        i = pl.program_id(0)
        g = pl.program_id(1)
        uv = _dot(h_ref[...], wi_ref[...], "nn")
        halo = jnp.where(i > 0, _dot(hh_ref[...], wi_ref[...], "nn"), 0.0)
        s = jnp.concatenate([halo, uv], axis=0)
        levels = []
        for k in (1, 2, 4, 8):
            s = s + _shift_rows(s, k)
            levels.append(s[H:])
        t = i * tr + lax.broadcasted_iota(jnp.int32, (tr, 1), 0)
        cnt = jnp.minimum(t + 1, jnp.left_shift(2, g)).astype(F32)
        y = _pool_level(g, levels) / cnt - uv
        yb = y.astype(BF16)
        z = _dot(yb, w_ref[...], "nn")
        y_ref[...] = yb
        z_ref[...] = z.astype(z_ref.dtype)
        xo_ref[...] = x_ref[...] + z * sc_ref[...]

    blk = pl.BlockSpec((tr, POOL_DIM), lambda i, g: (i, g))
    return _pcall(
        body, name=name, grid=(S // tr, D // POOL_DIM),
        in_specs=[pl.BlockSpec((tr, D), lambda i, g: (i, 0)),
                  pl.BlockSpec((H, D), lambda i, g: (jnp.maximum(i * (tr // H) - 1, 0), 0)),
                  pl.BlockSpec((D, POOL_DIM), lambda i, g: (0, g)),
                  pl.BlockSpec((None, POOL_DIM, POOL_DIM), lambda i, g: (g, 0, 0)),
                  pl.BlockSpec((1, POOL_DIM), lambda i, g: (0, g)), blk],
        out_specs=[blk, blk, blk],
        out_shape=[jax.ShapeDtypeStruct((S, D), BF16), jax.ShapeDtypeStruct((S, D), BF16),
                   jax.ShapeDtypeStruct((S, D), F32)],
        compiler_params=_params(("parallel", "parallel")))(h, h, w_in, w_group, scale, x_in)


def _pool_bwd(name, d_out, z, y, scale, w_group, tr=1024):
    S, D = d_out.shape
    H = POOL_HALO

    def body(d_ref, dn_ref, z_ref, y_ref, sc_ref, w_ref, du_ref, dw_ref, dsc_ref):
        g = pl.program_id(0)
        i = pl.program_id(1)
        dv = d_ref[...]
        dz = jnp.concatenate([dv, dn_ref[...]], axis=0) * sc_ref[...]
        dzb = dz.astype(BF16)
        dy = _dot(dzb, w_ref[...], "nt")
        t = i * tr + lax.broadcasted_iota(jnp.int32, (tr + H, 1), 0)
        cnt = jnp.minimum(t + 1, jnp.left_shift(2, g)).astype(F32)
        s = jnp.where(t < S, dy / cnt, 0.0)
        levels = []
        for k in (1, 2, 4, 8):
            s = s + _shift_rows(s, -k)
            levels.append(s[:tr])
        du_ref[...] = (_pool_level(g, levels) - dy[:tr]).astype(du_ref.dtype)
        dw = _dot(y_ref[...], dzb[:tr], "tn")
        dsc = jnp.sum(dv * z_ref[...].astype(F32), axis=0, keepdims=True)

        @pl.when(i == 0)
        def _():
            dw_ref[...] = dw
            dsc_ref[...] = dsc

        @pl.when(i > 0)
        def _():
            dw_ref[...] += dw
            dsc_ref[...] += dsc

    blk = pl.BlockSpec((tr, POOL_DIM), lambda g, i: (i, g))
    vec = pl.BlockSpec((1, POOL_DIM), lambda g, i: (0, g))
    mat = pl.BlockSpec((None, POOL_DIM, POOL_DIM), lambda g, i: (g, 0, 0))
    nxt = pl.BlockSpec((H, POOL_DIM), lambda g, i: (jnp.minimum((i + 1) * (tr // H), S // H - 1), g))
    return _pcall(
        body, name=name, grid=(D // POOL_DIM, S // tr), in_specs=[blk, nxt, blk, blk, vec, mat],
        out_specs=[blk, mat, vec],
        out_shape=[jax.ShapeDtypeStruct((S, D), BF16), jax.ShapeDtypeStruct((D // POOL_DIM, POOL_DIM, POOL_DIM), F32),
                   jax.ShapeDtypeStruct((1, D), F32)],
        compiler_params=_params(("parallel", "arbitrary")))(d_out, d_out, z, y, scale, w_group)


def _row_tile(rows, cols, target_bytes=1 << 20):
    best = rows
    for tr in range(8, rows + 1, 8):
        if rows % tr == 0 and tr * cols * 4 <= target_bytes:
            best = tr
    return best if best * cols * 4 <= 4 * target_bytes else rows


def _adamw_refs(w_ref, g_ref, m_ref, v_ref, go_ref, d_ref, mo_ref, vo_ref):
    gv = g_ref[...]
    m2 = ADAM_B1 * m_ref[...] + (1.0 - ADAM_B1) * gv
    v2 = ADAM_B2 * v_ref[...] + (1.0 - ADAM_B2) * (gv * gv)
    m_hat = m2 / (1.0 - ADAM_B1 ** ADAM_STEP)
    v_hat = v2 / (1.0 - ADAM_B2 ** ADAM_STEP)
    d_ref[...] = -ADAM_LR * (m_hat / (jnp.sqrt(v_hat) + ADAM_EPS) + ADAM_WD * w_ref[...])
    go_ref[...] = gv
    mo_ref[...] = m2
    vo_ref[...] = v2


def _adamw(name, w, g, m, v):
    R, C = w.shape
    tr = _row_tile(R, C) if R % 8 == 0 else R
    blk = pl.BlockSpec((tr, C), lambda i: (i, 0))
    sds = jax.ShapeDtypeStruct((R, C), F32)
    return _pcall(_adamw_refs, name=name, grid=(R // tr,), in_specs=[blk] * 4, out_specs=[blk] * 4,
                  out_shape=[sds] * 4, compiler_params=_params(("parallel",)))(w, g, m, v)


def _adamw_small(name, items):
    n = len(items)

    def body(*refs):
        for t in range(n):
            _adamw_refs(*refs[4 * t:4 * t + 4], *refs[4 * n + 4 * t:4 * n + 4 * t + 4])

    specs, shapes = [], []
    for w, _, _, _ in items:
        specs += [pl.BlockSpec(w.shape, lambda i: (0, 0))] * 4
        shapes += [jax.ShapeDtypeStruct(w.shape, F32)] * 4
    res = _pcall(body, name=name, grid=(1,), in_specs=specs, out_specs=specs, out_shape=shapes,
                 compiler_params=_params(("arbitrary",)))(*[a for item in items for a in item])
    return [res[4 * t:4 * t + 4] for t in range(n)]


def _sum_leading(name, a, out_dtype):
    n, R, C = a.shape
    tr = _row_tile(R, C) if R % 16 == 0 else R
    if tr % 16:
        tr = R

    def body(a_ref, o_ref):
        acc = a_ref[0].astype(F32)
        for j in range(1, n):
            acc = acc + a_ref[j].astype(F32)
        o_ref[...] = acc.astype(o_ref.dtype)

    return _pcall(body, name=name, grid=(R // tr,), in_specs=[pl.BlockSpec((n, tr, C), lambda i: (0, i, 0))],
                  out_specs=pl.BlockSpec((tr, C), lambda i: (i, 0)), out_shape=jax.ShapeDtypeStruct((R, C), out_dtype),
                  compiler_params=_params(("parallel",)))(a)


def _cast_many(name, items, chip, steps=4):
    tiles = []
    for w, _, dtype in items:
        R = w.shape[1]
        nt = steps if R % (steps * 16) == 0 else 1
        tiles.append((nt, R // nt))

    def body(c_ref, *refs):
        i = pl.program_id(0)
        for t, (nt, _) in enumerate(tiles):
            w_ref, o_ref = refs[t], refs[len(items) + t]

            def cast(w_ref=w_ref, o_ref=o_ref):
                o_ref[...] = w_ref[...].astype(o_ref.dtype)

            if nt == steps:
                cast()
            else:
                pl.when(i < nt)(cast)

    in_specs, out_specs, out_shape = [], [], []
    for (w, layer, dtype), (nt, tr) in zip(items, tiles):
        C = w.shape[2]
        in_specs.append(pl.BlockSpec((None, tr, C), lambda i, c_ref, l=layer, nt=nt: (l, jnp.minimum(i, nt - 1), 0)))
        out_specs.append(pl.BlockSpec((None, tr, C), lambda i, c_ref, nt=nt: (c_ref[0], jnp.minimum(i, nt - 1), 0)))
        out_shape.append(jax.ShapeDtypeStruct((N_CHIPS, w.shape[1], C), dtype))
    return _pcall(body, prefetch=1, name=name, grid=(steps,), in_specs=in_specs, out_specs=out_specs,
                  out_shape=out_shape, compiler_params=_params(("arbitrary",)))(chip, *[w for w, _, _ in items])


def _cast_qkv(name, w, chip, tr=256):
    _, R, C = w.shape

    def body(c_ref, w_ref, own_ref, *piece_refs):
        v = w_ref[...].astype(BF16)
        own_ref[...] = v
        for p, ref in enumerate(piece_refs):
            ref[...] = v[:, p * QKV_TILE:(p + 1) * QKV_TILE]

    piece = pl.BlockSpec((None, tr, QKV_TILE), lambda i, c_ref: (c_ref[0], i, 0))
    return _pcall(body, prefetch=1, name=name, grid=(R // tr,),
                  in_specs=[pl.BlockSpec((None, tr, C), lambda i, c_ref: (0, i, 0))],
                  out_specs=[pl.BlockSpec((tr, C), lambda i, c_ref: (i, 0))] + [piece] * QKV_PIECES,
                  out_shape=[jax.ShapeDtypeStruct((R, C), BF16)]
                  + [jax.ShapeDtypeStruct((N_CHIPS, R, QKV_TILE), BF16)] * QKV_PIECES,
                  compiler_params=_params(("parallel",)))(chip, w)


def _owner_sum(name, mine, landed, chip, core, out=None, layer=None, col=None):
    _, half, C = mine.shape
    k, n_col = col if col is not None else (0, 1)
    tr = _row_tile(half, C)
    if tr % 16:
        tr = half
    nrt = half // tr
    has_out = out is not None

    def body(*refs):
        m_ref, l_ref, o_ref = refs[2], refs[3], refs[-1]
        acc = m_ref[...].astype(F32)
        for j in range(3):
            acc = acc + l_ref[j].astype(F32)
        o_ref[...] = acc

    if layer is None:
        out_shape = jax.ShapeDtypeStruct((2 * half, n_col * C), F32)
        o_spec = pl.BlockSpec((tr, C), lambda i, ch, co: (co[0] * nrt + i, k))
    else:
        out_shape = jax.ShapeDtypeStruct((2, 2 * half, C), F32)
        o_spec = pl.BlockSpec((None, tr, C), lambda i, ch, co: (layer, co[0] * nrt + i, 0))
    in_specs = [pl.BlockSpec((None, tr, C), lambda i, ch, co: (ch[0], i, 0)),
                pl.BlockSpec((3, tr, C), lambda i, ch, co: (0, i, 0))]
    operands = [chip, core, mine, landed]
    aliases = {}
    if has_out:
        in_specs.append(ANY)
        operands.append(out)
        aliases = {4: 0}
    grid_spec = pltpu.PrefetchScalarGridSpec(num_scalar_prefetch=2, grid=(nrt,), in_specs=in_specs, out_specs=o_spec)
    return _pcall(body, name=name, grid_spec=grid_spec, out_shape=out_shape, input_output_aliases=aliases,
                  compiler_params=_params(("parallel",)))(*operands)


def _add_my_half(name, ps, qs, core):
    n = len(ps)

    def body(c_ref, *refs):
        for t in range(n):
            p_ref, q_ref, o_ref = refs[t], refs[n + t], refs[2 * n + t]
            o_ref[...] = (p_ref[...].astype(F32) + q_ref[...].astype(F32)).astype(o_ref.dtype)

    halves = [(p.shape[1] // 2, p.shape[2]) for p in ps]
    mine = [pl.BlockSpec((None, h, c), lambda s, c_ref: (s, c_ref[0], 0)) for h, c in halves]
    whole = [pl.BlockSpec((None, h, c), lambda s, c_ref: (s, 0, 0)) for h, c in halves]
    return _pcall(body, prefetch=1, name=name, grid=(N_CHIPS,), in_specs=mine + whole, out_specs=whole,
                  out_shape=[jax.ShapeDtypeStruct((N_CHIPS, h, c), BF16) for h, c in halves],
                  compiler_params=_params(("parallel",)))(core, *ps, *qs)


ANY = pl.BlockSpec(memory_space=pl.ANY)


def _place():
    x, y, c = lax.axis_index("x"), lax.axis_index("y"), lax.axis_index("c")
    other_chips = [(1 - x, y), (x, 1 - y), (1 - x, 1 - y)]
    return x, y, c, other_chips


def _remote(src, dst, send, recv, k, to):
    return pltpu.make_async_remote_copy(src_ref=src, dst_ref=dst, send_sem=send.at[k], recv_sem=recv.at[k],
                                        device_id=to, device_id_type=MESH)


def _in_place(bufs):
    return dict(operands=bufs, out_shapes=[jax.ShapeDtypeStruct(b.shape, b.dtype) for b in bufs],
                aliases={t: t for t in range(len(bufs))})


def _gather_over_ici(bufs, split):
    n = len(bufs)

    def copies(ins, outs, send, recv, base=0):
        x, y, c, chips = _place()
        me = 2 * x + y
        out = []
        for t in range(n):
            piece = outs[t].at[me]
            if split[t]:
                half = outs[t].shape[1] // 2
                piece = outs[t].at[me, pl.ds(c * half, half)]
            for j, (px, py) in enumerate(chips):
                out.append(_remote(piece, piece, send, recv, base + 3 * t + j, (px, py, c)))
        return out

    return _Rider(n_copies=3 * n, copies=copies, **_in_place(bufs))


def _gather_to_sibling(bufs):
    n = len(bufs)

    def copies(ins, outs, send, recv, base=0):
        x, y, c, chips = _place()
        out = []
        for t in range(n):
            half = outs[t].shape[1] // 2
            for j, (px, py) in enumerate(chips):
                piece = outs[t].at[2 * px + py, pl.ds(c * half, half)]
                out.append(_remote(piece, piece, send, recv, base + 3 * t + j, (x, y, 1 - c)))
        return out

    return _Rider(n_copies=3 * n, copies=copies, **_in_place(bufs))


def _gather_and_pass_rider(buf):
    half = buf.shape[1] // 2

    def pieces(outs):
        x, y, c, chips = _place()
        rows = lambda core: pl.ds(core * half, half)
        mine = outs[0].at[2 * x + y, rows(c)]
        landed = [outs[0].at[2 * px + py, rows(c)] for px, py in chips]
        theirs = [outs[0].at[2 * px + py, rows(1 - c)] for px, py in chips]
        return (x, y, c, chips), mine, landed, theirs

    def start(ins, outs, send, recv, base=0):
        (x, y, c, chips), mine, _, _ = pieces(outs)
        for j, (px, py) in enumerate(chips):
            _remote(mine, mine, send, recv, base + j, (px, py, c)).start()

    def finish(ins, outs, send, recv, base=0):
        (x, y, c, chips), mine, landed, theirs = pieces(outs)
        sibling = (x, y, 1 - c)
        passed = []
        for j, (px, py) in enumerate(chips):
            _remote(landed[j], landed[j], send, recv, base + j, (px, py, c)).wait_recv()
            passed.append(_remote(landed[j], landed[j], send, recv, base + 3 + j, sibling))
            passed[-1].start()
        for j in range(3):
            _remote(theirs[j], theirs[j], send, recv, base + 3 + j, sibling).wait_recv()
        for j, (px, py) in enumerate(chips):
            _remote(mine, mine, send, recv, base + j, (px, py, c)).wait_send()
            passed[j].wait_send()

    return _Rider(n_copies=6, start=start, finish=finish, **_in_place([buf]))


def _swap_halves_rider(parts):
    n = len(parts)

    def copies(ins, outs, send, recv, base=0):
        x, y, c, _ = _place()
        out = []
        for t in range(n):
            half = ins[t].shape[1] // 2
            out.append(_remote(ins[t].at[:, pl.ds((1 - c) * half, half)], outs[t], send, recv, base + t,
                               (x, y, 1 - c)))
        return out

    shapes = [jax.ShapeDtypeStruct((p.shape[0], p.shape[1] // 2, p.shape[2]), p.dtype) for p in parts]
    return _Rider(parts, shapes, {}, n, copies)


def _scatter_rider(sums):
    n = len(sums)

    def copies(ins, outs, send, recv, base=0):
        x, y, c, chips = _place()
        out = []
        for t in range(n):
            for j, (px, py) in enumerate(chips):
                out.append(_remote(ins[t].at[2 * px + py], outs[t].at[j], send, recv, base + 3 * t + j, (px, py, c)))
        return out

    shapes = [jax.ShapeDtypeStruct((3,) + s.shape[1:], s.dtype) for s in sums]
    return _Rider(sums, shapes, {}, 3 * n, copies)


def _share_rider(blocks, pieces):
    def copies(ins, outs, send, recv, base=0):
        x, y, c, _ = _place()
        out = []
        for k, (t, l, col) in enumerate(pieces):
            ref = outs[t] if l is None else outs[t].at[l]
            r2 = ref.shape[0] // 2
            piece = ref.at[pl.ds(c * r2, r2)]
            if col is not None:
                width = ref.shape[1] // col[1]
                piece = ref.at[pl.ds(c * r2, r2), pl.ds(col[0] * width, width)]
            out.append(_remote(piece, piece, send, recv, base + k, (x, y, 1 - c)))
        return out

    return _Rider(n_copies=len(pieces), copies=copies, **_in_place(blocks))


def _gather_small(name, v):
    def body(v_ref, out_ref, send_sem, recv_sem, local_sem):
        x, y, c, _ = _place()
        me = 4 * x + 2 * y + c
        flips = [(fx, fy, fc) for fx in (0, 1) for fy in (0, 1) for fc in (0, 1)][1:]
        local = pltpu.make_async_copy(v_ref, out_ref.at[me], local_sem)
        local.start()
        copies = []
        for j, (fx, fy, fc) in enumerate(flips):
            peer = (x ^ fx, y ^ fy, c ^ fc)
            copies.append(pltpu.make_async_remote_copy(
                src_ref=v_ref, dst_ref=out_ref.at[me], send_sem=send_sem.at[j], recv_sem=recv_sem.at[j],
                device_id=peer, device_id_type=MESH))
        for cp in copies:
            cp.start()
        for j, (fx, fy, fc) in enumerate(flips):
            peer = (x ^ fx, y ^ fy, c ^ fc)
            pltpu.make_async_remote_copy(
                src_ref=v_ref, dst_ref=out_ref.at[4 * peer[0] + 2 * peer[1] + peer[2]], send_sem=send_sem.at[j],
                recv_sem=recv_sem.at[j], device_id=peer, device_id_type=MESH).wait_recv()
        for cp in copies:
            cp.wait_send()
        local.wait()

    return _pcall(body, name=name, in_specs=[ANY], out_specs=ANY,
                  out_shape=jax.ShapeDtypeStruct((8,) + v.shape, v.dtype),
                  scratch_shapes=[pltpu.SemaphoreType.DMA((7,)), pltpu.SemaphoreType.DMA((7,)),
                                  pltpu.SemaphoreType.DMA])(v)


def _fold(a, dil):
    if dil == 1:
        return a
    S, C = a.shape
    return a.reshape(S // dil, dil, C).transpose(1, 0, 2).reshape(S, C)


def _unfold(a, dil):
    if dil == 1:
        return a
    S, C = a.shape
    return a.reshape(dil, S // dil, C).transpose(1, 0, 2).reshape(S, C)


def _fold_groups(a):
    return jnp.stack([_fold(a[g] if a.ndim == 3 else a, d) for g, d in enumerate(ATTN_DILATIONS)])


def _unfold_groups(a):
    return jnp.stack([_unfold(a[g], d) for g, d in enumerate(ATTN_DILATIONS)])


class _NoComm:
    def __init__(self, weights):
        self.weights = weights
        self.grads = {}
        self.chip = jnp.zeros((1,), jnp.int32)

    def w(self, name):
        return self.weights[name]

    def run(self, fn, name, *args, **kw):
        return fn(name, *args, **kw)

    def grad(self, name, value):
        self.grads[name] = value


def _local_step(xs, tgt, attn_norm, ffn_norm, final_norm, comm):
    run = comm.run
    hf = None
    for g in range(N_GROUPS):
        hf = run(_norm_fold, "fold%d" % g, g, xs, attn_norm, hf)
    qkv = run(_qkv_tiles, "qkv_own", None, hf, comm.w("wq_own"), None, comm.chip)
    for p in range(QKV_PIECES):
        qkv = run(_qkv_tiles, "qkv_piece%d" % p, p, hf, comm.w("wq%d" % p), qkv, comm.chip)
    o_f, lse_f = run(_attn_fwd, "attn_fwd", qkv)
    lse_t = _unfold_groups(lse_f)
    merged = run(_merge_fwd, "merge_fwd", o_f, lse_t)
    x1, h1 = run(_proj_res_norm, "attn_out", merged, comm.w("o"), xs, ffn_norm[0:1])
    gu0, act0 = run(_ffn_up, "ffn0_up", h1, comm.w("gu0"))
    x2, h2 = run(_proj_res_norm, "ffn0_down", act0, comm.w("d0"), x1, comm.w("pool_norm"))
    y, z, x3 = run(_pool_fwd, "pool_fwd", h2, comm.w("p"), comm.w("pg"), comm.w("pool_scale"), x2)
    h3 = run(_rms_fwd, "norm_ffn1", x3, ffn_norm[1:2])
    gu1, act1 = run(_ffn_up, "ffn1_up", h3, comm.w("gu1"))
    loss, dx4, d_final = run(_proj_res_loss, "ffn1_down", act1, comm.w("d1"), x3, final_norm, tgt)

    dgu1 = run(_ffn_down_bwd, "ffn1_down_bwd", dx4, comm.w("d1"), gu1)
    comm.grad("d1", run(_mm_tn, "ffn1_down_dw", act1, dx4, FF_SHARD, 2048))
    comm.grad("gu1", run(_ffn_dw_up, "ffn1_up_dw", h3, dgu1))
    dx3, d_ffn1 = run(_ffn_dh, "ffn1_up_dh", dgu1, comm.w("gu1"), x3, ffn_norm[1:2], dx4)

    du, dw_pg, d_scale = run(_pool_bwd, "pool_bwd", dx3, z, y, comm.w("pool_scale"), comm.w("pg"))
    comm.grad("pg", dw_pg)
    comm.grad("p", run(_mm_tn, "pool_in_dw", h2, du, D_MODEL, h2.shape[0]))
    dx2, d_pool = run(_dh_norm_bwd, "pool_in_dh", du, comm.w("p"), x2, comm.w("pool_norm"), dx3)

    dgu0 = run(_ffn_down_bwd, "ffn0_down_bwd", dx2, comm.w("d0"), gu0)
    comm.grad("d0", run(_mm_tn, "ffn0_down_dw", act0, dx2, FF_SHARD, 2048))
    comm.grad("gu0", run(_ffn_dw_up, "ffn0_up_dw", h1, dgu0))
    dx1, d_ffn0 = run(_ffn_dh, "ffn0_up_dh", dgu0, comm.w("gu0"), x1, ffn_norm[0:1], dx2)

    comm.grad("o", run(_mm_tn, "attn_out_dw", merged, dx1, D_MODEL, 2048))
    do_f, dl_t = run(_merge_bwd, "merge_bwd", dx1, comm.w("o"), o_f, lse_t)
    dqkv = run(_attn_bwd, "attn_bwd", qkv, do_f, lse_f, _fold_groups(dl_t))
    for p in range(QKV_PIECES):
        comm.grad("qkv%d" % p, run(_qkv_dw, "qkv_dw%d" % p, p, hf, dqkv))
    dhf = None
    for g in range(N_GROUPS):
        dhf = run(_qkv_dh, "qkv_dh%d" % g, g, dqkv, [comm.w("wq%d" % p) for p in range(QKV_PIECES)], dhf)
    grad_x, d_attn = run(_rms_bwd_folded, "norm_attn_bwd", xs, attn_norm, dhf, dx1)

    small = dict(attn=d_attn, ffn0=d_ffn0, ffn1=d_ffn1, final=d_final, pool=d_pool, scale=d_scale)
    return loss, grad_x, small


TENSORS = ("qkv", "o", "p", "pg", "gu0", "gu1", "d0", "d1")


def _block_of(name):
    if name[:-1] in ("gu", "d"):
        return name[:-1], int(name[-1]), None
    if name[:-1] == "qkv":
        return "qkv", None, (int(name[-1]), QKV_PIECES)
    return name, None, None


class _MeshComm:
    def __init__(self, bufs, chip_arr, core_arr, pg_rows):
        self.bufs = dict(bufs)
        self.chip, self.chip_arr, self.core_arr, self.pg_rows = chip_arr, chip_arr, core_arr, pg_rows
        self.parts, self.sums, self.blocks = {}, {}, {}
        ici = lambda *names: lambda: self.gather(names, True)
        d2d = lambda *names: lambda: self.gather(names, False)
        whole = lambda name: lambda: self.gather_and_pass(name)
        swap = lambda *names: lambda: self.swap(names)
        scatter = lambda *names: lambda: self.scatter(names)
        share = lambda *names: lambda: self.share(names)
        self.plan = {
            "qkv_own": [whole("wq0")],
            "qkv_piece0": [whole("wq1")],
            "qkv_piece1": [whole("wq2")],
            "attn_fwd": [ici("gu0", "o")],
            "merge_fwd": [ici("d0"), d2d("gu0", "o")],
            "attn_out": [ici("p", "pg", "pool_norm", "pool_scale"), d2d("d0")],
            "ffn0_up": [ici("gu1"), d2d("p", "pg")],
            "ffn0_down": [d2d("gu1"), ici("d1")],
            "pool_fwd": [d2d("d1")],
            "ffn1_up_dh": [swap("d1", "gu1")],
            "ffn0_down_bwd": [scatter("gu1")],
            "ffn0_down_dw": [scatter("d1")],
            "ffn0_up_dh": [swap("pg", "p", "d0", "gu0"), share("gu1", "d1")],
            "merge_bwd": [swap("o")],
            "attn_bwd": [scatter("gu0", "o")],
            "qkv_dw0": [scatter("d0")],
            "qkv_dw1": [scatter("p", "pg"), swap("qkv0")],
            "qkv_dw2": [share("gu0", "o", "d0"), scatter("qkv0"), swap("qkv1")],
            "qkv_dh0": [share("qkv0", "p", "pg"), scatter("qkv1"), swap("qkv2")],
            "qkv_dh1": [share("qkv1"), scatter("qkv2")],
            "qkv_dh2": [share("qkv2")],
        }

    def gather_and_pass(self, name):
        return _gather_and_pass_rider(self.bufs[name]), lambda res: self.bufs.update({name: res[0]})

    def gather(self, names, over_ici):
        bufs = [self.bufs[n] for n in names]
        rider = _gather_over_ici(bufs, [n in TENSORS for n in names]) if over_ici else _gather_to_sibling(bufs)
        return rider, lambda res: self.bufs.update(zip(names, res))

    def swap(self, names):
        def done(res):
            sums = _add_my_half("chip_sum_" + "_".join(names), [self.parts[n] for n in names], res, self.core_arr)
            self.sums.update(zip(names, sums))
        return _swap_halves_rider([self.parts[n] for n in names]), done

    def scatter(self, names):
        def done(res):
            for n, landed in zip(names, res):
                key, layer, col = _block_of(n)
                self.blocks[key] = _owner_sum("owner_sum_" + n, self.sums[n], landed, self.chip_arr, self.core_arr,
                                              out=self.blocks.get(key), layer=layer, col=col)
        return _scatter_rider([self.sums[n] for n in names]), done

    def share(self, names):
        keys, pieces = [], []
        for n in names:
            key, layer, col = _block_of(n)
            if key not in keys:
                keys.append(key)
            pieces.append((keys.index(key), layer, col))
        return _share_rider([self.blocks[k] for k in keys], pieces), lambda res: self.blocks.update(zip(keys, res))

    def run(self, fn, name, *args, **kw):
        made = [make() for make in self.plan.get(name, [])]
        if not made:
            return fn(name, *args, **kw)
        riders = [r for r, _ in made]
        out = _ride(riders[0] if len(riders) == 1 else _riders(*riders), fn, name, *args, **kw)
        for r, done in made:
            done(r.results)
        return out

    def w(self, name):
        b = self.bufs[name]
        if name in ("o", "p", "d0", "d1"):
            return b.reshape(-1, b.shape[-1])
        if name == "pg":
            b = b.reshape(N_CHIPS, 4, self.pg_rows, POOL_DIM).transpose(1, 0, 2, 3)
            return b.reshape(4, POOL_DIM, POOL_DIM)
        if name in ("pool_norm", "pool_scale"):
            return b.reshape(1, -1)
        return b

    def grad(self, name, value):
        if name == "pg":
            value = value.reshape(4, N_CHIPS, self.pg_rows, POOL_DIM).transpose(1, 0, 2, 3)
            value = value.reshape(N_CHIPS, 4 * self.pg_rows, POOL_DIM).astype(BF16)
        elif name in ("o", "p", "d0", "d1"):
            value = value.reshape(N_CHIPS, value.shape[0] // N_CHIPS, value.shape[1])
        self.parts[name] = value


def kernel(x, attn_norm, w_qkv, w_attn_out, pool_norm, w_pool_in, w_pool_group, pool_scale, ffn_norm, w_ffn_gate_up, w_ffn_down, final_norm, loss_target, m_attn_norm, m_w_qkv, m_w_attn_out, m_pool_norm, m_w_pool_in, m_w_pool_group, m_pool_scale, m_ffn_norm, m_w_ffn_gate_up, m_w_ffn_down, m_final_norm, v_attn_norm, v_w_qkv, v_w_attn_out, v_pool_norm, v_w_pool_in, v_w_pool_group, v_pool_scale, v_ffn_norm, v_w_ffn_gate_up, v_w_ffn_down, v_final_norm):
    D = D_MODEL
    chip = 2 * lax.axis_index("x") + lax.axis_index("y")
    core = lax.axis_index("c")
    pg_rows = w_pool_group.shape[2]

    chip_arr = chip.astype(jnp.int32).reshape(1)
    core_arr = core.astype(jnp.int32).reshape(1)

    pieces = _cast_qkv("cast_qkv", w_qkv, chip_arr)
    bufs = dict(zip(["wq_own"] + ["wq%d" % p for p in range(QKV_PIECES)], pieces))
    shards = [(w_attn_out, 0, BF16), (w_pool_in, 0, BF16), (w_pool_group.reshape(1, 4 * pg_rows, POOL_DIM), 0, BF16),
              (w_ffn_gate_up, 0, BF16), (w_ffn_gate_up, 1, BF16), (w_ffn_down, 0, BF16), (w_ffn_down, 1, BF16),
              (pool_norm[None], 0, F32), (pool_scale[None], 0, F32)]
    bufs.update(zip(TENSORS[1:] + ("pool_norm", "pool_scale"), _cast_many("cast_rest", shards, chip_arr)))

    comm = _MeshComm(bufs, chip_arr, core_arr, pg_rows)
    loss_part, grad_x, small = _local_step(x[0], loss_target[0], attn_norm, ffn_norm, final_norm.reshape(1, D), comm)
    g_w_qkv, g_w_o, g_w_p, g_w_pg, g_w_gu, g_w_d = [comm.blocks[k] for k in ("qkv", "o", "p", "pg", "gu", "d")]

    rows = jnp.concatenate([small["attn"], small["ffn0"], small["ffn1"], small["final"], small["pool"],
                            small["scale"], jnp.pad(loss_part, ((0, 0), (0, D - 1))), jnp.zeros((1, D), F32)], axis=0)
    all_rows = _gather_small("gather_gain_grads", rows)
    tot = _sum_leading("sum_gain_grads", all_rows, F32)
    loss = tot[6, 0]
    g_attn_norm = tot[0:1]
    g_ffn_norm = tot[1:3]
    g_final_norm = tot[3]
    g_pool_norm = lax.dynamic_slice(tot, (4, chip * POOL_DIM), (1, POOL_DIM))
    g_pool_scale = lax.dynamic_slice(tot, (5, chip * POOL_DIM), (1, POOL_DIM))

    def update(name, w, g, m, v):
        shape = w.shape
        cols = shape[-1]
        as2d = lambda a: a.reshape(-1, cols)
        return [a.reshape(shape) for a in _adamw("adamw_" + name, as2d(w), as2d(g), as2d(m), as2d(v))]

    gains = [(attn_norm, g_attn_norm, m_attn_norm, v_attn_norm), (pool_norm, g_pool_norm, m_pool_norm, v_pool_norm),
             (pool_scale, g_pool_scale, m_pool_scale, v_pool_scale), (ffn_norm, g_ffn_norm, m_ffn_norm, v_ffn_norm),
             (final_norm, g_final_norm, m_final_norm, v_final_norm)]
    as_rows = lambda a: a.reshape(-1, a.shape[-1])
    small_res = _adamw_small("adamw_gains", [tuple(as_rows(a) for a in item) for item in gains])
    small_res = [[a.reshape(item[0].shape) for a in r] for r, item in zip(small_res, gains)]
    results = [
        small_res[0],
        update("w_qkv", w_qkv, g_w_qkv, m_w_qkv, v_w_qkv),
        update("w_attn_out", w_attn_out, g_w_o, m_w_attn_out, v_w_attn_out),
        small_res[1],
        update("w_pool_in", w_pool_in, g_w_p, m_w_pool_in, v_w_pool_in),
        update("w_pool_group", w_pool_group, g_w_pg, m_w_pool_group, v_w_pool_group),
        small_res[2],
        small_res[3],
        update("w_ffn_gate_up", w_ffn_gate_up, g_w_gu, m_w_ffn_gate_up, v_w_ffn_gate_up),
        update("w_ffn_down", w_ffn_down, g_w_d, m_w_ffn_down, v_w_ffn_down),
        small_res[4],
    ]
    grads = [r[0] for r in results]
    deltas = [r[1] for r in results]
    new_m = [r[2] for r in results]
    new_v = [r[3] for r in results]
    return (loss, grad_x[None], *grads, *deltas, *new_m, *new_v)
```

```python
import functools

import jax
import jax.numpy as jnp
from jax import lax
from jax.experimental import pallas as pl
from jax.experimental.pallas import tpu as pltpu

F32 = jnp.float32
BF16 = jnp.bfloat16
MESH = pl.DeviceIdType.MESH

D_MODEL = 1024
N_HEADS = 8
HEAD_DIM = 128
Q_BLOCK = 128
ATTN_DILATIONS = (1, 4, 16)
N_GROUPS = 3
D_FF = 2816
N_CHIPS = 4
FF_SHARD = 2 * D_FF // N_CHIPS
QKV_SHARD = 3 * 3 * D_MODEL // N_CHIPS
QKV_TILE = 768
POOL_WINDOWS = (2, 4, 8, 16)
POOL_DIM = 256
POOL_HALO = 16
RMS_EPS = 1e-6
NEG = -1e30
ADAM_LR = 0.001
ADAM_B1 = 0.9
ADAM_B2 = 0.999
ADAM_EPS = 1e-08
ADAM_WD = 0.01
ADAM_STEP = 10
VMEM_LIMIT = 56 * 1024 * 1024

_DN = {
    "nn": (((1,), (0,)), ((), ())),
    "nt": (((1,), (1,)), ((), ())),
    "tn": (((0,), (0,)), ((), ())),
}


class _Rider:
    def __init__(self, operands, out_shapes, aliases, n_copies, copies=None, start=None, finish=None):
        self.operands = list(operands)
        self.out_shapes = list(out_shapes)
        self.aliases = dict(aliases)
        self.n_copies = n_copies
        self.results = None

        def start_copies(ins, outs, send, recv, base=0):
            for cp in copies(ins, outs, send, recv, base):
                cp.start()

        def wait_copies(ins, outs, send, recv, base=0):
            for cp in copies(ins, outs, send, recv, base):
                cp.wait()

        self.start = start or start_copies
        self.finish = finish or wait_copies


def _riders(*riders):
    operands, out_shapes, aliases, offsets = [], [], {}, []
    n = 0
    for r in riders:
        offsets.append((len(operands), len(out_shapes), n))
        aliases.update({len(operands) + i: len(out_shapes) + o for i, o in r.aliases.items()})
        operands += r.operands
        out_shapes += r.out_shapes
        n += r.n_copies

    def each(which):
        def go(ins, outs, send, recv, base=0):
            for r, (i0, o0, s0) in zip(riders, offsets):
                getattr(r, which)(ins[i0:i0 + len(r.operands)], outs[o0:o0 + len(r.out_shapes)], send, recv,
                                  base + s0)
        return go

    both = _Rider(operands, out_shapes, aliases, n, start=each("start"), finish=each("finish"))
    both.parts = (riders, offsets)
    return both


_pending_rider = []


def _ride(rider, fn, *args, **kw):
    _pending_rider.append(rider)
    out = fn(*args, **kw)
    assert not _pending_rider
    if hasattr(rider, "parts"):
        for r, (_, o0, _) in zip(*rider.parts):
            r.results = rider.results[o0:o0 + len(r.out_shapes)]
    return out


def _pcall(body, prefetch=0, **kw):
    def build(body, kw):
        if not prefetch:
            return pl.pallas_call(body, **kw)
        kw = dict(kw)
        grid_spec = pltpu.PrefetchScalarGridSpec(
            num_scalar_prefetch=prefetch, grid=kw.pop("grid"), in_specs=kw.pop("in_specs"),
            out_specs=kw.pop("out_specs"), scratch_shapes=kw.pop("scratch_shapes", []))
        return pl.pallas_call(body, grid_spec=grid_spec, **kw)

    if not _pending_rider:
        return build(body, kw)
    rider = _pending_rider.pop()
    grid = kw.get("grid", ())
    in_specs = list(kw.get("in_specs", []))
    single = not isinstance(kw["out_shape"], (list, tuple))
    out_shape = [kw["out_shape"]] if single else list(kw["out_shape"])
    out_specs = [kw["out_specs"]] if single else list(kw["out_specs"])
    scratch = list(kw.get("scratch_shapes", []))
    n_in, n_out, n_scr = len(in_specs), len(out_shape), len(scratch)
    r_in, r_out = len(rider.operands), len(rider.out_shapes)

    def wrapped(*refs):
        scalars, refs = refs[:prefetch], refs[prefetch:]
        ins, rins = refs[:n_in], refs[n_in:n_in + r_in]
        outs = refs[n_in + r_in:n_in + r_in + n_out]
        routs = refs[n_in + r_in + n_out:n_in + r_in + n_out + r_out]
        scr = refs[n_in + r_in + n_out + r_out:n_in + r_in + n_out + r_out + n_scr]
        send, recv = refs[-2:]
        ids = [pl.program_id(a) for a in range(len(grid))]
        first, last = True, True
        for a, pid in enumerate(ids):
            first = jnp.logical_and(first, pid == 0)
            last = jnp.logical_and(last, pid == grid[a] - 1)
        if grid:
            pl.when(first)(lambda: rider.start(rins, routs, send, recv))
        else:
            rider.start(rins, routs, send, recv)
        body(*scalars, *ins, *outs, *scr)
        if grid:
            pl.when(last)(lambda: rider.finish(rins, routs, send, recv))
        else:
            rider.finish(rins, routs, send, recv)

    any_spec = pl.BlockSpec(memory_space=pl.ANY)
    kw2 = dict(kw)
    kw2.update(
        in_specs=in_specs + [any_spec] * r_in, out_specs=out_specs + [any_spec] * r_out,
        out_shape=out_shape + rider.out_shapes,
        scratch_shapes=scratch + [pltpu.SemaphoreType.DMA((rider.n_copies,)),
                                  pltpu.SemaphoreType.DMA((rider.n_copies,))],
        input_output_aliases={**kw.get("input_output_aliases", {}),
                              **{prefetch + n_in + i: n_out + o for i, o in rider.aliases.items()}})
    if grid:
        kw2["compiler_params"] = _params(("arbitrary",) * len(grid))
    call = build(wrapped, kw2)

    def run(*operands):
        res = call(*operands, *rider.operands)
        rider.results = list(res[n_out:])
        return res[0] if single else list(res[:n_out])

    return run


def _params(sem):
    return pltpu.CompilerParams(dimension_semantics=sem, vmem_limit_bytes=VMEM_LIMIT)


def _dot(a, b, mode):
    return lax.dot_general(a, b, _DN[mode], preferred_element_type=F32)


def _mm(name, mode, grid, a, a_spec, b, b_spec, out_shape, o_spec, acc_shape, res=None, res_spec=None):
    nk = grid[-1]
    has_res = res is not None

    def body(*refs):
        a_ref, b_ref = refs[0], refs[1]
        res_ref = refs[2] if has_res else None
        o_ref = refs[2 + has_res]
        part = _dot(a_ref[...].astype(BF16), b_ref[...].astype(BF16), mode)
        if nk == 1:
            if has_res:
                part = part + res_ref[...]
            o_ref[...] = part.astype(o_ref.dtype)
            return
        acc_ref = refs[3 + has_res]
        k = pl.program_id(len(grid) - 1)

        @pl.when(k == 0)
        def _():
            acc_ref[...] = part

        @pl.when(k > 0)
        def _():
            acc_ref[...] += part

        @pl.when(k == nk - 1)
        def _():
            r = acc_ref[...]
            if has_res:
                r = r + res_ref[...]
            o_ref[...] = r.astype(o_ref.dtype)

    in_specs = [a_spec, b_spec] + ([res_spec] if has_res else [])
    operands = [a, b] + ([res] if has_res else [])
    scratch = [] if nk == 1 else [pltpu.VMEM(acc_shape, F32)]
    sem = ("parallel",) * (len(grid) - 1) + ("arbitrary",)
    return _pcall(body, name=name, grid=grid, in_specs=in_specs, out_specs=o_spec, out_shape=out_shape,
                  scratch_shapes=scratch, compiler_params=_params(sem))(*operands)


def _mm_tn(name, a, b, tm, tk):
    T, M = a.shape
    N = b.shape[1]
    return _mm(name, "tn", (M // tm, T // tk), a, pl.BlockSpec((tk, tm), lambda i, k: (k, i)),
               b, pl.BlockSpec((tk, N), lambda i, k: (k, 0)),
               jax.ShapeDtypeStruct((M, N), BF16), pl.BlockSpec((tm, N), lambda i, k: (i, 0)), (tm, N))


def _rms_fwd(name, x, g, tr=512):
    S, D = x.shape

    def body(x_ref, g_ref, o_ref):
        xf = x_ref[...]
        r = lax.rsqrt(jnp.mean(xf * xf, axis=-1, keepdims=True) + RMS_EPS)
        o_ref[...] = ((xf * r) * g_ref[...]).astype(o_ref.dtype)

    return _pcall(body, name=name, grid=(S // tr,),
                  in_specs=[pl.BlockSpec((tr, D), lambda i: (i, 0)), pl.BlockSpec((1, D), lambda i: (0, 0))],
                  out_specs=pl.BlockSpec((tr, D), lambda i: (i, 0)),
                  out_shape=jax.ShapeDtypeStruct((S, D), BF16), compiler_params=_params(("parallel",)))(x, g)


def _norm_bwd_rows(xf, gain, dh, dres):
    r = lax.rsqrt(jnp.mean(xf * xf, axis=-1, keepdims=True) + RMS_EPS)
    xh = xf * r
    t = dh * gain
    dx = dres + r * (t - xh * jnp.mean(t * xh, axis=-1, keepdims=True))
    return dx, jnp.sum(dh * xh, axis=0, keepdims=True)


def _accumulate(ref, value, first):
    @pl.when(first)
    def _():
        ref[...] = value

    @pl.when(jnp.logical_not(first))
    def _():
        ref[...] += value


def _rms_bwd_folded(name, x, g, dhf, dres, tb=512):
    S, D = x.shape

    def body(x_ref, g_ref, d0_ref, d1_ref, d2_ref, dres_ref, dx_ref, dg_ref, dh_ref):
        chunks = _lane_chunks(D)
        for c, cols in enumerate(chunks):
            dh_ref[c] = d0_ref[0, :, cols].astype(F32)
        for dil, ref in ((ATTN_DILATIONS[1], d1_ref), (ATTN_DILATIONS[2], d2_ref)):
            n = tb // dil
            for k in range(dil):
                for c, cols in enumerate(chunks):
                    dh_ref[c, _strided_rows(k, n, dil), :] += ref[k, :, cols].astype(F32)
        dh = jnp.concatenate([dh_ref[c] for c in range(len(chunks))], axis=1)
        dx, dg = _norm_bwd_rows(x_ref[...], g_ref[...], dh, dres_ref[...])
        dx_ref[...] = dx
        _accumulate(dg_ref, dg, pl.program_id(0) == 0)

    views, specs = _folded_views(dhf, tb)
    row = pl.BlockSpec((tb, D), lambda i: (i, 0))
    vec = pl.BlockSpec((1, D), lambda i: (0, 0))
    return _pcall(body, name=name, grid=(S // tb,), in_specs=[row, vec] + specs + [row], out_specs=[row, vec],
                  out_shape=[jax.ShapeDtypeStruct((S, D), F32), jax.ShapeDtypeStruct((1, D), F32)],
                  scratch_shapes=[pltpu.VMEM((D // LANES, tb, LANES), F32)],
                  compiler_params=_params(("arbitrary",)))(x, g, *views, dres)


def _proj_res_norm(name, a, w, res, gain, tm=512):
    M, K = a.shape
    D = w.shape[1]

    def body(a_ref, w_ref, res_ref, g_ref, x_ref, h_ref):
        xf = res_ref[...] + _dot(a_ref[...], w_ref[...], "nn")
        x_ref[...] = xf
        r = lax.rsqrt(jnp.mean(xf * xf, axis=-1, keepdims=True) + RMS_EPS)
        h_ref[...] = ((xf * r) * g_ref[...]).astype(h_ref.dtype)

    row = pl.BlockSpec((tm, D), lambda i: (i, 0))
    return _pcall(body, name=name, grid=(M // tm,),
                  in_specs=[pl.BlockSpec((tm, K), lambda i: (i, 0)), pl.BlockSpec((K, D), lambda i: (0, 0)), row,
                            pl.BlockSpec((1, D), lambda i: (0, 0))],
                  out_specs=[row, row],
                  out_shape=[jax.ShapeDtypeStruct((M, D), F32), jax.ShapeDtypeStruct((M, D), BF16)],
                  compiler_params=_params(("parallel",)))(a, w, res, gain)


def _proj_res_loss(name, a, w, res, gain, tgt, tm=512):
    M, K = a.shape
    D = w.shape[1]

    def body(a_ref, w_ref, res_ref, g_ref, t_ref, loss_ref, dx_ref, dg_ref):
        first = pl.program_id(0) == 0
        xf = res_ref[...] + _dot(a_ref[...], w_ref[...], "nn")
        r = lax.rsqrt(jnp.mean(xf * xf, axis=-1, keepdims=True) + RMS_EPS)
        xh = xf * r
        gv = g_ref[...]
        e = xh * gv - t_ref[...]
        lp = 0.5 * jnp.sum(jnp.mean(e * e, axis=-1, keepdims=True), axis=0, keepdims=True)
        dy = e * (1.0 / D)
        t = dy * gv
        dx_ref[...] = r * (t - xh * jnp.mean(t * xh, axis=-1, keepdims=True))
        _accumulate(dg_ref, jnp.sum(dy * xh, axis=0, keepdims=True), first)
        _accumulate(loss_ref, lp, first)

    row = pl.BlockSpec((tm, D), lambda i: (i, 0))
    vec = pl.BlockSpec((1, D), lambda i: (0, 0))
    return _pcall(body, name=name, grid=(M // tm,),
                  in_specs=[pl.BlockSpec((tm, K), lambda i: (i, 0)), pl.BlockSpec((K, D), lambda i: (0, 0)), row,
                            vec, row],
                  out_specs=[pl.BlockSpec((1, 1), lambda i: (0, 0)), row, vec],
                  out_shape=[jax.ShapeDtypeStruct((1, 1), F32), jax.ShapeDtypeStruct((M, D), F32),
                             jax.ShapeDtypeStruct((1, D), F32)],
                  compiler_params=_params(("arbitrary",)))(a, w, res, gain, tgt)


def _dh_norm_bwd(name, d, w, x, gain, dres, tm=512):
    M, N = d.shape
    D = w.shape[0]

    def body(d_ref, w_ref, x_ref, g_ref, dres_ref, dx_ref, dg_ref):
        dh = _dot(d_ref[...].astype(BF16), w_ref[...], "nt")
        dx, dg = _norm_bwd_rows(x_ref[...], g_ref[...], dh, dres_ref[...])
        dx_ref[...] = dx
        _accumulate(dg_ref, dg, pl.program_id(0) == 0)

    row = pl.BlockSpec((tm, D), lambda i: (i, 0))
    vec = pl.BlockSpec((1, D), lambda i: (0, 0))
    return _pcall(body, name=name, grid=(M // tm,),
                  in_specs=[pl.BlockSpec((tm, N), lambda i: (i, 0)), pl.BlockSpec((D, N), lambda i: (0, 0)), row, vec,
                            row],
                  out_specs=[row, vec],
                  out_shape=[jax.ShapeDtypeStruct((M, D), F32), jax.ShapeDtypeStruct((1, D), F32)],
                  compiler_params=_params(("arbitrary",)))(d, w, x, gain, dres)


def _sigmoid(v):
    return 0.5 * jnp.tanh(0.5 * v) + 0.5


def _ffn_up(name, h, w_gu, tm=512):
    S, D = h.shape
    W = FF_SHARD

    def body(h_ref, wg_ref, wu_ref, gu_ref, act_ref):
        hv = h_ref[...]
        gate = _dot(hv, wg_ref[...], "nn")
        up = _dot(hv, wu_ref[...], "nn")
        gu_ref[0] = gate.astype(gu_ref.dtype)
        gu_ref[1] = up.astype(gu_ref.dtype)
        sig = _sigmoid(gate)
        act_ref[...] = ((gate * sig) * up).astype(act_ref.dtype)

    return _pcall(
        body, name=name, grid=(2, S // tm),
        in_specs=[pl.BlockSpec((tm, D), lambda j, i: (i, 0)),
                  pl.BlockSpec((None, D, W), lambda j, i: (j, 0, 0)),
                  pl.BlockSpec((None, D, W), lambda j, i: (j + 2, 0, 0))],
        out_specs=[pl.BlockSpec((2, tm, W), lambda j, i: (0, i, j)), pl.BlockSpec((tm, W), lambda j, i: (i, j))],
        out_shape=[jax.ShapeDtypeStruct((2, S, D_FF), BF16), jax.ShapeDtypeStruct((S, D_FF), BF16)],
        compiler_params=_params(("parallel", "parallel")))(h, w_gu, w_gu)


def _ffn_down_bwd(name, dx, w_down, gu, tm=512):
    S, D = dx.shape
    W = FF_SHARD

    def body(dx_ref, w_ref, gu_ref, dgu_ref):
        dact = _dot(dx_ref[...].astype(BF16), w_ref[...], "nt")
        gate = gu_ref[0].astype(F32)
        up = gu_ref[1].astype(F32)
        sig = _sigmoid(gate)
        silu = gate * sig
        dgu_ref[0] = (dact * up * (sig * (1.0 + gate * (1.0 - sig)))).astype(dgu_ref.dtype)
        dgu_ref[1] = (dact * silu).astype(dgu_ref.dtype)

    blk = pl.BlockSpec((2, tm, W), lambda j, i: (0, i, j))
    return _pcall(
        body, name=name, grid=(2, S // tm),
        in_specs=[pl.BlockSpec((tm, D), lambda j, i: (i, 0)), pl.BlockSpec((W, D), lambda j, i: (j, 0)), blk],
        out_specs=blk, out_shape=jax.ShapeDtypeStruct((2, S, D_FF), BF16),
        compiler_params=_params(("parallel", "parallel")))(dx, w_down, gu)


def _ffn_dw_up(name, h, dgu, tk=2048):
    S, D = h.shape
    W = FF_SHARD
    tk = min(tk, S)
    return _mm(name, "tn", (N_CHIPS, S // tk), h, pl.BlockSpec((tk, D), lambda s, k: (k, 0)),
               dgu, pl.BlockSpec((None, tk, W), lambda s, k: (s // 2, k, s % 2)),
               jax.ShapeDtypeStruct((N_CHIPS, D, W), BF16), pl.BlockSpec((None, D, W), lambda s, k: (s, 0, 0)),
               (D, W))


def _ffn_dh(name, dgu, w_gu, x, gain, dres, tm=512):
    S = dgu.shape[1]
    W = FF_SHARD

    def body(a_ref, w_hbm, x_ref, g_ref, dres_ref, dx_ref, dg_ref, w_ref, acat_ref, sems):
        first = pl.program_id(0) == 0

        @pl.when(first)
        def _():
            copies = [pltpu.make_async_copy(w_hbm.at[s], w_ref.at[:, pl.ds(s * W, W)], sems.at[s])
                      for s in range(N_CHIPS)]
            for cp in copies:
                cp.start()
            for cp in copies:
                cp.wait()

        acat_ref[:, :D_FF] = a_ref[0]
        acat_ref[:, D_FF:] = a_ref[1]
        dh = _dot(acat_ref[...], w_ref[...], "nt")
        dx, dg = _norm_bwd_rows(x_ref[...], g_ref[...], dh, dres_ref[...])
        dx_ref[...] = dx
        _accumulate(dg_ref, dg, first)

    row = pl.BlockSpec((tm, D_MODEL), lambda i: (i, 0))
    vec = pl.BlockSpec((1, D_MODEL), lambda i: (0, 0))
    return _pcall(body, name=name, grid=(S // tm,),
                  in_specs=[pl.BlockSpec((2, tm, D_FF), lambda i: (0, i, 0)), pl.BlockSpec(memory_space=pl.ANY),
                            row, vec, row],
                  out_specs=[row, vec],
                  out_shape=[jax.ShapeDtypeStruct((S, D_MODEL), F32), jax.ShapeDtypeStruct((1, D_MODEL), F32)],
                  scratch_shapes=[pltpu.VMEM((D_MODEL, 2 * D_FF), BF16), pltpu.VMEM((tm, 2 * D_FF), BF16),
                                  pltpu.SemaphoreType.DMA((N_CHIPS,))],
                  compiler_params=_params(("arbitrary",)))(dgu, w_gu, x, gain, dres)


FOLD_TOKENS = 1024
LANES = 128


def _lane_chunks(width):
    return [slice(c * LANES, (c + 1) * LANES) for c in range(width // LANES)]


def _strided_rows(r, n, dil):
    return pl.ds(r, n) if dil == 1 else pl.ds(r, n, stride=dil)


def _norm_fold(name, g, x, gain, hf):
    S, D = x.shape
    dil = ATTN_DILATIONS[g]
    n = FOLD_TOKENS // dil
    seg = S // dil

    def body(*refs):
        x_ref, g_ref = refs[:2]
        hf_ref, xc_ref = refs[-2:]
        xf = x_ref[...]
        inv = lax.rsqrt(jnp.mean(xf * xf, axis=-1, keepdims=True) + RMS_EPS)
        h = (xf * inv) * g_ref[...]
        if dil == 1:
            hf_ref[0] = h.astype(BF16)
            return
        for c, cols in enumerate(_lane_chunks(D)):
            xc_ref[c] = h[:, cols]
        for r in range(dil):
            for c, cols in enumerate(_lane_chunks(D)):
                hf_ref[r, :, cols] = xc_ref[c, _strided_rows(r, n, dil), :].astype(BF16)

    in_specs = [pl.BlockSpec((FOLD_TOKENS, D), lambda i: (i, 0)), pl.BlockSpec((1, D), lambda i: (0, 0))]
    operands = [x, gain]
    aliases = {}
    if hf is not None:
        in_specs.append(pl.BlockSpec(memory_space=pl.ANY))
        operands.append(hf.reshape(N_GROUPS * dil, seg, D))
        aliases = {2: 0}
    hf = _pcall(body, name=name, grid=(S // FOLD_TOKENS,), in_specs=in_specs,
                out_specs=pl.BlockSpec((dil, n, D), lambda i: (g, i, 0)),
                out_shape=jax.ShapeDtypeStruct((N_GROUPS * dil, seg, D), BF16),
                scratch_shapes=[pltpu.VMEM((D // LANES, FOLD_TOKENS, LANES), F32)], input_output_aliases=aliases,
                compiler_params=_params(("arbitrary",)))(*operands)
    return hf.reshape(N_GROUPS, S, D)


def _qkv_tiles(name, piece, hf, w, qkv, chip, tm=1024):
    S = hf.shape[1]
    per_group = 3 * D_MODEL // QKV_TILE

    def tile(q, c_ref):
        if piece is None:
            return QKV_PIECES * c_ref[0] + q
        return QKV_PIECES * ((c_ref[0] + 1 + q) % N_CHIPS) + piece

    def body(*refs):
        a_ref, w_ref, o_ref = refs[1], refs[2], refs[-1]
        o_ref[...] = _dot(a_ref[...], w_ref[...], "nn").astype(o_ref.dtype)

    if piece is None:
        w_spec = pl.BlockSpec((D_MODEL, QKV_TILE), lambda q, i, c_ref: (0, q))
    else:
        w_spec = pl.BlockSpec((None, D_MODEL, QKV_TILE), lambda q, i, c_ref: ((c_ref[0] + 1 + q) % N_CHIPS, 0, 0))
    in_specs = [pl.BlockSpec((None, tm, D_MODEL), lambda q, i, c_ref: (tile(q, c_ref) // per_group, i, 0)), w_spec]
    operands = [chip, hf, w]
    aliases = {}
    if qkv is not None:
        in_specs.append(pl.BlockSpec(memory_space=pl.ANY))
        operands.append(qkv)
        aliases = {3: 0}
    return _pcall(
        body, prefetch=1, name=name, grid=(N_CHIPS - 1, S // tm), in_specs=in_specs,
        out_specs=pl.BlockSpec((None, tm, QKV_TILE),
                               lambda q, i, c_ref: (tile(q, c_ref) // per_group, i, tile(q, c_ref) % per_group)),
        out_shape=jax.ShapeDtypeStruct((N_GROUPS, S, 3 * D_MODEL), BF16), input_output_aliases=aliases,
        compiler_params=_params(("arbitrary", "arbitrary")))(*operands)


QKV_PIECES = QKV_SHARD // QKV_TILE


def _qkv_dw(name, p, hf, dqkv):
    S = hf.shape[1]
    per_group = 3 * D_MODEL // QKV_TILE
    tile = lambda s: QKV_PIECES * s + p
    return _mm(name, "tn", (N_CHIPS, 1),
               hf, pl.BlockSpec((None, S, D_MODEL), lambda s, k: (tile(s) // per_group, 0, 0)),
               dqkv, pl.BlockSpec((None, S, QKV_TILE), lambda s, k: (tile(s) // per_group, 0, tile(s) % per_group)),
               jax.ShapeDtypeStruct((N_CHIPS, D_MODEL, QKV_TILE), BF16),
               pl.BlockSpec((None, D_MODEL, QKV_TILE), lambda s, k: (s, 0, 0)), None)


def _qkv_dh(name, g, dqkv, w_pieces, dhf, tm=1024):
    S = dqkv.shape[1]
    per_group = 3 * D_MODEL // QKV_TILE

    def body(*refs):
        a_ref, w_hbm = refs[0], refs[1:1 + QKV_PIECES]
        o_ref, w_ref, sems = refs[-3:]

        @pl.when(pl.program_id(0) == 0)
        def _():
            copies = []
            for j in range(per_group):
                t = per_group * g + j
                copies.append(pltpu.make_async_copy(w_hbm[t % QKV_PIECES].at[t // QKV_PIECES],
                                                    w_ref.at[:, pl.ds(j * QKV_TILE, QKV_TILE)], sems.at[j]))
            for cp in copies:
                cp.start()
            for cp in copies:
                cp.wait()

        o_ref[...] = _dot(a_ref[...], w_ref[...], "nt").astype(o_ref.dtype)

    any_spec = pl.BlockSpec(memory_space=pl.ANY)
    in_specs = [pl.BlockSpec((None, tm, 3 * D_MODEL), lambda i: (g, i, 0))] + [any_spec] * QKV_PIECES
    operands = [dqkv] + list(w_pieces)
    aliases = {}
    if dhf is not None:
        in_specs.append(any_spec)
        operands.append(dhf)
        aliases = {1 + QKV_PIECES: 0}
    return _pcall(body, name=name, grid=(S // tm,), in_specs=in_specs,
                  out_specs=pl.BlockSpec((None, tm, D_MODEL), lambda i: (g, i, 0)),
                  out_shape=jax.ShapeDtypeStruct((N_GROUPS, S, D_MODEL), BF16),
                  scratch_shapes=[pltpu.VMEM((D_MODEL, 3 * D_MODEL), BF16), pltpu.SemaphoreType.DMA((per_group,))],
                  input_output_aliases=aliases, compiler_params=_params(("arbitrary",)))(*operands)


def _alibi_coef(g, h):
    vals = [-(2.0 ** (-8.0 * (gg * N_HEADS + h + 1) / (N_GROUPS * N_HEADS))) * ATTN_DILATIONS[gg]
            for gg in range(N_GROUPS)]
    return jnp.where(g == 0, vals[0], jnp.where(g == 1, vals[1], vals[2])).astype(F32)


def _blocks_per_segment(g, S):
    return jnp.right_shift(S // Q_BLOCK, 2 * g)


def _band_bias(pen):
    row = lax.broadcasted_iota(jnp.int32, (Q_BLOCK, 2 * Q_BLOCK), 0)
    col = lax.broadcasted_iota(jnp.int32, (Q_BLOCK, 2 * Q_BLOCK), 1)
    dist = Q_BLOCK + row - col
    valid = jnp.logical_and(dist >= 0, dist <= Q_BLOCK)
    base = jnp.where(valid, jnp.where(col < Q_BLOCK, pen, 0.0), NEG).astype(F32)
    return base, dist.astype(F32)


def _skewed(n_units, stages):
    state = {}
    for step in range(n_units + len(stages) - 1):
        for s, stage in enumerate(stages):
            u = step - s
            if 0 <= u < n_units:
                state[u] = stage(u, state.get(u))
                if s == len(stages) - 1:
                    del state[u]


def _attn_fwd(name, qkv, tq=1024):
    S = qkv.shape[1]
    nqb = tq // Q_BLOCK
    scale = HEAD_DIM ** -0.5

    def body(q_ref, k_ref, kp_ref, v_ref, vp_ref, o_ref, lse_ref, kf_ref, vf_ref):
        g = pl.program_id(0)
        i = pl.program_id(1)
        nb = _blocks_per_segment(g, S)
        kf_ref[:Q_BLOCK] = kp_ref[...]
        kf_ref[Q_BLOCK:] = k_ref[...]
        vf_ref[:Q_BLOCK] = vp_ref[...]
        vf_ref[Q_BLOCK:] = v_ref[...]
        coefs = [_alibi_coef(g, h) for h in range(N_HEADS)]
        units = [(b, h) for b in range(nqb) for h in range(N_HEADS)]
        bias = {}

        def scores(u, _):
            b, h = units[u]
            if b not in bias:
                pen = jnp.where((i * nqb + b) % nb != 0, 0.0, NEG).astype(F32)
                bias.clear()
                bias[b] = _band_bias(pen)
            base, dist = bias[b]
            cols = slice(h * HEAD_DIM, (h + 1) * HEAD_DIM)
            q = q_ref[b * Q_BLOCK:(b + 1) * Q_BLOCK, cols]
            kk = kf_ref[b * Q_BLOCK:(b + 2) * Q_BLOCK, cols]
            return _dot(q, kk, "nt") * scale + (coefs[h] * dist + base)

        def softmax(u, s):
            m = jnp.max(s, axis=-1, keepdims=True)
            p = jnp.exp(s - m)
            den = jnp.sum(p, axis=-1, keepdims=True)
            return (p * (1.0 / den)).astype(BF16), m + jnp.log(den)

        def output(u, st):
            b, h = units[u]
            pn, lse = st
            cols = slice(h * HEAD_DIM, (h + 1) * HEAD_DIM)
            rows = slice(b * Q_BLOCK, (b + 1) * Q_BLOCK)
            o_ref[rows, cols] = _dot(pn, vf_ref[b * Q_BLOCK:(b + 2) * Q_BLOCK, cols], "nn").astype(o_ref.dtype)
            lse_ref[rows, h:h + 1] = lse

        _skewed(len(units), [scores, softmax, output])

    main = lambda c: pl.BlockSpec((None, tq, D_MODEL), lambda g, i: (g, i, c))
    prev = lambda c: pl.BlockSpec((None, Q_BLOCK, D_MODEL), lambda g, i: (g, jnp.maximum(i * nqb - 1, 0), c))
    return _pcall(
        body, name=name, grid=(N_GROUPS, S // tq),
        in_specs=[main(0), main(1), prev(1), main(2), prev(2)],
        out_specs=[pl.BlockSpec((None, tq, D_MODEL), lambda g, i: (g, i, 0)),
                   pl.BlockSpec((None, tq, N_HEADS), lambda g, i: (g, i, 0))],
        out_shape=[jax.ShapeDtypeStruct((N_GROUPS, S, D_MODEL), BF16),
                   jax.ShapeDtypeStruct((N_GROUPS, S, N_HEADS), F32)],
        scratch_shapes=[pltpu.VMEM((tq + Q_BLOCK, D_MODEL), BF16), pltpu.VMEM((tq + Q_BLOCK, D_MODEL), BF16)],
        compiler_params=_params(("parallel", "parallel")))(qkv, qkv, qkv, qkv, qkv)


def _attn_bwd(name, qkv, do, lse, dl, tq=1024):
    S = qkv.shape[1]
    nqb = tq // Q_BLOCK
    n_blocks = S // Q_BLOCK
    scale = HEAD_DIM ** -0.5
    KEYS = 2 * Q_BLOCK

    def body(q_ref, k_ref, v_ref, kp_ref, vp_ref, qn_ref, do_ref, don_ref, lse_ref, lsen_ref, dl_ref, dln_ref,
             out_ref, kf_ref, vf_ref, dk_ref, dv_ref):
        g = pl.program_id(0)
        i = pl.program_id(1)
        nb = _blocks_per_segment(g, S)
        kf_ref[:Q_BLOCK] = kp_ref[...]
        kf_ref[Q_BLOCK:] = k_ref[...]
        vf_ref[:Q_BLOCK] = vp_ref[...]
        vf_ref[Q_BLOCK:] = v_ref[...]
        coefs = [_alibi_coef(g, h) for h in range(N_HEADS)]
        units = [(h, b) for h in range(N_HEADS) for b in range(nqb + 1)]
        bias = {}

        def band(b):
            if b not in bias:
                blk = i * nqb + b
                ok = blk % nb != 0
                if b == nqb:
                    ok = jnp.logical_and(ok, blk < n_blocks)
                bias[b] = _band_bias(jnp.where(ok, 0.0, NEG).astype(F32))
            return bias[b]

        def operands(h, b):
            cols = slice(h * HEAD_DIM, (h + 1) * HEAD_DIM)
            if b == nqb:
                keys = slice(tq, tq + Q_BLOCK)
                return (qn_ref[:, cols], don_ref[:, cols], lsen_ref[:, h:h + 1], dln_ref[:, h:h + 1],
                        kf_ref[keys, cols], vf_ref[keys, cols])
            rows = slice(b * Q_BLOCK, (b + 1) * Q_BLOCK)
            keys = slice(b * Q_BLOCK, b * Q_BLOCK + KEYS)
            return (q_ref[rows, cols], do_ref[rows, cols], lse_ref[rows, h:h + 1], dl_ref[rows, h:h + 1],
                    kf_ref[keys, cols], vf_ref[keys, cols])

        def probs(u, _):
            h, b = units[u]
            q, do_b, lse_b, dl_b, kk, vv = operands(h, b)
            base, dist = band(b)
            if b == nqb:
                base, dist = base[:, :Q_BLOCK], dist[:, :Q_BLOCK]
            p = jnp.exp(_dot(q, kk, "nt") * scale + (coefs[h] * dist + base) - lse_b)
            return p, _dot(do_b, vv, "nt")

        def dscores(u, st):
            h, b = units[u]
            p, dp = st
            dl_b = operands(h, b)[3]
            return ((p * (dp - dl_b)) * scale).astype(BF16), p.astype(BF16)

        def grads(u, st):
            h, b = units[u]
            ds, pb = st
            q, do_b, _, _, kk, _ = operands(h, b)
            cols = slice(h * HEAD_DIM, (h + 1) * HEAD_DIM)
            if b < nqb:
                out_ref[b * Q_BLOCK:(b + 1) * Q_BLOCK, cols] = _dot(ds, kk, "nn").astype(out_ref.dtype)
            if b == 0:
                dk_ref[:Q_BLOCK] = _dot(ds[:, Q_BLOCK:], q, "tn")
                dv_ref[:Q_BLOCK] = _dot(pb[:, Q_BLOCK:], do_b, "tn")
                return None
            dk = _dot(ds, q, "tn")
            dv = _dot(pb, do_b, "tn")
            before = slice((b - 1) * Q_BLOCK, b * Q_BLOCK)
            dk_ref[before] += dk[:Q_BLOCK]
            dv_ref[before] += dv[:Q_BLOCK]
            if b < nqb:
                own = slice(b * Q_BLOCK, (b + 1) * Q_BLOCK)
                dk_ref[own] = dk[Q_BLOCK:]
                dv_ref[own] = dv[Q_BLOCK:]
            else:
                out_ref[:, D_MODEL + h * HEAD_DIM:D_MODEL + (h + 1) * HEAD_DIM] = dk_ref[...].astype(out_ref.dtype)
                out_ref[:, 2 * D_MODEL + h * HEAD_DIM:2 * D_MODEL + (h + 1) * HEAD_DIM] = (
                    dv_ref[...].astype(out_ref.dtype))
            return None

        _skewed(len(units), [probs, dscores, grads])

    nxt_blk = lambda i: jnp.minimum((i + 1) * nqb, n_blocks - 1)
    main = lambda c: pl.BlockSpec((None, tq, D_MODEL), lambda g, i: (g, i, c))
    prev = lambda c: pl.BlockSpec((None, Q_BLOCK, D_MODEL), lambda g, i: (g, jnp.maximum(i * nqb - 1, 0), c))
    row = pl.BlockSpec((None, tq, D_MODEL), lambda g, i: (g, i, 0))
    row_n = pl.BlockSpec((None, Q_BLOCK, D_MODEL), lambda g, i: (g, nxt_blk(i), 0))
    col = pl.BlockSpec((None, tq, N_HEADS), lambda g, i: (g, i, 0))
    col_n = pl.BlockSpec((None, Q_BLOCK, N_HEADS), lambda g, i: (g, nxt_blk(i), 0))
    return _pcall(
        body, name=name, grid=(N_GROUPS, S // tq),
        in_specs=[main(0), main(1), main(2), prev(1), prev(2), row_n, row, row_n, col, col_n, col, col_n],
        out_specs=pl.BlockSpec((None, tq, 3 * D_MODEL), lambda g, i: (g, i, 0)),
        out_shape=jax.ShapeDtypeStruct((N_GROUPS, S, 3 * D_MODEL), BF16),
        scratch_shapes=[pltpu.VMEM((tq + Q_BLOCK, D_MODEL), BF16), pltpu.VMEM((tq + Q_BLOCK, D_MODEL), BF16),
                        pltpu.VMEM((tq, HEAD_DIM), F32), pltpu.VMEM((tq, HEAD_DIM), F32)],
        compiler_params=_params(("parallel", "parallel")))(qkv, qkv, qkv, qkv, qkv, qkv, do, do, lse, lse, dl, dl)


def _group_weights(lse_ref):
    l0, l1, l2 = lse_ref[0], lse_ref[1], lse_ref[2]
    m = jnp.maximum(jnp.maximum(l0, l1), l2)
    e = [jnp.exp(l0 - m), jnp.exp(l1 - m), jnp.exp(l2 - m)]
    den = e[0] + e[1] + e[2]
    return [ei / den for ei in e]


MERGE_TOKENS = 512


def _folded_views(a, tb):
    _, S, C = a.shape
    views, specs = [], []
    for g, dil in enumerate(ATTN_DILATIONS):
        views.append(a.reshape(N_GROUPS * dil, S // dil, C))
        specs.append(pl.BlockSpec((dil, tb // dil, C), lambda i, g=g: (g, i, 0)))
    return views, specs


def _unfold_into(dst_ref, folded_refs):
    for g, (dil, ref) in enumerate(zip(ATTN_DILATIONS, folded_refs)):
        n = ref.shape[1]
        for r in range(dil):
            for c, cols in enumerate(_lane_chunks(ref.shape[2])):
                dst_ref[g, c, _strided_rows(r, n, dil), :] = ref[r, :, cols].astype(F32)


def _merge_fwd(name, o, lse, tb=MERGE_TOKENS):
    S = o.shape[1]

    def body(o0_ref, o1_ref, o2_ref, lse_ref, out_ref, o_ref):
        _unfold_into(o_ref, (o0_ref, o1_ref, o2_ref))
        w = _group_weights(lse_ref)
        for h in range(N_HEADS):
            cols = slice(h * HEAD_DIM, (h + 1) * HEAD_DIM)
            acc = w[0][:, h:h + 1] * o_ref[0, h]
            for gg in range(1, N_GROUPS):
                acc = acc + w[gg][:, h:h + 1] * o_ref[gg, h]
            out_ref[:, cols] = acc.astype(out_ref.dtype)

    views, specs = _folded_views(o, tb)
    return _pcall(body, name=name, grid=(S // tb,),
                  in_specs=specs + [pl.BlockSpec((N_GROUPS, tb, N_HEADS), lambda i: (0, i, 0))],
                  out_specs=pl.BlockSpec((tb, D_MODEL), lambda i: (i, 0)),
                  out_shape=jax.ShapeDtypeStruct((S, D_MODEL), BF16),
                  scratch_shapes=[pltpu.VMEM((N_GROUPS, N_HEADS, tb, HEAD_DIM), F32)],
                  compiler_params=_params(("parallel",)))(*views, lse)


def _merge_bwd(name, dx, w_out, o, lse, tb=MERGE_TOKENS):
    S = o.shape[1]
    n_chunks = sum(ATTN_DILATIONS)

    def body(dx_ref, w_ref, o0_ref, o1_ref, o2_ref, lse_ref, do_hbm, dl_ref, o_ref, dt_ref, df_ref, d_ref, sems):
        i = pl.program_id(0)
        d_ref[...] = _dot(dx_ref[...].astype(BF16), w_ref[...], "nt")
        _unfold_into(o_ref, (o0_ref, o1_ref, o2_ref))
        w = _group_weights(lse_ref)
        for h in range(N_HEADS):
            cols = slice(h * HEAD_DIM, (h + 1) * HEAD_DIM)
            dv = d_ref[:, cols]
            merged = w[0][:, h:h + 1] * o_ref[0, h]
            for gg in range(1, N_GROUPS):
                merged = merged + w[gg][:, h:h + 1] * o_ref[gg, h]
            dsum = jnp.sum(dv * merged, axis=-1, keepdims=True)
            for gg in range(N_GROUPS):
                wg = w[gg][:, h:h + 1]
                dt_ref[gg, h] = wg * dv
                dl_ref[gg, :, h:h + 1] = wg * dsum
        copies = []
        for gg, dil in enumerate(ATTN_DILATIONS):
            n = tb // dil
            for r in range(dil):
                for h, cols in enumerate(_lane_chunks(D_MODEL)):
                    df_ref[gg, r * n:(r + 1) * n, cols] = (
                        dt_ref[gg, h, _strided_rows(r, n, dil), :].astype(df_ref.dtype))
                cp = pltpu.make_async_copy(df_ref.at[gg, pl.ds(r * n, n)],
                                           do_hbm.at[gg, pl.ds(r * (S // dil) + i * n, n)], sems.at[len(copies)])
                cp.start()
                copies.append(cp)
        for cp in copies:
            cp.wait()

    views, specs = _folded_views(o, tb)
    small = pl.BlockSpec((N_GROUPS, tb, N_HEADS), lambda i: (0, i, 0))
    return _pcall(body, name=name, grid=(S // tb,),
                  in_specs=[pl.BlockSpec((tb, D_MODEL), lambda i: (i, 0)),
                            pl.BlockSpec((D_MODEL, D_MODEL), lambda i: (0, 0))] + specs + [small],
                  out_specs=[pl.BlockSpec(memory_space=pl.ANY), small],
                  out_shape=[jax.ShapeDtypeStruct((N_GROUPS, S, D_MODEL), BF16),
                             jax.ShapeDtypeStruct((N_GROUPS, S, N_HEADS), F32)],
                  scratch_shapes=[pltpu.VMEM((N_GROUPS, N_HEADS, tb, HEAD_DIM), F32),
                                  pltpu.VMEM((N_GROUPS, N_HEADS, tb, HEAD_DIM), F32),
                                  pltpu.VMEM((N_GROUPS, tb, D_MODEL), BF16), pltpu.VMEM((tb, D_MODEL), F32),
                                  pltpu.SemaphoreType.DMA((n_chunks,))],
                  compiler_params=_params(("arbitrary",)))(dx, w_out, *views, lse)


def _shift_rows(a, k):
    return pltpu.roll(a, k % a.shape[0], axis=0)


def _pool_level(g, levels):
    return jnp.where(g == 0, levels[0], jnp.where(g == 1, levels[1], jnp.where(g == 2, levels[2], levels[3])))


def _pool_fwd(name, h, w_in, w_group, scale, x_in, tr=1024):
    S, D = h.shape
    H = POOL_HALO

    def body(h_ref, hh_ref, wi_ref, w_ref, sc_ref, x_ref, y_ref, z_ref, xo_ref):
        i = pl.program_id(0)
        g = pl.program_id(1)
        uv = _dot(h_ref[...], wi_ref[...], "nn")
        halo = jnp.where(i > 0, _dot(hh_ref[...], wi_ref[...], "nn"), 0.0)
        s = jnp.concatenate([halo, uv], axis=0)
        levels = []
        for k in (1, 2, 4, 8):
            s = s + _shift_rows(s, k)
            levels.append(s[H:])
        t = i * tr + lax.broadcasted_iota(jnp.int32, (tr, 1), 0)
        cnt = jnp.minimum(t + 1, jnp.left_shift(2, g)).astype(F32)
        y = _pool_level(g, levels) / cnt - uv
        yb = y.astype(BF16)
        z = _dot(yb, w_ref[...], "nn")
        y_ref[...] = yb
        z_ref[...] = z.astype(z_ref.dtype)
        xo_ref[...] = x_ref[...] + z * sc_ref[...]

    blk = pl.BlockSpec((tr, POOL_DIM), lambda i, g: (i, g))
    return _pcall(
        body, name=name, grid=(S // tr, D // POOL_DIM),
        in_specs=[pl.BlockSpec((tr, D), lambda i, g: (i, 0)),
                  pl.BlockSpec((H, D), lambda i, g: (jnp.maximum(i * (tr // H) - 1, 0), 0)),
                  pl.BlockSpec((D, POOL_DIM), lambda i, g: (0, g)),
                  pl.BlockSpec((None, POOL_DIM, POOL_DIM), lambda i, g: (g, 0, 0)),
                  pl.BlockSpec((1, POOL_DIM), lambda i, g: (0, g)), blk],
        out_specs=[blk, blk, blk],
        out_shape=[jax.ShapeDtypeStruct((S, D), BF16), jax.ShapeDtypeStruct((S, D), BF16),
                   jax.ShapeDtypeStruct((S, D), F32)],
        compiler_params=_params(("parallel", "parallel")))(h, h, w_in, w_group, scale, x_in)


def _pool_bwd(name, d_out, z, y, scale, w_group, tr=1024):
    S, D = d_out.shape
    H = POOL_HALO

    def body(d_ref, dn_ref, z_ref, y_ref, sc_ref, w_ref, du_ref, dw_ref, dsc_ref):
        g = pl.program_id(0)
        i = pl.program_id(1)
        dv = d_ref[...]
        dz = jnp.concatenate([dv, dn_ref[...]], axis=0) * sc_ref[...]
        dzb = dz.astype(BF16)
        dy = _dot(dzb, w_ref[...], "nt")
        t = i * tr + lax.broadcasted_iota(jnp.int32, (tr + H, 1), 0)
        cnt = jnp.minimum(t + 1, jnp.left_shift(2, g)).astype(F32)
        s = jnp.where(t < S, dy / cnt, 0.0)
        levels = []
        for k in (1, 2, 4, 8):
            s = s + _shift_rows(s, -k)
            levels.append(s[:tr])
        du_ref[...] = (_pool_level(g, levels) - dy[:tr]).astype(du_ref.dtype)
        dw = _dot(y_ref[...], dzb[:tr], "tn")
        dsc = jnp.sum(dv * z_ref[...].astype(F32), axis=0, keepdims=True)

        @pl.when(i == 0)
        def _():
            dw_ref[...] = dw
            dsc_ref[...] = dsc

        @pl.when(i > 0)
        def _():
            dw_ref[...] += dw
            dsc_ref[...] += dsc

    blk = pl.BlockSpec((tr, POOL_DIM), lambda g, i: (i, g))
    vec = pl.BlockSpec((1, POOL_DIM), lambda g, i: (0, g))
    mat = pl.BlockSpec((None, POOL_DIM, POOL_DIM), lambda g, i: (g, 0, 0))
    nxt = pl.BlockSpec((H, POOL_DIM), lambda g, i: (jnp.minimum((i + 1) * (tr // H), S // H - 1), g))
    return _pcall(
        body, name=name, grid=(D // POOL_DIM, S // tr), in_specs=[blk, nxt, blk, blk, vec, mat],
        out_specs=[blk, mat, vec],
        out_shape=[jax.ShapeDtypeStruct((S, D), BF16), jax.ShapeDtypeStruct((D // POOL_DIM, POOL_DIM, POOL_DIM), F32),
                   jax.ShapeDtypeStruct((1, D), F32)],
        compiler_params=_params(("parallel", "arbitrary")))(d_out, d_out, z, y, scale, w_group)


def _row_tile(rows, cols, target_bytes=1 << 20):
    best = rows
    for tr in range(8, rows + 1, 8):
        if rows % tr == 0 and tr * cols * 4 <= target_bytes:
            best = tr
    return best if best * cols * 4 <= 4 * target_bytes else rows


def _adamw_refs(w_ref, g_ref, m_ref, v_ref, go_ref, d_ref, mo_ref, vo_ref):
    gv = g_ref[...]
    m2 = ADAM_B1 * m_ref[...] + (1.0 - ADAM_B1) * gv
    v2 = ADAM_B2 * v_ref[...] + (1.0 - ADAM_B2) * (gv * gv)
    m_hat = m2 / (1.0 - ADAM_B1 ** ADAM_STEP)
    v_hat = v2 / (1.0 - ADAM_B2 ** ADAM_STEP)
    d_ref[...] = -ADAM_LR * (m_hat / (jnp.sqrt(v_hat) + ADAM_EPS) + ADAM_WD * w_ref[...])
    go_ref[...] = gv
    mo_ref[...] = m2
    vo_ref[...] = v2


def _adamw(name, w, g, m, v):
    R, C = w.shape
    tr = _row_tile(R, C) if R % 8 == 0 else R
    blk = pl.BlockSpec((tr, C), lambda i: (i, 0))
    sds = jax.ShapeDtypeStruct((R, C), F32)
    return _pcall(_adamw_refs, name=name, grid=(R // tr,), in_specs=[blk] * 4, out_specs=[blk] * 4,
                  out_shape=[sds] * 4, compiler_params=_params(("parallel",)))(w, g, m, v)


def _adamw_small(name, items):
    n = len(items)

    def body(*refs):
        for t in range(n):
            _adamw_refs(*refs[4 * t:4 * t + 4], *refs[4 * n + 4 * t:4 * n + 4 * t + 4])

    specs, shapes = [], []
    for w, _, _, _ in items:
        specs += [pl.BlockSpec(w.shape, lambda i: (0, 0))] * 4
        shapes += [jax.ShapeDtypeStruct(w.shape, F32)] * 4
    res = _pcall(body, name=name, grid=(1,), in_specs=specs, out_specs=specs, out_shape=shapes,
                 compiler_params=_params(("arbitrary",)))(*[a for item in items for a in item])
    return [res[4 * t:4 * t + 4] for t in range(n)]


def _sum_leading(name, a, out_dtype):
    n, R, C = a.shape
    tr = _row_tile(R, C) if R % 16 == 0 else R
    if tr % 16:
        tr = R

    def body(a_ref, o_ref):
        acc = a_ref[0].astype(F32)
        for j in range(1, n):
            acc = acc + a_ref[j].astype(F32)
        o_ref[...] = acc.astype(o_ref.dtype)

    return _pcall(body, name=name, grid=(R // tr,), in_specs=[pl.BlockSpec((n, tr, C), lambda i: (0, i, 0))],
                  out_specs=pl.BlockSpec((tr, C), lambda i: (i, 0)), out_shape=jax.ShapeDtypeStruct((R, C), out_dtype),
                  compiler_params=_params(("parallel",)))(a)


def _cast_many(name, items, chip, steps=4):
    tiles = []
    for w, _, dtype in items:
        R = w.shape[1]
        nt = steps if R % (steps * 16) == 0 else 1
        tiles.append((nt, R // nt))

    def body(c_ref, *refs):
        i = pl.program_id(0)
        for t, (nt, _) in enumerate(tiles):
            w_ref, o_ref = refs[t], refs[len(items) + t]

            def cast(w_ref=w_ref, o_ref=o_ref):
                o_ref[...] = w_ref[...].astype(o_ref.dtype)

            if nt == steps:
                cast()
            else:
                pl.when(i < nt)(cast)

    in_specs, out_specs, out_shape = [], [], []
    for (w, layer, dtype), (nt, tr) in zip(items, tiles):
        C = w.shape[2]
        in_specs.append(pl.BlockSpec((None, tr, C), lambda i, c_ref, l=layer, nt=nt: (l, jnp.minimum(i, nt - 1), 0)))
        out_specs.append(pl.BlockSpec((None, tr, C), lambda i, c_ref, nt=nt: (c_ref[0], jnp.minimum(i, nt - 1), 0)))
        out_shape.append(jax.ShapeDtypeStruct((N_CHIPS, w.shape[1], C), dtype))
    return _pcall(body, prefetch=1, name=name, grid=(steps,), in_specs=in_specs, out_specs=out_specs,
                  out_shape=out_shape, compiler_params=_params(("arbitrary",)))(chip, *[w for w, _, _ in items])


def _cast_qkv(name, w, chip, tr=256):
    _, R, C = w.shape

    def body(c_ref, w_ref, own_ref, *piece_refs):
        v = w_ref[...].astype(BF16)
        own_ref[...] = v
        for p, ref in enumerate(piece_refs):
            ref[...] = v[:, p * QKV_TILE:(p + 1) * QKV_TILE]

    piece = pl.BlockSpec((None, tr, QKV_TILE), lambda i, c_ref: (c_ref[0], i, 0))
    return _pcall(body, prefetch=1, name=name, grid=(R // tr,),
                  in_specs=[pl.BlockSpec((None, tr, C), lambda i, c_ref: (0, i, 0))],
                  out_specs=[pl.BlockSpec((tr, C), lambda i, c_ref: (i, 0))] + [piece] * QKV_PIECES,
                  out_shape=[jax.ShapeDtypeStruct((R, C), BF16)]
                  + [jax.ShapeDtypeStruct((N_CHIPS, R, QKV_TILE), BF16)] * QKV_PIECES,
                  compiler_params=_params(("parallel",)))(chip, w)


def _owner_sum(name, mine, landed, chip, core, out=None, layer=None, col=None):
    _, half, C = mine.shape
    k, n_col = col if col is not None else (0, 1)
    tr = _row_tile(half, C)
    if tr % 16:
        tr = half
    nrt = half // tr
    has_out = out is not None

    def body(*refs):
        m_ref, l_ref, o_ref = refs[2], refs[3], refs[-1]
        acc = m_ref[...].astype(F32)
        for j in range(3):
            acc = acc + l_ref[j].astype(F32)
        o_ref[...] = acc

    if layer is None:
        out_shape = jax.ShapeDtypeStruct((2 * half, n_col * C), F32)
        o_spec = pl.BlockSpec((tr, C), lambda i, ch, co: (co[0] * nrt + i, k))
    else:
        out_shape = jax.ShapeDtypeStruct((2, 2 * half, C), F32)
        o_spec = pl.BlockSpec((None, tr, C), lambda i, ch, co: (layer, co[0] * nrt + i, 0))
    in_specs = [pl.BlockSpec((None, tr, C), lambda i, ch, co: (ch[0], i, 0)),
                pl.BlockSpec((3, tr, C), lambda i, ch, co: (0, i, 0))]
    operands = [chip, core, mine, landed]
    aliases = {}
    if has_out:
        in_specs.append(ANY)
        operands.append(out)
        aliases = {4: 0}
    grid_spec = pltpu.PrefetchScalarGridSpec(num_scalar_prefetch=2, grid=(nrt,), in_specs=in_specs, out_specs=o_spec)
    return _pcall(body, name=name, grid_spec=grid_spec, out_shape=out_shape, input_output_aliases=aliases,
                  compiler_params=_params(("parallel",)))(*operands)


def _add_my_half(name, ps, qs, core):
    n = len(ps)

    def body(c_ref, *refs):
        for t in range(n):
            p_ref, q_ref, o_ref = refs[t], refs[n + t], refs[2 * n + t]
            o_ref[...] = (p_ref[...].astype(F32) + q_ref[...].astype(F32)).astype(o_ref.dtype)

    halves = [(p.shape[1] // 2, p.shape[2]) for p in ps]
    mine = [pl.BlockSpec((None, h, c), lambda s, c_ref: (s, c_ref[0], 0)) for h, c in halves]
    whole = [pl.BlockSpec((None, h, c), lambda s, c_ref: (s, 0, 0)) for h, c in halves]
    return _pcall(body, prefetch=1, name=name, grid=(N_CHIPS,), in_specs=mine + whole, out_specs=whole,
                  out_shape=[jax.ShapeDtypeStruct((N_CHIPS, h, c), BF16) for h, c in halves],
                  compiler_params=_params(("parallel",)))(core, *ps, *qs)


ANY = pl.BlockSpec(memory_space=pl.ANY)


def _place():
    x, y, c = lax.axis_index("x"), lax.axis_index("y"), lax.axis_index("c")
    other_chips = [(1 - x, y), (x, 1 - y), (1 - x, 1 - y)]
    return x, y, c, other_chips


def _remote(src, dst, send, recv, k, to):
    return pltpu.make_async_remote_copy(src_ref=src, dst_ref=dst, send_sem=send.at[k], recv_sem=recv.at[k],
                                        device_id=to, device_id_type=MESH)


def _in_place(bufs):
    return dict(operands=bufs, out_shapes=[jax.ShapeDtypeStruct(b.shape, b.dtype) for b in bufs],
                aliases={t: t for t in range(len(bufs))})


def _gather_rider(bufs, jobs):
    def copies(ins, outs, send, recv, base=0):
        x, y, c, chips = _place()
        out = []
        for t, over_ici, rows in jobs:
            ref = outs[t]
            if rows is not None:
                a, b, n = rows
                half = ref.shape[1] // 2
                rows = pl.ds(c * half + a * half // n, (b - a) * half // n)
            for px, py in chips:
                slot = 2 * x + y if over_ici else 2 * px + py
                piece = ref.at[slot] if rows is None else ref.at[slot, rows]
                out.append(_remote(piece, piece, send, recv, base + len(out), (px, py, c) if over_ici else (x, y, 1 - c)))
        return out

    return _Rider(n_copies=3 * len(jobs), copies=copies, **_in_place(bufs))


def _gather_and_pass_rider(buf):
    half = buf.shape[1] // 2

    def pieces(outs):
        x, y, c, chips = _place()
        rows = lambda core: pl.ds(core * half, half)
        mine = outs[0].at[2 * x + y, rows(c)]
        landed = [outs[0].at[2 * px + py, rows(c)] for px, py in chips]
        theirs = [outs[0].at[2 * px + py, rows(1 - c)] for px, py in chips]
        return (x, y, c, chips), mine, landed, theirs

    def start(ins, outs, send, recv, base=0):
        (x, y, c, chips), mine, _, _ = pieces(outs)
        for j, (px, py) in enumerate(chips):
            _remote(mine, mine, send, recv, base + j, (px, py, c)).start()

    def finish(ins, outs, send, recv, base=0):
        (x, y, c, chips), mine, landed, theirs = pieces(outs)
        sibling = (x, y, 1 - c)
        passed = []
        for j, (px, py) in enumerate(chips):
            _remote(landed[j], landed[j], send, recv, base + j, (px, py, c)).wait_recv()
            passed.append(_remote(landed[j], landed[j], send, recv, base + 3 + j, sibling))
            passed[-1].start()
        for j in range(3):
            _remote(theirs[j], theirs[j], send, recv, base + 3 + j, sibling).wait_recv()
        for j, (px, py) in enumerate(chips):
            _remote(mine, mine, send, recv, base + j, (px, py, c)).wait_send()
            passed[j].wait_send()

    return _Rider(n_copies=6, start=start, finish=finish, **_in_place([buf]))


def _swap_halves_rider(parts):
    n = len(parts)

    def copies(ins, outs, send, recv, base=0):
        x, y, c, _ = _place()
        out = []
        for t in range(n):
            half = ins[t].shape[1] // 2
            out.append(_remote(ins[t].at[:, pl.ds((1 - c) * half, half)], outs[t], send, recv, base + t,
                               (x, y, 1 - c)))
        return out

    shapes = [jax.ShapeDtypeStruct((p.shape[0], p.shape[1] // 2, p.shape[2]), p.dtype) for p in parts]
    return _Rider(parts, shapes, {}, n, copies)


def _scatter_rider(sums, landed, relations):
    n = len(sums)
    kept = [t for t in range(n) if landed[t] is not None]

    def copies(ins, outs, send, recv, base=0):
        x, y, c, chips = _place()
        out = []
        for t in range(n):
            for j in relations[t]:
                px, py = chips[j]
                out.append(_remote(ins[t].at[2 * px + py], outs[t].at[j], send, recv, base + len(out), (px, py, c)))
        return out

    shapes = [jax.ShapeDtypeStruct((3,) + s.shape[1:], s.dtype) for s in sums]
    return _Rider(list(sums) + [landed[t] for t in kept], shapes, {n + k: t for k, t in enumerate(kept)},
                  sum(len(r) for r in relations), copies)


def _share_rider(blocks, pieces):
    def copies(ins, outs, send, recv, base=0):
        x, y, c, _ = _place()
        out = []
        for k, (t, l, col) in enumerate(pieces):
            ref = outs[t] if l is None else outs[t].at[l]
            r2 = ref.shape[0] // 2
            piece = ref.at[pl.ds(c * r2, r2)]
            if col is not None:
                width = ref.shape[1] // col[1]
                piece = ref.at[pl.ds(c * r2, r2), pl.ds(col[0] * width, width)]
            out.append(_remote(piece, piece, send, recv, base + k, (x, y, 1 - c)))
        return out

    return _Rider(n_copies=len(pieces), copies=copies, **_in_place(blocks))


def _gather_small(name, v):
    def body(v_ref, out_ref, send_sem, recv_sem, local_sem):
        x, y, c, _ = _place()
        me = 4 * x + 2 * y + c
        flips = [(fx, fy, fc) for fx in (0, 1) for fy in (0, 1) for fc in (0, 1)][1:]
        local = pltpu.make_async_copy(v_ref, out_ref.at[me], local_sem)
        local.start()
        copies = []
        for j, (fx, fy, fc) in enumerate(flips):
            peer = (x ^ fx, y ^ fy, c ^ fc)
            copies.append(pltpu.make_async_remote_copy(
                src_ref=v_ref, dst_ref=out_ref.at[me], send_sem=send_sem.at[j], recv_sem=recv_sem.at[j],
                device_id=peer, device_id_type=MESH))
        for cp in copies:
            cp.start()
        for j, (fx, fy, fc) in enumerate(flips):
            peer = (x ^ fx, y ^ fy, c ^ fc)
            pltpu.make_async_remote_copy(
                src_ref=v_ref, dst_ref=out_ref.at[4 * peer[0] + 2 * peer[1] + peer[2]], send_sem=send_sem.at[j],
                recv_sem=recv_sem.at[j], device_id=peer, device_id_type=MESH).wait_recv()
        for cp in copies:
            cp.wait_send()
        local.wait()

    return _pcall(body, name=name, in_specs=[ANY], out_specs=ANY,
                  out_shape=jax.ShapeDtypeStruct((8,) + v.shape, v.dtype),
                  scratch_shapes=[pltpu.SemaphoreType.DMA((7,)), pltpu.SemaphoreType.DMA((7,)),
                                  pltpu.SemaphoreType.DMA])(v)


def _fold(a, dil):
    if dil == 1:
        return a
    S, C = a.shape
    return a.reshape(S // dil, dil, C).transpose(1, 0, 2).reshape(S, C)


def _unfold(a, dil):
    if dil == 1:
        return a
    S, C = a.shape
    return a.reshape(dil, S // dil, C).transpose(1, 0, 2).reshape(S, C)


def _fold_groups(a):
    return jnp.stack([_fold(a[g] if a.ndim == 3 else a, d) for g, d in enumerate(ATTN_DILATIONS)])


def _unfold_groups(a):
    return jnp.stack([_unfold(a[g], d) for g, d in enumerate(ATTN_DILATIONS)])


class _NoComm:
    def __init__(self, weights):
        self.weights = weights
        self.grads = {}
        self.chip = jnp.zeros((1,), jnp.int32)

    def w(self, name):
        return self.weights[name]

    def run(self, fn, name, *args, **kw):
        return fn(name, *args, **kw)

    def grad(self, name, value):
        self.grads[name] = value


def _local_step(xs, tgt, attn_norm, ffn_norm, final_norm, comm):
    run = comm.run
    hf = None
    for g in range(N_GROUPS):
        hf = run(_norm_fold, "fold%d" % g, g, xs, attn_norm, hf)
    qkv = run(_qkv_tiles, "qkv_own", None, hf, comm.w("wq_own"), None, comm.chip)
    for p in range(QKV_PIECES):
        qkv = run(_qkv_tiles, "qkv_piece%d" % p, p, hf, comm.w("wq%d" % p), qkv, comm.chip)
    o_f, lse_f = run(_attn_fwd, "attn_fwd", qkv)
    lse_t = _unfold_groups(lse_f)
    merged = run(_merge_fwd, "merge_fwd", o_f, lse_t)
    x1, h1 = run(_proj_res_norm, "attn_out", merged, comm.w("o"), xs, ffn_norm[0:1])
    gu0, act0 = run(_ffn_up, "ffn0_up", h1, comm.w("gu0"))
    x2, h2 = run(_proj_res_norm, "ffn0_down", act0, comm.w("d0"), x1, comm.w("pool_norm"))
    y, z, x3 = run(_pool_fwd, "pool_fwd", h2, comm.w("p"), comm.w("pg"), comm.w("pool_scale"), x2)
    h3 = run(_rms_fwd, "norm_ffn1", x3, ffn_norm[1:2])
    gu1, act1 = run(_ffn_up, "ffn1_up", h3, comm.w("gu1"))
    loss, dx4, d_final = run(_proj_res_loss, "ffn1_down", act1, comm.w("d1"), x3, final_norm, tgt)

    dgu1 = run(_ffn_down_bwd, "ffn1_down_bwd", dx4, comm.w("d1"), gu1)
    comm.grad("d1", run(_mm_tn, "ffn1_down_dw", act1, dx4, FF_SHARD, 2048))
    comm.grad("gu1", run(_ffn_dw_up, "ffn1_up_dw", h3, dgu1))
    dx3, d_ffn1 = run(_ffn_dh, "ffn1_up_dh", dgu1, comm.w("gu1"), x3, ffn_norm[1:2], dx4)

    du, dw_pg, d_scale = run(_pool_bwd, "pool_bwd", dx3, z, y, comm.w("pool_scale"), comm.w("pg"))
    comm.grad("pg", dw_pg)
    comm.grad("p", run(_mm_tn, "pool_in_dw", h2, du, D_MODEL, h2.shape[0]))
    dx2, d_pool = run(_dh_norm_bwd, "pool_in_dh", du, comm.w("p"), x2, comm.w("pool_norm"), dx3)

    dgu0 = run(_ffn_down_bwd, "ffn0_down_bwd", dx2, comm.w("d0"), gu0)
    comm.grad("d0", run(_mm_tn, "ffn0_down_dw", act0, dx2, FF_SHARD, 2048))
    comm.grad("gu0", run(_ffn_dw_up, "ffn0_up_dw", h1, dgu0))
    dx1, d_ffn0 = run(_ffn_dh, "ffn0_up_dh", dgu0, comm.w("gu0"), x1, ffn_norm[0:1], dx2)

    comm.grad("o", run(_mm_tn, "attn_out_dw", merged, dx1, D_MODEL, 2048))
    do_f, dl_t = run(_merge_bwd, "merge_bwd", dx1, comm.w("o"), o_f, lse_t)
    dqkv = run(_attn_bwd, "attn_bwd", qkv, do_f, lse_f, _fold_groups(dl_t))
    for p in range(QKV_PIECES):
        comm.grad("qkv%d" % p, run(_qkv_dw, "qkv_dw%d" % p, p, hf, dqkv))
    dhf = None
    for g in range(N_GROUPS):
        dhf = run(_qkv_dh, "qkv_dh%d" % g, g, dqkv, [comm.w("wq%d" % p) for p in range(QKV_PIECES)], dhf)
    grad_x, d_attn = run(_rms_bwd_folded, "norm_attn_bwd", xs, attn_norm, dhf, dx1)

    small = dict(attn=d_attn, ffn0=d_ffn0, ffn1=d_ffn1, final=d_final, pool=d_pool, scale=d_scale)
    return loss, grad_x, small


TENSORS = ("qkv", "o", "p", "pg", "gu0", "gu1", "d0", "d1")


def _block_of(name):
    if name[:-1] in ("gu", "d"):
        return name[:-1], int(name[-1]), None
    if name[:-1] == "qkv":
        return "qkv", None, (int(name[-1]), QKV_PIECES)
    return name, None, None


class _MeshComm:
    def __init__(self, bufs, chip_arr, core_arr, pg_rows):
        self.bufs = dict(bufs)
        self.chip, self.chip_arr, self.core_arr, self.pg_rows = chip_arr, chip_arr, core_arr, pg_rows
        self.parts, self.sums, self.blocks, self.landed, self.arrived = {}, {}, {}, {}, {}
        move = lambda ici=(), d2d=(): lambda: self.gather(ici, d2d)
        whole = lambda name: lambda: self.gather_and_pass(name)
        swap = lambda *names: lambda: self.swap(names)
        scatter = lambda *names: lambda: self.scatter(names)
        share = lambda *names: lambda: self.share(names)
        self.plan = {
            "qkv_own": [whole("wq0")],
            "qkv_piece0": [whole("wq1")],
            "qkv_piece1": [whole("wq2")],
            "qkv_piece2": [move(ici=["o", "gu0[0:1/4]"])],
            "attn_fwd": [move(ici=["gu0[1:4/4]"], d2d=["o", "gu0[0:1/4]"])],
            "merge_fwd": [move(ici=["d0[0:1/2]", "p", "pg"], d2d=["gu0[1:4/4]"])],
            "attn_out": [move(ici=["d0[1:2/2]", "pool_norm", "pool_scale"], d2d=["d0[0:1/2]", "p", "pg"])],
            "ffn0_up": [move(ici=["gu1[0:3/4]"], d2d=["d0[1:2/2]"])],
            "ffn0_down": [move(ici=["gu1[3:4/4]", "d1[0:1/2]"], d2d=["gu1[0:3/4]"])],
            "pool_fwd": [move(ici=["d1[1:2/2]"], d2d=["gu1[3:4/4]", "d1[0:1/2]"])],
            "norm_ffn1": [move(d2d=["d1[1:2/2]"])],
            "ffn1_up_dh": [swap("d1", "gu1")],
            "pool_bwd": [scatter("d1{01}")],
            "pool_in_dh": [scatter("d1{2}")],
            "ffn0_down_bwd": [scatter("gu1{01}")],
            "ffn0_down_dw": [scatter("gu1{2}")],
            "ffn0_up_dw": [share("gu1", "d1")],
            "ffn0_up_dh": [swap("pg", "p", "d0", "gu0")],
            "merge_bwd": [swap("o")],
            "attn_bwd": [scatter("gu0", "d0", "o")],
            "qkv_dw0": [scatter("p", "pg")],
            "qkv_dw1": [share("gu0", "o", "d0"), swap("qkv0")],
            "qkv_dw2": [share("p", "pg"), scatter("qkv0"), swap("qkv1")],
            "qkv_dh0": [share("qkv0"), scatter("qkv1"), swap("qkv2")],
            "qkv_dh1": [share("qkv1"), scatter("qkv2")],
            "qkv_dh2": [share("qkv2")],
        }

    def gather_and_pass(self, name):
        return _gather_and_pass_rider(self.bufs[name]), lambda res: self.bufs.update({name: res[0]})

    def gather(self, ici, d2d):
        names, jobs = [], []
        for over_ici, specs in ((True, ici), (False, d2d)):
            for spec in specs:
                name, _, rows = spec.partition("[")
                if name not in names:
                    names.append(name)
                rng = None
                if name in TENSORS:
                    ab, _, n = (rows[:-1] or "0:1/1").partition("/")
                    rng = (int(ab.split(":")[0]), int(ab.split(":")[1]), int(n))
                jobs.append((names.index(name), over_ici, rng))
        return _gather_rider([self.bufs[n] for n in names], jobs), lambda res: self.bufs.update(zip(names, res))

    def swap(self, names):
        def done(res):
            sums = _add_my_half("chip_sum_" + "_".join(names), [self.parts[n] for n in names], res, self.core_arr)
            self.sums.update(zip(names, sums))
        return _swap_halves_rider([self.parts[n] for n in names]), done

    def scatter(self, specs):
        names = [s.partition("{")[0] for s in specs]
        relations = [tuple(int(ch) for ch in s.partition("{")[2][:-1]) or (0, 1, 2) for s in specs]

        def done(res):
            for n, rel, landed in zip(names, relations, res):
                self.landed[n] = landed
                self.arrived[n] = self.arrived.get(n, ()) + rel
                if len(self.arrived[n]) < 3:
                    continue
                key, layer, col = _block_of(n)
                self.blocks[key] = _owner_sum("owner_sum_" + n, self.sums[n], landed, self.chip_arr, self.core_arr,
                                              out=self.blocks.get(key), layer=layer, col=col)
        rider = _scatter_rider([self.sums[n] for n in names], [self.landed.get(n) for n in names], relations)
        return rider, done

    def share(self, names):
        keys, pieces = [], []
        for n in names:
            key, layer, col = _block_of(n)
            if key not in keys:
                keys.append(key)
            pieces.append((keys.index(key), layer, col))
        return _share_rider([self.blocks[k] for k in keys], pieces), lambda res: self.blocks.update(zip(keys, res))

    def run(self, fn, name, *args, **kw):
        made = [make() for make in self.plan.get(name, [])]
        if not made:
            return fn(name, *args, **kw)
        riders = [r for r, _ in made]
        out = _ride(riders[0] if len(riders) == 1 else _riders(*riders), fn, name, *args, **kw)
        for r, done in made:
            done(r.results)
        return out

    def w(self, name):
        b = self.bufs[name]
        if name in ("o", "p", "d0", "d1"):
            return b.reshape(-1, b.shape[-1])
        if name == "pg":
            b = b.reshape(N_CHIPS, 4, self.pg_rows, POOL_DIM).transpose(1, 0, 2, 3)
            return b.reshape(4, POOL_DIM, POOL_DIM)
        if name in ("pool_norm", "pool_scale"):
            return b.reshape(1, -1)
        return b

    def grad(self, name, value):
        if name == "pg":
            value = value.reshape(4, N_CHIPS, self.pg_rows, POOL_DIM).transpose(1, 0, 2, 3)
            value = value.reshape(N_CHIPS, 4 * self.pg_rows, POOL_DIM).astype(BF16)
        elif name in ("o", "p", "d0", "d1"):
            value = value.reshape(N_CHIPS, value.shape[0] // N_CHIPS, value.shape[1])
        self.parts[name] = value


def kernel(x, attn_norm, w_qkv, w_attn_out, pool_norm, w_pool_in, w_pool_group, pool_scale, ffn_norm, w_ffn_gate_up, w_ffn_down, final_norm, loss_target, m_attn_norm, m_w_qkv, m_w_attn_out, m_pool_norm, m_w_pool_in, m_w_pool_group, m_pool_scale, m_ffn_norm, m_w_ffn_gate_up, m_w_ffn_down, m_final_norm, v_attn_norm, v_w_qkv, v_w_attn_out, v_pool_norm, v_w_pool_in, v_w_pool_group, v_pool_scale, v_ffn_norm, v_w_ffn_gate_up, v_w_ffn_down, v_final_norm):
    D = D_MODEL
    chip = 2 * lax.axis_index("x") + lax.axis_index("y")
    core = lax.axis_index("c")
    pg_rows = w_pool_group.shape[2]

    chip_arr = chip.astype(jnp.int32).reshape(1)
    core_arr = core.astype(jnp.int32).reshape(1)

    pieces = _cast_qkv("cast_qkv", w_qkv, chip_arr)
    bufs = dict(zip(["wq_own"] + ["wq%d" % p for p in range(QKV_PIECES)], pieces))
    shards = [(w_attn_out, 0, BF16), (w_pool_in, 0, BF16), (w_pool_group.reshape(1, 4 * pg_rows, POOL_DIM), 0, BF16),
              (w_ffn_gate_up, 0, BF16), (w_ffn_gate_up, 1, BF16), (w_ffn_down, 0, BF16), (w_ffn_down, 1, BF16),
              (pool_norm[None], 0, F32), (pool_scale[None], 0, F32)]
    bufs.update(zip(TENSORS[1:] + ("pool_norm", "pool_scale"), _cast_many("cast_rest", shards, chip_arr)))

    comm = _MeshComm(bufs, chip_arr, core_arr, pg_rows)
    loss_part, grad_x, small = _local_step(x[0], loss_target[0], attn_norm, ffn_norm, final_norm.reshape(1, D), comm)
    g_w_qkv, g_w_o, g_w_p, g_w_pg, g_w_gu, g_w_d = [comm.blocks[k] for k in ("qkv", "o", "p", "pg", "gu", "d")]

    rows = jnp.concatenate([small["attn"], small["ffn0"], small["ffn1"], small["final"], small["pool"],
                            small["scale"], jnp.pad(loss_part, ((0, 0), (0, D - 1))), jnp.zeros((1, D), F32)], axis=0)
    all_rows = _gather_small("gather_gain_grads", rows)
    tot = _sum_leading("sum_gain_grads", all_rows, F32)
    loss = tot[6, 0]
    g_attn_norm = tot[0:1]
    g_ffn_norm = tot[1:3]
    g_final_norm = tot[3]
    g_pool_norm = lax.dynamic_slice(tot, (4, chip * POOL_DIM), (1, POOL_DIM))
    g_pool_scale = lax.dynamic_slice(tot, (5, chip * POOL_DIM), (1, POOL_DIM))

    def update(name, w, g, m, v):
        shape = w.shape
        cols = shape[-1]
        as2d = lambda a: a.reshape(-1, cols)
        return [a.reshape(shape) for a in _adamw("adamw_" + name, as2d(w), as2d(g), as2d(m), as2d(v))]

    gains = [(attn_norm, g_attn_norm, m_attn_norm, v_attn_norm), (pool_norm, g_pool_norm, m_pool_norm, v_pool_norm),
             (pool_scale, g_pool_scale, m_pool_scale, v_pool_scale), (ffn_norm, g_ffn_norm, m_ffn_norm, v_ffn_norm),
             (final_norm, g_final_norm, m_final_norm, v_final_norm)]
    as_rows = lambda a: a.reshape(-1, a.shape[-1])
    small_res = _adamw_small("adamw_gains", [tuple(as_rows(a) for a in item) for item in gains])
    small_res = [[a.reshape(item[0].shape) for a in r] for r, item in zip(small_res, gains)]
    results = [
        small_res[0],
        update("w_qkv", w_qkv, g_w_qkv, m_w_qkv, v_w_qkv),
        update("w_attn_out", w_attn_out, g_w_o, m_w_attn_out, v_w_attn_out),
        small_res[1],
        update("w_pool_in", w_pool_in, g_w_p, m_w_pool_in, v_w_pool_in),
        update("w_pool_group", w_pool_group, g_w_pg, m_w_pool_group, v_w_pool_group),
        small_res[2],
        small_res[3],
        update("w_ffn_gate_up", w_ffn_gate_up, g_w_gu, m_w_ffn_gate_up, v_w_ffn_gate_up),
        update("w_ffn_down", w_ffn_down, g_w_d, m_w_ffn_down, v_w_ffn_down),
        small_res[4],
    ]
    grads = [r[0] for r in results]
    deltas = [r[1] for r in results]
    new_m = [r[2] for r in results]
    new_v = [r[3] for r in results]
    return (loss, grad_x[None], *grads, *deltas, *new_m, *new_v)
```

```python
import functools

import jax
import jax.numpy as jnp
from jax import lax
from jax.experimental import pallas as pl
from jax.experimental.pallas import tpu as pltpu

F32 = jnp.float32
BF16 = jnp.bfloat16
MESH = pl.DeviceIdType.MESH

D_MODEL = 1024
N_HEADS = 8
HEAD_DIM = 128
Q_BLOCK = 128
ATTN_DILATIONS = (1, 4, 16)
N_GROUPS = 3
D_FF = 2816
N_CHIPS = 4
FF_SHARD = 2 * D_FF // N_CHIPS
QKV_SHARD = 3 * 3 * D_MODEL // N_CHIPS
QKV_TILE = 768
POOL_DIM = 256
POOL_HALO = 16
RMS_EPS = 1e-6
NEG = -1e30
ADAM_LR = 0.001
ADAM_B1 = 0.9
ADAM_B2 = 0.999
ADAM_EPS = 1e-08
ADAM_WD = 0.01
ADAM_STEP = 10
VMEM_LIMIT = 56 * 1024 * 1024

_DN = {
    "nn": (((1,), (0,)), ((), ())),
    "nt": (((1,), (1,)), ((), ())),
    "tn": (((0,), (0,)), ((), ())),
}


class _Rider:
    def __init__(self, operands, out_shapes, aliases, n_copies, copies=None, start=None, finish=None):
        self.operands = list(operands)
        self.out_shapes = list(out_shapes)
        self.aliases = dict(aliases)
        self.n_copies = n_copies
        self.results = None

        def start_copies(ins, outs, send, recv, base=0):
            for cp in copies(ins, outs, send, recv, base):
                cp.start()

        def wait_copies(ins, outs, send, recv, base=0):
            for cp in copies(ins, outs, send, recv, base):
                cp.wait()

        self.start = start or start_copies
        self.finish = finish or wait_copies


def _riders(*riders):
    operands, out_shapes, aliases, offsets = [], [], {}, []
    n = 0
    for r in riders:
        offsets.append((len(operands), len(out_shapes), n))
        aliases.update({len(operands) + i: len(out_shapes) + o for i, o in r.aliases.items()})
        operands += r.operands
        out_shapes += r.out_shapes
        n += r.n_copies

    def each(which):
        def go(ins, outs, send, recv, base=0):
            for r, (i0, o0, s0) in zip(riders, offsets):
                getattr(r, which)(ins[i0:i0 + len(r.operands)], outs[o0:o0 + len(r.out_shapes)], send, recv,
                                  base + s0)
        return go

    both = _Rider(operands, out_shapes, aliases, n, start=each("start"), finish=each("finish"))
    both.parts = (riders, offsets)
    return both


_pending_rider = []


def _ride(rider, fn, *args, **kw):
    _pending_rider.append(rider)
    out = fn(*args, **kw)
    assert not _pending_rider
    if hasattr(rider, "parts"):
        for r, (_, o0, _) in zip(*rider.parts):
            r.results = rider.results[o0:o0 + len(r.out_shapes)]
    return out


def _pcall(body, prefetch=0, **kw):
    def build(body, kw):
        if not prefetch:
            return pl.pallas_call(body, **kw)
        kw = dict(kw)
        grid_spec = pltpu.PrefetchScalarGridSpec(
            num_scalar_prefetch=prefetch, grid=kw.pop("grid"), in_specs=kw.pop("in_specs"),
            out_specs=kw.pop("out_specs"), scratch_shapes=kw.pop("scratch_shapes", []))
        return pl.pallas_call(body, grid_spec=grid_spec, **kw)

    if not _pending_rider:
        return build(body, kw)
    rider = _pending_rider.pop()
    grid = kw.get("grid", ())
    in_specs = list(kw.get("in_specs", []))
    single = not isinstance(kw["out_shape"], (list, tuple))
    out_shape = [kw["out_shape"]] if single else list(kw["out_shape"])
    out_specs = [kw["out_specs"]] if single else list(kw["out_specs"])
    scratch = list(kw.get("scratch_shapes", []))
    n_in, n_out, n_scr = len(in_specs), len(out_shape), len(scratch)
    r_in, r_out = len(rider.operands), len(rider.out_shapes)

    def wrapped(*refs):
        scalars, refs = refs[:prefetch], refs[prefetch:]
        ins, rins = refs[:n_in], refs[n_in:n_in + r_in]
        outs = refs[n_in + r_in:n_in + r_in + n_out]
        routs = refs[n_in + r_in + n_out:n_in + r_in + n_out + r_out]
        scr = refs[n_in + r_in + n_out + r_out:n_in + r_in + n_out + r_out + n_scr]
        send, recv = refs[-2:]
        ids = [pl.program_id(a) for a in range(len(grid))]
        first, last = True, True
        for a, pid in enumerate(ids):
            first = jnp.logical_and(first, pid == 0)
            last = jnp.logical_and(last, pid == grid[a] - 1)
        if grid:
            pl.when(first)(lambda: rider.start(rins, routs, send, recv))
        else:
            rider.start(rins, routs, send, recv)
        body(*scalars, *ins, *outs, *scr)
        if grid:
            pl.when(last)(lambda: rider.finish(rins, routs, send, recv))
        else:
            rider.finish(rins, routs, send, recv)

    any_spec = pl.BlockSpec(memory_space=pl.ANY)
    kw2 = dict(kw)
    kw2.update(
        in_specs=in_specs + [any_spec] * r_in, out_specs=out_specs + [any_spec] * r_out,
        out_shape=out_shape + rider.out_shapes,
        scratch_shapes=scratch + [pltpu.SemaphoreType.DMA((rider.n_copies,)),
                                  pltpu.SemaphoreType.DMA((rider.n_copies,))],
        input_output_aliases={**kw.get("input_output_aliases", {}),
                              **{prefetch + n_in + i: n_out + o for i, o in rider.aliases.items()}})
    if grid:
        kw2["compiler_params"] = _params(("arbitrary",) * len(grid))
    call = build(wrapped, kw2)

    def run(*operands):
        res = call(*operands, *rider.operands)
        rider.results = list(res[n_out:])
        return res[0] if single else list(res[:n_out])

    return run


def _params(sem):
    return pltpu.CompilerParams(dimension_semantics=sem, vmem_limit_bytes=VMEM_LIMIT)


def _dot(a, b, mode):
    return lax.dot_general(a, b, _DN[mode], preferred_element_type=F32)


def _mm(name, mode, grid, a, a_spec, b, b_spec, out_shape, o_spec, acc_shape):
    nk = grid[-1]

    def body(a_ref, b_ref, o_ref, *acc):
        part = _dot(a_ref[...].astype(BF16), b_ref[...].astype(BF16), mode)
        if nk == 1:
            o_ref[...] = part.astype(o_ref.dtype)
            return
        acc_ref, = acc
        k = pl.program_id(len(grid) - 1)

        @pl.when(k == 0)
        def _():
            acc_ref[...] = part

        @pl.when(k > 0)
        def _():
            acc_ref[...] += part

        @pl.when(k == nk - 1)
        def _():
            o_ref[...] = acc_ref[...].astype(o_ref.dtype)

    scratch = [] if nk == 1 else [pltpu.VMEM(acc_shape, F32)]
    sem = ("parallel",) * (len(grid) - 1) + ("arbitrary",)
    return _pcall(body, name=name, grid=grid, in_specs=[a_spec, b_spec], out_specs=o_spec, out_shape=out_shape,
                  scratch_shapes=scratch, compiler_params=_params(sem))(a, b)


def _mm_tn(name, a, b, tm, tk):
    T, M = a.shape
    N = b.shape[1]
    return _mm(name, "tn", (M // tm, T // tk), a, pl.BlockSpec((tk, tm), lambda i, k: (k, i)),
               b, pl.BlockSpec((tk, N), lambda i, k: (k, 0)),
               jax.ShapeDtypeStruct((M, N), BF16), pl.BlockSpec((tm, N), lambda i, k: (i, 0)), (tm, N))


def _norm_bwd_rows(xf, gain, dh, dres):
    r = lax.rsqrt(jnp.mean(xf * xf, axis=-1, keepdims=True) + RMS_EPS)
    xh = xf * r
    t = dh * gain
    dx = dres + r * (t - xh * jnp.mean(t * xh, axis=-1, keepdims=True))
    return dx, jnp.sum(dh * xh, axis=0, keepdims=True)


def _accumulate(ref, value, first):
    @pl.when(first)
    def _():
        ref[...] = value

    @pl.when(jnp.logical_not(first))
    def _():
        ref[...] += value


def _rms_bwd_folded(name, x, g, dhf, dres, tb=512):
    S, D = x.shape

    def body(x_ref, g_ref, d0_ref, d1_ref, d2_ref, dres_ref, dx_ref, dg_ref, dh_ref):
        chunks = _lane_chunks(D)
        for c, cols in enumerate(chunks):
            dh_ref[c] = d0_ref[0, :, cols].astype(F32)
        for dil, ref in ((ATTN_DILATIONS[1], d1_ref), (ATTN_DILATIONS[2], d2_ref)):
            n = tb // dil
            for k in range(dil):
                for c, cols in enumerate(chunks):
                    dh_ref[c, _strided_rows(k, n, dil), :] += ref[k, :, cols].astype(F32)
        dh = jnp.concatenate([dh_ref[c] for c in range(len(chunks))], axis=1)
        dx, dg = _norm_bwd_rows(x_ref[...], g_ref[...], dh, dres_ref[...])
        dx_ref[...] = dx
        _accumulate(dg_ref, dg, pl.program_id(0) == 0)

    views, specs = _folded_views(dhf, tb)
    row = pl.BlockSpec((tb, D), lambda i: (i, 0))
    vec = pl.BlockSpec((1, D), lambda i: (0, 0))
    return _pcall(body, name=name, grid=(S // tb,), in_specs=[row, vec] + specs + [row], out_specs=[row, vec],
                  out_shape=[jax.ShapeDtypeStruct((S, D), F32), jax.ShapeDtypeStruct((1, D), F32)],
                  scratch_shapes=[pltpu.VMEM((D // LANES, tb, LANES), F32)],
                  compiler_params=_params(("arbitrary",)))(x, g, *views, dres)


def _proj_res_norm(name, a, w, res, gain, tm=512):
    M, K = a.shape
    D = w.shape[1]

    def body(a_ref, w_ref, res_ref, g_ref, x_ref, h_ref):
        xf = res_ref[...] + _dot(a_ref[...], w_ref[...], "nn")
        x_ref[...] = xf
        r = lax.rsqrt(jnp.mean(xf * xf, axis=-1, keepdims=True) + RMS_EPS)
        h_ref[...] = ((xf * r) * g_ref[...]).astype(h_ref.dtype)

    row = pl.BlockSpec((tm, D), lambda i: (i, 0))
    return _pcall(body, name=name, grid=(M // tm,),
                  in_specs=[pl.BlockSpec((tm, K), lambda i: (i, 0)), pl.BlockSpec((K, D), lambda i: (0, 0)), row,
                            pl.BlockSpec((1, D), lambda i: (0, 0))],
                  out_specs=[row, row],
                  out_shape=[jax.ShapeDtypeStruct((M, D), F32), jax.ShapeDtypeStruct((M, D), BF16)],
                  compiler_params=_params(("parallel",)))(a, w, res, gain)


def _proj_res_loss(name, a, w, res, gain, tgt, tm=512):
    M, K = a.shape
    D = w.shape[1]

    def body(a_ref, w_ref, res_ref, g_ref, t_ref, loss_ref, dx_ref, dg_ref):
        first = pl.program_id(0) == 0
        xf = res_ref[...] + _dot(a_ref[...], w_ref[...], "nn")
        r = lax.rsqrt(jnp.mean(xf * xf, axis=-1, keepdims=True) + RMS_EPS)
        xh = xf * r
        gv = g_ref[...]
        e = xh * gv - t_ref[...]
        lp = 0.5 * jnp.sum(jnp.mean(e * e, axis=-1, keepdims=True), axis=0, keepdims=True)
        dy = e * (1.0 / D)
        t = dy * gv
        dx_ref[...] = r * (t - xh * jnp.mean(t * xh, axis=-1, keepdims=True))
        _accumulate(dg_ref, jnp.sum(dy * xh, axis=0, keepdims=True), first)
        _accumulate(loss_ref, lp, first)

    row = pl.BlockSpec((tm, D), lambda i: (i, 0))
    vec = pl.BlockSpec((1, D), lambda i: (0, 0))
    return _pcall(body, name=name, grid=(M // tm,),
                  in_specs=[pl.BlockSpec((tm, K), lambda i: (i, 0)), pl.BlockSpec((K, D), lambda i: (0, 0)), row,
                            vec, row],
                  out_specs=[pl.BlockSpec((1, 1), lambda i: (0, 0)), row, vec],
                  out_shape=[jax.ShapeDtypeStruct((1, 1), F32), jax.ShapeDtypeStruct((M, D), F32),
                             jax.ShapeDtypeStruct((1, D), F32)],
                  compiler_params=_params(("arbitrary",)))(a, w, res, gain, tgt)


def _dh_norm_bwd(name, d, w, x, gain, dres, tm=512):
    M, N = d.shape
    D = w.shape[0]

    def body(d_ref, w_ref, x_ref, g_ref, dres_ref, dx_ref, dg_ref):
        dh = _dot(d_ref[...].astype(BF16), w_ref[...], "nt")
        dx, dg = _norm_bwd_rows(x_ref[...], g_ref[...], dh, dres_ref[...])
        dx_ref[...] = dx
        _accumulate(dg_ref, dg, pl.program_id(0) == 0)

    row = pl.BlockSpec((tm, D), lambda i: (i, 0))
    vec = pl.BlockSpec((1, D), lambda i: (0, 0))
    return _pcall(body, name=name, grid=(M // tm,),
                  in_specs=[pl.BlockSpec((tm, N), lambda i: (i, 0)), pl.BlockSpec((D, N), lambda i: (0, 0)), row, vec,
                            row],
                  out_specs=[row, vec],
                  out_shape=[jax.ShapeDtypeStruct((M, D), F32), jax.ShapeDtypeStruct((1, D), F32)],
                  compiler_params=_params(("arbitrary",)))(d, w, x, gain, dres)


def _sigmoid(v):
    return 0.5 * jnp.tanh(0.5 * v) + 0.5


def _ffn_up(name, h, w_gu, tm=512):
    S, D = h.shape
    W = FF_SHARD

    def body(h_ref, wg_ref, wu_ref, gu_ref, act_ref):
        hv = h_ref[...]
        gate = _dot(hv, wg_ref[...], "nn")
        up = _dot(hv, wu_ref[...], "nn")
        gu_ref[0] = gate.astype(gu_ref.dtype)
        gu_ref[1] = up.astype(gu_ref.dtype)
        sig = _sigmoid(gate)
        act_ref[...] = ((gate * sig) * up).astype(act_ref.dtype)

    return _pcall(
        body, name=name, grid=(2, S // tm),
        in_specs=[pl.BlockSpec((tm, D), lambda j, i: (i, 0)),
                  pl.BlockSpec((None, D, W), lambda j, i: (j, 0, 0)),
                  pl.BlockSpec((None, D, W), lambda j, i: (j + 2, 0, 0))],
        out_specs=[pl.BlockSpec((2, tm, W), lambda j, i: (0, i, j)), pl.BlockSpec((tm, W), lambda j, i: (i, j))],
        out_shape=[jax.ShapeDtypeStruct((2, S, D_FF), BF16), jax.ShapeDtypeStruct((S, D_FF), BF16)],
        compiler_params=_params(("parallel", "parallel")))(h, w_gu, w_gu)


def _ffn_down_bwd(name, dx, w_down, gu, tm=512):
    S, D = dx.shape
    W = FF_SHARD

    def body(dx_ref, w_ref, gu_ref, dgu_ref):
        dact = _dot(dx_ref[...].astype(BF16), w_ref[...], "nt")
        gate = gu_ref[0].astype(F32)
        up = gu_ref[1].astype(F32)
        sig = _sigmoid(gate)
        silu = gate * sig
        dgu_ref[0] = (dact * up * (sig * (1.0 + gate * (1.0 - sig)))).astype(dgu_ref.dtype)
        dgu_ref[1] = (dact * silu).astype(dgu_ref.dtype)

    blk = pl.BlockSpec((2, tm, W), lambda j, i: (0, i, j))
    return _pcall(
        body, name=name, grid=(2, S // tm),
        in_specs=[pl.BlockSpec((tm, D), lambda j, i: (i, 0)), pl.BlockSpec((W, D), lambda j, i: (j, 0)), blk],
        out_specs=blk, out_shape=jax.ShapeDtypeStruct((2, S, D_FF), BF16),
        compiler_params=_params(("parallel", "parallel")))(dx, w_down, gu)


def _ffn_dw_up(name, h, dgu, tk=2048):
    S, D = h.shape
    W = FF_SHARD
    tk = min(tk, S)
    return _mm(name, "tn", (N_CHIPS, S // tk), h, pl.BlockSpec((tk, D), lambda s, k: (k, 0)),
               dgu, pl.BlockSpec((None, tk, W), lambda s, k: (s // 2, k, s % 2)),
               jax.ShapeDtypeStruct((N_CHIPS, D, W), BF16), pl.BlockSpec((None, D, W), lambda s, k: (s, 0, 0)),
               (D, W))


def _ffn_dh(name, dgu, w_gu, x, gain, dres, tm=512):
    S = dgu.shape[1]
    W = FF_SHARD

    def body(a_ref, w_hbm, x_ref, g_ref, dres_ref, dx_ref, dg_ref, w_ref, acat_ref, sems):
        first = pl.program_id(0) == 0

        @pl.when(first)
        def _():
            copies = [pltpu.make_async_copy(w_hbm.at[s], w_ref.at[:, pl.ds(s * W, W)], sems.at[s])
                      for s in range(N_CHIPS)]
            for cp in copies:
                cp.start()
            for cp in copies:
                cp.wait()

        acat_ref[:, :D_FF] = a_ref[0]
        acat_ref[:, D_FF:] = a_ref[1]
        dh = _dot(acat_ref[...], w_ref[...], "nt")
        dx, dg = _norm_bwd_rows(x_ref[...], g_ref[...], dh, dres_ref[...])
        dx_ref[...] = dx
        _accumulate(dg_ref, dg, first)

    row = pl.BlockSpec((tm, D_MODEL), lambda i: (i, 0))
    vec = pl.BlockSpec((1, D_MODEL), lambda i: (0, 0))
    return _pcall(body, name=name, grid=(S // tm,),
                  in_specs=[pl.BlockSpec((2, tm, D_FF), lambda i: (0, i, 0)), pl.BlockSpec(memory_space=pl.ANY),
                            row, vec, row],
                  out_specs=[row, vec],
                  out_shape=[jax.ShapeDtypeStruct((S, D_MODEL), F32), jax.ShapeDtypeStruct((1, D_MODEL), F32)],
                  scratch_shapes=[pltpu.VMEM((D_MODEL, 2 * D_FF), BF16), pltpu.VMEM((tm, 2 * D_FF), BF16),
                                  pltpu.SemaphoreType.DMA((N_CHIPS,))],
                  compiler_params=_params(("arbitrary",)))(dgu, w_gu, x, gain, dres)


FOLD_TOKENS = 1024
LANES = 128


def _lane_chunks(width):
    return [slice(c * LANES, (c + 1) * LANES) for c in range(width // LANES)]


def _strided_rows(r, n, dil):
    return pl.ds(r, n) if dil == 1 else pl.ds(r, n, stride=dil)


def _norm_fold(name, x, gain):
    S, D = x.shape
    chunks = _lane_chunks(D)

    def body(x_ref, g_ref, hf_hbm, xc_ref, hs_ref, sems):
        i = pl.program_id(0)
        xf = x_ref[...]
        inv = lax.rsqrt(jnp.mean(xf * xf, axis=-1, keepdims=True) + RMS_EPS)
        h = (xf * inv) * g_ref[...]
        hs_ref[0] = h.astype(BF16)
        for c, cols in enumerate(chunks):
            xc_ref[c] = h[:, cols]
        copies = []
        for g, dil in enumerate(ATTN_DILATIONS):
            n = FOLD_TOKENS // dil
            for r in range(dil):
                if dil > 1:
                    for c, cols in enumerate(chunks):
                        hs_ref[g, r * n:(r + 1) * n, cols] = xc_ref[c, _strided_rows(r, n, dil), :].astype(BF16)
                cp = pltpu.make_async_copy(hs_ref.at[g, pl.ds(r * n, n)],
                                           hf_hbm.at[g, pl.ds(r * (S // dil) + i * n, n)], sems.at[len(copies)])
                cp.start()
                copies.append(cp)
        for cp in copies:
            cp.wait()

    return _pcall(body, name=name, grid=(S // FOLD_TOKENS,),
                  in_specs=[pl.BlockSpec((FOLD_TOKENS, D), lambda i: (i, 0)), pl.BlockSpec((1, D), lambda i: (0, 0))],
                  out_specs=pl.BlockSpec(memory_space=pl.ANY),
                  out_shape=jax.ShapeDtypeStruct((N_GROUPS, S, D), BF16),
                  scratch_shapes=[pltpu.VMEM((D // LANES, FOLD_TOKENS, LANES), F32),
                                  pltpu.VMEM((N_GROUPS, FOLD_TOKENS, D), BF16),
                                  pltpu.SemaphoreType.DMA((sum(ATTN_DILATIONS),))],
                  compiler_params=_params(("arbitrary",)))(x, gain)


def _qkv_tiles(name, piece, hf, w, qkv, chip, tm=1024):
    S = hf.shape[1]
    per_group = 3 * D_MODEL // QKV_TILE

    def tile(q, c_ref):
        if piece is None:
            return QKV_PIECES * c_ref[0] + q
        return QKV_PIECES * ((c_ref[0] + 1 + q) % N_CHIPS) + piece

    def body(*refs):
        a_ref, w_ref, o_ref = refs[1], refs[2], refs[-1]
        o_ref[...] = _dot(a_ref[...], w_ref[...], "nn").astype(o_ref.dtype)

    if piece is None:
        w_spec = pl.BlockSpec((D_MODEL, QKV_TILE), lambda q, i, c_ref: (0, q))
    else:
        w_spec = pl.BlockSpec((None, D_MODEL, QKV_TILE), lambda q, i, c_ref: ((c_ref[0] + 1 + q) % N_CHIPS, 0, 0))
    in_specs = [pl.BlockSpec((None, tm, D_MODEL), lambda q, i, c_ref: (tile(q, c_ref) // per_group, i, 0)), w_spec]
    operands = [chip, hf, w]
    aliases = {}
    if qkv is not None:
        in_specs.append(pl.BlockSpec(memory_space=pl.ANY))
        operands.append(qkv)
        aliases = {3: 0}
    return _pcall(
        body, prefetch=1, name=name, grid=(N_CHIPS - 1, S // tm), in_specs=in_specs,
        out_specs=pl.BlockSpec((None, tm, QKV_TILE),
                               lambda q, i, c_ref: (tile(q, c_ref) // per_group, i, tile(q, c_ref) % per_group)),
        out_shape=jax.ShapeDtypeStruct((N_GROUPS, S, 3 * D_MODEL), BF16), input_output_aliases=aliases,
        compiler_params=_params(("arbitrary", "arbitrary")))(*operands)


QKV_PIECES = QKV_SHARD // QKV_TILE


def _qkv_dw(name, p, hf, dqkv):
    S = hf.shape[1]
    per_group = 3 * D_MODEL // QKV_TILE
    tile = lambda s: QKV_PIECES * s + p
    return _mm(name, "tn", (N_CHIPS, 1),
               hf, pl.BlockSpec((None, S, D_MODEL), lambda s, k: (tile(s) // per_group, 0, 0)),
               dqkv, pl.BlockSpec((None, S, QKV_TILE), lambda s, k: (tile(s) // per_group, 0, tile(s) % per_group)),
               jax.ShapeDtypeStruct((N_CHIPS, D_MODEL, QKV_TILE), BF16),
               pl.BlockSpec((None, D_MODEL, QKV_TILE), lambda s, k: (s, 0, 0)), None)


def _qkv_dh(name, g, dqkv, w_pieces, dhf, tm=1024):
    S = dqkv.shape[1]
    per_group = 3 * D_MODEL // QKV_TILE

    def body(*refs):
        a_ref, w_hbm = refs[0], refs[1:1 + QKV_PIECES]
        o_ref, w_ref, sems = refs[-3:]

        @pl.when(pl.program_id(0) == 0)
        def _():
            copies = []
            for j in range(per_group):
                t = per_group * g + j
                copies.append(pltpu.make_async_copy(w_hbm[t % QKV_PIECES].at[t // QKV_PIECES],
                                                    w_ref.at[:, pl.ds(j * QKV_TILE, QKV_TILE)], sems.at[j]))
            for cp in copies:
                cp.start()
            for cp in copies:
                cp.wait()

        o_ref[...] = _dot(a_ref[...], w_ref[...], "nt").astype(o_ref.dtype)

    any_spec = pl.BlockSpec(memory_space=pl.ANY)
    in_specs = [pl.BlockSpec((None, tm, 3 * D_MODEL), lambda i: (g, i, 0))] + [any_spec] * QKV_PIECES
    operands = [dqkv] + list(w_pieces)
    aliases = {}
    if dhf is not None:
        in_specs.append(any_spec)
        operands.append(dhf)
        aliases = {1 + QKV_PIECES: 0}
    return _pcall(body, name=name, grid=(S // tm,), in_specs=in_specs,
                  out_specs=pl.BlockSpec((None, tm, D_MODEL), lambda i: (g, i, 0)),
                  out_shape=jax.ShapeDtypeStruct((N_GROUPS, S, D_MODEL), BF16),
                  scratch_shapes=[pltpu.VMEM((D_MODEL, 3 * D_MODEL), BF16), pltpu.SemaphoreType.DMA((per_group,))],
                  input_output_aliases=aliases, compiler_params=_params(("arbitrary",)))(*operands)


def _alibi_coef(g, h):
    vals = [-(2.0 ** (-8.0 * (gg * N_HEADS + h + 1) / (N_GROUPS * N_HEADS))) * ATTN_DILATIONS[gg]
            for gg in range(N_GROUPS)]
    return jnp.where(g == 0, vals[0], jnp.where(g == 1, vals[1], vals[2])).astype(F32)


def _blocks_per_segment(g, S):
    return jnp.right_shift(S // Q_BLOCK, 2 * g)


def _band_bias(pen):
    row = lax.broadcasted_iota(jnp.int32, (Q_BLOCK, 2 * Q_BLOCK), 0)
    col = lax.broadcasted_iota(jnp.int32, (Q_BLOCK, 2 * Q_BLOCK), 1)
    dist = Q_BLOCK + row - col
    valid = jnp.logical_and(dist >= 0, dist <= Q_BLOCK)
    base = jnp.where(valid, jnp.where(col < Q_BLOCK, pen, 0.0), NEG).astype(F32)
    return base, dist.astype(F32)


def _skewed(n_units, stages):
    state = {}
    for step in range(n_units + len(stages) - 1):
        for s, stage in enumerate(stages):
            u = step - s
            if 0 <= u < n_units:
                state[u] = stage(u, state.get(u))
                if s == len(stages) - 1:
                    del state[u]


def _attn_fwd(name, qkv, tq=1024):
    S = qkv.shape[1]
    nqb = tq // Q_BLOCK
    scale = HEAD_DIM ** -0.5

    def body(q_ref, k_ref, kp_ref, v_ref, vp_ref, o_ref, lse_ref, kf_ref, vf_ref):
        g = pl.program_id(0)
        i = pl.program_id(1)
        nb = _blocks_per_segment(g, S)
        kf_ref[:Q_BLOCK] = kp_ref[...]
        kf_ref[Q_BLOCK:] = k_ref[...]
        vf_ref[:Q_BLOCK] = vp_ref[...]
        vf_ref[Q_BLOCK:] = v_ref[...]
        coefs = [_alibi_coef(g, h) for h in range(N_HEADS)]
        units = [(b, h) for b in range(nqb) for h in range(N_HEADS)]
        bias = {}

        def scores(u, _):
            b, h = units[u]
            if b not in bias:
                pen = jnp.where((i * nqb + b) % nb != 0, 0.0, NEG).astype(F32)
                bias.clear()
                bias[b] = _band_bias(pen)
            base, dist = bias[b]
            cols = slice(h * HEAD_DIM, (h + 1) * HEAD_DIM)
            q = q_ref[b * Q_BLOCK:(b + 1) * Q_BLOCK, cols]
            kk = kf_ref[b * Q_BLOCK:(b + 2) * Q_BLOCK, cols]
            return _dot(q, kk, "nt") * scale + (coefs[h] * dist + base)

        def softmax(u, s):
            m = jnp.max(s, axis=-1, keepdims=True)
            p = jnp.exp(s - m)
            den = jnp.sum(p, axis=-1, keepdims=True)
            return (p * (1.0 / den)).astype(BF16), m + jnp.log(den)

        def output(u, st):
            b, h = units[u]
            pn, lse = st
            cols = slice(h * HEAD_DIM, (h + 1) * HEAD_DIM)
            rows = slice(b * Q_BLOCK, (b + 1) * Q_BLOCK)
            o_ref[rows, cols] = _dot(pn, vf_ref[b * Q_BLOCK:(b + 2) * Q_BLOCK, cols], "nn").astype(o_ref.dtype)
            lse_ref[rows, h:h + 1] = lse

        _skewed(len(units), [scores, softmax, output])

    main = lambda c: pl.BlockSpec((None, tq, D_MODEL), lambda g, i: (g, i, c))
    prev = lambda c: pl.BlockSpec((None, Q_BLOCK, D_MODEL), lambda g, i: (g, jnp.maximum(i * nqb - 1, 0), c))
    return _pcall(
        body, name=name, grid=(N_GROUPS, S // tq),
        in_specs=[main(0), main(1), prev(1), main(2), prev(2)],
        out_specs=[pl.BlockSpec((None, tq, D_MODEL), lambda g, i: (g, i, 0)),
                   pl.BlockSpec((None, tq, N_HEADS), lambda g, i: (g, i, 0))],
        out_shape=[jax.ShapeDtypeStruct((N_GROUPS, S, D_MODEL), BF16),
                   jax.ShapeDtypeStruct((N_GROUPS, S, N_HEADS), F32)],
        scratch_shapes=[pltpu.VMEM((tq + Q_BLOCK, D_MODEL), BF16), pltpu.VMEM((tq + Q_BLOCK, D_MODEL), BF16)],
        compiler_params=_params(("parallel", "parallel")))(qkv, qkv, qkv, qkv, qkv)


def _attn_bwd(name, qkv, do, lse, dl, tq=1024):
    S = qkv.shape[1]
    nqb = tq // Q_BLOCK
    n_blocks = S // Q_BLOCK
    scale = HEAD_DIM ** -0.5
    KEYS = 2 * Q_BLOCK

    def body(q_ref, k_ref, v_ref, kp_ref, vp_ref, qn_ref, do_ref, don_ref, lse_ref, lsen_ref, dl_ref, dln_ref,
             out_ref, kf_ref, vf_ref, dk_ref, dv_ref):
        g = pl.program_id(0)
        i = pl.program_id(1)
        nb = _blocks_per_segment(g, S)
        kf_ref[:Q_BLOCK] = kp_ref[...]
        kf_ref[Q_BLOCK:] = k_ref[...]
        vf_ref[:Q_BLOCK] = vp_ref[...]
        vf_ref[Q_BLOCK:] = v_ref[...]
        coefs = [_alibi_coef(g, h) for h in range(N_HEADS)]
        units = [(h, b) for h in range(N_HEADS) for b in range(nqb + 1)]
        bias = {}

        def band(b):
            if b not in bias:
                blk = i * nqb + b
                ok = blk % nb != 0
                if b == nqb:
                    ok = jnp.logical_and(ok, blk < n_blocks)
                bias[b] = _band_bias(jnp.where(ok, 0.0, NEG).astype(F32))
            return bias[b]

        def operands(h, b):
            cols = slice(h * HEAD_DIM, (h + 1) * HEAD_DIM)
            if b == nqb:
                keys = slice(tq, tq + Q_BLOCK)
                return (qn_ref[:, cols], don_ref[:, cols], lsen_ref[:, h:h + 1], dln_ref[:, h:h + 1],
                        kf_ref[keys, cols], vf_ref[keys, cols])
            rows = slice(b * Q_BLOCK, (b + 1) * Q_BLOCK)
            keys = slice(b * Q_BLOCK, b * Q_BLOCK + KEYS)
            return (q_ref[rows, cols], do_ref[rows, cols], lse_ref[rows, h:h + 1], dl_ref[rows, h:h + 1],
                    kf_ref[keys, cols], vf_ref[keys, cols])

        def probs(u, _):
            h, b = units[u]
            q, do_b, lse_b, dl_b, kk, vv = operands(h, b)
            base, dist = band(b)
            if b == nqb:
                base, dist = base[:, :Q_BLOCK], dist[:, :Q_BLOCK]
            p = jnp.exp(_dot(q, kk, "nt") * scale + (coefs[h] * dist + base) - lse_b)
            return p, _dot(do_b, vv, "nt")

        def dscores(u, st):
            h, b = units[u]
            p, dp = st
            dl_b = operands(h, b)[3]
            return ((p * (dp - dl_b)) * scale).astype(BF16), p.astype(BF16)

        def grads(u, st):
            h, b = units[u]
            ds, pb = st
            q, do_b, _, _, kk, _ = operands(h, b)
            cols = slice(h * HEAD_DIM, (h + 1) * HEAD_DIM)
            if b < nqb:
                out_ref[b * Q_BLOCK:(b + 1) * Q_BLOCK, cols] = _dot(ds, kk, "nn").astype(out_ref.dtype)
            if b == 0:
                dk_ref[:Q_BLOCK] = _dot(ds[:, Q_BLOCK:], q, "tn")
                dv_ref[:Q_BLOCK] = _dot(pb[:, Q_BLOCK:], do_b, "tn")
                return None
            dk = _dot(ds, q, "tn")
            dv = _dot(pb, do_b, "tn")
            before = slice((b - 1) * Q_BLOCK, b * Q_BLOCK)
            dk_ref[before] += dk[:Q_BLOCK]
            dv_ref[before] += dv[:Q_BLOCK]
            if b < nqb:
                own = slice(b * Q_BLOCK, (b + 1) * Q_BLOCK)
                dk_ref[own] = dk[Q_BLOCK:]
                dv_ref[own] = dv[Q_BLOCK:]
            else:
                out_ref[:, D_MODEL + h * HEAD_DIM:D_MODEL + (h + 1) * HEAD_DIM] = dk_ref[...].astype(out_ref.dtype)
                out_ref[:, 2 * D_MODEL + h * HEAD_DIM:2 * D_MODEL + (h + 1) * HEAD_DIM] = (
                    dv_ref[...].astype(out_ref.dtype))
            return None

        _skewed(len(units), [probs, dscores, grads])

    nxt_blk = lambda i: jnp.minimum((i + 1) * nqb, n_blocks - 1)
    main = lambda c: pl.BlockSpec((None, tq, D_MODEL), lambda g, i: (g, i, c))
    prev = lambda c: pl.BlockSpec((None, Q_BLOCK, D_MODEL), lambda g, i: (g, jnp.maximum(i * nqb - 1, 0), c))
    row = pl.BlockSpec((None, tq, D_MODEL), lambda g, i: (g, i, 0))
    row_n = pl.BlockSpec((None, Q_BLOCK, D_MODEL), lambda g, i: (g, nxt_blk(i), 0))
    col = pl.BlockSpec((None, tq, N_HEADS), lambda g, i: (g, i, 0))
    col_n = pl.BlockSpec((None, Q_BLOCK, N_HEADS), lambda g, i: (g, nxt_blk(i), 0))
    return _pcall(
        body, name=name, grid=(N_GROUPS, S // tq),
        in_specs=[main(0), main(1), main(2), prev(1), prev(2), row_n, row, row_n, col, col_n, col, col_n],
        out_specs=pl.BlockSpec((None, tq, 3 * D_MODEL), lambda g, i: (g, i, 0)),
        out_shape=jax.ShapeDtypeStruct((N_GROUPS, S, 3 * D_MODEL), BF16),
        scratch_shapes=[pltpu.VMEM((tq + Q_BLOCK, D_MODEL), BF16), pltpu.VMEM((tq + Q_BLOCK, D_MODEL), BF16),
                        pltpu.VMEM((tq, HEAD_DIM), F32), pltpu.VMEM((tq, HEAD_DIM), F32)],
        compiler_params=_params(("parallel", "parallel")))(qkv, qkv, qkv, qkv, qkv, qkv, do, do, lse, lse, dl, dl)


def _group_weights(lse_ref):
    l0, l1, l2 = lse_ref[0], lse_ref[1], lse_ref[2]
    m = jnp.maximum(jnp.maximum(l0, l1), l2)
    e = [jnp.exp(l0 - m), jnp.exp(l1 - m), jnp.exp(l2 - m)]
    den = e[0] + e[1] + e[2]
    return [ei / den for ei in e]


MERGE_TOKENS = 512


def _folded_views(a, tb):
    _, S, C = a.shape
    views, specs = [], []
    for g, dil in enumerate(ATTN_DILATIONS):
        views.append(a.reshape(N_GROUPS * dil, S // dil, C))
        specs.append(pl.BlockSpec((dil, tb // dil, C), lambda i, g=g: (g, i, 0)))
    return views, specs


def _unfold_into(dst_ref, folded_refs):
    for g, (dil, ref) in enumerate(zip(ATTN_DILATIONS, folded_refs)):
        n = ref.shape[1]
        for r in range(dil):
            for c, cols in enumerate(_lane_chunks(ref.shape[2])):
                dst_ref[g, c, _strided_rows(r, n, dil), :] = ref[r, :, cols].astype(F32)


def _merge_fwd(name, o, lse, tb=MERGE_TOKENS):
    S = o.shape[1]

    def body(o0_ref, o1_ref, o2_ref, lse_ref, out_ref, o_ref):
        _unfold_into(o_ref, (o0_ref, o1_ref, o2_ref))
        w = _group_weights(lse_ref)
        for h in range(N_HEADS):
            cols = slice(h * HEAD_DIM, (h + 1) * HEAD_DIM)
            acc = w[0][:, h:h + 1] * o_ref[0, h]
            for gg in range(1, N_GROUPS):
                acc = acc + w[gg][:, h:h + 1] * o_ref[gg, h]
            out_ref[:, cols] = acc.astype(out_ref.dtype)

    views, specs = _folded_views(o, tb)
    return _pcall(body, name=name, grid=(S // tb,),
                  in_specs=specs + [pl.BlockSpec((N_GROUPS, tb, N_HEADS), lambda i: (0, i, 0))],
                  out_specs=pl.BlockSpec((tb, D_MODEL), lambda i: (i, 0)),
                  out_shape=jax.ShapeDtypeStruct((S, D_MODEL), BF16),
                  scratch_shapes=[pltpu.VMEM((N_GROUPS, N_HEADS, tb, HEAD_DIM), F32)],
                  compiler_params=_params(("parallel",)))(*views, lse)


def _merge_bwd(name, dx, w_out, o, lse, tb=MERGE_TOKENS):
    S = o.shape[1]
    n_chunks = sum(ATTN_DILATIONS)

    def body(dx_ref, w_ref, o0_ref, o1_ref, o2_ref, lse_ref, do_hbm, dl_ref, o_ref, dt_ref, df_ref, d_ref, sems):
        i = pl.program_id(0)
        d_ref[...] = _dot(dx_ref[...].astype(BF16), w_ref[...], "nt")
        _unfold_into(o_ref, (o0_ref, o1_ref, o2_ref))
        w = _group_weights(lse_ref)
        for h in range(N_HEADS):
            cols = slice(h * HEAD_DIM, (h + 1) * HEAD_DIM)
            dv = d_ref[:, cols]
            merged = w[0][:, h:h + 1] * o_ref[0, h]
            for gg in range(1, N_GROUPS):
                merged = merged + w[gg][:, h:h + 1] * o_ref[gg, h]
            dsum = jnp.sum(dv * merged, axis=-1, keepdims=True)
            for gg in range(N_GROUPS):
                wg = w[gg][:, h:h + 1]
                dt_ref[gg, h] = wg * dv
                dl_ref[gg, :, h:h + 1] = wg * dsum
        copies = []
        for gg, dil in enumerate(ATTN_DILATIONS):
            n = tb // dil
            for r in range(dil):
                for h, cols in enumerate(_lane_chunks(D_MODEL)):
                    df_ref[gg, r * n:(r + 1) * n, cols] = (
                        dt_ref[gg, h, _strided_rows(r, n, dil), :].astype(df_ref.dtype))
                cp = pltpu.make_async_copy(df_ref.at[gg, pl.ds(r * n, n)],
                                           do_hbm.at[gg, pl.ds(r * (S // dil) + i * n, n)], sems.at[len(copies)])
                cp.start()
                copies.append(cp)
        for cp in copies:
            cp.wait()

    views, specs = _folded_views(o, tb)
    small = pl.BlockSpec((N_GROUPS, tb, N_HEADS), lambda i: (0, i, 0))
    return _pcall(body, name=name, grid=(S // tb,),
                  in_specs=[pl.BlockSpec((tb, D_MODEL), lambda i: (i, 0)),
                            pl.BlockSpec((D_MODEL, D_MODEL), lambda i: (0, 0))] + specs + [small],
                  out_specs=[pl.BlockSpec(memory_space=pl.ANY), small],
                  out_shape=[jax.ShapeDtypeStruct((N_GROUPS, S, D_MODEL), BF16),
                             jax.ShapeDtypeStruct((N_GROUPS, S, N_HEADS), F32)],
                  scratch_shapes=[pltpu.VMEM((N_GROUPS, N_HEADS, tb, HEAD_DIM), F32),
                                  pltpu.VMEM((N_GROUPS, N_HEADS, tb, HEAD_DIM), F32),
                                  pltpu.VMEM((N_GROUPS, tb, D_MODEL), BF16), pltpu.VMEM((tb, D_MODEL), F32),
                                  pltpu.SemaphoreType.DMA((n_chunks,))],
                  compiler_params=_params(("arbitrary",)))(dx, w_out, *views, lse)


def _shift_rows(a, k):
    return pltpu.roll(a, k % a.shape[0], axis=0)


def _pool_level(g, levels):
    return jnp.where(g == 0, levels[0], jnp.where(g == 1, levels[1], jnp.where(g == 2, levels[2], levels[3])))


def _pool_fwd(name, h, w_in, w_group, scale, x_in, gain_next, tr=1024):
    S, D = h.shape
    H = POOL_HALO
    n_groups = D // POOL_DIM

    def body(h_ref, hh_ref, wi_ref, w_ref, sc_ref, x_ref, gn_ref, y_ref, z_ref, xo_ref, hn_ref, xs_ref):
        i = pl.program_id(0)
        g = pl.program_id(1)
        uv = _dot(h_ref[...], wi_ref[...], "nn")
        halo = jnp.where(i > 0, _dot(hh_ref[...], wi_ref[...], "nn"), 0.0)
        s = jnp.concatenate([halo, uv], axis=0)
        levels = []
        for k in (1, 2, 4, 8):
            s = s + _shift_rows(s, k)
            levels.append(s[H:])
        t = i * tr + lax.broadcasted_iota(jnp.int32, (tr, 1), 0)
        cnt = jnp.minimum(t + 1, jnp.left_shift(2, g)).astype(F32)
        y = _pool_level(g, levels) / cnt - uv
        yb = y.astype(BF16)
        z = _dot(yb, w_ref[...], "nn")
        y_ref[...] = yb
        z_ref[...] = z.astype(z_ref.dtype)
        x_out = x_ref[...] + z * sc_ref[...]
        xo_ref[...] = x_out
        xs_ref[g] = x_out

        @pl.when(g == n_groups - 1)
        def _():
            xf = jnp.concatenate([xs_ref[k] for k in range(n_groups)], axis=1)
            r = lax.rsqrt(jnp.mean(xf * xf, axis=-1, keepdims=True) + RMS_EPS)
            hn_ref[...] = ((xf * r) * gn_ref[...]).astype(hn_ref.dtype)

    blk = pl.BlockSpec((tr, POOL_DIM), lambda i, g: (i, g))
    row = pl.BlockSpec((tr, D), lambda i, g: (i, 0))
    return _pcall(
        body, name=name, grid=(S // tr, n_groups),
        in_specs=[row, pl.BlockSpec((H, D), lambda i, g: (jnp.maximum(i * (tr // H) - 1, 0), 0)),
                  pl.BlockSpec((D, POOL_DIM), lambda i, g: (0, g)),
                  pl.BlockSpec((None, POOL_DIM, POOL_DIM), lambda i, g: (g, 0, 0)),
                  pl.BlockSpec((1, POOL_DIM), lambda i, g: (0, g)), blk, pl.BlockSpec((1, D), lambda i, g: (0, 0))],
        out_specs=[blk, blk, blk, row],
        out_shape=[jax.ShapeDtypeStruct((S, D), BF16), jax.ShapeDtypeStruct((S, D), BF16),
                   jax.ShapeDtypeStruct((S, D), F32), jax.ShapeDtypeStruct((S, D), BF16)],
        scratch_shapes=[pltpu.VMEM((n_groups, tr, POOL_DIM), F32)],
        compiler_params=_params(("parallel", "arbitrary")))(h, h, w_in, w_group, scale, x_in, gain_next)


def _pool_bwd(name, d_out, z, y, scale, w_group, tr=1024):
    S, D = d_out.shape
    H = POOL_HALO

    def body(d_ref, dn_ref, z_ref, y_ref, sc_ref, w_ref, du_ref, dw_ref, dsc_ref):
        g = pl.program_id(0)
        i = pl.program_id(1)
        dv = d_ref[...]
        dz = jnp.concatenate([dv, dn_ref[...]], axis=0) * sc_ref[...]
        dzb = dz.astype(BF16)
        dy = _dot(dzb, w_ref[...], "nt")
        t = i * tr + lax.broadcasted_iota(jnp.int32, (tr + H, 1), 0)
        cnt = jnp.minimum(t + 1, jnp.left_shift(2, g)).astype(F32)
        s = jnp.where(t < S, dy / cnt, 0.0)
        levels = []
        for k in (1, 2, 4, 8):
            s = s + _shift_rows(s, -k)
            levels.append(s[:tr])
        du_ref[...] = (_pool_level(g, levels) - dy[:tr]).astype(du_ref.dtype)
        dw = _dot(y_ref[...], dzb[:tr], "tn")
        dsc = jnp.sum(dv * z_ref[...].astype(F32), axis=0, keepdims=True)

        @pl.when(i == 0)
        def _():
            dw_ref[...] = dw
            dsc_ref[...] = dsc

        @pl.when(i > 0)
        def _():
            dw_ref[...] += dw
            dsc_ref[...] += dsc

    blk = pl.BlockSpec((tr, POOL_DIM), lambda g, i: (i, g))
    vec = pl.BlockSpec((1, POOL_DIM), lambda g, i: (0, g))
    mat = pl.BlockSpec((None, POOL_DIM, POOL_DIM), lambda g, i: (g, 0, 0))
    nxt = pl.BlockSpec((H, POOL_DIM), lambda g, i: (jnp.minimum((i + 1) * (tr // H), S // H - 1), g))
    return _pcall(
        body, name=name, grid=(D // POOL_DIM, S // tr), in_specs=[blk, nxt, blk, blk, vec, mat],
        out_specs=[blk, mat, vec],
        out_shape=[jax.ShapeDtypeStruct((S, D), BF16), jax.ShapeDtypeStruct((D // POOL_DIM, POOL_DIM, POOL_DIM), F32),
                   jax.ShapeDtypeStruct((1, D), F32)],
        compiler_params=_params(("parallel", "arbitrary")))(d_out, d_out, z, y, scale, w_group)


def _row_tile(rows, cols, target_bytes=1 << 20):
    best = rows
    for tr in range(8, rows + 1, 8):
        if rows % tr == 0 and tr * cols * 4 <= target_bytes:
            best = tr
    return best if best * cols * 4 <= 4 * target_bytes else rows


def _adamw_refs(w_ref, g_ref, m_ref, v_ref, go_ref, d_ref, mo_ref, vo_ref):
    gv = g_ref[...]
    m2 = ADAM_B1 * m_ref[...] + (1.0 - ADAM_B1) * gv
    v2 = ADAM_B2 * v_ref[...] + (1.0 - ADAM_B2) * (gv * gv)
    m_hat = m2 / (1.0 - ADAM_B1 ** ADAM_STEP)
    v_hat = v2 / (1.0 - ADAM_B2 ** ADAM_STEP)
    d_ref[...] = -ADAM_LR * (m_hat / (jnp.sqrt(v_hat) + ADAM_EPS) + ADAM_WD * w_ref[...])
    go_ref[...] = gv
    mo_ref[...] = m2
    vo_ref[...] = v2


def _adamw(name, w, g, m, v):
    R, C = w.shape
    tr = _row_tile(R, C) if R % 8 == 0 else R
    blk = pl.BlockSpec((tr, C), lambda i: (i, 0))
    sds = jax.ShapeDtypeStruct((R, C), F32)
    return _pcall(_adamw_refs, name=name, grid=(R // tr,), in_specs=[blk] * 4, out_specs=[blk] * 4,
                  out_shape=[sds] * 4, compiler_params=_params(("parallel",)))(w, g, m, v)


def _adamw_small(name, items):
    n = len(items)

    def body(*refs):
        for t in range(n):
            _adamw_refs(*refs[4 * t:4 * t + 4], *refs[4 * n + 4 * t:4 * n + 4 * t + 4])

    specs, shapes = [], []
    for w, _, _, _ in items:
        specs += [pl.BlockSpec(w.shape, lambda i: (0, 0))] * 4
        shapes += [jax.ShapeDtypeStruct(w.shape, F32)] * 4
    res = _pcall(body, name=name, grid=(1,), in_specs=specs, out_specs=specs, out_shape=shapes,
                 compiler_params=_params(("arbitrary",)))(*[a for item in items for a in item])
    return [res[4 * t:4 * t + 4] for t in range(n)]


def _sum_leading(name, a, out_dtype):
    n, R, C = a.shape
    tr = _row_tile(R, C) if R % 16 == 0 else R
    if tr % 16:
        tr = R

    def body(a_ref, o_ref):
        acc = a_ref[0].astype(F32)
        for j in range(1, n):
            acc = acc + a_ref[j].astype(F32)
        o_ref[...] = acc.astype(o_ref.dtype)

    return _pcall(body, name=name, grid=(R // tr,), in_specs=[pl.BlockSpec((n, tr, C), lambda i: (0, i, 0))],
                  out_specs=pl.BlockSpec((tr, C), lambda i: (i, 0)), out_shape=jax.ShapeDtypeStruct((R, C), out_dtype),
                  compiler_params=_params(("parallel",)))(a)


def _cast_many(name, items, chip, steps=4):
    tiles = []
    for w, _, dtype in items:
        R = w.shape[1]
        nt = steps if R % (steps * 16) == 0 else 1
        tiles.append((nt, R // nt))

    def body(c_ref, *refs):
        i = pl.program_id(0)
        for t, (nt, _) in enumerate(tiles):
            w_ref, o_ref = refs[t], refs[len(items) + t]

            def cast(w_ref=w_ref, o_ref=o_ref):
                o_ref[...] = w_ref[...].astype(o_ref.dtype)

            if nt == steps:
                cast()
            else:
                pl.when(i < nt)(cast)

    in_specs, out_specs, out_shape = [], [], []
    for (w, layer, dtype), (nt, tr) in zip(items, tiles):
        C = w.shape[2]
        in_specs.append(pl.BlockSpec((None, tr, C), lambda i, c_ref, l=layer, nt=nt: (l, jnp.minimum(i, nt - 1), 0)))
        out_specs.append(pl.BlockSpec((None, tr, C), lambda i, c_ref, nt=nt: (c_ref[0], jnp.minimum(i, nt - 1), 0)))
        out_shape.append(jax.ShapeDtypeStruct((N_CHIPS, w.shape[1], C), dtype))
    return _pcall(body, prefetch=1, name=name, grid=(steps,), in_specs=in_specs, out_specs=out_specs,
                  out_shape=out_shape, compiler_params=_params(("arbitrary",)))(chip, *[w for w, _, _ in items])


def _cast_qkv(name, w, chip, tr=256):
    _, R, C = w.shape

    def body(c_ref, w_ref, own_ref, *piece_refs):
        v = w_ref[...].astype(BF16)
        own_ref[...] = v
        for p, ref in enumerate(piece_refs):
            ref[...] = v[:, p * QKV_TILE:(p + 1) * QKV_TILE]

    piece = pl.BlockSpec((None, tr, QKV_TILE), lambda i, c_ref: (c_ref[0], i, 0))
    return _pcall(body, prefetch=1, name=name, grid=(R // tr,),
                  in_specs=[pl.BlockSpec((None, tr, C), lambda i, c_ref: (0, i, 0))],
                  out_specs=[pl.BlockSpec((tr, C), lambda i, c_ref: (i, 0))] + [piece] * QKV_PIECES,
                  out_shape=[jax.ShapeDtypeStruct((R, C), BF16)]
                  + [jax.ShapeDtypeStruct((N_CHIPS, R, QKV_TILE), BF16)] * QKV_PIECES,
                  compiler_params=_params(("parallel",)))(chip, w)


def _owner_sum(name, mine, landed, chip, core, out=None, layer=None, col=None):
    _, half, C = mine.shape
    k, n_col = col if col is not None else (0, 1)
    tr = _row_tile(half, C)
    if tr % 16:
        tr = half
    nrt = half // tr
    has_out = out is not None

    def body(*refs):
        m_ref, l_ref, o_ref = refs[2], refs[3], refs[-1]
        acc = m_ref[...].astype(F32)
        for j in range(3):
            acc = acc + l_ref[j].astype(F32)
        o_ref[...] = acc

    if layer is None:
        out_shape = jax.ShapeDtypeStruct((2 * half, n_col * C), F32)
        o_spec = pl.BlockSpec((tr, C), lambda i, ch, co: (co[0] * nrt + i, k))
    else:
        out_shape = jax.ShapeDtypeStruct((2, 2 * half, C), F32)
        o_spec = pl.BlockSpec((None, tr, C), lambda i, ch, co: (layer, co[0] * nrt + i, 0))
    in_specs = [pl.BlockSpec((None, tr, C), lambda i, ch, co: (ch[0], i, 0)),
                pl.BlockSpec((3, tr, C), lambda i, ch, co: (0, i, 0))]
    operands = [chip, core, mine, landed]
    aliases = {}
    if has_out:
        in_specs.append(ANY)
        operands.append(out)
        aliases = {4: 0}
    grid_spec = pltpu.PrefetchScalarGridSpec(num_scalar_prefetch=2, grid=(nrt,), in_specs=in_specs, out_specs=o_spec)
    return _pcall(body, name=name, grid_spec=grid_spec, out_shape=out_shape, input_output_aliases=aliases,
                  compiler_params=_params(("parallel",)))(*operands)


def _add_my_half(name, ps, qs, core):
    n = len(ps)

    def body(c_ref, *refs):
        for t in range(n):
            p_ref, q_ref, o_ref = refs[t], refs[n + t], refs[2 * n + t]
            o_ref[...] = (p_ref[...].astype(F32) + q_ref[...].astype(F32)).astype(o_ref.dtype)

    halves = [(p.shape[1] // 2, p.shape[2]) for p in ps]
    mine = [pl.BlockSpec((None, h, c), lambda s, c_ref: (s, c_ref[0], 0)) for h, c in halves]
    whole = [pl.BlockSpec((None, h, c), lambda s, c_ref: (s, 0, 0)) for h, c in halves]
    return _pcall(body, prefetch=1, name=name, grid=(N_CHIPS,), in_specs=mine + whole, out_specs=whole,
                  out_shape=[jax.ShapeDtypeStruct((N_CHIPS, h, c), BF16) for h, c in halves],
                  compiler_params=_params(("parallel",)))(core, *ps, *qs)


ANY = pl.BlockSpec(memory_space=pl.ANY)


def _place():
    x, y, c = lax.axis_index("x"), lax.axis_index("y"), lax.axis_index("c")
    other_chips = [(1 - x, y), (x, 1 - y), (1 - x, 1 - y)]
    return x, y, c, other_chips


def _remote(src, dst, send, recv, k, to):
    return pltpu.make_async_remote_copy(src_ref=src, dst_ref=dst, send_sem=send.at[k], recv_sem=recv.at[k],
                                        device_id=to, device_id_type=MESH)


def _in_place(bufs):
    return dict(operands=bufs, out_shapes=[jax.ShapeDtypeStruct(b.shape, b.dtype) for b in bufs],
                aliases={t: t for t in range(len(bufs))})


def _gather_rider(bufs, jobs):
    def copies(ins, outs, send, recv, base=0):
        x, y, c, chips = _place()
        out = []
        for t, over_ici, rows in jobs:
            ref = outs[t]
            if rows is not None:
                a, b, n = rows
                half = ref.shape[1] // 2
                rows = pl.ds(c * half + a * half // n, (b - a) * half // n)
            for px, py in chips:
                slot = 2 * x + y if over_ici else 2 * px + py
                piece = ref.at[slot] if rows is None else ref.at[slot, rows]
                out.append(_remote(piece, piece, send, recv, base + len(out), (px, py, c) if over_ici else (x, y, 1 - c)))
        return out

    return _Rider(n_copies=3 * len(jobs), copies=copies, **_in_place(bufs))


def _gather_and_pass_rider(buf):
    half = buf.shape[1] // 2

    def pieces(outs):
        x, y, c, chips = _place()
        rows = lambda core: pl.ds(core * half, half)
        mine = outs[0].at[2 * x + y, rows(c)]
        landed = [outs[0].at[2 * px + py, rows(c)] for px, py in chips]
        theirs = [outs[0].at[2 * px + py, rows(1 - c)] for px, py in chips]
        return (x, y, c, chips), mine, landed, theirs

    def start(ins, outs, send, recv, base=0):
        (x, y, c, chips), mine, _, _ = pieces(outs)
        for j, (px, py) in enumerate(chips):
            _remote(mine, mine, send, recv, base + j, (px, py, c)).start()

    def finish(ins, outs, send, recv, base=0):
        (x, y, c, chips), mine, landed, theirs = pieces(outs)
        sibling = (x, y, 1 - c)
        passed = []
        for j, (px, py) in enumerate(chips):
            _remote(landed[j], landed[j], send, recv, base + j, (px, py, c)).wait_recv()
            passed.append(_remote(landed[j], landed[j], send, recv, base + 3 + j, sibling))
            passed[-1].start()
        for j in range(3):
            _remote(theirs[j], theirs[j], send, recv, base + 3 + j, sibling).wait_recv()
        for j, (px, py) in enumerate(chips):
            _remote(mine, mine, send, recv, base + j, (px, py, c)).wait_send()
            passed[j].wait_send()

    return _Rider(n_copies=6, start=start, finish=finish, **_in_place([buf]))


def _swap_halves_rider(parts):
    n = len(parts)

    def copies(ins, outs, send, recv, base=0):
        x, y, c, _ = _place()
        out = []
        for t in range(n):
            half = ins[t].shape[1] // 2
            out.append(_remote(ins[t].at[:, pl.ds((1 - c) * half, half)], outs[t], send, recv, base + t,
                               (x, y, 1 - c)))
        return out

    shapes = [jax.ShapeDtypeStruct((p.shape[0], p.shape[1] // 2, p.shape[2]), p.dtype) for p in parts]
    return _Rider(parts, shapes, {}, n, copies)


def _scatter_rider(sums, landed, relations):
    n = len(sums)
    kept = [t for t in range(n) if landed[t] is not None]

    def copies(ins, outs, send, recv, base=0):
        x, y, c, chips = _place()
        out = []
        for t in range(n):
            for j in relations[t]:
                px, py = chips[j]
                out.append(_remote(ins[t].at[2 * px + py], outs[t].at[j], send, recv, base + len(out), (px, py, c)))
        return out

    shapes = [jax.ShapeDtypeStruct((3,) + s.shape[1:], s.dtype) for s in sums]
    return _Rider(list(sums) + [landed[t] for t in kept], shapes, {n + k: t for k, t in enumerate(kept)},
                  sum(len(r) for r in relations), copies)


def _share_rider(blocks, pieces):
    def copies(ins, outs, send, recv, base=0):
        x, y, c, _ = _place()
        out = []
        for k, (t, l, col) in enumerate(pieces):
            ref = outs[t] if l is None else outs[t].at[l]
            r2 = ref.shape[0] // 2
            piece = ref.at[pl.ds(c * r2, r2)]
            if col is not None:
                width = ref.shape[1] // col[1]
                piece = ref.at[pl.ds(c * r2, r2), pl.ds(col[0] * width, width)]
            out.append(_remote(piece, piece, send, recv, base + k, (x, y, 1 - c)))
        return out

    return _Rider(n_copies=len(pieces), copies=copies, **_in_place(blocks))


def _gather_small(name, v):
    def body(v_ref, out_ref, send_sem, recv_sem, local_sem):
        x, y, c, _ = _place()
        me = 4 * x + 2 * y + c
        flips = [(fx, fy, fc) for fx in (0, 1) for fy in (0, 1) for fc in (0, 1)][1:]
        local = pltpu.make_async_copy(v_ref, out_ref.at[me], local_sem)
        local.start()
        copies = []
        for j, (fx, fy, fc) in enumerate(flips):
            peer = (x ^ fx, y ^ fy, c ^ fc)
            copies.append(pltpu.make_async_remote_copy(
                src_ref=v_ref, dst_ref=out_ref.at[me], send_sem=send_sem.at[j], recv_sem=recv_sem.at[j],
                device_id=peer, device_id_type=MESH))
        for cp in copies:
            cp.start()
        for j, (fx, fy, fc) in enumerate(flips):
            peer = (x ^ fx, y ^ fy, c ^ fc)
            pltpu.make_async_remote_copy(
                src_ref=v_ref, dst_ref=out_ref.at[4 * peer[0] + 2 * peer[1] + peer[2]], send_sem=send_sem.at[j],
                recv_sem=recv_sem.at[j], device_id=peer, device_id_type=MESH).wait_recv()
        for cp in copies:
            cp.wait_send()
        local.wait()

    return _pcall(body, name=name, in_specs=[ANY], out_specs=ANY,
                  out_shape=jax.ShapeDtypeStruct((8,) + v.shape, v.dtype),
                  scratch_shapes=[pltpu.SemaphoreType.DMA((7,)), pltpu.SemaphoreType.DMA((7,)),
                                  pltpu.SemaphoreType.DMA])(v)


def _fold(a, dil):
    if dil == 1:
        return a
    S, C = a.shape
    return a.reshape(S // dil, dil, C).transpose(1, 0, 2).reshape(S, C)


def _unfold(a, dil):
    if dil == 1:
        return a
    S, C = a.shape
    return a.reshape(dil, S // dil, C).transpose(1, 0, 2).reshape(S, C)


def _fold_groups(a):
    return jnp.stack([_fold(a[g] if a.ndim == 3 else a, d) for g, d in enumerate(ATTN_DILATIONS)])


def _unfold_groups(a):
    return jnp.stack([_unfold(a[g], d) for g, d in enumerate(ATTN_DILATIONS)])


class _NoComm:
    def __init__(self, weights):
        self.weights = weights
        self.grads = {}
        self.chip = jnp.zeros((1,), jnp.int32)

    def w(self, name):
        return self.weights[name]

    def run(self, fn, name, *args, **kw):
        return fn(name, *args, **kw)

    def grad(self, name, value):
        self.grads[name] = value


def _local_step(xs, tgt, attn_norm, ffn_norm, final_norm, comm):
    run = comm.run
    hf = run(_norm_fold, "fold", xs, attn_norm)
    qkv = run(_qkv_tiles, "qkv_own", None, hf, comm.w("wq_own"), None, comm.chip)
    for p in range(QKV_PIECES):
        qkv = run(_qkv_tiles, "qkv_piece%d" % p, p, hf, comm.w("wq%d" % p), qkv, comm.chip)
    o_f, lse_f = run(_attn_fwd, "attn_fwd", qkv)
    lse_t = _unfold_groups(lse_f)
    merged = run(_merge_fwd, "merge_fwd", o_f, lse_t)
    x1, h1 = run(_proj_res_norm, "attn_out", merged, comm.w("o"), xs, ffn_norm[0:1])
    gu0, act0 = run(_ffn_up, "ffn0_up", h1, comm.w("gu0"))
    x2, h2 = run(_proj_res_norm, "ffn0_down", act0, comm.w("d0"), x1, comm.w("pool_norm"))
    y, z, x3, h3 = run(_pool_fwd, "pool_fwd", h2, comm.w("p"), comm.w("pg"), comm.w("pool_scale"), x2, ffn_norm[1:2])
    gu1, act1 = run(_ffn_up, "ffn1_up", h3, comm.w("gu1"))
    loss, dx4, d_final = run(_proj_res_loss, "ffn1_down", act1, comm.w("d1"), x3, final_norm, tgt)

    dgu1 = run(_ffn_down_bwd, "ffn1_down_bwd", dx4, comm.w("d1"), gu1)
    comm.grad("d1", run(_mm_tn, "ffn1_down_dw", act1, dx4, FF_SHARD, 2048))
    comm.grad("gu1", run(_ffn_dw_up, "ffn1_up_dw", h3, dgu1))
    dx3, d_ffn1 = run(_ffn_dh, "ffn1_up_dh", dgu1, comm.w("gu1"), x3, ffn_norm[1:2], dx4)

    du, dw_pg, d_scale = run(_pool_bwd, "pool_bwd", dx3, z, y, comm.w("pool_scale"), comm.w("pg"))
    comm.grad("pg", dw_pg)
    comm.grad("p", run(_mm_tn, "pool_in_dw", h2, du, D_MODEL, h2.shape[0]))
    dx2, d_pool = run(_dh_norm_bwd, "pool_in_dh", du, comm.w("p"), x2, comm.w("pool_norm"), dx3)

    dgu0 = run(_ffn_down_bwd, "ffn0_down_bwd", dx2, comm.w("d0"), gu0)
    comm.grad("d0", run(_mm_tn, "ffn0_down_dw", act0, dx2, FF_SHARD, 2048))
    comm.grad("gu0", run(_ffn_dw_up, "ffn0_up_dw", h1, dgu0))
    dx1, d_ffn0 = run(_ffn_dh, "ffn0_up_dh", dgu0, comm.w("gu0"), x1, ffn_norm[0:1], dx2)

    comm.grad("o", run(_mm_tn, "attn_out_dw", merged, dx1, D_MODEL, 2048))
    do_f, dl_t = run(_merge_bwd, "merge_bwd", dx1, comm.w("o"), o_f, lse_t)
    dqkv = run(_attn_bwd, "attn_bwd", qkv, do_f, lse_f, _fold_groups(dl_t))
    for p in range(QKV_PIECES):
        comm.grad("qkv%d" % p, run(_qkv_dw, "qkv_dw%d" % p, p, hf, dqkv))
    dhf = None
    for g in range(N_GROUPS):
        dhf = run(_qkv_dh, "qkv_dh%d" % g, g, dqkv, [comm.w("wq%d" % p) for p in range(QKV_PIECES)], dhf)
    grad_x, d_attn = run(_rms_bwd_folded, "norm_attn_bwd", xs, attn_norm, dhf, dx1)

    small = dict(attn=d_attn, ffn0=d_ffn0, ffn1=d_ffn1, final=d_final, pool=d_pool, scale=d_scale)
    return loss, grad_x, small


TENSORS = ("qkv", "o", "p", "pg", "gu0", "gu1", "d0", "d1")


def _block_of(name):
    if name[:-1] in ("gu", "d"):
        return name[:-1], int(name[-1]), None
    if name[:-1] == "qkv":
        return "qkv", None, (int(name[-1]), QKV_PIECES)
    return name, None, None


class _MeshComm:
    def __init__(self, bufs, chip_arr, core_arr, pg_rows):
        self.bufs = dict(bufs)
        self.chip, self.chip_arr, self.core_arr, self.pg_rows = chip_arr, chip_arr, core_arr, pg_rows
        self.parts, self.sums, self.blocks, self.landed, self.arrived = {}, {}, {}, {}, {}
        move = lambda ici=(), d2d=(): lambda: self.gather(ici, d2d)
        whole = lambda name: lambda: self.gather_and_pass(name)
        swap = lambda *names: lambda: self.swap(names)
        scatter = lambda *names: lambda: self.scatter(names)
        share = lambda *names: lambda: self.share(names)
        self.plan = {
            "fold": [whole("wq0")],
            "qkv_own": [whole("wq1")],
            "qkv_piece0": [whole("wq2")],
            "qkv_piece1": [move(ici=["o", "gu0[0:1/4]"])],
            "qkv_piece2": [move(ici=["gu0[1:2/4]"], d2d=["o", "gu0[0:1/4]"])],
            "attn_fwd": [move(ici=["gu0[2:4/4]", "d0[0:1/2]"], d2d=["gu0[1:2/4]"])],
            "merge_fwd": [move(ici=["p", "pg"], d2d=["gu0[2:4/4]", "d0[0:1/2]"])],
            "attn_out": [move(ici=["d0[1:2/2]", "pool_norm", "pool_scale"], d2d=["p", "pg"])],
            "ffn0_up": [move(ici=["gu1[0:3/4]"], d2d=["d0[1:2/2]"])],
            "ffn0_down": [move(ici=["gu1[3:4/4]", "d1[0:1/2]"], d2d=["gu1[0:3/4]"])],
            "pool_fwd": [move(ici=["d1[1:2/2]"], d2d=["gu1[3:4/4]", "d1[0:1/2]"])],
            "ffn1_up": [move(d2d=["d1[1:2/2]"])],
            "ffn1_up_dh": [swap("d1", "gu1")],
            "pool_bwd": [scatter("d1{01}")],
            "pool_in_dh": [scatter("d1{2}")],
            "ffn0_down_bwd": [scatter("gu1{01}")],
            "ffn0_down_dw": [scatter("gu1{2}")],
            "ffn0_up_dw": [share("gu1", "d1")],
            "ffn0_up_dh": [swap("pg", "p", "d0", "gu0")],
            "merge_bwd": [swap("o")],
            "attn_bwd": [scatter("gu0", "d0", "o")],
            "qkv_dw0": [scatter("p", "pg")],
            "qkv_dw1": [share("gu0", "o", "d0"), swap("qkv0")],
            "qkv_dw2": [share("p", "pg"), scatter("qkv0"), swap("qkv1")],
            "qkv_dh0": [share("qkv0"), scatter("qkv1"), swap("qkv2")],
            "qkv_dh1": [share("qkv1"), scatter("qkv2")],
            "qkv_dh2": [share("qkv2")],
        }

    def gather_and_pass(self, name):
        return _gather_and_pass_rider(self.bufs[name]), lambda res: self.bufs.update({name: res[0]})

    def gather(self, ici, d2d):
        names, jobs = [], []
        for over_ici, specs in ((True, ici), (False, d2d)):
            for spec in specs:
                name, _, rows = spec.partition("[")
                if name not in names:
                    names.append(name)
                rng = None
                if name in TENSORS:
                    ab, _, n = (rows[:-1] or "0:1/1").partition("/")
                    rng = (int(ab.split(":")[0]), int(ab.split(":")[1]), int(n))
                jobs.append((names.index(name), over_ici, rng))
        return _gather_rider([self.bufs[n] for n in names], jobs), lambda res: self.bufs.update(zip(names, res))

    def swap(self, names):
        def done(res):
            sums = _add_my_half("chip_sum_" + "_".join(names), [self.parts[n] for n in names], res, self.core_arr)
            self.sums.update(zip(names, sums))
        return _swap_halves_rider([self.parts[n] for n in names]), done

    def scatter(self, specs):
        names = [s.partition("{")[0] for s in specs]
        relations = [tuple(int(ch) for ch in s.partition("{")[2][:-1]) or (0, 1, 2) for s in specs]

        def done(res):
            for n, rel, landed in zip(names, relations, res):
                self.landed[n] = landed
                self.arrived[n] = self.arrived.get(n, ()) + rel
                if len(self.arrived[n]) < 3:
                    continue
                key, layer, col = _block_of(n)
                self.blocks[key] = _owner_sum("owner_sum_" + n, self.sums[n], landed, self.chip_arr, self.core_arr,
                                              out=self.blocks.get(key), layer=layer, col=col)
        rider = _scatter_rider([self.sums[n] for n in names], [self.landed.get(n) for n in names], relations)
        return rider, done

    def share(self, names):
        keys, pieces = [], []
        for n in names:
            key, layer, col = _block_of(n)
            if key not in keys:
                keys.append(key)
            pieces.append((keys.index(key), layer, col))
        return _share_rider([self.blocks[k] for k in keys], pieces), lambda res: self.blocks.update(zip(keys, res))

    def run(self, fn, name, *args, **kw):
        made = [make() for make in self.plan.get(name, [])]
        if not made:
            return fn(name, *args, **kw)
        riders = [r for r, _ in made]
        out = _ride(riders[0] if len(riders) == 1 else _riders(*riders), fn, name, *args, **kw)
        for r, done in made:
            done(r.results)
        return out

    def w(self, name):
        b = self.bufs[name]
        if name in ("o", "p", "d0", "d1"):
            return b.reshape(-1, b.shape[-1])
        if name == "pg":
            b = b.reshape(N_CHIPS, 4, self.pg_rows, POOL_DIM).transpose(1, 0, 2, 3)
            return b.reshape(4, POOL_DIM, POOL_DIM)
        if name in ("pool_norm", "pool_scale"):
            return b.reshape(1, -1)
        return b

    def grad(self, name, value):
        if name == "pg":
            value = value.reshape(4, N_CHIPS, self.pg_rows, POOL_DIM).transpose(1, 0, 2, 3)
            value = value.reshape(N_CHIPS, 4 * self.pg_rows, POOL_DIM).astype(BF16)
        elif name in ("o", "p", "d0", "d1"):
            value = value.reshape(N_CHIPS, value.shape[0] // N_CHIPS, value.shape[1])
        self.parts[name] = value


def kernel(x, attn_norm, w_qkv, w_attn_out, pool_norm, w_pool_in, w_pool_group, pool_scale, ffn_norm, w_ffn_gate_up, w_ffn_down, final_norm, loss_target, m_attn_norm, m_w_qkv, m_w_attn_out, m_pool_norm, m_w_pool_in, m_w_pool_group, m_pool_scale, m_ffn_norm, m_w_ffn_gate_up, m_w_ffn_down, m_final_norm, v_attn_norm, v_w_qkv, v_w_attn_out, v_pool_norm, v_w_pool_in, v_w_pool_group, v_pool_scale, v_ffn_norm, v_w_ffn_gate_up, v_w_ffn_down, v_final_norm):
    D = D_MODEL
    chip = 2 * lax.axis_index("x") + lax.axis_index("y")
    core = lax.axis_index("c")
    pg_rows = w_pool_group.shape[2]

    chip_arr = chip.astype(jnp.int32).reshape(1)
    core_arr = core.astype(jnp.int32).reshape(1)

    pieces = _cast_qkv("cast_qkv", w_qkv, chip_arr)
    bufs = dict(zip(["wq_own"] + ["wq%d" % p for p in range(QKV_PIECES)], pieces))
    shards = [(w_attn_out, 0, BF16), (w_pool_in, 0, BF16), (w_pool_group.reshape(1, 4 * pg_rows, POOL_DIM), 0, BF16),
              (w_ffn_gate_up, 0, BF16), (w_ffn_gate_up, 1, BF16), (w_ffn_down, 0, BF16), (w_ffn_down, 1, BF16),
              (pool_norm[None], 0, F32), (pool_scale[None], 0, F32)]
    bufs.update(zip(TENSORS[1:] + ("pool_norm", "pool_scale"), _cast_many("cast_rest", shards, chip_arr)))

    comm = _MeshComm(bufs, chip_arr, core_arr, pg_rows)
    loss_part, grad_x, small = _local_step(x[0], loss_target[0], attn_norm, ffn_norm, final_norm.reshape(1, D), comm)
    g_w_qkv, g_w_o, g_w_p, g_w_pg, g_w_gu, g_w_d = [comm.blocks[k] for k in ("qkv", "o", "p", "pg", "gu", "d")]

    rows = jnp.concatenate([small["attn"], small["ffn0"], small["ffn1"], small["final"], small["pool"],
                            small["scale"], jnp.pad(loss_part, ((0, 0), (0, D - 1))), jnp.zeros((1, D), F32)], axis=0)
    all_rows = _gather_small("gather_gain_grads", rows)
    tot = _sum_leading("sum_gain_grads", all_rows, F32)
    loss = tot[6, 0]
    g_attn_norm = tot[0:1]
    g_ffn_norm = tot[1:3]
    g_final_norm = tot[3]
    g_pool_norm = lax.dynamic_slice(tot, (4, chip * POOL_DIM), (1, POOL_DIM))
    g_pool_scale = lax.dynamic_slice(tot, (5, chip * POOL_DIM), (1, POOL_DIM))

    def update(name, w, g, m, v):
        shape = w.shape
        cols = shape[-1]
        as2d = lambda a: a.reshape(-1, cols)
        return [a.reshape(shape) for a in _adamw("adamw_" + name, as2d(w), as2d(g), as2d(m), as2d(v))]

    gains = [(attn_norm, g_attn_norm, m_attn_norm, v_attn_norm), (pool_norm, g_pool_norm, m_pool_norm, v_pool_norm),
             (pool_scale, g_pool_scale, m_pool_scale, v_pool_scale), (ffn_norm, g_ffn_norm, m_ffn_norm, v_ffn_norm),
             (final_norm, g_final_norm, m_final_norm, v_final_norm)]
    as_rows = lambda a: a.reshape(-1, a.shape[-1])
    small_res = _adamw_small("adamw_gains", [tuple(as_rows(a) for a in item) for item in gains])
    small_res = [[a.reshape(item[0].shape) for a in r] for r, item in zip(small_res, gains)]
    results = [
        small_res[0],
        update("w_qkv", w_qkv, g_w_qkv, m_w_qkv, v_w_qkv),
        update("w_attn_out", w_attn_out, g_w_o, m_w_attn_out, v_w_attn_out),
        small_res[1],
        update("w_pool_in", w_pool_in, g_w_p, m_w_pool_in, v_w_pool_in),
        update("w_pool_group", w_pool_group, g_w_pg, m_w_pool_group, v_w_pool_group),
        small_res[2],
        small_res[3],
        update("w_ffn_gate_up", w_ffn_gate_up, g_w_gu, m_w_ffn_gate_up, v_w_ffn_gate_up),
        update("w_ffn_down", w_ffn_down, g_w_d, m_w_ffn_down, v_w_ffn_down),
        small_res[4],
    ]
    grads = [r[0] for r in results]
    deltas = [r[1] for r in results]
    new_m = [r[2] for r in results]
    new_v = [r[3] for r in results]
    return (loss, grad_x[None], *grads, *deltas, *new_m, *new_v)
```

```python
import functools

import jax
import jax.numpy as jnp
from jax import lax
from jax.experimental import pallas as pl
from jax.experimental.pallas import tpu as pltpu

F32 = jnp.float32
BF16 = jnp.bfloat16
MESH = pl.DeviceIdType.MESH

D_MODEL = 1024
N_HEADS = 8
HEAD_DIM = 128
Q_BLOCK = 128
ATTN_DILATIONS = (1, 4, 16)
N_GROUPS = 3
D_FF = 2816
N_CHIPS = 4
FF_SHARD = 2 * D_FF // N_CHIPS
QKV_SHARD = 3 * 3 * D_MODEL // N_CHIPS
QKV_TILE = 768
POOL_DIM = 256
POOL_HALO = 16
RMS_EPS = 1e-6
NEG = -1e30
ADAM_LR = 0.001
ADAM_B1 = 0.9
ADAM_B2 = 0.999
ADAM_EPS = 1e-08
ADAM_WD = 0.01
ADAM_STEP = 10
VMEM_LIMIT = 56 * 1024 * 1024

_DN = {
    "nn": (((1,), (0,)), ((), ())),
    "nt": (((1,), (1,)), ((), ())),
    "tn": (((0,), (0,)), ((), ())),
}


class _Rider:
    def __init__(self, operands, out_shapes, aliases, n_copies, copies=None, start=None, finish=None):
        self.operands = list(operands)
        self.out_shapes = list(out_shapes)
        self.aliases = dict(aliases)
        self.n_copies = n_copies
        self.results = None

        def start_copies(ins, outs, send, recv, base=0):
            for cp in copies(ins, outs, send, recv, base):
                cp.start()

        def wait_copies(ins, outs, send, recv, base=0):
            for cp in copies(ins, outs, send, recv, base):
                cp.wait()

        self.start = start or start_copies
        self.finish = finish or wait_copies


def _riders(*riders):
    operands, out_shapes, aliases, offsets = [], [], {}, []
    n = 0
    for r in riders:
        offsets.append((len(operands), len(out_shapes), n))
        aliases.update({len(operands) + i: len(out_shapes) + o for i, o in r.aliases.items()})
        operands += r.operands
        out_shapes += r.out_shapes
        n += r.n_copies

    def each(which):
        def go(ins, outs, send, recv, base=0):
            for r, (i0, o0, s0) in zip(riders, offsets):
                getattr(r, which)(ins[i0:i0 + len(r.operands)], outs[o0:o0 + len(r.out_shapes)], send, recv,
                                  base + s0)
        return go

    both = _Rider(operands, out_shapes, aliases, n, start=each("start"), finish=each("finish"))
    both.parts = (riders, offsets)
    return both


_pending_rider = []


def _ride(rider, fn, *args, **kw):
    _pending_rider.append(rider)
    out = fn(*args, **kw)
    assert not _pending_rider
    if hasattr(rider, "parts"):
        for r, (_, o0, _) in zip(*rider.parts):
            r.results = rider.results[o0:o0 + len(r.out_shapes)]
    return out


def _pcall(body, prefetch=0, **kw):
    def build(body, kw):
        if not prefetch:
            return pl.pallas_call(body, **kw)
        kw = dict(kw)
        grid_spec = pltpu.PrefetchScalarGridSpec(
            num_scalar_prefetch=prefetch, grid=kw.pop("grid"), in_specs=kw.pop("in_specs"),
            out_specs=kw.pop("out_specs"), scratch_shapes=kw.pop("scratch_shapes", []))
        return pl.pallas_call(body, grid_spec=grid_spec, **kw)

    if not _pending_rider:
        return build(body, kw)
    rider = _pending_rider.pop()
    grid = kw.get("grid", ())
    in_specs = list(kw.get("in_specs", []))
    single = not isinstance(kw["out_shape"], (list, tuple))
    out_shape = [kw["out_shape"]] if single else list(kw["out_shape"])
    out_specs = [kw["out_specs"]] if single else list(kw["out_specs"])
    scratch = list(kw.get("scratch_shapes", []))
    n_in, n_out, n_scr = len(in_specs), len(out_shape), len(scratch)
    r_in, r_out = len(rider.operands), len(rider.out_shapes)

    def wrapped(*refs):
        scalars, refs = refs[:prefetch], refs[prefetch:]
        ins, rins = refs[:n_in], refs[n_in:n_in + r_in]
        outs = refs[n_in + r_in:n_in + r_in + n_out]
        routs = refs[n_in + r_in + n_out:n_in + r_in + n_out + r_out]
        scr = refs[n_in + r_in + n_out + r_out:n_in + r_in + n_out + r_out + n_scr]
        send, recv = refs[-2:]
        ids = [pl.program_id(a) for a in range(len(grid))]
        first, last = True, True
        for a, pid in enumerate(ids):
            first = jnp.logical_and(first, pid == 0)
            last = jnp.logical_and(last, pid == grid[a] - 1)
        if grid:
            pl.when(first)(lambda: rider.start(rins, routs, send, recv))
        else:
            rider.start(rins, routs, send, recv)
        body(*scalars, *ins, *outs, *scr)
        if grid:
            pl.when(last)(lambda: rider.finish(rins, routs, send, recv))
        else:
            rider.finish(rins, routs, send, recv)

    any_spec = pl.BlockSpec(memory_space=pl.ANY)
    kw2 = dict(kw)
    kw2.update(
        in_specs=in_specs + [any_spec] * r_in, out_specs=out_specs + [any_spec] * r_out,
        out_shape=out_shape + rider.out_shapes,
        scratch_shapes=scratch + [pltpu.SemaphoreType.DMA((rider.n_copies,)),
                                  pltpu.SemaphoreType.DMA((rider.n_copies,))],
        input_output_aliases={**kw.get("input_output_aliases", {}),
                              **{prefetch + n_in + i: n_out + o for i, o in rider.aliases.items()}})
    if grid:
        kw2["compiler_params"] = _params(("arbitrary",) * len(grid))
    call = build(wrapped, kw2)

    def run(*operands):
        res = call(*operands, *rider.operands)
        rider.results = list(res[n_out:])
        return res[0] if single else list(res[:n_out])

    return run


def _params(sem):
    return pltpu.CompilerParams(dimension_semantics=sem, vmem_limit_bytes=VMEM_LIMIT)


def _dot(a, b, mode):
    return lax.dot_general(a, b, _DN[mode], preferred_element_type=F32)


def _mm(name, mode, grid, a, a_spec, b, b_spec, out_shape, o_spec, acc_shape):
    nk = grid[-1]

    def body(a_ref, b_ref, o_ref, *acc):
        part = _dot(a_ref[...].astype(BF16), b_ref[...].astype(BF16), mode)
        if nk == 1:
            o_ref[...] = part.astype(o_ref.dtype)
            return
        acc_ref, = acc
        k = pl.program_id(len(grid) - 1)

        @pl.when(k == 0)
        def _():
            acc_ref[...] = part

        @pl.when(k > 0)
        def _():
            acc_ref[...] += part

        @pl.when(k == nk - 1)
        def _():
            o_ref[...] = acc_ref[...].astype(o_ref.dtype)

    scratch = [] if nk == 1 else [pltpu.VMEM(acc_shape, F32)]
    sem = ("parallel",) * (len(grid) - 1) + ("arbitrary",)
    return _pcall(body, name=name, grid=grid, in_specs=[a_spec, b_spec], out_specs=o_spec, out_shape=out_shape,
                  scratch_shapes=scratch, compiler_params=_params(sem))(a, b)


def _mm_tn(name, a, b, tm, tk):
    T, M = a.shape
    N = b.shape[1]
    return _mm(name, "tn", (M // tm, T // tk), a, pl.BlockSpec((tk, tm), lambda i, k: (k, i)),
               b, pl.BlockSpec((tk, N), lambda i, k: (k, 0)),
               jax.ShapeDtypeStruct((M, N), BF16), pl.BlockSpec((tm, N), lambda i, k: (i, 0)), (tm, N))


def _norm_bwd_rows(xf, gain, dh, dres):
    r = lax.rsqrt(jnp.mean(xf * xf, axis=-1, keepdims=True) + RMS_EPS)
    xh = xf * r
    t = dh * gain
    dx = dres + r * (t - xh * jnp.mean(t * xh, axis=-1, keepdims=True))
    return dx, jnp.sum(dh * xh, axis=0, keepdims=True)


def _accumulate(ref, value, first):
    @pl.when(first)
    def _():
        ref[...] = value

    @pl.when(jnp.logical_not(first))
    def _():
        ref[...] += value


def _rms_bwd_folded(name, x, g, dhf, dres, tb=512):
    S, D = x.shape

    def body(x_ref, g_ref, d0_ref, d1_ref, d2_ref, dres_ref, dx_ref, dg_ref, dh_ref):
        chunks = _lane_chunks(D)
        for c, cols in enumerate(chunks):
            dh_ref[c] = d0_ref[0, :, cols].astype(F32)
        for dil, ref in ((ATTN_DILATIONS[1], d1_ref), (ATTN_DILATIONS[2], d2_ref)):
            n = tb // dil
            for k in range(dil):
                for c, cols in enumerate(chunks):
                    dh_ref[c, _strided_rows(k, n, dil), :] += ref[k, :, cols].astype(F32)
        dh = jnp.concatenate([dh_ref[c] for c in range(len(chunks))], axis=1)
        dx, dg = _norm_bwd_rows(x_ref[...], g_ref[...], dh, dres_ref[...])
        dx_ref[...] = dx
        _accumulate(dg_ref, dg, pl.program_id(0) == 0)

    views, specs = _folded_views(dhf, tb)
    row = pl.BlockSpec((tb, D), lambda i: (i, 0))
    vec = pl.BlockSpec((1, D), lambda i: (0, 0))
    return _pcall(body, name=name, grid=(S // tb,), in_specs=[row, vec] + specs + [row], out_specs=[row, vec],
                  out_shape=[jax.ShapeDtypeStruct((S, D), F32), jax.ShapeDtypeStruct((1, D), F32)],
                  scratch_shapes=[pltpu.VMEM((D // LANES, tb, LANES), F32)],
                  compiler_params=_params(("arbitrary",)))(x, g, *views, dres)


def _proj_res_norm(name, a, w, res, gain, tm=512):
    M, K = a.shape
    D = w.shape[1]

    def body(a_ref, w_ref, res_ref, g_ref, x_ref, h_ref):
        xf = res_ref[...] + _dot(a_ref[...], w_ref[...], "nn")
        x_ref[...] = xf
        r = lax.rsqrt(jnp.mean(xf * xf, axis=-1, keepdims=True) + RMS_EPS)
        h_ref[...] = ((xf * r) * g_ref[...]).astype(h_ref.dtype)

    row = pl.BlockSpec((tm, D), lambda i: (i, 0))
    return _pcall(body, name=name, grid=(M // tm,),
                  in_specs=[pl.BlockSpec((tm, K), lambda i: (i, 0)), pl.BlockSpec((K, D), lambda i: (0, 0)), row,
                            pl.BlockSpec((1, D), lambda i: (0, 0))],
                  out_specs=[row, row],
                  out_shape=[jax.ShapeDtypeStruct((M, D), F32), jax.ShapeDtypeStruct((M, D), BF16)],
                  compiler_params=_params(("parallel",)))(a, w, res, gain)


def _proj_res_loss(name, a, w, res, gain, tgt, tm=512):
    M, K = a.shape
    D = w.shape[1]

    def body(a_ref, w_ref, res_ref, g_ref, t_ref, loss_ref, dx_ref, dxb_ref, dg_ref):
        first = pl.program_id(0) == 0
        xf = res_ref[...] + _dot(a_ref[...], w_ref[...], "nn")
        r = lax.rsqrt(jnp.mean(xf * xf, axis=-1, keepdims=True) + RMS_EPS)
        xh = xf * r
        gv = g_ref[...]
        e = xh * gv - t_ref[...]
        lp = 0.5 * jnp.sum(jnp.mean(e * e, axis=-1, keepdims=True), axis=0, keepdims=True)
        dy = e * (1.0 / D)
        t = dy * gv
        dx = r * (t - xh * jnp.mean(t * xh, axis=-1, keepdims=True))
        dx_ref[...] = dx
        dxb_ref[...] = dx.astype(dxb_ref.dtype)
        _accumulate(dg_ref, jnp.sum(dy * xh, axis=0, keepdims=True), first)
        _accumulate(loss_ref, lp, first)

    row = pl.BlockSpec((tm, D), lambda i: (i, 0))
    vec = pl.BlockSpec((1, D), lambda i: (0, 0))
    return _pcall(body, name=name, grid=(M // tm,),
                  in_specs=[pl.BlockSpec((tm, K), lambda i: (i, 0)), pl.BlockSpec((K, D), lambda i: (0, 0)), row,
                            vec, row],
                  out_specs=[pl.BlockSpec((1, 1), lambda i: (0, 0)), row, row, vec],
                  out_shape=[jax.ShapeDtypeStruct((1, 1), F32), jax.ShapeDtypeStruct((M, D), F32),
                             jax.ShapeDtypeStruct((M, D), BF16), jax.ShapeDtypeStruct((1, D), F32)],
                  compiler_params=_params(("arbitrary",)))(a, w, res, gain, tgt)


def _dh_norm_bwd(name, d, w, x, gain, dres, tm=512):
    M, N = d.shape
    D = w.shape[0]

    def body(d_ref, w_ref, x_ref, g_ref, dres_ref, dx_ref, dxb_ref, dg_ref):
        dh = _dot(d_ref[...].astype(BF16), w_ref[...], "nt")
        dx, dg = _norm_bwd_rows(x_ref[...], g_ref[...], dh, dres_ref[...])
        dx_ref[...] = dx
        dxb_ref[...] = dx.astype(dxb_ref.dtype)
        _accumulate(dg_ref, dg, pl.program_id(0) == 0)

    row = pl.BlockSpec((tm, D), lambda i: (i, 0))
    vec = pl.BlockSpec((1, D), lambda i: (0, 0))
    return _pcall(body, name=name, grid=(M // tm,),
                  in_specs=[pl.BlockSpec((tm, N), lambda i: (i, 0)), pl.BlockSpec((D, N), lambda i: (0, 0)), row, vec,
                            row],
                  out_specs=[row, row, vec],
                  out_shape=[jax.ShapeDtypeStruct((M, D), F32), jax.ShapeDtypeStruct((M, D), BF16),
                             jax.ShapeDtypeStruct((1, D), F32)],
                  compiler_params=_params(("arbitrary",)))(d, w, x, gain, dres)


def _sigmoid(v):
    return 0.5 * jnp.tanh(0.5 * v) + 0.5


def _ffn_up(name, h, w_gu, tm=1024):
    S, D = h.shape
    W = FF_SHARD

    def body(h_ref, wg_ref, wu_ref, gu_ref, act_ref):
        hv = h_ref[...]
        gate = _dot(hv, wg_ref[...], "nn")
        up = _dot(hv, wu_ref[...], "nn")
        gu_ref[0] = gate.astype(gu_ref.dtype)
        gu_ref[1] = up.astype(gu_ref.dtype)
        sig = _sigmoid(gate)
        act_ref[...] = ((gate * sig) * up).astype(act_ref.dtype)

    return _pcall(
        body, name=name, grid=(2, S // tm),
        in_specs=[pl.BlockSpec((tm, D), lambda j, i: (i, 0)),
                  pl.BlockSpec((None, D, W), lambda j, i: (j, 0, 0)),
                  pl.BlockSpec((None, D, W), lambda j, i: (j + 2, 0, 0))],
        out_specs=[pl.BlockSpec((2, tm, W), lambda j, i: (0, i, j)), pl.BlockSpec((tm, W), lambda j, i: (i, j))],
        out_shape=[jax.ShapeDtypeStruct((2, S, D_FF), BF16), jax.ShapeDtypeStruct((S, D_FF), BF16)],
        compiler_params=_params(("parallel", "parallel")))(h, w_gu, w_gu)


def _ffn_down_bwd(name, dx, w_down, gu, tm=1024):
    S, D = dx.shape
    W = FF_SHARD

    def body(dx_ref, w_ref, gu_ref, dgu_ref):
        dact = _dot(dx_ref[...].astype(BF16), w_ref[...], "nt")
        gate = gu_ref[0].astype(F32)
        up = gu_ref[1].astype(F32)
        sig = _sigmoid(gate)
        silu = gate * sig
        dgu_ref[0] = (dact * up * (sig * (1.0 + gate * (1.0 - sig)))).astype(dgu_ref.dtype)
        dgu_ref[1] = (dact * silu).astype(dgu_ref.dtype)

    blk = pl.BlockSpec((2, tm, W), lambda j, i: (0, i, j))
    return _pcall(
        body, name=name, grid=(2, S // tm),
        in_specs=[pl.BlockSpec((tm, D), lambda j, i: (i, 0)), pl.BlockSpec((W, D), lambda j, i: (j, 0)), blk],
        out_specs=blk, out_shape=jax.ShapeDtypeStruct((2, S, D_FF), BF16),
        compiler_params=_params(("parallel", "parallel")))(dx, w_down, gu)


def _ffn_dw_up(name, h, dgu, tk=2048):
    S, D = h.shape
    W = FF_SHARD
    tk = min(tk, S)
    return _mm(name, "tn", (N_CHIPS, S // tk), h, pl.BlockSpec((tk, D), lambda s, k: (k, 0)),
               dgu, pl.BlockSpec((None, tk, W), lambda s, k: (s // 2, k, s % 2)),
               jax.ShapeDtypeStruct((N_CHIPS, D, W), BF16), pl.BlockSpec((None, D, W), lambda s, k: (s, 0, 0)),
               (D, W))


def _ffn_dh(name, dgu, w_gu, x, gain, dres, tm=512):
    S = dgu.shape[1]
    W = FF_SHARD

    def body(a_ref, w_hbm, x_ref, g_ref, dres_ref, dx_ref, dg_ref, w_ref, acat_ref, sems):
        first = pl.program_id(0) == 0

        @pl.when(first)
        def _():
            copies = [pltpu.make_async_copy(w_hbm.at[s], w_ref.at[:, pl.ds(s * W, W)], sems.at[s])
                      for s in range(N_CHIPS)]
            for cp in copies:
                cp.start()
            for cp in copies:
                cp.wait()

        acat_ref[:, :D_FF] = a_ref[0]
        acat_ref[:, D_FF:] = a_ref[1]
        dh = _dot(acat_ref[...], w_ref[...], "nt")
        dx, dg = _norm_bwd_rows(x_ref[...], g_ref[...], dh, dres_ref[...])
        dx_ref[...] = dx
        _accumulate(dg_ref, dg, first)

    row = pl.BlockSpec((tm, D_MODEL), lambda i: (i, 0))
    vec = pl.BlockSpec((1, D_MODEL), lambda i: (0, 0))
    return _pcall(body, name=name, grid=(S // tm,),
                  in_specs=[pl.BlockSpec((2, tm, D_FF), lambda i: (0, i, 0)), pl.BlockSpec(memory_space=pl.ANY),
                            row, vec, row],
                  out_specs=[row, vec],
                  out_shape=[jax.ShapeDtypeStruct((S, D_MODEL), F32), jax.ShapeDtypeStruct((1, D_MODEL), F32)],
                  scratch_shapes=[pltpu.VMEM((D_MODEL, 2 * D_FF), BF16), pltpu.VMEM((tm, 2 * D_FF), BF16),
                                  pltpu.SemaphoreType.DMA((N_CHIPS,))],
                  compiler_params=_params(("arbitrary",)))(dgu, w_gu, x, gain, dres)


FOLD_TOKENS = 1024
LANES = 128


def _lane_chunks(width):
    return [slice(c * LANES, (c + 1) * LANES) for c in range(width // LANES)]


def _strided_rows(r, n, dil):
    return pl.ds(r, n) if dil == 1 else pl.ds(r, n, stride=dil)


def _norm_fold(name, x, gain):
    S, D = x.shape
    chunks = _lane_chunks(D)

    def body(x_ref, g_ref, hf_hbm, xc_ref, hs_ref, sems):
        i = pl.program_id(0)
        xf = x_ref[...]
        inv = lax.rsqrt(jnp.mean(xf * xf, axis=-1, keepdims=True) + RMS_EPS)
        h = (xf * inv) * g_ref[...]
        hs_ref[0] = h.astype(BF16)
        for c, cols in enumerate(chunks):
            xc_ref[c] = h[:, cols]
        copies = []
        for g, dil in enumerate(ATTN_DILATIONS):
            n = FOLD_TOKENS // dil
            for r in range(dil):
                if dil > 1:
                    for c, cols in enumerate(chunks):
                        hs_ref[g, r * n:(r + 1) * n, cols] = xc_ref[c, _strided_rows(r, n, dil), :].astype(BF16)
                cp = pltpu.make_async_copy(hs_ref.at[g, pl.ds(r * n, n)],
                                           hf_hbm.at[g, pl.ds(r * (S // dil) + i * n, n)], sems.at[len(copies)])
                cp.start()
                copies.append(cp)
        for cp in copies:
            cp.wait()

    return _pcall(body, name=name, grid=(S // FOLD_TOKENS,),
                  in_specs=[pl.BlockSpec((FOLD_TOKENS, D), lambda i: (i, 0)), pl.BlockSpec((1, D), lambda i: (0, 0))],
                  out_specs=pl.BlockSpec(memory_space=pl.ANY),
                  out_shape=jax.ShapeDtypeStruct((N_GROUPS, S, D), BF16),
                  scratch_shapes=[pltpu.VMEM((D // LANES, FOLD_TOKENS, LANES), F32),
                                  pltpu.VMEM((N_GROUPS, FOLD_TOKENS, D), BF16),
                                  pltpu.SemaphoreType.DMA((sum(ATTN_DILATIONS),))],
                  compiler_params=_params(("arbitrary",)))(x, gain)


def _qkv_tiles(name, piece, hf, w, qkv, chip, tm=1024):
    S = hf.shape[1]
    per_group = 3 * D_MODEL // QKV_TILE

    def tile(q, c_ref):
        if piece is None:
            return QKV_PIECES * c_ref[0] + q
        return QKV_PIECES * ((c_ref[0] + 1 + q) % N_CHIPS) + piece

    def body(*refs):
        a_ref, w_ref, o_ref = refs[1], refs[2], refs[-1]
        o_ref[...] = _dot(a_ref[...], w_ref[...], "nn").astype(o_ref.dtype)

    if piece is None:
        w_spec = pl.BlockSpec((D_MODEL, QKV_TILE), lambda q, i, c_ref: (0, q))
    else:
        w_spec = pl.BlockSpec((None, D_MODEL, QKV_TILE), lambda q, i, c_ref: ((c_ref[0] + 1 + q) % N_CHIPS, 0, 0))
    in_specs = [pl.BlockSpec((None, tm, D_MODEL), lambda q, i, c_ref: (tile(q, c_ref) // per_group, i, 0)), w_spec]
    operands = [chip, hf, w]
    aliases = {}
    if qkv is not None:
        in_specs.append(pl.BlockSpec(memory_space=pl.ANY))
        operands.append(qkv)
        aliases = {3: 0}
    return _pcall(
        body, prefetch=1, name=name, grid=(N_CHIPS - 1, S // tm), in_specs=in_specs,
        out_specs=pl.BlockSpec((None, tm, QKV_TILE),
                               lambda q, i, c_ref: (tile(q, c_ref) // per_group, i, tile(q, c_ref) % per_group)),
        out_shape=jax.ShapeDtypeStruct((N_GROUPS, S, 3 * D_MODEL), BF16), input_output_aliases=aliases,
        compiler_params=_params(("arbitrary", "arbitrary")))(*operands)


QKV_PIECES = QKV_SHARD // QKV_TILE


def _qkv_dw(name, p, hf, dqkv):
    S = hf.shape[1]
    per_group = 3 * D_MODEL // QKV_TILE
    tile = lambda s: QKV_PIECES * s + p
    return _mm(name, "tn", (N_CHIPS, 1),
               hf, pl.BlockSpec((None, S, D_MODEL), lambda s, k: (tile(s) // per_group, 0, 0)),
               dqkv, pl.BlockSpec((None, S, QKV_TILE), lambda s, k: (tile(s) // per_group, 0, tile(s) % per_group)),
               jax.ShapeDtypeStruct((N_CHIPS, D_MODEL, QKV_TILE), BF16),
               pl.BlockSpec((None, D_MODEL, QKV_TILE), lambda s, k: (s, 0, 0)), None)


def _qkv_dh(name, g, dqkv, w_pieces, dhf, tm=1024):
    S = dqkv.shape[1]
    per_group = 3 * D_MODEL // QKV_TILE

    def body(*refs):
        a_ref, w_hbm = refs[0], refs[1:1 + QKV_PIECES]
        o_ref, w_ref, sems = refs[-3:]

        @pl.when(pl.program_id(0) == 0)
        def _():
            copies = []
            for j in range(per_group):
                t = per_group * g + j
                copies.append(pltpu.make_async_copy(w_hbm[t % QKV_PIECES].at[t // QKV_PIECES],
                                                    w_ref.at[:, pl.ds(j * QKV_TILE, QKV_TILE)], sems.at[j]))
            for cp in copies:
                cp.start()
            for cp in copies:
                cp.wait()

        o_ref[...] = _dot(a_ref[...], w_ref[...], "nt").astype(o_ref.dtype)

    any_spec = pl.BlockSpec(memory_space=pl.ANY)
    in_specs = [pl.BlockSpec((None, tm, 3 * D_MODEL), lambda i: (g, i, 0))] + [any_spec] * QKV_PIECES
    operands = [dqkv] + list(w_pieces)
    aliases = {}
    if dhf is not None:
        in_specs.append(any_spec)
        operands.append(dhf)
        aliases = {1 + QKV_PIECES: 0}
    return _pcall(body, name=name, grid=(S // tm,), in_specs=in_specs,
                  out_specs=pl.BlockSpec((None, tm, D_MODEL), lambda i: (g, i, 0)),
                  out_shape=jax.ShapeDtypeStruct((N_GROUPS, S, D_MODEL), BF16),
                  scratch_shapes=[pltpu.VMEM((D_MODEL, 3 * D_MODEL), BF16), pltpu.SemaphoreType.DMA((per_group,))],
                  input_output_aliases=aliases, compiler_params=_params(("arbitrary",)))(*operands)


def _alibi_coef(g, h):
    vals = [-(2.0 ** (-8.0 * (gg * N_HEADS + h + 1) / (N_GROUPS * N_HEADS))) * ATTN_DILATIONS[gg]
            for gg in range(N_GROUPS)]
    return jnp.where(g == 0, vals[0], jnp.where(g == 1, vals[1], vals[2])).astype(F32)


def _blocks_per_segment(g, S):
    return jnp.right_shift(S // Q_BLOCK, 2 * g)


def _band_bias(pen):
    row = lax.broadcasted_iota(jnp.int32, (Q_BLOCK, 2 * Q_BLOCK), 0)
    col = lax.broadcasted_iota(jnp.int32, (Q_BLOCK, 2 * Q_BLOCK), 1)
    dist = Q_BLOCK + row - col
    valid = jnp.logical_and(dist >= 0, dist <= Q_BLOCK)
    base = jnp.where(valid, jnp.where(col < Q_BLOCK, pen, 0.0), NEG).astype(F32)
    return base, dist.astype(F32)


def _skewed(n_units, stages):
    state = {}
    for step in range(n_units + len(stages) - 1):
        for s, stage in enumerate(stages):
            u = step - s
            if 0 <= u < n_units:
                state[u] = stage(u, state.get(u))
                if s == len(stages) - 1:
                    del state[u]


def _attn_fwd(name, qkv, tq=1024):
    S = qkv.shape[1]
    nqb = tq // Q_BLOCK
    scale = HEAD_DIM ** -0.5

    def body(q_ref, k_ref, kp_ref, v_ref, vp_ref, o_ref, lse_ref, kf_ref, vf_ref):
        g = pl.program_id(0)
        i = pl.program_id(1)
        nb = _blocks_per_segment(g, S)
        kf_ref[:Q_BLOCK] = kp_ref[...]
        kf_ref[Q_BLOCK:] = k_ref[...]
        vf_ref[:Q_BLOCK] = vp_ref[...]
        vf_ref[Q_BLOCK:] = v_ref[...]
        coefs = [_alibi_coef(g, h) for h in range(N_HEADS)]
        units = [(b, h) for b in range(nqb) for h in range(N_HEADS)]
        bias = {}

        def scores(u, _):
            b, h = units[u]
            if b not in bias:
                pen = jnp.where((i * nqb + b) % nb != 0, 0.0, NEG).astype(F32)
                bias.clear()
                bias[b] = _band_bias(pen)
            base, dist = bias[b]
            cols = slice(h * HEAD_DIM, (h + 1) * HEAD_DIM)
            q = q_ref[b * Q_BLOCK:(b + 1) * Q_BLOCK, cols]
            kk = kf_ref[b * Q_BLOCK:(b + 2) * Q_BLOCK, cols]
            return _dot(q, kk, "nt") * scale + (coefs[h] * dist + base)

        def softmax(u, s):
            m = jnp.max(s, axis=-1, keepdims=True)
            p = jnp.exp(s - m)
            den = jnp.sum(p, axis=-1, keepdims=True)
            return (p * (1.0 / den)).astype(BF16), m + jnp.log(den)

        def output(u, st):
            b, h = units[u]
            pn, lse = st
            cols = slice(h * HEAD_DIM, (h + 1) * HEAD_DIM)
            rows = slice(b * Q_BLOCK, (b + 1) * Q_BLOCK)
            o_ref[rows, cols] = _dot(pn, vf_ref[b * Q_BLOCK:(b + 2) * Q_BLOCK, cols], "nn").astype(o_ref.dtype)
            lse_ref[rows, h:h + 1] = lse

        _skewed(len(units), [scores, softmax, output])

    main = lambda c: pl.BlockSpec((None, tq, D_MODEL), lambda g, i: (g, i, c))
    prev = lambda c: pl.BlockSpec((None, Q_BLOCK, D_MODEL), lambda g, i: (g, jnp.maximum(i * nqb - 1, 0), c))
    return _pcall(
        body, name=name, grid=(N_GROUPS, S // tq),
        in_specs=[main(0), main(1), prev(1), main(2), prev(2)],
        out_specs=[pl.BlockSpec((None, tq, D_MODEL), lambda g, i: (g, i, 0)),
                   pl.BlockSpec((None, tq, N_HEADS), lambda g, i: (g, i, 0))],
        out_shape=[jax.ShapeDtypeStruct((N_GROUPS, S, D_MODEL), BF16),
                   jax.ShapeDtypeStruct((N_GROUPS, S, N_HEADS), F32)],
        scratch_shapes=[pltpu.VMEM((tq + Q_BLOCK, D_MODEL), BF16), pltpu.VMEM((tq + Q_BLOCK, D_MODEL), BF16)],
        compiler_params=_params(("parallel", "parallel")))(qkv, qkv, qkv, qkv, qkv)


def _attn_bwd(name, qkv, do, lse, dl, tq=1024):
    S = qkv.shape[1]
    nqb = tq // Q_BLOCK
    n_blocks = S // Q_BLOCK
    scale = HEAD_DIM ** -0.5
    KEYS = 2 * Q_BLOCK

    def body(q_ref, k_ref, v_ref, kp_ref, vp_ref, qn_ref, do_ref, don_ref, lse_ref, lsen_ref, dl_ref, dln_ref,
             out_ref, kf_ref, vf_ref, dk_ref, dv_ref):
        g = pl.program_id(0)
        i = pl.program_id(1)
        nb = _blocks_per_segment(g, S)
        kf_ref[:Q_BLOCK] = kp_ref[...]
        kf_ref[Q_BLOCK:] = k_ref[...]
        vf_ref[:Q_BLOCK] = vp_ref[...]
        vf_ref[Q_BLOCK:] = v_ref[...]
        coefs = [_alibi_coef(g, h) for h in range(N_HEADS)]
        units = [(h, b) for h in range(N_HEADS) for b in range(nqb + 1)]
        bias = {}

        def band(b):
            if b not in bias:
                blk = i * nqb + b
                ok = blk % nb != 0
                if b == nqb:
                    ok = jnp.logical_and(ok, blk < n_blocks)
                bias[b] = _band_bias(jnp.where(ok, 0.0, NEG).astype(F32))
            return bias[b]

        def operands(h, b):
            cols = slice(h * HEAD_DIM, (h + 1) * HEAD_DIM)
            if b == nqb:
                keys = slice(tq, tq + Q_BLOCK)
                return (qn_ref[:, cols], don_ref[:, cols], lsen_ref[:, h:h + 1], dln_ref[:, h:h + 1],
                        kf_ref[keys, cols], vf_ref[keys, cols])
            rows = slice(b * Q_BLOCK, (b + 1) * Q_BLOCK)
            keys = slice(b * Q_BLOCK, b * Q_BLOCK + KEYS)
            return (q_ref[rows, cols], do_ref[rows, cols], lse_ref[rows, h:h + 1], dl_ref[rows, h:h + 1],
                    kf_ref[keys, cols], vf_ref[keys, cols])

        def probs(u, _):
            h, b = units[u]
            q, do_b, lse_b, dl_b, kk, vv = operands(h, b)
            base, dist = band(b)
            if b == nqb:
                base, dist = base[:, :Q_BLOCK], dist[:, :Q_BLOCK]
            p = jnp.exp(_dot(q, kk, "nt") * scale + (coefs[h] * dist + base) - lse_b)
            return p, _dot(do_b, vv, "nt")

        def dscores(u, st):
            h, b = units[u]
            p, dp = st
            dl_b = operands(h, b)[3]
            return ((p * (dp - dl_b)) * scale).astype(BF16), p.astype(BF16)

        def grads(u, st):
            h, b = units[u]
            ds, pb = st
            q, do_b, _, _, kk, _ = operands(h, b)
            cols = slice(h * HEAD_DIM, (h + 1) * HEAD_DIM)
            if b < nqb:
                out_ref[b * Q_BLOCK:(b + 1) * Q_BLOCK, cols] = _dot(ds, kk, "nn").astype(out_ref.dtype)
            if b == 0:
                dk_ref[:Q_BLOCK] = _dot(ds[:, Q_BLOCK:], q, "tn")
                dv_ref[:Q_BLOCK] = _dot(pb[:, Q_BLOCK:], do_b, "tn")
                return None
            dk = _dot(ds, q, "tn")
            dv = _dot(pb, do_b, "tn")
            before = slice((b - 1) * Q_BLOCK, b * Q_BLOCK)
            dk_ref[before] += dk[:Q_BLOCK]
            dv_ref[before] += dv[:Q_BLOCK]
            if b < nqb:
                own = slice(b * Q_BLOCK, (b + 1) * Q_BLOCK)
                dk_ref[own] = dk[Q_BLOCK:]
                dv_ref[own] = dv[Q_BLOCK:]
            else:
                out_ref[:, D_MODEL + h * HEAD_DIM:D_MODEL + (h + 1) * HEAD_DIM] = dk_ref[...].astype(out_ref.dtype)
                out_ref[:, 2 * D_MODEL + h * HEAD_DIM:2 * D_MODEL + (h + 1) * HEAD_DIM] = (
                    dv_ref[...].astype(out_ref.dtype))
            return None

        _skewed(len(units), [probs, dscores, grads])

    nxt_blk = lambda i: jnp.minimum((i + 1) * nqb, n_blocks - 1)
    main = lambda c: pl.BlockSpec((None, tq, D_MODEL), lambda g, i: (g, i, c))
    prev = lambda c: pl.BlockSpec((None, Q_BLOCK, D_MODEL), lambda g, i: (g, jnp.maximum(i * nqb - 1, 0), c))
    row = pl.BlockSpec((None, tq, D_MODEL), lambda g, i: (g, i, 0))
    row_n = pl.BlockSpec((None, Q_BLOCK, D_MODEL), lambda g, i: (g, nxt_blk(i), 0))
    col = pl.BlockSpec((None, tq, N_HEADS), lambda g, i: (g, i, 0))
    col_n = pl.BlockSpec((None, Q_BLOCK, N_HEADS), lambda g, i: (g, nxt_blk(i), 0))
    return _pcall(
        body, name=name, grid=(N_GROUPS, S // tq),
        in_specs=[main(0), main(1), main(2), prev(1), prev(2), row_n, row, row_n, col, col_n, col, col_n],
        out_specs=pl.BlockSpec((None, tq, 3 * D_MODEL), lambda g, i: (g, i, 0)),
        out_shape=jax.ShapeDtypeStruct((N_GROUPS, S, 3 * D_MODEL), BF16),
        scratch_shapes=[pltpu.VMEM((tq + Q_BLOCK, D_MODEL), BF16), pltpu.VMEM((tq + Q_BLOCK, D_MODEL), BF16),
                        pltpu.VMEM((tq, HEAD_DIM), F32), pltpu.VMEM((tq, HEAD_DIM), F32)],
        compiler_params=_params(("parallel", "parallel")))(qkv, qkv, qkv, qkv, qkv, qkv, do, do, lse, lse, dl, dl)


def _group_weights(lse_ref):
    l0, l1, l2 = lse_ref[0], lse_ref[1], lse_ref[2]
    m = jnp.maximum(jnp.maximum(l0, l1), l2)
    e = [jnp.exp(l0 - m), jnp.exp(l1 - m), jnp.exp(l2 - m)]
    den = e[0] + e[1] + e[2]
    return [ei / den for ei in e]


MERGE_TOKENS = 512


def _folded_views(a, tb):
    _, S, C = a.shape
    views, specs = [], []
    for g, dil in enumerate(ATTN_DILATIONS):
        views.append(a.reshape(N_GROUPS * dil, S // dil, C))
        specs.append(pl.BlockSpec((dil, tb // dil, C), lambda i, g=g: (g, i, 0)))
    return views, specs


def _unfold_into(dst_ref, folded_refs):
    for g, (dil, ref) in enumerate(zip(ATTN_DILATIONS, folded_refs)):
        n = ref.shape[1]
        for r in range(dil):
            for c, cols in enumerate(_lane_chunks(ref.shape[2])):
                dst_ref[g, c, _strided_rows(r, n, dil), :] = ref[r, :, cols].astype(F32)


def _merge_fwd(name, o, lse, tb=MERGE_TOKENS):
    S = o.shape[1]

    def body(o0_ref, o1_ref, o2_ref, lse_ref, out_ref, o_ref):
        _unfold_into(o_ref, (o0_ref, o1_ref, o2_ref))
        w = _group_weights(lse_ref)
        for h in range(N_HEADS):
            cols = slice(h * HEAD_DIM, (h + 1) * HEAD_DIM)
            acc = w[0][:, h:h + 1] * o_ref[0, h]
            for gg in range(1, N_GROUPS):
                acc = acc + w[gg][:, h:h + 1] * o_ref[gg, h]
            out_ref[:, cols] = acc.astype(out_ref.dtype)

    views, specs = _folded_views(o, tb)
    return _pcall(body, name=name, grid=(S // tb,),
                  in_specs=specs + [pl.BlockSpec((N_GROUPS, tb, N_HEADS), lambda i: (0, i, 0))],
                  out_specs=pl.BlockSpec((tb, D_MODEL), lambda i: (i, 0)),
                  out_shape=jax.ShapeDtypeStruct((S, D_MODEL), BF16),
                  scratch_shapes=[pltpu.VMEM((N_GROUPS, N_HEADS, tb, HEAD_DIM), F32)],
                  compiler_params=_params(("parallel",)))(*views, lse)


def _merge_bwd(name, dx, w_out, o, lse, tb=MERGE_TOKENS):
    S = o.shape[1]
    n_chunks = sum(ATTN_DILATIONS)

    def body(dx_ref, w_ref, o0_ref, o1_ref, o2_ref, lse_ref, do_hbm, dl_ref, o_ref, dt_ref, df_ref, d_ref, sems):
        i = pl.program_id(0)
        d_ref[...] = _dot(dx_ref[...].astype(BF16), w_ref[...], "nt")
        _unfold_into(o_ref, (o0_ref, o1_ref, o2_ref))
        w = _group_weights(lse_ref)
        for h in range(N_HEADS):
            cols = slice(h * HEAD_DIM, (h + 1) * HEAD_DIM)
            dv = d_ref[:, cols]
            merged = w[0][:, h:h + 1] * o_ref[0, h]
            for gg in range(1, N_GROUPS):
                merged = merged + w[gg][:, h:h + 1] * o_ref[gg, h]
            dsum = jnp.sum(dv * merged, axis=-1, keepdims=True)
            for gg in range(N_GROUPS):
                wg = w[gg][:, h:h + 1]
                dt_ref[gg, h] = wg * dv
                dl_ref[gg, :, h:h + 1] = wg * dsum
        copies = []
        for gg, dil in enumerate(ATTN_DILATIONS):
            n = tb // dil
            for r in range(dil):
                for h, cols in enumerate(_lane_chunks(D_MODEL)):
                    df_ref[gg, r * n:(r + 1) * n, cols] = (
                        dt_ref[gg, h, _strided_rows(r, n, dil), :].astype(df_ref.dtype))
                cp = pltpu.make_async_copy(df_ref.at[gg, pl.ds(r * n, n)],
                                           do_hbm.at[gg, pl.ds(r * (S // dil) + i * n, n)], sems.at[len(copies)])
                cp.start()
                copies.append(cp)
        for cp in copies:
            cp.wait()

    views, specs = _folded_views(o, tb)
    small = pl.BlockSpec((N_GROUPS, tb, N_HEADS), lambda i: (0, i, 0))
    return _pcall(body, name=name, grid=(S // tb,),
                  in_specs=[pl.BlockSpec((tb, D_MODEL), lambda i: (i, 0)),
                            pl.BlockSpec((D_MODEL, D_MODEL), lambda i: (0, 0))] + specs + [small],
                  out_specs=[pl.BlockSpec(memory_space=pl.ANY), small],
                  out_shape=[jax.ShapeDtypeStruct((N_GROUPS, S, D_MODEL), BF16),
                             jax.ShapeDtypeStruct((N_GROUPS, S, N_HEADS), F32)],
                  scratch_shapes=[pltpu.VMEM((N_GROUPS, N_HEADS, tb, HEAD_DIM), F32),
                                  pltpu.VMEM((N_GROUPS, N_HEADS, tb, HEAD_DIM), F32),
                                  pltpu.VMEM((N_GROUPS, tb, D_MODEL), BF16), pltpu.VMEM((tb, D_MODEL), F32),
                                  pltpu.SemaphoreType.DMA((n_chunks,))],
                  compiler_params=_params(("arbitrary",)))(dx, w_out, *views, lse)


def _shift_rows(a, k):
    return pltpu.roll(a, k % a.shape[0], axis=0)


def _pool_level(g, levels):
    return jnp.where(g == 0, levels[0], jnp.where(g == 1, levels[1], jnp.where(g == 2, levels[2], levels[3])))


def _pool_fwd(name, h, w_in, w_group, scale, x_in, gain_next, tr=1024):
    S, D = h.shape
    H = POOL_HALO
    n_groups = D // POOL_DIM

    def body(h_ref, hh_ref, wi_ref, w_ref, sc_ref, x_ref, gn_ref, y_ref, z_ref, xo_ref, hn_ref, xs_ref):
        i = pl.program_id(0)
        g = pl.program_id(1)
        uv = _dot(h_ref[...], wi_ref[...], "nn")
        halo = jnp.where(i > 0, _dot(hh_ref[...], wi_ref[...], "nn"), 0.0)
        s = jnp.concatenate([halo, uv], axis=0)
        levels = []
        for k in (1, 2, 4, 8):
            s = s + _shift_rows(s, k)
            levels.append(s[H:])
        t = i * tr + lax.broadcasted_iota(jnp.int32, (tr, 1), 0)
        cnt = jnp.minimum(t + 1, jnp.left_shift(2, g)).astype(F32)
        y = _pool_level(g, levels) / cnt - uv
        yb = y.astype(BF16)
        z = _dot(yb, w_ref[...], "nn")
        y_ref[...] = yb
        z_ref[...] = z.astype(z_ref.dtype)
        x_out = x_ref[...] + z * sc_ref[...]
        xo_ref[...] = x_out
        xs_ref[g] = x_out

        @pl.when(g == n_groups - 1)
        def _():
            xf = jnp.concatenate([xs_ref[k] for k in range(n_groups)], axis=1)
            r = lax.rsqrt(jnp.mean(xf * xf, axis=-1, keepdims=True) + RMS_EPS)
            hn_ref[...] = ((xf * r) * gn_ref[...]).astype(hn_ref.dtype)

    blk = pl.BlockSpec((tr, POOL_DIM), lambda i, g: (i, g))
    row = pl.BlockSpec((tr, D), lambda i, g: (i, 0))
    return _pcall(
        body, name=name, grid=(S // tr, n_groups),
        in_specs=[row, pl.BlockSpec((H, D), lambda i, g: (jnp.maximum(i * (tr // H) - 1, 0), 0)),
                  pl.BlockSpec((D, POOL_DIM), lambda i, g: (0, g)),
                  pl.BlockSpec((None, POOL_DIM, POOL_DIM), lambda i, g: (g, 0, 0)),
                  pl.BlockSpec((1, POOL_DIM), lambda i, g: (0, g)), blk, pl.BlockSpec((1, D), lambda i, g: (0, 0))],
        out_specs=[blk, blk, blk, row],
        out_shape=[jax.ShapeDtypeStruct((S, D), BF16), jax.ShapeDtypeStruct((S, D), BF16),
                   jax.ShapeDtypeStruct((S, D), F32), jax.ShapeDtypeStruct((S, D), BF16)],
        scratch_shapes=[pltpu.VMEM((n_groups, tr, POOL_DIM), F32)],
        compiler_params=_params(("parallel", "arbitrary")))(h, h, w_in, w_group, scale, x_in, gain_next)


def _pool_bwd(name, d_out, z, y, scale, w_group, tr=1024):
    S, D = d_out.shape
    H = POOL_HALO

    def body(d_ref, dn_ref, z_ref, y_ref, sc_ref, w_ref, du_ref, dw_ref, dsc_ref):
        g = pl.program_id(0)
        i = pl.program_id(1)
        dv = d_ref[...]
        dz = jnp.concatenate([dv, dn_ref[...]], axis=0) * sc_ref[...]
        dzb = dz.astype(BF16)
        dy = _dot(dzb, w_ref[...], "nt")
        t = i * tr + lax.broadcasted_iota(jnp.int32, (tr + H, 1), 0)
        cnt = jnp.minimum(t + 1, jnp.left_shift(2, g)).astype(F32)
        s = jnp.where(t < S, dy / cnt, 0.0)
        levels = []
        for k in (1, 2, 4, 8):
            s = s + _shift_rows(s, -k)
            levels.append(s[:tr])
        du_ref[...] = (_pool_level(g, levels) - dy[:tr]).astype(du_ref.dtype)
        dw = _dot(y_ref[...], dzb[:tr], "tn")
        dsc = jnp.sum(dv * z_ref[...].astype(F32), axis=0, keepdims=True)

        @pl.when(i == 0)
        def _():
            dw_ref[...] = dw
            dsc_ref[...] = dsc

        @pl.when(i > 0)
        def _():
            dw_ref[...] += dw
            dsc_ref[...] += dsc

    blk = pl.BlockSpec((tr, POOL_DIM), lambda g, i: (i, g))
    vec = pl.BlockSpec((1, POOL_DIM), lambda g, i: (0, g))
    mat = pl.BlockSpec((None, POOL_DIM, POOL_DIM), lambda g, i: (g, 0, 0))
    nxt = pl.BlockSpec((H, POOL_DIM), lambda g, i: (jnp.minimum((i + 1) * (tr // H), S // H - 1), g))
    return _pcall(
        body, name=name, grid=(D // POOL_DIM, S // tr), in_specs=[blk, nxt, blk, blk, vec, mat],
        out_specs=[blk, mat, vec],
        out_shape=[jax.ShapeDtypeStruct((S, D), BF16), jax.ShapeDtypeStruct((D // POOL_DIM, POOL_DIM, POOL_DIM), F32),
                   jax.ShapeDtypeStruct((1, D), F32)],
        compiler_params=_params(("parallel", "arbitrary")))(d_out, d_out, z, y, scale, w_group)


def _row_tile(rows, cols, target_bytes=1 << 20):
    best = rows
    for tr in range(8, rows + 1, 8):
        if rows % tr == 0 and tr * cols * 4 <= target_bytes:
            best = tr
    return best if best * cols * 4 <= 4 * target_bytes else rows


def _adamw_refs(w_ref, g_ref, m_ref, v_ref, go_ref, d_ref, mo_ref, vo_ref):
    gv = g_ref[...]
    m2 = ADAM_B1 * m_ref[...] + (1.0 - ADAM_B1) * gv
    v2 = ADAM_B2 * v_ref[...] + (1.0 - ADAM_B2) * (gv * gv)
    m_hat = m2 / (1.0 - ADAM_B1 ** ADAM_STEP)
    v_hat = v2 / (1.0 - ADAM_B2 ** ADAM_STEP)
    d_ref[...] = -ADAM_LR * (m_hat / (jnp.sqrt(v_hat) + ADAM_EPS) + ADAM_WD * w_ref[...])
    go_ref[...] = gv
    mo_ref[...] = m2
    vo_ref[...] = v2


def _adamw(name, w, g, m, v):
    R, C = w.shape
    tr = _row_tile(R, C) if R % 8 == 0 else R
    blk = pl.BlockSpec((tr, C), lambda i: (i, 0))
    sds = jax.ShapeDtypeStruct((R, C), F32)
    return _pcall(_adamw_refs, name=name, grid=(R // tr,), in_specs=[blk] * 4, out_specs=[blk] * 4,
                  out_shape=[sds] * 4, compiler_params=_params(("parallel",)))(w, g, m, v)


def _adamw_small(name, items):
    n = len(items)

    def body(*refs):
        for t in range(n):
            _adamw_refs(*refs[4 * t:4 * t + 4], *refs[4 * n + 4 * t:4 * n + 4 * t + 4])

    specs, shapes = [], []
    for w, _, _, _ in items:
        specs += [pl.BlockSpec(w.shape, lambda i: (0, 0))] * 4
        shapes += [jax.ShapeDtypeStruct(w.shape, F32)] * 4
    res = _pcall(body, name=name, grid=(1,), in_specs=specs, out_specs=specs, out_shape=shapes,
                 compiler_params=_params(("arbitrary",)))(*[a for item in items for a in item])
    return [res[4 * t:4 * t + 4] for t in range(n)]


def _sum_leading(name, a, out_dtype):
    n, R, C = a.shape
    tr = _row_tile(R, C) if R % 16 == 0 else R
    if tr % 16:
        tr = R

    def body(a_ref, o_ref):
        acc = a_ref[0].astype(F32)
        for j in range(1, n):
            acc = acc + a_ref[j].astype(F32)
        o_ref[...] = acc.astype(o_ref.dtype)

    return _pcall(body, name=name, grid=(R // tr,), in_specs=[pl.BlockSpec((n, tr, C), lambda i: (0, i, 0))],
                  out_specs=pl.BlockSpec((tr, C), lambda i: (i, 0)), out_shape=jax.ShapeDtypeStruct((R, C), out_dtype),
                  compiler_params=_params(("parallel",)))(a)


def _cast_many(name, items, chip, steps=4):
    tiles = []
    for w, _, dtype in items:
        R = w.shape[1]
        nt = steps if R % (steps * 16) == 0 else 1
        tiles.append((nt, R // nt))

    def body(c_ref, *refs):
        i = pl.program_id(0)
        for t, (nt, _) in enumerate(tiles):
            w_ref, o_ref = refs[t], refs[len(items) + t]

            def cast(w_ref=w_ref, o_ref=o_ref):
                o_ref[...] = w_ref[...].astype(o_ref.dtype)

            if nt == steps:
                cast()
            else:
                pl.when(i < nt)(cast)

    in_specs, out_specs, out_shape = [], [], []
    for (w, layer, dtype), (nt, tr) in zip(items, tiles):
        C = w.shape[2]
        in_specs.append(pl.BlockSpec((None, tr, C), lambda i, c_ref, l=layer, nt=nt: (l, jnp.minimum(i, nt - 1), 0)))
        out_specs.append(pl.BlockSpec((None, tr, C), lambda i, c_ref, nt=nt: (c_ref[0], jnp.minimum(i, nt - 1), 0)))
        out_shape.append(jax.ShapeDtypeStruct((N_CHIPS, w.shape[1], C), dtype))
    return _pcall(body, prefetch=1, name=name, grid=(steps,), in_specs=in_specs, out_specs=out_specs,
                  out_shape=out_shape, compiler_params=_params(("arbitrary",)))(chip, *[w for w, _, _ in items])


def _cast_qkv(name, w, chip, tr=256):
    _, R, C = w.shape

    def body(c_ref, w_ref, own_ref, *piece_refs):
        v = w_ref[...].astype(BF16)
        own_ref[...] = v
        for p, ref in enumerate(piece_refs):
            ref[...] = v[:, p * QKV_TILE:(p + 1) * QKV_TILE]

    piece = pl.BlockSpec((None, tr, QKV_TILE), lambda i, c_ref: (c_ref[0], i, 0))
    return _pcall(body, prefetch=1, name=name, grid=(R // tr,),
                  in_specs=[pl.BlockSpec((None, tr, C), lambda i, c_ref: (0, i, 0))],
                  out_specs=[pl.BlockSpec((tr, C), lambda i, c_ref: (i, 0))] + [piece] * QKV_PIECES,
                  out_shape=[jax.ShapeDtypeStruct((R, C), BF16)]
                  + [jax.ShapeDtypeStruct((N_CHIPS, R, QKV_TILE), BF16)] * QKV_PIECES,
                  compiler_params=_params(("parallel",)))(chip, w)


def _owner_sum(name, mine, landed, chip, core, out=None, layer=None, col=None):
    _, half, C = mine.shape
    k, n_col = col if col is not None else (0, 1)
    tr = _row_tile(half, C)
    if tr % 16:
        tr = half
    nrt = half // tr
    has_out = out is not None

    def body(*refs):
        m_ref, l_ref, o_ref = refs[2], refs[3], refs[-1]
        acc = m_ref[...].astype(F32)
        for j in range(3):
            acc = acc + l_ref[j].astype(F32)
        o_ref[...] = acc

    if layer is None:
        out_shape = jax.ShapeDtypeStruct((2 * half, n_col * C), F32)
        o_spec = pl.BlockSpec((tr, C), lambda i, ch, co: (co[0] * nrt + i, k))
    else:
        out_shape = jax.ShapeDtypeStruct((2, 2 * half, C), F32)
        o_spec = pl.BlockSpec((None, tr, C), lambda i, ch, co: (layer, co[0] * nrt + i, 0))
    in_specs = [pl.BlockSpec((None, tr, C), lambda i, ch, co: (ch[0], i, 0)),
                pl.BlockSpec((3, tr, C), lambda i, ch, co: (0, i, 0))]
    operands = [chip, core, mine, landed]
    aliases = {}
    if has_out:
        in_specs.append(ANY)
        operands.append(out)
        aliases = {4: 0}
    grid_spec = pltpu.PrefetchScalarGridSpec(num_scalar_prefetch=2, grid=(nrt,), in_specs=in_specs, out_specs=o_spec)
    return _pcall(body, name=name, grid_spec=grid_spec, out_shape=out_shape, input_output_aliases=aliases,
                  compiler_params=_params(("parallel",)))(*operands)


def _add_my_half(name, ps, qs, core):
    n = len(ps)

    def body(c_ref, *refs):
        for t in range(n):
            p_ref, q_ref, o_ref = refs[t], refs[n + t], refs[2 * n + t]
            o_ref[...] = (p_ref[...].astype(F32) + q_ref[...].astype(F32)).astype(o_ref.dtype)

    halves = [(p.shape[1] // 2, p.shape[2]) for p in ps]
    mine = [pl.BlockSpec((None, h, c), lambda s, c_ref: (s, c_ref[0], 0)) for h, c in halves]
    whole = [pl.BlockSpec((None, h, c), lambda s, c_ref: (s, 0, 0)) for h, c in halves]
    return _pcall(body, prefetch=1, name=name, grid=(N_CHIPS,), in_specs=mine + whole, out_specs=whole,
                  out_shape=[jax.ShapeDtypeStruct((N_CHIPS, h, c), BF16) for h, c in halves],
                  compiler_params=_params(("parallel",)))(core, *ps, *qs)


ANY = pl.BlockSpec(memory_space=pl.ANY)


def _place():
    x, y, c = lax.axis_index("x"), lax.axis_index("y"), lax.axis_index("c")
    other_chips = [(1 - x, y), (x, 1 - y), (1 - x, 1 - y)]
    return x, y, c, other_chips


def _remote(src, dst, send, recv, k, to):
    return pltpu.make_async_remote_copy(src_ref=src, dst_ref=dst, send_sem=send.at[k], recv_sem=recv.at[k],
                                        device_id=to, device_id_type=MESH)


def _in_place(bufs):
    return dict(operands=bufs, out_shapes=[jax.ShapeDtypeStruct(b.shape, b.dtype) for b in bufs],
                aliases={t: t for t in range(len(bufs))})


def _gather_rider(bufs, jobs):
    def copies(ins, outs, send, recv, base=0):
        x, y, c, chips = _place()
        out = []
        for t, over_ici, rows in jobs:
            ref = outs[t]
            if rows is not None:
                a, b, n = rows
                half = ref.shape[1] // 2
                rows = pl.ds(c * half + a * half // n, (b - a) * half // n)
            for px, py in chips:
                slot = 2 * x + y if over_ici else 2 * px + py
                piece = ref.at[slot] if rows is None else ref.at[slot, rows]
                out.append(_remote(piece, piece, send, recv, base + len(out), (px, py, c) if over_ici else (x, y, 1 - c)))
        return out

    return _Rider(n_copies=3 * len(jobs), copies=copies, **_in_place(bufs))


def _gather_and_pass_rider(buf):
    half = buf.shape[1] // 2

    def pieces(outs):
        x, y, c, chips = _place()
        rows = lambda core: pl.ds(core * half, half)
        mine = outs[0].at[2 * x + y, rows(c)]
        landed = [outs[0].at[2 * px + py, rows(c)] for px, py in chips]
        theirs = [outs[0].at[2 * px + py, rows(1 - c)] for px, py in chips]
        return (x, y, c, chips), mine, landed, theirs

    def start(ins, outs, send, recv, base=0):
        (x, y, c, chips), mine, _, _ = pieces(outs)
        for j, (px, py) in enumerate(chips):
            _remote(mine, mine, send, recv, base + j, (px, py, c)).start()

    def finish(ins, outs, send, recv, base=0):
        (x, y, c, chips), mine, landed, theirs = pieces(outs)
        sibling = (x, y, 1 - c)
        passed = []
        for j, (px, py) in enumerate(chips):
            _remote(landed[j], landed[j], send, recv, base + j, (px, py, c)).wait_recv()
            passed.append(_remote(landed[j], landed[j], send, recv, base + 3 + j, sibling))
            passed[-1].start()
        for j in range(3):
            _remote(theirs[j], theirs[j], send, recv, base + 3 + j, sibling).wait_recv()
        for j, (px, py) in enumerate(chips):
            _remote(mine, mine, send, recv, base + j, (px, py, c)).wait_send()
            passed[j].wait_send()

    return _Rider(n_copies=6, start=start, finish=finish, **_in_place([buf]))


def _swap_halves_rider(parts):
    n = len(parts)

    def copies(ins, outs, send, recv, base=0):
        x, y, c, _ = _place()
        out = []
        for t in range(n):
            half = ins[t].shape[1] // 2
            out.append(_remote(ins[t].at[:, pl.ds((1 - c) * half, half)], outs[t], send, recv, base + t,
                               (x, y, 1 - c)))
        return out

    shapes = [jax.ShapeDtypeStruct((p.shape[0], p.shape[1] // 2, p.shape[2]), p.dtype) for p in parts]
    return _Rider(parts, shapes, {}, n, copies)


def _scatter_rider(sums, landed, relations):
    n = len(sums)
    kept = [t for t in range(n) if landed[t] is not None]

    def copies(ins, outs, send, recv, base=0):
        x, y, c, chips = _place()
        out = []
        for t in range(n):
            for j in relations[t]:
                px, py = chips[j]
                out.append(_remote(ins[t].at[2 * px + py], outs[t].at[j], send, recv, base + len(out), (px, py, c)))
        return out

    shapes = [jax.ShapeDtypeStruct((3,) + s.shape[1:], s.dtype) for s in sums]
    return _Rider(list(sums) + [landed[t] for t in kept], shapes, {n + k: t for k, t in enumerate(kept)},
                  sum(len(r) for r in relations), copies)


def _share_rider(blocks, pieces):
    def copies(ins, outs, send, recv, base=0):
        x, y, c, _ = _place()
        out = []
        for k, (t, l, col) in enumerate(pieces):
            ref = outs[t] if l is None else outs[t].at[l]
            r2 = ref.shape[0] // 2
            piece = ref.at[pl.ds(c * r2, r2)]
            if col is not None:
                width = ref.shape[1] // col[1]
                piece = ref.at[pl.ds(c * r2, r2), pl.ds(col[0] * width, width)]
            out.append(_remote(piece, piece, send, recv, base + k, (x, y, 1 - c)))
        return out

    return _Rider(n_copies=len(pieces), copies=copies, **_in_place(blocks))


def _gather_small(name, v):
    def body(v_ref, out_ref, send_sem, recv_sem, local_sem):
        x, y, c, _ = _place()
        me = 4 * x + 2 * y + c
        flips = [(fx, fy, fc) for fx in (0, 1) for fy in (0, 1) for fc in (0, 1)][1:]
        local = pltpu.make_async_copy(v_ref, out_ref.at[me], local_sem)
        local.start()
        copies = []
        for j, (fx, fy, fc) in enumerate(flips):
            peer = (x ^ fx, y ^ fy, c ^ fc)
            copies.append(pltpu.make_async_remote_copy(
                src_ref=v_ref, dst_ref=out_ref.at[me], send_sem=send_sem.at[j], recv_sem=recv_sem.at[j],
                device_id=peer, device_id_type=MESH))
        for cp in copies:
            cp.start()
        for j, (fx, fy, fc) in enumerate(flips):
            peer = (x ^ fx, y ^ fy, c ^ fc)
            pltpu.make_async_remote_copy(
                src_ref=v_ref, dst_ref=out_ref.at[4 * peer[0] + 2 * peer[1] + peer[2]], send_sem=send_sem.at[j],
                recv_sem=recv_sem.at[j], device_id=peer, device_id_type=MESH).wait_recv()
        for cp in copies:
            cp.wait_send()
        local.wait()

    return _pcall(body, name=name, in_specs=[ANY], out_specs=ANY,
                  out_shape=jax.ShapeDtypeStruct((8,) + v.shape, v.dtype),
                  scratch_shapes=[pltpu.SemaphoreType.DMA((7,)), pltpu.SemaphoreType.DMA((7,)),
                                  pltpu.SemaphoreType.DMA])(v)


def _fold(a, dil):
    if dil == 1:
        return a
    S, C = a.shape
    return a.reshape(S // dil, dil, C).transpose(1, 0, 2).reshape(S, C)


def _unfold(a, dil):
    if dil == 1:
        return a
    S, C = a.shape
    return a.reshape(dil, S // dil, C).transpose(1, 0, 2).reshape(S, C)


def _fold_groups(a):
    return jnp.stack([_fold(a[g] if a.ndim == 3 else a, d) for g, d in enumerate(ATTN_DILATIONS)])


def _unfold_groups(a):
    return jnp.stack([_unfold(a[g], d) for g, d in enumerate(ATTN_DILATIONS)])


class _NoComm:
    def __init__(self, weights):
        self.weights = weights
        self.grads = {}
        self.chip = jnp.zeros((1,), jnp.int32)

    def w(self, name):
        return self.weights[name]

    def run(self, fn, name, *args, **kw):
        return fn(name, *args, **kw)

    def grad(self, name, value):
        self.grads[name] = value


def _local_step(xs, tgt, attn_norm, ffn_norm, final_norm, comm):
    run = comm.run
    hf = run(_norm_fold, "fold", xs, attn_norm)
    qkv = run(_qkv_tiles, "qkv_own", None, hf, comm.w("wq_own"), None, comm.chip)
    for p in range(QKV_PIECES):
        qkv = run(_qkv_tiles, "qkv_piece%d" % p, p, hf, comm.w("wq%d" % p), qkv, comm.chip)
    o_f, lse_f = run(_attn_fwd, "attn_fwd", qkv)
    lse_t = _unfold_groups(lse_f)
    merged = run(_merge_fwd, "merge_fwd", o_f, lse_t)
    x1, h1 = run(_proj_res_norm, "attn_out", merged, comm.w("o"), xs, ffn_norm[0:1])
    gu0, act0 = run(_ffn_up, "ffn0_up", h1, comm.w("gu0"))
    x2, h2 = run(_proj_res_norm, "ffn0_down", act0, comm.w("d0"), x1, comm.w("pool_norm"))
    y, z, x3, h3 = run(_pool_fwd, "pool_fwd", h2, comm.w("p"), comm.w("pg"), comm.w("pool_scale"), x2, ffn_norm[1:2])
    gu1, act1 = run(_ffn_up, "ffn1_up", h3, comm.w("gu1"))
    loss, dx4, dx4_b, d_final = run(_proj_res_loss, "ffn1_down", act1, comm.w("d1"), x3, final_norm, tgt)

    dgu1 = run(_ffn_down_bwd, "ffn1_down_bwd", dx4_b, comm.w("d1"), gu1)
    comm.grad("d1", run(_mm_tn, "ffn1_down_dw", act1, dx4_b, FF_SHARD, 2048))
    comm.grad("gu1", run(_ffn_dw_up, "ffn1_up_dw", h3, dgu1))
    dx3, d_ffn1 = run(_ffn_dh, "ffn1_up_dh", dgu1, comm.w("gu1"), x3, ffn_norm[1:2], dx4)

    du, dw_pg, d_scale = run(_pool_bwd, "pool_bwd", dx3, z, y, comm.w("pool_scale"), comm.w("pg"))
    comm.grad("pg", dw_pg)
    comm.grad("p", run(_mm_tn, "pool_in_dw", h2, du, D_MODEL, h2.shape[0]))
    dx2, dx2_b, d_pool = run(_dh_norm_bwd, "pool_in_dh", du, comm.w("p"), x2, comm.w("pool_norm"), dx3)

    dgu0 = run(_ffn_down_bwd, "ffn0_down_bwd", dx2_b, comm.w("d0"), gu0)
    comm.grad("d0", run(_mm_tn, "ffn0_down_dw", act0, dx2_b, FF_SHARD, 2048))
    comm.grad("gu0", run(_ffn_dw_up, "ffn0_up_dw", h1, dgu0))
    dx1, d_ffn0 = run(_ffn_dh, "ffn0_up_dh", dgu0, comm.w("gu0"), x1, ffn_norm[0:1], dx2)

    comm.grad("o", run(_mm_tn, "attn_out_dw", merged, dx1, D_MODEL, 2048))
    do_f, dl_t = run(_merge_bwd, "merge_bwd", dx1, comm.w("o"), o_f, lse_t)
    dqkv = run(_attn_bwd, "attn_bwd", qkv, do_f, lse_f, _fold_groups(dl_t))
    for p in range(QKV_PIECES):
        comm.grad("qkv%d" % p, run(_qkv_dw, "qkv_dw%d" % p, p, hf, dqkv))
    dhf = None
    for g in range(N_GROUPS):
        dhf = run(_qkv_dh, "qkv_dh%d" % g, g, dqkv, [comm.w("wq%d" % p) for p in range(QKV_PIECES)], dhf)
    grad_x, d_attn = run(_rms_bwd_folded, "norm_attn_bwd", xs, attn_norm, dhf, dx1)

    small = dict(attn=d_attn, ffn0=d_ffn0, ffn1=d_ffn1, final=d_final, pool=d_pool, scale=d_scale)
    return loss, grad_x, small


TENSORS = ("qkv", "o", "p", "pg", "gu0", "gu1", "d0", "d1")


def _block_of(name):
    if name[:-1] in ("gu", "d"):
        return name[:-1], int(name[-1]), None
    if name[:-1] == "qkv":
        return "qkv", None, (int(name[-1]), QKV_PIECES)
    return name, None, None


class _MeshComm:
    def __init__(self, bufs, chip_arr, core_arr, pg_rows):
        self.bufs = dict(bufs)
        self.chip, self.chip_arr, self.core_arr, self.pg_rows = chip_arr, chip_arr, core_arr, pg_rows
        self.parts, self.sums, self.blocks, self.landed, self.arrived = {}, {}, {}, {}, {}
        move = lambda ici=(), d2d=(): lambda: self.gather(ici, d2d)
        whole = lambda name: lambda: self.gather_and_pass(name)
        swap = lambda *names: lambda: self.swap(names)
        scatter = lambda *names: lambda: self.scatter(names)
        share = lambda *names: lambda: self.share(names)
        self.plan = {
            "fold": [whole("wq0")],
            "qkv_own": [whole("wq1")],
            "qkv_piece0": [whole("wq2")],
            "qkv_piece1": [move(ici=["o", "gu0[0:1/4]"])],
            "qkv_piece2": [move(ici=["gu0[1:2/4]"], d2d=["o", "gu0[0:1/4]"])],
            "attn_fwd": [move(ici=["gu0[2:4/4]", "d0[0:1/2]"], d2d=["gu0[1:2/4]"])],
            "merge_fwd": [move(ici=["p", "pg"], d2d=["gu0[2:4/4]", "d0[0:1/2]"])],
            "attn_out": [move(ici=["d0[1:2/2]", "pool_norm", "pool_scale"], d2d=["p", "pg"])],
            "ffn0_up": [move(ici=["gu1[0:3/4]"], d2d=["d0[1:2/2]"])],
            "ffn0_down": [move(ici=["gu1[3:4/4]", "d1[0:1/2]"], d2d=["gu1[0:3/4]"])],
            "pool_fwd": [move(ici=["d1[1:2/2]"], d2d=["gu1[3:4/4]", "d1[0:1/2]"])],
            "ffn1_up": [move(d2d=["d1[1:2/2]"])],
            "ffn1_up_dh": [swap("d1", "gu1")],
            "pool_bwd": [scatter("d1{01}")],
            "pool_in_dh": [scatter("d1{2}")],
            "ffn0_down_bwd": [scatter("gu1{01}")],
            "ffn0_down_dw": [scatter("gu1{2}")],
            "ffn0_up_dw": [share("gu1", "d1")],
            "ffn0_up_dh": [swap("pg", "p", "d0", "gu0")],
            "merge_bwd": [swap("o")],
            "attn_bwd": [scatter("gu0", "d0", "o")],
            "qkv_dw0": [scatter("p", "pg")],
            "qkv_dw1": [share("gu0", "o", "d0"), swap("qkv0")],
            "qkv_dw2": [share("p", "pg"), scatter("qkv0"), swap("qkv1")],
            "qkv_dh0": [share("qkv0"), scatter("qkv1"), swap("qkv2")],
            "qkv_dh1": [share("qkv1"), scatter("qkv2")],
            "qkv_dh2": [share("qkv2")],
        }

    def gather_and_pass(self, name):
        return _gather_and_pass_rider(self.bufs[name]), lambda res: self.bufs.update({name: res[0]})

    def gather(self, ici, d2d):
        names, jobs = [], []
        for over_ici, specs in ((True, ici), (False, d2d)):
            for spec in specs:
                name, _, rows = spec.partition("[")
                if name not in names:
                    names.append(name)
                rng = None
                if name in TENSORS:
                    ab, _, n = (rows[:-1] or "0:1/1").partition("/")
                    rng = (int(ab.split(":")[0]), int(ab.split(":")[1]), int(n))
                jobs.append((names.index(name), over_ici, rng))
        return _gather_rider([self.bufs[n] for n in names], jobs), lambda res: self.bufs.update(zip(names, res))

    def swap(self, names):
        def done(res):
            sums = _add_my_half("chip_sum_" + "_".join(names), [self.parts[n] for n in names], res, self.core_arr)
            self.sums.update(zip(names, sums))
        return _swap_halves_rider([self.parts[n] for n in names]), done

    def scatter(self, specs):
        names = [s.partition("{")[0] for s in specs]
        relations = [tuple(int(ch) for ch in s.partition("{")[2][:-1]) or (0, 1, 2) for s in specs]

        def done(res):
            for n, rel, landed in zip(names, relations, res):
                self.landed[n] = landed
                self.arrived[n] = self.arrived.get(n, ()) + rel
                if len(self.arrived[n]) < 3:
                    continue
                key, layer, col = _block_of(n)
                self.blocks[key] = _owner_sum("owner_sum_" + n, self.sums[n], landed, self.chip_arr, self.core_arr,
                                              out=self.blocks.get(key), layer=layer, col=col)
        rider = _scatter_rider([self.sums[n] for n in names], [self.landed.get(n) for n in names], relations)
        return rider, done

    def share(self, names):
        keys, pieces = [], []
        for n in names:
            key, layer, col = _block_of(n)
            if key not in keys:
                keys.append(key)
            pieces.append((keys.index(key), layer, col))
        return _share_rider([self.blocks[k] for k in keys], pieces), lambda res: self.blocks.update(zip(keys, res))

    def run(self, fn, name, *args, **kw):
        made = [make() for make in self.plan.get(name, [])]
        if not made:
            return fn(name, *args, **kw)
        riders = [r for r, _ in made]
        out = _ride(riders[0] if len(riders) == 1 else _riders(*riders), fn, name, *args, **kw)
        for r, done in made:
            done(r.results)
        return out

    def w(self, name):
        b = self.bufs[name]
        if name in ("o", "p", "d0", "d1"):
            return b.reshape(-1, b.shape[-1])
        if name == "pg":
            b = b.reshape(N_CHIPS, 4, self.pg_rows, POOL_DIM).transpose(1, 0, 2, 3)
            return b.reshape(4, POOL_DIM, POOL_DIM)
        if name in ("pool_norm", "pool_scale"):
            return b.reshape(1, -1)
        return b

    def grad(self, name, value):
        if name == "pg":
            value = value.reshape(4, N_CHIPS, self.pg_rows, POOL_DIM).transpose(1, 0, 2, 3)
            value = value.reshape(N_CHIPS, 4 * self.pg_rows, POOL_DIM).astype(BF16)
        elif name in ("o", "p", "d0", "d1"):
            value = value.reshape(N_CHIPS, value.shape[0] // N_CHIPS, value.shape[1])
        self.parts[name] = value


def kernel(x, attn_norm, w_qkv, w_attn_out, pool_norm, w_pool_in, w_pool_group, pool_scale, ffn_norm, w_ffn_gate_up, w_ffn_down, final_norm, loss_target, m_attn_norm, m_w_qkv, m_w_attn_out, m_pool_norm, m_w_pool_in, m_w_pool_group, m_pool_scale, m_ffn_norm, m_w_ffn_gate_up, m_w_ffn_down, m_final_norm, v_attn_norm, v_w_qkv, v_w_attn_out, v_pool_norm, v_w_pool_in, v_w_pool_group, v_pool_scale, v_ffn_norm, v_w_ffn_gate_up, v_w_ffn_down, v_final_norm):
    D = D_MODEL
    chip = 2 * lax.axis_index("x") + lax.axis_index("y")
    core = lax.axis_index("c")
    pg_rows = w_pool_group.shape[2]

    chip_arr = chip.astype(jnp.int32).reshape(1)
    core_arr = core.astype(jnp.int32).reshape(1)

    pieces = _cast_qkv("cast_qkv", w_qkv, chip_arr)
    bufs = dict(zip(["wq_own"] + ["wq%d" % p for p in range(QKV_PIECES)], pieces))
    shards = [(w_attn_out, 0, BF16), (w_pool_in, 0, BF16), (w_pool_group.reshape(1, 4 * pg_rows, POOL_DIM), 0, BF16),
              (w_ffn_gate_up, 0, BF16), (w_ffn_gate_up, 1, BF16), (w_ffn_down, 0, BF16), (w_ffn_down, 1, BF16),
              (pool_norm[None], 0, F32), (pool_scale[None], 0, F32)]
    bufs.update(zip(TENSORS[1:] + ("pool_norm", "pool_scale"), _cast_many("cast_rest", shards, chip_arr)))

    comm = _MeshComm(bufs, chip_arr, core_arr, pg_rows)
    loss_part, grad_x, small = _local_step(x[0], loss_target[0], attn_norm, ffn_norm, final_norm.reshape(1, D), comm)
    g_w_qkv, g_w_o, g_w_p, g_w_pg, g_w_gu, g_w_d = [comm.blocks[k] for k in ("qkv", "o", "p", "pg", "gu", "d")]

    rows = jnp.concatenate([small["attn"], small["ffn0"], small["ffn1"], small["final"], small["pool"],
                            small["scale"], jnp.pad(loss_part, ((0, 0), (0, D - 1))), jnp.zeros((1, D), F32)], axis=0)
    all_rows = _gather_small("gather_gain_grads", rows)
    tot = _sum_leading("sum_gain_grads", all_rows, F32)
    loss = tot[6, 0]
    g_attn_norm = tot[0:1]
    g_ffn_norm = tot[1:3]
    g_final_norm = tot[3]
    g_pool_norm = lax.dynamic_slice(tot, (4, chip * POOL_DIM), (1, POOL_DIM))
    g_pool_scale = lax.dynamic_slice(tot, (5, chip * POOL_DIM), (1, POOL_DIM))

    def update(name, w, g, m, v):
        shape = w.shape
        cols = shape[-1]
        as2d = lambda a: a.reshape(-1, cols)
        return [a.reshape(shape) for a in _adamw("adamw_" + name, as2d(w), as2d(g), as2d(m), as2d(v))]

    gains = [(attn_norm, g_attn_norm, m_attn_norm, v_attn_norm), (pool_norm, g_pool_norm, m_pool_norm, v_pool_norm),
             (pool_scale, g_pool_scale, m_pool_scale, v_pool_scale), (ffn_norm, g_ffn_norm, m_ffn_norm, v_ffn_norm),
             (final_norm, g_final_norm, m_final_norm, v_final_norm)]
    as_rows = lambda a: a.reshape(-1, a.shape[-1])
    small_res = _adamw_small("adamw_gains", [tuple(as_rows(a) for a in item) for item in gains])
    small_res = [[a.reshape(item[0].shape) for a in r] for r, item in zip(small_res, gains)]
    results = [
        small_res[0],
        update("w_qkv", w_qkv, g_w_qkv, m_w_qkv, v_w_qkv),
        update("w_attn_out", w_attn_out, g_w_o, m_w_attn_out, v_w_attn_out),
        small_res[1],
        update("w_pool_in", w_pool_in, g_w_p, m_w_pool_in, v_w_pool_in),
        update("w_pool_group", w_pool_group, g_w_pg, m_w_pool_group, v_w_pool_group),
        small_res[2],
        small_res[3],
        update("w_ffn_gate_up", w_ffn_gate_up, g_w_gu, m_w_ffn_gate_up, v_w_ffn_gate_up),
        update("w_ffn_down", w_ffn_down, g_w_d, m_w_ffn_down, v_w_ffn_down),
        small_res[4],
    ]
    grads = [r[0] for r in results]
    deltas = [r[1] for r in results]
    new_m = [r[2] for r in results]
    new_v = [r[3] for r in results]
    return (loss, grad_x[None], *grads, *deltas, *new_m, *new_v)
```

```python
import functools

import jax
import jax.numpy as jnp
from jax import lax
from jax.experimental import pallas as pl
from jax.experimental.pallas import tpu as pltpu

F32 = jnp.float32
BF16 = jnp.bfloat16
MESH = pl.DeviceIdType.MESH

D_MODEL = 1024
N_HEADS = 8
HEAD_DIM = 128
Q_BLOCK = 128
ATTN_DILATIONS = (1, 4, 16)
N_GROUPS = 3
D_FF = 2816
N_CHIPS = 4
FF_SHARD = 2 * D_FF // N_CHIPS
QKV_SHARD = 3 * 3 * D_MODEL // N_CHIPS
QKV_TILE = 768
POOL_DIM = 256
POOL_HALO = 16
RMS_EPS = 1e-6
NEG = -1e30
ADAM_LR = 0.001
ADAM_B1 = 0.9
ADAM_B2 = 0.999
ADAM_EPS = 1e-08
ADAM_WD = 0.01
ADAM_STEP = 10
VMEM_LIMIT = 56 * 1024 * 1024

_DN = {
    "nn": (((1,), (0,)), ((), ())),
    "nt": (((1,), (1,)), ((), ())),
    "tn": (((0,), (0,)), ((), ())),
}


class _Rider:
    def __init__(self, operands, out_shapes, aliases, n_copies, copies=None, start=None, finish=None):
        self.operands = list(operands)
        self.out_shapes = list(out_shapes)
        self.aliases = dict(aliases)
        self.n_copies = n_copies
        self.results = None

        def start_copies(ins, outs, send, recv, base=0):
            for cp in copies(ins, outs, send, recv, base):
                cp.start()

        def wait_copies(ins, outs, send, recv, base=0):
            for cp in copies(ins, outs, send, recv, base):
                cp.wait()

        self.start = start or start_copies
        self.finish = finish or wait_copies


def _riders(*riders):
    operands, out_shapes, aliases, offsets = [], [], {}, []
    n = 0
    for r in riders:
        offsets.append((len(operands), len(out_shapes), n))
        aliases.update({len(operands) + i: len(out_shapes) + o for i, o in r.aliases.items()})
        operands += r.operands
        out_shapes += r.out_shapes
        n += r.n_copies

    def each(which):
        def go(ins, outs, send, recv, base=0):
            for r, (i0, o0, s0) in zip(riders, offsets):
                getattr(r, which)(ins[i0:i0 + len(r.operands)], outs[o0:o0 + len(r.out_shapes)], send, recv,
                                  base + s0)
        return go

    both = _Rider(operands, out_shapes, aliases, n, start=each("start"), finish=each("finish"))
    both.parts = (riders, offsets)
    return both


_pending_rider = []


def _ride(rider, fn, *args, **kw):
    _pending_rider.append(rider)
    out = fn(*args, **kw)
    assert not _pending_rider
    if hasattr(rider, "parts"):
        for r, (_, o0, _) in zip(*rider.parts):
            r.results = rider.results[o0:o0 + len(r.out_shapes)]
    return out


def _pcall(body, prefetch=0, **kw):
    def build(body, kw):
        if not prefetch:
            return pl.pallas_call(body, **kw)
        kw = dict(kw)
        grid_spec = pltpu.PrefetchScalarGridSpec(
            num_scalar_prefetch=prefetch, grid=kw.pop("grid"), in_specs=kw.pop("in_specs"),
            out_specs=kw.pop("out_specs"), scratch_shapes=kw.pop("scratch_shapes", []))
        return pl.pallas_call(body, grid_spec=grid_spec, **kw)

    if not _pending_rider:
        return build(body, kw)
    rider = _pending_rider.pop()
    grid = kw.get("grid", ())
    in_specs = list(kw.get("in_specs", []))
    single = not isinstance(kw["out_shape"], (list, tuple))
    out_shape = [kw["out_shape"]] if single else list(kw["out_shape"])
    out_specs = [kw["out_specs"]] if single else list(kw["out_specs"])
    scratch = list(kw.get("scratch_shapes", []))
    n_in, n_out, n_scr = len(in_specs), len(out_shape), len(scratch)
    r_in, r_out = len(rider.operands), len(rider.out_shapes)

    def wrapped(*refs):
        scalars, refs = refs[:prefetch], refs[prefetch:]
        ins, rins = refs[:n_in], refs[n_in:n_in + r_in]
        outs = refs[n_in + r_in:n_in + r_in + n_out]
        routs = refs[n_in + r_in + n_out:n_in + r_in + n_out + r_out]
        scr = refs[n_in + r_in + n_out + r_out:n_in + r_in + n_out + r_out + n_scr]
        send, recv = refs[-2:]
        ids = [pl.program_id(a) for a in range(len(grid))]
        first, last = True, True
        for a, pid in enumerate(ids):
            first = jnp.logical_and(first, pid == 0)
            last = jnp.logical_and(last, pid == grid[a] - 1)
        if grid:
            pl.when(first)(lambda: rider.start(rins, routs, send, recv))
        else:
            rider.start(rins, routs, send, recv)
        body(*scalars, *ins, *outs, *scr)
        if grid:
            pl.when(last)(lambda: rider.finish(rins, routs, send, recv))
        else:
            rider.finish(rins, routs, send, recv)

    any_spec = pl.BlockSpec(memory_space=pl.ANY)
    kw2 = dict(kw)
    kw2.update(
        in_specs=in_specs + [any_spec] * r_in, out_specs=out_specs + [any_spec] * r_out,
        out_shape=out_shape + rider.out_shapes,
        scratch_shapes=scratch + [pltpu.SemaphoreType.DMA((rider.n_copies,)),
                                  pltpu.SemaphoreType.DMA((rider.n_copies,))],
        input_output_aliases={**kw.get("input_output_aliases", {}),
                              **{prefetch + n_in + i: n_out + o for i, o in rider.aliases.items()}})
    if grid:
        kw2["compiler_params"] = _params(("arbitrary",) * len(grid))
    call = build(wrapped, kw2)

    def run(*operands):
        res = call(*operands, *rider.operands)
        rider.results = list(res[n_out:])
        return res[0] if single else list(res[:n_out])

    return run


def _params(sem):
    return pltpu.CompilerParams(dimension_semantics=sem, vmem_limit_bytes=VMEM_LIMIT)


def _dot(a, b, mode):
    return lax.dot_general(a, b, _DN[mode], preferred_element_type=F32)


def _mm(name, mode, grid, a, a_spec, b, b_spec, out_shape, o_spec, acc_shape):
    nk = grid[-1]

    def body(a_ref, b_ref, o_ref, *acc):
        part = _dot(a_ref[...].astype(BF16), b_ref[...].astype(BF16), mode)
        if nk == 1:
            o_ref[...] = part.astype(o_ref.dtype)
            return
        acc_ref, = acc
        k = pl.program_id(len(grid) - 1)

        @pl.when(k == 0)
        def _():
            acc_ref[...] = part

        @pl.when(k > 0)
        def _():
            acc_ref[...] += part

        @pl.when(k == nk - 1)
        def _():
            o_ref[...] = acc_ref[...].astype(o_ref.dtype)

    scratch = [] if nk == 1 else [pltpu.VMEM(acc_shape, F32)]
    sem = ("parallel",) * (len(grid) - 1) + ("arbitrary",)
    return _pcall(body, name=name, grid=grid, in_specs=[a_spec, b_spec], out_specs=o_spec, out_shape=out_shape,
                  scratch_shapes=scratch, compiler_params=_params(sem))(a, b)


def _mm_tn(name, a, b, tm, tk):
    T, M = a.shape
    N = b.shape[1]
    return _mm(name, "tn", (M // tm, T // tk), a, pl.BlockSpec((tk, tm), lambda i, k: (k, i)),
               b, pl.BlockSpec((tk, N), lambda i, k: (k, 0)),
               jax.ShapeDtypeStruct((M, N), BF16), pl.BlockSpec((tm, N), lambda i, k: (i, 0)), (tm, N))


def _norm_bwd_rows(xf, gain, dh, dres):
    r = lax.rsqrt(jnp.mean(xf * xf, axis=-1, keepdims=True) + RMS_EPS)
    xh = xf * r
    t = dh * gain
    dx = dres + r * (t - xh * jnp.mean(t * xh, axis=-1, keepdims=True))
    return dx, jnp.sum(dh * xh, axis=0, keepdims=True)


def _accumulate(ref, value, first):
    @pl.when(first)
    def _():
        ref[...] = value

    @pl.when(jnp.logical_not(first))
    def _():
        ref[...] += value


def _rms_bwd_folded(name, x, g, dhf, dres, tb=512):
    S, D = x.shape

    def body(x_ref, g_ref, d0_ref, d1_ref, d2_ref, dres_ref, dx_ref, dg_ref, dh_ref):
        chunks = _lane_chunks(D)
        for c, cols in enumerate(chunks):
            dh_ref[c] = d0_ref[0, :, cols].astype(F32)
        for dil, ref in ((ATTN_DILATIONS[1], d1_ref), (ATTN_DILATIONS[2], d2_ref)):
            n = tb // dil
            for k in range(dil):
                for c, cols in enumerate(chunks):
                    dh_ref[c, _strided_rows(k, n, dil), :] += ref[k, :, cols].astype(F32)
        dh = jnp.concatenate([dh_ref[c] for c in range(len(chunks))], axis=1)
        dx, dg = _norm_bwd_rows(x_ref[...], g_ref[...], dh, dres_ref[...])
        dx_ref[...] = dx
        _accumulate(dg_ref, dg, pl.program_id(0) == 0)

    views, specs = _folded_views(dhf, tb)
    row = pl.BlockSpec((tb, D), lambda i: (i, 0))
    vec = pl.BlockSpec((1, D), lambda i: (0, 0))
    return _pcall(body, name=name, grid=(S // tb,), in_specs=[row, vec] + specs + [row], out_specs=[row, vec],
                  out_shape=[jax.ShapeDtypeStruct((S, D), F32), jax.ShapeDtypeStruct((1, D), F32)],
                  scratch_shapes=[pltpu.VMEM((D // LANES, tb, LANES), F32)],
                  compiler_params=_params(("arbitrary",)))(x, g, *views, dres)


def _proj_res_norm(name, a, w, res, gain, tm=512):
    M, K = a.shape
    D = w.shape[1]

    def body(a_ref, w_ref, res_ref, g_ref, x_ref, h_ref):
        xf = res_ref[...] + _dot(a_ref[...], w_ref[...], "nn")
        x_ref[...] = xf
        r = lax.rsqrt(jnp.mean(xf * xf, axis=-1, keepdims=True) + RMS_EPS)
        h_ref[...] = ((xf * r) * g_ref[...]).astype(h_ref.dtype)

    row = pl.BlockSpec((tm, D), lambda i: (i, 0))
    return _pcall(body, name=name, grid=(M // tm,),
                  in_specs=[pl.BlockSpec((tm, K), lambda i: (i, 0)), pl.BlockSpec((K, D), lambda i: (0, 0)), row,
                            pl.BlockSpec((1, D), lambda i: (0, 0))],
                  out_specs=[row, row],
                  out_shape=[jax.ShapeDtypeStruct((M, D), F32), jax.ShapeDtypeStruct((M, D), BF16)],
                  compiler_params=_params(("parallel",)))(a, w, res, gain)


def _proj_res_loss(name, a, w, res, gain, tgt, tm=512):
    M, K = a.shape
    D = w.shape[1]

    def body(a_ref, w_ref, res_ref, g_ref, t_ref, loss_ref, dx_ref, dxb_ref, dg_ref):
        first = pl.program_id(0) == 0
        xf = res_ref[...] + _dot(a_ref[...], w_ref[...], "nn")
        r = lax.rsqrt(jnp.mean(xf * xf, axis=-1, keepdims=True) + RMS_EPS)
        xh = xf * r
        gv = g_ref[...]
        e = xh * gv - t_ref[...]
        lp = 0.5 * jnp.sum(jnp.mean(e * e, axis=-1, keepdims=True), axis=0, keepdims=True)
        dy = e * (1.0 / D)
        t = dy * gv
        dx = r * (t - xh * jnp.mean(t * xh, axis=-1, keepdims=True))
        dx_ref[...] = dx
        dxb_ref[...] = dx.astype(dxb_ref.dtype)
        _accumulate(dg_ref, jnp.sum(dy * xh, axis=0, keepdims=True), first)
        _accumulate(loss_ref, lp, first)

    row = pl.BlockSpec((tm, D), lambda i: (i, 0))
    vec = pl.BlockSpec((1, D), lambda i: (0, 0))
    return _pcall(body, name=name, grid=(M // tm,),
                  in_specs=[pl.BlockSpec((tm, K), lambda i: (i, 0)), pl.BlockSpec((K, D), lambda i: (0, 0)), row,
                            vec, row],
                  out_specs=[pl.BlockSpec((1, 1), lambda i: (0, 0)), row, row, vec],
                  out_shape=[jax.ShapeDtypeStruct((1, 1), F32), jax.ShapeDtypeStruct((M, D), F32),
                             jax.ShapeDtypeStruct((M, D), BF16), jax.ShapeDtypeStruct((1, D), F32)],
                  compiler_params=_params(("arbitrary",)))(a, w, res, gain, tgt)


def _dh_norm_bwd(name, d, w, x, gain, dres, tm=512):
    M, N = d.shape
    D = w.shape[0]

    def body(d_ref, w_ref, x_ref, g_ref, dres_ref, dx_ref, dxb_ref, dg_ref):
        dh = _dot(d_ref[...].astype(BF16), w_ref[...], "nt")
        dx, dg = _norm_bwd_rows(x_ref[...], g_ref[...], dh, dres_ref[...])
        dx_ref[...] = dx
        dxb_ref[...] = dx.astype(dxb_ref.dtype)
        _accumulate(dg_ref, dg, pl.program_id(0) == 0)

    row = pl.BlockSpec((tm, D), lambda i: (i, 0))
    vec = pl.BlockSpec((1, D), lambda i: (0, 0))
    return _pcall(body, name=name, grid=(M // tm,),
                  in_specs=[pl.BlockSpec((tm, N), lambda i: (i, 0)), pl.BlockSpec((D, N), lambda i: (0, 0)), row, vec,
                            row],
                  out_specs=[row, row, vec],
                  out_shape=[jax.ShapeDtypeStruct((M, D), F32), jax.ShapeDtypeStruct((M, D), BF16),
                             jax.ShapeDtypeStruct((1, D), F32)],
                  compiler_params=_params(("arbitrary",)))(d, w, x, gain, dres)


def _sigmoid(v):
    return 0.5 * jnp.tanh(0.5 * v) + 0.5


def _ffn_up(name, h, w_gu, tm=1024):
    S, D = h.shape
    W = FF_SHARD

    def body(h_ref, wg_ref, wu_ref, gu_ref, act_ref):
        hv = h_ref[...]
        gate = _dot(hv, wg_ref[...], "nn")
        up = _dot(hv, wu_ref[...], "nn")
        gu_ref[0] = gate.astype(gu_ref.dtype)
        gu_ref[1] = up.astype(gu_ref.dtype)
        sig = _sigmoid(gate)
        act_ref[...] = ((gate * sig) * up).astype(act_ref.dtype)

    return _pcall(
        body, name=name, grid=(2, S // tm),
        in_specs=[pl.BlockSpec((tm, D), lambda j, i: (i, 0)),
                  pl.BlockSpec((None, D, W), lambda j, i: (j, 0, 0)),
                  pl.BlockSpec((None, D, W), lambda j, i: (j + 2, 0, 0))],
        out_specs=[pl.BlockSpec((2, tm, W), lambda j, i: (0, i, j)), pl.BlockSpec((tm, W), lambda j, i: (i, j))],
        out_shape=[jax.ShapeDtypeStruct((2, S, D_FF), BF16), jax.ShapeDtypeStruct((S, D_FF), BF16)],
        compiler_params=_params(("parallel", "parallel")))(h, w_gu, w_gu)


def _ffn_down_bwd(name, dx, w_down, gu, tm=1024):
    S, D = dx.shape
    W = FF_SHARD

    def body(dx_ref, w_ref, gu_ref, dgu_ref):
        dact = _dot(dx_ref[...].astype(BF16), w_ref[...], "nt")
        gate = gu_ref[0].astype(F32)
        up = gu_ref[1].astype(F32)
        sig = _sigmoid(gate)
        silu = gate * sig
        dgu_ref[0] = (dact * up * (sig * (1.0 + gate * (1.0 - sig)))).astype(dgu_ref.dtype)
        dgu_ref[1] = (dact * silu).astype(dgu_ref.dtype)

    blk = pl.BlockSpec((2, tm, W), lambda j, i: (0, i, j))
    return _pcall(
        body, name=name, grid=(2, S // tm),
        in_specs=[pl.BlockSpec((tm, D), lambda j, i: (i, 0)), pl.BlockSpec((W, D), lambda j, i: (j, 0)), blk],
        out_specs=blk, out_shape=jax.ShapeDtypeStruct((2, S, D_FF), BF16),
        compiler_params=_params(("parallel", "parallel")))(dx, w_down, gu)


def _ffn_dw_up(name, h, dgu, tk=2048):
    S, D = h.shape
    W = FF_SHARD
    tk = min(tk, S)
    return _mm(name, "tn", (N_CHIPS, S // tk), h, pl.BlockSpec((tk, D), lambda s, k: (k, 0)),
               dgu, pl.BlockSpec((None, tk, W), lambda s, k: (s // 2, k, s % 2)),
               jax.ShapeDtypeStruct((N_CHIPS, D, W), BF16), pl.BlockSpec((None, D, W), lambda s, k: (s, 0, 0)),
               (D, W))


def _ffn_dh(name, dgu, w_gu, x, gain, dres, tm=512):
    S = dgu.shape[1]
    W = FF_SHARD

    def body(a_ref, w_hbm, x_ref, g_ref, dres_ref, dx_ref, dg_ref, w_ref, acat_ref, sems):
        first = pl.program_id(0) == 0

        @pl.when(first)
        def _():
            copies = [pltpu.make_async_copy(w_hbm.at[s], w_ref.at[:, pl.ds(s * W, W)], sems.at[s])
                      for s in range(N_CHIPS)]
            for cp in copies:
                cp.start()
            for cp in copies:
                cp.wait()

        acat_ref[:, :D_FF] = a_ref[0]
        acat_ref[:, D_FF:] = a_ref[1]
        dh = _dot(acat_ref[...], w_ref[...], "nt")
        dx, dg = _norm_bwd_rows(x_ref[...], g_ref[...], dh, dres_ref[...])
        dx_ref[...] = dx
        _accumulate(dg_ref, dg, first)

    row = pl.BlockSpec((tm, D_MODEL), lambda i: (i, 0))
    vec = pl.BlockSpec((1, D_MODEL), lambda i: (0, 0))
    return _pcall(body, name=name, grid=(S // tm,),
                  in_specs=[pl.BlockSpec((2, tm, D_FF), lambda i: (0, i, 0)), pl.BlockSpec(memory_space=pl.ANY),
                            row, vec, row],
                  out_specs=[row, vec],
                  out_shape=[jax.ShapeDtypeStruct((S, D_MODEL), F32), jax.ShapeDtypeStruct((1, D_MODEL), F32)],
                  scratch_shapes=[pltpu.VMEM((D_MODEL, 2 * D_FF), BF16), pltpu.VMEM((tm, 2 * D_FF), BF16),
                                  pltpu.SemaphoreType.DMA((N_CHIPS,))],
                  compiler_params=_params(("arbitrary",)))(dgu, w_gu, x, gain, dres)


FOLD_TOKENS = 1024
LANES = 128


def _lane_chunks(width):
    return [slice(c * LANES, (c + 1) * LANES) for c in range(width // LANES)]


def _strided_rows(r, n, dil):
    return pl.ds(r, n) if dil == 1 else pl.ds(r, n, stride=dil)


def _norm_fold(name, x, gain):
    S, D = x.shape
    chunks = _lane_chunks(D)

    def body(x_ref, g_ref, hf_hbm, xc_ref, hs_ref, sems):
        i = pl.program_id(0)
        xf = x_ref[...]
        inv = lax.rsqrt(jnp.mean(xf * xf, axis=-1, keepdims=True) + RMS_EPS)
        h = (xf * inv) * g_ref[...]
        hs_ref[0] = h.astype(BF16)
        for c, cols in enumerate(chunks):
            xc_ref[c] = h[:, cols]
        copies = []
        for g, dil in enumerate(ATTN_DILATIONS):
            n = FOLD_TOKENS // dil
            for r in range(dil):
                if dil > 1:
                    for c, cols in enumerate(chunks):
                        hs_ref[g, r * n:(r + 1) * n, cols] = xc_ref[c, _strided_rows(r, n, dil), :].astype(BF16)
                cp = pltpu.make_async_copy(hs_ref.at[g, pl.ds(r * n, n)],
                                           hf_hbm.at[g, pl.ds(r * (S // dil) + i * n, n)], sems.at[len(copies)])
                cp.start()
                copies.append(cp)
        for cp in copies:
            cp.wait()

    return _pcall(body, name=name, grid=(S // FOLD_TOKENS,),
                  in_specs=[pl.BlockSpec((FOLD_TOKENS, D), lambda i: (i, 0)), pl.BlockSpec((1, D), lambda i: (0, 0))],
                  out_specs=pl.BlockSpec(memory_space=pl.ANY),
                  out_shape=jax.ShapeDtypeStruct((N_GROUPS, S, D), BF16),
                  scratch_shapes=[pltpu.VMEM((D // LANES, FOLD_TOKENS, LANES), F32),
                                  pltpu.VMEM((N_GROUPS, FOLD_TOKENS, D), BF16),
                                  pltpu.SemaphoreType.DMA((sum(ATTN_DILATIONS),))],
                  compiler_params=_params(("arbitrary",)))(x, gain)


def _qkv_tiles(name, piece, hf, w, qkv, chip, tm=1024):
    S = hf.shape[1]
    per_group = 3 * D_MODEL // QKV_TILE

    def tile(q, c_ref):
        if piece is None:
            return QKV_PIECES * c_ref[0] + q
        return QKV_PIECES * ((c_ref[0] + 1 + q) % N_CHIPS) + piece

    def body(*refs):
        a_ref, w_ref, o_ref = refs[1], refs[2], refs[-1]
        o_ref[...] = _dot(a_ref[...], w_ref[...], "nn").astype(o_ref.dtype)

    if piece is None:
        w_spec = pl.BlockSpec((D_MODEL, QKV_TILE), lambda q, i, c_ref: (0, q))
    else:
        w_spec = pl.BlockSpec((None, D_MODEL, QKV_TILE), lambda q, i, c_ref: ((c_ref[0] + 1 + q) % N_CHIPS, 0, 0))
    in_specs = [pl.BlockSpec((None, tm, D_MODEL), lambda q, i, c_ref: (tile(q, c_ref) // per_group, i, 0)), w_spec]
    operands = [chip, hf, w]
    aliases = {}
    if qkv is not None:
        in_specs.append(pl.BlockSpec(memory_space=pl.ANY))
        operands.append(qkv)
        aliases = {3: 0}
    return _pcall(
        body, prefetch=1, name=name, grid=(N_CHIPS - 1, S // tm), in_specs=in_specs,
        out_specs=pl.BlockSpec((None, tm, QKV_TILE),
                               lambda q, i, c_ref: (tile(q, c_ref) // per_group, i, tile(q, c_ref) % per_group)),
        out_shape=jax.ShapeDtypeStruct((N_GROUPS, S, 3 * D_MODEL), BF16), input_output_aliases=aliases,
        compiler_params=_params(("arbitrary", "arbitrary")))(*operands)


QKV_PIECES = QKV_SHARD // QKV_TILE


def _qkv_dw(name, p, hf, dqkv):
    S = hf.shape[1]
    per_group = 3 * D_MODEL // QKV_TILE
    tile = lambda s: QKV_PIECES * s + p
    return _mm(name, "tn", (N_CHIPS, 1),
               hf, pl.BlockSpec((None, S, D_MODEL), lambda s, k: (tile(s) // per_group, 0, 0)),
               dqkv, pl.BlockSpec((None, S, QKV_TILE), lambda s, k: (tile(s) // per_group, 0, tile(s) % per_group)),
               jax.ShapeDtypeStruct((N_CHIPS, D_MODEL, QKV_TILE), BF16),
               pl.BlockSpec((None, D_MODEL, QKV_TILE), lambda s, k: (s, 0, 0)), None)


def _qkv_dh(name, g, dqkv, w_pieces, dhf, tm=1024):
    S = dqkv.shape[1]
    per_group = 3 * D_MODEL // QKV_TILE

    def body(*refs):
        a_ref, w_hbm = refs[0], refs[1:1 + QKV_PIECES]
        o_ref, w_ref, sems = refs[-3:]

        @pl.when(pl.program_id(0) == 0)
        def _():
            copies = []
            for j in range(per_group):
                t = per_group * g + j
                copies.append(pltpu.make_async_copy(w_hbm[t % QKV_PIECES].at[t // QKV_PIECES],
                                                    w_ref.at[:, pl.ds(j * QKV_TILE, QKV_TILE)], sems.at[j]))
            for cp in copies:
                cp.start()
            for cp in copies:
                cp.wait()

        o_ref[...] = _dot(a_ref[...], w_ref[...], "nt").astype(o_ref.dtype)

    any_spec = pl.BlockSpec(memory_space=pl.ANY)
    in_specs = [pl.BlockSpec((None, tm, 3 * D_MODEL), lambda i: (g, i, 0))] + [any_spec] * QKV_PIECES
    operands = [dqkv] + list(w_pieces)
    aliases = {}
    if dhf is not None:
        in_specs.append(any_spec)
        operands.append(dhf)
        aliases = {1 + QKV_PIECES: 0}
    return _pcall(body, name=name, grid=(S // tm,), in_specs=in_specs,
                  out_specs=pl.BlockSpec((None, tm, D_MODEL), lambda i: (g, i, 0)),
                  out_shape=jax.ShapeDtypeStruct((N_GROUPS, S, D_MODEL), BF16),
                  scratch_shapes=[pltpu.VMEM((D_MODEL, 3 * D_MODEL), BF16), pltpu.SemaphoreType.DMA((per_group,))],
                  input_output_aliases=aliases, compiler_params=_params(("arbitrary",)))(*operands)


def _alibi_coef(g, h):
    vals = [-(2.0 ** (-8.0 * (gg * N_HEADS + h + 1) / (N_GROUPS * N_HEADS))) * ATTN_DILATIONS[gg]
            for gg in range(N_GROUPS)]
    return jnp.where(g == 0, vals[0], jnp.where(g == 1, vals[1], vals[2])).astype(F32)


def _blocks_per_segment(g, S):
    return jnp.right_shift(S // Q_BLOCK, 2 * g)


def _band_bias(pen):
    row = lax.broadcasted_iota(jnp.int32, (Q_BLOCK, 2 * Q_BLOCK), 0)
    col = lax.broadcasted_iota(jnp.int32, (Q_BLOCK, 2 * Q_BLOCK), 1)
    dist = Q_BLOCK + row - col
    valid = jnp.logical_and(dist >= 0, dist <= Q_BLOCK)
    base = jnp.where(valid, jnp.where(col < Q_BLOCK, pen, 0.0), NEG).astype(F32)
    return base, dist.astype(F32)


def _skewed(n_units, stages):
    state = {}
    for step in range(n_units + len(stages) - 1):
        for s, stage in enumerate(stages):
            u = step - s
            if 0 <= u < n_units:
                state[u] = stage(u, state.get(u))
                if s == len(stages) - 1:
                    del state[u]


def _attn_fwd(name, qkv, tq=1024):
    S = qkv.shape[1]
    nqb = tq // Q_BLOCK
    scale = HEAD_DIM ** -0.5

    def body(q_ref, k_ref, kp_ref, v_ref, vp_ref, o_ref, lse_ref, kf_ref, vf_ref):
        g = pl.program_id(0)
        i = pl.program_id(1)
        nb = _blocks_per_segment(g, S)
        kf_ref[:Q_BLOCK] = kp_ref[...]
        kf_ref[Q_BLOCK:] = k_ref[...]
        vf_ref[:Q_BLOCK] = vp_ref[...]
        vf_ref[Q_BLOCK:] = v_ref[...]
        coefs = [_alibi_coef(g, h) for h in range(N_HEADS)]
        units = [(b, h) for b in range(nqb) for h in range(N_HEADS)]
        bias = {}

        def scores(u, _):
            b, h = units[u]
            if b not in bias:
                pen = jnp.where((i * nqb + b) % nb != 0, 0.0, NEG).astype(F32)
                bias.clear()
                bias[b] = _band_bias(pen)
            base, dist = bias[b]
            cols = slice(h * HEAD_DIM, (h + 1) * HEAD_DIM)
            q = q_ref[b * Q_BLOCK:(b + 1) * Q_BLOCK, cols]
            kk = kf_ref[b * Q_BLOCK:(b + 2) * Q_BLOCK, cols]
            return _dot(q, kk, "nt") * scale + (coefs[h] * dist + base)

        def softmax(u, s):
            m = jnp.max(s, axis=-1, keepdims=True)
            p = jnp.exp(s - m)
            den = jnp.sum(p, axis=-1, keepdims=True)
            return (p * (1.0 / den)).astype(BF16), m + jnp.log(den)

        def output(u, st):
            b, h = units[u]
            pn, lse = st
            cols = slice(h * HEAD_DIM, (h + 1) * HEAD_DIM)
            rows = slice(b * Q_BLOCK, (b + 1) * Q_BLOCK)
            o_ref[rows, cols] = _dot(pn, vf_ref[b * Q_BLOCK:(b + 2) * Q_BLOCK, cols], "nn").astype(o_ref.dtype)
            lse_ref[rows, h:h + 1] = lse

        _skewed(len(units), [scores, softmax, output])

    main = lambda c: pl.BlockSpec((None, tq, D_MODEL), lambda g, i: (g, i, c))
    prev = lambda c: pl.BlockSpec((None, Q_BLOCK, D_MODEL), lambda g, i: (g, jnp.maximum(i * nqb - 1, 0), c))
    return _pcall(
        body, name=name, grid=(N_GROUPS, S // tq),
        in_specs=[main(0), main(1), prev(1), main(2), prev(2)],
        out_specs=[pl.BlockSpec((None, tq, D_MODEL), lambda g, i: (g, i, 0)),
                   pl.BlockSpec((None, tq, N_HEADS), lambda g, i: (g, i, 0))],
        out_shape=[jax.ShapeDtypeStruct((N_GROUPS, S, D_MODEL), BF16),
                   jax.ShapeDtypeStruct((N_GROUPS, S, N_HEADS), F32)],
        scratch_shapes=[pltpu.VMEM((tq + Q_BLOCK, D_MODEL), BF16), pltpu.VMEM((tq + Q_BLOCK, D_MODEL), BF16)],
        compiler_params=_params(("parallel", "parallel")))(qkv, qkv, qkv, qkv, qkv)


def _attn_bwd(name, qkv, do, lse, dl, tq=1024):
    S = qkv.shape[1]
    nqb = tq // Q_BLOCK
    n_blocks = S // Q_BLOCK
    scale = HEAD_DIM ** -0.5
    KEYS = 2 * Q_BLOCK

    def body(q_ref, k_ref, v_ref, kp_ref, vp_ref, qn_ref, do_ref, don_ref, lse_ref, lsen_ref, dl_ref, dln_ref,
             out_ref, kf_ref, vf_ref, dk_ref, dv_ref):
        g = pl.program_id(0)
        i = pl.program_id(1)
        nb = _blocks_per_segment(g, S)
        kf_ref[:Q_BLOCK] = kp_ref[...]
        kf_ref[Q_BLOCK:] = k_ref[...]
        vf_ref[:Q_BLOCK] = vp_ref[...]
        vf_ref[Q_BLOCK:] = v_ref[...]
        coefs = [_alibi_coef(g, h) for h in range(N_HEADS)]
        units = [(h, b) for h in range(N_HEADS) for b in range(nqb + 1)]
        bias = {}

        def band(b):
            if b not in bias:
                blk = i * nqb + b
                ok = blk % nb != 0
                if b == nqb:
                    ok = jnp.logical_and(ok, blk < n_blocks)
                bias[b] = _band_bias(jnp.where(ok, 0.0, NEG).astype(F32))
            return bias[b]

        def operands(h, b):
            cols = slice(h * HEAD_DIM, (h + 1) * HEAD_DIM)
            if b == nqb:
                keys = slice(tq, tq + Q_BLOCK)
                return (qn_ref[:, cols], don_ref[:, cols], lsen_ref[:, h:h + 1], dln_ref[:, h:h + 1],
                        kf_ref[keys, cols], vf_ref[keys, cols])
            rows = slice(b * Q_BLOCK, (b + 1) * Q_BLOCK)
            keys = slice(b * Q_BLOCK, b * Q_BLOCK + KEYS)
            return (q_ref[rows, cols], do_ref[rows, cols], lse_ref[rows, h:h + 1], dl_ref[rows, h:h + 1],
                    kf_ref[keys, cols], vf_ref[keys, cols])

        def probs(u, _):
            h, b = units[u]
            q, do_b, lse_b, dl_b, kk, vv = operands(h, b)
            base, dist = band(b)
            if b == nqb:
                base, dist = base[:, :Q_BLOCK], dist[:, :Q_BLOCK]
            p = jnp.exp(_dot(q, kk, "nt") * scale + (coefs[h] * dist + base) - lse_b)
            return p, _dot(do_b, vv, "nt")

        def dscores(u, st):
            h, b = units[u]
            p, dp = st
            dl_b = operands(h, b)[3]
            return ((p * (dp - dl_b)) * scale).astype(BF16), p.astype(BF16)

        def grads(u, st):
            h, b = units[u]
            ds, pb = st
            q, do_b, _, _, kk, _ = operands(h, b)
            cols = slice(h * HEAD_DIM, (h + 1) * HEAD_DIM)
            if b < nqb:
                out_ref[b * Q_BLOCK:(b + 1) * Q_BLOCK, cols] = _dot(ds, kk, "nn").astype(out_ref.dtype)
            if b == 0:
                dk_ref[:Q_BLOCK] = _dot(ds[:, Q_BLOCK:], q, "tn")
                dv_ref[:Q_BLOCK] = _dot(pb[:, Q_BLOCK:], do_b, "tn")
                return None
            dk = _dot(ds, q, "tn")
            dv = _dot(pb, do_b, "tn")
            before = slice((b - 1) * Q_BLOCK, b * Q_BLOCK)
            dk_ref[before] += dk[:Q_BLOCK]
            dv_ref[before] += dv[:Q_BLOCK]
            if b < nqb:
                own = slice(b * Q_BLOCK, (b + 1) * Q_BLOCK)
                dk_ref[own] = dk[Q_BLOCK:]
                dv_ref[own] = dv[Q_BLOCK:]
            else:
                out_ref[:, D_MODEL + h * HEAD_DIM:D_MODEL + (h + 1) * HEAD_DIM] = dk_ref[...].astype(out_ref.dtype)
                out_ref[:, 2 * D_MODEL + h * HEAD_DIM:2 * D_MODEL + (h + 1) * HEAD_DIM] = (
                    dv_ref[...].astype(out_ref.dtype))
            return None

        _skewed(len(units), [probs, dscores, grads])

    nxt_blk = lambda i: jnp.minimum((i + 1) * nqb, n_blocks - 1)
    main = lambda c: pl.BlockSpec((None, tq, D_MODEL), lambda g, i: (g, i, c))
    prev = lambda c: pl.BlockSpec((None, Q_BLOCK, D_MODEL), lambda g, i: (g, jnp.maximum(i * nqb - 1, 0), c))
    row = pl.BlockSpec((None, tq, D_MODEL), lambda g, i: (g, i, 0))
    row_n = pl.BlockSpec((None, Q_BLOCK, D_MODEL), lambda g, i: (g, nxt_blk(i), 0))
    col = pl.BlockSpec((None, tq, N_HEADS), lambda g, i: (g, i, 0))
    col_n = pl.BlockSpec((None, Q_BLOCK, N_HEADS), lambda g, i: (g, nxt_blk(i), 0))
    return _pcall(
        body, name=name, grid=(N_GROUPS, S // tq),
        in_specs=[main(0), main(1), main(2), prev(1), prev(2), row_n, row, row_n, col, col_n, col, col_n],
        out_specs=pl.BlockSpec((None, tq, 3 * D_MODEL), lambda g, i: (g, i, 0)),
        out_shape=jax.ShapeDtypeStruct((N_GROUPS, S, 3 * D_MODEL), BF16),
        scratch_shapes=[pltpu.VMEM((tq + Q_BLOCK, D_MODEL), BF16), pltpu.VMEM((tq + Q_BLOCK, D_MODEL), BF16),
                        pltpu.VMEM((tq, HEAD_DIM), F32), pltpu.VMEM((tq, HEAD_DIM), F32)],
        compiler_params=_params(("parallel", "parallel")))(qkv, qkv, qkv, qkv, qkv, qkv, do, do, lse, lse, dl, dl)


def _group_weights(lse_ref):
    l0, l1, l2 = lse_ref[0], lse_ref[1], lse_ref[2]
    m = jnp.maximum(jnp.maximum(l0, l1), l2)
    e = [jnp.exp(l0 - m), jnp.exp(l1 - m), jnp.exp(l2 - m)]
    den = e[0] + e[1] + e[2]
    return [ei / den for ei in e]


MERGE_TOKENS = 512


def _folded_views(a, tb):
    _, S, C = a.shape
    views, specs = [], []
    for g, dil in enumerate(ATTN_DILATIONS):
        views.append(a.reshape(N_GROUPS * dil, S // dil, C))
        specs.append(pl.BlockSpec((dil, tb // dil, C), lambda i, g=g: (g, i, 0)))
    return views, specs


def _unfold_into(dst_ref, folded_refs):
    for g, (dil, ref) in enumerate(zip(ATTN_DILATIONS, folded_refs)):
        n = ref.shape[1]
        for r in range(dil):
            for c, cols in enumerate(_lane_chunks(ref.shape[2])):
                dst_ref[g, c, _strided_rows(r, n, dil), :] = ref[r, :, cols].astype(F32)


def _merge_fwd(name, o, lse, tb=MERGE_TOKENS):
    S = o.shape[1]

    def body(o0_ref, o1_ref, o2_ref, lse_ref, out_ref, o_ref):
        _unfold_into(o_ref, (o0_ref, o1_ref, o2_ref))
        w = _group_weights(lse_ref)
        for h in range(N_HEADS):
            cols = slice(h * HEAD_DIM, (h + 1) * HEAD_DIM)
            acc = w[0][:, h:h + 1] * o_ref[0, h]
            for gg in range(1, N_GROUPS):
                acc = acc + w[gg][:, h:h + 1] * o_ref[gg, h]
            out_ref[:, cols] = acc.astype(out_ref.dtype)

    views, specs = _folded_views(o, tb)
    return _pcall(body, name=name, grid=(S // tb,),
                  in_specs=specs + [pl.BlockSpec((N_GROUPS, tb, N_HEADS), lambda i: (0, i, 0))],
                  out_specs=pl.BlockSpec((tb, D_MODEL), lambda i: (i, 0)),
                  out_shape=jax.ShapeDtypeStruct((S, D_MODEL), BF16),
                  scratch_shapes=[pltpu.VMEM((N_GROUPS, N_HEADS, tb, HEAD_DIM), F32)],
                  compiler_params=_params(("parallel",)))(*views, lse)


def _merge_bwd(name, dx, w_out, o, lse, tb=MERGE_TOKENS):
    S = o.shape[1]
    n_chunks = sum(ATTN_DILATIONS)

    def body(dx_ref, w_ref, o0_ref, o1_ref, o2_ref, lse_ref, do_hbm, dl_ref, o_ref, dt_ref, df_ref, d_ref, sems):
        i = pl.program_id(0)
        d_ref[...] = _dot(dx_ref[...].astype(BF16), w_ref[...], "nt")
        _unfold_into(o_ref, (o0_ref, o1_ref, o2_ref))
        w = _group_weights(lse_ref)
        for h in range(N_HEADS):
            cols = slice(h * HEAD_DIM, (h + 1) * HEAD_DIM)
            dv = d_ref[:, cols]
            merged = w[0][:, h:h + 1] * o_ref[0, h]
            for gg in range(1, N_GROUPS):
                merged = merged + w[gg][:, h:h + 1] * o_ref[gg, h]
            dsum = jnp.sum(dv * merged, axis=-1, keepdims=True)
            for gg in range(N_GROUPS):
                wg = w[gg][:, h:h + 1]
                dt_ref[gg, h] = wg * dv
                dl_ref[gg, :, h:h + 1] = wg * dsum
        copies = []
        for gg, dil in enumerate(ATTN_DILATIONS):
            n = tb // dil
            for r in range(dil):
                for h, cols in enumerate(_lane_chunks(D_MODEL)):
                    df_ref[gg, r * n:(r + 1) * n, cols] = (
                        dt_ref[gg, h, _strided_rows(r, n, dil), :].astype(df_ref.dtype))
                cp = pltpu.make_async_copy(df_ref.at[gg, pl.ds(r * n, n)],
                                           do_hbm.at[gg, pl.ds(r * (S // dil) + i * n, n)], sems.at[len(copies)])
                cp.start()
                copies.append(cp)
        for cp in copies:
            cp.wait()

    views, specs = _folded_views(o, tb)
    small = pl.BlockSpec((N_GROUPS, tb, N_HEADS), lambda i: (0, i, 0))
    return _pcall(body, name=name, grid=(S // tb,),
                  in_specs=[pl.BlockSpec((tb, D_MODEL), lambda i: (i, 0)),
                            pl.BlockSpec((D_MODEL, D_MODEL), lambda i: (0, 0))] + specs + [small],
                  out_specs=[pl.BlockSpec(memory_space=pl.ANY), small],
                  out_shape=[jax.ShapeDtypeStruct((N_GROUPS, S, D_MODEL), BF16),
                             jax.ShapeDtypeStruct((N_GROUPS, S, N_HEADS), F32)],
                  scratch_shapes=[pltpu.VMEM((N_GROUPS, N_HEADS, tb, HEAD_DIM), F32),
                                  pltpu.VMEM((N_GROUPS, N_HEADS, tb, HEAD_DIM), F32),
                                  pltpu.VMEM((N_GROUPS, tb, D_MODEL), BF16), pltpu.VMEM((tb, D_MODEL), F32),
                                  pltpu.SemaphoreType.DMA((n_chunks,))],
                  compiler_params=_params(("arbitrary",)))(dx, w_out, *views, lse)


def _shift_rows(a, k):
    return pltpu.roll(a, k % a.shape[0], axis=0)


def _pool_level(g, levels):
    return jnp.where(g == 0, levels[0], jnp.where(g == 1, levels[1], jnp.where(g == 2, levels[2], levels[3])))


def _pool_fwd(name, h, w_in, w_group, scale, x_in, gain_next, tr=1024):
    S, D = h.shape
    H = POOL_HALO
    n_groups = D // POOL_DIM

    def body(h_ref, hh_ref, wi_ref, w_ref, sc_ref, x_ref, gn_ref, y_ref, z_ref, xo_ref, hn_ref, xs_ref):
        i = pl.program_id(0)
        g = pl.program_id(1)
        uv = _dot(h_ref[...], wi_ref[...], "nn")
        halo = jnp.where(i > 0, _dot(hh_ref[...], wi_ref[...], "nn"), 0.0)
        s = jnp.concatenate([halo, uv], axis=0)
        levels = []
        for k in (1, 2, 4, 8):
            s = s + _shift_rows(s, k)
            levels.append(s[H:])
        t = i * tr + lax.broadcasted_iota(jnp.int32, (tr, 1), 0)
        cnt = jnp.minimum(t + 1, jnp.left_shift(2, g)).astype(F32)
        y = _pool_level(g, levels) / cnt - uv
        yb = y.astype(BF16)
        z = _dot(yb, w_ref[...], "nn")
        y_ref[...] = yb
        z_ref[...] = z.astype(z_ref.dtype)
        x_out = x_ref[...] + z * sc_ref[...]
        xo_ref[...] = x_out
        xs_ref[g] = x_out

        @pl.when(g == n_groups - 1)
        def _():
            xf = jnp.concatenate([xs_ref[k] for k in range(n_groups)], axis=1)
            r = lax.rsqrt(jnp.mean(xf * xf, axis=-1, keepdims=True) + RMS_EPS)
            hn_ref[...] = ((xf * r) * gn_ref[...]).astype(hn_ref.dtype)

    blk = pl.BlockSpec((tr, POOL_DIM), lambda i, g: (i, g))
    row = pl.BlockSpec((tr, D), lambda i, g: (i, 0))
    return _pcall(
        body, name=name, grid=(S // tr, n_groups),
        in_specs=[row, pl.BlockSpec((H, D), lambda i, g: (jnp.maximum(i * (tr // H) - 1, 0), 0)),
                  pl.BlockSpec((D, POOL_DIM), lambda i, g: (0, g)),
                  pl.BlockSpec((None, POOL_DIM, POOL_DIM), lambda i, g: (g, 0, 0)),
                  pl.BlockSpec((1, POOL_DIM), lambda i, g: (0, g)), blk, pl.BlockSpec((1, D), lambda i, g: (0, 0))],
        out_specs=[blk, blk, blk, row],
        out_shape=[jax.ShapeDtypeStruct((S, D), BF16), jax.ShapeDtypeStruct((S, D), BF16),
                   jax.ShapeDtypeStruct((S, D), F32), jax.ShapeDtypeStruct((S, D), BF16)],
        scratch_shapes=[pltpu.VMEM((n_groups, tr, POOL_DIM), F32)],
        compiler_params=_params(("parallel", "arbitrary")))(h, h, w_in, w_group, scale, x_in, gain_next)


def _pool_bwd(name, d_out, z, y, scale, w_group, tr=1024):
    S, D = d_out.shape
    H = POOL_HALO

    def body(d_ref, dn_ref, z_ref, y_ref, sc_ref, w_ref, du_ref, dw_ref, dsc_ref):
        g = pl.program_id(0)
        i = pl.program_id(1)
        dv = d_ref[...]
        dz = jnp.concatenate([dv, dn_ref[...]], axis=0) * sc_ref[...]
        dzb = dz.astype(BF16)
        dy = _dot(dzb, w_ref[...], "nt")
        t = i * tr + lax.broadcasted_iota(jnp.int32, (tr + H, 1), 0)
        cnt = jnp.minimum(t + 1, jnp.left_shift(2, g)).astype(F32)
        s = jnp.where(t < S, dy / cnt, 0.0)
        levels = []
        for k in (1, 2, 4, 8):
            s = s + _shift_rows(s, -k)
            levels.append(s[:tr])
        du_ref[...] = (_pool_level(g, levels) - dy[:tr]).astype(du_ref.dtype)
        dw = _dot(y_ref[...], dzb[:tr], "tn")
        dsc = jnp.sum(dv * z_ref[...].astype(F32), axis=0, keepdims=True)

        @pl.when(i == 0)
        def _():
            dw_ref[...] = dw
            dsc_ref[...] = dsc

        @pl.when(i > 0)
        def _():
            dw_ref[...] += dw
            dsc_ref[...] += dsc

    blk = pl.BlockSpec((tr, POOL_DIM), lambda g, i: (i, g))
    vec = pl.BlockSpec((1, POOL_DIM), lambda g, i: (0, g))
    mat = pl.BlockSpec((None, POOL_DIM, POOL_DIM), lambda g, i: (g, 0, 0))
    nxt = pl.BlockSpec((H, POOL_DIM), lambda g, i: (jnp.minimum((i + 1) * (tr // H), S // H - 1), g))
    return _pcall(
        body, name=name, grid=(D // POOL_DIM, S // tr), in_specs=[blk, nxt, blk, blk, vec, mat],
        out_specs=[blk, mat, vec],
        out_shape=[jax.ShapeDtypeStruct((S, D), BF16), jax.ShapeDtypeStruct((D // POOL_DIM, POOL_DIM, POOL_DIM), F32),
                   jax.ShapeDtypeStruct((1, D), F32)],
        compiler_params=_params(("parallel", "arbitrary")))(d_out, d_out, z, y, scale, w_group)


def _row_tile(rows, cols, target_bytes=1 << 20):
    best = rows
    for tr in range(8, rows + 1, 8):
        if rows % tr == 0 and tr * cols * 4 <= target_bytes:
            best = tr
    return best if best * cols * 4 <= 4 * target_bytes else rows


def _adamw_step(w, g, m, v):
    m2 = ADAM_B1 * m + (1.0 - ADAM_B1) * g
    v2 = ADAM_B2 * v + (1.0 - ADAM_B2) * (g * g)
    m_hat = m2 / (1.0 - ADAM_B1 ** ADAM_STEP)
    v_hat = v2 / (1.0 - ADAM_B2 ** ADAM_STEP)
    return -ADAM_LR * (m_hat / (jnp.sqrt(v_hat) + ADAM_EPS) + ADAM_WD * w), m2, v2


def _adamw_refs(w_ref, g_ref, m_ref, v_ref, go_ref, d_ref, mo_ref, vo_ref):
    gv = g_ref[...]
    d_ref[...], mo_ref[...], vo_ref[...] = _adamw_step(w_ref[...], gv, m_ref[...], v_ref[...])
    go_ref[...] = gv


def _adamw_small(name, items):
    n = len(items)

    def body(*refs):
        for t in range(n):
            _adamw_refs(*refs[4 * t:4 * t + 4], *refs[4 * n + 4 * t:4 * n + 4 * t + 4])

    specs, shapes = [], []
    for w, _, _, _ in items:
        specs += [pl.BlockSpec(w.shape, lambda i: (0, 0))] * 4
        shapes += [jax.ShapeDtypeStruct(w.shape, F32)] * 4
    res = _pcall(body, name=name, grid=(1,), in_specs=specs, out_specs=specs, out_shape=shapes,
                 compiler_params=_params(("arbitrary",)))(*[a for item in items for a in item])
    return [res[4 * t:4 * t + 4] for t in range(n)]


def _sum_leading(name, a, out_dtype):
    n, R, C = a.shape
    tr = _row_tile(R, C) if R % 16 == 0 else R
    if tr % 16:
        tr = R

    def body(a_ref, o_ref):
        acc = a_ref[0].astype(F32)
        for j in range(1, n):
            acc = acc + a_ref[j].astype(F32)
        o_ref[...] = acc.astype(o_ref.dtype)

    return _pcall(body, name=name, grid=(R // tr,), in_specs=[pl.BlockSpec((n, tr, C), lambda i: (0, i, 0))],
                  out_specs=pl.BlockSpec((tr, C), lambda i: (i, 0)), out_shape=jax.ShapeDtypeStruct((R, C), out_dtype),
                  compiler_params=_params(("parallel",)))(a)


def _cast_many(name, items, chip, steps=4):
    tiles = []
    for w, _, dtype in items:
        R = w.shape[1]
        nt = steps if R % (steps * 16) == 0 else 1
        tiles.append((nt, R // nt))

    def body(c_ref, *refs):
        i = pl.program_id(0)
        for t, (nt, _) in enumerate(tiles):
            w_ref, o_ref = refs[t], refs[len(items) + t]

            def cast(w_ref=w_ref, o_ref=o_ref):
                o_ref[...] = w_ref[...].astype(o_ref.dtype)

            if nt == steps:
                cast()
            else:
                pl.when(i < nt)(cast)

    in_specs, out_specs, out_shape = [], [], []
    for (w, layer, dtype), (nt, tr) in zip(items, tiles):
        C = w.shape[2]
        in_specs.append(pl.BlockSpec((None, tr, C), lambda i, c_ref, l=layer, nt=nt: (l, jnp.minimum(i, nt - 1), 0)))
        out_specs.append(pl.BlockSpec((None, tr, C), lambda i, c_ref, nt=nt: (c_ref[0], jnp.minimum(i, nt - 1), 0)))
        out_shape.append(jax.ShapeDtypeStruct((N_CHIPS, w.shape[1], C), dtype))
    return _pcall(body, prefetch=1, name=name, grid=(steps,), in_specs=in_specs, out_specs=out_specs,
                  out_shape=out_shape, compiler_params=_params(("arbitrary",)))(chip, *[w for w, _, _ in items])


def _cast_qkv(name, w, chip, tr=256):
    _, R, C = w.shape

    def body(c_ref, w_ref, own_ref, *piece_refs):
        v = w_ref[...].astype(BF16)
        own_ref[...] = v
        for p, ref in enumerate(piece_refs):
            ref[...] = v[:, p * QKV_TILE:(p + 1) * QKV_TILE]

    piece = pl.BlockSpec((None, tr, QKV_TILE), lambda i, c_ref: (c_ref[0], i, 0))
    return _pcall(body, prefetch=1, name=name, grid=(R // tr,),
                  in_specs=[pl.BlockSpec((None, tr, C), lambda i, c_ref: (0, i, 0))],
                  out_specs=[pl.BlockSpec((tr, C), lambda i, c_ref: (i, 0))] + [piece] * QKV_PIECES,
                  out_shape=[jax.ShapeDtypeStruct((R, C), BF16)]
                  + [jax.ShapeDtypeStruct((N_CHIPS, R, QKV_TILE), BF16)] * QKV_PIECES,
                  compiler_params=_params(("parallel",)))(chip, w)


def _owner_adamw(name, mine, landed, w, m, v, chip, core, outs, layer, col):
    _, half, C = mine.shape
    k, _ = col
    tr = _row_tile(half, C, 1 << 19)
    if tr % 16:
        tr = half
    nrt = half // tr

    def body(ch_ref, co_ref, mine_ref, landed_ref, w_ref, m_ref, v_ref, *rest):
        g_ref, d_ref, m2_ref, v2_ref = rest[-4:]
        g = mine_ref[...].astype(F32)
        for j in range(3):
            g = g + landed_ref[j].astype(F32)
        d_ref[...], m2_ref[...], v2_ref[...] = _adamw_step(w_ref[...], g, m_ref[...], v_ref[...])
        g_ref[...] = g

    piece = pl.BlockSpec((None, tr, C), lambda i, ch, co: (layer, co[0] * nrt + i, k))
    in_specs = [pl.BlockSpec((None, tr, C), lambda i, ch, co: (ch[0], i, 0)),
                pl.BlockSpec((3, tr, C), lambda i, ch, co: (0, i, 0)), piece, piece, piece]
    operands = [chip, core, mine, landed, w, m, v]
    aliases = {}
    if outs is not None:
        in_specs += [ANY] * 4
        operands += list(outs)
        aliases = {len(operands) - 4 + j: j for j in range(4)}
    return _pcall(body, prefetch=2, name=name, grid=(nrt,), in_specs=in_specs, out_specs=[piece] * 4,
                  out_shape=[jax.ShapeDtypeStruct(w.shape, F32)] * 4, input_output_aliases=aliases,
                  compiler_params=_params(("parallel",)))(*operands)


def _add_my_half(name, ps, qs, core):
    n = len(ps)

    def body(c_ref, *refs):
        for t in range(n):
            p_ref, q_ref, o_ref = refs[t], refs[n + t], refs[2 * n + t]
            o_ref[...] = (p_ref[...].astype(F32) + q_ref[...].astype(F32)).astype(o_ref.dtype)

    halves = [(p.shape[1] // 2, p.shape[2]) for p in ps]
    mine = [pl.BlockSpec((None, h, c), lambda s, c_ref: (s, c_ref[0], 0)) for h, c in halves]
    whole = [pl.BlockSpec((None, h, c), lambda s, c_ref: (s, 0, 0)) for h, c in halves]
    return _pcall(body, prefetch=1, name=name, grid=(N_CHIPS,), in_specs=mine + whole, out_specs=whole,
                  out_shape=[jax.ShapeDtypeStruct((N_CHIPS, h, c), BF16) for h, c in halves],
                  compiler_params=_params(("parallel",)))(core, *ps, *qs)


ANY = pl.BlockSpec(memory_space=pl.ANY)


def _place():
    x, y, c = lax.axis_index("x"), lax.axis_index("y"), lax.axis_index("c")
    other_chips = [(1 - x, y), (x, 1 - y), (1 - x, 1 - y)]
    return x, y, c, other_chips


def _remote(src, dst, send, recv, k, to):
    return pltpu.make_async_remote_copy(src_ref=src, dst_ref=dst, send_sem=send.at[k], recv_sem=recv.at[k],
                                        device_id=to, device_id_type=MESH)


def _in_place(bufs):
    return dict(operands=bufs, out_shapes=[jax.ShapeDtypeStruct(b.shape, b.dtype) for b in bufs],
                aliases={t: t for t in range(len(bufs))})


def _gather_rider(bufs, jobs):
    def copies(ins, outs, send, recv, base=0):
        x, y, c, chips = _place()
        out = []
        for t, over_ici, rows in jobs:
            ref = outs[t]
            if rows is not None:
                a, b, n = rows
                half = ref.shape[1] // 2
                rows = pl.ds(c * half + a * half // n, (b - a) * half // n)
            for px, py in chips:
                slot = 2 * x + y if over_ici else 2 * px + py
                piece = ref.at[slot] if rows is None else ref.at[slot, rows]
                out.append(_remote(piece, piece, send, recv, base + len(out), (px, py, c) if over_ici else (x, y, 1 - c)))
        return out

    return _Rider(n_copies=3 * len(jobs), copies=copies, **_in_place(bufs))


def _gather_and_pass_rider(buf):
    half = buf.shape[1] // 2

    def pieces(outs):
        x, y, c, chips = _place()
        rows = lambda core: pl.ds(core * half, half)
        mine = outs[0].at[2 * x + y, rows(c)]
        landed = [outs[0].at[2 * px + py, rows(c)] for px, py in chips]
        theirs = [outs[0].at[2 * px + py, rows(1 - c)] for px, py in chips]
        return (x, y, c, chips), mine, landed, theirs

    def start(ins, outs, send, recv, base=0):
        (x, y, c, chips), mine, _, _ = pieces(outs)
        for j, (px, py) in enumerate(chips):
            _remote(mine, mine, send, recv, base + j, (px, py, c)).start()

    def finish(ins, outs, send, recv, base=0):
        (x, y, c, chips), mine, landed, theirs = pieces(outs)
        sibling = (x, y, 1 - c)
        passed = []
        for j, (px, py) in enumerate(chips):
            _remote(landed[j], landed[j], send, recv, base + j, (px, py, c)).wait_recv()
            passed.append(_remote(landed[j], landed[j], send, recv, base + 3 + j, sibling))
            passed[-1].start()
        for j in range(3):
            _remote(theirs[j], theirs[j], send, recv, base + 3 + j, sibling).wait_recv()
        for j, (px, py) in enumerate(chips):
            _remote(mine, mine, send, recv, base + j, (px, py, c)).wait_send()
            passed[j].wait_send()

    return _Rider(n_copies=6, start=start, finish=finish, **_in_place([buf]))


def _swap_halves_rider(parts):
    n = len(parts)

    def copies(ins, outs, send, recv, base=0):
        x, y, c, _ = _place()
        out = []
        for t in range(n):
            half = ins[t].shape[1] // 2
            out.append(_remote(ins[t].at[:, pl.ds((1 - c) * half, half)], outs[t], send, recv, base + t,
                               (x, y, 1 - c)))
        return out

    shapes = [jax.ShapeDtypeStruct((p.shape[0], p.shape[1] // 2, p.shape[2]), p.dtype) for p in parts]
    return _Rider(parts, shapes, {}, n, copies)


def _scatter_rider(sums, landed, relations):
    n = len(sums)
    kept = [t for t in range(n) if landed[t] is not None]

    def copies(ins, outs, send, recv, base=0):
        x, y, c, chips = _place()
        out = []
        for t in range(n):
            for j in relations[t]:
                px, py = chips[j]
                out.append(_remote(ins[t].at[2 * px + py], outs[t].at[j], send, recv, base + len(out), (px, py, c)))
        return out

    shapes = [jax.ShapeDtypeStruct((3,) + s.shape[1:], s.dtype) for s in sums]
    return _Rider(list(sums) + [landed[t] for t in kept], shapes, {n + k: t for k, t in enumerate(kept)},
                  sum(len(r) for r in relations), copies)


def _share_rider(blocks, pieces):
    def copies(ins, outs, send, recv, base=0):
        x, y, c, _ = _place()
        out = []
        for k, (t, l, col) in enumerate(pieces):
            ref = outs[t].at[l]
            r2 = ref.shape[0] // 2
            width = ref.shape[1] // col[1]
            piece = ref.at[pl.ds(c * r2, r2), pl.ds(col[0] * width, width)]
            out.append(_remote(piece, piece, send, recv, base + k, (x, y, 1 - c)))
        return out

    return _Rider(n_copies=len(pieces), copies=copies, **_in_place(blocks))


def _hand_over(name, arrays):
    n = len(arrays)
    return _pcall(lambda *refs: None, name=name, in_specs=[ANY] * n, out_specs=[ANY] * n,
                  out_shape=[jax.ShapeDtypeStruct(a.shape, a.dtype) for a in arrays],
                  input_output_aliases={i: i for i in range(n)})(*arrays)


def _gather_small(name, v):
    def body(v_ref, out_ref, send_sem, recv_sem, local_sem):
        x, y, c, _ = _place()
        me = 4 * x + 2 * y + c
        flips = [(fx, fy, fc) for fx in (0, 1) for fy in (0, 1) for fc in (0, 1)][1:]
        local = pltpu.make_async_copy(v_ref, out_ref.at[me], local_sem)
        local.start()
        copies = []
        for j, (fx, fy, fc) in enumerate(flips):
            peer = (x ^ fx, y ^ fy, c ^ fc)
            copies.append(pltpu.make_async_remote_copy(
                src_ref=v_ref, dst_ref=out_ref.at[me], send_sem=send_sem.at[j], recv_sem=recv_sem.at[j],
                device_id=peer, device_id_type=MESH))
        for cp in copies:
            cp.start()
        for j, (fx, fy, fc) in enumerate(flips):
            peer = (x ^ fx, y ^ fy, c ^ fc)
            pltpu.make_async_remote_copy(
                src_ref=v_ref, dst_ref=out_ref.at[4 * peer[0] + 2 * peer[1] + peer[2]], send_sem=send_sem.at[j],
                recv_sem=recv_sem.at[j], device_id=peer, device_id_type=MESH).wait_recv()
        for cp in copies:
            cp.wait_send()
        local.wait()

    return _pcall(body, name=name, in_specs=[ANY], out_specs=ANY,
                  out_shape=jax.ShapeDtypeStruct((8,) + v.shape, v.dtype),
                  scratch_shapes=[pltpu.SemaphoreType.DMA((7,)), pltpu.SemaphoreType.DMA((7,)),
                                  pltpu.SemaphoreType.DMA])(v)


def _fold(a, dil):
    if dil == 1:
        return a
    S, C = a.shape
    return a.reshape(S // dil, dil, C).transpose(1, 0, 2).reshape(S, C)


def _unfold(a, dil):
    if dil == 1:
        return a
    S, C = a.shape
    return a.reshape(dil, S // dil, C).transpose(1, 0, 2).reshape(S, C)


def _fold_groups(a):
    return jnp.stack([_fold(a[g] if a.ndim == 3 else a, d) for g, d in enumerate(ATTN_DILATIONS)])


def _unfold_groups(a):
    return jnp.stack([_unfold(a[g], d) for g, d in enumerate(ATTN_DILATIONS)])


class _NoComm:
    def __init__(self, weights):
        self.weights = weights
        self.grads = {}
        self.chip = jnp.zeros((1,), jnp.int32)

    def w(self, name):
        return self.weights[name]

    def run(self, fn, name, *args, **kw):
        return fn(name, *args, **kw)

    def grad(self, name, value):
        self.grads[name] = value


def _local_step(xs, tgt, attn_norm, ffn_norm, final_norm, comm):
    run = comm.run
    hf = run(_norm_fold, "fold", xs, attn_norm)
    qkv = run(_qkv_tiles, "qkv_own", None, hf, comm.w("wq_own"), None, comm.chip)
    for p in range(QKV_PIECES):
        qkv = run(_qkv_tiles, "qkv_piece%d" % p, p, hf, comm.w("wq%d" % p), qkv, comm.chip)
    o_f, lse_f = run(_attn_fwd, "attn_fwd", qkv)
    lse_t = _unfold_groups(lse_f)
    merged = run(_merge_fwd, "merge_fwd", o_f, lse_t)
    x1, h1 = run(_proj_res_norm, "attn_out", merged, comm.w("o"), xs, ffn_norm[0:1])
    gu0, act0 = run(_ffn_up, "ffn0_up", h1, comm.w("gu0"))
    x2, h2 = run(_proj_res_norm, "ffn0_down", act0, comm.w("d0"), x1, comm.w("pool_norm"))
    y, z, x3, h3 = run(_pool_fwd, "pool_fwd", h2, comm.w("p"), comm.w("pg"), comm.w("pool_scale"), x2, ffn_norm[1:2])
    gu1, act1 = run(_ffn_up, "ffn1_up", h3, comm.w("gu1"))
    loss, dx4, dx4_b, d_final = run(_proj_res_loss, "ffn1_down", act1, comm.w("d1"), x3, final_norm, tgt)

    dgu1 = run(_ffn_down_bwd, "ffn1_down_bwd", dx4_b, comm.w("d1"), gu1)
    comm.grad("d1", run(_mm_tn, "ffn1_down_dw", act1, dx4_b, FF_SHARD, 2048))
    comm.grad("gu1", run(_ffn_dw_up, "ffn1_up_dw", h3, dgu1))
    dx3, d_ffn1 = run(_ffn_dh, "ffn1_up_dh", dgu1, comm.w("gu1"), x3, ffn_norm[1:2], dx4)

    du, dw_pg, d_scale = run(_pool_bwd, "pool_bwd", dx3, z, y, comm.w("pool_scale"), comm.w("pg"))
    comm.grad("pg", dw_pg)
    comm.grad("p", run(_mm_tn, "pool_in_dw", h2, du, D_MODEL, h2.shape[0]))
    dx2, dx2_b, d_pool = run(_dh_norm_bwd, "pool_in_dh", du, comm.w("p"), x2, comm.w("pool_norm"), dx3)

    dgu0 = run(_ffn_down_bwd, "ffn0_down_bwd", dx2_b, comm.w("d0"), gu0)
    comm.grad("d0", run(_mm_tn, "ffn0_down_dw", act0, dx2_b, FF_SHARD, 2048))
    comm.grad("gu0", run(_ffn_dw_up, "ffn0_up_dw", h1, dgu0))
    dx1, d_ffn0 = run(_ffn_dh, "ffn0_up_dh", dgu0, comm.w("gu0"), x1, ffn_norm[0:1], dx2)

    comm.grad("o", run(_mm_tn, "attn_out_dw", merged, dx1, D_MODEL, 2048))
    do_f, dl_t = run(_merge_bwd, "merge_bwd", dx1, comm.w("o"), o_f, lse_t)
    dqkv = run(_attn_bwd, "attn_bwd", qkv, do_f, lse_f, _fold_groups(dl_t))
    for p in range(QKV_PIECES):
        comm.grad("qkv%d" % p, run(_qkv_dw, "qkv_dw%d" % p, p, hf, dqkv))
    dhf = None
    for g in range(N_GROUPS):
        dhf = run(_qkv_dh, "qkv_dh%d" % g, g, dqkv, [comm.w("wq%d" % p) for p in range(QKV_PIECES)], dhf)
    grad_x, d_attn = run(_rms_bwd_folded, "norm_attn_bwd", xs, attn_norm, dhf, dx1)

    small = dict(attn=d_attn, ffn0=d_ffn0, ffn1=d_ffn1, final=d_final, pool=d_pool, scale=d_scale)
    return loss, grad_x, small


TENSORS = ("qkv", "o", "p", "pg", "gu0", "gu1", "d0", "d1")


def _block_of(name):
    if name[:-1] in ("gu", "d"):
        return name[:-1], int(name[-1]), (0, 1)
    if name[:-1] == "qkv":
        return "qkv", 0, (int(name[-1]), QKV_PIECES)
    return name, 0, (0, 1)


class _MeshComm:
    def __init__(self, bufs, state, chip_arr, core_arr, pg_rows):
        self.bufs = dict(bufs)
        self.state = state
        self.chip, self.chip_arr, self.core_arr, self.pg_rows = chip_arr, chip_arr, core_arr, pg_rows
        self.parts, self.sums, self.blocks, self.landed, self.arrived = {}, {}, {}, {}, {}
        move = lambda ici=(), d2d=(): lambda: self.gather(ici, d2d)
        whole = lambda name: lambda: self.gather_and_pass(name)
        swap = lambda *names: lambda: self.swap(names)
        scatter = lambda *names: lambda: self.scatter(names)
        share = lambda *names: lambda: self.share(names)
        self.plan = {
            "fold": [whole("wq0")],
            "qkv_own": [whole("wq1")],
            "qkv_piece0": [whole("wq2")],
            "qkv_piece1": [move(ici=["o", "gu0[0:1/4]"])],
            "qkv_piece2": [move(ici=["gu0[1:2/4]"], d2d=["o", "gu0[0:1/4]"])],
            "attn_fwd": [move(ici=["gu0[2:4/4]", "d0[0:1/2]"], d2d=["gu0[1:2/4]"])],
            "merge_fwd": [move(ici=["p", "pg"], d2d=["gu0[2:4/4]", "d0[0:1/2]"])],
            "attn_out": [move(ici=["d0[1:2/2]", "pool_norm", "pool_scale"], d2d=["p", "pg"])],
            "ffn0_up": [move(ici=["gu1[0:3/4]"], d2d=["d0[1:2/2]"])],
            "ffn0_down": [move(ici=["gu1[3:4/4]", "d1[0:1/2]"], d2d=["gu1[0:3/4]"])],
            "pool_fwd": [move(ici=["d1[1:2/2]"], d2d=["gu1[3:4/4]", "d1[0:1/2]"])],
            "ffn1_up": [move(d2d=["d1[1:2/2]"])],
            "ffn1_up_dh": [swap("d1", "gu1")],
            "pool_bwd": [scatter("d1{01}")],
            "pool_in_dh": [scatter("d1{2}")],
            "ffn0_down_bwd": [scatter("gu1{01}")],
            "ffn0_down_dw": [scatter("gu1{2}")],
            "ffn0_up_dw": [share("gu1", "d1")],
            "ffn0_up_dh": [swap("pg", "p", "d0", "gu0")],
            "merge_bwd": [swap("o")],
            "attn_bwd": [scatter("gu0", "d0", "o")],
            "qkv_dw0": [share("d0"), scatter("p", "pg")],
            "qkv_dw1": [share("gu0", "o"), swap("qkv0")],
            "qkv_dw2": [share("p", "pg"), scatter("qkv0"), swap("qkv1")],
            "qkv_dh0": [share("qkv0"), scatter("qkv1"), swap("qkv2")],
            "qkv_dh1": [share("qkv1"), scatter("qkv2")],
            "qkv_dh2": [share("qkv2")],
        }

    def gather_and_pass(self, name):
        return _gather_and_pass_rider(self.bufs[name]), lambda res: self.bufs.update({name: res[0]})

    def gather(self, ici, d2d):
        names, jobs = [], []
        for over_ici, specs in ((True, ici), (False, d2d)):
            for spec in specs:
                name, _, rows = spec.partition("[")
                if name not in names:
                    names.append(name)
                rng = None
                if name in TENSORS:
                    ab, _, n = (rows[:-1] or "0:1/1").partition("/")
                    rng = (int(ab.split(":")[0]), int(ab.split(":")[1]), int(n))
                jobs.append((names.index(name), over_ici, rng))
        return _gather_rider([self.bufs[n] for n in names], jobs), lambda res: self.bufs.update(zip(names, res))

    def swap(self, names):
        def done(res):
            sums = _add_my_half("chip_sum_" + "_".join(names), [self.parts[n] for n in names], res, self.core_arr)
            self.sums.update(zip(names, sums))
        return _swap_halves_rider([self.parts[n] for n in names]), done

    def scatter(self, specs):
        names = [s.partition("{")[0] for s in specs]
        relations = [tuple(int(ch) for ch in s.partition("{")[2][:-1]) or (0, 1, 2) for s in specs]

        def done(res):
            for n, rel, landed in zip(names, relations, res):
                self.landed[n] = landed
                self.arrived[n] = self.arrived.get(n, ()) + rel
                if len(self.arrived[n]) < 3:
                    continue
                key, layer, col = _block_of(n)
                self.blocks[key] = _owner_adamw("owner_adamw_" + n, self.sums[n], landed, *self.state[key],
                                                self.chip_arr, self.core_arr, self.blocks.get(key), layer, col)
        rider = _scatter_rider([self.sums[n] for n in names], [self.landed.get(n) for n in names], relations)
        return rider, done

    def share(self, names):
        keys, pieces = [], []
        for n in names:
            key, layer, col = _block_of(n)
            if key not in keys:
                keys.append(key)
            pieces += [(4 * keys.index(key) + j, layer, col) for j in range(4)]

        def done(res):
            for i, key in enumerate(keys):
                self.blocks[key] = res[4 * i:4 * i + 4]
        return _share_rider([a for k in keys for a in self.blocks[k]], pieces), done

    def run(self, fn, name, *args, **kw):
        made = [make() for make in self.plan.get(name, [])]
        if not made:
            return fn(name, *args, **kw)
        riders = [r for r, _ in made]
        out = _ride(riders[0] if len(riders) == 1 else _riders(*riders), fn, name, *args, **kw)
        for r, done in made:
            done(r.results)
        return out

    def w(self, name):
        b = self.bufs[name]
        if name in ("o", "p", "d0", "d1"):
            return b.reshape(-1, b.shape[-1])
        if name == "pg":
            b = b.reshape(N_CHIPS, 4, self.pg_rows, POOL_DIM).transpose(1, 0, 2, 3)
            return b.reshape(4, POOL_DIM, POOL_DIM)
        if name in ("pool_norm", "pool_scale"):
            return b.reshape(1, -1)
        return b

    def grad(self, name, value):
        if name == "pg":
            value = value.reshape(4, N_CHIPS, self.pg_rows, POOL_DIM).transpose(1, 0, 2, 3)
            value = value.reshape(N_CHIPS, 4 * self.pg_rows, POOL_DIM).astype(BF16)
        elif name in ("o", "p", "d0", "d1"):
            value = value.reshape(N_CHIPS, value.shape[0] // N_CHIPS, value.shape[1])
        self.parts[name] = value


def kernel(x, attn_norm, w_qkv, w_attn_out, pool_norm, w_pool_in, w_pool_group, pool_scale, ffn_norm, w_ffn_gate_up, w_ffn_down, final_norm, loss_target, m_attn_norm, m_w_qkv, m_w_attn_out, m_pool_norm, m_w_pool_in, m_w_pool_group, m_pool_scale, m_ffn_norm, m_w_ffn_gate_up, m_w_ffn_down, m_final_norm, v_attn_norm, v_w_qkv, v_w_attn_out, v_pool_norm, v_w_pool_in, v_w_pool_group, v_pool_scale, v_ffn_norm, v_w_ffn_gate_up, v_w_ffn_down, v_final_norm):
    D = D_MODEL
    chip = 2 * lax.axis_index("x") + lax.axis_index("y")
    core = lax.axis_index("c")
    pg_rows = w_pool_group.shape[2]

    chip_arr = chip.astype(jnp.int32).reshape(1)
    core_arr = core.astype(jnp.int32).reshape(1)

    pieces = _cast_qkv("cast_qkv", w_qkv, chip_arr)
    bufs = dict(zip(["wq_own"] + ["wq%d" % p for p in range(QKV_PIECES)], pieces))
    shards = [(w_attn_out, 0, BF16), (w_pool_in, 0, BF16), (w_pool_group.reshape(1, 4 * pg_rows, POOL_DIM), 0, BF16),
              (w_ffn_gate_up, 0, BF16), (w_ffn_gate_up, 1, BF16), (w_ffn_down, 0, BF16), (w_ffn_down, 1, BF16),
              (pool_norm[None], 0, F32), (pool_scale[None], 0, F32)]
    bufs.update(zip(TENSORS[1:] + ("pool_norm", "pool_scale"), _cast_many("cast_rest", shards, chip_arr)))

    as_block = lambda a: a.reshape((-1,) + a.shape[-2:]) if a.ndim == 3 else a.reshape(1, -1, a.shape[-1])
    owned = dict(qkv=(w_qkv, m_w_qkv, v_w_qkv), o=(w_attn_out, m_w_attn_out, v_w_attn_out),
                 p=(w_pool_in, m_w_pool_in, v_w_pool_in), pg=(w_pool_group, m_w_pool_group, v_w_pool_group),
                 gu=(w_ffn_gate_up, m_w_ffn_gate_up, v_w_ffn_gate_up), d=(w_ffn_down, m_w_ffn_down, v_w_ffn_down))
    state = {k: tuple(as_block(a) for a in wmv) for k, wmv in owned.items()}
    comm = _MeshComm(bufs, state, chip_arr, core_arr, pg_rows)
    loss_part, grad_x, small = _local_step(x[0], loss_target[0], attn_norm, ffn_norm, final_norm.reshape(1, D), comm)
    done = _hand_over("results", [a for k in owned for a in comm.blocks[k]])
    updated = {k: [a.reshape(owned[k][0].shape) for a in done[4 * i:4 * i + 4]] for i, k in enumerate(owned)}

    rows = jnp.concatenate([small["attn"], small["ffn0"], small["ffn1"], small["final"], small["pool"],
                            small["scale"], jnp.pad(loss_part, ((0, 0), (0, D - 1))), jnp.zeros((1, D), F32)], axis=0)
    all_rows = _gather_small("gather_gain_grads", rows)
    tot = _sum_leading("sum_gain_grads", all_rows, F32)
    loss = tot[6, 0]
    g_attn_norm = tot[0:1]
    g_ffn_norm = tot[1:3]
    g_final_norm = tot[3]
    g_pool_norm = lax.dynamic_slice(tot, (4, chip * POOL_DIM), (1, POOL_DIM))
    g_pool_scale = lax.dynamic_slice(tot, (5, chip * POOL_DIM), (1, POOL_DIM))

    gains = [(attn_norm, g_attn_norm, m_attn_norm, v_attn_norm), (pool_norm, g_pool_norm, m_pool_norm, v_pool_norm),
             (pool_scale, g_pool_scale, m_pool_scale, v_pool_scale), (ffn_norm, g_ffn_norm, m_ffn_norm, v_ffn_norm),
             (final_norm, g_final_norm, m_final_norm, v_final_norm)]
    as_rows = lambda a: a.reshape(-1, a.shape[-1])
    small_res = _adamw_small("adamw_gains", [tuple(as_rows(a) for a in item) for item in gains])
    small_res = [[a.reshape(item[0].shape) for a in r] for r, item in zip(small_res, gains)]
    results = [small_res[0], updated["qkv"], updated["o"], small_res[1], updated["p"], updated["pg"], small_res[2],
               small_res[3], updated["gu"], updated["d"], small_res[4]]
    grads = [r[0] for r in results]
    deltas = [r[1] for r in results]
    new_m = [r[2] for r in results]
    new_v = [r[3] for r in results]
    return (loss, grad_x[None], *grads, *deltas, *new_m, *new_v)
```

```python
import functools

import jax
import jax.numpy as jnp
from jax import lax
from jax.experimental import pallas as pl
from jax.experimental.pallas import tpu as pltpu

F32 = jnp.float32
BF16 = jnp.bfloat16
MESH = pl.DeviceIdType.MESH

D_MODEL = 1024
N_HEADS = 8
HEAD_DIM = 128
Q_BLOCK = 128
ATTN_DILATIONS = (1, 4, 16)
N_GROUPS = 3
D_FF = 2816
N_CHIPS = 4
FF_SHARD = 2 * D_FF // N_CHIPS
QKV_SHARD = 3 * 3 * D_MODEL // N_CHIPS
QKV_TILE = 768
POOL_DIM = 256
POOL_HALO = 16
RMS_EPS = 1e-6
NEG = -1e30
ADAM_LR = 0.001
ADAM_B1 = 0.9
ADAM_B2 = 0.999
ADAM_EPS = 1e-08
ADAM_WD = 0.01
ADAM_STEP = 10
VMEM_LIMIT = 56 * 1024 * 1024

_DN = {
    "nn": (((1,), (0,)), ((), ())),
    "nt": (((1,), (1,)), ((), ())),
    "tn": (((0,), (0,)), ((), ())),
}


class _Rider:
    def __init__(self, operands, out_shapes, aliases, n_copies, copies=None, start=None, finish=None):
        self.operands = list(operands)
        self.out_shapes = list(out_shapes)
        self.aliases = dict(aliases)
        self.n_copies = n_copies
        self.results = None

        def start_copies(ins, outs, send, recv, base=0):
            for cp in copies(ins, outs, send, recv, base):
                cp.start()

        def wait_copies(ins, outs, send, recv, base=0):
            for cp in copies(ins, outs, send, recv, base):
                cp.wait()

        self.start = start or start_copies
        self.finish = finish or wait_copies


def _riders(*riders):
    operands, out_shapes, aliases, offsets = [], [], {}, []
    n = 0
    for r in riders:
        offsets.append((len(operands), len(out_shapes), n))
        aliases.update({len(operands) + i: len(out_shapes) + o for i, o in r.aliases.items()})
        operands += r.operands
        out_shapes += r.out_shapes
        n += r.n_copies

    def each(which):
        def go(ins, outs, send, recv, base=0):
            for r, (i0, o0, s0) in zip(riders, offsets):
                getattr(r, which)(ins[i0:i0 + len(r.operands)], outs[o0:o0 + len(r.out_shapes)], send, recv,
                                  base + s0)
        return go

    both = _Rider(operands, out_shapes, aliases, n, start=each("start"), finish=each("finish"))
    both.parts = (riders, offsets)
    return both


_pending_rider = []


def _ride(rider, fn, *args, **kw):
    _pending_rider.append(rider)
    out = fn(*args, **kw)
    assert not _pending_rider
    if hasattr(rider, "parts"):
        for r, (_, o0, _) in zip(*rider.parts):
            r.results = rider.results[o0:o0 + len(r.out_shapes)]
    return out


def _pcall(body, prefetch=0, **kw):
    def build(body, kw):
        if not prefetch:
            return pl.pallas_call(body, **kw)
        kw = dict(kw)
        grid_spec = pltpu.PrefetchScalarGridSpec(
            num_scalar_prefetch=prefetch, grid=kw.pop("grid"), in_specs=kw.pop("in_specs"),
            out_specs=kw.pop("out_specs"), scratch_shapes=kw.pop("scratch_shapes", []))
        return pl.pallas_call(body, grid_spec=grid_spec, **kw)

    if not _pending_rider:
        return build(body, kw)
    rider = _pending_rider.pop()
    grid = kw.get("grid", ())
    in_specs = list(kw.get("in_specs", []))
    single = not isinstance(kw["out_shape"], (list, tuple))
    out_shape = [kw["out_shape"]] if single else list(kw["out_shape"])
    out_specs = [kw["out_specs"]] if single else list(kw["out_specs"])
    scratch = list(kw.get("scratch_shapes", []))
    n_in, n_out, n_scr = len(in_specs), len(out_shape), len(scratch)
    r_in, r_out = len(rider.operands), len(rider.out_shapes)

    def wrapped(*refs):
        scalars, refs = refs[:prefetch], refs[prefetch:]
        ins, rins = refs[:n_in], refs[n_in:n_in + r_in]
        outs = refs[n_in + r_in:n_in + r_in + n_out]
        routs = refs[n_in + r_in + n_out:n_in + r_in + n_out + r_out]
        scr = refs[n_in + r_in + n_out + r_out:n_in + r_in + n_out + r_out + n_scr]
        send, recv = refs[-2:]
        ids = [pl.program_id(a) for a in range(len(grid))]
        first, last = True, True
        for a, pid in enumerate(ids):
            first = jnp.logical_and(first, pid == 0)
            last = jnp.logical_and(last, pid == grid[a] - 1)
        if grid:
            pl.when(first)(lambda: rider.start(rins, routs, send, recv))
        else:
            rider.start(rins, routs, send, recv)
        body(*scalars, *ins, *outs, *scr)
        if grid:
            pl.when(last)(lambda: rider.finish(rins, routs, send, recv))
        else:
            rider.finish(rins, routs, send, recv)

    any_spec = pl.BlockSpec(memory_space=pl.ANY)
    kw2 = dict(kw)
    kw2.update(
        in_specs=in_specs + [any_spec] * r_in, out_specs=out_specs + [any_spec] * r_out,
        out_shape=out_shape + rider.out_shapes,
        scratch_shapes=scratch + [pltpu.SemaphoreType.DMA((rider.n_copies,)),
                                  pltpu.SemaphoreType.DMA((rider.n_copies,))],
        input_output_aliases={**kw.get("input_output_aliases", {}),
                              **{prefetch + n_in + i: n_out + o for i, o in rider.aliases.items()}})
    if grid:
        kw2["compiler_params"] = _params(("arbitrary",) * len(grid))
    call = build(wrapped, kw2)

    def run(*operands):
        res = call(*operands, *rider.operands)
        rider.results = list(res[n_out:])
        return res[0] if single else list(res[:n_out])

    return run


def _params(sem):
    return pltpu.CompilerParams(dimension_semantics=sem, vmem_limit_bytes=VMEM_LIMIT)


def _dot(a, b, mode):
    return lax.dot_general(a, b, _DN[mode], preferred_element_type=F32)


def _mm(name, mode, grid, a, a_spec, b, b_spec, out_shape, o_spec, acc_shape):
    nk = grid[-1]

    def body(a_ref, b_ref, o_ref, *acc):
        part = _dot(a_ref[...].astype(BF16), b_ref[...].astype(BF16), mode)
        if nk == 1:
            o_ref[...] = part.astype(o_ref.dtype)
            return
        acc_ref, = acc
        k = pl.program_id(len(grid) - 1)

        @pl.when(k == 0)
        def _():
            acc_ref[...] = part

        @pl.when(k > 0)
        def _():
            acc_ref[...] += part

        @pl.when(k == nk - 1)
        def _():
            o_ref[...] = acc_ref[...].astype(o_ref.dtype)

    scratch = [] if nk == 1 else [pltpu.VMEM(acc_shape, F32)]
    sem = ("parallel",) * (len(grid) - 1) + ("arbitrary",)
    return _pcall(body, name=name, grid=grid, in_specs=[a_spec, b_spec], out_specs=o_spec, out_shape=out_shape,
                  scratch_shapes=scratch, compiler_params=_params(sem))(a, b)


def _mm_tn(name, a, b, tm, tk):
    T, M = a.shape
    N = b.shape[1]
    return _mm(name, "tn", (M // tm, T // tk), a, pl.BlockSpec((tk, tm), lambda i, k: (k, i)),
               b, pl.BlockSpec((tk, N), lambda i, k: (k, 0)),
               jax.ShapeDtypeStruct((M, N), BF16), pl.BlockSpec((tm, N), lambda i, k: (i, 0)), (tm, N))


def _norm_bwd_rows(xf, gain, dh, dres):
    r = lax.rsqrt(jnp.mean(xf * xf, axis=-1, keepdims=True) + RMS_EPS)
    xh = xf * r
    t = dh * gain
    dx = dres + r * (t - xh * jnp.mean(t * xh, axis=-1, keepdims=True))
    return dx, jnp.sum(dh * xh, axis=0, keepdims=True)


def _accumulate(ref, value, first):
    @pl.when(first)
    def _():
        ref[...] = value

    @pl.when(jnp.logical_not(first))
    def _():
        ref[...] += value


def _rms_bwd_folded(name, x, g, dhf, dres, tb=512):
    S, D = x.shape

    def body(x_ref, g_ref, d0_ref, d1_ref, d2_ref, dres_ref, dx_ref, dg_ref, dh_ref):
        chunks = _lane_chunks(D)
        for c, cols in enumerate(chunks):
            dh_ref[c] = d0_ref[0, :, cols].astype(F32)
        for dil, ref in ((ATTN_DILATIONS[1], d1_ref), (ATTN_DILATIONS[2], d2_ref)):
            n = tb // dil
            for k in range(dil):
                for c, cols in enumerate(chunks):
                    dh_ref[c, _strided_rows(k, n, dil), :] += ref[k, :, cols].astype(F32)
        dh = jnp.concatenate([dh_ref[c] for c in range(len(chunks))], axis=1)
        dx, dg = _norm_bwd_rows(x_ref[...], g_ref[...], dh, dres_ref[...])
        dx_ref[...] = dx
        _accumulate(dg_ref, dg, pl.program_id(0) == 0)

    views, specs = _folded_views(dhf, tb)
    row = pl.BlockSpec((tb, D), lambda i: (i, 0))
    vec = pl.BlockSpec((1, D), lambda i: (0, 0))
    return _pcall(body, name=name, grid=(S // tb,), in_specs=[row, vec] + specs + [row], out_specs=[row, vec],
                  out_shape=[jax.ShapeDtypeStruct((S, D), F32), jax.ShapeDtypeStruct((1, D), F32)],
                  scratch_shapes=[pltpu.VMEM((D // LANES, tb, LANES), F32)],
                  compiler_params=_params(("arbitrary",)))(x, g, *views, dres)


def _proj_res_norm(name, a, w, res, gain, tm=512):
    M, K = a.shape
    D = w.shape[1]

    def body(a_ref, w_ref, res_ref, g_ref, x_ref, h_ref):
        xf = res_ref[...] + _dot(a_ref[...], w_ref[...], "nn")
        x_ref[...] = xf
        r = lax.rsqrt(jnp.mean(xf * xf, axis=-1, keepdims=True) + RMS_EPS)
        h_ref[...] = ((xf * r) * g_ref[...]).astype(h_ref.dtype)

    row = pl.BlockSpec((tm, D), lambda i: (i, 0))
    return _pcall(body, name=name, grid=(M // tm,),
                  in_specs=[pl.BlockSpec((tm, K), lambda i: (i, 0)), pl.BlockSpec((K, D), lambda i: (0, 0)), row,
                            pl.BlockSpec((1, D), lambda i: (0, 0))],
                  out_specs=[row, row],
                  out_shape=[jax.ShapeDtypeStruct((M, D), F32), jax.ShapeDtypeStruct((M, D), BF16)],
                  compiler_params=_params(("parallel",)))(a, w, res, gain)


def _proj_res_loss(name, a, w, res, gain, tgt, tm=512):
    M, K = a.shape
    D = w.shape[1]

    def body(a_ref, w_ref, res_ref, g_ref, t_ref, loss_ref, dx_ref, dxb_ref, dg_ref):
        first = pl.program_id(0) == 0
        xf = res_ref[...] + _dot(a_ref[...], w_ref[...], "nn")
        r = lax.rsqrt(jnp.mean(xf * xf, axis=-1, keepdims=True) + RMS_EPS)
        xh = xf * r
        gv = g_ref[...]
        e = xh * gv - t_ref[...]
        lp = 0.5 * jnp.sum(jnp.mean(e * e, axis=-1, keepdims=True), axis=0, keepdims=True)
        dy = e * (1.0 / D)
        t = dy * gv
        dx = r * (t - xh * jnp.mean(t * xh, axis=-1, keepdims=True))
        dx_ref[...] = dx
        dxb_ref[...] = dx.astype(dxb_ref.dtype)
        _accumulate(dg_ref, jnp.sum(dy * xh, axis=0, keepdims=True), first)
        _accumulate(loss_ref, lp, first)

    row = pl.BlockSpec((tm, D), lambda i: (i, 0))
    vec = pl.BlockSpec((1, D), lambda i: (0, 0))
    return _pcall(body, name=name, grid=(M // tm,),
                  in_specs=[pl.BlockSpec((tm, K), lambda i: (i, 0)), pl.BlockSpec((K, D), lambda i: (0, 0)), row,
                            vec, row],
                  out_specs=[pl.BlockSpec((1, 1), lambda i: (0, 0)), row, row, vec],
                  out_shape=[jax.ShapeDtypeStruct((1, 1), F32), jax.ShapeDtypeStruct((M, D), F32),
                             jax.ShapeDtypeStruct((M, D), BF16), jax.ShapeDtypeStruct((1, D), F32)],
                  compiler_params=_params(("arbitrary",)))(a, w, res, gain, tgt)


def _dh_norm_bwd(name, d, w, x, gain, dres, tm=512):
    M, N = d.shape
    D = w.shape[0]

    def body(d_ref, w_ref, x_ref, g_ref, dres_ref, dx_ref, dxb_ref, dg_ref):
        dh = _dot(d_ref[...].astype(BF16), w_ref[...], "nt")
        dx, dg = _norm_bwd_rows(x_ref[...], g_ref[...], dh, dres_ref[...])
        dx_ref[...] = dx
        dxb_ref[...] = dx.astype(dxb_ref.dtype)
        _accumulate(dg_ref, dg, pl.program_id(0) == 0)

    row = pl.BlockSpec((tm, D), lambda i: (i, 0))
    vec = pl.BlockSpec((1, D), lambda i: (0, 0))
    return _pcall(body, name=name, grid=(M // tm,),
                  in_specs=[pl.BlockSpec((tm, N), lambda i: (i, 0)), pl.BlockSpec((D, N), lambda i: (0, 0)), row, vec,
                            row],
                  out_specs=[row, row, vec],
                  out_shape=[jax.ShapeDtypeStruct((M, D), F32), jax.ShapeDtypeStruct((M, D), BF16),
                             jax.ShapeDtypeStruct((1, D), F32)],
                  compiler_params=_params(("arbitrary",)))(d, w, x, gain, dres)


def _sigmoid(v):
    return 0.5 * jnp.tanh(0.5 * v) + 0.5


def _ffn_up(name, h, w_gu, tm=1024):
    S, D = h.shape
    W = FF_SHARD

    def body(h_ref, wg_ref, wu_ref, gu_ref, act_ref):
        hv = h_ref[...]
        gate = _dot(hv, wg_ref[...], "nn")
        up = _dot(hv, wu_ref[...], "nn")
        gu_ref[0] = gate.astype(gu_ref.dtype)
        gu_ref[1] = up.astype(gu_ref.dtype)
        sig = _sigmoid(gate)
        act_ref[...] = ((gate * sig) * up).astype(act_ref.dtype)

    return _pcall(
        body, name=name, grid=(2, S // tm),
        in_specs=[pl.BlockSpec((tm, D), lambda j, i: (i, 0)),
                  pl.BlockSpec((None, D, W), lambda j, i: (j, 0, 0)),
                  pl.BlockSpec((None, D, W), lambda j, i: (j + 2, 0, 0))],
        out_specs=[pl.BlockSpec((2, tm, W), lambda j, i: (0, i, j)), pl.BlockSpec((tm, W), lambda j, i: (i, j))],
        out_shape=[jax.ShapeDtypeStruct((2, S, D_FF), BF16), jax.ShapeDtypeStruct((S, D_FF), BF16)],
        compiler_params=_params(("parallel", "parallel")))(h, w_gu, w_gu)


def _ffn_down_bwd(name, dx, w_down, gu, tm=1024):
    S, D = dx.shape
    W = FF_SHARD

    def body(dx_ref, w_ref, gu_ref, dgu_ref):
        dact = _dot(dx_ref[...].astype(BF16), w_ref[...], "nt")
        gate = gu_ref[0].astype(F32)
        up = gu_ref[1].astype(F32)
        sig = _sigmoid(gate)
        silu = gate * sig
        dgu_ref[0] = (dact * up * (sig * (1.0 + gate * (1.0 - sig)))).astype(dgu_ref.dtype)
        dgu_ref[1] = (dact * silu).astype(dgu_ref.dtype)

    blk = pl.BlockSpec((2, tm, W), lambda j, i: (0, i, j))
    return _pcall(
        body, name=name, grid=(2, S // tm),
        in_specs=[pl.BlockSpec((tm, D), lambda j, i: (i, 0)), pl.BlockSpec((W, D), lambda j, i: (j, 0)), blk],
        out_specs=blk, out_shape=jax.ShapeDtypeStruct((2, S, D_FF), BF16),
        compiler_params=_params(("parallel", "parallel")))(dx, w_down, gu)


def _ffn_dw_up(name, h, dgu, tk=2048):
    S, D = h.shape
    W = FF_SHARD
    tk = min(tk, S)
    return _mm(name, "tn", (N_CHIPS, S // tk), h, pl.BlockSpec((tk, D), lambda s, k: (k, 0)),
               dgu, pl.BlockSpec((None, tk, W), lambda s, k: (s // 2, k, s % 2)),
               jax.ShapeDtypeStruct((N_CHIPS, D, W), BF16), pl.BlockSpec((None, D, W), lambda s, k: (s, 0, 0)),
               (D, W))


def _ffn_dh(name, dgu, w_gu, x, gain, dres, tm=512):
    S = dgu.shape[1]
    W = FF_SHARD

    def body(a_ref, w_hbm, x_ref, g_ref, dres_ref, dx_ref, dg_ref, w_ref, acat_ref, sems):
        first = pl.program_id(0) == 0

        @pl.when(first)
        def _():
            copies = [pltpu.make_async_copy(w_hbm.at[s], w_ref.at[:, pl.ds(s * W, W)], sems.at[s])
                      for s in range(N_CHIPS)]
            for cp in copies:
                cp.start()
            for cp in copies:
                cp.wait()

        acat_ref[:, :D_FF] = a_ref[0]
        acat_ref[:, D_FF:] = a_ref[1]
        dh = _dot(acat_ref[...], w_ref[...], "nt")
        dx, dg = _norm_bwd_rows(x_ref[...], g_ref[...], dh, dres_ref[...])
        dx_ref[...] = dx
        _accumulate(dg_ref, dg, first)

    row = pl.BlockSpec((tm, D_MODEL), lambda i: (i, 0))
    vec = pl.BlockSpec((1, D_MODEL), lambda i: (0, 0))
    return _pcall(body, name=name, grid=(S // tm,),
                  in_specs=[pl.BlockSpec((2, tm, D_FF), lambda i: (0, i, 0)), pl.BlockSpec(memory_space=pl.ANY),
                            row, vec, row],
                  out_specs=[row, vec],
                  out_shape=[jax.ShapeDtypeStruct((S, D_MODEL), F32), jax.ShapeDtypeStruct((1, D_MODEL), F32)],
                  scratch_shapes=[pltpu.VMEM((D_MODEL, 2 * D_FF), BF16), pltpu.VMEM((tm, 2 * D_FF), BF16),
                                  pltpu.SemaphoreType.DMA((N_CHIPS,))],
                  compiler_params=_params(("arbitrary",)))(dgu, w_gu, x, gain, dres)


FOLD_TOKENS = 1024
LANES = 128


def _lane_chunks(width):
    return [slice(c * LANES, (c + 1) * LANES) for c in range(width // LANES)]


def _strided_rows(r, n, dil):
    return pl.ds(r, n) if dil == 1 else pl.ds(r, n, stride=dil)


def _norm_fold(name, x, gain):
    S, D = x.shape
    chunks = _lane_chunks(D)

    def body(x_ref, g_ref, hf_hbm, xc_ref, hs_ref, sems):
        i = pl.program_id(0)
        xf = x_ref[...]
        inv = lax.rsqrt(jnp.mean(xf * xf, axis=-1, keepdims=True) + RMS_EPS)
        h = (xf * inv) * g_ref[...]
        hs_ref[0] = h.astype(BF16)
        for c, cols in enumerate(chunks):
            xc_ref[c] = h[:, cols]
        copies = []
        for g, dil in enumerate(ATTN_DILATIONS):
            n = FOLD_TOKENS // dil
            for r in range(dil):
                if dil > 1:
                    for c, cols in enumerate(chunks):
                        hs_ref[g, r * n:(r + 1) * n, cols] = xc_ref[c, _strided_rows(r, n, dil), :].astype(BF16)
                cp = pltpu.make_async_copy(hs_ref.at[g, pl.ds(r * n, n)],
                                           hf_hbm.at[g, pl.ds(r * (S // dil) + i * n, n)], sems.at[len(copies)])
                cp.start()
                copies.append(cp)
        for cp in copies:
            cp.wait()

    return _pcall(body, name=name, grid=(S // FOLD_TOKENS,),
                  in_specs=[pl.BlockSpec((FOLD_TOKENS, D), lambda i: (i, 0)), pl.BlockSpec((1, D), lambda i: (0, 0))],
                  out_specs=pl.BlockSpec(memory_space=pl.ANY),
                  out_shape=jax.ShapeDtypeStruct((N_GROUPS, S, D), BF16),
                  scratch_shapes=[pltpu.VMEM((D // LANES, FOLD_TOKENS, LANES), F32),
                                  pltpu.VMEM((N_GROUPS, FOLD_TOKENS, D), BF16),
                                  pltpu.SemaphoreType.DMA((sum(ATTN_DILATIONS),))],
                  compiler_params=_params(("arbitrary",)))(x, gain)


def _qkv_tiles(name, piece, hf, w, qkv, chip, tm=1024):
    S = hf.shape[1]
    per_group = 3 * D_MODEL // QKV_TILE

    def tile(q, c_ref):
        if piece is None:
            return QKV_PIECES * c_ref[0] + q
        return QKV_PIECES * ((c_ref[0] + 1 + q) % N_CHIPS) + piece

    def body(*refs):
        a_ref, w_ref, o_ref = refs[1], refs[2], refs[-1]
        o_ref[...] = _dot(a_ref[...], w_ref[...], "nn").astype(o_ref.dtype)

    if piece is None:
        w_spec = pl.BlockSpec((D_MODEL, QKV_TILE), lambda q, i, c_ref: (0, q))
    else:
        w_spec = pl.BlockSpec((None, D_MODEL, QKV_TILE), lambda q, i, c_ref: ((c_ref[0] + 1 + q) % N_CHIPS, 0, 0))
    in_specs = [pl.BlockSpec((None, tm, D_MODEL), lambda q, i, c_ref: (tile(q, c_ref) // per_group, i, 0)), w_spec]
    operands = [chip, hf, w]
    aliases = {}
    if qkv is not None:
        in_specs.append(pl.BlockSpec(memory_space=pl.ANY))
        operands.append(qkv)
        aliases = {3: 0}
    return _pcall(
        body, prefetch=1, name=name, grid=(N_CHIPS - 1, S // tm), in_specs=in_specs,
        out_specs=pl.BlockSpec((None, tm, QKV_TILE),
                               lambda q, i, c_ref: (tile(q, c_ref) // per_group, i, tile(q, c_ref) % per_group)),
        out_shape=jax.ShapeDtypeStruct((N_GROUPS, S, 3 * D_MODEL), BF16), input_output_aliases=aliases,
        compiler_params=_params(("arbitrary", "arbitrary")))(*operands)


QKV_PIECES = QKV_SHARD // QKV_TILE


def _qkv_dw(name, p, hf, dqkv):
    S = hf.shape[1]
    per_group = 3 * D_MODEL // QKV_TILE
    tile = lambda s: QKV_PIECES * s + p
    return _mm(name, "tn", (N_CHIPS, 1),
               hf, pl.BlockSpec((None, S, D_MODEL), lambda s, k: (tile(s) // per_group, 0, 0)),
               dqkv, pl.BlockSpec((None, S, QKV_TILE), lambda s, k: (tile(s) // per_group, 0, tile(s) % per_group)),
               jax.ShapeDtypeStruct((N_CHIPS, D_MODEL, QKV_TILE), BF16),
               pl.BlockSpec((None, D_MODEL, QKV_TILE), lambda s, k: (s, 0, 0)), None)


def _qkv_dh(name, g, dqkv, w_pieces, dhf, tm=1024):
    S = dqkv.shape[1]
    per_group = 3 * D_MODEL // QKV_TILE

    def body(*refs):
        a_ref, w_hbm = refs[0], refs[1:1 + QKV_PIECES]
        o_ref, w_ref, sems = refs[-3:]

        @pl.when(pl.program_id(0) == 0)
        def _():
            copies = []
            for j in range(per_group):
                t = per_group * g + j
                copies.append(pltpu.make_async_copy(w_hbm[t % QKV_PIECES].at[t // QKV_PIECES],
                                                    w_ref.at[:, pl.ds(j * QKV_TILE, QKV_TILE)], sems.at[j]))
            for cp in copies:
                cp.start()
            for cp in copies:
                cp.wait()

        o_ref[...] = _dot(a_ref[...], w_ref[...], "nt").astype(o_ref.dtype)

    any_spec = pl.BlockSpec(memory_space=pl.ANY)
    in_specs = [pl.BlockSpec((None, tm, 3 * D_MODEL), lambda i: (g, i, 0))] + [any_spec] * QKV_PIECES
    operands = [dqkv] + list(w_pieces)
    aliases = {}
    if dhf is not None:
        in_specs.append(any_spec)
        operands.append(dhf)
        aliases = {1 + QKV_PIECES: 0}
    return _pcall(body, name=name, grid=(S // tm,), in_specs=in_specs,
                  out_specs=pl.BlockSpec((None, tm, D_MODEL), lambda i: (g, i, 0)),
                  out_shape=jax.ShapeDtypeStruct((N_GROUPS, S, D_MODEL), BF16),
                  scratch_shapes=[pltpu.VMEM((D_MODEL, 3 * D_MODEL), BF16), pltpu.SemaphoreType.DMA((per_group,))],
                  input_output_aliases=aliases, compiler_params=_params(("arbitrary",)))(*operands)


def _alibi_coef(g, h):
    vals = [-(2.0 ** (-8.0 * (gg * N_HEADS + h + 1) / (N_GROUPS * N_HEADS))) * ATTN_DILATIONS[gg]
            for gg in range(N_GROUPS)]
    return jnp.where(g == 0, vals[0], jnp.where(g == 1, vals[1], vals[2])).astype(F32)


def _blocks_per_segment(g, S):
    return jnp.right_shift(S // Q_BLOCK, 2 * g)


def _band_bias(pen):
    row = lax.broadcasted_iota(jnp.int32, (Q_BLOCK, 2 * Q_BLOCK), 0)
    col = lax.broadcasted_iota(jnp.int32, (Q_BLOCK, 2 * Q_BLOCK), 1)
    dist = Q_BLOCK + row - col
    valid = jnp.logical_and(dist >= 0, dist <= Q_BLOCK)
    base = jnp.where(valid, jnp.where(col < Q_BLOCK, pen, 0.0), NEG).astype(F32)
    return base, dist.astype(F32)


def _skewed(n_units, stages):
    state = {}
    for step in range(n_units + len(stages) - 1):
        for s, stage in enumerate(stages):
            u = step - s
            if 0 <= u < n_units:
                state[u] = stage(u, state.get(u))
                if s == len(stages) - 1:
                    del state[u]


def _attn_fwd(name, qkv, tq=1024):
    S = qkv.shape[1]
    nqb = tq // Q_BLOCK
    scale = HEAD_DIM ** -0.5

    def body(q_ref, k_ref, kp_ref, v_ref, vp_ref, o_ref, lse_ref, kf_ref, vf_ref):
        g = pl.program_id(0)
        i = pl.program_id(1)
        nb = _blocks_per_segment(g, S)
        kf_ref[:Q_BLOCK] = kp_ref[...]
        kf_ref[Q_BLOCK:] = k_ref[...]
        vf_ref[:Q_BLOCK] = vp_ref[...]
        vf_ref[Q_BLOCK:] = v_ref[...]
        coefs = [_alibi_coef(g, h) for h in range(N_HEADS)]
        units = [(b, h) for b in range(nqb) for h in range(N_HEADS)]
        bias = {}

        def scores(u, _):
            b, h = units[u]
            if b not in bias:
                pen = jnp.where((i * nqb + b) % nb != 0, 0.0, NEG).astype(F32)
                bias.clear()
                bias[b] = _band_bias(pen)
            base, dist = bias[b]
            cols = slice(h * HEAD_DIM, (h + 1) * HEAD_DIM)
            q = q_ref[b * Q_BLOCK:(b + 1) * Q_BLOCK, cols]
            kk = kf_ref[b * Q_BLOCK:(b + 2) * Q_BLOCK, cols]
            return _dot(q, kk, "nt") * scale + (coefs[h] * dist + base)

        def softmax(u, s):
            m = jnp.max(s, axis=-1, keepdims=True)
            p = jnp.exp(s - m)
            den = jnp.sum(p, axis=-1, keepdims=True)
            return (p * (1.0 / den)).astype(BF16), m + jnp.log(den)

        def output(u, st):
            b, h = units[u]
            pn, lse = st
            cols = slice(h * HEAD_DIM, (h + 1) * HEAD_DIM)
            rows = slice(b * Q_BLOCK, (b + 1) * Q_BLOCK)
            o_ref[rows, cols] = _dot(pn, vf_ref[b * Q_BLOCK:(b + 2) * Q_BLOCK, cols], "nn").astype(o_ref.dtype)
            lse_ref[rows, h:h + 1] = lse

        _skewed(len(units), [scores, softmax, output])

    main = lambda c: pl.BlockSpec((None, tq, D_MODEL), lambda g, i: (g, i, c))
    prev = lambda c: pl.BlockSpec((None, Q_BLOCK, D_MODEL), lambda g, i: (g, jnp.maximum(i * nqb - 1, 0), c))
    return _pcall(
        body, name=name, grid=(N_GROUPS, S // tq),
        in_specs=[main(0), main(1), prev(1), main(2), prev(2)],
        out_specs=[pl.BlockSpec((None, tq, D_MODEL), lambda g, i: (g, i, 0)),
                   pl.BlockSpec((None, tq, N_HEADS), lambda g, i: (g, i, 0))],
        out_shape=[jax.ShapeDtypeStruct((N_GROUPS, S, D_MODEL), BF16),
                   jax.ShapeDtypeStruct((N_GROUPS, S, N_HEADS), F32)],
        scratch_shapes=[pltpu.VMEM((tq + Q_BLOCK, D_MODEL), BF16), pltpu.VMEM((tq + Q_BLOCK, D_MODEL), BF16)],
        compiler_params=_params(("parallel", "parallel")))(qkv, qkv, qkv, qkv, qkv)


def _attn_bwd(name, qkv, do, lse, dl, tq=1024):
    S = qkv.shape[1]
    nqb = tq // Q_BLOCK
    n_blocks = S // Q_BLOCK
    scale = HEAD_DIM ** -0.5
    KEYS = 2 * Q_BLOCK

    def body(q_ref, k_ref, v_ref, kp_ref, vp_ref, qn_ref, do_ref, don_ref, lse_ref, lsen_ref, dl_ref, dln_ref,
             out_ref, kf_ref, vf_ref, dk_ref, dv_ref):
        g = pl.program_id(0)
        i = pl.program_id(1)
        nb = _blocks_per_segment(g, S)
        kf_ref[:Q_BLOCK] = kp_ref[...]
        kf_ref[Q_BLOCK:] = k_ref[...]
        vf_ref[:Q_BLOCK] = vp_ref[...]
        vf_ref[Q_BLOCK:] = v_ref[...]
        coefs = [_alibi_coef(g, h) for h in range(N_HEADS)]
        units = [(h, b) for h in range(N_HEADS) for b in range(nqb + 1)]
        bias = {}

        def band(b):
            if b not in bias:
                blk = i * nqb + b
                ok = blk % nb != 0
                if b == nqb:
                    ok = jnp.logical_and(ok, blk < n_blocks)
                bias[b] = _band_bias(jnp.where(ok, 0.0, NEG).astype(F32))
            return bias[b]

        def operands(h, b):
            cols = slice(h * HEAD_DIM, (h + 1) * HEAD_DIM)
            if b == nqb:
                keys = slice(tq, tq + Q_BLOCK)
                return (qn_ref[:, cols], don_ref[:, cols], lsen_ref[:, h:h + 1], dln_ref[:, h:h + 1],
                        kf_ref[keys, cols], vf_ref[keys, cols])
            rows = slice(b * Q_BLOCK, (b + 1) * Q_BLOCK)
            keys = slice(b * Q_BLOCK, b * Q_BLOCK + KEYS)
            return (q_ref[rows, cols], do_ref[rows, cols], lse_ref[rows, h:h + 1], dl_ref[rows, h:h + 1],
                    kf_ref[keys, cols], vf_ref[keys, cols])

        def probs(u, _):
            h, b = units[u]
            q, do_b, lse_b, dl_b, kk, vv = operands(h, b)
            base, dist = band(b)
            if b == nqb:
                base, dist = base[:, :Q_BLOCK], dist[:, :Q_BLOCK]
            p = jnp.exp(_dot(q, kk, "nt") * scale + (coefs[h] * dist + base) - lse_b)
            return p, _dot(do_b, vv, "nt")

        def dscores(u, st):
            h, b = units[u]
            p, dp = st
            dl_b = operands(h, b)[3]
            return ((p * (dp - dl_b)) * scale).astype(BF16), p.astype(BF16)

        def grads(u, st):
            h, b = units[u]
            ds, pb = st
            q, do_b, _, _, kk, _ = operands(h, b)
            cols = slice(h * HEAD_DIM, (h + 1) * HEAD_DIM)
            if b < nqb:
                out_ref[b * Q_BLOCK:(b + 1) * Q_BLOCK, cols] = _dot(ds, kk, "nn").astype(out_ref.dtype)
            if b == 0:
                dk_ref[:Q_BLOCK] = _dot(ds[:, Q_BLOCK:], q, "tn")
                dv_ref[:Q_BLOCK] = _dot(pb[:, Q_BLOCK:], do_b, "tn")
                return None
            dk = _dot(ds, q, "tn")
            dv = _dot(pb, do_b, "tn")
            before = slice((b - 1) * Q_BLOCK, b * Q_BLOCK)
            dk_ref[before] += dk[:Q_BLOCK]
            dv_ref[before] += dv[:Q_BLOCK]
            if b < nqb:
                own = slice(b * Q_BLOCK, (b + 1) * Q_BLOCK)
                dk_ref[own] = dk[Q_BLOCK:]
                dv_ref[own] = dv[Q_BLOCK:]
            else:
                out_ref[:, D_MODEL + h * HEAD_DIM:D_MODEL + (h + 1) * HEAD_DIM] = dk_ref[...].astype(out_ref.dtype)
                out_ref[:, 2 * D_MODEL + h * HEAD_DIM:2 * D_MODEL + (h + 1) * HEAD_DIM] = (
                    dv_ref[...].astype(out_ref.dtype))
            return None

        _skewed(len(units), [probs, dscores, grads])

    nxt_blk = lambda i: jnp.minimum((i + 1) * nqb, n_blocks - 1)
    main = lambda c: pl.BlockSpec((None, tq, D_MODEL), lambda g, i: (g, i, c))
    prev = lambda c: pl.BlockSpec((None, Q_BLOCK, D_MODEL), lambda g, i: (g, jnp.maximum(i * nqb - 1, 0), c))
    row = pl.BlockSpec((None, tq, D_MODEL), lambda g, i: (g, i, 0))
    row_n = pl.BlockSpec((None, Q_BLOCK, D_MODEL), lambda g, i: (g, nxt_blk(i), 0))
    col = pl.BlockSpec((None, tq, N_HEADS), lambda g, i: (g, i, 0))
    col_n = pl.BlockSpec((None, Q_BLOCK, N_HEADS), lambda g, i: (g, nxt_blk(i), 0))
    return _pcall(
        body, name=name, grid=(N_GROUPS, S // tq),
        in_specs=[main(0), main(1), main(2), prev(1), prev(2), row_n, row, row_n, col, col_n, col, col_n],
        out_specs=pl.BlockSpec((None, tq, 3 * D_MODEL), lambda g, i: (g, i, 0)),
        out_shape=jax.ShapeDtypeStruct((N_GROUPS, S, 3 * D_MODEL), BF16),
        scratch_shapes=[pltpu.VMEM((tq + Q_BLOCK, D_MODEL), BF16), pltpu.VMEM((tq + Q_BLOCK, D_MODEL), BF16),
                        pltpu.VMEM((tq, HEAD_DIM), F32), pltpu.VMEM((tq, HEAD_DIM), F32)],
        compiler_params=_params(("parallel", "parallel")))(qkv, qkv, qkv, qkv, qkv, qkv, do, do, lse, lse, dl, dl)


def _group_weights(lse_ref):
    l0, l1, l2 = lse_ref[0], lse_ref[1], lse_ref[2]
    m = jnp.maximum(jnp.maximum(l0, l1), l2)
    e = [jnp.exp(l0 - m), jnp.exp(l1 - m), jnp.exp(l2 - m)]
    den = e[0] + e[1] + e[2]
    return [ei / den for ei in e]


MERGE_TOKENS = 512


def _folded_views(a, tb):
    _, S, C = a.shape
    views, specs = [], []
    for g, dil in enumerate(ATTN_DILATIONS):
        views.append(a.reshape(N_GROUPS * dil, S // dil, C))
        specs.append(pl.BlockSpec((dil, tb // dil, C), lambda i, g=g: (g, i, 0)))
    return views, specs


def _unfold_into(dst_ref, folded_refs):
    for g, (dil, ref) in enumerate(zip(ATTN_DILATIONS, folded_refs)):
        n = ref.shape[1]
        for r in range(dil):
            for c, cols in enumerate(_lane_chunks(ref.shape[2])):
                dst_ref[g, c, _strided_rows(r, n, dil), :] = ref[r, :, cols].astype(F32)


def _merge_fwd(name, o, lse, tb=MERGE_TOKENS):
    S = o.shape[1]

    def body(o0_ref, o1_ref, o2_ref, lse_ref, out_ref, o_ref):
        _unfold_into(o_ref, (o0_ref, o1_ref, o2_ref))
        w = _group_weights(lse_ref)
        for h in range(N_HEADS):
            cols = slice(h * HEAD_DIM, (h + 1) * HEAD_DIM)
            acc = w[0][:, h:h + 1] * o_ref[0, h]
            for gg in range(1, N_GROUPS):
                acc = acc + w[gg][:, h:h + 1] * o_ref[gg, h]
            out_ref[:, cols] = acc.astype(out_ref.dtype)

    views, specs = _folded_views(o, tb)
    return _pcall(body, name=name, grid=(S // tb,),
                  in_specs=specs + [pl.BlockSpec((N_GROUPS, tb, N_HEADS), lambda i: (0, i, 0))],
                  out_specs=pl.BlockSpec((tb, D_MODEL), lambda i: (i, 0)),
                  out_shape=jax.ShapeDtypeStruct((S, D_MODEL), BF16),
                  scratch_shapes=[pltpu.VMEM((N_GROUPS, N_HEADS, tb, HEAD_DIM), F32)],
                  compiler_params=_params(("parallel",)))(*views, lse)


def _merge_bwd(name, dx, w_out, o, lse, tb=MERGE_TOKENS):
    S = o.shape[1]
    n_chunks = sum(ATTN_DILATIONS)

    def body(dx_ref, w_ref, o0_ref, o1_ref, o2_ref, lse_ref, do_hbm, dl_ref, o_ref, dt_ref, df_ref, d_ref, sems):
        i = pl.program_id(0)
        d_ref[...] = _dot(dx_ref[...].astype(BF16), w_ref[...], "nt")
        _unfold_into(o_ref, (o0_ref, o1_ref, o2_ref))
        w = _group_weights(lse_ref)
        for h in range(N_HEADS):
            cols = slice(h * HEAD_DIM, (h + 1) * HEAD_DIM)
            dv = d_ref[:, cols]
            merged = w[0][:, h:h + 1] * o_ref[0, h]
            for gg in range(1, N_GROUPS):
                merged = merged + w[gg][:, h:h + 1] * o_ref[gg, h]
            dsum = jnp.sum(dv * merged, axis=-1, keepdims=True)
            for gg in range(N_GROUPS):
                wg = w[gg][:, h:h + 1]
                dt_ref[gg, h] = wg * dv
                dl_ref[gg, :, h:h + 1] = wg * dsum
        copies = []
        for gg, dil in enumerate(ATTN_DILATIONS):
            n = tb // dil
            for r in range(dil):
                for h, cols in enumerate(_lane_chunks(D_MODEL)):
                    df_ref[gg, r * n:(r + 1) * n, cols] = (
                        dt_ref[gg, h, _strided_rows(r, n, dil), :].astype(df_ref.dtype))
                cp = pltpu.make_async_copy(df_ref.at[gg, pl.ds(r * n, n)],
                                           do_hbm.at[gg, pl.ds(r * (S // dil) + i * n, n)], sems.at[len(copies)])
                cp.start()
                copies.append(cp)
        for cp in copies:
            cp.wait()

    views, specs = _folded_views(o, tb)
    small = pl.BlockSpec((N_GROUPS, tb, N_HEADS), lambda i: (0, i, 0))
    return _pcall(body, name=name, grid=(S // tb,),
                  in_specs=[pl.BlockSpec((tb, D_MODEL), lambda i: (i, 0)),
                            pl.BlockSpec((D_MODEL, D_MODEL), lambda i: (0, 0))] + specs + [small],
                  out_specs=[pl.BlockSpec(memory_space=pl.ANY), small],
                  out_shape=[jax.ShapeDtypeStruct((N_GROUPS, S, D_MODEL), BF16),
                             jax.ShapeDtypeStruct((N_GROUPS, S, N_HEADS), F32)],
                  scratch_shapes=[pltpu.VMEM((N_GROUPS, N_HEADS, tb, HEAD_DIM), F32),
                                  pltpu.VMEM((N_GROUPS, N_HEADS, tb, HEAD_DIM), F32),
                                  pltpu.VMEM((N_GROUPS, tb, D_MODEL), BF16), pltpu.VMEM((tb, D_MODEL), F32),
                                  pltpu.SemaphoreType.DMA((n_chunks,))],
                  compiler_params=_params(("arbitrary",)))(dx, w_out, *views, lse)


def _shift_rows(a, k):
    return pltpu.roll(a, k % a.shape[0], axis=0)


def _pool_level(g, levels):
    return jnp.where(g == 0, levels[0], jnp.where(g == 1, levels[1], jnp.where(g == 2, levels[2], levels[3])))


def _pool_fwd(name, h, w_in, w_group, scale, x_in, gain_next, tr=1024):
    S, D = h.shape
    H = POOL_HALO
    n_groups = D // POOL_DIM

    def body(h_ref, hh_ref, wi_ref, w_ref, sc_ref, x_ref, gn_ref, y_ref, z_ref, xo_ref, hn_ref, xs_ref):
        i = pl.program_id(0)
        g = pl.program_id(1)
        uv = _dot(h_ref[...], wi_ref[...], "nn")
        halo = jnp.where(i > 0, _dot(hh_ref[...], wi_ref[...], "nn"), 0.0)
        s = jnp.concatenate([halo, uv], axis=0)
        levels = []
        for k in (1, 2, 4, 8):
            s = s + _shift_rows(s, k)
            levels.append(s[H:])
        t = i * tr + lax.broadcasted_iota(jnp.int32, (tr, 1), 0)
        cnt = jnp.minimum(t + 1, jnp.left_shift(2, g)).astype(F32)
        y = _pool_level(g, levels) / cnt - uv
        yb = y.astype(BF16)
        z = _dot(yb, w_ref[...], "nn")
        y_ref[...] = yb
        z_ref[...] = z.astype(z_ref.dtype)
        x_out = x_ref[...] + z * sc_ref[...]
        xo_ref[...] = x_out
        xs_ref[g] = x_out

        @pl.when(g == n_groups - 1)
        def _():
            xf = jnp.concatenate([xs_ref[k] for k in range(n_groups)], axis=1)
            r = lax.rsqrt(jnp.mean(xf * xf, axis=-1, keepdims=True) + RMS_EPS)
            hn_ref[...] = ((xf * r) * gn_ref[...]).astype(hn_ref.dtype)

    blk = pl.BlockSpec((tr, POOL_DIM), lambda i, g: (i, g))
    row = pl.BlockSpec((tr, D), lambda i, g: (i, 0))
    return _pcall(
        body, name=name, grid=(S // tr, n_groups),
        in_specs=[row, pl.BlockSpec((H, D), lambda i, g: (jnp.maximum(i * (tr // H) - 1, 0), 0)),
                  pl.BlockSpec((D, POOL_DIM), lambda i, g: (0, g)),
                  pl.BlockSpec((None, POOL_DIM, POOL_DIM), lambda i, g: (g, 0, 0)),
                  pl.BlockSpec((1, POOL_DIM), lambda i, g: (0, g)), blk, pl.BlockSpec((1, D), lambda i, g: (0, 0))],
        out_specs=[blk, blk, blk, row],
        out_shape=[jax.ShapeDtypeStruct((S, D), BF16), jax.ShapeDtypeStruct((S, D), BF16),
                   jax.ShapeDtypeStruct((S, D), F32), jax.ShapeDtypeStruct((S, D), BF16)],
        scratch_shapes=[pltpu.VMEM((n_groups, tr, POOL_DIM), F32)],
        compiler_params=_params(("parallel", "arbitrary")))(h, h, w_in, w_group, scale, x_in, gain_next)


def _pool_bwd(name, d_out, z, y, scale, w_group, tr=1024):
    S, D = d_out.shape
    H = POOL_HALO

    def body(d_ref, dn_ref, z_ref, y_ref, sc_ref, w_ref, du_ref, dw_ref, dsc_ref):
        g = pl.program_id(0)
        i = pl.program_id(1)
        dv = d_ref[...]
        dz = jnp.concatenate([dv, dn_ref[...]], axis=0) * sc_ref[...]
        dzb = dz.astype(BF16)
        dy = _dot(dzb, w_ref[...], "nt")
        t = i * tr + lax.broadcasted_iota(jnp.int32, (tr + H, 1), 0)
        cnt = jnp.minimum(t + 1, jnp.left_shift(2, g)).astype(F32)
        s = jnp.where(t < S, dy / cnt, 0.0)
        levels = []
        for k in (1, 2, 4, 8):
            s = s + _shift_rows(s, -k)
            levels.append(s[:tr])
        du_ref[...] = (_pool_level(g, levels) - dy[:tr]).astype(du_ref.dtype)
        dw = _dot(y_ref[...], dzb[:tr], "tn")
        dsc = jnp.sum(dv * z_ref[...].astype(F32), axis=0, keepdims=True)

        @pl.when(i == 0)
        def _():
            dw_ref[...] = dw
            dsc_ref[...] = dsc

        @pl.when(i > 0)
        def _():
            dw_ref[...] += dw
            dsc_ref[...] += dsc

    blk = pl.BlockSpec((tr, POOL_DIM), lambda g, i: (i, g))
    vec = pl.BlockSpec((1, POOL_DIM), lambda g, i: (0, g))
    mat = pl.BlockSpec((None, POOL_DIM, POOL_DIM), lambda g, i: (g, 0, 0))
    nxt = pl.BlockSpec((H, POOL_DIM), lambda g, i: (jnp.minimum((i + 1) * (tr // H), S // H - 1), g))
    return _pcall(
        body, name=name, grid=(D // POOL_DIM, S // tr), in_specs=[blk, nxt, blk, blk, vec, mat],
        out_specs=[blk, mat, vec],
        out_shape=[jax.ShapeDtypeStruct((S, D), BF16), jax.ShapeDtypeStruct((D // POOL_DIM, POOL_DIM, POOL_DIM), F32),
                   jax.ShapeDtypeStruct((1, D), F32)],
        compiler_params=_params(("parallel", "arbitrary")))(d_out, d_out, z, y, scale, w_group)


def _row_tile(rows, cols, target_bytes=1 << 20):
    best = rows
    for tr in range(8, rows + 1, 8):
        if rows % tr == 0 and tr * cols * 4 <= target_bytes:
            best = tr
    return best if best * cols * 4 <= 4 * target_bytes else rows


def _adamw_step(w, g, m, v):
    m2 = ADAM_B1 * m + (1.0 - ADAM_B1) * g
    v2 = ADAM_B2 * v + (1.0 - ADAM_B2) * (g * g)
    m_hat = m2 / (1.0 - ADAM_B1 ** ADAM_STEP)
    v_hat = v2 / (1.0 - ADAM_B2 ** ADAM_STEP)
    return -ADAM_LR * (m_hat / (jnp.sqrt(v_hat) + ADAM_EPS) + ADAM_WD * w), m2, v2


def _adamw_refs(w_ref, g_ref, m_ref, v_ref, go_ref, d_ref, mo_ref, vo_ref):
    gv = g_ref[...]
    d_ref[...], mo_ref[...], vo_ref[...] = _adamw_step(w_ref[...], gv, m_ref[...], v_ref[...])
    go_ref[...] = gv


def _adamw(name, w, g, m, v):
    R, C = w.shape
    tr = _row_tile(R, C) if R % 8 == 0 else R
    blk = pl.BlockSpec((tr, C), lambda i: (i, 0))
    sds = jax.ShapeDtypeStruct((R, C), F32)
    return _pcall(_adamw_refs, name=name, grid=(R // tr,), in_specs=[blk] * 4, out_specs=[blk] * 4,
                  out_shape=[sds] * 4, compiler_params=_params(("parallel",)))(w, g, m, v)


def _adamw_small(name, items):
    n = len(items)

    def body(*refs):
        for t in range(n):
            _adamw_refs(*refs[4 * t:4 * t + 4], *refs[4 * n + 4 * t:4 * n + 4 * t + 4])

    specs, shapes = [], []
    for w, _, _, _ in items:
        specs += [pl.BlockSpec(w.shape, lambda i: (0, 0))] * 4
        shapes += [jax.ShapeDtypeStruct(w.shape, F32)] * 4
    res = _pcall(body, name=name, grid=(1,), in_specs=specs, out_specs=specs, out_shape=shapes,
                 compiler_params=_params(("arbitrary",)))(*[a for item in items for a in item])
    return [res[4 * t:4 * t + 4] for t in range(n)]


def _sum_leading(name, a, out_dtype):
    n, R, C = a.shape
    tr = _row_tile(R, C) if R % 16 == 0 else R
    if tr % 16:
        tr = R

    def body(a_ref, o_ref):
        acc = a_ref[0].astype(F32)
        for j in range(1, n):
            acc = acc + a_ref[j].astype(F32)
        o_ref[...] = acc.astype(o_ref.dtype)

    return _pcall(body, name=name, grid=(R // tr,), in_specs=[pl.BlockSpec((n, tr, C), lambda i: (0, i, 0))],
                  out_specs=pl.BlockSpec((tr, C), lambda i: (i, 0)), out_shape=jax.ShapeDtypeStruct((R, C), out_dtype),
                  compiler_params=_params(("parallel",)))(a)


def _cast_many(name, items, chip, steps=4):
    tiles = []
    for w, _, dtype in items:
        R = w.shape[1]
        nt = steps if R % (steps * 16) == 0 else 1
        tiles.append((nt, R // nt))

    def body(c_ref, *refs):
        i = pl.program_id(0)
        for t, (nt, _) in enumerate(tiles):
            w_ref, o_ref = refs[t], refs[len(items) + t]

            def cast(w_ref=w_ref, o_ref=o_ref):
                o_ref[...] = w_ref[...].astype(o_ref.dtype)

            if nt == steps:
                cast()
            else:
                pl.when(i < nt)(cast)

    in_specs, out_specs, out_shape = [], [], []
    for (w, layer, dtype), (nt, tr) in zip(items, tiles):
        C = w.shape[2]
        in_specs.append(pl.BlockSpec((None, tr, C), lambda i, c_ref, l=layer, nt=nt: (l, jnp.minimum(i, nt - 1), 0)))
        out_specs.append(pl.BlockSpec((None, tr, C), lambda i, c_ref, nt=nt: (c_ref[0], jnp.minimum(i, nt - 1), 0)))
        out_shape.append(jax.ShapeDtypeStruct((N_CHIPS, w.shape[1], C), dtype))
    return _pcall(body, prefetch=1, name=name, grid=(steps,), in_specs=in_specs, out_specs=out_specs,
                  out_shape=out_shape, compiler_params=_params(("arbitrary",)))(chip, *[w for w, _, _ in items])


def _cast_qkv(name, w, chip, tr=256):
    _, R, C = w.shape

    def body(c_ref, w_ref, own_ref, *piece_refs):
        v = w_ref[...].astype(BF16)
        own_ref[...] = v
        for p, ref in enumerate(piece_refs):
            ref[...] = v[:, p * QKV_TILE:(p + 1) * QKV_TILE]

    piece = pl.BlockSpec((None, tr, QKV_TILE), lambda i, c_ref: (c_ref[0], i, 0))
    return _pcall(body, prefetch=1, name=name, grid=(R // tr,),
                  in_specs=[pl.BlockSpec((None, tr, C), lambda i, c_ref: (0, i, 0))],
                  out_specs=[pl.BlockSpec((tr, C), lambda i, c_ref: (i, 0))] + [piece] * QKV_PIECES,
                  out_shape=[jax.ShapeDtypeStruct((R, C), BF16)]
                  + [jax.ShapeDtypeStruct((N_CHIPS, R, QKV_TILE), BF16)] * QKV_PIECES,
                  compiler_params=_params(("parallel",)))(chip, w)


def _owner_sum(name, mine, landed, shape, chip, core, out, layer, col):
    _, half, C = mine.shape
    k, _ = col
    tr = _row_tile(half, C)
    if tr % 16:
        tr = half
    nrt = half // tr

    def body(ch_ref, co_ref, mine_ref, landed_ref, *rest):
        g = mine_ref[...].astype(F32)
        for j in range(3):
            g = g + landed_ref[j].astype(F32)
        rest[-1][...] = g

    piece = pl.BlockSpec((None, tr, C), lambda i, ch, co: (layer, co[0] * nrt + i, k))
    in_specs = [pl.BlockSpec((None, tr, C), lambda i, ch, co: (ch[0], i, 0)),
                pl.BlockSpec((3, tr, C), lambda i, ch, co: (0, i, 0))]
    operands = [chip, core, mine, landed]
    aliases = {}
    if out is not None:
        in_specs.append(ANY)
        operands.append(out)
        aliases = {4: 0}
    return _pcall(body, prefetch=2, name=name, grid=(nrt,), in_specs=in_specs, out_specs=piece,
                  out_shape=jax.ShapeDtypeStruct(shape, F32), input_output_aliases=aliases,
                  compiler_params=_params(("parallel",)))(*operands)


def _add_my_half(name, ps, qs, core):
    n = len(ps)

    def body(c_ref, *refs):
        for t in range(n):
            p_ref, q_ref, o_ref = refs[t], refs[n + t], refs[2 * n + t]
            o_ref[...] = (p_ref[...].astype(F32) + q_ref[...].astype(F32)).astype(o_ref.dtype)

    halves = [(p.shape[1] // 2, p.shape[2]) for p in ps]
    mine = [pl.BlockSpec((None, h, c), lambda s, c_ref: (s, c_ref[0], 0)) for h, c in halves]
    whole = [pl.BlockSpec((None, h, c), lambda s, c_ref: (s, 0, 0)) for h, c in halves]
    return _pcall(body, prefetch=1, name=name, grid=(N_CHIPS,), in_specs=mine + whole, out_specs=whole,
                  out_shape=[jax.ShapeDtypeStruct((N_CHIPS, h, c), BF16) for h, c in halves],
                  compiler_params=_params(("parallel",)))(core, *ps, *qs)


ANY = pl.BlockSpec(memory_space=pl.ANY)


def _place():
    x, y, c = lax.axis_index("x"), lax.axis_index("y"), lax.axis_index("c")
    other_chips = [(1 - x, y), (x, 1 - y), (1 - x, 1 - y)]
    return x, y, c, other_chips


def _remote(src, dst, send, recv, k, to):
    return pltpu.make_async_remote_copy(src_ref=src, dst_ref=dst, send_sem=send.at[k], recv_sem=recv.at[k],
                                        device_id=to, device_id_type=MESH)


def _in_place(bufs):
    return dict(operands=bufs, out_shapes=[jax.ShapeDtypeStruct(b.shape, b.dtype) for b in bufs],
                aliases={t: t for t in range(len(bufs))})


def _gather_rider(bufs, jobs):
    def copies(ins, outs, send, recv, base=0):
        x, y, c, chips = _place()
        out = []
        for t, over_ici, rows in jobs:
            ref = outs[t]
            if rows is not None:
                a, b, n = rows
                half = ref.shape[1] // 2
                rows = pl.ds(c * half + a * half // n, (b - a) * half // n)
            for px, py in chips:
                slot = 2 * x + y if over_ici else 2 * px + py
                piece = ref.at[slot] if rows is None else ref.at[slot, rows]
                out.append(_remote(piece, piece, send, recv, base + len(out), (px, py, c) if over_ici else (x, y, 1 - c)))
        return out

    return _Rider(n_copies=3 * len(jobs), copies=copies, **_in_place(bufs))


def _gather_and_pass_rider(buf):
    half = buf.shape[1] // 2

    def pieces(outs):
        x, y, c, chips = _place()
        rows = lambda core: pl.ds(core * half, half)
        mine = outs[0].at[2 * x + y, rows(c)]
        landed = [outs[0].at[2 * px + py, rows(c)] for px, py in chips]
        theirs = [outs[0].at[2 * px + py, rows(1 - c)] for px, py in chips]
        return (x, y, c, chips), mine, landed, theirs

    def start(ins, outs, send, recv, base=0):
        (x, y, c, chips), mine, _, _ = pieces(outs)
        for j, (px, py) in enumerate(chips):
            _remote(mine, mine, send, recv, base + j, (px, py, c)).start()

    def finish(ins, outs, send, recv, base=0):
        (x, y, c, chips), mine, landed, theirs = pieces(outs)
        sibling = (x, y, 1 - c)
        passed = []
        for j, (px, py) in enumerate(chips):
            _remote(landed[j], landed[j], send, recv, base + j, (px, py, c)).wait_recv()
            passed.append(_remote(landed[j], landed[j], send, recv, base + 3 + j, sibling))
            passed[-1].start()
        for j in range(3):
            _remote(theirs[j], theirs[j], send, recv, base + 3 + j, sibling).wait_recv()
        for j, (px, py) in enumerate(chips):
            _remote(mine, mine, send, recv, base + j, (px, py, c)).wait_send()
            passed[j].wait_send()

    return _Rider(n_copies=6, start=start, finish=finish, **_in_place([buf]))


def _swap_halves_rider(parts):
    n = len(parts)

    def copies(ins, outs, send, recv, base=0):
        x, y, c, _ = _place()
        out = []
        for t in range(n):
            half = ins[t].shape[1] // 2
            out.append(_remote(ins[t].at[:, pl.ds((1 - c) * half, half)], outs[t], send, recv, base + t,
                               (x, y, 1 - c)))
        return out

    shapes = [jax.ShapeDtypeStruct((p.shape[0], p.shape[1] // 2, p.shape[2]), p.dtype) for p in parts]
    return _Rider(parts, shapes, {}, n, copies)


def _scatter_rider(sums, landed, relations):
    n = len(sums)
    kept = [t for t in range(n) if landed[t] is not None]

    def copies(ins, outs, send, recv, base=0):
        x, y, c, chips = _place()
        out = []
        for t in range(n):
            for j in relations[t]:
                px, py = chips[j]
                out.append(_remote(ins[t].at[2 * px + py], outs[t].at[j], send, recv, base + len(out), (px, py, c)))
        return out

    shapes = [jax.ShapeDtypeStruct((3,) + s.shape[1:], s.dtype) for s in sums]
    return _Rider(list(sums) + [landed[t] for t in kept], shapes, {n + k: t for k, t in enumerate(kept)},
                  sum(len(r) for r in relations), copies)


def _share_rider(blocks, pieces):
    def copies(ins, outs, send, recv, base=0):
        x, y, c, _ = _place()
        out = []
        for k, (t, l, col) in enumerate(pieces):
            ref = outs[t].at[l]
            r2 = ref.shape[0] // 2
            width = ref.shape[1] // col[1]
            piece = ref.at[pl.ds(c * r2, r2), pl.ds(col[0] * width, width)]
            out.append(_remote(piece, piece, send, recv, base + k, (x, y, 1 - c)))
        return out

    return _Rider(n_copies=len(pieces), copies=copies, **_in_place(blocks))


def _gather_small(name, v):
    def body(v_ref, out_ref, send_sem, recv_sem, local_sem):
        x, y, c, _ = _place()
        me = 4 * x + 2 * y + c
        flips = [(fx, fy, fc) for fx in (0, 1) for fy in (0, 1) for fc in (0, 1)][1:]
        local = pltpu.make_async_copy(v_ref, out_ref.at[me], local_sem)
        local.start()
        copies = []
        for j, (fx, fy, fc) in enumerate(flips):
            peer = (x ^ fx, y ^ fy, c ^ fc)
            copies.append(pltpu.make_async_remote_copy(
                src_ref=v_ref, dst_ref=out_ref.at[me], send_sem=send_sem.at[j], recv_sem=recv_sem.at[j],
                device_id=peer, device_id_type=MESH))
        for cp in copies:
            cp.start()
        for j, (fx, fy, fc) in enumerate(flips):
            peer = (x ^ fx, y ^ fy, c ^ fc)
            pltpu.make_async_remote_copy(
                src_ref=v_ref, dst_ref=out_ref.at[4 * peer[0] + 2 * peer[1] + peer[2]], send_sem=send_sem.at[j],
                recv_sem=recv_sem.at[j], device_id=peer, device_id_type=MESH).wait_recv()
        for cp in copies:
            cp.wait_send()
        local.wait()

    return _pcall(body, name=name, in_specs=[ANY], out_specs=ANY,
                  out_shape=jax.ShapeDtypeStruct((8,) + v.shape, v.dtype),
                  scratch_shapes=[pltpu.SemaphoreType.DMA((7,)), pltpu.SemaphoreType.DMA((7,)),
                                  pltpu.SemaphoreType.DMA])(v)


def _fold(a, dil):
    if dil == 1:
        return a
    S, C = a.shape
    return a.reshape(S // dil, dil, C).transpose(1, 0, 2).reshape(S, C)


def _unfold(a, dil):
    if dil == 1:
        return a
    S, C = a.shape
    return a.reshape(dil, S // dil, C).transpose(1, 0, 2).reshape(S, C)


def _fold_groups(a):
    return jnp.stack([_fold(a[g] if a.ndim == 3 else a, d) for g, d in enumerate(ATTN_DILATIONS)])


def _unfold_groups(a):
    return jnp.stack([_unfold(a[g], d) for g, d in enumerate(ATTN_DILATIONS)])


class _NoComm:
    def __init__(self, weights):
        self.weights = weights
        self.grads = {}
        self.chip = jnp.zeros((1,), jnp.int32)

    def w(self, name):
        return self.weights[name]

    def run(self, fn, name, *args, **kw):
        return fn(name, *args, **kw)

    def grad(self, name, value):
        self.grads[name] = value


def _local_step(xs, tgt, attn_norm, ffn_norm, final_norm, comm):
    run = comm.run
    hf = run(_norm_fold, "fold", xs, attn_norm)
    qkv = run(_qkv_tiles, "qkv_own", None, hf, comm.w("wq_own"), None, comm.chip)
    for p in range(QKV_PIECES):
        qkv = run(_qkv_tiles, "qkv_piece%d" % p, p, hf, comm.w("wq%d" % p), qkv, comm.chip)
    o_f, lse_f = run(_attn_fwd, "attn_fwd", qkv)
    lse_t = _unfold_groups(lse_f)
    merged = run(_merge_fwd, "merge_fwd", o_f, lse_t)
    x1, h1 = run(_proj_res_norm, "attn_out", merged, comm.w("o"), xs, ffn_norm[0:1])
    gu0, act0 = run(_ffn_up, "ffn0_up", h1, comm.w("gu0"))
    x2, h2 = run(_proj_res_norm, "ffn0_down", act0, comm.w("d0"), x1, comm.w("pool_norm"))
    y, z, x3, h3 = run(_pool_fwd, "pool_fwd", h2, comm.w("p"), comm.w("pg"), comm.w("pool_scale"), x2, ffn_norm[1:2])
    gu1, act1 = run(_ffn_up, "ffn1_up", h3, comm.w("gu1"))
    loss, dx4, dx4_b, d_final = run(_proj_res_loss, "ffn1_down", act1, comm.w("d1"), x3, final_norm, tgt)

    dgu1 = run(_ffn_down_bwd, "ffn1_down_bwd", dx4_b, comm.w("d1"), gu1)
    comm.grad("d1", run(_mm_tn, "ffn1_down_dw", act1, dx4_b, FF_SHARD, 2048))
    comm.grad("gu1", run(_ffn_dw_up, "ffn1_up_dw", h3, dgu1))
    dx3, d_ffn1 = run(_ffn_dh, "ffn1_up_dh", dgu1, comm.w("gu1"), x3, ffn_norm[1:2], dx4)

    du, dw_pg, d_scale = run(_pool_bwd, "pool_bwd", dx3, z, y, comm.w("pool_scale"), comm.w("pg"))
    comm.grad("pg", dw_pg)
    comm.grad("p", run(_mm_tn, "pool_in_dw", h2, du, D_MODEL, h2.shape[0]))
    dx2, dx2_b, d_pool = run(_dh_norm_bwd, "pool_in_dh", du, comm.w("p"), x2, comm.w("pool_norm"), dx3)

    dgu0 = run(_ffn_down_bwd, "ffn0_down_bwd", dx2_b, comm.w("d0"), gu0)
    comm.grad("d0", run(_mm_tn, "ffn0_down_dw", act0, dx2_b, FF_SHARD, 2048))
    comm.grad("gu0", run(_ffn_dw_up, "ffn0_up_dw", h1, dgu0))
    dx1, d_ffn0 = run(_ffn_dh, "ffn0_up_dh", dgu0, comm.w("gu0"), x1, ffn_norm[0:1], dx2)

    comm.grad("o", run(_mm_tn, "attn_out_dw", merged, dx1, D_MODEL, 2048))
    do_f, dl_t = run(_merge_bwd, "merge_bwd", dx1, comm.w("o"), o_f, lse_t)
    dqkv = run(_attn_bwd, "attn_bwd", qkv, do_f, lse_f, _fold_groups(dl_t))
    for p in range(QKV_PIECES):
        comm.grad("qkv%d" % p, run(_qkv_dw, "qkv_dw%d" % p, p, hf, dqkv))
    dhf = None
    for g in range(N_GROUPS):
        dhf = run(_qkv_dh, "qkv_dh%d" % g, g, dqkv, [comm.w("wq%d" % p) for p in range(QKV_PIECES)], dhf)
    grad_x, d_attn = run(_rms_bwd_folded, "norm_attn_bwd", xs, attn_norm, dhf, dx1)

    small = dict(attn=d_attn, ffn0=d_ffn0, ffn1=d_ffn1, final=d_final, pool=d_pool, scale=d_scale)
    return loss, grad_x, small


TENSORS = ("qkv", "o", "p", "pg", "gu0", "gu1", "d0", "d1")


def _block_of(name):
    if name[:-1] in ("gu", "d"):
        return name[:-1], int(name[-1]), (0, 1)
    if name[:-1] == "qkv":
        return "qkv", 0, (int(name[-1]), QKV_PIECES)
    return name, 0, (0, 1)


class _MeshComm:
    def __init__(self, bufs, shapes, chip_arr, core_arr, pg_rows):
        self.bufs = dict(bufs)
        self.shapes = shapes
        self.chip, self.chip_arr, self.core_arr, self.pg_rows = chip_arr, chip_arr, core_arr, pg_rows
        self.parts, self.sums, self.blocks, self.landed, self.arrived = {}, {}, {}, {}, {}
        move = lambda ici=(), d2d=(): lambda: self.gather(ici, d2d)
        whole = lambda name: lambda: self.gather_and_pass(name)
        swap = lambda *names: lambda: self.swap(names)
        scatter = lambda *names: lambda: self.scatter(names)
        share = lambda *names: lambda: self.share(names)
        self.plan = {
            "fold": [whole("wq0")],
            "qkv_own": [whole("wq1")],
            "qkv_piece0": [whole("wq2")],
            "qkv_piece1": [move(ici=["o", "gu0[0:1/4]"])],
            "qkv_piece2": [move(ici=["gu0[1:2/4]"], d2d=["o", "gu0[0:1/4]"])],
            "attn_fwd": [move(ici=["gu0[2:4/4]", "d0[0:1/2]"], d2d=["gu0[1:2/4]"])],
            "merge_fwd": [move(ici=["p", "pg"], d2d=["gu0[2:4/4]", "d0[0:1/2]"])],
            "attn_out": [move(ici=["d0[1:2/2]", "pool_norm", "pool_scale"], d2d=["p", "pg"])],
            "ffn0_up": [move(ici=["gu1[0:3/4]"], d2d=["d0[1:2/2]"])],
            "ffn0_down": [move(ici=["gu1[3:4/4]", "d1[0:1/2]"], d2d=["gu1[0:3/4]"])],
            "pool_fwd": [move(ici=["d1[1:2/2]"], d2d=["gu1[3:4/4]", "d1[0:1/2]"])],
            "ffn1_up": [move(d2d=["d1[1:2/2]"])],
            "ffn1_up_dh": [swap("d1", "gu1")],
            "pool_bwd": [scatter("d1{01}")],
            "pool_in_dh": [scatter("d1{2}")],
            "ffn0_down_bwd": [scatter("gu1{01}")],
            "ffn0_down_dw": [scatter("gu1{2}")],
            "ffn0_up_dw": [share("gu1", "d1")],
            "ffn0_up_dh": [swap("pg", "p", "d0", "gu0")],
            "merge_bwd": [swap("o")],
            "attn_bwd": [scatter("gu0", "d0", "o")],
            "qkv_dw0": [share("d0"), scatter("p", "pg")],
            "qkv_dw1": [share("gu0", "o"), swap("qkv0")],
            "qkv_dw2": [share("p", "pg"), scatter("qkv0"), swap("qkv1")],
            "qkv_dh0": [share("qkv0"), scatter("qkv1"), swap("qkv2")],
            "qkv_dh1": [share("qkv1"), scatter("qkv2")],
            "qkv_dh2": [share("qkv2")],
        }

    def gather_and_pass(self, name):
        return _gather_and_pass_rider(self.bufs[name]), lambda res: self.bufs.update({name: res[0]})

    def gather(self, ici, d2d):
        names, jobs = [], []
        for over_ici, specs in ((True, ici), (False, d2d)):
            for spec in specs:
                name, _, rows = spec.partition("[")
                if name not in names:
                    names.append(name)
                rng = None
                if name in TENSORS:
                    ab, _, n = (rows[:-1] or "0:1/1").partition("/")
                    rng = (int(ab.split(":")[0]), int(ab.split(":")[1]), int(n))
                jobs.append((names.index(name), over_ici, rng))
        return _gather_rider([self.bufs[n] for n in names], jobs), lambda res: self.bufs.update(zip(names, res))

    def swap(self, names):
        def done(res):
            sums = _add_my_half("chip_sum_" + "_".join(names), [self.parts[n] for n in names], res, self.core_arr)
            self.sums.update(zip(names, sums))
        return _swap_halves_rider([self.parts[n] for n in names]), done

    def scatter(self, specs):
        names = [s.partition("{")[0] for s in specs]
        relations = [tuple(int(ch) for ch in s.partition("{")[2][:-1]) or (0, 1, 2) for s in specs]

        def done(res):
            for n, rel, landed in zip(names, relations, res):
                self.landed[n] = landed
                self.arrived[n] = self.arrived.get(n, ()) + rel
                if len(self.arrived[n]) < 3:
                    continue
                key, layer, col = _block_of(n)
                self.blocks[key] = _owner_sum("owner_sum_" + n, self.sums[n], landed, self.shapes[key],
                                              self.chip_arr, self.core_arr, self.blocks.get(key), layer, col)
        rider = _scatter_rider([self.sums[n] for n in names], [self.landed.get(n) for n in names], relations)
        return rider, done

    def share(self, names):
        keys, pieces = [], []
        for n in names:
            key, layer, col = _block_of(n)
            if key not in keys:
                keys.append(key)
            pieces.append((keys.index(key), layer, col))
        return _share_rider([self.blocks[k] for k in keys], pieces), lambda res: self.blocks.update(zip(keys, res))

    def run(self, fn, name, *args, **kw):
        made = [make() for make in self.plan.get(name, [])]
        if not made:
            return fn(name, *args, **kw)
        riders = [r for r, _ in made]
        out = _ride(riders[0] if len(riders) == 1 else _riders(*riders), fn, name, *args, **kw)
        for r, done in made:
            done(r.results)
        return out

    def w(self, name):
        b = self.bufs[name]
        if name in ("o", "p", "d0", "d1"):
            return b.reshape(-1, b.shape[-1])
        if name == "pg":
            b = b.reshape(N_CHIPS, 4, self.pg_rows, POOL_DIM).transpose(1, 0, 2, 3)
            return b.reshape(4, POOL_DIM, POOL_DIM)
        if name in ("pool_norm", "pool_scale"):
            return b.reshape(1, -1)
        return b

    def grad(self, name, value):
        if name == "pg":
            value = value.reshape(4, N_CHIPS, self.pg_rows, POOL_DIM).transpose(1, 0, 2, 3)
            value = value.reshape(N_CHIPS, 4 * self.pg_rows, POOL_DIM).astype(BF16)
        elif name in ("o", "p", "d0", "d1"):
            value = value.reshape(N_CHIPS, value.shape[0] // N_CHIPS, value.shape[1])
        self.parts[name] = value


def kernel(x, attn_norm, w_qkv, w_attn_out, pool_norm, w_pool_in, w_pool_group, pool_scale, ffn_norm, w_ffn_gate_up, w_ffn_down, final_norm, loss_target, m_attn_norm, m_w_qkv, m_w_attn_out, m_pool_norm, m_w_pool_in, m_w_pool_group, m_pool_scale, m_ffn_norm, m_w_ffn_gate_up, m_w_ffn_down, m_final_norm, v_attn_norm, v_w_qkv, v_w_attn_out, v_pool_norm, v_w_pool_in, v_w_pool_group, v_pool_scale, v_ffn_norm, v_w_ffn_gate_up, v_w_ffn_down, v_final_norm):
    D = D_MODEL
    chip = 2 * lax.axis_index("x") + lax.axis_index("y")
    core = lax.axis_index("c")
    pg_rows = w_pool_group.shape[2]

    chip_arr = chip.astype(jnp.int32).reshape(1)
    core_arr = core.astype(jnp.int32).reshape(1)

    pieces = _cast_qkv("cast_qkv", w_qkv, chip_arr)
    bufs = dict(zip(["wq_own"] + ["wq%d" % p for p in range(QKV_PIECES)], pieces))
    shards = [(w_attn_out, 0, BF16), (w_pool_in, 0, BF16), (w_pool_group.reshape(1, 4 * pg_rows, POOL_DIM), 0, BF16),
              (w_ffn_gate_up, 0, BF16), (w_ffn_gate_up, 1, BF16), (w_ffn_down, 0, BF16), (w_ffn_down, 1, BF16),
              (pool_norm[None], 0, F32), (pool_scale[None], 0, F32)]
    bufs.update(zip(TENSORS[1:] + ("pool_norm", "pool_scale"), _cast_many("cast_rest", shards, chip_arr)))

    as_block = lambda a: a.reshape((-1,) + a.shape[-2:]) if a.ndim == 3 else a.reshape(1, -1, a.shape[-1])
    owned = dict(qkv=(w_qkv, m_w_qkv, v_w_qkv), o=(w_attn_out, m_w_attn_out, v_w_attn_out),
                 p=(w_pool_in, m_w_pool_in, v_w_pool_in), pg=(w_pool_group, m_w_pool_group, v_w_pool_group),
                 gu=(w_ffn_gate_up, m_w_ffn_gate_up, v_w_ffn_gate_up), d=(w_ffn_down, m_w_ffn_down, v_w_ffn_down))
    comm = _MeshComm(bufs, {k: as_block(wmv[0]).shape for k, wmv in owned.items()}, chip_arr, core_arr, pg_rows)
    loss_part, grad_x, small = _local_step(x[0], loss_target[0], attn_norm, ffn_norm, final_norm.reshape(1, D), comm)

    rows = jnp.concatenate([small["attn"], small["ffn0"], small["ffn1"], small["final"], small["pool"],
                            small["scale"], jnp.pad(loss_part, ((0, 0), (0, D - 1))), jnp.zeros((1, D), F32)], axis=0)
    all_rows = _gather_small("gather_gain_grads", rows)
    tot = _sum_leading("sum_gain_grads", all_rows, F32)
    loss = tot[6, 0]
    g_attn_norm = tot[0:1]
    g_ffn_norm = tot[1:3]
    g_final_norm = tot[3]
    g_pool_norm = lax.dynamic_slice(tot, (4, chip * POOL_DIM), (1, POOL_DIM))
    g_pool_scale = lax.dynamic_slice(tot, (5, chip * POOL_DIM), (1, POOL_DIM))

    as_rows = lambda a: a.reshape(-1, a.shape[-1])
    updated = {}
    for k, (w, m, v) in owned.items():
        res = _adamw("adamw_" + k, as_rows(w), as_rows(comm.blocks[k]), as_rows(m), as_rows(v))
        updated[k] = [a.reshape(w.shape) for a in res]
    gains = [(attn_norm, g_attn_norm, m_attn_norm, v_attn_norm), (pool_norm, g_pool_norm, m_pool_norm, v_pool_norm),
             (pool_scale, g_pool_scale, m_pool_scale, v_pool_scale), (ffn_norm, g_ffn_norm, m_ffn_norm, v_ffn_norm),
             (final_norm, g_final_norm, m_final_norm, v_final_norm)]
    small_res = _adamw_small("adamw_gains", [tuple(as_rows(a) for a in item) for item in gains])
    small_res = [[a.reshape(item[0].shape) for a in r] for r, item in zip(small_res, gains)]
    results = [small_res[0], updated["qkv"], updated["o"], small_res[1], updated["p"], updated["pg"], small_res[2],
               small_res[3], updated["gu"], updated["d"], small_res[4]]
    grads = [r[0] for r in results]
    deltas = [r[1] for r in results]
    new_m = [r[2] for r in results]
    new_v = [r[3] for r in results]
    return (loss, grad_x[None], *grads, *deltas, *new_m, *new_v)
```

```python
import functools

import jax
import jax.numpy as jnp
from jax import lax
from jax.experimental import pallas as pl
from jax.experimental.pallas import tpu as pltpu

F32 = jnp.float32
BF16 = jnp.bfloat16
MESH = pl.DeviceIdType.MESH

D_MODEL = 1024
N_HEADS = 8
HEAD_DIM = 128
Q_BLOCK = 128
ATTN_DILATIONS = (1, 4, 16)
N_GROUPS = 3
D_FF = 2816
N_CHIPS = 4
FF_SHARD = 2 * D_FF // N_CHIPS
QKV_SHARD = 3 * 3 * D_MODEL // N_CHIPS
QKV_TILE = 768
POOL_DIM = 256
POOL_HALO = 16
RMS_EPS = 1e-6
NEG = -1e30
ADAM_LR = 0.001
ADAM_B1 = 0.9
ADAM_B2 = 0.999
ADAM_EPS = 1e-08
ADAM_WD = 0.01
ADAM_STEP = 10
VMEM_LIMIT = 56 * 1024 * 1024

_DN = {
    "nn": (((1,), (0,)), ((), ())),
    "nt": (((1,), (1,)), ((), ())),
    "tn": (((0,), (0,)), ((), ())),
}


class _Rider:
    def __init__(self, operands, out_shapes, aliases, n_copies, copies=None, start=None, finish=None):
        self.operands = list(operands)
        self.out_shapes = list(out_shapes)
        self.aliases = dict(aliases)
        self.n_copies = n_copies
        self.results = None

        def start_copies(ins, outs, send, recv, base=0):
            for cp in copies(ins, outs, send, recv, base):
                cp.start()

        def wait_copies(ins, outs, send, recv, base=0):
            for cp in copies(ins, outs, send, recv, base):
                cp.wait()

        self.start = start or start_copies
        self.finish = finish or wait_copies


def _riders(*riders):
    operands, out_shapes, aliases, offsets = [], [], {}, []
    n = 0
    for r in riders:
        offsets.append((len(operands), len(out_shapes), n))
        aliases.update({len(operands) + i: len(out_shapes) + o for i, o in r.aliases.items()})
        operands += r.operands
        out_shapes += r.out_shapes
        n += r.n_copies

    def each(which):
        def go(ins, outs, send, recv, base=0):
            for r, (i0, o0, s0) in zip(riders, offsets):
                getattr(r, which)(ins[i0:i0 + len(r.operands)], outs[o0:o0 + len(r.out_shapes)], send, recv,
                                  base + s0)
        return go

    both = _Rider(operands, out_shapes, aliases, n, start=each("start"), finish=each("finish"))
    both.parts = (riders, offsets)
    return both


_pending_rider = []


def _ride(rider, fn, *args, **kw):
    _pending_rider.append(rider)
    out = fn(*args, **kw)
    assert not _pending_rider
    if hasattr(rider, "parts"):
        for r, (_, o0, _) in zip(*rider.parts):
            r.results = rider.results[o0:o0 + len(r.out_shapes)]
    return out


def _pcall(body, prefetch=0, **kw):
    def build(body, kw):
        if not prefetch:
            return pl.pallas_call(body, **kw)
        kw = dict(kw)
        grid_spec = pltpu.PrefetchScalarGridSpec(
            num_scalar_prefetch=prefetch, grid=kw.pop("grid"), in_specs=kw.pop("in_specs"),
            out_specs=kw.pop("out_specs"), scratch_shapes=kw.pop("scratch_shapes", []))
        return pl.pallas_call(body, grid_spec=grid_spec, **kw)

    if not _pending_rider:
        return build(body, kw)
    rider = _pending_rider.pop()
    grid = kw.get("grid", ())
    in_specs = list(kw.get("in_specs", []))
    single = not isinstance(kw["out_shape"], (list, tuple))
    out_shape = [kw["out_shape"]] if single else list(kw["out_shape"])
    out_specs = [kw["out_specs"]] if single else list(kw["out_specs"])
    scratch = list(kw.get("scratch_shapes", []))
    n_in, n_out, n_scr = len(in_specs), len(out_shape), len(scratch)
    r_in, r_out = len(rider.operands), len(rider.out_shapes)

    def wrapped(*refs):
        scalars, refs = refs[:prefetch], refs[prefetch:]
        ins, rins = refs[:n_in], refs[n_in:n_in + r_in]
        outs = refs[n_in + r_in:n_in + r_in + n_out]
        routs = refs[n_in + r_in + n_out:n_in + r_in + n_out + r_out]
        scr = refs[n_in + r_in + n_out + r_out:n_in + r_in + n_out + r_out + n_scr]
        send, recv = refs[-2:]
        ids = [pl.program_id(a) for a in range(len(grid))]
        first, last = True, True
        for a, pid in enumerate(ids):
            first = jnp.logical_and(first, pid == 0)
            last = jnp.logical_and(last, pid == grid[a] - 1)
        if grid:
            pl.when(first)(lambda: rider.start(rins, routs, send, recv))
        else:
            rider.start(rins, routs, send, recv)
        body(*scalars, *ins, *outs, *scr)
        if grid:
            pl.when(last)(lambda: rider.finish(rins, routs, send, recv))
        else:
            rider.finish(rins, routs, send, recv)

    any_spec = pl.BlockSpec(memory_space=pl.ANY)
    kw2 = dict(kw)
    kw2.update(
        in_specs=in_specs + [any_spec] * r_in, out_specs=out_specs + [any_spec] * r_out,
        out_shape=out_shape + rider.out_shapes,
        scratch_shapes=scratch + [pltpu.SemaphoreType.DMA((rider.n_copies,)),
                                  pltpu.SemaphoreType.DMA((rider.n_copies,))],
        input_output_aliases={**kw.get("input_output_aliases", {}),
                              **{prefetch + n_in + i: n_out + o for i, o in rider.aliases.items()}})
    if grid:
        kw2["compiler_params"] = _params(("arbitrary",) * len(grid))
    call = build(wrapped, kw2)

    def run(*operands):
        res = call(*operands, *rider.operands)
        rider.results = list(res[n_out:])
        return res[0] if single else list(res[:n_out])

    return run


def _params(sem):
    return pltpu.CompilerParams(dimension_semantics=sem, vmem_limit_bytes=VMEM_LIMIT)


def _dot(a, b, mode):
    return lax.dot_general(a, b, _DN[mode], preferred_element_type=F32)


def _mm(name, mode, grid, a, a_spec, b, b_spec, out_shape, o_spec, acc_shape):
    nk = grid[-1]

    def body(a_ref, b_ref, o_ref, *acc):
        part = _dot(a_ref[...].astype(BF16), b_ref[...].astype(BF16), mode)
        if nk == 1:
            o_ref[...] = part.astype(o_ref.dtype)
            return
        acc_ref, = acc
        k = pl.program_id(len(grid) - 1)

        @pl.when(k == 0)
        def _():
            acc_ref[...] = part

        @pl.when(k > 0)
        def _():
            acc_ref[...] += part

        @pl.when(k == nk - 1)
        def _():
            o_ref[...] = acc_ref[...].astype(o_ref.dtype)

    scratch = [] if nk == 1 else [pltpu.VMEM(acc_shape, F32)]
    sem = ("parallel",) * (len(grid) - 1) + ("arbitrary",)
    return _pcall(body, name=name, grid=grid, in_specs=[a_spec, b_spec], out_specs=o_spec, out_shape=out_shape,
                  scratch_shapes=scratch, compiler_params=_params(sem))(a, b)


def _mm_tn(name, a, b, tm, tk):
    T, M = a.shape
    N = b.shape[1]
    return _mm(name, "tn", (M // tm, T // tk), a, pl.BlockSpec((tk, tm), lambda i, k: (k, i)),
               b, pl.BlockSpec((tk, N), lambda i, k: (k, 0)),
               jax.ShapeDtypeStruct((M, N), BF16), pl.BlockSpec((tm, N), lambda i, k: (i, 0)), (tm, N))


def _norm_bwd_rows(xf, gain, dh, dres):
    r = lax.rsqrt(jnp.mean(xf * xf, axis=-1, keepdims=True) + RMS_EPS)
    xh = xf * r
    t = dh * gain
    dx = dres + r * (t - xh * jnp.mean(t * xh, axis=-1, keepdims=True))
    return dx, jnp.sum(dh * xh, axis=0, keepdims=True)


def _accumulate(ref, value, first):
    @pl.when(first)
    def _():
        ref[...] = value

    @pl.when(jnp.logical_not(first))
    def _():
        ref[...] += value


def _rms_bwd_folded(name, x, g, dhf, dres, tb=512):
    S, D = x.shape

    def body(x_ref, g_ref, d0_ref, d1_ref, d2_ref, dres_ref, dx_ref, dg_ref, dh_ref):
        chunks = _lane_chunks(D)
        for c, cols in enumerate(chunks):
            dh_ref[c] = d0_ref[0, :, cols].astype(F32)
        for dil, ref in ((ATTN_DILATIONS[1], d1_ref), (ATTN_DILATIONS[2], d2_ref)):
            n = tb // dil
            for k in range(dil):
                for c, cols in enumerate(chunks):
                    dh_ref[c, _strided_rows(k, n, dil), :] += ref[k, :, cols].astype(F32)
        dh = jnp.concatenate([dh_ref[c] for c in range(len(chunks))], axis=1)
        dx, dg = _norm_bwd_rows(x_ref[...], g_ref[...], dh, dres_ref[...])
        dx_ref[...] = dx
        _accumulate(dg_ref, dg, pl.program_id(0) == 0)

    views, specs = _folded_views(dhf, tb)
    row = pl.BlockSpec((tb, D), lambda i: (i, 0))
    vec = pl.BlockSpec((1, D), lambda i: (0, 0))
    return _pcall(body, name=name, grid=(S // tb,), in_specs=[row, vec] + specs + [row], out_specs=[row, vec],
                  out_shape=[jax.ShapeDtypeStruct((S, D), F32), jax.ShapeDtypeStruct((1, D), F32)],
                  scratch_shapes=[pltpu.VMEM((D // LANES, tb, LANES), F32)],
                  compiler_params=_params(("arbitrary",)))(x, g, *views, dres)


def _proj_res_norm(name, a, w, res, gain, tm=512):
    M, K = a.shape
    D = w.shape[1]

    def body(a_ref, w_ref, res_ref, g_ref, x_ref, h_ref):
        xf = res_ref[...] + _dot(a_ref[...], w_ref[...], "nn")
        x_ref[...] = xf
        r = lax.rsqrt(jnp.mean(xf * xf, axis=-1, keepdims=True) + RMS_EPS)
        h_ref[...] = ((xf * r) * g_ref[...]).astype(h_ref.dtype)

    row = pl.BlockSpec((tm, D), lambda i: (i, 0))
    return _pcall(body, name=name, grid=(M // tm,),
                  in_specs=[pl.BlockSpec((tm, K), lambda i: (i, 0)), pl.BlockSpec((K, D), lambda i: (0, 0)), row,
                            pl.BlockSpec((1, D), lambda i: (0, 0))],
                  out_specs=[row, row],
                  out_shape=[jax.ShapeDtypeStruct((M, D), F32), jax.ShapeDtypeStruct((M, D), BF16)],
                  compiler_params=_params(("parallel",)))(a, w, res, gain)


def _proj_res_loss(name, a, w, res, gain, tgt, tm=512):
    M, K = a.shape
    D = w.shape[1]

    def body(a_ref, w_ref, res_ref, g_ref, t_ref, loss_ref, dx_ref, dxb_ref, dg_ref):
        first = pl.program_id(0) == 0
        xf = res_ref[...] + _dot(a_ref[...], w_ref[...], "nn")
        r = lax.rsqrt(jnp.mean(xf * xf, axis=-1, keepdims=True) + RMS_EPS)
        xh = xf * r
        gv = g_ref[...]
        e = xh * gv - t_ref[...]
        lp = 0.5 * jnp.sum(jnp.mean(e * e, axis=-1, keepdims=True), axis=0, keepdims=True)
        dy = e * (1.0 / D)
        t = dy * gv
        dx = r * (t - xh * jnp.mean(t * xh, axis=-1, keepdims=True))
        dx_ref[...] = dx
        dxb_ref[...] = dx.astype(dxb_ref.dtype)
        _accumulate(dg_ref, jnp.sum(dy * xh, axis=0, keepdims=True), first)
        _accumulate(loss_ref, lp, first)

    row = pl.BlockSpec((tm, D), lambda i: (i, 0))
    vec = pl.BlockSpec((1, D), lambda i: (0, 0))
    return _pcall(body, name=name, grid=(M // tm,),
                  in_specs=[pl.BlockSpec((tm, K), lambda i: (i, 0)), pl.BlockSpec((K, D), lambda i: (0, 0)), row,
                            vec, row],
                  out_specs=[pl.BlockSpec((1, 1), lambda i: (0, 0)), row, row, vec],
                  out_shape=[jax.ShapeDtypeStruct((1, 1), F32), jax.ShapeDtypeStruct((M, D), F32),
                             jax.ShapeDtypeStruct((M, D), BF16), jax.ShapeDtypeStruct((1, D), F32)],
                  compiler_params=_params(("arbitrary",)))(a, w, res, gain, tgt)


def _dh_norm_bwd(name, d, w, x, gain, dres, tm=512):
    M, N = d.shape
    D = w.shape[0]

    def body(d_ref, w_ref, x_ref, g_ref, dres_ref, dx_ref, dxb_ref, dg_ref):
        dh = _dot(d_ref[...].astype(BF16), w_ref[...], "nt")
        dx, dg = _norm_bwd_rows(x_ref[...], g_ref[...], dh, dres_ref[...])
        dx_ref[...] = dx
        dxb_ref[...] = dx.astype(dxb_ref.dtype)
        _accumulate(dg_ref, dg, pl.program_id(0) == 0)

    row = pl.BlockSpec((tm, D), lambda i: (i, 0))
    vec = pl.BlockSpec((1, D), lambda i: (0, 0))
    return _pcall(body, name=name, grid=(M // tm,),
                  in_specs=[pl.BlockSpec((tm, N), lambda i: (i, 0)), pl.BlockSpec((D, N), lambda i: (0, 0)), row, vec,
                            row],
                  out_specs=[row, row, vec],
                  out_shape=[jax.ShapeDtypeStruct((M, D), F32), jax.ShapeDtypeStruct((M, D), BF16),
                             jax.ShapeDtypeStruct((1, D), F32)],
                  compiler_params=_params(("arbitrary",)))(d, w, x, gain, dres)


def _sigmoid(v):
    return 0.5 * jnp.tanh(0.5 * v) + 0.5


def _ffn_up(name, h, w_gu, tm=1024):
    S, D = h.shape
    W = FF_SHARD

    def body(h_ref, wg_ref, wu_ref, gu_ref, act_ref):
        hv = h_ref[...]
        gate = _dot(hv, wg_ref[...], "nn")
        up = _dot(hv, wu_ref[...], "nn")
        gu_ref[0] = gate.astype(gu_ref.dtype)
        gu_ref[1] = up.astype(gu_ref.dtype)
        sig = _sigmoid(gate)
        act_ref[...] = ((gate * sig) * up).astype(act_ref.dtype)

    return _pcall(
        body, name=name, grid=(2, S // tm),
        in_specs=[pl.BlockSpec((tm, D), lambda j, i: (i, 0)),
                  pl.BlockSpec((None, D, W), lambda j, i: (j, 0, 0)),
                  pl.BlockSpec((None, D, W), lambda j, i: (j + 2, 0, 0))],
        out_specs=[pl.BlockSpec((2, tm, W), lambda j, i: (0, i, j)), pl.BlockSpec((tm, W), lambda j, i: (i, j))],
        out_shape=[jax.ShapeDtypeStruct((2, S, D_FF), BF16), jax.ShapeDtypeStruct((S, D_FF), BF16)],
        compiler_params=_params(("parallel", "parallel")))(h, w_gu, w_gu)


def _ffn_down_bwd(name, dx, w_down, gu, tm=1024):
    S, D = dx.shape
    W = FF_SHARD

    def body(dx_ref, w_ref, gu_ref, dgu_ref):
        dact = _dot(dx_ref[...].astype(BF16), w_ref[...], "nt")
        gate = gu_ref[0].astype(F32)
        up = gu_ref[1].astype(F32)
        sig = _sigmoid(gate)
        silu = gate * sig
        dgu_ref[0] = (dact * up * (sig * (1.0 + gate * (1.0 - sig)))).astype(dgu_ref.dtype)
        dgu_ref[1] = (dact * silu).astype(dgu_ref.dtype)

    blk = pl.BlockSpec((2, tm, W), lambda j, i: (0, i, j))
    return _pcall(
        body, name=name, grid=(2, S // tm),
        in_specs=[pl.BlockSpec((tm, D), lambda j, i: (i, 0)), pl.BlockSpec((W, D), lambda j, i: (j, 0)), blk],
        out_specs=blk, out_shape=jax.ShapeDtypeStruct((2, S, D_FF), BF16),
        compiler_params=_params(("parallel", "parallel")))(dx, w_down, gu)


def _ffn_dw_up(name, h, dgu, tk=2048):
    S, D = h.shape
    W = FF_SHARD
    tk = min(tk, S)
    return _mm(name, "tn", (N_CHIPS, S // tk), h, pl.BlockSpec((tk, D), lambda s, k: (k, 0)),
               dgu, pl.BlockSpec((None, tk, W), lambda s, k: (s // 2, k, s % 2)),
               jax.ShapeDtypeStruct((N_CHIPS, D, W), BF16), pl.BlockSpec((None, D, W), lambda s, k: (s, 0, 0)),
               (D, W))


def _ffn_dh(name, dgu, w_gu, x, gain, dres, tm=512):
    S = dgu.shape[1]
    W = FF_SHARD

    def body(a_ref, w_hbm, x_ref, g_ref, dres_ref, dx_ref, dg_ref, w_ref, acat_ref, sems):
        first = pl.program_id(0) == 0

        @pl.when(first)
        def _():
            copies = [pltpu.make_async_copy(w_hbm.at[s], w_ref.at[:, pl.ds(s * W, W)], sems.at[s])
                      for s in range(N_CHIPS)]
            for cp in copies:
                cp.start()
            for cp in copies:
                cp.wait()

        acat_ref[:, :D_FF] = a_ref[0]
        acat_ref[:, D_FF:] = a_ref[1]
        dh = _dot(acat_ref[...], w_ref[...], "nt")
        dx, dg = _norm_bwd_rows(x_ref[...], g_ref[...], dh, dres_ref[...])
        dx_ref[...] = dx
        _accumulate(dg_ref, dg, first)

    row = pl.BlockSpec((tm, D_MODEL), lambda i: (i, 0))
    vec = pl.BlockSpec((1, D_MODEL), lambda i: (0, 0))
    return _pcall(body, name=name, grid=(S // tm,),
                  in_specs=[pl.BlockSpec((2, tm, D_FF), lambda i: (0, i, 0)), pl.BlockSpec(memory_space=pl.ANY),
                            row, vec, row],
                  out_specs=[row, vec],
                  out_shape=[jax.ShapeDtypeStruct((S, D_MODEL), F32), jax.ShapeDtypeStruct((1, D_MODEL), F32)],
                  scratch_shapes=[pltpu.VMEM((D_MODEL, 2 * D_FF), BF16), pltpu.VMEM((tm, 2 * D_FF), BF16),
                                  pltpu.SemaphoreType.DMA((N_CHIPS,))],
                  compiler_params=_params(("arbitrary",)))(dgu, w_gu, x, gain, dres)


FOLD_TOKENS = 1024
LANES = 128


def _lane_chunks(width):
    return [slice(c * LANES, (c + 1) * LANES) for c in range(width // LANES)]


def _strided_rows(r, n, dil):
    return pl.ds(r, n) if dil == 1 else pl.ds(r, n, stride=dil)


def _norm_fold(name, x, gain):
    S, D = x.shape
    chunks = _lane_chunks(D)

    def body(x_ref, g_ref, hf_hbm, xc_ref, hs_ref, sems):
        i = pl.program_id(0)
        xf = x_ref[...]
        inv = lax.rsqrt(jnp.mean(xf * xf, axis=-1, keepdims=True) + RMS_EPS)
        h = (xf * inv) * g_ref[...]
        hs_ref[0] = h.astype(BF16)
        for c, cols in enumerate(chunks):
            xc_ref[c] = h[:, cols]
        copies = []
        for g, dil in enumerate(ATTN_DILATIONS):
            n = FOLD_TOKENS // dil
            for r in range(dil):
                if dil > 1:
                    for c, cols in enumerate(chunks):
                        hs_ref[g, r * n:(r + 1) * n, cols] = xc_ref[c, _strided_rows(r, n, dil), :].astype(BF16)
                cp = pltpu.make_async_copy(hs_ref.at[g, pl.ds(r * n, n)],
                                           hf_hbm.at[g, pl.ds(r * (S // dil) + i * n, n)], sems.at[len(copies)])
                cp.start()
                copies.append(cp)
        for cp in copies:
            cp.wait()

    return _pcall(body, name=name, grid=(S // FOLD_TOKENS,),
                  in_specs=[pl.BlockSpec((FOLD_TOKENS, D), lambda i: (i, 0)), pl.BlockSpec((1, D), lambda i: (0, 0))],
                  out_specs=pl.BlockSpec(memory_space=pl.ANY),
                  out_shape=jax.ShapeDtypeStruct((N_GROUPS, S, D), BF16),
                  scratch_shapes=[pltpu.VMEM((D // LANES, FOLD_TOKENS, LANES), F32),
                                  pltpu.VMEM((N_GROUPS, FOLD_TOKENS, D), BF16),
                                  pltpu.SemaphoreType.DMA((sum(ATTN_DILATIONS),))],
                  compiler_params=_params(("arbitrary",)))(x, gain)


def _qkv_tiles(name, piece, hf, w, qkv, chip, tm=1024):
    S = hf.shape[1]
    per_group = 3 * D_MODEL // QKV_TILE

    def tile(q, c_ref):
        if piece is None:
            return QKV_PIECES * c_ref[0] + q
        return QKV_PIECES * ((c_ref[0] + 1 + q) % N_CHIPS) + piece

    def body(*refs):
        a_ref, w_ref, o_ref = refs[1], refs[2], refs[-1]
        o_ref[...] = _dot(a_ref[...], w_ref[...], "nn").astype(o_ref.dtype)

    if piece is None:
        w_spec = pl.BlockSpec((D_MODEL, QKV_TILE), lambda q, i, c_ref: (0, q))
    else:
        w_spec = pl.BlockSpec((None, D_MODEL, QKV_TILE), lambda q, i, c_ref: ((c_ref[0] + 1 + q) % N_CHIPS, 0, 0))
    in_specs = [pl.BlockSpec((None, tm, D_MODEL), lambda q, i, c_ref: (tile(q, c_ref) // per_group, i, 0)), w_spec]
    operands = [chip, hf, w]
    aliases = {}
    if qkv is not None:
        in_specs.append(pl.BlockSpec(memory_space=pl.ANY))
        operands.append(qkv)
        aliases = {3: 0}
    return _pcall(
        body, prefetch=1, name=name, grid=(N_CHIPS - 1, S // tm), in_specs=in_specs,
        out_specs=pl.BlockSpec((None, tm, QKV_TILE),
                               lambda q, i, c_ref: (tile(q, c_ref) // per_group, i, tile(q, c_ref) % per_group)),
        out_shape=jax.ShapeDtypeStruct((N_GROUPS, S, 3 * D_MODEL), BF16), input_output_aliases=aliases,
        compiler_params=_params(("arbitrary", "arbitrary")))(*operands)


QKV_PIECES = QKV_SHARD // QKV_TILE


def _qkv_dw(name, p, hf, dqkv):
    S = hf.shape[1]
    per_group = 3 * D_MODEL // QKV_TILE
    tile = lambda s: QKV_PIECES * s + p
    return _mm(name, "tn", (N_CHIPS, 1),
               hf, pl.BlockSpec((None, S, D_MODEL), lambda s, k: (tile(s) // per_group, 0, 0)),
               dqkv, pl.BlockSpec((None, S, QKV_TILE), lambda s, k: (tile(s) // per_group, 0, tile(s) % per_group)),
               jax.ShapeDtypeStruct((N_CHIPS, D_MODEL, QKV_TILE), BF16),
               pl.BlockSpec((None, D_MODEL, QKV_TILE), lambda s, k: (s, 0, 0)), None)


def _qkv_dh(name, g, dqkv, w_pieces, dhf, tm=1024):
    S = dqkv.shape[1]
    per_group = 3 * D_MODEL // QKV_TILE

    def body(*refs):
        a_ref, w_hbm = refs[0], refs[1:1 + QKV_PIECES]
        o_ref, w_ref, sems = refs[-3:]

        @pl.when(pl.program_id(0) == 0)
        def _():
            copies = []
            for j in range(per_group):
                t = per_group * g + j
                copies.append(pltpu.make_async_copy(w_hbm[t % QKV_PIECES].at[t // QKV_PIECES],
                                                    w_ref.at[:, pl.ds(j * QKV_TILE, QKV_TILE)], sems.at[j]))
            for cp in copies:
                cp.start()
            for cp in copies:
                cp.wait()

        o_ref[...] = _dot(a_ref[...], w_ref[...], "nt").astype(o_ref.dtype)

    any_spec = pl.BlockSpec(memory_space=pl.ANY)
    in_specs = [pl.BlockSpec((None, tm, 3 * D_MODEL), lambda i: (g, i, 0))] + [any_spec] * QKV_PIECES
    operands = [dqkv] + list(w_pieces)
    aliases = {}
    if dhf is not None:
        in_specs.append(any_spec)
        operands.append(dhf)
        aliases = {1 + QKV_PIECES: 0}
    return _pcall(body, name=name, grid=(S // tm,), in_specs=in_specs,
                  out_specs=pl.BlockSpec((None, tm, D_MODEL), lambda i: (g, i, 0)),
                  out_shape=jax.ShapeDtypeStruct((N_GROUPS, S, D_MODEL), BF16),
                  scratch_shapes=[pltpu.VMEM((D_MODEL, 3 * D_MODEL), BF16), pltpu.SemaphoreType.DMA((per_group,))],
                  input_output_aliases=aliases, compiler_params=_params(("arbitrary",)))(*operands)


def _alibi_coef(g, h):
    vals = [-(2.0 ** (-8.0 * (gg * N_HEADS + h + 1) / (N_GROUPS * N_HEADS))) * ATTN_DILATIONS[gg]
            for gg in range(N_GROUPS)]
    return jnp.where(g == 0, vals[0], jnp.where(g == 1, vals[1], vals[2])).astype(F32)


def _blocks_per_segment(g, S):
    return jnp.right_shift(S // Q_BLOCK, 2 * g)


def _band_bias(pen):
    row = lax.broadcasted_iota(jnp.int32, (Q_BLOCK, 2 * Q_BLOCK), 0)
    col = lax.broadcasted_iota(jnp.int32, (Q_BLOCK, 2 * Q_BLOCK), 1)
    dist = Q_BLOCK + row - col
    valid = jnp.logical_and(dist >= 0, dist <= Q_BLOCK)
    base = jnp.where(valid, jnp.where(col < Q_BLOCK, pen, 0.0), NEG).astype(F32)
    return base, dist.astype(F32)


def _skewed(n_units, stages):
    state = {}
    for step in range(n_units + len(stages) - 1):
        for s, stage in enumerate(stages):
            u = step - s
            if 0 <= u < n_units:
                state[u] = stage(u, state.get(u))
                if s == len(stages) - 1:
                    del state[u]


def _attn_fwd(name, qkv, tq=1024):
    S = qkv.shape[1]
    nqb = tq // Q_BLOCK
    scale = HEAD_DIM ** -0.5

    def body(q_ref, k_ref, kp_ref, v_ref, vp_ref, o_ref, lse_ref, kf_ref, vf_ref):
        g = pl.program_id(0)
        i = pl.program_id(1)
        nb = _blocks_per_segment(g, S)
        kf_ref[:Q_BLOCK] = kp_ref[...]
        kf_ref[Q_BLOCK:] = k_ref[...]
        vf_ref[:Q_BLOCK] = vp_ref[...]
        vf_ref[Q_BLOCK:] = v_ref[...]
        coefs = [_alibi_coef(g, h) for h in range(N_HEADS)]
        units = [(b, h) for b in range(nqb) for h in range(N_HEADS)]
        bias = {}

        def scores(u, _):
            b, h = units[u]
            if b not in bias:
                pen = jnp.where((i * nqb + b) % nb != 0, 0.0, NEG).astype(F32)
                bias.clear()
                bias[b] = _band_bias(pen)
            base, dist = bias[b]
            cols = slice(h * HEAD_DIM, (h + 1) * HEAD_DIM)
            q = q_ref[b * Q_BLOCK:(b + 1) * Q_BLOCK, cols]
            kk = kf_ref[b * Q_BLOCK:(b + 2) * Q_BLOCK, cols]
            return _dot(q, kk, "nt") * scale + (coefs[h] * dist + base)

        def softmax(u, s):
            m = jnp.max(s, axis=-1, keepdims=True)
            p = jnp.exp(s - m)
            den = jnp.sum(p, axis=-1, keepdims=True)
            return (p * (1.0 / den)).astype(BF16), m + jnp.log(den)

        def output(u, st):
            b, h = units[u]
            pn, lse = st
            cols = slice(h * HEAD_DIM, (h + 1) * HEAD_DIM)
            rows = slice(b * Q_BLOCK, (b + 1) * Q_BLOCK)
            o_ref[rows, cols] = _dot(pn, vf_ref[b * Q_BLOCK:(b + 2) * Q_BLOCK, cols], "nn").astype(o_ref.dtype)
            lse_ref[rows, h:h + 1] = lse

        _skewed(len(units), [scores, softmax, output])

    main = lambda c: pl.BlockSpec((None, tq, D_MODEL), lambda g, i: (g, i, c))
    prev = lambda c: pl.BlockSpec((None, Q_BLOCK, D_MODEL), lambda g, i: (g, jnp.maximum(i * nqb - 1, 0), c))
    return _pcall(
        body, name=name, grid=(N_GROUPS, S // tq),
        in_specs=[main(0), main(1), prev(1), main(2), prev(2)],
        out_specs=[pl.BlockSpec((None, tq, D_MODEL), lambda g, i: (g, i, 0)),
                   pl.BlockSpec((None, tq, N_HEADS), lambda g, i: (g, i, 0))],
        out_shape=[jax.ShapeDtypeStruct((N_GROUPS, S, D_MODEL), BF16),
                   jax.ShapeDtypeStruct((N_GROUPS, S, N_HEADS), F32)],
        scratch_shapes=[pltpu.VMEM((tq + Q_BLOCK, D_MODEL), BF16), pltpu.VMEM((tq + Q_BLOCK, D_MODEL), BF16)],
        compiler_params=_params(("parallel", "parallel")))(qkv, qkv, qkv, qkv, qkv)


def _attn_bwd(name, qkv, do, lse, dl, tq=1024):
    S = qkv.shape[1]
    nqb = tq // Q_BLOCK
    n_blocks = S // Q_BLOCK
    scale = HEAD_DIM ** -0.5
    KEYS = 2 * Q_BLOCK

    def body(q_ref, k_ref, v_ref, kp_ref, vp_ref, qn_ref, do_ref, don_ref, lse_ref, lsen_ref, dl_ref, dln_ref,
             out_ref, kf_ref, vf_ref, dk_ref, dv_ref):
        g = pl.program_id(0)
        i = pl.program_id(1)
        nb = _blocks_per_segment(g, S)
        kf_ref[:Q_BLOCK] = kp_ref[...]
        kf_ref[Q_BLOCK:] = k_ref[...]
        vf_ref[:Q_BLOCK] = vp_ref[...]
        vf_ref[Q_BLOCK:] = v_ref[...]
        coefs = [_alibi_coef(g, h) for h in range(N_HEADS)]
        units = [(h, b) for h in range(N_HEADS) for b in range(nqb + 1)]
        bias = {}

        def band(b):
            if b not in bias:
                blk = i * nqb + b
                ok = blk % nb != 0
                if b == nqb:
                    ok = jnp.logical_and(ok, blk < n_blocks)
                bias[b] = _band_bias(jnp.where(ok, 0.0, NEG).astype(F32))
            return bias[b]

        def operands(h, b):
            cols = slice(h * HEAD_DIM, (h + 1) * HEAD_DIM)
            if b == nqb:
                keys = slice(tq, tq + Q_BLOCK)
                return (qn_ref[:, cols], don_ref[:, cols], lsen_ref[:, h:h + 1], dln_ref[:, h:h + 1],
                        kf_ref[keys, cols], vf_ref[keys, cols])
            rows = slice(b * Q_BLOCK, (b + 1) * Q_BLOCK)
            keys = slice(b * Q_BLOCK, b * Q_BLOCK + KEYS)
            return (q_ref[rows, cols], do_ref[rows, cols], lse_ref[rows, h:h + 1], dl_ref[rows, h:h + 1],
                    kf_ref[keys, cols], vf_ref[keys, cols])

        def probs(u, _):
            h, b = units[u]
            q, do_b, lse_b, dl_b, kk, vv = operands(h, b)
            base, dist = band(b)
            if b == nqb:
                base, dist = base[:, :Q_BLOCK], dist[:, :Q_BLOCK]
            p = jnp.exp(_dot(q, kk, "nt") * scale + (coefs[h] * dist + base) - lse_b)
            return p, _dot(do_b, vv, "nt")

        def dscores(u, st):
            h, b = units[u]
            p, dp = st
            dl_b = operands(h, b)[3]
            return ((p * (dp - dl_b)) * scale).astype(BF16), p.astype(BF16)

        def grads(u, st):
            h, b = units[u]
            ds, pb = st
            q, do_b, _, _, kk, _ = operands(h, b)
            cols = slice(h * HEAD_DIM, (h + 1) * HEAD_DIM)
            if b < nqb:
                out_ref[b * Q_BLOCK:(b + 1) * Q_BLOCK, cols] = _dot(ds, kk, "nn").astype(out_ref.dtype)
            if b == 0:
                dk_ref[:Q_BLOCK] = _dot(ds[:, Q_BLOCK:], q, "tn")
                dv_ref[:Q_BLOCK] = _dot(pb[:, Q_BLOCK:], do_b, "tn")
                return None
            dk = _dot(ds, q, "tn")
            dv = _dot(pb, do_b, "tn")
            before = slice((b - 1) * Q_BLOCK, b * Q_BLOCK)
            dk_ref[before] += dk[:Q_BLOCK]
            dv_ref[before] += dv[:Q_BLOCK]
            if b < nqb:
                own = slice(b * Q_BLOCK, (b + 1) * Q_BLOCK)
                dk_ref[own] = dk[Q_BLOCK:]
                dv_ref[own] = dv[Q_BLOCK:]
            else:
                out_ref[:, D_MODEL + h * HEAD_DIM:D_MODEL + (h + 1) * HEAD_DIM] = dk_ref[...].astype(out_ref.dtype)
                out_ref[:, 2 * D_MODEL + h * HEAD_DIM:2 * D_MODEL + (h + 1) * HEAD_DIM] = (
                    dv_ref[...].astype(out_ref.dtype))
            return None

        _skewed(len(units), [probs, dscores, grads])

    nxt_blk = lambda i: jnp.minimum((i + 1) * nqb, n_blocks - 1)
    main = lambda c: pl.BlockSpec((None, tq, D_MODEL), lambda g, i: (g, i, c))
    prev = lambda c: pl.BlockSpec((None, Q_BLOCK, D_MODEL), lambda g, i: (g, jnp.maximum(i * nqb - 1, 0), c))
    row = pl.BlockSpec((None, tq, D_MODEL), lambda g, i: (g, i, 0))
    row_n = pl.BlockSpec((None, Q_BLOCK, D_MODEL), lambda g, i: (g, nxt_blk(i), 0))
    col = pl.BlockSpec((None, tq, N_HEADS), lambda g, i: (g, i, 0))
    col_n = pl.BlockSpec((None, Q_BLOCK, N_HEADS), lambda g, i: (g, nxt_blk(i), 0))
    return _pcall(
        body, name=name, grid=(N_GROUPS, S // tq),
        in_specs=[main(0), main(1), main(2), prev(1), prev(2), row_n, row, row_n, col, col_n, col, col_n],
        out_specs=pl.BlockSpec((None, tq, 3 * D_MODEL), lambda g, i: (g, i, 0)),
        out_shape=jax.ShapeDtypeStruct((N_GROUPS, S, 3 * D_MODEL), BF16),
        scratch_shapes=[pltpu.VMEM((tq + Q_BLOCK, D_MODEL), BF16), pltpu.VMEM((tq + Q_BLOCK, D_MODEL), BF16),
                        pltpu.VMEM((tq, HEAD_DIM), F32), pltpu.VMEM((tq, HEAD_DIM), F32)],
        compiler_params=_params(("parallel", "parallel")))(qkv, qkv, qkv, qkv, qkv, qkv, do, do, lse, lse, dl, dl)


def _group_weights(lse_ref):
    l0, l1, l2 = lse_ref[0], lse_ref[1], lse_ref[2]
    m = jnp.maximum(jnp.maximum(l0, l1), l2)
    e = [jnp.exp(l0 - m), jnp.exp(l1 - m), jnp.exp(l2 - m)]
    den = e[0] + e[1] + e[2]
    return [ei / den for ei in e]


MERGE_TOKENS = 512


def _folded_views(a, tb):
    _, S, C = a.shape
    views, specs = [], []
    for g, dil in enumerate(ATTN_DILATIONS):
        views.append(a.reshape(N_GROUPS * dil, S // dil, C))
        specs.append(pl.BlockSpec((dil, tb // dil, C), lambda i, g=g: (g, i, 0)))
    return views, specs


def _unfold_into(dst_ref, folded_refs):
    for g, (dil, ref) in enumerate(zip(ATTN_DILATIONS, folded_refs)):
        n = ref.shape[1]
        for r in range(dil):
            for c, cols in enumerate(_lane_chunks(ref.shape[2])):
                dst_ref[g, c, _strided_rows(r, n, dil), :] = ref[r, :, cols].astype(F32)


def _merge_fwd(name, o, lse, tb=MERGE_TOKENS):
    S = o.shape[1]

    def body(o0_ref, o1_ref, o2_ref, lse_ref, out_ref, o_ref):
        _unfold_into(o_ref, (o0_ref, o1_ref, o2_ref))
        w = _group_weights(lse_ref)
        for h in range(N_HEADS):
            cols = slice(h * HEAD_DIM, (h + 1) * HEAD_DIM)
            acc = w[0][:, h:h + 1] * o_ref[0, h]
            for gg in range(1, N_GROUPS):
                acc = acc + w[gg][:, h:h + 1] * o_ref[gg, h]
            out_ref[:, cols] = acc.astype(out_ref.dtype)

    views, specs = _folded_views(o, tb)
    return _pcall(body, name=name, grid=(S // tb,),
                  in_specs=specs + [pl.BlockSpec((N_GROUPS, tb, N_HEADS), lambda i: (0, i, 0))],
                  out_specs=pl.BlockSpec((tb, D_MODEL), lambda i: (i, 0)),
                  out_shape=jax.ShapeDtypeStruct((S, D_MODEL), BF16),
                  scratch_shapes=[pltpu.VMEM((N_GROUPS, N_HEADS, tb, HEAD_DIM), F32)],
                  compiler_params=_params(("parallel",)))(*views, lse)


def _merge_bwd(name, dx, w_out, o, lse, tb=MERGE_TOKENS):
    S = o.shape[1]
    n_chunks = sum(ATTN_DILATIONS)

    def body(dx_ref, w_ref, o0_ref, o1_ref, o2_ref, lse_ref, do_hbm, dl_ref, o_ref, dt_ref, df_ref, d_ref, sems):
        i = pl.program_id(0)
        d_ref[...] = _dot(dx_ref[...].astype(BF16), w_ref[...], "nt")
        _unfold_into(o_ref, (o0_ref, o1_ref, o2_ref))
        w = _group_weights(lse_ref)
        for h in range(N_HEADS):
            cols = slice(h * HEAD_DIM, (h + 1) * HEAD_DIM)
            dv = d_ref[:, cols]
            merged = w[0][:, h:h + 1] * o_ref[0, h]
            for gg in range(1, N_GROUPS):
                merged = merged + w[gg][:, h:h + 1] * o_ref[gg, h]
            dsum = jnp.sum(dv * merged, axis=-1, keepdims=True)
            for gg in range(N_GROUPS):
                wg = w[gg][:, h:h + 1]
                dt_ref[gg, h] = wg * dv
                dl_ref[gg, :, h:h + 1] = wg * dsum
        copies = []
        for gg, dil in enumerate(ATTN_DILATIONS):
            n = tb // dil
            for r in range(dil):
                for h, cols in enumerate(_lane_chunks(D_MODEL)):
                    df_ref[gg, r * n:(r + 1) * n, cols] = (
                        dt_ref[gg, h, _strided_rows(r, n, dil), :].astype(df_ref.dtype))
                cp = pltpu.make_async_copy(df_ref.at[gg, pl.ds(r * n, n)],
                                           do_hbm.at[gg, pl.ds(r * (S // dil) + i * n, n)], sems.at[len(copies)])
                cp.start()
                copies.append(cp)
        for cp in copies:
            cp.wait()

    views, specs = _folded_views(o, tb)
    small = pl.BlockSpec((N_GROUPS, tb, N_HEADS), lambda i: (0, i, 0))
    return _pcall(body, name=name, grid=(S // tb,),
                  in_specs=[pl.BlockSpec((tb, D_MODEL), lambda i: (i, 0)),
                            pl.BlockSpec((D_MODEL, D_MODEL), lambda i: (0, 0))] + specs + [small],
                  out_specs=[pl.BlockSpec(memory_space=pl.ANY), small],
                  out_shape=[jax.ShapeDtypeStruct((N_GROUPS, S, D_MODEL), BF16),
                             jax.ShapeDtypeStruct((N_GROUPS, S, N_HEADS), F32)],
                  scratch_shapes=[pltpu.VMEM((N_GROUPS, N_HEADS, tb, HEAD_DIM), F32),
                                  pltpu.VMEM((N_GROUPS, N_HEADS, tb, HEAD_DIM), F32),
                                  pltpu.VMEM((N_GROUPS, tb, D_MODEL), BF16), pltpu.VMEM((tb, D_MODEL), F32),
                                  pltpu.SemaphoreType.DMA((n_chunks,))],
                  compiler_params=_params(("arbitrary",)))(dx, w_out, *views, lse)


def _shift_rows(a, k):
    return pltpu.roll(a, k % a.shape[0], axis=0)


def _pool_level(g, levels):
    return jnp.where(g == 0, levels[0], jnp.where(g == 1, levels[1], jnp.where(g == 2, levels[2], levels[3])))


def _pool_fwd(name, h, w_in, w_group, scale, x_in, gain_next, tr=1024):
    S, D = h.shape
    H = POOL_HALO
    n_groups = D // POOL_DIM

    def body(h_ref, hh_ref, wi_ref, w_ref, sc_ref, x_ref, gn_ref, y_ref, z_ref, xo_ref, hn_ref, xs_ref):
        i = pl.program_id(0)
        g = pl.program_id(1)
        uv = _dot(h_ref[...], wi_ref[...], "nn")
        halo = jnp.where(i > 0, _dot(hh_ref[...], wi_ref[...], "nn"), 0.0)
        s = jnp.concatenate([halo, uv], axis=0)
        levels = []
        for k in (1, 2, 4, 8):
            s = s + _shift_rows(s, k)
            levels.append(s[H:])
        t = i * tr + lax.broadcasted_iota(jnp.int32, (tr, 1), 0)
        cnt = jnp.minimum(t + 1, jnp.left_shift(2, g)).astype(F32)
        y = _pool_level(g, levels) / cnt - uv
        yb = y.astype(BF16)
        z = _dot(yb, w_ref[...], "nn")
        y_ref[...] = yb
        z_ref[...] = z.astype(z_ref.dtype)
        x_out = x_ref[...] + z * sc_ref[...]
        xo_ref[...] = x_out
        xs_ref[g] = x_out

        @pl.when(g == n_groups - 1)
        def _():
            xf = jnp.concatenate([xs_ref[k] for k in range(n_groups)], axis=1)
            r = lax.rsqrt(jnp.mean(xf * xf, axis=-1, keepdims=True) + RMS_EPS)
            hn_ref[...] = ((xf * r) * gn_ref[...]).astype(hn_ref.dtype)

    blk = pl.BlockSpec((tr, POOL_DIM), lambda i, g: (i, g))
    row = pl.BlockSpec((tr, D), lambda i, g: (i, 0))
    return _pcall(
        body, name=name, grid=(S // tr, n_groups),
        in_specs=[row, pl.BlockSpec((H, D), lambda i, g: (jnp.maximum(i * (tr // H) - 1, 0), 0)),
                  pl.BlockSpec((D, POOL_DIM), lambda i, g: (0, g)),
                  pl.BlockSpec((None, POOL_DIM, POOL_DIM), lambda i, g: (g, 0, 0)),
                  pl.BlockSpec((1, POOL_DIM), lambda i, g: (0, g)), blk, pl.BlockSpec((1, D), lambda i, g: (0, 0))],
        out_specs=[blk, blk, blk, row],
        out_shape=[jax.ShapeDtypeStruct((S, D), BF16), jax.ShapeDtypeStruct((S, D), BF16),
                   jax.ShapeDtypeStruct((S, D), F32), jax.ShapeDtypeStruct((S, D), BF16)],
        scratch_shapes=[pltpu.VMEM((n_groups, tr, POOL_DIM), F32)],
        compiler_params=_params(("parallel", "arbitrary")))(h, h, w_in, w_group, scale, x_in, gain_next)


def _pool_bwd(name, d_out, z, y, scale, w_group, tr=1024):
    S, D = d_out.shape
    H = POOL_HALO

    def body(d_ref, dn_ref, z_ref, y_ref, sc_ref, w_ref, du_ref, dw_ref, dsc_ref):
        g = pl.program_id(0)
        i = pl.program_id(1)
        dv = d_ref[...]
        dz = jnp.concatenate([dv, dn_ref[...]], axis=0) * sc_ref[...]
        dzb = dz.astype(BF16)
        dy = _dot(dzb, w_ref[...], "nt")
        t = i * tr + lax.broadcasted_iota(jnp.int32, (tr + H, 1), 0)
        cnt = jnp.minimum(t + 1, jnp.left_shift(2, g)).astype(F32)
        s = jnp.where(t < S, dy / cnt, 0.0)
        levels = []
        for k in (1, 2, 4, 8):
            s = s + _shift_rows(s, -k)
            levels.append(s[:tr])
        du_ref[...] = (_pool_level(g, levels) - dy[:tr]).astype(du_ref.dtype)
        dw = _dot(y_ref[...], dzb[:tr], "tn")
        dsc = jnp.sum(dv * z_ref[...].astype(F32), axis=0, keepdims=True)

        @pl.when(i == 0)
        def _():
            dw_ref[...] = dw
            dsc_ref[...] = dsc

        @pl.when(i > 0)
        def _():
            dw_ref[...] += dw
            dsc_ref[...] += dsc

    blk = pl.BlockSpec((tr, POOL_DIM), lambda g, i: (i, g))
    vec = pl.BlockSpec((1, POOL_DIM), lambda g, i: (0, g))
    mat = pl.BlockSpec((None, POOL_DIM, POOL_DIM), lambda g, i: (g, 0, 0))
    nxt = pl.BlockSpec((H, POOL_DIM), lambda g, i: (jnp.minimum((i + 1) * (tr // H), S // H - 1), g))
    return _pcall(
        body, name=name, grid=(D // POOL_DIM, S // tr), in_specs=[blk, nxt, blk, blk, vec, mat],
        out_specs=[blk, mat, vec],
        out_shape=[jax.ShapeDtypeStruct((S, D), BF16), jax.ShapeDtypeStruct((D // POOL_DIM, POOL_DIM, POOL_DIM), F32),
                   jax.ShapeDtypeStruct((1, D), F32)],
        compiler_params=_params(("parallel", "arbitrary")))(d_out, d_out, z, y, scale, w_group)


def _row_tile(rows, cols, target_bytes=1 << 20):
    best = rows
    for tr in range(8, rows + 1, 8):
        if rows % tr == 0 and tr * cols * 4 <= target_bytes:
            best = tr
    return best if best * cols * 4 <= 4 * target_bytes else rows


def _adamw_step(w, g, m, v):
    m2 = ADAM_B1 * m + (1.0 - ADAM_B1) * g
    v2 = ADAM_B2 * v + (1.0 - ADAM_B2) * (g * g)
    m_hat = m2 / (1.0 - ADAM_B1 ** ADAM_STEP)
    v_hat = v2 / (1.0 - ADAM_B2 ** ADAM_STEP)
    return -ADAM_LR * (m_hat / (jnp.sqrt(v_hat) + ADAM_EPS) + ADAM_WD * w), m2, v2


def _adamw_refs(w_ref, g_ref, m_ref, v_ref, go_ref, d_ref, mo_ref, vo_ref):
    gv = g_ref[...]
    d_ref[...], mo_ref[...], vo_ref[...] = _adamw_step(w_ref[...], gv, m_ref[...], v_ref[...])
    go_ref[...] = gv


def _adamw(name, items, steps=16):
    n = len(items)

    def body(*refs):
        for t in range(n):
            _adamw_refs(*refs[4 * t:4 * t + 4], *refs[4 * n + 4 * t:4 * n + 4 * t + 4])

    specs, shapes = [], []
    for w, _, _, _ in items:
        R, C = w.shape
        assert R % (8 * steps) == 0, (name, w.shape)
        specs += [pl.BlockSpec((R // steps, C), lambda i: (i, 0))] * 4
        shapes += [jax.ShapeDtypeStruct((R, C), F32)] * 4
    res = _pcall(body, name=name, grid=(steps,), in_specs=specs, out_specs=specs, out_shape=shapes,
                 compiler_params=_params(("parallel",)))(*[a for item in items for a in item])
    return [res[4 * t:4 * t + 4] for t in range(n)]


def _adamw_small(name, items):
    n = len(items)

    def body(*refs):
        for t in range(n):
            _adamw_refs(*refs[4 * t:4 * t + 4], *refs[4 * n + 4 * t:4 * n + 4 * t + 4])

    specs, shapes = [], []
    for w, _, _, _ in items:
        specs += [pl.BlockSpec(w.shape, lambda i: (0, 0))] * 4
        shapes += [jax.ShapeDtypeStruct(w.shape, F32)] * 4
    res = _pcall(body, name=name, grid=(1,), in_specs=specs, out_specs=specs, out_shape=shapes,
                 compiler_params=_params(("arbitrary",)))(*[a for item in items for a in item])
    return [res[4 * t:4 * t + 4] for t in range(n)]


def _sum_leading(name, a, out_dtype):
    n, R, C = a.shape
    tr = _row_tile(R, C) if R % 16 == 0 else R
    if tr % 16:
        tr = R

    def body(a_ref, o_ref):
        acc = a_ref[0].astype(F32)
        for j in range(1, n):
            acc = acc + a_ref[j].astype(F32)
        o_ref[...] = acc.astype(o_ref.dtype)

    return _pcall(body, name=name, grid=(R // tr,), in_specs=[pl.BlockSpec((n, tr, C), lambda i: (0, i, 0))],
                  out_specs=pl.BlockSpec((tr, C), lambda i: (i, 0)), out_shape=jax.ShapeDtypeStruct((R, C), out_dtype),
                  compiler_params=_params(("parallel",)))(a)


def _cast_many(name, items, chip, steps=4):
    tiles = []
    for w, _, dtype in items:
        R = w.shape[1]
        nt = steps if R % (steps * 16) == 0 else 1
        tiles.append((nt, R // nt))

    def body(c_ref, *refs):
        i = pl.program_id(0)
        for t, (nt, _) in enumerate(tiles):
            w_ref, o_ref = refs[t], refs[len(items) + t]

            def cast(w_ref=w_ref, o_ref=o_ref):
                o_ref[...] = w_ref[...].astype(o_ref.dtype)

            if nt == steps:
                cast()
            else:
                pl.when(i < nt)(cast)

    in_specs, out_specs, out_shape = [], [], []
    for (w, layer, dtype), (nt, tr) in zip(items, tiles):
        C = w.shape[2]
        in_specs.append(pl.BlockSpec((None, tr, C), lambda i, c_ref, l=layer, nt=nt: (l, jnp.minimum(i, nt - 1), 0)))
        out_specs.append(pl.BlockSpec((None, tr, C), lambda i, c_ref, nt=nt: (c_ref[0], jnp.minimum(i, nt - 1), 0)))
        out_shape.append(jax.ShapeDtypeStruct((N_CHIPS, w.shape[1], C), dtype))
    return _pcall(body, prefetch=1, name=name, grid=(steps,), in_specs=in_specs, out_specs=out_specs,
                  out_shape=out_shape, compiler_params=_params(("arbitrary",)))(chip, *[w for w, _, _ in items])


def _cast_qkv(name, w, chip, tr=256):
    _, R, C = w.shape

    def body(c_ref, w_ref, own_ref, *piece_refs):
        v = w_ref[...].astype(BF16)
        own_ref[...] = v
        for p, ref in enumerate(piece_refs):
            ref[...] = v[:, p * QKV_TILE:(p + 1) * QKV_TILE]

    piece = pl.BlockSpec((None, tr, QKV_TILE), lambda i, c_ref: (c_ref[0], i, 0))
    return _pcall(body, prefetch=1, name=name, grid=(R // tr,),
                  in_specs=[pl.BlockSpec((None, tr, C), lambda i, c_ref: (0, i, 0))],
                  out_specs=[pl.BlockSpec((tr, C), lambda i, c_ref: (i, 0))] + [piece] * QKV_PIECES,
                  out_shape=[jax.ShapeDtypeStruct((R, C), BF16)]
                  + [jax.ShapeDtypeStruct((N_CHIPS, R, QKV_TILE), BF16)] * QKV_PIECES,
                  compiler_params=_params(("parallel",)))(chip, w)


def _owner_sum(name, mine, landed, shape, chip, core, out, layer, col):
    _, half, C = mine.shape
    k, _ = col
    tr = _row_tile(half, C)
    if tr % 16:
        tr = half
    nrt = half // tr

    def body(ch_ref, co_ref, mine_ref, landed_ref, *rest):
        g = mine_ref[...].astype(F32)
        for j in range(3):
            g = g + landed_ref[j].astype(F32)
        rest[-1][...] = g

    piece = pl.BlockSpec((None, tr, C), lambda i, ch, co: (layer, co[0] * nrt + i, k))
    in_specs = [pl.BlockSpec((None, tr, C), lambda i, ch, co: (ch[0], i, 0)),
                pl.BlockSpec((3, tr, C), lambda i, ch, co: (0, i, 0))]
    operands = [chip, core, mine, landed]
    aliases = {}
    if out is not None:
        in_specs.append(ANY)
        operands.append(out)
        aliases = {4: 0}
    return _pcall(body, prefetch=2, name=name, grid=(nrt,), in_specs=in_specs, out_specs=piece,
                  out_shape=jax.ShapeDtypeStruct(shape, F32), input_output_aliases=aliases,
                  compiler_params=_params(("parallel",)))(*operands)


def _add_my_half(name, ps, qs, core):
    n = len(ps)

    def body(c_ref, *refs):
        for t in range(n):
            p_ref, q_ref, o_ref = refs[t], refs[n + t], refs[2 * n + t]
            o_ref[...] = (p_ref[...].astype(F32) + q_ref[...].astype(F32)).astype(o_ref.dtype)

    halves = [(p.shape[1] // 2, p.shape[2]) for p in ps]
    mine = [pl.BlockSpec((None, h, c), lambda s, c_ref: (s, c_ref[0], 0)) for h, c in halves]
    whole = [pl.BlockSpec((None, h, c), lambda s, c_ref: (s, 0, 0)) for h, c in halves]
    return _pcall(body, prefetch=1, name=name, grid=(N_CHIPS,), in_specs=mine + whole, out_specs=whole,
                  out_shape=[jax.ShapeDtypeStruct((N_CHIPS, h, c), BF16) for h, c in halves],
                  compiler_params=_params(("parallel",)))(core, *ps, *qs)


ANY = pl.BlockSpec(memory_space=pl.ANY)


def _place():
    x, y, c = lax.axis_index("x"), lax.axis_index("y"), lax.axis_index("c")
    other_chips = [(1 - x, y), (x, 1 - y), (1 - x, 1 - y)]
    return x, y, c, other_chips


def _remote(src, dst, send, recv, k, to):
    return pltpu.make_async_remote_copy(src_ref=src, dst_ref=dst, send_sem=send.at[k], recv_sem=recv.at[k],
                                        device_id=to, device_id_type=MESH)


def _in_place(bufs):
    return dict(operands=bufs, out_shapes=[jax.ShapeDtypeStruct(b.shape, b.dtype) for b in bufs],
                aliases={t: t for t in range(len(bufs))})


def _gather_rider(bufs, jobs):
    def copies(ins, outs, send, recv, base=0):
        x, y, c, chips = _place()
        out = []
        for t, over_ici, rows in jobs:
            ref = outs[t]
            if rows is not None:
                a, b, n = rows
                half = ref.shape[1] // 2
                rows = pl.ds(c * half + a * half // n, (b - a) * half // n)
            for px, py in chips:
                slot = 2 * x + y if over_ici else 2 * px + py
                piece = ref.at[slot] if rows is None else ref.at[slot, rows]
                out.append(_remote(piece, piece, send, recv, base + len(out), (px, py, c) if over_ici else (x, y, 1 - c)))
        return out

    return _Rider(n_copies=3 * len(jobs), copies=copies, **_in_place(bufs))


def _gather_and_pass_rider(buf):
    half = buf.shape[1] // 2

    def pieces(outs):
        x, y, c, chips = _place()
        rows = lambda core: pl.ds(core * half, half)
        mine = outs[0].at[2 * x + y, rows(c)]
        landed = [outs[0].at[2 * px + py, rows(c)] for px, py in chips]
        theirs = [outs[0].at[2 * px + py, rows(1 - c)] for px, py in chips]
        return (x, y, c, chips), mine, landed, theirs

    def start(ins, outs, send, recv, base=0):
        (x, y, c, chips), mine, _, _ = pieces(outs)
        for j, (px, py) in enumerate(chips):
            _remote(mine, mine, send, recv, base + j, (px, py, c)).start()

    def finish(ins, outs, send, recv, base=0):
        (x, y, c, chips), mine, landed, theirs = pieces(outs)
        sibling = (x, y, 1 - c)
        passed = []
        for j, (px, py) in enumerate(chips):
            _remote(landed[j], landed[j], send, recv, base + j, (px, py, c)).wait_recv()
            passed.append(_remote(landed[j], landed[j], send, recv, base + 3 + j, sibling))
            passed[-1].start()
        for j in range(3):
            _remote(theirs[j], theirs[j], send, recv, base + 3 + j, sibling).wait_recv()
        for j, (px, py) in enumerate(chips):
            _remote(mine, mine, send, recv, base + j, (px, py, c)).wait_send()
            passed[j].wait_send()

    return _Rider(n_copies=6, start=start, finish=finish, **_in_place([buf]))


def _swap_halves_rider(parts):
    n = len(parts)

    def copies(ins, outs, send, recv, base=0):
        x, y, c, _ = _place()
        out = []
        for t in range(n):
            half = ins[t].shape[1] // 2
            out.append(_remote(ins[t].at[:, pl.ds((1 - c) * half, half)], outs[t], send, recv, base + t,
                               (x, y, 1 - c)))
        return out

    shapes = [jax.ShapeDtypeStruct((p.shape[0], p.shape[1] // 2, p.shape[2]), p.dtype) for p in parts]
    return _Rider(parts, shapes, {}, n, copies)


def _scatter_rider(sums, landed, relations):
    n = len(sums)
    kept = [t for t in range(n) if landed[t] is not None]

    def copies(ins, outs, send, recv, base=0):
        x, y, c, chips = _place()
        out = []
        for t in range(n):
            for j in relations[t]:
                px, py = chips[j]
                out.append(_remote(ins[t].at[2 * px + py], outs[t].at[j], send, recv, base + len(out), (px, py, c)))
        return out

    shapes = [jax.ShapeDtypeStruct((3,) + s.shape[1:], s.dtype) for s in sums]
    return _Rider(list(sums) + [landed[t] for t in kept], shapes, {n + k: t for k, t in enumerate(kept)},
                  sum(len(r) for r in relations), copies)


def _share_rider(blocks, pieces):
    def copies(ins, outs, send, recv, base=0):
        x, y, c, _ = _place()
        out = []
        for k, (t, l, col) in enumerate(pieces):
            ref = outs[t].at[l]
            r2 = ref.shape[0] // 2
            width = ref.shape[1] // col[1]
            piece = ref.at[pl.ds(c * r2, r2), pl.ds(col[0] * width, width)]
            out.append(_remote(piece, piece, send, recv, base + k, (x, y, 1 - c)))
        return out

    return _Rider(n_copies=len(pieces), copies=copies, **_in_place(blocks))


def _gather_small(name, v):
    def body(v_ref, out_ref, send_sem, recv_sem, local_sem):
        x, y, c, _ = _place()
        me = 4 * x + 2 * y + c
        flips = [(fx, fy, fc) for fx in (0, 1) for fy in (0, 1) for fc in (0, 1)][1:]
        local = pltpu.make_async_copy(v_ref, out_ref.at[me], local_sem)
        local.start()
        copies = []
        for j, (fx, fy, fc) in enumerate(flips):
            peer = (x ^ fx, y ^ fy, c ^ fc)
            copies.append(pltpu.make_async_remote_copy(
                src_ref=v_ref, dst_ref=out_ref.at[me], send_sem=send_sem.at[j], recv_sem=recv_sem.at[j],
                device_id=peer, device_id_type=MESH))
        for cp in copies:
            cp.start()
        for j, (fx, fy, fc) in enumerate(flips):
            peer = (x ^ fx, y ^ fy, c ^ fc)
            pltpu.make_async_remote_copy(
                src_ref=v_ref, dst_ref=out_ref.at[4 * peer[0] + 2 * peer[1] + peer[2]], send_sem=send_sem.at[j],
                recv_sem=recv_sem.at[j], device_id=peer, device_id_type=MESH).wait_recv()
        for cp in copies:
            cp.wait_send()
        local.wait()

    return _pcall(body, name=name, in_specs=[ANY], out_specs=ANY,
                  out_shape=jax.ShapeDtypeStruct((8,) + v.shape, v.dtype),
                  scratch_shapes=[pltpu.SemaphoreType.DMA((7,)), pltpu.SemaphoreType.DMA((7,)),
                                  pltpu.SemaphoreType.DMA])(v)


def _fold(a, dil):
    if dil == 1:
        return a
    S, C = a.shape
    return a.reshape(S // dil, dil, C).transpose(1, 0, 2).reshape(S, C)


def _unfold(a, dil):
    if dil == 1:
        return a
    S, C = a.shape
    return a.reshape(dil, S // dil, C).transpose(1, 0, 2).reshape(S, C)


def _fold_groups(a):
    return jnp.stack([_fold(a[g] if a.ndim == 3 else a, d) for g, d in enumerate(ATTN_DILATIONS)])


def _unfold_groups(a):
    return jnp.stack([_unfold(a[g], d) for g, d in enumerate(ATTN_DILATIONS)])


class _NoComm:
    def __init__(self, weights):
        self.weights = weights
        self.grads = {}
        self.chip = jnp.zeros((1,), jnp.int32)

    def w(self, name):
        return self.weights[name]

    def run(self, fn, name, *args, **kw):
        return fn(name, *args, **kw)

    def grad(self, name, value):
        self.grads[name] = value


def _local_step(xs, tgt, attn_norm, ffn_norm, final_norm, comm):
    run = comm.run
    hf = run(_norm_fold, "fold", xs, attn_norm)
    qkv = run(_qkv_tiles, "qkv_own", None, hf, comm.w("wq_own"), None, comm.chip)
    for p in range(QKV_PIECES):
        qkv = run(_qkv_tiles, "qkv_piece%d" % p, p, hf, comm.w("wq%d" % p), qkv, comm.chip)
    o_f, lse_f = run(_attn_fwd, "attn_fwd", qkv)
    lse_t = _unfold_groups(lse_f)
    merged = run(_merge_fwd, "merge_fwd", o_f, lse_t)
    x1, h1 = run(_proj_res_norm, "attn_out", merged, comm.w("o"), xs, ffn_norm[0:1])
    gu0, act0 = run(_ffn_up, "ffn0_up", h1, comm.w("gu0"))
    x2, h2 = run(_proj_res_norm, "ffn0_down", act0, comm.w("d0"), x1, comm.w("pool_norm"))
    y, z, x3, h3 = run(_pool_fwd, "pool_fwd", h2, comm.w("p"), comm.w("pg"), comm.w("pool_scale"), x2, ffn_norm[1:2])
    gu1, act1 = run(_ffn_up, "ffn1_up", h3, comm.w("gu1"))
    loss, dx4, dx4_b, d_final = run(_proj_res_loss, "ffn1_down", act1, comm.w("d1"), x3, final_norm, tgt)

    dgu1 = run(_ffn_down_bwd, "ffn1_down_bwd", dx4_b, comm.w("d1"), gu1)
    comm.grad("d1", run(_mm_tn, "ffn1_down_dw", act1, dx4_b, FF_SHARD, 2048))
    comm.grad("gu1", run(_ffn_dw_up, "ffn1_up_dw", h3, dgu1))
    dx3, d_ffn1 = run(_ffn_dh, "ffn1_up_dh", dgu1, comm.w("gu1"), x3, ffn_norm[1:2], dx4)

    du, dw_pg, d_scale = run(_pool_bwd, "pool_bwd", dx3, z, y, comm.w("pool_scale"), comm.w("pg"))
    comm.grad("pg", dw_pg)
    comm.grad("p", run(_mm_tn, "pool_in_dw", h2, du, D_MODEL, h2.shape[0]))
    dx2, dx2_b, d_pool = run(_dh_norm_bwd, "pool_in_dh", du, comm.w("p"), x2, comm.w("pool_norm"), dx3)

    dgu0 = run(_ffn_down_bwd, "ffn0_down_bwd", dx2_b, comm.w("d0"), gu0)
    comm.grad("d0", run(_mm_tn, "ffn0_down_dw", act0, dx2_b, FF_SHARD, 2048))
    comm.grad("gu0", run(_ffn_dw_up, "ffn0_up_dw", h1, dgu0))
    dx1, d_ffn0 = run(_ffn_dh, "ffn0_up_dh", dgu0, comm.w("gu0"), x1, ffn_norm[0:1], dx2)

    comm.grad("o", run(_mm_tn, "attn_out_dw", merged, dx1, D_MODEL, 2048))
    do_f, dl_t = run(_merge_bwd, "merge_bwd", dx1, comm.w("o"), o_f, lse_t)
    dqkv = run(_attn_bwd, "attn_bwd", qkv, do_f, lse_f, _fold_groups(dl_t))
    for p in range(QKV_PIECES):
        comm.grad("qkv%d" % p, run(_qkv_dw, "qkv_dw%d" % p, p, hf, dqkv))
    dhf = None
    for g in range(N_GROUPS):
        dhf = run(_qkv_dh, "qkv_dh%d" % g, g, dqkv, [comm.w("wq%d" % p) for p in range(QKV_PIECES)], dhf)
    grad_x, d_attn = run(_rms_bwd_folded, "norm_attn_bwd", xs, attn_norm, dhf, dx1)

    small = dict(attn=d_attn, ffn0=d_ffn0, ffn1=d_ffn1, final=d_final, pool=d_pool, scale=d_scale)
    return loss, grad_x, small


TENSORS = ("qkv", "o", "p", "pg", "gu0", "gu1", "d0", "d1")


def _block_of(name):
    if name[:-1] in ("gu", "d"):
        return name[:-1], int(name[-1]), (0, 1)
    if name[:-1] == "qkv":
        return "qkv", 0, (int(name[-1]), QKV_PIECES)
    return name, 0, (0, 1)


class _MeshComm:
    def __init__(self, bufs, shapes, chip_arr, core_arr, pg_rows):
        self.bufs = dict(bufs)
        self.shapes = shapes
        self.chip, self.chip_arr, self.core_arr, self.pg_rows = chip_arr, chip_arr, core_arr, pg_rows
        self.parts, self.sums, self.blocks, self.landed, self.arrived = {}, {}, {}, {}, {}
        move = lambda ici=(), d2d=(): lambda: self.gather(ici, d2d)
        whole = lambda name: lambda: self.gather_and_pass(name)
        swap = lambda *names: lambda: self.swap(names)
        scatter = lambda *names: lambda: self.scatter(names)
        share = lambda *names: lambda: self.share(names)
        self.plan = {
            "fold": [whole("wq0")],
            "qkv_own": [whole("wq1")],
            "qkv_piece0": [whole("wq2")],
            "qkv_piece1": [move(ici=["o", "gu0[0:1/4]"])],
            "qkv_piece2": [move(ici=["gu0[1:2/4]"], d2d=["o", "gu0[0:1/4]"])],
            "attn_fwd": [move(ici=["gu0[2:4/4]", "d0[0:1/2]"], d2d=["gu0[1:2/4]"])],
            "merge_fwd": [move(ici=["p", "pg"], d2d=["gu0[2:4/4]", "d0[0:1/2]"])],
            "attn_out": [move(ici=["d0[1:2/2]", "pool_norm", "pool_scale"], d2d=["p", "pg"])],
            "ffn0_up": [move(ici=["gu1[0:3/4]"], d2d=["d0[1:2/2]"])],
            "ffn0_down": [move(ici=["gu1[3:4/4]", "d1[0:1/2]"], d2d=["gu1[0:3/4]"])],
            "pool_fwd": [move(ici=["d1[1:2/2]"], d2d=["gu1[3:4/4]", "d1[0:1/2]"])],
            "ffn1_up": [move(d2d=["d1[1:2/2]"])],
            "ffn1_up_dh": [swap("d1", "gu1")],
            "pool_bwd": [scatter("d1{01}")],
            "pool_in_dh": [scatter("d1{2}")],
            "ffn0_down_bwd": [scatter("gu1{01}")],
            "ffn0_down_dw": [scatter("gu1{2}")],
            "ffn0_up_dw": [share("gu1", "d1")],
            "ffn0_up_dh": [swap("pg", "p", "d0", "gu0")],
            "merge_bwd": [swap("o")],
            "attn_bwd": [scatter("gu0", "d0", "o")],
            "qkv_dw0": [share("d0"), scatter("p", "pg")],
            "qkv_dw1": [share("gu0", "o"), swap("qkv0")],
            "qkv_dw2": [share("p", "pg"), scatter("qkv0"), swap("qkv1")],
            "qkv_dh0": [share("qkv0"), scatter("qkv1"), swap("qkv2")],
            "qkv_dh1": [share("qkv1"), scatter("qkv2")],
            "qkv_dh2": [share("qkv2")],
        }

    def gather_and_pass(self, name):
        return _gather_and_pass_rider(self.bufs[name]), lambda res: self.bufs.update({name: res[0]})

    def gather(self, ici, d2d):
        names, jobs = [], []
        for over_ici, specs in ((True, ici), (False, d2d)):
            for spec in specs:
                name, _, rows = spec.partition("[")
                if name not in names:
                    names.append(name)
                rng = None
                if name in TENSORS:
                    ab, _, n = (rows[:-1] or "0:1/1").partition("/")
                    rng = (int(ab.split(":")[0]), int(ab.split(":")[1]), int(n))
                jobs.append((names.index(name), over_ici, rng))
        return _gather_rider([self.bufs[n] for n in names], jobs), lambda res: self.bufs.update(zip(names, res))

    def swap(self, names):
        def done(res):
            sums = _add_my_half("chip_sum_" + "_".join(names), [self.parts[n] for n in names], res, self.core_arr)
            self.sums.update(zip(names, sums))
        return _swap_halves_rider([self.parts[n] for n in names]), done

    def scatter(self, specs):
        names = [s.partition("{")[0] for s in specs]
        relations = [tuple(int(ch) for ch in s.partition("{")[2][:-1]) or (0, 1, 2) for s in specs]

        def done(res):
            for n, rel, landed in zip(names, relations, res):
                self.landed[n] = landed
                self.arrived[n] = self.arrived.get(n, ()) + rel
                if len(self.arrived[n]) < 3:
                    continue
                key, layer, col = _block_of(n)
                self.blocks[key] = _owner_sum("owner_sum_" + n, self.sums[n], landed, self.shapes[key],
                                              self.chip_arr, self.core_arr, self.blocks.get(key), layer, col)
        rider = _scatter_rider([self.sums[n] for n in names], [self.landed.get(n) for n in names], relations)
        return rider, done

    def share(self, names):
        keys, pieces = [], []
        for n in names:
            key, layer, col = _block_of(n)
            if key not in keys:
                keys.append(key)
            pieces.append((keys.index(key), layer, col))
        return _share_rider([self.blocks[k] for k in keys], pieces), lambda res: self.blocks.update(zip(keys, res))

    def run(self, fn, name, *args, **kw):
        made = [make() for make in self.plan.get(name, [])]
        if not made:
            return fn(name, *args, **kw)
        riders = [r for r, _ in made]
        out = _ride(riders[0] if len(riders) == 1 else _riders(*riders), fn, name, *args, **kw)
        for r, done in made:
            done(r.results)
        return out

    def w(self, name):
        b = self.bufs[name]
        if name in ("o", "p", "d0", "d1"):
            return b.reshape(-1, b.shape[-1])
        if name == "pg":
            b = b.reshape(N_CHIPS, 4, self.pg_rows, POOL_DIM).transpose(1, 0, 2, 3)
            return b.reshape(4, POOL_DIM, POOL_DIM)
        if name in ("pool_norm", "pool_scale"):
            return b.reshape(1, -1)
        return b

    def grad(self, name, value):
        if name == "pg":
            value = value.reshape(4, N_CHIPS, self.pg_rows, POOL_DIM).transpose(1, 0, 2, 3)
            value = value.reshape(N_CHIPS, 4 * self.pg_rows, POOL_DIM).astype(BF16)
        elif name in ("o", "p", "d0", "d1"):
            value = value.reshape(N_CHIPS, value.shape[0] // N_CHIPS, value.shape[1])
        self.parts[name] = value


def kernel(x, attn_norm, w_qkv, w_attn_out, pool_norm, w_pool_in, w_pool_group, pool_scale, ffn_norm, w_ffn_gate_up, w_ffn_down, final_norm, loss_target, m_attn_norm, m_w_qkv, m_w_attn_out, m_pool_norm, m_w_pool_in, m_w_pool_group, m_pool_scale, m_ffn_norm, m_w_ffn_gate_up, m_w_ffn_down, m_final_norm, v_attn_norm, v_w_qkv, v_w_attn_out, v_pool_norm, v_w_pool_in, v_w_pool_group, v_pool_scale, v_ffn_norm, v_w_ffn_gate_up, v_w_ffn_down, v_final_norm):
    D = D_MODEL
    chip = 2 * lax.axis_index("x") + lax.axis_index("y")
    core = lax.axis_index("c")
    pg_rows = w_pool_group.shape[2]

    chip_arr = chip.astype(jnp.int32).reshape(1)
    core_arr = core.astype(jnp.int32).reshape(1)

    pieces = _cast_qkv("cast_qkv", w_qkv, chip_arr)
    bufs = dict(zip(["wq_own"] + ["wq%d" % p for p in range(QKV_PIECES)], pieces))
    shards = [(w_attn_out, 0, BF16), (w_pool_in, 0, BF16), (w_pool_group.reshape(1, 4 * pg_rows, POOL_DIM), 0, BF16),
              (w_ffn_gate_up, 0, BF16), (w_ffn_gate_up, 1, BF16), (w_ffn_down, 0, BF16), (w_ffn_down, 1, BF16),
              (pool_norm[None], 0, F32), (pool_scale[None], 0, F32)]
    bufs.update(zip(TENSORS[1:] + ("pool_norm", "pool_scale"), _cast_many("cast_rest", shards, chip_arr)))

    as_block = lambda a: a.reshape((-1,) + a.shape[-2:]) if a.ndim == 3 else a.reshape(1, -1, a.shape[-1])
    owned = dict(qkv=(w_qkv, m_w_qkv, v_w_qkv), o=(w_attn_out, m_w_attn_out, v_w_attn_out),
                 p=(w_pool_in, m_w_pool_in, v_w_pool_in), pg=(w_pool_group, m_w_pool_group, v_w_pool_group),
                 gu=(w_ffn_gate_up, m_w_ffn_gate_up, v_w_ffn_gate_up), d=(w_ffn_down, m_w_ffn_down, v_w_ffn_down))
    comm = _MeshComm(bufs, {k: as_block(wmv[0]).shape for k, wmv in owned.items()}, chip_arr, core_arr, pg_rows)
    loss_part, grad_x, small = _local_step(x[0], loss_target[0], attn_norm, ffn_norm, final_norm.reshape(1, D), comm)

    rows = jnp.concatenate([small["attn"], small["ffn0"], small["ffn1"], small["final"], small["pool"],
                            small["scale"], jnp.pad(loss_part, ((0, 0), (0, D - 1))), jnp.zeros((1, D), F32)], axis=0)
    all_rows = _gather_small("gather_gain_grads", rows)
    tot = _sum_leading("sum_gain_grads", all_rows, F32)
    loss = tot[6, 0]
    g_attn_norm = tot[0:1]
    g_ffn_norm = tot[1:3]
    g_final_norm = tot[3]
    g_pool_norm = lax.dynamic_slice(tot, (4, chip * POOL_DIM), (1, POOL_DIM))
    g_pool_scale = lax.dynamic_slice(tot, (5, chip * POOL_DIM), (1, POOL_DIM))

    as_rows = lambda a: a.reshape(-1, a.shape[-1])
    res = _adamw("adamw_weights", [(as_rows(w), as_rows(comm.blocks[k]), as_rows(m), as_rows(v))
                                   for k, (w, m, v) in owned.items()])
    updated = {k: [a.reshape(owned[k][0].shape) for a in r] for k, r in zip(owned, res)}
    gains = [(attn_norm, g_attn_norm, m_attn_norm, v_attn_norm), (pool_norm, g_pool_norm, m_pool_norm, v_pool_norm),
             (pool_scale, g_pool_scale, m_pool_scale, v_pool_scale), (ffn_norm, g_ffn_norm, m_ffn_norm, v_ffn_norm),
             (final_norm, g_final_norm, m_final_norm, v_final_norm)]
    small_res = _adamw_small("adamw_gains", [tuple(as_rows(a) for a in item) for item in gains])
    small_res = [[a.reshape(item[0].shape) for a in r] for r, item in zip(small_res, gains)]
    results = [small_res[0], updated["qkv"], updated["o"], small_res[1], updated["p"], updated["pg"], small_res[2],
               small_res[3], updated["gu"], updated["d"], small_res[4]]
    grads = [r[0] for r in results]
    deltas = [r[1] for r in results]
    new_m = [r[2] for r in results]
    new_v = [r[3] for r in results]
    return (loss, grad_x[None], *grads, *deltas, *new_m, *new_v)
```

```python
import jax
import jax.numpy as jnp
from jax import lax
from jax.experimental import pallas as pl
from jax.experimental.pallas import tpu as pltpu

F32 = jnp.float32
BF16 = jnp.bfloat16
MESH = pl.DeviceIdType.MESH

D_MODEL = 1024
N_HEADS = 8
HEAD_DIM = 128
Q_BLOCK = 128
ATTN_DILATIONS = (1, 4, 16)
N_GROUPS = 3
D_FF = 2816
N_CHIPS = 4
FF_SHARD = 2 * D_FF // N_CHIPS
QKV_SHARD = 3 * 3 * D_MODEL // N_CHIPS
QKV_TILE = 768
POOL_DIM = 256
POOL_HALO = 16
RMS_EPS = 1e-6
NEG = -1e30
ADAM_LR = 0.001
ADAM_B1 = 0.9
ADAM_B2 = 0.999
ADAM_EPS = 1e-08
ADAM_WD = 0.01
ADAM_STEP = 10
VMEM_LIMIT = 56 * 1024 * 1024

_DN = {
    "nn": (((1,), (0,)), ((), ())),
    "nt": (((1,), (1,)), ((), ())),
    "tn": (((0,), (0,)), ((), ())),
}


class _Rider:
    def __init__(self, operands, out_shapes, aliases, n_copies, copies=None, start=None, finish=None):
        self.operands = list(operands)
        self.out_shapes = list(out_shapes)
        self.aliases = dict(aliases)
        self.n_copies = n_copies
        self.results = None

        def start_copies(ins, outs, send, recv, base=0):
            for cp in copies(ins, outs, send, recv, base):
                cp.start()

        def wait_copies(ins, outs, send, recv, base=0):
            for cp in copies(ins, outs, send, recv, base):
                cp.wait()

        self.start = start or start_copies
        self.finish = finish or wait_copies


def _riders(*riders):
    operands, out_shapes, aliases, offsets = [], [], {}, []
    n = 0
    for r in riders:
        offsets.append((len(operands), len(out_shapes), n))
        aliases.update({len(operands) + i: len(out_shapes) + o for i, o in r.aliases.items()})
        operands += r.operands
        out_shapes += r.out_shapes
        n += r.n_copies

    def each(which):
        def go(ins, outs, send, recv, base=0):
            for r, (i0, o0, s0) in zip(riders, offsets):
                getattr(r, which)(ins[i0:i0 + len(r.operands)], outs[o0:o0 + len(r.out_shapes)], send, recv,
                                  base + s0)
        return go

    both = _Rider(operands, out_shapes, aliases, n, start=each("start"), finish=each("finish"))
    both.parts = (riders, offsets)
    return both


_pending_rider = []


def _ride(rider, fn, *args, **kw):
    _pending_rider.append(rider)
    out = fn(*args, **kw)
    assert not _pending_rider
    if hasattr(rider, "parts"):
        for r, (_, o0, _) in zip(*rider.parts):
            r.results = rider.results[o0:o0 + len(r.out_shapes)]
    return out


def _pcall(body, prefetch=0, **kw):
    def build(body, kw):
        if not prefetch:
            return pl.pallas_call(body, **kw)
        kw = dict(kw)
        grid_spec = pltpu.PrefetchScalarGridSpec(
            num_scalar_prefetch=prefetch, grid=kw.pop("grid"), in_specs=kw.pop("in_specs"),
            out_specs=kw.pop("out_specs"), scratch_shapes=kw.pop("scratch_shapes", []))
        return pl.pallas_call(body, grid_spec=grid_spec, **kw)

    if not _pending_rider:
        return build(body, kw)
    rider = _pending_rider.pop()
    grid = kw.get("grid", ())
    in_specs = list(kw.get("in_specs", []))
    single = not isinstance(kw["out_shape"], (list, tuple))
    out_shape = [kw["out_shape"]] if single else list(kw["out_shape"])
    out_specs = [kw["out_specs"]] if single else list(kw["out_specs"])
    scratch = list(kw.get("scratch_shapes", []))
    n_in, n_out, n_scr = len(in_specs), len(out_shape), len(scratch)
    r_in, r_out = len(rider.operands), len(rider.out_shapes)

    def wrapped(*refs):
        scalars, refs = refs[:prefetch], refs[prefetch:]
        ins, rins = refs[:n_in], refs[n_in:n_in + r_in]
        outs = refs[n_in + r_in:n_in + r_in + n_out]
        routs = refs[n_in + r_in + n_out:n_in + r_in + n_out + r_out]
        scr = refs[n_in + r_in + n_out + r_out:n_in + r_in + n_out + r_out + n_scr]
        send, recv = refs[-2:]
        ids = [pl.program_id(a) for a in range(len(grid))]
        first, last = True, True
        for a, pid in enumerate(ids):
            first = jnp.logical_and(first, pid == 0)
            last = jnp.logical_and(last, pid == grid[a] - 1)
        if grid:
            pl.when(first)(lambda: rider.start(rins, routs, send, recv))
        else:
            rider.start(rins, routs, send, recv)
        body(*scalars, *ins, *outs, *scr)
        if grid:
            pl.when(last)(lambda: rider.finish(rins, routs, send, recv))
        else:
            rider.finish(rins, routs, send, recv)

    any_spec = pl.BlockSpec(memory_space=pl.ANY)
    kw2 = dict(kw)
    kw2.update(
        in_specs=in_specs + [any_spec] * r_in, out_specs=out_specs + [any_spec] * r_out,
        out_shape=out_shape + rider.out_shapes,
        scratch_shapes=scratch + [pltpu.SemaphoreType.DMA((rider.n_copies,)),
                                  pltpu.SemaphoreType.DMA((rider.n_copies,))],
        input_output_aliases={**kw.get("input_output_aliases", {}),
                              **{prefetch + n_in + i: n_out + o for i, o in rider.aliases.items()}})
    if grid:
        kw2["compiler_params"] = _params(("arbitrary",) * len(grid))
    call = build(wrapped, kw2)

    def run(*operands):
        res = call(*operands, *rider.operands)
        rider.results = list(res[n_out:])
        return res[0] if single else list(res[:n_out])

    return run


def _params(sem):
    return pltpu.CompilerParams(dimension_semantics=sem, vmem_limit_bytes=VMEM_LIMIT)


def _dot(a, b, mode):
    return lax.dot_general(a, b, _DN[mode], preferred_element_type=F32)


def _mm(name, mode, grid, a, a_spec, b, b_spec, out_shape, o_spec, acc_shape):
    nk = grid[-1]

    def body(a_ref, b_ref, o_ref, *acc):
        part = _dot(a_ref[...].astype(BF16), b_ref[...].astype(BF16), mode)
        if nk == 1:
            o_ref[...] = part.astype(o_ref.dtype)
            return
        acc_ref, = acc
        k = pl.program_id(len(grid) - 1)

        @pl.when(k == 0)
        def _():
            acc_ref[...] = part

        @pl.when(k > 0)
        def _():
            acc_ref[...] += part

        @pl.when(k == nk - 1)
        def _():
            o_ref[...] = acc_ref[...].astype(o_ref.dtype)

    scratch = [] if nk == 1 else [pltpu.VMEM(acc_shape, F32)]
    sem = ("parallel",) * (len(grid) - 1) + ("arbitrary",)
    return _pcall(body, name=name, grid=grid, in_specs=[a_spec, b_spec], out_specs=o_spec, out_shape=out_shape,
                  scratch_shapes=scratch, compiler_params=_params(sem))(a, b)


def _mm_tn(name, a, b, tm, tk):
    T, M = a.shape
    N = b.shape[1]
    return _mm(name, "tn", (M // tm, T // tk), a, pl.BlockSpec((tk, tm), lambda i, k: (k, i)),
               b, pl.BlockSpec((tk, N), lambda i, k: (k, 0)),
               jax.ShapeDtypeStruct((M, N), BF16), pl.BlockSpec((tm, N), lambda i, k: (i, 0)), (tm, N))


def _norm_bwd_rows(xf, gain, dh, dres):
    r = lax.rsqrt(jnp.mean(xf * xf, axis=-1, keepdims=True) + RMS_EPS)
    xh = xf * r
    t = dh * gain
    dx = dres + r * (t - xh * jnp.mean(t * xh, axis=-1, keepdims=True))
    return dx, jnp.sum(dh * xh, axis=0, keepdims=True)


def _accumulate(ref, value, first):
    @pl.when(first)
    def _():
        ref[...] = value

    @pl.when(jnp.logical_not(first))
    def _():
        ref[...] += value


def _rms_bwd_folded(name, x, g, dhf, dres, tb=512):
    S, D = x.shape

    def body(x_ref, g_ref, d0_ref, d1_ref, d2_ref, dres_ref, dx_ref, dg_ref, dh_ref):
        chunks = _lane_chunks(D)
        for c, cols in enumerate(chunks):
            dh_ref[c] = d0_ref[0, :, cols].astype(F32)
        for dil, ref in ((ATTN_DILATIONS[1], d1_ref), (ATTN_DILATIONS[2], d2_ref)):
            n = tb // dil
            for k in range(dil):
                for c, cols in enumerate(chunks):
                    dh_ref[c, _strided_rows(k, n, dil), :] += ref[k, :, cols].astype(F32)
        dh = jnp.concatenate([dh_ref[c] for c in range(len(chunks))], axis=1)
        dx, dg = _norm_bwd_rows(x_ref[...], g_ref[...], dh, dres_ref[...])
        dx_ref[...] = dx
        _accumulate(dg_ref, dg, pl.program_id(0) == 0)

    views, specs = _folded_views(dhf, tb)
    row = pl.BlockSpec((tb, D), lambda i: (i, 0))
    vec = pl.BlockSpec((1, D), lambda i: (0, 0))
    return _pcall(body, name=name, grid=(S // tb,), in_specs=[row, vec] + specs + [row], out_specs=[row, vec],
                  out_shape=[jax.ShapeDtypeStruct((S, D), F32), jax.ShapeDtypeStruct((1, D), F32)],
                  scratch_shapes=[pltpu.VMEM((D // LANES, tb, LANES), F32)],
                  compiler_params=_params(("arbitrary",)))(x, g, *views, dres)


def _proj_res_norm(name, a, w, res, gain, tm=512):
    M, K = a.shape
    D = w.shape[1]

    def body(a_ref, w_ref, res_ref, g_ref, x_ref, h_ref):
        xf = res_ref[...] + _dot(a_ref[...], w_ref[...], "nn")
        x_ref[...] = xf
        r = lax.rsqrt(jnp.mean(xf * xf, axis=-1, keepdims=True) + RMS_EPS)
        h_ref[...] = ((xf * r) * g_ref[...]).astype(h_ref.dtype)

    row = pl.BlockSpec((tm, D), lambda i: (i, 0))
    return _pcall(body, name=name, grid=(M // tm,),
                  in_specs=[pl.BlockSpec((tm, K), lambda i: (i, 0)), pl.BlockSpec((K, D), lambda i: (0, 0)), row,
                            pl.BlockSpec((1, D), lambda i: (0, 0))],
                  out_specs=[row, row],
                  out_shape=[jax.ShapeDtypeStruct((M, D), F32), jax.ShapeDtypeStruct((M, D), BF16)],
                  compiler_params=_params(("parallel",)))(a, w, res, gain)


def _proj_res_loss(name, a, w, res, gain, tgt, tm=512):
    M, K = a.shape
    D = w.shape[1]

    def body(a_ref, w_ref, res_ref, g_ref, t_ref, loss_ref, dx_ref, dxb_ref, dg_ref):
        first = pl.program_id(0) == 0
        xf = res_ref[...] + _dot(a_ref[...], w_ref[...], "nn")
        r = lax.rsqrt(jnp.mean(xf * xf, axis=-1, keepdims=True) + RMS_EPS)
        xh = xf * r
        gv = g_ref[...]
        e = xh * gv - t_ref[...]
        lp = 0.5 * jnp.sum(jnp.mean(e * e, axis=-1, keepdims=True), axis=0, keepdims=True)
        dy = e * (1.0 / D)
        t = dy * gv
        dx = r * (t - xh * jnp.mean(t * xh, axis=-1, keepdims=True))
        dx_ref[...] = dx
        dxb_ref[...] = dx.astype(dxb_ref.dtype)
        _accumulate(dg_ref, jnp.sum(dy * xh, axis=0, keepdims=True), first)
        _accumulate(loss_ref, lp, first)

    row = pl.BlockSpec((tm, D), lambda i: (i, 0))
    vec = pl.BlockSpec((1, D), lambda i: (0, 0))
    return _pcall(body, name=name, grid=(M // tm,),
                  in_specs=[pl.BlockSpec((tm, K), lambda i: (i, 0)), pl.BlockSpec((K, D), lambda i: (0, 0)), row,
                            vec, row],
                  out_specs=[pl.BlockSpec((1, 1), lambda i: (0, 0)), row, row, vec],
                  out_shape=[jax.ShapeDtypeStruct((1, 1), F32), jax.ShapeDtypeStruct((M, D), F32),
                             jax.ShapeDtypeStruct((M, D), BF16), jax.ShapeDtypeStruct((1, D), F32)],
                  compiler_params=_params(("arbitrary",)))(a, w, res, gain, tgt)


def _dh_norm_bwd(name, d, w, x, gain, dres, tm=512):
    M, N = d.shape
    D = w.shape[0]

    def body(d_ref, w_ref, x_ref, g_ref, dres_ref, dx_ref, dxb_ref, dg_ref):
        dh = _dot(d_ref[...].astype(BF16), w_ref[...], "nt")
        dx, dg = _norm_bwd_rows(x_ref[...], g_ref[...], dh, dres_ref[...])
        dx_ref[...] = dx
        dxb_ref[...] = dx.astype(dxb_ref.dtype)
        _accumulate(dg_ref, dg, pl.program_id(0) == 0)

    row = pl.BlockSpec((tm, D), lambda i: (i, 0))
    vec = pl.BlockSpec((1, D), lambda i: (0, 0))
    return _pcall(body, name=name, grid=(M // tm,),
                  in_specs=[pl.BlockSpec((tm, N), lambda i: (i, 0)), pl.BlockSpec((D, N), lambda i: (0, 0)), row, vec,
                            row],
                  out_specs=[row, row, vec],
                  out_shape=[jax.ShapeDtypeStruct((M, D), F32), jax.ShapeDtypeStruct((M, D), BF16),
                             jax.ShapeDtypeStruct((1, D), F32)],
                  compiler_params=_params(("arbitrary",)))(d, w, x, gain, dres)


def _sigmoid(v):
    return 0.5 * jnp.tanh(0.5 * v) + 0.5


def _ffn_up(name, h, w_gu, tm=1024):
    S, D = h.shape
    W = FF_SHARD

    def body(h_ref, wg_ref, wu_ref, gu_ref, act_ref):
        hv = h_ref[...]
        gate = _dot(hv, wg_ref[...], "nn")
        up = _dot(hv, wu_ref[...], "nn")
        gu_ref[0] = gate.astype(gu_ref.dtype)
        gu_ref[1] = up.astype(gu_ref.dtype)
        sig = _sigmoid(gate)
        act_ref[...] = ((gate * sig) * up).astype(act_ref.dtype)

    return _pcall(
        body, name=name, grid=(2, S // tm),
        in_specs=[pl.BlockSpec((tm, D), lambda j, i: (i, 0)),
                  pl.BlockSpec((None, D, W), lambda j, i: (j, 0, 0)),
                  pl.BlockSpec((None, D, W), lambda j, i: (j + 2, 0, 0))],
        out_specs=[pl.BlockSpec((2, tm, W), lambda j, i: (0, i, j)), pl.BlockSpec((tm, W), lambda j, i: (i, j))],
        out_shape=[jax.ShapeDtypeStruct((2, S, D_FF), BF16), jax.ShapeDtypeStruct((S, D_FF), BF16)],
        compiler_params=_params(("parallel", "parallel")))(h, w_gu, w_gu)


def _ffn_down_bwd(name, dx, w_down, gu, tm=1024):
    S, D = dx.shape
    W = FF_SHARD

    def body(dx_ref, w_ref, gu_ref, dgu_ref):
        dact = _dot(dx_ref[...].astype(BF16), w_ref[...], "nt")
        gate = gu_ref[0].astype(F32)
        up = gu_ref[1].astype(F32)
        sig = _sigmoid(gate)
        silu = gate * sig
        dgu_ref[0] = (dact * up * (sig * (1.0 + gate * (1.0 - sig)))).astype(dgu_ref.dtype)
        dgu_ref[1] = (dact * silu).astype(dgu_ref.dtype)

    blk = pl.BlockSpec((2, tm, W), lambda j, i: (0, i, j))
    return _pcall(
        body, name=name, grid=(2, S // tm),
        in_specs=[pl.BlockSpec((tm, D), lambda j, i: (i, 0)), pl.BlockSpec((W, D), lambda j, i: (j, 0)), blk],
        out_specs=blk, out_shape=jax.ShapeDtypeStruct((2, S, D_FF), BF16),
        compiler_params=_params(("parallel", "parallel")))(dx, w_down, gu)


def _ffn_dw_up(name, h, dgu, tk=2048):
    S, D = h.shape
    W = FF_SHARD
    tk = min(tk, S)
    return _mm(name, "tn", (N_CHIPS, S // tk), h, pl.BlockSpec((tk, D), lambda s, k: (k, 0)),
               dgu, pl.BlockSpec((None, tk, W), lambda s, k: (s // 2, k, s % 2)),
               jax.ShapeDtypeStruct((N_CHIPS, D, W), BF16), pl.BlockSpec((None, D, W), lambda s, k: (s, 0, 0)),
               (D, W))


def _ffn_dh(name, dgu, w_gu, x, gain, dres, tm=512):
    S = dgu.shape[1]
    W = FF_SHARD

    def body(a_ref, w_hbm, x_ref, g_ref, dres_ref, dx_ref, dg_ref, w_ref, acat_ref, sems):
        first = pl.program_id(0) == 0

        @pl.when(first)
        def _():
            copies = [pltpu.make_async_copy(w_hbm.at[s], w_ref.at[:, pl.ds(s * W, W)], sems.at[s])
                      for s in range(N_CHIPS)]
            for cp in copies:
                cp.start()
            for cp in copies:
                cp.wait()

        acat_ref[:, :D_FF] = a_ref[0]
        acat_ref[:, D_FF:] = a_ref[1]
        dh = _dot(acat_ref[...], w_ref[...], "nt")
        dx, dg = _norm_bwd_rows(x_ref[...], g_ref[...], dh, dres_ref[...])
        dx_ref[...] = dx
        _accumulate(dg_ref, dg, first)

    row = pl.BlockSpec((tm, D_MODEL), lambda i: (i, 0))
    vec = pl.BlockSpec((1, D_MODEL), lambda i: (0, 0))
    return _pcall(body, name=name, grid=(S // tm,),
                  in_specs=[pl.BlockSpec((2, tm, D_FF), lambda i: (0, i, 0)), pl.BlockSpec(memory_space=pl.ANY),
                            row, vec, row],
                  out_specs=[row, vec],
                  out_shape=[jax.ShapeDtypeStruct((S, D_MODEL), F32), jax.ShapeDtypeStruct((1, D_MODEL), F32)],
                  scratch_shapes=[pltpu.VMEM((D_MODEL, 2 * D_FF), BF16), pltpu.VMEM((tm, 2 * D_FF), BF16),
                                  pltpu.SemaphoreType.DMA((N_CHIPS,))],
                  compiler_params=_params(("arbitrary",)))(dgu, w_gu, x, gain, dres)


FOLD_TOKENS = 1024
LANES = 128


def _lane_chunks(width):
    return [slice(c * LANES, (c + 1) * LANES) for c in range(width // LANES)]


def _strided_rows(r, n, dil):
    return pl.ds(r, n) if dil == 1 else pl.ds(r, n, stride=dil)


def _norm_fold(name, x, gain):
    S, D = x.shape
    chunks = _lane_chunks(D)

    def body(x_ref, g_ref, hf_hbm, xc_ref, hs_ref, sems):
        i = pl.program_id(0)
        xf = x_ref[...]
        inv = lax.rsqrt(jnp.mean(xf * xf, axis=-1, keepdims=True) + RMS_EPS)
        h = (xf * inv) * g_ref[...]
        hs_ref[0] = h.astype(BF16)
        for c, cols in enumerate(chunks):
            xc_ref[c] = h[:, cols]
        copies = []
        for g, dil in enumerate(ATTN_DILATIONS):
            n = FOLD_TOKENS // dil
            for r in range(dil):
                if dil > 1:
                    for c, cols in enumerate(chunks):
                        hs_ref[g, r * n:(r + 1) * n, cols] = xc_ref[c, _strided_rows(r, n, dil), :].astype(BF16)
                cp = pltpu.make_async_copy(hs_ref.at[g, pl.ds(r * n, n)],
                                           hf_hbm.at[g, pl.ds(r * (S // dil) + i * n, n)], sems.at[len(copies)])
                cp.start()
                copies.append(cp)
        for cp in copies:
            cp.wait()

    return _pcall(body, name=name, grid=(S // FOLD_TOKENS,),
                  in_specs=[pl.BlockSpec((FOLD_TOKENS, D), lambda i: (i, 0)), pl.BlockSpec((1, D), lambda i: (0, 0))],
                  out_specs=pl.BlockSpec(memory_space=pl.ANY),
                  out_shape=jax.ShapeDtypeStruct((N_GROUPS, S, D), BF16),
                  scratch_shapes=[pltpu.VMEM((D // LANES, FOLD_TOKENS, LANES), F32),
                                  pltpu.VMEM((N_GROUPS, FOLD_TOKENS, D), BF16),
                                  pltpu.SemaphoreType.DMA((sum(ATTN_DILATIONS),))],
                  compiler_params=_params(("arbitrary",)))(x, gain)


def _qkv_tiles(name, piece, hf, w, qkv, chip, tm=1024):
    S = hf.shape[1]
    per_group = 3 * D_MODEL // QKV_TILE

    def tile(q, c_ref):
        if piece is None:
            return QKV_PIECES * c_ref[0] + q
        return QKV_PIECES * ((c_ref[0] + 1 + q) % N_CHIPS) + piece

    def body(*refs):
        a_ref, w_ref, o_ref = refs[1], refs[2], refs[-1]
        o_ref[...] = _dot(a_ref[...], w_ref[...], "nn").astype(o_ref.dtype)

    if piece is None:
        w_spec = pl.BlockSpec((D_MODEL, QKV_TILE), lambda q, i, c_ref: (0, q))
    else:
        w_spec = pl.BlockSpec((None, D_MODEL, QKV_TILE), lambda q, i, c_ref: ((c_ref[0] + 1 + q) % N_CHIPS, 0, 0))
    in_specs = [pl.BlockSpec((None, tm, D_MODEL), lambda q, i, c_ref: (tile(q, c_ref) // per_group, i, 0)), w_spec]
    operands = [chip, hf, w]
    aliases = {}
    if qkv is not None:
        in_specs.append(pl.BlockSpec(memory_space=pl.ANY))
        operands.append(qkv)
        aliases = {3: 0}
    return _pcall(
        body, prefetch=1, name=name, grid=(N_CHIPS - 1, S // tm), in_specs=in_specs,
        out_specs=pl.BlockSpec((None, tm, QKV_TILE),
                               lambda q, i, c_ref: (tile(q, c_ref) // per_group, i, tile(q, c_ref) % per_group)),
        out_shape=jax.ShapeDtypeStruct((N_GROUPS, S, 3 * D_MODEL), BF16), input_output_aliases=aliases,
        compiler_params=_params(("arbitrary", "arbitrary")))(*operands)


QKV_PIECES = QKV_SHARD // QKV_TILE


def _qkv_dw(name, p, hf, dqkv):
    S = hf.shape[1]
    per_group = 3 * D_MODEL // QKV_TILE
    tile = lambda s: QKV_PIECES * s + p
    return _mm(name, "tn", (N_CHIPS, 1),
               hf, pl.BlockSpec((None, S, D_MODEL), lambda s, k: (tile(s) // per_group, 0, 0)),
               dqkv, pl.BlockSpec((None, S, QKV_TILE), lambda s, k: (tile(s) // per_group, 0, tile(s) % per_group)),
               jax.ShapeDtypeStruct((N_CHIPS, D_MODEL, QKV_TILE), BF16),
               pl.BlockSpec((None, D_MODEL, QKV_TILE), lambda s, k: (s, 0, 0)), None)


def _qkv_dh(name, g, dqkv, w_pieces, dhf, tm=1024):
    S = dqkv.shape[1]
    per_group = 3 * D_MODEL // QKV_TILE

    def body(*refs):
        a_ref, w_hbm = refs[0], refs[1:1 + QKV_PIECES]
        o_ref, w_ref, sems = refs[-3:]

        @pl.when(pl.program_id(0) == 0)
        def _():
            copies = []
            for j in range(per_group):
                t = per_group * g + j
                copies.append(pltpu.make_async_copy(w_hbm[t % QKV_PIECES].at[t // QKV_PIECES],
                                                    w_ref.at[:, pl.ds(j * QKV_TILE, QKV_TILE)], sems.at[j]))
            for cp in copies:
                cp.start()
            for cp in copies:
                cp.wait()

        o_ref[...] = _dot(a_ref[...], w_ref[...], "nt").astype(o_ref.dtype)

    any_spec = pl.BlockSpec(memory_space=pl.ANY)
    in_specs = [pl.BlockSpec((None, tm, 3 * D_MODEL), lambda i: (g, i, 0))] + [any_spec] * QKV_PIECES
    operands = [dqkv] + list(w_pieces)
    aliases = {}
    if dhf is not None:
        in_specs.append(any_spec)
        operands.append(dhf)
        aliases = {1 + QKV_PIECES: 0}
    return _pcall(body, name=name, grid=(S // tm,), in_specs=in_specs,
                  out_specs=pl.BlockSpec((None, tm, D_MODEL), lambda i: (g, i, 0)),
                  out_shape=jax.ShapeDtypeStruct((N_GROUPS, S, D_MODEL), BF16),
                  scratch_shapes=[pltpu.VMEM((D_MODEL, 3 * D_MODEL), BF16), pltpu.SemaphoreType.DMA((per_group,))],
                  input_output_aliases=aliases, compiler_params=_params(("arbitrary",)))(*operands)


def _alibi_coef(g, h):
    vals = [-(2.0 ** (-8.0 * (gg * N_HEADS + h + 1) / (N_GROUPS * N_HEADS))) * ATTN_DILATIONS[gg]
            for gg in range(N_GROUPS)]
    return jnp.where(g == 0, vals[0], jnp.where(g == 1, vals[1], vals[2])).astype(F32)


def _blocks_per_segment(g, S):
    return jnp.right_shift(S // Q_BLOCK, 2 * g)


def _band_bias(pen):
    row = lax.broadcasted_iota(jnp.int32, (Q_BLOCK, 2 * Q_BLOCK), 0)
    col = lax.broadcasted_iota(jnp.int32, (Q_BLOCK, 2 * Q_BLOCK), 1)
    dist = Q_BLOCK + row - col
    valid = jnp.logical_and(dist >= 0, dist <= Q_BLOCK)
    base = jnp.where(valid, jnp.where(col < Q_BLOCK, pen, 0.0), NEG).astype(F32)
    return base, dist.astype(F32)


def _skewed(n_units, stages):
    state = {}
    for step in range(n_units + len(stages) - 1):
        for s, stage in enumerate(stages):
            u = step - s
            if 0 <= u < n_units:
                state[u] = stage(u, state.get(u))
                if s == len(stages) - 1:
                    del state[u]


def _attn_fwd(name, qkv, tq=1024):
    S = qkv.shape[1]
    nqb = tq // Q_BLOCK
    scale = HEAD_DIM ** -0.5

    def body(q_ref, k_ref, kp_ref, v_ref, vp_ref, o_ref, lse_ref, kf_ref, vf_ref):
        g = pl.program_id(0)
        i = pl.program_id(1)
        nb = _blocks_per_segment(g, S)
        kf_ref[:Q_BLOCK] = kp_ref[...]
        kf_ref[Q_BLOCK:] = k_ref[...]
        vf_ref[:Q_BLOCK] = vp_ref[...]
        vf_ref[Q_BLOCK:] = v_ref[...]
        coefs = [_alibi_coef(g, h) for h in range(N_HEADS)]
        units = [(b, h) for b in range(nqb) for h in range(N_HEADS)]
        bias = {}

        def scores(u, _):
            b, h = units[u]
            if b not in bias:
                pen = jnp.where((i * nqb + b) % nb != 0, 0.0, NEG).astype(F32)
                bias.clear()
                bias[b] = _band_bias(pen)
            base, dist = bias[b]
            cols = slice(h * HEAD_DIM, (h + 1) * HEAD_DIM)
            q = q_ref[b * Q_BLOCK:(b + 1) * Q_BLOCK, cols]
            kk = kf_ref[b * Q_BLOCK:(b + 2) * Q_BLOCK, cols]
            return _dot(q, kk, "nt") * scale + (coefs[h] * dist + base)

        def softmax(u, s):
            m = jnp.max(s, axis=-1, keepdims=True)
            p = jnp.exp(s - m)
            den = jnp.sum(p, axis=-1, keepdims=True)
            return (p * (1.0 / den)).astype(BF16), m + jnp.log(den)

        def output(u, st):
            b, h = units[u]
            pn, lse = st
            cols = slice(h * HEAD_DIM, (h + 1) * HEAD_DIM)
            rows = slice(b * Q_BLOCK, (b + 1) * Q_BLOCK)
            o_ref[rows, cols] = _dot(pn, vf_ref[b * Q_BLOCK:(b + 2) * Q_BLOCK, cols], "nn").astype(o_ref.dtype)
            lse_ref[rows, h:h + 1] = lse

        _skewed(len(units), [scores, softmax, output])

    main = lambda c: pl.BlockSpec((None, tq, D_MODEL), lambda g, i: (g, i, c))
    prev = lambda c: pl.BlockSpec((None, Q_BLOCK, D_MODEL), lambda g, i: (g, jnp.maximum(i * nqb - 1, 0), c))
    return _pcall(
        body, name=name, grid=(N_GROUPS, S // tq),
        in_specs=[main(0), main(1), prev(1), main(2), prev(2)],
        out_specs=[pl.BlockSpec((None, tq, D_MODEL), lambda g, i: (g, i, 0)),
                   pl.BlockSpec((None, tq, N_HEADS), lambda g, i: (g, i, 0))],
        out_shape=[jax.ShapeDtypeStruct((N_GROUPS, S, D_MODEL), BF16),
                   jax.ShapeDtypeStruct((N_GROUPS, S, N_HEADS), F32)],
        scratch_shapes=[pltpu.VMEM((tq + Q_BLOCK, D_MODEL), BF16), pltpu.VMEM((tq + Q_BLOCK, D_MODEL), BF16)],
        compiler_params=_params(("parallel", "parallel")))(qkv, qkv, qkv, qkv, qkv)


def _attn_bwd(name, qkv, do, lse, dl, tq=1024):
    S = qkv.shape[1]
    nqb = tq // Q_BLOCK
    n_blocks = S // Q_BLOCK
    scale = HEAD_DIM ** -0.5
    KEYS = 2 * Q_BLOCK

    def body(q_ref, k_ref, v_ref, kp_ref, vp_ref, qn_ref, do_ref, don_ref, lse_ref, lsen_ref, dl_ref, dln_ref,
             out_ref, kf_ref, vf_ref, dk_ref, dv_ref):
        g = pl.program_id(0)
        i = pl.program_id(1)
        nb = _blocks_per_segment(g, S)
        kf_ref[:Q_BLOCK] = kp_ref[...]
        kf_ref[Q_BLOCK:] = k_ref[...]
        vf_ref[:Q_BLOCK] = vp_ref[...]
        vf_ref[Q_BLOCK:] = v_ref[...]
        coefs = [_alibi_coef(g, h) for h in range(N_HEADS)]
        units = [(h, b) for h in range(N_HEADS) for b in range(nqb + 1)]
        bias = {}

        def band(b):
            if b not in bias:
                blk = i * nqb + b
                ok = blk % nb != 0
                if b == nqb:
                    ok = jnp.logical_and(ok, blk < n_blocks)
                bias[b] = _band_bias(jnp.where(ok, 0.0, NEG).astype(F32))
            return bias[b]

        def operands(h, b):
            cols = slice(h * HEAD_DIM, (h + 1) * HEAD_DIM)
            if b == nqb:
                keys = slice(tq, tq + Q_BLOCK)
                return (qn_ref[:, cols], don_ref[:, cols], lsen_ref[:, h:h + 1], dln_ref[:, h:h + 1],
                        kf_ref[keys, cols], vf_ref[keys, cols])
            rows = slice(b * Q_BLOCK, (b + 1) * Q_BLOCK)
            keys = slice(b * Q_BLOCK, b * Q_BLOCK + KEYS)
            return (q_ref[rows, cols], do_ref[rows, cols], lse_ref[rows, h:h + 1], dl_ref[rows, h:h + 1],
                    kf_ref[keys, cols], vf_ref[keys, cols])

        def probs(u, _):
            h, b = units[u]
            q, do_b, lse_b, dl_b, kk, vv = operands(h, b)
            base, dist = band(b)
            if b == nqb:
                base, dist = base[:, :Q_BLOCK], dist[:, :Q_BLOCK]
            p = jnp.exp(_dot(q, kk, "nt") * scale + (coefs[h] * dist + base) - lse_b)
            return p, _dot(do_b, vv, "nt")

        def dscores(u, st):
            h, b = units[u]
            p, dp = st
            dl_b = operands(h, b)[3]
            return ((p * (dp - dl_b)) * scale).astype(BF16), p.astype(BF16)

        def grads(u, st):
            h, b = units[u]
            ds, pb = st
            q, do_b, _, _, kk, _ = operands(h, b)
            cols = slice(h * HEAD_DIM, (h + 1) * HEAD_DIM)
            if b < nqb:
                out_ref[b * Q_BLOCK:(b + 1) * Q_BLOCK, cols] = _dot(ds, kk, "nn").astype(out_ref.dtype)
            if b == 0:
                dk_ref[:Q_BLOCK] = _dot(ds[:, Q_BLOCK:], q, "tn")
                dv_ref[:Q_BLOCK] = _dot(pb[:, Q_BLOCK:], do_b, "tn")
                return None
            dk = _dot(ds, q, "tn")
            dv = _dot(pb, do_b, "tn")
            before = slice((b - 1) * Q_BLOCK, b * Q_BLOCK)
            dk_ref[before] += dk[:Q_BLOCK]
            dv_ref[before] += dv[:Q_BLOCK]
            if b < nqb:
                own = slice(b * Q_BLOCK, (b + 1) * Q_BLOCK)
                dk_ref[own] = dk[Q_BLOCK:]
                dv_ref[own] = dv[Q_BLOCK:]
            else:
                out_ref[:, D_MODEL + h * HEAD_DIM:D_MODEL + (h + 1) * HEAD_DIM] = dk_ref[...].astype(out_ref.dtype)
                out_ref[:, 2 * D_MODEL + h * HEAD_DIM:2 * D_MODEL + (h + 1) * HEAD_DIM] = (
                    dv_ref[...].astype(out_ref.dtype))
            return None

        _skewed(len(units), [probs, dscores, grads])

    nxt_blk = lambda i: jnp.minimum((i + 1) * nqb, n_blocks - 1)
    main = lambda c: pl.BlockSpec((None, tq, D_MODEL), lambda g, i: (g, i, c))
    prev = lambda c: pl.BlockSpec((None, Q_BLOCK, D_MODEL), lambda g, i: (g, jnp.maximum(i * nqb - 1, 0), c))
    row = pl.BlockSpec((None, tq, D_MODEL), lambda g, i: (g, i, 0))
    row_n = pl.BlockSpec((None, Q_BLOCK, D_MODEL), lambda g, i: (g, nxt_blk(i), 0))
    col = pl.BlockSpec((None, tq, N_HEADS), lambda g, i: (g, i, 0))
    col_n = pl.BlockSpec((None, Q_BLOCK, N_HEADS), lambda g, i: (g, nxt_blk(i), 0))
    return _pcall(
        body, name=name, grid=(N_GROUPS, S // tq),
        in_specs=[main(0), main(1), main(2), prev(1), prev(2), row_n, row, row_n, col, col_n, col, col_n],
        out_specs=pl.BlockSpec((None, tq, 3 * D_MODEL), lambda g, i: (g, i, 0)),
        out_shape=jax.ShapeDtypeStruct((N_GROUPS, S, 3 * D_MODEL), BF16),
        scratch_shapes=[pltpu.VMEM((tq + Q_BLOCK, D_MODEL), BF16), pltpu.VMEM((tq + Q_BLOCK, D_MODEL), BF16),
                        pltpu.VMEM((tq, HEAD_DIM), F32), pltpu.VMEM((tq, HEAD_DIM), F32)],
        compiler_params=_params(("parallel", "parallel")))(qkv, qkv, qkv, qkv, qkv, qkv, do, do, lse, lse, dl, dl)


def _group_weights(lse_ref):
    l0, l1, l2 = lse_ref[0], lse_ref[1], lse_ref[2]
    m = jnp.maximum(jnp.maximum(l0, l1), l2)
    e = [jnp.exp(l0 - m), jnp.exp(l1 - m), jnp.exp(l2 - m)]
    den = e[0] + e[1] + e[2]
    return [ei / den for ei in e]


MERGE_TOKENS = 512


def _folded_views(a, tb):
    _, S, C = a.shape
    views, specs = [], []
    for g, dil in enumerate(ATTN_DILATIONS):
        views.append(a.reshape(N_GROUPS * dil, S // dil, C))
        specs.append(pl.BlockSpec((dil, tb // dil, C), lambda i, g=g: (g, i, 0)))
    return views, specs


def _unfold_into(dst_ref, folded_refs):
    for g, (dil, ref) in enumerate(zip(ATTN_DILATIONS, folded_refs)):
        n = ref.shape[1]
        for r in range(dil):
            for c, cols in enumerate(_lane_chunks(ref.shape[2])):
                dst_ref[g, c, _strided_rows(r, n, dil), :] = ref[r, :, cols].astype(F32)


def _merge_fwd(name, o, lse, tb=MERGE_TOKENS):
    S = o.shape[1]

    def body(o0_ref, o1_ref, o2_ref, lse_ref, out_ref, o_ref):
        _unfold_into(o_ref, (o0_ref, o1_ref, o2_ref))
        w = _group_weights(lse_ref)
        for h in range(N_HEADS):
            cols = slice(h * HEAD_DIM, (h + 1) * HEAD_DIM)
            acc = w[0][:, h:h + 1] * o_ref[0, h]
            for gg in range(1, N_GROUPS):
                acc = acc + w[gg][:, h:h + 1] * o_ref[gg, h]
            out_ref[:, cols] = acc.astype(out_ref.dtype)

    views, specs = _folded_views(o, tb)
    return _pcall(body, name=name, grid=(S // tb,),
                  in_specs=specs + [pl.BlockSpec((N_GROUPS, tb, N_HEADS), lambda i: (0, i, 0))],
                  out_specs=pl.BlockSpec((tb, D_MODEL), lambda i: (i, 0)),
                  out_shape=jax.ShapeDtypeStruct((S, D_MODEL), BF16),
                  scratch_shapes=[pltpu.VMEM((N_GROUPS, N_HEADS, tb, HEAD_DIM), F32)],
                  compiler_params=_params(("parallel",)))(*views, lse)


def _merge_bwd(name, dx, w_out, o, lse, tb=MERGE_TOKENS):
    S = o.shape[1]
    n_chunks = sum(ATTN_DILATIONS)

    def body(dx_ref, w_ref, o0_ref, o1_ref, o2_ref, lse_ref, do_hbm, dl_ref, o_ref, dt_ref, df_ref, d_ref, sems):
        i = pl.program_id(0)
        d_ref[...] = _dot(dx_ref[...].astype(BF16), w_ref[...], "nt")
        _unfold_into(o_ref, (o0_ref, o1_ref, o2_ref))
        w = _group_weights(lse_ref)
        for h in range(N_HEADS):
            cols = slice(h * HEAD_DIM, (h + 1) * HEAD_DIM)
            dv = d_ref[:, cols]
            merged = w[0][:, h:h + 1] * o_ref[0, h]
            for gg in range(1, N_GROUPS):
                merged = merged + w[gg][:, h:h + 1] * o_ref[gg, h]
            dsum = jnp.sum(dv * merged, axis=-1, keepdims=True)
            for gg in range(N_GROUPS):
                wg = w[gg][:, h:h + 1]
                dt_ref[gg, h] = wg * dv
                dl_ref[gg, :, h:h + 1] = wg * dsum
        copies = []
        for gg, dil in enumerate(ATTN_DILATIONS):
            n = tb // dil
            for r in range(dil):
                for h, cols in enumerate(_lane_chunks(D_MODEL)):
                    df_ref[gg, r * n:(r + 1) * n, cols] = (
                        dt_ref[gg, h, _strided_rows(r, n, dil), :].astype(df_ref.dtype))
                cp = pltpu.make_async_copy(df_ref.at[gg, pl.ds(r * n, n)],
                                           do_hbm.at[gg, pl.ds(r * (S // dil) + i * n, n)], sems.at[len(copies)])
                cp.start()
                copies.append(cp)
        for cp in copies:
            cp.wait()

    views, specs = _folded_views(o, tb)
    small = pl.BlockSpec((N_GROUPS, tb, N_HEADS), lambda i: (0, i, 0))
    return _pcall(body, name=name, grid=(S // tb,),
                  in_specs=[pl.BlockSpec((tb, D_MODEL), lambda i: (i, 0)),
                            pl.BlockSpec((D_MODEL, D_MODEL), lambda i: (0, 0))] + specs + [small],
                  out_specs=[pl.BlockSpec(memory_space=pl.ANY), small],
                  out_shape=[jax.ShapeDtypeStruct((N_GROUPS, S, D_MODEL), BF16),
                             jax.ShapeDtypeStruct((N_GROUPS, S, N_HEADS), F32)],
                  scratch_shapes=[pltpu.VMEM((N_GROUPS, N_HEADS, tb, HEAD_DIM), F32),
                                  pltpu.VMEM((N_GROUPS, N_HEADS, tb, HEAD_DIM), F32),
                                  pltpu.VMEM((N_GROUPS, tb, D_MODEL), BF16), pltpu.VMEM((tb, D_MODEL), F32),
                                  pltpu.SemaphoreType.DMA((n_chunks,))],
                  compiler_params=_params(("arbitrary",)))(dx, w_out, *views, lse)


def _shift_rows(a, k):
    return pltpu.roll(a, k % a.shape[0], axis=0)


def _pool_level(g, levels):
    return jnp.where(g == 0, levels[0], jnp.where(g == 1, levels[1], jnp.where(g == 2, levels[2], levels[3])))


def _pool_fwd(name, h, w_in, w_group, scale, x_in, gain_next, tr=1024):
    S, D = h.shape
    H = POOL_HALO
    n_groups = D // POOL_DIM

    def body(h_ref, hh_ref, wi_ref, w_ref, sc_ref, x_ref, gn_ref, y_ref, z_ref, xo_ref, hn_ref, xs_ref):
        i = pl.program_id(0)
        g = pl.program_id(1)
        uv = _dot(h_ref[...], wi_ref[...], "nn")
        halo = jnp.where(i > 0, _dot(hh_ref[...], wi_ref[...], "nn"), 0.0)
        s = jnp.concatenate([halo, uv], axis=0)
        levels = []
        for k in (1, 2, 4, 8):
            s = s + _shift_rows(s, k)
            levels.append(s[H:])
        t = i * tr + lax.broadcasted_iota(jnp.int32, (tr, 1), 0)
        cnt = jnp.minimum(t + 1, jnp.left_shift(2, g)).astype(F32)
        y = _pool_level(g, levels) / cnt - uv
        yb = y.astype(BF16)
        z = _dot(yb, w_ref[...], "nn")
        y_ref[...] = yb
        z_ref[...] = z.astype(z_ref.dtype)
        x_out = x_ref[...] + z * sc_ref[...]
        xo_ref[...] = x_out
        xs_ref[g] = x_out

        @pl.when(g == n_groups - 1)
        def _():
            xf = jnp.concatenate([xs_ref[k] for k in range(n_groups)], axis=1)
            r = lax.rsqrt(jnp.mean(xf * xf, axis=-1, keepdims=True) + RMS_EPS)
            hn_ref[...] = ((xf * r) * gn_ref[...]).astype(hn_ref.dtype)

    blk = pl.BlockSpec((tr, POOL_DIM), lambda i, g: (i, g))
    row = pl.BlockSpec((tr, D), lambda i, g: (i, 0))
    return _pcall(
        body, name=name, grid=(S // tr, n_groups),
        in_specs=[row, pl.BlockSpec((H, D), lambda i, g: (jnp.maximum(i * (tr // H) - 1, 0), 0)),
                  pl.BlockSpec((D, POOL_DIM), lambda i, g: (0, g)),
                  pl.BlockSpec((None, POOL_DIM, POOL_DIM), lambda i, g: (g, 0, 0)),
                  pl.BlockSpec((1, POOL_DIM), lambda i, g: (0, g)), blk, pl.BlockSpec((1, D), lambda i, g: (0, 0))],
        out_specs=[blk, blk, blk, row],
        out_shape=[jax.ShapeDtypeStruct((S, D), BF16), jax.ShapeDtypeStruct((S, D), BF16),
                   jax.ShapeDtypeStruct((S, D), F32), jax.ShapeDtypeStruct((S, D), BF16)],
        scratch_shapes=[pltpu.VMEM((n_groups, tr, POOL_DIM), F32)],
        compiler_params=_params(("parallel", "arbitrary")))(h, h, w_in, w_group, scale, x_in, gain_next)


def _pool_bwd(name, d_out, z, y, scale, w_group, tr=1024):
    S, D = d_out.shape
    H = POOL_HALO

    def body(d_ref, dn_ref, z_ref, y_ref, sc_ref, w_ref, du_ref, dw_ref, dsc_ref):
        g = pl.program_id(0)
        i = pl.program_id(1)
        dv = d_ref[...]
        dz = jnp.concatenate([dv, dn_ref[...]], axis=0) * sc_ref[...]
        dzb = dz.astype(BF16)
        dy = _dot(dzb, w_ref[...], "nt")
        t = i * tr + lax.broadcasted_iota(jnp.int32, (tr + H, 1), 0)
        cnt = jnp.minimum(t + 1, jnp.left_shift(2, g)).astype(F32)
        s = jnp.where(t < S, dy / cnt, 0.0)
        levels = []
        for k in (1, 2, 4, 8):
            s = s + _shift_rows(s, -k)
            levels.append(s[:tr])
        du_ref[...] = (_pool_level(g, levels) - dy[:tr]).astype(du_ref.dtype)
        dw = _dot(y_ref[...], dzb[:tr], "tn")
        dsc = jnp.sum(dv * z_ref[...].astype(F32), axis=0, keepdims=True)

        @pl.when(i == 0)
        def _():
            dw_ref[...] = dw
            dsc_ref[...] = dsc

        @pl.when(i > 0)
        def _():
            dw_ref[...] += dw
            dsc_ref[...] += dsc

    blk = pl.BlockSpec((tr, POOL_DIM), lambda g, i: (i, g))
    vec = pl.BlockSpec((1, POOL_DIM), lambda g, i: (0, g))
    mat = pl.BlockSpec((None, POOL_DIM, POOL_DIM), lambda g, i: (g, 0, 0))
    nxt = pl.BlockSpec((H, POOL_DIM), lambda g, i: (jnp.minimum((i + 1) * (tr // H), S // H - 1), g))
    return _pcall(
        body, name=name, grid=(D // POOL_DIM, S // tr), in_specs=[blk, nxt, blk, blk, vec, mat],
        out_specs=[blk, mat, vec],
        out_shape=[jax.ShapeDtypeStruct((S, D), BF16), jax.ShapeDtypeStruct((D // POOL_DIM, POOL_DIM, POOL_DIM), F32),
                   jax.ShapeDtypeStruct((1, D), F32)],
        compiler_params=_params(("parallel", "arbitrary")))(d_out, d_out, z, y, scale, w_group)


def _row_tile(rows, cols, target_bytes=1 << 20):
    best = rows
    for tr in range(8, rows + 1, 8):
        if rows % tr == 0 and tr * cols * 4 <= target_bytes:
            best = tr
    return best if best * cols * 4 <= 4 * target_bytes else rows


def _adamw_step(w, g, m, v):
    m2 = ADAM_B1 * m + (1.0 - ADAM_B1) * g
    v2 = ADAM_B2 * v + (1.0 - ADAM_B2) * (g * g)
    m_hat = m2 / (1.0 - ADAM_B1 ** ADAM_STEP)
    v_hat = v2 / (1.0 - ADAM_B2 ** ADAM_STEP)
    return -ADAM_LR * (m_hat / (jnp.sqrt(v_hat) + ADAM_EPS) + ADAM_WD * w), m2, v2


def _adamw_refs(w_ref, g_ref, m_ref, v_ref, go_ref, d_ref, mo_ref, vo_ref):
    gv = g_ref[...]
    d_ref[...], mo_ref[...], vo_ref[...] = _adamw_step(w_ref[...], gv, m_ref[...], v_ref[...])
    go_ref[...] = gv


def _adamw(name, items, steps=16):
    n = len(items)

    def body(*refs):
        for t in range(n):
            _adamw_refs(*refs[4 * t:4 * t + 4], *refs[4 * n + 4 * t:4 * n + 4 * t + 4])

    specs, shapes = [], []
    for w, _, _, _ in items:
        R, C = w.shape
        assert R % (8 * steps) == 0, (name, w.shape)
        specs += [pl.BlockSpec((R // steps, C), lambda i: (i, 0))] * 4
        shapes += [jax.ShapeDtypeStruct((R, C), F32)] * 4
    res = _pcall(body, name=name, grid=(steps,), in_specs=specs, out_specs=specs, out_shape=shapes,
                 compiler_params=_params(("parallel",)))(*[a for item in items for a in item])
    return [res[4 * t:4 * t + 4] for t in range(n)]


def _adamw_small(name, items):
    n = len(items)

    def body(*refs):
        for t in range(n):
            _adamw_refs(*refs[4 * t:4 * t + 4], *refs[4 * n + 4 * t:4 * n + 4 * t + 4])

    specs, shapes = [], []
    for w, _, _, _ in items:
        specs += [pl.BlockSpec(w.shape, lambda i: (0, 0))] * 4
        shapes += [jax.ShapeDtypeStruct(w.shape, F32)] * 4
    res = _pcall(body, name=name, grid=(1,), in_specs=specs, out_specs=specs, out_shape=shapes,
                 compiler_params=_params(("arbitrary",)))(*[a for item in items for a in item])
    return [res[4 * t:4 * t + 4] for t in range(n)]


def _sum_leading(name, a, out_dtype):
    n, R, C = a.shape
    tr = _row_tile(R, C) if R % 16 == 0 else R
    if tr % 16:
        tr = R

    def body(a_ref, o_ref):
        acc = a_ref[0].astype(F32)
        for j in range(1, n):
            acc = acc + a_ref[j].astype(F32)
        o_ref[...] = acc.astype(o_ref.dtype)

    return _pcall(body, name=name, grid=(R // tr,), in_specs=[pl.BlockSpec((n, tr, C), lambda i: (0, i, 0))],
                  out_specs=pl.BlockSpec((tr, C), lambda i: (i, 0)), out_shape=jax.ShapeDtypeStruct((R, C), out_dtype),
                  compiler_params=_params(("parallel",)))(a)


def _cast_many(name, items, chip, steps=4):
    tiles = []
    for w, _, dtype in items:
        R = w.shape[1]
        nt = steps if R % (steps * 16) == 0 else 1
        tiles.append((nt, R // nt))

    def body(c_ref, *refs):
        i = pl.program_id(0)
        for t, (nt, _) in enumerate(tiles):
            w_ref, o_ref = refs[t], refs[len(items) + t]

            def cast(w_ref=w_ref, o_ref=o_ref):
                o_ref[...] = w_ref[...].astype(o_ref.dtype)

            if nt == steps:
                cast()
            else:
                pl.when(i < nt)(cast)

    in_specs, out_specs, out_shape = [], [], []
    for (w, layer, dtype), (nt, tr) in zip(items, tiles):
        C = w.shape[2]
        in_specs.append(pl.BlockSpec((None, tr, C), lambda i, c_ref, l=layer, nt=nt: (l, jnp.minimum(i, nt - 1), 0)))
        out_specs.append(pl.BlockSpec((None, tr, C), lambda i, c_ref, nt=nt: (c_ref[0], jnp.minimum(i, nt - 1), 0)))
        out_shape.append(jax.ShapeDtypeStruct((N_CHIPS, w.shape[1], C), dtype))
    return _pcall(body, prefetch=1, name=name, grid=(steps,), in_specs=in_specs, out_specs=out_specs,
                  out_shape=out_shape, compiler_params=_params(("arbitrary",)))(chip, *[w for w, _, _ in items])


def _cast_qkv(name, w, chip, tr=256):
    _, R, C = w.shape

    def body(c_ref, w_ref, own_ref, *piece_refs):
        v = w_ref[...].astype(BF16)
        own_ref[...] = v
        for p, ref in enumerate(piece_refs):
            ref[...] = v[:, p * QKV_TILE:(p + 1) * QKV_TILE]

    piece = pl.BlockSpec((None, tr, QKV_TILE), lambda i, c_ref: (c_ref[0], i, 0))
    return _pcall(body, prefetch=1, name=name, grid=(R // tr,),
                  in_specs=[pl.BlockSpec((None, tr, C), lambda i, c_ref: (0, i, 0))],
                  out_specs=[pl.BlockSpec((tr, C), lambda i, c_ref: (i, 0))] + [piece] * QKV_PIECES,
                  out_shape=[jax.ShapeDtypeStruct((R, C), BF16)]
                  + [jax.ShapeDtypeStruct((N_CHIPS, R, QKV_TILE), BF16)] * QKV_PIECES,
                  compiler_params=_params(("parallel",)))(chip, w)


def _owner_sum(name, mine, landed, shape, chip, core, out, layer, col):
    _, half, C = mine.shape
    k, _ = col
    tr = _row_tile(half, C)
    if tr % 16:
        tr = half
    nrt = half // tr

    def body(ch_ref, co_ref, mine_ref, landed_ref, *rest):
        g = mine_ref[...].astype(F32)
        for j in range(3):
            g = g + landed_ref[j].astype(F32)
        rest[-1][...] = g

    piece = pl.BlockSpec((None, tr, C), lambda i, ch, co: (layer, co[0] * nrt + i, k))
    in_specs = [pl.BlockSpec((None, tr, C), lambda i, ch, co: (ch[0], i, 0)),
                pl.BlockSpec((3, tr, C), lambda i, ch, co: (0, i, 0))]
    operands = [chip, core, mine, landed]
    aliases = {}
    if out is not None:
        in_specs.append(ANY)
        operands.append(out)
        aliases = {4: 0}
    return _pcall(body, prefetch=2, name=name, grid=(nrt,), in_specs=in_specs, out_specs=piece,
                  out_shape=jax.ShapeDtypeStruct(shape, F32), input_output_aliases=aliases,
                  compiler_params=_params(("parallel",)))(*operands)


def _add_my_half(name, ps, qs, core):
    n = len(ps)

    def body(c_ref, *refs):
        for t in range(n):
            p_ref, q_ref, o_ref = refs[t], refs[n + t], refs[2 * n + t]
            o_ref[...] = (p_ref[...].astype(F32) + q_ref[...].astype(F32)).astype(o_ref.dtype)

    halves = [(p.shape[1] // 2, p.shape[2]) for p in ps]
    mine = [pl.BlockSpec((None, h, c), lambda s, c_ref: (s, c_ref[0], 0)) for h, c in halves]
    whole = [pl.BlockSpec((None, h, c), lambda s, c_ref: (s, 0, 0)) for h, c in halves]
    return _pcall(body, prefetch=1, name=name, grid=(N_CHIPS,), in_specs=mine + whole, out_specs=whole,
                  out_shape=[jax.ShapeDtypeStruct((N_CHIPS, h, c), BF16) for h, c in halves],
                  compiler_params=_params(("parallel",)))(core, *ps, *qs)


ANY = pl.BlockSpec(memory_space=pl.ANY)


def _place():
    x, y, c = lax.axis_index("x"), lax.axis_index("y"), lax.axis_index("c")
    other_chips = [(1 - x, y), (x, 1 - y), (1 - x, 1 - y)]
    return x, y, c, other_chips


def _remote(src, dst, send, recv, k, to):
    return pltpu.make_async_remote_copy(src_ref=src, dst_ref=dst, send_sem=send.at[k], recv_sem=recv.at[k],
                                        device_id=to, device_id_type=MESH)


def _in_place(bufs):
    return dict(operands=bufs, out_shapes=[jax.ShapeDtypeStruct(b.shape, b.dtype) for b in bufs],
                aliases={t: t for t in range(len(bufs))})


def _gather_rider(bufs, jobs):
    def copies(ins, outs, send, recv, base=0):
        x, y, c, chips = _place()
        out = []
        for t, over_ici, rows in jobs:
            ref = outs[t]
            if rows is not None:
                a, b, n = rows
                half = ref.shape[1] // 2
                rows = pl.ds(c * half + a * half // n, (b - a) * half // n)
            for px, py in chips:
                slot = 2 * x + y if over_ici else 2 * px + py
                piece = ref.at[slot] if rows is None else ref.at[slot, rows]
                out.append(_remote(piece, piece, send, recv, base + len(out), (px, py, c) if over_ici else (x, y, 1 - c)))
        return out

    return _Rider(n_copies=3 * len(jobs), copies=copies, **_in_place(bufs))


def _gather_and_pass_rider(buf):
    half = buf.shape[1] // 2

    def pieces(outs):
        x, y, c, chips = _place()
        rows = lambda core: pl.ds(core * half, half)
        mine = outs[0].at[2 * x + y, rows(c)]
        landed = [outs[0].at[2 * px + py, rows(c)] for px, py in chips]
        theirs = [outs[0].at[2 * px + py, rows(1 - c)] for px, py in chips]
        return (x, y, c, chips), mine, landed, theirs

    def start(ins, outs, send, recv, base=0):
        (x, y, c, chips), mine, _, _ = pieces(outs)
        for j, (px, py) in enumerate(chips):
            _remote(mine, mine, send, recv, base + j, (px, py, c)).start()

    def finish(ins, outs, send, recv, base=0):
        (x, y, c, chips), mine, landed, theirs = pieces(outs)
        sibling = (x, y, 1 - c)
        passed = []
        for j, (px, py) in enumerate(chips):
            _remote(landed[j], landed[j], send, recv, base + j, (px, py, c)).wait_recv()
            passed.append(_remote(landed[j], landed[j], send, recv, base + 3 + j, sibling))
            passed[-1].start()
        for j in range(3):
            _remote(theirs[j], theirs[j], send, recv, base + 3 + j, sibling).wait_recv()
        for j, (px, py) in enumerate(chips):
            _remote(mine, mine, send, recv, base + j, (px, py, c)).wait_send()
            passed[j].wait_send()

    return _Rider(n_copies=6, start=start, finish=finish, **_in_place([buf]))


def _swap_halves_rider(parts):
    n = len(parts)

    def copies(ins, outs, send, recv, base=0):
        x, y, c, _ = _place()
        out = []
        for t in range(n):
            half = ins[t].shape[1] // 2
            out.append(_remote(ins[t].at[:, pl.ds((1 - c) * half, half)], outs[t], send, recv, base + t,
                               (x, y, 1 - c)))
        return out

    shapes = [jax.ShapeDtypeStruct((p.shape[0], p.shape[1] // 2, p.shape[2]), p.dtype) for p in parts]
    return _Rider(parts, shapes, {}, n, copies)


def _scatter_rider(sums, landed, relations):
    n = len(sums)
    kept = [t for t in range(n) if landed[t] is not None]

    def copies(ins, outs, send, recv, base=0):
        x, y, c, chips = _place()
        out = []
        for t in range(n):
            for j in relations[t]:
                px, py = chips[j]
                out.append(_remote(ins[t].at[2 * px + py], outs[t].at[j], send, recv, base + len(out), (px, py, c)))
        return out

    shapes = [jax.ShapeDtypeStruct((3,) + s.shape[1:], s.dtype) for s in sums]
    return _Rider(list(sums) + [landed[t] for t in kept], shapes, {n + k: t for k, t in enumerate(kept)},
                  sum(len(r) for r in relations), copies)


def _share_rider(blocks, pieces):
    def copies(ins, outs, send, recv, base=0):
        x, y, c, _ = _place()
        out = []
        for k, (t, l, col) in enumerate(pieces):
            ref = outs[t].at[l]
            r2 = ref.shape[0] // 2
            width = ref.shape[1] // col[1]
            piece = ref.at[pl.ds(c * r2, r2), pl.ds(col[0] * width, width)]
            out.append(_remote(piece, piece, send, recv, base + k, (x, y, 1 - c)))
        return out

    return _Rider(n_copies=len(pieces), copies=copies, **_in_place(blocks))


def _gather_small(name, v):
    def body(v_ref, out_ref, send_sem, recv_sem, local_sem):
        x, y, c, _ = _place()
        me = 4 * x + 2 * y + c
        flips = [(fx, fy, fc) for fx in (0, 1) for fy in (0, 1) for fc in (0, 1)][1:]
        local = pltpu.make_async_copy(v_ref, out_ref.at[me], local_sem)
        local.start()
        copies = []
        for j, (fx, fy, fc) in enumerate(flips):
            peer = (x ^ fx, y ^ fy, c ^ fc)
            copies.append(pltpu.make_async_remote_copy(
                src_ref=v_ref, dst_ref=out_ref.at[me], send_sem=send_sem.at[j], recv_sem=recv_sem.at[j],
                device_id=peer, device_id_type=MESH))
        for cp in copies:
            cp.start()
        for j, (fx, fy, fc) in enumerate(flips):
            peer = (x ^ fx, y ^ fy, c ^ fc)
            pltpu.make_async_remote_copy(
                src_ref=v_ref, dst_ref=out_ref.at[4 * peer[0] + 2 * peer[1] + peer[2]], send_sem=send_sem.at[j],
                recv_sem=recv_sem.at[j], device_id=peer, device_id_type=MESH).wait_recv()
        for cp in copies:
            cp.wait_send()
        local.wait()

    return _pcall(body, name=name, in_specs=[ANY], out_specs=ANY,
                  out_shape=jax.ShapeDtypeStruct((8,) + v.shape, v.dtype),
                  scratch_shapes=[pltpu.SemaphoreType.DMA((7,)), pltpu.SemaphoreType.DMA((7,)),
                                  pltpu.SemaphoreType.DMA])(v)


def _fold(a, dil):
    if dil == 1:
        return a
    S, C = a.shape
    return a.reshape(S // dil, dil, C).transpose(1, 0, 2).reshape(S, C)


def _unfold(a, dil):
    if dil == 1:
        return a
    S, C = a.shape
    return a.reshape(dil, S // dil, C).transpose(1, 0, 2).reshape(S, C)


def _fold_groups(a):
    return jnp.stack([_fold(a[g] if a.ndim == 3 else a, d) for g, d in enumerate(ATTN_DILATIONS)])


def _unfold_groups(a):
    return jnp.stack([_unfold(a[g], d) for g, d in enumerate(ATTN_DILATIONS)])


class _NoComm:
    def __init__(self, weights):
        self.weights = weights
        self.grads = {}
        self.chip = jnp.zeros((1,), jnp.int32)

    def w(self, name):
        return self.weights[name]

    def run(self, fn, name, *args, **kw):
        return fn(name, *args, **kw)

    def grad(self, name, value):
        self.grads[name] = value


def _local_step(xs, tgt, attn_norm, ffn_norm, final_norm, comm):
    run = comm.run
    hf = run(_norm_fold, "fold", xs, attn_norm)
    qkv = run(_qkv_tiles, "qkv_own", None, hf, comm.w("wq_own"), None, comm.chip)
    for p in range(QKV_PIECES):
        qkv = run(_qkv_tiles, "qkv_piece%d" % p, p, hf, comm.w("wq%d" % p), qkv, comm.chip)
    o_f, lse_f = run(_attn_fwd, "attn_fwd", qkv)
    lse_t = _unfold_groups(lse_f)
    merged = run(_merge_fwd, "merge_fwd", o_f, lse_t)
    x1, h1 = run(_proj_res_norm, "attn_out", merged, comm.w("o"), xs, ffn_norm[0:1])
    gu0, act0 = run(_ffn_up, "ffn0_up", h1, comm.w("gu0"))
    x2, h2 = run(_proj_res_norm, "ffn0_down", act0, comm.w("d0"), x1, comm.w("pool_norm"))
    y, z, x3, h3 = run(_pool_fwd, "pool_fwd", h2, comm.w("p"), comm.w("pg"), comm.w("pool_scale"), x2, ffn_norm[1:2])
    gu1, act1 = run(_ffn_up, "ffn1_up", h3, comm.w("gu1"))
    loss, dx4, dx4_b, d_final = run(_proj_res_loss, "ffn1_down", act1, comm.w("d1"), x3, final_norm, tgt)

    dgu1 = run(_ffn_down_bwd, "ffn1_down_bwd", dx4_b, comm.w("d1"), gu1)
    comm.grad("d1", run(_mm_tn, "ffn1_down_dw", act1, dx4_b, FF_SHARD, 2048))
    comm.grad("gu1", run(_ffn_dw_up, "ffn1_up_dw", h3, dgu1))
    dx3, d_ffn1 = run(_ffn_dh, "ffn1_up_dh", dgu1, comm.w("gu1"), x3, ffn_norm[1:2], dx4)

    du, dw_pg, d_scale = run(_pool_bwd, "pool_bwd", dx3, z, y, comm.w("pool_scale"), comm.w("pg"))
    comm.grad("pg", dw_pg)
    comm.grad("p", run(_mm_tn, "pool_in_dw", h2, du, D_MODEL, h2.shape[0]))
    dx2, dx2_b, d_pool = run(_dh_norm_bwd, "pool_in_dh", du, comm.w("p"), x2, comm.w("pool_norm"), dx3)

    dgu0 = run(_ffn_down_bwd, "ffn0_down_bwd", dx2_b, comm.w("d0"), gu0)
    comm.grad("d0", run(_mm_tn, "ffn0_down_dw", act0, dx2_b, FF_SHARD, 2048))
    comm.grad("gu0", run(_ffn_dw_up, "ffn0_up_dw", h1, dgu0))
    dx1, d_ffn0 = run(_ffn_dh, "ffn0_up_dh", dgu0, comm.w("gu0"), x1, ffn_norm[0:1], dx2)

    comm.grad("o", run(_mm_tn, "attn_out_dw", merged, dx1, D_MODEL, 2048))
    do_f, dl_t = run(_merge_bwd, "merge_bwd", dx1, comm.w("o"), o_f, lse_t)
    dqkv = run(_attn_bwd, "attn_bwd", qkv, do_f, lse_f, _fold_groups(dl_t))
    for p in range(QKV_PIECES):
        comm.grad("qkv%d" % p, run(_qkv_dw, "qkv_dw%d" % p, p, hf, dqkv))
    dhf = None
    for g in range(N_GROUPS):
        dhf = run(_qkv_dh, "qkv_dh%d" % g, g, dqkv, [comm.w("wq%d" % p) for p in range(QKV_PIECES)], dhf)
    grad_x, d_attn = run(_rms_bwd_folded, "norm_attn_bwd", xs, attn_norm, dhf, dx1)

    small = dict(attn=d_attn, ffn0=d_ffn0, ffn1=d_ffn1, final=d_final, pool=d_pool, scale=d_scale)
    return loss, grad_x, small


TENSORS = ("qkv", "o", "p", "pg", "gu0", "gu1", "d0", "d1")


def _block_of(name):
    if name[:-1] in ("gu", "d"):
        return name[:-1], int(name[-1]), (0, 1)
    if name[:-1] == "qkv":
        return "qkv", 0, (int(name[-1]), QKV_PIECES)
    return name, 0, (0, 1)


class _MeshComm:
    def __init__(self, bufs, shapes, chip_arr, core_arr, pg_rows):
        self.bufs = dict(bufs)
        self.shapes = shapes
        self.chip, self.chip_arr, self.core_arr, self.pg_rows = chip_arr, chip_arr, core_arr, pg_rows
        self.parts, self.sums, self.blocks, self.landed, self.arrived = {}, {}, {}, {}, {}
        move = lambda ici=(), d2d=(): lambda: self.gather(ici, d2d)
        whole = lambda name: lambda: self.gather_and_pass(name)
        swap = lambda *names: lambda: self.swap(names)
        scatter = lambda *names: lambda: self.scatter(names)
        share = lambda *names: lambda: self.share(names)
        self.plan = {
            "fold": [whole("wq0")],
            "qkv_own": [whole("wq1")],
            "qkv_piece0": [whole("wq2")],
            "qkv_piece1": [move(ici=["o", "gu0[0:1/4]"])],
            "qkv_piece2": [move(ici=["gu0[1:2/4]"], d2d=["o", "gu0[0:1/4]"])],
            "attn_fwd": [move(ici=["gu0[2:4/4]", "d0[0:1/2]"], d2d=["gu0[1:2/4]"])],
            "merge_fwd": [move(ici=["p", "pg"], d2d=["gu0[2:4/4]", "d0[0:1/2]"])],
            "attn_out": [move(ici=["d0[1:2/2]", "pool_norm", "pool_scale"], d2d=["p", "pg"])],
            "ffn0_up": [move(ici=["gu1[0:3/4]"], d2d=["d0[1:2/2]"])],
            "ffn0_down": [move(ici=["gu1[3:4/4]"], d2d=["gu1[0:3/4]"])],
            "pool_fwd": [move(ici=["d1"], d2d=["gu1[3:4/4]"])],
            "ffn1_up": [move(d2d=["d1"])],
            "ffn1_up_dh": [swap("d1", "gu1")],
            "pool_bwd": [scatter("d1{01}")],
            "pool_in_dh": [scatter("d1{2}")],
            "ffn0_down_bwd": [scatter("gu1{01}")],
            "ffn0_down_dw": [scatter("gu1{2}")],
            "ffn0_up_dw": [share("gu1", "d1")],
            "ffn0_up_dh": [swap("pg", "p", "d0", "gu0")],
            "merge_bwd": [swap("o")],
            "attn_bwd": [scatter("gu0", "d0", "o")],
            "qkv_dw0": [share("d0"), scatter("p", "pg")],
            "qkv_dw1": [share("gu0", "o"), swap("qkv0")],
            "qkv_dw2": [share("p", "pg"), scatter("qkv0"), swap("qkv1")],
            "qkv_dh0": [share("qkv0"), scatter("qkv1"), swap("qkv2")],
            "qkv_dh1": [share("qkv1"), scatter("qkv2")],
            "qkv_dh2": [share("qkv2")],
        }

    def gather_and_pass(self, name):
        return _gather_and_pass_rider(self.bufs[name]), lambda res: self.bufs.update({name: res[0]})

    def gather(self, ici, d2d):
        names, jobs = [], []
        for over_ici, specs in ((True, ici), (False, d2d)):
            for spec in specs:
                name, _, rows = spec.partition("[")
                if name not in names:
                    names.append(name)
                rng = None
                if name in TENSORS:
                    ab, _, n = (rows[:-1] or "0:1/1").partition("/")
                    rng = (int(ab.split(":")[0]), int(ab.split(":")[1]), int(n))
                jobs.append((names.index(name), over_ici, rng))
        return _gather_rider([self.bufs[n] for n in names], jobs), lambda res: self.bufs.update(zip(names, res))

    def swap(self, names):
        def done(res):
            sums = _add_my_half("chip_sum_" + "_".join(names), [self.parts[n] for n in names], res, self.core_arr)
            self.sums.update(zip(names, sums))
        return _swap_halves_rider([self.parts[n] for n in names]), done

    def scatter(self, specs):
        names = [s.partition("{")[0] for s in specs]
        relations = [tuple(int(ch) for ch in s.partition("{")[2][:-1]) or (0, 1, 2) for s in specs]

        def done(res):
            for n, rel, landed in zip(names, relations, res):
                self.landed[n] = landed
                self.arrived[n] = self.arrived.get(n, ()) + rel
                if len(self.arrived[n]) < 3:
                    continue
                key, layer, col = _block_of(n)
                self.blocks[key] = _owner_sum("owner_sum_" + n, self.sums[n], landed, self.shapes[key],
                                              self.chip_arr, self.core_arr, self.blocks.get(key), layer, col)
        rider = _scatter_rider([self.sums[n] for n in names], [self.landed.get(n) for n in names], relations)
        return rider, done

    def share(self, names):
        keys, pieces = [], []
        for n in names:
            key, layer, col = _block_of(n)
            if key not in keys:
                keys.append(key)
            pieces.append((keys.index(key), layer, col))
        return _share_rider([self.blocks[k] for k in keys], pieces), lambda res: self.blocks.update(zip(keys, res))

    def run(self, fn, name, *args, **kw):
        made = [make() for make in self.plan.get(name, [])]
        if not made:
            return fn(name, *args, **kw)
        riders = [r for r, _ in made]
        out = _ride(riders[0] if len(riders) == 1 else _riders(*riders), fn, name, *args, **kw)
        for r, done in made:
            done(r.results)
        return out

    def w(self, name):
        b = self.bufs[name]
        if name in ("o", "p", "d0", "d1"):
            return b.reshape(-1, b.shape[-1])
        if name == "pg":
            b = b.reshape(N_CHIPS, 4, self.pg_rows, POOL_DIM).transpose(1, 0, 2, 3)
            return b.reshape(4, POOL_DIM, POOL_DIM)
        if name in ("pool_norm", "pool_scale"):
            return b.reshape(1, -1)
        return b

    def grad(self, name, value):
        if name == "pg":
            value = value.reshape(4, N_CHIPS, self.pg_rows, POOL_DIM).transpose(1, 0, 2, 3)
            value = value.reshape(N_CHIPS, 4 * self.pg_rows, POOL_DIM).astype(BF16)
        elif name in ("o", "p", "d0", "d1"):
            value = value.reshape(N_CHIPS, value.shape[0] // N_CHIPS, value.shape[1])
        self.parts[name] = value


def kernel(x, attn_norm, w_qkv, w_attn_out, pool_norm, w_pool_in, w_pool_group, pool_scale, ffn_norm, w_ffn_gate_up, w_ffn_down, final_norm, loss_target, m_attn_norm, m_w_qkv, m_w_attn_out, m_pool_norm, m_w_pool_in, m_w_pool_group, m_pool_scale, m_ffn_norm, m_w_ffn_gate_up, m_w_ffn_down, m_final_norm, v_attn_norm, v_w_qkv, v_w_attn_out, v_pool_norm, v_w_pool_in, v_w_pool_group, v_pool_scale, v_ffn_norm, v_w_ffn_gate_up, v_w_ffn_down, v_final_norm):
    D = D_MODEL
    chip = 2 * lax.axis_index("x") + lax.axis_index("y")
    core = lax.axis_index("c")
    pg_rows = w_pool_group.shape[2]

    chip_arr = chip.astype(jnp.int32).reshape(1)
    core_arr = core.astype(jnp.int32).reshape(1)

    pieces = _cast_qkv("cast_qkv", w_qkv, chip_arr)
    bufs = dict(zip(["wq_own"] + ["wq%d" % p for p in range(QKV_PIECES)], pieces))
    shards = [(w_attn_out, 0, BF16), (w_pool_in, 0, BF16), (w_pool_group.reshape(1, 4 * pg_rows, POOL_DIM), 0, BF16),
              (w_ffn_gate_up, 0, BF16), (w_ffn_gate_up, 1, BF16), (w_ffn_down, 0, BF16), (w_ffn_down, 1, BF16),
              (pool_norm[None], 0, F32), (pool_scale[None], 0, F32)]
    bufs.update(zip(TENSORS[1:] + ("pool_norm", "pool_scale"), _cast_many("cast_rest", shards, chip_arr)))

    as_block = lambda a: a.reshape((-1,) + a.shape[-2:]) if a.ndim == 3 else a.reshape(1, -1, a.shape[-1])
    owned = dict(qkv=(w_qkv, m_w_qkv, v_w_qkv), o=(w_attn_out, m_w_attn_out, v_w_attn_out),
                 p=(w_pool_in, m_w_pool_in, v_w_pool_in), pg=(w_pool_group, m_w_pool_group, v_w_pool_group),
                 gu=(w_ffn_gate_up, m_w_ffn_gate_up, v_w_ffn_gate_up), d=(w_ffn_down, m_w_ffn_down, v_w_ffn_down))
    comm = _MeshComm(bufs, {k: as_block(wmv[0]).shape for k, wmv in owned.items()}, chip_arr, core_arr, pg_rows)
    loss_part, grad_x, small = _local_step(x[0], loss_target[0], attn_norm, ffn_norm, final_norm.reshape(1, D), comm)

    rows = jnp.concatenate([small["attn"], small["ffn0"], small["ffn1"], small["final"], small["pool"],
                            small["scale"], jnp.pad(loss_part, ((0, 0), (0, D - 1))), jnp.zeros((1, D), F32)], axis=0)
    all_rows = _gather_small("gather_gain_grads", rows)
    tot = _sum_leading("sum_gain_grads", all_rows, F32)
    loss = tot[6, 0]
    g_attn_norm = tot[0:1]
    g_ffn_norm = tot[1:3]
    g_final_norm = tot[3]
    g_pool_norm = lax.dynamic_slice(tot, (4, chip * POOL_DIM), (1, POOL_DIM))
    g_pool_scale = lax.dynamic_slice(tot, (5, chip * POOL_DIM), (1, POOL_DIM))

    as_rows = lambda a: a.reshape(-1, a.shape[-1])
    res = _adamw("adamw_weights", [(as_rows(w), as_rows(comm.blocks[k]), as_rows(m), as_rows(v))
                                   for k, (w, m, v) in owned.items()])
    updated = {k: [a.reshape(owned[k][0].shape) for a in r] for k, r in zip(owned, res)}
    gains = [(attn_norm, g_attn_norm, m_attn_norm, v_attn_norm), (pool_norm, g_pool_norm, m_pool_norm, v_pool_norm),
             (pool_scale, g_pool_scale, m_pool_scale, v_pool_scale), (ffn_norm, g_ffn_norm, m_ffn_norm, v_ffn_norm),
             (final_norm, g_final_norm, m_final_norm, v_final_norm)]
    small_res = _adamw_small("adamw_gains", [tuple(as_rows(a) for a in item) for item in gains])
    small_res = [[a.reshape(item[0].shape) for a in r] for r, item in zip(small_res, gains)]
    results = [small_res[0], updated["qkv"], updated["o"], small_res[1], updated["p"], updated["pg"], small_res[2],
               small_res[3], updated["gu"], updated["d"], small_res[4]]
    grads = [r[0] for r in results]
    deltas = [r[1] for r in results]
    new_m = [r[2] for r in results]
    new_v = [r[3] for r in results]
    return (loss, grad_x[None], *grads, *deltas, *new_m, *new_v)
```

```python
import jax
import jax.numpy as jnp
from jax import lax
from jax.experimental import pallas as pl
from jax.experimental.pallas import tpu as pltpu

F32 = jnp.float32
BF16 = jnp.bfloat16
MESH = pl.DeviceIdType.MESH

D_MODEL = 1024
N_HEADS = 8
HEAD_DIM = 128
Q_BLOCK = 128
ATTN_DILATIONS = (1, 4, 16)
N_GROUPS = 3
D_FF = 2816
N_CHIPS = 4
FF_SHARD = 2 * D_FF // N_CHIPS
QKV_SHARD = 3 * 3 * D_MODEL // N_CHIPS
QKV_TILE = 768
POOL_DIM = 256
POOL_HALO = 16
RMS_EPS = 1e-6
NEG = -1e30
ADAM_LR = 0.001
ADAM_B1 = 0.9
ADAM_B2 = 0.999
ADAM_EPS = 1e-08
ADAM_WD = 0.01
ADAM_STEP = 10
VMEM_LIMIT = 56 * 1024 * 1024

_DN = {
    "nn": (((1,), (0,)), ((), ())),
    "nt": (((1,), (1,)), ((), ())),
    "tn": (((0,), (0,)), ((), ())),
}


class _Rider:
    def __init__(self, operands, out_shapes, aliases, n_copies, copies=None, start=None, finish=None):
        self.operands = list(operands)
        self.out_shapes = list(out_shapes)
        self.aliases = dict(aliases)
        self.n_copies = n_copies
        self.results = None

        def start_copies(ins, outs, send, recv, base=0):
            for cp in copies(ins, outs, send, recv, base):
                cp.start()

        def wait_copies(ins, outs, send, recv, base=0):
            for cp in copies(ins, outs, send, recv, base):
                cp.wait()

        self.start = start or start_copies
        self.finish = finish or wait_copies


def _riders(*riders):
    operands, out_shapes, aliases, offsets = [], [], {}, []
    n = 0
    for r in riders:
        offsets.append((len(operands), len(out_shapes), n))
        aliases.update({len(operands) + i: len(out_shapes) + o for i, o in r.aliases.items()})
        operands += r.operands
        out_shapes += r.out_shapes
        n += r.n_copies

    def each(which):
        def go(ins, outs, send, recv, base=0):
            for r, (i0, o0, s0) in zip(riders, offsets):
                getattr(r, which)(ins[i0:i0 + len(r.operands)], outs[o0:o0 + len(r.out_shapes)], send, recv,
                                  base + s0)
        return go

    both = _Rider(operands, out_shapes, aliases, n, start=each("start"), finish=each("finish"))
    both.parts = (riders, offsets)
    return both


_pending_rider = []


def _ride(rider, fn, *args, **kw):
    _pending_rider.append(rider)
    out = fn(*args, **kw)
    assert not _pending_rider
    if hasattr(rider, "parts"):
        for r, (_, o0, _) in zip(*rider.parts):
            r.results = rider.results[o0:o0 + len(r.out_shapes)]
    return out


def _pcall(body, prefetch=0, **kw):
    def build(body, kw):
        if not prefetch:
            return pl.pallas_call(body, **kw)
        kw = dict(kw)
        grid_spec = pltpu.PrefetchScalarGridSpec(
            num_scalar_prefetch=prefetch, grid=kw.pop("grid"), in_specs=kw.pop("in_specs"),
            out_specs=kw.pop("out_specs"), scratch_shapes=kw.pop("scratch_shapes", []))
        return pl.pallas_call(body, grid_spec=grid_spec, **kw)

    if not _pending_rider:
        return build(body, kw)
    rider = _pending_rider.pop()
    grid = kw.get("grid", ())
    in_specs = list(kw.get("in_specs", []))
    single = not isinstance(kw["out_shape"], (list, tuple))
    out_shape = [kw["out_shape"]] if single else list(kw["out_shape"])
    out_specs = [kw["out_specs"]] if single else list(kw["out_specs"])
    scratch = list(kw.get("scratch_shapes", []))
    n_in, n_out, n_scr = len(in_specs), len(out_shape), len(scratch)
    r_in, r_out = len(rider.operands), len(rider.out_shapes)

    def wrapped(*refs):
        scalars, refs = refs[:prefetch], refs[prefetch:]
        ins, rins = refs[:n_in], refs[n_in:n_in + r_in]
        outs = refs[n_in + r_in:n_in + r_in + n_out]
        routs = refs[n_in + r_in + n_out:n_in + r_in + n_out + r_out]
        scr = refs[n_in + r_in + n_out + r_out:n_in + r_in + n_out + r_out + n_scr]
        send, recv = refs[-2:]
        ids = [pl.program_id(a) for a in range(len(grid))]
        first, last = True, True
        for a, pid in enumerate(ids):
            first = jnp.logical_and(first, pid == 0)
            last = jnp.logical_and(last, pid == grid[a] - 1)
        if grid:
            pl.when(first)(lambda: rider.start(rins, routs, send, recv))
        else:
            rider.start(rins, routs, send, recv)
        body(*scalars, *ins, *outs, *scr)
        if grid:
            pl.when(last)(lambda: rider.finish(rins, routs, send, recv))
        else:
            rider.finish(rins, routs, send, recv)

    any_spec = pl.BlockSpec(memory_space=pl.ANY)
    kw2 = dict(kw)
    kw2.update(
        in_specs=in_specs + [any_spec] * r_in, out_specs=out_specs + [any_spec] * r_out,
        out_shape=out_shape + rider.out_shapes,
        scratch_shapes=scratch + [pltpu.SemaphoreType.DMA((rider.n_copies,)),
                                  pltpu.SemaphoreType.DMA((rider.n_copies,))],
        input_output_aliases={**kw.get("input_output_aliases", {}),
                              **{prefetch + n_in + i: n_out + o for i, o in rider.aliases.items()}})
    if grid:
        kw2["compiler_params"] = _params(("arbitrary",) * len(grid))
    call = build(wrapped, kw2)

    def run(*operands):
        res = call(*operands, *rider.operands)
        rider.results = list(res[n_out:])
        return res[0] if single else list(res[:n_out])

    return run


def _params(sem):
    return pltpu.CompilerParams(dimension_semantics=sem, vmem_limit_bytes=VMEM_LIMIT)


def _dot(a, b, mode):
    return lax.dot_general(a, b, _DN[mode], preferred_element_type=F32)


def _mm(name, mode, grid, a, a_spec, b, b_spec, out_shape, o_spec, acc_shape):
    nk = grid[-1]

    def body(a_ref, b_ref, o_ref, *acc):
        part = _dot(a_ref[...].astype(BF16), b_ref[...].astype(BF16), mode)
        if nk == 1:
            o_ref[...] = part.astype(o_ref.dtype)
            return
        acc_ref, = acc
        k = pl.program_id(len(grid) - 1)

        @pl.when(k == 0)
        def _():
            acc_ref[...] = part

        @pl.when(k > 0)
        def _():
            acc_ref[...] += part

        @pl.when(k == nk - 1)
        def _():
            o_ref[...] = acc_ref[...].astype(o_ref.dtype)

    scratch = [] if nk == 1 else [pltpu.VMEM(acc_shape, F32)]
    sem = ("parallel",) * (len(grid) - 1) + ("arbitrary",)
    return _pcall(body, name=name, grid=grid, in_specs=[a_spec, b_spec], out_specs=o_spec, out_shape=out_shape,
                  scratch_shapes=scratch, compiler_params=_params(sem))(a, b)


def _mm_tn(name, a, b, tm, tk):
    T, M = a.shape
    N = b.shape[1]
    return _mm(name, "tn", (M // tm, T // tk), a, pl.BlockSpec((tk, tm), lambda i, k: (k, i)),
               b, pl.BlockSpec((tk, N), lambda i, k: (k, 0)),
               jax.ShapeDtypeStruct((M, N), BF16), pl.BlockSpec((tm, N), lambda i, k: (i, 0)), (tm, N))


def _norm_bwd_rows(xf, gain, dh, dres):
    r = lax.rsqrt(jnp.mean(xf * xf, axis=-1, keepdims=True) + RMS_EPS)
    xh = xf * r
    t = dh * gain
    dx = dres + r * (t - xh * jnp.mean(t * xh, axis=-1, keepdims=True))
    return dx, jnp.sum(dh * xh, axis=0, keepdims=True)


def _accumulate(ref, value, first):
    @pl.when(first)
    def _():
        ref[...] = value

    @pl.when(jnp.logical_not(first))
    def _():
        ref[...] += value


def _rms_bwd_folded(name, x, g, dhf, dres, tb=512):
    S, D = x.shape

    def body(x_ref, g_ref, d0_ref, d1_ref, d2_ref, dres_ref, dx_ref, dg_ref, dh_ref):
        chunks = _lane_chunks(D)
        for c, cols in enumerate(chunks):
            dh_ref[c] = d0_ref[0, :, cols].astype(F32)
        for dil, ref in ((ATTN_DILATIONS[1], d1_ref), (ATTN_DILATIONS[2], d2_ref)):
            n = tb // dil
            for k in range(dil):
                for c, cols in enumerate(chunks):
                    dh_ref[c, _strided_rows(k, n, dil), :] += ref[k, :, cols].astype(F32)
        dh = jnp.concatenate([dh_ref[c] for c in range(len(chunks))], axis=1)
        dx, dg = _norm_bwd_rows(x_ref[...], g_ref[...], dh, dres_ref[...])
        dx_ref[...] = dx
        _accumulate(dg_ref, dg, pl.program_id(0) == 0)

    views, specs = _folded_views(dhf, tb)
    row = pl.BlockSpec((tb, D), lambda i: (i, 0))
    vec = pl.BlockSpec((1, D), lambda i: (0, 0))
    return _pcall(body, name=name, grid=(S // tb,), in_specs=[row, vec] + specs + [row], out_specs=[row, vec],
                  out_shape=[jax.ShapeDtypeStruct((S, D), F32), jax.ShapeDtypeStruct((1, D), F32)],
                  scratch_shapes=[pltpu.VMEM((D // LANES, tb, LANES), F32)],
                  compiler_params=_params(("arbitrary",)))(x, g, *views, dres)


def _proj_res_norm(name, a, w, res, gain, tm=512):
    M, K = a.shape
    D = w.shape[1]

    def body(a_ref, w_ref, res_ref, g_ref, x_ref, h_ref):
        xf = res_ref[...] + _dot(a_ref[...], w_ref[...], "nn")
        x_ref[...] = xf
        r = lax.rsqrt(jnp.mean(xf * xf, axis=-1, keepdims=True) + RMS_EPS)
        h_ref[...] = ((xf * r) * g_ref[...]).astype(h_ref.dtype)

    row = pl.BlockSpec((tm, D), lambda i: (i, 0))
    return _pcall(body, name=name, grid=(M // tm,),
                  in_specs=[pl.BlockSpec((tm, K), lambda i: (i, 0)), pl.BlockSpec((K, D), lambda i: (0, 0)), row,
                            pl.BlockSpec((1, D), lambda i: (0, 0))],
                  out_specs=[row, row],
                  out_shape=[jax.ShapeDtypeStruct((M, D), F32), jax.ShapeDtypeStruct((M, D), BF16)],
                  compiler_params=_params(("parallel",)))(a, w, res, gain)


def _proj_res_loss(name, a, w, res, gain, tgt, tm=512):
    M, K = a.shape
    D = w.shape[1]

    def body(a_ref, w_ref, res_ref, g_ref, t_ref, loss_ref, dx_ref, dxb_ref, dg_ref):
        first = pl.program_id(0) == 0
        xf = res_ref[...] + _dot(a_ref[...], w_ref[...], "nn")
        r = lax.rsqrt(jnp.mean(xf * xf, axis=-1, keepdims=True) + RMS_EPS)
        xh = xf * r
        gv = g_ref[...]
        e = xh * gv - t_ref[...]
        lp = 0.5 * jnp.sum(jnp.mean(e * e, axis=-1, keepdims=True), axis=0, keepdims=True)
        dy = e * (1.0 / D)
        t = dy * gv
        dx = r * (t - xh * jnp.mean(t * xh, axis=-1, keepdims=True))
        dx_ref[...] = dx
        dxb_ref[...] = dx.astype(dxb_ref.dtype)
        _accumulate(dg_ref, jnp.sum(dy * xh, axis=0, keepdims=True), first)
        _accumulate(loss_ref, lp, first)

    row = pl.BlockSpec((tm, D), lambda i: (i, 0))
    vec = pl.BlockSpec((1, D), lambda i: (0, 0))
    return _pcall(body, name=name, grid=(M // tm,),
                  in_specs=[pl.BlockSpec((tm, K), lambda i: (i, 0)), pl.BlockSpec((K, D), lambda i: (0, 0)), row,
                            vec, row],
                  out_specs=[pl.BlockSpec((1, 1), lambda i: (0, 0)), row, row, vec],
                  out_shape=[jax.ShapeDtypeStruct((1, 1), F32), jax.ShapeDtypeStruct((M, D), F32),
                             jax.ShapeDtypeStruct((M, D), BF16), jax.ShapeDtypeStruct((1, D), F32)],
                  compiler_params=_params(("arbitrary",)))(a, w, res, gain, tgt)


def _dh_norm_bwd(name, d, w, x, gain, dres, tm=512):
    M, N = d.shape
    D = w.shape[0]

    def body(d_ref, w_ref, x_ref, g_ref, dres_ref, dx_ref, dxb_ref, dg_ref):
        dh = _dot(d_ref[...].astype(BF16), w_ref[...], "nt")
        dx, dg = _norm_bwd_rows(x_ref[...], g_ref[...], dh, dres_ref[...])
        dx_ref[...] = dx
        dxb_ref[...] = dx.astype(dxb_ref.dtype)
        _accumulate(dg_ref, dg, pl.program_id(0) == 0)

    row = pl.BlockSpec((tm, D), lambda i: (i, 0))
    vec = pl.BlockSpec((1, D), lambda i: (0, 0))
    return _pcall(body, name=name, grid=(M // tm,),
                  in_specs=[pl.BlockSpec((tm, N), lambda i: (i, 0)), pl.BlockSpec((D, N), lambda i: (0, 0)), row, vec,
                            row],
                  out_specs=[row, row, vec],
                  out_shape=[jax.ShapeDtypeStruct((M, D), F32), jax.ShapeDtypeStruct((M, D), BF16),
                             jax.ShapeDtypeStruct((1, D), F32)],
                  compiler_params=_params(("arbitrary",)))(d, w, x, gain, dres)


def _sigmoid(v):
    return 0.5 * jnp.tanh(0.5 * v) + 0.5


def _ffn_up(name, h, w_gu, tm=1024):
    S, D = h.shape
    W = FF_SHARD

    def body(h_ref, wg_ref, wu_ref, gu_ref, act_ref):
        hv = h_ref[...]
        gate = _dot(hv, wg_ref[...], "nn")
        up = _dot(hv, wu_ref[...], "nn")
        gu_ref[0] = gate.astype(gu_ref.dtype)
        gu_ref[1] = up.astype(gu_ref.dtype)
        sig = _sigmoid(gate)
        act_ref[...] = ((gate * sig) * up).astype(act_ref.dtype)

    return _pcall(
        body, name=name, grid=(2, S // tm),
        in_specs=[pl.BlockSpec((tm, D), lambda j, i: (i, 0)),
                  pl.BlockSpec((None, D, W), lambda j, i: (j, 0, 0)),
                  pl.BlockSpec((None, D, W), lambda j, i: (j + 2, 0, 0))],
        out_specs=[pl.BlockSpec((2, tm, W), lambda j, i: (0, i, j)), pl.BlockSpec((tm, W), lambda j, i: (i, j))],
        out_shape=[jax.ShapeDtypeStruct((2, S, D_FF), BF16), jax.ShapeDtypeStruct((S, D_FF), BF16)],
        compiler_params=_params(("parallel", "parallel")))(h, w_gu, w_gu)


def _ffn_down_bwd(name, dx, w_down, gu, tm=1024):
    S, D = dx.shape
    W = FF_SHARD

    def body(dx_ref, w_ref, gu_ref, dgu_ref):
        dact = _dot(dx_ref[...].astype(BF16), w_ref[...], "nt")
        gate = gu_ref[0].astype(F32)
        up = gu_ref[1].astype(F32)
        sig = _sigmoid(gate)
        silu = gate * sig
        dgu_ref[0] = (dact * up * (sig * (1.0 + gate * (1.0 - sig)))).astype(dgu_ref.dtype)
        dgu_ref[1] = (dact * silu).astype(dgu_ref.dtype)

    blk = pl.BlockSpec((2, tm, W), lambda j, i: (0, i, j))
    return _pcall(
        body, name=name, grid=(2, S // tm),
        in_specs=[pl.BlockSpec((tm, D), lambda j, i: (i, 0)), pl.BlockSpec((W, D), lambda j, i: (j, 0)), blk],
        out_specs=blk, out_shape=jax.ShapeDtypeStruct((2, S, D_FF), BF16),
        compiler_params=_params(("parallel", "parallel")))(dx, w_down, gu)


def _ffn_dw_up(name, h, dgu, tk=2048):
    S, D = h.shape
    W = FF_SHARD
    tk = min(tk, S)
    return _mm(name, "tn", (N_CHIPS, S // tk), h, pl.BlockSpec((tk, D), lambda s, k: (k, 0)),
               dgu, pl.BlockSpec((None, tk, W), lambda s, k: (s // 2, k, s % 2)),
               jax.ShapeDtypeStruct((N_CHIPS, D, W), BF16), pl.BlockSpec((None, D, W), lambda s, k: (s, 0, 0)),
               (D, W))


def _ffn_dh(name, dgu, w_gu, x, gain, dres, tm=512):
    S = dgu.shape[1]
    W = FF_SHARD

    def body(a_ref, w_hbm, x_ref, g_ref, dres_ref, dx_ref, dg_ref, w_ref, acat_ref, sems):
        first = pl.program_id(0) == 0

        @pl.when(first)
        def _():
            copies = [pltpu.make_async_copy(w_hbm.at[s], w_ref.at[:, pl.ds(s * W, W)], sems.at[s])
                      for s in range(N_CHIPS)]
            for cp in copies:
                cp.start()
            for cp in copies:
                cp.wait()

        acat_ref[:, :D_FF] = a_ref[0]
        acat_ref[:, D_FF:] = a_ref[1]
        dh = _dot(acat_ref[...], w_ref[...], "nt")
        dx, dg = _norm_bwd_rows(x_ref[...], g_ref[...], dh, dres_ref[...])
        dx_ref[...] = dx
        _accumulate(dg_ref, dg, first)

    row = pl.BlockSpec((tm, D_MODEL), lambda i: (i, 0))
    vec = pl.BlockSpec((1, D_MODEL), lambda i: (0, 0))
    return _pcall(body, name=name, grid=(S // tm,),
                  in_specs=[pl.BlockSpec((2, tm, D_FF), lambda i: (0, i, 0)), pl.BlockSpec(memory_space=pl.ANY),
                            row, vec, row],
                  out_specs=[row, vec],
                  out_shape=[jax.ShapeDtypeStruct((S, D_MODEL), F32), jax.ShapeDtypeStruct((1, D_MODEL), F32)],
                  scratch_shapes=[pltpu.VMEM((D_MODEL, 2 * D_FF), BF16), pltpu.VMEM((tm, 2 * D_FF), BF16),
                                  pltpu.SemaphoreType.DMA((N_CHIPS,))],
                  compiler_params=_params(("arbitrary",)))(dgu, w_gu, x, gain, dres)


FOLD_TOKENS = 1024
LANES = 128


def _lane_chunks(width):
    return [slice(c * LANES, (c + 1) * LANES) for c in range(width // LANES)]


def _strided_rows(r, n, dil):
    return pl.ds(r, n) if dil == 1 else pl.ds(r, n, stride=dil)


def _norm_fold(name, x, gain, casts, chip):
    S, D = x.shape
    chunks = _lane_chunks(D)
    steps = S // FOLD_TOKENS
    n_cast = len(casts)
    tiles, cast_in, cast_out, cast_shapes = _cast_plan(casts, steps)

    def body(c_ref, x_ref, g_ref, *refs):
        hf_hbm = refs[n_cast]
        xc_ref, hs_ref, sems = refs[-3:]
        i = pl.program_id(0)
        _cast_step(i, steps, tiles, refs[:n_cast], refs[n_cast + 1:2 * n_cast + 1])
        xf = x_ref[...]
        inv = lax.rsqrt(jnp.mean(xf * xf, axis=-1, keepdims=True) + RMS_EPS)
        h = (xf * inv) * g_ref[...]
        hs_ref[0] = h.astype(BF16)
        for c, cols in enumerate(chunks):
            xc_ref[c] = h[:, cols]
        copies = []
        for g, dil in enumerate(ATTN_DILATIONS):
            n = FOLD_TOKENS // dil
            for r in range(dil):
                if dil > 1:
                    for c, cols in enumerate(chunks):
                        hs_ref[g, r * n:(r + 1) * n, cols] = xc_ref[c, _strided_rows(r, n, dil), :].astype(BF16)
                cp = pltpu.make_async_copy(hs_ref.at[g, pl.ds(r * n, n)],
                                           hf_hbm.at[g, pl.ds(r * (S // dil) + i * n, n)], sems.at[len(copies)])
                cp.start()
                copies.append(cp)
        for cp in copies:
            cp.wait()

    return _pcall(body, prefetch=1, name=name, grid=(steps,),
                  in_specs=[pl.BlockSpec((FOLD_TOKENS, D), lambda i, c_ref: (i, 0)),
                            pl.BlockSpec((1, D), lambda i, c_ref: (0, 0))] + cast_in,
                  out_specs=[pl.BlockSpec(memory_space=pl.ANY)] + cast_out,
                  out_shape=[jax.ShapeDtypeStruct((N_GROUPS, S, D), BF16)] + cast_shapes,
                  scratch_shapes=[pltpu.VMEM((D // LANES, FOLD_TOKENS, LANES), F32),
                                  pltpu.VMEM((N_GROUPS, FOLD_TOKENS, D), BF16),
                                  pltpu.SemaphoreType.DMA((sum(ATTN_DILATIONS),))],
                  compiler_params=_params(("arbitrary",)))(chip, x, gain, *[w for w, _, _ in casts])


def _qkv_tiles(name, piece, hf, w, qkv, chip, tm=2048):
    S = hf.shape[1]
    per_group = 3 * D_MODEL // QKV_TILE

    def tile(q, c_ref):
        if piece is None:
            return QKV_PIECES * c_ref[0] + q
        return QKV_PIECES * ((c_ref[0] + 1 + q) % N_CHIPS) + piece

    def body(*refs):
        a_ref, w_ref, o_ref = refs[1], refs[2], refs[-1]
        o_ref[...] = _dot(a_ref[...], w_ref[...], "nn").astype(o_ref.dtype)

    if piece is None:
        w_spec = pl.BlockSpec((D_MODEL, QKV_TILE), lambda q, i, c_ref: (0, q))
    else:
        w_spec = pl.BlockSpec((None, D_MODEL, QKV_TILE), lambda q, i, c_ref: ((c_ref[0] + 1 + q) % N_CHIPS, 0, 0))
    in_specs = [pl.BlockSpec((None, tm, D_MODEL), lambda q, i, c_ref: (tile(q, c_ref) // per_group, i, 0)), w_spec]
    operands = [chip, hf, w]
    aliases = {}
    if qkv is not None:
        in_specs.append(pl.BlockSpec(memory_space=pl.ANY))
        operands.append(qkv)
        aliases = {3: 0}
    return _pcall(
        body, prefetch=1, name=name, grid=(N_CHIPS - 1, S // tm), in_specs=in_specs,
        out_specs=pl.BlockSpec((None, tm, QKV_TILE),
                               lambda q, i, c_ref: (tile(q, c_ref) // per_group, i, tile(q, c_ref) % per_group)),
        out_shape=jax.ShapeDtypeStruct((N_GROUPS, S, 3 * D_MODEL), BF16), input_output_aliases=aliases,
        compiler_params=_params(("arbitrary", "arbitrary")))(*operands)


QKV_PIECES = QKV_SHARD // QKV_TILE


def _qkv_dw(name, p, hf, dqkv):
    S = hf.shape[1]
    per_group = 3 * D_MODEL // QKV_TILE
    tile = lambda s: QKV_PIECES * s + p
    return _mm(name, "tn", (N_CHIPS, 1),
               hf, pl.BlockSpec((None, S, D_MODEL), lambda s, k: (tile(s) // per_group, 0, 0)),
               dqkv, pl.BlockSpec((None, S, QKV_TILE), lambda s, k: (tile(s) // per_group, 0, tile(s) % per_group)),
               jax.ShapeDtypeStruct((N_CHIPS, D_MODEL, QKV_TILE), BF16),
               pl.BlockSpec((None, D_MODEL, QKV_TILE), lambda s, k: (s, 0, 0)), None)


def _qkv_dh(name, g, dqkv, w_pieces, dhf, tm=1024):
    S = dqkv.shape[1]
    per_group = 3 * D_MODEL // QKV_TILE

    def body(*refs):
        a_ref, w_hbm = refs[0], refs[1:1 + QKV_PIECES]
        o_ref, w_ref, sems = refs[-3:]

        @pl.when(pl.program_id(0) == 0)
        def _():
            copies = []
            for j in range(per_group):
                t = per_group * g + j
                copies.append(pltpu.make_async_copy(w_hbm[t % QKV_PIECES].at[t // QKV_PIECES],
                                                    w_ref.at[:, pl.ds(j * QKV_TILE, QKV_TILE)], sems.at[j]))
            for cp in copies:
                cp.start()
            for cp in copies:
                cp.wait()

        o_ref[...] = _dot(a_ref[...], w_ref[...], "nt").astype(o_ref.dtype)

    any_spec = pl.BlockSpec(memory_space=pl.ANY)
    in_specs = [pl.BlockSpec((None, tm, 3 * D_MODEL), lambda i: (g, i, 0))] + [any_spec] * QKV_PIECES
    operands = [dqkv] + list(w_pieces)
    aliases = {}
    if dhf is not None:
        in_specs.append(any_spec)
        operands.append(dhf)
        aliases = {1 + QKV_PIECES: 0}
    return _pcall(body, name=name, grid=(S // tm,), in_specs=in_specs,
                  out_specs=pl.BlockSpec((None, tm, D_MODEL), lambda i: (g, i, 0)),
                  out_shape=jax.ShapeDtypeStruct((N_GROUPS, S, D_MODEL), BF16),
                  scratch_shapes=[pltpu.VMEM((D_MODEL, 3 * D_MODEL), BF16), pltpu.SemaphoreType.DMA((per_group,))],
                  input_output_aliases=aliases, compiler_params=_params(("arbitrary",)))(*operands)


def _alibi_coef(g, h):
    vals = [-(2.0 ** (-8.0 * (gg * N_HEADS + h + 1) / (N_GROUPS * N_HEADS))) * ATTN_DILATIONS[gg]
            for gg in range(N_GROUPS)]
    return jnp.where(g == 0, vals[0], jnp.where(g == 1, vals[1], vals[2])).astype(F32)


def _blocks_per_segment(g, S):
    return jnp.right_shift(S // Q_BLOCK, 2 * g)


def _band_bias(pen):
    row = lax.broadcasted_iota(jnp.int32, (Q_BLOCK, 2 * Q_BLOCK), 0)
    col = lax.broadcasted_iota(jnp.int32, (Q_BLOCK, 2 * Q_BLOCK), 1)
    dist = Q_BLOCK + row - col
    valid = jnp.logical_and(dist >= 0, dist <= Q_BLOCK)
    base = jnp.where(valid, jnp.where(col < Q_BLOCK, pen, 0.0), NEG).astype(F32)
    return base, dist.astype(F32)


def _skewed(n_units, stages):
    state = {}
    for step in range(n_units + len(stages) - 1):
        for s, stage in enumerate(stages):
            u = step - s
            if 0 <= u < n_units:
                state[u] = stage(u, state.get(u))
                if s == len(stages) - 1:
                    del state[u]


def _attn_fwd(name, qkv, tq=1024):
    S = qkv.shape[1]
    nqb = tq // Q_BLOCK
    scale = HEAD_DIM ** -0.5

    def body(q_ref, k_ref, kp_ref, v_ref, vp_ref, o_ref, lse_ref, kf_ref, vf_ref):
        g = pl.program_id(0)
        i = pl.program_id(1)
        nb = _blocks_per_segment(g, S)
        kf_ref[:Q_BLOCK] = kp_ref[...]
        kf_ref[Q_BLOCK:] = k_ref[...]
        vf_ref[:Q_BLOCK] = vp_ref[...]
        vf_ref[Q_BLOCK:] = v_ref[...]
        coefs = [_alibi_coef(g, h) for h in range(N_HEADS)]
        units = [(b, h) for b in range(nqb) for h in range(N_HEADS)]
        bias = {}

        def scores(u, _):
            b, h = units[u]
            if b not in bias:
                pen = jnp.where((i * nqb + b) % nb != 0, 0.0, NEG).astype(F32)
                bias.clear()
                bias[b] = _band_bias(pen)
            base, dist = bias[b]
            cols = slice(h * HEAD_DIM, (h + 1) * HEAD_DIM)
            q = q_ref[b * Q_BLOCK:(b + 1) * Q_BLOCK, cols]
            kk = kf_ref[b * Q_BLOCK:(b + 2) * Q_BLOCK, cols]
            return _dot(q, kk, "nt") * scale + (coefs[h] * dist + base)

        def softmax(u, s):
            m = jnp.max(s, axis=-1, keepdims=True)
            p = jnp.exp(s - m)
            den = jnp.sum(p, axis=-1, keepdims=True)
            return (p * (1.0 / den)).astype(BF16), m + jnp.log(den)

        def output(u, st):
            b, h = units[u]
            pn, lse = st
            cols = slice(h * HEAD_DIM, (h + 1) * HEAD_DIM)
            rows = slice(b * Q_BLOCK, (b + 1) * Q_BLOCK)
            o_ref[rows, cols] = _dot(pn, vf_ref[b * Q_BLOCK:(b + 2) * Q_BLOCK, cols], "nn").astype(o_ref.dtype)
            lse_ref[rows, h:h + 1] = lse

        _skewed(len(units), [scores, softmax, output])

    main = lambda c: pl.BlockSpec((None, tq, D_MODEL), lambda g, i: (g, i, c))
    prev = lambda c: pl.BlockSpec((None, Q_BLOCK, D_MODEL), lambda g, i: (g, jnp.maximum(i * nqb - 1, 0), c))
    return _pcall(
        body, name=name, grid=(N_GROUPS, S // tq),
        in_specs=[main(0), main(1), prev(1), main(2), prev(2)],
        out_specs=[pl.BlockSpec((None, tq, D_MODEL), lambda g, i: (g, i, 0)),
                   pl.BlockSpec((None, tq, N_HEADS), lambda g, i: (g, i, 0))],
        out_shape=[jax.ShapeDtypeStruct((N_GROUPS, S, D_MODEL), BF16),
                   jax.ShapeDtypeStruct((N_GROUPS, S, N_HEADS), F32)],
        scratch_shapes=[pltpu.VMEM((tq + Q_BLOCK, D_MODEL), BF16), pltpu.VMEM((tq + Q_BLOCK, D_MODEL), BF16)],
        compiler_params=_params(("parallel", "parallel")))(qkv, qkv, qkv, qkv, qkv)


def _attn_bwd(name, qkv, do, lse, dl, tq=1024):
    S = qkv.shape[1]
    nqb = tq // Q_BLOCK
    n_blocks = S // Q_BLOCK
    scale = HEAD_DIM ** -0.5
    KEYS = 2 * Q_BLOCK

    def body(q_ref, k_ref, v_ref, kp_ref, vp_ref, qn_ref, do_ref, don_ref, lse_ref, lsen_ref, dl_ref, dln_ref,
             out_ref, kf_ref, vf_ref, dk_ref, dv_ref):
        g = pl.program_id(0)
        i = pl.program_id(1)
        nb = _blocks_per_segment(g, S)
        kf_ref[:Q_BLOCK] = kp_ref[...]
        kf_ref[Q_BLOCK:] = k_ref[...]
        vf_ref[:Q_BLOCK] = vp_ref[...]
        vf_ref[Q_BLOCK:] = v_ref[...]
        coefs = [_alibi_coef(g, h) for h in range(N_HEADS)]
        units = [(h, b) for h in range(N_HEADS) for b in range(nqb + 1)]
        bias = {}

        def band(b):
            if b not in bias:
                blk = i * nqb + b
                ok = blk % nb != 0
                if b == nqb:
                    ok = jnp.logical_and(ok, blk < n_blocks)
                bias[b] = _band_bias(jnp.where(ok, 0.0, NEG).astype(F32))
            return bias[b]

        def operands(h, b):
            cols = slice(h * HEAD_DIM, (h + 1) * HEAD_DIM)
            if b == nqb:
                keys = slice(tq, tq + Q_BLOCK)
                return (qn_ref[:, cols], don_ref[:, cols], lsen_ref[:, h:h + 1], dln_ref[:, h:h + 1],
                        kf_ref[keys, cols], vf_ref[keys, cols])
            rows = slice(b * Q_BLOCK, (b + 1) * Q_BLOCK)
            keys = slice(b * Q_BLOCK, b * Q_BLOCK + KEYS)
            return (q_ref[rows, cols], do_ref[rows, cols], lse_ref[rows, h:h + 1], dl_ref[rows, h:h + 1],
                    kf_ref[keys, cols], vf_ref[keys, cols])

        def probs(u, _):
            h, b = units[u]
            q, do_b, lse_b, dl_b, kk, vv = operands(h, b)
            base, dist = band(b)
            if b == nqb:
                base, dist = base[:, :Q_BLOCK], dist[:, :Q_BLOCK]
            p = jnp.exp(_dot(q, kk, "nt") * scale + (coefs[h] * dist + base) - lse_b)
            return p, _dot(do_b, vv, "nt")

        def dscores(u, st):
            h, b = units[u]
            p, dp = st
            dl_b = operands(h, b)[3]
            return ((p * (dp - dl_b)) * scale).astype(BF16), p.astype(BF16)

        def grads(u, st):
            h, b = units[u]
            ds, pb = st
            q, do_b, _, _, kk, _ = operands(h, b)
            cols = slice(h * HEAD_DIM, (h + 1) * HEAD_DIM)
            if b < nqb:
                out_ref[b * Q_BLOCK:(b + 1) * Q_BLOCK, cols] = _dot(ds, kk, "nn").astype(out_ref.dtype)
            if b == 0:
                dk_ref[:Q_BLOCK] = _dot(ds[:, Q_BLOCK:], q, "tn")
                dv_ref[:Q_BLOCK] = _dot(pb[:, Q_BLOCK:], do_b, "tn")
                return None
            dk = _dot(ds, q, "tn")
            dv = _dot(pb, do_b, "tn")
            before = slice((b - 1) * Q_BLOCK, b * Q_BLOCK)
            dk_ref[before] += dk[:Q_BLOCK]
            dv_ref[before] += dv[:Q_BLOCK]
            if b < nqb:
                own = slice(b * Q_BLOCK, (b + 1) * Q_BLOCK)
                dk_ref[own] = dk[Q_BLOCK:]
                dv_ref[own] = dv[Q_BLOCK:]
            else:
                out_ref[:, D_MODEL + h * HEAD_DIM:D_MODEL + (h + 1) * HEAD_DIM] = dk_ref[...].astype(out_ref.dtype)
                out_ref[:, 2 * D_MODEL + h * HEAD_DIM:2 * D_MODEL + (h + 1) * HEAD_DIM] = (
                    dv_ref[...].astype(out_ref.dtype))
            return None

        _skewed(len(units), [probs, dscores, grads])

    nxt_blk = lambda i: jnp.minimum((i + 1) * nqb, n_blocks - 1)
    main = lambda c: pl.BlockSpec((None, tq, D_MODEL), lambda g, i: (g, i, c))
    prev = lambda c: pl.BlockSpec((None, Q_BLOCK, D_MODEL), lambda g, i: (g, jnp.maximum(i * nqb - 1, 0), c))
    row = pl.BlockSpec((None, tq, D_MODEL), lambda g, i: (g, i, 0))
    row_n = pl.BlockSpec((None, Q_BLOCK, D_MODEL), lambda g, i: (g, nxt_blk(i), 0))
    col = pl.BlockSpec((None, tq, N_HEADS), lambda g, i: (g, i, 0))
    col_n = pl.BlockSpec((None, Q_BLOCK, N_HEADS), lambda g, i: (g, nxt_blk(i), 0))
    return _pcall(
        body, name=name, grid=(N_GROUPS, S // tq),
        in_specs=[main(0), main(1), main(2), prev(1), prev(2), row_n, row, row_n, col, col_n, col, col_n],
        out_specs=pl.BlockSpec((None, tq, 3 * D_MODEL), lambda g, i: (g, i, 0)),
        out_shape=jax.ShapeDtypeStruct((N_GROUPS, S, 3 * D_MODEL), BF16),
        scratch_shapes=[pltpu.VMEM((tq + Q_BLOCK, D_MODEL), BF16), pltpu.VMEM((tq + Q_BLOCK, D_MODEL), BF16),
                        pltpu.VMEM((tq, HEAD_DIM), F32), pltpu.VMEM((tq, HEAD_DIM), F32)],
        compiler_params=_params(("parallel", "parallel")))(qkv, qkv, qkv, qkv, qkv, qkv, do, do, lse, lse, dl, dl)


def _group_weights(lse_ref):
    l0, l1, l2 = lse_ref[0], lse_ref[1], lse_ref[2]
    m = jnp.maximum(jnp.maximum(l0, l1), l2)
    e = [jnp.exp(l0 - m), jnp.exp(l1 - m), jnp.exp(l2 - m)]
    den = e[0] + e[1] + e[2]
    return [ei / den for ei in e]


MERGE_TOKENS = 512


def _folded_views(a, tb):
    _, S, C = a.shape
    views, specs = [], []
    for g, dil in enumerate(ATTN_DILATIONS):
        views.append(a.reshape(N_GROUPS * dil, S // dil, C))
        specs.append(pl.BlockSpec((dil, tb // dil, C), lambda i, g=g: (g, i, 0)))
    return views, specs


def _unfold_into(dst_ref, folded_refs):
    for g, (dil, ref) in enumerate(zip(ATTN_DILATIONS, folded_refs)):
        n = ref.shape[1]
        for r in range(dil):
            for c, cols in enumerate(_lane_chunks(ref.shape[2])):
                dst_ref[g, c, _strided_rows(r, n, dil), :] = ref[r, :, cols].astype(F32)


def _merge_fwd(name, o, lse, tb=MERGE_TOKENS):
    S = o.shape[1]

    def body(o0_ref, o1_ref, o2_ref, lse_ref, out_ref, o_ref):
        _unfold_into(o_ref, (o0_ref, o1_ref, o2_ref))
        w = _group_weights(lse_ref)
        for h in range(N_HEADS):
            cols = slice(h * HEAD_DIM, (h + 1) * HEAD_DIM)
            acc = w[0][:, h:h + 1] * o_ref[0, h]
            for gg in range(1, N_GROUPS):
                acc = acc + w[gg][:, h:h + 1] * o_ref[gg, h]
            out_ref[:, cols] = acc.astype(out_ref.dtype)

    views, specs = _folded_views(o, tb)
    return _pcall(body, name=name, grid=(S // tb,),
                  in_specs=specs + [pl.BlockSpec((N_GROUPS, tb, N_HEADS), lambda i: (0, i, 0))],
                  out_specs=pl.BlockSpec((tb, D_MODEL), lambda i: (i, 0)),
                  out_shape=jax.ShapeDtypeStruct((S, D_MODEL), BF16),
                  scratch_shapes=[pltpu.VMEM((N_GROUPS, N_HEADS, tb, HEAD_DIM), F32)],
                  compiler_params=_params(("parallel",)))(*views, lse)


def _merge_bwd(name, dx, w_out, o, lse, tb=MERGE_TOKENS):
    S = o.shape[1]
    n_chunks = sum(ATTN_DILATIONS)

    def body(dx_ref, w_ref, o0_ref, o1_ref, o2_ref, lse_ref, do_hbm, dl_ref, o_ref, dt_ref, df_ref, d_ref, sems):
        i = pl.program_id(0)
        d_ref[...] = _dot(dx_ref[...].astype(BF16), w_ref[...], "nt")
        _unfold_into(o_ref, (o0_ref, o1_ref, o2_ref))
        w = _group_weights(lse_ref)
        for h in range(N_HEADS):
            cols = slice(h * HEAD_DIM, (h + 1) * HEAD_DIM)
            dv = d_ref[:, cols]
            merged = w[0][:, h:h + 1] * o_ref[0, h]
            for gg in range(1, N_GROUPS):
                merged = merged + w[gg][:, h:h + 1] * o_ref[gg, h]
            dsum = jnp.sum(dv * merged, axis=-1, keepdims=True)
            for gg in range(N_GROUPS):
                wg = w[gg][:, h:h + 1]
                dt_ref[gg, h] = wg * dv
                dl_ref[gg, :, h:h + 1] = wg * dsum
        copies = []
        for gg, dil in enumerate(ATTN_DILATIONS):
            n = tb // dil
            for r in range(dil):
                for h, cols in enumerate(_lane_chunks(D_MODEL)):
                    df_ref[gg, r * n:(r + 1) * n, cols] = (
                        dt_ref[gg, h, _strided_rows(r, n, dil), :].astype(df_ref.dtype))
                cp = pltpu.make_async_copy(df_ref.at[gg, pl.ds(r * n, n)],
                                           do_hbm.at[gg, pl.ds(r * (S // dil) + i * n, n)], sems.at[len(copies)])
                cp.start()
                copies.append(cp)
        for cp in copies:
            cp.wait()

    views, specs = _folded_views(o, tb)
    small = pl.BlockSpec((N_GROUPS, tb, N_HEADS), lambda i: (0, i, 0))
    return _pcall(body, name=name, grid=(S // tb,),
                  in_specs=[pl.BlockSpec((tb, D_MODEL), lambda i: (i, 0)),
                            pl.BlockSpec((D_MODEL, D_MODEL), lambda i: (0, 0))] + specs + [small],
                  out_specs=[pl.BlockSpec(memory_space=pl.ANY), small],
                  out_shape=[jax.ShapeDtypeStruct((N_GROUPS, S, D_MODEL), BF16),
                             jax.ShapeDtypeStruct((N_GROUPS, S, N_HEADS), F32)],
                  scratch_shapes=[pltpu.VMEM((N_GROUPS, N_HEADS, tb, HEAD_DIM), F32),
                                  pltpu.VMEM((N_GROUPS, N_HEADS, tb, HEAD_DIM), F32),
                                  pltpu.VMEM((N_GROUPS, tb, D_MODEL), BF16), pltpu.VMEM((tb, D_MODEL), F32),
                                  pltpu.SemaphoreType.DMA((n_chunks,))],
                  compiler_params=_params(("arbitrary",)))(dx, w_out, *views, lse)


def _shift_rows(a, k):
    return pltpu.roll(a, k % a.shape[0], axis=0)


def _pool_level(g, levels):
    return jnp.where(g == 0, levels[0], jnp.where(g == 1, levels[1], jnp.where(g == 2, levels[2], levels[3])))


def _pool_fwd(name, h, w_in, w_group, scale, x_in, gain_next, tr=1024):
    S, D = h.shape
    H = POOL_HALO
    n_groups = D // POOL_DIM

    def body(h_ref, hh_ref, wi_ref, w_ref, sc_ref, x_ref, gn_ref, y_ref, z_ref, xo_ref, hn_ref, xs_ref):
        i = pl.program_id(0)
        g = pl.program_id(1)
        uv = _dot(h_ref[...], wi_ref[...], "nn")
        halo = jnp.where(i > 0, _dot(hh_ref[...], wi_ref[...], "nn"), 0.0)
        s = jnp.concatenate([halo, uv], axis=0)
        levels = []
        for k in (1, 2, 4, 8):
            s = s + _shift_rows(s, k)
            levels.append(s[H:])
        t = i * tr + lax.broadcasted_iota(jnp.int32, (tr, 1), 0)
        cnt = jnp.minimum(t + 1, jnp.left_shift(2, g)).astype(F32)
        y = _pool_level(g, levels) / cnt - uv
        yb = y.astype(BF16)
        z = _dot(yb, w_ref[...], "nn")
        y_ref[...] = yb
        z_ref[...] = z.astype(z_ref.dtype)
        x_out = x_ref[...] + z * sc_ref[...]
        xo_ref[...] = x_out
        xs_ref[g] = x_out

        @pl.when(g == n_groups - 1)
        def _():
            xf = jnp.concatenate([xs_ref[k] for k in range(n_groups)], axis=1)
            r = lax.rsqrt(jnp.mean(xf * xf, axis=-1, keepdims=True) + RMS_EPS)
            hn_ref[...] = ((xf * r) * gn_ref[...]).astype(hn_ref.dtype)

    blk = pl.BlockSpec((tr, POOL_DIM), lambda i, g: (i, g))
    row = pl.BlockSpec((tr, D), lambda i, g: (i, 0))
    return _pcall(
        body, name=name, grid=(S // tr, n_groups),
        in_specs=[row, pl.BlockSpec((H, D), lambda i, g: (jnp.maximum(i * (tr // H) - 1, 0), 0)),
                  pl.BlockSpec((D, POOL_DIM), lambda i, g: (0, g)),
                  pl.BlockSpec((None, POOL_DIM, POOL_DIM), lambda i, g: (g, 0, 0)),
                  pl.BlockSpec((1, POOL_DIM), lambda i, g: (0, g)), blk, pl.BlockSpec((1, D), lambda i, g: (0, 0))],
        out_specs=[blk, blk, blk, row],
        out_shape=[jax.ShapeDtypeStruct((S, D), BF16), jax.ShapeDtypeStruct((S, D), BF16),
                   jax.ShapeDtypeStruct((S, D), F32), jax.ShapeDtypeStruct((S, D), BF16)],
        scratch_shapes=[pltpu.VMEM((n_groups, tr, POOL_DIM), F32)],
        compiler_params=_params(("parallel", "arbitrary")))(h, h, w_in, w_group, scale, x_in, gain_next)


def _pool_bwd(name, d_out, z, y, scale, w_group, tr=1024):
    S, D = d_out.shape
    H = POOL_HALO

    def body(d_ref, dn_ref, z_ref, y_ref, sc_ref, w_ref, du_ref, dw_ref, dsc_ref):
        g = pl.program_id(0)
        i = pl.program_id(1)
        dv = d_ref[...]
        dz = jnp.concatenate([dv, dn_ref[...]], axis=0) * sc_ref[...]
        dzb = dz.astype(BF16)
        dy = _dot(dzb, w_ref[...], "nt")
        t = i * tr + lax.broadcasted_iota(jnp.int32, (tr + H, 1), 0)
        cnt = jnp.minimum(t + 1, jnp.left_shift(2, g)).astype(F32)
        s = jnp.where(t < S, dy / cnt, 0.0)
        levels = []
        for k in (1, 2, 4, 8):
            s = s + _shift_rows(s, -k)
            levels.append(s[:tr])
        du_ref[...] = (_pool_level(g, levels) - dy[:tr]).astype(du_ref.dtype)
        dw = _dot(y_ref[...], dzb[:tr], "tn")
        dsc = jnp.sum(dv * z_ref[...].astype(F32), axis=0, keepdims=True)

        @pl.when(i == 0)
        def _():
            dw_ref[...] = dw
            dsc_ref[...] = dsc

        @pl.when(i > 0)
        def _():
            dw_ref[...] += dw
            dsc_ref[...] += dsc

    blk = pl.BlockSpec((tr, POOL_DIM), lambda g, i: (i, g))
    vec = pl.BlockSpec((1, POOL_DIM), lambda g, i: (0, g))
    mat = pl.BlockSpec((None, POOL_DIM, POOL_DIM), lambda g, i: (g, 0, 0))
    nxt = pl.BlockSpec((H, POOL_DIM), lambda g, i: (jnp.minimum((i + 1) * (tr // H), S // H - 1), g))
    return _pcall(
        body, name=name, grid=(D // POOL_DIM, S // tr), in_specs=[blk, nxt, blk, blk, vec, mat],
        out_specs=[blk, mat, vec],
        out_shape=[jax.ShapeDtypeStruct((S, D), BF16), jax.ShapeDtypeStruct((D // POOL_DIM, POOL_DIM, POOL_DIM), F32),
                   jax.ShapeDtypeStruct((1, D), F32)],
        compiler_params=_params(("parallel", "arbitrary")))(d_out, d_out, z, y, scale, w_group)


def _row_tile(rows, cols, target_bytes=1 << 20):
    best = rows
    for tr in range(8, rows + 1, 8):
        if rows % tr == 0 and tr * cols * 4 <= target_bytes:
            best = tr
    return best if best * cols * 4 <= 4 * target_bytes else rows


def _adamw_step(w, g, m, v):
    m2 = ADAM_B1 * m + (1.0 - ADAM_B1) * g
    v2 = ADAM_B2 * v + (1.0 - ADAM_B2) * (g * g)
    m_hat = m2 / (1.0 - ADAM_B1 ** ADAM_STEP)
    v_hat = v2 / (1.0 - ADAM_B2 ** ADAM_STEP)
    return -ADAM_LR * (m_hat / (jnp.sqrt(v_hat) + ADAM_EPS) + ADAM_WD * w), m2, v2


def _adamw_refs(w_ref, g_ref, m_ref, v_ref, go_ref, d_ref, mo_ref, vo_ref):
    gv = g_ref[...]
    d_ref[...], mo_ref[...], vo_ref[...] = _adamw_step(w_ref[...], gv, m_ref[...], v_ref[...])
    go_ref[...] = gv


def _adamw(name, items, steps=16):
    n = len(items)

    def body(*refs):
        for t in range(n):
            _adamw_refs(*refs[4 * t:4 * t + 4], *refs[4 * n + 4 * t:4 * n + 4 * t + 4])

    specs, shapes = [], []
    for w, _, _, _ in items:
        R, C = w.shape
        assert R % (8 * steps) == 0, (name, w.shape)
        specs += [pl.BlockSpec((R // steps, C), lambda i: (i, 0))] * 4
        shapes += [jax.ShapeDtypeStruct((R, C), F32)] * 4
    res = _pcall(body, name=name, grid=(steps,), in_specs=specs, out_specs=specs, out_shape=shapes,
                 compiler_params=_params(("parallel",)))(*[a for item in items for a in item])
    return [res[4 * t:4 * t + 4] for t in range(n)]


def _adamw_small(name, items):
    n = len(items)

    def body(*refs):
        for t in range(n):
            _adamw_refs(*refs[4 * t:4 * t + 4], *refs[4 * n + 4 * t:4 * n + 4 * t + 4])

    specs, shapes = [], []
    for w, _, _, _ in items:
        specs += [pl.BlockSpec(w.shape, lambda i: (0, 0))] * 4
        shapes += [jax.ShapeDtypeStruct(w.shape, F32)] * 4
    res = _pcall(body, name=name, grid=(1,), in_specs=specs, out_specs=specs, out_shape=shapes,
                 compiler_params=_params(("arbitrary",)))(*[a for item in items for a in item])
    return [res[4 * t:4 * t + 4] for t in range(n)]


def _sum_leading(name, a, out_dtype):
    n, R, C = a.shape
    tr = _row_tile(R, C) if R % 16 == 0 else R
    if tr % 16:
        tr = R

    def body(a_ref, o_ref):
        acc = a_ref[0].astype(F32)
        for j in range(1, n):
            acc = acc + a_ref[j].astype(F32)
        o_ref[...] = acc.astype(o_ref.dtype)

    return _pcall(body, name=name, grid=(R // tr,), in_specs=[pl.BlockSpec((n, tr, C), lambda i: (0, i, 0))],
                  out_specs=pl.BlockSpec((tr, C), lambda i: (i, 0)), out_shape=jax.ShapeDtypeStruct((R, C), out_dtype),
                  compiler_params=_params(("parallel",)))(a)


def _cast_plan(items, steps):
    tiles, in_specs, out_specs, out_shape = [], [], [], []
    for w, layer, dtype in items:
        _, R, C = w.shape
        nt = steps if R % (steps * 16) == 0 else 1
        tiles.append(nt)
        in_specs.append(pl.BlockSpec((None, R // nt, C),
                                     lambda i, c_ref, l=layer, nt=nt: (l, jnp.minimum(i, nt - 1), 0)))
        out_specs.append(pl.BlockSpec((None, R // nt, C),
                                      lambda i, c_ref, nt=nt: (c_ref[0], jnp.minimum(i, nt - 1), 0)))
        out_shape.append(jax.ShapeDtypeStruct((N_CHIPS, R, C), dtype))
    return tiles, in_specs, out_specs, out_shape


def _cast_step(i, steps, tiles, w_refs, o_refs):
    for nt, w_ref, o_ref in zip(tiles, w_refs, o_refs):
        def cast(w_ref=w_ref, o_ref=o_ref):
            o_ref[...] = w_ref[...].astype(o_ref.dtype)

        if nt == steps:
            cast()
        else:
            pl.when(i < nt)(cast)


def _cast_qkv(name, w, chip, tr=256):
    _, R, C = w.shape

    def body(c_ref, w_ref, own_ref, *piece_refs):
        v = w_ref[...].astype(BF16)
        own_ref[...] = v
        for p, ref in enumerate(piece_refs):
            ref[...] = v[:, p * QKV_TILE:(p + 1) * QKV_TILE]

    piece = pl.BlockSpec((None, tr, QKV_TILE), lambda i, c_ref: (c_ref[0], i, 0))
    return _pcall(body, prefetch=1, name=name, grid=(R // tr,),
                  in_specs=[pl.BlockSpec((None, tr, C), lambda i, c_ref: (0, i, 0))],
                  out_specs=[pl.BlockSpec((tr, C), lambda i, c_ref: (i, 0))] + [piece] * QKV_PIECES,
                  out_shape=[jax.ShapeDtypeStruct((R, C), BF16)]
                  + [jax.ShapeDtypeStruct((N_CHIPS, R, QKV_TILE), BF16)] * QKV_PIECES,
                  compiler_params=_params(("parallel",)))(chip, w)


def _owner_sum(name, mine, landed, shape, chip, core, out, layer, col):
    _, half, C = mine.shape
    k, _ = col
    tr = _row_tile(half, C)
    if tr % 16:
        tr = half
    nrt = half // tr

    def body(ch_ref, co_ref, mine_ref, landed_ref, *rest):
        g = mine_ref[...].astype(F32)
        for j in range(3):
            g = g + landed_ref[j].astype(F32)
        rest[-1][...] = g

    piece = pl.BlockSpec((None, tr, C), lambda i, ch, co: (layer, co[0] * nrt + i, k))
    in_specs = [pl.BlockSpec((None, tr, C), lambda i, ch, co: (ch[0], i, 0)),
                pl.BlockSpec((3, tr, C), lambda i, ch, co: (0, i, 0))]
    operands = [chip, core, mine, landed]
    aliases = {}
    if out is not None:
        in_specs.append(ANY)
        operands.append(out)
        aliases = {4: 0}
    return _pcall(body, prefetch=2, name=name, grid=(nrt,), in_specs=in_specs, out_specs=piece,
                  out_shape=jax.ShapeDtypeStruct(shape, F32), input_output_aliases=aliases,
                  compiler_params=_params(("parallel",)))(*operands)


def _add_my_half(name, ps, qs, core):
    n = len(ps)

    def body(c_ref, *refs):
        for t in range(n):
            p_ref, q_ref, o_ref = refs[t], refs[n + t], refs[2 * n + t]
            o_ref[...] = (p_ref[...].astype(F32) + q_ref[...].astype(F32)).astype(o_ref.dtype)

    halves = [(p.shape[1] // 2, p.shape[2]) for p in ps]
    mine = [pl.BlockSpec((None, h, c), lambda s, c_ref: (s, c_ref[0], 0)) for h, c in halves]
    whole = [pl.BlockSpec((None, h, c), lambda s, c_ref: (s, 0, 0)) for h, c in halves]
    return _pcall(body, prefetch=1, name=name, grid=(N_CHIPS,), in_specs=mine + whole, out_specs=whole,
                  out_shape=[jax.ShapeDtypeStruct((N_CHIPS, h, c), BF16) for h, c in halves],
                  compiler_params=_params(("parallel",)))(core, *ps, *qs)


ANY = pl.BlockSpec(memory_space=pl.ANY)


def _place():
    x, y, c = lax.axis_index("x"), lax.axis_index("y"), lax.axis_index("c")
    other_chips = [(1 - x, y), (x, 1 - y), (1 - x, 1 - y)]
    return x, y, c, other_chips


def _remote(src, dst, send, recv, k, to):
    return pltpu.make_async_remote_copy(src_ref=src, dst_ref=dst, send_sem=send.at[k], recv_sem=recv.at[k],
                                        device_id=to, device_id_type=MESH)


def _in_place(bufs):
    return dict(operands=bufs, out_shapes=[jax.ShapeDtypeStruct(b.shape, b.dtype) for b in bufs],
                aliases={t: t for t in range(len(bufs))})


def _gather_rider(bufs, jobs):
    def copies(ins, outs, send, recv, base=0):
        x, y, c, chips = _place()
        out = []
        for t, over_ici, rows in jobs:
            ref = outs[t]
            if rows is not None:
                a, b, n = rows
                half = ref.shape[1] // 2
                rows = pl.ds(c * half + a * half // n, (b - a) * half // n)
            for px, py in chips:
                slot = 2 * x + y if over_ici else 2 * px + py
                piece = ref.at[slot] if rows is None else ref.at[slot, rows]
                out.append(_remote(piece, piece, send, recv, base + len(out), (px, py, c) if over_ici else (x, y, 1 - c)))
        return out

    return _Rider(n_copies=3 * len(jobs), copies=copies, **_in_place(bufs))


def _gather_and_pass_rider(buf):
    half = buf.shape[1] // 2

    def pieces(outs):
        x, y, c, chips = _place()
        rows = lambda core: pl.ds(core * half, half)
        mine = outs[0].at[2 * x + y, rows(c)]
        landed = [outs[0].at[2 * px + py, rows(c)] for px, py in chips]
        theirs = [outs[0].at[2 * px + py, rows(1 - c)] for px, py in chips]
        return (x, y, c, chips), mine, landed, theirs

    def start(ins, outs, send, recv, base=0):
        (x, y, c, chips), mine, _, _ = pieces(outs)
        for j, (px, py) in enumerate(chips):
            _remote(mine, mine, send, recv, base + j, (px, py, c)).start()

    def finish(ins, outs, send, recv, base=0):
        (x, y, c, chips), mine, landed, theirs = pieces(outs)
        sibling = (x, y, 1 - c)
        passed = []
        for j, (px, py) in enumerate(chips):
            _remote(landed[j], landed[j], send, recv, base + j, (px, py, c)).wait_recv()
            passed.append(_remote(landed[j], landed[j], send, recv, base + 3 + j, sibling))
            passed[-1].start()
        for j in range(3):
            _remote(theirs[j], theirs[j], send, recv, base + 3 + j, sibling).wait_recv()
        for j, (px, py) in enumerate(chips):
            _remote(mine, mine, send, recv, base + j, (px, py, c)).wait_send()
            passed[j].wait_send()

    return _Rider(n_copies=6, start=start, finish=finish, **_in_place([buf]))


def _swap_halves_rider(parts):
    n = len(parts)

    def copies(ins, outs, send, recv, base=0):
        x, y, c, _ = _place()
        out = []
        for t in range(n):
            half = ins[t].shape[1] // 2
            out.append(_remote(ins[t].at[:, pl.ds((1 - c) * half, half)], outs[t], send, recv, base + t,
                               (x, y, 1 - c)))
        return out

    shapes = [jax.ShapeDtypeStruct((p.shape[0], p.shape[1] // 2, p.shape[2]), p.dtype) for p in parts]
    return _Rider(parts, shapes, {}, n, copies)


def _scatter_rider(sums, landed, relations):
    n = len(sums)
    kept = [t for t in range(n) if landed[t] is not None]

    def copies(ins, outs, send, recv, base=0):
        x, y, c, chips = _place()
        out = []
        for t in range(n):
            for j in relations[t]:
                px, py = chips[j]
                out.append(_remote(ins[t].at[2 * px + py], outs[t].at[j], send, recv, base + len(out), (px, py, c)))
        return out

    shapes = [jax.ShapeDtypeStruct((3,) + s.shape[1:], s.dtype) for s in sums]
    return _Rider(list(sums) + [landed[t] for t in kept], shapes, {n + k: t for k, t in enumerate(kept)},
                  sum(len(r) for r in relations), copies)


def _share_rider(blocks, pieces):
    def copies(ins, outs, send, recv, base=0):
        x, y, c, _ = _place()
        out = []
        for k, (t, l, col) in enumerate(pieces):
            ref = outs[t].at[l]
            r2 = ref.shape[0] // 2
            width = ref.shape[1] // col[1]
            piece = ref.at[pl.ds(c * r2, r2), pl.ds(col[0] * width, width)]
            out.append(_remote(piece, piece, send, recv, base + k, (x, y, 1 - c)))
        return out

    return _Rider(n_copies=len(pieces), copies=copies, **_in_place(blocks))


def _gather_small(name, v):
    def body(v_ref, out_ref, send_sem, recv_sem, local_sem):
        x, y, c, _ = _place()
        me = 4 * x + 2 * y + c
        flips = [(fx, fy, fc) for fx in (0, 1) for fy in (0, 1) for fc in (0, 1)][1:]
        local = pltpu.make_async_copy(v_ref, out_ref.at[me], local_sem)
        local.start()
        copies = []
        for j, (fx, fy, fc) in enumerate(flips):
            peer = (x ^ fx, y ^ fy, c ^ fc)
            copies.append(pltpu.make_async_remote_copy(
                src_ref=v_ref, dst_ref=out_ref.at[me], send_sem=send_sem.at[j], recv_sem=recv_sem.at[j],
                device_id=peer, device_id_type=MESH))
        for cp in copies:
            cp.start()
        for j, (fx, fy, fc) in enumerate(flips):
            peer = (x ^ fx, y ^ fy, c ^ fc)
            pltpu.make_async_remote_copy(
                src_ref=v_ref, dst_ref=out_ref.at[4 * peer[0] + 2 * peer[1] + peer[2]], send_sem=send_sem.at[j],
                recv_sem=recv_sem.at[j], device_id=peer, device_id_type=MESH).wait_recv()
        for cp in copies:
            cp.wait_send()
        local.wait()

    return _pcall(body, name=name, in_specs=[ANY], out_specs=ANY,
                  out_shape=jax.ShapeDtypeStruct((8,) + v.shape, v.dtype),
                  scratch_shapes=[pltpu.SemaphoreType.DMA((7,)), pltpu.SemaphoreType.DMA((7,)),
                                  pltpu.SemaphoreType.DMA])(v)


def _fold(a, dil):
    if dil == 1:
        return a
    S, C = a.shape
    return a.reshape(S // dil, dil, C).transpose(1, 0, 2).reshape(S, C)


def _unfold(a, dil):
    if dil == 1:
        return a
    S, C = a.shape
    return a.reshape(dil, S // dil, C).transpose(1, 0, 2).reshape(S, C)


def _fold_groups(a):
    return jnp.stack([_fold(a[g] if a.ndim == 3 else a, d) for g, d in enumerate(ATTN_DILATIONS)])


def _unfold_groups(a):
    return jnp.stack([_unfold(a[g], d) for g, d in enumerate(ATTN_DILATIONS)])


class _NoComm:
    def __init__(self, weights):
        self.weights = weights
        self.grads = {}
        self.chip = jnp.zeros((1,), jnp.int32)
        self.casts = []

    def cast_done(self, bufs):
        pass

    def w(self, name):
        return self.weights[name]

    def run(self, fn, name, *args, **kw):
        return fn(name, *args, **kw)

    def grad(self, name, value):
        self.grads[name] = value


def _local_step(xs, tgt, attn_norm, ffn_norm, final_norm, comm):
    run = comm.run
    hf, *cast = run(_norm_fold, "fold", xs, attn_norm, comm.casts, comm.chip)
    comm.cast_done(cast)
    qkv = run(_qkv_tiles, "qkv_own", None, hf, comm.w("wq_own"), None, comm.chip)
    for p in range(QKV_PIECES):
        qkv = run(_qkv_tiles, "qkv_piece%d" % p, p, hf, comm.w("wq%d" % p), qkv, comm.chip)
    o_f, lse_f = run(_attn_fwd, "attn_fwd", qkv)
    lse_t = _unfold_groups(lse_f)
    merged = run(_merge_fwd, "merge_fwd", o_f, lse_t)
    x1, h1 = run(_proj_res_norm, "attn_out", merged, comm.w("o"), xs, ffn_norm[0:1])
    gu0, act0 = run(_ffn_up, "ffn0_up", h1, comm.w("gu0"))
    x2, h2 = run(_proj_res_norm, "ffn0_down", act0, comm.w("d0"), x1, comm.w("pool_norm"))
    y, z, x3, h3 = run(_pool_fwd, "pool_fwd", h2, comm.w("p"), comm.w("pg"), comm.w("pool_scale"), x2, ffn_norm[1:2])
    gu1, act1 = run(_ffn_up, "ffn1_up", h3, comm.w("gu1"))
    loss, dx4, dx4_b, d_final = run(_proj_res_loss, "ffn1_down", act1, comm.w("d1"), x3, final_norm, tgt)

    dgu1 = run(_ffn_down_bwd, "ffn1_down_bwd", dx4_b, comm.w("d1"), gu1)
    comm.grad("d1", run(_mm_tn, "ffn1_down_dw", act1, dx4_b, FF_SHARD, 2048))
    comm.grad("gu1", run(_ffn_dw_up, "ffn1_up_dw", h3, dgu1))
    dx3, d_ffn1 = run(_ffn_dh, "ffn1_up_dh", dgu1, comm.w("gu1"), x3, ffn_norm[1:2], dx4)

    du, dw_pg, d_scale = run(_pool_bwd, "pool_bwd", dx3, z, y, comm.w("pool_scale"), comm.w("pg"))
    comm.grad("pg", dw_pg)
    comm.grad("p", run(_mm_tn, "pool_in_dw", h2, du, D_MODEL, h2.shape[0]))
    dx2, dx2_b, d_pool = run(_dh_norm_bwd, "pool_in_dh", du, comm.w("p"), x2, comm.w("pool_norm"), dx3)

    dgu0 = run(_ffn_down_bwd, "ffn0_down_bwd", dx2_b, comm.w("d0"), gu0)
    comm.grad("d0", run(_mm_tn, "ffn0_down_dw", act0, dx2_b, FF_SHARD, 2048))
    comm.grad("gu0", run(_ffn_dw_up, "ffn0_up_dw", h1, dgu0))
    dx1, d_ffn0 = run(_ffn_dh, "ffn0_up_dh", dgu0, comm.w("gu0"), x1, ffn_norm[0:1], dx2)

    comm.grad("o", run(_mm_tn, "attn_out_dw", merged, dx1, D_MODEL, 2048))
    do_f, dl_t = run(_merge_bwd, "merge_bwd", dx1, comm.w("o"), o_f, lse_t)
    dqkv = run(_attn_bwd, "attn_bwd", qkv, do_f, lse_f, _fold_groups(dl_t))
    for p in range(QKV_PIECES):
        comm.grad("qkv%d" % p, run(_qkv_dw, "qkv_dw%d" % p, p, hf, dqkv))
    dhf = None
    for g in range(N_GROUPS):
        dhf = run(_qkv_dh, "qkv_dh%d" % g, g, dqkv, [comm.w("wq%d" % p) for p in range(QKV_PIECES)], dhf)
    grad_x, d_attn = run(_rms_bwd_folded, "norm_attn_bwd", xs, attn_norm, dhf, dx1)

    small = dict(attn=d_attn, ffn0=d_ffn0, ffn1=d_ffn1, final=d_final, pool=d_pool, scale=d_scale)
    return loss, grad_x, small


TENSORS = ("o", "p", "pg", "gu0", "gu1", "d0", "d1")


def _block_of(name):
    if name[:-1] in ("gu", "d"):
        return name[:-1], int(name[-1]), (0, 1)
    if name[:-1] == "qkv":
        return "qkv", 0, (int(name[-1]), QKV_PIECES)
    return name, 0, (0, 1)


class _MeshComm:
    def __init__(self, bufs, casts, shapes, chip_arr, core_arr, pg_rows):
        self.bufs = dict(bufs)
        self.cast_names = list(casts)
        self.casts = list(casts.values())
        self.shapes = shapes
        self.chip, self.chip_arr, self.core_arr, self.pg_rows = chip_arr, chip_arr, core_arr, pg_rows
        self.parts, self.sums, self.blocks, self.landed, self.arrived = {}, {}, {}, {}, {}
        move = lambda ici=(), d2d=(): lambda: self.gather(ici, d2d)
        whole = lambda name: lambda: self.gather_and_pass(name)
        swap = lambda *names: lambda: self.swap(names)
        scatter = lambda *names: lambda: self.scatter(names)
        share = lambda *names: lambda: self.share(names)
        self.plan = {
            "fold": [whole("wq0")],
            "qkv_own": [whole("wq1")],
            "qkv_piece0": [whole("wq2")],
            "qkv_piece1": [move(ici=["o", "gu0[0:1/4]"])],
            "qkv_piece2": [move(ici=["gu0[1:2/4]"], d2d=["o", "gu0[0:1/4]"])],
            "attn_fwd": [move(ici=["gu0[2:4/4]", "d0[0:1/2]"], d2d=["gu0[1:2/4]"])],
            "merge_fwd": [move(ici=["p", "pg"], d2d=["gu0[2:4/4]", "d0[0:1/2]"])],
            "attn_out": [move(ici=["d0[1:2/2]", "pool_norm", "pool_scale"], d2d=["p", "pg"])],
            "ffn0_up": [move(ici=["gu1[0:3/4]"], d2d=["d0[1:2/2]"])],
            "ffn0_down": [move(ici=["gu1[3:4/4]"], d2d=["gu1[0:3/4]"])],
            "pool_fwd": [move(ici=["d1"], d2d=["gu1[3:4/4]"])],
            "ffn1_up": [move(d2d=["d1"])],
            "ffn1_up_dh": [swap("d1", "gu1")],
            "pool_bwd": [scatter("d1{01}")],
            "pool_in_dh": [scatter("d1{2}")],
            "ffn0_down_bwd": [scatter("gu1{01}")],
            "ffn0_down_dw": [scatter("gu1{2}")],
            "ffn0_up_dw": [share("gu1", "d1")],
            "ffn0_up_dh": [swap("pg", "p", "d0", "gu0")],
            "merge_bwd": [swap("o")],
            "attn_bwd": [scatter("gu0", "d0", "o")],
            "qkv_dw0": [share("d0"), scatter("p", "pg")],
            "qkv_dw1": [share("gu0", "o"), swap("qkv0")],
            "qkv_dw2": [share("p", "pg"), scatter("qkv0"), swap("qkv1")],
            "qkv_dh0": [share("qkv0"), scatter("qkv1"), swap("qkv2")],
            "qkv_dh1": [share("qkv1"), scatter("qkv2")],
            "qkv_dh2": [share("qkv2")],
        }

    def cast_done(self, bufs):
        self.bufs.update(zip(self.cast_names, bufs))

    def gather_and_pass(self, name):
        return _gather_and_pass_rider(self.bufs[name]), lambda res: self.bufs.update({name: res[0]})

    def gather(self, ici, d2d):
        names, jobs = [], []
        for over_ici, specs in ((True, ici), (False, d2d)):
            for spec in specs:
                name, _, rows = spec.partition("[")
                if name not in names:
                    names.append(name)
                rng = None
                if name in TENSORS:
                    ab, _, n = (rows[:-1] or "0:1/1").partition("/")
                    rng = (int(ab.split(":")[0]), int(ab.split(":")[1]), int(n))
                jobs.append((names.index(name), over_ici, rng))
        return _gather_rider([self.bufs[n] for n in names], jobs), lambda res: self.bufs.update(zip(names, res))

    def swap(self, names):
        def done(res):
            sums = _add_my_half("chip_sum_" + "_".join(names), [self.parts[n] for n in names], res, self.core_arr)
            self.sums.update(zip(names, sums))
        return _swap_halves_rider([self.parts[n] for n in names]), done

    def scatter(self, specs):
        names = [s.partition("{")[0] for s in specs]
        relations = [tuple(int(ch) for ch in s.partition("{")[2][:-1]) or (0, 1, 2) for s in specs]

        def done(res):
            for n, rel, landed in zip(names, relations, res):
                self.landed[n] = landed
                self.arrived[n] = self.arrived.get(n, ()) + rel
                if len(self.arrived[n]) < 3:
                    continue
                key, layer, col = _block_of(n)
                self.blocks[key] = _owner_sum("owner_sum_" + n, self.sums[n], landed, self.shapes[key],
                                              self.chip_arr, self.core_arr, self.blocks.get(key), layer, col)
        rider = _scatter_rider([self.sums[n] for n in names], [self.landed.get(n) for n in names], relations)
        return rider, done

    def share(self, names):
        keys, pieces = [], []
        for n in names:
            key, layer, col = _block_of(n)
            if key not in keys:
                keys.append(key)
            pieces.append((keys.index(key), layer, col))
        return _share_rider([self.blocks[k] for k in keys], pieces), lambda res: self.blocks.update(zip(keys, res))

    def run(self, fn, name, *args, **kw):
        made = [make() for make in self.plan.get(name, [])]
        if not made:
            return fn(name, *args, **kw)
        riders = [r for r, _ in made]
        out = _ride(riders[0] if len(riders) == 1 else _riders(*riders), fn, name, *args, **kw)
        for r, done in made:
            done(r.results)
        return out

    def w(self, name):
        b = self.bufs[name]
        if name in ("o", "p", "d0", "d1"):
            return b.reshape(-1, b.shape[-1])
        if name == "pg":
            b = b.reshape(N_CHIPS, 4, self.pg_rows, POOL_DIM).transpose(1, 0, 2, 3)
            return b.reshape(4, POOL_DIM, POOL_DIM)
        if name in ("pool_norm", "pool_scale"):
            return b.reshape(1, -1)
        return b

    def grad(self, name, value):
        if name == "pg":
            value = value.reshape(4, N_CHIPS, self.pg_rows, POOL_DIM).transpose(1, 0, 2, 3)
            value = value.reshape(N_CHIPS, 4 * self.pg_rows, POOL_DIM).astype(BF16)
        elif name in ("o", "p", "d0", "d1"):
            value = value.reshape(N_CHIPS, value.shape[0] // N_CHIPS, value.shape[1])
        self.parts[name] = value


def kernel(x, attn_norm, w_qkv, w_attn_out, pool_norm, w_pool_in, w_pool_group, pool_scale, ffn_norm, w_ffn_gate_up, w_ffn_down, final_norm, loss_target, m_attn_norm, m_w_qkv, m_w_attn_out, m_pool_norm, m_w_pool_in, m_w_pool_group, m_pool_scale, m_ffn_norm, m_w_ffn_gate_up, m_w_ffn_down, m_final_norm, v_attn_norm, v_w_qkv, v_w_attn_out, v_pool_norm, v_w_pool_in, v_w_pool_group, v_pool_scale, v_ffn_norm, v_w_ffn_gate_up, v_w_ffn_down, v_final_norm):
    D = D_MODEL
    chip = 2 * lax.axis_index("x") + lax.axis_index("y")
    core = lax.axis_index("c")
    pg_rows = w_pool_group.shape[2]

    chip_arr = chip.astype(jnp.int32).reshape(1)
    core_arr = core.astype(jnp.int32).reshape(1)

    pieces = _cast_qkv("cast_qkv", w_qkv, chip_arr)
    bufs = dict(zip(["wq_own"] + ["wq%d" % p for p in range(QKV_PIECES)], pieces))
    casts = dict(o=(w_attn_out, 0, BF16), p=(w_pool_in, 0, BF16),
                 pg=(w_pool_group.reshape(1, 4 * pg_rows, POOL_DIM), 0, BF16),
                 gu0=(w_ffn_gate_up, 0, BF16), gu1=(w_ffn_gate_up, 1, BF16), d0=(w_ffn_down, 0, BF16),
                 d1=(w_ffn_down, 1, BF16), pool_norm=(pool_norm[None], 0, F32), pool_scale=(pool_scale[None], 0, F32))

    as_block = lambda a: a.reshape((-1,) + a.shape[-2:]) if a.ndim == 3 else a.reshape(1, -1, a.shape[-1])
    owned = dict(qkv=(w_qkv, m_w_qkv, v_w_qkv), o=(w_attn_out, m_w_attn_out, v_w_attn_out),
                 p=(w_pool_in, m_w_pool_in, v_w_pool_in), pg=(w_pool_group, m_w_pool_group, v_w_pool_group),
                 gu=(w_ffn_gate_up, m_w_ffn_gate_up, v_w_ffn_gate_up), d=(w_ffn_down, m_w_ffn_down, v_w_ffn_down))
    shapes = {k: as_block(wmv[0]).shape for k, wmv in owned.items()}
    comm = _MeshComm(bufs, casts, shapes, chip_arr, core_arr, pg_rows)
    loss_part, grad_x, small = _local_step(x[0], loss_target[0], attn_norm, ffn_norm, final_norm.reshape(1, D), comm)

    rows = jnp.concatenate([small["attn"], small["ffn0"], small["ffn1"], small["final"], small["pool"],
                            small["scale"], jnp.pad(loss_part, ((0, 0), (0, D - 1))), jnp.zeros((1, D), F32)], axis=0)
    all_rows = _gather_small("gather_gain_grads", rows)
    tot = _sum_leading("sum_gain_grads", all_rows, F32)
    loss = tot[6, 0]
    g_attn_norm = tot[0:1]
    g_ffn_norm = tot[1:3]
    g_final_norm = tot[3]
    g_pool_norm = lax.dynamic_slice(tot, (4, chip * POOL_DIM), (1, POOL_DIM))
    g_pool_scale = lax.dynamic_slice(tot, (5, chip * POOL_DIM), (1, POOL_DIM))

    as_rows = lambda a: a.reshape(-1, a.shape[-1])
    res = _adamw("adamw_weights", [(as_rows(w), as_rows(comm.blocks[k]), as_rows(m), as_rows(v))
                                   for k, (w, m, v) in owned.items()])
    updated = {k: [a.reshape(owned[k][0].shape) for a in r] for k, r in zip(owned, res)}
    gains = [(attn_norm, g_attn_norm, m_attn_norm, v_attn_norm), (pool_norm, g_pool_norm, m_pool_norm, v_pool_norm),
             (pool_scale, g_pool_scale, m_pool_scale, v_pool_scale), (ffn_norm, g_ffn_norm, m_ffn_norm, v_ffn_norm),
             (final_norm, g_final_norm, m_final_norm, v_final_norm)]
    small_res = _adamw_small("adamw_gains", [tuple(as_rows(a) for a in item) for item in gains])
    small_res = [[a.reshape(item[0].shape) for a in r] for r, item in zip(small_res, gains)]
    results = [small_res[0], updated["qkv"], updated["o"], small_res[1], updated["p"], updated["pg"], small_res[2],
               small_res[3], updated["gu"], updated["d"], small_res[4]]
    grads = [r[0] for r in results]
    deltas = [r[1] for r in results]
    new_m = [r[2] for r in results]
    new_v = [r[3] for r in results]
    return (loss, grad_x[None], *grads, *deltas, *new_m, *new_v)
```

```python
import jax
import jax.numpy as jnp
from jax import lax
from jax.experimental import pallas as pl
from jax.experimental.pallas import tpu as pltpu

F32 = jnp.float32
BF16 = jnp.bfloat16
MESH = pl.DeviceIdType.MESH

D_MODEL = 1024
N_HEADS = 8
HEAD_DIM = 128
Q_BLOCK = 128
ATTN_DILATIONS = (1, 4, 16)
N_GROUPS = 3
D_FF = 2816
N_CHIPS = 4
FF_SHARD = 2 * D_FF // N_CHIPS
QKV_SHARD = 3 * 3 * D_MODEL // N_CHIPS
QKV_TILE = 768
POOL_DIM = 256
POOL_HALO = 16
RMS_EPS = 1e-6
NEG = -1e30
ADAM_LR = 0.001
ADAM_B1 = 0.9
ADAM_B2 = 0.999
ADAM_EPS = 1e-08
ADAM_WD = 0.01
ADAM_STEP = 10
VMEM_LIMIT = 56 * 1024 * 1024

_DN = {
    "nn": (((1,), (0,)), ((), ())),
    "nt": (((1,), (1,)), ((), ())),
    "tn": (((0,), (0,)), ((), ())),
}


class _Rider:
    def __init__(self, operands, out_shapes, aliases, n_copies, copies=None, start=None, finish=None):
        self.operands = list(operands)
        self.out_shapes = list(out_shapes)
        self.aliases = dict(aliases)
        self.n_copies = n_copies
        self.results = None

        def start_copies(ins, outs, send, recv, base=0):
            for cp in copies(ins, outs, send, recv, base):
                cp.start()

        def wait_copies(ins, outs, send, recv, base=0):
            for cp in copies(ins, outs, send, recv, base):
                cp.wait()

        self.start = start or start_copies
        self.finish = finish or wait_copies


def _riders(*riders):
    operands, out_shapes, aliases, offsets = [], [], {}, []
    n = 0
    for r in riders:
        offsets.append((len(operands), len(out_shapes), n))
        aliases.update({len(operands) + i: len(out_shapes) + o for i, o in r.aliases.items()})
        operands += r.operands
        out_shapes += r.out_shapes
        n += r.n_copies

    def each(which):
        def go(ins, outs, send, recv, base=0):
            for r, (i0, o0, s0) in zip(riders, offsets):
                getattr(r, which)(ins[i0:i0 + len(r.operands)], outs[o0:o0 + len(r.out_shapes)], send, recv,
                                  base + s0)
        return go

    both = _Rider(operands, out_shapes, aliases, n, start=each("start"), finish=each("finish"))
    both.parts = (riders, offsets)
    return both


_pending_rider = []


def _ride(rider, fn, *args, **kw):
    _pending_rider.append(rider)
    out = fn(*args, **kw)
    assert not _pending_rider
    if hasattr(rider, "parts"):
        for r, (_, o0, _) in zip(*rider.parts):
            r.results = rider.results[o0:o0 + len(r.out_shapes)]
    return out


def _pcall(body, prefetch=0, **kw):
    def build(body, kw):
        if not prefetch:
            return pl.pallas_call(body, **kw)
        kw = dict(kw)
        grid_spec = pltpu.PrefetchScalarGridSpec(
            num_scalar_prefetch=prefetch, grid=kw.pop("grid"), in_specs=kw.pop("in_specs"),
            out_specs=kw.pop("out_specs"), scratch_shapes=kw.pop("scratch_shapes", []))
        return pl.pallas_call(body, grid_spec=grid_spec, **kw)

    if not _pending_rider:
        return build(body, kw)
    rider = _pending_rider.pop()
    grid = kw.get("grid", ())
    in_specs = list(kw.get("in_specs", []))
    single = not isinstance(kw["out_shape"], (list, tuple))
    out_shape = [kw["out_shape"]] if single else list(kw["out_shape"])
    out_specs = [kw["out_specs"]] if single else list(kw["out_specs"])
    scratch = list(kw.get("scratch_shapes", []))
    n_in, n_out, n_scr = len(in_specs), len(out_shape), len(scratch)
    r_in, r_out = len(rider.operands), len(rider.out_shapes)

    def wrapped(*refs):
        scalars, refs = refs[:prefetch], refs[prefetch:]
        ins, rins = refs[:n_in], refs[n_in:n_in + r_in]
        outs = refs[n_in + r_in:n_in + r_in + n_out]
        routs = refs[n_in + r_in + n_out:n_in + r_in + n_out + r_out]
        scr = refs[n_in + r_in + n_out + r_out:n_in + r_in + n_out + r_out + n_scr]
        send, recv = refs[-2:]
        ids = [pl.program_id(a) for a in range(len(grid))]
        first, last = True, True
        for a, pid in enumerate(ids):
            first = jnp.logical_and(first, pid == 0)
            last = jnp.logical_and(last, pid == grid[a] - 1)
        if grid:
            pl.when(first)(lambda: rider.start(rins, routs, send, recv))
        else:
            rider.start(rins, routs, send, recv)
        body(*scalars, *ins, *outs, *scr)
        if grid:
            pl.when(last)(lambda: rider.finish(rins, routs, send, recv))
        else:
            rider.finish(rins, routs, send, recv)

    any_spec = pl.BlockSpec(memory_space=pl.ANY)
    kw2 = dict(kw)
    kw2.update(
        in_specs=in_specs + [any_spec] * r_in, out_specs=out_specs + [any_spec] * r_out,
        out_shape=out_shape + rider.out_shapes,
        scratch_shapes=scratch + [pltpu.SemaphoreType.DMA((rider.n_copies,)),
                                  pltpu.SemaphoreType.DMA((rider.n_copies,))],
        input_output_aliases={**kw.get("input_output_aliases", {}),
                              **{prefetch + n_in + i: n_out + o for i, o in rider.aliases.items()}})
    if grid:
        kw2["compiler_params"] = _params(("arbitrary",) * len(grid))
    call = build(wrapped, kw2)

    def run(*operands):
        res = call(*operands, *rider.operands)
        rider.results = list(res[n_out:])
        return res[0] if single else list(res[:n_out])

    return run


def _params(sem):
    return pltpu.CompilerParams(dimension_semantics=sem, vmem_limit_bytes=VMEM_LIMIT)


def _dot(a, b, mode):
    return lax.dot_general(a, b, _DN[mode], preferred_element_type=F32)


def _mm(name, mode, grid, a, a_spec, b, b_spec, out_shape, o_spec, acc_shape):
    nk = grid[-1]

    def body(a_ref, b_ref, o_ref, *acc):
        part = _dot(a_ref[...].astype(BF16), b_ref[...].astype(BF16), mode)
        if nk == 1:
            o_ref[...] = part.astype(o_ref.dtype)
            return
        acc_ref, = acc
        k = pl.program_id(len(grid) - 1)

        @pl.when(k == 0)
        def _():
            acc_ref[...] = part

        @pl.when(k > 0)
        def _():
            acc_ref[...] += part

        @pl.when(k == nk - 1)
        def _():
            o_ref[...] = acc_ref[...].astype(o_ref.dtype)

    scratch = [] if nk == 1 else [pltpu.VMEM(acc_shape, F32)]
    sem = ("parallel",) * (len(grid) - 1) + ("arbitrary",)
    return _pcall(body, name=name, grid=grid, in_specs=[a_spec, b_spec], out_specs=o_spec, out_shape=out_shape,
                  scratch_shapes=scratch, compiler_params=_params(sem))(a, b)


def _mm_tn(name, a, b, tm, tk):
    T, M = a.shape
    N = b.shape[1]
    return _mm(name, "tn", (M // tm, T // tk), a, pl.BlockSpec((tk, tm), lambda i, k: (k, i)),
               b, pl.BlockSpec((tk, N), lambda i, k: (k, 0)),
               jax.ShapeDtypeStruct((M, N), BF16), pl.BlockSpec((tm, N), lambda i, k: (i, 0)), (tm, N))


def _norm_bwd_rows(xf, gain, dh, dres):
    r = lax.rsqrt(jnp.mean(xf * xf, axis=-1, keepdims=True) + RMS_EPS)
    xh = xf * r
    t = dh * gain
    dx = dres + r * (t - xh * jnp.mean(t * xh, axis=-1, keepdims=True))
    return dx, jnp.sum(dh * xh, axis=0, keepdims=True)


def _accumulate(ref, value, first):
    @pl.when(first)
    def _():
        ref[...] = value

    @pl.when(jnp.logical_not(first))
    def _():
        ref[...] += value


def _rms_bwd_folded(name, x, g, dhf, dres, tb=512):
    S, D = x.shape

    def body(x_ref, g_ref, d0_ref, d1_ref, d2_ref, dres_ref, dx_ref, dg_ref, dh_ref):
        chunks = _lane_chunks(D)
        for c, cols in enumerate(chunks):
            dh_ref[c] = d0_ref[0, :, cols].astype(F32)
        for dil, ref in ((ATTN_DILATIONS[1], d1_ref), (ATTN_DILATIONS[2], d2_ref)):
            n = tb // dil
            for k in range(dil):
                for c, cols in enumerate(chunks):
                    dh_ref[c, _strided_rows(k, n, dil), :] += ref[k, :, cols].astype(F32)
        dh = jnp.concatenate([dh_ref[c] for c in range(len(chunks))], axis=1)
        dx, dg = _norm_bwd_rows(x_ref[...], g_ref[...], dh, dres_ref[...])
        dx_ref[...] = dx
        _accumulate(dg_ref, dg, pl.program_id(0) == 0)

    views, specs = _folded_views(dhf, tb)
    row = pl.BlockSpec((tb, D), lambda i: (i, 0))
    vec = pl.BlockSpec((1, D), lambda i: (0, 0))
    return _pcall(body, name=name, grid=(S // tb,), in_specs=[row, vec] + specs + [row], out_specs=[row, vec],
                  out_shape=[jax.ShapeDtypeStruct((S, D), F32), jax.ShapeDtypeStruct((1, D), F32)],
                  scratch_shapes=[pltpu.VMEM((D // LANES, tb, LANES), F32)],
                  compiler_params=_params(("arbitrary",)))(x, g, *views, dres)


def _proj_res_norm(name, a, w, res, gain, tm=512):
    M, K = a.shape
    D = w.shape[1]

    def body(a_ref, w_ref, res_ref, g_ref, x_ref, h_ref):
        xf = res_ref[...] + _dot(a_ref[...], w_ref[...], "nn")
        x_ref[...] = xf
        r = lax.rsqrt(jnp.mean(xf * xf, axis=-1, keepdims=True) + RMS_EPS)
        h_ref[...] = ((xf * r) * g_ref[...]).astype(h_ref.dtype)

    row = pl.BlockSpec((tm, D), lambda i: (i, 0))
    return _pcall(body, name=name, grid=(M // tm,),
                  in_specs=[pl.BlockSpec((tm, K), lambda i: (i, 0)), pl.BlockSpec((K, D), lambda i: (0, 0)), row,
                            pl.BlockSpec((1, D), lambda i: (0, 0))],
                  out_specs=[row, row],
                  out_shape=[jax.ShapeDtypeStruct((M, D), F32), jax.ShapeDtypeStruct((M, D), BF16)],
                  compiler_params=_params(("parallel",)))(a, w, res, gain)


def _proj_res_loss(name, a, w, res, gain, tgt, tm=512):
    M, K = a.shape
    D = w.shape[1]

    def body(a_ref, w_ref, res_ref, g_ref, t_ref, loss_ref, dx_ref, dxb_ref, dg_ref):
        first = pl.program_id(0) == 0
        xf = res_ref[...] + _dot(a_ref[...], w_ref[...], "nn")
        r = lax.rsqrt(jnp.mean(xf * xf, axis=-1, keepdims=True) + RMS_EPS)
        xh = xf * r
        gv = g_ref[...]
        e = xh * gv - t_ref[...]
        lp = 0.5 * jnp.sum(jnp.mean(e * e, axis=-1, keepdims=True), axis=0, keepdims=True)
        dy = e * (1.0 / D)
        t = dy * gv
        dx = r * (t - xh * jnp.mean(t * xh, axis=-1, keepdims=True))
        dx_ref[...] = dx
        dxb_ref[...] = dx.astype(dxb_ref.dtype)
        _accumulate(dg_ref, jnp.sum(dy * xh, axis=0, keepdims=True), first)
        _accumulate(loss_ref, lp, first)

    row = pl.BlockSpec((tm, D), lambda i: (i, 0))
    vec = pl.BlockSpec((1, D), lambda i: (0, 0))
    return _pcall(body, name=name, grid=(M // tm,),
                  in_specs=[pl.BlockSpec((tm, K), lambda i: (i, 0)), pl.BlockSpec((K, D), lambda i: (0, 0)), row,
                            vec, row],
                  out_specs=[pl.BlockSpec((1, 1), lambda i: (0, 0)), row, row, vec],
                  out_shape=[jax.ShapeDtypeStruct((1, 1), F32), jax.ShapeDtypeStruct((M, D), F32),
                             jax.ShapeDtypeStruct((M, D), BF16), jax.ShapeDtypeStruct((1, D), F32)],
                  compiler_params=_params(("arbitrary",)))(a, w, res, gain, tgt)


def _dh_norm_bwd(name, d, w, x, gain, dres, tm=512):
    M, N = d.shape
    D = w.shape[0]

    def body(d_ref, w_ref, x_ref, g_ref, dres_ref, dx_ref, dxb_ref, dg_ref):
        dh = _dot(d_ref[...].astype(BF16), w_ref[...], "nt")
        dx, dg = _norm_bwd_rows(x_ref[...], g_ref[...], dh, dres_ref[...])
        dx_ref[...] = dx
        dxb_ref[...] = dx.astype(dxb_ref.dtype)
        _accumulate(dg_ref, dg, pl.program_id(0) == 0)

    row = pl.BlockSpec((tm, D), lambda i: (i, 0))
    vec = pl.BlockSpec((1, D), lambda i: (0, 0))
    return _pcall(body, name=name, grid=(M // tm,),
                  in_specs=[pl.BlockSpec((tm, N), lambda i: (i, 0)), pl.BlockSpec((D, N), lambda i: (0, 0)), row, vec,
                            row],
                  out_specs=[row, row, vec],
                  out_shape=[jax.ShapeDtypeStruct((M, D), F32), jax.ShapeDtypeStruct((M, D), BF16),
                             jax.ShapeDtypeStruct((1, D), F32)],
                  compiler_params=_params(("arbitrary",)))(d, w, x, gain, dres)


def _sigmoid(v):
    return 0.5 * jnp.tanh(0.5 * v) + 0.5


def _ffn_up(name, h, w_gu, tm=1024):
    S, D = h.shape
    W = FF_SHARD

    def body(h_ref, wg_ref, wu_ref, gu_ref, act_ref):
        hv = h_ref[...]
        gate = _dot(hv, wg_ref[...], "nn")
        up = _dot(hv, wu_ref[...], "nn")
        gu_ref[0] = gate.astype(gu_ref.dtype)
        gu_ref[1] = up.astype(gu_ref.dtype)
        sig = _sigmoid(gate)
        act_ref[...] = ((gate * sig) * up).astype(act_ref.dtype)

    return _pcall(
        body, name=name, grid=(2, S // tm),
        in_specs=[pl.BlockSpec((tm, D), lambda j, i: (i, 0)),
                  pl.BlockSpec((None, D, W), lambda j, i: (j, 0, 0)),
                  pl.BlockSpec((None, D, W), lambda j, i: (j + 2, 0, 0))],
        out_specs=[pl.BlockSpec((2, tm, W), lambda j, i: (0, i, j)), pl.BlockSpec((tm, W), lambda j, i: (i, j))],
        out_shape=[jax.ShapeDtypeStruct((2, S, D_FF), BF16), jax.ShapeDtypeStruct((S, D_FF), BF16)],
        compiler_params=_params(("parallel", "parallel")))(h, w_gu, w_gu)


def _ffn_down_bwd(name, dx, w_down, gu, tm=1024):
    S, D = dx.shape
    W = FF_SHARD

    def body(dx_ref, w_ref, gu_ref, dgu_ref):
        dact = _dot(dx_ref[...].astype(BF16), w_ref[...], "nt")
        gate = gu_ref[0].astype(F32)
        up = gu_ref[1].astype(F32)
        sig = _sigmoid(gate)
        silu = gate * sig
        dgu_ref[0] = (dact * up * (sig * (1.0 + gate * (1.0 - sig)))).astype(dgu_ref.dtype)
        dgu_ref[1] = (dact * silu).astype(dgu_ref.dtype)

    blk = pl.BlockSpec((2, tm, W), lambda j, i: (0, i, j))
    return _pcall(
        body, name=name, grid=(2, S // tm),
        in_specs=[pl.BlockSpec((tm, D), lambda j, i: (i, 0)), pl.BlockSpec((W, D), lambda j, i: (j, 0)), blk],
        out_specs=blk, out_shape=jax.ShapeDtypeStruct((2, S, D_FF), BF16),
        compiler_params=_params(("parallel", "parallel")))(dx, w_down, gu)


def _ffn_dw_up(name, h, dgu, tk=2048):
    S, D = h.shape
    W = FF_SHARD
    tk = min(tk, S)
    return _mm(name, "tn", (N_CHIPS, S // tk), h, pl.BlockSpec((tk, D), lambda s, k: (k, 0)),
               dgu, pl.BlockSpec((None, tk, W), lambda s, k: (s // 2, k, s % 2)),
               jax.ShapeDtypeStruct((N_CHIPS, D, W), BF16), pl.BlockSpec((None, D, W), lambda s, k: (s, 0, 0)),
               (D, W))


def _ffn_dh(name, dgu, w_gu, x, gain, dres, tm=512):
    S = dgu.shape[1]
    W = FF_SHARD

    def body(a_ref, w_hbm, x_ref, g_ref, dres_ref, dx_ref, dg_ref, w_ref, acat_ref, sems):
        first = pl.program_id(0) == 0

        copies = [pltpu.make_async_copy(w_hbm.at[s], w_ref.at[:, pl.ds(s * W, W)], sems.at[s])
                  for s in range(N_CHIPS)]

        @pl.when(first)
        def _():
            for cp in copies:
                cp.start()

        acat_ref[:, :D_FF] = a_ref[0]
        acat_ref[:, D_FF:] = a_ref[1]

        def as_they_land():
            acc = None
            for s, cp in enumerate(copies):
                cp.wait()
                part = _dot(acat_ref[:, s * W:(s + 1) * W], w_ref[:, s * W:(s + 1) * W], "nt")
                acc = part if acc is None else acc + part
            return acc

        dh = lax.cond(first, as_they_land, lambda: _dot(acat_ref[...], w_ref[...], "nt"))
        dx, dg = _norm_bwd_rows(x_ref[...], g_ref[...], dh, dres_ref[...])
        dx_ref[...] = dx
        _accumulate(dg_ref, dg, first)

    row = pl.BlockSpec((tm, D_MODEL), lambda i: (i, 0))
    vec = pl.BlockSpec((1, D_MODEL), lambda i: (0, 0))
    return _pcall(body, name=name, grid=(S // tm,),
                  in_specs=[pl.BlockSpec((2, tm, D_FF), lambda i: (0, i, 0)), pl.BlockSpec(memory_space=pl.ANY),
                            row, vec, row],
                  out_specs=[row, vec],
                  out_shape=[jax.ShapeDtypeStruct((S, D_MODEL), F32), jax.ShapeDtypeStruct((1, D_MODEL), F32)],
                  scratch_shapes=[pltpu.VMEM((D_MODEL, 2 * D_FF), BF16), pltpu.VMEM((tm, 2 * D_FF), BF16),
                                  pltpu.SemaphoreType.DMA((N_CHIPS,))],
                  compiler_params=_params(("arbitrary",)))(dgu, w_gu, x, gain, dres)


FOLD_TOKENS = 1024
LANES = 128


def _lane_chunks(width):
    return [slice(c * LANES, (c + 1) * LANES) for c in range(width // LANES)]


def _strided_rows(r, n, dil):
    return pl.ds(r, n) if dil == 1 else pl.ds(r, n, stride=dil)


def _norm_fold(name, x, gain):
    S, D = x.shape
    chunks = _lane_chunks(D)

    def body(x_ref, g_ref, hf_hbm, xc_ref, hs_ref, sems):
        i = pl.program_id(0)
        xf = x_ref[...]
        inv = lax.rsqrt(jnp.mean(xf * xf, axis=-1, keepdims=True) + RMS_EPS)
        h = (xf * inv) * g_ref[...]
        hs_ref[0] = h.astype(BF16)
        for c, cols in enumerate(chunks):
            xc_ref[c] = h[:, cols]
        copies = []
        for g, dil in enumerate(ATTN_DILATIONS):
            n = FOLD_TOKENS // dil
            for r in range(dil):
                if dil > 1:
                    for c, cols in enumerate(chunks):
                        hs_ref[g, r * n:(r + 1) * n, cols] = xc_ref[c, _strided_rows(r, n, dil), :].astype(BF16)
                cp = pltpu.make_async_copy(hs_ref.at[g, pl.ds(r * n, n)],
                                           hf_hbm.at[g, pl.ds(r * (S // dil) + i * n, n)], sems.at[len(copies)])
                cp.start()
                copies.append(cp)
        for cp in copies:
            cp.wait()

    return _pcall(body, name=name, grid=(S // FOLD_TOKENS,),
                  in_specs=[pl.BlockSpec((FOLD_TOKENS, D), lambda i: (i, 0)), pl.BlockSpec((1, D), lambda i: (0, 0))],
                  out_specs=pl.BlockSpec(memory_space=pl.ANY),
                  out_shape=jax.ShapeDtypeStruct((N_GROUPS, S, D), BF16),
                  scratch_shapes=[pltpu.VMEM((D // LANES, FOLD_TOKENS, LANES), F32),
                                  pltpu.VMEM((N_GROUPS, FOLD_TOKENS, D), BF16),
                                  pltpu.SemaphoreType.DMA((sum(ATTN_DILATIONS),))],
                  compiler_params=_params(("arbitrary",)))(x, gain)


def _qkv_tiles(name, piece, hf, w, qkv, chip, tm=1024):
    S = hf.shape[1]
    per_group = 3 * D_MODEL // QKV_TILE

    def tile(q, c_ref):
        if piece is None:
            return QKV_PIECES * c_ref[0] + q
        return QKV_PIECES * ((c_ref[0] + 1 + q) % N_CHIPS) + piece

    def body(*refs):
        a_ref, w_ref, o_ref = refs[1], refs[2], refs[-1]
        o_ref[...] = _dot(a_ref[...], w_ref[...], "nn").astype(o_ref.dtype)

    if piece is None:
        w_spec = pl.BlockSpec((D_MODEL, QKV_TILE), lambda q, i, c_ref: (0, q))
    else:
        w_spec = pl.BlockSpec((None, D_MODEL, QKV_TILE), lambda q, i, c_ref: ((c_ref[0] + 1 + q) % N_CHIPS, 0, 0))
    in_specs = [pl.BlockSpec((None, tm, D_MODEL), lambda q, i, c_ref: (tile(q, c_ref) // per_group, i, 0)), w_spec]
    operands = [chip, hf, w]
    aliases = {}
    if qkv is not None:
        in_specs.append(pl.BlockSpec(memory_space=pl.ANY))
        operands.append(qkv)
        aliases = {3: 0}
    return _pcall(
        body, prefetch=1, name=name, grid=(N_CHIPS - 1, S // tm), in_specs=in_specs,
        out_specs=pl.BlockSpec((None, tm, QKV_TILE),
                               lambda q, i, c_ref: (tile(q, c_ref) // per_group, i, tile(q, c_ref) % per_group)),
        out_shape=jax.ShapeDtypeStruct((N_GROUPS, S, 3 * D_MODEL), BF16), input_output_aliases=aliases,
        compiler_params=_params(("arbitrary", "arbitrary")))(*operands)


QKV_PIECES = QKV_SHARD // QKV_TILE


def _qkv_dw(name, p, hf, dqkv):
    S = hf.shape[1]
    per_group = 3 * D_MODEL // QKV_TILE
    tile = lambda s: QKV_PIECES * s + p
    return _mm(name, "tn", (N_CHIPS, 1),
               hf, pl.BlockSpec((None, S, D_MODEL), lambda s, k: (tile(s) // per_group, 0, 0)),
               dqkv, pl.BlockSpec((None, S, QKV_TILE), lambda s, k: (tile(s) // per_group, 0, tile(s) % per_group)),
               jax.ShapeDtypeStruct((N_CHIPS, D_MODEL, QKV_TILE), BF16),
               pl.BlockSpec((None, D_MODEL, QKV_TILE), lambda s, k: (s, 0, 0)), None)


def _qkv_dh(name, g, dqkv, w_pieces, dhf, tm=1024):
    S = dqkv.shape[1]
    per_group = 3 * D_MODEL // QKV_TILE

    def body(*refs):
        a_ref, w_hbm = refs[0], refs[1:1 + QKV_PIECES]
        o_ref, w_ref, sems = refs[-3:]

        first = pl.program_id(0) == 0
        copies = []
        for j in range(per_group):
            t = per_group * g + j
            copies.append(pltpu.make_async_copy(w_hbm[t % QKV_PIECES].at[t // QKV_PIECES],
                                                w_ref.at[:, pl.ds(j * QKV_TILE, QKV_TILE)], sems.at[j]))

        @pl.when(first)
        def _():
            for cp in copies:
                cp.start()

        def as_they_land():
            acc = None
            for j, cp in enumerate(copies):
                cp.wait()
                cols = slice(j * QKV_TILE, (j + 1) * QKV_TILE)
                part = _dot(a_ref[:, cols], w_ref[:, cols], "nt")
                acc = part if acc is None else acc + part
            return acc

        dh = lax.cond(first, as_they_land, lambda: _dot(a_ref[...], w_ref[...], "nt"))
        o_ref[...] = dh.astype(o_ref.dtype)

    any_spec = pl.BlockSpec(memory_space=pl.ANY)
    in_specs = [pl.BlockSpec((None, tm, 3 * D_MODEL), lambda i: (g, i, 0))] + [any_spec] * QKV_PIECES
    operands = [dqkv] + list(w_pieces)
    aliases = {}
    if dhf is not None:
        in_specs.append(any_spec)
        operands.append(dhf)
        aliases = {1 + QKV_PIECES: 0}
    return _pcall(body, name=name, grid=(S // tm,), in_specs=in_specs,
                  out_specs=pl.BlockSpec((None, tm, D_MODEL), lambda i: (g, i, 0)),
                  out_shape=jax.ShapeDtypeStruct((N_GROUPS, S, D_MODEL), BF16),
                  scratch_shapes=[pltpu.VMEM((D_MODEL, 3 * D_MODEL), BF16), pltpu.SemaphoreType.DMA((per_group,))],
                  input_output_aliases=aliases, compiler_params=_params(("arbitrary",)))(*operands)


def _alibi_coef(g, h):
    vals = [-(2.0 ** (-8.0 * (gg * N_HEADS + h + 1) / (N_GROUPS * N_HEADS))) * ATTN_DILATIONS[gg]
            for gg in range(N_GROUPS)]
    return jnp.where(g == 0, vals[0], jnp.where(g == 1, vals[1], vals[2])).astype(F32)


def _blocks_per_segment(g, S):
    return jnp.right_shift(S // Q_BLOCK, 2 * g)


def _band_bias(pen):
    row = lax.broadcasted_iota(jnp.int32, (Q_BLOCK, 2 * Q_BLOCK), 0)
    col = lax.broadcasted_iota(jnp.int32, (Q_BLOCK, 2 * Q_BLOCK), 1)
    dist = Q_BLOCK + row - col
    valid = jnp.logical_and(dist >= 0, dist <= Q_BLOCK)
    base = jnp.where(valid, jnp.where(col < Q_BLOCK, pen, 0.0), NEG).astype(F32)
    return base, dist.astype(F32)


def _skewed(n_units, stages):
    state = {}
    for step in range(n_units + len(stages) - 1):
        for s, stage in enumerate(stages):
            u = step - s
            if 0 <= u < n_units:
                state[u] = stage(u, state.get(u))
                if s == len(stages) - 1:
                    del state[u]


def _attn_fwd(name, qkv, tq=1024):
    S = qkv.shape[1]
    nqb = tq // Q_BLOCK
    scale = HEAD_DIM ** -0.5

    def body(q_ref, k_ref, kp_ref, v_ref, vp_ref, o_ref, lse_ref, kf_ref, vf_ref):
        g = pl.program_id(0)
        i = pl.program_id(1)
        nb = _blocks_per_segment(g, S)
        kf_ref[:Q_BLOCK] = kp_ref[...]
        kf_ref[Q_BLOCK:] = k_ref[...]
        vf_ref[:Q_BLOCK] = vp_ref[...]
        vf_ref[Q_BLOCK:] = v_ref[...]
        coefs = [_alibi_coef(g, h) for h in range(N_HEADS)]
        units = [(b, h) for b in range(nqb) for h in range(N_HEADS)]
        bias = {}

        def scores(u, _):
            b, h = units[u]
            if b not in bias:
                pen = jnp.where((i * nqb + b) % nb != 0, 0.0, NEG).astype(F32)
                bias.clear()
                bias[b] = _band_bias(pen)
            base, dist = bias[b]
            cols = slice(h * HEAD_DIM, (h + 1) * HEAD_DIM)
            q = q_ref[b * Q_BLOCK:(b + 1) * Q_BLOCK, cols]
            kk = kf_ref[b * Q_BLOCK:(b + 2) * Q_BLOCK, cols]
            return _dot(q, kk, "nt") * scale + (coefs[h] * dist + base)

        def softmax(u, s):
            m = jnp.max(s, axis=-1, keepdims=True)
            p = jnp.exp(s - m)
            den = jnp.sum(p, axis=-1, keepdims=True)
            return (p * (1.0 / den)).astype(BF16), m + jnp.log(den)

        def output(u, st):
            b, h = units[u]
            pn, lse = st
            cols = slice(h * HEAD_DIM, (h + 1) * HEAD_DIM)
            rows = slice(b * Q_BLOCK, (b + 1) * Q_BLOCK)
            o_ref[rows, cols] = _dot(pn, vf_ref[b * Q_BLOCK:(b + 2) * Q_BLOCK, cols], "nn").astype(o_ref.dtype)
            lse_ref[rows, h:h + 1] = lse

        _skewed(len(units), [scores, softmax, output])

    main = lambda c: pl.BlockSpec((None, tq, D_MODEL), lambda g, i: (g, i, c))
    prev = lambda c: pl.BlockSpec((None, Q_BLOCK, D_MODEL), lambda g, i: (g, jnp.maximum(i * nqb - 1, 0), c))
    return _pcall(
        body, name=name, grid=(N_GROUPS, S // tq),
        in_specs=[main(0), main(1), prev(1), main(2), prev(2)],
        out_specs=[pl.BlockSpec((None, tq, D_MODEL), lambda g, i: (g, i, 0)),
                   pl.BlockSpec((None, tq, N_HEADS), lambda g, i: (g, i, 0))],
        out_shape=[jax.ShapeDtypeStruct((N_GROUPS, S, D_MODEL), BF16),
                   jax.ShapeDtypeStruct((N_GROUPS, S, N_HEADS), F32)],
        scratch_shapes=[pltpu.VMEM((tq + Q_BLOCK, D_MODEL), BF16), pltpu.VMEM((tq + Q_BLOCK, D_MODEL), BF16)],
        compiler_params=_params(("parallel", "parallel")))(qkv, qkv, qkv, qkv, qkv)


def _attn_bwd(name, qkv, do, lse, dl, tq=1024):
    S = qkv.shape[1]
    nqb = tq // Q_BLOCK
    n_blocks = S // Q_BLOCK
    scale = HEAD_DIM ** -0.5
    KEYS = 2 * Q_BLOCK

    def body(q_ref, k_ref, v_ref, kp_ref, vp_ref, qn_ref, do_ref, don_ref, lse_ref, lsen_ref, dl_ref, dln_ref,
             out_ref, kf_ref, vf_ref, dk_ref, dv_ref):
        g = pl.program_id(0)
        i = pl.program_id(1)
        nb = _blocks_per_segment(g, S)
        kf_ref[:Q_BLOCK] = kp_ref[...]
        kf_ref[Q_BLOCK:] = k_ref[...]
        vf_ref[:Q_BLOCK] = vp_ref[...]
        vf_ref[Q_BLOCK:] = v_ref[...]
        coefs = [_alibi_coef(g, h) for h in range(N_HEADS)]
        units = [(h, b) for h in range(N_HEADS) for b in range(nqb + 1)]
        bias = {}

        def band(b):
            if b not in bias:
                blk = i * nqb + b
                ok = blk % nb != 0
                if b == nqb:
                    ok = jnp.logical_and(ok, blk < n_blocks)
                bias[b] = _band_bias(jnp.where(ok, 0.0, NEG).astype(F32))
            return bias[b]

        def operands(h, b):
            cols = slice(h * HEAD_DIM, (h + 1) * HEAD_DIM)
            if b == nqb:
                keys = slice(tq, tq + Q_BLOCK)
                return (qn_ref[:, cols], don_ref[:, cols], lsen_ref[:, h:h + 1], dln_ref[:, h:h + 1],
                        kf_ref[keys, cols], vf_ref[keys, cols])
            rows = slice(b * Q_BLOCK, (b + 1) * Q_BLOCK)
            keys = slice(b * Q_BLOCK, b * Q_BLOCK + KEYS)
            return (q_ref[rows, cols], do_ref[rows, cols], lse_ref[rows, h:h + 1], dl_ref[rows, h:h + 1],
                    kf_ref[keys, cols], vf_ref[keys, cols])

        def probs(u, _):
            h, b = units[u]
            q, do_b, lse_b, dl_b, kk, vv = operands(h, b)
            base, dist = band(b)
            if b == nqb:
                base, dist = base[:, :Q_BLOCK], dist[:, :Q_BLOCK]
            p = jnp.exp(_dot(q, kk, "nt") * scale + (coefs[h] * dist + base) - lse_b)
            return p, _dot(do_b, vv, "nt")

        def dscores(u, st):
            h, b = units[u]
            p, dp = st
            dl_b = operands(h, b)[3]
            return ((p * (dp - dl_b)) * scale).astype(BF16), p.astype(BF16)

        def grads(u, st):
            h, b = units[u]
            ds, pb = st
            q, do_b, _, _, kk, _ = operands(h, b)
            cols = slice(h * HEAD_DIM, (h + 1) * HEAD_DIM)
            if b < nqb:
                out_ref[b * Q_BLOCK:(b + 1) * Q_BLOCK, cols] = _dot(ds, kk, "nn").astype(out_ref.dtype)
            if b == 0:
                dk_ref[:Q_BLOCK] = _dot(ds[:, Q_BLOCK:], q, "tn")
                dv_ref[:Q_BLOCK] = _dot(pb[:, Q_BLOCK:], do_b, "tn")
                return None
            dk = _dot(ds, q, "tn")
            dv = _dot(pb, do_b, "tn")
            before = slice((b - 1) * Q_BLOCK, b * Q_BLOCK)
            dk_ref[before] += dk[:Q_BLOCK]
            dv_ref[before] += dv[:Q_BLOCK]
            if b < nqb:
                own = slice(b * Q_BLOCK, (b + 1) * Q_BLOCK)
                dk_ref[own] = dk[Q_BLOCK:]
                dv_ref[own] = dv[Q_BLOCK:]
            else:
                out_ref[:, D_MODEL + h * HEAD_DIM:D_MODEL + (h + 1) * HEAD_DIM] = dk_ref[...].astype(out_ref.dtype)
                out_ref[:, 2 * D_MODEL + h * HEAD_DIM:2 * D_MODEL + (h + 1) * HEAD_DIM] = (
                    dv_ref[...].astype(out_ref.dtype))
            return None

        _skewed(len(units), [probs, dscores, grads])

    nxt_blk = lambda i: jnp.minimum((i + 1) * nqb, n_blocks - 1)
    main = lambda c: pl.BlockSpec((None, tq, D_MODEL), lambda g, i: (g, i, c))
    prev = lambda c: pl.BlockSpec((None, Q_BLOCK, D_MODEL), lambda g, i: (g, jnp.maximum(i * nqb - 1, 0), c))
    row = pl.BlockSpec((None, tq, D_MODEL), lambda g, i: (g, i, 0))
    row_n = pl.BlockSpec((None, Q_BLOCK, D_MODEL), lambda g, i: (g, nxt_blk(i), 0))
    col = pl.BlockSpec((None, tq, N_HEADS), lambda g, i: (g, i, 0))
    col_n = pl.BlockSpec((None, Q_BLOCK, N_HEADS), lambda g, i: (g, nxt_blk(i), 0))
    return _pcall(
        body, name=name, grid=(N_GROUPS, S // tq),
        in_specs=[main(0), main(1), main(2), prev(1), prev(2), row_n, row, row_n, col, col_n, col, col_n],
        out_specs=pl.BlockSpec((None, tq, 3 * D_MODEL), lambda g, i: (g, i, 0)),
        out_shape=jax.ShapeDtypeStruct((N_GROUPS, S, 3 * D_MODEL), BF16),
        scratch_shapes=[pltpu.VMEM((tq + Q_BLOCK, D_MODEL), BF16), pltpu.VMEM((tq + Q_BLOCK, D_MODEL), BF16),
                        pltpu.VMEM((tq, HEAD_DIM), F32), pltpu.VMEM((tq, HEAD_DIM), F32)],
        compiler_params=_params(("parallel", "parallel")))(qkv, qkv, qkv, qkv, qkv, qkv, do, do, lse, lse, dl, dl)


def _group_weights(lse_ref):
    l0, l1, l2 = lse_ref[0], lse_ref[1], lse_ref[2]
    m = jnp.maximum(jnp.maximum(l0, l1), l2)
    e = [jnp.exp(l0 - m), jnp.exp(l1 - m), jnp.exp(l2 - m)]
    den = e[0] + e[1] + e[2]
    return [ei / den for ei in e]


MERGE_TOKENS = 512


def _folded_views(a, tb):
    _, S, C = a.shape
    views, specs = [], []
    for g, dil in enumerate(ATTN_DILATIONS):
        views.append(a.reshape(N_GROUPS * dil, S // dil, C))
        specs.append(pl.BlockSpec((dil, tb // dil, C), lambda i, g=g: (g, i, 0)))
    return views, specs


def _unfold_into(dst_ref, folded_refs):
    for g, (dil, ref) in enumerate(zip(ATTN_DILATIONS, folded_refs)):
        n = ref.shape[1]
        for r in range(dil):
            for c, cols in enumerate(_lane_chunks(ref.shape[2])):
                dst_ref[g, c, _strided_rows(r, n, dil), :] = ref[r, :, cols].astype(F32)


def _merge_fwd(name, o, lse, tb=MERGE_TOKENS):
    S = o.shape[1]

    def body(o0_ref, o1_ref, o2_ref, lse_ref, out_ref, o_ref):
        _unfold_into(o_ref, (o0_ref, o1_ref, o2_ref))
        w = _group_weights(lse_ref)
        for h in range(N_HEADS):
            cols = slice(h * HEAD_DIM, (h + 1) * HEAD_DIM)
            acc = w[0][:, h:h + 1] * o_ref[0, h]
            for gg in range(1, N_GROUPS):
                acc = acc + w[gg][:, h:h + 1] * o_ref[gg, h]
            out_ref[:, cols] = acc.astype(out_ref.dtype)

    views, specs = _folded_views(o, tb)
    return _pcall(body, name=name, grid=(S // tb,),
                  in_specs=specs + [pl.BlockSpec((N_GROUPS, tb, N_HEADS), lambda i: (0, i, 0))],
                  out_specs=pl.BlockSpec((tb, D_MODEL), lambda i: (i, 0)),
                  out_shape=jax.ShapeDtypeStruct((S, D_MODEL), BF16),
                  scratch_shapes=[pltpu.VMEM((N_GROUPS, N_HEADS, tb, HEAD_DIM), F32)],
                  compiler_params=_params(("parallel",)))(*views, lse)


def _merge_bwd(name, dx, w_out, o, lse, tb=MERGE_TOKENS):
    S = o.shape[1]
    n_chunks = sum(ATTN_DILATIONS)

    def body(dx_ref, w_ref, o0_ref, o1_ref, o2_ref, lse_ref, do_hbm, dl_ref, o_ref, dt_ref, df_ref, d_ref, sems):
        i = pl.program_id(0)
        d_ref[...] = _dot(dx_ref[...].astype(BF16), w_ref[...], "nt")
        _unfold_into(o_ref, (o0_ref, o1_ref, o2_ref))
        w = _group_weights(lse_ref)
        for h in range(N_HEADS):
            cols = slice(h * HEAD_DIM, (h + 1) * HEAD_DIM)
            dv = d_ref[:, cols]
            merged = w[0][:, h:h + 1] * o_ref[0, h]
            for gg in range(1, N_GROUPS):
                merged = merged + w[gg][:, h:h + 1] * o_ref[gg, h]
            dsum = jnp.sum(dv * merged, axis=-1, keepdims=True)
            for gg in range(N_GROUPS):
                wg = w[gg][:, h:h + 1]
                dt_ref[gg, h] = wg * dv
                dl_ref[gg, :, h:h + 1] = wg * dsum
        copies = []
        for gg, dil in enumerate(ATTN_DILATIONS):
            n = tb // dil
            for r in range(dil):
                for h, cols in enumerate(_lane_chunks(D_MODEL)):
                    df_ref[gg, r * n:(r + 1) * n, cols] = (
                        dt_ref[gg, h, _strided_rows(r, n, dil), :].astype(df_ref.dtype))
                cp = pltpu.make_async_copy(df_ref.at[gg, pl.ds(r * n, n)],
                                           do_hbm.at[gg, pl.ds(r * (S // dil) + i * n, n)], sems.at[len(copies)])
                cp.start()
                copies.append(cp)
        for cp in copies:
            cp.wait()

    views, specs = _folded_views(o, tb)
    small = pl.BlockSpec((N_GROUPS, tb, N_HEADS), lambda i: (0, i, 0))
    return _pcall(body, name=name, grid=(S // tb,),
                  in_specs=[pl.BlockSpec((tb, D_MODEL), lambda i: (i, 0)),
                            pl.BlockSpec((D_MODEL, D_MODEL), lambda i: (0, 0))] + specs + [small],
                  out_specs=[pl.BlockSpec(memory_space=pl.ANY), small],
                  out_shape=[jax.ShapeDtypeStruct((N_GROUPS, S, D_MODEL), BF16),
                             jax.ShapeDtypeStruct((N_GROUPS, S, N_HEADS), F32)],
                  scratch_shapes=[pltpu.VMEM((N_GROUPS, N_HEADS, tb, HEAD_DIM), F32),
                                  pltpu.VMEM((N_GROUPS, N_HEADS, tb, HEAD_DIM), F32),
                                  pltpu.VMEM((N_GROUPS, tb, D_MODEL), BF16), pltpu.VMEM((tb, D_MODEL), F32),
                                  pltpu.SemaphoreType.DMA((n_chunks,))],
                  compiler_params=_params(("arbitrary",)))(dx, w_out, *views, lse)


def _shift_rows(a, k):
    return pltpu.roll(a, k % a.shape[0], axis=0)


def _pool_level(g, levels):
    return jnp.where(g == 0, levels[0], jnp.where(g == 1, levels[1], jnp.where(g == 2, levels[2], levels[3])))


def _pool_fwd(name, h, w_in, w_group, scale, x_in, gain_next, tr=1024):
    S, D = h.shape
    H = POOL_HALO
    n_groups = D // POOL_DIM

    def body(h_ref, hh_ref, wi_ref, w_ref, sc_ref, x_ref, gn_ref, y_ref, z_ref, xo_ref, hn_ref, xs_ref):
        i = pl.program_id(0)
        g = pl.program_id(1)
        uv = _dot(h_ref[...], wi_ref[...], "nn")
        halo = jnp.where(i > 0, _dot(hh_ref[...], wi_ref[...], "nn"), 0.0)
        s = jnp.concatenate([halo, uv], axis=0)
        levels = []
        for k in (1, 2, 4, 8):
            s = s + _shift_rows(s, k)
            levels.append(s[H:])
        t = i * tr + lax.broadcasted_iota(jnp.int32, (tr, 1), 0)
        cnt = jnp.minimum(t + 1, jnp.left_shift(2, g)).astype(F32)
        y = _pool_level(g, levels) / cnt - uv
        yb = y.astype(BF16)
        z = _dot(yb, w_ref[...], "nn")
        y_ref[...] = yb
        z_ref[...] = z.astype(z_ref.dtype)
        x_out = x_ref[...] + z * sc_ref[...]
        xo_ref[...] = x_out
        xs_ref[g] = x_out

        @pl.when(g == n_groups - 1)
        def _():
            xf = jnp.concatenate([xs_ref[k] for k in range(n_groups)], axis=1)
            r = lax.rsqrt(jnp.mean(xf * xf, axis=-1, keepdims=True) + RMS_EPS)
            hn_ref[...] = ((xf * r) * gn_ref[...]).astype(hn_ref.dtype)

    blk = pl.BlockSpec((tr, POOL_DIM), lambda i, g: (i, g))
    row = pl.BlockSpec((tr, D), lambda i, g: (i, 0))
    return _pcall(
        body, name=name, grid=(S // tr, n_groups),
        in_specs=[row, pl.BlockSpec((H, D), lambda i, g: (jnp.maximum(i * (tr // H) - 1, 0), 0)),
                  pl.BlockSpec((D, POOL_DIM), lambda i, g: (0, g)),
                  pl.BlockSpec((None, POOL_DIM, POOL_DIM), lambda i, g: (g, 0, 0)),
                  pl.BlockSpec((1, POOL_DIM), lambda i, g: (0, g)), blk, pl.BlockSpec((1, D), lambda i, g: (0, 0))],
        out_specs=[blk, blk, blk, row],
        out_shape=[jax.ShapeDtypeStruct((S, D), BF16), jax.ShapeDtypeStruct((S, D), BF16),
                   jax.ShapeDtypeStruct((S, D), F32), jax.ShapeDtypeStruct((S, D), BF16)],
        scratch_shapes=[pltpu.VMEM((n_groups, tr, POOL_DIM), F32)],
        compiler_params=_params(("parallel", "arbitrary")))(h, h, w_in, w_group, scale, x_in, gain_next)


def _pool_bwd(name, d_out, z, y, scale, w_group, tr=1024):
    S, D = d_out.shape
    H = POOL_HALO

    def body(d_ref, dn_ref, z_ref, y_ref, sc_ref, w_ref, du_ref, dw_ref, dsc_ref):
        g = pl.program_id(0)
        i = pl.program_id(1)
        dv = d_ref[...]
        dz = jnp.concatenate([dv, dn_ref[...]], axis=0) * sc_ref[...]
        dzb = dz.astype(BF16)
        dy = _dot(dzb, w_ref[...], "nt")
        t = i * tr + lax.broadcasted_iota(jnp.int32, (tr + H, 1), 0)
        cnt = jnp.minimum(t + 1, jnp.left_shift(2, g)).astype(F32)
        s = jnp.where(t < S, dy / cnt, 0.0)
        levels = []
        for k in (1, 2, 4, 8):
            s = s + _shift_rows(s, -k)
            levels.append(s[:tr])
        du_ref[...] = (_pool_level(g, levels) - dy[:tr]).astype(du_ref.dtype)
        dw = _dot(y_ref[...], dzb[:tr], "tn")
        dsc = jnp.sum(dv * z_ref[...].astype(F32), axis=0, keepdims=True)

        @pl.when(i == 0)
        def _():
            dw_ref[...] = dw
            dsc_ref[...] = dsc

        @pl.when(i > 0)
        def _():
            dw_ref[...] += dw
            dsc_ref[...] += dsc

    blk = pl.BlockSpec((tr, POOL_DIM), lambda g, i: (i, g))
    vec = pl.BlockSpec((1, POOL_DIM), lambda g, i: (0, g))
    mat = pl.BlockSpec((None, POOL_DIM, POOL_DIM), lambda g, i: (g, 0, 0))
    nxt = pl.BlockSpec((H, POOL_DIM), lambda g, i: (jnp.minimum((i + 1) * (tr // H), S // H - 1), g))
    return _pcall(
        body, name=name, grid=(D // POOL_DIM, S // tr), in_specs=[blk, nxt, blk, blk, vec, mat],
        out_specs=[blk, mat, vec],
        out_shape=[jax.ShapeDtypeStruct((S, D), BF16), jax.ShapeDtypeStruct((D // POOL_DIM, POOL_DIM, POOL_DIM), F32),
                   jax.ShapeDtypeStruct((1, D), F32)],
        compiler_params=_params(("parallel", "arbitrary")))(d_out, d_out, z, y, scale, w_group)


def _row_tile(rows, cols, target_bytes=1 << 20):
    best = rows
    for tr in range(8, rows + 1, 8):
        if rows % tr == 0 and tr * cols * 4 <= target_bytes:
            best = tr
    return best if best * cols * 4 <= 4 * target_bytes else rows


def _adamw_step(w, g, m, v):
    m2 = ADAM_B1 * m + (1.0 - ADAM_B1) * g
    v2 = ADAM_B2 * v + (1.0 - ADAM_B2) * (g * g)
    m_hat = m2 / (1.0 - ADAM_B1 ** ADAM_STEP)
    v_hat = v2 / (1.0 - ADAM_B2 ** ADAM_STEP)
    return -ADAM_LR * (m_hat / (jnp.sqrt(v_hat) + ADAM_EPS) + ADAM_WD * w), m2, v2


def _adamw_refs(w_ref, g_ref, m_ref, v_ref, go_ref, d_ref, mo_ref, vo_ref):
    gv = g_ref[...]
    d_ref[...], mo_ref[...], vo_ref[...] = _adamw_step(w_ref[...], gv, m_ref[...], v_ref[...])
    go_ref[...] = gv


def _adamw(name, items, steps=16):
    n = len(items)

    def body(*refs):
        for t in range(n):
            _adamw_refs(*refs[4 * t:4 * t + 4], *refs[4 * n + 4 * t:4 * n + 4 * t + 4])

    specs, shapes = [], []
    for w, _, _, _ in items:
        R, C = w.shape
        assert R % (8 * steps) == 0, (name, w.shape)
        specs += [pl.BlockSpec((R // steps, C), lambda i: (i, 0))] * 4
        shapes += [jax.ShapeDtypeStruct((R, C), F32)] * 4
    res = _pcall(body, name=name, grid=(steps,), in_specs=specs, out_specs=specs, out_shape=shapes,
                 compiler_params=_params(("parallel",)))(*[a for item in items for a in item])
    return [res[4 * t:4 * t + 4] for t in range(n)]


def _adamw_small(name, items):
    n = len(items)

    def body(*refs):
        for t in range(n):
            _adamw_refs(*refs[4 * t:4 * t + 4], *refs[4 * n + 4 * t:4 * n + 4 * t + 4])

    specs, shapes = [], []
    for w, _, _, _ in items:
        specs += [pl.BlockSpec(w.shape, lambda i: (0, 0))] * 4
        shapes += [jax.ShapeDtypeStruct(w.shape, F32)] * 4
    res = _pcall(body, name=name, grid=(1,), in_specs=specs, out_specs=specs, out_shape=shapes,
                 compiler_params=_params(("arbitrary",)))(*[a for item in items for a in item])
    return [res[4 * t:4 * t + 4] for t in range(n)]


def _sum_leading(name, a, out_dtype):
    n, R, C = a.shape
    tr = _row_tile(R, C) if R % 16 == 0 else R
    if tr % 16:
        tr = R

    def body(a_ref, o_ref):
        acc = a_ref[0].astype(F32)
        for j in range(1, n):
            acc = acc + a_ref[j].astype(F32)
        o_ref[...] = acc.astype(o_ref.dtype)

    return _pcall(body, name=name, grid=(R // tr,), in_specs=[pl.BlockSpec((n, tr, C), lambda i: (0, i, 0))],
                  out_specs=pl.BlockSpec((tr, C), lambda i: (i, 0)), out_shape=jax.ShapeDtypeStruct((R, C), out_dtype),
                  compiler_params=_params(("parallel",)))(a)


def _cast_many(name, items, chip, steps=4):
    tiles = []
    for w, _, dtype in items:
        R = w.shape[1]
        nt = steps if R % (steps * 16) == 0 else 1
        tiles.append((nt, R // nt))

    def body(c_ref, *refs):
        i = pl.program_id(0)
        for t, (nt, _) in enumerate(tiles):
            w_ref, o_ref = refs[t], refs[len(items) + t]

            def cast(w_ref=w_ref, o_ref=o_ref):
                o_ref[...] = w_ref[...].astype(o_ref.dtype)

            if nt == steps:
                cast()
            else:
                pl.when(i < nt)(cast)

    in_specs, out_specs, out_shape = [], [], []
    for (w, layer, dtype), (nt, tr) in zip(items, tiles):
        C = w.shape[2]
        in_specs.append(pl.BlockSpec((None, tr, C), lambda i, c_ref, l=layer, nt=nt: (l, jnp.minimum(i, nt - 1), 0)))
        out_specs.append(pl.BlockSpec((None, tr, C), lambda i, c_ref, nt=nt: (c_ref[0], jnp.minimum(i, nt - 1), 0)))
        out_shape.append(jax.ShapeDtypeStruct((N_CHIPS, w.shape[1], C), dtype))
    return _pcall(body, prefetch=1, name=name, grid=(steps,), in_specs=in_specs, out_specs=out_specs,
                  out_shape=out_shape, compiler_params=_params(("arbitrary",)))(chip, *[w for w, _, _ in items])


def _cast_qkv(name, w, chip, tr=256):
    _, R, C = w.shape

    def body(c_ref, w_ref, own_ref, *piece_refs):
        v = w_ref[...].astype(BF16)
        own_ref[...] = v
        for p, ref in enumerate(piece_refs):
            ref[...] = v[:, p * QKV_TILE:(p + 1) * QKV_TILE]

    piece = pl.BlockSpec((None, tr, QKV_TILE), lambda i, c_ref: (c_ref[0], i, 0))
    return _pcall(body, prefetch=1, name=name, grid=(R // tr,),
                  in_specs=[pl.BlockSpec((None, tr, C), lambda i, c_ref: (0, i, 0))],
                  out_specs=[pl.BlockSpec((tr, C), lambda i, c_ref: (i, 0))] + [piece] * QKV_PIECES,
                  out_shape=[jax.ShapeDtypeStruct((R, C), BF16)]
                  + [jax.ShapeDtypeStruct((N_CHIPS, R, QKV_TILE), BF16)] * QKV_PIECES,
                  compiler_params=_params(("parallel",)))(chip, w)


def _owner_sum(name, mine, landed, shape, chip, core, out, layer, col):
    _, half, C = mine.shape
    k, _ = col
    tr = _row_tile(half, C)
    if tr % 16:
        tr = half
    nrt = half // tr

    def body(ch_ref, co_ref, mine_ref, landed_ref, *rest):
        g = mine_ref[...].astype(F32)
        for j in range(3):
            g = g + landed_ref[j].astype(F32)
        rest[-1][...] = g

    piece = pl.BlockSpec((None, tr, C), lambda i, ch, co: (layer, co[0] * nrt + i, k))
    in_specs = [pl.BlockSpec((None, tr, C), lambda i, ch, co: (ch[0], i, 0)),
                pl.BlockSpec((3, tr, C), lambda i, ch, co: (0, i, 0))]
    operands = [chip, core, mine, landed]
    aliases = {}
    if out is not None:
        in_specs.append(ANY)
        operands.append(out)
        aliases = {4: 0}
    return _pcall(body, prefetch=2, name=name, grid=(nrt,), in_specs=in_specs, out_specs=piece,
                  out_shape=jax.ShapeDtypeStruct(shape, F32), input_output_aliases=aliases,
                  compiler_params=_params(("parallel",)))(*operands)


def _add_my_half(name, ps, qs, core):
    n = len(ps)

    def body(c_ref, *refs):
        for t in range(n):
            p_ref, q_ref, o_ref = refs[t], refs[n + t], refs[2 * n + t]
            o_ref[...] = (p_ref[...].astype(F32) + q_ref[...].astype(F32)).astype(o_ref.dtype)

    halves = [(p.shape[1] // 2, p.shape[2]) for p in ps]
    mine = [pl.BlockSpec((None, h, c), lambda s, c_ref: (s, c_ref[0], 0)) for h, c in halves]
    whole = [pl.BlockSpec((None, h, c), lambda s, c_ref: (s, 0, 0)) for h, c in halves]
    return _pcall(body, prefetch=1, name=name, grid=(N_CHIPS,), in_specs=mine + whole, out_specs=whole,
                  out_shape=[jax.ShapeDtypeStruct((N_CHIPS, h, c), BF16) for h, c in halves],
                  compiler_params=_params(("parallel",)))(core, *ps, *qs)


ANY = pl.BlockSpec(memory_space=pl.ANY)


def _place():
    x, y, c = lax.axis_index("x"), lax.axis_index("y"), lax.axis_index("c")
    other_chips = [(1 - x, y), (x, 1 - y), (1 - x, 1 - y)]
    return x, y, c, other_chips


def _remote(src, dst, send, recv, k, to):
    return pltpu.make_async_remote_copy(src_ref=src, dst_ref=dst, send_sem=send.at[k], recv_sem=recv.at[k],
                                        device_id=to, device_id_type=MESH)


def _in_place(bufs):
    return dict(operands=bufs, out_shapes=[jax.ShapeDtypeStruct(b.shape, b.dtype) for b in bufs],
                aliases={t: t for t in range(len(bufs))})


def _gather_rider(bufs, jobs):
    def copies(ins, outs, send, recv, base=0):
        x, y, c, chips = _place()
        out = []
        for t, over_ici, rows in jobs:
            ref = outs[t]
            if rows is not None:
                a, b, n = rows
                half = ref.shape[1] // 2
                rows = pl.ds(c * half + a * half // n, (b - a) * half // n)
            for px, py in chips:
                slot = 2 * x + y if over_ici else 2 * px + py
                piece = ref.at[slot] if rows is None else ref.at[slot, rows]
                out.append(_remote(piece, piece, send, recv, base + len(out), (px, py, c) if over_ici else (x, y, 1 - c)))
        return out

    return _Rider(n_copies=3 * len(jobs), copies=copies, **_in_place(bufs))


def _gather_and_pass_rider(buf):
    half = buf.shape[1] // 2

    def pieces(outs):
        x, y, c, chips = _place()
        rows = lambda core: pl.ds(core * half, half)
        mine = outs[0].at[2 * x + y, rows(c)]
        landed = [outs[0].at[2 * px + py, rows(c)] for px, py in chips]
        theirs = [outs[0].at[2 * px + py, rows(1 - c)] for px, py in chips]
        return (x, y, c, chips), mine, landed, theirs

    def start(ins, outs, send, recv, base=0):
        (x, y, c, chips), mine, _, _ = pieces(outs)
        for j, (px, py) in enumerate(chips):
            _remote(mine, mine, send, recv, base + j, (px, py, c)).start()

    def finish(ins, outs, send, recv, base=0):
        (x, y, c, chips), mine, landed, theirs = pieces(outs)
        sibling = (x, y, 1 - c)
        passed = []
        for j, (px, py) in enumerate(chips):
            _remote(landed[j], landed[j], send, recv, base + j, (px, py, c)).wait_recv()
            passed.append(_remote(landed[j], landed[j], send, recv, base + 3 + j, sibling))
            passed[-1].start()
        for j in range(3):
            _remote(theirs[j], theirs[j], send, recv, base + 3 + j, sibling).wait_recv()
        for j, (px, py) in enumerate(chips):
            _remote(mine, mine, send, recv, base + j, (px, py, c)).wait_send()
            passed[j].wait_send()

    return _Rider(n_copies=6, start=start, finish=finish, **_in_place([buf]))


def _swap_halves_rider(parts):
    n = len(parts)

    def copies(ins, outs, send, recv, base=0):
        x, y, c, _ = _place()
        out = []
        for t in range(n):
            half = ins[t].shape[1] // 2
            out.append(_remote(ins[t].at[:, pl.ds((1 - c) * half, half)], outs[t], send, recv, base + t,
                               (x, y, 1 - c)))
        return out

    shapes = [jax.ShapeDtypeStruct((p.shape[0], p.shape[1] // 2, p.shape[2]), p.dtype) for p in parts]
    return _Rider(parts, shapes, {}, n, copies)


def _scatter_rider(sums, landed, relations):
    n = len(sums)
    kept = [t for t in range(n) if landed[t] is not None]

    def copies(ins, outs, send, recv, base=0):
        x, y, c, chips = _place()
        out = []
        for t in range(n):
            for j in relations[t]:
                px, py = chips[j]
                out.append(_remote(ins[t].at[2 * px + py], outs[t].at[j], send, recv, base + len(out), (px, py, c)))
        return out

    shapes = [jax.ShapeDtypeStruct((3,) + s.shape[1:], s.dtype) for s in sums]
    return _Rider(list(sums) + [landed[t] for t in kept], shapes, {n + k: t for k, t in enumerate(kept)},
                  sum(len(r) for r in relations), copies)


def _share_rider(blocks, pieces):
    def copies(ins, outs, send, recv, base=0):
        x, y, c, _ = _place()
        out = []
        for k, (t, l, col) in enumerate(pieces):
            ref = outs[t].at[l]
            r2 = ref.shape[0] // 2
            width = ref.shape[1] // col[1]
            piece = ref.at[pl.ds(c * r2, r2), pl.ds(col[0] * width, width)]
            out.append(_remote(piece, piece, send, recv, base + k, (x, y, 1 - c)))
        return out

    return _Rider(n_copies=len(pieces), copies=copies, **_in_place(blocks))


def _gather_small(name, v):
    def body(v_ref, out_ref, send_sem, recv_sem, local_sem):
        x, y, c, _ = _place()
        me = 4 * x + 2 * y + c
        flips = [(fx, fy, fc) for fx in (0, 1) for fy in (0, 1) for fc in (0, 1)][1:]
        local = pltpu.make_async_copy(v_ref, out_ref.at[me], local_sem)
        local.start()
        copies = []
        for j, (fx, fy, fc) in enumerate(flips):
            peer = (x ^ fx, y ^ fy, c ^ fc)
            copies.append(pltpu.make_async_remote_copy(
                src_ref=v_ref, dst_ref=out_ref.at[me], send_sem=send_sem.at[j], recv_sem=recv_sem.at[j],
                device_id=peer, device_id_type=MESH))
        for cp in copies:
            cp.start()
        for j, (fx, fy, fc) in enumerate(flips):
            peer = (x ^ fx, y ^ fy, c ^ fc)
            pltpu.make_async_remote_copy(
                src_ref=v_ref, dst_ref=out_ref.at[4 * peer[0] + 2 * peer[1] + peer[2]], send_sem=send_sem.at[j],
                recv_sem=recv_sem.at[j], device_id=peer, device_id_type=MESH).wait_recv()
        for cp in copies:
            cp.wait_send()
        local.wait()

    return _pcall(body, name=name, in_specs=[ANY], out_specs=ANY,
                  out_shape=jax.ShapeDtypeStruct((8,) + v.shape, v.dtype),
                  scratch_shapes=[pltpu.SemaphoreType.DMA((7,)), pltpu.SemaphoreType.DMA((7,)),
                                  pltpu.SemaphoreType.DMA])(v)


def _fold(a, dil):
    if dil == 1:
        return a
    S, C = a.shape
    return a.reshape(S // dil, dil, C).transpose(1, 0, 2).reshape(S, C)


def _unfold(a, dil):
    if dil == 1:
        return a
    S, C = a.shape
    return a.reshape(dil, S // dil, C).transpose(1, 0, 2).reshape(S, C)


def _fold_groups(a):
    return jnp.stack([_fold(a[g] if a.ndim == 3 else a, d) for g, d in enumerate(ATTN_DILATIONS)])


def _unfold_groups(a):
    return jnp.stack([_unfold(a[g], d) for g, d in enumerate(ATTN_DILATIONS)])


class _NoComm:
    def __init__(self, weights):
        self.weights = weights
        self.grads = {}
        self.chip = jnp.zeros((1,), jnp.int32)

    def w(self, name):
        return self.weights[name]

    def run(self, fn, name, *args, **kw):
        return fn(name, *args, **kw)

    def grad(self, name, value):
        self.grads[name] = value


def _local_step(xs, tgt, attn_norm, ffn_norm, final_norm, comm):
    run = comm.run
    hf = run(_norm_fold, "fold", xs, attn_norm)
    qkv = run(_qkv_tiles, "qkv_own", None, hf, comm.w("wq_own"), None, comm.chip)
    for p in range(QKV_PIECES):
        qkv = run(_qkv_tiles, "qkv_piece%d" % p, p, hf, comm.w("wq%d" % p), qkv, comm.chip)
    o_f, lse_f = run(_attn_fwd, "attn_fwd", qkv)
    lse_t = _unfold_groups(lse_f)
    merged = run(_merge_fwd, "merge_fwd", o_f, lse_t)
    x1, h1 = run(_proj_res_norm, "attn_out", merged, comm.w("o"), xs, ffn_norm[0:1])
    gu0, act0 = run(_ffn_up, "ffn0_up", h1, comm.w("gu0"))
    x2, h2 = run(_proj_res_norm, "ffn0_down", act0, comm.w("d0"), x1, comm.w("pool_norm"))
    y, z, x3, h3 = run(_pool_fwd, "pool_fwd", h2, comm.w("p"), comm.w("pg"), comm.w("pool_scale"), x2, ffn_norm[1:2])
    gu1, act1 = run(_ffn_up, "ffn1_up", h3, comm.w("gu1"))
    loss, dx4, dx4_b, d_final = run(_proj_res_loss, "ffn1_down", act1, comm.w("d1"), x3, final_norm, tgt)

    dgu1 = run(_ffn_down_bwd, "ffn1_down_bwd", dx4_b, comm.w("d1"), gu1)
    comm.grad("d1", run(_mm_tn, "ffn1_down_dw", act1, dx4_b, FF_SHARD, 2048))
    comm.grad("gu1", run(_ffn_dw_up, "ffn1_up_dw", h3, dgu1))
    dx3, d_ffn1 = run(_ffn_dh, "ffn1_up_dh", dgu1, comm.w("gu1"), x3, ffn_norm[1:2], dx4)

    du, dw_pg, d_scale = run(_pool_bwd, "pool_bwd", dx3, z, y, comm.w("pool_scale"), comm.w("pg"))
    comm.grad("pg", dw_pg)
    comm.grad("p", run(_mm_tn, "pool_in_dw", h2, du, D_MODEL, h2.shape[0]))
    dx2, dx2_b, d_pool = run(_dh_norm_bwd, "pool_in_dh", du, comm.w("p"), x2, comm.w("pool_norm"), dx3)

    dgu0 = run(_ffn_down_bwd, "ffn0_down_bwd", dx2_b, comm.w("d0"), gu0)
    comm.grad("d0", run(_mm_tn, "ffn0_down_dw", act0, dx2_b, FF_SHARD, 2048))
    comm.grad("gu0", run(_ffn_dw_up, "ffn0_up_dw", h1, dgu0))
    dx1, d_ffn0 = run(_ffn_dh, "ffn0_up_dh", dgu0, comm.w("gu0"), x1, ffn_norm[0:1], dx2)

    comm.grad("o", run(_mm_tn, "attn_out_dw", merged, dx1, D_MODEL, 2048))
    do_f, dl_t = run(_merge_bwd, "merge_bwd", dx1, comm.w("o"), o_f, lse_t)
    dqkv = run(_attn_bwd, "attn_bwd", qkv, do_f, lse_f, _fold_groups(dl_t))
    for p in range(QKV_PIECES):
        comm.grad("qkv%d" % p, run(_qkv_dw, "qkv_dw%d" % p, p, hf, dqkv))
    dhf = None
    for g in range(N_GROUPS):
        dhf = run(_qkv_dh, "qkv_dh%d" % g, g, dqkv, [comm.w("wq%d" % p) for p in range(QKV_PIECES)], dhf)
    grad_x, d_attn = run(_rms_bwd_folded, "norm_attn_bwd", xs, attn_norm, dhf, dx1)

    small = dict(attn=d_attn, ffn0=d_ffn0, ffn1=d_ffn1, final=d_final, pool=d_pool, scale=d_scale)
    return loss, grad_x, small


TENSORS = ("qkv", "o", "p", "pg", "gu0", "gu1", "d0", "d1")


def _block_of(name):
    if name[:-1] in ("gu", "d"):
        return name[:-1], int(name[-1]), (0, 1)
    if name[:-1] == "qkv":
        return "qkv", 0, (int(name[-1]), QKV_PIECES)
    return name, 0, (0, 1)


class _MeshComm:
    def __init__(self, bufs, shapes, chip_arr, core_arr, pg_rows):
        self.bufs = dict(bufs)
        self.shapes = shapes
        self.chip, self.chip_arr, self.core_arr, self.pg_rows = chip_arr, chip_arr, core_arr, pg_rows
        self.parts, self.sums, self.blocks, self.landed, self.arrived = {}, {}, {}, {}, {}
        move = lambda ici=(), d2d=(): lambda: self.gather(ici, d2d)
        whole = lambda name: lambda: self.gather_and_pass(name)
        swap = lambda *names: lambda: self.swap(names)
        scatter = lambda *names: lambda: self.scatter(names)
        share = lambda *names: lambda: self.share(names)
        self.plan = {
            "fold": [whole("wq0")],
            "qkv_own": [whole("wq1")],
            "qkv_piece0": [whole("wq2")],
            "qkv_piece1": [move(ici=["o", "gu0[0:1/4]"])],
            "qkv_piece2": [move(ici=["gu0[1:2/4]"], d2d=["o", "gu0[0:1/4]"])],
            "attn_fwd": [move(ici=["gu0[2:4/4]", "d0[0:1/2]"], d2d=["gu0[1:2/4]"])],
            "merge_fwd": [move(ici=["p", "pg"], d2d=["gu0[2:4/4]", "d0[0:1/2]"])],
            "attn_out": [move(ici=["d0[1:2/2]", "pool_norm", "pool_scale"], d2d=["p", "pg"])],
            "ffn0_up": [move(ici=["gu1[0:3/4]"], d2d=["d0[1:2/2]"])],
            "ffn0_down": [move(ici=["gu1[3:4/4]"], d2d=["gu1[0:3/4]"])],
            "pool_fwd": [move(ici=["d1"], d2d=["gu1[3:4/4]"])],
            "ffn1_up": [move(d2d=["d1"])],
            "ffn1_up_dh": [swap("d1", "gu1")],
            "pool_bwd": [scatter("d1{01}")],
            "pool_in_dh": [scatter("d1{2}")],
            "ffn0_down_bwd": [scatter("gu1{01}")],
            "ffn0_down_dw": [scatter("gu1{2}")],
            "ffn0_up_dw": [share("gu1", "d1")],
            "ffn0_up_dh": [swap("pg", "p", "d0", "gu0")],
            "merge_bwd": [swap("o")],
            "attn_bwd": [scatter("gu0", "d0", "o")],
            "qkv_dw0": [share("d0"), scatter("p", "pg")],
            "qkv_dw1": [share("gu0", "o"), swap("qkv0")],
            "qkv_dw2": [share("p", "pg"), scatter("qkv0"), swap("qkv1")],
            "qkv_dh0": [share("qkv0"), scatter("qkv1"), swap("qkv2")],
            "qkv_dh1": [share("qkv1"), scatter("qkv2")],
            "qkv_dh2": [share("qkv2")],
        }

    def gather_and_pass(self, name):
        return _gather_and_pass_rider(self.bufs[name]), lambda res: self.bufs.update({name: res[0]})

    def gather(self, ici, d2d):
        names, jobs = [], []
        for over_ici, specs in ((True, ici), (False, d2d)):
            for spec in specs:
                name, _, rows = spec.partition("[")
                if name not in names:
                    names.append(name)
                rng = None
                if name in TENSORS:
                    ab, _, n = (rows[:-1] or "0:1/1").partition("/")
                    rng = (int(ab.split(":")[0]), int(ab.split(":")[1]), int(n))
                jobs.append((names.index(name), over_ici, rng))
        return _gather_rider([self.bufs[n] for n in names], jobs), lambda res: self.bufs.update(zip(names, res))

    def swap(self, names):
        def done(res):
            sums = _add_my_half("chip_sum_" + "_".join(names), [self.parts[n] for n in names], res, self.core_arr)
            self.sums.update(zip(names, sums))
        return _swap_halves_rider([self.parts[n] for n in names]), done

    def scatter(self, specs):
        names = [s.partition("{")[0] for s in specs]
        relations = [tuple(int(ch) for ch in s.partition("{")[2][:-1]) or (0, 1, 2) for s in specs]

        def done(res):
            for n, rel, landed in zip(names, relations, res):
                self.landed[n] = landed
                self.arrived[n] = self.arrived.get(n, ()) + rel
                if len(self.arrived[n]) < 3:
                    continue
                key, layer, col = _block_of(n)
                self.blocks[key] = _owner_sum("owner_sum_" + n, self.sums[n], landed, self.shapes[key],
                                              self.chip_arr, self.core_arr, self.blocks.get(key), layer, col)
        rider = _scatter_rider([self.sums[n] for n in names], [self.landed.get(n) for n in names], relations)
        return rider, done

    def share(self, names):
        keys, pieces = [], []
        for n in names:
            key, layer, col = _block_of(n)
            if key not in keys:
                keys.append(key)
            pieces.append((keys.index(key), layer, col))
        return _share_rider([self.blocks[k] for k in keys], pieces), lambda res: self.blocks.update(zip(keys, res))

    def run(self, fn, name, *args, **kw):
        made = [make() for make in self.plan.get(name, [])]
        if not made:
            return fn(name, *args, **kw)
        riders = [r for r, _ in made]
        out = _ride(riders[0] if len(riders) == 1 else _riders(*riders), fn, name, *args, **kw)
        for r, done in made:
            done(r.results)
        return out

    def w(self, name):
        b = self.bufs[name]
        if name in ("o", "p", "d0", "d1"):
            return b.reshape(-1, b.shape[-1])
        if name == "pg":
            b = b.reshape(N_CHIPS, 4, self.pg_rows, POOL_DIM).transpose(1, 0, 2, 3)
            return b.reshape(4, POOL_DIM, POOL_DIM)
        if name in ("pool_norm", "pool_scale"):
            return b.reshape(1, -1)
        return b

    def grad(self, name, value):
        if name == "pg":
            value = value.reshape(4, N_CHIPS, self.pg_rows, POOL_DIM).transpose(1, 0, 2, 3)
            value = value.reshape(N_CHIPS, 4 * self.pg_rows, POOL_DIM).astype(BF16)
        elif name in ("o", "p", "d0", "d1"):
            value = value.reshape(N_CHIPS, value.shape[0] // N_CHIPS, value.shape[1])
        self.parts[name] = value


def kernel(x, attn_norm, w_qkv, w_attn_out, pool_norm, w_pool_in, w_pool_group, pool_scale, ffn_norm, w_ffn_gate_up, w_ffn_down, final_norm, loss_target, m_attn_norm, m_w_qkv, m_w_attn_out, m_pool_norm, m_w_pool_in, m_w_pool_group, m_pool_scale, m_ffn_norm, m_w_ffn_gate_up, m_w_ffn_down, m_final_norm, v_attn_norm, v_w_qkv, v_w_attn_out, v_pool_norm, v_w_pool_in, v_w_pool_group, v_pool_scale, v_ffn_norm, v_w_ffn_gate_up, v_w_ffn_down, v_final_norm):
    D = D_MODEL
    chip = 2 * lax.axis_index("x") + lax.axis_index("y")
    core = lax.axis_index("c")
    pg_rows = w_pool_group.shape[2]

    chip_arr = chip.astype(jnp.int32).reshape(1)
    core_arr = core.astype(jnp.int32).reshape(1)

    pieces = _cast_qkv("cast_qkv", w_qkv, chip_arr)
    bufs = dict(zip(["wq_own"] + ["wq%d" % p for p in range(QKV_PIECES)], pieces))
    shards = [(w_attn_out, 0, BF16), (w_pool_in, 0, BF16), (w_pool_group.reshape(1, 4 * pg_rows, POOL_DIM), 0, BF16),
              (w_ffn_gate_up, 0, BF16), (w_ffn_gate_up, 1, BF16), (w_ffn_down, 0, BF16), (w_ffn_down, 1, BF16),
              (pool_norm[None], 0, F32), (pool_scale[None], 0, F32)]
    bufs.update(zip(TENSORS[1:] + ("pool_norm", "pool_scale"), _cast_many("cast_rest", shards, chip_arr)))

    as_block = lambda a: a.reshape((-1,) + a.shape[-2:]) if a.ndim == 3 else a.reshape(1, -1, a.shape[-1])
    owned = dict(qkv=(w_qkv, m_w_qkv, v_w_qkv), o=(w_attn_out, m_w_attn_out, v_w_attn_out),
                 p=(w_pool_in, m_w_pool_in, v_w_pool_in), pg=(w_pool_group, m_w_pool_group, v_w_pool_group),
                 gu=(w_ffn_gate_up, m_w_ffn_gate_up, v_w_ffn_gate_up), d=(w_ffn_down, m_w_ffn_down, v_w_ffn_down))
    comm = _MeshComm(bufs, {k: as_block(wmv[0]).shape for k, wmv in owned.items()}, chip_arr, core_arr, pg_rows)
    loss_part, grad_x, small = _local_step(x[0], loss_target[0], attn_norm, ffn_norm, final_norm.reshape(1, D), comm)

    rows = jnp.concatenate([small["attn"], small["ffn0"], small["ffn1"], small["final"], small["pool"],
                            small["scale"], jnp.pad(loss_part, ((0, 0), (0, D - 1))), jnp.zeros((1, D), F32)], axis=0)
    all_rows = _gather_small("gather_gain_grads", rows)
    tot = _sum_leading("sum_gain_grads", all_rows, F32)
    loss = tot[6, 0]
    g_attn_norm = tot[0:1]
    g_ffn_norm = tot[1:3]
    g_final_norm = tot[3]
    g_pool_norm = lax.dynamic_slice(tot, (4, chip * POOL_DIM), (1, POOL_DIM))
    g_pool_scale = lax.dynamic_slice(tot, (5, chip * POOL_DIM), (1, POOL_DIM))

    as_rows = lambda a: a.reshape(-1, a.shape[-1])
    res = _adamw("adamw_weights", [(as_rows(w), as_rows(comm.blocks[k]), as_rows(m), as_rows(v))
                                   for k, (w, m, v) in owned.items()])
    updated = {k: [a.reshape(owned[k][0].shape) for a in r] for k, r in zip(owned, res)}
    gains = [(attn_norm, g_attn_norm, m_attn_norm, v_attn_norm), (pool_norm, g_pool_norm, m_pool_norm, v_pool_norm),
             (pool_scale, g_pool_scale, m_pool_scale, v_pool_scale), (ffn_norm, g_ffn_norm, m_ffn_norm, v_ffn_norm),
             (final_norm, g_final_norm, m_final_norm, v_final_norm)]
    small_res = _adamw_small("adamw_gains", [tuple(as_rows(a) for a in item) for item in gains])
    small_res = [[a.reshape(item[0].shape) for a in r] for r, item in zip(small_res, gains)]
    results = [small_res[0], updated["qkv"], updated["o"], small_res[1], updated["p"], updated["pg"], small_res[2],
               small_res[3], updated["gu"], updated["d"], small_res[4]]
    grads = [r[0] for r in results]
    deltas = [r[1] for r in results]
    new_m = [r[2] for r in results]
    new_v = [r[3] for r in results]
    return (loss, grad_x[None], *grads, *deltas, *new_m, *new_v)
```

```python
import jax
import jax.numpy as jnp
from jax import lax
from jax.experimental import pallas as pl
from jax.experimental.pallas import tpu as pltpu

F32 = jnp.float32
BF16 = jnp.bfloat16
MESH = pl.DeviceIdType.MESH

D_MODEL = 1024
N_HEADS = 8
HEAD_DIM = 128
Q_BLOCK = 128
ATTN_DILATIONS = (1, 4, 16)
N_GROUPS = 3
D_FF = 2816
N_CHIPS = 4
FF_SHARD = 2 * D_FF // N_CHIPS
QKV_SHARD = 3 * 3 * D_MODEL // N_CHIPS
QKV_TILE = 768
POOL_DIM = 256
POOL_HALO = 16
RMS_EPS = 1e-6
NEG = -1e30
ADAM_LR = 0.001
ADAM_B1 = 0.9
ADAM_B2 = 0.999
ADAM_EPS = 1e-08
ADAM_WD = 0.01
ADAM_STEP = 10
VMEM_LIMIT = 56 * 1024 * 1024

_DN = {
    "nn": (((1,), (0,)), ((), ())),
    "nt": (((1,), (1,)), ((), ())),
    "tn": (((0,), (0,)), ((), ())),
}


class _Rider:
    def __init__(self, operands, out_shapes, aliases, n_copies, copies=None, start=None, finish=None):
        self.operands = list(operands)
        self.out_shapes = list(out_shapes)
        self.aliases = dict(aliases)
        self.n_copies = n_copies
        self.results = None

        def start_copies(ins, outs, send, recv, base=0):
            for cp in copies(ins, outs, send, recv, base):
                cp.start()

        def wait_copies(ins, outs, send, recv, base=0):
            for cp in copies(ins, outs, send, recv, base):
                cp.wait()

        self.start = start or start_copies
        self.finish = finish or wait_copies


def _riders(*riders):
    operands, out_shapes, aliases, offsets = [], [], {}, []
    n = 0
    for r in riders:
        offsets.append((len(operands), len(out_shapes), n))
        aliases.update({len(operands) + i: len(out_shapes) + o for i, o in r.aliases.items()})
        operands += r.operands
        out_shapes += r.out_shapes
        n += r.n_copies

    def each(which):
        def go(ins, outs, send, recv, base=0):
            for r, (i0, o0, s0) in zip(riders, offsets):
                getattr(r, which)(ins[i0:i0 + len(r.operands)], outs[o0:o0 + len(r.out_shapes)], send, recv,
                                  base + s0)
        return go

    both = _Rider(operands, out_shapes, aliases, n, start=each("start"), finish=each("finish"))
    both.parts = (riders, offsets)
    return both


_pending_rider = []


def _ride(rider, fn, *args, **kw):
    _pending_rider.append(rider)
    out = fn(*args, **kw)
    assert not _pending_rider
    if hasattr(rider, "parts"):
        for r, (_, o0, _) in zip(*rider.parts):
            r.results = rider.results[o0:o0 + len(r.out_shapes)]
    return out


def _pcall(body, prefetch=0, **kw):
    def build(body, kw):
        if not prefetch:
            return pl.pallas_call(body, **kw)
        kw = dict(kw)
        grid_spec = pltpu.PrefetchScalarGridSpec(
            num_scalar_prefetch=prefetch, grid=kw.pop("grid"), in_specs=kw.pop("in_specs"),
            out_specs=kw.pop("out_specs"), scratch_shapes=kw.pop("scratch_shapes", []))
        return pl.pallas_call(body, grid_spec=grid_spec, **kw)

    if not _pending_rider:
        return build(body, kw)
    rider = _pending_rider.pop()
    grid = kw.get("grid", ())
    in_specs = list(kw.get("in_specs", []))
    single = not isinstance(kw["out_shape"], (list, tuple))
    out_shape = [kw["out_shape"]] if single else list(kw["out_shape"])
    out_specs = [kw["out_specs"]] if single else list(kw["out_specs"])
    scratch = list(kw.get("scratch_shapes", []))
    n_in, n_out, n_scr = len(in_specs), len(out_shape), len(scratch)
    r_in, r_out = len(rider.operands), len(rider.out_shapes)

    def wrapped(*refs):
        scalars, refs = refs[:prefetch], refs[prefetch:]
        ins, rins = refs[:n_in], refs[n_in:n_in + r_in]
        outs = refs[n_in + r_in:n_in + r_in + n_out]
        routs = refs[n_in + r_in + n_out:n_in + r_in + n_out + r_out]
        scr = refs[n_in + r_in + n_out + r_out:n_in + r_in + n_out + r_out + n_scr]
        send, recv = refs[-2:]
        ids = [pl.program_id(a) for a in range(len(grid))]
        first, last = True, True
        for a, pid in enumerate(ids):
            first = jnp.logical_and(first, pid == 0)
            last = jnp.logical_and(last, pid == grid[a] - 1)
        if grid:
            pl.when(first)(lambda: rider.start(rins, routs, send, recv))
        else:
            rider.start(rins, routs, send, recv)
        body(*scalars, *ins, *outs, *scr)
        if grid:
            pl.when(last)(lambda: rider.finish(rins, routs, send, recv))
        else:
            rider.finish(rins, routs, send, recv)

    any_spec = pl.BlockSpec(memory_space=pl.ANY)
    kw2 = dict(kw)
    kw2.update(
        in_specs=in_specs + [any_spec] * r_in, out_specs=out_specs + [any_spec] * r_out,
        out_shape=out_shape + rider.out_shapes,
        scratch_shapes=scratch + [pltpu.SemaphoreType.DMA((rider.n_copies,)),
                                  pltpu.SemaphoreType.DMA((rider.n_copies,))],
        input_output_aliases={**kw.get("input_output_aliases", {}),
                              **{prefetch + n_in + i: n_out + o for i, o in rider.aliases.items()}})
    if grid:
        kw2["compiler_params"] = _params(("arbitrary",) * len(grid))
    call = build(wrapped, kw2)

    def run(*operands):
        res = call(*operands, *rider.operands)
        rider.results = list(res[n_out:])
        return res[0] if single else list(res[:n_out])

    return run


def _params(sem):
    return pltpu.CompilerParams(dimension_semantics=sem, vmem_limit_bytes=VMEM_LIMIT)


def _dot(a, b, mode):
    return lax.dot_general(a, b, _DN[mode], preferred_element_type=F32)


def _mm(name, mode, grid, a, a_spec, b, b_spec, out_shape, o_spec, acc_shape):
    nk = grid[-1]

    def body(a_ref, b_ref, o_ref, *acc):
        part = _dot(a_ref[...].astype(BF16), b_ref[...].astype(BF16), mode)
        if nk == 1:
            o_ref[...] = part.astype(o_ref.dtype)
            return
        acc_ref, = acc
        k = pl.program_id(len(grid) - 1)

        @pl.when(k == 0)
        def _():
            acc_ref[...] = part

        @pl.when(k > 0)
        def _():
            acc_ref[...] += part

        @pl.when(k == nk - 1)
        def _():
            o_ref[...] = acc_ref[...].astype(o_ref.dtype)

    scratch = [] if nk == 1 else [pltpu.VMEM(acc_shape, F32)]
    sem = ("parallel",) * (len(grid) - 1) + ("arbitrary",)
    return _pcall(body, name=name, grid=grid, in_specs=[a_spec, b_spec], out_specs=o_spec, out_shape=out_shape,
                  scratch_shapes=scratch, compiler_params=_params(sem))(a, b)


def _mm_tn(name, a, b, tm, tk):
    T, M = a.shape
    N = b.shape[1]
    return _mm(name, "tn", (M // tm, T // tk), a, pl.BlockSpec((tk, tm), lambda i, k: (k, i)),
               b, pl.BlockSpec((tk, N), lambda i, k: (k, 0)),
               jax.ShapeDtypeStruct((M, N), BF16), pl.BlockSpec((tm, N), lambda i, k: (i, 0)), (tm, N))


def _norm_bwd_rows(xf, gain, dh, dres):
    r = lax.rsqrt(jnp.mean(xf * xf, axis=-1, keepdims=True) + RMS_EPS)
    xh = xf * r
    t = dh * gain
    dx = dres + r * (t - xh * jnp.mean(t * xh, axis=-1, keepdims=True))
    return dx, jnp.sum(dh * xh, axis=0, keepdims=True)


def _accumulate(ref, value, first):
    @pl.when(first)
    def _():
        ref[...] = value

    @pl.when(jnp.logical_not(first))
    def _():
        ref[...] += value


def _rms_bwd_folded(name, x, g, dhf, dres, tb=512):
    S, D = x.shape

    def body(x_ref, g_ref, d0_ref, d1_ref, d2_ref, dres_ref, dx_ref, dg_ref, dh_ref):
        chunks = _lane_chunks(D)
        for c, cols in enumerate(chunks):
            dh_ref[c] = d0_ref[0, :, cols].astype(F32)
        for dil, ref in ((ATTN_DILATIONS[1], d1_ref), (ATTN_DILATIONS[2], d2_ref)):
            n = tb // dil
            for k in range(dil):
                for c, cols in enumerate(chunks):
                    dh_ref[c, _strided_rows(k, n, dil), :] += ref[k, :, cols].astype(F32)
        dh = jnp.concatenate([dh_ref[c] for c in range(len(chunks))], axis=1)
        dx, dg = _norm_bwd_rows(x_ref[...], g_ref[...], dh, dres_ref[...])
        dx_ref[...] = dx
        _accumulate(dg_ref, dg, pl.program_id(0) == 0)

    views, specs = _folded_views(dhf, tb)
    row = pl.BlockSpec((tb, D), lambda i: (i, 0))
    vec = pl.BlockSpec((1, D), lambda i: (0, 0))
    return _pcall(body, name=name, grid=(S // tb,), in_specs=[row, vec] + specs + [row], out_specs=[row, vec],
                  out_shape=[jax.ShapeDtypeStruct((S, D), F32), jax.ShapeDtypeStruct((1, D), F32)],
                  scratch_shapes=[pltpu.VMEM((D // LANES, tb, LANES), F32)],
                  compiler_params=_params(("arbitrary",)))(x, g, *views, dres)


def _proj_res_norm(name, a, w, res, gain, tm=512):
    M, K = a.shape
    D = w.shape[1]

    def body(a_ref, w_ref, res_ref, g_ref, x_ref, h_ref):
        xf = res_ref[...] + _dot(a_ref[...], w_ref[...], "nn")
        x_ref[...] = xf
        r = lax.rsqrt(jnp.mean(xf * xf, axis=-1, keepdims=True) + RMS_EPS)
        h_ref[...] = ((xf * r) * g_ref[...]).astype(h_ref.dtype)

    row = pl.BlockSpec((tm, D), lambda i: (i, 0))
    return _pcall(body, name=name, grid=(M // tm,),
                  in_specs=[pl.BlockSpec((tm, K), lambda i: (i, 0)), pl.BlockSpec((K, D), lambda i: (0, 0)), row,
                            pl.BlockSpec((1, D), lambda i: (0, 0))],
                  out_specs=[row, row],
                  out_shape=[jax.ShapeDtypeStruct((M, D), F32), jax.ShapeDtypeStruct((M, D), BF16)],
                  compiler_params=_params(("parallel",)))(a, w, res, gain)


def _proj_res_loss(name, a, w, res, gain, tgt, tm=512):
    M, K = a.shape
    D = w.shape[1]

    def body(a_ref, w_ref, res_ref, g_ref, t_ref, loss_ref, dx_ref, dxb_ref, dg_ref):
        first = pl.program_id(0) == 0
        xf = res_ref[...] + _dot(a_ref[...], w_ref[...], "nn")
        r = lax.rsqrt(jnp.mean(xf * xf, axis=-1, keepdims=True) + RMS_EPS)
        xh = xf * r
        gv = g_ref[...]
        e = xh * gv - t_ref[...]
        lp = 0.5 * jnp.sum(jnp.mean(e * e, axis=-1, keepdims=True), axis=0, keepdims=True)
        dy = e * (1.0 / D)
        t = dy * gv
        dx = r * (t - xh * jnp.mean(t * xh, axis=-1, keepdims=True))
        dx_ref[...] = dx
        dxb_ref[...] = dx.astype(dxb_ref.dtype)
        _accumulate(dg_ref, jnp.sum(dy * xh, axis=0, keepdims=True), first)
        _accumulate(loss_ref, lp, first)

    row = pl.BlockSpec((tm, D), lambda i: (i, 0))
    vec = pl.BlockSpec((1, D), lambda i: (0, 0))
    return _pcall(body, name=name, grid=(M // tm,),
                  in_specs=[pl.BlockSpec((tm, K), lambda i: (i, 0)), pl.BlockSpec((K, D), lambda i: (0, 0)), row,
                            vec, row],
                  out_specs=[pl.BlockSpec((1, 1), lambda i: (0, 0)), row, row, vec],
                  out_shape=[jax.ShapeDtypeStruct((1, 1), F32), jax.ShapeDtypeStruct((M, D), F32),
                             jax.ShapeDtypeStruct((M, D), BF16), jax.ShapeDtypeStruct((1, D), F32)],
                  compiler_params=_params(("arbitrary",)))(a, w, res, gain, tgt)


def _dh_norm_bwd(name, d, w, x, gain, dres, tm=512):
    M, N = d.shape
    D = w.shape[0]

    def body(d_ref, w_ref, x_ref, g_ref, dres_ref, dx_ref, dxb_ref, dg_ref):
        dh = _dot(d_ref[...].astype(BF16), w_ref[...], "nt")
        dx, dg = _norm_bwd_rows(x_ref[...], g_ref[...], dh, dres_ref[...])
        dx_ref[...] = dx
        dxb_ref[...] = dx.astype(dxb_ref.dtype)
        _accumulate(dg_ref, dg, pl.program_id(0) == 0)

    row = pl.BlockSpec((tm, D), lambda i: (i, 0))
    vec = pl.BlockSpec((1, D), lambda i: (0, 0))
    return _pcall(body, name=name, grid=(M // tm,),
                  in_specs=[pl.BlockSpec((tm, N), lambda i: (i, 0)), pl.BlockSpec((D, N), lambda i: (0, 0)), row, vec,
                            row],
                  out_specs=[row, row, vec],
                  out_shape=[jax.ShapeDtypeStruct((M, D), F32), jax.ShapeDtypeStruct((M, D), BF16),
                             jax.ShapeDtypeStruct((1, D), F32)],
                  compiler_params=_params(("arbitrary",)))(d, w, x, gain, dres)


def _sigmoid(v):
    return 0.5 * jnp.tanh(0.5 * v) + 0.5


def _ffn_up(name, h, w_gu, tm=1024):
    S, D = h.shape
    W = FF_SHARD

    def body(h_ref, wg_ref, wu_ref, gu_ref, act_ref):
        hv = h_ref[...]
        gate = _dot(hv, wg_ref[...], "nn")
        up = _dot(hv, wu_ref[...], "nn")
        gu_ref[0] = gate.astype(gu_ref.dtype)
        gu_ref[1] = up.astype(gu_ref.dtype)
        sig = _sigmoid(gate)
        act_ref[...] = ((gate * sig) * up).astype(act_ref.dtype)

    return _pcall(
        body, name=name, grid=(2, S // tm),
        in_specs=[pl.BlockSpec((tm, D), lambda j, i: (i, 0)),
                  pl.BlockSpec((None, D, W), lambda j, i: (j, 0, 0)),
                  pl.BlockSpec((None, D, W), lambda j, i: (j + 2, 0, 0))],
        out_specs=[pl.BlockSpec((2, tm, W), lambda j, i: (0, i, j)), pl.BlockSpec((tm, W), lambda j, i: (i, j))],
        out_shape=[jax.ShapeDtypeStruct((2, S, D_FF), BF16), jax.ShapeDtypeStruct((S, D_FF), BF16)],
        compiler_params=_params(("parallel", "parallel")))(h, w_gu, w_gu)


def _ffn_down_bwd(name, dx, w_down, gu, tm=1024):
    S, D = dx.shape
    W = FF_SHARD

    def body(dx_ref, w_ref, gu_ref, dgu_ref):
        dact = _dot(dx_ref[...].astype(BF16), w_ref[...], "nt")
        gate = gu_ref[0].astype(F32)
        up = gu_ref[1].astype(F32)
        sig = _sigmoid(gate)
        silu = gate * sig
        dgu_ref[0] = (dact * up * (sig * (1.0 + gate * (1.0 - sig)))).astype(dgu_ref.dtype)
        dgu_ref[1] = (dact * silu).astype(dgu_ref.dtype)

    blk = pl.BlockSpec((2, tm, W), lambda j, i: (0, i, j))
    return _pcall(
        body, name=name, grid=(2, S // tm),
        in_specs=[pl.BlockSpec((tm, D), lambda j, i: (i, 0)), pl.BlockSpec((W, D), lambda j, i: (j, 0)), blk],
        out_specs=blk, out_shape=jax.ShapeDtypeStruct((2, S, D_FF), BF16),
        compiler_params=_params(("parallel", "parallel")))(dx, w_down, gu)


def _ffn_dw_up(name, h, dgu, tk=2048):
    S, D = h.shape
    W = FF_SHARD
    tk = min(tk, S)
    return _mm(name, "tn", (N_CHIPS, S // tk), h, pl.BlockSpec((tk, D), lambda s, k: (k, 0)),
               dgu, pl.BlockSpec((None, tk, W), lambda s, k: (s // 2, k, s % 2)),
               jax.ShapeDtypeStruct((N_CHIPS, D, W), BF16), pl.BlockSpec((None, D, W), lambda s, k: (s, 0, 0)),
               (D, W))


def _ffn_dh(name, dgu, w_gu, x, gain, dres, tm=512):
    S = dgu.shape[1]
    W = FF_SHARD

    def body(a_ref, w_hbm, x_ref, g_ref, dres_ref, dx_ref, dg_ref, w_ref, acat_ref, sems):
        first = pl.program_id(0) == 0

        copies = [pltpu.make_async_copy(w_hbm.at[s], w_ref.at[:, pl.ds(s * W, W)], sems.at[s])
                  for s in range(N_CHIPS)]

        @pl.when(first)
        def _():
            for cp in copies:
                cp.start()

        acat_ref[:, :D_FF] = a_ref[0]
        acat_ref[:, D_FF:] = a_ref[1]

        @pl.when(first)
        def _():
            for cp in copies:
                cp.wait()

        dh = _dot(acat_ref[...], w_ref[...], "nt")
        dx, dg = _norm_bwd_rows(x_ref[...], g_ref[...], dh, dres_ref[...])
        dx_ref[...] = dx
        _accumulate(dg_ref, dg, first)

    row = pl.BlockSpec((tm, D_MODEL), lambda i: (i, 0))
    vec = pl.BlockSpec((1, D_MODEL), lambda i: (0, 0))
    return _pcall(body, name=name, grid=(S // tm,),
                  in_specs=[pl.BlockSpec((2, tm, D_FF), lambda i: (0, i, 0)), pl.BlockSpec(memory_space=pl.ANY),
                            row, vec, row],
                  out_specs=[row, vec],
                  out_shape=[jax.ShapeDtypeStruct((S, D_MODEL), F32), jax.ShapeDtypeStruct((1, D_MODEL), F32)],
                  scratch_shapes=[pltpu.VMEM((D_MODEL, 2 * D_FF), BF16), pltpu.VMEM((tm, 2 * D_FF), BF16),
                                  pltpu.SemaphoreType.DMA((N_CHIPS,))],
                  compiler_params=_params(("arbitrary",)))(dgu, w_gu, x, gain, dres)


FOLD_TOKENS = 1024
LANES = 128


def _lane_chunks(width):
    return [slice(c * LANES, (c + 1) * LANES) for c in range(width // LANES)]


def _strided_rows(r, n, dil):
    return pl.ds(r, n) if dil == 1 else pl.ds(r, n, stride=dil)


def _norm_fold(name, x, gain):
    S, D = x.shape
    chunks = _lane_chunks(D)

    def body(x_ref, g_ref, hf_hbm, xc_ref, hs_ref, sems):
        i = pl.program_id(0)
        xf = x_ref[...]
        inv = lax.rsqrt(jnp.mean(xf * xf, axis=-1, keepdims=True) + RMS_EPS)
        h = (xf * inv) * g_ref[...]
        hs_ref[0] = h.astype(BF16)
        for c, cols in enumerate(chunks):
            xc_ref[c] = h[:, cols]
        copies = []
        for g, dil in enumerate(ATTN_DILATIONS):
            n = FOLD_TOKENS // dil
            for r in range(dil):
                if dil > 1:
                    for c, cols in enumerate(chunks):
                        hs_ref[g, r * n:(r + 1) * n, cols] = xc_ref[c, _strided_rows(r, n, dil), :].astype(BF16)
                cp = pltpu.make_async_copy(hs_ref.at[g, pl.ds(r * n, n)],
                                           hf_hbm.at[g, pl.ds(r * (S // dil) + i * n, n)], sems.at[len(copies)])
                cp.start()
                copies.append(cp)
        for cp in copies:
            cp.wait()

    return _pcall(body, name=name, grid=(S // FOLD_TOKENS,),
                  in_specs=[pl.BlockSpec((FOLD_TOKENS, D), lambda i: (i, 0)), pl.BlockSpec((1, D), lambda i: (0, 0))],
                  out_specs=pl.BlockSpec(memory_space=pl.ANY),
                  out_shape=jax.ShapeDtypeStruct((N_GROUPS, S, D), BF16),
                  scratch_shapes=[pltpu.VMEM((D // LANES, FOLD_TOKENS, LANES), F32),
                                  pltpu.VMEM((N_GROUPS, FOLD_TOKENS, D), BF16),
                                  pltpu.SemaphoreType.DMA((sum(ATTN_DILATIONS),))],
                  compiler_params=_params(("arbitrary",)))(x, gain)


def _qkv_tiles(name, piece, hf, w, qkv, chip, tm=1024):
    S = hf.shape[1]
    per_group = 3 * D_MODEL // QKV_TILE

    def tile(q, c_ref):
        if piece is None:
            return QKV_PIECES * c_ref[0] + q
        return QKV_PIECES * ((c_ref[0] + 1 + q) % N_CHIPS) + piece

    def body(*refs):
        a_ref, w_ref, o_ref = refs[1], refs[2], refs[-1]
        o_ref[...] = _dot(a_ref[...], w_ref[...], "nn").astype(o_ref.dtype)

    if piece is None:
        w_spec = pl.BlockSpec((D_MODEL, QKV_TILE), lambda q, i, c_ref: (0, q))
    else:
        w_spec = pl.BlockSpec((None, D_MODEL, QKV_TILE), lambda q, i, c_ref: ((c_ref[0] + 1 + q) % N_CHIPS, 0, 0))
    in_specs = [pl.BlockSpec((None, tm, D_MODEL), lambda q, i, c_ref: (tile(q, c_ref) // per_group, i, 0)), w_spec]
    operands = [chip, hf, w]
    aliases = {}
    if qkv is not None:
        in_specs.append(pl.BlockSpec(memory_space=pl.ANY))
        operands.append(qkv)
        aliases = {3: 0}
    return _pcall(
        body, prefetch=1, name=name, grid=(N_CHIPS - 1, S // tm), in_specs=in_specs,
        out_specs=pl.BlockSpec((None, tm, QKV_TILE),
                               lambda q, i, c_ref: (tile(q, c_ref) // per_group, i, tile(q, c_ref) % per_group)),
        out_shape=jax.ShapeDtypeStruct((N_GROUPS, S, 3 * D_MODEL), BF16), input_output_aliases=aliases,
        compiler_params=_params(("arbitrary", "arbitrary")))(*operands)


QKV_PIECES = QKV_SHARD // QKV_TILE


def _qkv_dw(name, p, hf, dqkv):
    S = hf.shape[1]
    per_group = 3 * D_MODEL // QKV_TILE
    tile = lambda s: QKV_PIECES * s + p
    return _mm(name, "tn", (N_CHIPS, 1),
               hf, pl.BlockSpec((None, S, D_MODEL), lambda s, k: (tile(s) // per_group, 0, 0)),
               dqkv, pl.BlockSpec((None, S, QKV_TILE), lambda s, k: (tile(s) // per_group, 0, tile(s) % per_group)),
               jax.ShapeDtypeStruct((N_CHIPS, D_MODEL, QKV_TILE), BF16),
               pl.BlockSpec((None, D_MODEL, QKV_TILE), lambda s, k: (s, 0, 0)), None)


def _qkv_dh(name, g, dqkv, w_pieces, dhf, tm=1024):
    S = dqkv.shape[1]
    per_group = 3 * D_MODEL // QKV_TILE

    def body(*refs):
        a_ref, w_hbm = refs[0], refs[1:1 + QKV_PIECES]
        o_ref, w_ref, sems = refs[-3:]

        @pl.when(pl.program_id(0) == 0)
        def _():
            copies = []
            for j in range(per_group):
                t = per_group * g + j
                copies.append(pltpu.make_async_copy(w_hbm[t % QKV_PIECES].at[t // QKV_PIECES],
                                                    w_ref.at[:, pl.ds(j * QKV_TILE, QKV_TILE)], sems.at[j]))
            for cp in copies:
                cp.start()
            for cp in copies:
                cp.wait()

        o_ref[...] = _dot(a_ref[...], w_ref[...], "nt").astype(o_ref.dtype)

    any_spec = pl.BlockSpec(memory_space=pl.ANY)
    in_specs = [pl.BlockSpec((None, tm, 3 * D_MODEL), lambda i: (g, i, 0))] + [any_spec] * QKV_PIECES
    operands = [dqkv] + list(w_pieces)
    aliases = {}
    if dhf is not None:
        in_specs.append(any_spec)
        operands.append(dhf)
        aliases = {1 + QKV_PIECES: 0}
    return _pcall(body, name=name, grid=(S // tm,), in_specs=in_specs,
                  out_specs=pl.BlockSpec((None, tm, D_MODEL), lambda i: (g, i, 0)),
                  out_shape=jax.ShapeDtypeStruct((N_GROUPS, S, D_MODEL), BF16),
                  scratch_shapes=[pltpu.VMEM((D_MODEL, 3 * D_MODEL), BF16), pltpu.SemaphoreType.DMA((per_group,))],
                  input_output_aliases=aliases, compiler_params=_params(("arbitrary",)))(*operands)


def _alibi_coef(g, h):
    vals = [-(2.0 ** (-8.0 * (gg * N_HEADS + h + 1) / (N_GROUPS * N_HEADS))) * ATTN_DILATIONS[gg]
            for gg in range(N_GROUPS)]
    return jnp.where(g == 0, vals[0], jnp.where(g == 1, vals[1], vals[2])).astype(F32)


def _blocks_per_segment(g, S):
    return jnp.right_shift(S // Q_BLOCK, 2 * g)


def _band_bias(pen):
    row = lax.broadcasted_iota(jnp.int32, (Q_BLOCK, 2 * Q_BLOCK), 0)
    col = lax.broadcasted_iota(jnp.int32, (Q_BLOCK, 2 * Q_BLOCK), 1)
    dist = Q_BLOCK + row - col
    valid = jnp.logical_and(dist >= 0, dist <= Q_BLOCK)
    base = jnp.where(valid, jnp.where(col < Q_BLOCK, pen, 0.0), NEG).astype(F32)
    return base, dist.astype(F32)


def _skewed(n_units, stages):
    state = {}
    for step in range(n_units + len(stages) - 1):
        for s, stage in enumerate(stages):
            u = step - s
            if 0 <= u < n_units:
                state[u] = stage(u, state.get(u))
                if s == len(stages) - 1:
                    del state[u]


def _attn_fwd(name, qkv, tq=1024):
    S = qkv.shape[1]
    nqb = tq // Q_BLOCK
    scale = HEAD_DIM ** -0.5

    def body(q_ref, k_ref, kp_ref, v_ref, vp_ref, o_ref, lse_ref, kf_ref, vf_ref):
        g = pl.program_id(0)
        i = pl.program_id(1)
        nb = _blocks_per_segment(g, S)
        kf_ref[:Q_BLOCK] = kp_ref[...]
        kf_ref[Q_BLOCK:] = k_ref[...]
        vf_ref[:Q_BLOCK] = vp_ref[...]
        vf_ref[Q_BLOCK:] = v_ref[...]
        coefs = [_alibi_coef(g, h) for h in range(N_HEADS)]
        units = [(b, h) for b in range(nqb) for h in range(N_HEADS)]
        bias = {}

        def scores(u, _):
            b, h = units[u]
            if b not in bias:
                pen = jnp.where((i * nqb + b) % nb != 0, 0.0, NEG).astype(F32)
                bias.clear()
                bias[b] = _band_bias(pen)
            base, dist = bias[b]
            cols = slice(h * HEAD_DIM, (h + 1) * HEAD_DIM)
            q = q_ref[b * Q_BLOCK:(b + 1) * Q_BLOCK, cols]
            kk = kf_ref[b * Q_BLOCK:(b + 2) * Q_BLOCK, cols]
            return _dot(q, kk, "nt") * scale + (coefs[h] * dist + base)

        def softmax(u, s):
            m = jnp.max(s, axis=-1, keepdims=True)
            p = jnp.exp(s - m)
            den = jnp.sum(p, axis=-1, keepdims=True)
            return (p * (1.0 / den)).astype(BF16), m + jnp.log(den)

        def output(u, st):
            b, h = units[u]
            pn, lse = st
            cols = slice(h * HEAD_DIM, (h + 1) * HEAD_DIM)
            rows = slice(b * Q_BLOCK, (b + 1) * Q_BLOCK)
            o_ref[rows, cols] = _dot(pn, vf_ref[b * Q_BLOCK:(b + 2) * Q_BLOCK, cols], "nn").astype(o_ref.dtype)
            lse_ref[rows, h:h + 1] = lse

        _skewed(len(units), [scores, softmax, output])

    main = lambda c: pl.BlockSpec((None, tq, D_MODEL), lambda g, i: (g, i, c))
    prev = lambda c: pl.BlockSpec((None, Q_BLOCK, D_MODEL), lambda g, i: (g, jnp.maximum(i * nqb - 1, 0), c))
    return _pcall(
        body, name=name, grid=(N_GROUPS, S // tq),
        in_specs=[main(0), main(1), prev(1), main(2), prev(2)],
        out_specs=[pl.BlockSpec((None, tq, D_MODEL), lambda g, i: (g, i, 0)),
                   pl.BlockSpec((None, tq, N_HEADS), lambda g, i: (g, i, 0))],
        out_shape=[jax.ShapeDtypeStruct((N_GROUPS, S, D_MODEL), BF16),
                   jax.ShapeDtypeStruct((N_GROUPS, S, N_HEADS), F32)],
        scratch_shapes=[pltpu.VMEM((tq + Q_BLOCK, D_MODEL), BF16), pltpu.VMEM((tq + Q_BLOCK, D_MODEL), BF16)],
        compiler_params=_params(("parallel", "parallel")))(qkv, qkv, qkv, qkv, qkv)


def _attn_bwd(name, qkv, do, lse, dl, tq=1024):
    S = qkv.shape[1]
    nqb = tq // Q_BLOCK
    n_blocks = S // Q_BLOCK
    scale = HEAD_DIM ** -0.5
    KEYS = 2 * Q_BLOCK

    def body(q_ref, k_ref, v_ref, kp_ref, vp_ref, qn_ref, do_ref, don_ref, lse_ref, lsen_ref, dl_ref, dln_ref,
             out_ref, kf_ref, vf_ref, dk_ref, dv_ref):
        g = pl.program_id(0)
        i = pl.program_id(1)
        nb = _blocks_per_segment(g, S)
        kf_ref[:Q_BLOCK] = kp_ref[...]
        kf_ref[Q_BLOCK:] = k_ref[...]
        vf_ref[:Q_BLOCK] = vp_ref[...]
        vf_ref[Q_BLOCK:] = v_ref[...]
        coefs = [_alibi_coef(g, h) for h in range(N_HEADS)]
        units = [(h, b) for h in range(N_HEADS) for b in range(nqb + 1)]
        bias = {}

        def band(b):
            if b not in bias:
                blk = i * nqb + b
                ok = blk % nb != 0
                if b == nqb:
                    ok = jnp.logical_and(ok, blk < n_blocks)
                bias[b] = _band_bias(jnp.where(ok, 0.0, NEG).astype(F32))
            return bias[b]

        def operands(h, b):
            cols = slice(h * HEAD_DIM, (h + 1) * HEAD_DIM)
            if b == nqb:
                keys = slice(tq, tq + Q_BLOCK)
                return (qn_ref[:, cols], don_ref[:, cols], lsen_ref[:, h:h + 1], dln_ref[:, h:h + 1],
                        kf_ref[keys, cols], vf_ref[keys, cols])
            rows = slice(b * Q_BLOCK, (b + 1) * Q_BLOCK)
            keys = slice(b * Q_BLOCK, b * Q_BLOCK + KEYS)
            return (q_ref[rows, cols], do_ref[rows, cols], lse_ref[rows, h:h + 1], dl_ref[rows, h:h + 1],
                    kf_ref[keys, cols], vf_ref[keys, cols])

        def probs(u, _):
            h, b = units[u]
            q, do_b, lse_b, dl_b, kk, vv = operands(h, b)
            base, dist = band(b)
            if b == nqb:
                base, dist = base[:, :Q_BLOCK], dist[:, :Q_BLOCK]
            p = jnp.exp(_dot(q, kk, "nt") * scale + (coefs[h] * dist + base) - lse_b)
            return p, _dot(do_b, vv, "nt")

        def dscores(u, st):
            h, b = units[u]
            p, dp = st
            dl_b = operands(h, b)[3]
            return ((p * (dp - dl_b)) * scale).astype(BF16), p.astype(BF16)

        def grads(u, st):
            h, b = units[u]
            ds, pb = st
            q, do_b, _, _, kk, _ = operands(h, b)
            cols = slice(h * HEAD_DIM, (h + 1) * HEAD_DIM)
            if b < nqb:
                out_ref[b * Q_BLOCK:(b + 1) * Q_BLOCK, cols] = _dot(ds, kk, "nn").astype(out_ref.dtype)
            if b == 0:
                dk_ref[:Q_BLOCK] = _dot(ds[:, Q_BLOCK:], q, "tn")
                dv_ref[:Q_BLOCK] = _dot(pb[:, Q_BLOCK:], do_b, "tn")
                return None
            dk = _dot(ds, q, "tn")
            dv = _dot(pb, do_b, "tn")
            before = slice((b - 1) * Q_BLOCK, b * Q_BLOCK)
            dk_ref[before] += dk[:Q_BLOCK]
            dv_ref[before] += dv[:Q_BLOCK]
            if b < nqb:
                own = slice(b * Q_BLOCK, (b + 1) * Q_BLOCK)
                dk_ref[own] = dk[Q_BLOCK:]
                dv_ref[own] = dv[Q_BLOCK:]
            else:
                out_ref[:, D_MODEL + h * HEAD_DIM:D_MODEL + (h + 1) * HEAD_DIM] = dk_ref[...].astype(out_ref.dtype)
                out_ref[:, 2 * D_MODEL + h * HEAD_DIM:2 * D_MODEL + (h + 1) * HEAD_DIM] = (
                    dv_ref[...].astype(out_ref.dtype))
            return None

        _skewed(len(units), [probs, dscores, grads])

    nxt_blk = lambda i: jnp.minimum((i + 1) * nqb, n_blocks - 1)
    main = lambda c: pl.BlockSpec((None, tq, D_MODEL), lambda g, i: (g, i, c))
    prev = lambda c: pl.BlockSpec((None, Q_BLOCK, D_MODEL), lambda g, i: (g, jnp.maximum(i * nqb - 1, 0), c))
    row = pl.BlockSpec((None, tq, D_MODEL), lambda g, i: (g, i, 0))
    row_n = pl.BlockSpec((None, Q_BLOCK, D_MODEL), lambda g, i: (g, nxt_blk(i), 0))
    col = pl.BlockSpec((None, tq, N_HEADS), lambda g, i: (g, i, 0))
    col_n = pl.BlockSpec((None, Q_BLOCK, N_HEADS), lambda g, i: (g, nxt_blk(i), 0))
    return _pcall(
        body, name=name, grid=(N_GROUPS, S // tq),
        in_specs=[main(0), main(1), main(2), prev(1), prev(2), row_n, row, row_n, col, col_n, col, col_n],
        out_specs=pl.BlockSpec((None, tq, 3 * D_MODEL), lambda g, i: (g, i, 0)),
        out_shape=jax.ShapeDtypeStruct((N_GROUPS, S, 3 * D_MODEL), BF16),
        scratch_shapes=[pltpu.VMEM((tq + Q_BLOCK, D_MODEL), BF16), pltpu.VMEM((tq + Q_BLOCK, D_MODEL), BF16),
                        pltpu.VMEM((tq, HEAD_DIM), F32), pltpu.VMEM((tq, HEAD_DIM), F32)],
        compiler_params=_params(("parallel", "parallel")))(qkv, qkv, qkv, qkv, qkv, qkv, do, do, lse, lse, dl, dl)


def _group_weights(lse_ref):
    l0, l1, l2 = lse_ref[0], lse_ref[1], lse_ref[2]
    m = jnp.maximum(jnp.maximum(l0, l1), l2)
    e = [jnp.exp(l0 - m), jnp.exp(l1 - m), jnp.exp(l2 - m)]
    den = e[0] + e[1] + e[2]
    return [ei / den for ei in e]


MERGE_TOKENS = 512


def _folded_views(a, tb):
    _, S, C = a.shape
    views, specs = [], []
    for g, dil in enumerate(ATTN_DILATIONS):
        views.append(a.reshape(N_GROUPS * dil, S // dil, C))
        specs.append(pl.BlockSpec((dil, tb // dil, C), lambda i, g=g: (g, i, 0)))
    return views, specs


def _unfold_into(dst_ref, folded_refs):
    for g, (dil, ref) in enumerate(zip(ATTN_DILATIONS, folded_refs)):
        n = ref.shape[1]
        for r in range(dil):
            for c, cols in enumerate(_lane_chunks(ref.shape[2])):
                dst_ref[g, c, _strided_rows(r, n, dil), :] = ref[r, :, cols].astype(F32)


def _merge_fwd(name, o, lse, tb=MERGE_TOKENS):
    S = o.shape[1]

    def body(o0_ref, o1_ref, o2_ref, lse_ref, out_ref, o_ref):
        _unfold_into(o_ref, (o0_ref, o1_ref, o2_ref))
        w = _group_weights(lse_ref)
        for h in range(N_HEADS):
            cols = slice(h * HEAD_DIM, (h + 1) * HEAD_DIM)
            acc = w[0][:, h:h + 1] * o_ref[0, h]
            for gg in range(1, N_GROUPS):
                acc = acc + w[gg][:, h:h + 1] * o_ref[gg, h]
            out_ref[:, cols] = acc.astype(out_ref.dtype)

    views, specs = _folded_views(o, tb)
    return _pcall(body, name=name, grid=(S // tb,),
                  in_specs=specs + [pl.BlockSpec((N_GROUPS, tb, N_HEADS), lambda i: (0, i, 0))],
                  out_specs=pl.BlockSpec((tb, D_MODEL), lambda i: (i, 0)),
                  out_shape=jax.ShapeDtypeStruct((S, D_MODEL), BF16),
                  scratch_shapes=[pltpu.VMEM((N_GROUPS, N_HEADS, tb, HEAD_DIM), F32)],
                  compiler_params=_params(("parallel",)))(*views, lse)


def _merge_bwd(name, dx, w_out, o, lse, tb=MERGE_TOKENS):
    S = o.shape[1]
    n_chunks = sum(ATTN_DILATIONS)

    def body(dx_ref, w_ref, o0_ref, o1_ref, o2_ref, lse_ref, do_hbm, dl_ref, o_ref, dt_ref, df_ref, d_ref, sems):
        i = pl.program_id(0)
        d_ref[...] = _dot(dx_ref[...].astype(BF16), w_ref[...], "nt")
        _unfold_into(o_ref, (o0_ref, o1_ref, o2_ref))
        w = _group_weights(lse_ref)
        for h in range(N_HEADS):
            cols = slice(h * HEAD_DIM, (h + 1) * HEAD_DIM)
            dv = d_ref[:, cols]
            merged = w[0][:, h:h + 1] * o_ref[0, h]
            for gg in range(1, N_GROUPS):
                merged = merged + w[gg][:, h:h + 1] * o_ref[gg, h]
            dsum = jnp.sum(dv * merged, axis=-1, keepdims=True)
            for gg in range(N_GROUPS):
                wg = w[gg][:, h:h + 1]
                dt_ref[gg, h] = wg * dv
                dl_ref[gg, :, h:h + 1] = wg * dsum
        copies = []
        for gg, dil in enumerate(ATTN_DILATIONS):
            n = tb // dil
            for r in range(dil):
                for h, cols in enumerate(_lane_chunks(D_MODEL)):
                    df_ref[gg, r * n:(r + 1) * n, cols] = (
                        dt_ref[gg, h, _strided_rows(r, n, dil), :].astype(df_ref.dtype))
                cp = pltpu.make_async_copy(df_ref.at[gg, pl.ds(r * n, n)],
                                           do_hbm.at[gg, pl.ds(r * (S // dil) + i * n, n)], sems.at[len(copies)])
                cp.start()
                copies.append(cp)
        for cp in copies:
            cp.wait()

    views, specs = _folded_views(o, tb)
    small = pl.BlockSpec((N_GROUPS, tb, N_HEADS), lambda i: (0, i, 0))
    return _pcall(body, name=name, grid=(S // tb,),
                  in_specs=[pl.BlockSpec((tb, D_MODEL), lambda i: (i, 0)),
                            pl.BlockSpec((D_MODEL, D_MODEL), lambda i: (0, 0))] + specs + [small],
                  out_specs=[pl.BlockSpec(memory_space=pl.ANY), small],
                  out_shape=[jax.ShapeDtypeStruct((N_GROUPS, S, D_MODEL), BF16),
                             jax.ShapeDtypeStruct((N_GROUPS, S, N_HEADS), F32)],
                  scratch_shapes=[pltpu.VMEM((N_GROUPS, N_HEADS, tb, HEAD_DIM), F32),
                                  pltpu.VMEM((N_GROUPS, N_HEADS, tb, HEAD_DIM), F32),
                                  pltpu.VMEM((N_GROUPS, tb, D_MODEL), BF16), pltpu.VMEM((tb, D_MODEL), F32),
                                  pltpu.SemaphoreType.DMA((n_chunks,))],
                  compiler_params=_params(("arbitrary",)))(dx, w_out, *views, lse)


def _shift_rows(a, k):
    return pltpu.roll(a, k % a.shape[0], axis=0)


def _pool_level(g, levels):
    return jnp.where(g == 0, levels[0], jnp.where(g == 1, levels[1], jnp.where(g == 2, levels[2], levels[3])))


def _pool_fwd(name, h, w_in, w_group, scale, x_in, gain_next, tr=1024):
    S, D = h.shape
    H = POOL_HALO
    n_groups = D // POOL_DIM

    def body(h_ref, hh_ref, wi_ref, w_ref, sc_ref, x_ref, gn_ref, y_ref, z_ref, xo_ref, hn_ref, xs_ref):
        i = pl.program_id(0)
        g = pl.program_id(1)
        uv = _dot(h_ref[...], wi_ref[...], "nn")
        halo = jnp.where(i > 0, _dot(hh_ref[...], wi_ref[...], "nn"), 0.0)
        s = jnp.concatenate([halo, uv], axis=0)
        levels = []
        for k in (1, 2, 4, 8):
            s = s + _shift_rows(s, k)
            levels.append(s[H:])
        t = i * tr + lax.broadcasted_iota(jnp.int32, (tr, 1), 0)
        cnt = jnp.minimum(t + 1, jnp.left_shift(2, g)).astype(F32)
        y = _pool_level(g, levels) / cnt - uv
        yb = y.astype(BF16)
        z = _dot(yb, w_ref[...], "nn")
        y_ref[...] = yb
        z_ref[...] = z.astype(z_ref.dtype)
        x_out = x_ref[...] + z * sc_ref[...]
        xo_ref[...] = x_out
        xs_ref[g] = x_out

        @pl.when(g == n_groups - 1)
        def _():
            xf = jnp.concatenate([xs_ref[k] for k in range(n_groups)], axis=1)
            r = lax.rsqrt(jnp.mean(xf * xf, axis=-1, keepdims=True) + RMS_EPS)
            hn_ref[...] = ((xf * r) * gn_ref[...]).astype(hn_ref.dtype)

    blk = pl.BlockSpec((tr, POOL_DIM), lambda i, g: (i, g))
    row = pl.BlockSpec((tr, D), lambda i, g: (i, 0))
    return _pcall(
        body, name=name, grid=(S // tr, n_groups),
        in_specs=[row, pl.BlockSpec((H, D), lambda i, g: (jnp.maximum(i * (tr // H) - 1, 0), 0)),
                  pl.BlockSpec((D, POOL_DIM), lambda i, g: (0, g)),
                  pl.BlockSpec((None, POOL_DIM, POOL_DIM), lambda i, g: (g, 0, 0)),
                  pl.BlockSpec((1, POOL_DIM), lambda i, g: (0, g)), blk, pl.BlockSpec((1, D), lambda i, g: (0, 0))],
        out_specs=[blk, blk, blk, row],
        out_shape=[jax.ShapeDtypeStruct((S, D), BF16), jax.ShapeDtypeStruct((S, D), BF16),
                   jax.ShapeDtypeStruct((S, D), F32), jax.ShapeDtypeStruct((S, D), BF16)],
        scratch_shapes=[pltpu.VMEM((n_groups, tr, POOL_DIM), F32)],
        compiler_params=_params(("parallel", "arbitrary")))(h, h, w_in, w_group, scale, x_in, gain_next)


def _pool_bwd(name, d_out, z, y, scale, w_group, tr=1024):
    S, D = d_out.shape
    H = POOL_HALO

    def body(d_ref, dn_ref, z_ref, y_ref, sc_ref, w_ref, du_ref, dw_ref, dsc_ref):
        g = pl.program_id(0)
        i = pl.program_id(1)
        dv = d_ref[...]
        dz = jnp.concatenate([dv, dn_ref[...]], axis=0) * sc_ref[...]
        dzb = dz.astype(BF16)
        dy = _dot(dzb, w_ref[...], "nt")
        t = i * tr + lax.broadcasted_iota(jnp.int32, (tr + H, 1), 0)
        cnt = jnp.minimum(t + 1, jnp.left_shift(2, g)).astype(F32)
        s = jnp.where(t < S, dy / cnt, 0.0)
        levels = []
        for k in (1, 2, 4, 8):
            s = s + _shift_rows(s, -k)
            levels.append(s[:tr])
        du_ref[...] = (_pool_level(g, levels) - dy[:tr]).astype(du_ref.dtype)
        dw = _dot(y_ref[...], dzb[:tr], "tn")
        dsc = jnp.sum(dv * z_ref[...].astype(F32), axis=0, keepdims=True)

        @pl.when(i == 0)
        def _():
            dw_ref[...] = dw
            dsc_ref[...] = dsc

        @pl.when(i > 0)
        def _():
            dw_ref[...] += dw
            dsc_ref[...] += dsc

    blk = pl.BlockSpec((tr, POOL_DIM), lambda g, i: (i, g))
    vec = pl.BlockSpec((1, POOL_DIM), lambda g, i: (0, g))
    mat = pl.BlockSpec((None, POOL_DIM, POOL_DIM), lambda g, i: (g, 0, 0))
    nxt = pl.BlockSpec((H, POOL_DIM), lambda g, i: (jnp.minimum((i + 1) * (tr // H), S // H - 1), g))
    return _pcall(
        body, name=name, grid=(D // POOL_DIM, S // tr), in_specs=[blk, nxt, blk, blk, vec, mat],
        out_specs=[blk, mat, vec],
        out_shape=[jax.ShapeDtypeStruct((S, D), BF16), jax.ShapeDtypeStruct((D // POOL_DIM, POOL_DIM, POOL_DIM), F32),
                   jax.ShapeDtypeStruct((1, D), F32)],
        compiler_params=_params(("parallel", "arbitrary")))(d_out, d_out, z, y, scale, w_group)


def _row_tile(rows, cols, target_bytes=1 << 20):
    best = rows
    for tr in range(8, rows + 1, 8):
        if rows % tr == 0 and tr * cols * 4 <= target_bytes:
            best = tr
    return best if best * cols * 4 <= 4 * target_bytes else rows


def _adamw_step(w, g, m, v):
    m2 = ADAM_B1 * m + (1.0 - ADAM_B1) * g
    v2 = ADAM_B2 * v + (1.0 - ADAM_B2) * (g * g)
    m_hat = m2 / (1.0 - ADAM_B1 ** ADAM_STEP)
    v_hat = v2 / (1.0 - ADAM_B2 ** ADAM_STEP)
    return -ADAM_LR * (m_hat / (jnp.sqrt(v_hat) + ADAM_EPS) + ADAM_WD * w), m2, v2


def _adamw_refs(w_ref, g_ref, m_ref, v_ref, go_ref, d_ref, mo_ref, vo_ref):
    gv = g_ref[...]
    d_ref[...], mo_ref[...], vo_ref[...] = _adamw_step(w_ref[...], gv, m_ref[...], v_ref[...])
    go_ref[...] = gv


def _adamw(name, items, steps=16):
    n = len(items)

    def body(*refs):
        for t in range(n):
            _adamw_refs(*refs[4 * t:4 * t + 4], *refs[4 * n + 4 * t:4 * n + 4 * t + 4])

    specs, shapes = [], []
    for w, _, _, _ in items:
        R, C = w.shape
        assert R % (8 * steps) == 0, (name, w.shape)
        specs += [pl.BlockSpec((R // steps, C), lambda i: (i, 0))] * 4
        shapes += [jax.ShapeDtypeStruct((R, C), F32)] * 4
    res = _pcall(body, name=name, grid=(steps,), in_specs=specs, out_specs=specs, out_shape=shapes,
                 compiler_params=_params(("parallel",)))(*[a for item in items for a in item])
    return [res[4 * t:4 * t + 4] for t in range(n)]


def _adamw_small(name, items):
    n = len(items)

    def body(*refs):
        for t in range(n):
            _adamw_refs(*refs[4 * t:4 * t + 4], *refs[4 * n + 4 * t:4 * n + 4 * t + 4])

    specs, shapes = [], []
    for w, _, _, _ in items:
        specs += [pl.BlockSpec(w.shape, lambda i: (0, 0))] * 4
        shapes += [jax.ShapeDtypeStruct(w.shape, F32)] * 4
    res = _pcall(body, name=name, grid=(1,), in_specs=specs, out_specs=specs, out_shape=shapes,
                 compiler_params=_params(("arbitrary",)))(*[a for item in items for a in item])
    return [res[4 * t:4 * t + 4] for t in range(n)]


def _sum_leading(name, a, out_dtype):
    n, R, C = a.shape
    tr = _row_tile(R, C) if R % 16 == 0 else R
    if tr % 16:
        tr = R

    def body(a_ref, o_ref):
        acc = a_ref[0].astype(F32)
        for j in range(1, n):
            acc = acc + a_ref[j].astype(F32)
        o_ref[...] = acc.astype(o_ref.dtype)

    return _pcall(body, name=name, grid=(R // tr,), in_specs=[pl.BlockSpec((n, tr, C), lambda i: (0, i, 0))],
                  out_specs=pl.BlockSpec((tr, C), lambda i: (i, 0)), out_shape=jax.ShapeDtypeStruct((R, C), out_dtype),
                  compiler_params=_params(("parallel",)))(a)


def _cast_many(name, items, chip, steps=4):
    tiles = []
    for w, _, dtype in items:
        R = w.shape[1]
        nt = steps if R % (steps * 16) == 0 else 1
        tiles.append((nt, R // nt))

    def body(c_ref, *refs):
        i = pl.program_id(0)
        for t, (nt, _) in enumerate(tiles):
            w_ref, o_ref = refs[t], refs[len(items) + t]

            def cast(w_ref=w_ref, o_ref=o_ref):
                o_ref[...] = w_ref[...].astype(o_ref.dtype)

            if nt == steps:
                cast()
            else:
                pl.when(i < nt)(cast)

    in_specs, out_specs, out_shape = [], [], []
    for (w, layer, dtype), (nt, tr) in zip(items, tiles):
        C = w.shape[2]
        in_specs.append(pl.BlockSpec((None, tr, C), lambda i, c_ref, l=layer, nt=nt: (l, jnp.minimum(i, nt - 1), 0)))
        out_specs.append(pl.BlockSpec((None, tr, C), lambda i, c_ref, nt=nt: (c_ref[0], jnp.minimum(i, nt - 1), 0)))
        out_shape.append(jax.ShapeDtypeStruct((N_CHIPS, w.shape[1], C), dtype))
    return _pcall(body, prefetch=1, name=name, grid=(steps,), in_specs=in_specs, out_specs=out_specs,
                  out_shape=out_shape, compiler_params=_params(("arbitrary",)))(chip, *[w for w, _, _ in items])


def _cast_qkv(name, w, chip, tr=256):
    _, R, C = w.shape

    def body(c_ref, w_ref, own_ref, *piece_refs):
        v = w_ref[...].astype(BF16)
        own_ref[...] = v
        for p, ref in enumerate(piece_refs):
            ref[...] = v[:, p * QKV_TILE:(p + 1) * QKV_TILE]

    piece = pl.BlockSpec((None, tr, QKV_TILE), lambda i, c_ref: (c_ref[0], i, 0))
    return _pcall(body, prefetch=1, name=name, grid=(R // tr,),
                  in_specs=[pl.BlockSpec((None, tr, C), lambda i, c_ref: (0, i, 0))],
                  out_specs=[pl.BlockSpec((tr, C), lambda i, c_ref: (i, 0))] + [piece] * QKV_PIECES,
                  out_shape=[jax.ShapeDtypeStruct((R, C), BF16)]
                  + [jax.ShapeDtypeStruct((N_CHIPS, R, QKV_TILE), BF16)] * QKV_PIECES,
                  compiler_params=_params(("parallel",)))(chip, w)


def _owner_sum(name, mine, landed, shape, chip, core, out, layer, col):
    _, half, C = mine.shape
    k, _ = col
    tr = _row_tile(half, C)
    if tr % 16:
        tr = half
    nrt = half // tr

    def body(ch_ref, co_ref, mine_ref, landed_ref, *rest):
        g = mine_ref[...].astype(F32)
        for j in range(3):
            g = g + landed_ref[j].astype(F32)
        rest[-1][...] = g

    piece = pl.BlockSpec((None, tr, C), lambda i, ch, co: (layer, co[0] * nrt + i, k))
    in_specs = [pl.BlockSpec((None, tr, C), lambda i, ch, co: (ch[0], i, 0)),
                pl.BlockSpec((3, tr, C), lambda i, ch, co: (0, i, 0))]
    operands = [chip, core, mine, landed]
    aliases = {}
    if out is not None:
        in_specs.append(ANY)
        operands.append(out)
        aliases = {4: 0}
    return _pcall(body, prefetch=2, name=name, grid=(nrt,), in_specs=in_specs, out_specs=piece,
                  out_shape=jax.ShapeDtypeStruct(shape, F32), input_output_aliases=aliases,
                  compiler_params=_params(("parallel",)))(*operands)


def _add_my_half(name, ps, qs, core):
    n = len(ps)

    def body(c_ref, *refs):
        for t in range(n):
            p_ref, q_ref, o_ref = refs[t], refs[n + t], refs[2 * n + t]
            o_ref[...] = (p_ref[...].astype(F32) + q_ref[...].astype(F32)).astype(o_ref.dtype)

    halves = [(p.shape[1] // 2, p.shape[2]) for p in ps]
    mine = [pl.BlockSpec((None, h, c), lambda s, c_ref: (s, c_ref[0], 0)) for h, c in halves]
    whole = [pl.BlockSpec((None, h, c), lambda s, c_ref: (s, 0, 0)) for h, c in halves]
    return _pcall(body, prefetch=1, name=name, grid=(N_CHIPS,), in_specs=mine + whole, out_specs=whole,
                  out_shape=[jax.ShapeDtypeStruct((N_CHIPS, h, c), BF16) for h, c in halves],
                  compiler_params=_params(("parallel",)))(core, *ps, *qs)


ANY = pl.BlockSpec(memory_space=pl.ANY)


def _place():
    x, y, c = lax.axis_index("x"), lax.axis_index("y"), lax.axis_index("c")
    other_chips = [(1 - x, y), (x, 1 - y), (1 - x, 1 - y)]
    return x, y, c, other_chips


def _remote(src, dst, send, recv, k, to):
    return pltpu.make_async_remote_copy(src_ref=src, dst_ref=dst, send_sem=send.at[k], recv_sem=recv.at[k],
                                        device_id=to, device_id_type=MESH)


def _in_place(bufs):
    return dict(operands=bufs, out_shapes=[jax.ShapeDtypeStruct(b.shape, b.dtype) for b in bufs],
                aliases={t: t for t in range(len(bufs))})


def _gather_rider(bufs, jobs):
    def copies(ins, outs, send, recv, base=0):
        x, y, c, chips = _place()
        out = []
        for t, over_ici, rows in jobs:
            ref = outs[t]
            if rows is not None:
                a, b, n = rows
                half = ref.shape[1] // 2
                rows = pl.ds(c * half + a * half // n, (b - a) * half // n)
            for px, py in chips:
                slot = 2 * x + y if over_ici else 2 * px + py
                piece = ref.at[slot] if rows is None else ref.at[slot, rows]
                out.append(_remote(piece, piece, send, recv, base + len(out), (px, py, c) if over_ici else (x, y, 1 - c)))
        return out

    return _Rider(n_copies=3 * len(jobs), copies=copies, **_in_place(bufs))


def _gather_and_pass_rider(buf):
    half = buf.shape[1] // 2

    def pieces(outs):
        x, y, c, chips = _place()
        rows = lambda core: pl.ds(core * half, half)
        mine = outs[0].at[2 * x + y, rows(c)]
        landed = [outs[0].at[2 * px + py, rows(c)] for px, py in chips]
        theirs = [outs[0].at[2 * px + py, rows(1 - c)] for px, py in chips]
        return (x, y, c, chips), mine, landed, theirs

    def start(ins, outs, send, recv, base=0):
        (x, y, c, chips), mine, _, _ = pieces(outs)
        for j, (px, py) in enumerate(chips):
            _remote(mine, mine, send, recv, base + j, (px, py, c)).start()

    def finish(ins, outs, send, recv, base=0):
        (x, y, c, chips), mine, landed, theirs = pieces(outs)
        sibling = (x, y, 1 - c)
        passed = []
        for j, (px, py) in enumerate(chips):
            _remote(landed[j], landed[j], send, recv, base + j, (px, py, c)).wait_recv()
            passed.append(_remote(landed[j], landed[j], send, recv, base + 3 + j, sibling))
            passed[-1].start()
        for j in range(3):
            _remote(theirs[j], theirs[j], send, recv, base + 3 + j, sibling).wait_recv()
        for j, (px, py) in enumerate(chips):
            _remote(mine, mine, send, recv, base + j, (px, py, c)).wait_send()
            passed[j].wait_send()

    return _Rider(n_copies=6, start=start, finish=finish, **_in_place([buf]))


def _swap_halves_rider(parts):
    n = len(parts)

    def copies(ins, outs, send, recv, base=0):
        x, y, c, _ = _place()
        out = []
        for t in range(n):
            half = ins[t].shape[1] // 2
            out.append(_remote(ins[t].at[:, pl.ds((1 - c) * half, half)], outs[t], send, recv, base + t,
                               (x, y, 1 - c)))
        return out

    shapes = [jax.ShapeDtypeStruct((p.shape[0], p.shape[1] // 2, p.shape[2]), p.dtype) for p in parts]
    return _Rider(parts, shapes, {}, n, copies)


def _scatter_rider(sums, landed, relations):
    n = len(sums)
    kept = [t for t in range(n) if landed[t] is not None]

    def copies(ins, outs, send, recv, base=0):
        x, y, c, chips = _place()
        out = []
        for t in range(n):
            for j in relations[t]:
                px, py = chips[j]
                out.append(_remote(ins[t].at[2 * px + py], outs[t].at[j], send, recv, base + len(out), (px, py, c)))
        return out

    shapes = [jax.ShapeDtypeStruct((3,) + s.shape[1:], s.dtype) for s in sums]
    return _Rider(list(sums) + [landed[t] for t in kept], shapes, {n + k: t for k, t in enumerate(kept)},
                  sum(len(r) for r in relations), copies)


def _share_rider(blocks, pieces):
    def copies(ins, outs, send, recv, base=0):
        x, y, c, _ = _place()
        out = []
        for k, (t, l, col) in enumerate(pieces):
            ref = outs[t].at[l]
            r2 = ref.shape[0] // 2
            width = ref.shape[1] // col[1]
            piece = ref.at[pl.ds(c * r2, r2), pl.ds(col[0] * width, width)]
            out.append(_remote(piece, piece, send, recv, base + k, (x, y, 1 - c)))
        return out

    return _Rider(n_copies=len(pieces), copies=copies, **_in_place(blocks))


def _gather_small(name, v):
    def body(v_ref, out_ref, send_sem, recv_sem, local_sem):
        x, y, c, _ = _place()
        me = 4 * x + 2 * y + c
        flips = [(fx, fy, fc) for fx in (0, 1) for fy in (0, 1) for fc in (0, 1)][1:]
        local = pltpu.make_async_copy(v_ref, out_ref.at[me], local_sem)
        local.start()
        copies = []
        for j, (fx, fy, fc) in enumerate(flips):
            peer = (x ^ fx, y ^ fy, c ^ fc)
            copies.append(pltpu.make_async_remote_copy(
                src_ref=v_ref, dst_ref=out_ref.at[me], send_sem=send_sem.at[j], recv_sem=recv_sem.at[j],
                device_id=peer, device_id_type=MESH))
        for cp in copies:
            cp.start()
        for j, (fx, fy, fc) in enumerate(flips):
            peer = (x ^ fx, y ^ fy, c ^ fc)
            pltpu.make_async_remote_copy(
                src_ref=v_ref, dst_ref=out_ref.at[4 * peer[0] + 2 * peer[1] + peer[2]], send_sem=send_sem.at[j],
                recv_sem=recv_sem.at[j], device_id=peer, device_id_type=MESH).wait_recv()
        for cp in copies:
            cp.wait_send()
        local.wait()

    return _pcall(body, name=name, in_specs=[ANY], out_specs=ANY,
                  out_shape=jax.ShapeDtypeStruct((8,) + v.shape, v.dtype),
                  scratch_shapes=[pltpu.SemaphoreType.DMA((7,)), pltpu.SemaphoreType.DMA((7,)),
                                  pltpu.SemaphoreType.DMA])(v)


def _fold(a, dil):
    if dil == 1:
        return a
    S, C = a.shape
    return a.reshape(S // dil, dil, C).transpose(1, 0, 2).reshape(S, C)


def _unfold(a, dil):
    if dil == 1:
        return a
    S, C = a.shape
    return a.reshape(dil, S // dil, C).transpose(1, 0, 2).reshape(S, C)


def _fold_groups(a):
    return jnp.stack([_fold(a[g] if a.ndim == 3 else a, d) for g, d in enumerate(ATTN_DILATIONS)])


def _unfold_groups(a):
    return jnp.stack([_unfold(a[g], d) for g, d in enumerate(ATTN_DILATIONS)])


class _NoComm:
    def __init__(self, weights):
        self.weights = weights
        self.grads = {}
        self.chip = jnp.zeros((1,), jnp.int32)

    def w(self, name):
        return self.weights[name]

    def run(self, fn, name, *args, **kw):
        return fn(name, *args, **kw)

    def grad(self, name, value):
        self.grads[name] = value


def _local_step(xs, tgt, attn_norm, ffn_norm, final_norm, comm):
    run = comm.run
    hf = run(_norm_fold, "fold", xs, attn_norm)
    qkv = run(_qkv_tiles, "qkv_own", None, hf, comm.w("wq_own"), None, comm.chip)
    for p in range(QKV_PIECES):
        qkv = run(_qkv_tiles, "qkv_piece%d" % p, p, hf, comm.w("wq%d" % p), qkv, comm.chip)
    o_f, lse_f = run(_attn_fwd, "attn_fwd", qkv)
    lse_t = _unfold_groups(lse_f)
    merged = run(_merge_fwd, "merge_fwd", o_f, lse_t)
    x1, h1 = run(_proj_res_norm, "attn_out", merged, comm.w("o"), xs, ffn_norm[0:1])
    gu0, act0 = run(_ffn_up, "ffn0_up", h1, comm.w("gu0"))
    x2, h2 = run(_proj_res_norm, "ffn0_down", act0, comm.w("d0"), x1, comm.w("pool_norm"))
    y, z, x3, h3 = run(_pool_fwd, "pool_fwd", h2, comm.w("p"), comm.w("pg"), comm.w("pool_scale"), x2, ffn_norm[1:2])
    gu1, act1 = run(_ffn_up, "ffn1_up", h3, comm.w("gu1"))
    loss, dx4, dx4_b, d_final = run(_proj_res_loss, "ffn1_down", act1, comm.w("d1"), x3, final_norm, tgt)

    dgu1 = run(_ffn_down_bwd, "ffn1_down_bwd", dx4_b, comm.w("d1"), gu1)
    comm.grad("d1", run(_mm_tn, "ffn1_down_dw", act1, dx4_b, FF_SHARD, 2048))
    comm.grad("gu1", run(_ffn_dw_up, "ffn1_up_dw", h3, dgu1))
    dx3, d_ffn1 = run(_ffn_dh, "ffn1_up_dh", dgu1, comm.w("gu1"), x3, ffn_norm[1:2], dx4)

    du, dw_pg, d_scale = run(_pool_bwd, "pool_bwd", dx3, z, y, comm.w("pool_scale"), comm.w("pg"))
    comm.grad("pg", dw_pg)
    comm.grad("p", run(_mm_tn, "pool_in_dw", h2, du, D_MODEL, h2.shape[0]))
    dx2, dx2_b, d_pool = run(_dh_norm_bwd, "pool_in_dh", du, comm.w("p"), x2, comm.w("pool_norm"), dx3)

    dgu0 = run(_ffn_down_bwd, "ffn0_down_bwd", dx2_b, comm.w("d0"), gu0)
    comm.grad("d0", run(_mm_tn, "ffn0_down_dw", act0, dx2_b, FF_SHARD, 2048))
    comm.grad("gu0", run(_ffn_dw_up, "ffn0_up_dw", h1, dgu0))
    dx1, d_ffn0 = run(_ffn_dh, "ffn0_up_dh", dgu0, comm.w("gu0"), x1, ffn_norm[0:1], dx2)

    comm.grad("o", run(_mm_tn, "attn_out_dw", merged, dx1, D_MODEL, 2048))
    do_f, dl_t = run(_merge_bwd, "merge_bwd", dx1, comm.w("o"), o_f, lse_t)
    dqkv = run(_attn_bwd, "attn_bwd", qkv, do_f, lse_f, _fold_groups(dl_t))
    for p in range(QKV_PIECES):
        comm.grad("qkv%d" % p, run(_qkv_dw, "qkv_dw%d" % p, p, hf, dqkv))
    dhf = None
    for g in range(N_GROUPS):
        dhf = run(_qkv_dh, "qkv_dh%d" % g, g, dqkv, [comm.w("wq%d" % p) for p in range(QKV_PIECES)], dhf)
    grad_x, d_attn = run(_rms_bwd_folded, "norm_attn_bwd", xs, attn_norm, dhf, dx1)

    small = dict(attn=d_attn, ffn0=d_ffn0, ffn1=d_ffn1, final=d_final, pool=d_pool, scale=d_scale)
    return loss, grad_x, small


TENSORS = ("qkv", "o", "p", "pg", "gu0", "gu1", "d0", "d1")


def _block_of(name):
    if name[:-1] in ("gu", "d"):
        return name[:-1], int(name[-1]), (0, 1)
    if name[:-1] == "qkv":
        return "qkv", 0, (int(name[-1]), QKV_PIECES)
    return name, 0, (0, 1)


class _MeshComm:
    def __init__(self, bufs, shapes, chip_arr, core_arr, pg_rows):
        self.bufs = dict(bufs)
        self.shapes = shapes
        self.chip, self.chip_arr, self.core_arr, self.pg_rows = chip_arr, chip_arr, core_arr, pg_rows
        self.parts, self.sums, self.blocks, self.landed, self.arrived = {}, {}, {}, {}, {}
        move = lambda ici=(), d2d=(): lambda: self.gather(ici, d2d)
        whole = lambda name: lambda: self.gather_and_pass(name)
        swap = lambda *names: lambda: self.swap(names)
        scatter = lambda *names: lambda: self.scatter(names)
        share = lambda *names: lambda: self.share(names)
        self.plan = {
            "fold": [whole("wq0")],
            "qkv_own": [whole("wq1")],
            "qkv_piece0": [whole("wq2")],
            "qkv_piece1": [move(ici=["o", "gu0[0:1/4]"])],
            "qkv_piece2": [move(ici=["gu0[1:2/4]"], d2d=["o", "gu0[0:1/4]"])],
            "attn_fwd": [move(ici=["gu0[2:4/4]", "d0[0:1/2]"], d2d=["gu0[1:2/4]"])],
            "merge_fwd": [move(ici=["p", "pg"], d2d=["gu0[2:4/4]", "d0[0:1/2]"])],
            "attn_out": [move(ici=["d0[1:2/2]", "pool_norm", "pool_scale"], d2d=["p", "pg"])],
            "ffn0_up": [move(ici=["gu1[0:3/4]"], d2d=["d0[1:2/2]"])],
            "ffn0_down": [move(ici=["gu1[3:4/4]"], d2d=["gu1[0:3/4]"])],
            "pool_fwd": [move(ici=["d1"], d2d=["gu1[3:4/4]"])],
            "ffn1_up": [move(d2d=["d1"])],
            "ffn1_up_dh": [swap("d1", "gu1")],
            "pool_bwd": [scatter("d1{01}")],
            "pool_in_dh": [scatter("d1{2}")],
            "ffn0_down_bwd": [scatter("gu1{01}")],
            "ffn0_down_dw": [scatter("gu1{2}")],
            "ffn0_up_dw": [share("gu1", "d1")],
            "ffn0_up_dh": [swap("pg", "p", "d0", "gu0")],
            "merge_bwd": [swap("o")],
            "attn_bwd": [scatter("gu0", "d0", "o")],
            "qkv_dw0": [share("d0"), scatter("p", "pg")],
            "qkv_dw1": [share("gu0", "o"), swap("qkv0")],
            "qkv_dw2": [share("p", "pg"), scatter("qkv0"), swap("qkv1")],
            "qkv_dh0": [share("qkv0"), scatter("qkv1"), swap("qkv2")],
            "qkv_dh1": [share("qkv1"), scatter("qkv2")],
            "qkv_dh2": [share("qkv2")],
        }

    def gather_and_pass(self, name):
        return _gather_and_pass_rider(self.bufs[name]), lambda res: self.bufs.update({name: res[0]})

    def gather(self, ici, d2d):
        names, jobs = [], []
        for over_ici, specs in ((True, ici), (False, d2d)):
            for spec in specs:
                name, _, rows = spec.partition("[")
                if name not in names:
                    names.append(name)
                rng = None
                if name in TENSORS:
                    ab, _, n = (rows[:-1] or "0:1/1").partition("/")
                    rng = (int(ab.split(":")[0]), int(ab.split(":")[1]), int(n))
                jobs.append((names.index(name), over_ici, rng))
        return _gather_rider([self.bufs[n] for n in names], jobs), lambda res: self.bufs.update(zip(names, res))

    def swap(self, names):
        def done(res):
            sums = _add_my_half("chip_sum_" + "_".join(names), [self.parts[n] for n in names], res, self.core_arr)
            self.sums.update(zip(names, sums))
        return _swap_halves_rider([self.parts[n] for n in names]), done

    def scatter(self, specs):
        names = [s.partition("{")[0] for s in specs]
        relations = [tuple(int(ch) for ch in s.partition("{")[2][:-1]) or (0, 1, 2) for s in specs]

        def done(res):
            for n, rel, landed in zip(names, relations, res):
                self.landed[n] = landed
                self.arrived[n] = self.arrived.get(n, ()) + rel
                if len(self.arrived[n]) < 3:
                    continue
                key, layer, col = _block_of(n)
                self.blocks[key] = _owner_sum("owner_sum_" + n, self.sums[n], landed, self.shapes[key],
                                              self.chip_arr, self.core_arr, self.blocks.get(key), layer, col)
        rider = _scatter_rider([self.sums[n] for n in names], [self.landed.get(n) for n in names], relations)
        return rider, done

    def share(self, names):
        keys, pieces = [], []
        for n in names:
            key, layer, col = _block_of(n)
            if key not in keys:
                keys.append(key)
            pieces.append((keys.index(key), layer, col))
        return _share_rider([self.blocks[k] for k in keys], pieces), lambda res: self.blocks.update(zip(keys, res))

    def run(self, fn, name, *args, **kw):
        made = [make() for make in self.plan.get(name, [])]
        if not made:
            return fn(name, *args, **kw)
        riders = [r for r, _ in made]
        out = _ride(riders[0] if len(riders) == 1 else _riders(*riders), fn, name, *args, **kw)
        for r, done in made:
            done(r.results)
        return out

    def w(self, name):
        b = self.bufs[name]
        if name in ("o", "p", "d0", "d1"):
            return b.reshape(-1, b.shape[-1])
        if name == "pg":
            b = b.reshape(N_CHIPS, 4, self.pg_rows, POOL_DIM).transpose(1, 0, 2, 3)
            return b.reshape(4, POOL_DIM, POOL_DIM)
        if name in ("pool_norm", "pool_scale"):
            return b.reshape(1, -1)
        return b

    def grad(self, name, value):
        if name == "pg":
            value = value.reshape(4, N_CHIPS, self.pg_rows, POOL_DIM).transpose(1, 0, 2, 3)
            value = value.reshape(N_CHIPS, 4 * self.pg_rows, POOL_DIM).astype(BF16)
        elif name in ("o", "p", "d0", "d1"):
            value = value.reshape(N_CHIPS, value.shape[0] // N_CHIPS, value.shape[1])
        self.parts[name] = value


def kernel(x, attn_norm, w_qkv, w_attn_out, pool_norm, w_pool_in, w_pool_group, pool_scale, ffn_norm, w_ffn_gate_up, w_ffn_down, final_norm, loss_target, m_attn_norm, m_w_qkv, m_w_attn_out, m_pool_norm, m_w_pool_in, m_w_pool_group, m_pool_scale, m_ffn_norm, m_w_ffn_gate_up, m_w_ffn_down, m_final_norm, v_attn_norm, v_w_qkv, v_w_attn_out, v_pool_norm, v_w_pool_in, v_w_pool_group, v_pool_scale, v_ffn_norm, v_w_ffn_gate_up, v_w_ffn_down, v_final_norm):
    D = D_MODEL
    chip = 2 * lax.axis_index("x") + lax.axis_index("y")
    core = lax.axis_index("c")
    pg_rows = w_pool_group.shape[2]

    chip_arr = chip.astype(jnp.int32).reshape(1)
    core_arr = core.astype(jnp.int32).reshape(1)

    pieces = _cast_qkv("cast_qkv", w_qkv, chip_arr)
    bufs = dict(zip(["wq_own"] + ["wq%d" % p for p in range(QKV_PIECES)], pieces))
    shards = [(w_attn_out, 0, BF16), (w_pool_in, 0, BF16), (w_pool_group.reshape(1, 4 * pg_rows, POOL_DIM), 0, BF16),
              (w_ffn_gate_up, 0, BF16), (w_ffn_gate_up, 1, BF16), (w_ffn_down, 0, BF16), (w_ffn_down, 1, BF16),
              (pool_norm[None], 0, F32), (pool_scale[None], 0, F32)]
    bufs.update(zip(TENSORS[1:] + ("pool_norm", "pool_scale"), _cast_many("cast_rest", shards, chip_arr)))

    as_block = lambda a: a.reshape((-1,) + a.shape[-2:]) if a.ndim == 3 else a.reshape(1, -1, a.shape[-1])
    owned = dict(qkv=(w_qkv, m_w_qkv, v_w_qkv), o=(w_attn_out, m_w_attn_out, v_w_attn_out),
                 p=(w_pool_in, m_w_pool_in, v_w_pool_in), pg=(w_pool_group, m_w_pool_group, v_w_pool_group),
                 gu=(w_ffn_gate_up, m_w_ffn_gate_up, v_w_ffn_gate_up), d=(w_ffn_down, m_w_ffn_down, v_w_ffn_down))
    comm = _MeshComm(bufs, {k: as_block(wmv[0]).shape for k, wmv in owned.items()}, chip_arr, core_arr, pg_rows)
    loss_part, grad_x, small = _local_step(x[0], loss_target[0], attn_norm, ffn_norm, final_norm.reshape(1, D), comm)

    rows = jnp.concatenate([small["attn"], small["ffn0"], small["ffn1"], small["final"], small["pool"],
                            small["scale"], jnp.pad(loss_part, ((0, 0), (0, D - 1))), jnp.zeros((1, D), F32)], axis=0)
    all_rows = _gather_small("gather_gain_grads", rows)
    tot = _sum_leading("sum_gain_grads", all_rows, F32)
    loss = tot[6, 0]
    g_attn_norm = tot[0:1]
    g_ffn_norm = tot[1:3]
    g_final_norm = tot[3]
    g_pool_norm = lax.dynamic_slice(tot, (4, chip * POOL_DIM), (1, POOL_DIM))
    g_pool_scale = lax.dynamic_slice(tot, (5, chip * POOL_DIM), (1, POOL_DIM))

    as_rows = lambda a: a.reshape(-1, a.shape[-1])
    res = _adamw("adamw_weights", [(as_rows(w), as_rows(comm.blocks[k]), as_rows(m), as_rows(v))
                                   for k, (w, m, v) in owned.items()])
    updated = {k: [a.reshape(owned[k][0].shape) for a in r] for k, r in zip(owned, res)}
    gains = [(attn_norm, g_attn_norm, m_attn_norm, v_attn_norm), (pool_norm, g_pool_norm, m_pool_norm, v_pool_norm),
             (pool_scale, g_pool_scale, m_pool_scale, v_pool_scale), (ffn_norm, g_ffn_norm, m_ffn_norm, v_ffn_norm),
             (final_norm, g_final_norm, m_final_norm, v_final_norm)]
    small_res = _adamw_small("adamw_gains", [tuple(as_rows(a) for a in item) for item in gains])
    small_res = [[a.reshape(item[0].shape) for a in r] for r, item in zip(small_res, gains)]
    results = [small_res[0], updated["qkv"], updated["o"], small_res[1], updated["p"], updated["pg"], small_res[2],
               small_res[3], updated["gu"], updated["d"], small_res[4]]
    grads = [r[0] for r in results]
    deltas = [r[1] for r in results]
    new_m = [r[2] for r in results]
    new_v = [r[3] for r in results]
    return (loss, grad_x[None], *grads, *deltas, *new_m, *new_v)
```

```python
import jax
import jax.numpy as jnp
from jax import lax
from jax.experimental import pallas as pl
from jax.experimental.pallas import tpu as pltpu

F32 = jnp.float32
BF16 = jnp.bfloat16
MESH = pl.DeviceIdType.MESH

D_MODEL = 1024
N_HEADS = 8
HEAD_DIM = 128
Q_BLOCK = 128
ATTN_DILATIONS = (1, 4, 16)
N_GROUPS = 3
D_FF = 2816
N_CHIPS = 4
FF_SHARD = 2 * D_FF // N_CHIPS
QKV_SHARD = 3 * 3 * D_MODEL // N_CHIPS
QKV_TILE = 768
POOL_DIM = 256
POOL_HALO = 16
RMS_EPS = 1e-6
NEG = -1e30
ADAM_LR = 0.001
ADAM_B1 = 0.9
ADAM_B2 = 0.999
ADAM_EPS = 1e-08
ADAM_WD = 0.01
ADAM_STEP = 10
VMEM_LIMIT = 56 * 1024 * 1024

_DN = {
    "nn": (((1,), (0,)), ((), ())),
    "nt": (((1,), (1,)), ((), ())),
    "tn": (((0,), (0,)), ((), ())),
}


class _Rider:
    def __init__(self, operands, out_shapes, aliases, n_copies, copies=None, start=None, finish=None):
        self.operands = list(operands)
        self.out_shapes = list(out_shapes)
        self.aliases = dict(aliases)
        self.n_copies = n_copies
        self.results = None

        def start_copies(ins, outs, send, recv, base=0):
            for cp in copies(ins, outs, send, recv, base):
                cp.start()

        def wait_copies(ins, outs, send, recv, base=0):
            for cp in copies(ins, outs, send, recv, base):
                cp.wait()

        self.start = start or start_copies
        self.finish = finish or wait_copies


def _riders(*riders):
    operands, out_shapes, aliases, offsets = [], [], {}, []
    n = 0
    for r in riders:
        offsets.append((len(operands), len(out_shapes), n))
        aliases.update({len(operands) + i: len(out_shapes) + o for i, o in r.aliases.items()})
        operands += r.operands
        out_shapes += r.out_shapes
        n += r.n_copies

    def each(which):
        def go(ins, outs, send, recv, base=0):
            for r, (i0, o0, s0) in zip(riders, offsets):
                getattr(r, which)(ins[i0:i0 + len(r.operands)], outs[o0:o0 + len(r.out_shapes)], send, recv,
                                  base + s0)
        return go

    both = _Rider(operands, out_shapes, aliases, n, start=each("start"), finish=each("finish"))
    both.parts = (riders, offsets)
    return both


_pending_rider = []


def _ride(rider, fn, *args, **kw):
    _pending_rider.append(rider)
    out = fn(*args, **kw)
    assert not _pending_rider
    if hasattr(rider, "parts"):
        for r, (_, o0, _) in zip(*rider.parts):
            r.results = rider.results[o0:o0 + len(r.out_shapes)]
    return out


def _pcall(body, prefetch=0, **kw):
    def build(body, kw):
        if not prefetch:
            return pl.pallas_call(body, **kw)
        kw = dict(kw)
        grid_spec = pltpu.PrefetchScalarGridSpec(
            num_scalar_prefetch=prefetch, grid=kw.pop("grid"), in_specs=kw.pop("in_specs"),
            out_specs=kw.pop("out_specs"), scratch_shapes=kw.pop("scratch_shapes", []))
        return pl.pallas_call(body, grid_spec=grid_spec, **kw)

    if not _pending_rider:
        return build(body, kw)
    rider = _pending_rider.pop()
    grid = kw.get("grid", ())
    in_specs = list(kw.get("in_specs", []))
    single = not isinstance(kw["out_shape"], (list, tuple))
    out_shape = [kw["out_shape"]] if single else list(kw["out_shape"])
    out_specs = [kw["out_specs"]] if single else list(kw["out_specs"])
    scratch = list(kw.get("scratch_shapes", []))
    n_in, n_out, n_scr = len(in_specs), len(out_shape), len(scratch)
    r_in, r_out = len(rider.operands), len(rider.out_shapes)

    def wrapped(*refs):
        scalars, refs = refs[:prefetch], refs[prefetch:]
        ins, rins = refs[:n_in], refs[n_in:n_in + r_in]
        outs = refs[n_in + r_in:n_in + r_in + n_out]
        routs = refs[n_in + r_in + n_out:n_in + r_in + n_out + r_out]
        scr = refs[n_in + r_in + n_out + r_out:n_in + r_in + n_out + r_out + n_scr]
        send, recv = refs[-2:]
        ids = [pl.program_id(a) for a in range(len(grid))]
        first, last = True, True
        for a, pid in enumerate(ids):
            first = jnp.logical_and(first, pid == 0)
            last = jnp.logical_and(last, pid == grid[a] - 1)
        if grid:
            pl.when(first)(lambda: rider.start(rins, routs, send, recv))
        else:
            rider.start(rins, routs, send, recv)
        body(*scalars, *ins, *outs, *scr)
        if grid:
            pl.when(last)(lambda: rider.finish(rins, routs, send, recv))
        else:
            rider.finish(rins, routs, send, recv)

    any_spec = pl.BlockSpec(memory_space=pl.ANY)
    kw2 = dict(kw)
    kw2.update(
        in_specs=in_specs + [any_spec] * r_in, out_specs=out_specs + [any_spec] * r_out,
        out_shape=out_shape + rider.out_shapes,
        scratch_shapes=scratch + [pltpu.SemaphoreType.DMA((rider.n_copies,)),
                                  pltpu.SemaphoreType.DMA((rider.n_copies,))],
        input_output_aliases={**kw.get("input_output_aliases", {}),
                              **{prefetch + n_in + i: n_out + o for i, o in rider.aliases.items()}})
    if grid:
        kw2["compiler_params"] = _params(("arbitrary",) * len(grid))
    call = build(wrapped, kw2)

    def run(*operands):
        res = call(*operands, *rider.operands)
        rider.results = list(res[n_out:])
        return res[0] if single else list(res[:n_out])

    return run


def _params(sem):
    return pltpu.CompilerParams(dimension_semantics=sem, vmem_limit_bytes=VMEM_LIMIT)


def _dot(a, b, mode):
    return lax.dot_general(a, b, _DN[mode], preferred_element_type=F32)


def _mm(name, mode, grid, a, a_spec, b, b_spec, out_shape, o_spec, acc_shape):
    nk = grid[-1]

    def body(a_ref, b_ref, o_ref, *acc):
        part = _dot(a_ref[...].astype(BF16), b_ref[...].astype(BF16), mode)
        if nk == 1:
            o_ref[...] = part.astype(o_ref.dtype)
            return
        acc_ref, = acc
        k = pl.program_id(len(grid) - 1)

        @pl.when(k == 0)
        def _():
            acc_ref[...] = part

        @pl.when(k > 0)
        def _():
            acc_ref[...] += part

        @pl.when(k == nk - 1)
        def _():
            o_ref[...] = acc_ref[...].astype(o_ref.dtype)

    scratch = [] if nk == 1 else [pltpu.VMEM(acc_shape, F32)]
    sem = ("parallel",) * (len(grid) - 1) + ("arbitrary",)
    return _pcall(body, name=name, grid=grid, in_specs=[a_spec, b_spec], out_specs=o_spec, out_shape=out_shape,
                  scratch_shapes=scratch, compiler_params=_params(sem))(a, b)


def _mm_tn(name, a, b, tm, tk):
    T, M = a.shape
    N = b.shape[1]
    return _mm(name, "tn", (M // tm, T // tk), a, pl.BlockSpec((tk, tm), lambda i, k: (k, i)),
               b, pl.BlockSpec((tk, N), lambda i, k: (k, 0)),
               jax.ShapeDtypeStruct((M, N), BF16), pl.BlockSpec((tm, N), lambda i, k: (i, 0)), (tm, N))


def _norm_bwd_rows(xf, gain, dh, dres):
    r = lax.rsqrt(jnp.mean(xf * xf, axis=-1, keepdims=True) + RMS_EPS)
    xh = xf * r
    t = dh * gain
    dx = dres + r * (t - xh * jnp.mean(t * xh, axis=-1, keepdims=True))
    return dx, jnp.sum(dh * xh, axis=0, keepdims=True)


def _accumulate(ref, value, first):
    @pl.when(first)
    def _():
        ref[...] = value

    @pl.when(jnp.logical_not(first))
    def _():
        ref[...] += value


def _rms_bwd_folded(name, x, g, dhf, dres, tb=512):
    S, D = x.shape

    def body(x_ref, g_ref, d0_ref, d1_ref, d2_ref, dres_ref, dx_ref, dg_ref, dh_ref):
        chunks = _lane_chunks(D)
        for c, cols in enumerate(chunks):
            dh_ref[c] = d0_ref[0, :, cols].astype(F32)
        for dil, ref in ((ATTN_DILATIONS[1], d1_ref), (ATTN_DILATIONS[2], d2_ref)):
            n = tb // dil
            for k in range(dil):
                for c, cols in enumerate(chunks):
                    dh_ref[c, _strided_rows(k, n, dil), :] += ref[k, :, cols].astype(F32)
        dh = jnp.concatenate([dh_ref[c] for c in range(len(chunks))], axis=1)
        dx, dg = _norm_bwd_rows(x_ref[...], g_ref[...], dh, dres_ref[...])
        dx_ref[...] = dx
        _accumulate(dg_ref, dg, pl.program_id(0) == 0)

    views, specs = _folded_views(dhf, tb)
    row = pl.BlockSpec((tb, D), lambda i: (i, 0))
    vec = pl.BlockSpec((1, D), lambda i: (0, 0))
    return _pcall(body, name=name, grid=(S // tb,), in_specs=[row, vec] + specs + [row], out_specs=[row, vec],
                  out_shape=[jax.ShapeDtypeStruct((S, D), F32), jax.ShapeDtypeStruct((1, D), F32)],
                  scratch_shapes=[pltpu.VMEM((D // LANES, tb, LANES), F32)],
                  compiler_params=_params(("arbitrary",)))(x, g, *views, dres)


def _proj_res_norm(name, a, w, res, gain, tm=512):
    M, K = a.shape
    D = w.shape[1]

    def body(a_ref, w_ref, res_ref, g_ref, x_ref, h_ref):
        xf = res_ref[...] + _dot(a_ref[...], w_ref[...], "nn")
        x_ref[...] = xf
        r = lax.rsqrt(jnp.mean(xf * xf, axis=-1, keepdims=True) + RMS_EPS)
        h_ref[...] = ((xf * r) * g_ref[...]).astype(h_ref.dtype)

    row = pl.BlockSpec((tm, D), lambda i: (i, 0))
    return _pcall(body, name=name, grid=(M // tm,),
                  in_specs=[pl.BlockSpec((tm, K), lambda i: (i, 0)), pl.BlockSpec((K, D), lambda i: (0, 0)), row,
                            pl.BlockSpec((1, D), lambda i: (0, 0))],
                  out_specs=[row, row],
                  out_shape=[jax.ShapeDtypeStruct((M, D), F32), jax.ShapeDtypeStruct((M, D), BF16)],
                  compiler_params=_params(("parallel",)))(a, w, res, gain)


def _proj_res_loss(name, a, w, res, gain, tgt, tm=512):
    M, K = a.shape
    D = w.shape[1]

    def body(a_ref, w_ref, res_ref, g_ref, t_ref, loss_ref, dx_ref, dxb_ref, dg_ref):
        first = pl.program_id(0) == 0
        xf = res_ref[...] + _dot(a_ref[...], w_ref[...], "nn")
        r = lax.rsqrt(jnp.mean(xf * xf, axis=-1, keepdims=True) + RMS_EPS)
        xh = xf * r
        gv = g_ref[...]
        e = xh * gv - t_ref[...]
        lp = 0.5 * jnp.sum(jnp.mean(e * e, axis=-1, keepdims=True), axis=0, keepdims=True)
        dy = e * (1.0 / D)
        t = dy * gv
        dx = r * (t - xh * jnp.mean(t * xh, axis=-1, keepdims=True))
        dx_ref[...] = dx
        dxb_ref[...] = dx.astype(dxb_ref.dtype)
        _accumulate(dg_ref, jnp.sum(dy * xh, axis=0, keepdims=True), first)
        _accumulate(loss_ref, lp, first)

    row = pl.BlockSpec((tm, D), lambda i: (i, 0))
    vec = pl.BlockSpec((1, D), lambda i: (0, 0))
    return _pcall(body, name=name, grid=(M // tm,),
                  in_specs=[pl.BlockSpec((tm, K), lambda i: (i, 0)), pl.BlockSpec((K, D), lambda i: (0, 0)), row,
                            vec, row],
                  out_specs=[pl.BlockSpec((1, 1), lambda i: (0, 0)), row, row, vec],
                  out_shape=[jax.ShapeDtypeStruct((1, 1), F32), jax.ShapeDtypeStruct((M, D), F32),
                             jax.ShapeDtypeStruct((M, D), BF16), jax.ShapeDtypeStruct((1, D), F32)],
                  compiler_params=_params(("arbitrary",)))(a, w, res, gain, tgt)


def _dh_norm_bwd(name, d, w, x, gain, dres, tm=512):
    M, N = d.shape
    D = w.shape[0]

    def body(d_ref, w_ref, x_ref, g_ref, dres_ref, dx_ref, dxb_ref, dg_ref):
        dh = _dot(d_ref[...].astype(BF16), w_ref[...], "nt")
        dx, dg = _norm_bwd_rows(x_ref[...], g_ref[...], dh, dres_ref[...])
        dx_ref[...] = dx
        dxb_ref[...] = dx.astype(dxb_ref.dtype)
        _accumulate(dg_ref, dg, pl.program_id(0) == 0)

    row = pl.BlockSpec((tm, D), lambda i: (i, 0))
    vec = pl.BlockSpec((1, D), lambda i: (0, 0))
    return _pcall(body, name=name, grid=(M // tm,),
                  in_specs=[pl.BlockSpec((tm, N), lambda i: (i, 0)), pl.BlockSpec((D, N), lambda i: (0, 0)), row, vec,
                            row],
                  out_specs=[row, row, vec],
                  out_shape=[jax.ShapeDtypeStruct((M, D), F32), jax.ShapeDtypeStruct((M, D), BF16),
                             jax.ShapeDtypeStruct((1, D), F32)],
                  compiler_params=_params(("arbitrary",)))(d, w, x, gain, dres)


def _sigmoid(v):
    return 0.5 * jnp.tanh(0.5 * v) + 0.5


def _ffn_up(name, h, w_gu, tm=1024):
    S, D = h.shape
    W = FF_SHARD

    def body(h_ref, wg_ref, wu_ref, gu_ref, act_ref):
        hv = h_ref[...]
        gate = _dot(hv, wg_ref[...], "nn")
        up = _dot(hv, wu_ref[...], "nn")
        gu_ref[0] = gate.astype(gu_ref.dtype)
        gu_ref[1] = up.astype(gu_ref.dtype)
        sig = _sigmoid(gate)
        act_ref[...] = ((gate * sig) * up).astype(act_ref.dtype)

    return _pcall(
        body, name=name, grid=(2, S // tm),
        in_specs=[pl.BlockSpec((tm, D), lambda j, i: (i, 0)),
                  pl.BlockSpec((None, D, W), lambda j, i: (j, 0, 0)),
                  pl.BlockSpec((None, D, W), lambda j, i: (j + 2, 0, 0))],
        out_specs=[pl.BlockSpec((2, tm, W), lambda j, i: (0, i, j)), pl.BlockSpec((tm, W), lambda j, i: (i, j))],
        out_shape=[jax.ShapeDtypeStruct((2, S, D_FF), BF16), jax.ShapeDtypeStruct((S, D_FF), BF16)],
        compiler_params=_params(("parallel", "parallel")))(h, w_gu, w_gu)


def _ffn_down_bwd(name, dx, w_down, gu, tm=1024):
    S, D = dx.shape
    W = FF_SHARD

    def body(dx_ref, w_ref, gu_ref, dgu_ref):
        dact = _dot(dx_ref[...].astype(BF16), w_ref[...], "nt")
        gate = gu_ref[0].astype(F32)
        up = gu_ref[1].astype(F32)
        sig = _sigmoid(gate)
        silu = gate * sig
        dgu_ref[0] = (dact * up * (sig * (1.0 + gate * (1.0 - sig)))).astype(dgu_ref.dtype)
        dgu_ref[1] = (dact * silu).astype(dgu_ref.dtype)

    blk = pl.BlockSpec((2, tm, W), lambda j, i: (0, i, j))
    return _pcall(
        body, name=name, grid=(2, S // tm),
        in_specs=[pl.BlockSpec((tm, D), lambda j, i: (i, 0)), pl.BlockSpec((W, D), lambda j, i: (j, 0)), blk],
        out_specs=blk, out_shape=jax.ShapeDtypeStruct((2, S, D_FF), BF16),
        compiler_params=_params(("parallel", "parallel")))(dx, w_down, gu)


def _ffn_dw_up(name, h, dgu, tk=2048):
    S, D = h.shape
    W = FF_SHARD
    tk = min(tk, S)
    return _mm(name, "tn", (N_CHIPS, S // tk), h, pl.BlockSpec((tk, D), lambda s, k: (k, 0)),
               dgu, pl.BlockSpec((None, tk, W), lambda s, k: (s // 2, k, s % 2)),
               jax.ShapeDtypeStruct((N_CHIPS, D, W), BF16), pl.BlockSpec((None, D, W), lambda s, k: (s, 0, 0)),
               (D, W))


def _ffn_dh(name, dgu, w_gu, x, gain, dres, tm=512):
    S = dgu.shape[1]
    W = FF_SHARD

    def body(a_ref, w_ref, x_ref, g_ref, dres_ref, dx_ref, dg_ref):
        dh = None
        for s in range(N_CHIPS):
            half, at = divmod(s * W, D_FF)
            part = _dot(a_ref[half, :, at:at + W], w_ref[s], "nt")
            dh = part if dh is None else dh + part
        dx, dg = _norm_bwd_rows(x_ref[...], g_ref[...], dh, dres_ref[...])
        dx_ref[...] = dx
        _accumulate(dg_ref, dg, pl.program_id(0) == 0)

    row = pl.BlockSpec((tm, D_MODEL), lambda i: (i, 0))
    vec = pl.BlockSpec((1, D_MODEL), lambda i: (0, 0))
    return _pcall(body, name=name, grid=(S // tm,),
                  in_specs=[pl.BlockSpec((2, tm, D_FF), lambda i: (0, i, 0)),
                            pl.BlockSpec((N_CHIPS, D_MODEL, W), lambda i: (0, 0, 0), pipeline_mode=pl.Buffered(1)),
                            row, vec, row],
                  out_specs=[row, vec],
                  out_shape=[jax.ShapeDtypeStruct((S, D_MODEL), F32), jax.ShapeDtypeStruct((1, D_MODEL), F32)],
                  compiler_params=_params(("arbitrary",)))(dgu, w_gu, x, gain, dres)


FOLD_TOKENS = 1024
LANES = 128


def _lane_chunks(width):
    return [slice(c * LANES, (c + 1) * LANES) for c in range(width // LANES)]


def _strided_rows(r, n, dil):
    return pl.ds(r, n) if dil == 1 else pl.ds(r, n, stride=dil)


def _norm_fold(name, x, gain):
    S, D = x.shape
    chunks = _lane_chunks(D)

    def body(x_ref, g_ref, hf_hbm, xc_ref, hs_ref, sems):
        i = pl.program_id(0)
        xf = x_ref[...]
        inv = lax.rsqrt(jnp.mean(xf * xf, axis=-1, keepdims=True) + RMS_EPS)
        h = (xf * inv) * g_ref[...]
        hs_ref[0] = h.astype(BF16)
        for c, cols in enumerate(chunks):
            xc_ref[c] = h[:, cols]
        copies = []
        for g, dil in enumerate(ATTN_DILATIONS):
            n = FOLD_TOKENS // dil
            for r in range(dil):
                if dil > 1:
                    for c, cols in enumerate(chunks):
                        hs_ref[g, r * n:(r + 1) * n, cols] = xc_ref[c, _strided_rows(r, n, dil), :].astype(BF16)
                cp = pltpu.make_async_copy(hs_ref.at[g, pl.ds(r * n, n)],
                                           hf_hbm.at[g, pl.ds(r * (S // dil) + i * n, n)], sems.at[len(copies)])
                cp.start()
                copies.append(cp)
        for cp in copies:
            cp.wait()

    return _pcall(body, name=name, grid=(S // FOLD_TOKENS,),
                  in_specs=[pl.BlockSpec((FOLD_TOKENS, D), lambda i: (i, 0)), pl.BlockSpec((1, D), lambda i: (0, 0))],
                  out_specs=pl.BlockSpec(memory_space=pl.ANY),
                  out_shape=jax.ShapeDtypeStruct((N_GROUPS, S, D), BF16),
                  scratch_shapes=[pltpu.VMEM((D // LANES, FOLD_TOKENS, LANES), F32),
                                  pltpu.VMEM((N_GROUPS, FOLD_TOKENS, D), BF16),
                                  pltpu.SemaphoreType.DMA((sum(ATTN_DILATIONS),))],
                  compiler_params=_params(("arbitrary",)))(x, gain)


def _qkv_tiles(name, piece, hf, w, qkv, chip, tm=1024):
    S = hf.shape[1]
    per_group = 3 * D_MODEL // QKV_TILE

    def tile(q, c_ref):
        if piece is None:
            return QKV_PIECES * c_ref[0] + q
        return QKV_PIECES * ((c_ref[0] + 1 + q) % N_CHIPS) + piece

    def body(*refs):
        a_ref, w_ref, o_ref = refs[1], refs[2], refs[-1]
        o_ref[...] = _dot(a_ref[...], w_ref[...], "nn").astype(o_ref.dtype)

    if piece is None:
        w_spec = pl.BlockSpec((D_MODEL, QKV_TILE), lambda q, i, c_ref: (0, q))
    else:
        w_spec = pl.BlockSpec((None, D_MODEL, QKV_TILE), lambda q, i, c_ref: ((c_ref[0] + 1 + q) % N_CHIPS, 0, 0))
    in_specs = [pl.BlockSpec((None, tm, D_MODEL), lambda q, i, c_ref: (tile(q, c_ref) // per_group, i, 0)), w_spec]
    operands = [chip, hf, w]
    aliases = {}
    if qkv is not None:
        in_specs.append(pl.BlockSpec(memory_space=pl.ANY))
        operands.append(qkv)
        aliases = {3: 0}
    return _pcall(
        body, prefetch=1, name=name, grid=(N_CHIPS - 1, S // tm), in_specs=in_specs,
        out_specs=pl.BlockSpec((None, tm, QKV_TILE),
                               lambda q, i, c_ref: (tile(q, c_ref) // per_group, i, tile(q, c_ref) % per_group)),
        out_shape=jax.ShapeDtypeStruct((N_GROUPS, S, 3 * D_MODEL), BF16), input_output_aliases=aliases,
        compiler_params=_params(("arbitrary", "arbitrary")))(*operands)


QKV_PIECES = QKV_SHARD // QKV_TILE


def _qkv_dw(name, p, hf, dqkv):
    S = hf.shape[1]
    per_group = 3 * D_MODEL // QKV_TILE
    tile = lambda s: QKV_PIECES * s + p
    return _mm(name, "tn", (N_CHIPS, 1),
               hf, pl.BlockSpec((None, S, D_MODEL), lambda s, k: (tile(s) // per_group, 0, 0)),
               dqkv, pl.BlockSpec((None, S, QKV_TILE), lambda s, k: (tile(s) // per_group, 0, tile(s) % per_group)),
               jax.ShapeDtypeStruct((N_CHIPS, D_MODEL, QKV_TILE), BF16),
               pl.BlockSpec((None, D_MODEL, QKV_TILE), lambda s, k: (s, 0, 0)), None)


def _qkv_dh(name, g, dqkv, w_pieces, dhf, tm=1024):
    S = dqkv.shape[1]
    per_group = 3 * D_MODEL // QKV_TILE

    def body(*refs):
        a_ref, w_hbm = refs[0], refs[1:1 + QKV_PIECES]
        o_ref, w_ref, sems = refs[-3:]

        @pl.when(pl.program_id(0) == 0)
        def _():
            copies = []
            for j in range(per_group):
                t = per_group * g + j
                copies.append(pltpu.make_async_copy(w_hbm[t % QKV_PIECES].at[t // QKV_PIECES],
                                                    w_ref.at[:, pl.ds(j * QKV_TILE, QKV_TILE)], sems.at[j]))
            for cp in copies:
                cp.start()
            for cp in copies:
                cp.wait()

        o_ref[...] = _dot(a_ref[...], w_ref[...], "nt").astype(o_ref.dtype)

    any_spec = pl.BlockSpec(memory_space=pl.ANY)
    in_specs = [pl.BlockSpec((None, tm, 3 * D_MODEL), lambda i: (g, i, 0))] + [any_spec] * QKV_PIECES
    operands = [dqkv] + list(w_pieces)
    aliases = {}
    if dhf is not None:
        in_specs.append(any_spec)
        operands.append(dhf)
        aliases = {1 + QKV_PIECES: 0}
    return _pcall(body, name=name, grid=(S // tm,), in_specs=in_specs,
                  out_specs=pl.BlockSpec((None, tm, D_MODEL), lambda i: (g, i, 0)),
                  out_shape=jax.ShapeDtypeStruct((N_GROUPS, S, D_MODEL), BF16),
                  scratch_shapes=[pltpu.VMEM((D_MODEL, 3 * D_MODEL), BF16), pltpu.SemaphoreType.DMA((per_group,))],
                  input_output_aliases=aliases, compiler_params=_params(("arbitrary",)))(*operands)


def _alibi_coef(g, h):
    vals = [-(2.0 ** (-8.0 * (gg * N_HEADS + h + 1) / (N_GROUPS * N_HEADS))) * ATTN_DILATIONS[gg]
            for gg in range(N_GROUPS)]
    return jnp.where(g == 0, vals[0], jnp.where(g == 1, vals[1], vals[2])).astype(F32)


def _blocks_per_segment(g, S):
    return jnp.right_shift(S // Q_BLOCK, 2 * g)


def _band_bias(pen):
    row = lax.broadcasted_iota(jnp.int32, (Q_BLOCK, 2 * Q_BLOCK), 0)
    col = lax.broadcasted_iota(jnp.int32, (Q_BLOCK, 2 * Q_BLOCK), 1)
    dist = Q_BLOCK + row - col
    valid = jnp.logical_and(dist >= 0, dist <= Q_BLOCK)
    base = jnp.where(valid, jnp.where(col < Q_BLOCK, pen, 0.0), NEG).astype(F32)
    return base, dist.astype(F32)


def _skewed(n_units, stages):
    state = {}
    for step in range(n_units + len(stages) - 1):
        for s, stage in enumerate(stages):
            u = step - s
            if 0 <= u < n_units:
                state[u] = stage(u, state.get(u))
                if s == len(stages) - 1:
                    del state[u]


def _attn_fwd(name, qkv, tq=1024):
    S = qkv.shape[1]
    nqb = tq // Q_BLOCK
    scale = HEAD_DIM ** -0.5

    def body(q_ref, k_ref, kp_ref, v_ref, vp_ref, o_ref, lse_ref, kf_ref, vf_ref):
        g = pl.program_id(0)
        i = pl.program_id(1)
        nb = _blocks_per_segment(g, S)
        kf_ref[:Q_BLOCK] = kp_ref[...]
        kf_ref[Q_BLOCK:] = k_ref[...]
        vf_ref[:Q_BLOCK] = vp_ref[...]
        vf_ref[Q_BLOCK:] = v_ref[...]
        coefs = [_alibi_coef(g, h) for h in range(N_HEADS)]
        units = [(b, h) for b in range(nqb) for h in range(N_HEADS)]
        bias = {}

        def scores(u, _):
            b, h = units[u]
            if b not in bias:
                pen = jnp.where((i * nqb + b) % nb != 0, 0.0, NEG).astype(F32)
                bias.clear()
                bias[b] = _band_bias(pen)
            base, dist = bias[b]
            cols = slice(h * HEAD_DIM, (h + 1) * HEAD_DIM)
            q = q_ref[b * Q_BLOCK:(b + 1) * Q_BLOCK, cols]
            kk = kf_ref[b * Q_BLOCK:(b + 2) * Q_BLOCK, cols]
            return _dot(q, kk, "nt") * scale + (coefs[h] * dist + base)

        def softmax(u, s):
            m = jnp.max(s, axis=-1, keepdims=True)
            p = jnp.exp(s - m)
            den = jnp.sum(p, axis=-1, keepdims=True)
            return (p * (1.0 / den)).astype(BF16), m + jnp.log(den)

        def output(u, st):
            b, h = units[u]
            pn, lse = st
            cols = slice(h * HEAD_DIM, (h + 1) * HEAD_DIM)
            rows = slice(b * Q_BLOCK, (b + 1) * Q_BLOCK)
            o_ref[rows, cols] = _dot(pn, vf_ref[b * Q_BLOCK:(b + 2) * Q_BLOCK, cols], "nn").astype(o_ref.dtype)
            lse_ref[rows, h:h + 1] = lse

        _skewed(len(units), [scores, softmax, output])

    main = lambda c: pl.BlockSpec((None, tq, D_MODEL), lambda g, i: (g, i, c))
    prev = lambda c: pl.BlockSpec((None, Q_BLOCK, D_MODEL), lambda g, i: (g, jnp.maximum(i * nqb - 1, 0), c))
    return _pcall(
        body, name=name, grid=(N_GROUPS, S // tq),
        in_specs=[main(0), main(1), prev(1), main(2), prev(2)],
        out_specs=[pl.BlockSpec((None, tq, D_MODEL), lambda g, i: (g, i, 0)),
                   pl.BlockSpec((None, tq, N_HEADS), lambda g, i: (g, i, 0))],
        out_shape=[jax.ShapeDtypeStruct((N_GROUPS, S, D_MODEL), BF16),
                   jax.ShapeDtypeStruct((N_GROUPS, S, N_HEADS), F32)],
        scratch_shapes=[pltpu.VMEM((tq + Q_BLOCK, D_MODEL), BF16), pltpu.VMEM((tq + Q_BLOCK, D_MODEL), BF16)],
        compiler_params=_params(("parallel", "parallel")))(qkv, qkv, qkv, qkv, qkv)


def _attn_bwd(name, qkv, do, lse, dl, tq=1024):
    S = qkv.shape[1]
    nqb = tq // Q_BLOCK
    n_blocks = S // Q_BLOCK
    scale = HEAD_DIM ** -0.5
    KEYS = 2 * Q_BLOCK

    def body(q_ref, k_ref, v_ref, kp_ref, vp_ref, qn_ref, do_ref, don_ref, lse_ref, lsen_ref, dl_ref, dln_ref,
             out_ref, kf_ref, vf_ref, dk_ref, dv_ref):
        g = pl.program_id(0)
        i = pl.program_id(1)
        nb = _blocks_per_segment(g, S)
        kf_ref[:Q_BLOCK] = kp_ref[...]
        kf_ref[Q_BLOCK:] = k_ref[...]
        vf_ref[:Q_BLOCK] = vp_ref[...]
        vf_ref[Q_BLOCK:] = v_ref[...]
        coefs = [_alibi_coef(g, h) for h in range(N_HEADS)]
        units = [(h, b) for h in range(N_HEADS) for b in range(nqb + 1)]
        bias = {}

        def band(b):
            if b not in bias:
                blk = i * nqb + b
                ok = blk % nb != 0
                if b == nqb:
                    ok = jnp.logical_and(ok, blk < n_blocks)
                bias[b] = _band_bias(jnp.where(ok, 0.0, NEG).astype(F32))
            return bias[b]

        def operands(h, b):
            cols = slice(h * HEAD_DIM, (h + 1) * HEAD_DIM)
            if b == nqb:
                keys = slice(tq, tq + Q_BLOCK)
                return (qn_ref[:, cols], don_ref[:, cols], lsen_ref[:, h:h + 1], dln_ref[:, h:h + 1],
                        kf_ref[keys, cols], vf_ref[keys, cols])
            rows = slice(b * Q_BLOCK, (b + 1) * Q_BLOCK)
            keys = slice(b * Q_BLOCK, b * Q_BLOCK + KEYS)
            return (q_ref[rows, cols], do_ref[rows, cols], lse_ref[rows, h:h + 1], dl_ref[rows, h:h + 1],
                    kf_ref[keys, cols], vf_ref[keys, cols])

        def probs(u, _):
            h, b = units[u]
            q, do_b, lse_b, dl_b, kk, vv = operands(h, b)
            base, dist = band(b)
            if b == nqb:
                base, dist = base[:, :Q_BLOCK], dist[:, :Q_BLOCK]
            p = jnp.exp(_dot(q, kk, "nt") * scale + (coefs[h] * dist + base) - lse_b)
            return p, _dot(do_b, vv, "nt")

        def dscores(u, st):
            h, b = units[u]
            p, dp = st
            dl_b = operands(h, b)[3]
            return ((p * (dp - dl_b)) * scale).astype(BF16), p.astype(BF16)

        def grads(u, st):
            h, b = units[u]
            ds, pb = st
            q, do_b, _, _, kk, _ = operands(h, b)
            cols = slice(h * HEAD_DIM, (h + 1) * HEAD_DIM)
            if b < nqb:
                out_ref[b * Q_BLOCK:(b + 1) * Q_BLOCK, cols] = _dot(ds, kk, "nn").astype(out_ref.dtype)
            if b == 0:
                dk_ref[:Q_BLOCK] = _dot(ds[:, Q_BLOCK:], q, "tn")
                dv_ref[:Q_BLOCK] = _dot(pb[:, Q_BLOCK:], do_b, "tn")
                return None
            dk = _dot(ds, q, "tn")
            dv = _dot(pb, do_b, "tn")
            before = slice((b - 1) * Q_BLOCK, b * Q_BLOCK)
            dk_ref[before] += dk[:Q_BLOCK]
            dv_ref[before] += dv[:Q_BLOCK]
            if b < nqb:
                own = slice(b * Q_BLOCK, (b + 1) * Q_BLOCK)
                dk_ref[own] = dk[Q_BLOCK:]
                dv_ref[own] = dv[Q_BLOCK:]
            else:
                out_ref[:, D_MODEL + h * HEAD_DIM:D_MODEL + (h + 1) * HEAD_DIM] = dk_ref[...].astype(out_ref.dtype)
                out_ref[:, 2 * D_MODEL + h * HEAD_DIM:2 * D_MODEL + (h + 1) * HEAD_DIM] = (
                    dv_ref[...].astype(out_ref.dtype))
            return None

        _skewed(len(units), [probs, dscores, grads])

    nxt_blk = lambda i: jnp.minimum((i + 1) * nqb, n_blocks - 1)
    main = lambda c: pl.BlockSpec((None, tq, D_MODEL), lambda g, i: (g, i, c))
    prev = lambda c: pl.BlockSpec((None, Q_BLOCK, D_MODEL), lambda g, i: (g, jnp.maximum(i * nqb - 1, 0), c))
    row = pl.BlockSpec((None, tq, D_MODEL), lambda g, i: (g, i, 0))
    row_n = pl.BlockSpec((None, Q_BLOCK, D_MODEL), lambda g, i: (g, nxt_blk(i), 0))
    col = pl.BlockSpec((None, tq, N_HEADS), lambda g, i: (g, i, 0))
    col_n = pl.BlockSpec((None, Q_BLOCK, N_HEADS), lambda g, i: (g, nxt_blk(i), 0))
    return _pcall(
        body, name=name, grid=(N_GROUPS, S // tq),
        in_specs=[main(0), main(1), main(2), prev(1), prev(2), row_n, row, row_n, col, col_n, col, col_n],
        out_specs=pl.BlockSpec((None, tq, 3 * D_MODEL), lambda g, i: (g, i, 0)),
        out_shape=jax.ShapeDtypeStruct((N_GROUPS, S, 3 * D_MODEL), BF16),
        scratch_shapes=[pltpu.VMEM((tq + Q_BLOCK, D_MODEL), BF16), pltpu.VMEM((tq + Q_BLOCK, D_MODEL), BF16),
                        pltpu.VMEM((tq, HEAD_DIM), F32), pltpu.VMEM((tq, HEAD_DIM), F32)],
        compiler_params=_params(("parallel", "parallel")))(qkv, qkv, qkv, qkv, qkv, qkv, do, do, lse, lse, dl, dl)


def _group_weights(lse_ref):
    l0, l1, l2 = lse_ref[0], lse_ref[1], lse_ref[2]
    m = jnp.maximum(jnp.maximum(l0, l1), l2)
    e = [jnp.exp(l0 - m), jnp.exp(l1 - m), jnp.exp(l2 - m)]
    den = e[0] + e[1] + e[2]
    return [ei / den for ei in e]


MERGE_TOKENS = 512


def _folded_views(a, tb):
    _, S, C = a.shape
    views, specs = [], []
    for g, dil in enumerate(ATTN_DILATIONS):
        views.append(a.reshape(N_GROUPS * dil, S // dil, C))
        specs.append(pl.BlockSpec((dil, tb // dil, C), lambda i, g=g: (g, i, 0)))
    return views, specs


def _unfold_into(dst_ref, folded_refs):
    for g, (dil, ref) in enumerate(zip(ATTN_DILATIONS, folded_refs)):
        n = ref.shape[1]
        for r in range(dil):
            for c, cols in enumerate(_lane_chunks(ref.shape[2])):
                dst_ref[g, c, _strided_rows(r, n, dil), :] = ref[r, :, cols].astype(F32)


def _merge_fwd(name, o, lse, tb=MERGE_TOKENS):
    S = o.shape[1]

    def body(o0_ref, o1_ref, o2_ref, lse_ref, out_ref, o_ref):
        _unfold_into(o_ref, (o0_ref, o1_ref, o2_ref))
        w = _group_weights(lse_ref)
        for h in range(N_HEADS):
            cols = slice(h * HEAD_DIM, (h + 1) * HEAD_DIM)
            acc = w[0][:, h:h + 1] * o_ref[0, h]
            for gg in range(1, N_GROUPS):
                acc = acc + w[gg][:, h:h + 1] * o_ref[gg, h]
            out_ref[:, cols] = acc.astype(out_ref.dtype)

    views, specs = _folded_views(o, tb)
    return _pcall(body, name=name, grid=(S // tb,),
                  in_specs=specs + [pl.BlockSpec((N_GROUPS, tb, N_HEADS), lambda i: (0, i, 0))],
                  out_specs=pl.BlockSpec((tb, D_MODEL), lambda i: (i, 0)),
                  out_shape=jax.ShapeDtypeStruct((S, D_MODEL), BF16),
                  scratch_shapes=[pltpu.VMEM((N_GROUPS, N_HEADS, tb, HEAD_DIM), F32)],
                  compiler_params=_params(("parallel",)))(*views, lse)


def _merge_bwd(name, dx, w_out, o, lse, tb=MERGE_TOKENS):
    S = o.shape[1]
    n_chunks = sum(ATTN_DILATIONS)

    def body(dx_ref, w_ref, o0_ref, o1_ref, o2_ref, lse_ref, do_hbm, dl_ref, o_ref, dt_ref, df_ref, d_ref, sems):
        i = pl.program_id(0)
        d_ref[...] = _dot(dx_ref[...].astype(BF16), w_ref[...], "nt")
        _unfold_into(o_ref, (o0_ref, o1_ref, o2_ref))
        w = _group_weights(lse_ref)
        for h in range(N_HEADS):
            cols = slice(h * HEAD_DIM, (h + 1) * HEAD_DIM)
            dv = d_ref[:, cols]
            merged = w[0][:, h:h + 1] * o_ref[0, h]
            for gg in range(1, N_GROUPS):
                merged = merged + w[gg][:, h:h + 1] * o_ref[gg, h]
            dsum = jnp.sum(dv * merged, axis=-1, keepdims=True)
            for gg in range(N_GROUPS):
                wg = w[gg][:, h:h + 1]
                dt_ref[gg, h] = wg * dv
                dl_ref[gg, :, h:h + 1] = wg * dsum
        copies = []
        for gg, dil in enumerate(ATTN_DILATIONS):
            n = tb // dil
            for r in range(dil):
                for h, cols in enumerate(_lane_chunks(D_MODEL)):
                    df_ref[gg, r * n:(r + 1) * n, cols] = (
                        dt_ref[gg, h, _strided_rows(r, n, dil), :].astype(df_ref.dtype))
                cp = pltpu.make_async_copy(df_ref.at[gg, pl.ds(r * n, n)],
                                           do_hbm.at[gg, pl.ds(r * (S // dil) + i * n, n)], sems.at[len(copies)])
                cp.start()
                copies.append(cp)
        for cp in copies:
            cp.wait()

    views, specs = _folded_views(o, tb)
    small = pl.BlockSpec((N_GROUPS, tb, N_HEADS), lambda i: (0, i, 0))
    return _pcall(body, name=name, grid=(S // tb,),
                  in_specs=[pl.BlockSpec((tb, D_MODEL), lambda i: (i, 0)),
                            pl.BlockSpec((D_MODEL, D_MODEL), lambda i: (0, 0))] + specs + [small],
                  out_specs=[pl.BlockSpec(memory_space=pl.ANY), small],
                  out_shape=[jax.ShapeDtypeStruct((N_GROUPS, S, D_MODEL), BF16),
                             jax.ShapeDtypeStruct((N_GROUPS, S, N_HEADS), F32)],
                  scratch_shapes=[pltpu.VMEM((N_GROUPS, N_HEADS, tb, HEAD_DIM), F32),
                                  pltpu.VMEM((N_GROUPS, N_HEADS, tb, HEAD_DIM), F32),
                                  pltpu.VMEM((N_GROUPS, tb, D_MODEL), BF16), pltpu.VMEM((tb, D_MODEL), F32),
                                  pltpu.SemaphoreType.DMA((n_chunks,))],
                  compiler_params=_params(("arbitrary",)))(dx, w_out, *views, lse)


def _shift_rows(a, k):
    return pltpu.roll(a, k % a.shape[0], axis=0)


def _pool_level(g, levels):
    return jnp.where(g == 0, levels[0], jnp.where(g == 1, levels[1], jnp.where(g == 2, levels[2], levels[3])))


def _pool_fwd(name, h, w_in, w_group, scale, x_in, gain_next, tr=1024):
    S, D = h.shape
    H = POOL_HALO
    n_groups = D // POOL_DIM

    def body(h_ref, hh_ref, wi_ref, w_ref, sc_ref, x_ref, gn_ref, y_ref, z_ref, xo_ref, hn_ref, xs_ref):
        i = pl.program_id(0)
        g = pl.program_id(1)
        uv = _dot(h_ref[...], wi_ref[...], "nn")
        halo = jnp.where(i > 0, _dot(hh_ref[...], wi_ref[...], "nn"), 0.0)
        s = jnp.concatenate([halo, uv], axis=0)
        levels = []
        for k in (1, 2, 4, 8):
            s = s + _shift_rows(s, k)
            levels.append(s[H:])
        t = i * tr + lax.broadcasted_iota(jnp.int32, (tr, 1), 0)
        cnt = jnp.minimum(t + 1, jnp.left_shift(2, g)).astype(F32)
        y = _pool_level(g, levels) / cnt - uv
        yb = y.astype(BF16)
        z = _dot(yb, w_ref[...], "nn")
        y_ref[...] = yb
        z_ref[...] = z.astype(z_ref.dtype)
        x_out = x_ref[...] + z * sc_ref[...]
        xo_ref[...] = x_out
        xs_ref[g] = x_out

        @pl.when(g == n_groups - 1)
        def _():
            xf = jnp.concatenate([xs_ref[k] for k in range(n_groups)], axis=1)
            r = lax.rsqrt(jnp.mean(xf * xf, axis=-1, keepdims=True) + RMS_EPS)
            hn_ref[...] = ((xf * r) * gn_ref[...]).astype(hn_ref.dtype)

    blk = pl.BlockSpec((tr, POOL_DIM), lambda i, g: (i, g))
    row = pl.BlockSpec((tr, D), lambda i, g: (i, 0))
    return _pcall(
        body, name=name, grid=(S // tr, n_groups),
        in_specs=[row, pl.BlockSpec((H, D), lambda i, g: (jnp.maximum(i * (tr // H) - 1, 0), 0)),
                  pl.BlockSpec((D, POOL_DIM), lambda i, g: (0, g)),
                  pl.BlockSpec((None, POOL_DIM, POOL_DIM), lambda i, g: (g, 0, 0)),
                  pl.BlockSpec((1, POOL_DIM), lambda i, g: (0, g)), blk, pl.BlockSpec((1, D), lambda i, g: (0, 0))],
        out_specs=[blk, blk, blk, row],
        out_shape=[jax.ShapeDtypeStruct((S, D), BF16), jax.ShapeDtypeStruct((S, D), BF16),
                   jax.ShapeDtypeStruct((S, D), F32), jax.ShapeDtypeStruct((S, D), BF16)],
        scratch_shapes=[pltpu.VMEM((n_groups, tr, POOL_DIM), F32)],
        compiler_params=_params(("parallel", "arbitrary")))(h, h, w_in, w_group, scale, x_in, gain_next)


def _pool_bwd(name, d_out, z, y, scale, w_group, tr=1024):
    S, D = d_out.shape
    H = POOL_HALO

    def body(d_ref, dn_ref, z_ref, y_ref, sc_ref, w_ref, du_ref, dw_ref, dsc_ref):
        g = pl.program_id(0)
        i = pl.program_id(1)
        dv = d_ref[...]
        dz = jnp.concatenate([dv, dn_ref[...]], axis=0) * sc_ref[...]
        dzb = dz.astype(BF16)
        dy = _dot(dzb, w_ref[...], "nt")
        t = i * tr + lax.broadcasted_iota(jnp.int32, (tr + H, 1), 0)
        cnt = jnp.minimum(t + 1, jnp.left_shift(2, g)).astype(F32)
        s = jnp.where(t < S, dy / cnt, 0.0)
        levels = []
        for k in (1, 2, 4, 8):
            s = s + _shift_rows(s, -k)
            levels.append(s[:tr])
        du_ref[...] = (_pool_level(g, levels) - dy[:tr]).astype(du_ref.dtype)
        dw = _dot(y_ref[...], dzb[:tr], "tn")
        dsc = jnp.sum(dv * z_ref[...].astype(F32), axis=0, keepdims=True)

        @pl.when(i == 0)
        def _():
            dw_ref[...] = dw
            dsc_ref[...] = dsc

        @pl.when(i > 0)
        def _():
            dw_ref[...] += dw
            dsc_ref[...] += dsc

    blk = pl.BlockSpec((tr, POOL_DIM), lambda g, i: (i, g))
    vec = pl.BlockSpec((1, POOL_DIM), lambda g, i: (0, g))
    mat = pl.BlockSpec((None, POOL_DIM, POOL_DIM), lambda g, i: (g, 0, 0))
    nxt = pl.BlockSpec((H, POOL_DIM), lambda g, i: (jnp.minimum((i + 1) * (tr // H), S // H - 1), g))
    return _pcall(
        body, name=name, grid=(D // POOL_DIM, S // tr), in_specs=[blk, nxt, blk, blk, vec, mat],
        out_specs=[blk, mat, vec],
        out_shape=[jax.ShapeDtypeStruct((S, D), BF16), jax.ShapeDtypeStruct((D // POOL_DIM, POOL_DIM, POOL_DIM), F32),
                   jax.ShapeDtypeStruct((1, D), F32)],
        compiler_params=_params(("parallel", "arbitrary")))(d_out, d_out, z, y, scale, w_group)


def _row_tile(rows, cols, target_bytes=1 << 20):
    best = rows
    for tr in range(8, rows + 1, 8):
        if rows % tr == 0 and tr * cols * 4 <= target_bytes:
            best = tr
    return best if best * cols * 4 <= 4 * target_bytes else rows


def _adamw_step(w, g, m, v):
    m2 = ADAM_B1 * m + (1.0 - ADAM_B1) * g
    v2 = ADAM_B2 * v + (1.0 - ADAM_B2) * (g * g)
    m_hat = m2 / (1.0 - ADAM_B1 ** ADAM_STEP)
    v_hat = v2 / (1.0 - ADAM_B2 ** ADAM_STEP)
    return -ADAM_LR * (m_hat / (jnp.sqrt(v_hat) + ADAM_EPS) + ADAM_WD * w), m2, v2


def _adamw_refs(w_ref, g_ref, m_ref, v_ref, go_ref, d_ref, mo_ref, vo_ref):
    gv = g_ref[...]
    d_ref[...], mo_ref[...], vo_ref[...] = _adamw_step(w_ref[...], gv, m_ref[...], v_ref[...])
    go_ref[...] = gv


def _adamw(name, items, steps=16):
    n = len(items)

    def body(*refs):
        for t in range(n):
            _adamw_refs(*refs[4 * t:4 * t + 4], *refs[4 * n + 4 * t:4 * n + 4 * t + 4])

    specs, shapes = [], []
    for w, _, _, _ in items:
        R, C = w.shape
        assert R % (8 * steps) == 0, (name, w.shape)
        specs += [pl.BlockSpec((R // steps, C), lambda i: (i, 0))] * 4
        shapes += [jax.ShapeDtypeStruct((R, C), F32)] * 4
    res = _pcall(body, name=name, grid=(steps,), in_specs=specs, out_specs=specs, out_shape=shapes,
                 compiler_params=_params(("parallel",)))(*[a for item in items for a in item])
    return [res[4 * t:4 * t + 4] for t in range(n)]


def _adamw_small(name, items):
    n = len(items)

    def body(*refs):
        for t in range(n):
            _adamw_refs(*refs[4 * t:4 * t + 4], *refs[4 * n + 4 * t:4 * n + 4 * t + 4])

    specs, shapes = [], []
    for w, _, _, _ in items:
        specs += [pl.BlockSpec(w.shape, lambda i: (0, 0))] * 4
        shapes += [jax.ShapeDtypeStruct(w.shape, F32)] * 4
    res = _pcall(body, name=name, grid=(1,), in_specs=specs, out_specs=specs, out_shape=shapes,
                 compiler_params=_params(("arbitrary",)))(*[a for item in items for a in item])
    return [res[4 * t:4 * t + 4] for t in range(n)]


def _sum_leading(name, a, out_dtype):
    n, R, C = a.shape
    tr = _row_tile(R, C) if R % 16 == 0 else R
    if tr % 16:
        tr = R

    def body(a_ref, o_ref):
        acc = a_ref[0].astype(F32)
        for j in range(1, n):
            acc = acc + a_ref[j].astype(F32)
        o_ref[...] = acc.astype(o_ref.dtype)

    return _pcall(body, name=name, grid=(R // tr,), in_specs=[pl.BlockSpec((n, tr, C), lambda i: (0, i, 0))],
                  out_specs=pl.BlockSpec((tr, C), lambda i: (i, 0)), out_shape=jax.ShapeDtypeStruct((R, C), out_dtype),
                  compiler_params=_params(("parallel",)))(a)


def _cast_many(name, items, chip, steps=4):
    tiles = []
    for w, _, dtype in items:
        R = w.shape[1]
        nt = steps if R % (steps * 16) == 0 else 1
        tiles.append((nt, R // nt))

    def body(c_ref, *refs):
        i = pl.program_id(0)
        for t, (nt, _) in enumerate(tiles):
            w_ref, o_ref = refs[t], refs[len(items) + t]

            def cast(w_ref=w_ref, o_ref=o_ref):
                o_ref[...] = w_ref[...].astype(o_ref.dtype)

            if nt == steps:
                cast()
            else:
                pl.when(i < nt)(cast)

    in_specs, out_specs, out_shape = [], [], []
    for (w, layer, dtype), (nt, tr) in zip(items, tiles):
        C = w.shape[2]
        in_specs.append(pl.BlockSpec((None, tr, C), lambda i, c_ref, l=layer, nt=nt: (l, jnp.minimum(i, nt - 1), 0)))
        out_specs.append(pl.BlockSpec((None, tr, C), lambda i, c_ref, nt=nt: (c_ref[0], jnp.minimum(i, nt - 1), 0)))
        out_shape.append(jax.ShapeDtypeStruct((N_CHIPS, w.shape[1], C), dtype))
    return _pcall(body, prefetch=1, name=name, grid=(steps,), in_specs=in_specs, out_specs=out_specs,
                  out_shape=out_shape, compiler_params=_params(("arbitrary",)))(chip, *[w for w, _, _ in items])


def _cast_qkv(name, w, chip, tr=256):
    _, R, C = w.shape

    def body(c_ref, w_ref, own_ref, *piece_refs):
        v = w_ref[...].astype(BF16)
        own_ref[...] = v
        for p, ref in enumerate(piece_refs):
            ref[...] = v[:, p * QKV_TILE:(p + 1) * QKV_TILE]

    piece = pl.BlockSpec((None, tr, QKV_TILE), lambda i, c_ref: (c_ref[0], i, 0))
    return _pcall(body, prefetch=1, name=name, grid=(R // tr,),
                  in_specs=[pl.BlockSpec((None, tr, C), lambda i, c_ref: (0, i, 0))],
                  out_specs=[pl.BlockSpec((tr, C), lambda i, c_ref: (i, 0))] + [piece] * QKV_PIECES,
                  out_shape=[jax.ShapeDtypeStruct((R, C), BF16)]
                  + [jax.ShapeDtypeStruct((N_CHIPS, R, QKV_TILE), BF16)] * QKV_PIECES,
                  compiler_params=_params(("parallel",)))(chip, w)


def _owner_sum(name, mine, landed, shape, chip, core, out, layer, col):
    _, half, C = mine.shape
    k, _ = col
    tr = _row_tile(half, C)
    if tr % 16:
        tr = half
    nrt = half // tr

    def body(ch_ref, co_ref, mine_ref, landed_ref, *rest):
        g = mine_ref[...].astype(F32)
        for j in range(3):
            g = g + landed_ref[j].astype(F32)
        rest[-1][...] = g

    piece = pl.BlockSpec((None, tr, C), lambda i, ch, co: (layer, co[0] * nrt + i, k))
    in_specs = [pl.BlockSpec((None, tr, C), lambda i, ch, co: (ch[0], i, 0)),
                pl.BlockSpec((3, tr, C), lambda i, ch, co: (0, i, 0))]
    operands = [chip, core, mine, landed]
    aliases = {}
    if out is not None:
        in_specs.append(ANY)
        operands.append(out)
        aliases = {4: 0}
    return _pcall(body, prefetch=2, name=name, grid=(nrt,), in_specs=in_specs, out_specs=piece,
                  out_shape=jax.ShapeDtypeStruct(shape, F32), input_output_aliases=aliases,
                  compiler_params=_params(("parallel",)))(*operands)


def _add_my_half(name, ps, qs, core):
    n = len(ps)

    def body(c_ref, *refs):
        for t in range(n):
            p_ref, q_ref, o_ref = refs[t], refs[n + t], refs[2 * n + t]
            o_ref[...] = (p_ref[...].astype(F32) + q_ref[...].astype(F32)).astype(o_ref.dtype)

    halves = [(p.shape[1] // 2, p.shape[2]) for p in ps]
    mine = [pl.BlockSpec((None, h, c), lambda s, c_ref: (s, c_ref[0], 0)) for h, c in halves]
    whole = [pl.BlockSpec((None, h, c), lambda s, c_ref: (s, 0, 0)) for h, c in halves]
    return _pcall(body, prefetch=1, name=name, grid=(N_CHIPS,), in_specs=mine + whole, out_specs=whole,
                  out_shape=[jax.ShapeDtypeStruct((N_CHIPS, h, c), BF16) for h, c in halves],
                  compiler_params=_params(("parallel",)))(core, *ps, *qs)


ANY = pl.BlockSpec(memory_space=pl.ANY)


def _place():
    x, y, c = lax.axis_index("x"), lax.axis_index("y"), lax.axis_index("c")
    other_chips = [(1 - x, y), (x, 1 - y), (1 - x, 1 - y)]
    return x, y, c, other_chips


def _remote(src, dst, send, recv, k, to):
    return pltpu.make_async_remote_copy(src_ref=src, dst_ref=dst, send_sem=send.at[k], recv_sem=recv.at[k],
                                        device_id=to, device_id_type=MESH)


def _in_place(bufs):
    return dict(operands=bufs, out_shapes=[jax.ShapeDtypeStruct(b.shape, b.dtype) for b in bufs],
                aliases={t: t for t in range(len(bufs))})


def _gather_rider(bufs, jobs):
    def copies(ins, outs, send, recv, base=0):
        x, y, c, chips = _place()
        out = []
        for t, over_ici, rows in jobs:
            ref = outs[t]
            if rows is not None:
                a, b, n = rows
                half = ref.shape[1] // 2
                rows = pl.ds(c * half + a * half // n, (b - a) * half // n)
            for px, py in chips:
                slot = 2 * x + y if over_ici else 2 * px + py
                piece = ref.at[slot] if rows is None else ref.at[slot, rows]
                out.append(_remote(piece, piece, send, recv, base + len(out), (px, py, c) if over_ici else (x, y, 1 - c)))
        return out

    return _Rider(n_copies=3 * len(jobs), copies=copies, **_in_place(bufs))


def _gather_and_pass_rider(buf):
    half = buf.shape[1] // 2

    def pieces(outs):
        x, y, c, chips = _place()
        rows = lambda core: pl.ds(core * half, half)
        mine = outs[0].at[2 * x + y, rows(c)]
        landed = [outs[0].at[2 * px + py, rows(c)] for px, py in chips]
        theirs = [outs[0].at[2 * px + py, rows(1 - c)] for px, py in chips]
        return (x, y, c, chips), mine, landed, theirs

    def start(ins, outs, send, recv, base=0):
        (x, y, c, chips), mine, _, _ = pieces(outs)
        for j, (px, py) in enumerate(chips):
            _remote(mine, mine, send, recv, base + j, (px, py, c)).start()

    def finish(ins, outs, send, recv, base=0):
        (x, y, c, chips), mine, landed, theirs = pieces(outs)
        sibling = (x, y, 1 - c)
        passed = []
        for j, (px, py) in enumerate(chips):
            _remote(landed[j], landed[j], send, recv, base + j, (px, py, c)).wait_recv()
            passed.append(_remote(landed[j], landed[j], send, recv, base + 3 + j, sibling))
            passed[-1].start()
        for j in range(3):
            _remote(theirs[j], theirs[j], send, recv, base + 3 + j, sibling).wait_recv()
        for j, (px, py) in enumerate(chips):
            _remote(mine, mine, send, recv, base + j, (px, py, c)).wait_send()
            passed[j].wait_send()

    return _Rider(n_copies=6, start=start, finish=finish, **_in_place([buf]))


def _swap_halves_rider(parts):
    n = len(parts)

    def copies(ins, outs, send, recv, base=0):
        x, y, c, _ = _place()
        out = []
        for t in range(n):
            half = ins[t].shape[1] // 2
            out.append(_remote(ins[t].at[:, pl.ds((1 - c) * half, half)], outs[t], send, recv, base + t,
                               (x, y, 1 - c)))
        return out

    shapes = [jax.ShapeDtypeStruct((p.shape[0], p.shape[1] // 2, p.shape[2]), p.dtype) for p in parts]
    return _Rider(parts, shapes, {}, n, copies)


def _scatter_rider(sums, landed, relations):
    n = len(sums)
    kept = [t for t in range(n) if landed[t] is not None]

    def copies(ins, outs, send, recv, base=0):
        x, y, c, chips = _place()
        out = []
        for t in range(n):
            for j in relations[t]:
                px, py = chips[j]
                out.append(_remote(ins[t].at[2 * px + py], outs[t].at[j], send, recv, base + len(out), (px, py, c)))
        return out

    shapes = [jax.ShapeDtypeStruct((3,) + s.shape[1:], s.dtype) for s in sums]
    return _Rider(list(sums) + [landed[t] for t in kept], shapes, {n + k: t for k, t in enumerate(kept)},
                  sum(len(r) for r in relations), copies)


def _share_rider(blocks, pieces):
    def copies(ins, outs, send, recv, base=0):
        x, y, c, _ = _place()
        out = []
        for k, (t, l, col) in enumerate(pieces):
            ref = outs[t].at[l]
            r2 = ref.shape[0] // 2
            width = ref.shape[1] // col[1]
            piece = ref.at[pl.ds(c * r2, r2), pl.ds(col[0] * width, width)]
            out.append(_remote(piece, piece, send, recv, base + k, (x, y, 1 - c)))
        return out

    return _Rider(n_copies=len(pieces), copies=copies, **_in_place(blocks))


def _gather_small(name, v):
    def body(v_ref, out_ref, send_sem, recv_sem, local_sem):
        x, y, c, _ = _place()
        me = 4 * x + 2 * y + c
        flips = [(fx, fy, fc) for fx in (0, 1) for fy in (0, 1) for fc in (0, 1)][1:]
        local = pltpu.make_async_copy(v_ref, out_ref.at[me], local_sem)
        local.start()
        copies = []
        for j, (fx, fy, fc) in enumerate(flips):
            peer = (x ^ fx, y ^ fy, c ^ fc)
            copies.append(pltpu.make_async_remote_copy(
                src_ref=v_ref, dst_ref=out_ref.at[me], send_sem=send_sem.at[j], recv_sem=recv_sem.at[j],
                device_id=peer, device_id_type=MESH))
        for cp in copies:
            cp.start()
        for j, (fx, fy, fc) in enumerate(flips):
            peer = (x ^ fx, y ^ fy, c ^ fc)
            pltpu.make_async_remote_copy(
                src_ref=v_ref, dst_ref=out_ref.at[4 * peer[0] + 2 * peer[1] + peer[2]], send_sem=send_sem.at[j],
                recv_sem=recv_sem.at[j], device_id=peer, device_id_type=MESH).wait_recv()
        for cp in copies:
            cp.wait_send()
        local.wait()

    return _pcall(body, name=name, in_specs=[ANY], out_specs=ANY,
                  out_shape=jax.ShapeDtypeStruct((8,) + v.shape, v.dtype),
                  scratch_shapes=[pltpu.SemaphoreType.DMA((7,)), pltpu.SemaphoreType.DMA((7,)),
                                  pltpu.SemaphoreType.DMA])(v)


def _fold(a, dil):
    if dil == 1:
        return a
    S, C = a.shape
    return a.reshape(S // dil, dil, C).transpose(1, 0, 2).reshape(S, C)


def _unfold(a, dil):
    if dil == 1:
        return a
    S, C = a.shape
    return a.reshape(dil, S // dil, C).transpose(1, 0, 2).reshape(S, C)


def _fold_groups(a):
    return jnp.stack([_fold(a[g] if a.ndim == 3 else a, d) for g, d in enumerate(ATTN_DILATIONS)])


def _unfold_groups(a):
    return jnp.stack([_unfold(a[g], d) for g, d in enumerate(ATTN_DILATIONS)])


class _NoComm:
    def __init__(self, weights):
        self.weights = weights
        self.grads = {}
        self.chip = jnp.zeros((1,), jnp.int32)

    def w(self, name):
        return self.weights[name]

    def run(self, fn, name, *args, **kw):
        return fn(name, *args, **kw)

    def grad(self, name, value):
        self.grads[name] = value


def _local_step(xs, tgt, attn_norm, ffn_norm, final_norm, comm):
    run = comm.run
    hf = run(_norm_fold, "fold", xs, attn_norm)
    qkv = run(_qkv_tiles, "qkv_own", None, hf, comm.w("wq_own"), None, comm.chip)
    for p in range(QKV_PIECES):
        qkv = run(_qkv_tiles, "qkv_piece%d" % p, p, hf, comm.w("wq%d" % p), qkv, comm.chip)
    o_f, lse_f = run(_attn_fwd, "attn_fwd", qkv)
    lse_t = _unfold_groups(lse_f)
    merged = run(_merge_fwd, "merge_fwd", o_f, lse_t)
    x1, h1 = run(_proj_res_norm, "attn_out", merged, comm.w("o"), xs, ffn_norm[0:1])
    gu0, act0 = run(_ffn_up, "ffn0_up", h1, comm.w("gu0"))
    x2, h2 = run(_proj_res_norm, "ffn0_down", act0, comm.w("d0"), x1, comm.w("pool_norm"))
    y, z, x3, h3 = run(_pool_fwd, "pool_fwd", h2, comm.w("p"), comm.w("pg"), comm.w("pool_scale"), x2, ffn_norm[1:2])
    gu1, act1 = run(_ffn_up, "ffn1_up", h3, comm.w("gu1"))
    loss, dx4, dx4_b, d_final = run(_proj_res_loss, "ffn1_down", act1, comm.w("d1"), x3, final_norm, tgt)

    dgu1 = run(_ffn_down_bwd, "ffn1_down_bwd", dx4_b, comm.w("d1"), gu1)
    comm.grad("d1", run(_mm_tn, "ffn1_down_dw", act1, dx4_b, FF_SHARD, 2048))
    comm.grad("gu1", run(_ffn_dw_up, "ffn1_up_dw", h3, dgu1))
    dx3, d_ffn1 = run(_ffn_dh, "ffn1_up_dh", dgu1, comm.w("gu1"), x3, ffn_norm[1:2], dx4)

    du, dw_pg, d_scale = run(_pool_bwd, "pool_bwd", dx3, z, y, comm.w("pool_scale"), comm.w("pg"))
    comm.grad("pg", dw_pg)
    comm.grad("p", run(_mm_tn, "pool_in_dw", h2, du, D_MODEL, h2.shape[0]))
    dx2, dx2_b, d_pool = run(_dh_norm_bwd, "pool_in_dh", du, comm.w("p"), x2, comm.w("pool_norm"), dx3)

    dgu0 = run(_ffn_down_bwd, "ffn0_down_bwd", dx2_b, comm.w("d0"), gu0)
    comm.grad("d0", run(_mm_tn, "ffn0_down_dw", act0, dx2_b, FF_SHARD, 2048))
    comm.grad("gu0", run(_ffn_dw_up, "ffn0_up_dw", h1, dgu0))
    dx1, d_ffn0 = run(_ffn_dh, "ffn0_up_dh", dgu0, comm.w("gu0"), x1, ffn_norm[0:1], dx2)

    comm.grad("o", run(_mm_tn, "attn_out_dw", merged, dx1, D_MODEL, 2048))
    do_f, dl_t = run(_merge_bwd, "merge_bwd", dx1, comm.w("o"), o_f, lse_t)
    dqkv = run(_attn_bwd, "attn_bwd", qkv, do_f, lse_f, _fold_groups(dl_t))
    for p in range(QKV_PIECES):
        comm.grad("qkv%d" % p, run(_qkv_dw, "qkv_dw%d" % p, p, hf, dqkv))
    dhf = None
    for g in range(N_GROUPS):
        dhf = run(_qkv_dh, "qkv_dh%d" % g, g, dqkv, [comm.w("wq%d" % p) for p in range(QKV_PIECES)], dhf)
    grad_x, d_attn = run(_rms_bwd_folded, "norm_attn_bwd", xs, attn_norm, dhf, dx1)

    small = dict(attn=d_attn, ffn0=d_ffn0, ffn1=d_ffn1, final=d_final, pool=d_pool, scale=d_scale)
    return loss, grad_x, small


TENSORS = ("qkv", "o", "p", "pg", "gu0", "gu1", "d0", "d1")


def _block_of(name):
    if name[:-1] in ("gu", "d"):
        return name[:-1], int(name[-1]), (0, 1)
    if name[:-1] == "qkv":
        return "qkv", 0, (int(name[-1]), QKV_PIECES)
    return name, 0, (0, 1)


class _MeshComm:
    def __init__(self, bufs, shapes, chip_arr, core_arr, pg_rows):
        self.bufs = dict(bufs)
        self.shapes = shapes
        self.chip, self.chip_arr, self.core_arr, self.pg_rows = chip_arr, chip_arr, core_arr, pg_rows
        self.parts, self.sums, self.blocks, self.landed, self.arrived = {}, {}, {}, {}, {}
        move = lambda ici=(), d2d=(): lambda: self.gather(ici, d2d)
        whole = lambda name: lambda: self.gather_and_pass(name)
        swap = lambda *names: lambda: self.swap(names)
        scatter = lambda *names: lambda: self.scatter(names)
        share = lambda *names: lambda: self.share(names)
        self.plan = {
            "fold": [whole("wq0")],
            "qkv_own": [whole("wq1")],
            "qkv_piece0": [whole("wq2")],
            "qkv_piece1": [move(ici=["o", "gu0[0:1/4]"])],
            "qkv_piece2": [move(ici=["gu0[1:2/4]"], d2d=["o", "gu0[0:1/4]"])],
            "attn_fwd": [move(ici=["gu0[2:4/4]", "d0[0:1/2]"], d2d=["gu0[1:2/4]"])],
            "merge_fwd": [move(ici=["p", "pg"], d2d=["gu0[2:4/4]", "d0[0:1/2]"])],
            "attn_out": [move(ici=["d0[1:2/2]", "pool_norm", "pool_scale"], d2d=["p", "pg"])],
            "ffn0_up": [move(ici=["gu1[0:3/4]"], d2d=["d0[1:2/2]"])],
            "ffn0_down": [move(ici=["gu1[3:4/4]"], d2d=["gu1[0:3/4]"])],
            "pool_fwd": [move(ici=["d1"], d2d=["gu1[3:4/4]"])],
            "ffn1_up": [move(d2d=["d1"])],
            "ffn1_up_dh": [swap("d1", "gu1")],
            "pool_bwd": [scatter("d1{01}")],
            "pool_in_dh": [scatter("d1{2}")],
            "ffn0_down_bwd": [scatter("gu1{01}")],
            "ffn0_down_dw": [scatter("gu1{2}")],
            "ffn0_up_dw": [share("gu1", "d1")],
            "ffn0_up_dh": [swap("pg", "p", "d0", "gu0")],
            "merge_bwd": [swap("o")],
            "attn_bwd": [scatter("gu0", "d0", "o")],
            "qkv_dw0": [share("d0"), scatter("p", "pg")],
            "qkv_dw1": [share("gu0", "o"), swap("qkv0")],
            "qkv_dw2": [share("p", "pg"), scatter("qkv0"), swap("qkv1")],
            "qkv_dh0": [share("qkv0"), scatter("qkv1"), swap("qkv2")],
            "qkv_dh1": [share("qkv1"), scatter("qkv2")],
            "qkv_dh2": [share("qkv2")],
        }

    def gather_and_pass(self, name):
        return _gather_and_pass_rider(self.bufs[name]), lambda res: self.bufs.update({name: res[0]})

    def gather(self, ici, d2d):
        names, jobs = [], []
        for over_ici, specs in ((True, ici), (False, d2d)):
            for spec in specs:
                name, _, rows = spec.partition("[")
                if name not in names:
                    names.append(name)
                rng = None
                if name in TENSORS:
                    ab, _, n = (rows[:-1] or "0:1/1").partition("/")
                    rng = (int(ab.split(":")[0]), int(ab.split(":")[1]), int(n))
                jobs.append((names.index(name), over_ici, rng))
        return _gather_rider([self.bufs[n] for n in names], jobs), lambda res: self.bufs.update(zip(names, res))

    def swap(self, names):
        def done(res):
            sums = _add_my_half("chip_sum_" + "_".join(names), [self.parts[n] for n in names], res, self.core_arr)
            self.sums.update(zip(names, sums))
        return _swap_halves_rider([self.parts[n] for n in names]), done

    def scatter(self, specs):
        names = [s.partition("{")[0] for s in specs]
        relations = [tuple(int(ch) for ch in s.partition("{")[2][:-1]) or (0, 1, 2) for s in specs]

        def done(res):
            for n, rel, landed in zip(names, relations, res):
                self.landed[n] = landed
                self.arrived[n] = self.arrived.get(n, ()) + rel
                if len(self.arrived[n]) < 3:
                    continue
                key, layer, col = _block_of(n)
                self.blocks[key] = _owner_sum("owner_sum_" + n, self.sums[n], landed, self.shapes[key],
                                              self.chip_arr, self.core_arr, self.blocks.get(key), layer, col)
        rider = _scatter_rider([self.sums[n] for n in names], [self.landed.get(n) for n in names], relations)
        return rider, done

    def share(self, names):
        keys, pieces = [], []
        for n in names:
            key, layer, col = _block_of(n)
            if key not in keys:
                keys.append(key)
            pieces.append((keys.index(key), layer, col))
        return _share_rider([self.blocks[k] for k in keys], pieces), lambda res: self.blocks.update(zip(keys, res))

    def run(self, fn, name, *args, **kw):
        made = [make() for make in self.plan.get(name, [])]
        if not made:
            return fn(name, *args, **kw)
        riders = [r for r, _ in made]
        out = _ride(riders[0] if len(riders) == 1 else _riders(*riders), fn, name, *args, **kw)
        for r, done in made:
            done(r.results)
        return out

    def w(self, name):
        b = self.bufs[name]
        if name in ("o", "p", "d0", "d1"):
            return b.reshape(-1, b.shape[-1])
        if name == "pg":
            b = b.reshape(N_CHIPS, 4, self.pg_rows, POOL_DIM).transpose(1, 0, 2, 3)
            return b.reshape(4, POOL_DIM, POOL_DIM)
        if name in ("pool_norm", "pool_scale"):
            return b.reshape(1, -1)
        return b

    def grad(self, name, value):
        if name == "pg":
            value = value.reshape(4, N_CHIPS, self.pg_rows, POOL_DIM).transpose(1, 0, 2, 3)
            value = value.reshape(N_CHIPS, 4 * self.pg_rows, POOL_DIM).astype(BF16)
        elif name in ("o", "p", "d0", "d1"):
            value = value.reshape(N_CHIPS, value.shape[0] // N_CHIPS, value.shape[1])
        self.parts[name] = value


def kernel(x, attn_norm, w_qkv, w_attn_out, pool_norm, w_pool_in, w_pool_group, pool_scale, ffn_norm, w_ffn_gate_up, w_ffn_down, final_norm, loss_target, m_attn_norm, m_w_qkv, m_w_attn_out, m_pool_norm, m_w_pool_in, m_w_pool_group, m_pool_scale, m_ffn_norm, m_w_ffn_gate_up, m_w_ffn_down, m_final_norm, v_attn_norm, v_w_qkv, v_w_attn_out, v_pool_norm, v_w_pool_in, v_w_pool_group, v_pool_scale, v_ffn_norm, v_w_ffn_gate_up, v_w_ffn_down, v_final_norm):
    D = D_MODEL
    chip = 2 * lax.axis_index("x") + lax.axis_index("y")
    core = lax.axis_index("c")
    pg_rows = w_pool_group.shape[2]

    chip_arr = chip.astype(jnp.int32).reshape(1)
    core_arr = core.astype(jnp.int32).reshape(1)

    pieces = _cast_qkv("cast_qkv", w_qkv, chip_arr)
    bufs = dict(zip(["wq_own"] + ["wq%d" % p for p in range(QKV_PIECES)], pieces))
    shards = [(w_attn_out, 0, BF16), (w_pool_in, 0, BF16), (w_pool_group.reshape(1, 4 * pg_rows, POOL_DIM), 0, BF16),
              (w_ffn_gate_up, 0, BF16), (w_ffn_gate_up, 1, BF16), (w_ffn_down, 0, BF16), (w_ffn_down, 1, BF16),
              (pool_norm[None], 0, F32), (pool_scale[None], 0, F32)]
    bufs.update(zip(TENSORS[1:] + ("pool_norm", "pool_scale"), _cast_many("cast_rest", shards, chip_arr)))

    as_block = lambda a: a.reshape((-1,) + a.shape[-2:]) if a.ndim == 3 else a.reshape(1, -1, a.shape[-1])
    owned = dict(qkv=(w_qkv, m_w_qkv, v_w_qkv), o=(w_attn_out, m_w_attn_out, v_w_attn_out),
                 p=(w_pool_in, m_w_pool_in, v_w_pool_in), pg=(w_pool_group, m_w_pool_group, v_w_pool_group),
                 gu=(w_ffn_gate_up, m_w_ffn_gate_up, v_w_ffn_gate_up), d=(w_ffn_down, m_w_ffn_down, v_w_ffn_down))
    comm = _MeshComm(bufs, {k: as_block(wmv[0]).shape for k, wmv in owned.items()}, chip_arr, core_arr, pg_rows)
    loss_part, grad_x, small = _local_step(x[0], loss_target[0], attn_norm, ffn_norm, final_norm.reshape(1, D), comm)

    rows = jnp.concatenate([small["attn"], small["ffn0"], small["ffn1"], small["final"], small["pool"],
                            small["scale"], jnp.pad(loss_part, ((0, 0), (0, D - 1))), jnp.zeros((1, D), F32)], axis=0)
    all_rows = _gather_small("gather_gain_grads", rows)
    tot = _sum_leading("sum_gain_grads", all_rows, F32)
    loss = tot[6, 0]
    g_attn_norm = tot[0:1]
    g_ffn_norm = tot[1:3]
    g_final_norm = tot[3]
    g_pool_norm = lax.dynamic_slice(tot, (4, chip * POOL_DIM), (1, POOL_DIM))
    g_pool_scale = lax.dynamic_slice(tot, (5, chip * POOL_DIM), (1, POOL_DIM))

    as_rows = lambda a: a.reshape(-1, a.shape[-1])
    res = _adamw("adamw_weights", [(as_rows(w), as_rows(comm.blocks[k]), as_rows(m), as_rows(v))
                                   for k, (w, m, v) in owned.items()])
    updated = {k: [a.reshape(owned[k][0].shape) for a in r] for k, r in zip(owned, res)}
    gains = [(attn_norm, g_attn_norm, m_attn_norm, v_attn_norm), (pool_norm, g_pool_norm, m_pool_norm, v_pool_norm),
             (pool_scale, g_pool_scale, m_pool_scale, v_pool_scale), (ffn_norm, g_ffn_norm, m_ffn_norm, v_ffn_norm),
             (final_norm, g_final_norm, m_final_norm, v_final_norm)]
    small_res = _adamw_small("adamw_gains", [tuple(as_rows(a) for a in item) for item in gains])
    small_res = [[a.reshape(item[0].shape) for a in r] for r, item in zip(small_res, gains)]
    results = [small_res[0], updated["qkv"], updated["o"], small_res[1], updated["p"], updated["pg"], small_res[2],
               small_res[3], updated["gu"], updated["d"], small_res[4]]
    grads = [r[0] for r in results]
    deltas = [r[1] for r in results]
    new_m = [r[2] for r in results]
    new_v = [r[3] for r in results]
    return (loss, grad_x[None], *grads, *deltas, *new_m, *new_v)
```

```python
import jax
import jax.numpy as jnp
from jax import lax
from jax.experimental import pallas as pl
from jax.experimental.pallas import tpu as pltpu

F32 = jnp.float32
BF16 = jnp.bfloat16
MESH = pl.DeviceIdType.MESH

D_MODEL = 1024
N_HEADS = 8
HEAD_DIM = 128
Q_BLOCK = 128
ATTN_DILATIONS = (1, 4, 16)
N_GROUPS = 3
D_FF = 2816
N_CHIPS = 4
FF_SHARD = 2 * D_FF // N_CHIPS
QKV_SHARD = 3 * 3 * D_MODEL // N_CHIPS
QKV_TILE = 768
POOL_DIM = 256
POOL_HALO = 16
RMS_EPS = 1e-6
NEG = -1e30
ADAM_LR = 0.001
ADAM_B1 = 0.9
ADAM_B2 = 0.999
ADAM_EPS = 1e-08
ADAM_WD = 0.01
ADAM_STEP = 10
VMEM_LIMIT = 56 * 1024 * 1024

_DN = {
    "nn": (((1,), (0,)), ((), ())),
    "nt": (((1,), (1,)), ((), ())),
    "tn": (((0,), (0,)), ((), ())),
}


class _Rider:
    def __init__(self, operands, out_shapes, aliases, n_copies, copies=None, start=None, finish=None):
        self.operands = list(operands)
        self.out_shapes = list(out_shapes)
        self.aliases = dict(aliases)
        self.n_copies = n_copies
        self.results = None

        def start_copies(ins, outs, send, recv, base=0):
            for cp in copies(ins, outs, send, recv, base):
                cp.start()

        def wait_copies(ins, outs, send, recv, base=0):
            for cp in copies(ins, outs, send, recv, base):
                cp.wait()

        self.start = start or start_copies
        self.finish = finish or wait_copies


def _riders(*riders):
    operands, out_shapes, aliases, offsets = [], [], {}, []
    n = 0
    for r in riders:
        offsets.append((len(operands), len(out_shapes), n))
        aliases.update({len(operands) + i: len(out_shapes) + o for i, o in r.aliases.items()})
        operands += r.operands
        out_shapes += r.out_shapes
        n += r.n_copies

    def each(which):
        def go(ins, outs, send, recv, base=0):
            for r, (i0, o0, s0) in zip(riders, offsets):
                getattr(r, which)(ins[i0:i0 + len(r.operands)], outs[o0:o0 + len(r.out_shapes)], send, recv,
                                  base + s0)
        return go

    both = _Rider(operands, out_shapes, aliases, n, start=each("start"), finish=each("finish"))
    both.parts = (riders, offsets)
    return both


_pending_rider = []


def _ride(rider, fn, *args, **kw):
    _pending_rider.append(rider)
    out = fn(*args, **kw)
    assert not _pending_rider
    if hasattr(rider, "parts"):
        for r, (_, o0, _) in zip(*rider.parts):
            r.results = rider.results[o0:o0 + len(r.out_shapes)]
    return out


def _pcall(body, prefetch=0, **kw):
    def build(body, kw):
        if not prefetch:
            return pl.pallas_call(body, **kw)
        kw = dict(kw)
        grid_spec = pltpu.PrefetchScalarGridSpec(
            num_scalar_prefetch=prefetch, grid=kw.pop("grid"), in_specs=kw.pop("in_specs"),
            out_specs=kw.pop("out_specs"), scratch_shapes=kw.pop("scratch_shapes", []))
        return pl.pallas_call(body, grid_spec=grid_spec, **kw)

    if not _pending_rider:
        return build(body, kw)
    rider = _pending_rider.pop()
    grid = kw.get("grid", ())
    in_specs = list(kw.get("in_specs", []))
    single = not isinstance(kw["out_shape"], (list, tuple))
    out_shape = [kw["out_shape"]] if single else list(kw["out_shape"])
    out_specs = [kw["out_specs"]] if single else list(kw["out_specs"])
    scratch = list(kw.get("scratch_shapes", []))
    n_in, n_out, n_scr = len(in_specs), len(out_shape), len(scratch)
    r_in, r_out = len(rider.operands), len(rider.out_shapes)

    def wrapped(*refs):
        scalars, refs = refs[:prefetch], refs[prefetch:]
        ins, rins = refs[:n_in], refs[n_in:n_in + r_in]
        outs = refs[n_in + r_in:n_in + r_in + n_out]
        routs = refs[n_in + r_in + n_out:n_in + r_in + n_out + r_out]
        scr = refs[n_in + r_in + n_out + r_out:n_in + r_in + n_out + r_out + n_scr]
        send, recv = refs[-2:]
        ids = [pl.program_id(a) for a in range(len(grid))]
        first, last = True, True
        for a, pid in enumerate(ids):
            first = jnp.logical_and(first, pid == 0)
            last = jnp.logical_and(last, pid == grid[a] - 1)
        if grid:
            pl.when(first)(lambda: rider.start(rins, routs, send, recv))
        else:
            rider.start(rins, routs, send, recv)
        body(*scalars, *ins, *outs, *scr)
        if grid:
            pl.when(last)(lambda: rider.finish(rins, routs, send, recv))
        else:
            rider.finish(rins, routs, send, recv)

    any_spec = pl.BlockSpec(memory_space=pl.ANY)
    kw2 = dict(kw)
    kw2.update(
        in_specs=in_specs + [any_spec] * r_in, out_specs=out_specs + [any_spec] * r_out,
        out_shape=out_shape + rider.out_shapes,
        scratch_shapes=scratch + [pltpu.SemaphoreType.DMA((rider.n_copies,)),
                                  pltpu.SemaphoreType.DMA((rider.n_copies,))],
        input_output_aliases={**kw.get("input_output_aliases", {}),
                              **{prefetch + n_in + i: n_out + o for i, o in rider.aliases.items()}})
    if grid:
        kw2["compiler_params"] = _params(("arbitrary",) * len(grid))
    call = build(wrapped, kw2)

    def run(*operands):
        res = call(*operands, *rider.operands)
        rider.results = list(res[n_out:])
        return res[0] if single else list(res[:n_out])

    return run


def _params(sem):
    return pltpu.CompilerParams(dimension_semantics=sem, vmem_limit_bytes=VMEM_LIMIT)


def _dot(a, b, mode):
    return lax.dot_general(a, b, _DN[mode], preferred_element_type=F32)


def _mm(name, mode, grid, a, a_spec, b, b_spec, out_shape, o_spec, acc_shape):
    nk = grid[-1]

    def body(a_ref, b_ref, o_ref, *acc):
        part = _dot(a_ref[...].astype(BF16), b_ref[...].astype(BF16), mode)
        if nk == 1:
            o_ref[...] = part.astype(o_ref.dtype)
            return
        acc_ref, = acc
        k = pl.program_id(len(grid) - 1)

        @pl.when(k == 0)
        def _():
            acc_ref[...] = part

        @pl.when(k > 0)
        def _():
            acc_ref[...] += part

        @pl.when(k == nk - 1)
        def _():
            o_ref[...] = acc_ref[...].astype(o_ref.dtype)

    scratch = [] if nk == 1 else [pltpu.VMEM(acc_shape, F32)]
    sem = ("parallel",) * (len(grid) - 1) + ("arbitrary",)
    return _pcall(body, name=name, grid=grid, in_specs=[a_spec, b_spec], out_specs=o_spec, out_shape=out_shape,
                  scratch_shapes=scratch, compiler_params=_params(sem))(a, b)


def _mm_tn(name, a, b, tm, tk):
    T, M = a.shape
    N = b.shape[1]
    return _mm(name, "tn", (M // tm, T // tk), a, pl.BlockSpec((tk, tm), lambda i, k: (k, i)),
               b, pl.BlockSpec((tk, N), lambda i, k: (k, 0)),
               jax.ShapeDtypeStruct((M, N), BF16), pl.BlockSpec((tm, N), lambda i, k: (i, 0)), (tm, N))


def _norm_bwd_rows(xf, gain, dh, dres):
    r = lax.rsqrt(jnp.mean(xf * xf, axis=-1, keepdims=True) + RMS_EPS)
    xh = xf * r
    t = dh * gain
    dx = dres + r * (t - xh * jnp.mean(t * xh, axis=-1, keepdims=True))
    return dx, jnp.sum(dh * xh, axis=0, keepdims=True)


def _accumulate(ref, value, first):
    @pl.when(first)
    def _():
        ref[...] = value

    @pl.when(jnp.logical_not(first))
    def _():
        ref[...] += value


def _rms_bwd_folded(name, x, g, dhf, dres, tb=512):
    S, D = x.shape

    def body(x_ref, g_ref, d0_ref, d1_ref, d2_ref, dres_ref, dx_ref, dg_ref, dh_ref):
        chunks = _lane_chunks(D)
        for c, cols in enumerate(chunks):
            dh_ref[c] = d0_ref[0, :, cols].astype(F32)
        for dil, ref in ((ATTN_DILATIONS[1], d1_ref), (ATTN_DILATIONS[2], d2_ref)):
            n = tb // dil
            for k in range(dil):
                for c, cols in enumerate(chunks):
                    dh_ref[c, _strided_rows(k, n, dil), :] += ref[k, :, cols].astype(F32)
        dh = jnp.concatenate([dh_ref[c] for c in range(len(chunks))], axis=1)
        dx, dg = _norm_bwd_rows(x_ref[...], g_ref[...], dh, dres_ref[...])
        dx_ref[...] = dx
        _accumulate(dg_ref, dg, pl.program_id(0) == 0)

    views, specs = _folded_views(dhf, tb)
    row = pl.BlockSpec((tb, D), lambda i: (i, 0))
    vec = pl.BlockSpec((1, D), lambda i: (0, 0))
    return _pcall(body, name=name, grid=(S // tb,), in_specs=[row, vec] + specs + [row], out_specs=[row, vec],
                  out_shape=[jax.ShapeDtypeStruct((S, D), F32), jax.ShapeDtypeStruct((1, D), F32)],
                  scratch_shapes=[pltpu.VMEM((D // LANES, tb, LANES), F32)],
                  compiler_params=_params(("arbitrary",)))(x, g, *views, dres)


def _proj_res_norm(name, a, w, res, gain, tm=512):
    M, K = a.shape
    D = w.shape[1]

    def body(a_ref, w_ref, res_ref, g_ref, x_ref, h_ref):
        xf = res_ref[...] + _dot(a_ref[...], w_ref[...], "nn")
        x_ref[...] = xf
        r = lax.rsqrt(jnp.mean(xf * xf, axis=-1, keepdims=True) + RMS_EPS)
        h_ref[...] = ((xf * r) * g_ref[...]).astype(h_ref.dtype)

    row = pl.BlockSpec((tm, D), lambda i: (i, 0))
    return _pcall(body, name=name, grid=(M // tm,),
                  in_specs=[pl.BlockSpec((tm, K), lambda i: (i, 0)), pl.BlockSpec((K, D), lambda i: (0, 0)), row,
                            pl.BlockSpec((1, D), lambda i: (0, 0))],
                  out_specs=[row, row],
                  out_shape=[jax.ShapeDtypeStruct((M, D), F32), jax.ShapeDtypeStruct((M, D), BF16)],
                  compiler_params=_params(("parallel",)))(a, w, res, gain)


def _proj_res_loss(name, a, w, res, gain, tgt, tm=512):
    M, K = a.shape
    D = w.shape[1]

    def body(a_ref, w_ref, res_ref, g_ref, t_ref, loss_ref, dx_ref, dxb_ref, dg_ref):
        first = pl.program_id(0) == 0
        xf = res_ref[...] + _dot(a_ref[...], w_ref[...], "nn")
        r = lax.rsqrt(jnp.mean(xf * xf, axis=-1, keepdims=True) + RMS_EPS)
        xh = xf * r
        gv = g_ref[...]
        e = xh * gv - t_ref[...]
        lp = 0.5 * jnp.sum(jnp.mean(e * e, axis=-1, keepdims=True), axis=0, keepdims=True)
        dy = e * (1.0 / D)
        t = dy * gv
        dx = r * (t - xh * jnp.mean(t * xh, axis=-1, keepdims=True))
        dx_ref[...] = dx
        dxb_ref[...] = dx.astype(dxb_ref.dtype)
        _accumulate(dg_ref, jnp.sum(dy * xh, axis=0, keepdims=True), first)
        _accumulate(loss_ref, lp, first)

    row = pl.BlockSpec((tm, D), lambda i: (i, 0))
    vec = pl.BlockSpec((1, D), lambda i: (0, 0))
    return _pcall(body, name=name, grid=(M // tm,),
                  in_specs=[pl.BlockSpec((tm, K), lambda i: (i, 0)), pl.BlockSpec((K, D), lambda i: (0, 0)), row,
                            vec, row],
                  out_specs=[pl.BlockSpec((1, 1), lambda i: (0, 0)), row, row, vec],
                  out_shape=[jax.ShapeDtypeStruct((1, 1), F32), jax.ShapeDtypeStruct((M, D), F32),
                             jax.ShapeDtypeStruct((M, D), BF16), jax.ShapeDtypeStruct((1, D), F32)],
                  compiler_params=_params(("arbitrary",)))(a, w, res, gain, tgt)


def _dh_norm_bwd(name, d, w, x, gain, dres, tm=512):
    M, N = d.shape
    D = w.shape[0]

    def body(d_ref, w_ref, x_ref, g_ref, dres_ref, dx_ref, dxb_ref, dg_ref):
        dh = _dot(d_ref[...].astype(BF16), w_ref[...], "nt")
        dx, dg = _norm_bwd_rows(x_ref[...], g_ref[...], dh, dres_ref[...])
        dx_ref[...] = dx
        dxb_ref[...] = dx.astype(dxb_ref.dtype)
        _accumulate(dg_ref, dg, pl.program_id(0) == 0)

    row = pl.BlockSpec((tm, D), lambda i: (i, 0))
    vec = pl.BlockSpec((1, D), lambda i: (0, 0))
    return _pcall(body, name=name, grid=(M // tm,),
                  in_specs=[pl.BlockSpec((tm, N), lambda i: (i, 0)), pl.BlockSpec((D, N), lambda i: (0, 0)), row, vec,
                            row],
                  out_specs=[row, row, vec],
                  out_shape=[jax.ShapeDtypeStruct((M, D), F32), jax.ShapeDtypeStruct((M, D), BF16),
                             jax.ShapeDtypeStruct((1, D), F32)],
                  compiler_params=_params(("arbitrary",)))(d, w, x, gain, dres)


def _sigmoid(v):
    return 0.5 * jnp.tanh(0.5 * v) + 0.5


def _ffn_up(name, h, w_gu, tm=1024):
    S, D = h.shape
    W = FF_SHARD

    def body(h_ref, wg_ref, wu_ref, gu_ref, act_ref):
        hv = h_ref[...]
        gate = _dot(hv, wg_ref[...], "nn")
        up = _dot(hv, wu_ref[...], "nn")
        gu_ref[0] = gate.astype(gu_ref.dtype)
        gu_ref[1] = up.astype(gu_ref.dtype)
        sig = _sigmoid(gate)
        act_ref[...] = ((gate * sig) * up).astype(act_ref.dtype)

    return _pcall(
        body, name=name, grid=(2, S // tm),
        in_specs=[pl.BlockSpec((tm, D), lambda j, i: (i, 0)),
                  pl.BlockSpec((None, D, W), lambda j, i: (j, 0, 0)),
                  pl.BlockSpec((None, D, W), lambda j, i: (j + 2, 0, 0))],
        out_specs=[pl.BlockSpec((2, tm, W), lambda j, i: (0, i, j)), pl.BlockSpec((tm, W), lambda j, i: (i, j))],
        out_shape=[jax.ShapeDtypeStruct((2, S, D_FF), BF16), jax.ShapeDtypeStruct((S, D_FF), BF16)],
        compiler_params=_params(("parallel", "parallel")))(h, w_gu, w_gu)


def _ffn_down_bwd(name, dx, w_down, gu, tm=1024):
    S, D = dx.shape
    W = FF_SHARD

    def body(dx_ref, w_ref, gu_ref, dgu_ref):
        dact = _dot(dx_ref[...].astype(BF16), w_ref[...], "nt")
        gate = gu_ref[0].astype(F32)
        up = gu_ref[1].astype(F32)
        sig = _sigmoid(gate)
        silu = gate * sig
        dgu_ref[0] = (dact * up * (sig * (1.0 + gate * (1.0 - sig)))).astype(dgu_ref.dtype)
        dgu_ref[1] = (dact * silu).astype(dgu_ref.dtype)

    blk = pl.BlockSpec((2, tm, W), lambda j, i: (0, i, j))
    return _pcall(
        body, name=name, grid=(2, S // tm),
        in_specs=[pl.BlockSpec((tm, D), lambda j, i: (i, 0)), pl.BlockSpec((W, D), lambda j, i: (j, 0)), blk],
        out_specs=blk, out_shape=jax.ShapeDtypeStruct((2, S, D_FF), BF16),
        compiler_params=_params(("parallel", "parallel")))(dx, w_down, gu)


def _ffn_dw_up(name, h, dgu, tk=2048):
    S, D = h.shape
    W = FF_SHARD
    tk = min(tk, S)
    return _mm(name, "tn", (N_CHIPS, S // tk), h, pl.BlockSpec((tk, D), lambda s, k: (k, 0)),
               dgu, pl.BlockSpec((None, tk, W), lambda s, k: (s // 2, k, s % 2)),
               jax.ShapeDtypeStruct((N_CHIPS, D, W), BF16), pl.BlockSpec((None, D, W), lambda s, k: (s, 0, 0)),
               (D, W))


def _ffn_dh(name, dgu, w_gu, x, gain, dres, tm=512):
    S = dgu.shape[1]
    W = FF_SHARD

    def body(a_ref, w_hbm, x_ref, g_ref, dres_ref, dx_ref, dg_ref, w_ref, sems):
        first = pl.program_id(0) == 0

        @pl.when(first)
        def _():
            copies = [pltpu.make_async_copy(w_hbm.at[s], w_ref.at[:, pl.ds(s * W, W)], sems.at[s])
                      for s in range(N_CHIPS)]
            for cp in copies:
                cp.start()
            for cp in copies:
                cp.wait()

        dh = _dot(a_ref[0], w_ref[:, :D_FF], "nt") + _dot(a_ref[1], w_ref[:, D_FF:], "nt")
        dx, dg = _norm_bwd_rows(x_ref[...], g_ref[...], dh, dres_ref[...])
        dx_ref[...] = dx
        _accumulate(dg_ref, dg, first)

    row = pl.BlockSpec((tm, D_MODEL), lambda i: (i, 0))
    vec = pl.BlockSpec((1, D_MODEL), lambda i: (0, 0))
    return _pcall(body, name=name, grid=(S // tm,),
                  in_specs=[pl.BlockSpec((2, tm, D_FF), lambda i: (0, i, 0)), pl.BlockSpec(memory_space=pl.ANY),
                            row, vec, row],
                  out_specs=[row, vec],
                  out_shape=[jax.ShapeDtypeStruct((S, D_MODEL), F32), jax.ShapeDtypeStruct((1, D_MODEL), F32)],
                  scratch_shapes=[pltpu.VMEM((D_MODEL, 2 * D_FF), BF16), pltpu.SemaphoreType.DMA((N_CHIPS,))],
                  compiler_params=_params(("arbitrary",)))(dgu, w_gu, x, gain, dres)


FOLD_TOKENS = 1024
LANES = 128


def _lane_chunks(width):
    return [slice(c * LANES, (c + 1) * LANES) for c in range(width // LANES)]


def _strided_rows(r, n, dil):
    return pl.ds(r, n) if dil == 1 else pl.ds(r, n, stride=dil)


def _norm_fold(name, x, gain):
    S, D = x.shape
    chunks = _lane_chunks(D)

    def body(x_ref, g_ref, hf_hbm, xc_ref, hs_ref, sems):
        i = pl.program_id(0)
        xf = x_ref[...]
        inv = lax.rsqrt(jnp.mean(xf * xf, axis=-1, keepdims=True) + RMS_EPS)
        h = (xf * inv) * g_ref[...]
        hs_ref[0] = h.astype(BF16)
        for c, cols in enumerate(chunks):
            xc_ref[c] = h[:, cols]
        copies = []
        for g, dil in enumerate(ATTN_DILATIONS):
            n = FOLD_TOKENS // dil
            for r in range(dil):
                if dil > 1:
                    for c, cols in enumerate(chunks):
                        hs_ref[g, r * n:(r + 1) * n, cols] = xc_ref[c, _strided_rows(r, n, dil), :].astype(BF16)
                cp = pltpu.make_async_copy(hs_ref.at[g, pl.ds(r * n, n)],
                                           hf_hbm.at[g, pl.ds(r * (S // dil) + i * n, n)], sems.at[len(copies)])
                cp.start()
                copies.append(cp)
        for cp in copies:
            cp.wait()

    return _pcall(body, name=name, grid=(S // FOLD_TOKENS,),
                  in_specs=[pl.BlockSpec((FOLD_TOKENS, D), lambda i: (i, 0)), pl.BlockSpec((1, D), lambda i: (0, 0))],
                  out_specs=pl.BlockSpec(memory_space=pl.ANY),
                  out_shape=jax.ShapeDtypeStruct((N_GROUPS, S, D), BF16),
                  scratch_shapes=[pltpu.VMEM((D // LANES, FOLD_TOKENS, LANES), F32),
                                  pltpu.VMEM((N_GROUPS, FOLD_TOKENS, D), BF16),
                                  pltpu.SemaphoreType.DMA((sum(ATTN_DILATIONS),))],
                  compiler_params=_params(("arbitrary",)))(x, gain)


def _qkv_tiles(name, piece, hf, w, qkv, chip, tm=1024):
    S = hf.shape[1]
    per_group = 3 * D_MODEL // QKV_TILE

    def tile(q, c_ref):
        if piece is None:
            return QKV_PIECES * c_ref[0] + q
        return QKV_PIECES * ((c_ref[0] + 1 + q) % N_CHIPS) + piece

    def body(*refs):
        a_ref, w_ref, o_ref = refs[1], refs[2], refs[-1]
        o_ref[...] = _dot(a_ref[...], w_ref[...], "nn").astype(o_ref.dtype)

    if piece is None:
        w_spec = pl.BlockSpec((D_MODEL, QKV_TILE), lambda q, i, c_ref: (0, q))
    else:
        w_spec = pl.BlockSpec((None, D_MODEL, QKV_TILE), lambda q, i, c_ref: ((c_ref[0] + 1 + q) % N_CHIPS, 0, 0))
    in_specs = [pl.BlockSpec((None, tm, D_MODEL), lambda q, i, c_ref: (tile(q, c_ref) // per_group, i, 0)), w_spec]
    operands = [chip, hf, w]
    aliases = {}
    if qkv is not None:
        in_specs.append(pl.BlockSpec(memory_space=pl.ANY))
        operands.append(qkv)
        aliases = {3: 0}
    return _pcall(
        body, prefetch=1, name=name, grid=(N_CHIPS - 1, S // tm), in_specs=in_specs,
        out_specs=pl.BlockSpec((None, tm, QKV_TILE),
                               lambda q, i, c_ref: (tile(q, c_ref) // per_group, i, tile(q, c_ref) % per_group)),
        out_shape=jax.ShapeDtypeStruct((N_GROUPS, S, 3 * D_MODEL), BF16), input_output_aliases=aliases,
        compiler_params=_params(("arbitrary", "arbitrary")))(*operands)


QKV_PIECES = QKV_SHARD // QKV_TILE


def _qkv_dw(name, p, hf, dqkv):
    S = hf.shape[1]
    per_group = 3 * D_MODEL // QKV_TILE
    tile = lambda s: QKV_PIECES * s + p
    return _mm(name, "tn", (N_CHIPS, 1),
               hf, pl.BlockSpec((None, S, D_MODEL), lambda s, k: (tile(s) // per_group, 0, 0)),
               dqkv, pl.BlockSpec((None, S, QKV_TILE), lambda s, k: (tile(s) // per_group, 0, tile(s) % per_group)),
               jax.ShapeDtypeStruct((N_CHIPS, D_MODEL, QKV_TILE), BF16),
               pl.BlockSpec((None, D_MODEL, QKV_TILE), lambda s, k: (s, 0, 0)), None)


def _qkv_dh(name, g, dqkv, w_pieces, dhf, tm=1024):
    S = dqkv.shape[1]
    per_group = 3 * D_MODEL // QKV_TILE

    def body(*refs):
        a_ref, w_hbm = refs[0], refs[1:1 + QKV_PIECES]
        o_ref, w_ref, sems = refs[-3:]

        @pl.when(pl.program_id(0) == 0)
        def _():
            copies = []
            for j in range(per_group):
                t = per_group * g + j
                copies.append(pltpu.make_async_copy(w_hbm[t % QKV_PIECES].at[t // QKV_PIECES],
                                                    w_ref.at[:, pl.ds(j * QKV_TILE, QKV_TILE)], sems.at[j]))
            for cp in copies:
                cp.start()
            for cp in copies:
                cp.wait()

        o_ref[...] = _dot(a_ref[...], w_ref[...], "nt").astype(o_ref.dtype)

    any_spec = pl.BlockSpec(memory_space=pl.ANY)
    in_specs = [pl.BlockSpec((None, tm, 3 * D_MODEL), lambda i: (g, i, 0))] + [any_spec] * QKV_PIECES
    operands = [dqkv] + list(w_pieces)
    aliases = {}
    if dhf is not None:
        in_specs.append(any_spec)
        operands.append(dhf)
        aliases = {1 + QKV_PIECES: 0}
    return _pcall(body, name=name, grid=(S // tm,), in_specs=in_specs,
                  out_specs=pl.BlockSpec((None, tm, D_MODEL), lambda i: (g, i, 0)),
                  out_shape=jax.ShapeDtypeStruct((N_GROUPS, S, D_MODEL), BF16),
                  scratch_shapes=[pltpu.VMEM((D_MODEL, 3 * D_MODEL), BF16), pltpu.SemaphoreType.DMA((per_group,))],
                  input_output_aliases=aliases, compiler_params=_params(("arbitrary",)))(*operands)


def _alibi_coef(g, h):
    vals = [-(2.0 ** (-8.0 * (gg * N_HEADS + h + 1) / (N_GROUPS * N_HEADS))) * ATTN_DILATIONS[gg]
            for gg in range(N_GROUPS)]
    return jnp.where(g == 0, vals[0], jnp.where(g == 1, vals[1], vals[2])).astype(F32)


def _blocks_per_segment(g, S):
    return jnp.right_shift(S // Q_BLOCK, 2 * g)


def _band_bias(pen):
    row = lax.broadcasted_iota(jnp.int32, (Q_BLOCK, 2 * Q_BLOCK), 0)
    col = lax.broadcasted_iota(jnp.int32, (Q_BLOCK, 2 * Q_BLOCK), 1)
    dist = Q_BLOCK + row - col
    valid = jnp.logical_and(dist >= 0, dist <= Q_BLOCK)
    base = jnp.where(valid, jnp.where(col < Q_BLOCK, pen, 0.0), NEG).astype(F32)
    return base, dist.astype(F32)


def _skewed(n_units, stages):
    state = {}
    for step in range(n_units + len(stages) - 1):
        for s, stage in enumerate(stages):
            u = step - s
            if 0 <= u < n_units:
                state[u] = stage(u, state.get(u))
                if s == len(stages) - 1:
                    del state[u]


def _attn_fwd(name, qkv, tq=1024):
    S = qkv.shape[1]
    nqb = tq // Q_BLOCK
    scale = HEAD_DIM ** -0.5

    def body(q_ref, k_ref, kp_ref, v_ref, vp_ref, o_ref, lse_ref, kf_ref, vf_ref):
        g = pl.program_id(0)
        i = pl.program_id(1)
        nb = _blocks_per_segment(g, S)
        kf_ref[:Q_BLOCK] = kp_ref[...]
        kf_ref[Q_BLOCK:] = k_ref[...]
        vf_ref[:Q_BLOCK] = vp_ref[...]
        vf_ref[Q_BLOCK:] = v_ref[...]
        coefs = [_alibi_coef(g, h) for h in range(N_HEADS)]
        units = [(b, h) for b in range(nqb) for h in range(N_HEADS)]
        bias = {}

        def scores(u, _):
            b, h = units[u]
            if b not in bias:
                pen = jnp.where((i * nqb + b) % nb != 0, 0.0, NEG).astype(F32)
                bias.clear()
                bias[b] = _band_bias(pen)
            base, dist = bias[b]
            cols = slice(h * HEAD_DIM, (h + 1) * HEAD_DIM)
            q = q_ref[b * Q_BLOCK:(b + 1) * Q_BLOCK, cols]
            kk = kf_ref[b * Q_BLOCK:(b + 2) * Q_BLOCK, cols]
            return _dot(q, kk, "nt") * scale + (coefs[h] * dist + base)

        def softmax(u, s):
            m = jnp.max(s, axis=-1, keepdims=True)
            p = jnp.exp(s - m)
            den = jnp.sum(p, axis=-1, keepdims=True)
            return (p * (1.0 / den)).astype(BF16), m + jnp.log(den)

        def output(u, st):
            b, h = units[u]
            pn, lse = st
            cols = slice(h * HEAD_DIM, (h + 1) * HEAD_DIM)
            rows = slice(b * Q_BLOCK, (b + 1) * Q_BLOCK)
            o_ref[rows, cols] = _dot(pn, vf_ref[b * Q_BLOCK:(b + 2) * Q_BLOCK, cols], "nn").astype(o_ref.dtype)
            lse_ref[rows, h:h + 1] = lse

        _skewed(len(units), [scores, softmax, output])

    main = lambda c: pl.BlockSpec((None, tq, D_MODEL), lambda g, i: (g, i, c))
    prev = lambda c: pl.BlockSpec((None, Q_BLOCK, D_MODEL), lambda g, i: (g, jnp.maximum(i * nqb - 1, 0), c))
    return _pcall(
        body, name=name, grid=(N_GROUPS, S // tq),
        in_specs=[main(0), main(1), prev(1), main(2), prev(2)],
        out_specs=[pl.BlockSpec((None, tq, D_MODEL), lambda g, i: (g, i, 0)),
                   pl.BlockSpec((None, tq, N_HEADS), lambda g, i: (g, i, 0))],
        out_shape=[jax.ShapeDtypeStruct((N_GROUPS, S, D_MODEL), BF16),
                   jax.ShapeDtypeStruct((N_GROUPS, S, N_HEADS), F32)],
        scratch_shapes=[pltpu.VMEM((tq + Q_BLOCK, D_MODEL), BF16), pltpu.VMEM((tq + Q_BLOCK, D_MODEL), BF16)],
        compiler_params=_params(("parallel", "parallel")))(qkv, qkv, qkv, qkv, qkv)


def _attn_bwd(name, qkv, do, lse, dl, tq=1024):
    S = qkv.shape[1]
    nqb = tq // Q_BLOCK
    n_blocks = S // Q_BLOCK
    scale = HEAD_DIM ** -0.5
    KEYS = 2 * Q_BLOCK

    def body(q_ref, k_ref, v_ref, kp_ref, vp_ref, qn_ref, do_ref, don_ref, lse_ref, lsen_ref, dl_ref, dln_ref,
             out_ref, kf_ref, vf_ref, dk_ref, dv_ref):
        g = pl.program_id(0)
        i = pl.program_id(1)
        nb = _blocks_per_segment(g, S)
        kf_ref[:Q_BLOCK] = kp_ref[...]
        kf_ref[Q_BLOCK:] = k_ref[...]
        vf_ref[:Q_BLOCK] = vp_ref[...]
        vf_ref[Q_BLOCK:] = v_ref[...]
        coefs = [_alibi_coef(g, h) for h in range(N_HEADS)]
        units = [(h, b) for h in range(N_HEADS) for b in range(nqb + 1)]
        bias = {}

        def band(b):
            if b not in bias:
                blk = i * nqb + b
                ok = blk % nb != 0
                if b == nqb:
                    ok = jnp.logical_and(ok, blk < n_blocks)
                bias[b] = _band_bias(jnp.where(ok, 0.0, NEG).astype(F32))
            return bias[b]

        def operands(h, b):
            cols = slice(h * HEAD_DIM, (h + 1) * HEAD_DIM)
            if b == nqb:
                keys = slice(tq, tq + Q_BLOCK)
                return (qn_ref[:, cols], don_ref[:, cols], lsen_ref[:, h:h + 1], dln_ref[:, h:h + 1],
                        kf_ref[keys, cols], vf_ref[keys, cols])
            rows = slice(b * Q_BLOCK, (b + 1) * Q_BLOCK)
            keys = slice(b * Q_BLOCK, b * Q_BLOCK + KEYS)
            return (q_ref[rows, cols], do_ref[rows, cols], lse_ref[rows, h:h + 1], dl_ref[rows, h:h + 1],
                    kf_ref[keys, cols], vf_ref[keys, cols])

        def probs(u, _):
            h, b = units[u]
            q, do_b, lse_b, dl_b, kk, vv = operands(h, b)
            base, dist = band(b)
            if b == nqb:
                base, dist = base[:, :Q_BLOCK], dist[:, :Q_BLOCK]
            p = jnp.exp(_dot(q, kk, "nt") * scale + (coefs[h] * dist + base) - lse_b)
            return p, _dot(do_b, vv, "nt")

        def dscores(u, st):
            h, b = units[u]
            p, dp = st
            dl_b = operands(h, b)[3]
            return ((p * (dp - dl_b)) * scale).astype(BF16), p.astype(BF16)

        def grads(u, st):
            h, b = units[u]
            ds, pb = st
            q, do_b, _, _, kk, _ = operands(h, b)
            cols = slice(h * HEAD_DIM, (h + 1) * HEAD_DIM)
            if b < nqb:
                out_ref[b * Q_BLOCK:(b + 1) * Q_BLOCK, cols] = _dot(ds, kk, "nn").astype(out_ref.dtype)
            if b == 0:
                dk_ref[:Q_BLOCK] = _dot(ds[:, Q_BLOCK:], q, "tn")
                dv_ref[:Q_BLOCK] = _dot(pb[:, Q_BLOCK:], do_b, "tn")
                return None
            dk = _dot(ds, q, "tn")
            dv = _dot(pb, do_b, "tn")
            before = slice((b - 1) * Q_BLOCK, b * Q_BLOCK)
            dk_ref[before] += dk[:Q_BLOCK]
            dv_ref[before] += dv[:Q_BLOCK]
            if b < nqb:
                own = slice(b * Q_BLOCK, (b + 1) * Q_BLOCK)
                dk_ref[own] = dk[Q_BLOCK:]
                dv_ref[own] = dv[Q_BLOCK:]
            else:
                out_ref[:, D_MODEL + h * HEAD_DIM:D_MODEL + (h + 1) * HEAD_DIM] = dk_ref[...].astype(out_ref.dtype)
                out_ref[:, 2 * D_MODEL + h * HEAD_DIM:2 * D_MODEL + (h + 1) * HEAD_DIM] = (
                    dv_ref[...].astype(out_ref.dtype))
            return None

        _skewed(len(units), [probs, dscores, grads])

    nxt_blk = lambda i: jnp.minimum((i + 1) * nqb, n_blocks - 1)
    main = lambda c: pl.BlockSpec((None, tq, D_MODEL), lambda g, i: (g, i, c))
    prev = lambda c: pl.BlockSpec((None, Q_BLOCK, D_MODEL), lambda g, i: (g, jnp.maximum(i * nqb - 1, 0), c))
    row = pl.BlockSpec((None, tq, D_MODEL), lambda g, i: (g, i, 0))
    row_n = pl.BlockSpec((None, Q_BLOCK, D_MODEL), lambda g, i: (g, nxt_blk(i), 0))
    col = pl.BlockSpec((None, tq, N_HEADS), lambda g, i: (g, i, 0))
    col_n = pl.BlockSpec((None, Q_BLOCK, N_HEADS), lambda g, i: (g, nxt_blk(i), 0))
    return _pcall(
        body, name=name, grid=(N_GROUPS, S // tq),
        in_specs=[main(0), main(1), main(2), prev(1), prev(2), row_n, row, row_n, col, col_n, col, col_n],
        out_specs=pl.BlockSpec((None, tq, 3 * D_MODEL), lambda g, i: (g, i, 0)),
        out_shape=jax.ShapeDtypeStruct((N_GROUPS, S, 3 * D_MODEL), BF16),
        scratch_shapes=[pltpu.VMEM((tq + Q_BLOCK, D_MODEL), BF16), pltpu.VMEM((tq + Q_BLOCK, D_MODEL), BF16),
                        pltpu.VMEM((tq, HEAD_DIM), F32), pltpu.VMEM((tq, HEAD_DIM), F32)],
        compiler_params=_params(("parallel", "parallel")))(qkv, qkv, qkv, qkv, qkv, qkv, do, do, lse, lse, dl, dl)


def _group_weights(lse_ref):
    l0, l1, l2 = lse_ref[0], lse_ref[1], lse_ref[2]
    m = jnp.maximum(jnp.maximum(l0, l1), l2)
    e = [jnp.exp(l0 - m), jnp.exp(l1 - m), jnp.exp(l2 - m)]
    den = e[0] + e[1] + e[2]
    return [ei / den for ei in e]


MERGE_TOKENS = 512


def _folded_views(a, tb):
    _, S, C = a.shape
    views, specs = [], []
    for g, dil in enumerate(ATTN_DILATIONS):
        views.append(a.reshape(N_GROUPS * dil, S // dil, C))
        specs.append(pl.BlockSpec((dil, tb // dil, C), lambda i, g=g: (g, i, 0)))
    return views, specs


def _unfold_into(dst_ref, folded_refs):
    for g, (dil, ref) in enumerate(zip(ATTN_DILATIONS, folded_refs)):
        n = ref.shape[1]
        for r in range(dil):
            for c, cols in enumerate(_lane_chunks(ref.shape[2])):
                dst_ref[g, c, _strided_rows(r, n, dil), :] = ref[r, :, cols].astype(F32)


def _merge_fwd(name, o, lse, tb=MERGE_TOKENS):
    S = o.shape[1]

    def body(o0_ref, o1_ref, o2_ref, lse_ref, out_ref, o_ref):
        _unfold_into(o_ref, (o0_ref, o1_ref, o2_ref))
        w = _group_weights(lse_ref)
        for h in range(N_HEADS):
            cols = slice(h * HEAD_DIM, (h + 1) * HEAD_DIM)
            acc = w[0][:, h:h + 1] * o_ref[0, h]
            for gg in range(1, N_GROUPS):
                acc = acc + w[gg][:, h:h + 1] * o_ref[gg, h]
            out_ref[:, cols] = acc.astype(out_ref.dtype)

    views, specs = _folded_views(o, tb)
    return _pcall(body, name=name, grid=(S // tb,),
                  in_specs=specs + [pl.BlockSpec((N_GROUPS, tb, N_HEADS), lambda i: (0, i, 0))],
                  out_specs=pl.BlockSpec((tb, D_MODEL), lambda i: (i, 0)),
                  out_shape=jax.ShapeDtypeStruct((S, D_MODEL), BF16),
                  scratch_shapes=[pltpu.VMEM((N_GROUPS, N_HEADS, tb, HEAD_DIM), F32)],
                  compiler_params=_params(("parallel",)))(*views, lse)


def _merge_bwd(name, dx, w_out, o, lse, tb=MERGE_TOKENS):
    S = o.shape[1]
    n_chunks = sum(ATTN_DILATIONS)

    def body(dx_ref, w_ref, o0_ref, o1_ref, o2_ref, lse_ref, do_hbm, dl_ref, o_ref, dt_ref, df_ref, d_ref, sems):
        i = pl.program_id(0)
        d_ref[...] = _dot(dx_ref[...].astype(BF16), w_ref[...], "nt")
        _unfold_into(o_ref, (o0_ref, o1_ref, o2_ref))
        w = _group_weights(lse_ref)
        for h in range(N_HEADS):
            cols = slice(h * HEAD_DIM, (h + 1) * HEAD_DIM)
            dv = d_ref[:, cols]
            merged = w[0][:, h:h + 1] * o_ref[0, h]
            for gg in range(1, N_GROUPS):
                merged = merged + w[gg][:, h:h + 1] * o_ref[gg, h]
            dsum = jnp.sum(dv * merged, axis=-1, keepdims=True)
            for gg in range(N_GROUPS):
                wg = w[gg][:, h:h + 1]
                dt_ref[gg, h] = wg * dv
                dl_ref[gg, :, h:h + 1] = wg * dsum
        copies = []
        for gg, dil in enumerate(ATTN_DILATIONS):
            n = tb // dil
            for r in range(dil):
                for h, cols in enumerate(_lane_chunks(D_MODEL)):
                    df_ref[gg, r * n:(r + 1) * n, cols] = (
                        dt_ref[gg, h, _strided_rows(r, n, dil), :].astype(df_ref.dtype))
                cp = pltpu.make_async_copy(df_ref.at[gg, pl.ds(r * n, n)],
                                           do_hbm.at[gg, pl.ds(r * (S // dil) + i * n, n)], sems.at[len(copies)])
                cp.start()
                copies.append(cp)
        for cp in copies:
            cp.wait()

    views, specs = _folded_views(o, tb)
    small = pl.BlockSpec((N_GROUPS, tb, N_HEADS), lambda i: (0, i, 0))
    return _pcall(body, name=name, grid=(S // tb,),
                  in_specs=[pl.BlockSpec((tb, D_MODEL), lambda i: (i, 0)),
                            pl.BlockSpec((D_MODEL, D_MODEL), lambda i: (0, 0))] + specs + [small],
                  out_specs=[pl.BlockSpec(memory_space=pl.ANY), small],
                  out_shape=[jax.ShapeDtypeStruct((N_GROUPS, S, D_MODEL), BF16),
                             jax.ShapeDtypeStruct((N_GROUPS, S, N_HEADS), F32)],
                  scratch_shapes=[pltpu.VMEM((N_GROUPS, N_HEADS, tb, HEAD_DIM), F32),
                                  pltpu.VMEM((N_GROUPS, N_HEADS, tb, HEAD_DIM), F32),
                                  pltpu.VMEM((N_GROUPS, tb, D_MODEL), BF16), pltpu.VMEM((tb, D_MODEL), F32),
                                  pltpu.SemaphoreType.DMA((n_chunks,))],
                  compiler_params=_params(("arbitrary",)))(dx, w_out, *views, lse)


def _shift_rows(a, k):
    return pltpu.roll(a, k % a.shape[0], axis=0)


def _pool_level(g, levels):
    return jnp.where(g == 0, levels[0], jnp.where(g == 1, levels[1], jnp.where(g == 2, levels[2], levels[3])))


def _pool_fwd(name, h, w_in, w_group, scale, x_in, gain_next, tr=1024):
    S, D = h.shape
    H = POOL_HALO
    n_groups = D // POOL_DIM

    def body(h_ref, hh_ref, wi_ref, w_ref, sc_ref, x_ref, gn_ref, y_ref, z_ref, xo_ref, hn_ref, xs_ref):
        i = pl.program_id(0)
        g = pl.program_id(1)
        uv = _dot(h_ref[...], wi_ref[...], "nn")
        halo = jnp.where(i > 0, _dot(hh_ref[...], wi_ref[...], "nn"), 0.0)
        s = jnp.concatenate([halo, uv], axis=0)
        levels = []
        for k in (1, 2, 4, 8):
            s = s + _shift_rows(s, k)
            levels.append(s[H:])
        t = i * tr + lax.broadcasted_iota(jnp.int32, (tr, 1), 0)
        cnt = jnp.minimum(t + 1, jnp.left_shift(2, g)).astype(F32)
        y = _pool_level(g, levels) / cnt - uv
        yb = y.astype(BF16)
        z = _dot(yb, w_ref[...], "nn")
        y_ref[...] = yb
        z_ref[...] = z.astype(z_ref.dtype)
        x_out = x_ref[...] + z * sc_ref[...]
        xo_ref[...] = x_out
        xs_ref[g] = x_out

        @pl.when(g == n_groups - 1)
        def _():
            xf = jnp.concatenate([xs_ref[k] for k in range(n_groups)], axis=1)
            r = lax.rsqrt(jnp.mean(xf * xf, axis=-1, keepdims=True) + RMS_EPS)
            hn_ref[...] = ((xf * r) * gn_ref[...]).astype(hn_ref.dtype)

    blk = pl.BlockSpec((tr, POOL_DIM), lambda i, g: (i, g))
    row = pl.BlockSpec((tr, D), lambda i, g: (i, 0))
    return _pcall(
        body, name=name, grid=(S // tr, n_groups),
        in_specs=[row, pl.BlockSpec((H, D), lambda i, g: (jnp.maximum(i * (tr // H) - 1, 0), 0)),
                  pl.BlockSpec((D, POOL_DIM), lambda i, g: (0, g)),
                  pl.BlockSpec((None, POOL_DIM, POOL_DIM), lambda i, g: (g, 0, 0)),
                  pl.BlockSpec((1, POOL_DIM), lambda i, g: (0, g)), blk, pl.BlockSpec((1, D), lambda i, g: (0, 0))],
        out_specs=[blk, blk, blk, row],
        out_shape=[jax.ShapeDtypeStruct((S, D), BF16), jax.ShapeDtypeStruct((S, D), BF16),
                   jax.ShapeDtypeStruct((S, D), F32), jax.ShapeDtypeStruct((S, D), BF16)],
        scratch_shapes=[pltpu.VMEM((n_groups, tr, POOL_DIM), F32)],
        compiler_params=_params(("parallel", "arbitrary")))(h, h, w_in, w_group, scale, x_in, gain_next)


def _pool_bwd(name, d_out, z, y, scale, w_group, tr=1024):
    S, D = d_out.shape
    H = POOL_HALO

    def body(d_ref, dn_ref, z_ref, y_ref, sc_ref, w_ref, du_ref, dw_ref, dsc_ref):
        g = pl.program_id(0)
        i = pl.program_id(1)
        dv = d_ref[...]
        dz = jnp.concatenate([dv, dn_ref[...]], axis=0) * sc_ref[...]
        dzb = dz.astype(BF16)
        dy = _dot(dzb, w_ref[...], "nt")
        t = i * tr + lax.broadcasted_iota(jnp.int32, (tr + H, 1), 0)
        cnt = jnp.minimum(t + 1, jnp.left_shift(2, g)).astype(F32)
        s = jnp.where(t < S, dy / cnt, 0.0)
        levels = []
        for k in (1, 2, 4, 8):
            s = s + _shift_rows(s, -k)
            levels.append(s[:tr])
        du_ref[...] = (_pool_level(g, levels) - dy[:tr]).astype(du_ref.dtype)
        dw = _dot(y_ref[...], dzb[:tr], "tn")
        dsc = jnp.sum(dv * z_ref[...].astype(F32), axis=0, keepdims=True)

        @pl.when(i == 0)
        def _():
            dw_ref[...] = dw
            dsc_ref[...] = dsc

        @pl.when(i > 0)
        def _():
            dw_ref[...] += dw
            dsc_ref[...] += dsc

    blk = pl.BlockSpec((tr, POOL_DIM), lambda g, i: (i, g))
    vec = pl.BlockSpec((1, POOL_DIM), lambda g, i: (0, g))
    mat = pl.BlockSpec((None, POOL_DIM, POOL_DIM), lambda g, i: (g, 0, 0))
    nxt = pl.BlockSpec((H, POOL_DIM), lambda g, i: (jnp.minimum((i + 1) * (tr // H), S // H - 1), g))
    return _pcall(
        body, name=name, grid=(D // POOL_DIM, S // tr), in_specs=[blk, nxt, blk, blk, vec, mat],
        out_specs=[blk, mat, vec],
        out_shape=[jax.ShapeDtypeStruct((S, D), BF16), jax.ShapeDtypeStruct((D // POOL_DIM, POOL_DIM, POOL_DIM), F32),
                   jax.ShapeDtypeStruct((1, D), F32)],
        compiler_params=_params(("parallel", "arbitrary")))(d_out, d_out, z, y, scale, w_group)


def _row_tile(rows, cols, target_bytes=1 << 20):
    best = rows
    for tr in range(8, rows + 1, 8):
        if rows % tr == 0 and tr * cols * 4 <= target_bytes:
            best = tr
    return best if best * cols * 4 <= 4 * target_bytes else rows


def _adamw_step(w, g, m, v):
    m2 = ADAM_B1 * m + (1.0 - ADAM_B1) * g
    v2 = ADAM_B2 * v + (1.0 - ADAM_B2) * (g * g)
    m_hat = m2 / (1.0 - ADAM_B1 ** ADAM_STEP)
    v_hat = v2 / (1.0 - ADAM_B2 ** ADAM_STEP)
    return -ADAM_LR * (m_hat / (jnp.sqrt(v_hat) + ADAM_EPS) + ADAM_WD * w), m2, v2


def _adamw_refs(w_ref, g_ref, m_ref, v_ref, go_ref, d_ref, mo_ref, vo_ref):
    gv = g_ref[...]
    d_ref[...], mo_ref[...], vo_ref[...] = _adamw_step(w_ref[...], gv, m_ref[...], v_ref[...])
    go_ref[...] = gv


def _adamw(name, items, steps=16):
    n = len(items)

    def body(*refs):
        for t in range(n):
            _adamw_refs(*refs[4 * t:4 * t + 4], *refs[4 * n + 4 * t:4 * n + 4 * t + 4])

    specs, shapes = [], []
    for w, _, _, _ in items:
        R, C = w.shape
        assert R % (8 * steps) == 0, (name, w.shape)
        specs += [pl.BlockSpec((R // steps, C), lambda i: (i, 0))] * 4
        shapes += [jax.ShapeDtypeStruct((R, C), F32)] * 4
    res = _pcall(body, name=name, grid=(steps,), in_specs=specs, out_specs=specs, out_shape=shapes,
                 compiler_params=_params(("parallel",)))(*[a for item in items for a in item])
    return [res[4 * t:4 * t + 4] for t in range(n)]


def _adamw_small(name, items):
    n = len(items)

    def body(*refs):
        for t in range(n):
            _adamw_refs(*refs[4 * t:4 * t + 4], *refs[4 * n + 4 * t:4 * n + 4 * t + 4])

    specs, shapes = [], []
    for w, _, _, _ in items:
        specs += [pl.BlockSpec(w.shape, lambda i: (0, 0))] * 4
        shapes += [jax.ShapeDtypeStruct(w.shape, F32)] * 4
    res = _pcall(body, name=name, grid=(1,), in_specs=specs, out_specs=specs, out_shape=shapes,
                 compiler_params=_params(("arbitrary",)))(*[a for item in items for a in item])
    return [res[4 * t:4 * t + 4] for t in range(n)]


def _sum_leading(name, a, out_dtype):
    n, R, C = a.shape
    tr = _row_tile(R, C) if R % 16 == 0 else R
    if tr % 16:
        tr = R

    def body(a_ref, o_ref):
        acc = a_ref[0].astype(F32)
        for j in range(1, n):
            acc = acc + a_ref[j].astype(F32)
        o_ref[...] = acc.astype(o_ref.dtype)

    return _pcall(body, name=name, grid=(R // tr,), in_specs=[pl.BlockSpec((n, tr, C), lambda i: (0, i, 0))],
                  out_specs=pl.BlockSpec((tr, C), lambda i: (i, 0)), out_shape=jax.ShapeDtypeStruct((R, C), out_dtype),
                  compiler_params=_params(("parallel",)))(a)


def _cast_many(name, items, chip, steps=4):
    tiles = []
    for w, _, dtype in items:
        R = w.shape[1]
        nt = steps if R % (steps * 16) == 0 else 1
        tiles.append((nt, R // nt))

    def body(c_ref, *refs):
        i = pl.program_id(0)
        for t, (nt, _) in enumerate(tiles):
            w_ref, o_ref = refs[t], refs[len(items) + t]

            def cast(w_ref=w_ref, o_ref=o_ref):
                o_ref[...] = w_ref[...].astype(o_ref.dtype)

            if nt == steps:
                cast()
            else:
                pl.when(i < nt)(cast)

    in_specs, out_specs, out_shape = [], [], []
    for (w, layer, dtype), (nt, tr) in zip(items, tiles):
        C = w.shape[2]
        in_specs.append(pl.BlockSpec((None, tr, C), lambda i, c_ref, l=layer, nt=nt: (l, jnp.minimum(i, nt - 1), 0)))
        out_specs.append(pl.BlockSpec((None, tr, C), lambda i, c_ref, nt=nt: (c_ref[0], jnp.minimum(i, nt - 1), 0)))
        out_shape.append(jax.ShapeDtypeStruct((N_CHIPS, w.shape[1], C), dtype))
    return _pcall(body, prefetch=1, name=name, grid=(steps,), in_specs=in_specs, out_specs=out_specs,
                  out_shape=out_shape, compiler_params=_params(("arbitrary",)))(chip, *[w for w, _, _ in items])


def _cast_qkv(name, w, chip, tr=256):
    _, R, C = w.shape

    def body(c_ref, w_ref, own_ref, *piece_refs):
        v = w_ref[...].astype(BF16)
        own_ref[...] = v
        for p, ref in enumerate(piece_refs):
            ref[...] = v[:, p * QKV_TILE:(p + 1) * QKV_TILE]

    piece = pl.BlockSpec((None, tr, QKV_TILE), lambda i, c_ref: (c_ref[0], i, 0))
    return _pcall(body, prefetch=1, name=name, grid=(R // tr,),
                  in_specs=[pl.BlockSpec((None, tr, C), lambda i, c_ref: (0, i, 0))],
                  out_specs=[pl.BlockSpec((tr, C), lambda i, c_ref: (i, 0))] + [piece] * QKV_PIECES,
                  out_shape=[jax.ShapeDtypeStruct((R, C), BF16)]
                  + [jax.ShapeDtypeStruct((N_CHIPS, R, QKV_TILE), BF16)] * QKV_PIECES,
                  compiler_params=_params(("parallel",)))(chip, w)


def _owner_sum(name, mine, landed, shape, chip, core, out, layer, col):
    _, half, C = mine.shape
    k, _ = col
    tr = _row_tile(half, C)
    if tr % 16:
        tr = half
    nrt = half // tr

    def body(ch_ref, co_ref, mine_ref, landed_ref, *rest):
        g = mine_ref[...].astype(F32)
        for j in range(3):
            g = g + landed_ref[j].astype(F32)
        rest[-1][...] = g

    piece = pl.BlockSpec((None, tr, C), lambda i, ch, co: (layer, co[0] * nrt + i, k))
    in_specs = [pl.BlockSpec((None, tr, C), lambda i, ch, co: (ch[0], i, 0)),
                pl.BlockSpec((3, tr, C), lambda i, ch, co: (0, i, 0))]
    operands = [chip, core, mine, landed]
    aliases = {}
    if out is not None:
        in_specs.append(ANY)
        operands.append(out)
        aliases = {4: 0}
    return _pcall(body, prefetch=2, name=name, grid=(nrt,), in_specs=in_specs, out_specs=piece,
                  out_shape=jax.ShapeDtypeStruct(shape, F32), input_output_aliases=aliases,
                  compiler_params=_params(("parallel",)))(*operands)


def _add_my_half(name, ps, qs, core):
    n = len(ps)

    def body(c_ref, *refs):
        for t in range(n):
            p_ref, q_ref, o_ref = refs[t], refs[n + t], refs[2 * n + t]
            o_ref[...] = (p_ref[...].astype(F32) + q_ref[...].astype(F32)).astype(o_ref.dtype)

    halves = [(p.shape[1] // 2, p.shape[2]) for p in ps]
    mine = [pl.BlockSpec((None, h, c), lambda s, c_ref: (s, c_ref[0], 0)) for h, c in halves]
    whole = [pl.BlockSpec((None, h, c), lambda s, c_ref: (s, 0, 0)) for h, c in halves]
    return _pcall(body, prefetch=1, name=name, grid=(N_CHIPS,), in_specs=mine + whole, out_specs=whole,
                  out_shape=[jax.ShapeDtypeStruct((N_CHIPS, h, c), BF16) for h, c in halves],
                  compiler_params=_params(("parallel",)))(core, *ps, *qs)


ANY = pl.BlockSpec(memory_space=pl.ANY)


def _place():
    x, y, c = lax.axis_index("x"), lax.axis_index("y"), lax.axis_index("c")
    other_chips = [(1 - x, y), (x, 1 - y), (1 - x, 1 - y)]
    return x, y, c, other_chips


def _remote(src, dst, send, recv, k, to):
    return pltpu.make_async_remote_copy(src_ref=src, dst_ref=dst, send_sem=send.at[k], recv_sem=recv.at[k],
                                        device_id=to, device_id_type=MESH)


def _in_place(bufs):
    return dict(operands=bufs, out_shapes=[jax.ShapeDtypeStruct(b.shape, b.dtype) for b in bufs],
                aliases={t: t for t in range(len(bufs))})


def _gather_rider(bufs, jobs):
    def copies(ins, outs, send, recv, base=0):
        x, y, c, chips = _place()
        out = []
        for t, over_ici, rows in jobs:
            ref = outs[t]
            if rows is not None:
                a, b, n = rows
                half = ref.shape[1] // 2
                rows = pl.ds(c * half + a * half // n, (b - a) * half // n)
            for px, py in chips:
                slot = 2 * x + y if over_ici else 2 * px + py
                piece = ref.at[slot] if rows is None else ref.at[slot, rows]
                out.append(_remote(piece, piece, send, recv, base + len(out), (px, py, c) if over_ici else (x, y, 1 - c)))
        return out

    return _Rider(n_copies=3 * len(jobs), copies=copies, **_in_place(bufs))


def _gather_and_pass_rider(buf):
    half = buf.shape[1] // 2

    def pieces(outs):
        x, y, c, chips = _place()
        rows = lambda core: pl.ds(core * half, half)
        mine = outs[0].at[2 * x + y, rows(c)]
        landed = [outs[0].at[2 * px + py, rows(c)] for px, py in chips]
        theirs = [outs[0].at[2 * px + py, rows(1 - c)] for px, py in chips]
        return (x, y, c, chips), mine, landed, theirs

    def start(ins, outs, send, recv, base=0):
        (x, y, c, chips), mine, _, _ = pieces(outs)
        for j, (px, py) in enumerate(chips):
            _remote(mine, mine, send, recv, base + j, (px, py, c)).start()

    def finish(ins, outs, send, recv, base=0):
        (x, y, c, chips), mine, landed, theirs = pieces(outs)
        sibling = (x, y, 1 - c)
        passed = []
        for j, (px, py) in enumerate(chips):
            _remote(landed[j], landed[j], send, recv, base + j, (px, py, c)).wait_recv()
            passed.append(_remote(landed[j], landed[j], send, recv, base + 3 + j, sibling))
            passed[-1].start()
        for j in range(3):
            _remote(theirs[j], theirs[j], send, recv, base + 3 + j, sibling).wait_recv()
        for j, (px, py) in enumerate(chips):
            _remote(mine, mine, send, recv, base + j, (px, py, c)).wait_send()
            passed[j].wait_send()

    return _Rider(n_copies=6, start=start, finish=finish, **_in_place([buf]))


def _swap_halves_rider(parts):
    n = len(parts)

    def copies(ins, outs, send, recv, base=0):
        x, y, c, _ = _place()
        out = []
        for t in range(n):
            half = ins[t].shape[1] // 2
            out.append(_remote(ins[t].at[:, pl.ds((1 - c) * half, half)], outs[t], send, recv, base + t,
                               (x, y, 1 - c)))
        return out

    shapes = [jax.ShapeDtypeStruct((p.shape[0], p.shape[1] // 2, p.shape[2]), p.dtype) for p in parts]
    return _Rider(parts, shapes, {}, n, copies)


def _scatter_rider(sums, landed, relations):
    n = len(sums)
    kept = [t for t in range(n) if landed[t] is not None]

    def copies(ins, outs, send, recv, base=0):
        x, y, c, chips = _place()
        out = []
        for t in range(n):
            for j in relations[t]:
                px, py = chips[j]
                out.append(_remote(ins[t].at[2 * px + py], outs[t].at[j], send, recv, base + len(out), (px, py, c)))
        return out

    shapes = [jax.ShapeDtypeStruct((3,) + s.shape[1:], s.dtype) for s in sums]
    return _Rider(list(sums) + [landed[t] for t in kept], shapes, {n + k: t for k, t in enumerate(kept)},
                  sum(len(r) for r in relations), copies)


def _share_rider(blocks, pieces):
    def copies(ins, outs, send, recv, base=0):
        x, y, c, _ = _place()
        out = []
        for k, (t, l, col) in enumerate(pieces):
            ref = outs[t].at[l]
            r2 = ref.shape[0] // 2
            width = ref.shape[1] // col[1]
            piece = ref.at[pl.ds(c * r2, r2), pl.ds(col[0] * width, width)]
            out.append(_remote(piece, piece, send, recv, base + k, (x, y, 1 - c)))
        return out

    return _Rider(n_copies=len(pieces), copies=copies, **_in_place(blocks))


def _gather_small(name, v):
    def body(v_ref, out_ref, send_sem, recv_sem, local_sem):
        x, y, c, _ = _place()
        me = 4 * x + 2 * y + c
        flips = [(fx, fy, fc) for fx in (0, 1) for fy in (0, 1) for fc in (0, 1)][1:]
        local = pltpu.make_async_copy(v_ref, out_ref.at[me], local_sem)
        local.start()
        copies = []
        for j, (fx, fy, fc) in enumerate(flips):
            peer = (x ^ fx, y ^ fy, c ^ fc)
            copies.append(pltpu.make_async_remote_copy(
                src_ref=v_ref, dst_ref=out_ref.at[me], send_sem=send_sem.at[j], recv_sem=recv_sem.at[j],
                device_id=peer, device_id_type=MESH))
        for cp in copies:
            cp.start()
        for j, (fx, fy, fc) in enumerate(flips):
            peer = (x ^ fx, y ^ fy, c ^ fc)
            pltpu.make_async_remote_copy(
                src_ref=v_ref, dst_ref=out_ref.at[4 * peer[0] + 2 * peer[1] + peer[2]], send_sem=send_sem.at[j],
                recv_sem=recv_sem.at[j], device_id=peer, device_id_type=MESH).wait_recv()
        for cp in copies:
            cp.wait_send()
        local.wait()

    return _pcall(body, name=name, in_specs=[ANY], out_specs=ANY,
                  out_shape=jax.ShapeDtypeStruct((8,) + v.shape, v.dtype),
                  scratch_shapes=[pltpu.SemaphoreType.DMA((7,)), pltpu.SemaphoreType.DMA((7,)),
                                  pltpu.SemaphoreType.DMA])(v)


def _fold(a, dil):
    if dil == 1:
        return a
    S, C = a.shape
    return a.reshape(S // dil, dil, C).transpose(1, 0, 2).reshape(S, C)


def _unfold(a, dil):
    if dil == 1:
        return a
    S, C = a.shape
    return a.reshape(dil, S // dil, C).transpose(1, 0, 2).reshape(S, C)


def _fold_groups(a):
    return jnp.stack([_fold(a[g] if a.ndim == 3 else a, d) for g, d in enumerate(ATTN_DILATIONS)])


def _unfold_groups(a):
    return jnp.stack([_unfold(a[g], d) for g, d in enumerate(ATTN_DILATIONS)])


class _NoComm:
    def __init__(self, weights):
        self.weights = weights
        self.grads = {}
        self.chip = jnp.zeros((1,), jnp.int32)

    def w(self, name):
        return self.weights[name]

    def run(self, fn, name, *args, **kw):
        return fn(name, *args, **kw)

    def grad(self, name, value):
        self.grads[name] = value


def _local_step(xs, tgt, attn_norm, ffn_norm, final_norm, comm):
    run = comm.run
    hf = run(_norm_fold, "fold", xs, attn_norm)
    qkv = run(_qkv_tiles, "qkv_own", None, hf, comm.w("wq_own"), None, comm.chip)
    for p in range(QKV_PIECES):
        qkv = run(_qkv_tiles, "qkv_piece%d" % p, p, hf, comm.w("wq%d" % p), qkv, comm.chip)
    o_f, lse_f = run(_attn_fwd, "attn_fwd", qkv)
    lse_t = _unfold_groups(lse_f)
    merged = run(_merge_fwd, "merge_fwd", o_f, lse_t)
    x1, h1 = run(_proj_res_norm, "attn_out", merged, comm.w("o"), xs, ffn_norm[0:1])
    gu0, act0 = run(_ffn_up, "ffn0_up", h1, comm.w("gu0"))
    x2, h2 = run(_proj_res_norm, "ffn0_down", act0, comm.w("d0"), x1, comm.w("pool_norm"))
    y, z, x3, h3 = run(_pool_fwd, "pool_fwd", h2, comm.w("p"), comm.w("pg"), comm.w("pool_scale"), x2, ffn_norm[1:2])
    gu1, act1 = run(_ffn_up, "ffn1_up", h3, comm.w("gu1"))
    loss, dx4, dx4_b, d_final = run(_proj_res_loss, "ffn1_down", act1, comm.w("d1"), x3, final_norm, tgt)

    dgu1 = run(_ffn_down_bwd, "ffn1_down_bwd", dx4_b, comm.w("d1"), gu1)
    comm.grad("d1", run(_mm_tn, "ffn1_down_dw", act1, dx4_b, FF_SHARD, 2048))
    comm.grad("gu1", run(_ffn_dw_up, "ffn1_up_dw", h3, dgu1))
    dx3, d_ffn1 = run(_ffn_dh, "ffn1_up_dh", dgu1, comm.w("gu1"), x3, ffn_norm[1:2], dx4)

    du, dw_pg, d_scale = run(_pool_bwd, "pool_bwd", dx3, z, y, comm.w("pool_scale"), comm.w("pg"))
    comm.grad("pg", dw_pg)
    comm.grad("p", run(_mm_tn, "pool_in_dw", h2, du, D_MODEL, h2.shape[0]))
    dx2, dx2_b, d_pool = run(_dh_norm_bwd, "pool_in_dh", du, comm.w("p"), x2, comm.w("pool_norm"), dx3)

    dgu0 = run(_ffn_down_bwd, "ffn0_down_bwd", dx2_b, comm.w("d0"), gu0)
    comm.grad("d0", run(_mm_tn, "ffn0_down_dw", act0, dx2_b, FF_SHARD, 2048))
    comm.grad("gu0", run(_ffn_dw_up, "ffn0_up_dw", h1, dgu0))
    dx1, d_ffn0 = run(_ffn_dh, "ffn0_up_dh", dgu0, comm.w("gu0"), x1, ffn_norm[0:1], dx2)

    comm.grad("o", run(_mm_tn, "attn_out_dw", merged, dx1, D_MODEL, 2048))
    do_f, dl_t = run(_merge_bwd, "merge_bwd", dx1, comm.w("o"), o_f, lse_t)
    dqkv = run(_attn_bwd, "attn_bwd", qkv, do_f, lse_f, _fold_groups(dl_t))
    for p in range(QKV_PIECES):
        comm.grad("qkv%d" % p, run(_qkv_dw, "qkv_dw%d" % p, p, hf, dqkv))
    dhf = None
    for g in range(N_GROUPS):
        dhf = run(_qkv_dh, "qkv_dh%d" % g, g, dqkv, [comm.w("wq%d" % p) for p in range(QKV_PIECES)], dhf)
    grad_x, d_attn = run(_rms_bwd_folded, "norm_attn_bwd", xs, attn_norm, dhf, dx1)

    small = dict(attn=d_attn, ffn0=d_ffn0, ffn1=d_ffn1, final=d_final, pool=d_pool, scale=d_scale)
    return loss, grad_x, small


TENSORS = ("qkv", "o", "p", "pg", "gu0", "gu1", "d0", "d1")


def _block_of(name):
    if name[:-1] in ("gu", "d"):
        return name[:-1], int(name[-1]), (0, 1)
    if name[:-1] == "qkv":
        return "qkv", 0, (int(name[-1]), QKV_PIECES)
    return name, 0, (0, 1)


class _MeshComm:
    def __init__(self, bufs, shapes, chip_arr, core_arr, pg_rows):
        self.bufs = dict(bufs)
        self.shapes = shapes
        self.chip, self.chip_arr, self.core_arr, self.pg_rows = chip_arr, chip_arr, core_arr, pg_rows
        self.parts, self.sums, self.blocks, self.landed, self.arrived = {}, {}, {}, {}, {}
        move = lambda ici=(), d2d=(): lambda: self.gather(ici, d2d)
        whole = lambda name: lambda: self.gather_and_pass(name)
        swap = lambda *names: lambda: self.swap(names)
        scatter = lambda *names: lambda: self.scatter(names)
        share = lambda *names: lambda: self.share(names)
        self.plan = {
            "fold": [whole("wq0")],
            "qkv_own": [whole("wq1")],
            "qkv_piece0": [whole("wq2")],
            "qkv_piece1": [move(ici=["o", "gu0[0:1/4]"])],
            "qkv_piece2": [move(ici=["gu0[1:2/4]"], d2d=["o", "gu0[0:1/4]"])],
            "attn_fwd": [move(ici=["gu0[2:4/4]", "d0[0:1/2]"], d2d=["gu0[1:2/4]"])],
            "merge_fwd": [move(ici=["p", "pg"], d2d=["gu0[2:4/4]", "d0[0:1/2]"])],
            "attn_out": [move(ici=["d0[1:2/2]", "pool_norm", "pool_scale"], d2d=["p", "pg"])],
            "ffn0_up": [move(ici=["gu1[0:3/4]"], d2d=["d0[1:2/2]"])],
            "ffn0_down": [move(ici=["gu1[3:4/4]"], d2d=["gu1[0:3/4]"])],
            "pool_fwd": [move(ici=["d1"], d2d=["gu1[3:4/4]"])],
            "ffn1_up": [move(d2d=["d1"])],
            "ffn1_up_dh": [swap("d1", "gu1")],
            "pool_bwd": [scatter("d1{01}")],
            "pool_in_dh": [scatter("d1{2}")],
            "ffn0_down_bwd": [scatter("gu1{01}")],
            "ffn0_down_dw": [scatter("gu1{2}")],
            "ffn0_up_dw": [share("gu1", "d1")],
            "ffn0_up_dh": [swap("pg", "p", "d0", "gu0")],
            "merge_bwd": [swap("o")],
            "attn_bwd": [scatter("gu0", "d0", "o")],
            "qkv_dw0": [share("d0"), scatter("p", "pg")],
            "qkv_dw1": [share("gu0", "o"), swap("qkv0")],
            "qkv_dw2": [share("p", "pg"), scatter("qkv0"), swap("qkv1")],
            "qkv_dh0": [share("qkv0"), scatter("qkv1"), swap("qkv2")],
            "qkv_dh1": [share("qkv1"), scatter("qkv2")],
            "qkv_dh2": [share("qkv2")],
        }

    def gather_and_pass(self, name):
        return _gather_and_pass_rider(self.bufs[name]), lambda res: self.bufs.update({name: res[0]})

    def gather(self, ici, d2d):
        names, jobs = [], []
        for over_ici, specs in ((True, ici), (False, d2d)):
            for spec in specs:
                name, _, rows = spec.partition("[")
                if name not in names:
                    names.append(name)
                rng = None
                if name in TENSORS:
                    ab, _, n = (rows[:-1] or "0:1/1").partition("/")
                    rng = (int(ab.split(":")[0]), int(ab.split(":")[1]), int(n))
                jobs.append((names.index(name), over_ici, rng))
        return _gather_rider([self.bufs[n] for n in names], jobs), lambda res: self.bufs.update(zip(names, res))

    def swap(self, names):
        def done(res):
            sums = _add_my_half("chip_sum_" + "_".join(names), [self.parts[n] for n in names], res, self.core_arr)
            self.sums.update(zip(names, sums))
        return _swap_halves_rider([self.parts[n] for n in names]), done

    def scatter(self, specs):
        names = [s.partition("{")[0] for s in specs]
        relations = [tuple(int(ch) for ch in s.partition("{")[2][:-1]) or (0, 1, 2) for s in specs]

        def done(res):
            for n, rel, landed in zip(names, relations, res):
                self.landed[n] = landed
                self.arrived[n] = self.arrived.get(n, ()) + rel
                if len(self.arrived[n]) < 3:
                    continue
                key, layer, col = _block_of(n)
                self.blocks[key] = _owner_sum("owner_sum_" + n, self.sums[n], landed, self.shapes[key],
                                              self.chip_arr, self.core_arr, self.blocks.get(key), layer, col)
        rider = _scatter_rider([self.sums[n] for n in names], [self.landed.get(n) for n in names], relations)
        return rider, done

    def share(self, names):
        keys, pieces = [], []
        for n in names:
            key, layer, col = _block_of(n)
            if key not in keys:
                keys.append(key)
            pieces.append((keys.index(key), layer, col))
        return _share_rider([self.blocks[k] for k in keys], pieces), lambda res: self.blocks.update(zip(keys, res))

    def run(self, fn, name, *args, **kw):
        made = [make() for make in self.plan.get(name, [])]
        if not made:
            return fn(name, *args, **kw)
        riders = [r for r, _ in made]
        out = _ride(riders[0] if len(riders) == 1 else _riders(*riders), fn, name, *args, **kw)
        for r, done in made:
            done(r.results)
        return out

    def w(self, name):
        b = self.bufs[name]
        if name in ("o", "p", "d0", "d1"):
            return b.reshape(-1, b.shape[-1])
        if name == "pg":
            b = b.reshape(N_CHIPS, 4, self.pg_rows, POOL_DIM).transpose(1, 0, 2, 3)
            return b.reshape(4, POOL_DIM, POOL_DIM)
        if name in ("pool_norm", "pool_scale"):
            return b.reshape(1, -1)
        return b

    def grad(self, name, value):
        if name == "pg":
            value = value.reshape(4, N_CHIPS, self.pg_rows, POOL_DIM).transpose(1, 0, 2, 3)
            value = value.reshape(N_CHIPS, 4 * self.pg_rows, POOL_DIM).astype(BF16)
        elif name in ("o", "p", "d0", "d1"):
            value = value.reshape(N_CHIPS, value.shape[0] // N_CHIPS, value.shape[1])
        self.parts[name] = value


def kernel(x, attn_norm, w_qkv, w_attn_out, pool_norm, w_pool_in, w_pool_group, pool_scale, ffn_norm, w_ffn_gate_up, w_ffn_down, final_norm, loss_target, m_attn_norm, m_w_qkv, m_w_attn_out, m_pool_norm, m_w_pool_in, m_w_pool_group, m_pool_scale, m_ffn_norm, m_w_ffn_gate_up, m_w_ffn_down, m_final_norm, v_attn_norm, v_w_qkv, v_w_attn_out, v_pool_norm, v_w_pool_in, v_w_pool_group, v_pool_scale, v_ffn_norm, v_w_ffn_gate_up, v_w_ffn_down, v_final_norm):
    D = D_MODEL
    chip = 2 * lax.axis_index("x") + lax.axis_index("y")
    core = lax.axis_index("c")
    pg_rows = w_pool_group.shape[2]

    chip_arr = chip.astype(jnp.int32).reshape(1)
    core_arr = core.astype(jnp.int32).reshape(1)

    pieces = _cast_qkv("cast_qkv", w_qkv, chip_arr)
    bufs = dict(zip(["wq_own"] + ["wq%d" % p for p in range(QKV_PIECES)], pieces))
    shards = [(w_attn_out, 0, BF16), (w_pool_in, 0, BF16), (w_pool_group.reshape(1, 4 * pg_rows, POOL_DIM), 0, BF16),
              (w_ffn_gate_up, 0, BF16), (w_ffn_gate_up, 1, BF16), (w_ffn_down, 0, BF16), (w_ffn_down, 1, BF16),
              (pool_norm[None], 0, F32), (pool_scale[None], 0, F32)]
    bufs.update(zip(TENSORS[1:] + ("pool_norm", "pool_scale"), _cast_many("cast_rest", shards, chip_arr)))

    as_block = lambda a: a.reshape((-1,) + a.shape[-2:]) if a.ndim == 3 else a.reshape(1, -1, a.shape[-1])
    owned = dict(qkv=(w_qkv, m_w_qkv, v_w_qkv), o=(w_attn_out, m_w_attn_out, v_w_attn_out),
                 p=(w_pool_in, m_w_pool_in, v_w_pool_in), pg=(w_pool_group, m_w_pool_group, v_w_pool_group),
                 gu=(w_ffn_gate_up, m_w_ffn_gate_up, v_w_ffn_gate_up), d=(w_ffn_down, m_w_ffn_down, v_w_ffn_down))
    comm = _MeshComm(bufs, {k: as_block(wmv[0]).shape for k, wmv in owned.items()}, chip_arr, core_arr, pg_rows)
    loss_part, grad_x, small = _local_step(x[0], loss_target[0], attn_norm, ffn_norm, final_norm.reshape(1, D), comm)

    rows = jnp.concatenate([small["attn"], small["ffn0"], small["ffn1"], small["final"], small["pool"],
                            small["scale"], jnp.pad(loss_part, ((0, 0), (0, D - 1))), jnp.zeros((1, D), F32)], axis=0)
    all_rows = _gather_small("gather_gain_grads", rows)
    tot = _sum_leading("sum_gain_grads", all_rows, F32)
    loss = tot[6, 0]
    g_attn_norm = tot[0:1]
    g_ffn_norm = tot[1:3]
    g_final_norm = tot[3]
    g_pool_norm = lax.dynamic_slice(tot, (4, chip * POOL_DIM), (1, POOL_DIM))
    g_pool_scale = lax.dynamic_slice(tot, (5, chip * POOL_DIM), (1, POOL_DIM))

    as_rows = lambda a: a.reshape(-1, a.shape[-1])
    res = _adamw("adamw_weights", [(as_rows(w), as_rows(comm.blocks[k]), as_rows(m), as_rows(v))
                                   for k, (w, m, v) in owned.items()])
    updated = {k: [a.reshape(owned[k][0].shape) for a in r] for k, r in zip(owned, res)}
    gains = [(attn_norm, g_attn_norm, m_attn_norm, v_attn_norm), (pool_norm, g_pool_norm, m_pool_norm, v_pool_norm),
             (pool_scale, g_pool_scale, m_pool_scale, v_pool_scale), (ffn_norm, g_ffn_norm, m_ffn_norm, v_ffn_norm),
             (final_norm, g_final_norm, m_final_norm, v_final_norm)]
    small_res = _adamw_small("adamw_gains", [tuple(as_rows(a) for a in item) for item in gains])
    small_res = [[a.reshape(item[0].shape) for a in r] for r, item in zip(small_res, gains)]
    results = [small_res[0], updated["qkv"], updated["o"], small_res[1], updated["p"], updated["pg"], small_res[2],
               small_res[3], updated["gu"], updated["d"], small_res[4]]
    grads = [r[0] for r in results]
    deltas = [r[1] for r in results]
    new_m = [r[2] for r in results]
    new_v = [r[3] for r in results]
    return (loss, grad_x[None], *grads, *deltas, *new_m, *new_v)
```

```python
import jax
import jax.numpy as jnp
from jax import lax
from jax.experimental import pallas as pl
from jax.experimental.pallas import tpu as pltpu

F32 = jnp.float32
BF16 = jnp.bfloat16
MESH = pl.DeviceIdType.MESH

D_MODEL = 1024
N_HEADS = 8
HEAD_DIM = 128
Q_BLOCK = 128
ATTN_DILATIONS = (1, 4, 16)
N_GROUPS = 3
D_FF = 2816
N_CHIPS = 4
FF_SHARD = 2 * D_FF // N_CHIPS
QKV_SHARD = 3 * 3 * D_MODEL // N_CHIPS
QKV_TILE = 768
POOL_DIM = 256
POOL_HALO = 16
RMS_EPS = 1e-6
NEG = -1e30
ADAM_LR = 0.001
ADAM_B1 = 0.9
ADAM_B2 = 0.999
ADAM_EPS = 1e-08
ADAM_WD = 0.01
ADAM_STEP = 10
VMEM_LIMIT = 56 * 1024 * 1024

_DN = {
    "nn": (((1,), (0,)), ((), ())),
    "nt": (((1,), (1,)), ((), ())),
    "tn": (((0,), (0,)), ((), ())),
}


class _Rider:
    def __init__(self, operands, out_shapes, aliases, n_copies, copies=None, start=None, finish=None):
        self.operands = list(operands)
        self.out_shapes = list(out_shapes)
        self.aliases = dict(aliases)
        self.n_copies = n_copies
        self.results = None

        def start_copies(ins, outs, send, recv, base=0):
            for cp in copies(ins, outs, send, recv, base):
                cp.start()

        def wait_copies(ins, outs, send, recv, base=0):
            for cp in copies(ins, outs, send, recv, base):
                cp.wait()

        self.start = start or start_copies
        self.finish = finish or wait_copies


def _riders(*riders):
    operands, out_shapes, aliases, offsets = [], [], {}, []
    n = 0
    for r in riders:
        offsets.append((len(operands), len(out_shapes), n))
        aliases.update({len(operands) + i: len(out_shapes) + o for i, o in r.aliases.items()})
        operands += r.operands
        out_shapes += r.out_shapes
        n += r.n_copies

    def each(which):
        def go(ins, outs, send, recv, base=0):
            for r, (i0, o0, s0) in zip(riders, offsets):
                getattr(r, which)(ins[i0:i0 + len(r.operands)], outs[o0:o0 + len(r.out_shapes)], send, recv,
                                  base + s0)
        return go

    both = _Rider(operands, out_shapes, aliases, n, start=each("start"), finish=each("finish"))
    both.parts = (riders, offsets)
    return both


_pending_rider = []


def _ride(rider, fn, *args, **kw):
    _pending_rider.append(rider)
    out = fn(*args, **kw)
    assert not _pending_rider
    if hasattr(rider, "parts"):
        for r, (_, o0, _) in zip(*rider.parts):
            r.results = rider.results[o0:o0 + len(r.out_shapes)]
    return out


def _pcall(body, prefetch=0, **kw):
    def build(body, kw):
        if not prefetch:
            return pl.pallas_call(body, **kw)
        kw = dict(kw)
        grid_spec = pltpu.PrefetchScalarGridSpec(
            num_scalar_prefetch=prefetch, grid=kw.pop("grid"), in_specs=kw.pop("in_specs"),
            out_specs=kw.pop("out_specs"), scratch_shapes=kw.pop("scratch_shapes", []))
        return pl.pallas_call(body, grid_spec=grid_spec, **kw)

    if not _pending_rider:
        return build(body, kw)
    rider = _pending_rider.pop()
    grid = kw.get("grid", ())
    in_specs = list(kw.get("in_specs", []))
    single = not isinstance(kw["out_shape"], (list, tuple))
    out_shape = [kw["out_shape"]] if single else list(kw["out_shape"])
    out_specs = [kw["out_specs"]] if single else list(kw["out_specs"])
    scratch = list(kw.get("scratch_shapes", []))
    n_in, n_out, n_scr = len(in_specs), len(out_shape), len(scratch)
    r_in, r_out = len(rider.operands), len(rider.out_shapes)

    def wrapped(*refs):
        scalars, refs = refs[:prefetch], refs[prefetch:]
        ins, rins = refs[:n_in], refs[n_in:n_in + r_in]
        outs = refs[n_in + r_in:n_in + r_in + n_out]
        routs = refs[n_in + r_in + n_out:n_in + r_in + n_out + r_out]
        scr = refs[n_in + r_in + n_out + r_out:n_in + r_in + n_out + r_out + n_scr]
        send, recv = refs[-2:]
        ids = [pl.program_id(a) for a in range(len(grid))]
        first, last = True, True
        for a, pid in enumerate(ids):
            first = jnp.logical_and(first, pid == 0)
            last = jnp.logical_and(last, pid == grid[a] - 1)
        if grid:
            pl.when(first)(lambda: rider.start(rins, routs, send, recv))
        else:
            rider.start(rins, routs, send, recv)
        body(*scalars, *ins, *outs, *scr)
        if grid:
            pl.when(last)(lambda: rider.finish(rins, routs, send, recv))
        else:
            rider.finish(rins, routs, send, recv)

    any_spec = pl.BlockSpec(memory_space=pl.ANY)
    kw2 = dict(kw)
    kw2.update(
        in_specs=in_specs + [any_spec] * r_in, out_specs=out_specs + [any_spec] * r_out,
        out_shape=out_shape + rider.out_shapes,
        scratch_shapes=scratch + [pltpu.SemaphoreType.DMA((rider.n_copies,)),
                                  pltpu.SemaphoreType.DMA((rider.n_copies,))],
        input_output_aliases={**kw.get("input_output_aliases", {}),
                              **{prefetch + n_in + i: n_out + o for i, o in rider.aliases.items()}})
    if grid:
        kw2["compiler_params"] = _params(("arbitrary",) * len(grid))
    call = build(wrapped, kw2)

    def run(*operands):
        res = call(*operands, *rider.operands)
        rider.results = list(res[n_out:])
        return res[0] if single else list(res[:n_out])

    return run


def _params(sem):
    return pltpu.CompilerParams(dimension_semantics=sem, vmem_limit_bytes=VMEM_LIMIT)


def _dot(a, b, mode):
    return lax.dot_general(a, b, _DN[mode], preferred_element_type=F32)


def _mm(name, mode, grid, a, a_spec, b, b_spec, out_shape, o_spec, acc_shape):
    nk = grid[-1]

    def body(a_ref, b_ref, o_ref, *acc):
        part = _dot(a_ref[...].astype(BF16), b_ref[...].astype(BF16), mode)
        if nk == 1:
            o_ref[...] = part.astype(o_ref.dtype)
            return
        acc_ref, = acc
        k = pl.program_id(len(grid) - 1)

        @pl.when(k == 0)
        def _():
            acc_ref[...] = part

        @pl.when(k > 0)
        def _():
            acc_ref[...] += part

        @pl.when(k == nk - 1)
        def _():
            o_ref[...] = acc_ref[...].astype(o_ref.dtype)

    scratch = [] if nk == 1 else [pltpu.VMEM(acc_shape, F32)]
    sem = ("parallel",) * (len(grid) - 1) + ("arbitrary",)
    return _pcall(body, name=name, grid=grid, in_specs=[a_spec, b_spec], out_specs=o_spec, out_shape=out_shape,
                  scratch_shapes=scratch, compiler_params=_params(sem))(a, b)


def _mm_tn(name, a, b, tm, tk):
    T, M = a.shape
    N = b.shape[1]
    return _mm(name, "tn", (M // tm, T // tk), a, pl.BlockSpec((tk, tm), lambda i, k: (k, i)),
               b, pl.BlockSpec((tk, N), lambda i, k: (k, 0)),
               jax.ShapeDtypeStruct((M, N), BF16), pl.BlockSpec((tm, N), lambda i, k: (i, 0)), (tm, N))


def _norm_bwd_rows(xf, gain, dh, dres):
    r = lax.rsqrt(jnp.mean(xf * xf, axis=-1, keepdims=True) + RMS_EPS)
    xh = xf * r
    t = dh * gain
    dx = dres + r * (t - xh * jnp.mean(t * xh, axis=-1, keepdims=True))
    return dx, jnp.sum(dh * xh, axis=0, keepdims=True)


def _accumulate(ref, value, first):
    @pl.when(first)
    def _():
        ref[...] = value

    @pl.when(jnp.logical_not(first))
    def _():
        ref[...] += value


def _rms_bwd_folded(name, x, g, dhf, dres, tb=512):
    S, D = x.shape

    def body(x_ref, g_ref, d0_ref, d1_ref, d2_ref, dres_ref, dx_ref, dg_ref, dh_ref):
        chunks = _lane_chunks(D)
        for c, cols in enumerate(chunks):
            dh_ref[c] = d0_ref[0, :, cols].astype(F32)
        for dil, ref in ((ATTN_DILATIONS[1], d1_ref), (ATTN_DILATIONS[2], d2_ref)):
            n = tb // dil
            for k in range(dil):
                for c, cols in enumerate(chunks):
                    dh_ref[c, _strided_rows(k, n, dil), :] += ref[k, :, cols].astype(F32)
        dh = jnp.concatenate([dh_ref[c] for c in range(len(chunks))], axis=1)
        dx, dg = _norm_bwd_rows(x_ref[...], g_ref[...], dh, dres_ref[...])
        dx_ref[...] = dx
        _accumulate(dg_ref, dg, pl.program_id(0) == 0)

    views, specs = _folded_views(dhf, tb)
    row = pl.BlockSpec((tb, D), lambda i: (i, 0))
    vec = pl.BlockSpec((1, D), lambda i: (0, 0))
    return _pcall(body, name=name, grid=(S // tb,), in_specs=[row, vec] + specs + [row], out_specs=[row, vec],
                  out_shape=[jax.ShapeDtypeStruct((S, D), F32), jax.ShapeDtypeStruct((1, D), F32)],
                  scratch_shapes=[pltpu.VMEM((D // LANES, tb, LANES), F32)],
                  compiler_params=_params(("arbitrary",)))(x, g, *views, dres)


def _proj_res_norm(name, a, w, res, gain, tm=512):
    M, K = a.shape
    D = w.shape[1]

    def body(a_ref, w_ref, res_ref, g_ref, x_ref, h_ref):
        xf = res_ref[...] + _dot(a_ref[...], w_ref[...], "nn")
        x_ref[...] = xf
        r = lax.rsqrt(jnp.mean(xf * xf, axis=-1, keepdims=True) + RMS_EPS)
        h_ref[...] = ((xf * r) * g_ref[...]).astype(h_ref.dtype)

    row = pl.BlockSpec((tm, D), lambda i: (i, 0))
    return _pcall(body, name=name, grid=(M // tm,),
                  in_specs=[pl.BlockSpec((tm, K), lambda i: (i, 0)), pl.BlockSpec((K, D), lambda i: (0, 0)), row,
                            pl.BlockSpec((1, D), lambda i: (0, 0))],
                  out_specs=[row, row],
                  out_shape=[jax.ShapeDtypeStruct((M, D), F32), jax.ShapeDtypeStruct((M, D), BF16)],
                  compiler_params=_params(("parallel",)))(a, w, res, gain)


def _proj_res_loss(name, a, w, res, gain, tgt, tm=512):
    M, K = a.shape
    D = w.shape[1]

    def body(a_ref, w_ref, res_ref, g_ref, t_ref, loss_ref, dx_ref, dxb_ref, dg_ref):
        first = pl.program_id(0) == 0
        xf = res_ref[...] + _dot(a_ref[...], w_ref[...], "nn")
        r = lax.rsqrt(jnp.mean(xf * xf, axis=-1, keepdims=True) + RMS_EPS)
        xh = xf * r
        gv = g_ref[...]
        e = xh * gv - t_ref[...]
        lp = 0.5 * jnp.sum(jnp.mean(e * e, axis=-1, keepdims=True), axis=0, keepdims=True)
        dy = e * (1.0 / D)
        t = dy * gv
        dx = r * (t - xh * jnp.mean(t * xh, axis=-1, keepdims=True))
        dx_ref[...] = dx
        dxb_ref[...] = dx.astype(dxb_ref.dtype)
        _accumulate(dg_ref, jnp.sum(dy * xh, axis=0, keepdims=True), first)
        _accumulate(loss_ref, lp, first)

    row = pl.BlockSpec((tm, D), lambda i: (i, 0))
    vec = pl.BlockSpec((1, D), lambda i: (0, 0))
    return _pcall(body, name=name, grid=(M // tm,),
                  in_specs=[pl.BlockSpec((tm, K), lambda i: (i, 0)), pl.BlockSpec((K, D), lambda i: (0, 0)), row,
                            vec, row],
                  out_specs=[pl.BlockSpec((1, 1), lambda i: (0, 0)), row, row, vec],
                  out_shape=[jax.ShapeDtypeStruct((1, 1), F32), jax.ShapeDtypeStruct((M, D), F32),
                             jax.ShapeDtypeStruct((M, D), BF16), jax.ShapeDtypeStruct((1, D), F32)],
                  compiler_params=_params(("arbitrary",)))(a, w, res, gain, tgt)


def _dh_norm_bwd(name, d, w, x, gain, dres, tm=512):
    M, N = d.shape
    D = w.shape[0]

    def body(d_ref, w_ref, x_ref, g_ref, dres_ref, dx_ref, dxb_ref, dg_ref):
        dh = _dot(d_ref[...].astype(BF16), w_ref[...], "nt")
        dx, dg = _norm_bwd_rows(x_ref[...], g_ref[...], dh, dres_ref[...])
        dx_ref[...] = dx
        dxb_ref[...] = dx.astype(dxb_ref.dtype)
        _accumulate(dg_ref, dg, pl.program_id(0) == 0)

    row = pl.BlockSpec((tm, D), lambda i: (i, 0))
    vec = pl.BlockSpec((1, D), lambda i: (0, 0))
    return _pcall(body, name=name, grid=(M // tm,),
                  in_specs=[pl.BlockSpec((tm, N), lambda i: (i, 0)), pl.BlockSpec((D, N), lambda i: (0, 0)), row, vec,
                            row],
                  out_specs=[row, row, vec],
                  out_shape=[jax.ShapeDtypeStruct((M, D), F32), jax.ShapeDtypeStruct((M, D), BF16),
                             jax.ShapeDtypeStruct((1, D), F32)],
                  compiler_params=_params(("arbitrary",)))(d, w, x, gain, dres)


def _sigmoid(v):
    return 0.5 * jnp.tanh(0.5 * v) + 0.5


def _ffn_up(name, h, w_gu, tm=1024):
    S, D = h.shape
    W = FF_SHARD

    def body(h_ref, wg_ref, wu_ref, gu_ref, act_ref):
        hv = h_ref[...]
        gate = _dot(hv, wg_ref[...], "nn")
        up = _dot(hv, wu_ref[...], "nn")
        gu_ref[0] = gate.astype(gu_ref.dtype)
        gu_ref[1] = up.astype(gu_ref.dtype)
        sig = _sigmoid(gate)
        act_ref[...] = ((gate * sig) * up).astype(act_ref.dtype)

    return _pcall(
        body, name=name, grid=(2, S // tm),
        in_specs=[pl.BlockSpec((tm, D), lambda j, i: (i, 0)),
                  pl.BlockSpec((None, D, W), lambda j, i: (j, 0, 0)),
                  pl.BlockSpec((None, D, W), lambda j, i: (j + 2, 0, 0))],
        out_specs=[pl.BlockSpec((2, tm, W), lambda j, i: (0, i, j)), pl.BlockSpec((tm, W), lambda j, i: (i, j))],
        out_shape=[jax.ShapeDtypeStruct((2, S, D_FF), BF16), jax.ShapeDtypeStruct((S, D_FF), BF16)],
        compiler_params=_params(("parallel", "parallel")))(h, w_gu, w_gu)


def _ffn_down_bwd(name, dx, w_down, gu, tm=1024):
    S, D = dx.shape
    W = FF_SHARD

    def body(dx_ref, w_ref, gu_ref, dgu_ref):
        dact = _dot(dx_ref[...].astype(BF16), w_ref[...], "nt")
        gate = gu_ref[0].astype(F32)
        up = gu_ref[1].astype(F32)
        sig = _sigmoid(gate)
        silu = gate * sig
        dgu_ref[0] = (dact * up * (sig * (1.0 + gate * (1.0 - sig)))).astype(dgu_ref.dtype)
        dgu_ref[1] = (dact * silu).astype(dgu_ref.dtype)

    blk = pl.BlockSpec((2, tm, W), lambda j, i: (0, i, j))
    return _pcall(
        body, name=name, grid=(2, S // tm),
        in_specs=[pl.BlockSpec((tm, D), lambda j, i: (i, 0)), pl.BlockSpec((W, D), lambda j, i: (j, 0)), blk],
        out_specs=blk, out_shape=jax.ShapeDtypeStruct((2, S, D_FF), BF16),
        compiler_params=_params(("parallel", "parallel")))(dx, w_down, gu)


def _ffn_dw_up(name, h, dgu, tk=2048):
    S, D = h.shape
    W = FF_SHARD
    tk = min(tk, S)
    return _mm(name, "tn", (N_CHIPS, S // tk), h, pl.BlockSpec((tk, D), lambda s, k: (k, 0)),
               dgu, pl.BlockSpec((None, tk, W), lambda s, k: (s // 2, k, s % 2)),
               jax.ShapeDtypeStruct((N_CHIPS, D, W), BF16), pl.BlockSpec((None, D, W), lambda s, k: (s, 0, 0)),
               (D, W))


def _ffn_dh(name, dgu, w_gu, x, gain, dres, tm=512):
    S = dgu.shape[1]
    W = FF_SHARD

    def body(a_ref, w_hbm, x_ref, g_ref, dres_ref, dx_ref, dxb_ref, dg_ref, w_ref, sems):
        first = pl.program_id(0) == 0

        @pl.when(first)
        def _():
            copies = [pltpu.make_async_copy(w_hbm.at[s], w_ref.at[:, pl.ds(s * W, W)], sems.at[s])
                      for s in range(N_CHIPS)]
            for cp in copies:
                cp.start()
            for cp in copies:
                cp.wait()

        dh = _dot(a_ref[0], w_ref[:, :D_FF], "nt") + _dot(a_ref[1], w_ref[:, D_FF:], "nt")
        dx, dg = _norm_bwd_rows(x_ref[...], g_ref[...], dh, dres_ref[...])
        dx_ref[...] = dx
        dxb_ref[...] = dx.astype(dxb_ref.dtype)
        _accumulate(dg_ref, dg, first)

    row = pl.BlockSpec((tm, D_MODEL), lambda i: (i, 0))
    vec = pl.BlockSpec((1, D_MODEL), lambda i: (0, 0))
    return _pcall(body, name=name, grid=(S // tm,),
                  in_specs=[pl.BlockSpec((2, tm, D_FF), lambda i: (0, i, 0)), pl.BlockSpec(memory_space=pl.ANY),
                            row, vec, row],
                  out_specs=[row, row, vec],
                  out_shape=[jax.ShapeDtypeStruct((S, D_MODEL), F32), jax.ShapeDtypeStruct((S, D_MODEL), BF16),
                             jax.ShapeDtypeStruct((1, D_MODEL), F32)],
                  scratch_shapes=[pltpu.VMEM((D_MODEL, 2 * D_FF), BF16), pltpu.SemaphoreType.DMA((N_CHIPS,))],
                  compiler_params=_params(("arbitrary",)))(dgu, w_gu, x, gain, dres)


FOLD_TOKENS = 1024
LANES = 128


def _lane_chunks(width):
    return [slice(c * LANES, (c + 1) * LANES) for c in range(width // LANES)]


def _strided_rows(r, n, dil):
    return pl.ds(r, n) if dil == 1 else pl.ds(r, n, stride=dil)


def _norm_fold(name, x, gain):
    S, D = x.shape
    chunks = _lane_chunks(D)

    def body(x_ref, g_ref, hf_hbm, xc_ref, hs_ref, sems):
        i = pl.program_id(0)
        xf = x_ref[...]
        inv = lax.rsqrt(jnp.mean(xf * xf, axis=-1, keepdims=True) + RMS_EPS)
        h = (xf * inv) * g_ref[...]
        hs_ref[0] = h.astype(BF16)
        for c, cols in enumerate(chunks):
            xc_ref[c] = h[:, cols]
        copies = []
        for g, dil in enumerate(ATTN_DILATIONS):
            n = FOLD_TOKENS // dil
            for r in range(dil):
                if dil > 1:
                    for c, cols in enumerate(chunks):
                        hs_ref[g, r * n:(r + 1) * n, cols] = xc_ref[c, _strided_rows(r, n, dil), :].astype(BF16)
                cp = pltpu.make_async_copy(hs_ref.at[g, pl.ds(r * n, n)],
                                           hf_hbm.at[g, pl.ds(r * (S // dil) + i * n, n)], sems.at[len(copies)])
                cp.start()
                copies.append(cp)
        for cp in copies:
            cp.wait()

    return _pcall(body, name=name, grid=(S // FOLD_TOKENS,),
                  in_specs=[pl.BlockSpec((FOLD_TOKENS, D), lambda i: (i, 0)), pl.BlockSpec((1, D), lambda i: (0, 0))],
                  out_specs=pl.BlockSpec(memory_space=pl.ANY),
                  out_shape=jax.ShapeDtypeStruct((N_GROUPS, S, D), BF16),
                  scratch_shapes=[pltpu.VMEM((D // LANES, FOLD_TOKENS, LANES), F32),
                                  pltpu.VMEM((N_GROUPS, FOLD_TOKENS, D), BF16),
                                  pltpu.SemaphoreType.DMA((sum(ATTN_DILATIONS),))],
                  compiler_params=_params(("arbitrary",)))(x, gain)


def _qkv_tiles(name, piece, hf, w, qkv, chip, tm=1024):
    S = hf.shape[1]
    per_group = 3 * D_MODEL // QKV_TILE

    def tile(q, c_ref):
        if piece is None:
            return QKV_PIECES * c_ref[0] + q
        return QKV_PIECES * ((c_ref[0] + 1 + q) % N_CHIPS) + piece

    def body(*refs):
        a_ref, w_ref, o_ref = refs[1], refs[2], refs[-1]
        o_ref[...] = _dot(a_ref[...], w_ref[...], "nn").astype(o_ref.dtype)

    if piece is None:
        w_spec = pl.BlockSpec((D_MODEL, QKV_TILE), lambda q, i, c_ref: (0, q))
    else:
        w_spec = pl.BlockSpec((None, D_MODEL, QKV_TILE), lambda q, i, c_ref: ((c_ref[0] + 1 + q) % N_CHIPS, 0, 0))
    in_specs = [pl.BlockSpec((None, tm, D_MODEL), lambda q, i, c_ref: (tile(q, c_ref) // per_group, i, 0)), w_spec]
    operands = [chip, hf, w]
    aliases = {}
    if qkv is not None:
        in_specs.append(pl.BlockSpec(memory_space=pl.ANY))
        operands.append(qkv)
        aliases = {3: 0}
    return _pcall(
        body, prefetch=1, name=name, grid=(N_CHIPS - 1, S // tm), in_specs=in_specs,
        out_specs=pl.BlockSpec((None, tm, QKV_TILE),
                               lambda q, i, c_ref: (tile(q, c_ref) // per_group, i, tile(q, c_ref) % per_group)),
        out_shape=jax.ShapeDtypeStruct((N_GROUPS, S, 3 * D_MODEL), BF16), input_output_aliases=aliases,
        compiler_params=_params(("arbitrary", "arbitrary")))(*operands)


QKV_PIECES = QKV_SHARD // QKV_TILE


def _qkv_dw(name, p, hf, dqkv):
    S = hf.shape[1]
    per_group = 3 * D_MODEL // QKV_TILE
    tile = lambda s: QKV_PIECES * s + p
    return _mm(name, "tn", (N_CHIPS, 1),
               hf, pl.BlockSpec((None, S, D_MODEL), lambda s, k: (tile(s) // per_group, 0, 0)),
               dqkv, pl.BlockSpec((None, S, QKV_TILE), lambda s, k: (tile(s) // per_group, 0, tile(s) % per_group)),
               jax.ShapeDtypeStruct((N_CHIPS, D_MODEL, QKV_TILE), BF16),
               pl.BlockSpec((None, D_MODEL, QKV_TILE), lambda s, k: (s, 0, 0)), None)


def _qkv_dh(name, g, dqkv, w_pieces, dhf, tm=1024):
    S = dqkv.shape[1]
    per_group = 3 * D_MODEL // QKV_TILE

    def body(*refs):
        a_ref, w_hbm = refs[0], refs[1:1 + QKV_PIECES]
        o_ref, w_ref, sems = refs[-3:]

        @pl.when(pl.program_id(0) == 0)
        def _():
            copies = []
            for j in range(per_group):
                t = per_group * g + j
                copies.append(pltpu.make_async_copy(w_hbm[t % QKV_PIECES].at[t // QKV_PIECES],
                                                    w_ref.at[:, pl.ds(j * QKV_TILE, QKV_TILE)], sems.at[j]))
            for cp in copies:
                cp.start()
            for cp in copies:
                cp.wait()

        o_ref[...] = _dot(a_ref[...], w_ref[...], "nt").astype(o_ref.dtype)

    any_spec = pl.BlockSpec(memory_space=pl.ANY)
    in_specs = [pl.BlockSpec((None, tm, 3 * D_MODEL), lambda i: (g, i, 0))] + [any_spec] * QKV_PIECES
    operands = [dqkv] + list(w_pieces)
    aliases = {}
    if dhf is not None:
        in_specs.append(any_spec)
        operands.append(dhf)
        aliases = {1 + QKV_PIECES: 0}
    return _pcall(body, name=name, grid=(S // tm,), in_specs=in_specs,
                  out_specs=pl.BlockSpec((None, tm, D_MODEL), lambda i: (g, i, 0)),
                  out_shape=jax.ShapeDtypeStruct((N_GROUPS, S, D_MODEL), BF16),
                  scratch_shapes=[pltpu.VMEM((D_MODEL, 3 * D_MODEL), BF16), pltpu.SemaphoreType.DMA((per_group,))],
                  input_output_aliases=aliases, compiler_params=_params(("arbitrary",)))(*operands)


def _alibi_coef(g, h):
    vals = [-(2.0 ** (-8.0 * (gg * N_HEADS + h + 1) / (N_GROUPS * N_HEADS))) * ATTN_DILATIONS[gg]
            for gg in range(N_GROUPS)]
    return jnp.where(g == 0, vals[0], jnp.where(g == 1, vals[1], vals[2])).astype(F32)


def _blocks_per_segment(g, S):
    return jnp.right_shift(S // Q_BLOCK, 2 * g)


def _band_bias(pen):
    row = lax.broadcasted_iota(jnp.int32, (Q_BLOCK, 2 * Q_BLOCK), 0)
    col = lax.broadcasted_iota(jnp.int32, (Q_BLOCK, 2 * Q_BLOCK), 1)
    dist = Q_BLOCK + row - col
    valid = jnp.logical_and(dist >= 0, dist <= Q_BLOCK)
    base = jnp.where(valid, jnp.where(col < Q_BLOCK, pen, 0.0), NEG).astype(F32)
    return base, dist.astype(F32)


def _skewed(n_units, stages):
    state = {}
    for step in range(n_units + len(stages) - 1):
        for s, stage in enumerate(stages):
            u = step - s
            if 0 <= u < n_units:
                state[u] = stage(u, state.get(u))
                if s == len(stages) - 1:
                    del state[u]


def _attn_fwd(name, qkv, tq=1024):
    S = qkv.shape[1]
    nqb = tq // Q_BLOCK
    scale = HEAD_DIM ** -0.5

    def body(q_ref, k_ref, kp_ref, v_ref, vp_ref, o_ref, lse_ref, kf_ref, vf_ref):
        g = pl.program_id(0)
        i = pl.program_id(1)
        nb = _blocks_per_segment(g, S)
        kf_ref[:Q_BLOCK] = kp_ref[...]
        kf_ref[Q_BLOCK:] = k_ref[...]
        vf_ref[:Q_BLOCK] = vp_ref[...]
        vf_ref[Q_BLOCK:] = v_ref[...]
        coefs = [_alibi_coef(g, h) for h in range(N_HEADS)]
        units = [(b, h) for b in range(nqb) for h in range(N_HEADS)]
        bias = {}

        def scores(u, _):
            b, h = units[u]
            if b not in bias:
                pen = jnp.where((i * nqb + b) % nb != 0, 0.0, NEG).astype(F32)
                bias.clear()
                bias[b] = _band_bias(pen)
            base, dist = bias[b]
            cols = slice(h * HEAD_DIM, (h + 1) * HEAD_DIM)
            q = q_ref[b * Q_BLOCK:(b + 1) * Q_BLOCK, cols]
            kk = kf_ref[b * Q_BLOCK:(b + 2) * Q_BLOCK, cols]
            return _dot(q, kk, "nt") * scale + (coefs[h] * dist + base)

        def softmax(u, s):
            m = jnp.max(s, axis=-1, keepdims=True)
            p = jnp.exp(s - m)
            den = jnp.sum(p, axis=-1, keepdims=True)
            return (p * (1.0 / den)).astype(BF16), m + jnp.log(den)

        def output(u, st):
            b, h = units[u]
            pn, lse = st
            cols = slice(h * HEAD_DIM, (h + 1) * HEAD_DIM)
            rows = slice(b * Q_BLOCK, (b + 1) * Q_BLOCK)
            o_ref[rows, cols] = _dot(pn, vf_ref[b * Q_BLOCK:(b + 2) * Q_BLOCK, cols], "nn").astype(o_ref.dtype)
            lse_ref[rows, h:h + 1] = lse

        _skewed(len(units), [scores, softmax, output])

    main = lambda c: pl.BlockSpec((None, tq, D_MODEL), lambda g, i: (g, i, c))
    prev = lambda c: pl.BlockSpec((None, Q_BLOCK, D_MODEL), lambda g, i: (g, jnp.maximum(i * nqb - 1, 0), c))
    return _pcall(
        body, name=name, grid=(N_GROUPS, S // tq),
        in_specs=[main(0), main(1), prev(1), main(2), prev(2)],
        out_specs=[pl.BlockSpec((None, tq, D_MODEL), lambda g, i: (g, i, 0)),
                   pl.BlockSpec((None, tq, N_HEADS), lambda g, i: (g, i, 0))],
        out_shape=[jax.ShapeDtypeStruct((N_GROUPS, S, D_MODEL), BF16),
                   jax.ShapeDtypeStruct((N_GROUPS, S, N_HEADS), F32)],
        scratch_shapes=[pltpu.VMEM((tq + Q_BLOCK, D_MODEL), BF16), pltpu.VMEM((tq + Q_BLOCK, D_MODEL), BF16)],
        compiler_params=_params(("parallel", "parallel")))(qkv, qkv, qkv, qkv, qkv)


def _attn_bwd(name, qkv, do, lse, dl, tq=1024):
    S = qkv.shape[1]
    nqb = tq // Q_BLOCK
    n_blocks = S // Q_BLOCK
    scale = HEAD_DIM ** -0.5
    KEYS = 2 * Q_BLOCK

    def body(q_ref, k_ref, v_ref, kp_ref, vp_ref, qn_ref, do_ref, don_ref, lse_ref, lsen_ref, dl_ref, dln_ref,
             out_ref, kf_ref, vf_ref, dk_ref, dv_ref):
        g = pl.program_id(0)
        i = pl.program_id(1)
        nb = _blocks_per_segment(g, S)
        kf_ref[:Q_BLOCK] = kp_ref[...]
        kf_ref[Q_BLOCK:] = k_ref[...]
        vf_ref[:Q_BLOCK] = vp_ref[...]
        vf_ref[Q_BLOCK:] = v_ref[...]
        coefs = [_alibi_coef(g, h) for h in range(N_HEADS)]
        units = [(h, b) for h in range(N_HEADS) for b in range(nqb + 1)]
        bias = {}

        def band(b):
            if b not in bias:
                blk = i * nqb + b
                ok = blk % nb != 0
                if b == nqb:
                    ok = jnp.logical_and(ok, blk < n_blocks)
                bias[b] = _band_bias(jnp.where(ok, 0.0, NEG).astype(F32))
            return bias[b]

        def operands(h, b):
            cols = slice(h * HEAD_DIM, (h + 1) * HEAD_DIM)
            if b == nqb:
                keys = slice(tq, tq + Q_BLOCK)
                return (qn_ref[:, cols], don_ref[:, cols], lsen_ref[:, h:h + 1], dln_ref[:, h:h + 1],
                        kf_ref[keys, cols], vf_ref[keys, cols])
            rows = slice(b * Q_BLOCK, (b + 1) * Q_BLOCK)
            keys = slice(b * Q_BLOCK, b * Q_BLOCK + KEYS)
            return (q_ref[rows, cols], do_ref[rows, cols], lse_ref[rows, h:h + 1], dl_ref[rows, h:h + 1],
                    kf_ref[keys, cols], vf_ref[keys, cols])

        def probs(u, _):
            h, b = units[u]
            q, do_b, lse_b, dl_b, kk, vv = operands(h, b)
            base, dist = band(b)
            if b == nqb:
                base, dist = base[:, :Q_BLOCK], dist[:, :Q_BLOCK]
            p = jnp.exp(_dot(q, kk, "nt") * scale + (coefs[h] * dist + base) - lse_b)
            return p, _dot(do_b, vv, "nt")

        def dscores(u, st):
            h, b = units[u]
            p, dp = st
            dl_b = operands(h, b)[3]
            return ((p * (dp - dl_b)) * scale).astype(BF16), p.astype(BF16)

        def grads(u, st):
            h, b = units[u]
            ds, pb = st
            q, do_b, _, _, kk, _ = operands(h, b)
            cols = slice(h * HEAD_DIM, (h + 1) * HEAD_DIM)
            if b < nqb:
                out_ref[b * Q_BLOCK:(b + 1) * Q_BLOCK, cols] = _dot(ds, kk, "nn").astype(out_ref.dtype)
            if b == 0:
                dk_ref[:Q_BLOCK] = _dot(ds[:, Q_BLOCK:], q, "tn")
                dv_ref[:Q_BLOCK] = _dot(pb[:, Q_BLOCK:], do_b, "tn")
                return None
            dk = _dot(ds, q, "tn")
            dv = _dot(pb, do_b, "tn")
            before = slice((b - 1) * Q_BLOCK, b * Q_BLOCK)
            dk_ref[before] += dk[:Q_BLOCK]
            dv_ref[before] += dv[:Q_BLOCK]
            if b < nqb:
                own = slice(b * Q_BLOCK, (b + 1) * Q_BLOCK)
                dk_ref[own] = dk[Q_BLOCK:]
                dv_ref[own] = dv[Q_BLOCK:]
            else:
                out_ref[:, D_MODEL + h * HEAD_DIM:D_MODEL + (h + 1) * HEAD_DIM] = dk_ref[...].astype(out_ref.dtype)
                out_ref[:, 2 * D_MODEL + h * HEAD_DIM:2 * D_MODEL + (h + 1) * HEAD_DIM] = (
                    dv_ref[...].astype(out_ref.dtype))
            return None

        _skewed(len(units), [probs, dscores, grads])

    nxt_blk = lambda i: jnp.minimum((i + 1) * nqb, n_blocks - 1)
    main = lambda c: pl.BlockSpec((None, tq, D_MODEL), lambda g, i: (g, i, c))
    prev = lambda c: pl.BlockSpec((None, Q_BLOCK, D_MODEL), lambda g, i: (g, jnp.maximum(i * nqb - 1, 0), c))
    row = pl.BlockSpec((None, tq, D_MODEL), lambda g, i: (g, i, 0))
    row_n = pl.BlockSpec((None, Q_BLOCK, D_MODEL), lambda g, i: (g, nxt_blk(i), 0))
    col = pl.BlockSpec((None, tq, N_HEADS), lambda g, i: (g, i, 0))
    col_n = pl.BlockSpec((None, Q_BLOCK, N_HEADS), lambda g, i: (g, nxt_blk(i), 0))
    return _pcall(
        body, name=name, grid=(N_GROUPS, S // tq),
        in_specs=[main(0), main(1), main(2), prev(1), prev(2), row_n, row, row_n, col, col_n, col, col_n],
        out_specs=pl.BlockSpec((None, tq, 3 * D_MODEL), lambda g, i: (g, i, 0)),
        out_shape=jax.ShapeDtypeStruct((N_GROUPS, S, 3 * D_MODEL), BF16),
        scratch_shapes=[pltpu.VMEM((tq + Q_BLOCK, D_MODEL), BF16), pltpu.VMEM((tq + Q_BLOCK, D_MODEL), BF16),
                        pltpu.VMEM((tq, HEAD_DIM), F32), pltpu.VMEM((tq, HEAD_DIM), F32)],
        compiler_params=_params(("parallel", "parallel")))(qkv, qkv, qkv, qkv, qkv, qkv, do, do, lse, lse, dl, dl)


def _group_weights(lse_ref):
    l0, l1, l2 = lse_ref[0], lse_ref[1], lse_ref[2]
    m = jnp.maximum(jnp.maximum(l0, l1), l2)
    e = [jnp.exp(l0 - m), jnp.exp(l1 - m), jnp.exp(l2 - m)]
    den = e[0] + e[1] + e[2]
    return [ei / den for ei in e]


MERGE_TOKENS = 512


def _folded_views(a, tb):
    _, S, C = a.shape
    views, specs = [], []
    for g, dil in enumerate(ATTN_DILATIONS):
        views.append(a.reshape(N_GROUPS * dil, S // dil, C))
        specs.append(pl.BlockSpec((dil, tb // dil, C), lambda i, g=g: (g, i, 0)))
    return views, specs


def _unfold_into(dst_ref, folded_refs):
    for g, (dil, ref) in enumerate(zip(ATTN_DILATIONS, folded_refs)):
        n = ref.shape[1]
        for r in range(dil):
            for c, cols in enumerate(_lane_chunks(ref.shape[2])):
                dst_ref[g, c, _strided_rows(r, n, dil), :] = ref[r, :, cols].astype(F32)


def _merge_fwd(name, o, lse, tb=MERGE_TOKENS):
    S = o.shape[1]

    def body(o0_ref, o1_ref, o2_ref, lse_ref, out_ref, o_ref):
        _unfold_into(o_ref, (o0_ref, o1_ref, o2_ref))
        w = _group_weights(lse_ref)
        for h in range(N_HEADS):
            cols = slice(h * HEAD_DIM, (h + 1) * HEAD_DIM)
            acc = w[0][:, h:h + 1] * o_ref[0, h]
            for gg in range(1, N_GROUPS):
                acc = acc + w[gg][:, h:h + 1] * o_ref[gg, h]
            out_ref[:, cols] = acc.astype(out_ref.dtype)

    views, specs = _folded_views(o, tb)
    return _pcall(body, name=name, grid=(S // tb,),
                  in_specs=specs + [pl.BlockSpec((N_GROUPS, tb, N_HEADS), lambda i: (0, i, 0))],
                  out_specs=pl.BlockSpec((tb, D_MODEL), lambda i: (i, 0)),
                  out_shape=jax.ShapeDtypeStruct((S, D_MODEL), BF16),
                  scratch_shapes=[pltpu.VMEM((N_GROUPS, N_HEADS, tb, HEAD_DIM), F32)],
                  compiler_params=_params(("parallel",)))(*views, lse)


def _merge_bwd(name, dx, w_out, o, lse, tb=MERGE_TOKENS):
    S = o.shape[1]
    n_chunks = sum(ATTN_DILATIONS)

    def body(dx_ref, w_ref, o0_ref, o1_ref, o2_ref, lse_ref, do_hbm, dl_ref, o_ref, dt_ref, df_ref, d_ref, sems):
        i = pl.program_id(0)
        d_ref[...] = _dot(dx_ref[...].astype(BF16), w_ref[...], "nt")
        _unfold_into(o_ref, (o0_ref, o1_ref, o2_ref))
        w = _group_weights(lse_ref)
        for h in range(N_HEADS):
            cols = slice(h * HEAD_DIM, (h + 1) * HEAD_DIM)
            dv = d_ref[:, cols]
            merged = w[0][:, h:h + 1] * o_ref[0, h]
            for gg in range(1, N_GROUPS):
                merged = merged + w[gg][:, h:h + 1] * o_ref[gg, h]
            dsum = jnp.sum(dv * merged, axis=-1, keepdims=True)
            for gg in range(N_GROUPS):
                wg = w[gg][:, h:h + 1]
                dt_ref[gg, h] = wg * dv
                dl_ref[gg, :, h:h + 1] = wg * dsum
        copies = []
        for gg, dil in enumerate(ATTN_DILATIONS):
            n = tb // dil
            for r in range(dil):
                for h, cols in enumerate(_lane_chunks(D_MODEL)):
                    df_ref[gg, r * n:(r + 1) * n, cols] = (
                        dt_ref[gg, h, _strided_rows(r, n, dil), :].astype(df_ref.dtype))
                cp = pltpu.make_async_copy(df_ref.at[gg, pl.ds(r * n, n)],
                                           do_hbm.at[gg, pl.ds(r * (S // dil) + i * n, n)], sems.at[len(copies)])
                cp.start()
                copies.append(cp)
        for cp in copies:
            cp.wait()

    views, specs = _folded_views(o, tb)
    small = pl.BlockSpec((N_GROUPS, tb, N_HEADS), lambda i: (0, i, 0))
    return _pcall(body, name=name, grid=(S // tb,),
                  in_specs=[pl.BlockSpec((tb, D_MODEL), lambda i: (i, 0)),
                            pl.BlockSpec((D_MODEL, D_MODEL), lambda i: (0, 0))] + specs + [small],
                  out_specs=[pl.BlockSpec(memory_space=pl.ANY), small],
                  out_shape=[jax.ShapeDtypeStruct((N_GROUPS, S, D_MODEL), BF16),
                             jax.ShapeDtypeStruct((N_GROUPS, S, N_HEADS), F32)],
                  scratch_shapes=[pltpu.VMEM((N_GROUPS, N_HEADS, tb, HEAD_DIM), F32),
                                  pltpu.VMEM((N_GROUPS, N_HEADS, tb, HEAD_DIM), F32),
                                  pltpu.VMEM((N_GROUPS, tb, D_MODEL), BF16), pltpu.VMEM((tb, D_MODEL), F32),
                                  pltpu.SemaphoreType.DMA((n_chunks,))],
                  compiler_params=_params(("arbitrary",)))(dx, w_out, *views, lse)


def _shift_rows(a, k):
    return pltpu.roll(a, k % a.shape[0], axis=0)


def _pool_level(g, levels):
    return jnp.where(g == 0, levels[0], jnp.where(g == 1, levels[1], jnp.where(g == 2, levels[2], levels[3])))


def _pool_fwd(name, h, w_in, w_group, scale, x_in, gain_next, tr=1024):
    S, D = h.shape
    H = POOL_HALO
    n_groups = D // POOL_DIM

    def body(h_ref, hh_ref, wi_ref, w_ref, sc_ref, x_ref, gn_ref, y_ref, z_ref, xo_ref, hn_ref, xs_ref):
        i = pl.program_id(0)
        g = pl.program_id(1)
        uv = _dot(h_ref[...], wi_ref[...], "nn")
        halo = jnp.where(i > 0, _dot(hh_ref[...], wi_ref[...], "nn"), 0.0)
        s = jnp.concatenate([halo, uv], axis=0)
        levels = []
        for k in (1, 2, 4, 8):
            s = s + _shift_rows(s, k)
            levels.append(s[H:])
        t = i * tr + lax.broadcasted_iota(jnp.int32, (tr, 1), 0)
        cnt = jnp.minimum(t + 1, jnp.left_shift(2, g)).astype(F32)
        y = _pool_level(g, levels) / cnt - uv
        yb = y.astype(BF16)
        z = _dot(yb, w_ref[...], "nn")
        y_ref[...] = yb
        z_ref[...] = z.astype(z_ref.dtype)
        x_out = x_ref[...] + z * sc_ref[...]
        xo_ref[...] = x_out
        xs_ref[g] = x_out

        @pl.when(g == n_groups - 1)
        def _():
            xf = jnp.concatenate([xs_ref[k] for k in range(n_groups)], axis=1)
            r = lax.rsqrt(jnp.mean(xf * xf, axis=-1, keepdims=True) + RMS_EPS)
            hn_ref[...] = ((xf * r) * gn_ref[...]).astype(hn_ref.dtype)

    blk = pl.BlockSpec((tr, POOL_DIM), lambda i, g: (i, g))
    row = pl.BlockSpec((tr, D), lambda i, g: (i, 0))
    return _pcall(
        body, name=name, grid=(S // tr, n_groups),
        in_specs=[row, pl.BlockSpec((H, D), lambda i, g: (jnp.maximum(i * (tr // H) - 1, 0), 0)),
                  pl.BlockSpec((D, POOL_DIM), lambda i, g: (0, g)),
                  pl.BlockSpec((None, POOL_DIM, POOL_DIM), lambda i, g: (g, 0, 0)),
                  pl.BlockSpec((1, POOL_DIM), lambda i, g: (0, g)), blk, pl.BlockSpec((1, D), lambda i, g: (0, 0))],
        out_specs=[blk, blk, blk, row],
        out_shape=[jax.ShapeDtypeStruct((S, D), BF16), jax.ShapeDtypeStruct((S, D), BF16),
                   jax.ShapeDtypeStruct((S, D), F32), jax.ShapeDtypeStruct((S, D), BF16)],
        scratch_shapes=[pltpu.VMEM((n_groups, tr, POOL_DIM), F32)],
        compiler_params=_params(("parallel", "arbitrary")))(h, h, w_in, w_group, scale, x_in, gain_next)


def _pool_bwd(name, d_out, z, y, scale, w_group, tr=1024):
    S, D = d_out.shape
    H = POOL_HALO

    def body(d_ref, dn_ref, z_ref, y_ref, sc_ref, w_ref, du_ref, dw_ref, dsc_ref):
        g = pl.program_id(0)
        i = pl.program_id(1)
        dv = d_ref[...]
        dz = jnp.concatenate([dv, dn_ref[...]], axis=0) * sc_ref[...]
        dzb = dz.astype(BF16)
        dy = _dot(dzb, w_ref[...], "nt")
        t = i * tr + lax.broadcasted_iota(jnp.int32, (tr + H, 1), 0)
        cnt = jnp.minimum(t + 1, jnp.left_shift(2, g)).astype(F32)
        s = jnp.where(t < S, dy / cnt, 0.0)
        levels = []
        for k in (1, 2, 4, 8):
            s = s + _shift_rows(s, -k)
            levels.append(s[:tr])
        du_ref[...] = (_pool_level(g, levels) - dy[:tr]).astype(du_ref.dtype)
        dw = _dot(y_ref[...], dzb[:tr], "tn")
        dsc = jnp.sum(dv * z_ref[...].astype(F32), axis=0, keepdims=True)

        @pl.when(i == 0)
        def _():
            dw_ref[...] = dw
            dsc_ref[...] = dsc

        @pl.when(i > 0)
        def _():
            dw_ref[...] += dw
            dsc_ref[...] += dsc

    blk = pl.BlockSpec((tr, POOL_DIM), lambda g, i: (i, g))
    vec = pl.BlockSpec((1, POOL_DIM), lambda g, i: (0, g))
    mat = pl.BlockSpec((None, POOL_DIM, POOL_DIM), lambda g, i: (g, 0, 0))
    nxt = pl.BlockSpec((H, POOL_DIM), lambda g, i: (jnp.minimum((i + 1) * (tr // H), S // H - 1), g))
    return _pcall(
        body, name=name, grid=(D // POOL_DIM, S // tr), in_specs=[blk, nxt, blk, blk, vec, mat],
        out_specs=[blk, mat, vec],
        out_shape=[jax.ShapeDtypeStruct((S, D), BF16), jax.ShapeDtypeStruct((D // POOL_DIM, POOL_DIM, POOL_DIM), F32),
                   jax.ShapeDtypeStruct((1, D), F32)],
        compiler_params=_params(("parallel", "arbitrary")))(d_out, d_out, z, y, scale, w_group)


def _row_tile(rows, cols, target_bytes=1 << 20):
    best = rows
    for tr in range(8, rows + 1, 8):
        if rows % tr == 0 and tr * cols * 4 <= target_bytes:
            best = tr
    return best if best * cols * 4 <= 4 * target_bytes else rows


def _adamw_step(w, g, m, v):
    m2 = ADAM_B1 * m + (1.0 - ADAM_B1) * g
    v2 = ADAM_B2 * v + (1.0 - ADAM_B2) * (g * g)
    m_hat = m2 / (1.0 - ADAM_B1 ** ADAM_STEP)
    v_hat = v2 / (1.0 - ADAM_B2 ** ADAM_STEP)
    return -ADAM_LR * (m_hat / (jnp.sqrt(v_hat) + ADAM_EPS) + ADAM_WD * w), m2, v2


def _adamw_refs(w_ref, g_ref, m_ref, v_ref, go_ref, d_ref, mo_ref, vo_ref):
    gv = g_ref[...]
    d_ref[...], mo_ref[...], vo_ref[...] = _adamw_step(w_ref[...], gv, m_ref[...], v_ref[...])
    go_ref[...] = gv


def _adamw(name, items, steps=16):
    n = len(items)

    def body(*refs):
        for t in range(n):
            _adamw_refs(*refs[4 * t:4 * t + 4], *refs[4 * n + 4 * t:4 * n + 4 * t + 4])

    specs, shapes = [], []
    for w, _, _, _ in items:
        R, C = w.shape
        assert R % (8 * steps) == 0, (name, w.shape)
        specs += [pl.BlockSpec((R // steps, C), lambda i: (i, 0))] * 4
        shapes += [jax.ShapeDtypeStruct((R, C), F32)] * 4
    res = _pcall(body, name=name, grid=(steps,), in_specs=specs, out_specs=specs, out_shape=shapes,
                 compiler_params=_params(("parallel",)))(*[a for item in items for a in item])
    return [res[4 * t:4 * t + 4] for t in range(n)]


def _adamw_small(name, items):
    n = len(items)

    def body(*refs):
        for t in range(n):
            _adamw_refs(*refs[4 * t:4 * t + 4], *refs[4 * n + 4 * t:4 * n + 4 * t + 4])

    specs, shapes = [], []
    for w, _, _, _ in items:
        specs += [pl.BlockSpec(w.shape, lambda i: (0, 0))] * 4
        shapes += [jax.ShapeDtypeStruct(w.shape, F32)] * 4
    res = _pcall(body, name=name, grid=(1,), in_specs=specs, out_specs=specs, out_shape=shapes,
                 compiler_params=_params(("arbitrary",)))(*[a for item in items for a in item])
    return [res[4 * t:4 * t + 4] for t in range(n)]


def _sum_leading(name, a, out_dtype):
    n, R, C = a.shape
    tr = _row_tile(R, C) if R % 16 == 0 else R
    if tr % 16:
        tr = R

    def body(a_ref, o_ref):
        acc = a_ref[0].astype(F32)
        for j in range(1, n):
            acc = acc + a_ref[j].astype(F32)
        o_ref[...] = acc.astype(o_ref.dtype)

    return _pcall(body, name=name, grid=(R // tr,), in_specs=[pl.BlockSpec((n, tr, C), lambda i: (0, i, 0))],
                  out_specs=pl.BlockSpec((tr, C), lambda i: (i, 0)), out_shape=jax.ShapeDtypeStruct((R, C), out_dtype),
                  compiler_params=_params(("parallel",)))(a)


def _cast_many(name, items, chip, steps=4):
    tiles = []
    for w, _, dtype in items:
        R = w.shape[1]
        nt = steps if R % (steps * 16) == 0 else 1
        tiles.append((nt, R // nt))

    def body(c_ref, *refs):
        i = pl.program_id(0)
        for t, (nt, _) in enumerate(tiles):
            w_ref, o_ref = refs[t], refs[len(items) + t]

            def cast(w_ref=w_ref, o_ref=o_ref):
                o_ref[...] = w_ref[...].astype(o_ref.dtype)

            if nt == steps:
                cast()
            else:
                pl.when(i < nt)(cast)

    in_specs, out_specs, out_shape = [], [], []
    for (w, layer, dtype), (nt, tr) in zip(items, tiles):
        C = w.shape[2]
        in_specs.append(pl.BlockSpec((None, tr, C), lambda i, c_ref, l=layer, nt=nt: (l, jnp.minimum(i, nt - 1), 0)))
        out_specs.append(pl.BlockSpec((None, tr, C), lambda i, c_ref, nt=nt: (c_ref[0], jnp.minimum(i, nt - 1), 0)))
        out_shape.append(jax.ShapeDtypeStruct((N_CHIPS, w.shape[1], C), dtype))
    return _pcall(body, prefetch=1, name=name, grid=(steps,), in_specs=in_specs, out_specs=out_specs,
                  out_shape=out_shape, compiler_params=_params(("arbitrary",)))(chip, *[w for w, _, _ in items])


def _cast_qkv(name, w, chip, tr=256):
    _, R, C = w.shape

    def body(c_ref, w_ref, own_ref, *piece_refs):
        v = w_ref[...].astype(BF16)
        own_ref[...] = v
        for p, ref in enumerate(piece_refs):
            ref[...] = v[:, p * QKV_TILE:(p + 1) * QKV_TILE]

    piece = pl.BlockSpec((None, tr, QKV_TILE), lambda i, c_ref: (c_ref[0], i, 0))
    return _pcall(body, prefetch=1, name=name, grid=(R // tr,),
                  in_specs=[pl.BlockSpec((None, tr, C), lambda i, c_ref: (0, i, 0))],
                  out_specs=[pl.BlockSpec((tr, C), lambda i, c_ref: (i, 0))] + [piece] * QKV_PIECES,
                  out_shape=[jax.ShapeDtypeStruct((R, C), BF16)]
                  + [jax.ShapeDtypeStruct((N_CHIPS, R, QKV_TILE), BF16)] * QKV_PIECES,
                  compiler_params=_params(("parallel",)))(chip, w)


def _owner_sum(name, mine, landed, shape, chip, core, out, layer, col):
    _, half, C = mine.shape
    k, _ = col
    tr = _row_tile(half, C)
    if tr % 16:
        tr = half
    nrt = half // tr

    def body(ch_ref, co_ref, mine_ref, landed_ref, *rest):
        g = mine_ref[...].astype(F32)
        for j in range(3):
            g = g + landed_ref[j].astype(F32)
        rest[-1][...] = g

    piece = pl.BlockSpec((None, tr, C), lambda i, ch, co: (layer, co[0] * nrt + i, k))
    in_specs = [pl.BlockSpec((None, tr, C), lambda i, ch, co: (ch[0], i, 0)),
                pl.BlockSpec((3, tr, C), lambda i, ch, co: (0, i, 0))]
    operands = [chip, core, mine, landed]
    aliases = {}
    if out is not None:
        in_specs.append(ANY)
        operands.append(out)
        aliases = {4: 0}
    return _pcall(body, prefetch=2, name=name, grid=(nrt,), in_specs=in_specs, out_specs=piece,
                  out_shape=jax.ShapeDtypeStruct(shape, F32), input_output_aliases=aliases,
                  compiler_params=_params(("parallel",)))(*operands)


def _add_my_half(name, ps, qs, core):
    n = len(ps)

    def body(c_ref, *refs):
        for t in range(n):
            p_ref, q_ref, o_ref = refs[t], refs[n + t], refs[2 * n + t]
            o_ref[...] = (p_ref[...].astype(F32) + q_ref[...].astype(F32)).astype(o_ref.dtype)

    halves = [(p.shape[1] // 2, p.shape[2]) for p in ps]
    mine = [pl.BlockSpec((None, h, c), lambda s, c_ref: (s, c_ref[0], 0)) for h, c in halves]
    whole = [pl.BlockSpec((None, h, c), lambda s, c_ref: (s, 0, 0)) for h, c in halves]
    return _pcall(body, prefetch=1, name=name, grid=(N_CHIPS,), in_specs=mine + whole, out_specs=whole,
                  out_shape=[jax.ShapeDtypeStruct((N_CHIPS, h, c), BF16) for h, c in halves],
                  compiler_params=_params(("parallel",)))(core, *ps, *qs)


ANY = pl.BlockSpec(memory_space=pl.ANY)


def _place():
    x, y, c = lax.axis_index("x"), lax.axis_index("y"), lax.axis_index("c")
    other_chips = [(1 - x, y), (x, 1 - y), (1 - x, 1 - y)]
    return x, y, c, other_chips


def _remote(src, dst, send, recv, k, to):
    return pltpu.make_async_remote_copy(src_ref=src, dst_ref=dst, send_sem=send.at[k], recv_sem=recv.at[k],
                                        device_id=to, device_id_type=MESH)


def _in_place(bufs):
    return dict(operands=bufs, out_shapes=[jax.ShapeDtypeStruct(b.shape, b.dtype) for b in bufs],
                aliases={t: t for t in range(len(bufs))})


def _gather_rider(bufs, jobs):
    def copies(ins, outs, send, recv, base=0):
        x, y, c, chips = _place()
        out = []
        for t, over_ici, rows in jobs:
            ref = outs[t]
            if rows is not None:
                a, b, n = rows
                half = ref.shape[1] // 2
                rows = pl.ds(c * half + a * half // n, (b - a) * half // n)
            for px, py in chips:
                slot = 2 * x + y if over_ici else 2 * px + py
                piece = ref.at[slot] if rows is None else ref.at[slot, rows]
                out.append(_remote(piece, piece, send, recv, base + len(out), (px, py, c) if over_ici else (x, y, 1 - c)))
        return out

    return _Rider(n_copies=3 * len(jobs), copies=copies, **_in_place(bufs))


def _gather_and_pass_rider(buf):
    half = buf.shape[1] // 2

    def pieces(outs):
        x, y, c, chips = _place()
        rows = lambda core: pl.ds(core * half, half)
        mine = outs[0].at[2 * x + y, rows(c)]
        landed = [outs[0].at[2 * px + py, rows(c)] for px, py in chips]
        theirs = [outs[0].at[2 * px + py, rows(1 - c)] for px, py in chips]
        return (x, y, c, chips), mine, landed, theirs

    def start(ins, outs, send, recv, base=0):
        (x, y, c, chips), mine, _, _ = pieces(outs)
        for j, (px, py) in enumerate(chips):
            _remote(mine, mine, send, recv, base + j, (px, py, c)).start()

    def finish(ins, outs, send, recv, base=0):
        (x, y, c, chips), mine, landed, theirs = pieces(outs)
        sibling = (x, y, 1 - c)
        passed = []
        for j, (px, py) in enumerate(chips):
            _remote(landed[j], landed[j], send, recv, base + j, (px, py, c)).wait_recv()
            passed.append(_remote(landed[j], landed[j], send, recv, base + 3 + j, sibling))
            passed[-1].start()
        for j in range(3):
            _remote(theirs[j], theirs[j], send, recv, base + 3 + j, sibling).wait_recv()
        for j, (px, py) in enumerate(chips):
            _remote(mine, mine, send, recv, base + j, (px, py, c)).wait_send()
            passed[j].wait_send()

    return _Rider(n_copies=6, start=start, finish=finish, **_in_place([buf]))


def _swap_halves_rider(parts):
    n = len(parts)

    def copies(ins, outs, send, recv, base=0):
        x, y, c, _ = _place()
        out = []
        for t in range(n):
            half = ins[t].shape[1] // 2
            out.append(_remote(ins[t].at[:, pl.ds((1 - c) * half, half)], outs[t], send, recv, base + t,
                               (x, y, 1 - c)))
        return out

    shapes = [jax.ShapeDtypeStruct((p.shape[0], p.shape[1] // 2, p.shape[2]), p.dtype) for p in parts]
    return _Rider(parts, shapes, {}, n, copies)


def _scatter_rider(sums, landed, relations):
    n = len(sums)
    kept = [t for t in range(n) if landed[t] is not None]

    def copies(ins, outs, send, recv, base=0):
        x, y, c, chips = _place()
        out = []
        for t in range(n):
            for j in relations[t]:
                px, py = chips[j]
                out.append(_remote(ins[t].at[2 * px + py], outs[t].at[j], send, recv, base + len(out), (px, py, c)))
        return out

    shapes = [jax.ShapeDtypeStruct((3,) + s.shape[1:], s.dtype) for s in sums]
    return _Rider(list(sums) + [landed[t] for t in kept], shapes, {n + k: t for k, t in enumerate(kept)},
                  sum(len(r) for r in relations), copies)


def _share_rider(blocks, pieces):
    def copies(ins, outs, send, recv, base=0):
        x, y, c, _ = _place()
        out = []
        for k, (t, l, col) in enumerate(pieces):
            ref = outs[t].at[l]
            r2 = ref.shape[0] // 2
            width = ref.shape[1] // col[1]
            piece = ref.at[pl.ds(c * r2, r2), pl.ds(col[0] * width, width)]
            out.append(_remote(piece, piece, send, recv, base + k, (x, y, 1 - c)))
        return out

    return _Rider(n_copies=len(pieces), copies=copies, **_in_place(blocks))


def _gather_small(name, v):
    def body(v_ref, out_ref, send_sem, recv_sem, local_sem):
        x, y, c, _ = _place()
        me = 4 * x + 2 * y + c
        flips = [(fx, fy, fc) for fx in (0, 1) for fy in (0, 1) for fc in (0, 1)][1:]
        local = pltpu.make_async_copy(v_ref, out_ref.at[me], local_sem)
        local.start()
        copies = []
        for j, (fx, fy, fc) in enumerate(flips):
            peer = (x ^ fx, y ^ fy, c ^ fc)
            copies.append(pltpu.make_async_remote_copy(
                src_ref=v_ref, dst_ref=out_ref.at[me], send_sem=send_sem.at[j], recv_sem=recv_sem.at[j],
                device_id=peer, device_id_type=MESH))
        for cp in copies:
            cp.start()
        for j, (fx, fy, fc) in enumerate(flips):
            peer = (x ^ fx, y ^ fy, c ^ fc)
            pltpu.make_async_remote_copy(
                src_ref=v_ref, dst_ref=out_ref.at[4 * peer[0] + 2 * peer[1] + peer[2]], send_sem=send_sem.at[j],
                recv_sem=recv_sem.at[j], device_id=peer, device_id_type=MESH).wait_recv()
        for cp in copies:
            cp.wait_send()
        local.wait()

    return _pcall(body, name=name, in_specs=[ANY], out_specs=ANY,
                  out_shape=jax.ShapeDtypeStruct((8,) + v.shape, v.dtype),
                  scratch_shapes=[pltpu.SemaphoreType.DMA((7,)), pltpu.SemaphoreType.DMA((7,)),
                                  pltpu.SemaphoreType.DMA])(v)


def _fold(a, dil):
    if dil == 1:
        return a
    S, C = a.shape
    return a.reshape(S // dil, dil, C).transpose(1, 0, 2).reshape(S, C)


def _unfold(a, dil):
    if dil == 1:
        return a
    S, C = a.shape
    return a.reshape(dil, S // dil, C).transpose(1, 0, 2).reshape(S, C)


def _fold_groups(a):
    return jnp.stack([_fold(a[g] if a.ndim == 3 else a, d) for g, d in enumerate(ATTN_DILATIONS)])


def _unfold_groups(a):
    return jnp.stack([_unfold(a[g], d) for g, d in enumerate(ATTN_DILATIONS)])


class _NoComm:
    def __init__(self, weights):
        self.weights = weights
        self.grads = {}
        self.chip = jnp.zeros((1,), jnp.int32)

    def w(self, name):
        return self.weights[name]

    def run(self, fn, name, *args, **kw):
        return fn(name, *args, **kw)

    def grad(self, name, value):
        self.grads[name] = value


def _local_step(xs, tgt, attn_norm, ffn_norm, final_norm, comm):
    run = comm.run
    hf = run(_norm_fold, "fold", xs, attn_norm)
    qkv = run(_qkv_tiles, "qkv_own", None, hf, comm.w("wq_own"), None, comm.chip)
    for p in range(QKV_PIECES):
        qkv = run(_qkv_tiles, "qkv_piece%d" % p, p, hf, comm.w("wq%d" % p), qkv, comm.chip)
    o_f, lse_f = run(_attn_fwd, "attn_fwd", qkv)
    lse_t = _unfold_groups(lse_f)
    merged = run(_merge_fwd, "merge_fwd", o_f, lse_t)
    x1, h1 = run(_proj_res_norm, "attn_out", merged, comm.w("o"), xs, ffn_norm[0:1])
    gu0, act0 = run(_ffn_up, "ffn0_up", h1, comm.w("gu0"))
    x2, h2 = run(_proj_res_norm, "ffn0_down", act0, comm.w("d0"), x1, comm.w("pool_norm"))
    y, z, x3, h3 = run(_pool_fwd, "pool_fwd", h2, comm.w("p"), comm.w("pg"), comm.w("pool_scale"), x2, ffn_norm[1:2])
    gu1, act1 = run(_ffn_up, "ffn1_up", h3, comm.w("gu1"))
    loss, dx4, dx4_b, d_final = run(_proj_res_loss, "ffn1_down", act1, comm.w("d1"), x3, final_norm, tgt)

    dgu1 = run(_ffn_down_bwd, "ffn1_down_bwd", dx4_b, comm.w("d1"), gu1)
    comm.grad("d1", run(_mm_tn, "ffn1_down_dw", act1, dx4_b, FF_SHARD, 2048))
    comm.grad("gu1", run(_ffn_dw_up, "ffn1_up_dw", h3, dgu1))
    dx3, _, d_ffn1 = run(_ffn_dh, "ffn1_up_dh", dgu1, comm.w("gu1"), x3, ffn_norm[1:2], dx4)

    du, dw_pg, d_scale = run(_pool_bwd, "pool_bwd", dx3, z, y, comm.w("pool_scale"), comm.w("pg"))
    comm.grad("pg", dw_pg)
    comm.grad("p", run(_mm_tn, "pool_in_dw", h2, du, D_MODEL, h2.shape[0]))
    dx2, dx2_b, d_pool = run(_dh_norm_bwd, "pool_in_dh", du, comm.w("p"), x2, comm.w("pool_norm"), dx3)

    dgu0 = run(_ffn_down_bwd, "ffn0_down_bwd", dx2_b, comm.w("d0"), gu0)
    comm.grad("d0", run(_mm_tn, "ffn0_down_dw", act0, dx2_b, FF_SHARD, 2048))
    comm.grad("gu0", run(_ffn_dw_up, "ffn0_up_dw", h1, dgu0))
    dx1, dx1_b, d_ffn0 = run(_ffn_dh, "ffn0_up_dh", dgu0, comm.w("gu0"), x1, ffn_norm[0:1], dx2)

    comm.grad("o", run(_mm_tn, "attn_out_dw", merged, dx1_b, D_MODEL, 2048))
    do_f, dl_t = run(_merge_bwd, "merge_bwd", dx1_b, comm.w("o"), o_f, lse_t)
    dqkv = run(_attn_bwd, "attn_bwd", qkv, do_f, lse_f, _fold_groups(dl_t))
    for p in range(QKV_PIECES):
        comm.grad("qkv%d" % p, run(_qkv_dw, "qkv_dw%d" % p, p, hf, dqkv))
    dhf = None
    for g in range(N_GROUPS):
        dhf = run(_qkv_dh, "qkv_dh%d" % g, g, dqkv, [comm.w("wq%d" % p) for p in range(QKV_PIECES)], dhf)
    grad_x, d_attn = run(_rms_bwd_folded, "norm_attn_bwd", xs, attn_norm, dhf, dx1)

    small = dict(attn=d_attn, ffn0=d_ffn0, ffn1=d_ffn1, final=d_final, pool=d_pool, scale=d_scale)
    return loss, grad_x, small


TENSORS = ("qkv", "o", "p", "pg", "gu0", "gu1", "d0", "d1")


def _block_of(name):
    if name[:-1] in ("gu", "d"):
        return name[:-1], int(name[-1]), (0, 1)
    if name[:-1] == "qkv":
        return "qkv", 0, (int(name[-1]), QKV_PIECES)
    return name, 0, (0, 1)


class _MeshComm:
    def __init__(self, bufs, shapes, chip_arr, core_arr, pg_rows):
        self.bufs = dict(bufs)
        self.shapes = shapes
        self.chip, self.chip_arr, self.core_arr, self.pg_rows = chip_arr, chip_arr, core_arr, pg_rows
        self.parts, self.sums, self.blocks, self.landed, self.arrived = {}, {}, {}, {}, {}
        move = lambda ici=(), d2d=(): lambda: self.gather(ici, d2d)
        whole = lambda name: lambda: self.gather_and_pass(name)
        swap = lambda *names: lambda: self.swap(names)
        scatter = lambda *names: lambda: self.scatter(names)
        share = lambda *names: lambda: self.share(names)
        self.plan = {
            "fold": [whole("wq0")],
            "qkv_own": [whole("wq1")],
            "qkv_piece0": [whole("wq2")],
            "qkv_piece1": [move(ici=["o", "gu0[0:1/4]"])],
            "qkv_piece2": [move(ici=["gu0[1:2/4]"], d2d=["o", "gu0[0:1/4]"])],
            "attn_fwd": [move(ici=["gu0[2:4/4]", "d0[0:1/2]"], d2d=["gu0[1:2/4]"])],
            "merge_fwd": [move(ici=["p", "pg"], d2d=["gu0[2:4/4]", "d0[0:1/2]"])],
            "attn_out": [move(ici=["d0[1:2/2]", "pool_norm", "pool_scale"], d2d=["p", "pg"])],
            "ffn0_up": [move(ici=["gu1[0:3/4]"], d2d=["d0[1:2/2]"])],
            "ffn0_down": [move(ici=["gu1[3:4/4]"], d2d=["gu1[0:3/4]"])],
            "pool_fwd": [move(ici=["d1"], d2d=["gu1[3:4/4]"])],
            "ffn1_up": [move(d2d=["d1"])],
            "ffn1_up_dh": [swap("d1", "gu1")],
            "pool_bwd": [scatter("d1{01}")],
            "pool_in_dh": [scatter("d1{2}")],
            "ffn0_down_bwd": [scatter("gu1{01}")],
            "ffn0_down_dw": [scatter("gu1{2}")],
            "ffn0_up_dw": [share("gu1", "d1")],
            "ffn0_up_dh": [swap("pg", "p", "d0", "gu0")],
            "merge_bwd": [swap("o")],
            "attn_bwd": [scatter("gu0", "d0", "o")],
            "qkv_dw0": [share("d0"), scatter("p", "pg")],
            "qkv_dw1": [share("gu0", "o"), swap("qkv0")],
            "qkv_dw2": [share("p", "pg"), scatter("qkv0"), swap("qkv1")],
            "qkv_dh0": [share("qkv0"), scatter("qkv1"), swap("qkv2")],
            "qkv_dh1": [share("qkv1"), scatter("qkv2")],
            "qkv_dh2": [share("qkv2")],
        }

    def gather_and_pass(self, name):
        return _gather_and_pass_rider(self.bufs[name]), lambda res: self.bufs.update({name: res[0]})

    def gather(self, ici, d2d):
        names, jobs = [], []
        for over_ici, specs in ((True, ici), (False, d2d)):
            for spec in specs:
                name, _, rows = spec.partition("[")
                if name not in names:
                    names.append(name)
                rng = None
                if name in TENSORS:
                    ab, _, n = (rows[:-1] or "0:1/1").partition("/")
                    rng = (int(ab.split(":")[0]), int(ab.split(":")[1]), int(n))
                jobs.append((names.index(name), over_ici, rng))
        return _gather_rider([self.bufs[n] for n in names], jobs), lambda res: self.bufs.update(zip(names, res))

    def swap(self, names):
        def done(res):
            sums = _add_my_half("chip_sum_" + "_".join(names), [self.parts[n] for n in names], res, self.core_arr)
            self.sums.update(zip(names, sums))
        return _swap_halves_rider([self.parts[n] for n in names]), done

    def scatter(self, specs):
        names = [s.partition("{")[0] for s in specs]
        relations = [tuple(int(ch) for ch in s.partition("{")[2][:-1]) or (0, 1, 2) for s in specs]

        def done(res):
            for n, rel, landed in zip(names, relations, res):
                self.landed[n] = landed
                self.arrived[n] = self.arrived.get(n, ()) + rel
                if len(self.arrived[n]) < 3:
                    continue
                key, layer, col = _block_of(n)
                self.blocks[key] = _owner_sum("owner_sum_" + n, self.sums[n], landed, self.shapes[key],
                                              self.chip_arr, self.core_arr, self.blocks.get(key), layer, col)
        rider = _scatter_rider([self.sums[n] for n in names], [self.landed.get(n) for n in names], relations)
        return rider, done

    def share(self, names):
        keys, pieces = [], []
        for n in names:
            key, layer, col = _block_of(n)
            if key not in keys:
                keys.append(key)
            pieces.append((keys.index(key), layer, col))
        return _share_rider([self.blocks[k] for k in keys], pieces), lambda res: self.blocks.update(zip(keys, res))

    def run(self, fn, name, *args, **kw):
        made = [make() for make in self.plan.get(name, [])]
        if not made:
            return fn(name, *args, **kw)
        riders = [r for r, _ in made]
        out = _ride(riders[0] if len(riders) == 1 else _riders(*riders), fn, name, *args, **kw)
        for r, done in made:
            done(r.results)
        return out

    def w(self, name):
        b = self.bufs[name]
        if name in ("o", "p", "d0", "d1"):
            return b.reshape(-1, b.shape[-1])
        if name == "pg":
            b = b.reshape(N_CHIPS, 4, self.pg_rows, POOL_DIM).transpose(1, 0, 2, 3)
            return b.reshape(4, POOL_DIM, POOL_DIM)
        if name in ("pool_norm", "pool_scale"):
            return b.reshape(1, -1)
        return b

    def grad(self, name, value):
        if name == "pg":
            value = value.reshape(4, N_CHIPS, self.pg_rows, POOL_DIM).transpose(1, 0, 2, 3)
            value = value.reshape(N_CHIPS, 4 * self.pg_rows, POOL_DIM).astype(BF16)
        elif name in ("o", "p", "d0", "d1"):
            value = value.reshape(N_CHIPS, value.shape[0] // N_CHIPS, value.shape[1])
        self.parts[name] = value


def kernel(x, attn_norm, w_qkv, w_attn_out, pool_norm, w_pool_in, w_pool_group, pool_scale, ffn_norm, w_ffn_gate_up, w_ffn_down, final_norm, loss_target, m_attn_norm, m_w_qkv, m_w_attn_out, m_pool_norm, m_w_pool_in, m_w_pool_group, m_pool_scale, m_ffn_norm, m_w_ffn_gate_up, m_w_ffn_down, m_final_norm, v_attn_norm, v_w_qkv, v_w_attn_out, v_pool_norm, v_w_pool_in, v_w_pool_group, v_pool_scale, v_ffn_norm, v_w_ffn_gate_up, v_w_ffn_down, v_final_norm):
    D = D_MODEL
    chip = 2 * lax.axis_index("x") + lax.axis_index("y")
    core = lax.axis_index("c")
    pg_rows = w_pool_group.shape[2]

    chip_arr = chip.astype(jnp.int32).reshape(1)
    core_arr = core.astype(jnp.int32).reshape(1)

    pieces = _cast_qkv("cast_qkv", w_qkv, chip_arr)
    bufs = dict(zip(["wq_own"] + ["wq%d" % p for p in range(QKV_PIECES)], pieces))
    shards = [(w_attn_out, 0, BF16), (w_pool_in, 0, BF16), (w_pool_group.reshape(1, 4 * pg_rows, POOL_DIM), 0, BF16),
              (w_ffn_gate_up, 0, BF16), (w_ffn_gate_up, 1, BF16), (w_ffn_down, 0, BF16), (w_ffn_down, 1, BF16),
              (pool_norm[None], 0, F32), (pool_scale[None], 0, F32)]
    bufs.update(zip(TENSORS[1:] + ("pool_norm", "pool_scale"), _cast_many("cast_rest", shards, chip_arr)))

    as_block = lambda a: a.reshape((-1,) + a.shape[-2:]) if a.ndim == 3 else a.reshape(1, -1, a.shape[-1])
    owned = dict(qkv=(w_qkv, m_w_qkv, v_w_qkv), o=(w_attn_out, m_w_attn_out, v_w_attn_out),
                 p=(w_pool_in, m_w_pool_in, v_w_pool_in), pg=(w_pool_group, m_w_pool_group, v_w_pool_group),
                 gu=(w_ffn_gate_up, m_w_ffn_gate_up, v_w_ffn_gate_up), d=(w_ffn_down, m_w_ffn_down, v_w_ffn_down))
    comm = _MeshComm(bufs, {k: as_block(wmv[0]).shape for k, wmv in owned.items()}, chip_arr, core_arr, pg_rows)
    loss_part, grad_x, small = _local_step(x[0], loss_target[0], attn_norm, ffn_norm, final_norm.reshape(1, D), comm)

    rows = jnp.concatenate([small["attn"], small["ffn0"], small["ffn1"], small["final"], small["pool"],
                            small["scale"], jnp.pad(loss_part, ((0, 0), (0, D - 1))), jnp.zeros((1, D), F32)], axis=0)
    all_rows = _gather_small("gather_gain_grads", rows)
    tot = _sum_leading("sum_gain_grads", all_rows, F32)
    loss = tot[6, 0]
    g_attn_norm = tot[0:1]
    g_ffn_norm = tot[1:3]
    g_final_norm = tot[3]
    g_pool_norm = lax.dynamic_slice(tot, (4, chip * POOL_DIM), (1, POOL_DIM))
    g_pool_scale = lax.dynamic_slice(tot, (5, chip * POOL_DIM), (1, POOL_DIM))

    as_rows = lambda a: a.reshape(-1, a.shape[-1])
    res = _adamw("adamw_weights", [(as_rows(w), as_rows(comm.blocks[k]), as_rows(m), as_rows(v))
                                   for k, (w, m, v) in owned.items()])
    updated = {k: [a.reshape(owned[k][0].shape) for a in r] for k, r in zip(owned, res)}
    gains = [(attn_norm, g_attn_norm, m_attn_norm, v_attn_norm), (pool_norm, g_pool_norm, m_pool_norm, v_pool_norm),
             (pool_scale, g_pool_scale, m_pool_scale, v_pool_scale), (ffn_norm, g_ffn_norm, m_ffn_norm, v_ffn_norm),
             (final_norm, g_final_norm, m_final_norm, v_final_norm)]
    small_res = _adamw_small("adamw_gains", [tuple(as_rows(a) for a in item) for item in gains])
    small_res = [[a.reshape(item[0].shape) for a in r] for r, item in zip(small_res, gains)]
    results = [small_res[0], updated["qkv"], updated["o"], small_res[1], updated["p"], updated["pg"], small_res[2],
               small_res[3], updated["gu"], updated["d"], small_res[4]]
    grads = [r[0] for r in results]
    deltas = [r[1] for r in results]
    new_m = [r[2] for r in results]
    new_v = [r[3] for r in results]
    return (loss, grad_x[None], *grads, *deltas, *new_m, *new_v)
```

```python
import jax
import jax.numpy as jnp
from jax import lax
from jax.experimental import pallas as pl
from jax.experimental.pallas import tpu as pltpu

F32 = jnp.float32
BF16 = jnp.bfloat16
MESH = pl.DeviceIdType.MESH

D_MODEL = 1024
N_HEADS = 8
HEAD_DIM = 128
Q_BLOCK = 128
ATTN_DILATIONS = (1, 4, 16)
N_GROUPS = 3
D_FF = 2816
N_CHIPS = 4
FF_SHARD = 2 * D_FF // N_CHIPS
QKV_SHARD = 3 * 3 * D_MODEL // N_CHIPS
QKV_TILE = 768
POOL_DIM = 256
POOL_HALO = 16
RMS_EPS = 1e-6
NEG = -1e30
ADAM_LR = 0.001
ADAM_B1 = 0.9
ADAM_B2 = 0.999
ADAM_EPS = 1e-08
ADAM_WD = 0.01
ADAM_STEP = 10
VMEM_LIMIT = 56 * 1024 * 1024

_DN = {
    "nn": (((1,), (0,)), ((), ())),
    "nt": (((1,), (1,)), ((), ())),
    "tn": (((0,), (0,)), ((), ())),
}


class _Rider:
    def __init__(self, operands, out_shapes, aliases, n_copies, copies=None, start=None, finish=None):
        self.operands = list(operands)
        self.out_shapes = list(out_shapes)
        self.aliases = dict(aliases)
        self.n_copies = n_copies
        self.results = None

        def start_copies(ins, outs, send, recv, base=0):
            for cp in copies(ins, outs, send, recv, base):
                cp.start()

        def wait_copies(ins, outs, send, recv, base=0):
            for cp in copies(ins, outs, send, recv, base):
                cp.wait()

        self.start = start or start_copies
        self.finish = finish or wait_copies


def _riders(*riders):
    operands, out_shapes, aliases, offsets = [], [], {}, []
    n = 0
    for r in riders:
        offsets.append((len(operands), len(out_shapes), n))
        aliases.update({len(operands) + i: len(out_shapes) + o for i, o in r.aliases.items()})
        operands += r.operands
        out_shapes += r.out_shapes
        n += r.n_copies

    def each(which):
        def go(ins, outs, send, recv, base=0):
            for r, (i0, o0, s0) in zip(riders, offsets):
                getattr(r, which)(ins[i0:i0 + len(r.operands)], outs[o0:o0 + len(r.out_shapes)], send, recv,
                                  base + s0)
        return go

    both = _Rider(operands, out_shapes, aliases, n, start=each("start"), finish=each("finish"))
    both.parts = (riders, offsets)
    return both


_pending_rider = []


def _ride(rider, fn, *args, **kw):
    _pending_rider.append(rider)
    out = fn(*args, **kw)
    assert not _pending_rider
    if hasattr(rider, "parts"):
        for r, (_, o0, _) in zip(*rider.parts):
            r.results = rider.results[o0:o0 + len(r.out_shapes)]
    return out


def _pcall(body, prefetch=0, **kw):
    def build(body, kw):
        if not prefetch:
            return pl.pallas_call(body, **kw)
        kw = dict(kw)
        grid_spec = pltpu.PrefetchScalarGridSpec(
            num_scalar_prefetch=prefetch, grid=kw.pop("grid"), in_specs=kw.pop("in_specs"),
            out_specs=kw.pop("out_specs"), scratch_shapes=kw.pop("scratch_shapes", []))
        return pl.pallas_call(body, grid_spec=grid_spec, **kw)

    if not _pending_rider:
        return build(body, kw)
    rider = _pending_rider.pop()
    grid = kw.get("grid", ())
    in_specs = list(kw.get("in_specs", []))
    single = not isinstance(kw["out_shape"], (list, tuple))
    out_shape = [kw["out_shape"]] if single else list(kw["out_shape"])
    out_specs = [kw["out_specs"]] if single else list(kw["out_specs"])
    scratch = list(kw.get("scratch_shapes", []))
    n_in, n_out, n_scr = len(in_specs), len(out_shape), len(scratch)
    r_in, r_out = len(rider.operands), len(rider.out_shapes)

    def wrapped(*refs):
        scalars, refs = refs[:prefetch], refs[prefetch:]
        ins, rins = refs[:n_in], refs[n_in:n_in + r_in]
        outs = refs[n_in + r_in:n_in + r_in + n_out]
        routs = refs[n_in + r_in + n_out:n_in + r_in + n_out + r_out]
        scr = refs[n_in + r_in + n_out + r_out:n_in + r_in + n_out + r_out + n_scr]
        send, recv = refs[-2:]
        ids = [pl.program_id(a) for a in range(len(grid))]
        first, last = True, True
        for a, pid in enumerate(ids):
            first = jnp.logical_and(first, pid == 0)
            last = jnp.logical_and(last, pid == grid[a] - 1)
        if grid:
            pl.when(first)(lambda: rider.start(rins, routs, send, recv))
        else:
            rider.start(rins, routs, send, recv)
        body(*scalars, *ins, *outs, *scr)
        if grid:
            pl.when(last)(lambda: rider.finish(rins, routs, send, recv))
        else:
            rider.finish(rins, routs, send, recv)

    any_spec = pl.BlockSpec(memory_space=pl.ANY)
    kw2 = dict(kw)
    kw2.update(
        in_specs=in_specs + [any_spec] * r_in, out_specs=out_specs + [any_spec] * r_out,
        out_shape=out_shape + rider.out_shapes,
        scratch_shapes=scratch + [pltpu.SemaphoreType.DMA((rider.n_copies,)),
                                  pltpu.SemaphoreType.DMA((rider.n_copies,))],
        input_output_aliases={**kw.get("input_output_aliases", {}),
                              **{prefetch + n_in + i: n_out + o for i, o in rider.aliases.items()}})
    if grid:
        kw2["compiler_params"] = _params(("arbitrary",) * len(grid))
    call = build(wrapped, kw2)

    def run(*operands):
        res = call(*operands, *rider.operands)
        rider.results = list(res[n_out:])
        return res[0] if single else list(res[:n_out])

    return run


def _params(sem):
    return pltpu.CompilerParams(dimension_semantics=sem, vmem_limit_bytes=VMEM_LIMIT)


def _dot(a, b, mode):
    return lax.dot_general(a, b, _DN[mode], preferred_element_type=F32)


def _mm(name, mode, grid, a, a_spec, b, b_spec, out_shape, o_spec, acc_shape):
    nk = grid[-1]

    def body(a_ref, b_ref, o_ref, *acc):
        part = _dot(a_ref[...].astype(BF16), b_ref[...].astype(BF16), mode)
        if nk == 1:
            o_ref[...] = part.astype(o_ref.dtype)
            return
        acc_ref, = acc
        k = pl.program_id(len(grid) - 1)

        @pl.when(k == 0)
        def _():
            acc_ref[...] = part

        @pl.when(k > 0)
        def _():
            acc_ref[...] += part

        @pl.when(k == nk - 1)
        def _():
            o_ref[...] = acc_ref[...].astype(o_ref.dtype)

    scratch = [] if nk == 1 else [pltpu.VMEM(acc_shape, F32)]
    sem = ("parallel",) * (len(grid) - 1) + ("arbitrary",)
    return _pcall(body, name=name, grid=grid, in_specs=[a_spec, b_spec], out_specs=o_spec, out_shape=out_shape,
                  scratch_shapes=scratch, compiler_params=_params(sem))(a, b)


def _mm_tn(name, a, b, tm, tk):
    T, M = a.shape
    N = b.shape[1]
    return _mm(name, "tn", (M // tm, T // tk), a, pl.BlockSpec((tk, tm), lambda i, k: (k, i)),
               b, pl.BlockSpec((tk, N), lambda i, k: (k, 0)),
               jax.ShapeDtypeStruct((M, N), BF16), pl.BlockSpec((tm, N), lambda i, k: (i, 0)), (tm, N))


def _norm_bwd_rows(xf, gain, dh, dres):
    r = lax.rsqrt(jnp.mean(xf * xf, axis=-1, keepdims=True) + RMS_EPS)
    xh = xf * r
    t = dh * gain
    dx = dres + r * (t - xh * jnp.mean(t * xh, axis=-1, keepdims=True))
    return dx, jnp.sum(dh * xh, axis=0, keepdims=True)


def _accumulate(ref, value, first):
    @pl.when(first)
    def _():
        ref[...] = value

    @pl.when(jnp.logical_not(first))
    def _():
        ref[...] += value


def _rms_bwd_folded(name, x, g, dhf, dres, tb=512):
    S, D = x.shape

    def body(x_ref, g_ref, d0_ref, d1_ref, d2_ref, dres_ref, dx_ref, dg_ref, dh_ref):
        chunks = _lane_chunks(D)
        for c, cols in enumerate(chunks):
            dh_ref[c] = d0_ref[0, :, cols].astype(F32)
        for dil, ref in ((ATTN_DILATIONS[1], d1_ref), (ATTN_DILATIONS[2], d2_ref)):
            n = tb // dil
            for k in range(dil):
                for c, cols in enumerate(chunks):
                    dh_ref[c, _strided_rows(k, n, dil), :] += ref[k, :, cols].astype(F32)
        dh = jnp.concatenate([dh_ref[c] for c in range(len(chunks))], axis=1)
        dx, dg = _norm_bwd_rows(x_ref[...], g_ref[...], dh, dres_ref[...])
        dx_ref[...] = dx
        _accumulate(dg_ref, dg, pl.program_id(0) == 0)

    views, specs = _folded_views(dhf, tb)
    row = pl.BlockSpec((tb, D), lambda i: (i, 0))
    vec = pl.BlockSpec((1, D), lambda i: (0, 0))
    return _pcall(body, name=name, grid=(S // tb,), in_specs=[row, vec] + specs + [row], out_specs=[row, vec],
                  out_shape=[jax.ShapeDtypeStruct((S, D), F32), jax.ShapeDtypeStruct((1, D), F32)],
                  scratch_shapes=[pltpu.VMEM((D // LANES, tb, LANES), F32)],
                  compiler_params=_params(("arbitrary",)))(x, g, *views, dres)


def _proj_res_norm(name, a, w, res, gain, tm=512):
    M, K = a.shape
    D = w.shape[1]

    def body(a_ref, w_ref, res_ref, g_ref, x_ref, h_ref):
        xf = res_ref[...] + _dot(a_ref[...], w_ref[...], "nn")
        x_ref[...] = xf
        r = lax.rsqrt(jnp.mean(xf * xf, axis=-1, keepdims=True) + RMS_EPS)
        h_ref[...] = ((xf * r) * g_ref[...]).astype(h_ref.dtype)

    row = pl.BlockSpec((tm, D), lambda i: (i, 0))
    return _pcall(body, name=name, grid=(M // tm,),
                  in_specs=[pl.BlockSpec((tm, K), lambda i: (i, 0)), pl.BlockSpec((K, D), lambda i: (0, 0)), row,
                            pl.BlockSpec((1, D), lambda i: (0, 0))],
                  out_specs=[row, row],
                  out_shape=[jax.ShapeDtypeStruct((M, D), F32), jax.ShapeDtypeStruct((M, D), BF16)],
                  compiler_params=_params(("parallel",)))(a, w, res, gain)


def _proj_res_loss(name, a, w, res, gain, tgt, tm=512):
    M, K = a.shape
    D = w.shape[1]

    def body(a_ref, w_ref, res_ref, g_ref, t_ref, loss_ref, dx_ref, dxb_ref, dg_ref):
        first = pl.program_id(0) == 0
        xf = res_ref[...] + _dot(a_ref[...], w_ref[...], "nn")
        r = lax.rsqrt(jnp.mean(xf * xf, axis=-1, keepdims=True) + RMS_EPS)
        xh = xf * r
        gv = g_ref[...]
        e = xh * gv - t_ref[...]
        lp = 0.5 * jnp.sum(jnp.mean(e * e, axis=-1, keepdims=True), axis=0, keepdims=True)
        dy = e * (1.0 / D)
        t = dy * gv
        dx = r * (t - xh * jnp.mean(t * xh, axis=-1, keepdims=True))
        dx_ref[...] = dx
        dxb_ref[...] = dx.astype(dxb_ref.dtype)
        _accumulate(dg_ref, jnp.sum(dy * xh, axis=0, keepdims=True), first)
        _accumulate(loss_ref, lp, first)

    row = pl.BlockSpec((tm, D), lambda i: (i, 0))
    vec = pl.BlockSpec((1, D), lambda i: (0, 0))
    return _pcall(body, name=name, grid=(M // tm,),
                  in_specs=[pl.BlockSpec((tm, K), lambda i: (i, 0)), pl.BlockSpec((K, D), lambda i: (0, 0)), row,
                            vec, row],
                  out_specs=[pl.BlockSpec((1, 1), lambda i: (0, 0)), row, row, vec],
                  out_shape=[jax.ShapeDtypeStruct((1, 1), F32), jax.ShapeDtypeStruct((M, D), F32),
                             jax.ShapeDtypeStruct((M, D), BF16), jax.ShapeDtypeStruct((1, D), F32)],
                  compiler_params=_params(("arbitrary",)))(a, w, res, gain, tgt)


def _dh_norm_bwd(name, d, w, x, gain, dres, tm=512):
    M, N = d.shape
    D = w.shape[0]

    def body(d_ref, w_ref, x_ref, g_ref, dres_ref, dx_ref, dxb_ref, dg_ref):
        dh = _dot(d_ref[...].astype(BF16), w_ref[...], "nt")
        dx, dg = _norm_bwd_rows(x_ref[...], g_ref[...], dh, dres_ref[...])
        dx_ref[...] = dx
        dxb_ref[...] = dx.astype(dxb_ref.dtype)
        _accumulate(dg_ref, dg, pl.program_id(0) == 0)

    row = pl.BlockSpec((tm, D), lambda i: (i, 0))
    vec = pl.BlockSpec((1, D), lambda i: (0, 0))
    return _pcall(body, name=name, grid=(M // tm,),
                  in_specs=[pl.BlockSpec((tm, N), lambda i: (i, 0)), pl.BlockSpec((D, N), lambda i: (0, 0)), row, vec,
                            row],
                  out_specs=[row, row, vec],
                  out_shape=[jax.ShapeDtypeStruct((M, D), F32), jax.ShapeDtypeStruct((M, D), BF16),
                             jax.ShapeDtypeStruct((1, D), F32)],
                  compiler_params=_params(("arbitrary",)))(d, w, x, gain, dres)


def _sigmoid(v):
    return 0.5 * jnp.tanh(0.5 * v) + 0.5


def _ffn_up(name, h, w_gu, tm=1024):
    S, D = h.shape
    W = FF_SHARD

    def body(h_ref, wg_ref, wu_ref, gu_ref, act_ref):
        hv = h_ref[...]
        gate = _dot(hv, wg_ref[...], "nn")
        up = _dot(hv, wu_ref[...], "nn")
        gu_ref[0] = gate.astype(gu_ref.dtype)
        gu_ref[1] = up.astype(gu_ref.dtype)
        sig = _sigmoid(gate)
        act_ref[...] = ((gate * sig) * up).astype(act_ref.dtype)

    return _pcall(
        body, name=name, grid=(2, S // tm),
        in_specs=[pl.BlockSpec((tm, D), lambda j, i: (i, 0)),
                  pl.BlockSpec((None, D, W), lambda j, i: (j, 0, 0)),
                  pl.BlockSpec((None, D, W), lambda j, i: (j + 2, 0, 0))],
        out_specs=[pl.BlockSpec((2, tm, W), lambda j, i: (0, i, j)), pl.BlockSpec((tm, W), lambda j, i: (i, j))],
        out_shape=[jax.ShapeDtypeStruct((2, S, D_FF), BF16), jax.ShapeDtypeStruct((S, D_FF), BF16)],
        compiler_params=_params(("parallel", "parallel")))(h, w_gu, w_gu)


def _ffn_down_bwd(name, dx, w_down, gu, tm=1024):
    S, D = dx.shape
    W = FF_SHARD

    def body(dx_ref, w_ref, gu_ref, dgu_ref):
        dact = _dot(dx_ref[...].astype(BF16), w_ref[...], "nt")
        gate = gu_ref[0].astype(F32)
        up = gu_ref[1].astype(F32)
        sig = _sigmoid(gate)
        silu = gate * sig
        dgu_ref[0] = (dact * up * (sig * (1.0 + gate * (1.0 - sig)))).astype(dgu_ref.dtype)
        dgu_ref[1] = (dact * silu).astype(dgu_ref.dtype)

    blk = pl.BlockSpec((2, tm, W), lambda j, i: (0, i, j))
    return _pcall(
        body, name=name, grid=(2, S // tm),
        in_specs=[pl.BlockSpec((tm, D), lambda j, i: (i, 0)), pl.BlockSpec((W, D), lambda j, i: (j, 0)), blk],
        out_specs=blk, out_shape=jax.ShapeDtypeStruct((2, S, D_FF), BF16),
        compiler_params=_params(("parallel", "parallel")))(dx, w_down, gu)


def _ffn_dw_up(name, h, dgu, tk=2048):
    S, D = h.shape
    W = FF_SHARD
    tk = min(tk, S)
    return _mm(name, "tn", (N_CHIPS, S // tk), h, pl.BlockSpec((tk, D), lambda s, k: (k, 0)),
               dgu, pl.BlockSpec((None, tk, W), lambda s, k: (s // 2, k, s % 2)),
               jax.ShapeDtypeStruct((N_CHIPS, D, W), BF16), pl.BlockSpec((None, D, W), lambda s, k: (s, 0, 0)),
               (D, W))


def _ffn_dh(name, dgu, w_gu, x, gain, dres, tm=512):
    S = dgu.shape[1]
    W = FF_SHARD

    def body(a_ref, w_hbm, x_ref, g_ref, dres_ref, dx_ref, dg_ref, w_ref, sems):
        first = pl.program_id(0) == 0

        @pl.when(first)
        def _():
            copies = [pltpu.make_async_copy(w_hbm.at[s], w_ref.at[:, pl.ds(s * W, W)], sems.at[s])
                      for s in range(N_CHIPS)]
            for cp in copies:
                cp.start()
            for cp in copies:
                cp.wait()

        dh = _dot(a_ref[0], w_ref[:, :D_FF], "nt") + _dot(a_ref[1], w_ref[:, D_FF:], "nt")
        dx, dg = _norm_bwd_rows(x_ref[...], g_ref[...], dh, dres_ref[...])
        dx_ref[...] = dx
        _accumulate(dg_ref, dg, first)

    row = pl.BlockSpec((tm, D_MODEL), lambda i: (i, 0))
    vec = pl.BlockSpec((1, D_MODEL), lambda i: (0, 0))
    return _pcall(body, name=name, grid=(S // tm,),
                  in_specs=[pl.BlockSpec((2, tm, D_FF), lambda i: (0, i, 0)), pl.BlockSpec(memory_space=pl.ANY),
                            row, vec, row],
                  out_specs=[row, vec],
                  out_shape=[jax.ShapeDtypeStruct((S, D_MODEL), F32), jax.ShapeDtypeStruct((1, D_MODEL), F32)],
                  scratch_shapes=[pltpu.VMEM((D_MODEL, 2 * D_FF), BF16), pltpu.SemaphoreType.DMA((N_CHIPS,))],
                  compiler_params=_params(("arbitrary",)))(dgu, w_gu, x, gain, dres)


FOLD_TOKENS = 1024
LANES = 128


def _lane_chunks(width):
    return [slice(c * LANES, (c + 1) * LANES) for c in range(width // LANES)]


def _strided_rows(r, n, dil):
    return pl.ds(r, n) if dil == 1 else pl.ds(r, n, stride=dil)


def _norm_fold(name, x, gain):
    S, D = x.shape
    chunks = _lane_chunks(D)

    def body(x_ref, g_ref, hf_hbm, xc_ref, hs_ref, sems):
        i = pl.program_id(0)
        xf = x_ref[...]
        inv = lax.rsqrt(jnp.mean(xf * xf, axis=-1, keepdims=True) + RMS_EPS)
        h = (xf * inv) * g_ref[...]
        hs_ref[0] = h.astype(BF16)
        for c, cols in enumerate(chunks):
            xc_ref[c] = h[:, cols]
        copies = []
        for g, dil in enumerate(ATTN_DILATIONS):
            n = FOLD_TOKENS // dil
            for r in range(dil):
                if dil > 1:
                    for c, cols in enumerate(chunks):
                        hs_ref[g, r * n:(r + 1) * n, cols] = xc_ref[c, _strided_rows(r, n, dil), :].astype(BF16)
                cp = pltpu.make_async_copy(hs_ref.at[g, pl.ds(r * n, n)],
                                           hf_hbm.at[g, pl.ds(r * (S // dil) + i * n, n)], sems.at[len(copies)])
                cp.start()
                copies.append(cp)
        for cp in copies:
            cp.wait()

    return _pcall(body, name=name, grid=(S // FOLD_TOKENS,),
                  in_specs=[pl.BlockSpec((FOLD_TOKENS, D), lambda i: (i, 0)), pl.BlockSpec((1, D), lambda i: (0, 0))],
                  out_specs=pl.BlockSpec(memory_space=pl.ANY),
                  out_shape=jax.ShapeDtypeStruct((N_GROUPS, S, D), BF16),
                  scratch_shapes=[pltpu.VMEM((D // LANES, FOLD_TOKENS, LANES), F32),
                                  pltpu.VMEM((N_GROUPS, FOLD_TOKENS, D), BF16),
                                  pltpu.SemaphoreType.DMA((sum(ATTN_DILATIONS),))],
                  compiler_params=_params(("arbitrary",)))(x, gain)


def _qkv_tiles(name, piece, hf, w, qkv, chip, tm=1024):
    S = hf.shape[1]
    per_group = 3 * D_MODEL // QKV_TILE

    def tile(q, c_ref):
        if piece is None:
            return QKV_PIECES * c_ref[0] + q
        return QKV_PIECES * ((c_ref[0] + 1 + q) % N_CHIPS) + piece

    def body(*refs):
        a_ref, w_ref, o_ref = refs[1], refs[2], refs[-1]
        o_ref[...] = _dot(a_ref[...], w_ref[...], "nn").astype(o_ref.dtype)

    if piece is None:
        w_spec = pl.BlockSpec((D_MODEL, QKV_TILE), lambda q, i, c_ref: (0, q))
    else:
        w_spec = pl.BlockSpec((None, D_MODEL, QKV_TILE), lambda q, i, c_ref: ((c_ref[0] + 1 + q) % N_CHIPS, 0, 0))
    in_specs = [pl.BlockSpec((None, tm, D_MODEL), lambda q, i, c_ref: (tile(q, c_ref) // per_group, i, 0)), w_spec]
    operands = [chip, hf, w]
    aliases = {}
    if qkv is not None:
        in_specs.append(pl.BlockSpec(memory_space=pl.ANY))
        operands.append(qkv)
        aliases = {3: 0}
    return _pcall(
        body, prefetch=1, name=name, grid=(N_CHIPS - 1, S // tm), in_specs=in_specs,
        out_specs=pl.BlockSpec((None, tm, QKV_TILE),
                               lambda q, i, c_ref: (tile(q, c_ref) // per_group, i, tile(q, c_ref) % per_group)),
        out_shape=jax.ShapeDtypeStruct((N_GROUPS, S, 3 * D_MODEL), BF16), input_output_aliases=aliases,
        compiler_params=_params(("arbitrary", "arbitrary")))(*operands)


QKV_PIECES = QKV_SHARD // QKV_TILE


def _qkv_dw(name, p, hf, dqkv):
    S = hf.shape[1]
    per_group = 3 * D_MODEL // QKV_TILE
    tile = lambda s: QKV_PIECES * s + p
    return _mm(name, "tn", (N_CHIPS, 1),
               hf, pl.BlockSpec((None, S, D_MODEL), lambda s, k: (tile(s) // per_group, 0, 0)),
               dqkv, pl.BlockSpec((None, S, QKV_TILE), lambda s, k: (tile(s) // per_group, 0, tile(s) % per_group)),
               jax.ShapeDtypeStruct((N_CHIPS, D_MODEL, QKV_TILE), BF16),
               pl.BlockSpec((None, D_MODEL, QKV_TILE), lambda s, k: (s, 0, 0)), None)


def _qkv_dh(name, g, dqkv, w_pieces, dhf, tm=1024):
    S = dqkv.shape[1]
    per_group = 3 * D_MODEL // QKV_TILE

    def body(*refs):
        a_ref, w_hbm = refs[0], refs[1:1 + QKV_PIECES]
        o_ref, w_ref, sems = refs[-3:]

        @pl.when(pl.program_id(0) == 0)
        def _():
            copies = []
            for j in range(per_group):
                t = per_group * g + j
                copies.append(pltpu.make_async_copy(w_hbm[t % QKV_PIECES].at[t // QKV_PIECES],
                                                    w_ref.at[:, pl.ds(j * QKV_TILE, QKV_TILE)], sems.at[j]))
            for cp in copies:
                cp.start()
            for cp in copies:
                cp.wait()

        o_ref[...] = _dot(a_ref[...], w_ref[...], "nt").astype(o_ref.dtype)

    any_spec = pl.BlockSpec(memory_space=pl.ANY)
    in_specs = [pl.BlockSpec((None, tm, 3 * D_MODEL), lambda i: (g, i, 0))] + [any_spec] * QKV_PIECES
    operands = [dqkv] + list(w_pieces)
    aliases = {}
    if dhf is not None:
        in_specs.append(any_spec)
        operands.append(dhf)
        aliases = {1 + QKV_PIECES: 0}
    return _pcall(body, name=name, grid=(S // tm,), in_specs=in_specs,
                  out_specs=pl.BlockSpec((None, tm, D_MODEL), lambda i: (g, i, 0)),
                  out_shape=jax.ShapeDtypeStruct((N_GROUPS, S, D_MODEL), BF16),
                  scratch_shapes=[pltpu.VMEM((D_MODEL, 3 * D_MODEL), BF16), pltpu.SemaphoreType.DMA((per_group,))],
                  input_output_aliases=aliases, compiler_params=_params(("arbitrary",)))(*operands)


def _alibi_coef(g, h):
    vals = [-(2.0 ** (-8.0 * (gg * N_HEADS + h + 1) / (N_GROUPS * N_HEADS))) * ATTN_DILATIONS[gg]
            for gg in range(N_GROUPS)]
    return jnp.where(g == 0, vals[0], jnp.where(g == 1, vals[1], vals[2])).astype(F32)


def _blocks_per_segment(g, S):
    return jnp.right_shift(S // Q_BLOCK, 2 * g)


def _band_bias(pen):
    row = lax.broadcasted_iota(jnp.int32, (Q_BLOCK, 2 * Q_BLOCK), 0)
    col = lax.broadcasted_iota(jnp.int32, (Q_BLOCK, 2 * Q_BLOCK), 1)
    dist = Q_BLOCK + row - col
    valid = jnp.logical_and(dist >= 0, dist <= Q_BLOCK)
    base = jnp.where(valid, jnp.where(col < Q_BLOCK, pen, 0.0), NEG).astype(F32)
    return base, dist.astype(F32)


def _skewed(n_units, stages):
    state = {}
    for step in range(n_units + len(stages) - 1):
        for s, stage in enumerate(stages):
            u = step - s
            if 0 <= u < n_units:
                state[u] = stage(u, state.get(u))
                if s == len(stages) - 1:
                    del state[u]


def _attn_fwd(name, qkv, tq=1024):
    S = qkv.shape[1]
    nqb = tq // Q_BLOCK
    scale = HEAD_DIM ** -0.5

    def body(q_ref, k_ref, kp_ref, v_ref, vp_ref, o_ref, lse_ref, kf_ref, vf_ref):
        g = pl.program_id(0)
        i = pl.program_id(1)
        nb = _blocks_per_segment(g, S)
        kf_ref[:Q_BLOCK] = kp_ref[...]
        kf_ref[Q_BLOCK:] = k_ref[...]
        vf_ref[:Q_BLOCK] = vp_ref[...]
        vf_ref[Q_BLOCK:] = v_ref[...]
        coefs = [_alibi_coef(g, h) for h in range(N_HEADS)]
        units = [(b, h) for b in range(nqb) for h in range(N_HEADS)]
        bias = {}

        def scores(u, _):
            b, h = units[u]
            if b not in bias:
                pen = jnp.where((i * nqb + b) % nb != 0, 0.0, NEG).astype(F32)
                bias.clear()
                bias[b] = _band_bias(pen)
            base, dist = bias[b]
            cols = slice(h * HEAD_DIM, (h + 1) * HEAD_DIM)
            q = q_ref[b * Q_BLOCK:(b + 1) * Q_BLOCK, cols]
            kk = kf_ref[b * Q_BLOCK:(b + 2) * Q_BLOCK, cols]
            return _dot(q, kk, "nt") * scale + (coefs[h] * dist + base)

        def softmax(u, s):
            m = jnp.max(s, axis=-1, keepdims=True)
            p = jnp.exp(s - m)
            den = jnp.sum(p, axis=-1, keepdims=True)
            return (p * (1.0 / den)).astype(BF16), m + jnp.log(den)

        def output(u, st):
            b, h = units[u]
            pn, lse = st
            cols = slice(h * HEAD_DIM, (h + 1) * HEAD_DIM)
            rows = slice(b * Q_BLOCK, (b + 1) * Q_BLOCK)
            o_ref[rows, cols] = _dot(pn, vf_ref[b * Q_BLOCK:(b + 2) * Q_BLOCK, cols], "nn").astype(o_ref.dtype)
            lse_ref[rows, h:h + 1] = lse

        _skewed(len(units), [scores, softmax, output])

    main = lambda c: pl.BlockSpec((None, tq, D_MODEL), lambda g, i: (g, i, c))
    prev = lambda c: pl.BlockSpec((None, Q_BLOCK, D_MODEL), lambda g, i: (g, jnp.maximum(i * nqb - 1, 0), c))
    return _pcall(
        body, name=name, grid=(N_GROUPS, S // tq),
        in_specs=[main(0), main(1), prev(1), main(2), prev(2)],
        out_specs=[pl.BlockSpec((None, tq, D_MODEL), lambda g, i: (g, i, 0)),
                   pl.BlockSpec((None, tq, N_HEADS), lambda g, i: (g, i, 0))],
        out_shape=[jax.ShapeDtypeStruct((N_GROUPS, S, D_MODEL), BF16),
                   jax.ShapeDtypeStruct((N_GROUPS, S, N_HEADS), F32)],
        scratch_shapes=[pltpu.VMEM((tq + Q_BLOCK, D_MODEL), BF16), pltpu.VMEM((tq + Q_BLOCK, D_MODEL), BF16)],
        compiler_params=_params(("parallel", "parallel")))(qkv, qkv, qkv, qkv, qkv)


def _attn_bwd(name, qkv, do, lse, dl, tq=1024):
    S = qkv.shape[1]
    nqb = tq // Q_BLOCK
    n_blocks = S // Q_BLOCK
    scale = HEAD_DIM ** -0.5
    KEYS = 2 * Q_BLOCK

    def body(q_ref, k_ref, v_ref, kp_ref, vp_ref, qn_ref, do_ref, don_ref, lse_ref, lsen_ref, dl_ref, dln_ref,
             out_ref, kf_ref, vf_ref, dk_ref, dv_ref):
        g = pl.program_id(0)
        i = pl.program_id(1)
        nb = _blocks_per_segment(g, S)
        kf_ref[:Q_BLOCK] = kp_ref[...]
        kf_ref[Q_BLOCK:] = k_ref[...]
        vf_ref[:Q_BLOCK] = vp_ref[...]
        vf_ref[Q_BLOCK:] = v_ref[...]
        coefs = [_alibi_coef(g, h) for h in range(N_HEADS)]
        units = [(h, b) for h in range(N_HEADS) for b in range(nqb + 1)]
        bias = {}

        def band(b):
            if b not in bias:
                blk = i * nqb + b
                ok = blk % nb != 0
                if b == nqb:
                    ok = jnp.logical_and(ok, blk < n_blocks)
                bias[b] = _band_bias(jnp.where(ok, 0.0, NEG).astype(F32))
            return bias[b]

        def operands(h, b):
            cols = slice(h * HEAD_DIM, (h + 1) * HEAD_DIM)
            if b == nqb:
                keys = slice(tq, tq + Q_BLOCK)
                return (qn_ref[:, cols], don_ref[:, cols], lsen_ref[:, h:h + 1], dln_ref[:, h:h + 1],
                        kf_ref[keys, cols], vf_ref[keys, cols])
            rows = slice(b * Q_BLOCK, (b + 1) * Q_BLOCK)
            keys = slice(b * Q_BLOCK, b * Q_BLOCK + KEYS)
            return (q_ref[rows, cols], do_ref[rows, cols], lse_ref[rows, h:h + 1], dl_ref[rows, h:h + 1],
                    kf_ref[keys, cols], vf_ref[keys, cols])

        def probs(u, _):
            h, b = units[u]
            q, do_b, lse_b, dl_b, kk, vv = operands(h, b)
            base, dist = band(b)
            if b == nqb:
                base, dist = base[:, :Q_BLOCK], dist[:, :Q_BLOCK]
            p = jnp.exp(_dot(q, kk, "nt") * scale + (coefs[h] * dist + base) - lse_b)
            return p, _dot(do_b, vv, "nt")

        def dscores(u, st):
            h, b = units[u]
            p, dp = st
            dl_b = operands(h, b)[3]
            return ((p * (dp - dl_b)) * scale).astype(BF16), p.astype(BF16)

        def grads(u, st):
            h, b = units[u]
            ds, pb = st
            q, do_b, _, _, kk, _ = operands(h, b)
            cols = slice(h * HEAD_DIM, (h + 1) * HEAD_DIM)
            if b < nqb:
                out_ref[b * Q_BLOCK:(b + 1) * Q_BLOCK, cols] = _dot(ds, kk, "nn").astype(out_ref.dtype)
            if b == 0:
                dk_ref[:Q_BLOCK] = _dot(ds[:, Q_BLOCK:], q, "tn")
                dv_ref[:Q_BLOCK] = _dot(pb[:, Q_BLOCK:], do_b, "tn")
                return None
            dk = _dot(ds, q, "tn")
            dv = _dot(pb, do_b, "tn")
            before = slice((b - 1) * Q_BLOCK, b * Q_BLOCK)
            dk_ref[before] += dk[:Q_BLOCK]
            dv_ref[before] += dv[:Q_BLOCK]
            if b < nqb:
                own = slice(b * Q_BLOCK, (b + 1) * Q_BLOCK)
                dk_ref[own] = dk[Q_BLOCK:]
                dv_ref[own] = dv[Q_BLOCK:]
            else:
                out_ref[:, D_MODEL + h * HEAD_DIM:D_MODEL + (h + 1) * HEAD_DIM] = dk_ref[...].astype(out_ref.dtype)
                out_ref[:, 2 * D_MODEL + h * HEAD_DIM:2 * D_MODEL + (h + 1) * HEAD_DIM] = (
                    dv_ref[...].astype(out_ref.dtype))
            return None

        _skewed(len(units), [probs, dscores, grads])

    nxt_blk = lambda i: jnp.minimum((i + 1) * nqb, n_blocks - 1)
    main = lambda c: pl.BlockSpec((None, tq, D_MODEL), lambda g, i: (g, i, c))
    prev = lambda c: pl.BlockSpec((None, Q_BLOCK, D_MODEL), lambda g, i: (g, jnp.maximum(i * nqb - 1, 0), c))
    row = pl.BlockSpec((None, tq, D_MODEL), lambda g, i: (g, i, 0))
    row_n = pl.BlockSpec((None, Q_BLOCK, D_MODEL), lambda g, i: (g, nxt_blk(i), 0))
    col = pl.BlockSpec((None, tq, N_HEADS), lambda g, i: (g, i, 0))
    col_n = pl.BlockSpec((None, Q_BLOCK, N_HEADS), lambda g, i: (g, nxt_blk(i), 0))
    return _pcall(
        body, name=name, grid=(N_GROUPS, S // tq),
        in_specs=[main(0), main(1), main(2), prev(1), prev(2), row_n, row, row_n, col, col_n, col, col_n],
        out_specs=pl.BlockSpec((None, tq, 3 * D_MODEL), lambda g, i: (g, i, 0)),
        out_shape=jax.ShapeDtypeStruct((N_GROUPS, S, 3 * D_MODEL), BF16),
        scratch_shapes=[pltpu.VMEM((tq + Q_BLOCK, D_MODEL), BF16), pltpu.VMEM((tq + Q_BLOCK, D_MODEL), BF16),
                        pltpu.VMEM((tq, HEAD_DIM), F32), pltpu.VMEM((tq, HEAD_DIM), F32)],
        compiler_params=_params(("parallel", "parallel")))(qkv, qkv, qkv, qkv, qkv, qkv, do, do, lse, lse, dl, dl)


def _group_weights(lse_ref):
    l0, l1, l2 = lse_ref[0], lse_ref[1], lse_ref[2]
    m = jnp.maximum(jnp.maximum(l0, l1), l2)
    e = [jnp.exp(l0 - m), jnp.exp(l1 - m), jnp.exp(l2 - m)]
    den = e[0] + e[1] + e[2]
    return [ei / den for ei in e]


MERGE_TOKENS = 512


def _folded_views(a, tb):
    _, S, C = a.shape
    views, specs = [], []
    for g, dil in enumerate(ATTN_DILATIONS):
        views.append(a.reshape(N_GROUPS * dil, S // dil, C))
        specs.append(pl.BlockSpec((dil, tb // dil, C), lambda i, g=g: (g, i, 0)))
    return views, specs


def _unfold_into(dst_ref, folded_refs):
    for g, (dil, ref) in enumerate(zip(ATTN_DILATIONS, folded_refs)):
        n = ref.shape[1]
        for r in range(dil):
            for c, cols in enumerate(_lane_chunks(ref.shape[2])):
                dst_ref[g, c, _strided_rows(r, n, dil), :] = ref[r, :, cols].astype(F32)


def _merge_fwd(name, o, lse, tb=MERGE_TOKENS):
    S = o.shape[1]

    def body(o0_ref, o1_ref, o2_ref, lse_ref, out_ref, o_ref):
        _unfold_into(o_ref, (o0_ref, o1_ref, o2_ref))
        w = _group_weights(lse_ref)
        for h in range(N_HEADS):
            cols = slice(h * HEAD_DIM, (h + 1) * HEAD_DIM)
            acc = w[0][:, h:h + 1] * o_ref[0, h]
            for gg in range(1, N_GROUPS):
                acc = acc + w[gg][:, h:h + 1] * o_ref[gg, h]
            out_ref[:, cols] = acc.astype(out_ref.dtype)

    views, specs = _folded_views(o, tb)
    return _pcall(body, name=name, grid=(S // tb,),
                  in_specs=specs + [pl.BlockSpec((N_GROUPS, tb, N_HEADS), lambda i: (0, i, 0))],
                  out_specs=pl.BlockSpec((tb, D_MODEL), lambda i: (i, 0)),
                  out_shape=jax.ShapeDtypeStruct((S, D_MODEL), BF16),
                  scratch_shapes=[pltpu.VMEM((N_GROUPS, N_HEADS, tb, HEAD_DIM), F32)],
                  compiler_params=_params(("parallel",)))(*views, lse)


def _merge_bwd(name, dx, w_out, o, lse, tb=MERGE_TOKENS):
    S = o.shape[1]
    n_chunks = sum(ATTN_DILATIONS)

    def body(dx_ref, w_ref, o0_ref, o1_ref, o2_ref, lse_ref, do_hbm, dl_ref, o_ref, dt_ref, df_ref, d_ref, sems):
        i = pl.program_id(0)
        slot = i % 2

        def folded_copies(slot, step):
            out = []
            for gg, dil in enumerate(ATTN_DILATIONS):
                n = tb // dil
                for r in range(dil):
                    out.append(pltpu.make_async_copy(df_ref.at[slot, gg, pl.ds(r * n, n)],
                                                     do_hbm.at[gg, pl.ds(r * (S // dil) + step * n, n)],
                                                     sems.at[slot, len(out)]))
            return out

        @pl.when(i >= 2)
        def _():
            for cp in folded_copies(slot, i - 2):
                cp.wait()

        d_ref[...] = _dot(dx_ref[...].astype(BF16), w_ref[...], "nt")
        _unfold_into(o_ref, (o0_ref, o1_ref, o2_ref))
        w = _group_weights(lse_ref)
        for h in range(N_HEADS):
            cols = slice(h * HEAD_DIM, (h + 1) * HEAD_DIM)
            dv = d_ref[:, cols]
            merged = w[0][:, h:h + 1] * o_ref[0, h]
            for gg in range(1, N_GROUPS):
                merged = merged + w[gg][:, h:h + 1] * o_ref[gg, h]
            dsum = jnp.sum(dv * merged, axis=-1, keepdims=True)
            for gg in range(N_GROUPS):
                wg = w[gg][:, h:h + 1]
                dt_ref[gg, h] = wg * dv
                dl_ref[gg, :, h:h + 1] = wg * dsum
        copies = folded_copies(slot, i)
        started = 0
        for gg, dil in enumerate(ATTN_DILATIONS):
            n = tb // dil
            for r in range(dil):
                for h, cols in enumerate(_lane_chunks(D_MODEL)):
                    df_ref[slot, gg, r * n:(r + 1) * n, cols] = (
                        dt_ref[gg, h, _strided_rows(r, n, dil), :].astype(df_ref.dtype))
                copies[started].start()
                started += 1

        @pl.when(i == pl.num_programs(0) - 1)
        def _():
            for cp in folded_copies(1 - slot, i - 1) + copies:
                cp.wait()

    assert S // tb >= 2
    views, specs = _folded_views(o, tb)
    small = pl.BlockSpec((N_GROUPS, tb, N_HEADS), lambda i: (0, i, 0))
    return _pcall(body, name=name, grid=(S // tb,),
                  in_specs=[pl.BlockSpec((tb, D_MODEL), lambda i: (i, 0)),
                            pl.BlockSpec((D_MODEL, D_MODEL), lambda i: (0, 0))] + specs + [small],
                  out_specs=[pl.BlockSpec(memory_space=pl.ANY), small],
                  out_shape=[jax.ShapeDtypeStruct((N_GROUPS, S, D_MODEL), BF16),
                             jax.ShapeDtypeStruct((N_GROUPS, S, N_HEADS), F32)],
                  scratch_shapes=[pltpu.VMEM((N_GROUPS, N_HEADS, tb, HEAD_DIM), F32),
                                  pltpu.VMEM((N_GROUPS, N_HEADS, tb, HEAD_DIM), F32),
                                  pltpu.VMEM((2, N_GROUPS, tb, D_MODEL), BF16), pltpu.VMEM((tb, D_MODEL), F32),
                                  pltpu.SemaphoreType.DMA((2, n_chunks))],
                  compiler_params=_params(("arbitrary",)))(dx, w_out, *views, lse)


def _shift_rows(a, k):
    return pltpu.roll(a, k % a.shape[0], axis=0)


def _pool_level(g, levels):
    return jnp.where(g == 0, levels[0], jnp.where(g == 1, levels[1], jnp.where(g == 2, levels[2], levels[3])))


def _pool_fwd(name, h, w_in, w_group, scale, x_in, gain_next, tr=1024):
    S, D = h.shape
    H = POOL_HALO
    n_groups = D // POOL_DIM

    def body(h_ref, hh_ref, wi_ref, w_ref, sc_ref, x_ref, gn_ref, y_ref, z_ref, xo_ref, hn_ref, xs_ref):
        i = pl.program_id(0)
        g = pl.program_id(1)
        uv = _dot(h_ref[...], wi_ref[...], "nn")
        halo = jnp.where(i > 0, _dot(hh_ref[...], wi_ref[...], "nn"), 0.0)
        s = jnp.concatenate([halo, uv], axis=0)
        levels = []
        for k in (1, 2, 4, 8):
            s = s + _shift_rows(s, k)
            levels.append(s[H:])
        t = i * tr + lax.broadcasted_iota(jnp.int32, (tr, 1), 0)
        cnt = jnp.minimum(t + 1, jnp.left_shift(2, g)).astype(F32)
        y = _pool_level(g, levels) / cnt - uv
        yb = y.astype(BF16)
        z = _dot(yb, w_ref[...], "nn")
        y_ref[...] = yb
        z_ref[...] = z.astype(z_ref.dtype)
        x_out = x_ref[...] + z * sc_ref[...]
        xo_ref[...] = x_out
        xs_ref[g] = x_out

        @pl.when(g == n_groups - 1)
        def _():
            xf = jnp.concatenate([xs_ref[k] for k in range(n_groups)], axis=1)
            r = lax.rsqrt(jnp.mean(xf * xf, axis=-1, keepdims=True) + RMS_EPS)
            hn_ref[...] = ((xf * r) * gn_ref[...]).astype(hn_ref.dtype)

    blk = pl.BlockSpec((tr, POOL_DIM), lambda i, g: (i, g))
    row = pl.BlockSpec((tr, D), lambda i, g: (i, 0))
    return _pcall(
        body, name=name, grid=(S // tr, n_groups),
        in_specs=[row, pl.BlockSpec((H, D), lambda i, g: (jnp.maximum(i * (tr // H) - 1, 0), 0)),
                  pl.BlockSpec((D, POOL_DIM), lambda i, g: (0, g)),
                  pl.BlockSpec((None, POOL_DIM, POOL_DIM), lambda i, g: (g, 0, 0)),
                  pl.BlockSpec((1, POOL_DIM), lambda i, g: (0, g)), blk, pl.BlockSpec((1, D), lambda i, g: (0, 0))],
        out_specs=[blk, blk, blk, row],
        out_shape=[jax.ShapeDtypeStruct((S, D), BF16), jax.ShapeDtypeStruct((S, D), BF16),
                   jax.ShapeDtypeStruct((S, D), F32), jax.ShapeDtypeStruct((S, D), BF16)],
        scratch_shapes=[pltpu.VMEM((n_groups, tr, POOL_DIM), F32)],
        compiler_params=_params(("parallel", "arbitrary")))(h, h, w_in, w_group, scale, x_in, gain_next)


def _pool_bwd(name, d_out, z, y, scale, w_group, tr=1024):
    S, D = d_out.shape
    H = POOL_HALO

    def body(d_ref, dn_ref, z_ref, y_ref, sc_ref, w_ref, du_ref, dw_ref, dsc_ref):
        g = pl.program_id(0)
        i = pl.program_id(1)
        dv = d_ref[...]
        dz = jnp.concatenate([dv, dn_ref[...]], axis=0) * sc_ref[...]
        dzb = dz.astype(BF16)
        dy = _dot(dzb, w_ref[...], "nt")
        t = i * tr + lax.broadcasted_iota(jnp.int32, (tr + H, 1), 0)
        cnt = jnp.minimum(t + 1, jnp.left_shift(2, g)).astype(F32)
        s = jnp.where(t < S, dy / cnt, 0.0)
        levels = []
        for k in (1, 2, 4, 8):
            s = s + _shift_rows(s, -k)
            levels.append(s[:tr])
        du_ref[...] = (_pool_level(g, levels) - dy[:tr]).astype(du_ref.dtype)
        dw = _dot(y_ref[...], dzb[:tr], "tn")
        dsc = jnp.sum(dv * z_ref[...].astype(F32), axis=0, keepdims=True)

        @pl.when(i == 0)
        def _():
            dw_ref[...] = dw
            dsc_ref[...] = dsc

        @pl.when(i > 0)
        def _():
            dw_ref[...] += dw
            dsc_ref[...] += dsc

    blk = pl.BlockSpec((tr, POOL_DIM), lambda g, i: (i, g))
    vec = pl.BlockSpec((1, POOL_DIM), lambda g, i: (0, g))
    mat = pl.BlockSpec((None, POOL_DIM, POOL_DIM), lambda g, i: (g, 0, 0))
    nxt = pl.BlockSpec((H, POOL_DIM), lambda g, i: (jnp.minimum((i + 1) * (tr // H), S // H - 1), g))
    return _pcall(
        body, name=name, grid=(D // POOL_DIM, S // tr), in_specs=[blk, nxt, blk, blk, vec, mat],
        out_specs=[blk, mat, vec],
        out_shape=[jax.ShapeDtypeStruct((S, D), BF16), jax.ShapeDtypeStruct((D // POOL_DIM, POOL_DIM, POOL_DIM), F32),
                   jax.ShapeDtypeStruct((1, D), F32)],
        compiler_params=_params(("parallel", "arbitrary")))(d_out, d_out, z, y, scale, w_group)


def _row_tile(rows, cols, target_bytes=1 << 20):
    best = rows
    for tr in range(8, rows + 1, 8):
        if rows % tr == 0 and tr * cols * 4 <= target_bytes:
            best = tr
    return best if best * cols * 4 <= 4 * target_bytes else rows


def _adamw_step(w, g, m, v):
    m2 = ADAM_B1 * m + (1.0 - ADAM_B1) * g
    v2 = ADAM_B2 * v + (1.0 - ADAM_B2) * (g * g)
    m_hat = m2 / (1.0 - ADAM_B1 ** ADAM_STEP)
    v_hat = v2 / (1.0 - ADAM_B2 ** ADAM_STEP)
    return -ADAM_LR * (m_hat / (jnp.sqrt(v_hat) + ADAM_EPS) + ADAM_WD * w), m2, v2


def _adamw_refs(w_ref, g_ref, m_ref, v_ref, go_ref, d_ref, mo_ref, vo_ref):
    gv = g_ref[...]
    d_ref[...], mo_ref[...], vo_ref[...] = _adamw_step(w_ref[...], gv, m_ref[...], v_ref[...])
    go_ref[...] = gv


def _adamw(name, items, steps=16):
    n = len(items)

    def body(*refs):
        for t in range(n):
            _adamw_refs(*refs[4 * t:4 * t + 4], *refs[4 * n + 4 * t:4 * n + 4 * t + 4])

    specs, shapes = [], []
    for w, _, _, _ in items:
        R, C = w.shape
        assert R % (8 * steps) == 0, (name, w.shape)
        specs += [pl.BlockSpec((R // steps, C), lambda i: (i, 0))] * 4
        shapes += [jax.ShapeDtypeStruct((R, C), F32)] * 4
    res = _pcall(body, name=name, grid=(steps,), in_specs=specs, out_specs=specs, out_shape=shapes,
                 compiler_params=_params(("parallel",)))(*[a for item in items for a in item])
    return [res[4 * t:4 * t + 4] for t in range(n)]


def _adamw_small(name, items):
    n = len(items)

    def body(*refs):
        for t in range(n):
            _adamw_refs(*refs[4 * t:4 * t + 4], *refs[4 * n + 4 * t:4 * n + 4 * t + 4])

    specs, shapes = [], []
    for w, _, _, _ in items:
        specs += [pl.BlockSpec(w.shape, lambda i: (0, 0))] * 4
        shapes += [jax.ShapeDtypeStruct(w.shape, F32)] * 4
    res = _pcall(body, name=name, grid=(1,), in_specs=specs, out_specs=specs, out_shape=shapes,
                 compiler_params=_params(("arbitrary",)))(*[a for item in items for a in item])
    return [res[4 * t:4 * t + 4] for t in range(n)]


def _sum_leading(name, a, out_dtype):
    n, R, C = a.shape
    tr = _row_tile(R, C) if R % 16 == 0 else R
    if tr % 16:
        tr = R

    def body(a_ref, o_ref):
        acc = a_ref[0].astype(F32)
        for j in range(1, n):
            acc = acc + a_ref[j].astype(F32)
        o_ref[...] = acc.astype(o_ref.dtype)

    return _pcall(body, name=name, grid=(R // tr,), in_specs=[pl.BlockSpec((n, tr, C), lambda i: (0, i, 0))],
                  out_specs=pl.BlockSpec((tr, C), lambda i: (i, 0)), out_shape=jax.ShapeDtypeStruct((R, C), out_dtype),
                  compiler_params=_params(("parallel",)))(a)


def _cast_many(name, items, chip, steps=4):
    tiles = []
    for w, _, dtype in items:
        R = w.shape[1]
        nt = steps if R % (steps * 16) == 0 else 1
        tiles.append((nt, R // nt))

    def body(c_ref, *refs):
        i = pl.program_id(0)
        for t, (nt, _) in enumerate(tiles):
            w_ref, o_ref = refs[t], refs[len(items) + t]

            def cast(w_ref=w_ref, o_ref=o_ref):
                o_ref[...] = w_ref[...].astype(o_ref.dtype)

            if nt == steps:
                cast()
            else:
                pl.when(i < nt)(cast)

    in_specs, out_specs, out_shape = [], [], []
    for (w, layer, dtype), (nt, tr) in zip(items, tiles):
        C = w.shape[2]
        in_specs.append(pl.BlockSpec((None, tr, C), lambda i, c_ref, l=layer, nt=nt: (l, jnp.minimum(i, nt - 1), 0)))
        out_specs.append(pl.BlockSpec((None, tr, C), lambda i, c_ref, nt=nt: (c_ref[0], jnp.minimum(i, nt - 1), 0)))
        out_shape.append(jax.ShapeDtypeStruct((N_CHIPS, w.shape[1], C), dtype))
    return _pcall(body, prefetch=1, name=name, grid=(steps,), in_specs=in_specs, out_specs=out_specs,
                  out_shape=out_shape, compiler_params=_params(("arbitrary",)))(chip, *[w for w, _, _ in items])


def _cast_qkv(name, w, chip, tr=256):
    _, R, C = w.shape

    def body(c_ref, w_ref, own_ref, *piece_refs):
        v = w_ref[...].astype(BF16)
        own_ref[...] = v
        for p, ref in enumerate(piece_refs):
            ref[...] = v[:, p * QKV_TILE:(p + 1) * QKV_TILE]

    piece = pl.BlockSpec((None, tr, QKV_TILE), lambda i, c_ref: (c_ref[0], i, 0))
    return _pcall(body, prefetch=1, name=name, grid=(R // tr,),
                  in_specs=[pl.BlockSpec((None, tr, C), lambda i, c_ref: (0, i, 0))],
                  out_specs=[pl.BlockSpec((tr, C), lambda i, c_ref: (i, 0))] + [piece] * QKV_PIECES,
                  out_shape=[jax.ShapeDtypeStruct((R, C), BF16)]
                  + [jax.ShapeDtypeStruct((N_CHIPS, R, QKV_TILE), BF16)] * QKV_PIECES,
                  compiler_params=_params(("parallel",)))(chip, w)


def _owner_sum(name, mine, landed, shape, chip, core, out, layer, col):
    _, half, C = mine.shape
    k, _ = col
    tr = _row_tile(half, C)
    if tr % 16:
        tr = half
    nrt = half // tr

    def body(ch_ref, co_ref, mine_ref, landed_ref, *rest):
        g = mine_ref[...].astype(F32)
        for j in range(3):
            g = g + landed_ref[j].astype(F32)
        rest[-1][...] = g

    piece = pl.BlockSpec((None, tr, C), lambda i, ch, co: (layer, co[0] * nrt + i, k))
    in_specs = [pl.BlockSpec((None, tr, C), lambda i, ch, co: (ch[0], i, 0)),
                pl.BlockSpec((3, tr, C), lambda i, ch, co: (0, i, 0))]
    operands = [chip, core, mine, landed]
    aliases = {}
    if out is not None:
        in_specs.append(ANY)
        operands.append(out)
        aliases = {4: 0}
    return _pcall(body, prefetch=2, name=name, grid=(nrt,), in_specs=in_specs, out_specs=piece,
                  out_shape=jax.ShapeDtypeStruct(shape, F32), input_output_aliases=aliases,
                  compiler_params=_params(("parallel",)))(*operands)


def _add_my_half(name, ps, qs, core):
    n = len(ps)

    def body(c_ref, *refs):
        for t in range(n):
            p_ref, q_ref, o_ref = refs[t], refs[n + t], refs[2 * n + t]
            o_ref[...] = (p_ref[...].astype(F32) + q_ref[...].astype(F32)).astype(o_ref.dtype)

    halves = [(p.shape[1] // 2, p.shape[2]) for p in ps]
    mine = [pl.BlockSpec((None, h, c), lambda s, c_ref: (s, c_ref[0], 0)) for h, c in halves]
    whole = [pl.BlockSpec((None, h, c), lambda s, c_ref: (s, 0, 0)) for h, c in halves]
    return _pcall(body, prefetch=1, name=name, grid=(N_CHIPS,), in_specs=mine + whole, out_specs=whole,
                  out_shape=[jax.ShapeDtypeStruct((N_CHIPS, h, c), BF16) for h, c in halves],
                  compiler_params=_params(("parallel",)))(core, *ps, *qs)


ANY = pl.BlockSpec(memory_space=pl.ANY)


def _place():
    x, y, c = lax.axis_index("x"), lax.axis_index("y"), lax.axis_index("c")
    other_chips = [(1 - x, y), (x, 1 - y), (1 - x, 1 - y)]
    return x, y, c, other_chips


def _remote(src, dst, send, recv, k, to):
    return pltpu.make_async_remote_copy(src_ref=src, dst_ref=dst, send_sem=send.at[k], recv_sem=recv.at[k],
                                        device_id=to, device_id_type=MESH)


def _in_place(bufs):
    return dict(operands=bufs, out_shapes=[jax.ShapeDtypeStruct(b.shape, b.dtype) for b in bufs],
                aliases={t: t for t in range(len(bufs))})


def _gather_rider(bufs, jobs):
    def copies(ins, outs, send, recv, base=0):
        x, y, c, chips = _place()
        out = []
        for t, over_ici, rows in jobs:
            ref = outs[t]
            if rows is not None:
                a, b, n = rows
                half = ref.shape[1] // 2
                rows = pl.ds(c * half + a * half // n, (b - a) * half // n)
            for px, py in chips:
                slot = 2 * x + y if over_ici else 2 * px + py
                piece = ref.at[slot] if rows is None else ref.at[slot, rows]
                out.append(_remote(piece, piece, send, recv, base + len(out), (px, py, c) if over_ici else (x, y, 1 - c)))
        return out

    return _Rider(n_copies=3 * len(jobs), copies=copies, **_in_place(bufs))


def _gather_and_pass_rider(buf):
    half = buf.shape[1] // 2

    def pieces(outs):
        x, y, c, chips = _place()
        rows = lambda core: pl.ds(core * half, half)
        mine = outs[0].at[2 * x + y, rows(c)]
        landed = [outs[0].at[2 * px + py, rows(c)] for px, py in chips]
        theirs = [outs[0].at[2 * px + py, rows(1 - c)] for px, py in chips]
        return (x, y, c, chips), mine, landed, theirs

    def start(ins, outs, send, recv, base=0):
        (x, y, c, chips), mine, _, _ = pieces(outs)
        for j, (px, py) in enumerate(chips):
            _remote(mine, mine, send, recv, base + j, (px, py, c)).start()

    def finish(ins, outs, send, recv, base=0):
        (x, y, c, chips), mine, landed, theirs = pieces(outs)
        sibling = (x, y, 1 - c)
        passed = []
        for j, (px, py) in enumerate(chips):
            _remote(landed[j], landed[j], send, recv, base + j, (px, py, c)).wait_recv()
            passed.append(_remote(landed[j], landed[j], send, recv, base + 3 + j, sibling))
            passed[-1].start()
        for j in range(3):
            _remote(theirs[j], theirs[j], send, recv, base + 3 + j, sibling).wait_recv()
        for j, (px, py) in enumerate(chips):
            _remote(mine, mine, send, recv, base + j, (px, py, c)).wait_send()
            passed[j].wait_send()

    return _Rider(n_copies=6, start=start, finish=finish, **_in_place([buf]))


def _swap_halves_rider(parts):
    n = len(parts)

    def copies(ins, outs, send, recv, base=0):
        x, y, c, _ = _place()
        out = []
        for t in range(n):
            half = ins[t].shape[1] // 2
            out.append(_remote(ins[t].at[:, pl.ds((1 - c) * half, half)], outs[t], send, recv, base + t,
                               (x, y, 1 - c)))
        return out

    shapes = [jax.ShapeDtypeStruct((p.shape[0], p.shape[1] // 2, p.shape[2]), p.dtype) for p in parts]
    return _Rider(parts, shapes, {}, n, copies)


def _scatter_rider(sums, landed, relations):
    n = len(sums)
    kept = [t for t in range(n) if landed[t] is not None]

    def copies(ins, outs, send, recv, base=0):
        x, y, c, chips = _place()
        out = []
        for t in range(n):
            for j in relations[t]:
                px, py = chips[j]
                out.append(_remote(ins[t].at[2 * px + py], outs[t].at[j], send, recv, base + len(out), (px, py, c)))
        return out

    shapes = [jax.ShapeDtypeStruct((3,) + s.shape[1:], s.dtype) for s in sums]
    return _Rider(list(sums) + [landed[t] for t in kept], shapes, {n + k: t for k, t in enumerate(kept)},
                  sum(len(r) for r in relations), copies)


def _share_rider(blocks, pieces):
    def copies(ins, outs, send, recv, base=0):
        x, y, c, _ = _place()
        out = []
        for k, (t, l, col) in enumerate(pieces):
            ref = outs[t].at[l]
            r2 = ref.shape[0] // 2
            width = ref.shape[1] // col[1]
            piece = ref.at[pl.ds(c * r2, r2), pl.ds(col[0] * width, width)]
            out.append(_remote(piece, piece, send, recv, base + k, (x, y, 1 - c)))
        return out

    return _Rider(n_copies=len(pieces), copies=copies, **_in_place(blocks))


def _gather_small(name, v):
    def body(v_ref, out_ref, send_sem, recv_sem, local_sem):
        x, y, c, _ = _place()
        me = 4 * x + 2 * y + c
        flips = [(fx, fy, fc) for fx in (0, 1) for fy in (0, 1) for fc in (0, 1)][1:]
        local = pltpu.make_async_copy(v_ref, out_ref.at[me], local_sem)
        local.start()
        copies = []
        for j, (fx, fy, fc) in enumerate(flips):
            peer = (x ^ fx, y ^ fy, c ^ fc)
            copies.append(pltpu.make_async_remote_copy(
                src_ref=v_ref, dst_ref=out_ref.at[me], send_sem=send_sem.at[j], recv_sem=recv_sem.at[j],
                device_id=peer, device_id_type=MESH))
        for cp in copies:
            cp.start()
        for j, (fx, fy, fc) in enumerate(flips):
            peer = (x ^ fx, y ^ fy, c ^ fc)
            pltpu.make_async_remote_copy(
                src_ref=v_ref, dst_ref=out_ref.at[4 * peer[0] + 2 * peer[1] + peer[2]], send_sem=send_sem.at[j],
                recv_sem=recv_sem.at[j], device_id=peer, device_id_type=MESH).wait_recv()
        for cp in copies:
            cp.wait_send()
        local.wait()

    return _pcall(body, name=name, in_specs=[ANY], out_specs=ANY,
                  out_shape=jax.ShapeDtypeStruct((8,) + v.shape, v.dtype),
                  scratch_shapes=[pltpu.SemaphoreType.DMA((7,)), pltpu.SemaphoreType.DMA((7,)),
                                  pltpu.SemaphoreType.DMA])(v)


def _fold(a, dil):
    if dil == 1:
        return a
    S, C = a.shape
    return a.reshape(S // dil, dil, C).transpose(1, 0, 2).reshape(S, C)


def _unfold(a, dil):
    if dil == 1:
        return a
    S, C = a.shape
    return a.reshape(dil, S // dil, C).transpose(1, 0, 2).reshape(S, C)


def _fold_groups(a):
    return jnp.stack([_fold(a[g] if a.ndim == 3 else a, d) for g, d in enumerate(ATTN_DILATIONS)])


def _unfold_groups(a):
    return jnp.stack([_unfold(a[g], d) for g, d in enumerate(ATTN_DILATIONS)])


class _NoComm:
    def __init__(self, weights):
        self.weights = weights
        self.grads = {}
        self.chip = jnp.zeros((1,), jnp.int32)

    def w(self, name):
        return self.weights[name]

    def run(self, fn, name, *args, **kw):
        return fn(name, *args, **kw)

    def grad(self, name, value):
        self.grads[name] = value


def _local_step(xs, tgt, attn_norm, ffn_norm, final_norm, comm):
    run = comm.run
    hf = run(_norm_fold, "fold", xs, attn_norm)
    qkv = run(_qkv_tiles, "qkv_own", None, hf, comm.w("wq_own"), None, comm.chip)
    for p in range(QKV_PIECES):
        qkv = run(_qkv_tiles, "qkv_piece%d" % p, p, hf, comm.w("wq%d" % p), qkv, comm.chip)
    o_f, lse_f = run(_attn_fwd, "attn_fwd", qkv)
    lse_t = _unfold_groups(lse_f)
    merged = run(_merge_fwd, "merge_fwd", o_f, lse_t)
    x1, h1 = run(_proj_res_norm, "attn_out", merged, comm.w("o"), xs, ffn_norm[0:1])
    gu0, act0 = run(_ffn_up, "ffn0_up", h1, comm.w("gu0"))
    x2, h2 = run(_proj_res_norm, "ffn0_down", act0, comm.w("d0"), x1, comm.w("pool_norm"))
    y, z, x3, h3 = run(_pool_fwd, "pool_fwd", h2, comm.w("p"), comm.w("pg"), comm.w("pool_scale"), x2, ffn_norm[1:2])
    gu1, act1 = run(_ffn_up, "ffn1_up", h3, comm.w("gu1"))
    loss, dx4, dx4_b, d_final = run(_proj_res_loss, "ffn1_down", act1, comm.w("d1"), x3, final_norm, tgt)

    dgu1 = run(_ffn_down_bwd, "ffn1_down_bwd", dx4_b, comm.w("d1"), gu1)
    comm.grad("d1", run(_mm_tn, "ffn1_down_dw", act1, dx4_b, FF_SHARD, 2048))
    comm.grad("gu1", run(_ffn_dw_up, "ffn1_up_dw", h3, dgu1))
    dx3, d_ffn1 = run(_ffn_dh, "ffn1_up_dh", dgu1, comm.w("gu1"), x3, ffn_norm[1:2], dx4)

    du, dw_pg, d_scale = run(_pool_bwd, "pool_bwd", dx3, z, y, comm.w("pool_scale"), comm.w("pg"))
    comm.grad("pg", dw_pg)
    comm.grad("p", run(_mm_tn, "pool_in_dw", h2, du, D_MODEL, h2.shape[0]))
    dx2, dx2_b, d_pool = run(_dh_norm_bwd, "pool_in_dh", du, comm.w("p"), x2, comm.w("pool_norm"), dx3)

    dgu0 = run(_ffn_down_bwd, "ffn0_down_bwd", dx2_b, comm.w("d0"), gu0)
    comm.grad("d0", run(_mm_tn, "ffn0_down_dw", act0, dx2_b, FF_SHARD, 2048))
    comm.grad("gu0", run(_ffn_dw_up, "ffn0_up_dw", h1, dgu0))
    dx1, d_ffn0 = run(_ffn_dh, "ffn0_up_dh", dgu0, comm.w("gu0"), x1, ffn_norm[0:1], dx2)

    comm.grad("o", run(_mm_tn, "attn_out_dw", merged, dx1, D_MODEL, 2048))
    do_f, dl_t = run(_merge_bwd, "merge_bwd", dx1, comm.w("o"), o_f, lse_t)
    dqkv = run(_attn_bwd, "attn_bwd", qkv, do_f, lse_f, _fold_groups(dl_t))
    for p in range(QKV_PIECES):
        comm.grad("qkv%d" % p, run(_qkv_dw, "qkv_dw%d" % p, p, hf, dqkv))
    dhf = None
    for g in range(N_GROUPS):
        dhf = run(_qkv_dh, "qkv_dh%d" % g, g, dqkv, [comm.w("wq%d" % p) for p in range(QKV_PIECES)], dhf)
    grad_x, d_attn = run(_rms_bwd_folded, "norm_attn_bwd", xs, attn_norm, dhf, dx1)

    small = dict(attn=d_attn, ffn0=d_ffn0, ffn1=d_ffn1, final=d_final, pool=d_pool, scale=d_scale)
    return loss, grad_x, small


TENSORS = ("qkv", "o", "p", "pg", "gu0", "gu1", "d0", "d1")


def _block_of(name):
    if name[:-1] in ("gu", "d"):
        return name[:-1], int(name[-1]), (0, 1)
    if name[:-1] == "qkv":
        return "qkv", 0, (int(name[-1]), QKV_PIECES)
    return name, 0, (0, 1)


class _MeshComm:
    def __init__(self, bufs, shapes, chip_arr, core_arr, pg_rows):
        self.bufs = dict(bufs)
        self.shapes = shapes
        self.chip, self.chip_arr, self.core_arr, self.pg_rows = chip_arr, chip_arr, core_arr, pg_rows
        self.parts, self.sums, self.blocks, self.landed, self.arrived = {}, {}, {}, {}, {}
        move = lambda ici=(), d2d=(): lambda: self.gather(ici, d2d)
        whole = lambda name: lambda: self.gather_and_pass(name)
        swap = lambda *names: lambda: self.swap(names)
        scatter = lambda *names: lambda: self.scatter(names)
        share = lambda *names: lambda: self.share(names)
        self.plan = {
            "fold": [whole("wq0")],
            "qkv_own": [whole("wq1")],
            "qkv_piece0": [whole("wq2")],
            "qkv_piece1": [move(ici=["o", "gu0[0:1/4]"])],
            "qkv_piece2": [move(ici=["gu0[1:2/4]"], d2d=["o", "gu0[0:1/4]"])],
            "attn_fwd": [move(ici=["gu0[2:4/4]", "d0[0:1/2]"], d2d=["gu0[1:2/4]"])],
            "merge_fwd": [move(ici=["p", "pg"], d2d=["gu0[2:4/4]", "d0[0:1/2]"])],
            "attn_out": [move(ici=["d0[1:2/2]", "pool_norm", "pool_scale"], d2d=["p", "pg"])],
            "ffn0_up": [move(ici=["gu1[0:3/4]"], d2d=["d0[1:2/2]"])],
            "ffn0_down": [move(ici=["gu1[3:4/4]"], d2d=["gu1[0:3/4]"])],
            "pool_fwd": [move(ici=["d1"], d2d=["gu1[3:4/4]"])],
            "ffn1_up": [move(d2d=["d1"])],
            "ffn1_up_dh": [swap("d1", "gu1")],
            "pool_bwd": [scatter("d1{01}")],
            "pool_in_dh": [scatter("d1{2}")],
            "ffn0_down_bwd": [scatter("gu1{01}")],
            "ffn0_down_dw": [scatter("gu1{2}")],
            "ffn0_up_dw": [share("gu1", "d1")],
            "ffn0_up_dh": [swap("pg", "p", "d0", "gu0")],
            "merge_bwd": [swap("o")],
            "attn_bwd": [scatter("gu0", "d0", "o")],
            "qkv_dw0": [share("d0"), scatter("p", "pg")],
            "qkv_dw1": [share("gu0", "o"), swap("qkv0")],
            "qkv_dw2": [share("p", "pg"), scatter("qkv0"), swap("qkv1")],
            "qkv_dh0": [share("qkv0"), scatter("qkv1"), swap("qkv2")],
            "qkv_dh1": [share("qkv1"), scatter("qkv2")],
            "qkv_dh2": [share("qkv2")],
        }

    def gather_and_pass(self, name):
        return _gather_and_pass_rider(self.bufs[name]), lambda res: self.bufs.update({name: res[0]})

    def gather(self, ici, d2d):
        names, jobs = [], []
        for over_ici, specs in ((True, ici), (False, d2d)):
            for spec in specs:
                name, _, rows = spec.partition("[")
                if name not in names:
                    names.append(name)
                rng = None
                if name in TENSORS:
                    ab, _, n = (rows[:-1] or "0:1/1").partition("/")
                    rng = (int(ab.split(":")[0]), int(ab.split(":")[1]), int(n))
                jobs.append((names.index(name), over_ici, rng))
        return _gather_rider([self.bufs[n] for n in names], jobs), lambda res: self.bufs.update(zip(names, res))

    def swap(self, names):
        def done(res):
            sums = _add_my_half("chip_sum_" + "_".join(names), [self.parts[n] for n in names], res, self.core_arr)
            self.sums.update(zip(names, sums))
        return _swap_halves_rider([self.parts[n] for n in names]), done

    def scatter(self, specs):
        names = [s.partition("{")[0] for s in specs]
        relations = [tuple(int(ch) for ch in s.partition("{")[2][:-1]) or (0, 1, 2) for s in specs]

        def done(res):
            for n, rel, landed in zip(names, relations, res):
                self.landed[n] = landed
                self.arrived[n] = self.arrived.get(n, ()) + rel
                if len(self.arrived[n]) < 3:
                    continue
                key, layer, col = _block_of(n)
                self.blocks[key] = _owner_sum("owner_sum_" + n, self.sums[n], landed, self.shapes[key],
                                              self.chip_arr, self.core_arr, self.blocks.get(key), layer, col)
        rider = _scatter_rider([self.sums[n] for n in names], [self.landed.get(n) for n in names], relations)
        return rider, done

    def share(self, names):
        keys, pieces = [], []
        for n in names:
            key, layer, col = _block_of(n)
            if key not in keys:
                keys.append(key)
            pieces.append((keys.index(key), layer, col))
        return _share_rider([self.blocks[k] for k in keys], pieces), lambda res: self.blocks.update(zip(keys, res))

    def run(self, fn, name, *args, **kw):
        made = [make() for make in self.plan.get(name, [])]
        if not made:
            return fn(name, *args, **kw)
        riders = [r for r, _ in made]
        out = _ride(riders[0] if len(riders) == 1 else _riders(*riders), fn, name, *args, **kw)
        for r, done in made:
            done(r.results)
        return out

    def w(self, name):
        b = self.bufs[name]
        if name in ("o", "p", "d0", "d1"):
            return b.reshape(-1, b.shape[-1])
        if name == "pg":
            b = b.reshape(N_CHIPS, 4, self.pg_rows, POOL_DIM).transpose(1, 0, 2, 3)
            return b.reshape(4, POOL_DIM, POOL_DIM)
        if name in ("pool_norm", "pool_scale"):
            return b.reshape(1, -1)
        return b

    def grad(self, name, value):
        if name == "pg":
            value = value.reshape(4, N_CHIPS, self.pg_rows, POOL_DIM).transpose(1, 0, 2, 3)
            value = value.reshape(N_CHIPS, 4 * self.pg_rows, POOL_DIM).astype(BF16)
        elif name in ("o", "p", "d0", "d1"):
            value = value.reshape(N_CHIPS, value.shape[0] // N_CHIPS, value.shape[1])
        self.parts[name] = value


def kernel(x, attn_norm, w_qkv, w_attn_out, pool_norm, w_pool_in, w_pool_group, pool_scale, ffn_norm, w_ffn_gate_up, w_ffn_down, final_norm, loss_target, m_attn_norm, m_w_qkv, m_w_attn_out, m_pool_norm, m_w_pool_in, m_w_pool_group, m_pool_scale, m_ffn_norm, m_w_ffn_gate_up, m_w_ffn_down, m_final_norm, v_attn_norm, v_w_qkv, v_w_attn_out, v_pool_norm, v_w_pool_in, v_w_pool_group, v_pool_scale, v_ffn_norm, v_w_ffn_gate_up, v_w_ffn_down, v_final_norm):
    D = D_MODEL
    chip = 2 * lax.axis_index("x") + lax.axis_index("y")
    core = lax.axis_index("c")
    pg_rows = w_pool_group.shape[2]

    chip_arr = chip.astype(jnp.int32).reshape(1)
    core_arr = core.astype(jnp.int32).reshape(1)

    pieces = _cast_qkv("cast_qkv", w_qkv, chip_arr)
    bufs = dict(zip(["wq_own"] + ["wq%d" % p for p in range(QKV_PIECES)], pieces))
    shards = [(w_attn_out, 0, BF16), (w_pool_in, 0, BF16), (w_pool_group.reshape(1, 4 * pg_rows, POOL_DIM), 0, BF16),
              (w_ffn_gate_up, 0, BF16), (w_ffn_gate_up, 1, BF16), (w_ffn_down, 0, BF16), (w_ffn_down, 1, BF16),
              (pool_norm[None], 0, F32), (pool_scale[None], 0, F32)]
    bufs.update(zip(TENSORS[1:] + ("pool_norm", "pool_scale"), _cast_many("cast_rest", shards, chip_arr)))

    as_block = lambda a: a.reshape((-1,) + a.shape[-2:]) if a.ndim == 3 else a.reshape(1, -1, a.shape[-1])
    owned = dict(qkv=(w_qkv, m_w_qkv, v_w_qkv), o=(w_attn_out, m_w_attn_out, v_w_attn_out),
                 p=(w_pool_in, m_w_pool_in, v_w_pool_in), pg=(w_pool_group, m_w_pool_group, v_w_pool_group),
                 gu=(w_ffn_gate_up, m_w_ffn_gate_up, v_w_ffn_gate_up), d=(w_ffn_down, m_w_ffn_down, v_w_ffn_down))
    comm = _MeshComm(bufs, {k: as_block(wmv[0]).shape for k, wmv in owned.items()}, chip_arr, core_arr, pg_rows)
    loss_part, grad_x, small = _local_step(x[0], loss_target[0], attn_norm, ffn_norm, final_norm.reshape(1, D), comm)

    rows = jnp.concatenate([small["attn"], small["ffn0"], small["ffn1"], small["final"], small["pool"],
                            small["scale"], jnp.pad(loss_part, ((0, 0), (0, D - 1))), jnp.zeros((1, D), F32)], axis=0)
    all_rows = _gather_small("gather_gain_grads", rows)
    tot = _sum_leading("sum_gain_grads", all_rows, F32)
    loss = tot[6, 0]
    g_attn_norm = tot[0:1]
    g_ffn_norm = tot[1:3]
    g_final_norm = tot[3]
    g_pool_norm = lax.dynamic_slice(tot, (4, chip * POOL_DIM), (1, POOL_DIM))
    g_pool_scale = lax.dynamic_slice(tot, (5, chip * POOL_DIM), (1, POOL_DIM))

    as_rows = lambda a: a.reshape(-1, a.shape[-1])
    res = _adamw("adamw_weights", [(as_rows(w), as_rows(comm.blocks[k]), as_rows(m), as_rows(v))
                                   for k, (w, m, v) in owned.items()])
    updated = {k: [a.reshape(owned[k][0].shape) for a in r] for k, r in zip(owned, res)}
    gains = [(attn_norm, g_attn_norm, m_attn_norm, v_attn_norm), (pool_norm, g_pool_norm, m_pool_norm, v_pool_norm),
             (pool_scale, g_pool_scale, m_pool_scale, v_pool_scale), (ffn_norm, g_ffn_norm, m_ffn_norm, v_ffn_norm),
             (final_norm, g_final_norm, m_final_norm, v_final_norm)]
    small_res = _adamw_small("adamw_gains", [tuple(as_rows(a) for a in item) for item in gains])
    small_res = [[a.reshape(item[0].shape) for a in r] for r, item in zip(small_res, gains)]
    results = [small_res[0], updated["qkv"], updated["o"], small_res[1], updated["p"], updated["pg"], small_res[2],
               small_res[3], updated["gu"], updated["d"], small_res[4]]
    grads = [r[0] for r in results]
    deltas = [r[1] for r in results]
    new_m = [r[2] for r in results]
    new_v = [r[3] for r in results]
    return (loss, grad_x[None], *grads, *deltas, *new_m, *new_v)
```

```python
import jax
import jax.numpy as jnp
from jax import lax
from jax.experimental import pallas as pl
from jax.experimental.pallas import tpu as pltpu

F32 = jnp.float32
BF16 = jnp.bfloat16
MESH = pl.DeviceIdType.MESH

D_MODEL = 1024
N_HEADS = 8
HEAD_DIM = 128
Q_BLOCK = 128
ATTN_DILATIONS = (1, 4, 16)
N_GROUPS = 3
D_FF = 2816
N_CHIPS = 4
FF_SHARD = 2 * D_FF // N_CHIPS
QKV_SHARD = 3 * 3 * D_MODEL // N_CHIPS
QKV_TILE = 768
POOL_DIM = 256
POOL_HALO = 16
RMS_EPS = 1e-6
NEG = -1e30
ADAM_LR = 0.001
ADAM_B1 = 0.9
ADAM_B2 = 0.999
ADAM_EPS = 1e-08
ADAM_WD = 0.01
ADAM_STEP = 10
VMEM_LIMIT = 56 * 1024 * 1024

_DN = {
    "nn": (((1,), (0,)), ((), ())),
    "nt": (((1,), (1,)), ((), ())),
    "tn": (((0,), (0,)), ((), ())),
}


class _Rider:
    def __init__(self, operands, out_shapes, aliases, n_copies, copies=None, start=None, finish=None):
        self.operands = list(operands)
        self.out_shapes = list(out_shapes)
        self.aliases = dict(aliases)
        self.n_copies = n_copies
        self.results = None

        def start_copies(ins, outs, send, recv, base=0):
            for cp in copies(ins, outs, send, recv, base):
                cp.start()

        def wait_copies(ins, outs, send, recv, base=0):
            for cp in copies(ins, outs, send, recv, base):
                cp.wait()

        self.start = start or start_copies
        self.finish = finish or wait_copies


def _riders(*riders):
    operands, out_shapes, aliases, offsets = [], [], {}, []
    n = 0
    for r in riders:
        offsets.append((len(operands), len(out_shapes), n))
        aliases.update({len(operands) + i: len(out_shapes) + o for i, o in r.aliases.items()})
        operands += r.operands
        out_shapes += r.out_shapes
        n += r.n_copies

    def each(which):
        def go(ins, outs, send, recv, base=0):
            for r, (i0, o0, s0) in zip(riders, offsets):
                getattr(r, which)(ins[i0:i0 + len(r.operands)], outs[o0:o0 + len(r.out_shapes)], send, recv,
                                  base + s0)
        return go

    both = _Rider(operands, out_shapes, aliases, n, start=each("start"), finish=each("finish"))
    both.parts = (riders, offsets)
    return both


_pending_rider = []


def _ride(rider, fn, *args, **kw):
    _pending_rider.append(rider)
    out = fn(*args, **kw)
    assert not _pending_rider
    if hasattr(rider, "parts"):
        for r, (_, o0, _) in zip(*rider.parts):
            r.results = rider.results[o0:o0 + len(r.out_shapes)]
    return out


def _pcall(body, prefetch=0, **kw):
    def build(body, kw):
        if not prefetch:
            return pl.pallas_call(body, **kw)
        kw = dict(kw)
        grid_spec = pltpu.PrefetchScalarGridSpec(
            num_scalar_prefetch=prefetch, grid=kw.pop("grid"), in_specs=kw.pop("in_specs"),
            out_specs=kw.pop("out_specs"), scratch_shapes=kw.pop("scratch_shapes", []))
        return pl.pallas_call(body, grid_spec=grid_spec, **kw)

    if not _pending_rider:
        return build(body, kw)
    rider = _pending_rider.pop()
    grid = kw.get("grid", ())
    in_specs = list(kw.get("in_specs", []))
    single = not isinstance(kw["out_shape"], (list, tuple))
    out_shape = [kw["out_shape"]] if single else list(kw["out_shape"])
    out_specs = [kw["out_specs"]] if single else list(kw["out_specs"])
    scratch = list(kw.get("scratch_shapes", []))
    n_in, n_out, n_scr = len(in_specs), len(out_shape), len(scratch)
    r_in, r_out = len(rider.operands), len(rider.out_shapes)

    def wrapped(*refs):
        scalars, refs = refs[:prefetch], refs[prefetch:]
        ins, rins = refs[:n_in], refs[n_in:n_in + r_in]
        outs = refs[n_in + r_in:n_in + r_in + n_out]
        routs = refs[n_in + r_in + n_out:n_in + r_in + n_out + r_out]
        scr = refs[n_in + r_in + n_out + r_out:n_in + r_in + n_out + r_out + n_scr]
        send, recv = refs[-2:]
        ids = [pl.program_id(a) for a in range(len(grid))]
        first, last = True, True
        for a, pid in enumerate(ids):
            first = jnp.logical_and(first, pid == 0)
            last = jnp.logical_and(last, pid == grid[a] - 1)
        if grid:
            pl.when(first)(lambda: rider.start(rins, routs, send, recv))
        else:
            rider.start(rins, routs, send, recv)
        body(*scalars, *ins, *outs, *scr)
        if grid:
            pl.when(last)(lambda: rider.finish(rins, routs, send, recv))
        else:
            rider.finish(rins, routs, send, recv)

    any_spec = pl.BlockSpec(memory_space=pl.ANY)
    kw2 = dict(kw)
    kw2.update(
        in_specs=in_specs + [any_spec] * r_in, out_specs=out_specs + [any_spec] * r_out,
        out_shape=out_shape + rider.out_shapes,
        scratch_shapes=scratch + [pltpu.SemaphoreType.DMA((rider.n_copies,)),
                                  pltpu.SemaphoreType.DMA((rider.n_copies,))],
        input_output_aliases={**kw.get("input_output_aliases", {}),
                              **{prefetch + n_in + i: n_out + o for i, o in rider.aliases.items()}})
    if grid:
        kw2["compiler_params"] = _params(("arbitrary",) * len(grid))
    call = build(wrapped, kw2)

    def run(*operands):
        res = call(*operands, *rider.operands)
        rider.results = list(res[n_out:])
        return res[0] if single else list(res[:n_out])

    return run


def _params(sem):
    return pltpu.CompilerParams(dimension_semantics=sem, vmem_limit_bytes=VMEM_LIMIT)


def _dot(a, b, mode):
    return lax.dot_general(a, b, _DN[mode], preferred_element_type=F32)


def _mm(name, mode, grid, a, a_spec, b, b_spec, out_shape, o_spec, acc_shape):
    nk = grid[-1]

    def body(a_ref, b_ref, o_ref, *acc):
        part = _dot(a_ref[...].astype(BF16), b_ref[...].astype(BF16), mode)
        if nk == 1:
            o_ref[...] = part.astype(o_ref.dtype)
            return
        acc_ref, = acc
        k = pl.program_id(len(grid) - 1)

        @pl.when(k == 0)
        def _():
            acc_ref[...] = part

        @pl.when(k > 0)
        def _():
            acc_ref[...] += part

        @pl.when(k == nk - 1)
        def _():
            o_ref[...] = acc_ref[...].astype(o_ref.dtype)

    scratch = [] if nk == 1 else [pltpu.VMEM(acc_shape, F32)]
    sem = ("parallel",) * (len(grid) - 1) + ("arbitrary",)
    return _pcall(body, name=name, grid=grid, in_specs=[a_spec, b_spec], out_specs=o_spec, out_shape=out_shape,
                  scratch_shapes=scratch, compiler_params=_params(sem))(a, b)


def _mm_tn(name, a, b, tm, tk):
    T, M = a.shape
    N = b.shape[1]
    return _mm(name, "tn", (M // tm, T // tk), a, pl.BlockSpec((tk, tm), lambda i, k: (k, i)),
               b, pl.BlockSpec((tk, N), lambda i, k: (k, 0)),
               jax.ShapeDtypeStruct((M, N), BF16), pl.BlockSpec((tm, N), lambda i, k: (i, 0)), (tm, N))


def _norm_bwd_rows(xf, gain, dh, dres):
    r = lax.rsqrt(jnp.mean(xf * xf, axis=-1, keepdims=True) + RMS_EPS)
    xh = xf * r
    t = dh * gain
    dx = dres + r * (t - xh * jnp.mean(t * xh, axis=-1, keepdims=True))
    return dx, jnp.sum(dh * xh, axis=0, keepdims=True)


def _accumulate(ref, value, first):
    @pl.when(first)
    def _():
        ref[...] = value

    @pl.when(jnp.logical_not(first))
    def _():
        ref[...] += value


def _rms_bwd_folded(name, x, g, dhf, dres, tb=512):
    S, D = x.shape

    def body(x_ref, g_ref, d0_ref, d1_ref, d2_ref, dres_ref, dx_ref, dg_ref, dh_ref):
        chunks = _lane_chunks(D)
        for c, cols in enumerate(chunks):
            dh_ref[c] = d0_ref[0, :, cols].astype(F32)
        for dil, ref in ((ATTN_DILATIONS[1], d1_ref), (ATTN_DILATIONS[2], d2_ref)):
            n = tb // dil
            for k in range(dil):
                for c, cols in enumerate(chunks):
                    dh_ref[c, _strided_rows(k, n, dil), :] += ref[k, :, cols].astype(F32)
        dh = jnp.concatenate([dh_ref[c] for c in range(len(chunks))], axis=1)
        dx, dg = _norm_bwd_rows(x_ref[...], g_ref[...], dh, dres_ref[...])
        dx_ref[...] = dx
        _accumulate(dg_ref, dg, pl.program_id(0) == 0)

    views, specs = _folded_views(dhf, tb)
    row = pl.BlockSpec((tb, D), lambda i: (i, 0))
    vec = pl.BlockSpec((1, D), lambda i: (0, 0))
    return _pcall(body, name=name, grid=(S // tb,), in_specs=[row, vec] + specs + [row], out_specs=[row, vec],
                  out_shape=[jax.ShapeDtypeStruct((S, D), F32), jax.ShapeDtypeStruct((1, D), F32)],
                  scratch_shapes=[pltpu.VMEM((D // LANES, tb, LANES), F32)],
                  compiler_params=_params(("arbitrary",)))(x, g, *views, dres)


def _proj_res_norm(name, a, w, res, gain, tm=512):
    M, K = a.shape
    D = w.shape[1]

    def body(a_ref, w_ref, res_ref, g_ref, x_ref, h_ref):
        xf = res_ref[...] + _dot(a_ref[...], w_ref[...], "nn")
        x_ref[...] = xf
        r = lax.rsqrt(jnp.mean(xf * xf, axis=-1, keepdims=True) + RMS_EPS)
        h_ref[...] = ((xf * r) * g_ref[...]).astype(h_ref.dtype)

    row = pl.BlockSpec((tm, D), lambda i: (i, 0))
    return _pcall(body, name=name, grid=(M // tm,),
                  in_specs=[pl.BlockSpec((tm, K), lambda i: (i, 0)), pl.BlockSpec((K, D), lambda i: (0, 0)), row,
                            pl.BlockSpec((1, D), lambda i: (0, 0))],
                  out_specs=[row, row],
                  out_shape=[jax.ShapeDtypeStruct((M, D), F32), jax.ShapeDtypeStruct((M, D), BF16)],
                  compiler_params=_params(("parallel",)))(a, w, res, gain)


def _proj_res_loss(name, a, w, res, gain, tgt, tm=512):
    M, K = a.shape
    D = w.shape[1]

    def body(a_ref, w_ref, res_ref, g_ref, t_ref, loss_ref, dx_ref, dxb_ref, dg_ref):
        first = pl.program_id(0) == 0
        xf = res_ref[...] + _dot(a_ref[...], w_ref[...], "nn")
        r = lax.rsqrt(jnp.mean(xf * xf, axis=-1, keepdims=True) + RMS_EPS)
        xh = xf * r
        gv = g_ref[...]
        e = xh * gv - t_ref[...]
        lp = 0.5 * jnp.sum(jnp.mean(e * e, axis=-1, keepdims=True), axis=0, keepdims=True)
        dy = e * (1.0 / D)
        t = dy * gv
        dx = r * (t - xh * jnp.mean(t * xh, axis=-1, keepdims=True))
        dx_ref[...] = dx
        dxb_ref[...] = dx.astype(dxb_ref.dtype)
        _accumulate(dg_ref, jnp.sum(dy * xh, axis=0, keepdims=True), first)
        _accumulate(loss_ref, lp, first)

    row = pl.BlockSpec((tm, D), lambda i: (i, 0))
    vec = pl.BlockSpec((1, D), lambda i: (0, 0))
    return _pcall(body, name=name, grid=(M // tm,),
                  in_specs=[pl.BlockSpec((tm, K), lambda i: (i, 0)), pl.BlockSpec((K, D), lambda i: (0, 0)), row,
                            vec, row],
                  out_specs=[pl.BlockSpec((1, 1), lambda i: (0, 0)), row, row, vec],
                  out_shape=[jax.ShapeDtypeStruct((1, 1), F32), jax.ShapeDtypeStruct((M, D), F32),
                             jax.ShapeDtypeStruct((M, D), BF16), jax.ShapeDtypeStruct((1, D), F32)],
                  compiler_params=_params(("arbitrary",)))(a, w, res, gain, tgt)


def _dh_norm_bwd(name, d, w, x, gain, dres, tm=512):
    M, N = d.shape
    D = w.shape[0]

    def body(d_ref, w_ref, x_ref, g_ref, dres_ref, dx_ref, dxb_ref, dg_ref):
        dh = _dot(d_ref[...].astype(BF16), w_ref[...], "nt")
        dx, dg = _norm_bwd_rows(x_ref[...], g_ref[...], dh, dres_ref[...])
        dx_ref[...] = dx
        dxb_ref[...] = dx.astype(dxb_ref.dtype)
        _accumulate(dg_ref, dg, pl.program_id(0) == 0)

    row = pl.BlockSpec((tm, D), lambda i: (i, 0))
    vec = pl.BlockSpec((1, D), lambda i: (0, 0))
    return _pcall(body, name=name, grid=(M // tm,),
                  in_specs=[pl.BlockSpec((tm, N), lambda i: (i, 0)), pl.BlockSpec((D, N), lambda i: (0, 0)), row, vec,
                            row],
                  out_specs=[row, row, vec],
                  out_shape=[jax.ShapeDtypeStruct((M, D), F32), jax.ShapeDtypeStruct((M, D), BF16),
                             jax.ShapeDtypeStruct((1, D), F32)],
                  compiler_params=_params(("arbitrary",)))(d, w, x, gain, dres)


def _sigmoid(v):
    return 0.5 * jnp.tanh(0.5 * v) + 0.5


def _ffn_up(name, h, w_gu, tm=1024):
    S, D = h.shape
    W = FF_SHARD

    def body(h_ref, wg_ref, wu_ref, gu_ref, act_ref):
        hv = h_ref[...]
        gate = _dot(hv, wg_ref[...], "nn")
        up = _dot(hv, wu_ref[...], "nn")
        gu_ref[0] = gate.astype(gu_ref.dtype)
        gu_ref[1] = up.astype(gu_ref.dtype)
        sig = _sigmoid(gate)
        act_ref[...] = ((gate * sig) * up).astype(act_ref.dtype)

    return _pcall(
        body, name=name, grid=(2, S // tm),
        in_specs=[pl.BlockSpec((tm, D), lambda j, i: (i, 0)),
                  pl.BlockSpec((None, D, W), lambda j, i: (j, 0, 0)),
                  pl.BlockSpec((None, D, W), lambda j, i: (j + 2, 0, 0))],
        out_specs=[pl.BlockSpec((2, tm, W), lambda j, i: (0, i, j)), pl.BlockSpec((tm, W), lambda j, i: (i, j))],
        out_shape=[jax.ShapeDtypeStruct((2, S, D_FF), BF16), jax.ShapeDtypeStruct((S, D_FF), BF16)],
        compiler_params=_params(("parallel", "parallel")))(h, w_gu, w_gu)


def _ffn_down_bwd(name, dx, w_down, gu, tm=1024):
    S, D = dx.shape
    W = FF_SHARD

    def body(dx_ref, w_ref, gu_ref, dgu_ref):
        dact = _dot(dx_ref[...].astype(BF16), w_ref[...], "nt")
        gate = gu_ref[0].astype(F32)
        up = gu_ref[1].astype(F32)
        sig = _sigmoid(gate)
        silu = gate * sig
        dgu_ref[0] = (dact * up * (sig * (1.0 + gate * (1.0 - sig)))).astype(dgu_ref.dtype)
        dgu_ref[1] = (dact * silu).astype(dgu_ref.dtype)

    blk = pl.BlockSpec((2, tm, W), lambda j, i: (0, i, j))
    return _pcall(
        body, name=name, grid=(2, S // tm),
        in_specs=[pl.BlockSpec((tm, D), lambda j, i: (i, 0)), pl.BlockSpec((W, D), lambda j, i: (j, 0)), blk],
        out_specs=blk, out_shape=jax.ShapeDtypeStruct((2, S, D_FF), BF16),
        compiler_params=_params(("parallel", "parallel")))(dx, w_down, gu)


def _ffn_dw_up(name, h, dgu, tk=2048):
    S, D = h.shape
    W = FF_SHARD
    tk = min(tk, S)
    return _mm(name, "tn", (N_CHIPS, S // tk), h, pl.BlockSpec((tk, D), lambda s, k: (k, 0)),
               dgu, pl.BlockSpec((None, tk, W), lambda s, k: (s // 2, k, s % 2)),
               jax.ShapeDtypeStruct((N_CHIPS, D, W), BF16), pl.BlockSpec((None, D, W), lambda s, k: (s, 0, 0)),
               (D, W))


def _ffn_dh(name, dgu, w_gu, x, gain, dres, tm=512):
    S = dgu.shape[1]
    W = FF_SHARD

    def body(a_ref, w_hbm, x_ref, g_ref, dres_ref, dx_ref, dg_ref, w_ref, sems):
        first = pl.program_id(0) == 0

        @pl.when(first)
        def _():
            copies = [pltpu.make_async_copy(w_hbm.at[s], w_ref.at[:, pl.ds(s * W, W)], sems.at[s])
                      for s in range(N_CHIPS)]
            for cp in copies:
                cp.start()
            for cp in copies:
                cp.wait()

        dh = _dot(a_ref[0], w_ref[:, :D_FF], "nt") + _dot(a_ref[1], w_ref[:, D_FF:], "nt")
        dx, dg = _norm_bwd_rows(x_ref[...], g_ref[...], dh, dres_ref[...])
        dx_ref[...] = dx
        _accumulate(dg_ref, dg, first)

    row = pl.BlockSpec((tm, D_MODEL), lambda i: (i, 0))
    vec = pl.BlockSpec((1, D_MODEL), lambda i: (0, 0))
    return _pcall(body, name=name, grid=(S // tm,),
                  in_specs=[pl.BlockSpec((2, tm, D_FF), lambda i: (0, i, 0)), pl.BlockSpec(memory_space=pl.ANY),
                            row, vec, row],
                  out_specs=[row, vec],
                  out_shape=[jax.ShapeDtypeStruct((S, D_MODEL), F32), jax.ShapeDtypeStruct((1, D_MODEL), F32)],
                  scratch_shapes=[pltpu.VMEM((D_MODEL, 2 * D_FF), BF16), pltpu.SemaphoreType.DMA((N_CHIPS,))],
                  compiler_params=_params(("arbitrary",)))(dgu, w_gu, x, gain, dres)


FOLD_TOKENS = 1024
LANES = 128


def _lane_chunks(width):
    return [slice(c * LANES, (c + 1) * LANES) for c in range(width // LANES)]


def _strided_rows(r, n, dil):
    return pl.ds(r, n) if dil == 1 else pl.ds(r, n, stride=dil)


def _norm_fold(name, x, gain):
    S, D = x.shape
    chunks = _lane_chunks(D)

    def body(x_ref, g_ref, hf_hbm, xc_ref, hs_ref, sems):
        i = pl.program_id(0)
        slot = i % 2

        def folded_copies(slot, step):
            out = []
            for g, dil in enumerate(ATTN_DILATIONS):
                n = FOLD_TOKENS // dil
                for r in range(dil):
                    out.append(pltpu.make_async_copy(hs_ref.at[slot, g, pl.ds(r * n, n)],
                                                     hf_hbm.at[g, pl.ds(r * (S // dil) + step * n, n)],
                                                     sems.at[slot, len(out)]))
            return out

        @pl.when(i >= 2)
        def _():
            for cp in folded_copies(slot, i - 2):
                cp.wait()

        xf = x_ref[...]
        inv = lax.rsqrt(jnp.mean(xf * xf, axis=-1, keepdims=True) + RMS_EPS)
        h = (xf * inv) * g_ref[...]
        hs_ref[slot, 0] = h.astype(BF16)
        for c, cols in enumerate(chunks):
            xc_ref[c] = h[:, cols]
        copies = folded_copies(slot, i)
        started = 0
        for g, dil in enumerate(ATTN_DILATIONS):
            n = FOLD_TOKENS // dil
            for r in range(dil):
                if dil > 1:
                    for c, cols in enumerate(chunks):
                        hs_ref[slot, g, r * n:(r + 1) * n, cols] = (
                            xc_ref[c, _strided_rows(r, n, dil), :].astype(BF16))
                copies[started].start()
                started += 1

        @pl.when(i == pl.num_programs(0) - 1)
        def _():
            for cp in folded_copies(1 - slot, i - 1) + copies:
                cp.wait()

    assert S // FOLD_TOKENS >= 2
    return _pcall(body, name=name, grid=(S // FOLD_TOKENS,),
                  in_specs=[pl.BlockSpec((FOLD_TOKENS, D), lambda i: (i, 0)), pl.BlockSpec((1, D), lambda i: (0, 0))],
                  out_specs=pl.BlockSpec(memory_space=pl.ANY),
                  out_shape=jax.ShapeDtypeStruct((N_GROUPS, S, D), BF16),
                  scratch_shapes=[pltpu.VMEM((D // LANES, FOLD_TOKENS, LANES), F32),
                                  pltpu.VMEM((2, N_GROUPS, FOLD_TOKENS, D), BF16),
                                  pltpu.SemaphoreType.DMA((2, sum(ATTN_DILATIONS)))],
                  compiler_params=_params(("arbitrary",)))(x, gain)


def _qkv_tiles(name, piece, hf, w, qkv, chip, tm=1024):
    S = hf.shape[1]
    per_group = 3 * D_MODEL // QKV_TILE

    def tile(q, c_ref):
        if piece is None:
            return QKV_PIECES * c_ref[0] + q
        return QKV_PIECES * ((c_ref[0] + 1 + q) % N_CHIPS) + piece

    def body(*refs):
        a_ref, w_ref, o_ref = refs[1], refs[2], refs[-1]
        o_ref[...] = _dot(a_ref[...], w_ref[...], "nn").astype(o_ref.dtype)

    if piece is None:
        w_spec = pl.BlockSpec((D_MODEL, QKV_TILE), lambda q, i, c_ref: (0, q))
    else:
        w_spec = pl.BlockSpec((None, D_MODEL, QKV_TILE), lambda q, i, c_ref: ((c_ref[0] + 1 + q) % N_CHIPS, 0, 0))
    in_specs = [pl.BlockSpec((None, tm, D_MODEL), lambda q, i, c_ref: (tile(q, c_ref) // per_group, i, 0)), w_spec]
    operands = [chip, hf, w]
    aliases = {}
    if qkv is not None:
        in_specs.append(pl.BlockSpec(memory_space=pl.ANY))
        operands.append(qkv)
        aliases = {3: 0}
    return _pcall(
        body, prefetch=1, name=name, grid=(N_CHIPS - 1, S // tm), in_specs=in_specs,
        out_specs=pl.BlockSpec((None, tm, QKV_TILE),
                               lambda q, i, c_ref: (tile(q, c_ref) // per_group, i, tile(q, c_ref) % per_group)),
        out_shape=jax.ShapeDtypeStruct((N_GROUPS, S, 3 * D_MODEL), BF16), input_output_aliases=aliases,
        compiler_params=_params(("arbitrary", "arbitrary")))(*operands)


QKV_PIECES = QKV_SHARD // QKV_TILE


def _qkv_dw(name, p, hf, dqkv):
    S = hf.shape[1]
    per_group = 3 * D_MODEL // QKV_TILE
    tile = lambda s: QKV_PIECES * s + p
    return _mm(name, "tn", (N_CHIPS, 1),
               hf, pl.BlockSpec((None, S, D_MODEL), lambda s, k: (tile(s) // per_group, 0, 0)),
               dqkv, pl.BlockSpec((None, S, QKV_TILE), lambda s, k: (tile(s) // per_group, 0, tile(s) % per_group)),
               jax.ShapeDtypeStruct((N_CHIPS, D_MODEL, QKV_TILE), BF16),
               pl.BlockSpec((None, D_MODEL, QKV_TILE), lambda s, k: (s, 0, 0)), None)


def _qkv_dh(name, g, dqkv, w_pieces, dhf, tm=1024):
    S = dqkv.shape[1]
    per_group = 3 * D_MODEL // QKV_TILE

    def body(*refs):
        a_ref, w_hbm = refs[0], refs[1:1 + QKV_PIECES]
        o_ref, w_ref, sems = refs[-3:]

        @pl.when(pl.program_id(0) == 0)
        def _():
            copies = []
            for j in range(per_group):
                t = per_group * g + j
                copies.append(pltpu.make_async_copy(w_hbm[t % QKV_PIECES].at[t // QKV_PIECES],
                                                    w_ref.at[:, pl.ds(j * QKV_TILE, QKV_TILE)], sems.at[j]))
            for cp in copies:
                cp.start()
            for cp in copies:
                cp.wait()

        o_ref[...] = _dot(a_ref[...], w_ref[...], "nt").astype(o_ref.dtype)

    any_spec = pl.BlockSpec(memory_space=pl.ANY)
    in_specs = [pl.BlockSpec((None, tm, 3 * D_MODEL), lambda i: (g, i, 0))] + [any_spec] * QKV_PIECES
    operands = [dqkv] + list(w_pieces)
    aliases = {}
    if dhf is not None:
        in_specs.append(any_spec)
        operands.append(dhf)
        aliases = {1 + QKV_PIECES: 0}
    return _pcall(body, name=name, grid=(S // tm,), in_specs=in_specs,
                  out_specs=pl.BlockSpec((None, tm, D_MODEL), lambda i: (g, i, 0)),
                  out_shape=jax.ShapeDtypeStruct((N_GROUPS, S, D_MODEL), BF16),
                  scratch_shapes=[pltpu.VMEM((D_MODEL, 3 * D_MODEL), BF16), pltpu.SemaphoreType.DMA((per_group,))],
                  input_output_aliases=aliases, compiler_params=_params(("arbitrary",)))(*operands)


def _alibi_coef(g, h):
    vals = [-(2.0 ** (-8.0 * (gg * N_HEADS + h + 1) / (N_GROUPS * N_HEADS))) * ATTN_DILATIONS[gg]
            for gg in range(N_GROUPS)]
    return jnp.where(g == 0, vals[0], jnp.where(g == 1, vals[1], vals[2])).astype(F32)


def _blocks_per_segment(g, S):
    return jnp.right_shift(S // Q_BLOCK, 2 * g)


def _band_bias(pen):
    row = lax.broadcasted_iota(jnp.int32, (Q_BLOCK, 2 * Q_BLOCK), 0)
    col = lax.broadcasted_iota(jnp.int32, (Q_BLOCK, 2 * Q_BLOCK), 1)
    dist = Q_BLOCK + row - col
    valid = jnp.logical_and(dist >= 0, dist <= Q_BLOCK)
    base = jnp.where(valid, jnp.where(col < Q_BLOCK, pen, 0.0), NEG).astype(F32)
    return base, dist.astype(F32)


def _skewed(n_units, stages):
    state = {}
    for step in range(n_units + len(stages) - 1):
        for s, stage in enumerate(stages):
            u = step - s
            if 0 <= u < n_units:
                state[u] = stage(u, state.get(u))
                if s == len(stages) - 1:
                    del state[u]


def _attn_fwd(name, qkv, tq=1024):
    S = qkv.shape[1]
    nqb = tq // Q_BLOCK
    scale = HEAD_DIM ** -0.5

    def body(q_ref, k_ref, kp_ref, v_ref, vp_ref, o_ref, lse_ref, kf_ref, vf_ref):
        g = pl.program_id(0)
        i = pl.program_id(1)
        nb = _blocks_per_segment(g, S)
        kf_ref[:Q_BLOCK] = kp_ref[...]
        kf_ref[Q_BLOCK:] = k_ref[...]
        vf_ref[:Q_BLOCK] = vp_ref[...]
        vf_ref[Q_BLOCK:] = v_ref[...]
        coefs = [_alibi_coef(g, h) for h in range(N_HEADS)]
        units = [(b, h) for b in range(nqb) for h in range(N_HEADS)]
        bias = {}

        def scores(u, _):
            b, h = units[u]
            if b not in bias:
                pen = jnp.where((i * nqb + b) % nb != 0, 0.0, NEG).astype(F32)
                bias.clear()
                bias[b] = _band_bias(pen)
            base, dist = bias[b]
            cols = slice(h * HEAD_DIM, (h + 1) * HEAD_DIM)
            q = q_ref[b * Q_BLOCK:(b + 1) * Q_BLOCK, cols]
            kk = kf_ref[b * Q_BLOCK:(b + 2) * Q_BLOCK, cols]
            return _dot(q, kk, "nt") * scale + (coefs[h] * dist + base)

        def softmax(u, s):
            m = jnp.max(s, axis=-1, keepdims=True)
            p = jnp.exp(s - m)
            den = jnp.sum(p, axis=-1, keepdims=True)
            return (p * (1.0 / den)).astype(BF16), m + jnp.log(den)

        def output(u, st):
            b, h = units[u]
            pn, lse = st
            cols = slice(h * HEAD_DIM, (h + 1) * HEAD_DIM)
            rows = slice(b * Q_BLOCK, (b + 1) * Q_BLOCK)
            o_ref[rows, cols] = _dot(pn, vf_ref[b * Q_BLOCK:(b + 2) * Q_BLOCK, cols], "nn").astype(o_ref.dtype)
            lse_ref[rows, h:h + 1] = lse

        _skewed(len(units), [scores, softmax, output])

    main = lambda c: pl.BlockSpec((None, tq, D_MODEL), lambda g, i: (g, i, c))
    prev = lambda c: pl.BlockSpec((None, Q_BLOCK, D_MODEL), lambda g, i: (g, jnp.maximum(i * nqb - 1, 0), c))
    return _pcall(
        body, name=name, grid=(N_GROUPS, S // tq),
        in_specs=[main(0), main(1), prev(1), main(2), prev(2)],
        out_specs=[pl.BlockSpec((None, tq, D_MODEL), lambda g, i: (g, i, 0)),
                   pl.BlockSpec((None, tq, N_HEADS), lambda g, i: (g, i, 0))],
        out_shape=[jax.ShapeDtypeStruct((N_GROUPS, S, D_MODEL), BF16),
                   jax.ShapeDtypeStruct((N_GROUPS, S, N_HEADS), F32)],
        scratch_shapes=[pltpu.VMEM((tq + Q_BLOCK, D_MODEL), BF16), pltpu.VMEM((tq + Q_BLOCK, D_MODEL), BF16)],
        compiler_params=_params(("parallel", "parallel")))(qkv, qkv, qkv, qkv, qkv)


def _attn_bwd(name, qkv, do, lse, dl, tq=1024):
    S = qkv.shape[1]
    nqb = tq // Q_BLOCK
    n_blocks = S // Q_BLOCK
    scale = HEAD_DIM ** -0.5
    KEYS = 2 * Q_BLOCK

    def body(q_ref, k_ref, v_ref, kp_ref, vp_ref, qn_ref, do_ref, don_ref, lse_ref, lsen_ref, dl_ref, dln_ref,
             out_ref, kf_ref, vf_ref, dk_ref, dv_ref):
        g = pl.program_id(0)
        i = pl.program_id(1)
        nb = _blocks_per_segment(g, S)
        kf_ref[:Q_BLOCK] = kp_ref[...]
        kf_ref[Q_BLOCK:] = k_ref[...]
        vf_ref[:Q_BLOCK] = vp_ref[...]
        vf_ref[Q_BLOCK:] = v_ref[...]
        coefs = [_alibi_coef(g, h) for h in range(N_HEADS)]
        units = [(h, b) for h in range(N_HEADS) for b in range(nqb + 1)]
        bias = {}

        def band(b):
            if b not in bias:
                blk = i * nqb + b
                ok = blk % nb != 0
                if b == nqb:
                    ok = jnp.logical_and(ok, blk < n_blocks)
                bias[b] = _band_bias(jnp.where(ok, 0.0, NEG).astype(F32))
            return bias[b]

        def operands(h, b):
            cols = slice(h * HEAD_DIM, (h + 1) * HEAD_DIM)
            if b == nqb:
                keys = slice(tq, tq + Q_BLOCK)
                return (qn_ref[:, cols], don_ref[:, cols], lsen_ref[:, h:h + 1], dln_ref[:, h:h + 1],
                        kf_ref[keys, cols], vf_ref[keys, cols])
            rows = slice(b * Q_BLOCK, (b + 1) * Q_BLOCK)
            keys = slice(b * Q_BLOCK, b * Q_BLOCK + KEYS)
            return (q_ref[rows, cols], do_ref[rows, cols], lse_ref[rows, h:h + 1], dl_ref[rows, h:h + 1],
                    kf_ref[keys, cols], vf_ref[keys, cols])

        def probs(u, _):
            h, b = units[u]
            q, do_b, lse_b, dl_b, kk, vv = operands(h, b)
            base, dist = band(b)
            if b == nqb:
                base, dist = base[:, :Q_BLOCK], dist[:, :Q_BLOCK]
            p = jnp.exp(_dot(q, kk, "nt") * scale + (coefs[h] * dist + base) - lse_b)
            return p, _dot(do_b, vv, "nt")

        def dscores(u, st):
            h, b = units[u]
            p, dp = st
            dl_b = operands(h, b)[3]
            return ((p * (dp - dl_b)) * scale).astype(BF16), p.astype(BF16)

        def grads(u, st):
            h, b = units[u]
            ds, pb = st
            q, do_b, _, _, kk, _ = operands(h, b)
            cols = slice(h * HEAD_DIM, (h + 1) * HEAD_DIM)
            if b < nqb:
                out_ref[b * Q_BLOCK:(b + 1) * Q_BLOCK, cols] = _dot(ds, kk, "nn").astype(out_ref.dtype)
            if b == 0:
                dk_ref[:Q_BLOCK] = _dot(ds[:, Q_BLOCK:], q, "tn")
                dv_ref[:Q_BLOCK] = _dot(pb[:, Q_BLOCK:], do_b, "tn")
                return None
            dk = _dot(ds, q, "tn")
            dv = _dot(pb, do_b, "tn")
            before = slice((b - 1) * Q_BLOCK, b * Q_BLOCK)
            dk_ref[before] += dk[:Q_BLOCK]
            dv_ref[before] += dv[:Q_BLOCK]
            if b < nqb:
                own = slice(b * Q_BLOCK, (b + 1) * Q_BLOCK)
                dk_ref[own] = dk[Q_BLOCK:]
                dv_ref[own] = dv[Q_BLOCK:]
            else:
                out_ref[:, D_MODEL + h * HEAD_DIM:D_MODEL + (h + 1) * HEAD_DIM] = dk_ref[...].astype(out_ref.dtype)
                out_ref[:, 2 * D_MODEL + h * HEAD_DIM:2 * D_MODEL + (h + 1) * HEAD_DIM] = (
                    dv_ref[...].astype(out_ref.dtype))
            return None

        _skewed(len(units), [probs, dscores, grads])

    nxt_blk = lambda i: jnp.minimum((i + 1) * nqb, n_blocks - 1)
    main = lambda c: pl.BlockSpec((None, tq, D_MODEL), lambda g, i: (g, i, c))
    prev = lambda c: pl.BlockSpec((None, Q_BLOCK, D_MODEL), lambda g, i: (g, jnp.maximum(i * nqb - 1, 0), c))
    row = pl.BlockSpec((None, tq, D_MODEL), lambda g, i: (g, i, 0))
    row_n = pl.BlockSpec((None, Q_BLOCK, D_MODEL), lambda g, i: (g, nxt_blk(i), 0))
    col = pl.BlockSpec((None, tq, N_HEADS), lambda g, i: (g, i, 0))
    col_n = pl.BlockSpec((None, Q_BLOCK, N_HEADS), lambda g, i: (g, nxt_blk(i), 0))
    return _pcall(
        body, name=name, grid=(N_GROUPS, S // tq),
        in_specs=[main(0), main(1), main(2), prev(1), prev(2), row_n, row, row_n, col, col_n, col, col_n],
        out_specs=pl.BlockSpec((None, tq, 3 * D_MODEL), lambda g, i: (g, i, 0)),
        out_shape=jax.ShapeDtypeStruct((N_GROUPS, S, 3 * D_MODEL), BF16),
        scratch_shapes=[pltpu.VMEM((tq + Q_BLOCK, D_MODEL), BF16), pltpu.VMEM((tq + Q_BLOCK, D_MODEL), BF16),
                        pltpu.VMEM((tq, HEAD_DIM), F32), pltpu.VMEM((tq, HEAD_DIM), F32)],
        compiler_params=_params(("parallel", "parallel")))(qkv, qkv, qkv, qkv, qkv, qkv, do, do, lse, lse, dl, dl)


def _group_weights(lse_ref):
    l0, l1, l2 = lse_ref[0], lse_ref[1], lse_ref[2]
    m = jnp.maximum(jnp.maximum(l0, l1), l2)
    e = [jnp.exp(l0 - m), jnp.exp(l1 - m), jnp.exp(l2 - m)]
    den = e[0] + e[1] + e[2]
    return [ei / den for ei in e]


MERGE_TOKENS = 512


def _folded_views(a, tb):
    _, S, C = a.shape
    views, specs = [], []
    for g, dil in enumerate(ATTN_DILATIONS):
        views.append(a.reshape(N_GROUPS * dil, S // dil, C))
        specs.append(pl.BlockSpec((dil, tb // dil, C), lambda i, g=g: (g, i, 0)))
    return views, specs


def _unfold_into(dst_ref, folded_refs):
    for g, (dil, ref) in enumerate(zip(ATTN_DILATIONS, folded_refs)):
        n = ref.shape[1]
        for r in range(dil):
            for c, cols in enumerate(_lane_chunks(ref.shape[2])):
                dst_ref[g, c, _strided_rows(r, n, dil), :] = ref[r, :, cols].astype(F32)


def _merge_fwd(name, o, lse, tb=MERGE_TOKENS):
    S = o.shape[1]

    def body(o0_ref, o1_ref, o2_ref, lse_ref, out_ref, o_ref):
        _unfold_into(o_ref, (o0_ref, o1_ref, o2_ref))
        w = _group_weights(lse_ref)
        for h in range(N_HEADS):
            cols = slice(h * HEAD_DIM, (h + 1) * HEAD_DIM)
            acc = w[0][:, h:h + 1] * o_ref[0, h]
            for gg in range(1, N_GROUPS):
                acc = acc + w[gg][:, h:h + 1] * o_ref[gg, h]
            out_ref[:, cols] = acc.astype(out_ref.dtype)

    views, specs = _folded_views(o, tb)
    return _pcall(body, name=name, grid=(S // tb,),
                  in_specs=specs + [pl.BlockSpec((N_GROUPS, tb, N_HEADS), lambda i: (0, i, 0))],
                  out_specs=pl.BlockSpec((tb, D_MODEL), lambda i: (i, 0)),
                  out_shape=jax.ShapeDtypeStruct((S, D_MODEL), BF16),
                  scratch_shapes=[pltpu.VMEM((N_GROUPS, N_HEADS, tb, HEAD_DIM), F32)],
                  compiler_params=_params(("parallel",)))(*views, lse)


def _merge_bwd(name, dx, w_out, o, lse, tb=MERGE_TOKENS):
    S = o.shape[1]
    n_chunks = sum(ATTN_DILATIONS)

    def body(dx_ref, w_ref, o0_ref, o1_ref, o2_ref, lse_ref, do_hbm, dl_ref, o_ref, dt_ref, df_ref, d_ref, sems):
        i = pl.program_id(0)
        slot = i % 2

        def folded_copies(slot, step):
            out = []
            for gg, dil in enumerate(ATTN_DILATIONS):
                n = tb // dil
                for r in range(dil):
                    out.append(pltpu.make_async_copy(df_ref.at[slot, gg, pl.ds(r * n, n)],
                                                     do_hbm.at[gg, pl.ds(r * (S // dil) + step * n, n)],
                                                     sems.at[slot, len(out)]))
            return out

        @pl.when(i >= 2)
        def _():
            for cp in folded_copies(slot, i - 2):
                cp.wait()

        d_ref[...] = _dot(dx_ref[...].astype(BF16), w_ref[...], "nt")
        _unfold_into(o_ref, (o0_ref, o1_ref, o2_ref))
        w = _group_weights(lse_ref)
        for h in range(N_HEADS):
            cols = slice(h * HEAD_DIM, (h + 1) * HEAD_DIM)
            dv = d_ref[:, cols]
            merged = w[0][:, h:h + 1] * o_ref[0, h]
            for gg in range(1, N_GROUPS):
                merged = merged + w[gg][:, h:h + 1] * o_ref[gg, h]
            dsum = jnp.sum(dv * merged, axis=-1, keepdims=True)
            for gg in range(N_GROUPS):
                wg = w[gg][:, h:h + 1]
                dt_ref[gg, h] = wg * dv
                dl_ref[gg, :, h:h + 1] = wg * dsum
        copies = folded_copies(slot, i)
        started = 0
        for gg, dil in enumerate(ATTN_DILATIONS):
            n = tb // dil
            for r in range(dil):
                for h, cols in enumerate(_lane_chunks(D_MODEL)):
                    df_ref[slot, gg, r * n:(r + 1) * n, cols] = (
                        dt_ref[gg, h, _strided_rows(r, n, dil), :].astype(df_ref.dtype))
                copies[started].start()
                started += 1

        @pl.when(i == pl.num_programs(0) - 1)
        def _():
            for cp in folded_copies(1 - slot, i - 1) + copies:
                cp.wait()

    assert S // tb >= 2
    views, specs = _folded_views(o, tb)
    small = pl.BlockSpec((N_GROUPS, tb, N_HEADS), lambda i: (0, i, 0))
    return _pcall(body, name=name, grid=(S // tb,),
                  in_specs=[pl.BlockSpec((tb, D_MODEL), lambda i: (i, 0)),
                            pl.BlockSpec((D_MODEL, D_MODEL), lambda i: (0, 0))] + specs + [small],
                  out_specs=[pl.BlockSpec(memory_space=pl.ANY), small],
                  out_shape=[jax.ShapeDtypeStruct((N_GROUPS, S, D_MODEL), BF16),
                             jax.ShapeDtypeStruct((N_GROUPS, S, N_HEADS), F32)],
                  scratch_shapes=[pltpu.VMEM((N_GROUPS, N_HEADS, tb, HEAD_DIM), F32),
                                  pltpu.VMEM((N_GROUPS, N_HEADS, tb, HEAD_DIM), F32),
                                  pltpu.VMEM((2, N_GROUPS, tb, D_MODEL), BF16), pltpu.VMEM((tb, D_MODEL), F32),
                                  pltpu.SemaphoreType.DMA((2, n_chunks))],
                  compiler_params=_params(("arbitrary",)))(dx, w_out, *views, lse)


def _shift_rows(a, k):
    return pltpu.roll(a, k % a.shape[0], axis=0)


def _pool_level(g, levels):
    return jnp.where(g == 0, levels[0], jnp.where(g == 1, levels[1], jnp.where(g == 2, levels[2], levels[3])))


def _pool_fwd(name, h, w_in, w_group, scale, x_in, gain_next, tr=1024):
    S, D = h.shape
    H = POOL_HALO
    n_groups = D // POOL_DIM

    def body(h_ref, hh_ref, wi_ref, w_ref, sc_ref, x_ref, gn_ref, y_ref, z_ref, xo_ref, hn_ref, xs_ref):
        i = pl.program_id(0)
        g = pl.program_id(1)
        uv = _dot(h_ref[...], wi_ref[...], "nn")
        halo = jnp.where(i > 0, _dot(hh_ref[...], wi_ref[...], "nn"), 0.0)
        s = jnp.concatenate([halo, uv], axis=0)
        levels = []
        for k in (1, 2, 4, 8):
            s = s + _shift_rows(s, k)
            levels.append(s[H:])
        t = i * tr + lax.broadcasted_iota(jnp.int32, (tr, 1), 0)
        cnt = jnp.minimum(t + 1, jnp.left_shift(2, g)).astype(F32)
        y = _pool_level(g, levels) / cnt - uv
        yb = y.astype(BF16)
        z = _dot(yb, w_ref[...], "nn")
        y_ref[...] = yb
        z_ref[...] = z.astype(z_ref.dtype)
        x_out = x_ref[...] + z * sc_ref[...]
        xo_ref[...] = x_out
        xs_ref[g] = x_out

        @pl.when(g == n_groups - 1)
        def _():
            xf = jnp.concatenate([xs_ref[k] for k in range(n_groups)], axis=1)
            r = lax.rsqrt(jnp.mean(xf * xf, axis=-1, keepdims=True) + RMS_EPS)
            hn_ref[...] = ((xf * r) * gn_ref[...]).astype(hn_ref.dtype)

    blk = pl.BlockSpec((tr, POOL_DIM), lambda i, g: (i, g))
    row = pl.BlockSpec((tr, D), lambda i, g: (i, 0))
    return _pcall(
        body, name=name, grid=(S // tr, n_groups),
        in_specs=[row, pl.BlockSpec((H, D), lambda i, g: (jnp.maximum(i * (tr // H) - 1, 0), 0)),
                  pl.BlockSpec((D, POOL_DIM), lambda i, g: (0, g)),
                  pl.BlockSpec((None, POOL_DIM, POOL_DIM), lambda i, g: (g, 0, 0)),
                  pl.BlockSpec((1, POOL_DIM), lambda i, g: (0, g)), blk, pl.BlockSpec((1, D), lambda i, g: (0, 0))],
        out_specs=[blk, blk, blk, row],
        out_shape=[jax.ShapeDtypeStruct((S, D), BF16), jax.ShapeDtypeStruct((S, D), BF16),
                   jax.ShapeDtypeStruct((S, D), F32), jax.ShapeDtypeStruct((S, D), BF16)],
        scratch_shapes=[pltpu.VMEM((n_groups, tr, POOL_DIM), F32)],
        compiler_params=_params(("parallel", "arbitrary")))(h, h, w_in, w_group, scale, x_in, gain_next)


def _pool_bwd(name, d_out, z, y, scale, w_group, tr=1024):
    S, D = d_out.shape
    H = POOL_HALO

    def body(d_ref, dn_ref, z_ref, y_ref, sc_ref, w_ref, du_ref, dw_ref, dsc_ref):
        g = pl.program_id(0)
        i = pl.program_id(1)
        dv = d_ref[...]
        dz = jnp.concatenate([dv, dn_ref[...]], axis=0) * sc_ref[...]
        dzb = dz.astype(BF16)
        dy = _dot(dzb, w_ref[...], "nt")
        t = i * tr + lax.broadcasted_iota(jnp.int32, (tr + H, 1), 0)
        cnt = jnp.minimum(t + 1, jnp.left_shift(2, g)).astype(F32)
        s = jnp.where(t < S, dy / cnt, 0.0)
        levels = []
        for k in (1, 2, 4, 8):
            s = s + _shift_rows(s, -k)
            levels.append(s[:tr])
        du_ref[...] = (_pool_level(g, levels) - dy[:tr]).astype(du_ref.dtype)
        dw = _dot(y_ref[...], dzb[:tr], "tn")
        dsc = jnp.sum(dv * z_ref[...].astype(F32), axis=0, keepdims=True)

        @pl.when(i == 0)
        def _():
            dw_ref[...] = dw
            dsc_ref[...] = dsc

        @pl.when(i > 0)
        def _():
            dw_ref[...] += dw
            dsc_ref[...] += dsc

    blk = pl.BlockSpec((tr, POOL_DIM), lambda g, i: (i, g))
    vec = pl.BlockSpec((1, POOL_DIM), lambda g, i: (0, g))
    mat = pl.BlockSpec((None, POOL_DIM, POOL_DIM), lambda g, i: (g, 0, 0))
    nxt = pl.BlockSpec((H, POOL_DIM), lambda g, i: (jnp.minimum((i + 1) * (tr // H), S // H - 1), g))
    return _pcall(
        body, name=name, grid=(D // POOL_DIM, S // tr), in_specs=[blk, nxt, blk, blk, vec, mat],
        out_specs=[blk, mat, vec],
        out_shape=[jax.ShapeDtypeStruct((S, D), BF16), jax.ShapeDtypeStruct((D // POOL_DIM, POOL_DIM, POOL_DIM), F32),
                   jax.ShapeDtypeStruct((1, D), F32)],
        compiler_params=_params(("parallel", "arbitrary")))(d_out, d_out, z, y, scale, w_group)


def _row_tile(rows, cols, target_bytes=1 << 20):
    best = rows
    for tr in range(8, rows + 1, 8):
        if rows % tr == 0 and tr * cols * 4 <= target_bytes:
            best = tr
    return best if best * cols * 4 <= 4 * target_bytes else rows


def _adamw_step(w, g, m, v):
    m2 = ADAM_B1 * m + (1.0 - ADAM_B1) * g
    v2 = ADAM_B2 * v + (1.0 - ADAM_B2) * (g * g)
    m_hat = m2 / (1.0 - ADAM_B1 ** ADAM_STEP)
    v_hat = v2 / (1.0 - ADAM_B2 ** ADAM_STEP)
    return -ADAM_LR * (m_hat / (jnp.sqrt(v_hat) + ADAM_EPS) + ADAM_WD * w), m2, v2


def _adamw_refs(w_ref, g_ref, m_ref, v_ref, go_ref, d_ref, mo_ref, vo_ref):
    gv = g_ref[...]
    d_ref[...], mo_ref[...], vo_ref[...] = _adamw_step(w_ref[...], gv, m_ref[...], v_ref[...])
    go_ref[...] = gv


def _adamw(name, items, steps=16):
    n = len(items)

    def body(*refs):
        for t in range(n):
            _adamw_refs(*refs[4 * t:4 * t + 4], *refs[4 * n + 4 * t:4 * n + 4 * t + 4])

    specs, shapes = [], []
    for w, _, _, _ in items:
        R, C = w.shape
        assert R % (8 * steps) == 0, (name, w.shape)
        specs += [pl.BlockSpec((R // steps, C), lambda i: (i, 0))] * 4
        shapes += [jax.ShapeDtypeStruct((R, C), F32)] * 4
    res = _pcall(body, name=name, grid=(steps,), in_specs=specs, out_specs=specs, out_shape=shapes,
                 compiler_params=_params(("parallel",)))(*[a for item in items for a in item])
    return [res[4 * t:4 * t + 4] for t in range(n)]


def _adamw_small(name, items):
    n = len(items)

    def body(*refs):
        for t in range(n):
            _adamw_refs(*refs[4 * t:4 * t + 4], *refs[4 * n + 4 * t:4 * n + 4 * t + 4])

    specs, shapes = [], []
    for w, _, _, _ in items:
        specs += [pl.BlockSpec(w.shape, lambda i: (0, 0))] * 4
        shapes += [jax.ShapeDtypeStruct(w.shape, F32)] * 4
    res = _pcall(body, name=name, grid=(1,), in_specs=specs, out_specs=specs, out_shape=shapes,
                 compiler_params=_params(("arbitrary",)))(*[a for item in items for a in item])
    return [res[4 * t:4 * t + 4] for t in range(n)]


def _sum_leading(name, a, out_dtype):
    n, R, C = a.shape
    tr = _row_tile(R, C) if R % 16 == 0 else R
    if tr % 16:
        tr = R

    def body(a_ref, o_ref):
        acc = a_ref[0].astype(F32)
        for j in range(1, n):
            acc = acc + a_ref[j].astype(F32)
        o_ref[...] = acc.astype(o_ref.dtype)

    return _pcall(body, name=name, grid=(R // tr,), in_specs=[pl.BlockSpec((n, tr, C), lambda i: (0, i, 0))],
                  out_specs=pl.BlockSpec((tr, C), lambda i: (i, 0)), out_shape=jax.ShapeDtypeStruct((R, C), out_dtype),
                  compiler_params=_params(("parallel",)))(a)


def _cast_many(name, items, chip, steps=4):
    tiles = []
    for w, _, dtype in items:
        R = w.shape[1]
        nt = steps if R % (steps * 16) == 0 else 1
        tiles.append((nt, R // nt))

    def body(c_ref, *refs):
        i = pl.program_id(0)
        for t, (nt, _) in enumerate(tiles):
            w_ref, o_ref = refs[t], refs[len(items) + t]

            def cast(w_ref=w_ref, o_ref=o_ref):
                o_ref[...] = w_ref[...].astype(o_ref.dtype)

            if nt == steps:
                cast()
            else:
                pl.when(i < nt)(cast)

    in_specs, out_specs, out_shape = [], [], []
    for (w, layer, dtype), (nt, tr) in zip(items, tiles):
        C = w.shape[2]
        in_specs.append(pl.BlockSpec((None, tr, C), lambda i, c_ref, l=layer, nt=nt: (l, jnp.minimum(i, nt - 1), 0)))
        out_specs.append(pl.BlockSpec((None, tr, C), lambda i, c_ref, nt=nt: (c_ref[0], jnp.minimum(i, nt - 1), 0)))
        out_shape.append(jax.ShapeDtypeStruct((N_CHIPS, w.shape[1], C), dtype))
    return _pcall(body, prefetch=1, name=name, grid=(steps,), in_specs=in_specs, out_specs=out_specs,
                  out_shape=out_shape, compiler_params=_params(("arbitrary",)))(chip, *[w for w, _, _ in items])


def _cast_qkv(name, w, chip, tr=256):
    _, R, C = w.shape

    def body(c_ref, w_ref, own_ref, *piece_refs):
        v = w_ref[...].astype(BF16)
        own_ref[...] = v
        for p, ref in enumerate(piece_refs):
            ref[...] = v[:, p * QKV_TILE:(p + 1) * QKV_TILE]

    piece = pl.BlockSpec((None, tr, QKV_TILE), lambda i, c_ref: (c_ref[0], i, 0))
    return _pcall(body, prefetch=1, name=name, grid=(R // tr,),
                  in_specs=[pl.BlockSpec((None, tr, C), lambda i, c_ref: (0, i, 0))],
                  out_specs=[pl.BlockSpec((tr, C), lambda i, c_ref: (i, 0))] + [piece] * QKV_PIECES,
                  out_shape=[jax.ShapeDtypeStruct((R, C), BF16)]
                  + [jax.ShapeDtypeStruct((N_CHIPS, R, QKV_TILE), BF16)] * QKV_PIECES,
                  compiler_params=_params(("parallel",)))(chip, w)


def _owner_sum(name, mine, landed, shape, chip, core, out, layer, col):
    _, half, C = mine.shape
    k, _ = col
    tr = _row_tile(half, C)
    if tr % 16:
        tr = half
    nrt = half // tr

    def body(ch_ref, co_ref, mine_ref, landed_ref, *rest):
        g = mine_ref[...].astype(F32)
        for j in range(3):
            g = g + landed_ref[j].astype(F32)
        rest[-1][...] = g

    piece = pl.BlockSpec((None, tr, C), lambda i, ch, co: (layer, co[0] * nrt + i, k))
    in_specs = [pl.BlockSpec((None, tr, C), lambda i, ch, co: (ch[0], i, 0)),
                pl.BlockSpec((3, tr, C), lambda i, ch, co: (0, i, 0))]
    operands = [chip, core, mine, landed]
    aliases = {}
    if out is not None:
        in_specs.append(ANY)
        operands.append(out)
        aliases = {4: 0}
    return _pcall(body, prefetch=2, name=name, grid=(nrt,), in_specs=in_specs, out_specs=piece,
                  out_shape=jax.ShapeDtypeStruct(shape, F32), input_output_aliases=aliases,
                  compiler_params=_params(("parallel",)))(*operands)


def _add_my_half(name, ps, qs, core):
    n = len(ps)

    def body(c_ref, *refs):
        for t in range(n):
            p_ref, q_ref, o_ref = refs[t], refs[n + t], refs[2 * n + t]
            o_ref[...] = (p_ref[...].astype(F32) + q_ref[...].astype(F32)).astype(o_ref.dtype)

    halves = [(p.shape[1] // 2, p.shape[2]) for p in ps]
    mine = [pl.BlockSpec((None, h, c), lambda s, c_ref: (s, c_ref[0], 0)) for h, c in halves]
    whole = [pl.BlockSpec((None, h, c), lambda s, c_ref: (s, 0, 0)) for h, c in halves]
    return _pcall(body, prefetch=1, name=name, grid=(N_CHIPS,), in_specs=mine + whole, out_specs=whole,
                  out_shape=[jax.ShapeDtypeStruct((N_CHIPS, h, c), BF16) for h, c in halves],
                  compiler_params=_params(("parallel",)))(core, *ps, *qs)


ANY = pl.BlockSpec(memory_space=pl.ANY)


def _place():
    x, y, c = lax.axis_index("x"), lax.axis_index("y"), lax.axis_index("c")
    other_chips = [(1 - x, y), (x, 1 - y), (1 - x, 1 - y)]
    return x, y, c, other_chips


def _remote(src, dst, send, recv, k, to):
    return pltpu.make_async_remote_copy(src_ref=src, dst_ref=dst, send_sem=send.at[k], recv_sem=recv.at[k],
                                        device_id=to, device_id_type=MESH)


def _in_place(bufs):
    return dict(operands=bufs, out_shapes=[jax.ShapeDtypeStruct(b.shape, b.dtype) for b in bufs],
                aliases={t: t for t in range(len(bufs))})


def _gather_rider(bufs, jobs):
    def copies(ins, outs, send, recv, base=0):
        x, y, c, chips = _place()
        out = []
        for t, over_ici, rows in jobs:
            ref = outs[t]
            if rows is not None:
                a, b, n = rows
                half = ref.shape[1] // 2
                rows = pl.ds(c * half + a * half // n, (b - a) * half // n)
            for px, py in chips:
                slot = 2 * x + y if over_ici else 2 * px + py
                piece = ref.at[slot] if rows is None else ref.at[slot, rows]
                out.append(_remote(piece, piece, send, recv, base + len(out), (px, py, c) if over_ici else (x, y, 1 - c)))
        return out

    return _Rider(n_copies=3 * len(jobs), copies=copies, **_in_place(bufs))


def _gather_and_pass_rider(buf):
    half = buf.shape[1] // 2

    def pieces(outs):
        x, y, c, chips = _place()
        rows = lambda core: pl.ds(core * half, half)
        mine = outs[0].at[2 * x + y, rows(c)]
        landed = [outs[0].at[2 * px + py, rows(c)] for px, py in chips]
        theirs = [outs[0].at[2 * px + py, rows(1 - c)] for px, py in chips]
        return (x, y, c, chips), mine, landed, theirs

    def start(ins, outs, send, recv, base=0):
        (x, y, c, chips), mine, _, _ = pieces(outs)
        for j, (px, py) in enumerate(chips):
            _remote(mine, mine, send, recv, base + j, (px, py, c)).start()

    def finish(ins, outs, send, recv, base=0):
        (x, y, c, chips), mine, landed, theirs = pieces(outs)
        sibling = (x, y, 1 - c)
        passed = []
        for j, (px, py) in enumerate(chips):
            _remote(landed[j], landed[j], send, recv, base + j, (px, py, c)).wait_recv()
            passed.append(_remote(landed[j], landed[j], send, recv, base + 3 + j, sibling))
            passed[-1].start()
        for j in range(3):
            _remote(theirs[j], theirs[j], send, recv, base + 3 + j, sibling).wait_recv()
        for j, (px, py) in enumerate(chips):
            _remote(mine, mine, send, recv, base + j, (px, py, c)).wait_send()
            passed[j].wait_send()

    return _Rider(n_copies=6, start=start, finish=finish, **_in_place([buf]))


def _swap_halves_rider(parts):
    n = len(parts)

    def copies(ins, outs, send, recv, base=0):
        x, y, c, _ = _place()
        out = []
        for t in range(n):
            half = ins[t].shape[1] // 2
            out.append(_remote(ins[t].at[:, pl.ds((1 - c) * half, half)], outs[t], send, recv, base + t,
                               (x, y, 1 - c)))
        return out

    shapes = [jax.ShapeDtypeStruct((p.shape[0], p.shape[1] // 2, p.shape[2]), p.dtype) for p in parts]
    return _Rider(parts, shapes, {}, n, copies)


def _scatter_rider(sums, landed, relations):
    n = len(sums)
    kept = [t for t in range(n) if landed[t] is not None]

    def copies(ins, outs, send, recv, base=0):
        x, y, c, chips = _place()
        out = []
        for t in range(n):
            for j in relations[t]:
                px, py = chips[j]
                out.append(_remote(ins[t].at[2 * px + py], outs[t].at[j], send, recv, base + len(out), (px, py, c)))
        return out

    shapes = [jax.ShapeDtypeStruct((3,) + s.shape[1:], s.dtype) for s in sums]
    return _Rider(list(sums) + [landed[t] for t in kept], shapes, {n + k: t for k, t in enumerate(kept)},
                  sum(len(r) for r in relations), copies)


def _share_rider(blocks, pieces):
    def copies(ins, outs, send, recv, base=0):
        x, y, c, _ = _place()
        out = []
        for k, (t, l, col) in enumerate(pieces):
            ref = outs[t].at[l]
            r2 = ref.shape[0] // 2
            width = ref.shape[1] // col[1]
            piece = ref.at[pl.ds(c * r2, r2), pl.ds(col[0] * width, width)]
            out.append(_remote(piece, piece, send, recv, base + k, (x, y, 1 - c)))
        return out

    return _Rider(n_copies=len(pieces), copies=copies, **_in_place(blocks))


def _gather_small(name, v):
    def body(v_ref, out_ref, send_sem, recv_sem, local_sem):
        x, y, c, _ = _place()
        me = 4 * x + 2 * y + c
        flips = [(fx, fy, fc) for fx in (0, 1) for fy in (0, 1) for fc in (0, 1)][1:]
        local = pltpu.make_async_copy(v_ref, out_ref.at[me], local_sem)
        local.start()
        copies = []
        for j, (fx, fy, fc) in enumerate(flips):
            peer = (x ^ fx, y ^ fy, c ^ fc)
            copies.append(pltpu.make_async_remote_copy(
                src_ref=v_ref, dst_ref=out_ref.at[me], send_sem=send_sem.at[j], recv_sem=recv_sem.at[j],
                device_id=peer, device_id_type=MESH))
        for cp in copies:
            cp.start()
        for j, (fx, fy, fc) in enumerate(flips):
            peer = (x ^ fx, y ^ fy, c ^ fc)
            pltpu.make_async_remote_copy(
                src_ref=v_ref, dst_ref=out_ref.at[4 * peer[0] + 2 * peer[1] + peer[2]], send_sem=send_sem.at[j],
                recv_sem=recv_sem.at[j], device_id=peer, device_id_type=MESH).wait_recv()
        for cp in copies:
            cp.wait_send()
        local.wait()

    return _pcall(body, name=name, in_specs=[ANY], out_specs=ANY,
                  out_shape=jax.ShapeDtypeStruct((8,) + v.shape, v.dtype),
                  scratch_shapes=[pltpu.SemaphoreType.DMA((7,)), pltpu.SemaphoreType.DMA((7,)),
                                  pltpu.SemaphoreType.DMA])(v)


def _fold(a, dil):
    if dil == 1:
        return a
    S, C = a.shape
    return a.reshape(S // dil, dil, C).transpose(1, 0, 2).reshape(S, C)


def _unfold(a, dil):
    if dil == 1:
        return a
    S, C = a.shape
    return a.reshape(dil, S // dil, C).transpose(1, 0, 2).reshape(S, C)


def _fold_groups(a):
    return jnp.stack([_fold(a[g] if a.ndim == 3 else a, d) for g, d in enumerate(ATTN_DILATIONS)])


def _unfold_groups(a):
    return jnp.stack([_unfold(a[g], d) for g, d in enumerate(ATTN_DILATIONS)])


class _NoComm:
    def __init__(self, weights):
        self.weights = weights
        self.grads = {}
        self.chip = jnp.zeros((1,), jnp.int32)

    def w(self, name):
        return self.weights[name]

    def run(self, fn, name, *args, **kw):
        return fn(name, *args, **kw)

    def grad(self, name, value):
        self.grads[name] = value


def _local_step(xs, tgt, attn_norm, ffn_norm, final_norm, comm):
    run = comm.run
    hf = run(_norm_fold, "fold", xs, attn_norm)
    qkv = run(_qkv_tiles, "qkv_own", None, hf, comm.w("wq_own"), None, comm.chip)
    for p in range(QKV_PIECES):
        qkv = run(_qkv_tiles, "qkv_piece%d" % p, p, hf, comm.w("wq%d" % p), qkv, comm.chip)
    o_f, lse_f = run(_attn_fwd, "attn_fwd", qkv)
    lse_t = _unfold_groups(lse_f)
    merged = run(_merge_fwd, "merge_fwd", o_f, lse_t)
    x1, h1 = run(_proj_res_norm, "attn_out", merged, comm.w("o"), xs, ffn_norm[0:1])
    gu0, act0 = run(_ffn_up, "ffn0_up", h1, comm.w("gu0"))
    x2, h2 = run(_proj_res_norm, "ffn0_down", act0, comm.w("d0"), x1, comm.w("pool_norm"))
    y, z, x3, h3 = run(_pool_fwd, "pool_fwd", h2, comm.w("p"), comm.w("pg"), comm.w("pool_scale"), x2, ffn_norm[1:2])
    gu1, act1 = run(_ffn_up, "ffn1_up", h3, comm.w("gu1"))
    loss, dx4, dx4_b, d_final = run(_proj_res_loss, "ffn1_down", act1, comm.w("d1"), x3, final_norm, tgt)

    dgu1 = run(_ffn_down_bwd, "ffn1_down_bwd", dx4_b, comm.w("d1"), gu1)
    comm.grad("d1", run(_mm_tn, "ffn1_down_dw", act1, dx4_b, FF_SHARD, 2048))
    comm.grad("gu1", run(_ffn_dw_up, "ffn1_up_dw", h3, dgu1))
    dx3, d_ffn1 = run(_ffn_dh, "ffn1_up_dh", dgu1, comm.w("gu1"), x3, ffn_norm[1:2], dx4)

    du, dw_pg, d_scale = run(_pool_bwd, "pool_bwd", dx3, z, y, comm.w("pool_scale"), comm.w("pg"))
    comm.grad("pg", dw_pg)
    comm.grad("p", run(_mm_tn, "pool_in_dw", h2, du, D_MODEL, h2.shape[0]))
    dx2, dx2_b, d_pool = run(_dh_norm_bwd, "pool_in_dh", du, comm.w("p"), x2, comm.w("pool_norm"), dx3)

    dgu0 = run(_ffn_down_bwd, "ffn0_down_bwd", dx2_b, comm.w("d0"), gu0)
    comm.grad("d0", run(_mm_tn, "ffn0_down_dw", act0, dx2_b, FF_SHARD, 2048))
    comm.grad("gu0", run(_ffn_dw_up, "ffn0_up_dw", h1, dgu0))
    dx1, d_ffn0 = run(_ffn_dh, "ffn0_up_dh", dgu0, comm.w("gu0"), x1, ffn_norm[0:1], dx2)

    comm.grad("o", run(_mm_tn, "attn_out_dw", merged, dx1, D_MODEL, 2048))
    do_f, dl_t = run(_merge_bwd, "merge_bwd", dx1, comm.w("o"), o_f, lse_t)
    dqkv = run(_attn_bwd, "attn_bwd", qkv, do_f, lse_f, _fold_groups(dl_t))
    for p in range(QKV_PIECES):
        comm.grad("qkv%d" % p, run(_qkv_dw, "qkv_dw%d" % p, p, hf, dqkv))
    dhf = None
    for g in range(N_GROUPS):
        dhf = run(_qkv_dh, "qkv_dh%d" % g, g, dqkv, [comm.w("wq%d" % p) for p in range(QKV_PIECES)], dhf)
    grad_x, d_attn = run(_rms_bwd_folded, "norm_attn_bwd", xs, attn_norm, dhf, dx1)

    small = dict(attn=d_attn, ffn0=d_ffn0, ffn1=d_ffn1, final=d_final, pool=d_pool, scale=d_scale)
    return loss, grad_x, small


TENSORS = ("qkv", "o", "p", "pg", "gu0", "gu1", "d0", "d1")


def _block_of(name):
    if name[:-1] in ("gu", "d"):
        return name[:-1], int(name[-1]), (0, 1)
    if name[:-1] == "qkv":
        return "qkv", 0, (int(name[-1]), QKV_PIECES)
    return name, 0, (0, 1)


class _MeshComm:
    def __init__(self, bufs, shapes, chip_arr, core_arr, pg_rows):
        self.bufs = dict(bufs)
        self.shapes = shapes
        self.chip, self.chip_arr, self.core_arr, self.pg_rows = chip_arr, chip_arr, core_arr, pg_rows
        self.parts, self.sums, self.blocks, self.landed, self.arrived = {}, {}, {}, {}, {}
        move = lambda ici=(), d2d=(): lambda: self.gather(ici, d2d)
        whole = lambda name: lambda: self.gather_and_pass(name)
        swap = lambda *names: lambda: self.swap(names)
        scatter = lambda *names: lambda: self.scatter(names)
        share = lambda *names: lambda: self.share(names)
        self.plan = {
            "fold": [whole("wq0")],
            "qkv_own": [whole("wq1")],
            "qkv_piece0": [whole("wq2")],
            "qkv_piece1": [move(ici=["o", "gu0[0:1/4]"])],
            "qkv_piece2": [move(ici=["gu0[1:2/4]"], d2d=["o", "gu0[0:1/4]"])],
            "attn_fwd": [move(ici=["gu0[2:4/4]", "d0[0:1/2]"], d2d=["gu0[1:2/4]"])],
            "merge_fwd": [move(ici=["p", "pg"], d2d=["gu0[2:4/4]", "d0[0:1/2]"])],
            "attn_out": [move(ici=["d0[1:2/2]", "pool_norm", "pool_scale"], d2d=["p", "pg"])],
            "ffn0_up": [move(ici=["gu1[0:3/4]"], d2d=["d0[1:2/2]"])],
            "ffn0_down": [move(ici=["gu1[3:4/4]"], d2d=["gu1[0:3/4]"])],
            "pool_fwd": [move(ici=["d1"], d2d=["gu1[3:4/4]"])],
            "ffn1_up": [move(d2d=["d1"])],
            "ffn1_up_dh": [swap("d1", "gu1")],
            "pool_bwd": [scatter("d1{01}")],
            "pool_in_dh": [scatter("d1{2}")],
            "ffn0_down_bwd": [scatter("gu1{01}")],
            "ffn0_down_dw": [scatter("gu1{2}")],
            "ffn0_up_dw": [share("gu1", "d1")],
            "ffn0_up_dh": [swap("pg", "p", "d0", "gu0")],
            "merge_bwd": [swap("o")],
            "attn_bwd": [scatter("gu0", "d0", "o")],
            "qkv_dw0": [share("d0"), scatter("p", "pg")],
            "qkv_dw1": [share("gu0", "o"), swap("qkv0")],
            "qkv_dw2": [share("p", "pg"), scatter("qkv0"), swap("qkv1")],
            "qkv_dh0": [share("qkv0"), scatter("qkv1"), swap("qkv2")],
            "qkv_dh1": [share("qkv1"), scatter("qkv2")],
            "qkv_dh2": [share("qkv2")],
        }

    def gather_and_pass(self, name):
        return _gather_and_pass_rider(self.bufs[name]), lambda res: self.bufs.update({name: res[0]})

    def gather(self, ici, d2d):
        names, jobs = [], []
        for over_ici, specs in ((True, ici), (False, d2d)):
            for spec in specs:
                name, _, rows = spec.partition("[")
                if name not in names:
                    names.append(name)
                rng = None
                if name in TENSORS:
                    ab, _, n = (rows[:-1] or "0:1/1").partition("/")
                    rng = (int(ab.split(":")[0]), int(ab.split(":")[1]), int(n))
                jobs.append((names.index(name), over_ici, rng))
        return _gather_rider([self.bufs[n] for n in names], jobs), lambda res: self.bufs.update(zip(names, res))

    def swap(self, names):
        def done(res):
            sums = _add_my_half("chip_sum_" + "_".join(names), [self.parts[n] for n in names], res, self.core_arr)
            self.sums.update(zip(names, sums))
        return _swap_halves_rider([self.parts[n] for n in names]), done

    def scatter(self, specs):
        names = [s.partition("{")[0] for s in specs]
        relations = [tuple(int(ch) for ch in s.partition("{")[2][:-1]) or (0, 1, 2) for s in specs]

        def done(res):
            for n, rel, landed in zip(names, relations, res):
                self.landed[n] = landed
                self.arrived[n] = self.arrived.get(n, ()) + rel
                if len(self.arrived[n]) < 3:
                    continue
                key, layer, col = _block_of(n)
                self.blocks[key] = _owner_sum("owner_sum_" + n, self.sums[n], landed, self.shapes[key],
                                              self.chip_arr, self.core_arr, self.blocks.get(key), layer, col)
        rider = _scatter_rider([self.sums[n] for n in names], [self.landed.get(n) for n in names], relations)
        return rider, done

    def share(self, names):
        keys, pieces = [], []
        for n in names:
            key, layer, col = _block_of(n)
            if key not in keys:
                keys.append(key)
            pieces.append((keys.index(key), layer, col))
        return _share_rider([self.blocks[k] for k in keys], pieces), lambda res: self.blocks.update(zip(keys, res))

    def run(self, fn, name, *args, **kw):
        made = [make() for make in self.plan.get(name, [])]
        if not made:
            return fn(name, *args, **kw)
        riders = [r for r, _ in made]
        out = _ride(riders[0] if len(riders) == 1 else _riders(*riders), fn, name, *args, **kw)
        for r, done in made:
            done(r.results)
        return out

    def w(self, name):
        b = self.bufs[name]
        if name in ("o", "p", "d0", "d1"):
            return b.reshape(-1, b.shape[-1])
        if name == "pg":
            b = b.reshape(N_CHIPS, 4, self.pg_rows, POOL_DIM).transpose(1, 0, 2, 3)
            return b.reshape(4, POOL_DIM, POOL_DIM)
        if name in ("pool_norm", "pool_scale"):
            return b.reshape(1, -1)
        return b

    def grad(self, name, value):
        if name == "pg":
            value = value.reshape(4, N_CHIPS, self.pg_rows, POOL_DIM).transpose(1, 0, 2, 3)
            value = value.reshape(N_CHIPS, 4 * self.pg_rows, POOL_DIM).astype(BF16)
        elif name in ("o", "p", "d0", "d1"):
            value = value.reshape(N_CHIPS, value.shape[0] // N_CHIPS, value.shape[1])
        self.parts[name] = value


def kernel(x, attn_norm, w_qkv, w_attn_out, pool_norm, w_pool_in, w_pool_group, pool_scale, ffn_norm, w_ffn_gate_up, w_ffn_down, final_norm, loss_target, m_attn_norm, m_w_qkv, m_w_attn_out, m_pool_norm, m_w_pool_in, m_w_pool_group, m_pool_scale, m_ffn_norm, m_w_ffn_gate_up, m_w_ffn_down, m_final_norm, v_attn_norm, v_w_qkv, v_w_attn_out, v_pool_norm, v_w_pool_in, v_w_pool_group, v_pool_scale, v_ffn_norm, v_w_ffn_gate_up, v_w_ffn_down, v_final_norm):
    D = D_MODEL
    chip = 2 * lax.axis_index("x") + lax.axis_index("y")
    core = lax.axis_index("c")
    pg_rows = w_pool_group.shape[2]

    chip_arr = chip.astype(jnp.int32).reshape(1)
    core_arr = core.astype(jnp.int32).reshape(1)

    pieces = _cast_qkv("cast_qkv", w_qkv, chip_arr)
    bufs = dict(zip(["wq_own"] + ["wq%d" % p for p in range(QKV_PIECES)], pieces))
    shards = [(w_attn_out, 0, BF16), (w_pool_in, 0, BF16), (w_pool_group.reshape(1, 4 * pg_rows, POOL_DIM), 0, BF16),
              (w_ffn_gate_up, 0, BF16), (w_ffn_gate_up, 1, BF16), (w_ffn_down, 0, BF16), (w_ffn_down, 1, BF16),
              (pool_norm[None], 0, F32), (pool_scale[None], 0, F32)]
    bufs.update(zip(TENSORS[1:] + ("pool_norm", "pool_scale"), _cast_many("cast_rest", shards, chip_arr)))

    as_block = lambda a: a.reshape((-1,) + a.shape[-2:]) if a.ndim == 3 else a.reshape(1, -1, a.shape[-1])
    owned = dict(qkv=(w_qkv, m_w_qkv, v_w_qkv), o=(w_attn_out, m_w_attn_out, v_w_attn_out),
                 p=(w_pool_in, m_w_pool_in, v_w_pool_in), pg=(w_pool_group, m_w_pool_group, v_w_pool_group),
                 gu=(w_ffn_gate_up, m_w_ffn_gate_up, v_w_ffn_gate_up), d=(w_ffn_down, m_w_ffn_down, v_w_ffn_down))
    comm = _MeshComm(bufs, {k: as_block(wmv[0]).shape for k, wmv in owned.items()}, chip_arr, core_arr, pg_rows)
    loss_part, grad_x, small = _local_step(x[0], loss_target[0], attn_norm, ffn_norm, final_norm.reshape(1, D), comm)

    rows = jnp.concatenate([small["attn"], small["ffn0"], small["ffn1"], small["final"], small["pool"],
                            small["scale"], jnp.pad(loss_part, ((0, 0), (0, D - 1))), jnp.zeros((1, D), F32)], axis=0)
    all_rows = _gather_small("gather_gain_grads", rows)
    tot = _sum_leading("sum_gain_grads", all_rows, F32)
    loss = tot[6, 0]
    g_attn_norm = tot[0:1]
    g_ffn_norm = tot[1:3]
    g_final_norm = tot[3]
    g_pool_norm = lax.dynamic_slice(tot, (4, chip * POOL_DIM), (1, POOL_DIM))
    g_pool_scale = lax.dynamic_slice(tot, (5, chip * POOL_DIM), (1, POOL_DIM))

    as_rows = lambda a: a.reshape(-1, a.shape[-1])
    res = _adamw("adamw_weights", [(as_rows(w), as_rows(comm.blocks[k]), as_rows(m), as_rows(v))
                                   for k, (w, m, v) in owned.items()])
    updated = {k: [a.reshape(owned[k][0].shape) for a in r] for k, r in zip(owned, res)}
    gains = [(attn_norm, g_attn_norm, m_attn_norm, v_attn_norm), (pool_norm, g_pool_norm, m_pool_norm, v_pool_norm),
             (pool_scale, g_pool_scale, m_pool_scale, v_pool_scale), (ffn_norm, g_ffn_norm, m_ffn_norm, v_ffn_norm),
             (final_norm, g_final_norm, m_final_norm, v_final_norm)]
    small_res = _adamw_small("adamw_gains", [tuple(as_rows(a) for a in item) for item in gains])
    small_res = [[a.reshape(item[0].shape) for a in r] for r, item in zip(small_res, gains)]
    results = [small_res[0], updated["qkv"], updated["o"], small_res[1], updated["p"], updated["pg"], small_res[2],
               small_res[3], updated["gu"], updated["d"], small_res[4]]
    grads = [r[0] for r in results]
    deltas = [r[1] for r in results]
    new_m = [r[2] for r in results]
    new_v = [r[3] for r in results]
    return (loss, grad_x[None], *grads, *deltas, *new_m, *new_v)
```
